```python
import jax, jax.numpy as jnp
from jax import lax
import numpy as np

D_MODEL = 1024
BATCH = 8
SEQ = 2048
DEPTH = 2

N_A_LAYERS = DEPTH // 2
N_B_LAYERS = DEPTH - N_A_LAYERS
POOL_WINDOWS = (2, 4, 8, 16)
N_POOL_GROUPS = len(POOL_WINDOWS)
POOL_GROUP = D_MODEL // N_POOL_GROUPS
HEAD_DIM = 64
N_HEADS = D_MODEL // HEAD_DIM
N_KV_HEADS = 4
GQA_GROUP = N_HEADS // N_KV_HEADS
WINDOW = 128
BLOCK = 128
D_FF = -(-8 * D_MODEL // (3 * 256)) * 256
PLE_DIM = 256
EPS = 1e-6
NEG_INF = -1e30

kernel_name = "yoco_pool_swa_sink_hybrid"


def rmsnorm(x, g):
    xf = x.astype(jnp.float32)
    y = xf * lax.rsqrt(jnp.mean(xf * xf, axis=-1, keepdims=True) + EPS)
    return (y * g.astype(jnp.float32)).astype(x.dtype)


def causal_multiscale_pool(h):
    B, S, C = h.shape
    hf = h.astype(jnp.float32)
    cp = jnp.concatenate([jnp.zeros((B, 1, C), jnp.float32), jnp.cumsum(hf, axis=1)], axis=1)
    pos1 = jnp.arange(1, S + 1, dtype=jnp.int32)
    outs = []
    for gi, w in enumerate(POOL_WINDOWS):
        sl = cp[:, :, gi * POOL_GROUP:(gi + 1) * POOL_GROUP]
        hi = sl[:, 1:]
        lo = jnp.concatenate([jnp.zeros((B, w - 1, POOL_GROUP), jnp.float32),
                              sl[:, :S - w + 1]], axis=1)
        cnt = jnp.minimum(pos1, w).astype(jnp.float32)[None, :, None]
        outs.append((hi - lo) / cnt)
    pooled = jnp.concatenate(outs, axis=-1)
    return (pooled - hf).astype(h.dtype)


def pool_mixer(h, w_pool, scale):
    B, S, _ = h.shape
    d = causal_multiscale_pool(h).reshape(B, S, N_POOL_GROUPS, POOL_GROUP)
    y = jnp.einsum('bsgc,gcd->bsgd', d, w_pool).reshape(B, S, D_MODEL)
    return y * scale


def alibi_slopes():
    h = jnp.arange(1, N_HEADS + 1, dtype=jnp.float32)
    return jnp.exp2(-8.0 * h / N_HEADS).reshape(N_KV_HEADS, GQA_GROUP)


def swa_sink_attention(q, k, v, sinks):
    B, S = q.shape[:2]
    nb = S // BLOCK
    qb = q.reshape(B, nb, BLOCK, N_KV_HEADS, GQA_GROUP, HEAD_DIM)
    kb = k.reshape(B, nb, BLOCK, N_KV_HEADS, HEAD_DIM)
    vb = v.reshape(B, nb, BLOCK, N_KV_HEADS, HEAD_DIM)
    pad = ((0, 0), (1, 0), (0, 0), (0, 0), (0, 0))
    kwin = jnp.concatenate([jnp.pad(kb, pad)[:, :-1], kb], axis=2)
    vwin = jnp.concatenate([jnp.pad(vb, pad)[:, :-1], vb], axis=2)
    scores = jnp.einsum('bnqkgd,bnskd->bnkgqs', qb, kwin,
                        preferred_element_type=jnp.float32) * (HEAD_DIM ** -0.5)
    qi = jnp.arange(BLOCK)[:, None]
    si = jnp.arange(2 * BLOCK)[None, :]
    rel = BLOCK + qi - si
    kpos = (jnp.arange(nb)[:, None, None] - 1) * BLOCK + si[None]
    valid = (rel >= 0)[None] & (rel < WINDOW)[None] & (kpos >= 0)
    bias = -alibi_slopes()[:, :, None, None] * rel.astype(jnp.float32)
    scores = jnp.where(valid[None, :, None, None], scores + bias[None, None], NEG_INF)
    sink = jnp.broadcast_to(sinks.astype(jnp.float32).reshape(N_KV_HEADS, GQA_GROUP)[None, None, :, :, None, None],
                            scores.shape[:-1] + (1,))
    probs = jax.nn.softmax(jnp.concatenate([scores, sink], axis=-1), axis=-1)[..., :-1]
    out = jnp.einsum('bnkgqs,bnskd->bnqkgd', probs.astype(v.dtype), vwin)
    return out.reshape(B, S, N_HEADS * HEAD_DIM)


def swiglu(h, w_gu, w_down):
    gu = h @ w_gu
    g, u = gu[..., :D_FF], gu[..., D_FF:]
    return (jax.nn.silu(g) * u) @ w_down


def _fwd_setup_inputs(seed: int = 0) -> dict:
    key = jax.random.key(seed)
    ks = jax.random.split(key, 24)
    f32 = jnp.float32
    nrm = lambda k, s, fan: jax.random.normal(k, s, f32) * (fan ** -0.5)
    gain = lambda k, s: 1.0 + 0.1 * jax.random.normal(k, s, f32)
    KV = N_KV_HEADS * HEAD_DIM
    QD = N_HEADS * HEAD_DIM
    return {
        "x": jax.random.normal(ks[0], (BATCH, SEQ, D_MODEL), f32),
        "p": jax.random.normal(ks[1], (DEPTH, BATCH, SEQ, PLE_DIM), f32),
        "pre_mix_g": gain(ks[2], (DEPTH, D_MODEL)),
        "post_mix_g": gain(ks[3], (DEPTH, D_MODEL)),
        "pre_ffn_g": gain(ks[4], (DEPTH, D_MODEL)),
        "post_ffn_g": gain(ks[5], (DEPTH, D_MODEL)),
        "pool_w": nrm(ks[6], (N_A_LAYERS, N_POOL_GROUPS, POOL_GROUP, POOL_GROUP), POOL_GROUP),
        "pool_scale": gain(ks[7], (N_A_LAYERS, D_MODEL)),
        "kv_g": gain(ks[8], (D_MODEL,)),
        "w_kv": nrm(ks[9], (D_MODEL, 2 * KV), D_MODEL),
        "w_q": nrm(ks[10], (N_B_LAYERS, D_MODEL, QD), D_MODEL),
        "sinks": 0.5 * jax.random.normal(ks[11], (N_B_LAYERS, N_HEADS), f32),
        "w_o": nrm(ks[12], (N_B_LAYERS, QD, D_MODEL), QD),
        "w_gu": nrm(ks[13], (DEPTH, D_MODEL, 2 * D_FF), D_MODEL),
        "w_down": nrm(ks[14], (DEPTH, D_FF, D_MODEL), D_FF),
        "ple_g": gain(ks[15], (DEPTH, D_MODEL)),
        "w_ple_gate": nrm(ks[16], (DEPTH, D_MODEL, D_MODEL), D_MODEL),
        "w_ple_proj": nrm(ks[17], (DEPTH, PLE_DIM, D_MODEL), PLE_DIM),
        "ple_post_g": gain(ks[18], (DEPTH, D_MODEL)),
    }


def _fwd_reference(x, p, pre_mix_g, post_mix_g, pre_ffn_g, post_ffn_g, pool_w, pool_scale,
              kv_g, w_kv, w_q, sinks, w_o, w_gu, w_down, ple_g, w_ple_gate, w_ple_proj,
              ple_post_g):
    B, S, _ = x.shape
    KV = N_KV_HEADS * HEAD_DIM
    k = v = None
    for i in range(DEPTH):
        h = rmsnorm(x, pre_mix_g[i])
        if i < N_A_LAYERS:
            y = pool_mixer(h, pool_w[i], pool_scale[i])
        else:
            b = i - N_A_LAYERS
            q = (h @ w_q[b]).reshape(B, S, N_HEADS, HEAD_DIM)
            y = swa_sink_attention(q, k, v, sinks[b]) @ w_o[b]
        x = x + rmsnorm(y, post_mix_g[i])
        h = rmsnorm(x, pre_ffn_g[i])
        x = x + rmsnorm(swiglu(h, w_gu[i], w_down[i]), post_ffn_g[i])
        gate = jax.nn.sigmoid(rmsnorm(x, ple_g[i]) @ w_ple_gate[i])
        e = (p[i].astype(x.dtype) @ w_ple_proj[i]) * gate
        x = x + rmsnorm(e, ple_post_g[i])
        if i == N_A_LAYERS - 1:
            kv = rmsnorm(x, kv_g) @ w_kv
            k = kv[..., :KV].reshape(B, S, N_KV_HEADS, HEAD_DIM)
            v = kv[..., KV:].reshape(B, S, N_KV_HEADS, HEAD_DIM)
    return x


import jax as _jax
import jax.numpy as _jnp

TWIN_FORMAT = 'train_step'
FWD_PARAMS = ['x', 'p', 'pre_mix_g', 'post_mix_g', 'pre_ffn_g', 'post_ffn_g', 'pool_w', 'pool_scale', 'kv_g', 'w_kv', 'w_q', 'sinks', 'w_o', 'w_gu', 'w_down', 'ple_g', 'w_ple_gate', 'w_ple_proj', 'ple_post_g']
TWIN_WEIGHTS = ['pre_mix_g', 'post_mix_g', 'pre_ffn_g', 'post_ffn_g', 'pool_w', 'pool_scale', 'kv_g', 'w_kv', 'w_q', 'sinks', 'w_o', 'w_gu', 'w_down', 'ple_g', 'w_ple_gate', 'w_ple_proj', 'ple_post_g']
TWIN_DIFF_INPUT = 'x'
TWIN_INPUTS = ['x', 'p', 'pre_mix_g', 'post_mix_g', 'pre_ffn_g', 'post_ffn_g', 'pool_w', 'pool_scale', 'kv_g', 'w_kv', 'w_q', 'sinks', 'w_o', 'w_gu', 'w_down', 'ple_g', 'w_ple_gate', 'w_ple_proj', 'ple_post_g', 'loss_target', 'm_pre_mix_g', 'm_post_mix_g', 'm_pre_ffn_g', 'm_post_ffn_g', 'm_pool_w', 'm_pool_scale', 'm_kv_g', 'm_w_kv', 'm_w_q', 'm_sinks', 'm_w_o', 'm_w_gu', 'm_w_down', 'm_ple_g', 'm_w_ple_gate', 'm_w_ple_proj', 'm_ple_post_g', 'v_pre_mix_g', 'v_post_mix_g', 'v_pre_ffn_g', 'v_post_ffn_g', 'v_pool_w', 'v_pool_scale', 'v_kv_g', 'v_w_kv', 'v_w_q', 'v_sinks', 'v_w_o', 'v_w_gu', 'v_w_down', 'v_ple_g', 'v_w_ple_gate', 'v_w_ple_proj', 'v_ple_post_g']
TWIN_OUTPUTS = ['loss', 'grad_x', 'grad_pre_mix_g', 'grad_post_mix_g', 'grad_pre_ffn_g', 'grad_post_ffn_g', 'grad_pool_w', 'grad_pool_scale', 'grad_kv_g', 'grad_w_kv', 'grad_w_q', 'grad_sinks', 'grad_w_o', 'grad_w_gu', 'grad_w_down', 'grad_ple_g', 'grad_w_ple_gate', 'grad_w_ple_proj', 'grad_ple_post_g', 'delta_pre_mix_g', 'delta_post_mix_g', 'delta_pre_ffn_g', 'delta_post_ffn_g', 'delta_pool_w', 'delta_pool_scale', 'delta_kv_g', 'delta_w_kv', 'delta_w_q', 'delta_sinks', 'delta_w_o', 'delta_w_gu', 'delta_w_down', 'delta_ple_g', 'delta_w_ple_gate', 'delta_w_ple_proj', 'delta_ple_post_g', 'new_m_pre_mix_g', 'new_m_post_mix_g', 'new_m_pre_ffn_g', 'new_m_post_ffn_g', 'new_m_pool_w', 'new_m_pool_scale', 'new_m_kv_g', 'new_m_w_kv', 'new_m_w_q', 'new_m_sinks', 'new_m_w_o', 'new_m_w_gu', 'new_m_w_down', 'new_m_ple_g', 'new_m_w_ple_gate', 'new_m_w_ple_proj', 'new_m_ple_post_g', 'new_v_pre_mix_g', 'new_v_post_mix_g', 'new_v_pre_ffn_g', 'new_v_post_ffn_g', 'new_v_pool_w', 'new_v_pool_scale', 'new_v_kv_g', 'new_v_w_kv', 'new_v_w_q', 'new_v_sinks', 'new_v_w_o', 'new_v_w_gu', 'new_v_w_down', 'new_v_ple_g', 'new_v_w_ple_gate', 'new_v_w_ple_proj', 'new_v_ple_post_g']
TWIN_LEAF_KINDS = {'loss': 'loss', 'grad_x': 'grad_x', 'grad_pre_mix_g': 'grad_w', 'grad_post_mix_g': 'grad_w', 'grad_pre_ffn_g': 'grad_w', 'grad_post_ffn_g': 'grad_w', 'grad_pool_w': 'grad_w', 'grad_pool_scale': 'grad_w', 'grad_kv_g': 'grad_w', 'grad_w_kv': 'grad_w', 'grad_w_q': 'grad_w', 'grad_sinks': 'grad_w', 'grad_w_o': 'grad_w', 'grad_w_gu': 'grad_w', 'grad_w_down': 'grad_w', 'grad_ple_g': 'grad_w', 'grad_w_ple_gate': 'grad_w', 'grad_w_ple_proj': 'grad_w', 'grad_ple_post_g': 'grad_w', 'delta_pre_mix_g': 'delta_w', 'delta_post_mix_g': 'delta_w', 'delta_pre_ffn_g': 'delta_w', 'delta_post_ffn_g': 'delta_w', 'delta_pool_w': 'delta_w', 'delta_pool_scale': 'delta_w', 'delta_kv_g': 'delta_w', 'delta_w_kv': 'delta_w', 'delta_w_q': 'delta_w', 'delta_sinks': 'delta_w', 'delta_w_o': 'delta_w', 'delta_w_gu': 'delta_w', 'delta_w_down': 'delta_w', 'delta_ple_g': 'delta_w', 'delta_w_ple_gate': 'delta_w', 'delta_w_ple_proj': 'delta_w', 'delta_ple_post_g': 'delta_w', 'new_m_pre_mix_g': 'new_m', 'new_m_post_mix_g': 'new_m', 'new_m_pre_ffn_g': 'new_m', 'new_m_post_ffn_g': 'new_m', 'new_m_pool_w': 'new_m', 'new_m_pool_scale': 'new_m', 'new_m_kv_g': 'new_m', 'new_m_w_kv': 'new_m', 'new_m_w_q': 'new_m', 'new_m_sinks': 'new_m', 'new_m_w_o': 'new_m', 'new_m_w_gu': 'new_m', 'new_m_w_down': 'new_m', 'new_m_ple_g': 'new_m', 'new_m_w_ple_gate': 'new_m', 'new_m_w_ple_proj': 'new_m', 'new_m_ple_post_g': 'new_m', 'new_v_pre_mix_g': 'new_v', 'new_v_post_mix_g': 'new_v', 'new_v_pre_ffn_g': 'new_v', 'new_v_post_ffn_g': 'new_v', 'new_v_pool_w': 'new_v', 'new_v_pool_scale': 'new_v', 'new_v_kv_g': 'new_v', 'new_v_w_kv': 'new_v', 'new_v_w_q': 'new_v', 'new_v_sinks': 'new_v', 'new_v_w_o': 'new_v', 'new_v_w_gu': 'new_v', 'new_v_w_down': 'new_v', 'new_v_ple_g': 'new_v', 'new_v_w_ple_gate': 'new_v', 'new_v_w_ple_proj': 'new_v', 'new_v_ple_post_g': 'new_v'}


def _forward(args):
    return _fwd_reference(*[args[k] for k in FWD_PARAMS])


def _output_shape():
    out = _jax.eval_shape(lambda: _forward(_fwd_setup_inputs(0)))
    return out.shape, out.dtype

N_MICROBATCH = 1
ADAM_LR = 0.001
ADAM_B1 = 0.9
ADAM_B2 = 0.999
ADAM_EPS = 1e-08
ADAM_WD = 0.01
ADAM_STEP = 10
PER_EXAMPLE_BATCH_AXIS = {'x': 0, 'p': 1, 'loss_target': 0}
SHARED_INPUTS = []
_WEIGHT_DTYPES = {'pre_mix_g': _jnp.float32, 'post_mix_g': _jnp.float32, 'pre_ffn_g': _jnp.float32, 'post_ffn_g': _jnp.float32, 'pool_w': _jnp.float32, 'pool_scale': _jnp.float32, 'kv_g': _jnp.float32, 'w_kv': _jnp.float32, 'w_q': _jnp.float32, 'sinks': _jnp.float32, 'w_o': _jnp.float32, 'w_gu': _jnp.float32, 'w_down': _jnp.float32, 'ple_g': _jnp.float32, 'w_ple_gate': _jnp.float32, 'w_ple_proj': _jnp.float32, 'ple_post_g': _jnp.float32}
MOMENT_SCALE = {'pre_mix_g': 7.815090e-01, 'post_mix_g': 1.648683e+01, 'pre_ffn_g': 7.069640e-01, 'post_ffn_g': 1.602142e+01, 'pool_w': 1.015797e+00, 'pool_scale': 3.176886e+00, 'kv_g': 7.046289e-01, 'w_kv': 9.470450e-01, 'w_q': 3.384750e-01, 'sinks': 8.584790e-01, 'w_o': 6.120134e-01, 'w_gu': 2.970103e-01, 'w_down': 5.344881e-01, 'ple_g': 1.701682e-01, 'w_ple_gate': 1.707466e-01, 'w_ple_proj': 4.834549e-01, 'ple_post_g': 1.610280e+01}


def _to_microbatches(a, axis):
    t = _jnp.moveaxis(a, axis, 0)
    t = t.reshape((N_MICROBATCH, t.shape[0] // N_MICROBATCH) + t.shape[1:])
    return _jnp.moveaxis(t, 1, axis + 1)


def setup_inputs(seed: int = 0) -> dict:
    inp = _fwd_setup_inputs(seed)
    key = _jax.random.fold_in(_jax.random.key(seed), 7919)
    shape, _ = _output_shape()
    out = dict(inp)
    out["loss_target"] = _jax.random.normal(_jax.random.fold_in(key, 0), shape, _jnp.float32)
    for i, name in enumerate(TWIN_WEIGHTS):
        w = inp[name].astype(_jnp.float32)
        if MOMENT_SCALE is None:
            s = _jnp.sqrt(_jnp.mean(_jnp.square(w)) + 1e-30)
        else:
            s = MOMENT_SCALE[name]
        km, kv = _jax.random.split(_jax.random.fold_in(key, i + 1))
        out[name] = w
        out["m_" + name] = s * _jax.random.normal(km, w.shape, _jnp.float32)
        out["v_" + name] = (s * s) * _jax.random.uniform(kv, w.shape, _jnp.float32, 0.5, 1.5)
    if N_MICROBATCH > 1:
        for name, axis in PER_EXAMPLE_BATCH_AXIS.items():
            out[name] = _to_microbatches(out[name], axis)
    return {'x': out['x'], 'p': out['p'], 'pre_mix_g': out['pre_mix_g'], 'post_mix_g': out['post_mix_g'], 'pre_ffn_g': out['pre_ffn_g'], 'post_ffn_g': out['post_ffn_g'], 'pool_w': out['pool_w'], 'pool_scale': out['pool_scale'], 'kv_g': out['kv_g'], 'w_kv': out['w_kv'], 'w_q': out['w_q'], 'sinks': out['sinks'], 'w_o': out['w_o'], 'w_gu': out['w_gu'], 'w_down': out['w_down'], 'ple_g': out['ple_g'], 'w_ple_gate': out['w_ple_gate'], 'w_ple_proj': out['w_ple_proj'], 'ple_post_g': out['ple_post_g'], 'loss_target': out['loss_target'], 'm_pre_mix_g': out['m_pre_mix_g'], 'm_post_mix_g': out['m_post_mix_g'], 'm_pre_ffn_g': out['m_pre_ffn_g'], 'm_post_ffn_g': out['m_post_ffn_g'], 'm_pool_w': out['m_pool_w'], 'm_pool_scale': out['m_pool_scale'], 'm_kv_g': out['m_kv_g'], 'm_w_kv': out['m_w_kv'], 'm_w_q': out['m_w_q'], 'm_sinks': out['m_sinks'], 'm_w_o': out['m_w_o'], 'm_w_gu': out['m_w_gu'], 'm_w_down': out['m_w_down'], 'm_ple_g': out['m_ple_g'], 'm_w_ple_gate': out['m_w_ple_gate'], 'm_w_ple_proj': out['m_w_ple_proj'], 'm_ple_post_g': out['m_ple_post_g'], 'v_pre_mix_g': out['v_pre_mix_g'], 'v_post_mix_g': out['v_post_mix_g'], 'v_pre_ffn_g': out['v_pre_ffn_g'], 'v_post_ffn_g': out['v_post_ffn_g'], 'v_pool_w': out['v_pool_w'], 'v_pool_scale': out['v_pool_scale'], 'v_kv_g': out['v_kv_g'], 'v_w_kv': out['v_w_kv'], 'v_w_q': out['v_w_q'], 'v_sinks': out['v_sinks'], 'v_w_o': out['v_w_o'], 'v_w_gu': out['v_w_gu'], 'v_w_down': out['v_w_down'], 'v_ple_g': out['v_ple_g'], 'v_w_ple_gate': out['v_w_ple_gate'], 'v_w_ple_proj': out['v_w_ple_proj'], 'v_ple_post_g': out['v_ple_post_g']}


def _loss(weights, diff, rest, loss_target):
    with _jax.named_scope("forward"):
        args = {**rest, TWIN_DIFF_INPUT: diff, **{k: w.astype(_WEIGHT_DTYPES[k]) for k, w in weights.items()}}
        y = _forward(args)
    with _jax.named_scope("loss_head"):
        err = _jnp.square(y.astype(_jnp.float32) - loss_target)
        return 0.5 * _jnp.sum(_jnp.mean(err, axis=-1)) if err.ndim else 0.5 * err


def _adamw(w, g, m, v):
    m = ADAM_B1 * m + (1.0 - ADAM_B1) * g
    v = ADAM_B2 * v + (1.0 - ADAM_B2) * _jnp.square(g)
    m_hat = m / (1.0 - ADAM_B1 ** ADAM_STEP)
    v_hat = v / (1.0 - ADAM_B2 ** ADAM_STEP)
    delta = -ADAM_LR * (m_hat / (_jnp.sqrt(v_hat) + ADAM_EPS) + ADAM_WD * w)
    return delta, m, v


def reference(x, p, pre_mix_g, post_mix_g, pre_ffn_g, post_ffn_g, pool_w, pool_scale, kv_g, w_kv, w_q, sinks, w_o, w_gu, w_down, ple_g, w_ple_gate, w_ple_proj, ple_post_g, loss_target, m_pre_mix_g, m_post_mix_g, m_pre_ffn_g, m_post_ffn_g, m_pool_w, m_pool_scale, m_kv_g, m_w_kv, m_w_q, m_sinks, m_w_o, m_w_gu, m_w_down, m_ple_g, m_w_ple_gate, m_w_ple_proj, m_ple_post_g, v_pre_mix_g, v_post_mix_g, v_pre_ffn_g, v_post_ffn_g, v_pool_w, v_pool_scale, v_kv_g, v_w_kv, v_w_q, v_sinks, v_w_o, v_w_gu, v_w_down, v_ple_g, v_w_ple_gate, v_w_ple_proj, v_ple_post_g):
    given = dict(x=x, p=p, pre_mix_g=pre_mix_g, post_mix_g=post_mix_g, pre_ffn_g=pre_ffn_g, post_ffn_g=post_ffn_g, pool_w=pool_w, pool_scale=pool_scale, kv_g=kv_g, w_kv=w_kv, w_q=w_q, sinks=sinks, w_o=w_o, w_gu=w_gu, w_down=w_down, ple_g=ple_g, w_ple_gate=w_ple_gate, w_ple_proj=w_ple_proj, ple_post_g=ple_post_g, loss_target=loss_target, m_pre_mix_g=m_pre_mix_g, m_post_mix_g=m_post_mix_g, m_pre_ffn_g=m_pre_ffn_g, m_post_ffn_g=m_post_ffn_g, m_pool_w=m_pool_w, m_pool_scale=m_pool_scale, m_kv_g=m_kv_g, m_w_kv=m_w_kv, m_w_q=m_w_q, m_sinks=m_sinks, m_w_o=m_w_o, m_w_gu=m_w_gu, m_w_down=m_w_down, m_ple_g=m_ple_g, m_w_ple_gate=m_w_ple_gate, m_w_ple_proj=m_w_ple_proj, m_ple_post_g=m_ple_post_g, v_pre_mix_g=v_pre_mix_g, v_post_mix_g=v_post_mix_g, v_pre_ffn_g=v_pre_ffn_g, v_post_ffn_g=v_post_ffn_g, v_pool_w=v_pool_w, v_pool_scale=v_pool_scale, v_kv_g=v_kv_g, v_w_kv=v_w_kv, v_w_q=v_w_q, v_sinks=v_sinks, v_w_o=v_w_o, v_w_gu=v_w_gu, v_w_down=v_w_down, v_ple_g=v_ple_g, v_w_ple_gate=v_w_ple_gate, v_w_ple_proj=v_w_ple_proj, v_ple_post_g=v_ple_post_g)
    weights = {n: given[n] for n in TWIN_WEIGHTS}
    shared = {n: given[n] for n in SHARED_INPUTS}
    per_example = {n: given[n] for n in ['x', 'p']}
    grad_fn = _jax.value_and_grad(_loss, argnums=(0, 1))

    def one_microbatch(ex, loss_target):
        ex = dict(ex)
        diff = ex.pop(TWIN_DIFF_INPUT)
        return grad_fn(weights, diff, {**shared, **ex}, loss_target)

    if N_MICROBATCH == 1:
        loss, (grad_w, grad_x) = one_microbatch(per_example, given["loss_target"])
    else:
        def body(carry, xs):
            loss_sum, grad_sum = carry
            l_k, (gw_k, gx_k) = one_microbatch(xs[0], xs[1])
            with _jax.named_scope("update"):
                return (loss_sum + l_k, _jax.tree.map(_jnp.add, grad_sum, gw_k)), gx_k

        init = (_jnp.zeros((), _jnp.float32), _jax.tree.map(_jnp.zeros_like, weights))
        (loss, grad_w), grad_x = _jax.lax.scan(body, init, (per_example, given["loss_target"]))
    with _jax.named_scope("update"):
        delta_w, new_m, new_v = {}, {}, {}
        for n in TWIN_WEIGHTS:
            delta_w[n], new_m[n], new_v[n] = _adamw(weights[n], grad_w[n], given["m_" + n], given["v_" + n])
    return (loss, grad_x, *[grad_w[n] for n in TWIN_WEIGHTS], *[delta_w[n] for n in TWIN_WEIGHTS],
            *[new_m[n] for n in TWIN_WEIGHTS], *[new_v[n] for n in TWIN_WEIGHTS])
```

```python
import collections
import functools

import jax
import jax.numpy as jnp
from jax import lax
from jax.experimental import pallas as pl
from jax.experimental.pallas import tpu as pltpu

D = 1024
FF = 2816
N_HEADS = 16
HEAD_DIM = 64
N_KV_HEADS = 4
GQA = N_HEADS // N_KV_HEADS
KVD = N_KV_HEADS * HEAD_DIM
PLE = 256
BLK = 128
WINDOWS = (2, 4, 8, 16)
POOL_G = 256
HALO = 16
EPS = 1e-6
NEG_INF = -1e30
ATT_SCALE = HEAD_DIM ** -0.5
SLOPES = tuple(2.0 ** (-8.0 * (h + 1) / N_HEADS) for h in range(N_HEADS))
N_CHIPS = 4
N_DEV = 8

LR, B1, B2, AEPS, WD, STEP = 0.001, 0.9, 0.999, 1e-08, 0.01, 10
BC1 = 1.0 - B1 ** STEP
BC2 = 1.0 - B2 ** STEP

BF = jnp.bfloat16
F32 = jnp.float32
MESH = pl.DeviceIdType.MESH
VMEM_LIMIT_V7X = 56 * 1024 * 1024
TM = 256
TM_FFN_BWD = 512
FF_CHUNK = 256
FF_HALF = FF // 2

VSPEC = pl.BlockSpec(memory_space=pltpu.VMEM)
SSPEC = pl.BlockSpec(memory_space=pltpu.SMEM)
ANYSPEC = pl.BlockSpec(memory_space=pl.ANY)


def _params(n_grid=0, **kw):
    sem = ("arbitrary",) * n_grid if n_grid else None
    return pltpu.CompilerParams(dimension_semantics=sem, vmem_limit_bytes=VMEM_LIMIT_V7X, **kw)


def _rms_fwd(x, g):
    r = lax.rsqrt(jnp.mean(x * x, axis=-1, keepdims=True) + EPS)
    return x * r * g


def _rms_bwd(x, g, dy):
    r = lax.rsqrt(jnp.mean(x * x, axis=-1, keepdims=True) + EPS)
    xn = x * r
    dxn = dy * g
    dx = r * (dxn - xn * jnp.mean(dxn * xn, axis=-1, keepdims=True))
    return dx, dy * xn


def _rowsum(a):
    return jnp.sum(a, axis=0, keepdims=True)


def _sigmoid(z):
    return 1.0 / (1.0 + jnp.exp(-z))


def _dot(a, b):
    return jnp.dot(a, b, preferred_element_type=F32)


def _dot_nt(a, b):
    return lax.dot_general(a, b, (((1,), (1,)), ((), ())), preferred_element_type=F32)


def _dot_tn(a, b):
    return lax.dot_general(a, b, (((0,), (0,)), ((), ())), preferred_element_type=F32)


def _row_spec(tm, width=D):
    return pl.BlockSpec((tm, width), lambda i: (i, 0))


def _const_spec(shape):
    zeros = (0,) * len(shape)
    return pl.BlockSpec(tuple(shape), lambda *_: zeros)


def _layer_spec(layer, shape):
    zeros = (0,) * len(shape)
    return pl.BlockSpec((None,) + tuple(shape), lambda *_: (layer,) + zeros, pipeline_mode=pl.Buffered(1))


def _sds(shape, dtype=F32):
    return jax.ShapeDtypeStruct(tuple(shape), dtype)


def _pool_delta(he, pos, first):
    del first
    out = []
    for gi, w in enumerate(WINDOWS):
        hg = he[:, gi * POOL_G:(gi + 1) * POOL_G]
        s = hg
        k = 1
        while k < w:
            s = s + pltpu.roll(s, k, 0)
            k *= 2
        cnt = jnp.maximum(jnp.minimum(pos + 1, w), 1).astype(F32)
        out.append(s / cnt - hg)
    return out


def _load_with_halo_before(x_ref, i, tm):
    r0 = pl.multiple_of(i * tm, tm)
    hs = pl.multiple_of(jnp.maximum(i * tm - HALO, 0), 8)
    xh = jnp.where(i > 0, x_ref[pl.ds(hs, HALO), :], 0.0)
    xt = x_ref[pl.ds(r0, tm), :]
    return xt, jnp.concatenate([xh, xt], axis=0)


def _mixa_fwd(x, pre_g, pool_w, pool_scale, post_g):
    s_len = x.shape[0]
    n = s_len // TM

    def body(x_ref, pg_ref, w_ref, sc_ref, qg_ref, y_ref, x1_ref):
        i = pl.program_id(0)
        xt, xe = _load_with_halo_before(x_ref, i, TM)
        he = _rms_fwd(xe, pg_ref[0:1, :])
        pos = i * TM - HALO + lax.broadcasted_iota(jnp.int32, (TM + HALO, 1), 0)
        ds = _pool_delta(he, pos, HALO)
        ys = [_dot(ds[gi][HALO:, :].astype(BF), w_ref[gi]) for gi in range(len(WINDOWS))]
        y = jnp.concatenate(ys, axis=1) * sc_ref[...]
        y_ref[...] = y
        x1_ref[...] = xt + _rms_fwd(y, qg_ref[0:1, :])

    return pl.pallas_call(
        body, name="mixa_fwd", grid=(n,),
        in_specs=[VSPEC, VSPEC, VSPEC, VSPEC, VSPEC],
        out_specs=[_row_spec(TM), _row_spec(TM)],
        out_shape=[_sds((s_len, D)), _sds((s_len, D))],
        compiler_params=_params(1),
    )(x, pre_g, pool_w, pool_scale, post_g)


def _mixa_bwd(dx1, x, y, pre_g, pool_w, pool_scale, post_g):
    s_len = x.shape[0]
    n = s_len // TM
    ng = len(WINDOWS)

    def body(dx_ref, x_ref, y_ref, pg_ref, w_ref, sc_ref, qg_ref,
             dx0_ref, dw_ref, dsc_ref, dqg_ref, dpg_ref, wacc):
        i = pl.program_id(0)

        @pl.when(i == 0)
        def _():
            wacc[...] = jnp.zeros_like(wacc)
            dsc_ref[...] = jnp.zeros_like(dsc_ref)
            dqg_ref[...] = jnp.zeros_like(dqg_ref)
            dpg_ref[...] = jnp.zeros_like(dpg_ref)

        r0 = pl.multiple_of(i * TM, TM)
        xt, xe = _load_with_halo_before(x_ref, i, TM)
        he = _rms_fwd(xe, pg_ref[0:1, :])
        pos_b = i * TM - HALO + lax.broadcasted_iota(jnp.int32, (TM + HALO, 1), 0)
        ds = _pool_delta(he, pos_b, HALO)

        last = i == n - 1
        a0 = pl.multiple_of(jnp.minimum(i * TM + TM, s_len - HALO), 8)
        ye = jnp.concatenate([y_ref[pl.ds(r0, TM), :], y_ref[pl.ds(a0, HALO), :]], axis=0)
        dt = dx_ref[pl.ds(r0, TM), :]
        de = jnp.concatenate([dt, jnp.where(last, 0.0, dx_ref[pl.ds(a0, HALO), :])], axis=0)
        dye, prod = _rms_bwd(ye, qg_ref[0:1, :], de)
        dqg_ref[...] += _rowsum(prod[:TM, :])
        dys = dye * sc_ref[...]
        pos_a = i * TM + lax.broadcasted_iota(jnp.int32, (TM + HALO, 1), 0)

        dhs, dscs = [], []
        for gi, w in enumerate(WINDOWS):
            sl = slice(gi * POOL_G, (gi + 1) * POOL_G)
            wg = w_ref[gi]
            dys_g = dys[:, sl].astype(BF)
            d_g = ds[gi][HALO:, :].astype(BF)
            ypre = _dot(d_g, wg)
            dscs.append(_rowsum(dye[:TM, sl] * ypre))
            wacc[gi] += _dot_tn(d_g, dys_g[:TM, :])
            dd = _dot_nt(dys_g, wg)
            cnt = jnp.minimum(pos_a + 1, w).astype(F32)
            a = dd / cnt
            k = 1
            while k < w:
                a = a + pltpu.roll(a, TM + HALO - k, 0)
                k *= 2
            dhs.append(a[:TM, :] - dd[:TM, :])
        dsc_ref[...] += jnp.concatenate(dscs, axis=1)
        dh = jnp.concatenate(dhs, axis=1)
        dxp, prod2 = _rms_bwd(xt, pg_ref[0:1, :], dh)
        dpg_ref[...] += _rowsum(prod2)
        dx0_ref[...] = dt + dxp

        @pl.when(last)
        def _():
            dw_ref[...] = wacc[...].astype(BF)

    return pl.pallas_call(
        body, name="mixa_bwd", grid=(n,),
        in_specs=[VSPEC] * 7,
        out_specs=[_row_spec(TM), _const_spec((ng, POOL_G, POOL_G)), _const_spec((1, D)),
                   _const_spec((1, D)), _const_spec((1, D))],
        out_shape=[_sds((s_len, D)), _sds((ng, POOL_G, POOL_G), BF), _sds((1, D)), _sds((1, D)), _sds((1, D))],
        scratch_shapes=[pltpu.VMEM((ng, POOL_G, POOL_G), F32)],
        compiler_params=_params(1),
    )(dx1, x, y, pre_g, pool_w, pool_scale, post_g)


def _ffn_fwd(layer, x1, pre_g, wgu, wd, post_g):
    s_len = x1.shape[0]

    def body(x_ref, pg_ref, wgu_ref, wd_ref, qg_ref, f_ref, x2_ref):
        x = x_ref[...]
        h = _rms_fwd(x, pg_ref[layer:layer + 1, :]).astype(BF)
        f = jnp.zeros((TM, D), F32)
        for c in range(FF // FF_HALF):
            cols = slice(c * FF_HALF, (c + 1) * FF_HALF)
            g = _dot(h, wgu_ref[0, :, cols])
            u = _dot(h, wgu_ref[1, :, cols])
            act = g * _sigmoid(g) * u
            f = f + _dot(act.astype(BF), wd_ref[cols, :])
        f_ref[...] = f
        x2_ref[...] = x + _rms_fwd(f, qg_ref[layer:layer + 1, :])

    return pl.pallas_call(
        body, name=f"ffn_fwd{layer}", grid=(s_len // TM,),
        in_specs=[_row_spec(TM), VSPEC, _layer_spec(layer, (2, D, FF)), _layer_spec(layer, (FF, D)), VSPEC],
        out_specs=[_row_spec(TM), _row_spec(TM)],
        out_shape=[_sds((s_len, D)), _sds((s_len, D))],
        compiler_params=_params(1),
    )(x1, pre_g, wgu, wd, post_g)


def _ffn_bwd(layer, dx2, x1, f, pre_g, wgu, wd, post_g, dwgu_prev, dwd_prev):
    s_len = x1.shape[0]
    tm = TM_FFN_BWD
    n = s_len // tm
    nc = FF // FF_CHUNK
    n_layers = wgu.shape[0]

    def edge_rows(c, i):
        return (jnp.where((c == 0) | (c == nc - 1), i, n - 1), 0)

    def body(*refs):
        if dwgu_prev is None:
            (dx_ref, x_ref, f_ref, pg_ref, wgu_ref, wd_ref, qg_ref,
             dx1_ref, dwgu_ref, dwd_ref, dpg_ref, dqg_ref, h_s, df_s, dh_s, accg, accu, accd) = refs
        else:
            (dx_ref, x_ref, f_ref, pg_ref, wgu_ref, wd_ref, qg_ref, _, _,
             dx1_ref, dwgu_ref, dwd_ref, dpg_ref, dqg_ref, h_s, df_s, dh_s, accg, accu, accd) = refs
        c = pl.program_id(0)
        i = pl.program_id(1)
        rows = pl.ds(pl.multiple_of(i * tm, tm), tm)
        pg = pg_ref[layer:layer + 1, :]

        @pl.when((c == 0) & (i == 0))
        def _():
            dpg_ref[...] = jnp.zeros_like(dpg_ref)
            dqg_ref[...] = jnp.zeros_like(dqg_ref)

        @pl.when(c == 0)
        def _():
            h_s[rows, :] = _rms_fwd(x_ref[...], pg).astype(BF)
            df, prod = _rms_bwd(f_ref[...], qg_ref[layer:layer + 1, :], dx_ref[...])
            df_s[rows, :] = df.astype(BF)
            dqg_ref[...] += _rowsum(prod)

        @pl.when(i == 0)
        def _():
            accg[...] = jnp.zeros_like(accg)
            accu[...] = jnp.zeros_like(accu)
            accd[...] = jnp.zeros_like(accd)

        h = h_s[rows, :]
        df = df_s[rows, :]
        wg = wgu_ref[0]
        wu = wgu_ref[1]
        g = _dot(h, wg)
        u = _dot(h, wu)
        sg = _sigmoid(g)
        a = g * sg
        dact = _dot_nt(df, wd_ref[...])
        accd[...] += _dot_tn((a * u).astype(BF), df)
        du = (dact * a).astype(BF)
        dg = (dact * u * (sg * (1.0 + g * (1.0 - sg)))).astype(BF)
        accg[...] += _dot_tn(h, dg)
        accu[...] += _dot_tn(h, du)
        dh = _dot_nt(dg, wg) + _dot_nt(du, wu)

        @pl.when(c == 0)
        def _():
            dh_s[rows, :] = dh

        @pl.when((c > 0) & (c < nc - 1))
        def _():
            dh_s[rows, :] += dh

        @pl.when(c == nc - 1)
        def _():
            dxp, prod = _rms_bwd(x_ref[...], pg, dh_s[rows, :] + dh)
            dpg_ref[...] += _rowsum(prod)
            dx1_ref[...] = dx_ref[...] + dxp

        @pl.when(i == n - 1)
        def _():
            dwgu_ref[0] = accg[...].astype(BF)
            dwgu_ref[1] = accu[...].astype(BF)
            dwd_ref[...] = accd[...].astype(BF)

    in_specs = [
        pl.BlockSpec((tm, D), edge_rows),
        pl.BlockSpec((tm, D), edge_rows),
        pl.BlockSpec((tm, D), lambda c, i: (jnp.where(c == 0, i, n - 1), 0)),
        VSPEC,
        pl.BlockSpec((None, 2, D, FF_CHUNK), lambda c, i: (layer, 0, 0, c)),
        pl.BlockSpec((None, FF_CHUNK, D), lambda c, i: (layer, c, 0)),
        VSPEC,
    ]
    args = [dx2, x1, f, pre_g, wgu, wd, post_g]
    aliases = {}
    if dwgu_prev is not None:
        in_specs += [ANYSPEC, ANYSPEC]
        args += [dwgu_prev, dwd_prev]
        aliases = {7: 1, 8: 2}
    return pl.pallas_call(
        body, name=f"ffn_bwd{layer}", grid=(nc, n),
        in_specs=in_specs,
        out_specs=[pl.BlockSpec((tm, D), lambda c, i: (jnp.where(c == nc - 1, i, 0), 0)),
                   pl.BlockSpec((None, 2, D, FF_CHUNK), lambda c, i: (layer, 0, 0, c)),
                   pl.BlockSpec((None, FF_CHUNK, D), lambda c, i: (layer, c, 0)),
                   _const_spec((1, D)), _const_spec((1, D))],
        out_shape=[_sds((s_len, D)), _sds((n_layers, 2, D, FF), BF), _sds((n_layers, FF, D), BF),
                   _sds((1, D)), _sds((1, D))],
        scratch_shapes=[pltpu.VMEM((s_len, D), BF), pltpu.VMEM((s_len, D), BF), pltpu.VMEM((s_len, D), F32),
                        pltpu.VMEM((D, FF_CHUNK), F32), pltpu.VMEM((D, FF_CHUNK), F32),
                        pltpu.VMEM((FF_CHUNK, D), F32)],
        input_output_aliases=aliases,
        compiler_params=_params(2),
    )(*args)


def _ple_fwd(layer, x2, p, ple_g, w_gate, w_proj, post_g, target=None):
    s_len = x2.shape[0]
    final = target is not None

    def body(*refs):
        if final:
            x_ref, p_ref, g_ref, wg_ref, wp_ref, qg_ref, t_ref, z_ref, pe_ref, dx_ref, lv_ref = refs
        else:
            x_ref, p_ref, g_ref, wg_ref, wp_ref, qg_ref, z_ref, pe_ref, x3_ref = refs
        x = x_ref[...]
        r = _rms_fwd(x, g_ref[layer:layer + 1, :]).astype(BF)
        z = _dot(r, wg_ref[...])
        pe = _dot(p_ref[...].astype(BF), wp_ref[...])
        z_ref[...] = z
        pe_ref[...] = pe
        x3 = x + _rms_fwd(pe * _sigmoid(z), qg_ref[layer:layer + 1, :])
        if final:
            @pl.when(pl.program_id(0) == 0)
            def _():
                lv_ref[...] = jnp.zeros_like(lv_ref)
            err = x3 - t_ref[...]
            dx_ref[...] = err * (1.0 / D)
            lv_ref[...] += _rowsum(err * err)
        else:
            x3_ref[...] = x3

    in_specs = [_row_spec(TM), _row_spec(TM, PLE), VSPEC, _layer_spec(layer, (D, D)),
                _layer_spec(layer, (PLE, D)), VSPEC]
    args = [x2, p, ple_g, w_gate, w_proj, post_g]
    out_specs = [_row_spec(TM), _row_spec(TM), _row_spec(TM)]
    out_shape = [_sds((s_len, D))] * 3
    if final:
        in_specs.append(_row_spec(TM))
        args.append(target)
        out_specs.append(_const_spec((1, D)))
        out_shape.append(_sds((1, D)))
    return pl.pallas_call(
        body, name=f"ple_fwd{layer}", grid=(s_len // TM,),
        in_specs=in_specs, out_specs=out_specs, out_shape=out_shape,
        compiler_params=_params(1),
    )(*args)


def _ple_bwd(layer, dx3, x2, z, pe, p, ple_g, w_gate, post_g, dwg_prev, dwp_prev):
    s_len = x2.shape[0]
    n = s_len // TM
    n_layers = w_gate.shape[0]

    def body(*refs):
        if dwg_prev is None:
            (dx_ref, x_ref, z_ref, pe_ref, p_ref, g_ref, wg_ref, qg_ref,
             dx2_ref, dwg_ref, dwp_ref, dg_ref, dqg_ref, gacc, pacc) = refs
        else:
            (dx_ref, x_ref, z_ref, pe_ref, p_ref, g_ref, wg_ref, qg_ref, _, _,
             dx2_ref, dwg_ref, dwp_ref, dg_ref, dqg_ref, gacc, pacc) = refs
        i = pl.program_id(0)

        @pl.when(i == 0)
        def _():
            gacc[...] = jnp.zeros_like(gacc)
            pacc[...] = jnp.zeros_like(pacc)
            dg_ref[...] = jnp.zeros_like(dg_ref)
            dqg_ref[...] = jnp.zeros_like(dqg_ref)

        dx = dx_ref[...]
        x = x_ref[...]
        pe_v = pe_ref[...]
        gate = _sigmoid(z_ref[...])
        de, prod = _rms_bwd(pe_v * gate, qg_ref[layer:layer + 1, :], dx)
        dqg_ref[...] += _rowsum(prod)
        dpe = (de * gate).astype(BF)
        dz = (de * pe_v * gate * (1.0 - gate)).astype(BF)
        pacc[...] += _dot_tn(p_ref[...].astype(BF), dpe)
        g = g_ref[layer:layer + 1, :]
        r = _rms_fwd(x, g).astype(BF)
        gacc[...] += _dot_tn(r, dz)
        dr = _dot_nt(dz, wg_ref[...])
        dxp, prod2 = _rms_bwd(x, g, dr)
        dg_ref[...] += _rowsum(prod2)
        dx2_ref[...] = dx + dxp

        @pl.when(i == n - 1)
        def _():
            dwg_ref[...] = gacc[...].astype(BF)
            dwp_ref[...] = pacc[...].astype(BF)

    in_specs = [_row_spec(TM), _row_spec(TM), _row_spec(TM), _row_spec(TM), _row_spec(TM, PLE), VSPEC,
                _layer_spec(layer, (D, D)), VSPEC]
    args = [dx3, x2, z, pe, p, ple_g, w_gate, post_g]
    aliases = {}
    if dwg_prev is not None:
        in_specs += [ANYSPEC, ANYSPEC]
        args += [dwg_prev, dwp_prev]
        aliases = {8: 1, 9: 2}
    return pl.pallas_call(
        body, name=f"ple_bwd{layer}", grid=(n,),
        in_specs=in_specs,
        out_specs=[_row_spec(TM),
                   pl.BlockSpec((None, D, D), lambda i: (layer, 0, 0)),
                   pl.BlockSpec((None, PLE, D), lambda i: (layer, 0, 0)),
                   _const_spec((1, D)), _const_spec((1, D))],
        out_shape=[_sds((s_len, D)), _sds((n_layers, D, D), BF), _sds((n_layers, PLE, D), BF),
                   _sds((1, D)), _sds((1, D))],
        scratch_shapes=[pltpu.VMEM((D, D), F32), pltpu.VMEM((PLE, D), F32)],
        input_output_aliases=aliases,
        compiler_params=_params(1),
    )(*args)


def _qkv_fwd(x3, q_g, kv_g, w_q, w_kv):
    s_len = x3.shape[0]

    def body(x_ref, qg_ref, kg_ref, wq_ref, wkv_ref, q_ref, kv_ref):
        x = x_ref[...]
        q_ref[...] = _dot(_rms_fwd(x, qg_ref[1:2, :]).astype(BF), wq_ref[...]).astype(BF)
        kv_ref[...] = _dot(_rms_fwd(x, kg_ref[...]).astype(BF), wkv_ref[...]).astype(BF)

    return pl.pallas_call(
        body, name="qkv_fwd", grid=(s_len // TM,),
        in_specs=[_row_spec(TM), VSPEC, VSPEC, VSPEC, VSPEC],
        out_specs=[_row_spec(TM), _row_spec(TM, 2 * KVD)],
        out_shape=[_sds((s_len, D), BF), _sds((s_len, 2 * KVD), BF)],
        compiler_params=_params(1),
    )(x3, q_g, kv_g, w_q, w_kv)


def _qkv_bwd(dq, dkv, x3, dx4, q_g, kv_g, w_q, w_kv):
    s_len = x3.shape[0]
    n = s_len // TM

    def body(dq_ref, dkv_ref, x_ref, dx_ref, qg_ref, kg_ref, wq_ref, wkv_ref,
             dx3_ref, dwq_ref, dwkv_ref, dqg_ref, dkg_ref, qacc, kacc):
        i = pl.program_id(0)

        @pl.when(i == 0)
        def _():
            qacc[...] = jnp.zeros_like(qacc)
            kacc[...] = jnp.zeros_like(kacc)
            dqg_ref[...] = jnp.zeros_like(dqg_ref)
            dkg_ref[...] = jnp.zeros_like(dkg_ref)

        x = x_ref[...]
        qg = qg_ref[1:2, :]
        kg = kg_ref[...]
        dq_v = dq_ref[...]
        dkv_v = dkv_ref[...].astype(BF)
        qacc[...] += _dot_tn(_rms_fwd(x, qg).astype(BF), dq_v)
        kacc[...] += _dot_tn(_rms_fwd(x, kg).astype(BF), dkv_v)
        dxq, prod_q = _rms_bwd(x, qg, _dot_nt(dq_v, wq_ref[...]))
        dxk, prod_k = _rms_bwd(x, kg, _dot_nt(dkv_v, wkv_ref[...]))
        dqg_ref[...] += _rowsum(prod_q)
        dkg_ref[...] += _rowsum(prod_k)
        dx3_ref[...] = dx_ref[...] + dxq + dxk

        @pl.when(i == n - 1)
        def _():
            dwq_ref[...] = qacc[...].astype(BF)
            dwkv_ref[...] = kacc[...].astype(BF)

    return pl.pallas_call(
        body, name="qkv_bwd", grid=(n,),
        in_specs=[_row_spec(TM), _row_spec(TM, 2 * KVD), _row_spec(TM), _row_spec(TM), VSPEC, VSPEC, VSPEC, VSPEC],
        out_specs=[_row_spec(TM), _const_spec((D, D)), _const_spec((D, 2 * KVD)),
                   _const_spec((1, D)), _const_spec((1, D))],
        out_shape=[_sds((s_len, D)), _sds((D, D), BF), _sds((D, 2 * KVD), BF), _sds((1, D)), _sds((1, D))],
        scratch_shapes=[pltpu.VMEM((D, D), F32), pltpu.VMEM((D, 2 * KVD), F32)],
        compiler_params=_params(1),
    )(dq, dkv, x3, dx4, q_g, kv_g, w_q, w_kv)


def _attn_block(i, q, kvw, sink_ref):
    off = jnp.where(i > 0, BLK, 0)
    rel = (lax.broadcasted_iota(jnp.int32, (BLK, 2 * BLK), 0)
           - lax.broadcasted_iota(jnp.int32, (BLK, 2 * BLK), 1) + off)
    valid = (rel >= 0) & (rel < BLK)
    relf = rel.astype(F32)
    out = []
    for h in range(N_HEADS):
        kh = h // GQA
        qh = q[:, h * HEAD_DIM:(h + 1) * HEAD_DIM]
        k = kvw[:, kh * HEAD_DIM:(kh + 1) * HEAD_DIM]
        v = kvw[:, KVD + kh * HEAD_DIM:KVD + (kh + 1) * HEAD_DIM]
        s = _dot_nt(qh, k) * ATT_SCALE - SLOPES[h] * relf
        s = jnp.where(valid, s, NEG_INF)
        sink = sink_ref[0, h]
        m = jnp.maximum(jnp.max(s, axis=-1, keepdims=True), sink)
        e = jnp.exp(s - m)
        es = jnp.exp(sink - m)
        inv = 1.0 / (jnp.sum(e, axis=-1, keepdims=True) + es)
        out.append((e * inv, es * inv, qh, k, v))
    return out


def _kv_window(kv_ref, i):
    ks = pl.multiple_of(jnp.maximum(i * BLK - BLK, 0), BLK)
    return ks, kv_ref[pl.ds(ks, 2 * BLK), :]


def _attn_fwd(q, kv, sinks, x3, w_o, post_g):
    s_len = q.shape[0]

    def body(q_ref, kv_ref, sk_ref, x_ref, wo_ref, g_ref, a_ref, y_ref, x4_ref):
        i = pl.program_id(0)
        _, kvw = _kv_window(kv_ref, i)
        heads = _attn_block(i, q_ref[...], kvw, sk_ref)
        attn = jnp.concatenate([_dot(p.astype(BF), v) for p, _, _, _, v in heads], axis=1)
        a_ref[...] = attn
        y = _dot(attn.astype(BF), wo_ref[...])
        y_ref[...] = y
        x4_ref[...] = x_ref[...] + _rms_fwd(y, g_ref[1:2, :])

    return pl.pallas_call(
        body, name="attn_fwd", grid=(s_len // BLK,),
        in_specs=[_row_spec(BLK), VSPEC, SSPEC, _row_spec(BLK), VSPEC, VSPEC],
        out_specs=[_row_spec(BLK)] * 3,
        out_shape=[_sds((s_len, D))] * 3,
        compiler_params=_params(1),
    )(q, kv, sinks, x3, w_o, post_g)


def _attn_bwd(dx4, y, attn, q, kv, sinks, w_o, post_g):
    s_len = q.shape[0]
    n = s_len // BLK

    def body(dx_ref, y_ref, a_ref, q_ref, kv_ref, sk_ref, wo_ref, g_ref,
             dq_ref, dkv_ref, dwo_ref, dg_ref, dsk_ref, wacc):
        i = pl.program_id(0)

        @pl.when(i == 0)
        def _():
            dkv_ref[...] = jnp.zeros_like(dkv_ref)
            wacc[...] = jnp.zeros_like(wacc)
            dg_ref[...] = jnp.zeros_like(dg_ref)
            dsk_ref[...] = jnp.zeros_like(dsk_ref)

        dy, prod = _rms_bwd(y_ref[...], g_ref[1:2, :], dx_ref[...])
        dg_ref[...] += _rowsum(prod)
        dyb = dy.astype(BF)
        attn = a_ref[...]
        wacc[...] += _dot_tn(attn.astype(BF), dyb)
        d_o = _dot_nt(dyb, wo_ref[...])
        dod = d_o * attn
        ks, kvw = _kv_window(kv_ref, i)
        heads = _attn_block(i, q_ref[...], kvw, sk_ref)
        lane = lax.broadcasted_iota(jnp.int32, (1, D), 1)
        dqs = []
        dks = [None] * N_KV_HEADS
        dvs = [None] * N_KV_HEADS
        dsk = jnp.zeros((1, D), F32)
        for h, (p, ps, qh, k, v) in enumerate(heads):
            kh = h // GQA
            hs = slice(h * HEAD_DIM, (h + 1) * HEAD_DIM)
            do_h = d_o[:, hs].astype(BF)
            dsum = jnp.sum(dod[:, hs], axis=-1, keepdims=True)
            dp = _dot_nt(do_h, v)
            dsb = (p * (dp - dsum) * ATT_SCALE).astype(BF)
            dsk = dsk + jnp.where(lane == h, -_rowsum(ps * dsum), 0.0)
            dqs.append(_dot(dsb, k))
            dk = _dot_tn(dsb, qh)
            dv = _dot_tn(p.astype(BF), do_h)
            dks[kh] = dk if dks[kh] is None else dks[kh] + dk
            dvs[kh] = dv if dvs[kh] is None else dvs[kh] + dv
        dsk_ref[...] += dsk
        dq_ref[...] = jnp.concatenate(dqs, axis=1).astype(BF)
        dkv_ref[pl.ds(ks, 2 * BLK), :] += jnp.concatenate(dks + dvs, axis=1)

        @pl.when(i == n - 1)
        def _():
            dwo_ref[...] = wacc[...].astype(BF)

    return pl.pallas_call(
        body, name="attn_bwd", grid=(n,),
        in_specs=[_row_spec(BLK), _row_spec(BLK), _row_spec(BLK), _row_spec(BLK), VSPEC, SSPEC, VSPEC, VSPEC],
        out_specs=[_row_spec(BLK), _const_spec((s_len, 2 * KVD)), _const_spec((D, D)),
                   _const_spec((1, D)), _const_spec((1, D))],
        out_shape=[_sds((s_len, D), BF), _sds((s_len, 2 * KVD)), _sds((D, D), BF), _sds((1, D)), _sds((1, D))],
        scratch_shapes=[pltpu.VMEM((D, D), F32)],
        compiler_params=_params(1),
    )(dx4, y, attn, q, kv, sinks, w_o, post_g)


Big = collections.namedtuple("Big", "name L A R C rb")

BIGS = (
    Big("w_gu", 2, 2, D, FF_HALF, 256),
    Big("w_down", 2, 4, FF // N_CHIPS, D, 352),
    Big("w_ple_gate", 2, 4, D // N_CHIPS, D, 128),
    Big("w_ple_proj", 2, 1, PLE, D // N_CHIPS, 128),
    Big("w_q", 1, 4, D // N_CHIPS, D, 128),
    Big("w_o", 1, 4, D // N_CHIPS, D, 128),
    Big("w_kv", 1, 4, D // N_CHIPS, 2 * KVD, 128),
    Big("pool_w", 4, 4, POOL_G // N_CHIPS, POOL_G, 32),
)
POOL_SCALE = Big("pool_scale", 1, 1, 1, D // N_CHIPS, 1)


def _ncb(t):
    return N_CHIPS // t.A


def _full_shape(t, rows=None):
    return (t.L, t.A, t.R if rows is None else rows, _ncb(t) * t.C)


def _slot_index(t, k):
    return k // _ncb(t), k % _ncb(t)


def _slot(ref, t, k, row0, rows):
    a, cb = _slot_index(t, k)
    return ref.at[:, a, pl.ds(row0, rows), pl.ds(pl.multiple_of(cb * t.C, 128), t.C)]


def _place(t, w, kc, out_dtype):
    rb = min(t.R, 2 * t.rb)

    def body(kc_ref, w_ref, o_ref):
        del kc_ref
        o_ref[...] = w_ref[...].astype(out_dtype)

    def out_map(l, j, kc_ref):
        a, cb = _slot_index(t, kc_ref[0])
        return (l, a, j, cb)

    return pl.pallas_call(
        body, name=f"place_{t.name}",
        grid_spec=pltpu.PrefetchScalarGridSpec(
            num_scalar_prefetch=1, grid=(t.L, t.R // rb),
            in_specs=[pl.BlockSpec((None, rb, t.C), lambda l, j, kc_ref: (l, j, 0))],
            out_specs=pl.BlockSpec((None, None, rb, t.C), out_map)),
        out_shape=_sds(_full_shape(t), out_dtype),
        compiler_params=_params(2),
    )(kc, w)


def _mesh_position():
    x, y, c = lax.axis_index("x"), lax.axis_index("y"), lax.axis_index("c")
    chips = [(1 - x, y), (x, 1 - y), (1 - x, 1 - y)]
    return x, y, c, chips


def _all_gather(fulls):
    specs = BIGS + (POOL_SCALE,)
    nt = len(specs)

    def body(*refs):
        outs = refs[nt:2 * nt]
        send, recv, fsend, frecv = refs[2 * nt:]
        x, y, c, chips = _mesh_position()
        sibling = (x, y, 1 - c)
        me = 2 * x + y

        def ici(ti, j, k_src, dev):
            t = specs[ti]
            split = t.R > 1
            rows = t.R // 2 if split else t.R
            region = _slot(outs[ti], t, k_src, c * rows if split else 0, rows)
            return pltpu.make_async_remote_copy(region, region, send.at[ti, j], recv.at[ti, j],
                                                device_id=dev, device_id_type=MESH)

        def d2d(ti, j, k_src, core):
            t = specs[ti]
            rows = t.R // 2
            region = _slot(outs[ti], t, k_src, core * rows, rows)
            return pltpu.make_async_remote_copy(region, region, fsend.at[ti, j], frecv.at[ti, j],
                                                device_id=sibling, device_id_type=MESH)

        started = []
        for j, (cx, cy) in enumerate(chips):
            for ti in range(nt):
                cp = ici(ti, j, me, (cx, cy, c))
                cp.start()
                started.append(cp)
        for j, (cx, cy) in enumerate(chips):
            k_src = 2 * cx + cy
            for ti in range(nt):
                ici(ti, j, k_src, (cx, cy, c)).wait_recv()
                if specs[ti].R > 1:
                    cp = d2d(ti, j, k_src, c)
                    cp.start()
                    started.append(cp)
        for j, (cx, cy) in enumerate(chips):
            k_src = 2 * cx + cy
            for ti in range(nt):
                if specs[ti].R > 1:
                    d2d(ti, j, k_src, 1 - c).wait_recv()
        for cp in started:
            cp.wait_send()

    sems = pltpu.SemaphoreType.DMA((nt, N_CHIPS - 1))
    return pl.pallas_call(
        body, name="weights_all_gather",
        in_specs=[ANYSPEC] * nt, out_specs=[ANYSPEC] * nt,
        out_shape=[_sds(a.shape, a.dtype) for a in fulls],
        scratch_shapes=[sems, sems, sems, sems],
        input_output_aliases={i: i for i in range(nt)},
        compiler_params=_params(),
    )(*fulls)


def _pair_exchange(grads):
    nt = len(BIGS)

    def body(*refs):
        gs = refs[:nt]
        lands = refs[nt:2 * nt]
        send, recv = refs[2 * nt:]
        x, y, c, _ = _mesh_position()
        cps = []
        for ti, t in enumerate(BIGS):
            half = t.R // 2
            cp = pltpu.make_async_remote_copy(gs[ti].at[:, :, pl.ds((1 - c) * half, half), :], lands[ti],
                                              send.at[ti], recv.at[ti],
                                              device_id=(x, y, 1 - c), device_id_type=MESH)
            cp.start()
            cps.append(cp)
        for cp in cps:
            cp.wait()

    return pl.pallas_call(
        body, name="grads_pair_exchange",
        in_specs=[ANYSPEC] * nt, out_specs=[ANYSPEC] * nt,
        out_shape=[_sds(_full_shape(t, t.R // 2), BF) for t in BIGS],
        scratch_shapes=[pltpu.SemaphoreType.DMA((nt,)), pltpu.SemaphoreType.DMA((nt,))],
        compiler_params=_params(),
    )(*grads)


def _pair_sum(t, g, land, kc):
    half = t.R // 2
    nj = half // t.rb
    w = _ncb(t) * t.C

    def body(kc_ref, g_ref, l_ref, o_ref):
        del kc_ref
        o_ref[...] = (g_ref[...].astype(F32) + l_ref[...].astype(F32)).astype(BF)

    return pl.pallas_call(
        body, name=f"pair_sum_{t.name}",
        grid_spec=pltpu.PrefetchScalarGridSpec(
            num_scalar_prefetch=1, grid=(t.L, t.A, nj),
            in_specs=[pl.BlockSpec((None, None, t.rb, w), lambda l, a, j, kc_ref: (l, a, kc_ref[1] * nj + j, 0)),
                      pl.BlockSpec((None, None, t.rb, w), lambda l, a, j, kc_ref: (l, a, j, 0))],
            out_specs=pl.BlockSpec((None, None, t.rb, w), lambda l, a, j, kc_ref: (l, a, j, 0))),
        out_shape=_sds(_full_shape(t, half), BF),
        compiler_params=_params(3),
    )(kc, g, land)


def _chip_exchange(sums):
    nt = len(BIGS)

    def body(*refs):
        ss = refs[:nt]
        lands = refs[nt:2 * nt]
        send, recv = refs[2 * nt:]
        x, y, c, chips = _mesh_position()
        cps = []
        for j, (cx, cy) in enumerate(chips):
            for ti, t in enumerate(BIGS):
                cp = pltpu.make_async_remote_copy(_slot(ss[ti], t, 2 * cx + cy, 0, t.R // 2), lands[ti].at[j],
                                                  send.at[ti, j], recv.at[ti, j],
                                                  device_id=(cx, cy, c), device_id_type=MESH)
                cp.start()
                cps.append(cp)
        for cp in cps:
            cp.wait()

    sems = pltpu.SemaphoreType.DMA((nt, N_CHIPS - 1))
    return pl.pallas_call(
        body, name="grads_chip_exchange",
        in_specs=[ANYSPEC] * nt, out_specs=[ANYSPEC] * nt,
        out_shape=[_sds((N_CHIPS - 1, t.L, t.R // 2, t.C), BF) for t in BIGS],
        scratch_shapes=[sems, sems],
        compiler_params=_params(),
    )(*sums)


def _chip_sum(t, s, land, kc):
    half = t.R // 2
    nj = half // t.rb

    def body(kc_ref, s_ref, l_ref, o_ref):
        del kc_ref
        acc = s_ref[...].astype(F32)
        for j in range(N_CHIPS - 1):
            acc = acc + l_ref[j].astype(F32)
        o_ref[...] = acc

    def own_map(l, j, kc_ref):
        a, cb = _slot_index(t, kc_ref[0])
        return (l, a, j, cb)

    return pl.pallas_call(
        body, name=f"chip_sum_{t.name}",
        grid_spec=pltpu.PrefetchScalarGridSpec(
            num_scalar_prefetch=1, grid=(t.L, nj),
            in_specs=[pl.BlockSpec((None, None, t.rb, t.C), own_map),
                      pl.BlockSpec((N_CHIPS - 1, None, t.rb, t.C), lambda l, j, kc_ref: (0, l, j, 0))],
            out_specs=pl.BlockSpec((None, t.rb, t.C), lambda l, j, kc_ref: (l, kc_ref[1] * nj + j, 0))),
        out_shape=_sds((t.L, t.R, t.C)),
        compiler_params=_params(2),
    )(kc, s, land)


def _pair_share(halves):
    nt = len(BIGS)

    def body(*refs):
        outs = refs[nt:2 * nt]
        send, recv = refs[2 * nt:]
        x, y, c, _ = _mesh_position()
        cps = []
        for ti, t in enumerate(BIGS):
            half = t.R // 2
            mine = outs[ti].at[:, pl.ds(c * half, half), :]
            cp = pltpu.make_async_remote_copy(mine, mine, send.at[ti], recv.at[ti],
                                              device_id=(x, y, 1 - c), device_id_type=MESH)
            cp.start()
            cps.append(cp)
        for ti, t in enumerate(BIGS):
            half = t.R // 2
            theirs = outs[ti].at[:, pl.ds((1 - c) * half, half), :]
            pltpu.make_async_remote_copy(theirs, theirs, send.at[ti], recv.at[ti],
                                         device_id=(x, y, 1 - c), device_id_type=MESH).wait_recv()
        for cp in cps:
            cp.wait_send()

    return pl.pallas_call(
        body, name="grads_pair_share",
        in_specs=[ANYSPEC] * nt, out_specs=[ANYSPEC] * nt,
        out_shape=[_sds(a.shape, a.dtype) for a in halves],
        scratch_shapes=[pltpu.SemaphoreType.DMA((nt,)), pltpu.SemaphoreType.DMA((nt,))],
        input_output_aliases={i: i for i in range(nt)},
        compiler_params=_params(),
    )(*halves)


def _adamw_math(w, g, m, v):
    m = B1 * m + (1.0 - B1) * g
    v = B2 * v + (1.0 - B2) * (g * g)
    delta = -LR * ((m / BC1) / (jnp.sqrt(v / BC2) + AEPS) + WD * w)
    return delta, m, v


def _adamw(t, w, g, m, v):
    rb = t.rb

    def body(w_ref, g_ref, m_ref, v_ref, d_ref, nm_ref, nv_ref):
        d_ref[...], nm_ref[...], nv_ref[...] = _adamw_math(w_ref[...], g_ref[...], m_ref[...], v_ref[...])

    spec = pl.BlockSpec((None, rb, t.C), lambda l, j: (l, j, 0))
    shape = _sds((t.L, t.R, t.C))
    return pl.pallas_call(
        body, name=f"adamw_{t.name}", grid=(t.L, t.R // rb),
        in_specs=[spec] * 4, out_specs=[spec] * 3, out_shape=[shape] * 3,
        compiler_params=_params(2),
    )(w, g, m, v)


GAIN_ROWS = {"pre_mix_g": 0, "post_mix_g": 2, "pre_ffn_g": 4, "post_ffn_g": 6, "ple_g": 8, "ple_post_g": 10}
ROW_KV_G, ROW_POOL_SCALE, ROW_SINKS, ROW_LOSS, PACK_ROWS = 12, 13, 14, 15, 16


def _small_sync(rows, small_w, small_m, small_v):
    names = list(GAIN_ROWS) + ["kv_g", "pool_scale", "sinks"]

    def body(*refs):
        row_refs = refs[:PACK_ROWS]
        pos = PACK_ROWS
        w_refs = dict(zip(names, refs[pos:pos + len(names)])); pos += len(names)
        m_refs = dict(zip(names, refs[pos:pos + len(names)])); pos += len(names)
        v_refs = dict(zip(names, refs[pos:pos + len(names)])); pos += len(names)
        loss_ref = refs[pos]; pos += 1
        out_refs = {nm: refs[pos + 4 * k: pos + 4 * k + 4] for k, nm in enumerate(names)}
        pos += 4 * len(names)
        pack, land, send, recv = refs[pos:]

        x, y, c, _ = _mesh_position()
        me = 4 * x + 2 * y + c
        for r in range(PACK_ROWS):
            pack[r:r + 1, :] = row_refs[r][...]
        cps = []
        for j in range(1, N_DEV):
            peer = (x ^ (j >> 2), y ^ ((j >> 1) & 1), c ^ (j & 1))
            cp = pltpu.make_async_remote_copy(pack, land.at[me], send.at[j], recv.at[j],
                                              device_id=peer, device_id_type=MESH)
            cp.start()
            cps.append(cp)
        land[me] = pack[...]
        for j in range(1, N_DEV):
            pltpu.make_async_remote_copy(pack, land.at[me ^ j], send.at[j], recv.at[j],
                                         device_id=(x, y, c), device_id_type=MESH).wait_recv()
        for cp in cps:
            cp.wait_send()
        tot = land[0]
        for d in range(1, N_DEV):
            tot = tot + land[d]

        loss_ref[...] = 0.5 * jnp.sum(tot[ROW_LOSS:ROW_LOSS + 1, :], axis=-1, keepdims=True) * (1.0 / D)

        def update(nm, g):
            g_ref, d_ref, nm_ref, nv_ref = out_refs[nm]
            g_ref[...] = g
            d_ref[...], nm_ref[...], nv_ref[...] = _adamw_math(w_refs[nm][...], g, m_refs[nm][...], v_refs[nm][...])

        for nm, r in GAIN_ROWS.items():
            update(nm, tot[r:r + 2, :])
        update("kv_g", tot[ROW_KV_G:ROW_KV_G + 1, :])
        k = 2 * x + y
        width = D // N_CHIPS
        g_scale = jnp.zeros((1, width), F32)
        for kk in range(N_CHIPS):
            g_scale = g_scale + jnp.where(k == kk, tot[ROW_POOL_SCALE:ROW_POOL_SCALE + 1, kk * width:(kk + 1) * width], 0.0)
        update("pool_scale", g_scale)
        update("sinks", tot[ROW_SINKS:ROW_SINKS + 1, 0:N_HEADS])

    ins = list(rows) + [small_w[nm] for nm in names] + [small_m[nm] for nm in names] + [small_v[nm] for nm in names]
    out_shape = [_sds((1, 1))]
    for nm in names:
        out_shape += [_sds(small_w[nm].shape)] * 4
    outs = pl.pallas_call(
        body, name="small_sync",
        in_specs=[VSPEC] * len(ins), out_specs=[VSPEC] * len(out_shape), out_shape=out_shape,
        scratch_shapes=[pltpu.VMEM((PACK_ROWS, D), F32), pltpu.VMEM((N_DEV, PACK_ROWS, D), F32),
                        pltpu.SemaphoreType.DMA((N_DEV,)), pltpu.SemaphoreType.DMA((N_DEV,))],
        compiler_params=_params(),
    )(*ins)
    loss = outs[0]
    per = {nm: outs[1 + 4 * k: 5 + 4 * k] for k, nm in enumerate(names)}
    return loss, per


def _local_step(x, p, target, gains, sinks, kv_g, w):
    y0, x1 = _mixa_fwd(x, gains["pre_mix_g"], w["pool_w"], w["pool_scale"], gains["post_mix_g"])
    f0, x2 = _ffn_fwd(0, x1, gains["pre_ffn_g"], w["w_gu"], w["w_down"], gains["post_ffn_g"])
    z0, pe0, x3 = _ple_fwd(0, x2, p[0], gains["ple_g"], w["w_ple_gate"], w["w_ple_proj"], gains["ple_post_g"])
    q, kv = _qkv_fwd(x3, gains["pre_mix_g"], kv_g, w["w_q"], w["w_kv"])
    attn, y1, x4 = _attn_fwd(q, kv, sinks, x3, w["w_o"], gains["post_mix_g"])
    f1, x5 = _ffn_fwd(1, x4, gains["pre_ffn_g"], w["w_gu"], w["w_down"], gains["post_ffn_g"])
    z1, pe1, dx6, loss_row = _ple_fwd(1, x5, p[1], gains["ple_g"], w["w_ple_gate"], w["w_ple_proj"],
                                      gains["ple_post_g"], target)

    dx5, dwg, dwp, d_ple1, d_plepost1 = _ple_bwd(1, dx6, x5, z1, pe1, p[1], gains["ple_g"], w["w_ple_gate"],
                                                 gains["ple_post_g"], None, None)
    dx4, dwgu, dwd, d_preffn1, d_postffn1 = _ffn_bwd(1, dx5, x4, f1, gains["pre_ffn_g"], w["w_gu"], w["w_down"],
                                                     gains["post_ffn_g"], None, None)
    dq, dkv, dwo, d_postmix1, d_sinks = _attn_bwd(dx4, y1, attn, q, kv, sinks, w["w_o"], gains["post_mix_g"])
    dx3, dwq, dwkv, d_premix1, d_kvg = _qkv_bwd(dq, dkv, x3, dx4, gains["pre_mix_g"], kv_g, w["w_q"], w["w_kv"])
    dx2, dwg, dwp, d_ple0, d_plepost0 = _ple_bwd(0, dx3, x2, z0, pe0, p[0], gains["ple_g"], w["w_ple_gate"],
                                                 gains["ple_post_g"], dwg, dwp)
    dx1, dwgu, dwd, d_preffn0, d_postffn0 = _ffn_bwd(0, dx2, x1, f0, gains["pre_ffn_g"], w["w_gu"], w["w_down"],
                                                     gains["post_ffn_g"], dwgu, dwd)
    dx0, dpool, d_scale, d_postmix0, d_premix0 = _mixa_bwd(dx1, x, y0, gains["pre_mix_g"], w["pool_w"],
                                                           w["pool_scale"], gains["post_mix_g"])
    big = {"w_gu": dwgu, "w_down": dwd, "w_ple_gate": dwg, "w_ple_proj": dwp, "w_q": dwq, "w_o": dwo,
           "w_kv": dwkv, "pool_w": dpool}
    rows = [d_premix0, d_premix1, d_postmix0, d_postmix1, d_preffn0, d_preffn1, d_postffn0, d_postffn1,
            d_ple0, d_ple1, d_plepost0, d_plepost1, d_kvg, d_scale, d_sinks, loss_row]
    return dx0, big, rows


def _compute_layout(t, full):
    if t.name == "w_gu":
        return full
    if t.name == "pool_w":
        return full.reshape(len(WINDOWS), POOL_G, POOL_G)
    if t.name == "pool_scale":
        return full.reshape(1, D)
    if t.name in ("w_q", "w_o", "w_kv"):
        return full.reshape(t.A * t.R, t.C)
    return full.reshape(t.L, t.A * t.R, _ncb(t) * t.C)


def kernel(x, p, pre_mix_g, post_mix_g, pre_ffn_g, post_ffn_g, pool_w, pool_scale, kv_g, w_kv, w_q, sinks, w_o, w_gu, w_down, ple_g, w_ple_gate, w_ple_proj, ple_post_g, loss_target, m_pre_mix_g, m_post_mix_g, m_pre_ffn_g, m_post_ffn_g, m_pool_w, m_pool_scale, m_kv_g, m_w_kv, m_w_q, m_sinks, m_w_o, m_w_gu, m_w_down, m_ple_g, m_w_ple_gate, m_w_ple_proj, m_ple_post_g, v_pre_mix_g, v_post_mix_g, v_pre_ffn_g, v_post_ffn_g, v_pool_w, v_pool_scale, v_kv_g, v_w_kv, v_w_q, v_sinks, v_w_o, v_w_gu, v_w_down, v_ple_g, v_w_ple_gate, v_w_ple_proj, v_ple_post_g):
    weights = dict(pre_mix_g=pre_mix_g, post_mix_g=post_mix_g, pre_ffn_g=pre_ffn_g, post_ffn_g=post_ffn_g,
                   pool_w=pool_w, pool_scale=pool_scale, kv_g=kv_g, w_kv=w_kv, w_q=w_q, sinks=sinks, w_o=w_o,
                   w_gu=w_gu, w_down=w_down, ple_g=ple_g, w_ple_gate=w_ple_gate, w_ple_proj=w_ple_proj,
                   ple_post_g=ple_post_g)
    m_in = dict(pre_mix_g=m_pre_mix_g, post_mix_g=m_post_mix_g, pre_ffn_g=m_pre_ffn_g, post_ffn_g=m_post_ffn_g,
                pool_w=m_pool_w, pool_scale=m_pool_scale, kv_g=m_kv_g, w_kv=m_w_kv, w_q=m_w_q, sinks=m_sinks,
                w_o=m_w_o, w_gu=m_w_gu, w_down=m_w_down, ple_g=m_ple_g, w_ple_gate=m_w_ple_gate,
                w_ple_proj=m_w_ple_proj, ple_post_g=m_ple_post_g)
    v_in = dict(pre_mix_g=v_pre_mix_g, post_mix_g=v_post_mix_g, pre_ffn_g=v_pre_ffn_g, post_ffn_g=v_post_ffn_g,
                pool_w=v_pool_w, pool_scale=v_pool_scale, kv_g=v_kv_g, w_kv=v_w_kv, w_q=v_w_q, sinks=v_sinks,
                w_o=v_w_o, w_gu=v_w_gu, w_down=v_w_down, ple_g=v_ple_g, w_ple_gate=v_w_ple_gate,
                w_ple_proj=v_w_ple_proj, ple_post_g=v_ple_post_g)
    order = ["pre_mix_g", "post_mix_g", "pre_ffn_g", "post_ffn_g", "pool_w", "pool_scale", "kv_g", "w_kv", "w_q",
             "sinks", "w_o", "w_gu", "w_down", "ple_g", "w_ple_gate", "w_ple_proj", "ple_post_g"]

    kc = jnp.stack([2 * lax.axis_index("x") + lax.axis_index("y"), lax.axis_index("c")]).astype(jnp.int32)
    s_len = x.shape[1]

    def shard_view(t, a):
        return a.reshape(t.L, t.R, t.C)

    placed = [_place(t, shard_view(t, weights[t.name]), kc, BF) for t in BIGS]
    placed.append(_place(POOL_SCALE, shard_view(POOL_SCALE, pool_scale), kc, F32))
    gathered = _all_gather(placed)
    w = {t.name: _compute_layout(t, a) for t, a in zip(BIGS + (POOL_SCALE,), gathered)}

    gains = {nm: weights[nm] for nm in GAIN_ROWS}
    dx, big, rows = _local_step(x.reshape(s_len, D), p.reshape(2, s_len, PLE), loss_target.reshape(s_len, D),
                                gains, sinks, kv_g.reshape(1, D), w)

    small_names = list(GAIN_ROWS) + ["kv_g", "pool_scale", "sinks"]
    as2d = lambda a: a.reshape(1, D) if a.ndim == 1 else a
    loss, small = _small_sync(rows, {nm: as2d(weights[nm]) for nm in small_names},
                              {nm: as2d(m_in[nm]) for nm in small_names},
                              {nm: as2d(v_in[nm]) for nm in small_names})

    grads = [big[t.name].reshape(_full_shape(t)) for t in BIGS]
    lands = _pair_exchange(grads)
    sums = [_pair_sum(t, g, l, kc) for t, g, l in zip(BIGS, grads, lands)]
    lands2 = _chip_exchange(sums)
    halves = [_chip_sum(t, s, l, kc) for t, s, l in zip(BIGS, sums, lands2)]
    full_grads = _pair_share(halves)

    out = {"grad": {}, "delta": {}, "new_m": {}, "new_v": {}}
    for t, g in zip(BIGS, full_grads):
        nm = t.name
        d, nm_, nv_ = _adamw(t, shard_view(t, weights[nm]), g, shard_view(t, m_in[nm]), shard_view(t, v_in[nm]))
        shape = weights[nm].shape
        out["grad"][nm], out["delta"][nm] = g.reshape(shape), d.reshape(shape)
        out["new_m"][nm], out["new_v"][nm] = nm_.reshape(shape), nv_.reshape(shape)
    for nm in small_names:
        shape = weights[nm].shape
        for kind, a in zip(("grad", "delta", "new_m", "new_v"), small[nm]):
            out[kind][nm] = a.reshape(shape)

    return (loss.reshape(()), dx.reshape(x.shape),
            *[out["grad"][nm] for nm in order], *[out["delta"][nm] for nm in order],
            *[out["new_m"][nm] for nm in order], *[out["new_v"][nm] for nm in order])
```

```python
import collections

import jax
import jax.numpy as jnp
from jax import lax
from jax.experimental import pallas as pl
from jax.experimental.pallas import tpu as pltpu

D = 1024
FF = 2816
N_HEADS = 16
HEAD_DIM = 64
N_KV_HEADS = 4
GQA = N_HEADS // N_KV_HEADS
KVD = N_KV_HEADS * HEAD_DIM
PLE = 256
BLK = 128
WINDOWS = (2, 4, 8, 16)
POOL_G = 256
HALO = 16
EPS = 1e-6
NEG_INF = -1e30
ATT_SCALE = HEAD_DIM ** -0.5
SLOPES = tuple(2.0 ** (-8.0 * (h + 1) / N_HEADS) for h in range(N_HEADS))
N_CHIPS = 4
N_DEV = 8

LR, B1, B2, AEPS, WD, STEP = 0.001, 0.9, 0.999, 1e-08, 0.01, 10
BC1 = 1.0 - B1 ** STEP
BC2 = 1.0 - B2 ** STEP

BF = jnp.bfloat16
F32 = jnp.float32
MESH = pl.DeviceIdType.MESH
VMEM_LIMIT_V7X = 56 * 1024 * 1024
TM = 256
TM_FFN_BWD = 512
FF_CHUNK = 256
FF_HALF = FF // 2

VSPEC = pl.BlockSpec(memory_space=pltpu.VMEM)
SSPEC = pl.BlockSpec(memory_space=pltpu.SMEM)
ANYSPEC = pl.BlockSpec(memory_space=pl.ANY)


def _params(n_grid=0):
    sem = ("arbitrary",) * n_grid if n_grid else None
    return pltpu.CompilerParams(dimension_semantics=sem, vmem_limit_bytes=VMEM_LIMIT_V7X)


def _sds(shape, dtype=F32):
    return jax.ShapeDtypeStruct(tuple(shape), dtype)


Rider = collections.namedtuple("Rider", "arrays out_shapes aliases scratch start finish")


def _call(body, *, name, grid, in_specs, out_specs, out_shape, args, scratch_shapes=(), rider=None):
    ni, no, ns = len(in_specs), len(out_specs), len(scratch_shapes)
    if rider is None:
        outs = pl.pallas_call(body, name=name, grid=grid, in_specs=in_specs, out_specs=out_specs,
                              out_shape=out_shape, scratch_shapes=list(scratch_shapes),
                              compiler_params=_params(len(grid)))(*args)
        return list(outs), []
    ri, ro = len(rider.arrays), len(rider.out_shapes)

    def full(*refs):
        ins, refs = refs[:ni], refs[ni:]
        rins, refs = refs[:ri], refs[ri:]
        outs, refs = refs[:no], refs[no:]
        routs, refs = refs[:ro], refs[ro:]
        scr, rscr = refs[:ns], refs[ns:]
        ids = [pl.program_id(a) for a in range(len(grid))]
        first = ids[0] == 0
        last = ids[0] == grid[0] - 1
        for a in range(1, len(grid)):
            first = first & (ids[a] == 0)
            last = last & (ids[a] == grid[a] - 1)

        @pl.when(first)
        def _():
            rider.start(rins, routs, rscr)

        body(*ins, *outs, *scr)

        @pl.when(last)
        def _():
            rider.finish(rins, routs, rscr)

    outs = pl.pallas_call(
        full, name=name, grid=grid,
        in_specs=list(in_specs) + [ANYSPEC] * ri, out_specs=list(out_specs) + [ANYSPEC] * ro,
        out_shape=list(out_shape) + list(rider.out_shapes),
        scratch_shapes=list(scratch_shapes) + list(rider.scratch),
        input_output_aliases={ni + a: no + b for a, b in rider.aliases.items()},
        compiler_params=_params(len(grid)))(*args, *rider.arrays)
    return list(outs[:no]), list(outs[no:])


def _run(name, rider):
    ri = len(rider.arrays)

    def body(*refs):
        rins, routs, rscr = refs[:ri], refs[ri:ri + len(rider.out_shapes)], refs[ri + len(rider.out_shapes):]
        rider.start(rins, routs, rscr)
        rider.finish(rins, routs, rscr)

    return pl.pallas_call(
        body, name=name, in_specs=[ANYSPEC] * ri, out_specs=[ANYSPEC] * len(rider.out_shapes),
        out_shape=list(rider.out_shapes), scratch_shapes=list(rider.scratch),
        input_output_aliases=dict(rider.aliases), compiler_params=_params())(*rider.arrays)


def _rms_fwd(x, g):
    r = lax.rsqrt(jnp.mean(x * x, axis=-1, keepdims=True) + EPS)
    return x * r * g


def _rms_bwd(x, g, dy):
    r = lax.rsqrt(jnp.mean(x * x, axis=-1, keepdims=True) + EPS)
    xn = x * r
    dxn = dy * g
    dx = r * (dxn - xn * jnp.mean(dxn * xn, axis=-1, keepdims=True))
    return dx, dy * xn


def _rowsum(a):
    return jnp.sum(a, axis=0, keepdims=True)


def _sigmoid(z):
    return 1.0 / (1.0 + jnp.exp(-z))


def _dot(a, b):
    return jnp.dot(a, b, preferred_element_type=F32)


def _dot_nt(a, b):
    return lax.dot_general(a, b, (((1,), (1,)), ((), ())), preferred_element_type=F32)


def _dot_tn(a, b):
    return lax.dot_general(a, b, (((0,), (0,)), ((), ())), preferred_element_type=F32)


def _row_spec(tm, width=D):
    return pl.BlockSpec((tm, width), lambda i: (i, 0))


def _const_spec(shape):
    zeros = (0,) * len(shape)
    return pl.BlockSpec(tuple(shape), lambda *_: zeros)


def _pool_delta(he, pos):
    out = []
    for gi, w in enumerate(WINDOWS):
        hg = he[:, gi * POOL_G:(gi + 1) * POOL_G]
        s = hg
        k = 1
        while k < w:
            s = s + pltpu.roll(s, k, 0)
            k *= 2
        cnt = jnp.maximum(jnp.minimum(pos + 1, w), 1).astype(F32)
        out.append(s / cnt - hg)
    return out


def _load_with_halo_before(x_ref, i, tm):
    r0 = pl.multiple_of(i * tm, tm)
    hs = pl.multiple_of(jnp.maximum(i * tm - HALO, 0), 8)
    xh = jnp.where(i > 0, x_ref[pl.ds(hs, HALO), :], 0.0)
    xt = x_ref[pl.ds(r0, tm), :]
    return xt, jnp.concatenate([xh, xt], axis=0)


def _mixa_fwd(x, pre_g, pool_w, pool_scale, post_g):
    s_len = x.shape[0]
    n = s_len // TM

    def body(x_ref, pg_ref, w_ref, sc_ref, qg_ref, y_ref, x1_ref):
        i = pl.program_id(0)
        xt, xe = _load_with_halo_before(x_ref, i, TM)
        he = _rms_fwd(xe, pg_ref[0:1, :])
        pos = i * TM - HALO + lax.broadcasted_iota(jnp.int32, (TM + HALO, 1), 0)
        ds = _pool_delta(he, pos)
        ys = [_dot(ds[gi][HALO:, :].astype(BF), w_ref[gi]) for gi in range(len(WINDOWS))]
        y = jnp.concatenate(ys, axis=1) * sc_ref[...]
        y_ref[...] = y
        x1_ref[...] = xt + _rms_fwd(y, qg_ref[0:1, :])

    outs, _ = _call(body, name="mixa_fwd", grid=(n,),
                    in_specs=[VSPEC] * 5, out_specs=[_row_spec(TM), _row_spec(TM)],
                    out_shape=[_sds((s_len, D)), _sds((s_len, D))],
                    args=[x, pre_g, pool_w, pool_scale, post_g])
    return outs


def _mixa_bwd(dx1, x, y, pre_g, pool_w, pool_scale, post_g):
    s_len = x.shape[0]
    n = s_len // TM
    ng = len(WINDOWS)

    def body(dx_ref, x_ref, y_ref, pg_ref, w_ref, sc_ref, qg_ref,
             dx0_ref, dw_ref, dsc_ref, dqg_ref, dpg_ref, wacc):
        i = pl.program_id(0)

        @pl.when(i == 0)
        def _():
            wacc[...] = jnp.zeros_like(wacc)
            dsc_ref[...] = jnp.zeros_like(dsc_ref)
            dqg_ref[...] = jnp.zeros_like(dqg_ref)
            dpg_ref[...] = jnp.zeros_like(dpg_ref)

        r0 = pl.multiple_of(i * TM, TM)
        xt, xe = _load_with_halo_before(x_ref, i, TM)
        he = _rms_fwd(xe, pg_ref[0:1, :])
        pos_b = i * TM - HALO + lax.broadcasted_iota(jnp.int32, (TM + HALO, 1), 0)
        ds = _pool_delta(he, pos_b)

        last = i == n - 1
        a0 = pl.multiple_of(jnp.minimum(i * TM + TM, s_len - HALO), 8)
        ye = jnp.concatenate([y_ref[pl.ds(r0, TM), :], y_ref[pl.ds(a0, HALO), :]], axis=0)
        dt = dx_ref[pl.ds(r0, TM), :]
        de = jnp.concatenate([dt, jnp.where(last, 0.0, dx_ref[pl.ds(a0, HALO), :])], axis=0)
        dye, prod = _rms_bwd(ye, qg_ref[0:1, :], de)
        dqg_ref[...] += _rowsum(prod[:TM, :])
        dys = dye * sc_ref[...]
        pos_a = i * TM + lax.broadcasted_iota(jnp.int32, (TM + HALO, 1), 0)

        dhs, dscs = [], []
        for gi, w in enumerate(WINDOWS):
            sl = slice(gi * POOL_G, (gi + 1) * POOL_G)
            wg = w_ref[gi]
            dys_g = dys[:, sl].astype(BF)
            d_g = ds[gi][HALO:, :].astype(BF)
            ypre = _dot(d_g, wg)
            dscs.append(_rowsum(dye[:TM, sl] * ypre))
            wacc[gi] += _dot_tn(d_g, dys_g[:TM, :])
            dd = _dot_nt(dys_g, wg)
            cnt = jnp.minimum(pos_a + 1, w).astype(F32)
            a = dd / cnt
            k = 1
            while k < w:
                a = a + pltpu.roll(a, TM + HALO - k, 0)
                k *= 2
            dhs.append(a[:TM, :] - dd[:TM, :])
        dsc_ref[...] += jnp.concatenate(dscs, axis=1)
        dh = jnp.concatenate(dhs, axis=1)
        dxp, prod2 = _rms_bwd(xt, pg_ref[0:1, :], dh)
        dpg_ref[...] += _rowsum(prod2)
        dx0_ref[...] = dt + dxp

        @pl.when(last)
        def _():
            dw_ref[...] = wacc[...].astype(BF)

    outs, _ = _call(
        body, name="mixa_bwd", grid=(n,), in_specs=[VSPEC] * 7,
        out_specs=[_row_spec(TM), _const_spec((ng, POOL_G, POOL_G)), _const_spec((1, D)),
                   _const_spec((1, D)), _const_spec((1, D))],
        out_shape=[_sds((s_len, D)), _sds((ng, POOL_G, POOL_G), BF), _sds((1, D)), _sds((1, D)), _sds((1, D))],
        scratch_shapes=[pltpu.VMEM((ng, POOL_G, POOL_G), F32)],
        args=[dx1, x, y, pre_g, pool_w, pool_scale, post_g])
    return outs


def _ffn_fwd(layer, x1, pre_g, wgu, wd, post_g, rider=None):
    s_len = x1.shape[0]

    def body(x_ref, pg_ref, wgu_ref, wd_ref, qg_ref, f_ref, x2_ref):
        x = x_ref[...]
        h = _rms_fwd(x, pg_ref[layer:layer + 1, :]).astype(BF)
        f = jnp.zeros((TM, D), F32)
        for c in range(FF // FF_HALF):
            cols = slice(c * FF_HALF, (c + 1) * FF_HALF)
            g = _dot(h, wgu_ref[0, :, cols])
            u = _dot(h, wgu_ref[1, :, cols])
            act = g * _sigmoid(g) * u
            f = f + _dot(act.astype(BF), wd_ref[cols, :])
        f_ref[...] = f
        x2_ref[...] = x + _rms_fwd(f, qg_ref[layer:layer + 1, :])

    return _call(body, name=f"ffn_fwd{layer}", grid=(s_len // TM,),
                 in_specs=[_row_spec(TM), VSPEC, VSPEC, VSPEC, VSPEC],
                 out_specs=[_row_spec(TM), _row_spec(TM)],
                 out_shape=[_sds((s_len, D)), _sds((s_len, D))],
                 args=[x1, pre_g, wgu, wd, post_g], rider=rider)


def _ffn_bwd(layer, dx2, x1, f, pre_g, wgu, wd, post_g, rider=None):
    s_len = x1.shape[0]
    tm = TM_FFN_BWD
    n = s_len // tm
    nc = FF // FF_CHUNK

    def edge_rows(c, i):
        return (jnp.where((c == 0) | (c == nc - 1), i, n - 1), 0)

    def body(dx_ref, x_ref, f_ref, pg_ref, wgu_ref, wd_ref, qg_ref,
             dx1_ref, dwgu_ref, dwd_ref, dpg_ref, dqg_ref, h_s, df_s, dh_s, accg, accu, accd):
        c = pl.program_id(0)
        i = pl.program_id(1)
        rows = pl.ds(pl.multiple_of(i * tm, tm), tm)
        pg = pg_ref[layer:layer + 1, :]

        @pl.when((c == 0) & (i == 0))
        def _():
            dpg_ref[...] = jnp.zeros_like(dpg_ref)
            dqg_ref[...] = jnp.zeros_like(dqg_ref)

        @pl.when(c == 0)
        def _():
            h_s[rows, :] = _rms_fwd(x_ref[...], pg).astype(BF)
            df, prod = _rms_bwd(f_ref[...], qg_ref[layer:layer + 1, :], dx_ref[...])
            df_s[rows, :] = df.astype(BF)
            dqg_ref[...] += _rowsum(prod)

        @pl.when(i == 0)
        def _():
            accg[...] = jnp.zeros_like(accg)
            accu[...] = jnp.zeros_like(accu)
            accd[...] = jnp.zeros_like(accd)

        h = h_s[rows, :]
        df = df_s[rows, :]
        wg = wgu_ref[0]
        wu = wgu_ref[1]
        g = _dot(h, wg)
        u = _dot(h, wu)
        sg = _sigmoid(g)
        a = g * sg
        dact = _dot_nt(df, wd_ref[...])
        accd[...] += _dot_tn((a * u).astype(BF), df)
        du = (dact * a).astype(BF)
        dg = (dact * u * (sg * (1.0 + g * (1.0 - sg)))).astype(BF)
        accg[...] += _dot_tn(h, dg)
        accu[...] += _dot_tn(h, du)
        dh = _dot_nt(dg, wg) + _dot_nt(du, wu)

        @pl.when(c == 0)
        def _():
            dh_s[rows, :] = dh

        @pl.when((c > 0) & (c < nc - 1))
        def _():
            dh_s[rows, :] += dh

        @pl.when(c == nc - 1)
        def _():
            dxp, prod = _rms_bwd(x_ref[...], pg, dh_s[rows, :] + dh)
            dpg_ref[...] += _rowsum(prod)
            dx1_ref[...] = dx_ref[...] + dxp

        @pl.when(i == n - 1)
        def _():
            dwgu_ref[0] = accg[...].astype(BF)
            dwgu_ref[1] = accu[...].astype(BF)
            dwd_ref[...] = accd[...].astype(BF)

    return _call(
        body, name=f"ffn_bwd{layer}", grid=(nc, n),
        in_specs=[pl.BlockSpec((tm, D), edge_rows), pl.BlockSpec((tm, D), edge_rows),
                  pl.BlockSpec((tm, D), lambda c, i: (jnp.where(c == 0, i, n - 1), 0)),
                  VSPEC,
                  pl.BlockSpec((2, D, FF_CHUNK), lambda c, i: (0, 0, c)),
                  pl.BlockSpec((FF_CHUNK, D), lambda c, i: (c, 0)),
                  VSPEC],
        out_specs=[pl.BlockSpec((tm, D), lambda c, i: (jnp.where(c == nc - 1, i, 0), 0)),
                   pl.BlockSpec((2, D, FF_CHUNK), lambda c, i: (0, 0, c)),
                   pl.BlockSpec((FF_CHUNK, D), lambda c, i: (c, 0)),
                   _const_spec((1, D)), _const_spec((1, D))],
        out_shape=[_sds((s_len, D)), _sds((2, D, FF), BF), _sds((FF, D), BF), _sds((1, D)), _sds((1, D))],
        scratch_shapes=[pltpu.VMEM((s_len, D), BF), pltpu.VMEM((s_len, D), BF), pltpu.VMEM((s_len, D), F32),
                        pltpu.VMEM((D, FF_CHUNK), F32), pltpu.VMEM((D, FF_CHUNK), F32),
                        pltpu.VMEM((FF_CHUNK, D), F32)],
        args=[dx2, x1, f, pre_g, wgu, wd, post_g], rider=rider)


def _ple_fwd(layer, x2, p, ple_g, w_gate, w_proj, post_g, target=None, rider=None):
    s_len = x2.shape[0]
    final = target is not None

    def body(*refs):
        if final:
            x_ref, p_ref, g_ref, wg_ref, wp_ref, qg_ref, t_ref, z_ref, pe_ref, dx_ref, lv_ref = refs
        else:
            x_ref, p_ref, g_ref, wg_ref, wp_ref, qg_ref, z_ref, pe_ref, x3_ref = refs
        x = x_ref[...]
        r = _rms_fwd(x, g_ref[layer:layer + 1, :]).astype(BF)
        z = _dot(r, wg_ref[...])
        pe = _dot(p_ref[...].astype(BF), wp_ref[...])
        z_ref[...] = z
        pe_ref[...] = pe
        x3 = x + _rms_fwd(pe * _sigmoid(z), qg_ref[layer:layer + 1, :])
        if final:
            @pl.when(pl.program_id(0) == 0)
            def _():
                lv_ref[...] = jnp.zeros_like(lv_ref)
            err = x3 - t_ref[...]
            dx_ref[...] = err * (1.0 / D)
            lv_ref[...] += _rowsum(err * err)
        else:
            x3_ref[...] = x3

    p_spec = pl.BlockSpec((None, TM, PLE), lambda i: (layer, i, 0))
    in_specs = [_row_spec(TM), p_spec, VSPEC, VSPEC, VSPEC, VSPEC]
    args = [x2, p, ple_g, w_gate, w_proj, post_g]
    out_specs = [_row_spec(TM), _row_spec(TM), _row_spec(TM)]
    out_shape = [_sds((s_len, D))] * 3
    if final:
        in_specs.append(_row_spec(TM))
        args.append(target)
        out_specs.append(_const_spec((1, D)))
        out_shape.append(_sds((1, D)))
    return _call(body, name=f"ple_fwd{layer}", grid=(s_len // TM,), in_specs=in_specs, out_specs=out_specs,
                 out_shape=out_shape, args=args, rider=rider)


def _ple_bwd(layer, dx3, x2, z, pe, p, ple_g, w_gate, post_g):
    s_len = x2.shape[0]
    n = s_len // TM

    def body(dx_ref, x_ref, z_ref, pe_ref, p_ref, g_ref, wg_ref, qg_ref,
             dx2_ref, dwg_ref, dwp_ref, dg_ref, dqg_ref, gacc, pacc):
        i = pl.program_id(0)

        @pl.when(i == 0)
        def _():
            gacc[...] = jnp.zeros_like(gacc)
            pacc[...] = jnp.zeros_like(pacc)
            dg_ref[...] = jnp.zeros_like(dg_ref)
            dqg_ref[...] = jnp.zeros_like(dqg_ref)

        dx = dx_ref[...]
        x = x_ref[...]
        pe_v = pe_ref[...]
        gate = _sigmoid(z_ref[...])
        de, prod = _rms_bwd(pe_v * gate, qg_ref[layer:layer + 1, :], dx)
        dqg_ref[...] += _rowsum(prod)
        dpe = (de * gate).astype(BF)
        dz = (de * pe_v * gate * (1.0 - gate)).astype(BF)
        pacc[...] += _dot_tn(p_ref[...].astype(BF), dpe)
        g = g_ref[layer:layer + 1, :]
        r = _rms_fwd(x, g).astype(BF)
        gacc[...] += _dot_tn(r, dz)
        dr = _dot_nt(dz, wg_ref[...])
        dxp, prod2 = _rms_bwd(x, g, dr)
        dg_ref[...] += _rowsum(prod2)
        dx2_ref[...] = dx + dxp

        @pl.when(i == n - 1)
        def _():
            dwg_ref[...] = gacc[...].astype(BF)
            dwp_ref[...] = pacc[...].astype(BF)

    p_spec = pl.BlockSpec((None, TM, PLE), lambda i: (layer, i, 0))
    outs, _ = _call(
        body, name=f"ple_bwd{layer}", grid=(n,),
        in_specs=[_row_spec(TM), _row_spec(TM), _row_spec(TM), _row_spec(TM), p_spec, VSPEC, VSPEC, VSPEC],
        out_specs=[_row_spec(TM), _const_spec((D, D)), _const_spec((PLE, D)), _const_spec((1, D)), _const_spec((1, D))],
        out_shape=[_sds((s_len, D)), _sds((D, D), BF), _sds((PLE, D), BF), _sds((1, D)), _sds((1, D))],
        scratch_shapes=[pltpu.VMEM((D, D), F32), pltpu.VMEM((PLE, D), F32)],
        args=[dx3, x2, z, pe, p, ple_g, w_gate, post_g])
    return outs


def _qkv_fwd(x3, q_g, kv_g, w_q, w_kv, rider=None):
    s_len = x3.shape[0]

    def body(x_ref, qg_ref, kg_ref, wq_ref, wkv_ref, q_ref, kv_ref):
        x = x_ref[...]
        q_ref[...] = _dot(_rms_fwd(x, qg_ref[1:2, :]).astype(BF), wq_ref[...]).astype(BF)
        kv_ref[...] = _dot(_rms_fwd(x, kg_ref[...]).astype(BF), wkv_ref[...]).astype(BF)

    return _call(body, name="qkv_fwd", grid=(s_len // TM,),
                 in_specs=[_row_spec(TM), VSPEC, VSPEC, VSPEC, VSPEC],
                 out_specs=[_row_spec(TM), _row_spec(TM, 2 * KVD)],
                 out_shape=[_sds((s_len, D), BF), _sds((s_len, 2 * KVD), BF)],
                 args=[x3, q_g, kv_g, w_q, w_kv], rider=rider)


def _qkv_bwd(dq, dkv, x3, dx4, q_g, kv_g, w_q, w_kv):
    s_len = x3.shape[0]
    n = s_len // TM

    def body(dq_ref, dkv_ref, x_ref, dx_ref, qg_ref, kg_ref, wq_ref, wkv_ref,
             dx3_ref, dwq_ref, dwkv_ref, dqg_ref, dkg_ref, qacc, kacc):
        i = pl.program_id(0)

        @pl.when(i == 0)
        def _():
            qacc[...] = jnp.zeros_like(qacc)
            kacc[...] = jnp.zeros_like(kacc)
            dqg_ref[...] = jnp.zeros_like(dqg_ref)
            dkg_ref[...] = jnp.zeros_like(dkg_ref)

        x = x_ref[...]
        qg = qg_ref[1:2, :]
        kg = kg_ref[...]
        dq_v = dq_ref[...]
        dkv_v = dkv_ref[...].astype(BF)
        qacc[...] += _dot_tn(_rms_fwd(x, qg).astype(BF), dq_v)
        kacc[...] += _dot_tn(_rms_fwd(x, kg).astype(BF), dkv_v)
        dxq, prod_q = _rms_bwd(x, qg, _dot_nt(dq_v, wq_ref[...]))
        dxk, prod_k = _rms_bwd(x, kg, _dot_nt(dkv_v, wkv_ref[...]))
        dqg_ref[...] += _rowsum(prod_q)
        dkg_ref[...] += _rowsum(prod_k)
        dx3_ref[...] = dx_ref[...] + dxq + dxk

        @pl.when(i == n - 1)
        def _():
            dwq_ref[...] = qacc[...].astype(BF)
            dwkv_ref[...] = kacc[...].astype(BF)

    outs, _ = _call(
        body, name="qkv_bwd", grid=(n,),
        in_specs=[_row_spec(TM), _row_spec(TM, 2 * KVD), _row_spec(TM), _row_spec(TM), VSPEC, VSPEC, VSPEC, VSPEC],
        out_specs=[_row_spec(TM), _const_spec((D, D)), _const_spec((D, 2 * KVD)),
                   _const_spec((1, D)), _const_spec((1, D))],
        out_shape=[_sds((s_len, D)), _sds((D, D), BF), _sds((D, 2 * KVD), BF), _sds((1, D)), _sds((1, D))],
        scratch_shapes=[pltpu.VMEM((D, D), F32), pltpu.VMEM((D, 2 * KVD), F32)],
        args=[dq, dkv, x3, dx4, q_g, kv_g, w_q, w_kv])
    return outs


def _attn_block(i, q, kvw, sink_ref):
    off = jnp.where(i > 0, BLK, 0)
    rel = (lax.broadcasted_iota(jnp.int32, (BLK, 2 * BLK), 0)
           - lax.broadcasted_iota(jnp.int32, (BLK, 2 * BLK), 1) + off)
    valid = (rel >= 0) & (rel < BLK)
    relf = rel.astype(F32)
    out = []
    for h in range(N_HEADS):
        kh = h // GQA
        qh = q[:, h * HEAD_DIM:(h + 1) * HEAD_DIM]
        k = kvw[:, kh * HEAD_DIM:(kh + 1) * HEAD_DIM]
        v = kvw[:, KVD + kh * HEAD_DIM:KVD + (kh + 1) * HEAD_DIM]
        s = _dot_nt(qh, k) * ATT_SCALE - SLOPES[h] * relf
        s = jnp.where(valid, s, NEG_INF)
        sink = sink_ref[0, h]
        m = jnp.maximum(jnp.max(s, axis=-1, keepdims=True), sink)
        e = jnp.exp(s - m)
        es = jnp.exp(sink - m)
        inv = 1.0 / (jnp.sum(e, axis=-1, keepdims=True) + es)
        out.append((e * inv, es * inv, qh, k, v))
    return out


def _kv_window(kv_ref, i):
    ks = pl.multiple_of(jnp.maximum(i * BLK - BLK, 0), BLK)
    return ks, kv_ref[pl.ds(ks, 2 * BLK), :]


def _attn_fwd(q, kv, sinks, x3, w_o, post_g, rider=None):
    s_len = q.shape[0]

    def body(q_ref, kv_ref, sk_ref, x_ref, wo_ref, g_ref, a_ref, y_ref, x4_ref):
        i = pl.program_id(0)
        _, kvw = _kv_window(kv_ref, i)
        heads = _attn_block(i, q_ref[...], kvw, sk_ref)
        attn = jnp.concatenate([_dot(p.astype(BF), v) for p, _, _, _, v in heads], axis=1)
        a_ref[...] = attn
        y = _dot(attn.astype(BF), wo_ref[...])
        y_ref[...] = y
        x4_ref[...] = x_ref[...] + _rms_fwd(y, g_ref[1:2, :])

    return _call(body, name="attn_fwd", grid=(s_len // BLK,),
                 in_specs=[_row_spec(BLK), VSPEC, SSPEC, _row_spec(BLK), VSPEC, VSPEC],
                 out_specs=[_row_spec(BLK)] * 3, out_shape=[_sds((s_len, D))] * 3,
                 args=[q, kv, sinks, x3, w_o, post_g], rider=rider)


def _attn_bwd(dx4, y, attn, q, kv, sinks, w_o, post_g, rider=None):
    s_len = q.shape[0]
    n = s_len // BLK

    def body(dx_ref, y_ref, a_ref, q_ref, kv_ref, sk_ref, wo_ref, g_ref,
             dq_ref, dkv_ref, dwo_ref, dg_ref, dsk_ref, wacc):
        i = pl.program_id(0)

        @pl.when(i == 0)
        def _():
            dkv_ref[...] = jnp.zeros_like(dkv_ref)
            wacc[...] = jnp.zeros_like(wacc)
            dg_ref[...] = jnp.zeros_like(dg_ref)
            dsk_ref[...] = jnp.zeros_like(dsk_ref)

        dy, prod = _rms_bwd(y_ref[...], g_ref[1:2, :], dx_ref[...])
        dg_ref[...] += _rowsum(prod)
        dyb = dy.astype(BF)
        attn = a_ref[...]
        wacc[...] += _dot_tn(attn.astype(BF), dyb)
        d_o = _dot_nt(dyb, wo_ref[...])
        dod = d_o * attn
        ks, kvw = _kv_window(kv_ref, i)
        heads = _attn_block(i, q_ref[...], kvw, sk_ref)
        lane = lax.broadcasted_iota(jnp.int32, (1, D), 1)
        dqs = []
        dks = [None] * N_KV_HEADS
        dvs = [None] * N_KV_HEADS
        dsk = jnp.zeros((1, D), F32)
        for h, (p, ps, qh, k, v) in enumerate(heads):
            kh = h // GQA
            hs = slice(h * HEAD_DIM, (h + 1) * HEAD_DIM)
            do_h = d_o[:, hs].astype(BF)
            dsum = jnp.sum(dod[:, hs], axis=-1, keepdims=True)
            dp = _dot_nt(do_h, v)
            dsb = (p * (dp - dsum) * ATT_SCALE).astype(BF)
            dsk = dsk + jnp.where(lane == h, -_rowsum(ps * dsum), 0.0)
            dqs.append(_dot(dsb, k))
            dk = _dot_tn(dsb, qh)
            dv = _dot_tn(p.astype(BF), do_h)
            dks[kh] = dk if dks[kh] is None else dks[kh] + dk
            dvs[kh] = dv if dvs[kh] is None else dvs[kh] + dv
        dsk_ref[...] += dsk
        dq_ref[...] = jnp.concatenate(dqs, axis=1).astype(BF)
        dkv_ref[pl.ds(ks, 2 * BLK), :] += jnp.concatenate(dks + dvs, axis=1)

        @pl.when(i == n - 1)
        def _():
            dwo_ref[...] = wacc[...].astype(BF)

    return _call(
        body, name="attn_bwd", grid=(n,),
        in_specs=[_row_spec(BLK), _row_spec(BLK), _row_spec(BLK), _row_spec(BLK), VSPEC, SSPEC, VSPEC, VSPEC],
        out_specs=[_row_spec(BLK), _const_spec((s_len, 2 * KVD)), _const_spec((D, D)),
                   _const_spec((1, D)), _const_spec((1, D))],
        out_shape=[_sds((s_len, D), BF), _sds((s_len, 2 * KVD)), _sds((D, D), BF), _sds((1, D)), _sds((1, D))],
        scratch_shapes=[pltpu.VMEM((D, D), F32)],
        args=[dx4, y, attn, q, kv, sinks, w_o, post_g], rider=rider)


Big = collections.namedtuple("Big", "name src layer L A R C rb")


def _bigs():
    out = {"pool_w": Big("pool_w", "pool_w", None, 4, 4, POOL_G // N_CHIPS, POOL_G, 32)}
    for l in range(2):
        out[f"w_gu{l}"] = Big(f"w_gu{l}", "w_gu", l, 1, 2, D, FF_HALF, 256)
        out[f"w_down{l}"] = Big(f"w_down{l}", "w_down", l, 1, 4, FF // N_CHIPS, D, 352)
        out[f"w_ple_gate{l}"] = Big(f"w_ple_gate{l}", "w_ple_gate", l, 1, 4, D // N_CHIPS, D, 128)
        out[f"w_ple_proj{l}"] = Big(f"w_ple_proj{l}", "w_ple_proj", l, 1, 1, PLE, D // N_CHIPS, 128)
    out["w_q"] = Big("w_q", "w_q", None, 1, 4, D // N_CHIPS, D, 128)
    out["w_o"] = Big("w_o", "w_o", None, 1, 4, D // N_CHIPS, D, 128)
    out["w_kv"] = Big("w_kv", "w_kv", None, 1, 4, D // N_CHIPS, 2 * KVD, 128)
    return out


BIGS = _bigs()
POOL_SCALE = Big("pool_scale", "pool_scale", None, 1, 1, 1, D // N_CHIPS, 1)
BIG_SOURCES = ("w_gu", "w_down", "w_ple_gate", "w_ple_proj", "w_q", "w_o", "w_kv", "pool_w")


def _ncb(t):
    return N_CHIPS // t.A


def _full_shape(t, rows=None):
    return (t.L, t.A, t.R if rows is None else rows, _ncb(t) * t.C)


def _slot_index(t, k):
    return k // _ncb(t), k % _ncb(t)


def _slot(ref, t, k, row0, rows):
    a, cb = _slot_index(t, k)
    return ref.at[:, a, pl.ds(row0, rows), pl.ds(pl.multiple_of(cb * t.C, 128), t.C)]


def _place(t, w, kc, out_dtype):
    rb = min(t.R, 2 * t.rb)

    def body(kc_ref, w_ref, o_ref):
        del kc_ref
        o_ref[...] = w_ref[...].astype(out_dtype)

    def in_map(l, j, kc_ref):
        return (l if t.layer is None else t.layer, j, 0)

    def out_map(l, j, kc_ref):
        a, cb = _slot_index(t, kc_ref[0])
        return (l, a, j, cb)

    return pl.pallas_call(
        body, name=f"place_{t.name}",
        grid_spec=pltpu.PrefetchScalarGridSpec(
            num_scalar_prefetch=1, grid=(t.L, t.R // rb),
            in_specs=[pl.BlockSpec((None, rb, t.C), in_map)],
            out_specs=pl.BlockSpec((None, None, rb, t.C), out_map)),
        out_shape=_sds(_full_shape(t), out_dtype),
        compiler_params=_params(2),
    )(kc, w)


def _mesh_position():
    x, y, c = lax.axis_index("x"), lax.axis_index("y"), lax.axis_index("c")
    chips = [(1 - x, y), (x, 1 - y), (1 - x, 1 - y)]
    return x, y, c, chips


def _gather_rider(parts, fulls):
    nt = len(parts)

    def ici(outs, sems, ti, j, k_src, c, dev):
        t, r0, r1 = parts[ti]
        split = t.R > 1
        rows = (r1 - r0) // 2 if split else t.R
        region = _slot(outs[ti], t, k_src, r0 + c * rows if split else 0, rows)
        return pltpu.make_async_remote_copy(region, region, sems[0].at[ti, j], sems[1].at[ti, j],
                                            device_id=dev, device_id_type=MESH)

    def d2d(outs, sems, ti, j, k_src, core, sibling):
        t, r0, r1 = parts[ti]
        rows = (r1 - r0) // 2
        region = _slot(outs[ti], t, k_src, r0 + core * rows, rows)
        return pltpu.make_async_remote_copy(region, region, sems[2].at[ti, j], sems[3].at[ti, j],
                                            device_id=sibling, device_id_type=MESH)

    def start(ins, outs, sems):
        x, y, c, chips = _mesh_position()
        for j, (cx, cy) in enumerate(chips):
            for ti in range(nt):
                ici(outs, sems, ti, j, 2 * x + y, c, (cx, cy, c)).start()

    def finish(ins, outs, sems):
        x, y, c, chips = _mesh_position()
        sibling = (x, y, 1 - c)
        for j, (cx, cy) in enumerate(chips):
            for ti in range(nt):
                ici(outs, sems, ti, j, 2 * cx + cy, c, (cx, cy, c)).wait_recv()
                if parts[ti][0].R > 1:
                    d2d(outs, sems, ti, j, 2 * cx + cy, c, sibling).start()
        for j, (cx, cy) in enumerate(chips):
            for ti in range(nt):
                if parts[ti][0].R > 1:
                    d2d(outs, sems, ti, j, 2 * cx + cy, 1 - c, sibling).wait_recv()
        for j, (cx, cy) in enumerate(chips):
            for ti in range(nt):
                ici(outs, sems, ti, j, 2 * x + y, c, (cx, cy, c)).wait_send()
                if parts[ti][0].R > 1:
                    d2d(outs, sems, ti, j, 2 * cx + cy, c, sibling).wait_send()

    sems = pltpu.SemaphoreType.DMA((nt, N_CHIPS - 1))
    return Rider(list(fulls), [_sds(a.shape, a.dtype) for a in fulls], {i: i for i in range(nt)},
                 [sems, sems, sems, sems], start, finish)


def _pair_exchange(name, specs, grads):
    nt = len(specs)

    def body(*refs):
        gs = refs[:nt]
        lands = refs[nt:2 * nt]
        send, recv = refs[2 * nt:]
        x, y, c, _ = _mesh_position()
        cps = []
        for ti, t in enumerate(specs):
            half = t.R // 2
            cp = pltpu.make_async_remote_copy(gs[ti].at[:, :, pl.ds((1 - c) * half, half), :], lands[ti],
                                              send.at[ti], recv.at[ti],
                                              device_id=(x, y, 1 - c), device_id_type=MESH)
            cp.start()
            cps.append(cp)
        for cp in cps:
            cp.wait()

    return pl.pallas_call(
        body, name=name,
        in_specs=[ANYSPEC] * nt, out_specs=[ANYSPEC] * nt,
        out_shape=[_sds(_full_shape(t, t.R // 2), BF) for t in specs],
        scratch_shapes=[pltpu.SemaphoreType.DMA((nt,)), pltpu.SemaphoreType.DMA((nt,))],
        compiler_params=_params(),
    )(*grads)


def _pair_sum(t, g, land, kc):
    half = t.R // 2
    nj = half // t.rb
    w = _ncb(t) * t.C

    def body(kc_ref, g_ref, l_ref, o_ref):
        del kc_ref
        o_ref[...] = (g_ref[...].astype(F32) + l_ref[...].astype(F32)).astype(BF)

    return pl.pallas_call(
        body, name=f"pair_sum_{t.name}",
        grid_spec=pltpu.PrefetchScalarGridSpec(
            num_scalar_prefetch=1, grid=(t.L, nj),
            in_specs=[pl.BlockSpec((None, t.A, t.rb, w), lambda l, j, kc_ref: (l, 0, kc_ref[1] * nj + j, 0)),
                      pl.BlockSpec((None, t.A, t.rb, w), lambda l, j, kc_ref: (l, 0, j, 0))],
            out_specs=pl.BlockSpec((None, t.A, t.rb, w), lambda l, j, kc_ref: (l, 0, j, 0))),
        out_shape=_sds(_full_shape(t, half), BF),
        compiler_params=_params(2),
    )(kc, g, land)


def _scatter_rider(specs, sums):
    nt = len(specs)

    def copy(ins, outs, sems, ti, j, chip, c):
        t = specs[ti]
        cx, cy = chip
        return pltpu.make_async_remote_copy(_slot(ins[ti], t, 2 * cx + cy, 0, t.R // 2), outs[ti].at[j],
                                            sems[0].at[ti, j], sems[1].at[ti, j],
                                            device_id=(cx, cy, c), device_id_type=MESH)

    def start(ins, outs, sems):
        _, _, c, chips = _mesh_position()
        for j, chip in enumerate(chips):
            for ti in range(nt):
                copy(ins, outs, sems, ti, j, chip, c).start()

    def finish(ins, outs, sems):
        _, _, c, chips = _mesh_position()
        for j, chip in enumerate(chips):
            for ti in range(nt):
                copy(ins, outs, sems, ti, j, chip, c).wait()

    sems = pltpu.SemaphoreType.DMA((nt, N_CHIPS - 1))
    return Rider(list(sums), [_sds((N_CHIPS - 1, t.L, t.R // 2, t.C), BF) for t in specs], {}, [sems, sems],
                 start, finish)


def _chip_sum(t, s, land, kc, n_layers, prev):
    half = t.R // 2
    nj = half // t.rb

    def body(*refs):
        s_ref, l_ref, o_ref = refs[1], refs[2], refs[-1]
        acc = s_ref[...].astype(F32)
        for j in range(N_CHIPS - 1):
            acc = acc + l_ref[j].astype(F32)
        o_ref[...] = acc

    def own_map(l, j, kc_ref):
        a, cb = _slot_index(t, kc_ref[0])
        return (l, a, j, cb)

    def out_map(l, j, kc_ref):
        return (l if t.layer is None else t.layer, kc_ref[1] * nj + j, 0)

    in_specs = [pl.BlockSpec((None, None, t.rb, t.C), own_map),
                pl.BlockSpec((N_CHIPS - 1, None, t.rb, t.C), lambda l, j, kc_ref: (0, l, j, 0))]
    args = [kc, s, land]
    aliases = {}
    if prev is not None:
        in_specs.append(ANYSPEC)
        args.append(prev)
        aliases = {3: 0}
    return pl.pallas_call(
        body, name=f"chip_sum_{t.name}",
        grid_spec=pltpu.PrefetchScalarGridSpec(
            num_scalar_prefetch=1, grid=(t.L, nj), in_specs=in_specs,
            out_specs=pl.BlockSpec((None, t.rb, t.C), out_map)),
        out_shape=_sds((n_layers, t.R, t.C)),
        input_output_aliases=aliases,
        compiler_params=_params(2),
    )(*args)


def _pair_share(halves):
    nt = len(halves)

    def body(*refs):
        outs = refs[nt:2 * nt]
        send, recv = refs[2 * nt:]
        x, y, c, _ = _mesh_position()
        cps = []
        for ti in range(nt):
            half = halves[ti].shape[1] // 2
            mine = outs[ti].at[:, pl.ds(c * half, half), :]
            cp = pltpu.make_async_remote_copy(mine, mine, send.at[ti], recv.at[ti],
                                              device_id=(x, y, 1 - c), device_id_type=MESH)
            cp.start()
            cps.append(cp)
        for ti in range(nt):
            half = halves[ti].shape[1] // 2
            theirs = outs[ti].at[:, pl.ds((1 - c) * half, half), :]
            pltpu.make_async_remote_copy(theirs, theirs, send.at[ti], recv.at[ti],
                                         device_id=(x, y, 1 - c), device_id_type=MESH).wait_recv()
        for cp in cps:
            cp.wait_send()

    return pl.pallas_call(
        body, name="grads_pair_share",
        in_specs=[ANYSPEC] * nt, out_specs=[ANYSPEC] * nt,
        out_shape=[_sds(a.shape, a.dtype) for a in halves],
        scratch_shapes=[pltpu.SemaphoreType.DMA((nt,)), pltpu.SemaphoreType.DMA((nt,))],
        input_output_aliases={i: i for i in range(nt)},
        compiler_params=_params(),
    )(*halves)


def _adamw_math(w, g, m, v):
    m = B1 * m + (1.0 - B1) * g
    v = B2 * v + (1.0 - B2) * (g * g)
    delta = -LR * ((m / BC1) / (jnp.sqrt(v / BC2) + AEPS) + WD * w)
    return delta, m, v


def _adamw(name, rb, w, g, m, v):
    n_layers, r, c = w.shape

    def body(w_ref, g_ref, m_ref, v_ref, go_ref, d_ref, nm_ref, nv_ref):
        g_v = g_ref[...]
        go_ref[...] = g_v
        d_ref[...], nm_ref[...], nv_ref[...] = _adamw_math(w_ref[...], g_v, m_ref[...], v_ref[...])

    spec = pl.BlockSpec((None, rb, c), lambda l, j: (l, j, 0))
    return pl.pallas_call(
        body, name=f"adamw_{name}", grid=(n_layers, r // rb),
        in_specs=[spec] * 4, out_specs=[spec] * 4, out_shape=[_sds(w.shape)] * 4,
        compiler_params=_params(2),
    )(w, g, m, v)


GAIN_ROWS = {"pre_mix_g": 0, "post_mix_g": 2, "pre_ffn_g": 4, "post_ffn_g": 6, "ple_g": 8, "ple_post_g": 10}
ROW_KV_G, ROW_POOL_SCALE, ROW_SINKS, ROW_LOSS, PACK_ROWS = 12, 13, 14, 15, 16
SMALL_NAMES = tuple(GAIN_ROWS) + ("kv_g", "pool_scale", "sinks")


def _small_all_reduce(rows):
    def body(*refs):
        row_refs = refs[:PACK_ROWS]
        tot_ref, pack, land, send, recv = refs[PACK_ROWS:]
        x, y, c, _ = _mesh_position()
        me = 4 * x + 2 * y + c
        for r in range(PACK_ROWS):
            pack[r:r + 1, :] = row_refs[r][...]
        cps = []
        for j in range(1, N_DEV):
            peer = (x ^ (j >> 2), y ^ ((j >> 1) & 1), c ^ (j & 1))
            cp = pltpu.make_async_remote_copy(pack, land.at[me], send.at[j], recv.at[j],
                                              device_id=peer, device_id_type=MESH)
            cp.start()
            cps.append(cp)
        land[me] = pack[...]
        for j in range(1, N_DEV):
            pltpu.make_async_remote_copy(pack, land.at[me ^ j], send.at[j], recv.at[j],
                                         device_id=(x, y, c), device_id_type=MESH).wait_recv()
        for cp in cps:
            cp.wait_send()
        tot = land[0]
        for d in range(1, N_DEV):
            tot = tot + land[d]
        tot_ref[...] = tot

    return pl.pallas_call(
        body, name="small_all_reduce",
        in_specs=[VSPEC] * PACK_ROWS, out_specs=VSPEC, out_shape=_sds((PACK_ROWS, D)),
        scratch_shapes=[pltpu.VMEM((PACK_ROWS, D), F32), pltpu.VMEM((N_DEV, PACK_ROWS, D), F32),
                        pltpu.SemaphoreType.DMA((N_DEV,)), pltpu.SemaphoreType.DMA((N_DEV,))],
        compiler_params=_params(),
    )(*rows)


def _small_adamw(tot, kc, small_w, small_m, small_v):
    names = SMALL_NAMES
    n = len(names)

    def body(*refs):
        tot_ref, kc_ref = refs[0], refs[1]
        w_refs = dict(zip(names, refs[2:2 + n]))
        m_refs = dict(zip(names, refs[2 + n:2 + 2 * n]))
        v_refs = dict(zip(names, refs[2 + 2 * n:2 + 3 * n]))
        loss_ref = refs[2 + 3 * n]
        out_refs = {nm: refs[3 + 3 * n + 4 * k: 7 + 3 * n + 4 * k] for k, nm in enumerate(names)}
        tot = tot_ref[...]
        loss_ref[...] = 0.5 * jnp.sum(tot[ROW_LOSS:ROW_LOSS + 1, :], axis=-1, keepdims=True) * (1.0 / D)

        def update(nm, g):
            g_ref, d_ref, nm_ref, nv_ref = out_refs[nm]
            g_ref[...] = g
            d_ref[...], nm_ref[...], nv_ref[...] = _adamw_math(w_refs[nm][...], g, m_refs[nm][...], v_refs[nm][...])

        for nm, r in GAIN_ROWS.items():
            update(nm, tot[r:r + 2, :])
        update("kv_g", tot[ROW_KV_G:ROW_KV_G + 1, :])
        k = kc_ref[0]
        width = D // N_CHIPS
        g_scale = jnp.zeros((1, width), F32)
        for kk in range(N_CHIPS):
            g_scale = g_scale + jnp.where(k == kk, tot[ROW_POOL_SCALE:ROW_POOL_SCALE + 1, kk * width:(kk + 1) * width], 0.0)
        update("pool_scale", g_scale)
        update("sinks", tot[ROW_SINKS:ROW_SINKS + 1, 0:N_HEADS])

    ins = [tot, kc] + [small_w[nm] for nm in names] + [small_m[nm] for nm in names] + [small_v[nm] for nm in names]
    out_shape = [_sds((1, 1))]
    for nm in names:
        out_shape += [_sds(small_w[nm].shape)] * 4
    outs = pl.pallas_call(
        body, name="small_adamw",
        in_specs=[VSPEC, SSPEC] + [VSPEC] * (3 * n), out_specs=[VSPEC] * len(out_shape), out_shape=out_shape,
        compiler_params=_params(),
    )(*ins)
    return outs[0], {nm: outs[1 + 4 * k: 5 + 4 * k] for k, nm in enumerate(names)}


def _compute_layout(t, full):
    if t.src == "w_gu":
        return full.reshape(2, D, FF)
    if t.src == "pool_w":
        return full.reshape(len(WINDOWS), POOL_G, POOL_G)
    if t.src == "pool_scale":
        return full.reshape(1, D)
    return full.reshape(t.A * t.R, _ncb(t) * t.C)


def kernel(x, p, pre_mix_g, post_mix_g, pre_ffn_g, post_ffn_g, pool_w, pool_scale, kv_g, w_kv, w_q, sinks, w_o, w_gu, w_down, ple_g, w_ple_gate, w_ple_proj, ple_post_g, loss_target, m_pre_mix_g, m_post_mix_g, m_pre_ffn_g, m_post_ffn_g, m_pool_w, m_pool_scale, m_kv_g, m_w_kv, m_w_q, m_sinks, m_w_o, m_w_gu, m_w_down, m_ple_g, m_w_ple_gate, m_w_ple_proj, m_ple_post_g, v_pre_mix_g, v_post_mix_g, v_pre_ffn_g, v_post_ffn_g, v_pool_w, v_pool_scale, v_kv_g, v_w_kv, v_w_q, v_sinks, v_w_o, v_w_gu, v_w_down, v_ple_g, v_w_ple_gate, v_w_ple_proj, v_ple_post_g):
    weights = dict(pre_mix_g=pre_mix_g, post_mix_g=post_mix_g, pre_ffn_g=pre_ffn_g, post_ffn_g=post_ffn_g,
                   pool_w=pool_w, pool_scale=pool_scale, kv_g=kv_g, w_kv=w_kv, w_q=w_q, sinks=sinks, w_o=w_o,
                   w_gu=w_gu, w_down=w_down, ple_g=ple_g, w_ple_gate=w_ple_gate, w_ple_proj=w_ple_proj,
                   ple_post_g=ple_post_g)
    m_in = dict(pre_mix_g=m_pre_mix_g, post_mix_g=m_post_mix_g, pre_ffn_g=m_pre_ffn_g, post_ffn_g=m_post_ffn_g,
                pool_w=m_pool_w, pool_scale=m_pool_scale, kv_g=m_kv_g, w_kv=m_w_kv, w_q=m_w_q, sinks=m_sinks,
                w_o=m_w_o, w_gu=m_w_gu, w_down=m_w_down, ple_g=m_ple_g, w_ple_gate=m_w_ple_gate,
                w_ple_proj=m_w_ple_proj, ple_post_g=m_ple_post_g)
    v_in = dict(pre_mix_g=v_pre_mix_g, post_mix_g=v_post_mix_g, pre_ffn_g=v_pre_ffn_g, post_ffn_g=v_post_ffn_g,
                pool_w=v_pool_w, pool_scale=v_pool_scale, kv_g=v_kv_g, w_kv=v_w_kv, w_q=v_w_q, sinks=v_sinks,
                w_o=v_w_o, w_gu=v_w_gu, w_down=v_w_down, ple_g=v_ple_g, w_ple_gate=v_w_ple_gate,
                w_ple_proj=v_w_ple_proj, ple_post_g=v_ple_post_g)
    order = ["pre_mix_g", "post_mix_g", "pre_ffn_g", "post_ffn_g", "pool_w", "pool_scale", "kv_g", "w_kv", "w_q",
             "sinks", "w_o", "w_gu", "w_down", "ple_g", "w_ple_gate", "w_ple_proj", "ple_post_g"]

    kc = jnp.stack([2 * lax.axis_index("x") + lax.axis_index("y"), lax.axis_index("c")]).astype(jnp.int32)
    s_len = x.shape[1]
    x2d = x.reshape(s_len, D)
    p3d = p.reshape(2, s_len, PLE)
    target = loss_target.reshape(s_len, D)
    kv_g2d = kv_g.reshape(1, D)
    gains = {nm: weights[nm] for nm in GAIN_ROWS}

    def shard_view(src, a):
        t = next(t for t in BIGS.values() if t.src == src)
        return a.reshape(-1, t.R, t.C)

    placed = {nm: _place(t, shard_view(t.src, weights[t.src]), kc, BF) for nm, t in BIGS.items()}
    placed["pool_scale"] = _place(POOL_SCALE, pool_scale.reshape(1, 1, D // N_CHIPS), kc, F32)
    specs = dict(BIGS, pool_scale=POOL_SCALE)

    def gather(names, rows=None):
        rows = rows or {}
        parts = [(specs[nm],) + tuple(rows.get(nm, (0, specs[nm].R))) for nm in names]
        return _gather_rider(parts, [placed[nm] for nm in names])

    def take(names, results):
        for nm, a in zip(names, results):
            placed[nm] = a

    def weight(nm):
        return _compute_layout(specs[nm], placed[nm])

    first = ["pool_w", "pool_scale", "w_gu0", "w_down0"]
    take(first, _run("weights_gather_first", gather(first)))

    y0, x1 = _mixa_fwd(x2d, gains["pre_mix_g"], weight("pool_w"), weight("pool_scale"), gains["post_mix_g"])

    ride = ["w_ple_gate0", "w_ple_proj0", "w_q", "w_kv", "w_o", "w_gu1"]
    (f0, x2), got = _ffn_fwd(0, x1, gains["pre_ffn_g"], weight("w_gu0"), weight("w_down0"), gains["post_ffn_g"],
                             rider=gather(ride, {"w_gu1": (0, 256)}))
    take(ride, got)

    ride = ["w_ple_gate1", "w_ple_proj1"]
    (z0, pe0, x3), got = _ple_fwd(0, x2, p3d, gains["ple_g"], weight("w_ple_gate0"), weight("w_ple_proj0"),
                                  gains["ple_post_g"], rider=gather(ride))
    take(ride, got)

    ride = ["w_gu1"]
    (q, kv), got = _qkv_fwd(x3, gains["pre_mix_g"], kv_g2d, weight("w_q"), weight("w_kv"),
                            rider=gather(ride, {"w_gu1": (256, 448)}))
    take(ride, got)

    ride = ["w_down1", "w_gu1"]
    (attn, y1, x4), got = _attn_fwd(q, kv, sinks, x3, weight("w_o"), gains["post_mix_g"],
                                    rider=gather(ride, {"w_gu1": (448, D)}))
    take(ride, got)

    (f1, x5), _ = _ffn_fwd(1, x4, gains["pre_ffn_g"], weight("w_gu1"), weight("w_down1"), gains["post_ffn_g"])
    (z1, pe1, dx6, loss_row), _ = _ple_fwd(1, x5, p3d, gains["ple_g"], weight("w_ple_gate1"), weight("w_ple_proj1"),
                                           gains["ple_post_g"], target=target)

    local = {}
    landed = {}

    def pair_stage(tag, names):
        ts = [BIGS[nm] for nm in names]
        gs = [local[nm].reshape(_full_shape(t)) for nm, t in zip(names, ts)]
        lands = _pair_exchange(f"grads_pair_exchange_{tag}", ts, gs)
        return [_pair_sum(t, g, l, kc) for t, g, l in zip(ts, gs, lands)]

    def scatter(names, sums):
        return _scatter_rider([BIGS[nm] for nm in names], sums)

    dx5, local["w_ple_gate1"], local["w_ple_proj1"], d_ple1, d_plepost1 = _ple_bwd(
        1, dx6, x5, z1, pe1, p3d, gains["ple_g"], weight("w_ple_gate1"), gains["ple_post_g"])
    (dx4, local["w_gu1"], local["w_down1"], d_preffn1, d_postffn1), _ = _ffn_bwd(
        1, dx5, x4, f1, gains["pre_ffn_g"], weight("w_gu1"), weight("w_down1"), gains["post_ffn_g"])

    group_a = ["w_ple_gate1", "w_ple_proj1", "w_gu1"]
    sums_a = pair_stage("a", group_a)
    (dq, dkv, local["w_o"], d_postmix1, d_sinks), got = _attn_bwd(
        dx4, y1, attn, q, kv, sinks, weight("w_o"), gains["post_mix_g"], rider=scatter(group_a, sums_a))
    for nm, s, l in zip(group_a, sums_a, got):
        landed[nm] = (s, l)

    dx3, local["w_q"], local["w_kv"], d_premix1, d_kvg = _qkv_bwd(
        dq, dkv, x3, dx4, gains["pre_mix_g"], kv_g2d, weight("w_q"), weight("w_kv"))
    dx2, local["w_ple_gate0"], local["w_ple_proj0"], d_ple0, d_plepost0 = _ple_bwd(
        0, dx3, x2, z0, pe0, p3d, gains["ple_g"], weight("w_ple_gate0"), gains["ple_post_g"])

    group_b = ["w_down1", "w_o", "w_q", "w_kv", "w_ple_gate0", "w_ple_proj0"]
    sums_b = pair_stage("b", group_b)
    (dx1, local["w_gu0"], local["w_down0"], d_preffn0, d_postffn0), got = _ffn_bwd(
        0, dx2, x1, f0, gains["pre_ffn_g"], weight("w_gu0"), weight("w_down0"), gains["post_ffn_g"],
        rider=scatter(group_b, sums_b))
    for nm, s, l in zip(group_b, sums_b, got):
        landed[nm] = (s, l)

    dx0, local["pool_w"], d_scale, d_postmix0, d_premix0 = _mixa_bwd(
        dx1, x2d, y0, gains["pre_mix_g"], weight("pool_w"), weight("pool_scale"), gains["post_mix_g"])

    group_c = ["w_gu0", "w_down0", "pool_w"]
    sums_c = pair_stage("c", group_c)
    got = _run("grads_chip_exchange_c", scatter(group_c, sums_c))
    for nm, s, l in zip(group_c, sums_c, got):
        landed[nm] = (s, l)

    rows = [d_premix0, d_premix1, d_postmix0, d_postmix1, d_preffn0, d_preffn1, d_postffn0, d_postffn1,
            d_ple0, d_ple1, d_plepost0, d_plepost1, d_kvg, d_scale, d_sinks, loss_row]
    as2d = lambda a: a.reshape(1, D) if a.ndim == 1 else a
    tot = _small_all_reduce(rows)
    loss, small = _small_adamw(tot, kc, {nm: as2d(weights[nm]) for nm in SMALL_NAMES},
                               {nm: as2d(m_in[nm]) for nm in SMALL_NAMES},
                               {nm: as2d(v_in[nm]) for nm in SMALL_NAMES})

    halves = []
    for src in BIG_SOURCES:
        ts = [t for t in BIGS.values() if t.src == src]
        n_layers = shard_view(src, weights[src]).shape[0]
        acc = None
        for t in ts:
            s, l = landed[t.name]
            acc = _chip_sum(t, s, l, kc, n_layers, acc)
        halves.append(acc)
    full_grads = _pair_share(halves)

    out = {"grad": {}, "delta": {}, "new_m": {}, "new_v": {}}
    for src, g in zip(BIG_SOURCES, full_grads):
        t = next(t for t in BIGS.values() if t.src == src)
        res = _adamw(src, t.rb, shard_view(src, weights[src]), g, shard_view(src, m_in[src]), shard_view(src, v_in[src]))
        shape = weights[src].shape
        for kind, a in zip(("grad", "delta", "new_m", "new_v"), res):
            out[kind][src] = a.reshape(shape)
    for nm in SMALL_NAMES:
        shape = weights[nm].shape
        for kind, a in zip(("grad", "delta", "new_m", "new_v"), small[nm]):
            out[kind][nm] = a.reshape(shape)

    return (loss.reshape(()), dx0.reshape(x.shape),
            *[out["grad"][nm] for nm in order], *[out["delta"][nm] for nm in order],
            *[out["new_m"][nm] for nm in order], *[out["new_v"][nm] for nm in order])
```

```python
import collections

import jax
import jax.numpy as jnp
from jax import lax
from jax.experimental import pallas as pl
from jax.experimental.pallas import tpu as pltpu

D = 1024
FF = 2816
N_HEADS = 16
HEAD_DIM = 64
N_KV_HEADS = 4
GQA = N_HEADS // N_KV_HEADS
KVD = N_KV_HEADS * HEAD_DIM
PLE = 256
BLK = 128
WINDOWS = (2, 4, 8, 16)
POOL_G = 256
HALO = 16
EPS = 1e-6
NEG_INF = -1e30
ATT_SCALE = HEAD_DIM ** -0.5
SLOPES = tuple(2.0 ** (-8.0 * (h + 1) / N_HEADS) for h in range(N_HEADS))
N_CHIPS = 4
N_DEV = 8

LR, B1, B2, AEPS, WD, STEP = 0.001, 0.9, 0.999, 1e-08, 0.01, 10
BC1 = 1.0 - B1 ** STEP
BC2 = 1.0 - B2 ** STEP

BF = jnp.bfloat16
F32 = jnp.float32
MESH = pl.DeviceIdType.MESH
VMEM_LIMIT_V7X = 56 * 1024 * 1024
TM = 256
TM_FFN_BWD = 512
FF_CHUNK = 256
FF_HALF = FF // 2

VSPEC = pl.BlockSpec(memory_space=pltpu.VMEM)
SSPEC = pl.BlockSpec(memory_space=pltpu.SMEM)
ANYSPEC = pl.BlockSpec(memory_space=pl.ANY)


def _params(n_grid=0):
    sem = ("arbitrary",) * n_grid if n_grid else None
    return pltpu.CompilerParams(dimension_semantics=sem, vmem_limit_bytes=VMEM_LIMIT_V7X)


def _sds(shape, dtype=F32):
    return jax.ShapeDtypeStruct(tuple(shape), dtype)


Rider = collections.namedtuple("Rider", "arrays out_shapes aliases scratch start mid finish")
MID_NUM, MID_DEN = 5, 8


def _call(body, *, name, grid, in_specs, out_specs, out_shape, args, scratch_shapes=(), rider=None):
    ni, no, ns = len(in_specs), len(out_specs), len(scratch_shapes)
    if rider is None:
        outs = pl.pallas_call(body, name=name, grid=grid, in_specs=in_specs, out_specs=out_specs,
                              out_shape=out_shape, scratch_shapes=list(scratch_shapes),
                              compiler_params=_params(len(grid)))(*args)
        return list(outs), []
    ri, ro = len(rider.arrays), len(rider.out_shapes)

    def full(*refs):
        ins, refs = refs[:ni], refs[ni:]
        rins, refs = refs[:ri], refs[ri:]
        outs, refs = refs[:no], refs[no:]
        routs, refs = refs[:ro], refs[ro:]
        scr, rscr = refs[:ns], refs[ns:]
        ids = [pl.program_id(a) for a in range(len(grid))]
        first = ids[0] == 0
        last = ids[0] == grid[0] - 1
        for a in range(1, len(grid)):
            first = first & (ids[a] == 0)
            last = last & (ids[a] == grid[a] - 1)

        @pl.when(first)
        def _():
            rider.start(rins, routs, rscr)

        if rider.mid is not None:
            assert len(grid) == 1

            @pl.when(ids[0] == (grid[0] * MID_NUM) // MID_DEN)
            def _():
                rider.mid(rins, routs, rscr)

        body(*ins, *outs, *scr)

        @pl.when(last)
        def _():
            rider.finish(rins, routs, rscr)

    outs = pl.pallas_call(
        full, name=name, grid=grid,
        in_specs=list(in_specs) + [ANYSPEC] * ri, out_specs=list(out_specs) + [ANYSPEC] * ro,
        out_shape=list(out_shape) + list(rider.out_shapes),
        scratch_shapes=list(scratch_shapes) + list(rider.scratch),
        input_output_aliases={ni + a: no + b for a, b in rider.aliases.items()},
        compiler_params=_params(len(grid)))(*args, *rider.arrays)
    return list(outs[:no]), list(outs[no:])


def _run(name, rider):
    ri = len(rider.arrays)

    def body(*refs):
        rins, routs, rscr = refs[:ri], refs[ri:ri + len(rider.out_shapes)], refs[ri + len(rider.out_shapes):]
        rider.start(rins, routs, rscr)
        if rider.mid is not None:
            rider.mid(rins, routs, rscr)
        rider.finish(rins, routs, rscr)

    return pl.pallas_call(
        body, name=name, in_specs=[ANYSPEC] * ri, out_specs=[ANYSPEC] * len(rider.out_shapes),
        out_shape=list(rider.out_shapes), scratch_shapes=list(rider.scratch),
        input_output_aliases=dict(rider.aliases), compiler_params=_params())(*rider.arrays)


def _rms_fwd(x, g):
    r = lax.rsqrt(jnp.mean(x * x, axis=-1, keepdims=True) + EPS)
    return x * r * g


def _rms_bwd(x, g, dy):
    r = lax.rsqrt(jnp.mean(x * x, axis=-1, keepdims=True) + EPS)
    xn = x * r
    dxn = dy * g
    dx = r * (dxn - xn * jnp.mean(dxn * xn, axis=-1, keepdims=True))
    return dx, dy * xn


def _rowsum(a):
    return jnp.sum(a, axis=0, keepdims=True)


def _sigmoid(z):
    return 1.0 / (1.0 + jnp.exp(-z))


def _dot(a, b):
    return jnp.dot(a, b, preferred_element_type=F32)


def _dot_nt(a, b):
    return lax.dot_general(a, b, (((1,), (1,)), ((), ())), preferred_element_type=F32)


def _dot_tn(a, b):
    return lax.dot_general(a, b, (((0,), (0,)), ((), ())), preferred_element_type=F32)


def _row_spec(tm, width=D):
    return pl.BlockSpec((tm, width), lambda i: (i, 0))


def _const_spec(shape):
    zeros = (0,) * len(shape)
    return pl.BlockSpec(tuple(shape), lambda *_: zeros)


def _pool_delta(he, pos):
    out = []
    for gi, w in enumerate(WINDOWS):
        hg = he[:, gi * POOL_G:(gi + 1) * POOL_G]
        s = hg
        k = 1
        while k < w:
            s = s + pltpu.roll(s, k, 0)
            k *= 2
        cnt = jnp.maximum(jnp.minimum(pos + 1, w), 1).astype(F32)
        out.append(s / cnt - hg)
    return out


def _load_with_halo_before(x_ref, i, tm):
    r0 = pl.multiple_of(i * tm, tm)
    hs = pl.multiple_of(jnp.maximum(i * tm - HALO, 0), 8)
    xh = jnp.where(i > 0, x_ref[pl.ds(hs, HALO), :], 0.0)
    xt = x_ref[pl.ds(r0, tm), :]
    return xt, jnp.concatenate([xh, xt], axis=0)


def _mixa_fwd(x, pre_g, pool_w, pool_scale, post_g):
    s_len = x.shape[0]
    n = s_len // TM

    def body(x_ref, pg_ref, w_ref, sc_ref, qg_ref, y_ref, x1_ref):
        i = pl.program_id(0)
        xt, xe = _load_with_halo_before(x_ref, i, TM)
        he = _rms_fwd(xe, pg_ref[0:1, :])
        pos = i * TM - HALO + lax.broadcasted_iota(jnp.int32, (TM + HALO, 1), 0)
        ds = _pool_delta(he, pos)
        ys = [_dot(ds[gi][HALO:, :].astype(BF), w_ref[gi]) for gi in range(len(WINDOWS))]
        y = jnp.concatenate(ys, axis=1) * sc_ref[...]
        y_ref[...] = y
        x1_ref[...] = xt + _rms_fwd(y, qg_ref[0:1, :])

    outs, _ = _call(body, name="mixa_fwd", grid=(n,),
                    in_specs=[VSPEC] * 5, out_specs=[_row_spec(TM), _row_spec(TM)],
                    out_shape=[_sds((s_len, D)), _sds((s_len, D))],
                    args=[x, pre_g, pool_w, pool_scale, post_g])
    return outs


def _mixa_bwd(dx1, x, y, pre_g, pool_w, pool_scale, post_g):
    s_len = x.shape[0]
    n = s_len // TM
    ng = len(WINDOWS)

    def body(dx_ref, x_ref, y_ref, pg_ref, w_ref, sc_ref, qg_ref,
             dx0_ref, dw_ref, dsc_ref, dqg_ref, dpg_ref, wacc):
        i = pl.program_id(0)

        @pl.when(i == 0)
        def _():
            wacc[...] = jnp.zeros_like(wacc)
            dsc_ref[...] = jnp.zeros_like(dsc_ref)
            dqg_ref[...] = jnp.zeros_like(dqg_ref)
            dpg_ref[...] = jnp.zeros_like(dpg_ref)

        r0 = pl.multiple_of(i * TM, TM)
        xt, xe = _load_with_halo_before(x_ref, i, TM)
        he = _rms_fwd(xe, pg_ref[0:1, :])
        pos_b = i * TM - HALO + lax.broadcasted_iota(jnp.int32, (TM + HALO, 1), 0)
        ds = _pool_delta(he, pos_b)

        last = i == n - 1
        a0 = pl.multiple_of(jnp.minimum(i * TM + TM, s_len - HALO), 8)
        ye = jnp.concatenate([y_ref[pl.ds(r0, TM), :], y_ref[pl.ds(a0, HALO), :]], axis=0)
        dt = dx_ref[pl.ds(r0, TM), :]
        de = jnp.concatenate([dt, jnp.where(last, 0.0, dx_ref[pl.ds(a0, HALO), :])], axis=0)
        dye, prod = _rms_bwd(ye, qg_ref[0:1, :], de)
        dqg_ref[...] += _rowsum(prod[:TM, :])
        dys = dye * sc_ref[...]
        pos_a = i * TM + lax.broadcasted_iota(jnp.int32, (TM + HALO, 1), 0)

        dhs, dscs = [], []
        for gi, w in enumerate(WINDOWS):
            sl = slice(gi * POOL_G, (gi + 1) * POOL_G)
            wg = w_ref[gi]
            dys_g = dys[:, sl].astype(BF)
            d_g = ds[gi][HALO:, :].astype(BF)
            ypre = _dot(d_g, wg)
            dscs.append(_rowsum(dye[:TM, sl] * ypre))
            wacc[gi] += _dot_tn(d_g, dys_g[:TM, :])
            dd = _dot_nt(dys_g, wg)
            cnt = jnp.minimum(pos_a + 1, w).astype(F32)
            a = dd / cnt
            k = 1
            while k < w:
                a = a + pltpu.roll(a, TM + HALO - k, 0)
                k *= 2
            dhs.append(a[:TM, :] - dd[:TM, :])
        dsc_ref[...] += jnp.concatenate(dscs, axis=1)
        dh = jnp.concatenate(dhs, axis=1)
        dxp, prod2 = _rms_bwd(xt, pg_ref[0:1, :], dh)
        dpg_ref[...] += _rowsum(prod2)
        dx0_ref[...] = dt + dxp

        @pl.when(last)
        def _():
            dw_ref[...] = wacc[...].astype(BF)

    outs, _ = _call(
        body, name="mixa_bwd", grid=(n,), in_specs=[VSPEC] * 7,
        out_specs=[_row_spec(TM), _const_spec((ng, POOL_G, POOL_G)), _const_spec((1, D)),
                   _const_spec((1, D)), _const_spec((1, D))],
        out_shape=[_sds((s_len, D)), _sds((ng, POOL_G, POOL_G), BF), _sds((1, D)), _sds((1, D)), _sds((1, D))],
        scratch_shapes=[pltpu.VMEM((ng, POOL_G, POOL_G), F32)],
        args=[dx1, x, y, pre_g, pool_w, pool_scale, post_g])
    return outs


def _ffn_fwd(layer, x1, pre_g, wgu, wd, post_g, rider=None):
    s_len = x1.shape[0]

    def body(x_ref, pg_ref, wgu_ref, wd_ref, qg_ref, f_ref, x2_ref):
        x = x_ref[...]
        h = _rms_fwd(x, pg_ref[layer:layer + 1, :]).astype(BF)
        f = jnp.zeros((TM, D), F32)
        for c in range(FF // FF_HALF):
            cols = slice(c * FF_HALF, (c + 1) * FF_HALF)
            g = _dot(h, wgu_ref[0, :, cols])
            u = _dot(h, wgu_ref[1, :, cols])
            act = g * _sigmoid(g) * u
            f = f + _dot(act.astype(BF), wd_ref[cols, :])
        f_ref[...] = f
        x2_ref[...] = x + _rms_fwd(f, qg_ref[layer:layer + 1, :])

    return _call(body, name=f"ffn_fwd{layer}", grid=(s_len // TM,),
                 in_specs=[_row_spec(TM), VSPEC, VSPEC, VSPEC, VSPEC],
                 out_specs=[_row_spec(TM), _row_spec(TM)],
                 out_shape=[_sds((s_len, D)), _sds((s_len, D))],
                 args=[x1, pre_g, wgu, wd, post_g], rider=rider)


def _ffn_bwd(layer, dx2, x1, f, pre_g, wgu, wd, post_g, rider=None):
    s_len = x1.shape[0]
    tm = TM_FFN_BWD
    n = s_len // tm
    nc = FF // FF_CHUNK

    def edge_rows(c, i):
        return (jnp.where((c == 0) | (c == nc - 1), i, n - 1), 0)

    def body(dx_ref, x_ref, f_ref, pg_ref, wgu_ref, wd_ref, qg_ref,
             dx1_ref, dwgu_ref, dwd_ref, dpg_ref, dqg_ref, h_s, df_s, dh_s, accg, accu, accd):
        c = pl.program_id(0)
        i = pl.program_id(1)
        rows = pl.ds(pl.multiple_of(i * tm, tm), tm)
        pg = pg_ref[layer:layer + 1, :]

        @pl.when((c == 0) & (i == 0))
        def _():
            dpg_ref[...] = jnp.zeros_like(dpg_ref)
            dqg_ref[...] = jnp.zeros_like(dqg_ref)

        @pl.when(c == 0)
        def _():
            h_s[rows, :] = _rms_fwd(x_ref[...], pg).astype(BF)
            df, prod = _rms_bwd(f_ref[...], qg_ref[layer:layer + 1, :], dx_ref[...])
            df_s[rows, :] = df.astype(BF)
            dqg_ref[...] += _rowsum(prod)

        @pl.when(i == 0)
        def _():
            accg[...] = jnp.zeros_like(accg)
            accu[...] = jnp.zeros_like(accu)
            accd[...] = jnp.zeros_like(accd)

        h = h_s[rows, :]
        df = df_s[rows, :]
        wg = wgu_ref[0]
        wu = wgu_ref[1]
        g = _dot(h, wg)
        u = _dot(h, wu)
        sg = _sigmoid(g)
        a = g * sg
        dact = _dot_nt(df, wd_ref[...])
        accd[...] += _dot_tn((a * u).astype(BF), df)
        du = (dact * a).astype(BF)
        dg = (dact * u * (sg * (1.0 + g * (1.0 - sg)))).astype(BF)
        accg[...] += _dot_tn(h, dg)
        accu[...] += _dot_tn(h, du)
        dh = _dot_nt(dg, wg) + _dot_nt(du, wu)

        @pl.when(c == 0)
        def _():
            dh_s[rows, :] = dh

        @pl.when((c > 0) & (c < nc - 1))
        def _():
            dh_s[rows, :] += dh

        @pl.when(c == nc - 1)
        def _():
            dxp, prod = _rms_bwd(x_ref[...], pg, dh_s[rows, :] + dh)
            dpg_ref[...] += _rowsum(prod)
            dx1_ref[...] = dx_ref[...] + dxp

        @pl.when(i == n - 1)
        def _():
            dwgu_ref[0] = accg[...].astype(BF)
            dwgu_ref[1] = accu[...].astype(BF)
            dwd_ref[...] = accd[...].astype(BF)

    return _call(
        body, name=f"ffn_bwd{layer}", grid=(nc, n),
        in_specs=[pl.BlockSpec((tm, D), edge_rows), pl.BlockSpec((tm, D), edge_rows),
                  pl.BlockSpec((tm, D), lambda c, i: (jnp.where(c == 0, i, n - 1), 0)),
                  VSPEC,
                  pl.BlockSpec((2, D, FF_CHUNK), lambda c, i: (0, 0, c)),
                  pl.BlockSpec((FF_CHUNK, D), lambda c, i: (c, 0)),
                  VSPEC],
        out_specs=[pl.BlockSpec((tm, D), lambda c, i: (jnp.where(c == nc - 1, i, 0), 0)),
                   pl.BlockSpec((2, D, FF_CHUNK), lambda c, i: (0, 0, c)),
                   pl.BlockSpec((FF_CHUNK, D), lambda c, i: (c, 0)),
                   _const_spec((1, D)), _const_spec((1, D))],
        out_shape=[_sds((s_len, D)), _sds((2, D, FF), BF), _sds((FF, D), BF), _sds((1, D)), _sds((1, D))],
        scratch_shapes=[pltpu.VMEM((s_len, D), BF), pltpu.VMEM((s_len, D), BF), pltpu.VMEM((s_len, D), F32),
                        pltpu.VMEM((D, FF_CHUNK), F32), pltpu.VMEM((D, FF_CHUNK), F32),
                        pltpu.VMEM((FF_CHUNK, D), F32)],
        args=[dx2, x1, f, pre_g, wgu, wd, post_g], rider=rider)


def _ple_fwd(layer, x2, p, ple_g, w_gate, w_proj, post_g, target=None, rider=None):
    s_len = x2.shape[0]
    final = target is not None

    def body(*refs):
        if final:
            x_ref, p_ref, g_ref, wg_ref, wp_ref, qg_ref, t_ref, z_ref, pe_ref, dx_ref, lv_ref = refs
        else:
            x_ref, p_ref, g_ref, wg_ref, wp_ref, qg_ref, z_ref, pe_ref, x3_ref = refs
        x = x_ref[...]
        r = _rms_fwd(x, g_ref[layer:layer + 1, :]).astype(BF)
        z = _dot(r, wg_ref[...])
        pe = _dot(p_ref[...].astype(BF), wp_ref[...])
        z_ref[...] = z
        pe_ref[...] = pe
        x3 = x + _rms_fwd(pe * _sigmoid(z), qg_ref[layer:layer + 1, :])
        if final:
            @pl.when(pl.program_id(0) == 0)
            def _():
                lv_ref[...] = jnp.zeros_like(lv_ref)
            err = x3 - t_ref[...]
            dx_ref[...] = err * (1.0 / D)
            lv_ref[...] += _rowsum(err * err)
        else:
            x3_ref[...] = x3

    p_spec = pl.BlockSpec((None, TM, PLE), lambda i: (layer, i, 0))
    in_specs = [_row_spec(TM), p_spec, VSPEC, VSPEC, VSPEC, VSPEC]
    args = [x2, p, ple_g, w_gate, w_proj, post_g]
    out_specs = [_row_spec(TM), _row_spec(TM), _row_spec(TM)]
    out_shape = [_sds((s_len, D))] * 3
    if final:
        in_specs.append(_row_spec(TM))
        args.append(target)
        out_specs.append(_const_spec((1, D)))
        out_shape.append(_sds((1, D)))
    return _call(body, name=f"ple_fwd{layer}", grid=(s_len // TM,), in_specs=in_specs, out_specs=out_specs,
                 out_shape=out_shape, args=args, rider=rider)


def _ple_bwd(layer, dx3, x2, z, pe, p, ple_g, w_gate, post_g):
    s_len = x2.shape[0]
    n = s_len // TM

    def body(dx_ref, x_ref, z_ref, pe_ref, p_ref, g_ref, wg_ref, qg_ref,
             dx2_ref, dwg_ref, dwp_ref, dg_ref, dqg_ref, gacc, pacc):
        i = pl.program_id(0)

        @pl.when(i == 0)
        def _():
            gacc[...] = jnp.zeros_like(gacc)
            pacc[...] = jnp.zeros_like(pacc)
            dg_ref[...] = jnp.zeros_like(dg_ref)
            dqg_ref[...] = jnp.zeros_like(dqg_ref)

        dx = dx_ref[...]
        x = x_ref[...]
        pe_v = pe_ref[...]
        gate = _sigmoid(z_ref[...])
        de, prod = _rms_bwd(pe_v * gate, qg_ref[layer:layer + 1, :], dx)
        dqg_ref[...] += _rowsum(prod)
        dpe = (de * gate).astype(BF)
        dz = (de * pe_v * gate * (1.0 - gate)).astype(BF)
        pacc[...] += _dot_tn(p_ref[...].astype(BF), dpe)
        g = g_ref[layer:layer + 1, :]
        r = _rms_fwd(x, g).astype(BF)
        gacc[...] += _dot_tn(r, dz)
        dr = _dot_nt(dz, wg_ref[...])
        dxp, prod2 = _rms_bwd(x, g, dr)
        dg_ref[...] += _rowsum(prod2)
        dx2_ref[...] = dx + dxp

        @pl.when(i == n - 1)
        def _():
            dwg_ref[...] = gacc[...].astype(BF)
            dwp_ref[...] = pacc[...].astype(BF)

    p_spec = pl.BlockSpec((None, TM, PLE), lambda i: (layer, i, 0))
    outs, _ = _call(
        body, name=f"ple_bwd{layer}", grid=(n,),
        in_specs=[_row_spec(TM), _row_spec(TM), _row_spec(TM), _row_spec(TM), p_spec, VSPEC, VSPEC, VSPEC],
        out_specs=[_row_spec(TM), _const_spec((D, D)), _const_spec((PLE, D)), _const_spec((1, D)), _const_spec((1, D))],
        out_shape=[_sds((s_len, D)), _sds((D, D), BF), _sds((PLE, D), BF), _sds((1, D)), _sds((1, D))],
        scratch_shapes=[pltpu.VMEM((D, D), F32), pltpu.VMEM((PLE, D), F32)],
        args=[dx3, x2, z, pe, p, ple_g, w_gate, post_g])
    return outs


def _qkv_fwd(x3, q_g, kv_g, w_q, w_kv, rider=None):
    s_len = x3.shape[0]

    def body(x_ref, qg_ref, kg_ref, wq_ref, wkv_ref, q_ref, kv_ref):
        x = x_ref[...]
        q_ref[...] = _dot(_rms_fwd(x, qg_ref[1:2, :]).astype(BF), wq_ref[...]).astype(BF)
        kv_ref[...] = _dot(_rms_fwd(x, kg_ref[...]).astype(BF), wkv_ref[...]).astype(BF)

    return _call(body, name="qkv_fwd", grid=(s_len // TM,),
                 in_specs=[_row_spec(TM), VSPEC, VSPEC, VSPEC, VSPEC],
                 out_specs=[_row_spec(TM), _row_spec(TM, 2 * KVD)],
                 out_shape=[_sds((s_len, D), BF), _sds((s_len, 2 * KVD), BF)],
                 args=[x3, q_g, kv_g, w_q, w_kv], rider=rider)


def _qkv_bwd(dq, dkv, x3, dx4, q_g, kv_g, w_q, w_kv):
    s_len = x3.shape[0]
    n = s_len // TM

    def body(dq_ref, dkv_ref, x_ref, dx_ref, qg_ref, kg_ref, wq_ref, wkv_ref,
             dx3_ref, dwq_ref, dwkv_ref, dqg_ref, dkg_ref, qacc, kacc):
        i = pl.program_id(0)

        @pl.when(i == 0)
        def _():
            qacc[...] = jnp.zeros_like(qacc)
            kacc[...] = jnp.zeros_like(kacc)
            dqg_ref[...] = jnp.zeros_like(dqg_ref)
            dkg_ref[...] = jnp.zeros_like(dkg_ref)

        x = x_ref[...]
        qg = qg_ref[1:2, :]
        kg = kg_ref[...]
        dq_v = dq_ref[...]
        dkv_v = dkv_ref[...].astype(BF)
        qacc[...] += _dot_tn(_rms_fwd(x, qg).astype(BF), dq_v)
        kacc[...] += _dot_tn(_rms_fwd(x, kg).astype(BF), dkv_v)
        dxq, prod_q = _rms_bwd(x, qg, _dot_nt(dq_v, wq_ref[...]))
        dxk, prod_k = _rms_bwd(x, kg, _dot_nt(dkv_v, wkv_ref[...]))
        dqg_ref[...] += _rowsum(prod_q)
        dkg_ref[...] += _rowsum(prod_k)
        dx3_ref[...] = dx_ref[...] + dxq + dxk

        @pl.when(i == n - 1)
        def _():
            dwq_ref[...] = qacc[...].astype(BF)
            dwkv_ref[...] = kacc[...].astype(BF)

    outs, _ = _call(
        body, name="qkv_bwd", grid=(n,),
        in_specs=[_row_spec(TM), _row_spec(TM, 2 * KVD), _row_spec(TM), _row_spec(TM), VSPEC, VSPEC, VSPEC, VSPEC],
        out_specs=[_row_spec(TM), _const_spec((D, D)), _const_spec((D, 2 * KVD)),
                   _const_spec((1, D)), _const_spec((1, D))],
        out_shape=[_sds((s_len, D)), _sds((D, D), BF), _sds((D, 2 * KVD), BF), _sds((1, D)), _sds((1, D))],
        scratch_shapes=[pltpu.VMEM((D, D), F32), pltpu.VMEM((D, 2 * KVD), F32)],
        args=[dq, dkv, x3, dx4, q_g, kv_g, w_q, w_kv])
    return outs


def _attn_block(i, q, kvw, sink_ref):
    off = jnp.where(i > 0, BLK, 0)
    rel = (lax.broadcasted_iota(jnp.int32, (BLK, 2 * BLK), 0)
           - lax.broadcasted_iota(jnp.int32, (BLK, 2 * BLK), 1) + off)
    valid = (rel >= 0) & (rel < BLK)
    relf = rel.astype(F32)
    out = []
    for h in range(N_HEADS):
        kh = h // GQA
        qh = q[:, h * HEAD_DIM:(h + 1) * HEAD_DIM]
        k = kvw[:, kh * HEAD_DIM:(kh + 1) * HEAD_DIM]
        v = kvw[:, KVD + kh * HEAD_DIM:KVD + (kh + 1) * HEAD_DIM]
        s = _dot_nt(qh, k) * ATT_SCALE - SLOPES[h] * relf
        s = jnp.where(valid, s, NEG_INF)
        sink = sink_ref[0, h]
        m = jnp.maximum(jnp.max(s, axis=-1, keepdims=True), sink)
        e = jnp.exp(s - m)
        es = jnp.exp(sink - m)
        inv = 1.0 / (jnp.sum(e, axis=-1, keepdims=True) + es)
        out.append((e * inv, es * inv, qh, k, v))
    return out


def _kv_window(kv_ref, i):
    ks = pl.multiple_of(jnp.maximum(i * BLK - BLK, 0), BLK)
    return ks, kv_ref[pl.ds(ks, 2 * BLK), :]


def _attn_fwd(q, kv, sinks, x3, w_o, post_g, rider=None):
    s_len = q.shape[0]

    def body(q_ref, kv_ref, sk_ref, x_ref, wo_ref, g_ref, a_ref, y_ref, x4_ref):
        i = pl.program_id(0)
        _, kvw = _kv_window(kv_ref, i)
        heads = _attn_block(i, q_ref[...], kvw, sk_ref)
        attn = jnp.concatenate([_dot(p.astype(BF), v) for p, _, _, _, v in heads], axis=1)
        a_ref[...] = attn
        y = _dot(attn.astype(BF), wo_ref[...])
        y_ref[...] = y
        x4_ref[...] = x_ref[...] + _rms_fwd(y, g_ref[1:2, :])

    return _call(body, name="attn_fwd", grid=(s_len // BLK,),
                 in_specs=[_row_spec(BLK), VSPEC, SSPEC, _row_spec(BLK), VSPEC, VSPEC],
                 out_specs=[_row_spec(BLK)] * 3, out_shape=[_sds((s_len, D))] * 3,
                 args=[q, kv, sinks, x3, w_o, post_g], rider=rider)


def _attn_bwd(dx4, y, attn, q, kv, sinks, w_o, post_g, rider=None):
    s_len = q.shape[0]
    n = s_len // BLK

    def body(dx_ref, y_ref, a_ref, q_ref, kv_ref, sk_ref, wo_ref, g_ref,
             dq_ref, dkv_ref, dwo_ref, dg_ref, dsk_ref, wacc):
        i = pl.program_id(0)

        @pl.when(i == 0)
        def _():
            dkv_ref[...] = jnp.zeros_like(dkv_ref)
            wacc[...] = jnp.zeros_like(wacc)
            dg_ref[...] = jnp.zeros_like(dg_ref)
            dsk_ref[...] = jnp.zeros_like(dsk_ref)

        dy, prod = _rms_bwd(y_ref[...], g_ref[1:2, :], dx_ref[...])
        dg_ref[...] += _rowsum(prod)
        dyb = dy.astype(BF)
        attn = a_ref[...]
        wacc[...] += _dot_tn(attn.astype(BF), dyb)
        d_o = _dot_nt(dyb, wo_ref[...])
        dod = d_o * attn
        ks, kvw = _kv_window(kv_ref, i)
        heads = _attn_block(i, q_ref[...], kvw, sk_ref)
        lane = lax.broadcasted_iota(jnp.int32, (1, D), 1)
        dqs = []
        dks = [None] * N_KV_HEADS
        dvs = [None] * N_KV_HEADS
        dsk = jnp.zeros((1, D), F32)
        for h, (p, ps, qh, k, v) in enumerate(heads):
            kh = h // GQA
            hs = slice(h * HEAD_DIM, (h + 1) * HEAD_DIM)
            do_h = d_o[:, hs].astype(BF)
            dsum = jnp.sum(dod[:, hs], axis=-1, keepdims=True)
            dp = _dot_nt(do_h, v)
            dsb = (p * (dp - dsum) * ATT_SCALE).astype(BF)
            dsk = dsk + jnp.where(lane == h, -_rowsum(ps * dsum), 0.0)
            dqs.append(_dot(dsb, k))
            dk = _dot_tn(dsb, qh)
            dv = _dot_tn(p.astype(BF), do_h)
            dks[kh] = dk if dks[kh] is None else dks[kh] + dk
            dvs[kh] = dv if dvs[kh] is None else dvs[kh] + dv
        dsk_ref[...] += dsk
        dq_ref[...] = jnp.concatenate(dqs, axis=1).astype(BF)
        dkv_ref[pl.ds(ks, 2 * BLK), :] += jnp.concatenate(dks + dvs, axis=1)

        @pl.when(i == n - 1)
        def _():
            dwo_ref[...] = wacc[...].astype(BF)

    return _call(
        body, name="attn_bwd", grid=(n,),
        in_specs=[_row_spec(BLK), _row_spec(BLK), _row_spec(BLK), _row_spec(BLK), VSPEC, SSPEC, VSPEC, VSPEC],
        out_specs=[_row_spec(BLK), _const_spec((s_len, 2 * KVD)), _const_spec((D, D)),
                   _const_spec((1, D)), _const_spec((1, D))],
        out_shape=[_sds((s_len, D), BF), _sds((s_len, 2 * KVD)), _sds((D, D), BF), _sds((1, D)), _sds((1, D))],
        scratch_shapes=[pltpu.VMEM((D, D), F32)],
        args=[dx4, y, attn, q, kv, sinks, w_o, post_g], rider=rider)


Big = collections.namedtuple("Big", "name src layer L A R C rb")


def _bigs():
    out = {"pool_w": Big("pool_w", "pool_w", None, 4, 4, POOL_G // N_CHIPS, POOL_G, 32)}
    for l in range(2):
        out[f"w_gu{l}"] = Big(f"w_gu{l}", "w_gu", l, 1, 2, D, FF_HALF, 256)
        out[f"w_down{l}"] = Big(f"w_down{l}", "w_down", l, 1, 4, FF // N_CHIPS, D, 352)
        out[f"w_ple_gate{l}"] = Big(f"w_ple_gate{l}", "w_ple_gate", l, 1, 4, D // N_CHIPS, D, 128)
        out[f"w_ple_proj{l}"] = Big(f"w_ple_proj{l}", "w_ple_proj", l, 1, 1, PLE, D // N_CHIPS, 128)
    out["w_q"] = Big("w_q", "w_q", None, 1, 4, D // N_CHIPS, D, 128)
    out["w_o"] = Big("w_o", "w_o", None, 1, 4, D // N_CHIPS, D, 128)
    out["w_kv"] = Big("w_kv", "w_kv", None, 1, 4, D // N_CHIPS, 2 * KVD, 128)
    return out


BIGS = _bigs()
POOL_SCALE = Big("pool_scale", "pool_scale", None, 1, 1, 1, D // N_CHIPS, 1)
BIG_SOURCES = ("w_gu", "w_down", "w_ple_gate", "w_ple_proj", "w_q", "w_o", "w_kv", "pool_w")


def _ncb(t):
    return N_CHIPS // t.A


def _full_shape(t, rows=None):
    return (t.L, t.A, t.R if rows is None else rows, _ncb(t) * t.C)


def _slot_index(t, k):
    return k // _ncb(t), k % _ncb(t)


def _slot(ref, t, k, row0, rows):
    a, cb = _slot_index(t, k)
    return ref.at[:, a, pl.ds(row0, rows), pl.ds(pl.multiple_of(cb * t.C, 128), t.C)]


def _place(t, w, kc, out_dtype):
    rb = min(t.R, 2 * t.rb)

    def body(kc_ref, w_ref, o_ref):
        del kc_ref
        o_ref[...] = w_ref[...].astype(out_dtype)

    def in_map(l, j, kc_ref):
        return (l if t.layer is None else t.layer, j, 0)

    def out_map(l, j, kc_ref):
        a, cb = _slot_index(t, kc_ref[0])
        return (l, a, j, cb)

    return pl.pallas_call(
        body, name=f"place_{t.name}",
        grid_spec=pltpu.PrefetchScalarGridSpec(
            num_scalar_prefetch=1, grid=(t.L, t.R // rb),
            in_specs=[pl.BlockSpec((None, rb, t.C), in_map)],
            out_specs=pl.BlockSpec((None, None, rb, t.C), out_map)),
        out_shape=_sds(_full_shape(t), out_dtype),
        compiler_params=_params(2),
    )(kc, w)


def _mesh_position():
    x, y, c = lax.axis_index("x"), lax.axis_index("y"), lax.axis_index("c")
    chips = [(1 - x, y), (x, 1 - y), (1 - x, 1 - y)]
    return x, y, c, chips


def _gather_rider(parts, fulls):
    nt = len(parts)
    TO_X, TO_Y, FWD_X, FWD_Y, SIB_X, SIB_Y, SIB_D = range(7)

    def rows_of(ti, core):
        t, r0, r1 = parts[ti]
        h = (r1 - r0) // 2
        return r0 + core * h, h

    def copy(outs, sems, kind, ti, k_src, row0, rows, dev):
        region = _slot(outs[ti], parts[ti][0], k_src, row0, rows)
        return pltpu.make_async_remote_copy(region, region, sems[0].at[ti, kind], sems[1].at[ti, kind],
                                            device_id=dev, device_id_type=MESH)

    def plan(outs, sems):
        x, y, c, _ = _mesh_position()
        me, kx, ky, kd = 2 * x + y, 2 * (1 - x) + y, 2 * x + (1 - y), 2 * (1 - x) + (1 - y)
        dev_x, dev_y, dev_d, sib = (1 - x, y, c), (x, 1 - y, c), (1 - x, 1 - y, c), (x, y, 1 - c)

        def whole(ti):
            return 0, parts[ti][0].R

        def mk(kind, k_send, k_recv, dev, send_rows, recv_rows):
            def build(ti, side):
                k_src = k_send if side == "s" else k_recv
                row0, rows = (send_rows if side == "s" else recv_rows)(ti)
                return copy(outs, sems, kind, ti, k_src, row0, rows, dev)
            return build

        def first_half(core):
            return lambda ti: (rows_of(ti, core)[0], rows_of(ti, core)[1] // 2)

        def second_half(core):
            return lambda ti: (rows_of(ti, core)[0] + rows_of(ti, core)[1] // 2, rows_of(ti, core)[1] // 2)

        mine = lambda ti: rows_of(ti, c)
        theirs = lambda ti: rows_of(ti, 1 - c)
        split = {
            TO_X: mk(TO_X, me, kx, dev_x, mine, mine),
            TO_Y: mk(TO_Y, me, ky, dev_y, mine, mine),
            FWD_X: mk(FWD_X, ky, kd, dev_x, first_half(c), first_half(c)),
            FWD_Y: mk(FWD_Y, kx, kd, dev_y, second_half(c), second_half(c)),
            SIB_X: mk(SIB_X, kx, kx, sib, mine, theirs),
            SIB_Y: mk(SIB_Y, ky, ky, sib, mine, theirs),
            SIB_D: mk(SIB_D, kd, kd, sib, mine, theirs),
        }
        direct = {
            TO_X: mk(TO_X, me, kx, dev_x, whole, whole),
            TO_Y: mk(TO_Y, me, ky, dev_y, whole, whole),
            FWD_X: mk(FWD_X, me, kd, dev_d, whole, whole),
        }
        return split, direct

    is_split = [t.R > 1 for t, _, _ in parts]

    def start(ins, outs, sems):
        split, direct = plan(outs, sems)
        for ti in range(nt):
            kinds = split if is_split[ti] else direct
            kinds[TO_X](ti, "s").start()
            kinds[TO_Y](ti, "s").start()
            if not is_split[ti]:
                kinds[FWD_X](ti, "s").start()

    def mid(ins, outs, sems):
        split, _ = plan(outs, sems)
        for ti in range(nt):
            if is_split[ti]:
                split[TO_Y](ti, "r").wait_recv()
                split[FWD_X](ti, "s").start()
                split[SIB_Y](ti, "s").start()
        for ti in range(nt):
            if is_split[ti]:
                split[TO_X](ti, "r").wait_recv()
                split[FWD_Y](ti, "s").start()
                split[SIB_X](ti, "s").start()

    def finish(ins, outs, sems):
        split, direct = plan(outs, sems)
        for ti in range(nt):
            if is_split[ti]:
                split[FWD_X](ti, "r").wait_recv()
                split[FWD_Y](ti, "r").wait_recv()
                split[SIB_D](ti, "s").start()
            else:
                for kind in (TO_X, TO_Y, FWD_X):
                    direct[kind](ti, "r").wait_recv()
        for ti in range(nt):
            if is_split[ti]:
                for kind in (SIB_X, SIB_Y, SIB_D):
                    split[kind](ti, "r").wait_recv()
        for ti in range(nt):
            kinds = split if is_split[ti] else direct
            for kind in kinds:
                kinds[kind](ti, "s").wait_send()

    sems = pltpu.SemaphoreType.DMA((nt, 7))
    return Rider(list(fulls), [_sds(a.shape, a.dtype) for a in fulls], {i: i for i in range(nt)},
                 [sems, sems], start, mid, finish)


def _pair_exchange(name, specs, grads):
    nt = len(specs)

    def body(*refs):
        gs = refs[:nt]
        lands = refs[nt:2 * nt]
        send, recv = refs[2 * nt:]
        x, y, c, _ = _mesh_position()
        cps = []
        for ti, t in enumerate(specs):
            half = t.R // 2
            cp = pltpu.make_async_remote_copy(gs[ti].at[:, :, pl.ds((1 - c) * half, half), :], lands[ti],
                                              send.at[ti], recv.at[ti],
                                              device_id=(x, y, 1 - c), device_id_type=MESH)
            cp.start()
            cps.append(cp)
        for cp in cps:
            cp.wait()

    return pl.pallas_call(
        body, name=name,
        in_specs=[ANYSPEC] * nt, out_specs=[ANYSPEC] * nt,
        out_shape=[_sds(_full_shape(t, t.R // 2), BF) for t in specs],
        scratch_shapes=[pltpu.SemaphoreType.DMA((nt,)), pltpu.SemaphoreType.DMA((nt,))],
        compiler_params=_params(),
    )(*grads)


def _pair_sum(t, g, land, kc):
    half = t.R // 2
    nj = half // t.rb
    w = _ncb(t) * t.C

    def body(kc_ref, g_ref, l_ref, o_ref):
        del kc_ref
        o_ref[...] = (g_ref[...].astype(F32) + l_ref[...].astype(F32)).astype(BF)

    return pl.pallas_call(
        body, name=f"pair_sum_{t.name}",
        grid_spec=pltpu.PrefetchScalarGridSpec(
            num_scalar_prefetch=1, grid=(t.L, nj),
            in_specs=[pl.BlockSpec((None, t.A, t.rb, w), lambda l, j, kc_ref: (l, 0, kc_ref[1] * nj + j, 0)),
                      pl.BlockSpec((None, t.A, t.rb, w), lambda l, j, kc_ref: (l, 0, j, 0))],
            out_specs=pl.BlockSpec((None, t.A, t.rb, w), lambda l, j, kc_ref: (l, 0, j, 0))),
        out_shape=_sds(_full_shape(t, half), BF),
        compiler_params=_params(2),
    )(kc, g, land)


def _scatter_rider(specs, sums):
    nt = len(specs)

    def copy(ins, outs, sems, ti, j, chip, c):
        t = specs[ti]
        cx, cy = chip
        return pltpu.make_async_remote_copy(_slot(ins[ti], t, 2 * cx + cy, 0, t.R // 2), outs[ti].at[j],
                                            sems[0].at[ti, j], sems[1].at[ti, j],
                                            device_id=(cx, cy, c), device_id_type=MESH)

    def start(ins, outs, sems):
        _, _, c, chips = _mesh_position()
        for j, chip in enumerate(chips):
            for ti in range(nt):
                copy(ins, outs, sems, ti, j, chip, c).start()

    def finish(ins, outs, sems):
        _, _, c, chips = _mesh_position()
        for j, chip in enumerate(chips):
            for ti in range(nt):
                copy(ins, outs, sems, ti, j, chip, c).wait()

    sems = pltpu.SemaphoreType.DMA((nt, N_CHIPS - 1))
    return Rider(list(sums), [_sds((N_CHIPS - 1, t.L, t.R // 2, t.C), BF) for t in specs], {}, [sems, sems],
                 start, None, finish)


def _chip_sum(t, s, land, kc, n_layers, prev):
    half = t.R // 2
    nj = half // t.rb

    def body(*refs):
        s_ref, l_ref, o_ref = refs[1], refs[2], refs[-1]
        acc = s_ref[...].astype(F32)
        for j in range(N_CHIPS - 1):
            acc = acc + l_ref[j].astype(F32)
        o_ref[...] = acc

    def own_map(l, j, kc_ref):
        a, cb = _slot_index(t, kc_ref[0])
        return (l, a, j, cb)

    def out_map(l, j, kc_ref):
        return (l if t.layer is None else t.layer, kc_ref[1] * nj + j, 0)

    in_specs = [pl.BlockSpec((None, None, t.rb, t.C), own_map),
                pl.BlockSpec((N_CHIPS - 1, None, t.rb, t.C), lambda l, j, kc_ref: (0, l, j, 0))]
    args = [kc, s, land]
    aliases = {}
    if prev is not None:
        in_specs.append(ANYSPEC)
        args.append(prev)
        aliases = {3: 0}
    return pl.pallas_call(
        body, name=f"chip_sum_{t.name}",
        grid_spec=pltpu.PrefetchScalarGridSpec(
            num_scalar_prefetch=1, grid=(t.L, nj), in_specs=in_specs,
            out_specs=pl.BlockSpec((None, t.rb, t.C), out_map)),
        out_shape=_sds((n_layers, t.R, t.C)),
        input_output_aliases=aliases,
        compiler_params=_params(2),
    )(*args)


def _pair_share(halves):
    nt = len(halves)

    def body(*refs):
        outs = refs[nt:2 * nt]
        send, recv = refs[2 * nt:]
        x, y, c, _ = _mesh_position()
        cps = []
        for ti in range(nt):
            half = halves[ti].shape[1] // 2
            mine = outs[ti].at[:, pl.ds(c * half, half), :]
            cp = pltpu.make_async_remote_copy(mine, mine, send.at[ti], recv.at[ti],
                                              device_id=(x, y, 1 - c), device_id_type=MESH)
            cp.start()
            cps.append(cp)
        for ti in range(nt):
            half = halves[ti].shape[1] // 2
            theirs = outs[ti].at[:, pl.ds((1 - c) * half, half), :]
            pltpu.make_async_remote_copy(theirs, theirs, send.at[ti], recv.at[ti],
                                         device_id=(x, y, 1 - c), device_id_type=MESH).wait_recv()
        for cp in cps:
            cp.wait_send()

    return pl.pallas_call(
        body, name="grads_pair_share",
        in_specs=[ANYSPEC] * nt, out_specs=[ANYSPEC] * nt,
        out_shape=[_sds(a.shape, a.dtype) for a in halves],
        scratch_shapes=[pltpu.SemaphoreType.DMA((nt,)), pltpu.SemaphoreType.DMA((nt,))],
        input_output_aliases={i: i for i in range(nt)},
        compiler_params=_params(),
    )(*halves)


def _adamw_math(w, g, m, v):
    m = B1 * m + (1.0 - B1) * g
    v = B2 * v + (1.0 - B2) * (g * g)
    delta = -LR * ((m / BC1) / (jnp.sqrt(v / BC2) + AEPS) + WD * w)
    return delta, m, v


def _adamw(name, rb, w, g, m, v):
    n_layers, r, c = w.shape

    def body(w_ref, g_ref, m_ref, v_ref, go_ref, d_ref, nm_ref, nv_ref):
        g_v = g_ref[...]
        go_ref[...] = g_v
        d_ref[...], nm_ref[...], nv_ref[...] = _adamw_math(w_ref[...], g_v, m_ref[...], v_ref[...])

    spec = pl.BlockSpec((None, rb, c), lambda l, j: (l, j, 0))
    return pl.pallas_call(
        body, name=f"adamw_{name}", grid=(n_layers, r // rb),
        in_specs=[spec] * 4, out_specs=[spec] * 4, out_shape=[_sds(w.shape)] * 4,
        compiler_params=_params(2),
    )(w, g, m, v)


GAIN_ROWS = {"pre_mix_g": 0, "post_mix_g": 2, "pre_ffn_g": 4, "post_ffn_g": 6, "ple_g": 8, "ple_post_g": 10}
ROW_KV_G, ROW_POOL_SCALE, ROW_SINKS, ROW_LOSS, PACK_ROWS = 12, 13, 14, 15, 16
SMALL_NAMES = tuple(GAIN_ROWS) + ("kv_g", "pool_scale", "sinks")


def _small_all_reduce(rows):
    def body(*refs):
        row_refs = refs[:PACK_ROWS]
        tot_ref, pack, land, send, recv = refs[PACK_ROWS:]
        x, y, c, _ = _mesh_position()
        me = 4 * x + 2 * y + c
        for r in range(PACK_ROWS):
            pack[r:r + 1, :] = row_refs[r][...]
        cps = []
        for j in range(1, N_DEV):
            peer = (x ^ (j >> 2), y ^ ((j >> 1) & 1), c ^ (j & 1))
            cp = pltpu.make_async_remote_copy(pack, land.at[me], send.at[j], recv.at[j],
                                              device_id=peer, device_id_type=MESH)
            cp.start()
            cps.append(cp)
        land[me] = pack[...]
        for j in range(1, N_DEV):
            pltpu.make_async_remote_copy(pack, land.at[me ^ j], send.at[j], recv.at[j],
                                         device_id=(x, y, c), device_id_type=MESH).wait_recv()
        for cp in cps:
            cp.wait_send()
        tot = land[0]
        for d in range(1, N_DEV):
            tot = tot + land[d]
        tot_ref[...] = tot

    return pl.pallas_call(
        body, name="small_all_reduce",
        in_specs=[VSPEC] * PACK_ROWS, out_specs=VSPEC, out_shape=_sds((PACK_ROWS, D)),
        scratch_shapes=[pltpu.VMEM((PACK_ROWS, D), F32), pltpu.VMEM((N_DEV, PACK_ROWS, D), F32),
                        pltpu.SemaphoreType.DMA((N_DEV,)), pltpu.SemaphoreType.DMA((N_DEV,))],
        compiler_params=_params(),
    )(*rows)


def _small_adamw(tot, kc, small_w, small_m, small_v):
    names = SMALL_NAMES
    n = len(names)

    def body(*refs):
        tot_ref, kc_ref = refs[0], refs[1]
        w_refs = dict(zip(names, refs[2:2 + n]))
        m_refs = dict(zip(names, refs[2 + n:2 + 2 * n]))
        v_refs = dict(zip(names, refs[2 + 2 * n:2 + 3 * n]))
        loss_ref = refs[2 + 3 * n]
        out_refs = {nm: refs[3 + 3 * n + 4 * k: 7 + 3 * n + 4 * k] for k, nm in enumerate(names)}
        tot = tot_ref[...]
        loss_ref[...] = 0.5 * jnp.sum(tot[ROW_LOSS:ROW_LOSS + 1, :], axis=-1, keepdims=True) * (1.0 / D)

        def update(nm, g):
            g_ref, d_ref, nm_ref, nv_ref = out_refs[nm]
            g_ref[...] = g
            d_ref[...], nm_ref[...], nv_ref[...] = _adamw_math(w_refs[nm][...], g, m_refs[nm][...], v_refs[nm][...])

        for nm, r in GAIN_ROWS.items():
            update(nm, tot[r:r + 2, :])
        update("kv_g", tot[ROW_KV_G:ROW_KV_G + 1, :])
        k = kc_ref[0]
        width = D // N_CHIPS
        g_scale = jnp.zeros((1, width), F32)
        for kk in range(N_CHIPS):
            g_scale = g_scale + jnp.where(k == kk, tot[ROW_POOL_SCALE:ROW_POOL_SCALE + 1, kk * width:(kk + 1) * width], 0.0)
        update("pool_scale", g_scale)
        update("sinks", tot[ROW_SINKS:ROW_SINKS + 1, 0:N_HEADS])

    ins = [tot, kc] + [small_w[nm] for nm in names] + [small_m[nm] for nm in names] + [small_v[nm] for nm in names]
    out_shape = [_sds((1, 1))]
    for nm in names:
        out_shape += [_sds(small_w[nm].shape)] * 4
    outs = pl.pallas_call(
        body, name="small_adamw",
        in_specs=[VSPEC, SSPEC] + [VSPEC] * (3 * n), out_specs=[VSPEC] * len(out_shape), out_shape=out_shape,
        compiler_params=_params(),
    )(*ins)
    return outs[0], {nm: outs[1 + 4 * k: 5 + 4 * k] for k, nm in enumerate(names)}


def _compute_layout(t, full):
    if t.src == "w_gu":
        return full.reshape(2, D, FF)
    if t.src == "pool_w":
        return full.reshape(len(WINDOWS), POOL_G, POOL_G)
    if t.src == "pool_scale":
        return full.reshape(1, D)
    return full.reshape(t.A * t.R, _ncb(t) * t.C)


def kernel(x, p, pre_mix_g, post_mix_g, pre_ffn_g, post_ffn_g, pool_w, pool_scale, kv_g, w_kv, w_q, sinks, w_o, w_gu, w_down, ple_g, w_ple_gate, w_ple_proj, ple_post_g, loss_target, m_pre_mix_g, m_post_mix_g, m_pre_ffn_g, m_post_ffn_g, m_pool_w, m_pool_scale, m_kv_g, m_w_kv, m_w_q, m_sinks, m_w_o, m_w_gu, m_w_down, m_ple_g, m_w_ple_gate, m_w_ple_proj, m_ple_post_g, v_pre_mix_g, v_post_mix_g, v_pre_ffn_g, v_post_ffn_g, v_pool_w, v_pool_scale, v_kv_g, v_w_kv, v_w_q, v_sinks, v_w_o, v_w_gu, v_w_down, v_ple_g, v_w_ple_gate, v_w_ple_proj, v_ple_post_g):
    weights = dict(pre_mix_g=pre_mix_g, post_mix_g=post_mix_g, pre_ffn_g=pre_ffn_g, post_ffn_g=post_ffn_g,
                   pool_w=pool_w, pool_scale=pool_scale, kv_g=kv_g, w_kv=w_kv, w_q=w_q, sinks=sinks, w_o=w_o,
                   w_gu=w_gu, w_down=w_down, ple_g=ple_g, w_ple_gate=w_ple_gate, w_ple_proj=w_ple_proj,
                   ple_post_g=ple_post_g)
    m_in = dict(pre_mix_g=m_pre_mix_g, post_mix_g=m_post_mix_g, pre_ffn_g=m_pre_ffn_g, post_ffn_g=m_post_ffn_g,
                pool_w=m_pool_w, pool_scale=m_pool_scale, kv_g=m_kv_g, w_kv=m_w_kv, w_q=m_w_q, sinks=m_sinks,
                w_o=m_w_o, w_gu=m_w_gu, w_down=m_w_down, ple_g=m_ple_g, w_ple_gate=m_w_ple_gate,
                w_ple_proj=m_w_ple_proj, ple_post_g=m_ple_post_g)
    v_in = dict(pre_mix_g=v_pre_mix_g, post_mix_g=v_post_mix_g, pre_ffn_g=v_pre_ffn_g, post_ffn_g=v_post_ffn_g,
                pool_w=v_pool_w, pool_scale=v_pool_scale, kv_g=v_kv_g, w_kv=v_w_kv, w_q=v_w_q, sinks=v_sinks,
                w_o=v_w_o, w_gu=v_w_gu, w_down=v_w_down, ple_g=v_ple_g, w_ple_gate=v_w_ple_gate,
                w_ple_proj=v_w_ple_proj, ple_post_g=v_ple_post_g)
    order = ["pre_mix_g", "post_mix_g", "pre_ffn_g", "post_ffn_g", "pool_w", "pool_scale", "kv_g", "w_kv", "w_q",
             "sinks", "w_o", "w_gu", "w_down", "ple_g", "w_ple_gate", "w_ple_proj", "ple_post_g"]

    kc = jnp.stack([2 * lax.axis_index("x") + lax.axis_index("y"), lax.axis_index("c")]).astype(jnp.int32)
    s_len = x.shape[1]
    x2d = x.reshape(s_len, D)
    p3d = p.reshape(2, s_len, PLE)
    target = loss_target.reshape(s_len, D)
    kv_g2d = kv_g.reshape(1, D)
    gains = {nm: weights[nm] for nm in GAIN_ROWS}

    def shard_view(src, a):
        t = next(t for t in BIGS.values() if t.src == src)
        return a.reshape(-1, t.R, t.C)

    placed = {nm: _place(t, shard_view(t.src, weights[t.src]), kc, BF) for nm, t in BIGS.items()}
    placed["pool_scale"] = _place(POOL_SCALE, pool_scale.reshape(1, 1, D // N_CHIPS), kc, F32)
    specs = dict(BIGS, pool_scale=POOL_SCALE)

    def gather(names, rows=None):
        rows = rows or {}
        parts = [(specs[nm],) + tuple(rows.get(nm, (0, specs[nm].R))) for nm in names]
        return _gather_rider(parts, [placed[nm] for nm in names])

    def take(names, results):
        for nm, a in zip(names, results):
            placed[nm] = a

    def weight(nm):
        return _compute_layout(specs[nm], placed[nm])

    first = ["pool_w", "pool_scale", "w_gu0", "w_down0"]
    take(first, _run("weights_gather_first", gather(first)))

    y0, x1 = _mixa_fwd(x2d, gains["pre_mix_g"], weight("pool_w"), weight("pool_scale"), gains["post_mix_g"])

    ride = ["w_ple_gate0", "w_ple_proj0", "w_q", "w_kv", "w_o", "w_gu1"]
    (f0, x2), got = _ffn_fwd(0, x1, gains["pre_ffn_g"], weight("w_gu0"), weight("w_down0"), gains["post_ffn_g"],
                             rider=gather(ride, {"w_gu1": (0, 320)}))
    take(ride, got)

    ride = ["w_ple_gate1", "w_ple_proj1", "w_gu1"]
    (z0, pe0, x3), got = _ple_fwd(0, x2, p3d, gains["ple_g"], weight("w_ple_gate0"), weight("w_ple_proj0"),
                                  gains["ple_post_g"], rider=gather(ride, {"w_gu1": (320, 448)}))
    take(ride, got)

    ride = ["w_gu1"]
    (q, kv), got = _qkv_fwd(x3, gains["pre_mix_g"], kv_g2d, weight("w_q"), weight("w_kv"),
                            rider=gather(ride, {"w_gu1": (448, 704)}))
    take(ride, got)

    ride = ["w_down1", "w_gu1"]
    (attn, y1, x4), got = _attn_fwd(q, kv, sinks, x3, weight("w_o"), gains["post_mix_g"],
                                    rider=gather(ride, {"w_gu1": (704, D)}))
    take(ride, got)

    (f1, x5), _ = _ffn_fwd(1, x4, gains["pre_ffn_g"], weight("w_gu1"), weight("w_down1"), gains["post_ffn_g"])
    (z1, pe1, dx6, loss_row), _ = _ple_fwd(1, x5, p3d, gains["ple_g"], weight("w_ple_gate1"), weight("w_ple_proj1"),
                                           gains["ple_post_g"], target=target)

    local = {}
    landed = {}

    def pair_stage(tag, names):
        ts = [BIGS[nm] for nm in names]
        gs = [local[nm].reshape(_full_shape(t)) for nm, t in zip(names, ts)]
        lands = _pair_exchange(f"grads_pair_exchange_{tag}", ts, gs)
        return [_pair_sum(t, g, l, kc) for t, g, l in zip(ts, gs, lands)]

    def scatter(names, sums):
        return _scatter_rider([BIGS[nm] for nm in names], sums)

    dx5, local["w_ple_gate1"], local["w_ple_proj1"], d_ple1, d_plepost1 = _ple_bwd(
        1, dx6, x5, z1, pe1, p3d, gains["ple_g"], weight("w_ple_gate1"), gains["ple_post_g"])
    (dx4, local["w_gu1"], local["w_down1"], d_preffn1, d_postffn1), _ = _ffn_bwd(
        1, dx5, x4, f1, gains["pre_ffn_g"], weight("w_gu1"), weight("w_down1"), gains["post_ffn_g"])

    group_a = ["w_ple_gate1", "w_ple_proj1", "w_gu1"]
    sums_a = pair_stage("a", group_a)
    (dq, dkv, local["w_o"], d_postmix1, d_sinks), got = _attn_bwd(
        dx4, y1, attn, q, kv, sinks, weight("w_o"), gains["post_mix_g"], rider=scatter(group_a, sums_a))
    for nm, s, l in zip(group_a, sums_a, got):
        landed[nm] = (s, l)

    dx3, local["w_q"], local["w_kv"], d_premix1, d_kvg = _qkv_bwd(
        dq, dkv, x3, dx4, gains["pre_mix_g"], kv_g2d, weight("w_q"), weight("w_kv"))
    dx2, local["w_ple_gate0"], local["w_ple_proj0"], d_ple0, d_plepost0 = _ple_bwd(
        0, dx3, x2, z0, pe0, p3d, gains["ple_g"], weight("w_ple_gate0"), gains["ple_post_g"])

    group_b = ["w_down1", "w_o", "w_q", "w_kv", "w_ple_gate0", "w_ple_proj0"]
    sums_b = pair_stage("b", group_b)
    (dx1, local["w_gu0"], local["w_down0"], d_preffn0, d_postffn0), got = _ffn_bwd(
        0, dx2, x1, f0, gains["pre_ffn_g"], weight("w_gu0"), weight("w_down0"), gains["post_ffn_g"],
        rider=scatter(group_b, sums_b))
    for nm, s, l in zip(group_b, sums_b, got):
        landed[nm] = (s, l)

    dx0, local["pool_w"], d_scale, d_postmix0, d_premix0 = _mixa_bwd(
        dx1, x2d, y0, gains["pre_mix_g"], weight("pool_w"), weight("pool_scale"), gains["post_mix_g"])

    group_c = ["w_gu0", "w_down0", "pool_w"]
    sums_c = pair_stage("c", group_c)
    got = _run("grads_chip_exchange_c", scatter(group_c, sums_c))
    for nm, s, l in zip(group_c, sums_c, got):
        landed[nm] = (s, l)

    rows = [d_premix0, d_premix1, d_postmix0, d_postmix1, d_preffn0, d_preffn1, d_postffn0, d_postffn1,
            d_ple0, d_ple1, d_plepost0, d_plepost1, d_kvg, d_scale, d_sinks, loss_row]
    as2d = lambda a: a.reshape(1, D) if a.ndim == 1 else a
    tot = _small_all_reduce(rows)
    loss, small = _small_adamw(tot, kc, {nm: as2d(weights[nm]) for nm in SMALL_NAMES},
                               {nm: as2d(m_in[nm]) for nm in SMALL_NAMES},
                               {nm: as2d(v_in[nm]) for nm in SMALL_NAMES})

    halves = []
    for src in BIG_SOURCES:
        ts = [t for t in BIGS.values() if t.src == src]
        n_layers = shard_view(src, weights[src]).shape[0]
        acc = None
        for t in ts:
            s, l = landed[t.name]
            acc = _chip_sum(t, s, l, kc, n_layers, acc)
        halves.append(acc)
    full_grads = _pair_share(halves)

    out = {"grad": {}, "delta": {}, "new_m": {}, "new_v": {}}
    for src, g in zip(BIG_SOURCES, full_grads):
        t = next(t for t in BIGS.values() if t.src == src)
        res = _adamw(src, t.rb, shard_view(src, weights[src]), g, shard_view(src, m_in[src]), shard_view(src, v_in[src]))
        shape = weights[src].shape
        for kind, a in zip(("grad", "delta", "new_m", "new_v"), res):
            out[kind][src] = a.reshape(shape)
    for nm in SMALL_NAMES:
        shape = weights[nm].shape
        for kind, a in zip(("grad", "delta", "new_m", "new_v"), small[nm]):
            out[kind][nm] = a.reshape(shape)

    return (loss.reshape(()), dx0.reshape(x.shape),
            *[out["grad"][nm] for nm in order], *[out["delta"][nm] for nm in order],
            *[out["new_m"][nm] for nm in order], *[out["new_v"][nm] for nm in order])
```

```python
import collections

import jax
import jax.numpy as jnp
from jax import lax
from jax.experimental import pallas as pl
from jax.experimental.pallas import tpu as pltpu

D = 1024
FF = 2816
N_HEADS = 16
HEAD_DIM = 64
N_KV_HEADS = 4
GQA = N_HEADS // N_KV_HEADS
KVD = N_KV_HEADS * HEAD_DIM
PLE = 256
BLK = 128
WINDOWS = (2, 4, 8, 16)
POOL_G = 256
HALO = 16
EPS = 1e-6
NEG_INF = -1e30
ATT_SCALE = HEAD_DIM ** -0.5
SLOPES = tuple(2.0 ** (-8.0 * (h + 1) / N_HEADS) for h in range(N_HEADS))
N_CHIPS = 4
N_DEV = 8

LR, B1, B2, AEPS, WD, STEP = 0.001, 0.9, 0.999, 1e-08, 0.01, 10
BC1 = 1.0 - B1 ** STEP
BC2 = 1.0 - B2 ** STEP

BF = jnp.bfloat16
F32 = jnp.float32
MESH = pl.DeviceIdType.MESH
VMEM_LIMIT_V7X = 58 * 1024 * 1024
TM = 256
TM_FFN_BWD = 512
FF_CHUNK = 256
FF_HALF = FF // 2

VSPEC = pl.BlockSpec(memory_space=pltpu.VMEM)
SSPEC = pl.BlockSpec(memory_space=pltpu.SMEM)
ANYSPEC = pl.BlockSpec(memory_space=pl.ANY)


def _params(n_grid=0):
    sem = ("arbitrary",) * n_grid if n_grid else None
    return pltpu.CompilerParams(dimension_semantics=sem, vmem_limit_bytes=VMEM_LIMIT_V7X)


def _sds(shape, dtype=F32):
    return jax.ShapeDtypeStruct(tuple(shape), dtype)


Rider = collections.namedtuple("Rider", "arrays out_shapes aliases scratch start mid finish")
MID_NUM, MID_DEN = 5, 8


def _call(body, *, name, grid, in_specs, out_specs, out_shape, args, scratch_shapes=(), rider=None):
    ni, no, ns = len(in_specs), len(out_specs), len(scratch_shapes)
    if rider is None:
        outs = pl.pallas_call(body, name=name, grid=grid, in_specs=in_specs, out_specs=out_specs,
                              out_shape=out_shape, scratch_shapes=list(scratch_shapes),
                              compiler_params=_params(len(grid)))(*args)
        return list(outs), []
    ri, ro = len(rider.arrays), len(rider.out_shapes)

    def full(*refs):
        ins, refs = refs[:ni], refs[ni:]
        rins, refs = refs[:ri], refs[ri:]
        outs, refs = refs[:no], refs[no:]
        routs, refs = refs[:ro], refs[ro:]
        scr, rscr = refs[:ns], refs[ns:]
        ids = [pl.program_id(a) for a in range(len(grid))]
        first = ids[0] == 0
        last = ids[0] == grid[0] - 1
        for a in range(1, len(grid)):
            first = first & (ids[a] == 0)
            last = last & (ids[a] == grid[a] - 1)

        @pl.when(first)
        def _():
            rider.start(rins, routs, rscr)

        if rider.mid is not None:
            assert len(grid) == 1

            @pl.when(ids[0] == (grid[0] * MID_NUM) // MID_DEN)
            def _():
                rider.mid(rins, routs, rscr)

        body(*ins, *outs, *scr)

        @pl.when(last)
        def _():
            rider.finish(rins, routs, rscr)

    outs = pl.pallas_call(
        full, name=name, grid=grid,
        in_specs=list(in_specs) + [ANYSPEC] * ri, out_specs=list(out_specs) + [ANYSPEC] * ro,
        out_shape=list(out_shape) + list(rider.out_shapes),
        scratch_shapes=list(scratch_shapes) + list(rider.scratch),
        input_output_aliases={ni + a: no + b for a, b in rider.aliases.items()},
        compiler_params=_params(len(grid)))(*args, *rider.arrays)
    return list(outs[:no]), list(outs[no:])


def _run(name, rider):
    ri = len(rider.arrays)

    def body(*refs):
        rins, routs, rscr = refs[:ri], refs[ri:ri + len(rider.out_shapes)], refs[ri + len(rider.out_shapes):]
        rider.start(rins, routs, rscr)
        if rider.mid is not None:
            rider.mid(rins, routs, rscr)
        rider.finish(rins, routs, rscr)

    return pl.pallas_call(
        body, name=name, in_specs=[ANYSPEC] * ri, out_specs=[ANYSPEC] * len(rider.out_shapes),
        out_shape=list(rider.out_shapes), scratch_shapes=list(rider.scratch),
        input_output_aliases=dict(rider.aliases), compiler_params=_params())(*rider.arrays)


def _rms_fwd(x, g):
    r = lax.rsqrt(jnp.mean(x * x, axis=-1, keepdims=True) + EPS)
    return x * r * g


def _rms_bwd(x, g, dy):
    r = lax.rsqrt(jnp.mean(x * x, axis=-1, keepdims=True) + EPS)
    xn = x * r
    dxn = dy * g
    dx = r * (dxn - xn * jnp.mean(dxn * xn, axis=-1, keepdims=True))
    return dx, dy * xn


def _rowsum(a):
    return jnp.sum(a, axis=0, keepdims=True)


def _sigmoid(z):
    return 1.0 / (1.0 + jnp.exp(-z))


def _dot(a, b):
    return jnp.dot(a, b, preferred_element_type=F32)


def _dot_nt(a, b):
    return lax.dot_general(a, b, (((1,), (1,)), ((), ())), preferred_element_type=F32)


def _dot_tn(a, b):
    return lax.dot_general(a, b, (((0,), (0,)), ((), ())), preferred_element_type=F32)


def _row_spec(tm, width=D):
    return pl.BlockSpec((tm, width), lambda i: (i, 0))


def _const_spec(shape):
    zeros = (0,) * len(shape)
    return pl.BlockSpec(tuple(shape), lambda *_: zeros)


def _pool_delta(he, pos):
    out = []
    for gi, w in enumerate(WINDOWS):
        hg = he[:, gi * POOL_G:(gi + 1) * POOL_G]
        s = hg
        k = 1
        while k < w:
            s = s + pltpu.roll(s, k, 0)
            k *= 2
        cnt = jnp.maximum(jnp.minimum(pos + 1, w), 1).astype(F32)
        out.append(s / cnt - hg)
    return out


def _load_with_halo_before(x_ref, i, tm):
    r0 = pl.multiple_of(i * tm, tm)
    hs = pl.multiple_of(jnp.maximum(i * tm - HALO, 0), 8)
    xh = jnp.where(i > 0, x_ref[pl.ds(hs, HALO), :], 0.0)
    xt = x_ref[pl.ds(r0, tm), :]
    return xt, jnp.concatenate([xh, xt], axis=0)


def _mixa_fwd(x, pre_g, pool_w, pool_scale, post_g):
    s_len = x.shape[0]
    n = s_len // TM

    def body(x_ref, pg_ref, w_ref, sc_ref, qg_ref, y_ref, x1_ref):
        i = pl.program_id(0)
        xt, xe = _load_with_halo_before(x_ref, i, TM)
        he = _rms_fwd(xe, pg_ref[0:1, :])
        pos = i * TM - HALO + lax.broadcasted_iota(jnp.int32, (TM + HALO, 1), 0)
        ds = _pool_delta(he, pos)
        ys = [_dot(ds[gi][HALO:, :].astype(BF), w_ref[gi]) for gi in range(len(WINDOWS))]
        y = jnp.concatenate(ys, axis=1) * sc_ref[...]
        y_ref[...] = y
        x1_ref[...] = xt + _rms_fwd(y, qg_ref[0:1, :])

    outs, _ = _call(body, name="mixa_fwd", grid=(n,),
                    in_specs=[VSPEC] * 5, out_specs=[_row_spec(TM), _row_spec(TM)],
                    out_shape=[_sds((s_len, D)), _sds((s_len, D))],
                    args=[x, pre_g, pool_w, pool_scale, post_g])
    return outs


def _mixa_bwd(dx1, x, y, pre_g, pool_w, pool_scale, post_g):
    s_len = x.shape[0]
    n = s_len // TM
    ng = len(WINDOWS)

    def body(dx_ref, x_ref, y_ref, pg_ref, w_ref, sc_ref, qg_ref,
             dx0_ref, dw_ref, dsc_ref, dqg_ref, dpg_ref, wacc):
        i = pl.program_id(0)

        @pl.when(i == 0)
        def _():
            wacc[...] = jnp.zeros_like(wacc)
            dsc_ref[...] = jnp.zeros_like(dsc_ref)
            dqg_ref[...] = jnp.zeros_like(dqg_ref)
            dpg_ref[...] = jnp.zeros_like(dpg_ref)

        r0 = pl.multiple_of(i * TM, TM)
        xt, xe = _load_with_halo_before(x_ref, i, TM)
        he = _rms_fwd(xe, pg_ref[0:1, :])
        pos_b = i * TM - HALO + lax.broadcasted_iota(jnp.int32, (TM + HALO, 1), 0)
        ds = _pool_delta(he, pos_b)

        last = i == n - 1
        a0 = pl.multiple_of(jnp.minimum(i * TM + TM, s_len - HALO), 8)
        ye = jnp.concatenate([y_ref[pl.ds(r0, TM), :], y_ref[pl.ds(a0, HALO), :]], axis=0)
        dt = dx_ref[pl.ds(r0, TM), :]
        de = jnp.concatenate([dt, jnp.where(last, 0.0, dx_ref[pl.ds(a0, HALO), :])], axis=0)
        dye, prod = _rms_bwd(ye, qg_ref[0:1, :], de)
        dqg_ref[...] += _rowsum(prod[:TM, :])
        dys = dye * sc_ref[...]
        pos_a = i * TM + lax.broadcasted_iota(jnp.int32, (TM + HALO, 1), 0)

        dhs, dscs = [], []
        for gi, w in enumerate(WINDOWS):
            sl = slice(gi * POOL_G, (gi + 1) * POOL_G)
            wg = w_ref[gi]
            dys_g = dys[:, sl].astype(BF)
            d_g = ds[gi][HALO:, :].astype(BF)
            ypre = _dot(d_g, wg)
            dscs.append(_rowsum(dye[:TM, sl] * ypre))
            wacc[gi] += _dot_tn(d_g, dys_g[:TM, :])
            dd = _dot_nt(dys_g, wg)
            cnt = jnp.minimum(pos_a + 1, w).astype(F32)
            a = dd / cnt
            k = 1
            while k < w:
                a = a + pltpu.roll(a, TM + HALO - k, 0)
                k *= 2
            dhs.append(a[:TM, :] - dd[:TM, :])
        dsc_ref[...] += jnp.concatenate(dscs, axis=1)
        dh = jnp.concatenate(dhs, axis=1)
        dxp, prod2 = _rms_bwd(xt, pg_ref[0:1, :], dh)
        dpg_ref[...] += _rowsum(prod2)
        dx0_ref[...] = dt + dxp

        @pl.when(last)
        def _():
            dw_ref[...] = wacc[...].astype(BF)

    outs, _ = _call(
        body, name="mixa_bwd", grid=(n,), in_specs=[VSPEC] * 7,
        out_specs=[_row_spec(TM), _const_spec((ng, POOL_G, POOL_G)), _const_spec((1, D)),
                   _const_spec((1, D)), _const_spec((1, D))],
        out_shape=[_sds((s_len, D)), _sds((ng, POOL_G, POOL_G), BF), _sds((1, D)), _sds((1, D)), _sds((1, D))],
        scratch_shapes=[pltpu.VMEM((ng, POOL_G, POOL_G), F32)],
        args=[dx1, x, y, pre_g, pool_w, pool_scale, post_g])
    return outs


def _ffn_fwd(layer, x1, pre_g, wgu, wd, post_g, rider=None):
    s_len = x1.shape[0]

    def body(x_ref, pg_ref, wgu_ref, wd_ref, qg_ref, f_ref, x2_ref):
        x = x_ref[...]
        h = _rms_fwd(x, pg_ref[layer:layer + 1, :]).astype(BF)
        f = jnp.zeros((TM, D), F32)
        for c in range(FF // FF_HALF):
            cols = slice(c * FF_HALF, (c + 1) * FF_HALF)
            g = _dot(h, wgu_ref[0, :, cols])
            u = _dot(h, wgu_ref[1, :, cols])
            act = g * _sigmoid(g) * u
            f = f + _dot(act.astype(BF), wd_ref[cols, :])
        f_ref[...] = f
        x2_ref[...] = x + _rms_fwd(f, qg_ref[layer:layer + 1, :])

    return _call(body, name=f"ffn_fwd{layer}", grid=(s_len // TM,),
                 in_specs=[_row_spec(TM), VSPEC, VSPEC, VSPEC, VSPEC],
                 out_specs=[_row_spec(TM), _row_spec(TM)],
                 out_shape=[_sds((s_len, D)), _sds((s_len, D))],
                 args=[x1, pre_g, wgu, wd, post_g], rider=rider)


GU_PIECE = 128
DN_PIECE = 64
DN_SLOT = FF // N_CHIPS
HALF_D = D // 2


def _ffn_bwd(layer, dx2, x1, f, pre_g, wgu, wd, post_g, rider=None):
    s_len = x1.shape[0]
    tm = TM_FFN_BWD
    n = s_len // tm
    nc = FF // FF_CHUNK
    n_gu, n_dn = FF_CHUNK // GU_PIECE, FF_CHUNK // DN_PIECE
    n_pieces = 2 * n_gu + n_dn

    def edge_rows(c, i):
        return (jnp.where((c == 0) | (c == nc - 1), i, n - 1), 0)

    def exchange(c, accg, accu, accd, own_gu_ref, land_gu_ref, own_dn_ref, land_dn_ref,
                 pl_gu, pl_dn, sib_gu, sib_dn, mine_gu, mine_dn, sum_gu, sum_dn,
                 psend, precv, ssend, lsem, rrecv):
        x, y, core = lax.axis_index("x"), lax.axis_index("y"), lax.axis_index("c")
        lower = core == 0

        def pair_copy(cc, part):
            p = cc % 2
            src, dst = ((sib_gu, pl_gu), (sib_dn, pl_dn))[part]
            return pltpu.make_async_remote_copy(src.at[p], dst.at[cc], psend.at[p, part], precv.at[cc, part],
                                                device_id=(x, y, 1 - core), device_id_type=MESH)

        def scatter(cc, wait):
            p = cc % 2
            jobs = []
            for gu in range(2):
                for hc in range(n_gu):
                    hidden = cc * FF_CHUNK + hc * GU_PIECE
                    k = hidden // FF_HALF
                    off = pl.multiple_of(hidden - k * FF_HALF, GU_PIECE)
                    jobs.append((sum_gu.at[p, gu, :, pl.ds(hc * GU_PIECE, GU_PIECE)], k + 2 * gu, 0,
                                 own_gu_ref, land_gu_ref, (slice(None), pl.ds(off, GU_PIECE))))
            for q in range(n_dn):
                hidden = cc * FF_CHUNK + q * DN_PIECE
                k = hidden // DN_SLOT
                off = pl.multiple_of(hidden - k * DN_SLOT, DN_PIECE)
                jobs.append((sum_dn.at[p, pl.ds(q * DN_PIECE, DN_PIECE), :], k, 1,
                             own_dn_ref, land_dn_ref, (pl.ds(off, DN_PIECE), slice(None))))
            for pi, (src, k, t, own_ref, land_ref, where) in enumerate(jobs):
                kx, ky = k // 2, k % 2
                fx, fy = (kx != x).astype(jnp.int32), (ky != y).astype(jnp.int32)
                local = (fx + fy) == 0
                j = jnp.maximum(fx + 2 * fy - 1, 0)

                @pl.when(local)
                def _():
                    if not wait:
                        cp = pltpu.make_async_copy(src, own_ref.at[where], lsem.at[p, pi])
                        cp.start()
                        cp.wait()

                @pl.when(jnp.logical_not(local))
                def _():
                    cp = pltpu.make_async_remote_copy(src, land_ref.at[(j,) + where], ssend.at[p, pi],
                                                      rrecv.at[t, j], device_id=(kx, ky, core), device_id_type=MESH)
                    if wait:
                        cp.wait_send()
                    else:
                        cp.start()

        def add_and_scatter(cc):
            p = cc % 2
            pair_copy(cc, 0).wait_recv()
            pair_copy(cc, 1).wait_recv()
            sum_gu[p] = (mine_gu[...] + pl_gu[cc].astype(F32)).astype(BF)
            sum_dn[p] = (mine_dn[...] + pl_dn[cc].astype(F32)).astype(BF)
            scatter(cc, wait=False)

        @pl.when(c >= 1)
        def _():
            @pl.when(c >= 3)
            def _():
                scatter(c - 3, wait=True)
            add_and_scatter(c - 1)

        @pl.when(c >= 2)
        def _():
            pair_copy(c - 2, 0).wait_send()
            pair_copy(c - 2, 1).wait_send()

        p = c % 2
        g_v, u_v, d_v = accg[...], accu[...], accd[...]
        sib_gu[p, 0] = jnp.where(lower, g_v[HALF_D:, :], g_v[:HALF_D, :]).astype(BF)
        sib_gu[p, 1] = jnp.where(lower, u_v[HALF_D:, :], u_v[:HALF_D, :]).astype(BF)
        sib_dn[p] = jnp.where(lower, d_v[:, HALF_D:], d_v[:, :HALF_D]).astype(BF)
        mine_gu[0] = jnp.where(lower, g_v[:HALF_D, :], g_v[HALF_D:, :])
        mine_gu[1] = jnp.where(lower, u_v[:HALF_D, :], u_v[HALF_D:, :])
        mine_dn[...] = jnp.where(lower, d_v[:, :HALF_D], d_v[:, HALF_D:])
        pair_copy(c, 0).start()
        pair_copy(c, 1).start()

        @pl.when(c == nc - 1)
        def _():
            scatter(nc - 3, wait=True)
            add_and_scatter(nc - 1)
            for cc in (nc - 2, nc - 1):
                pair_copy(cc, 0).wait_send()
                pair_copy(cc, 1).wait_send()
                scatter(cc, wait=True)
            for t, land_ref in enumerate((land_gu_ref, land_dn_ref)):
                for j in range(N_CHIPS - 1):
                    pltpu.make_async_remote_copy(land_ref.at[j], land_ref.at[j], ssend.at[0, 0], rrecv.at[t, j],
                                                 device_id=(x, y, core), device_id_type=MESH).wait_recv()

    def body(dx_ref, x_ref, f_ref, pg_ref, wgu_ref, wd_ref, qg_ref,
             dx1_ref, dpg_ref, dqg_ref, own_gu_ref, land_gu_ref, own_dn_ref, land_dn_ref,
             h_s, df_s, dh_s, accg, accu, accd, *comm):
        c = pl.program_id(0)
        i = pl.program_id(1)
        rows = pl.ds(pl.multiple_of(i * tm, tm), tm)
        pg = pg_ref[layer:layer + 1, :]

        @pl.when((c == 0) & (i == 0))
        def _():
            dpg_ref[...] = jnp.zeros_like(dpg_ref)
            dqg_ref[...] = jnp.zeros_like(dqg_ref)

        @pl.when(c == 0)
        def _():
            h_s[rows, :] = _rms_fwd(x_ref[...], pg).astype(BF)
            df, prod = _rms_bwd(f_ref[...], qg_ref[layer:layer + 1, :], dx_ref[...])
            df_s[rows, :] = df.astype(BF)
            dqg_ref[...] += _rowsum(prod)

        @pl.when(i == 0)
        def _():
            accg[...] = jnp.zeros_like(accg)
            accu[...] = jnp.zeros_like(accu)
            accd[...] = jnp.zeros_like(accd)

        h = h_s[rows, :]
        df = df_s[rows, :]
        wg = wgu_ref[0]
        wu = wgu_ref[1]
        g = _dot(h, wg)
        u = _dot(h, wu)
        sg = _sigmoid(g)
        a = g * sg
        dact = _dot_nt(df, wd_ref[...])
        accd[...] += _dot_tn((a * u).astype(BF), df)
        du = (dact * a).astype(BF)
        dg = (dact * u * (sg * (1.0 + g * (1.0 - sg)))).astype(BF)
        accg[...] += _dot_tn(h, dg)
        accu[...] += _dot_tn(h, du)
        dh = _dot_nt(dg, wg) + _dot_nt(du, wu)

        @pl.when(c == 0)
        def _():
            dh_s[rows, :] = dh

        @pl.when((c > 0) & (c < nc - 1))
        def _():
            dh_s[rows, :] += dh

        @pl.when(c == nc - 1)
        def _():
            dxp, prod = _rms_bwd(x_ref[...], pg, dh_s[rows, :] + dh)
            dpg_ref[...] += _rowsum(prod)
            dx1_ref[...] = dx_ref[...] + dxp

        @pl.when(i == n - 1)
        def _():
            exchange(c, accg, accu, accd, own_gu_ref, land_gu_ref, own_dn_ref, land_dn_ref, *comm)

    dma = pltpu.SemaphoreType.DMA
    return _call(
        body, name=f"ffn_bwd{layer}", grid=(nc, n),
        in_specs=[pl.BlockSpec((tm, D), edge_rows), pl.BlockSpec((tm, D), edge_rows),
                  pl.BlockSpec((tm, D), lambda c, i: (jnp.where(c == 0, i, n - 1), 0),
                               pipeline_mode=pl.Buffered(1)),
                  VSPEC,
                  pl.BlockSpec((2, D, FF_CHUNK), lambda c, i: (0, 0, c)),
                  pl.BlockSpec((FF_CHUNK, D), lambda c, i: (c, 0)),
                  VSPEC],
        out_specs=[pl.BlockSpec((tm, D), lambda c, i: (jnp.where(c == nc - 1, i, 0), 0)),
                   _const_spec((1, D)), _const_spec((1, D)), ANYSPEC, ANYSPEC, ANYSPEC, ANYSPEC],
        out_shape=[_sds((s_len, D)), _sds((1, D)), _sds((1, D)),
                   _sds((HALF_D, FF_HALF), BF), _sds((N_CHIPS - 1, HALF_D, FF_HALF), BF),
                   _sds((DN_SLOT, HALF_D), BF), _sds((N_CHIPS - 1, DN_SLOT, HALF_D), BF)],
        scratch_shapes=[pltpu.VMEM((s_len, D), BF), pltpu.VMEM((s_len, D), BF), pltpu.VMEM((s_len, D), F32),
                        pltpu.VMEM((D, FF_CHUNK), F32), pltpu.VMEM((D, FF_CHUNK), F32),
                        pltpu.VMEM((FF_CHUNK, D), F32),
                        pltpu.VMEM((nc, 2, HALF_D, FF_CHUNK), BF), pltpu.VMEM((nc, FF_CHUNK, HALF_D), BF),
                        pltpu.VMEM((2, 2, HALF_D, FF_CHUNK), BF), pltpu.VMEM((2, FF_CHUNK, HALF_D), BF),
                        pltpu.VMEM((2, HALF_D, FF_CHUNK), F32), pltpu.VMEM((FF_CHUNK, HALF_D), F32),
                        pltpu.VMEM((2, 2, HALF_D, FF_CHUNK), BF), pltpu.VMEM((2, FF_CHUNK, HALF_D), BF),
                        dma((2, 2)), dma((nc, 2)), dma((2, n_pieces)), dma((2, n_pieces)), dma((2, N_CHIPS - 1))],
        args=[dx2, x1, f, pre_g, wgu, wd, post_g], rider=rider)


def _ple_fwd(layer, x2, p, ple_g, w_gate, w_proj, post_g, target=None, rider=None):
    s_len = x2.shape[0]
    final = target is not None

    def body(*refs):
        if final:
            x_ref, p_ref, g_ref, wg_ref, wp_ref, qg_ref, t_ref, z_ref, pe_ref, dx_ref, lv_ref = refs
        else:
            x_ref, p_ref, g_ref, wg_ref, wp_ref, qg_ref, z_ref, pe_ref, x3_ref = refs
        x = x_ref[...]
        r = _rms_fwd(x, g_ref[layer:layer + 1, :]).astype(BF)
        z = _dot(r, wg_ref[...])
        pe = _dot(p_ref[...].astype(BF), wp_ref[...])
        z_ref[...] = z
        pe_ref[...] = pe
        x3 = x + _rms_fwd(pe * _sigmoid(z), qg_ref[layer:layer + 1, :])
        if final:
            @pl.when(pl.program_id(0) == 0)
            def _():
                lv_ref[...] = jnp.zeros_like(lv_ref)
            err = x3 - t_ref[...]
            dx_ref[...] = err * (1.0 / D)
            lv_ref[...] += _rowsum(err * err)
        else:
            x3_ref[...] = x3

    p_spec = pl.BlockSpec((None, TM, PLE), lambda i: (layer, i, 0))
    in_specs = [_row_spec(TM), p_spec, VSPEC, VSPEC, VSPEC, VSPEC]
    args = [x2, p, ple_g, w_gate, w_proj, post_g]
    out_specs = [_row_spec(TM), _row_spec(TM), _row_spec(TM)]
    out_shape = [_sds((s_len, D))] * 3
    if final:
        in_specs.append(_row_spec(TM))
        args.append(target)
        out_specs.append(_const_spec((1, D)))
        out_shape.append(_sds((1, D)))
    return _call(body, name=f"ple_fwd{layer}", grid=(s_len // TM,), in_specs=in_specs, out_specs=out_specs,
                 out_shape=out_shape, args=args, rider=rider)


def _ple_bwd(layer, dx3, x2, z, pe, p, ple_g, w_gate, post_g, rider=None):
    s_len = x2.shape[0]
    n = s_len // TM

    def body(dx_ref, x_ref, z_ref, pe_ref, p_ref, g_ref, wg_ref, qg_ref,
             dx2_ref, dwg_ref, dwp_ref, dg_ref, dqg_ref, gacc, pacc):
        i = pl.program_id(0)

        @pl.when(i == 0)
        def _():
            gacc[...] = jnp.zeros_like(gacc)
            pacc[...] = jnp.zeros_like(pacc)
            dg_ref[...] = jnp.zeros_like(dg_ref)
            dqg_ref[...] = jnp.zeros_like(dqg_ref)

        dx = dx_ref[...]
        x = x_ref[...]
        pe_v = pe_ref[...]
        gate = _sigmoid(z_ref[...])
        de, prod = _rms_bwd(pe_v * gate, qg_ref[layer:layer + 1, :], dx)
        dqg_ref[...] += _rowsum(prod)
        dpe = (de * gate).astype(BF)
        dz = (de * pe_v * gate * (1.0 - gate)).astype(BF)
        pacc[...] += _dot_tn(p_ref[...].astype(BF), dpe)
        g = g_ref[layer:layer + 1, :]
        r = _rms_fwd(x, g).astype(BF)
        gacc[...] += _dot_tn(r, dz)
        dr = _dot_nt(dz, wg_ref[...])
        dxp, prod2 = _rms_bwd(x, g, dr)
        dg_ref[...] += _rowsum(prod2)
        dx2_ref[...] = dx + dxp

        @pl.when(i == n - 1)
        def _():
            dwg_ref[...] = gacc[...].astype(BF)
            dwp_ref[...] = pacc[...].astype(BF)

    p_spec = pl.BlockSpec((None, TM, PLE), lambda i: (layer, i, 0))
    return _call(
        body, name=f"ple_bwd{layer}", grid=(n,),
        in_specs=[_row_spec(TM), _row_spec(TM), _row_spec(TM), _row_spec(TM), p_spec, VSPEC, VSPEC, VSPEC],
        out_specs=[_row_spec(TM), _const_spec((D, D)), _const_spec((PLE, D)), _const_spec((1, D)), _const_spec((1, D))],
        out_shape=[_sds((s_len, D)), _sds((D, D), BF), _sds((PLE, D), BF), _sds((1, D)), _sds((1, D))],
        scratch_shapes=[pltpu.VMEM((D, D), F32), pltpu.VMEM((PLE, D), F32)],
        args=[dx3, x2, z, pe, p, ple_g, w_gate, post_g], rider=rider)


def _qkv_fwd(x3, q_g, kv_g, w_q, w_kv, rider=None):
    s_len = x3.shape[0]

    def body(x_ref, qg_ref, kg_ref, wq_ref, wkv_ref, q_ref, kv_ref):
        x = x_ref[...]
        q_ref[...] = _dot(_rms_fwd(x, qg_ref[1:2, :]).astype(BF), wq_ref[...]).astype(BF)
        kv_ref[...] = _dot(_rms_fwd(x, kg_ref[...]).astype(BF), wkv_ref[...]).astype(BF)

    return _call(body, name="qkv_fwd", grid=(s_len // TM,),
                 in_specs=[_row_spec(TM), VSPEC, VSPEC, VSPEC, VSPEC],
                 out_specs=[_row_spec(TM), _row_spec(TM, 2 * KVD)],
                 out_shape=[_sds((s_len, D), BF), _sds((s_len, 2 * KVD), BF)],
                 args=[x3, q_g, kv_g, w_q, w_kv], rider=rider)


def _qkv_bwd(dq, dkv, x3, dx4, q_g, kv_g, w_q, w_kv):
    s_len = x3.shape[0]
    n = s_len // TM

    def body(dq_ref, dkv_ref, x_ref, dx_ref, qg_ref, kg_ref, wq_ref, wkv_ref,
             dx3_ref, dwq_ref, dwkv_ref, dqg_ref, dkg_ref, qacc, kacc):
        i = pl.program_id(0)

        @pl.when(i == 0)
        def _():
            qacc[...] = jnp.zeros_like(qacc)
            kacc[...] = jnp.zeros_like(kacc)
            dqg_ref[...] = jnp.zeros_like(dqg_ref)
            dkg_ref[...] = jnp.zeros_like(dkg_ref)

        x = x_ref[...]
        qg = qg_ref[1:2, :]
        kg = kg_ref[...]
        dq_v = dq_ref[...]
        dkv_v = dkv_ref[...].astype(BF)
        qacc[...] += _dot_tn(_rms_fwd(x, qg).astype(BF), dq_v)
        kacc[...] += _dot_tn(_rms_fwd(x, kg).astype(BF), dkv_v)
        dxq, prod_q = _rms_bwd(x, qg, _dot_nt(dq_v, wq_ref[...]))
        dxk, prod_k = _rms_bwd(x, kg, _dot_nt(dkv_v, wkv_ref[...]))
        dqg_ref[...] += _rowsum(prod_q)
        dkg_ref[...] += _rowsum(prod_k)
        dx3_ref[...] = dx_ref[...] + dxq + dxk

        @pl.when(i == n - 1)
        def _():
            dwq_ref[...] = qacc[...].astype(BF)
            dwkv_ref[...] = kacc[...].astype(BF)

    outs, _ = _call(
        body, name="qkv_bwd", grid=(n,),
        in_specs=[_row_spec(TM), _row_spec(TM, 2 * KVD), _row_spec(TM), _row_spec(TM), VSPEC, VSPEC, VSPEC, VSPEC],
        out_specs=[_row_spec(TM), _const_spec((D, D)), _const_spec((D, 2 * KVD)),
                   _const_spec((1, D)), _const_spec((1, D))],
        out_shape=[_sds((s_len, D)), _sds((D, D), BF), _sds((D, 2 * KVD), BF), _sds((1, D)), _sds((1, D))],
        scratch_shapes=[pltpu.VMEM((D, D), F32), pltpu.VMEM((D, 2 * KVD), F32)],
        args=[dq, dkv, x3, dx4, q_g, kv_g, w_q, w_kv])
    return outs


def _attn_block(i, q, kvw, sink_ref):
    off = jnp.where(i > 0, BLK, 0)
    rel = (lax.broadcasted_iota(jnp.int32, (BLK, 2 * BLK), 0)
           - lax.broadcasted_iota(jnp.int32, (BLK, 2 * BLK), 1) + off)
    valid = (rel >= 0) & (rel < BLK)
    relf = rel.astype(F32)
    out = []
    for h in range(N_HEADS):
        kh = h // GQA
        qh = q[:, h * HEAD_DIM:(h + 1) * HEAD_DIM]
        k = kvw[:, kh * HEAD_DIM:(kh + 1) * HEAD_DIM]
        v = kvw[:, KVD + kh * HEAD_DIM:KVD + (kh + 1) * HEAD_DIM]
        s = _dot_nt(qh, k) * ATT_SCALE - SLOPES[h] * relf
        s = jnp.where(valid, s, NEG_INF)
        sink = sink_ref[0, h]
        m = jnp.maximum(jnp.max(s, axis=-1, keepdims=True), sink)
        e = jnp.exp(s - m)
        es = jnp.exp(sink - m)
        inv = 1.0 / (jnp.sum(e, axis=-1, keepdims=True) + es)
        out.append((e * inv, es * inv, qh, k, v))
    return out


def _kv_window(kv_ref, i):
    ks = pl.multiple_of(jnp.maximum(i * BLK - BLK, 0), BLK)
    return ks, kv_ref[pl.ds(ks, 2 * BLK), :]


def _attn_fwd(q, kv, sinks, x3, w_o, post_g, rider=None):
    s_len = q.shape[0]

    def body(q_ref, kv_ref, sk_ref, x_ref, wo_ref, g_ref, a_ref, y_ref, x4_ref):
        i = pl.program_id(0)
        _, kvw = _kv_window(kv_ref, i)
        heads = _attn_block(i, q_ref[...], kvw, sk_ref)
        attn = jnp.concatenate([_dot(p.astype(BF), v) for p, _, _, _, v in heads], axis=1)
        a_ref[...] = attn
        y = _dot(attn.astype(BF), wo_ref[...])
        y_ref[...] = y
        x4_ref[...] = x_ref[...] + _rms_fwd(y, g_ref[1:2, :])

    return _call(body, name="attn_fwd", grid=(s_len // BLK,),
                 in_specs=[_row_spec(BLK), VSPEC, SSPEC, _row_spec(BLK), VSPEC, VSPEC],
                 out_specs=[_row_spec(BLK)] * 3, out_shape=[_sds((s_len, D))] * 3,
                 args=[q, kv, sinks, x3, w_o, post_g], rider=rider)


def _attn_bwd(dx4, y, attn, q, kv, sinks, w_o, post_g, rider=None):
    s_len = q.shape[0]
    n = s_len // BLK

    def body(dx_ref, y_ref, a_ref, q_ref, kv_ref, sk_ref, wo_ref, g_ref,
             dq_ref, dkv_ref, dwo_ref, dg_ref, dsk_ref, wacc):
        i = pl.program_id(0)

        @pl.when(i == 0)
        def _():
            dkv_ref[...] = jnp.zeros_like(dkv_ref)
            wacc[...] = jnp.zeros_like(wacc)
            dg_ref[...] = jnp.zeros_like(dg_ref)
            dsk_ref[...] = jnp.zeros_like(dsk_ref)

        dy, prod = _rms_bwd(y_ref[...], g_ref[1:2, :], dx_ref[...])
        dg_ref[...] += _rowsum(prod)
        dyb = dy.astype(BF)
        attn = a_ref[...]
        wacc[...] += _dot_tn(attn.astype(BF), dyb)
        d_o = _dot_nt(dyb, wo_ref[...])
        dod = d_o * attn
        ks, kvw = _kv_window(kv_ref, i)
        heads = _attn_block(i, q_ref[...], kvw, sk_ref)
        lane = lax.broadcasted_iota(jnp.int32, (1, D), 1)
        dqs = []
        dks = [None] * N_KV_HEADS
        dvs = [None] * N_KV_HEADS
        dsk = jnp.zeros((1, D), F32)
        for h, (p, ps, qh, k, v) in enumerate(heads):
            kh = h // GQA
            hs = slice(h * HEAD_DIM, (h + 1) * HEAD_DIM)
            do_h = d_o[:, hs].astype(BF)
            dsum = jnp.sum(dod[:, hs], axis=-1, keepdims=True)
            dp = _dot_nt(do_h, v)
            dsb = (p * (dp - dsum) * ATT_SCALE).astype(BF)
            dsk = dsk + jnp.where(lane == h, -_rowsum(ps * dsum), 0.0)
            dqs.append(_dot(dsb, k))
            dk = _dot_tn(dsb, qh)
            dv = _dot_tn(p.astype(BF), do_h)
            dks[kh] = dk if dks[kh] is None else dks[kh] + dk
            dvs[kh] = dv if dvs[kh] is None else dvs[kh] + dv
        dsk_ref[...] += dsk
        dq_ref[...] = jnp.concatenate(dqs, axis=1).astype(BF)
        dkv_ref[pl.ds(ks, 2 * BLK), :] += jnp.concatenate(dks + dvs, axis=1)

        @pl.when(i == n - 1)
        def _():
            dwo_ref[...] = wacc[...].astype(BF)

    return _call(
        body, name="attn_bwd", grid=(n,),
        in_specs=[_row_spec(BLK), _row_spec(BLK), _row_spec(BLK), _row_spec(BLK), VSPEC, SSPEC, VSPEC, VSPEC],
        out_specs=[_row_spec(BLK), _const_spec((s_len, 2 * KVD)), _const_spec((D, D)),
                   _const_spec((1, D)), _const_spec((1, D))],
        out_shape=[_sds((s_len, D), BF), _sds((s_len, 2 * KVD)), _sds((D, D), BF), _sds((1, D)), _sds((1, D))],
        scratch_shapes=[pltpu.VMEM((D, D), F32)],
        args=[dx4, y, attn, q, kv, sinks, w_o, post_g], rider=rider)


Big = collections.namedtuple("Big", "name src layer L A R C rb")


def _bigs():
    out = {"pool_w": Big("pool_w", "pool_w", None, 4, 4, POOL_G // N_CHIPS, POOL_G, 32)}
    for l in range(2):
        out[f"w_gu{l}"] = Big(f"w_gu{l}", "w_gu", l, 1, 2, D, FF_HALF, 256)
        out[f"w_down{l}"] = Big(f"w_down{l}", "w_down", l, 1, 4, FF // N_CHIPS, D, 352)
        out[f"w_ple_gate{l}"] = Big(f"w_ple_gate{l}", "w_ple_gate", l, 1, 4, D // N_CHIPS, D, 128)
        out[f"w_ple_proj{l}"] = Big(f"w_ple_proj{l}", "w_ple_proj", l, 1, 1, PLE, D // N_CHIPS, 128)
    out["w_q"] = Big("w_q", "w_q", None, 1, 4, D // N_CHIPS, D, 128)
    out["w_o"] = Big("w_o", "w_o", None, 1, 4, D // N_CHIPS, D, 128)
    out["w_kv"] = Big("w_kv", "w_kv", None, 1, 4, D // N_CHIPS, 2 * KVD, 128)
    return out


BIGS = _bigs()
POOL_SCALE = Big("pool_scale", "pool_scale", None, 1, 1, 1, D // N_CHIPS, 1)
BIG_SOURCES = ("w_gu", "w_down", "w_ple_gate", "w_ple_proj", "w_q", "w_o", "w_kv", "pool_w")


def _ncb(t):
    return N_CHIPS // t.A


def _full_shape(t, rows=None):
    return (t.L, t.A, t.R if rows is None else rows, _ncb(t) * t.C)


def _slot_index(t, k):
    return k // _ncb(t), k % _ncb(t)


def _slot(ref, t, k, row0, rows):
    a, cb = _slot_index(t, k)
    return ref.at[:, a, pl.ds(row0, rows), pl.ds(pl.multiple_of(cb * t.C, 128), t.C)]


def _place(t, w, kc, out_dtype):
    rb = min(t.R, 2 * t.rb)

    def body(kc_ref, w_ref, o_ref):
        del kc_ref
        o_ref[...] = w_ref[...].astype(out_dtype)

    def in_map(l, j, kc_ref):
        return (l if t.layer is None else t.layer, j, 0)

    def out_map(l, j, kc_ref):
        a, cb = _slot_index(t, kc_ref[0])
        return (l, a, j, cb)

    return pl.pallas_call(
        body, name=f"place_{t.name}",
        grid_spec=pltpu.PrefetchScalarGridSpec(
            num_scalar_prefetch=1, grid=(t.L, t.R // rb),
            in_specs=[pl.BlockSpec((None, rb, t.C), in_map)],
            out_specs=pl.BlockSpec((None, None, rb, t.C), out_map)),
        out_shape=_sds(_full_shape(t), out_dtype),
        compiler_params=_params(2),
    )(kc, w)


def _mesh_position():
    x, y, c = lax.axis_index("x"), lax.axis_index("y"), lax.axis_index("c")
    chips = [(1 - x, y), (x, 1 - y), (1 - x, 1 - y)]
    return x, y, c, chips


def _gather_rider(parts, fulls):
    nt = len(parts)
    TO_X, TO_Y, FWD_X, FWD_Y, SIB_X, SIB_Y, SIB_D = range(7)

    def rows_of(ti, core):
        t, r0, r1 = parts[ti]
        h = (r1 - r0) // 2
        return r0 + core * h, h

    def copy(outs, sems, kind, ti, k_src, row0, rows, dev):
        region = _slot(outs[ti], parts[ti][0], k_src, row0, rows)
        return pltpu.make_async_remote_copy(region, region, sems[0].at[ti, kind], sems[1].at[ti, kind],
                                            device_id=dev, device_id_type=MESH)

    def plan(outs, sems):
        x, y, c, _ = _mesh_position()
        me, kx, ky, kd = 2 * x + y, 2 * (1 - x) + y, 2 * x + (1 - y), 2 * (1 - x) + (1 - y)
        dev_x, dev_y, dev_d, sib = (1 - x, y, c), (x, 1 - y, c), (1 - x, 1 - y, c), (x, y, 1 - c)

        def whole(ti):
            return 0, parts[ti][0].R

        def mk(kind, k_send, k_recv, dev, send_rows, recv_rows):
            def build(ti, side):
                k_src = k_send if side == "s" else k_recv
                row0, rows = (send_rows if side == "s" else recv_rows)(ti)
                return copy(outs, sems, kind, ti, k_src, row0, rows, dev)
            return build

        def first_half(core):
            return lambda ti: (rows_of(ti, core)[0], rows_of(ti, core)[1] // 2)

        def second_half(core):
            return lambda ti: (rows_of(ti, core)[0] + rows_of(ti, core)[1] // 2, rows_of(ti, core)[1] // 2)

        mine = lambda ti: rows_of(ti, c)
        theirs = lambda ti: rows_of(ti, 1 - c)
        split = {
            TO_X: mk(TO_X, me, kx, dev_x, mine, mine),
            TO_Y: mk(TO_Y, me, ky, dev_y, mine, mine),
            FWD_X: mk(FWD_X, ky, kd, dev_x, first_half(c), first_half(c)),
            FWD_Y: mk(FWD_Y, kx, kd, dev_y, second_half(c), second_half(c)),
            SIB_X: mk(SIB_X, kx, kx, sib, mine, theirs),
            SIB_Y: mk(SIB_Y, ky, ky, sib, mine, theirs),
            SIB_D: mk(SIB_D, kd, kd, sib, mine, theirs),
        }
        direct = {
            TO_X: mk(TO_X, me, kx, dev_x, whole, whole),
            TO_Y: mk(TO_Y, me, ky, dev_y, whole, whole),
            FWD_X: mk(FWD_X, me, kd, dev_d, whole, whole),
        }
        return split, direct

    is_split = [t.R > 1 for t, _, _ in parts]

    def start(ins, outs, sems):
        split, direct = plan(outs, sems)
        for ti in range(nt):
            kinds = split if is_split[ti] else direct
            kinds[TO_X](ti, "s").start()
            kinds[TO_Y](ti, "s").start()
            if not is_split[ti]:
                kinds[FWD_X](ti, "s").start()

    def mid(ins, outs, sems):
        split, _ = plan(outs, sems)
        for ti in range(nt):
            if is_split[ti]:
                split[TO_Y](ti, "r").wait_recv()
                split[FWD_X](ti, "s").start()
                split[SIB_Y](ti, "s").start()
        for ti in range(nt):
            if is_split[ti]:
                split[TO_X](ti, "r").wait_recv()
                split[FWD_Y](ti, "s").start()
                split[SIB_X](ti, "s").start()

    def finish(ins, outs, sems):
        split, direct = plan(outs, sems)
        for ti in range(nt):
            if is_split[ti]:
                split[FWD_X](ti, "r").wait_recv()
                split[FWD_Y](ti, "r").wait_recv()
                split[SIB_D](ti, "s").start()
            else:
                for kind in (TO_X, TO_Y, FWD_X):
                    direct[kind](ti, "r").wait_recv()
        for ti in range(nt):
            if is_split[ti]:
                for kind in (SIB_X, SIB_Y, SIB_D):
                    split[kind](ti, "r").wait_recv()
        for ti in range(nt):
            kinds = split if is_split[ti] else direct
            for kind in kinds:
                kinds[kind](ti, "s").wait_send()

    sems = pltpu.SemaphoreType.DMA((nt, 7))
    return Rider(list(fulls), [_sds(a.shape, a.dtype) for a in fulls], {i: i for i in range(nt)},
                 [sems, sems], start, mid, finish)


def _pair_exchange(name, specs, grads):
    nt = len(specs)

    def body(*refs):
        gs = refs[:nt]
        lands = refs[nt:2 * nt]
        send, recv = refs[2 * nt:]
        x, y, c, _ = _mesh_position()
        cps = []
        for ti, t in enumerate(specs):
            half = t.R // 2
            cp = pltpu.make_async_remote_copy(gs[ti].at[:, :, pl.ds((1 - c) * half, half), :], lands[ti],
                                              send.at[ti], recv.at[ti],
                                              device_id=(x, y, 1 - c), device_id_type=MESH)
            cp.start()
            cps.append(cp)
        for cp in cps:
            cp.wait()

    return pl.pallas_call(
        body, name=name,
        in_specs=[ANYSPEC] * nt, out_specs=[ANYSPEC] * nt,
        out_shape=[_sds(_full_shape(t, t.R // 2), BF) for t in specs],
        scratch_shapes=[pltpu.SemaphoreType.DMA((nt,)), pltpu.SemaphoreType.DMA((nt,))],
        compiler_params=_params(),
    )(*grads)


def _pair_sum(t, g, land, kc):
    half = t.R // 2
    nj = half // t.rb
    w = _ncb(t) * t.C

    def body(kc_ref, g_ref, l_ref, o_ref):
        del kc_ref
        o_ref[...] = (g_ref[...].astype(F32) + l_ref[...].astype(F32)).astype(BF)

    return pl.pallas_call(
        body, name=f"pair_sum_{t.name}",
        grid_spec=pltpu.PrefetchScalarGridSpec(
            num_scalar_prefetch=1, grid=(t.L, nj),
            in_specs=[pl.BlockSpec((None, t.A, t.rb, w), lambda l, j, kc_ref: (l, 0, kc_ref[1] * nj + j, 0)),
                      pl.BlockSpec((None, t.A, t.rb, w), lambda l, j, kc_ref: (l, 0, j, 0))],
            out_specs=pl.BlockSpec((None, t.A, t.rb, w), lambda l, j, kc_ref: (l, 0, j, 0))),
        out_shape=_sds(_full_shape(t, half), BF),
        compiler_params=_params(2),
    )(kc, g, land)


def _scatter_rider(specs, sums):
    nt = len(specs)

    def copy(ins, outs, sems, ti, j, chip, c):
        t = specs[ti]
        cx, cy = chip
        return pltpu.make_async_remote_copy(_slot(ins[ti], t, 2 * cx + cy, 0, t.R // 2), outs[ti].at[j],
                                            sems[0].at[ti, j], sems[1].at[ti, j],
                                            device_id=(cx, cy, c), device_id_type=MESH)

    def start(ins, outs, sems):
        _, _, c, chips = _mesh_position()
        for j, chip in enumerate(chips):
            for ti in range(nt):
                copy(ins, outs, sems, ti, j, chip, c).start()

    def finish(ins, outs, sems):
        _, _, c, chips = _mesh_position()
        for j, chip in enumerate(chips):
            for ti in range(nt):
                copy(ins, outs, sems, ti, j, chip, c).wait()

    sems = pltpu.SemaphoreType.DMA((nt, N_CHIPS - 1))
    return Rider(list(sums), [_sds((N_CHIPS - 1, t.L, t.R // 2, t.C), BF) for t in specs], {}, [sems, sems],
                 start, None, finish)


def _chip_sum(t, s, land, kc, n_layers, prev):
    half = t.R // 2
    nj = half // t.rb

    def body(*refs):
        s_ref, l_ref, o_ref = refs[1], refs[2], refs[-1]
        acc = s_ref[...].astype(F32)
        for j in range(N_CHIPS - 1):
            acc = acc + l_ref[j].astype(F32)
        o_ref[...] = acc

    def own_map(l, j, kc_ref):
        a, cb = _slot_index(t, kc_ref[0])
        return (l, a, j, cb)

    def out_map(l, j, kc_ref):
        return (l if t.layer is None else t.layer, kc_ref[1] * nj + j, 0)

    in_specs = [pl.BlockSpec((None, None, t.rb, t.C), own_map),
                pl.BlockSpec((N_CHIPS - 1, None, t.rb, t.C), lambda l, j, kc_ref: (0, l, j, 0))]
    args = [kc, s, land]
    aliases = {}
    if prev is not None:
        in_specs.append(ANYSPEC)
        args.append(prev)
        aliases = {3: 0}
    return pl.pallas_call(
        body, name=f"chip_sum_{t.name}",
        grid_spec=pltpu.PrefetchScalarGridSpec(
            num_scalar_prefetch=1, grid=(t.L, nj), in_specs=in_specs,
            out_specs=pl.BlockSpec((None, t.rb, t.C), out_map)),
        out_shape=_sds((n_layers, t.R, t.C)),
        input_output_aliases=aliases,
        compiler_params=_params(2),
    )(*args)


def _chip_sum_fused(t, own, land, kc, n_layers, prev, by_cols):
    rows, cols = own.shape
    nj = rows // t.rb

    def body(*refs):
        o_ref, l_ref, out_ref = refs[1], refs[2], refs[-1]
        acc = o_ref[...].astype(F32)
        for j in range(N_CHIPS - 1):
            acc = acc + l_ref[j].astype(F32)
        out_ref[...] = acc

    def out_map(j, kc_ref):
        return (t.layer, j, kc_ref[1]) if by_cols else (t.layer, kc_ref[1] * nj + j, 0)

    in_specs = [pl.BlockSpec((t.rb, cols), lambda j, kc_ref: (j, 0)),
                pl.BlockSpec((N_CHIPS - 1, t.rb, cols), lambda j, kc_ref: (0, j, 0))]
    args = [kc, own, land]
    aliases = {}
    if prev is not None:
        in_specs.append(ANYSPEC)
        args.append(prev)
        aliases = {3: 0}
    return pl.pallas_call(
        body, name=f"chip_sum_{t.name}",
        grid_spec=pltpu.PrefetchScalarGridSpec(
            num_scalar_prefetch=1, grid=(nj,), in_specs=in_specs,
            out_specs=pl.BlockSpec((None, t.rb, cols), out_map)),
        out_shape=_sds((n_layers, t.R, t.C)),
        input_output_aliases=aliases,
        compiler_params=_params(1),
    )(*args)


def _pair_share(halves, by_cols):
    nt = len(halves)

    def part(ref, ti, core):
        axis = 2 if by_cols[ti] else 1
        half = halves[ti].shape[axis] // 2
        piece = pl.ds(pl.multiple_of(core * half, 128 if by_cols[ti] else 8), half)
        return ref.at[:, :, piece] if by_cols[ti] else ref.at[:, piece, :]

    def body(*refs):
        outs = refs[nt:2 * nt]
        send, recv = refs[2 * nt:]
        x, y, c, _ = _mesh_position()
        cps = []
        for ti in range(nt):
            mine = part(outs[ti], ti, c)
            cp = pltpu.make_async_remote_copy(mine, mine, send.at[ti], recv.at[ti],
                                              device_id=(x, y, 1 - c), device_id_type=MESH)
            cp.start()
            cps.append(cp)
        for ti in range(nt):
            theirs = part(outs[ti], ti, 1 - c)
            pltpu.make_async_remote_copy(theirs, theirs, send.at[ti], recv.at[ti],
                                         device_id=(x, y, 1 - c), device_id_type=MESH).wait_recv()
        for cp in cps:
            cp.wait_send()

    return pl.pallas_call(
        body, name="grads_pair_share",
        in_specs=[ANYSPEC] * nt, out_specs=[ANYSPEC] * nt,
        out_shape=[_sds(a.shape, a.dtype) for a in halves],
        scratch_shapes=[pltpu.SemaphoreType.DMA((nt,)), pltpu.SemaphoreType.DMA((nt,))],
        input_output_aliases={i: i for i in range(nt)},
        compiler_params=_params(),
    )(*halves)


def _adamw_math(w, g, m, v):
    m = B1 * m + (1.0 - B1) * g
    v = B2 * v + (1.0 - B2) * (g * g)
    delta = -LR * ((m / BC1) / (jnp.sqrt(v / BC2) + AEPS) + WD * w)
    return delta, m, v


def _adamw(name, rb, w, g, m, v):
    n_layers, r, c = w.shape

    def body(w_ref, g_ref, m_ref, v_ref, go_ref, d_ref, nm_ref, nv_ref):
        g_v = g_ref[...]
        go_ref[...] = g_v
        d_ref[...], nm_ref[...], nv_ref[...] = _adamw_math(w_ref[...], g_v, m_ref[...], v_ref[...])

    spec = pl.BlockSpec((None, rb, c), lambda l, j: (l, j, 0))
    return pl.pallas_call(
        body, name=f"adamw_{name}", grid=(n_layers, r // rb),
        in_specs=[spec] * 4, out_specs=[spec] * 4, out_shape=[_sds(w.shape)] * 4,
        compiler_params=_params(2),
    )(w, g, m, v)


GAIN_ROWS = {"pre_mix_g": 0, "post_mix_g": 2, "pre_ffn_g": 4, "post_ffn_g": 6, "ple_g": 8, "ple_post_g": 10}
ROW_KV_G, ROW_POOL_SCALE, ROW_SINKS, ROW_LOSS, PACK_ROWS = 12, 13, 14, 15, 16
SMALL_NAMES = tuple(GAIN_ROWS) + ("kv_g", "pool_scale", "sinks")


def _small_all_reduce(rows, dpool):
    ng, pr = len(WINDOWS), POOL_G // N_CHIPS

    def body(*refs):
        row_refs = refs[:PACK_ROWS]
        dpool_ref, tot_ref, gpool_ref, pack, land, pland, send, recv, psend, precv = refs[PACK_ROWS:]
        x, y, c, _ = _mesh_position()
        me = 4 * x + 2 * y + c
        for r in range(PACK_ROWS):
            pack[r:r + 1, :] = row_refs[r][...]

        def shard_of(k):
            return dpool_ref.at[:, pl.ds(pl.multiple_of(k * pr, pr), pr), :]

        cps = []
        for j in range(1, N_DEV):
            px, py, pc = x ^ (j >> 2), y ^ ((j >> 1) & 1), c ^ (j & 1)
            cps.append(pltpu.make_async_remote_copy(pack, land.at[me], send.at[j], recv.at[j],
                                                    device_id=(px, py, pc), device_id_type=MESH))
            cps.append(pltpu.make_async_remote_copy(shard_of(2 * px + py), pland.at[me], psend.at[j], precv.at[j],
                                                    device_id=(px, py, pc), device_id_type=MESH))
        for cp in cps:
            cp.start()
        land[me] = pack[...]
        pland[me] = dpool_ref[:, pl.ds(pl.multiple_of((2 * x + y) * pr, pr), pr), :]
        for j in range(1, N_DEV):
            pltpu.make_async_remote_copy(pack, land.at[me ^ j], send.at[j], recv.at[j],
                                         device_id=(x, y, c), device_id_type=MESH).wait_recv()
            pltpu.make_async_remote_copy(shard_of(0), pland.at[me ^ j], psend.at[j], precv.at[j],
                                         device_id=(x, y, c), device_id_type=MESH).wait_recv()
        for cp in cps:
            cp.wait_send()
        tot = land[0]
        gp = pland[0].astype(F32)
        for d in range(1, N_DEV):
            tot = tot + land[d]
            gp = gp + pland[d].astype(F32)
        tot_ref[...] = tot
        gpool_ref[...] = gp

    sems = pltpu.SemaphoreType.DMA((N_DEV,))
    return pl.pallas_call(
        body, name="small_all_reduce",
        in_specs=[VSPEC] * (PACK_ROWS + 1), out_specs=[VSPEC, VSPEC],
        out_shape=[_sds((PACK_ROWS, D)), _sds((ng, pr, POOL_G))],
        scratch_shapes=[pltpu.VMEM((PACK_ROWS, D), F32), pltpu.VMEM((N_DEV, PACK_ROWS, D), F32),
                        pltpu.VMEM((N_DEV, ng, pr, POOL_G), BF), sems, sems, sems, sems],
        compiler_params=_params(),
    )(*rows, dpool)


def _small_adamw(tot, kc, small_w, small_m, small_v):
    names = SMALL_NAMES
    n = len(names)

    def body(*refs):
        tot_ref, kc_ref = refs[0], refs[1]
        w_refs = dict(zip(names, refs[2:2 + n]))
        m_refs = dict(zip(names, refs[2 + n:2 + 2 * n]))
        v_refs = dict(zip(names, refs[2 + 2 * n:2 + 3 * n]))
        loss_ref = refs[2 + 3 * n]
        out_refs = {nm: refs[3 + 3 * n + 4 * k: 7 + 3 * n + 4 * k] for k, nm in enumerate(names)}
        tot = tot_ref[...]
        loss_ref[...] = 0.5 * jnp.sum(tot[ROW_LOSS:ROW_LOSS + 1, :], axis=-1, keepdims=True) * (1.0 / D)

        def update(nm, g):
            g_ref, d_ref, nm_ref, nv_ref = out_refs[nm]
            g_ref[...] = g
            d_ref[...], nm_ref[...], nv_ref[...] = _adamw_math(w_refs[nm][...], g, m_refs[nm][...], v_refs[nm][...])

        for nm, r in GAIN_ROWS.items():
            update(nm, tot[r:r + 2, :])
        update("kv_g", tot[ROW_KV_G:ROW_KV_G + 1, :])
        k = kc_ref[0]
        width = D // N_CHIPS
        g_scale = jnp.zeros((1, width), F32)
        for kk in range(N_CHIPS):
            g_scale = g_scale + jnp.where(k == kk, tot[ROW_POOL_SCALE:ROW_POOL_SCALE + 1, kk * width:(kk + 1) * width], 0.0)
        update("pool_scale", g_scale)
        update("sinks", tot[ROW_SINKS:ROW_SINKS + 1, 0:N_HEADS])

    ins = [tot, kc] + [small_w[nm] for nm in names] + [small_m[nm] for nm in names] + [small_v[nm] for nm in names]
    out_shape = [_sds((1, 1))]
    for nm in names:
        out_shape += [_sds(small_w[nm].shape)] * 4
    outs = pl.pallas_call(
        body, name="small_adamw",
        in_specs=[VSPEC, SSPEC] + [VSPEC] * (3 * n), out_specs=[VSPEC] * len(out_shape), out_shape=out_shape,
        compiler_params=_params(),
    )(*ins)
    return outs[0], {nm: outs[1 + 4 * k: 5 + 4 * k] for k, nm in enumerate(names)}


def _compute_layout(t, full):
    if t.src == "w_gu":
        return full.reshape(2, D, FF)
    if t.src == "pool_w":
        return full.reshape(len(WINDOWS), POOL_G, POOL_G)
    if t.src == "pool_scale":
        return full.reshape(1, D)
    return full.reshape(t.A * t.R, _ncb(t) * t.C)


def kernel(x, p, pre_mix_g, post_mix_g, pre_ffn_g, post_ffn_g, pool_w, pool_scale, kv_g, w_kv, w_q, sinks, w_o, w_gu, w_down, ple_g, w_ple_gate, w_ple_proj, ple_post_g, loss_target, m_pre_mix_g, m_post_mix_g, m_pre_ffn_g, m_post_ffn_g, m_pool_w, m_pool_scale, m_kv_g, m_w_kv, m_w_q, m_sinks, m_w_o, m_w_gu, m_w_down, m_ple_g, m_w_ple_gate, m_w_ple_proj, m_ple_post_g, v_pre_mix_g, v_post_mix_g, v_pre_ffn_g, v_post_ffn_g, v_pool_w, v_pool_scale, v_kv_g, v_w_kv, v_w_q, v_sinks, v_w_o, v_w_gu, v_w_down, v_ple_g, v_w_ple_gate, v_w_ple_proj, v_ple_post_g):
    weights = dict(pre_mix_g=pre_mix_g, post_mix_g=post_mix_g, pre_ffn_g=pre_ffn_g, post_ffn_g=post_ffn_g,
                   pool_w=pool_w, pool_scale=pool_scale, kv_g=kv_g, w_kv=w_kv, w_q=w_q, sinks=sinks, w_o=w_o,
                   w_gu=w_gu, w_down=w_down, ple_g=ple_g, w_ple_gate=w_ple_gate, w_ple_proj=w_ple_proj,
                   ple_post_g=ple_post_g)
    m_in = dict(pre_mix_g=m_pre_mix_g, post_mix_g=m_post_mix_g, pre_ffn_g=m_pre_ffn_g, post_ffn_g=m_post_ffn_g,
                pool_w=m_pool_w, pool_scale=m_pool_scale, kv_g=m_kv_g, w_kv=m_w_kv, w_q=m_w_q, sinks=m_sinks,
                w_o=m_w_o, w_gu=m_w_gu, w_down=m_w_down, ple_g=m_ple_g, w_ple_gate=m_w_ple_gate,
                w_ple_proj=m_w_ple_proj, ple_post_g=m_ple_post_g)
    v_in = dict(pre_mix_g=v_pre_mix_g, post_mix_g=v_post_mix_g, pre_ffn_g=v_pre_ffn_g, post_ffn_g=v_post_ffn_g,
                pool_w=v_pool_w, pool_scale=v_pool_scale, kv_g=v_kv_g, w_kv=v_w_kv, w_q=v_w_q, sinks=v_sinks,
                w_o=v_w_o, w_gu=v_w_gu, w_down=v_w_down, ple_g=v_ple_g, w_ple_gate=v_w_ple_gate,
                w_ple_proj=v_w_ple_proj, ple_post_g=v_ple_post_g)
    order = ["pre_mix_g", "post_mix_g", "pre_ffn_g", "post_ffn_g", "pool_w", "pool_scale", "kv_g", "w_kv", "w_q",
             "sinks", "w_o", "w_gu", "w_down", "ple_g", "w_ple_gate", "w_ple_proj", "ple_post_g"]

    kc = jnp.stack([2 * lax.axis_index("x") + lax.axis_index("y"), lax.axis_index("c")]).astype(jnp.int32)
    s_len = x.shape[1]
    x2d = x.reshape(s_len, D)
    p3d = p.reshape(2, s_len, PLE)
    target = loss_target.reshape(s_len, D)
    kv_g2d = kv_g.reshape(1, D)
    gains = {nm: weights[nm] for nm in GAIN_ROWS}

    def shard_view(src, a):
        t = next(t for t in BIGS.values() if t.src == src)
        return a.reshape(-1, t.R, t.C)

    placed = {nm: _place(t, shard_view(t.src, weights[t.src]), kc, BF) for nm, t in BIGS.items()}
    placed["pool_scale"] = _place(POOL_SCALE, pool_scale.reshape(1, 1, D // N_CHIPS), kc, F32)
    specs = dict(BIGS, pool_scale=POOL_SCALE)

    def gather(names, rows=None):
        rows = rows or {}
        parts = [(specs[nm],) + tuple(rows.get(nm, (0, specs[nm].R))) for nm in names]
        return _gather_rider(parts, [placed[nm] for nm in names])

    def take(names, results):
        for nm, a in zip(names, results):
            placed[nm] = a

    def weight(nm):
        return _compute_layout(specs[nm], placed[nm])

    first = ["pool_w", "pool_scale", "w_gu0", "w_down0"]
    take(first, _run("weights_gather_first", gather(first)))

    y0, x1 = _mixa_fwd(x2d, gains["pre_mix_g"], weight("pool_w"), weight("pool_scale"), gains["post_mix_g"])

    ride = ["w_ple_gate0", "w_ple_proj0", "w_q", "w_kv", "w_o", "w_gu1"]
    (f0, x2), got = _ffn_fwd(0, x1, gains["pre_ffn_g"], weight("w_gu0"), weight("w_down0"), gains["post_ffn_g"],
                             rider=gather(ride, {"w_gu1": (0, 320)}))
    take(ride, got)

    ride = ["w_ple_gate1", "w_ple_proj1", "w_gu1"]
    (z0, pe0, x3), got = _ple_fwd(0, x2, p3d, gains["ple_g"], weight("w_ple_gate0"), weight("w_ple_proj0"),
                                  gains["ple_post_g"], rider=gather(ride, {"w_gu1": (320, 448)}))
    take(ride, got)

    ride = ["w_gu1"]
    (q, kv), got = _qkv_fwd(x3, gains["pre_mix_g"], kv_g2d, weight("w_q"), weight("w_kv"),
                            rider=gather(ride, {"w_gu1": (448, 704)}))
    take(ride, got)

    ride = ["w_down1", "w_gu1"]
    (attn, y1, x4), got = _attn_fwd(q, kv, sinks, x3, weight("w_o"), gains["post_mix_g"],
                                    rider=gather(ride, {"w_gu1": (704, D)}))
    take(ride, got)

    (f1, x5), _ = _ffn_fwd(1, x4, gains["pre_ffn_g"], weight("w_gu1"), weight("w_down1"), gains["post_ffn_g"])
    (z1, pe1, dx6, loss_row), _ = _ple_fwd(1, x5, p3d, gains["ple_g"], weight("w_ple_gate1"), weight("w_ple_proj1"),
                                           gains["ple_post_g"], target=target)

    local = {}
    landed = {}
    fused = {}

    def pair_stage(tag, names):
        ts = [BIGS[nm] for nm in names]
        gs = [local[nm].reshape(_full_shape(t)) for nm, t in zip(names, ts)]
        lands = _pair_exchange(f"grads_pair_exchange_{tag}", ts, gs)
        return [_pair_sum(t, g, l, kc) for t, g, l in zip(ts, gs, lands)]

    def scatter(names, sums):
        return _scatter_rider([BIGS[nm] for nm in names], sums)

    def keep(names, sums, got):
        for nm, s, l in zip(names, sums, got):
            landed[nm] = (s, l)

    (dx5, local["w_ple_gate1"], local["w_ple_proj1"], d_ple1, d_plepost1), _ = _ple_bwd(
        1, dx6, x5, z1, pe1, p3d, gains["ple_g"], weight("w_ple_gate1"), gains["ple_post_g"])

    group_a = ["w_ple_gate1", "w_ple_proj1"]
    sums_a = pair_stage("a", group_a)
    (dx4, d_preffn1, d_postffn1, *scattered), got = _ffn_bwd(
        1, dx5, x4, f1, gains["pre_ffn_g"], weight("w_gu1"), weight("w_down1"), gains["post_ffn_g"],
        rider=scatter(group_a, sums_a))
    fused["w_gu1"], fused["w_down1"] = scattered[0:2], scattered[2:4]
    keep(group_a, sums_a, got)

    (dq, dkv, local["w_o"], d_postmix1, d_sinks), _ = _attn_bwd(
        dx4, y1, attn, q, kv, sinks, weight("w_o"), gains["post_mix_g"])
    dx3, local["w_q"], local["w_kv"], d_premix1, d_kvg = _qkv_bwd(
        dq, dkv, x3, dx4, gains["pre_mix_g"], kv_g2d, weight("w_q"), weight("w_kv"))

    group_b = ["w_o", "w_q", "w_kv"]
    sums_b = pair_stage("b", group_b)
    (dx2, local["w_ple_gate0"], local["w_ple_proj0"], d_ple0, d_plepost0), got = _ple_bwd(
        0, dx3, x2, z0, pe0, p3d, gains["ple_g"], weight("w_ple_gate0"), gains["ple_post_g"],
        rider=scatter(group_b, sums_b))
    keep(group_b, sums_b, got)

    group_c = ["w_ple_gate0", "w_ple_proj0"]
    sums_c = pair_stage("c", group_c)
    (dx1, d_preffn0, d_postffn0, *scattered), got = _ffn_bwd(
        0, dx2, x1, f0, gains["pre_ffn_g"], weight("w_gu0"), weight("w_down0"), gains["post_ffn_g"],
        rider=scatter(group_c, sums_c))
    fused["w_gu0"], fused["w_down0"] = scattered[0:2], scattered[2:4]
    keep(group_c, sums_c, got)

    dx0, d_pool, d_scale, d_postmix0, d_premix0 = _mixa_bwd(
        dx1, x2d, y0, gains["pre_mix_g"], weight("pool_w"), weight("pool_scale"), gains["post_mix_g"])

    rows = [d_premix0, d_premix1, d_postmix0, d_postmix1, d_preffn0, d_preffn1, d_postffn0, d_postffn1,
            d_ple0, d_ple1, d_plepost0, d_plepost1, d_kvg, d_scale, d_sinks, loss_row]
    as2d = lambda a: a.reshape(1, D) if a.ndim == 1 else a
    tot, g_pool = _small_all_reduce(rows, d_pool)
    loss, small = _small_adamw(tot, kc, {nm: as2d(weights[nm]) for nm in SMALL_NAMES},
                               {nm: as2d(m_in[nm]) for nm in SMALL_NAMES},
                               {nm: as2d(v_in[nm]) for nm in SMALL_NAMES})

    shared = [src for src in BIG_SOURCES if src != "pool_w"]
    halves = []
    for src in shared:
        n_layers = shard_view(src, weights[src]).shape[0]
        acc = None
        for t in [t for t in BIGS.values() if t.src == src]:
            if t.name in fused:
                own, land = fused[t.name]
                acc = _chip_sum_fused(t, own, land, kc, n_layers, acc, by_cols=src == "w_down")
            else:
                s, l = landed[t.name]
                acc = _chip_sum(t, s, l, kc, n_layers, acc)
        halves.append(acc)
    full_grads = dict(zip(shared, _pair_share(halves, [src == "w_down" for src in shared])))
    full_grads["pool_w"] = g_pool

    out = {"grad": {}, "delta": {}, "new_m": {}, "new_v": {}}
    for src in BIG_SOURCES:
        g = full_grads[src]
        t = next(t for t in BIGS.values() if t.src == src)
        res = _adamw(src, t.rb, shard_view(src, weights[src]), g, shard_view(src, m_in[src]), shard_view(src, v_in[src]))
        shape = weights[src].shape
        for kind, a in zip(("grad", "delta", "new_m", "new_v"), res):
            out[kind][src] = a.reshape(shape)
    for nm in SMALL_NAMES:
        shape = weights[nm].shape
        for kind, a in zip(("grad", "delta", "new_m", "new_v"), small[nm]):
            out[kind][nm] = a.reshape(shape)

    return (loss.reshape(()), dx0.reshape(x.shape),
            *[out["grad"][nm] for nm in order], *[out["delta"][nm] for nm in order],
            *[out["new_m"][nm] for nm in order], *[out["new_v"][nm] for nm in order])
```

```python
import collections

import jax
import jax.numpy as jnp
from jax import lax
from jax.experimental import pallas as pl
from jax.experimental.pallas import tpu as pltpu

D = 1024
FF = 2816
N_HEADS = 16
HEAD_DIM = 64
N_KV_HEADS = 4
GQA = N_HEADS // N_KV_HEADS
KVD = N_KV_HEADS * HEAD_DIM
PLE = 256
BLK = 128
WINDOWS = (2, 4, 8, 16)
POOL_G = 256
HALO = 16
EPS = 1e-6
NEG_INF = -1e30
ATT_SCALE = HEAD_DIM ** -0.5
SLOPES = tuple(2.0 ** (-8.0 * (h + 1) / N_HEADS) for h in range(N_HEADS))
N_CHIPS = 4
N_DEV = 8

LR, B1, B2, AEPS, WD, STEP = 0.001, 0.9, 0.999, 1e-08, 0.01, 10
BC1 = 1.0 - B1 ** STEP
BC2 = 1.0 - B2 ** STEP

BF = jnp.bfloat16
F32 = jnp.float32
MESH = pl.DeviceIdType.MESH
VMEM_LIMIT_V7X = 58 * 1024 * 1024
TM = 256
TM_FFN_BWD = 512
FF_CHUNK = 256
FF_HALF = FF // 2

VSPEC = pl.BlockSpec(memory_space=pltpu.VMEM)
SSPEC = pl.BlockSpec(memory_space=pltpu.SMEM)
ANYSPEC = pl.BlockSpec(memory_space=pl.ANY)


def _params(n_grid=0):
    sem = ("arbitrary",) * n_grid if n_grid else None
    return pltpu.CompilerParams(dimension_semantics=sem, vmem_limit_bytes=VMEM_LIMIT_V7X)


def _sds(shape, dtype=F32):
    return jax.ShapeDtypeStruct(tuple(shape), dtype)


Rider = collections.namedtuple("Rider", "arrays out_shapes aliases scratch start mid finish")
MID_NUM, MID_DEN = 5, 8


def _call(body, *, name, grid, in_specs, out_specs, out_shape, args, scratch_shapes=(), rider=None, prefetch=None):
    ni, no, ns = len(in_specs), len(out_specs), len(scratch_shapes)
    npre = 0 if prefetch is None else 1
    pre = [] if prefetch is None else [prefetch]
    if rider is None:
        rider = Rider([], [], {}, [], None, None, None)
    ri, ro = len(rider.arrays), len(rider.out_shapes)

    def full(*refs):
        pre_refs, refs = refs[:npre], refs[npre:]
        ins, refs = refs[:ni], refs[ni:]
        rins, refs = refs[:ri], refs[ri:]
        outs, refs = refs[:no], refs[no:]
        routs, refs = refs[:ro], refs[ro:]
        scr, rscr = refs[:ns], refs[ns:]
        ids = [pl.program_id(a) for a in range(len(grid))]
        first = ids[0] == 0
        last = ids[0] == grid[0] - 1
        for a in range(1, len(grid)):
            first = first & (ids[a] == 0)
            last = last & (ids[a] == grid[a] - 1)

        if rider.start is not None:
            @pl.when(first)
            def _():
                rider.start(rins, routs, rscr)

        if rider.mid is not None:
            assert len(grid) == 1

            @pl.when(ids[0] == (grid[0] * MID_NUM) // MID_DEN)
            def _():
                rider.mid(rins, routs, rscr)

        body(*pre_refs, *ins, *outs, *scr)

        if rider.finish is not None:
            @pl.when(last)
            def _():
                rider.finish(rins, routs, rscr)

    outs = pl.pallas_call(
        full, name=name,
        grid_spec=pltpu.PrefetchScalarGridSpec(
            num_scalar_prefetch=npre, grid=grid,
            in_specs=list(in_specs) + [ANYSPEC] * ri, out_specs=list(out_specs) + [ANYSPEC] * ro,
            scratch_shapes=list(scratch_shapes) + list(rider.scratch)),
        out_shape=list(out_shape) + list(rider.out_shapes),
        input_output_aliases={npre + ni + a: no + b for a, b in rider.aliases.items()},
        compiler_params=_params(len(grid)))(*pre, *args, *rider.arrays)
    return list(outs[:no]), list(outs[no:])


def _run(name, rider):
    ri = len(rider.arrays)

    def body(*refs):
        rins, routs, rscr = refs[:ri], refs[ri:ri + len(rider.out_shapes)], refs[ri + len(rider.out_shapes):]
        rider.start(rins, routs, rscr)
        if rider.mid is not None:
            rider.mid(rins, routs, rscr)
        rider.finish(rins, routs, rscr)

    return pl.pallas_call(
        body, name=name, in_specs=[ANYSPEC] * ri, out_specs=[ANYSPEC] * len(rider.out_shapes),
        out_shape=list(rider.out_shapes), scratch_shapes=list(rider.scratch),
        input_output_aliases=dict(rider.aliases), compiler_params=_params())(*rider.arrays)


def _rms_fwd(x, g):
    r = lax.rsqrt(jnp.mean(x * x, axis=-1, keepdims=True) + EPS)
    return x * r * g


def _rms_bwd(x, g, dy):
    r = lax.rsqrt(jnp.mean(x * x, axis=-1, keepdims=True) + EPS)
    xn = x * r
    dxn = dy * g
    dx = r * (dxn - xn * jnp.mean(dxn * xn, axis=-1, keepdims=True))
    return dx, dy * xn


def _rowsum(a):
    return jnp.sum(a, axis=0, keepdims=True)


def _sigmoid(z):
    return 1.0 / (1.0 + jnp.exp(-z))


def _dot(a, b):
    return jnp.dot(a, b, preferred_element_type=F32)


def _dot_nt(a, b):
    return lax.dot_general(a, b, (((1,), (1,)), ((), ())), preferred_element_type=F32)


def _dot_tn(a, b):
    return lax.dot_general(a, b, (((0,), (0,)), ((), ())), preferred_element_type=F32)


def _row_spec(tm, width=D):
    return pl.BlockSpec((tm, width), lambda i: (i, 0))


def _const_spec(shape):
    zeros = (0,) * len(shape)
    return pl.BlockSpec(tuple(shape), lambda *_: zeros)


def _pool_delta(he, pos):
    out = []
    for gi, w in enumerate(WINDOWS):
        hg = he[:, gi * POOL_G:(gi + 1) * POOL_G]
        s = hg
        k = 1
        while k < w:
            s = s + pltpu.roll(s, k, 0)
            k *= 2
        cnt = jnp.maximum(jnp.minimum(pos + 1, w), 1).astype(F32)
        out.append(s / cnt - hg)
    return out


def _load_with_halo_before(x_ref, i, tm):
    r0 = pl.multiple_of(i * tm, tm)
    hs = pl.multiple_of(jnp.maximum(i * tm - HALO, 0), 8)
    xh = jnp.where(i > 0, x_ref[pl.ds(hs, HALO), :], 0.0)
    xt = x_ref[pl.ds(r0, tm), :]
    return xt, jnp.concatenate([xh, xt], axis=0)


def _mixa_fwd(x, pre_g, pool_w, pool_scale, post_g):
    s_len = x.shape[0]
    n = s_len // TM

    def body(x_ref, pg_ref, w_ref, sc_ref, qg_ref, y_ref, x1_ref):
        i = pl.program_id(0)
        xt, xe = _load_with_halo_before(x_ref, i, TM)
        he = _rms_fwd(xe, pg_ref[0:1, :])
        pos = i * TM - HALO + lax.broadcasted_iota(jnp.int32, (TM + HALO, 1), 0)
        ds = _pool_delta(he, pos)
        ys = [_dot(ds[gi][HALO:, :].astype(BF), w_ref[gi]) for gi in range(len(WINDOWS))]
        y = jnp.concatenate(ys, axis=1) * sc_ref[...]
        y_ref[...] = y
        x1_ref[...] = xt + _rms_fwd(y, qg_ref[0:1, :])

    outs, _ = _call(body, name="mixa_fwd", grid=(n,),
                    in_specs=[VSPEC] * 5, out_specs=[_row_spec(TM), _row_spec(TM)],
                    out_shape=[_sds((s_len, D)), _sds((s_len, D))],
                    args=[x, pre_g, pool_w, pool_scale, post_g])
    return outs


def _mixa_bwd(dx1, x, y, pre_g, pool_w, pool_scale, post_g):
    s_len = x.shape[0]
    n = s_len // TM
    ng = len(WINDOWS)

    def body(dx_ref, x_ref, y_ref, pg_ref, w_ref, sc_ref, qg_ref,
             dx0_ref, dw_ref, dsc_ref, dqg_ref, dpg_ref, wacc):
        i = pl.program_id(0)

        @pl.when(i == 0)
        def _():
            wacc[...] = jnp.zeros_like(wacc)
            dsc_ref[...] = jnp.zeros_like(dsc_ref)
            dqg_ref[...] = jnp.zeros_like(dqg_ref)
            dpg_ref[...] = jnp.zeros_like(dpg_ref)

        r0 = pl.multiple_of(i * TM, TM)
        xt, xe = _load_with_halo_before(x_ref, i, TM)
        he = _rms_fwd(xe, pg_ref[0:1, :])
        pos_b = i * TM - HALO + lax.broadcasted_iota(jnp.int32, (TM + HALO, 1), 0)
        ds = _pool_delta(he, pos_b)

        last = i == n - 1
        a0 = pl.multiple_of(jnp.minimum(i * TM + TM, s_len - HALO), 8)
        ye = jnp.concatenate([y_ref[pl.ds(r0, TM), :], y_ref[pl.ds(a0, HALO), :]], axis=0)
        dt = dx_ref[pl.ds(r0, TM), :]
        de = jnp.concatenate([dt, jnp.where(last, 0.0, dx_ref[pl.ds(a0, HALO), :])], axis=0)
        dye, prod = _rms_bwd(ye, qg_ref[0:1, :], de)
        dqg_ref[...] += _rowsum(prod[:TM, :])
        dys = dye * sc_ref[...]
        pos_a = i * TM + lax.broadcasted_iota(jnp.int32, (TM + HALO, 1), 0)

        dhs, dscs = [], []
        for gi, w in enumerate(WINDOWS):
            sl = slice(gi * POOL_G, (gi + 1) * POOL_G)
            wg = w_ref[gi]
            dys_g = dys[:, sl].astype(BF)
            d_g = ds[gi][HALO:, :].astype(BF)
            ypre = _dot(d_g, wg)
            dscs.append(_rowsum(dye[:TM, sl] * ypre))
            wacc[gi] += _dot_tn(d_g, dys_g[:TM, :])
            dd = _dot_nt(dys_g, wg)
            cnt = jnp.minimum(pos_a + 1, w).astype(F32)
            a = dd / cnt
            k = 1
            while k < w:
                a = a + pltpu.roll(a, TM + HALO - k, 0)
                k *= 2
            dhs.append(a[:TM, :] - dd[:TM, :])
        dsc_ref[...] += jnp.concatenate(dscs, axis=1)
        dh = jnp.concatenate(dhs, axis=1)
        dxp, prod2 = _rms_bwd(xt, pg_ref[0:1, :], dh)
        dpg_ref[...] += _rowsum(prod2)
        dx0_ref[...] = dt + dxp

        @pl.when(last)
        def _():
            dw_ref[...] = wacc[...].astype(BF)

    outs, _ = _call(
        body, name="mixa_bwd", grid=(n,), in_specs=[VSPEC] * 7,
        out_specs=[_row_spec(TM), _const_spec((ng, POOL_G, POOL_G)), _const_spec((1, D)),
                   _const_spec((1, D)), _const_spec((1, D))],
        out_shape=[_sds((s_len, D)), _sds((ng, POOL_G, POOL_G), BF), _sds((1, D)), _sds((1, D)), _sds((1, D))],
        scratch_shapes=[pltpu.VMEM((ng, POOL_G, POOL_G), F32)],
        args=[dx1, x, y, pre_g, pool_w, pool_scale, post_g])
    return outs


def _ffn_fwd(layer, x1, pre_g, wgu, wd, post_g, rider=None):
    s_len = x1.shape[0]

    def body(x_ref, pg_ref, wgu_ref, wd_ref, qg_ref, f_ref, x2_ref):
        x = x_ref[...]
        h = _rms_fwd(x, pg_ref[layer:layer + 1, :]).astype(BF)
        f = jnp.zeros((TM, D), F32)
        for c in range(FF // FF_HALF):
            cols = slice(c * FF_HALF, (c + 1) * FF_HALF)
            g = _dot(h, wgu_ref[0, :, cols])
            u = _dot(h, wgu_ref[1, :, cols])
            act = g * _sigmoid(g) * u
            f = f + _dot(act.astype(BF), wd_ref[cols, :])
        f_ref[...] = f
        x2_ref[...] = x + _rms_fwd(f, qg_ref[layer:layer + 1, :])

    return _call(body, name=f"ffn_fwd{layer}", grid=(s_len // TM,),
                 in_specs=[_row_spec(TM), VSPEC, VSPEC, VSPEC, VSPEC],
                 out_specs=[_row_spec(TM), _row_spec(TM)],
                 out_shape=[_sds((s_len, D)), _sds((s_len, D))],
                 args=[x1, pre_g, wgu, wd, post_g], rider=rider)


GU_PIECE = 128
DN_PIECE = 64
DN_SLOT = FF // N_CHIPS
HALF_D = D // 2


def _ffn_bwd(layer, dx2, x1, f, pre_g, wgu, wd, post_g, kc, rider=None):
    s_len = x1.shape[0]
    tm = TM_FFN_BWD
    n = s_len // tm
    nc = FF // FF_CHUNK
    n_gu, n_dn = FF_CHUNK // GU_PIECE, FF_CHUNK // DN_PIECE
    n_pieces = 2 * n_gu + n_dn

    def edge_rows(c, i, kc_ref):
        return (jnp.where((c == 0) | (c == nc - 1), i, n - 1), 0)

    def chunk_at(c, kc_ref):
        return (c + (kc_ref[0] * nc) // N_CHIPS) % nc

    def exchange(kc_ref, c, accg, accu, accd, own_gu_ref, land_gu_ref, own_dn_ref, land_dn_ref,
                 pl_gu, pl_dn, sib_gu, sib_dn, mine_gu, mine_dn, sum_gu, sum_dn,
                 psend, precv, ssend, lsem, rrecv):
        x, y, core = lax.axis_index("x"), lax.axis_index("y"), lax.axis_index("c")
        lower = core == 0

        def pair_copy(cc, part):
            p = cc % 2
            src, dst = ((sib_gu, pl_gu), (sib_dn, pl_dn))[part]
            return pltpu.make_async_remote_copy(src.at[p], dst.at[cc], psend.at[p, part], precv.at[cc, part],
                                                device_id=(x, y, 1 - core), device_id_type=MESH)

        def scatter(cc, wait):
            p = cc % 2
            jobs = []
            for gu in range(2):
                for hc in range(n_gu):
                    hidden = chunk_at(cc, kc_ref) * FF_CHUNK + hc * GU_PIECE
                    k = hidden // FF_HALF
                    off = pl.multiple_of(hidden - k * FF_HALF, GU_PIECE)
                    jobs.append((sum_gu.at[p, gu, :, pl.ds(hc * GU_PIECE, GU_PIECE)], k + 2 * gu, 0,
                                 own_gu_ref, land_gu_ref, (slice(None), pl.ds(off, GU_PIECE))))
            for q in range(n_dn):
                hidden = chunk_at(cc, kc_ref) * FF_CHUNK + q * DN_PIECE
                k = hidden // DN_SLOT
                off = pl.multiple_of(hidden - k * DN_SLOT, DN_PIECE)
                jobs.append((sum_dn.at[p, pl.ds(q * DN_PIECE, DN_PIECE), :], k, 1,
                             own_dn_ref, land_dn_ref, (pl.ds(off, DN_PIECE), slice(None))))
            for pi, (src, k, t, own_ref, land_ref, where) in enumerate(jobs):
                kx, ky = k // 2, k % 2
                fx, fy = (kx != x).astype(jnp.int32), (ky != y).astype(jnp.int32)
                local = (fx + fy) == 0
                j = jnp.maximum(fx + 2 * fy - 1, 0)

                @pl.when(local)
                def _():
                    if not wait:
                        cp = pltpu.make_async_copy(src, own_ref.at[where], lsem.at[p, pi])
                        cp.start()
                        cp.wait()

                @pl.when(jnp.logical_not(local))
                def _():
                    cp = pltpu.make_async_remote_copy(src, land_ref.at[(j,) + where], ssend.at[p, pi],
                                                      rrecv.at[t, j], device_id=(kx, ky, core), device_id_type=MESH)
                    if wait:
                        cp.wait_send()
                    else:
                        cp.start()

        def add_and_scatter(cc):
            p = cc % 2
            pair_copy(cc, 0).wait_recv()
            pair_copy(cc, 1).wait_recv()
            sum_gu[p] = (mine_gu[...] + pl_gu[cc].astype(F32)).astype(BF)
            sum_dn[p] = (mine_dn[...] + pl_dn[cc].astype(F32)).astype(BF)
            scatter(cc, wait=False)

        @pl.when(c >= 1)
        def _():
            @pl.when(c >= 3)
            def _():
                scatter(c - 3, wait=True)
            add_and_scatter(c - 1)

        @pl.when(c >= 2)
        def _():
            pair_copy(c - 2, 0).wait_send()
            pair_copy(c - 2, 1).wait_send()

        p = c % 2
        g_v, u_v, d_v = accg[...], accu[...], accd[...]
        sib_gu[p, 0] = jnp.where(lower, g_v[HALF_D:, :], g_v[:HALF_D, :]).astype(BF)
        sib_gu[p, 1] = jnp.where(lower, u_v[HALF_D:, :], u_v[:HALF_D, :]).astype(BF)
        sib_dn[p] = jnp.where(lower, d_v[:, HALF_D:], d_v[:, :HALF_D]).astype(BF)
        mine_gu[0] = jnp.where(lower, g_v[:HALF_D, :], g_v[HALF_D:, :])
        mine_gu[1] = jnp.where(lower, u_v[:HALF_D, :], u_v[HALF_D:, :])
        mine_dn[...] = jnp.where(lower, d_v[:, :HALF_D], d_v[:, HALF_D:])
        pair_copy(c, 0).start()
        pair_copy(c, 1).start()

        @pl.when(c == nc - 1)
        def _():
            scatter(nc - 3, wait=True)
            add_and_scatter(nc - 1)
            for cc in (nc - 2, nc - 1):
                pair_copy(cc, 0).wait_send()
                pair_copy(cc, 1).wait_send()
                scatter(cc, wait=True)
            for t, land_ref in enumerate((land_gu_ref, land_dn_ref)):
                for j in range(N_CHIPS - 1):
                    pltpu.make_async_remote_copy(land_ref.at[j], land_ref.at[j], ssend.at[0, 0], rrecv.at[t, j],
                                                 device_id=(x, y, core), device_id_type=MESH).wait_recv()

    def body(kc_ref, dx_ref, x_ref, f_ref, pg_ref, wgu_ref, wd_ref, qg_ref,
             dx1_ref, dpg_ref, dqg_ref, own_gu_ref, land_gu_ref, own_dn_ref, land_dn_ref,
             h_s, df_s, dh_s, accg, accu, accd, *comm):
        c = pl.program_id(0)
        i = pl.program_id(1)
        rows = pl.ds(pl.multiple_of(i * tm, tm), tm)
        pg = pg_ref[layer:layer + 1, :]

        @pl.when((c == 0) & (i == 0))
        def _():
            dpg_ref[...] = jnp.zeros_like(dpg_ref)
            dqg_ref[...] = jnp.zeros_like(dqg_ref)

        @pl.when(c == 0)
        def _():
            h_s[rows, :] = _rms_fwd(x_ref[...], pg).astype(BF)
            df, prod = _rms_bwd(f_ref[...], qg_ref[layer:layer + 1, :], dx_ref[...])
            df_s[rows, :] = df.astype(BF)
            dqg_ref[...] += _rowsum(prod)

        @pl.when(i == 0)
        def _():
            accg[...] = jnp.zeros_like(accg)
            accu[...] = jnp.zeros_like(accu)
            accd[...] = jnp.zeros_like(accd)

        h = h_s[rows, :]
        df = df_s[rows, :]
        wg = wgu_ref[0]
        wu = wgu_ref[1]
        g = _dot(h, wg)
        u = _dot(h, wu)
        sg = _sigmoid(g)
        a = g * sg
        dact = _dot_nt(df, wd_ref[...])
        accd[...] += _dot_tn((a * u).astype(BF), df)
        du = (dact * a).astype(BF)
        dg = (dact * u * (sg * (1.0 + g * (1.0 - sg)))).astype(BF)
        accg[...] += _dot_tn(h, dg)
        accu[...] += _dot_tn(h, du)
        dh = _dot_nt(dg, wg) + _dot_nt(du, wu)

        @pl.when(c == 0)
        def _():
            dh_s[rows, :] = dh

        @pl.when((c > 0) & (c < nc - 1))
        def _():
            dh_s[rows, :] += dh

        @pl.when(c == nc - 1)
        def _():
            dxp, prod = _rms_bwd(x_ref[...], pg, dh_s[rows, :] + dh)
            dpg_ref[...] += _rowsum(prod)
            dx1_ref[...] = dx_ref[...] + dxp

        @pl.when(i == n - 1)
        def _():
            exchange(kc_ref, c, accg, accu, accd, own_gu_ref, land_gu_ref, own_dn_ref, land_dn_ref, *comm)

    dma = pltpu.SemaphoreType.DMA
    return _call(
        body, name=f"ffn_bwd{layer}", grid=(nc, n),
        in_specs=[pl.BlockSpec((tm, D), edge_rows), pl.BlockSpec((tm, D), edge_rows),
                  pl.BlockSpec((tm, D), lambda c, i, kc_ref: (jnp.where(c == 0, i, n - 1), 0),
                               pipeline_mode=pl.Buffered(1)),
                  VSPEC,
                  pl.BlockSpec((2, D, FF_CHUNK), lambda c, i, kc_ref: (0, 0, chunk_at(c, kc_ref))),
                  pl.BlockSpec((FF_CHUNK, D), lambda c, i, kc_ref: (chunk_at(c, kc_ref), 0)),
                  VSPEC],
        out_specs=[pl.BlockSpec((tm, D), lambda c, i, kc_ref: (jnp.where(c == nc - 1, i, 0), 0)),
                   _const_spec((1, D)), _const_spec((1, D)), ANYSPEC, ANYSPEC, ANYSPEC, ANYSPEC],
        out_shape=[_sds((s_len, D)), _sds((1, D)), _sds((1, D)),
                   _sds((HALF_D, FF_HALF), BF), _sds((N_CHIPS - 1, HALF_D, FF_HALF), BF),
                   _sds((DN_SLOT, HALF_D), BF), _sds((N_CHIPS - 1, DN_SLOT, HALF_D), BF)],
        scratch_shapes=[pltpu.VMEM((s_len, D), BF), pltpu.VMEM((s_len, D), BF), pltpu.VMEM((s_len, D), F32),
                        pltpu.VMEM((D, FF_CHUNK), F32), pltpu.VMEM((D, FF_CHUNK), F32),
                        pltpu.VMEM((FF_CHUNK, D), F32),
                        pltpu.VMEM((nc, 2, HALF_D, FF_CHUNK), BF), pltpu.VMEM((nc, FF_CHUNK, HALF_D), BF),
                        pltpu.VMEM((2, 2, HALF_D, FF_CHUNK), BF), pltpu.VMEM((2, FF_CHUNK, HALF_D), BF),
                        pltpu.VMEM((2, HALF_D, FF_CHUNK), F32), pltpu.VMEM((FF_CHUNK, HALF_D), F32),
                        pltpu.VMEM((2, 2, HALF_D, FF_CHUNK), BF), pltpu.VMEM((2, FF_CHUNK, HALF_D), BF),
                        dma((2, 2)), dma((nc, 2)), dma((2, n_pieces)), dma((2, n_pieces)), dma((2, N_CHIPS - 1))],
        args=[dx2, x1, f, pre_g, wgu, wd, post_g], rider=rider, prefetch=kc)


def _ple_fwd(layer, x2, p, ple_g, w_gate, w_proj, post_g, target=None, rider=None):
    s_len = x2.shape[0]
    final = target is not None

    def body(*refs):
        if final:
            x_ref, p_ref, g_ref, wg_ref, wp_ref, qg_ref, t_ref, z_ref, pe_ref, dx_ref, lv_ref = refs
        else:
            x_ref, p_ref, g_ref, wg_ref, wp_ref, qg_ref, z_ref, pe_ref, x3_ref = refs
        x = x_ref[...]
        r = _rms_fwd(x, g_ref[layer:layer + 1, :]).astype(BF)
        z = _dot(r, wg_ref[...])
        pe = _dot(p_ref[...].astype(BF), wp_ref[...])
        z_ref[...] = z
        pe_ref[...] = pe
        x3 = x + _rms_fwd(pe * _sigmoid(z), qg_ref[layer:layer + 1, :])
        if final:
            @pl.when(pl.program_id(0) == 0)
            def _():
                lv_ref[...] = jnp.zeros_like(lv_ref)
            err = x3 - t_ref[...]
            dx_ref[...] = err * (1.0 / D)
            lv_ref[...] += _rowsum(err * err)
        else:
            x3_ref[...] = x3

    p_spec = pl.BlockSpec((None, TM, PLE), lambda i: (layer, i, 0))
    in_specs = [_row_spec(TM), p_spec, VSPEC, VSPEC, VSPEC, VSPEC]
    args = [x2, p, ple_g, w_gate, w_proj, post_g]
    out_specs = [_row_spec(TM), _row_spec(TM), _row_spec(TM)]
    out_shape = [_sds((s_len, D))] * 3
    if final:
        in_specs.append(_row_spec(TM))
        args.append(target)
        out_specs.append(_const_spec((1, D)))
        out_shape.append(_sds((1, D)))
    return _call(body, name=f"ple_fwd{layer}", grid=(s_len // TM,), in_specs=in_specs, out_specs=out_specs,
                 out_shape=out_shape, args=args, rider=rider)


def _ple_bwd(layer, dx3, x2, z, pe, p, ple_g, w_gate, post_g, rider=None):
    s_len = x2.shape[0]
    n = s_len // TM

    def body(dx_ref, x_ref, z_ref, pe_ref, p_ref, g_ref, wg_ref, qg_ref,
             dx2_ref, dwg_ref, dwp_ref, dg_ref, dqg_ref, gacc, pacc):
        i = pl.program_id(0)

        @pl.when(i == 0)
        def _():
            gacc[...] = jnp.zeros_like(gacc)
            pacc[...] = jnp.zeros_like(pacc)
            dg_ref[...] = jnp.zeros_like(dg_ref)
            dqg_ref[...] = jnp.zeros_like(dqg_ref)

        dx = dx_ref[...]
        x = x_ref[...]
        pe_v = pe_ref[...]
        gate = _sigmoid(z_ref[...])
        de, prod = _rms_bwd(pe_v * gate, qg_ref[layer:layer + 1, :], dx)
        dqg_ref[...] += _rowsum(prod)
        dpe = (de * gate).astype(BF)
        dz = (de * pe_v * gate * (1.0 - gate)).astype(BF)
        pacc[...] += _dot_tn(p_ref[...].astype(BF), dpe)
        g = g_ref[layer:layer + 1, :]
        r = _rms_fwd(x, g).astype(BF)
        gacc[...] += _dot_tn(r, dz)
        dr = _dot_nt(dz, wg_ref[...])
        dxp, prod2 = _rms_bwd(x, g, dr)
        dg_ref[...] += _rowsum(prod2)
        dx2_ref[...] = dx + dxp

        @pl.when(i == n - 1)
        def _():
            dwg_ref[...] = gacc[...].astype(BF)
            dwp_ref[...] = pacc[...].astype(BF)

    p_spec = pl.BlockSpec((None, TM, PLE), lambda i: (layer, i, 0))
    return _call(
        body, name=f"ple_bwd{layer}", grid=(n,),
        in_specs=[_row_spec(TM), _row_spec(TM), _row_spec(TM), _row_spec(TM), p_spec, VSPEC, VSPEC, VSPEC],
        out_specs=[_row_spec(TM), _const_spec((D, D)), _const_spec((PLE, D)), _const_spec((1, D)), _const_spec((1, D))],
        out_shape=[_sds((s_len, D)), _sds((D, D), BF), _sds((PLE, D), BF), _sds((1, D)), _sds((1, D))],
        scratch_shapes=[pltpu.VMEM((D, D), F32), pltpu.VMEM((PLE, D), F32)],
        args=[dx3, x2, z, pe, p, ple_g, w_gate, post_g], rider=rider)


def _qkv_fwd(x3, q_g, kv_g, w_q, w_kv, rider=None):
    s_len = x3.shape[0]

    def body(x_ref, qg_ref, kg_ref, wq_ref, wkv_ref, q_ref, kv_ref):
        x = x_ref[...]
        q_ref[...] = _dot(_rms_fwd(x, qg_ref[1:2, :]).astype(BF), wq_ref[...]).astype(BF)
        kv_ref[...] = _dot(_rms_fwd(x, kg_ref[...]).astype(BF), wkv_ref[...]).astype(BF)

    return _call(body, name="qkv_fwd", grid=(s_len // TM,),
                 in_specs=[_row_spec(TM), VSPEC, VSPEC, VSPEC, VSPEC],
                 out_specs=[_row_spec(TM), _row_spec(TM, 2 * KVD)],
                 out_shape=[_sds((s_len, D), BF), _sds((s_len, 2 * KVD), BF)],
                 args=[x3, q_g, kv_g, w_q, w_kv], rider=rider)


def _qkv_bwd(dq, dkv, x3, dx4, q_g, kv_g, w_q, w_kv):
    s_len = x3.shape[0]
    n = s_len // TM

    def body(dq_ref, dkv_ref, x_ref, dx_ref, qg_ref, kg_ref, wq_ref, wkv_ref,
             dx3_ref, dwq_ref, dwkv_ref, dqg_ref, dkg_ref, qacc, kacc):
        i = pl.program_id(0)

        @pl.when(i == 0)
        def _():
            qacc[...] = jnp.zeros_like(qacc)
            kacc[...] = jnp.zeros_like(kacc)
            dqg_ref[...] = jnp.zeros_like(dqg_ref)
            dkg_ref[...] = jnp.zeros_like(dkg_ref)

        x = x_ref[...]
        qg = qg_ref[1:2, :]
        kg = kg_ref[...]
        dq_v = dq_ref[...]
        dkv_v = dkv_ref[...].astype(BF)
        qacc[...] += _dot_tn(_rms_fwd(x, qg).astype(BF), dq_v)
        kacc[...] += _dot_tn(_rms_fwd(x, kg).astype(BF), dkv_v)
        dxq, prod_q = _rms_bwd(x, qg, _dot_nt(dq_v, wq_ref[...]))
        dxk, prod_k = _rms_bwd(x, kg, _dot_nt(dkv_v, wkv_ref[...]))
        dqg_ref[...] += _rowsum(prod_q)
        dkg_ref[...] += _rowsum(prod_k)
        dx3_ref[...] = dx_ref[...] + dxq + dxk

        @pl.when(i == n - 1)
        def _():
            dwq_ref[...] = qacc[...].astype(BF)
            dwkv_ref[...] = kacc[...].astype(BF)

    outs, _ = _call(
        body, name="qkv_bwd", grid=(n,),
        in_specs=[_row_spec(TM), _row_spec(TM, 2 * KVD), _row_spec(TM), _row_spec(TM), VSPEC, VSPEC, VSPEC, VSPEC],
        out_specs=[_row_spec(TM), _const_spec((D, D)), _const_spec((D, 2 * KVD)),
                   _const_spec((1, D)), _const_spec((1, D))],
        out_shape=[_sds((s_len, D)), _sds((D, D), BF), _sds((D, 2 * KVD), BF), _sds((1, D)), _sds((1, D))],
        scratch_shapes=[pltpu.VMEM((D, D), F32), pltpu.VMEM((D, 2 * KVD), F32)],
        args=[dq, dkv, x3, dx4, q_g, kv_g, w_q, w_kv])
    return outs


def _attn_block(i, q, kvw, sink_ref):
    off = jnp.where(i > 0, BLK, 0)
    rel = (lax.broadcasted_iota(jnp.int32, (BLK, 2 * BLK), 0)
           - lax.broadcasted_iota(jnp.int32, (BLK, 2 * BLK), 1) + off)
    valid = (rel >= 0) & (rel < BLK)
    relf = rel.astype(F32)
    out = []
    for h in range(N_HEADS):
        kh = h // GQA
        qh = q[:, h * HEAD_DIM:(h + 1) * HEAD_DIM]
        k = kvw[:, kh * HEAD_DIM:(kh + 1) * HEAD_DIM]
        v = kvw[:, KVD + kh * HEAD_DIM:KVD + (kh + 1) * HEAD_DIM]
        s = _dot_nt(qh, k) * ATT_SCALE - SLOPES[h] * relf
        s = jnp.where(valid, s, NEG_INF)
        sink = sink_ref[0, h]
        m = jnp.maximum(jnp.max(s, axis=-1, keepdims=True), sink)
        e = jnp.exp(s - m)
        es = jnp.exp(sink - m)
        inv = 1.0 / (jnp.sum(e, axis=-1, keepdims=True) + es)
        out.append((e * inv, es * inv, qh, k, v))
    return out


def _kv_window(kv_ref, i):
    ks = pl.multiple_of(jnp.maximum(i * BLK - BLK, 0), BLK)
    return ks, kv_ref[pl.ds(ks, 2 * BLK), :]


def _attn_fwd(q, kv, sinks, x3, w_o, post_g, rider=None):
    s_len = q.shape[0]

    def body(q_ref, kv_ref, sk_ref, x_ref, wo_ref, g_ref, a_ref, y_ref, x4_ref):
        i = pl.program_id(0)
        _, kvw = _kv_window(kv_ref, i)
        heads = _attn_block(i, q_ref[...], kvw, sk_ref)
        attn = jnp.concatenate([_dot(p.astype(BF), v) for p, _, _, _, v in heads], axis=1)
        a_ref[...] = attn
        y = _dot(attn.astype(BF), wo_ref[...])
        y_ref[...] = y
        x4_ref[...] = x_ref[...] + _rms_fwd(y, g_ref[1:2, :])

    return _call(body, name="attn_fwd", grid=(s_len // BLK,),
                 in_specs=[_row_spec(BLK), VSPEC, SSPEC, _row_spec(BLK), VSPEC, VSPEC],
                 out_specs=[_row_spec(BLK)] * 3, out_shape=[_sds((s_len, D))] * 3,
                 args=[q, kv, sinks, x3, w_o, post_g], rider=rider)


def _attn_bwd(dx4, y, attn, q, kv, sinks, w_o, post_g, rider=None):
    s_len = q.shape[0]
    n = s_len // BLK

    def body(dx_ref, y_ref, a_ref, q_ref, kv_ref, sk_ref, wo_ref, g_ref,
             dq_ref, dkv_ref, dwo_ref, dg_ref, dsk_ref, wacc):
        i = pl.program_id(0)

        @pl.when(i == 0)
        def _():
            dkv_ref[...] = jnp.zeros_like(dkv_ref)
            wacc[...] = jnp.zeros_like(wacc)
            dg_ref[...] = jnp.zeros_like(dg_ref)
            dsk_ref[...] = jnp.zeros_like(dsk_ref)

        dy, prod = _rms_bwd(y_ref[...], g_ref[1:2, :], dx_ref[...])
        dg_ref[...] += _rowsum(prod)
        dyb = dy.astype(BF)
        attn = a_ref[...]
        wacc[...] += _dot_tn(attn.astype(BF), dyb)
        d_o = _dot_nt(dyb, wo_ref[...])
        dod = d_o * attn
        ks, kvw = _kv_window(kv_ref, i)
        heads = _attn_block(i, q_ref[...], kvw, sk_ref)
        lane = lax.broadcasted_iota(jnp.int32, (1, D), 1)
        dqs = []
        dks = [None] * N_KV_HEADS
        dvs = [None] * N_KV_HEADS
        dsk = jnp.zeros((1, D), F32)
        for h, (p, ps, qh, k, v) in enumerate(heads):
            kh = h // GQA
            hs = slice(h * HEAD_DIM, (h + 1) * HEAD_DIM)
            do_h = d_o[:, hs].astype(BF)
            dsum = jnp.sum(dod[:, hs], axis=-1, keepdims=True)
            dp = _dot_nt(do_h, v)
            dsb = (p * (dp - dsum) * ATT_SCALE).astype(BF)
            dsk = dsk + jnp.where(lane == h, -_rowsum(ps * dsum), 0.0)
            dqs.append(_dot(dsb, k))
            dk = _dot_tn(dsb, qh)
            dv = _dot_tn(p.astype(BF), do_h)
            dks[kh] = dk if dks[kh] is None else dks[kh] + dk
            dvs[kh] = dv if dvs[kh] is None else dvs[kh] + dv
        dsk_ref[...] += dsk
        dq_ref[...] = jnp.concatenate(dqs, axis=1).astype(BF)
        dkv_ref[pl.ds(ks, 2 * BLK), :] += jnp.concatenate(dks + dvs, axis=1)

        @pl.when(i == n - 1)
        def _():
            dwo_ref[...] = wacc[...].astype(BF)

    return _call(
        body, name="attn_bwd", grid=(n,),
        in_specs=[_row_spec(BLK), _row_spec(BLK), _row_spec(BLK), _row_spec(BLK), VSPEC, SSPEC, VSPEC, VSPEC],
        out_specs=[_row_spec(BLK), _const_spec((s_len, 2 * KVD)), _const_spec((D, D)),
                   _const_spec((1, D)), _const_spec((1, D))],
        out_shape=[_sds((s_len, D), BF), _sds((s_len, 2 * KVD)), _sds((D, D), BF), _sds((1, D)), _sds((1, D))],
        scratch_shapes=[pltpu.VMEM((D, D), F32)],
        args=[dx4, y, attn, q, kv, sinks, w_o, post_g], rider=rider)


Big = collections.namedtuple("Big", "name src layer L A R C rb")


def _bigs():
    out = {"pool_w": Big("pool_w", "pool_w", None, 4, 4, POOL_G // N_CHIPS, POOL_G, 32)}
    for l in range(2):
        out[f"w_gu{l}"] = Big(f"w_gu{l}", "w_gu", l, 1, 2, D, FF_HALF, 256)
        out[f"w_down{l}"] = Big(f"w_down{l}", "w_down", l, 1, 4, FF // N_CHIPS, D, 352)
        out[f"w_ple_gate{l}"] = Big(f"w_ple_gate{l}", "w_ple_gate", l, 1, 4, D // N_CHIPS, D, 128)
        out[f"w_ple_proj{l}"] = Big(f"w_ple_proj{l}", "w_ple_proj", l, 1, 1, PLE, D // N_CHIPS, 128)
    out["w_q"] = Big("w_q", "w_q", None, 1, 4, D // N_CHIPS, D, 128)
    out["w_o"] = Big("w_o", "w_o", None, 1, 4, D // N_CHIPS, D, 128)
    out["w_kv"] = Big("w_kv", "w_kv", None, 1, 4, D // N_CHIPS, 2 * KVD, 128)
    return out


BIGS = _bigs()
POOL_SCALE = Big("pool_scale", "pool_scale", None, 1, 1, 1, D // N_CHIPS, 1)
BIG_SOURCES = ("w_gu", "w_down", "w_ple_gate", "w_ple_proj", "w_q", "w_o", "w_kv", "pool_w")


def _ncb(t):
    return N_CHIPS // t.A


def _full_shape(t, rows=None):
    return (t.L, t.A, t.R if rows is None else rows, _ncb(t) * t.C)


def _slot_index(t, k):
    return k // _ncb(t), k % _ncb(t)


def _slot(ref, t, k, row0, rows):
    a, cb = _slot_index(t, k)
    return ref.at[:, a, pl.ds(row0, rows), pl.ds(pl.multiple_of(cb * t.C, 128), t.C)]


def _place(t, w, kc, out_dtype):
    rb = min(t.R, 2 * t.rb)

    def body(kc_ref, w_ref, o_ref):
        del kc_ref
        o_ref[...] = w_ref[...].astype(out_dtype)

    def in_map(l, j, kc_ref):
        return (l if t.layer is None else t.layer, j, 0)

    def out_map(l, j, kc_ref):
        a, cb = _slot_index(t, kc_ref[0])
        return (l, a, j, cb)

    return pl.pallas_call(
        body, name=f"place_{t.name}",
        grid_spec=pltpu.PrefetchScalarGridSpec(
            num_scalar_prefetch=1, grid=(t.L, t.R // rb),
            in_specs=[pl.BlockSpec((None, rb, t.C), in_map)],
            out_specs=pl.BlockSpec((None, None, rb, t.C), out_map)),
        out_shape=_sds(_full_shape(t), out_dtype),
        compiler_params=_params(2),
    )(kc, w)


def _mesh_position():
    x, y, c = lax.axis_index("x"), lax.axis_index("y"), lax.axis_index("c")
    chips = [(1 - x, y), (x, 1 - y), (1 - x, 1 - y)]
    return x, y, c, chips


def _gather_rider(parts, fulls):
    nt = len(parts)
    TO_X, TO_Y, FWD_X, FWD_Y, SIB_X, SIB_Y, SIB_D = range(7)

    def rows_of(ti, core):
        t, r0, r1 = parts[ti]
        h = (r1 - r0) // 2
        return r0 + core * h, h

    def copy(outs, sems, kind, ti, k_src, row0, rows, dev):
        region = _slot(outs[ti], parts[ti][0], k_src, row0, rows)
        return pltpu.make_async_remote_copy(region, region, sems[0].at[ti, kind], sems[1].at[ti, kind],
                                            device_id=dev, device_id_type=MESH)

    def plan(outs, sems):
        x, y, c, _ = _mesh_position()
        me, kx, ky, kd = 2 * x + y, 2 * (1 - x) + y, 2 * x + (1 - y), 2 * (1 - x) + (1 - y)
        dev_x, dev_y, dev_d, sib = (1 - x, y, c), (x, 1 - y, c), (1 - x, 1 - y, c), (x, y, 1 - c)

        def whole(ti):
            return 0, parts[ti][0].R

        def mk(kind, k_send, k_recv, dev, send_rows, recv_rows):
            def build(ti, side):
                k_src = k_send if side == "s" else k_recv
                row0, rows = (send_rows if side == "s" else recv_rows)(ti)
                return copy(outs, sems, kind, ti, k_src, row0, rows, dev)
            return build

        def first_half(core):
            return lambda ti: (rows_of(ti, core)[0], rows_of(ti, core)[1] // 2)

        def second_half(core):
            return lambda ti: (rows_of(ti, core)[0] + rows_of(ti, core)[1] // 2, rows_of(ti, core)[1] // 2)

        mine = lambda ti: rows_of(ti, c)
        theirs = lambda ti: rows_of(ti, 1 - c)
        split = {
            TO_X: mk(TO_X, me, kx, dev_x, mine, mine),
            TO_Y: mk(TO_Y, me, ky, dev_y, mine, mine),
            FWD_X: mk(FWD_X, ky, kd, dev_x, first_half(c), first_half(c)),
            FWD_Y: mk(FWD_Y, kx, kd, dev_y, second_half(c), second_half(c)),
            SIB_X: mk(SIB_X, kx, kx, sib, mine, theirs),
            SIB_Y: mk(SIB_Y, ky, ky, sib, mine, theirs),
            SIB_D: mk(SIB_D, kd, kd, sib, mine, theirs),
        }
        direct = {
            TO_X: mk(TO_X, me, kx, dev_x, whole, whole),
            TO_Y: mk(TO_Y, me, ky, dev_y, whole, whole),
            FWD_X: mk(FWD_X, me, kd, dev_d, whole, whole),
        }
        return split, direct

    is_split = [t.R > 1 for t, _, _ in parts]

    def start(ins, outs, sems):
        split, direct = plan(outs, sems)
        for ti in range(nt):
            kinds = split if is_split[ti] else direct
            kinds[TO_X](ti, "s").start()
            kinds[TO_Y](ti, "s").start()
            if not is_split[ti]:
                kinds[FWD_X](ti, "s").start()

    def mid(ins, outs, sems):
        split, _ = plan(outs, sems)
        for ti in range(nt):
            if is_split[ti]:
                split[TO_Y](ti, "r").wait_recv()
                split[FWD_X](ti, "s").start()
                split[SIB_Y](ti, "s").start()
        for ti in range(nt):
            if is_split[ti]:
                split[TO_X](ti, "r").wait_recv()
                split[FWD_Y](ti, "s").start()
                split[SIB_X](ti, "s").start()

    def finish(ins, outs, sems):
        split, direct = plan(outs, sems)
        for ti in range(nt):
            if is_split[ti]:
                split[FWD_X](ti, "r").wait_recv()
                split[FWD_Y](ti, "r").wait_recv()
                split[SIB_D](ti, "s").start()
            else:
                for kind in (TO_X, TO_Y, FWD_X):
                    direct[kind](ti, "r").wait_recv()
        for ti in range(nt):
            if is_split[ti]:
                for kind in (SIB_X, SIB_Y, SIB_D):
                    split[kind](ti, "r").wait_recv()
        for ti in range(nt):
            kinds = split if is_split[ti] else direct
            for kind in kinds:
                kinds[kind](ti, "s").wait_send()

    sems = pltpu.SemaphoreType.DMA((nt, 7))
    return Rider(list(fulls), [_sds(a.shape, a.dtype) for a in fulls], {i: i for i in range(nt)},
                 [sems, sems], start, mid, finish)


def _pair_exchange(name, specs, grads):
    nt = len(specs)

    def body(*refs):
        gs = refs[:nt]
        lands = refs[nt:2 * nt]
        send, recv = refs[2 * nt:]
        x, y, c, _ = _mesh_position()
        cps = []
        for ti, t in enumerate(specs):
            half = t.R // 2
            cp = pltpu.make_async_remote_copy(gs[ti].at[:, :, pl.ds((1 - c) * half, half), :], lands[ti],
                                              send.at[ti], recv.at[ti],
                                              device_id=(x, y, 1 - c), device_id_type=MESH)
            cp.start()
            cps.append(cp)
        for cp in cps:
            cp.wait()

    return pl.pallas_call(
        body, name=name,
        in_specs=[ANYSPEC] * nt, out_specs=[ANYSPEC] * nt,
        out_shape=[_sds(_full_shape(t, t.R // 2), BF) for t in specs],
        scratch_shapes=[pltpu.SemaphoreType.DMA((nt,)), pltpu.SemaphoreType.DMA((nt,))],
        compiler_params=_params(),
    )(*grads)


def _pair_sum(t, g, land, kc):
    half = t.R // 2
    nj = half // t.rb
    w = _ncb(t) * t.C

    def body(kc_ref, g_ref, l_ref, o_ref):
        del kc_ref
        o_ref[...] = (g_ref[...].astype(F32) + l_ref[...].astype(F32)).astype(BF)

    return pl.pallas_call(
        body, name=f"pair_sum_{t.name}",
        grid_spec=pltpu.PrefetchScalarGridSpec(
            num_scalar_prefetch=1, grid=(t.L, nj),
            in_specs=[pl.BlockSpec((None, t.A, t.rb, w), lambda l, j, kc_ref: (l, 0, kc_ref[1] * nj + j, 0)),
                      pl.BlockSpec((None, t.A, t.rb, w), lambda l, j, kc_ref: (l, 0, j, 0))],
            out_specs=pl.BlockSpec((None, t.A, t.rb, w), lambda l, j, kc_ref: (l, 0, j, 0))),
        out_shape=_sds(_full_shape(t, half), BF),
        compiler_params=_params(2),
    )(kc, g, land)


def _scatter_rider(specs, sums):
    nt = len(specs)

    def copy(ins, outs, sems, ti, j, chip, c):
        t = specs[ti]
        cx, cy = chip
        return pltpu.make_async_remote_copy(_slot(ins[ti], t, 2 * cx + cy, 0, t.R // 2), outs[ti].at[j],
                                            sems[0].at[ti, j], sems[1].at[ti, j],
                                            device_id=(cx, cy, c), device_id_type=MESH)

    def start(ins, outs, sems):
        _, _, c, chips = _mesh_position()
        for j, chip in enumerate(chips):
            for ti in range(nt):
                copy(ins, outs, sems, ti, j, chip, c).start()

    def finish(ins, outs, sems):
        _, _, c, chips = _mesh_position()
        for j, chip in enumerate(chips):
            for ti in range(nt):
                copy(ins, outs, sems, ti, j, chip, c).wait()

    sems = pltpu.SemaphoreType.DMA((nt, N_CHIPS - 1))
    return Rider(list(sums), [_sds((N_CHIPS - 1, t.L, t.R // 2, t.C), BF) for t in specs], {}, [sems, sems],
                 start, None, finish)


def _chip_sum(t, s, land, kc, n_layers, prev):
    half = t.R // 2
    nj = half // t.rb

    def body(*refs):
        s_ref, l_ref, o_ref = refs[1], refs[2], refs[-1]
        acc = s_ref[...].astype(F32)
        for j in range(N_CHIPS - 1):
            acc = acc + l_ref[j].astype(F32)
        o_ref[...] = acc

    def own_map(l, j, kc_ref):
        a, cb = _slot_index(t, kc_ref[0])
        return (l, a, j, cb)

    def out_map(l, j, kc_ref):
        return (l if t.layer is None else t.layer, kc_ref[1] * nj + j, 0)

    in_specs = [pl.BlockSpec((None, None, t.rb, t.C), own_map),
                pl.BlockSpec((N_CHIPS - 1, None, t.rb, t.C), lambda l, j, kc_ref: (0, l, j, 0))]
    args = [kc, s, land]
    aliases = {}
    if prev is not None:
        in_specs.append(ANYSPEC)
        args.append(prev)
        aliases = {3: 0}
    return pl.pallas_call(
        body, name=f"chip_sum_{t.name}",
        grid_spec=pltpu.PrefetchScalarGridSpec(
            num_scalar_prefetch=1, grid=(t.L, nj), in_specs=in_specs,
            out_specs=pl.BlockSpec((None, t.rb, t.C), out_map)),
        out_shape=_sds((n_layers, t.R, t.C)),
        input_output_aliases=aliases,
        compiler_params=_params(2),
    )(*args)


def _chip_sum_fused(t, own, land, kc, n_layers, prev, by_cols):
    rows, cols = own.shape
    nj = rows // t.rb

    def body(*refs):
        o_ref, l_ref, out_ref = refs[1], refs[2], refs[-1]
        acc = o_ref[...].astype(F32)
        for j in range(N_CHIPS - 1):
            acc = acc + l_ref[j].astype(F32)
        out_ref[...] = acc

    def out_map(j, kc_ref):
        return (t.layer, j, kc_ref[1]) if by_cols else (t.layer, kc_ref[1] * nj + j, 0)

    in_specs = [pl.BlockSpec((t.rb, cols), lambda j, kc_ref: (j, 0)),
                pl.BlockSpec((N_CHIPS - 1, t.rb, cols), lambda j, kc_ref: (0, j, 0))]
    args = [kc, own, land]
    aliases = {}
    if prev is not None:
        in_specs.append(ANYSPEC)
        args.append(prev)
        aliases = {3: 0}
    return pl.pallas_call(
        body, name=f"chip_sum_{t.name}",
        grid_spec=pltpu.PrefetchScalarGridSpec(
            num_scalar_prefetch=1, grid=(nj,), in_specs=in_specs,
            out_specs=pl.BlockSpec((None, t.rb, cols), out_map)),
        out_shape=_sds((n_layers, t.R, t.C)),
        input_output_aliases=aliases,
        compiler_params=_params(1),
    )(*args)


def _pair_share(halves, by_cols):
    nt = len(halves)

    def part(ref, ti, core):
        axis = 2 if by_cols[ti] else 1
        half = halves[ti].shape[axis] // 2
        piece = pl.ds(pl.multiple_of(core * half, 128 if by_cols[ti] else 8), half)
        return ref.at[:, :, piece] if by_cols[ti] else ref.at[:, piece, :]

    def body(*refs):
        outs = refs[nt:2 * nt]
        send, recv = refs[2 * nt:]
        x, y, c, _ = _mesh_position()
        cps = []
        for ti in range(nt):
            mine = part(outs[ti], ti, c)
            cp = pltpu.make_async_remote_copy(mine, mine, send.at[ti], recv.at[ti],
                                              device_id=(x, y, 1 - c), device_id_type=MESH)
            cp.start()
            cps.append(cp)
        for ti in range(nt):
            theirs = part(outs[ti], ti, 1 - c)
            pltpu.make_async_remote_copy(theirs, theirs, send.at[ti], recv.at[ti],
                                         device_id=(x, y, 1 - c), device_id_type=MESH).wait_recv()
        for cp in cps:
            cp.wait_send()

    return pl.pallas_call(
        body, name="grads_pair_share",
        in_specs=[ANYSPEC] * nt, out_specs=[ANYSPEC] * nt,
        out_shape=[_sds(a.shape, a.dtype) for a in halves],
        scratch_shapes=[pltpu.SemaphoreType.DMA((nt,)), pltpu.SemaphoreType.DMA((nt,))],
        input_output_aliases={i: i for i in range(nt)},
        compiler_params=_params(),
    )(*halves)


def _adamw_math(w, g, m, v):
    m = B1 * m + (1.0 - B1) * g
    v = B2 * v + (1.0 - B2) * (g * g)
    delta = -LR * ((m / BC1) / (jnp.sqrt(v / BC2) + AEPS) + WD * w)
    return delta, m, v


def _adamw(name, rb, w, g, m, v):
    n_layers, r, c = w.shape

    def body(w_ref, g_ref, m_ref, v_ref, go_ref, d_ref, nm_ref, nv_ref):
        g_v = g_ref[...]
        go_ref[...] = g_v
        d_ref[...], nm_ref[...], nv_ref[...] = _adamw_math(w_ref[...], g_v, m_ref[...], v_ref[...])

    spec = pl.BlockSpec((None, rb, c), lambda l, j: (l, j, 0))
    return pl.pallas_call(
        body, name=f"adamw_{name}", grid=(n_layers, r // rb),
        in_specs=[spec] * 4, out_specs=[spec] * 4, out_shape=[_sds(w.shape)] * 4,
        compiler_params=_params(2),
    )(w, g, m, v)


GAIN_ROWS = {"pre_mix_g": 0, "post_mix_g": 2, "pre_ffn_g": 4, "post_ffn_g": 6, "ple_g": 8, "ple_post_g": 10}
ROW_KV_G, ROW_POOL_SCALE, ROW_SINKS, ROW_LOSS, PACK_ROWS = 12, 13, 14, 15, 16
SMALL_NAMES = tuple(GAIN_ROWS) + ("kv_g", "pool_scale", "sinks")


def _small_all_reduce(rows, dpool):
    ng, pr = len(WINDOWS), POOL_G // N_CHIPS

    def body(*refs):
        row_refs = refs[:PACK_ROWS]
        dpool_ref, tot_ref, gpool_ref, pack, land, pland, send, recv, psend, precv = refs[PACK_ROWS:]
        x, y, c, _ = _mesh_position()
        me = 4 * x + 2 * y + c
        for r in range(PACK_ROWS):
            pack[r:r + 1, :] = row_refs[r][...]

        def shard_of(k):
            return dpool_ref.at[:, pl.ds(pl.multiple_of(k * pr, pr), pr), :]

        cps = []
        for j in range(1, N_DEV):
            px, py, pc = x ^ (j >> 2), y ^ ((j >> 1) & 1), c ^ (j & 1)
            cps.append(pltpu.make_async_remote_copy(pack, land.at[me], send.at[j], recv.at[j],
                                                    device_id=(px, py, pc), device_id_type=MESH))
            cps.append(pltpu.make_async_remote_copy(shard_of(2 * px + py), pland.at[me], psend.at[j], precv.at[j],
                                                    device_id=(px, py, pc), device_id_type=MESH))
        for cp in cps:
            cp.start()
        land[me] = pack[...]
        pland[me] = dpool_ref[:, pl.ds(pl.multiple_of((2 * x + y) * pr, pr), pr), :]
        for j in range(1, N_DEV):
            pltpu.make_async_remote_copy(pack, land.at[me ^ j], send.at[j], recv.at[j],
                                         device_id=(x, y, c), device_id_type=MESH).wait_recv()
            pltpu.make_async_remote_copy(shard_of(0), pland.at[me ^ j], psend.at[j], precv.at[j],
                                         device_id=(x, y, c), device_id_type=MESH).wait_recv()
        for cp in cps:
            cp.wait_send()
        tot = land[0]
        gp = pland[0].astype(F32)
        for d in range(1, N_DEV):
            tot = tot + land[d]
            gp = gp + pland[d].astype(F32)
        tot_ref[...] = tot
        gpool_ref[...] = gp

    sems = pltpu.SemaphoreType.DMA((N_DEV,))
    return pl.pallas_call(
        body, name="small_all_reduce",
        in_specs=[VSPEC] * (PACK_ROWS + 1), out_specs=[VSPEC, VSPEC],
        out_shape=[_sds((PACK_ROWS, D)), _sds((ng, pr, POOL_G))],
        scratch_shapes=[pltpu.VMEM((PACK_ROWS, D), F32), pltpu.VMEM((N_DEV, PACK_ROWS, D), F32),
                        pltpu.VMEM((N_DEV, ng, pr, POOL_G), BF), sems, sems, sems, sems],
        compiler_params=_params(),
    )(*rows, dpool)


def _small_adamw(tot, kc, small_w, small_m, small_v):
    names = SMALL_NAMES
    n = len(names)

    def body(*refs):
        tot_ref, kc_ref = refs[0], refs[1]
        w_refs = dict(zip(names, refs[2:2 + n]))
        m_refs = dict(zip(names, refs[2 + n:2 + 2 * n]))
        v_refs = dict(zip(names, refs[2 + 2 * n:2 + 3 * n]))
        loss_ref = refs[2 + 3 * n]
        out_refs = {nm: refs[3 + 3 * n + 4 * k: 7 + 3 * n + 4 * k] for k, nm in enumerate(names)}
        tot = tot_ref[...]
        loss_ref[...] = 0.5 * jnp.sum(tot[ROW_LOSS:ROW_LOSS + 1, :], axis=-1, keepdims=True) * (1.0 / D)

        def update(nm, g):
            g_ref, d_ref, nm_ref, nv_ref = out_refs[nm]
            g_ref[...] = g
            d_ref[...], nm_ref[...], nv_ref[...] = _adamw_math(w_refs[nm][...], g, m_refs[nm][...], v_refs[nm][...])

        for nm, r in GAIN_ROWS.items():
            update(nm, tot[r:r + 2, :])
        update("kv_g", tot[ROW_KV_G:ROW_KV_G + 1, :])
        k = kc_ref[0]
        width = D // N_CHIPS
        g_scale = jnp.zeros((1, width), F32)
        for kk in range(N_CHIPS):
            g_scale = g_scale + jnp.where(k == kk, tot[ROW_POOL_SCALE:ROW_POOL_SCALE + 1, kk * width:(kk + 1) * width], 0.0)
        update("pool_scale", g_scale)
        update("sinks", tot[ROW_SINKS:ROW_SINKS + 1, 0:N_HEADS])

    ins = [tot, kc] + [small_w[nm] for nm in names] + [small_m[nm] for nm in names] + [small_v[nm] for nm in names]
    out_shape = [_sds((1, 1))]
    for nm in names:
        out_shape += [_sds(small_w[nm].shape)] * 4
    outs = pl.pallas_call(
        body, name="small_adamw",
        in_specs=[VSPEC, SSPEC] + [VSPEC] * (3 * n), out_specs=[VSPEC] * len(out_shape), out_shape=out_shape,
        compiler_params=_params(),
    )(*ins)
    return outs[0], {nm: outs[1 + 4 * k: 5 + 4 * k] for k, nm in enumerate(names)}


def _compute_layout(t, full):
    if t.src == "w_gu":
        return full.reshape(2, D, FF)
    if t.src == "pool_w":
        return full.reshape(len(WINDOWS), POOL_G, POOL_G)
    if t.src == "pool_scale":
        return full.reshape(1, D)
    return full.reshape(t.A * t.R, _ncb(t) * t.C)


def kernel(x, p, pre_mix_g, post_mix_g, pre_ffn_g, post_ffn_g, pool_w, pool_scale, kv_g, w_kv, w_q, sinks, w_o, w_gu, w_down, ple_g, w_ple_gate, w_ple_proj, ple_post_g, loss_target, m_pre_mix_g, m_post_mix_g, m_pre_ffn_g, m_post_ffn_g, m_pool_w, m_pool_scale, m_kv_g, m_w_kv, m_w_q, m_sinks, m_w_o, m_w_gu, m_w_down, m_ple_g, m_w_ple_gate, m_w_ple_proj, m_ple_post_g, v_pre_mix_g, v_post_mix_g, v_pre_ffn_g, v_post_ffn_g, v_pool_w, v_pool_scale, v_kv_g, v_w_kv, v_w_q, v_sinks, v_w_o, v_w_gu, v_w_down, v_ple_g, v_w_ple_gate, v_w_ple_proj, v_ple_post_g):
    weights = dict(pre_mix_g=pre_mix_g, post_mix_g=post_mix_g, pre_ffn_g=pre_ffn_g, post_ffn_g=post_ffn_g,
                   pool_w=pool_w, pool_scale=pool_scale, kv_g=kv_g, w_kv=w_kv, w_q=w_q, sinks=sinks, w_o=w_o,
                   w_gu=w_gu, w_down=w_down, ple_g=ple_g, w_ple_gate=w_ple_gate, w_ple_proj=w_ple_proj,
                   ple_post_g=ple_post_g)
    m_in = dict(pre_mix_g=m_pre_mix_g, post_mix_g=m_post_mix_g, pre_ffn_g=m_pre_ffn_g, post_ffn_g=m_post_ffn_g,
                pool_w=m_pool_w, pool_scale=m_pool_scale, kv_g=m_kv_g, w_kv=m_w_kv, w_q=m_w_q, sinks=m_sinks,
                w_o=m_w_o, w_gu=m_w_gu, w_down=m_w_down, ple_g=m_ple_g, w_ple_gate=m_w_ple_gate,
                w_ple_proj=m_w_ple_proj, ple_post_g=m_ple_post_g)
    v_in = dict(pre_mix_g=v_pre_mix_g, post_mix_g=v_post_mix_g, pre_ffn_g=v_pre_ffn_g, post_ffn_g=v_post_ffn_g,
                pool_w=v_pool_w, pool_scale=v_pool_scale, kv_g=v_kv_g, w_kv=v_w_kv, w_q=v_w_q, sinks=v_sinks,
                w_o=v_w_o, w_gu=v_w_gu, w_down=v_w_down, ple_g=v_ple_g, w_ple_gate=v_w_ple_gate,
                w_ple_proj=v_w_ple_proj, ple_post_g=v_ple_post_g)
    order = ["pre_mix_g", "post_mix_g", "pre_ffn_g", "post_ffn_g", "pool_w", "pool_scale", "kv_g", "w_kv", "w_q",
             "sinks", "w_o", "w_gu", "w_down", "ple_g", "w_ple_gate", "w_ple_proj", "ple_post_g"]

    kc = jnp.stack([2 * lax.axis_index("x") + lax.axis_index("y"), lax.axis_index("c")]).astype(jnp.int32)
    s_len = x.shape[1]
    x2d = x.reshape(s_len, D)
    p3d = p.reshape(2, s_len, PLE)
    target = loss_target.reshape(s_len, D)
    kv_g2d = kv_g.reshape(1, D)
    gains = {nm: weights[nm] for nm in GAIN_ROWS}

    def shard_view(src, a):
        t = next(t for t in BIGS.values() if t.src == src)
        return a.reshape(-1, t.R, t.C)

    placed = {nm: _place(t, shard_view(t.src, weights[t.src]), kc, BF) for nm, t in BIGS.items()}
    placed["pool_scale"] = _place(POOL_SCALE, pool_scale.reshape(1, 1, D // N_CHIPS), kc, F32)
    specs = dict(BIGS, pool_scale=POOL_SCALE)

    def gather(names, rows=None):
        rows = rows or {}
        parts = [(specs[nm],) + tuple(rows.get(nm, (0, specs[nm].R))) for nm in names]
        return _gather_rider(parts, [placed[nm] for nm in names])

    def take(names, results):
        for nm, a in zip(names, results):
            placed[nm] = a

    def weight(nm):
        return _compute_layout(specs[nm], placed[nm])

    first = ["pool_w", "pool_scale", "w_gu0", "w_down0"]
    take(first, _run("weights_gather_first", gather(first)))

    y0, x1 = _mixa_fwd(x2d, gains["pre_mix_g"], weight("pool_w"), weight("pool_scale"), gains["post_mix_g"])

    ride = ["w_ple_gate0", "w_ple_proj0", "w_q", "w_kv", "w_o", "w_gu1"]
    (f0, x2), got = _ffn_fwd(0, x1, gains["pre_ffn_g"], weight("w_gu0"), weight("w_down0"), gains["post_ffn_g"],
                             rider=gather(ride, {"w_gu1": (0, 320)}))
    take(ride, got)

    ride = ["w_ple_gate1", "w_ple_proj1", "w_gu1"]
    (z0, pe0, x3), got = _ple_fwd(0, x2, p3d, gains["ple_g"], weight("w_ple_gate0"), weight("w_ple_proj0"),
                                  gains["ple_post_g"], rider=gather(ride, {"w_gu1": (320, 448)}))
    take(ride, got)

    ride = ["w_gu1"]
    (q, kv), got = _qkv_fwd(x3, gains["pre_mix_g"], kv_g2d, weight("w_q"), weight("w_kv"),
                            rider=gather(ride, {"w_gu1": (448, 704)}))
    take(ride, got)

    ride = ["w_down1", "w_gu1"]
    (attn, y1, x4), got = _attn_fwd(q, kv, sinks, x3, weight("w_o"), gains["post_mix_g"],
                                    rider=gather(ride, {"w_gu1": (704, D)}))
    take(ride, got)

    (f1, x5), _ = _ffn_fwd(1, x4, gains["pre_ffn_g"], weight("w_gu1"), weight("w_down1"), gains["post_ffn_g"])
    (z1, pe1, dx6, loss_row), _ = _ple_fwd(1, x5, p3d, gains["ple_g"], weight("w_ple_gate1"), weight("w_ple_proj1"),
                                           gains["ple_post_g"], target=target)

    local = {}
    landed = {}
    fused = {}

    def pair_stage(tag, names):
        ts = [BIGS[nm] for nm in names]
        gs = [local[nm].reshape(_full_shape(t)) for nm, t in zip(names, ts)]
        lands = _pair_exchange(f"grads_pair_exchange_{tag}", ts, gs)
        return [_pair_sum(t, g, l, kc) for t, g, l in zip(ts, gs, lands)]

    def scatter(names, sums):
        return _scatter_rider([BIGS[nm] for nm in names], sums)

    def keep(names, sums, got):
        for nm, s, l in zip(names, sums, got):
            landed[nm] = (s, l)

    (dx5, local["w_ple_gate1"], local["w_ple_proj1"], d_ple1, d_plepost1), _ = _ple_bwd(
        1, dx6, x5, z1, pe1, p3d, gains["ple_g"], weight("w_ple_gate1"), gains["ple_post_g"])

    group_a = ["w_ple_gate1", "w_ple_proj1"]
    sums_a = pair_stage("a", group_a)
    (dx4, d_preffn1, d_postffn1, *scattered), got = _ffn_bwd(
        1, dx5, x4, f1, gains["pre_ffn_g"], weight("w_gu1"), weight("w_down1"), gains["post_ffn_g"], kc,
        rider=scatter(group_a, sums_a))
    fused["w_gu1"], fused["w_down1"] = scattered[0:2], scattered[2:4]
    keep(group_a, sums_a, got)

    (dq, dkv, local["w_o"], d_postmix1, d_sinks), _ = _attn_bwd(
        dx4, y1, attn, q, kv, sinks, weight("w_o"), gains["post_mix_g"])
    dx3, local["w_q"], local["w_kv"], d_premix1, d_kvg = _qkv_bwd(
        dq, dkv, x3, dx4, gains["pre_mix_g"], kv_g2d, weight("w_q"), weight("w_kv"))

    group_b = ["w_o", "w_q", "w_kv"]
    sums_b = pair_stage("b", group_b)
    (dx2, local["w_ple_gate0"], local["w_ple_proj0"], d_ple0, d_plepost0), got = _ple_bwd(
        0, dx3, x2, z0, pe0, p3d, gains["ple_g"], weight("w_ple_gate0"), gains["ple_post_g"],
        rider=scatter(group_b, sums_b))
    keep(group_b, sums_b, got)

    group_c = ["w_ple_gate0", "w_ple_proj0"]
    sums_c = pair_stage("c", group_c)
    (dx1, d_preffn0, d_postffn0, *scattered), got = _ffn_bwd(
        0, dx2, x1, f0, gains["pre_ffn_g"], weight("w_gu0"), weight("w_down0"), gains["post_ffn_g"], kc,
        rider=scatter(group_c, sums_c))
    fused["w_gu0"], fused["w_down0"] = scattered[0:2], scattered[2:4]
    keep(group_c, sums_c, got)

    dx0, d_pool, d_scale, d_postmix0, d_premix0 = _mixa_bwd(
        dx1, x2d, y0, gains["pre_mix_g"], weight("pool_w"), weight("pool_scale"), gains["post_mix_g"])

    rows = [d_premix0, d_premix1, d_postmix0, d_postmix1, d_preffn0, d_preffn1, d_postffn0, d_postffn1,
            d_ple0, d_ple1, d_plepost0, d_plepost1, d_kvg, d_scale, d_sinks, loss_row]
    as2d = lambda a: a.reshape(1, D) if a.ndim == 1 else a
    tot, g_pool = _small_all_reduce(rows, d_pool)
    loss, small = _small_adamw(tot, kc, {nm: as2d(weights[nm]) for nm in SMALL_NAMES},
                               {nm: as2d(m_in[nm]) for nm in SMALL_NAMES},
                               {nm: as2d(v_in[nm]) for nm in SMALL_NAMES})

    shared = [src for src in BIG_SOURCES if src != "pool_w"]
    halves = []
    for src in shared:
        n_layers = shard_view(src, weights[src]).shape[0]
        acc = None
        for t in [t for t in BIGS.values() if t.src == src]:
            if t.name in fused:
                own, land = fused[t.name]
                acc = _chip_sum_fused(t, own, land, kc, n_layers, acc, by_cols=src == "w_down")
            else:
                s, l = landed[t.name]
                acc = _chip_sum(t, s, l, kc, n_layers, acc)
        halves.append(acc)
    full_grads = dict(zip(shared, _pair_share(halves, [src == "w_down" for src in shared])))
    full_grads["pool_w"] = g_pool

    out = {"grad": {}, "delta": {}, "new_m": {}, "new_v": {}}
    for src in BIG_SOURCES:
        g = full_grads[src]
        t = next(t for t in BIGS.values() if t.src == src)
        res = _adamw(src, t.rb, shard_view(src, weights[src]), g, shard_view(src, m_in[src]), shard_view(src, v_in[src]))
        shape = weights[src].shape
        for kind, a in zip(("grad", "delta", "new_m", "new_v"), res):
            out[kind][src] = a.reshape(shape)
    for nm in SMALL_NAMES:
        shape = weights[nm].shape
        for kind, a in zip(("grad", "delta", "new_m", "new_v"), small[nm]):
            out[kind][nm] = a.reshape(shape)

    return (loss.reshape(()), dx0.reshape(x.shape),
            *[out["grad"][nm] for nm in order], *[out["delta"][nm] for nm in order],
            *[out["new_m"][nm] for nm in order], *[out["new_v"][nm] for nm in order])
```

```python
import collections

import jax
import jax.numpy as jnp
from jax import lax
from jax.experimental import pallas as pl
from jax.experimental.pallas import tpu as pltpu

D = 1024
FF = 2816
N_HEADS = 16
HEAD_DIM = 64
N_KV_HEADS = 4
GQA = N_HEADS // N_KV_HEADS
KVD = N_KV_HEADS * HEAD_DIM
PLE = 256
BLK = 128
WINDOWS = (2, 4, 8, 16)
POOL_G = 256
HALO = 16
EPS = 1e-6
NEG_INF = -1e30
ATT_SCALE = HEAD_DIM ** -0.5
SLOPES = tuple(2.0 ** (-8.0 * (h + 1) / N_HEADS) for h in range(N_HEADS))
N_CHIPS = 4
N_DEV = 8

LR, B1, B2, AEPS, WD, STEP = 0.001, 0.9, 0.999, 1e-08, 0.01, 10
BC1 = 1.0 - B1 ** STEP
BC2 = 1.0 - B2 ** STEP

BF = jnp.bfloat16
F32 = jnp.float32
MESH = pl.DeviceIdType.MESH
VMEM_LIMIT_V7X = 58 * 1024 * 1024
TM = 256
TM_FFN_BWD = 512
FF_CHUNK = 256
FF_HALF = FF // 2

VSPEC = pl.BlockSpec(memory_space=pltpu.VMEM)
SSPEC = pl.BlockSpec(memory_space=pltpu.SMEM)
ANYSPEC = pl.BlockSpec(memory_space=pl.ANY)


def _params(n_grid=0):
    sem = ("arbitrary",) * n_grid if n_grid else None
    return pltpu.CompilerParams(dimension_semantics=sem, vmem_limit_bytes=VMEM_LIMIT_V7X)


def _sds(shape, dtype=F32):
    return jax.ShapeDtypeStruct(tuple(shape), dtype)


Rider = collections.namedtuple("Rider", "arrays out_shapes aliases scratch start mid finish")
MID_NUM, MID_DEN = 5, 8


def _call(body, *, name, grid, in_specs, out_specs, out_shape, args, scratch_shapes=(), rider=None, prefetch=None):
    ni, no, ns = len(in_specs), len(out_specs), len(scratch_shapes)
    npre = 0 if prefetch is None else 1
    pre = [] if prefetch is None else [prefetch]
    if rider is None:
        rider = Rider([], [], {}, [], None, None, None)
    ri, ro = len(rider.arrays), len(rider.out_shapes)

    def full(*refs):
        pre_refs, refs = refs[:npre], refs[npre:]
        ins, refs = refs[:ni], refs[ni:]
        rins, refs = refs[:ri], refs[ri:]
        outs, refs = refs[:no], refs[no:]
        routs, refs = refs[:ro], refs[ro:]
        scr, rscr = refs[:ns], refs[ns:]
        ids = [pl.program_id(a) for a in range(len(grid))]
        first = ids[0] == 0
        last = ids[0] == grid[0] - 1
        for a in range(1, len(grid)):
            first = first & (ids[a] == 0)
            last = last & (ids[a] == grid[a] - 1)

        if rider.start is not None:
            @pl.when(first)
            def _():
                rider.start(rins, routs, rscr)

        if rider.mid is not None:
            assert len(grid) == 1

            @pl.when(ids[0] == (grid[0] * MID_NUM) // MID_DEN)
            def _():
                rider.mid(rins, routs, rscr)

        body(*pre_refs, *ins, *outs, *scr)

        if rider.finish is not None:
            @pl.when(last)
            def _():
                rider.finish(rins, routs, rscr)

    outs = pl.pallas_call(
        full, name=name,
        grid_spec=pltpu.PrefetchScalarGridSpec(
            num_scalar_prefetch=npre, grid=grid,
            in_specs=list(in_specs) + [ANYSPEC] * ri, out_specs=list(out_specs) + [ANYSPEC] * ro,
            scratch_shapes=list(scratch_shapes) + list(rider.scratch)),
        out_shape=list(out_shape) + list(rider.out_shapes),
        input_output_aliases={npre + ni + a: no + b for a, b in rider.aliases.items()},
        compiler_params=_params(len(grid)))(*pre, *args, *rider.arrays)
    return list(outs[:no]), list(outs[no:])


def _run(name, rider):
    ri = len(rider.arrays)

    def body(*refs):
        rins, routs, rscr = refs[:ri], refs[ri:ri + len(rider.out_shapes)], refs[ri + len(rider.out_shapes):]
        rider.start(rins, routs, rscr)
        if rider.mid is not None:
            rider.mid(rins, routs, rscr)
        rider.finish(rins, routs, rscr)

    return pl.pallas_call(
        body, name=name, in_specs=[ANYSPEC] * ri, out_specs=[ANYSPEC] * len(rider.out_shapes),
        out_shape=list(rider.out_shapes), scratch_shapes=list(rider.scratch),
        input_output_aliases=dict(rider.aliases), compiler_params=_params())(*rider.arrays)


def _rms_fwd(x, g):
    r = lax.rsqrt(jnp.mean(x * x, axis=-1, keepdims=True) + EPS)
    return x * r * g


def _rms_bwd(x, g, dy):
    r = lax.rsqrt(jnp.mean(x * x, axis=-1, keepdims=True) + EPS)
    xn = x * r
    dxn = dy * g
    dx = r * (dxn - xn * jnp.mean(dxn * xn, axis=-1, keepdims=True))
    return dx, dy * xn


def _rowsum(a):
    return jnp.sum(a, axis=0, keepdims=True)


def _sigmoid(z):
    return 1.0 / (1.0 + jnp.exp(-z))


def _dot(a, b):
    return jnp.dot(a, b, preferred_element_type=F32)


def _dot_nt(a, b):
    return lax.dot_general(a, b, (((1,), (1,)), ((), ())), preferred_element_type=F32)


def _dot_tn(a, b):
    return lax.dot_general(a, b, (((0,), (0,)), ((), ())), preferred_element_type=F32)


def _row_spec(tm, width=D):
    return pl.BlockSpec((tm, width), lambda i: (i, 0))


def _const_spec(shape):
    zeros = (0,) * len(shape)
    return pl.BlockSpec(tuple(shape), lambda *_: zeros)


def _pool_delta(he, pos):
    out = []
    for gi, w in enumerate(WINDOWS):
        hg = he[:, gi * POOL_G:(gi + 1) * POOL_G]
        s = hg
        k = 1
        while k < w:
            s = s + pltpu.roll(s, k, 0)
            k *= 2
        cnt = jnp.maximum(jnp.minimum(pos + 1, w), 1).astype(F32)
        out.append(s / cnt - hg)
    return out


def _load_with_halo_before(x_ref, i, tm):
    r0 = pl.multiple_of(i * tm, tm)
    hs = pl.multiple_of(jnp.maximum(i * tm - HALO, 0), 8)
    xh = jnp.where(i > 0, x_ref[pl.ds(hs, HALO), :], 0.0)
    xt = x_ref[pl.ds(r0, tm), :]
    return xt, jnp.concatenate([xh, xt], axis=0)


def _mixa_fwd(x, pre_g, pool_w, pool_scale, post_g):
    s_len = x.shape[0]
    n = s_len // TM

    def body(x_ref, pg_ref, w_ref, sc_ref, qg_ref, y_ref, x1_ref):
        i = pl.program_id(0)
        xt, xe = _load_with_halo_before(x_ref, i, TM)
        he = _rms_fwd(xe, pg_ref[0:1, :])
        pos = i * TM - HALO + lax.broadcasted_iota(jnp.int32, (TM + HALO, 1), 0)
        ds = _pool_delta(he, pos)
        ys = [_dot(ds[gi][HALO:, :].astype(BF), w_ref[gi]) for gi in range(len(WINDOWS))]
        y = jnp.concatenate(ys, axis=1) * sc_ref[...]
        y_ref[...] = y
        x1_ref[...] = xt + _rms_fwd(y, qg_ref[0:1, :])

    outs, _ = _call(body, name="mixa_fwd", grid=(n,),
                    in_specs=[VSPEC] * 5, out_specs=[_row_spec(TM), _row_spec(TM)],
                    out_shape=[_sds((s_len, D)), _sds((s_len, D))],
                    args=[x, pre_g, pool_w, pool_scale, post_g])
    return outs


def _mixa_bwd(dx1, x, y, pre_g, pool_w, pool_scale, post_g):
    s_len = x.shape[0]
    n = s_len // TM
    ng = len(WINDOWS)

    def body(dx_ref, x_ref, y_ref, pg_ref, w_ref, sc_ref, qg_ref,
             dx0_ref, dw_ref, dsc_ref, dqg_ref, dpg_ref, wacc):
        i = pl.program_id(0)

        @pl.when(i == 0)
        def _():
            wacc[...] = jnp.zeros_like(wacc)
            dsc_ref[...] = jnp.zeros_like(dsc_ref)
            dqg_ref[...] = jnp.zeros_like(dqg_ref)
            dpg_ref[...] = jnp.zeros_like(dpg_ref)

        r0 = pl.multiple_of(i * TM, TM)
        xt, xe = _load_with_halo_before(x_ref, i, TM)
        he = _rms_fwd(xe, pg_ref[0:1, :])
        pos_b = i * TM - HALO + lax.broadcasted_iota(jnp.int32, (TM + HALO, 1), 0)
        ds = _pool_delta(he, pos_b)

        last = i == n - 1
        a0 = pl.multiple_of(jnp.minimum(i * TM + TM, s_len - HALO), 8)
        ye = jnp.concatenate([y_ref[pl.ds(r0, TM), :], y_ref[pl.ds(a0, HALO), :]], axis=0)
        dt = dx_ref[pl.ds(r0, TM), :]
        de = jnp.concatenate([dt, jnp.where(last, 0.0, dx_ref[pl.ds(a0, HALO), :])], axis=0)
        dye, prod = _rms_bwd(ye, qg_ref[0:1, :], de)
        dqg_ref[...] += _rowsum(prod[:TM, :])
        dys = dye * sc_ref[...]
        pos_a = i * TM + lax.broadcasted_iota(jnp.int32, (TM + HALO, 1), 0)

        dhs, dscs = [], []
        for gi, w in enumerate(WINDOWS):
            sl = slice(gi * POOL_G, (gi + 1) * POOL_G)
            wg = w_ref[gi]
            dys_g = dys[:, sl].astype(BF)
            d_g = ds[gi][HALO:, :].astype(BF)
            ypre = _dot(d_g, wg)
            dscs.append(_rowsum(dye[:TM, sl] * ypre))
            wacc[gi] += _dot_tn(d_g, dys_g[:TM, :])
            dd = _dot_nt(dys_g, wg)
            cnt = jnp.minimum(pos_a + 1, w).astype(F32)
            a = dd / cnt
            k = 1
            while k < w:
                a = a + pltpu.roll(a, TM + HALO - k, 0)
                k *= 2
            dhs.append(a[:TM, :] - dd[:TM, :])
        dsc_ref[...] += jnp.concatenate(dscs, axis=1)
        dh = jnp.concatenate(dhs, axis=1)
        dxp, prod2 = _rms_bwd(xt, pg_ref[0:1, :], dh)
        dpg_ref[...] += _rowsum(prod2)
        dx0_ref[...] = dt + dxp

        @pl.when(last)
        def _():
            dw_ref[...] = wacc[...].astype(BF)

    outs, _ = _call(
        body, name="mixa_bwd", grid=(n,), in_specs=[VSPEC] * 7,
        out_specs=[_row_spec(TM), _const_spec((ng, POOL_G, POOL_G)), _const_spec((1, D)),
                   _const_spec((1, D)), _const_spec((1, D))],
        out_shape=[_sds((s_len, D)), _sds((ng, POOL_G, POOL_G), BF), _sds((1, D)), _sds((1, D)), _sds((1, D))],
        scratch_shapes=[pltpu.VMEM((ng, POOL_G, POOL_G), F32)],
        args=[dx1, x, y, pre_g, pool_w, pool_scale, post_g])
    return outs


def _ffn_fwd(layer, x1, pre_g, wgu, wd, post_g, rider=None):
    s_len = x1.shape[0]

    def body(x_ref, pg_ref, wgu_ref, wd_ref, qg_ref, f_ref, x2_ref):
        x = x_ref[...]
        h = _rms_fwd(x, pg_ref[layer:layer + 1, :]).astype(BF)
        f = jnp.zeros((TM, D), F32)
        for c in range(FF // FF_HALF):
            cols = slice(c * FF_HALF, (c + 1) * FF_HALF)
            g = _dot(h, wgu_ref[0, :, cols])
            u = _dot(h, wgu_ref[1, :, cols])
            act = g * _sigmoid(g) * u
            f = f + _dot(act.astype(BF), wd_ref[cols, :])
        f_ref[...] = f
        x2_ref[...] = x + _rms_fwd(f, qg_ref[layer:layer + 1, :])

    return _call(body, name=f"ffn_fwd{layer}", grid=(s_len // TM,),
                 in_specs=[_row_spec(TM), VSPEC, VSPEC, VSPEC, VSPEC],
                 out_specs=[_row_spec(TM), _row_spec(TM)],
                 out_shape=[_sds((s_len, D)), _sds((s_len, D))],
                 args=[x1, pre_g, wgu, wd, post_g], rider=rider)


GU_PIECE = 128
DN_PIECE = 64
DN_SLOT = FF // N_CHIPS
HALF_D = D // 2


def _ffn_bwd(layer, dx2, x1, f, pre_g, wgu, wd, post_g, kc, rider=None):
    s_len = x1.shape[0]
    tm = TM_FFN_BWD
    n = s_len // tm
    nc = FF // FF_CHUNK
    n_gu, n_dn = FF_CHUNK // GU_PIECE, FF_CHUNK // DN_PIECE
    n_pieces = 2 * n_gu + n_dn
    n_blk = FF_HALF // GU_PIECE

    def edge_rows(c, i, kc_ref):
        return (jnp.where((c == 0) | (c == nc - 1), i, n - 1), 0)

    def chunk_at(c, kc_ref):
        return (c + (kc_ref[0] * nc) // N_CHIPS) % nc

    def exchange(kc_ref, c, accg, accu, accd, own_gu_ref, land_gu_ref, own_dn_ref, land_dn_ref,
                 pl_gu, pl_dn, sib_gu, sib_dn, mine_gu, mine_dn, sum_gu, sum_dn,
                 psend, precv, ssend, lsem, rrecv):
        x, y, core = lax.axis_index("x"), lax.axis_index("y"), lax.axis_index("c")
        lower = core == 0

        def pair_copy(cc, part):
            p = cc % 2
            src, dst = ((sib_gu, pl_gu), (sib_dn, pl_dn))[part]
            return pltpu.make_async_remote_copy(src.at[p], dst.at[cc], psend.at[p, part], precv.at[cc, part],
                                                device_id=(x, y, 1 - core), device_id_type=MESH)

        def scatter(cc, wait):
            p = cc % 2
            jobs = []
            for gu in range(2):
                for hc in range(n_gu):
                    hidden = chunk_at(cc, kc_ref) * FF_CHUNK + hc * GU_PIECE
                    k = hidden // FF_HALF
                    jobs.append((sum_gu.at[p, gu, hc], k + 2 * gu, 0,
                                 own_gu_ref, land_gu_ref, ((hidden - k * FF_HALF) // GU_PIECE,)))
            for q in range(n_dn):
                hidden = chunk_at(cc, kc_ref) * FF_CHUNK + q * DN_PIECE
                k = hidden // DN_SLOT
                off = pl.multiple_of(hidden - k * DN_SLOT, DN_PIECE)
                jobs.append((sum_dn.at[p, pl.ds(q * DN_PIECE, DN_PIECE), :], k, 1,
                             own_dn_ref, land_dn_ref, (pl.ds(off, DN_PIECE), slice(None))))
            for pi, (src, k, t, own_ref, land_ref, where) in enumerate(jobs):
                kx, ky = k // 2, k % 2
                fx, fy = (kx != x).astype(jnp.int32), (ky != y).astype(jnp.int32)
                local = (fx + fy) == 0
                j = jnp.maximum(fx + 2 * fy - 1, 0)

                @pl.when(local)
                def _():
                    cp = pltpu.make_async_copy(src, own_ref.at[where], lsem.at[p, pi])
                    if wait:
                        cp.wait()
                    else:
                        cp.start()

                @pl.when(jnp.logical_not(local))
                def _():
                    cp = pltpu.make_async_remote_copy(src, land_ref.at[(j,) + where], ssend.at[p, pi],
                                                      rrecv.at[t, j], device_id=(kx, ky, core), device_id_type=MESH)
                    if wait:
                        cp.wait_send()
                    else:
                        cp.start()

        def add_and_scatter(cc):
            p = cc % 2
            pair_copy(cc, 0).wait_recv()
            pair_copy(cc, 1).wait_recv()
            s_gu = (mine_gu[...] + pl_gu[cc].astype(F32)).astype(BF)
            for hc in range(n_gu):
                sum_gu[p, :, hc] = s_gu[:, :, hc * GU_PIECE:(hc + 1) * GU_PIECE]
            sum_dn[p] = (mine_dn[...] + pl_dn[cc].astype(F32)).astype(BF)
            scatter(cc, wait=False)

        @pl.when(c >= 1)
        def _():
            @pl.when(c >= 3)
            def _():
                scatter(c - 3, wait=True)
            add_and_scatter(c - 1)

        @pl.when(c >= 2)
        def _():
            pair_copy(c - 2, 0).wait_send()
            pair_copy(c - 2, 1).wait_send()

        p = c % 2
        g_v, u_v, d_v = accg[...], accu[...], accd[...]
        sib_gu[p, 0] = jnp.where(lower, g_v[HALF_D:, :], g_v[:HALF_D, :]).astype(BF)
        sib_gu[p, 1] = jnp.where(lower, u_v[HALF_D:, :], u_v[:HALF_D, :]).astype(BF)
        sib_dn[p] = jnp.where(lower, d_v[:, HALF_D:], d_v[:, :HALF_D]).astype(BF)
        mine_gu[0] = jnp.where(lower, g_v[:HALF_D, :], g_v[HALF_D:, :])
        mine_gu[1] = jnp.where(lower, u_v[:HALF_D, :], u_v[HALF_D:, :])
        mine_dn[...] = jnp.where(lower, d_v[:, :HALF_D], d_v[:, HALF_D:])
        pair_copy(c, 0).start()
        pair_copy(c, 1).start()

        @pl.when(c == nc - 1)
        def _():
            scatter(nc - 3, wait=True)
            add_and_scatter(nc - 1)
            for cc in (nc - 2, nc - 1):
                pair_copy(cc, 0).wait_send()
                pair_copy(cc, 1).wait_send()
                scatter(cc, wait=True)
            for t, land_ref in enumerate((land_gu_ref, land_dn_ref)):
                for j in range(N_CHIPS - 1):
                    pltpu.make_async_remote_copy(land_ref.at[j], land_ref.at[j], ssend.at[0, 0], rrecv.at[t, j],
                                                 device_id=(x, y, core), device_id_type=MESH).wait_recv()

    def body(kc_ref, dx_ref, x_ref, f_ref, pg_ref, wgu_ref, wd_ref, qg_ref,
             dx1_ref, dpg_ref, dqg_ref, own_gu_ref, land_gu_ref, own_dn_ref, land_dn_ref,
             h_s, df_s, dh_s, accg, accu, accd, *comm):
        c = pl.program_id(0)
        i = pl.program_id(1)
        rows = pl.ds(pl.multiple_of(i * tm, tm), tm)
        pg = pg_ref[layer:layer + 1, :]

        @pl.when((c == 0) & (i == 0))
        def _():
            dpg_ref[...] = jnp.zeros_like(dpg_ref)
            dqg_ref[...] = jnp.zeros_like(dqg_ref)

        @pl.when(c == 0)
        def _():
            h_s[rows, :] = _rms_fwd(x_ref[...], pg).astype(BF)
            df, prod = _rms_bwd(f_ref[...], qg_ref[layer:layer + 1, :], dx_ref[...])
            df_s[rows, :] = df.astype(BF)
            dqg_ref[...] += _rowsum(prod)

        @pl.when(i == 0)
        def _():
            accg[...] = jnp.zeros_like(accg)
            accu[...] = jnp.zeros_like(accu)
            accd[...] = jnp.zeros_like(accd)

        h = h_s[rows, :]
        df = df_s[rows, :]
        wg = wgu_ref[0]
        wu = wgu_ref[1]
        g = _dot(h, wg)
        u = _dot(h, wu)
        sg = _sigmoid(g)
        a = g * sg
        dact = _dot_nt(df, wd_ref[...])
        accd[...] += _dot_tn((a * u).astype(BF), df)
        du = (dact * a).astype(BF)
        dg = (dact * u * (sg * (1.0 + g * (1.0 - sg)))).astype(BF)
        accg[...] += _dot_tn(h, dg)
        accu[...] += _dot_tn(h, du)
        dh = _dot_nt(dg, wg) + _dot_nt(du, wu)

        @pl.when(c == 0)
        def _():
            dh_s[rows, :] = dh

        @pl.when((c > 0) & (c < nc - 1))
        def _():
            dh_s[rows, :] += dh

        @pl.when(c == nc - 1)
        def _():
            dxp, prod = _rms_bwd(x_ref[...], pg, dh_s[rows, :] + dh)
            dpg_ref[...] += _rowsum(prod)
            dx1_ref[...] = dx_ref[...] + dxp

        @pl.when(i == n - 1)
        def _():
            exchange(kc_ref, c, accg, accu, accd, own_gu_ref, land_gu_ref, own_dn_ref, land_dn_ref, *comm)

    dma = pltpu.SemaphoreType.DMA
    return _call(
        body, name=f"ffn_bwd{layer}", grid=(nc, n),
        in_specs=[pl.BlockSpec((tm, D), edge_rows), pl.BlockSpec((tm, D), edge_rows),
                  pl.BlockSpec((tm, D), lambda c, i, kc_ref: (jnp.where(c == 0, i, n - 1), 0),
                               pipeline_mode=pl.Buffered(1)),
                  VSPEC,
                  pl.BlockSpec((2, D, FF_CHUNK), lambda c, i, kc_ref: (0, 0, chunk_at(c, kc_ref))),
                  pl.BlockSpec((FF_CHUNK, D), lambda c, i, kc_ref: (chunk_at(c, kc_ref), 0)),
                  VSPEC],
        out_specs=[pl.BlockSpec((tm, D), lambda c, i, kc_ref: (jnp.where(c == nc - 1, i, 0), 0)),
                   _const_spec((1, D)), _const_spec((1, D)), ANYSPEC, ANYSPEC, ANYSPEC, ANYSPEC],
        out_shape=[_sds((s_len, D)), _sds((1, D)), _sds((1, D)),
                   _sds((n_blk, HALF_D, GU_PIECE), BF), _sds((N_CHIPS - 1, n_blk, HALF_D, GU_PIECE), BF),
                   _sds((DN_SLOT, HALF_D), BF), _sds((N_CHIPS - 1, DN_SLOT, HALF_D), BF)],
        scratch_shapes=[pltpu.VMEM((s_len, D), BF), pltpu.VMEM((s_len, D), BF), pltpu.VMEM((s_len, D), F32),
                        pltpu.VMEM((D, FF_CHUNK), F32), pltpu.VMEM((D, FF_CHUNK), F32),
                        pltpu.VMEM((FF_CHUNK, D), F32),
                        pltpu.VMEM((nc, 2, HALF_D, FF_CHUNK), BF), pltpu.VMEM((nc, FF_CHUNK, HALF_D), BF),
                        pltpu.VMEM((2, 2, HALF_D, FF_CHUNK), BF), pltpu.VMEM((2, FF_CHUNK, HALF_D), BF),
                        pltpu.VMEM((2, HALF_D, FF_CHUNK), F32), pltpu.VMEM((FF_CHUNK, HALF_D), F32),
                        pltpu.VMEM((2, 2, n_gu, HALF_D, GU_PIECE), BF), pltpu.VMEM((2, FF_CHUNK, HALF_D), BF),
                        dma((2, 2)), dma((nc, 2)), dma((2, n_pieces)), dma((2, n_pieces)), dma((2, N_CHIPS - 1))],
        args=[dx2, x1, f, pre_g, wgu, wd, post_g], rider=rider, prefetch=kc)


def _ple_fwd(layer, x2, p, ple_g, w_gate, w_proj, post_g, target=None, rider=None):
    s_len = x2.shape[0]
    final = target is not None

    def body(*refs):
        if final:
            x_ref, p_ref, g_ref, wg_ref, wp_ref, qg_ref, t_ref, z_ref, pe_ref, dx_ref, lv_ref = refs
        else:
            x_ref, p_ref, g_ref, wg_ref, wp_ref, qg_ref, z_ref, pe_ref, x3_ref = refs
        x = x_ref[...]
        r = _rms_fwd(x, g_ref[layer:layer + 1, :]).astype(BF)
        z = _dot(r, wg_ref[...])
        pe = _dot(p_ref[...].astype(BF), wp_ref[...])
        z_ref[...] = z
        pe_ref[...] = pe
        x3 = x + _rms_fwd(pe * _sigmoid(z), qg_ref[layer:layer + 1, :])
        if final:
            @pl.when(pl.program_id(0) == 0)
            def _():
                lv_ref[...] = jnp.zeros_like(lv_ref)
            err = x3 - t_ref[...]
            dx_ref[...] = err * (1.0 / D)
            lv_ref[...] += _rowsum(err * err)
        else:
            x3_ref[...] = x3

    p_spec = pl.BlockSpec((None, TM, PLE), lambda i: (layer, i, 0))
    in_specs = [_row_spec(TM), p_spec, VSPEC, VSPEC, VSPEC, VSPEC]
    args = [x2, p, ple_g, w_gate, w_proj, post_g]
    out_specs = [_row_spec(TM), _row_spec(TM), _row_spec(TM)]
    out_shape = [_sds((s_len, D))] * 3
    if final:
        in_specs.append(_row_spec(TM))
        args.append(target)
        out_specs.append(_const_spec((1, D)))
        out_shape.append(_sds((1, D)))
    return _call(body, name=f"ple_fwd{layer}", grid=(s_len // TM,), in_specs=in_specs, out_specs=out_specs,
                 out_shape=out_shape, args=args, rider=rider)


def _ple_bwd(layer, dx3, x2, z, pe, p, ple_g, w_gate, post_g, rider=None):
    s_len = x2.shape[0]
    n = s_len // TM

    def body(dx_ref, x_ref, z_ref, pe_ref, p_ref, g_ref, wg_ref, qg_ref,
             dx2_ref, dwg_ref, dwp_ref, dg_ref, dqg_ref, gacc, pacc):
        i = pl.program_id(0)

        @pl.when(i == 0)
        def _():
            gacc[...] = jnp.zeros_like(gacc)
            pacc[...] = jnp.zeros_like(pacc)
            dg_ref[...] = jnp.zeros_like(dg_ref)
            dqg_ref[...] = jnp.zeros_like(dqg_ref)

        dx = dx_ref[...]
        x = x_ref[...]
        pe_v = pe_ref[...]
        gate = _sigmoid(z_ref[...])
        de, prod = _rms_bwd(pe_v * gate, qg_ref[layer:layer + 1, :], dx)
        dqg_ref[...] += _rowsum(prod)
        dpe = (de * gate).astype(BF)
        dz = (de * pe_v * gate * (1.0 - gate)).astype(BF)
        pacc[...] += _dot_tn(p_ref[...].astype(BF), dpe)
        g = g_ref[layer:layer + 1, :]
        r = _rms_fwd(x, g).astype(BF)
        gacc[...] += _dot_tn(r, dz)
        dr = _dot_nt(dz, wg_ref[...])
        dxp, prod2 = _rms_bwd(x, g, dr)
        dg_ref[...] += _rowsum(prod2)
        dx2_ref[...] = dx + dxp

        @pl.when(i == n - 1)
        def _():
            dwg_ref[...] = gacc[...].astype(BF)
            dwp_ref[...] = pacc[...].astype(BF)

    p_spec = pl.BlockSpec((None, TM, PLE), lambda i: (layer, i, 0))
    return _call(
        body, name=f"ple_bwd{layer}", grid=(n,),
        in_specs=[_row_spec(TM), _row_spec(TM), _row_spec(TM), _row_spec(TM), p_spec, VSPEC, VSPEC, VSPEC],
        out_specs=[_row_spec(TM), _const_spec((D, D)), _const_spec((PLE, D)), _const_spec((1, D)), _const_spec((1, D))],
        out_shape=[_sds((s_len, D)), _sds((D, D), BF), _sds((PLE, D), BF), _sds((1, D)), _sds((1, D))],
        scratch_shapes=[pltpu.VMEM((D, D), F32), pltpu.VMEM((PLE, D), F32)],
        args=[dx3, x2, z, pe, p, ple_g, w_gate, post_g], rider=rider)


def _qkv_fwd(x3, q_g, kv_g, w_q, w_kv, rider=None):
    s_len = x3.shape[0]

    def body(x_ref, qg_ref, kg_ref, wq_ref, wkv_ref, q_ref, kv_ref):
        x = x_ref[...]
        q_ref[...] = _dot(_rms_fwd(x, qg_ref[1:2, :]).astype(BF), wq_ref[...]).astype(BF)
        kv_ref[...] = _dot(_rms_fwd(x, kg_ref[...]).astype(BF), wkv_ref[...]).astype(BF)

    return _call(body, name="qkv_fwd", grid=(s_len // TM,),
                 in_specs=[_row_spec(TM), VSPEC, VSPEC, VSPEC, VSPEC],
                 out_specs=[_row_spec(TM), _row_spec(TM, 2 * KVD)],
                 out_shape=[_sds((s_len, D), BF), _sds((s_len, 2 * KVD), BF)],
                 args=[x3, q_g, kv_g, w_q, w_kv], rider=rider)


def _qkv_bwd(dq, dkv, x3, dx4, q_g, kv_g, w_q, w_kv):
    s_len = x3.shape[0]
    n = s_len // TM

    def body(dq_ref, dkv_ref, x_ref, dx_ref, qg_ref, kg_ref, wq_ref, wkv_ref,
             dx3_ref, dwq_ref, dwkv_ref, dqg_ref, dkg_ref, qacc, kacc):
        i = pl.program_id(0)

        @pl.when(i == 0)
        def _():
            qacc[...] = jnp.zeros_like(qacc)
            kacc[...] = jnp.zeros_like(kacc)
            dqg_ref[...] = jnp.zeros_like(dqg_ref)
            dkg_ref[...] = jnp.zeros_like(dkg_ref)

        x = x_ref[...]
        qg = qg_ref[1:2, :]
        kg = kg_ref[...]
        dq_v = dq_ref[...]
        dkv_v = dkv_ref[...].astype(BF)
        qacc[...] += _dot_tn(_rms_fwd(x, qg).astype(BF), dq_v)
        kacc[...] += _dot_tn(_rms_fwd(x, kg).astype(BF), dkv_v)
        dxq, prod_q = _rms_bwd(x, qg, _dot_nt(dq_v, wq_ref[...]))
        dxk, prod_k = _rms_bwd(x, kg, _dot_nt(dkv_v, wkv_ref[...]))
        dqg_ref[...] += _rowsum(prod_q)
        dkg_ref[...] += _rowsum(prod_k)
        dx3_ref[...] = dx_ref[...] + dxq + dxk

        @pl.when(i == n - 1)
        def _():
            dwq_ref[...] = qacc[...].astype(BF)
            dwkv_ref[...] = kacc[...].astype(BF)

    outs, _ = _call(
        body, name="qkv_bwd", grid=(n,),
        in_specs=[_row_spec(TM), _row_spec(TM, 2 * KVD), _row_spec(TM), _row_spec(TM), VSPEC, VSPEC, VSPEC, VSPEC],
        out_specs=[_row_spec(TM), _const_spec((D, D)), _const_spec((D, 2 * KVD)),
                   _const_spec((1, D)), _const_spec((1, D))],
        out_shape=[_sds((s_len, D)), _sds((D, D), BF), _sds((D, 2 * KVD), BF), _sds((1, D)), _sds((1, D))],
        scratch_shapes=[pltpu.VMEM((D, D), F32), pltpu.VMEM((D, 2 * KVD), F32)],
        args=[dq, dkv, x3, dx4, q_g, kv_g, w_q, w_kv])
    return outs


def _attn_block(i, q, kvw, sink_ref):
    off = jnp.where(i > 0, BLK, 0)
    rel = (lax.broadcasted_iota(jnp.int32, (BLK, 2 * BLK), 0)
           - lax.broadcasted_iota(jnp.int32, (BLK, 2 * BLK), 1) + off)
    valid = (rel >= 0) & (rel < BLK)
    relf = rel.astype(F32)
    out = []
    for h in range(N_HEADS):
        kh = h // GQA
        qh = q[:, h * HEAD_DIM:(h + 1) * HEAD_DIM]
        k = kvw[:, kh * HEAD_DIM:(kh + 1) * HEAD_DIM]
        v = kvw[:, KVD + kh * HEAD_DIM:KVD + (kh + 1) * HEAD_DIM]
        s = _dot_nt(qh, k) * ATT_SCALE - SLOPES[h] * relf
        s = jnp.where(valid, s, NEG_INF)
        sink = sink_ref[0, h]
        m = jnp.maximum(jnp.max(s, axis=-1, keepdims=True), sink)
        e = jnp.exp(s - m)
        es = jnp.exp(sink - m)
        inv = 1.0 / (jnp.sum(e, axis=-1, keepdims=True) + es)
        out.append((e * inv, es * inv, qh, k, v))
    return out


def _kv_window(kv_ref, i):
    ks = pl.multiple_of(jnp.maximum(i * BLK - BLK, 0), BLK)
    return ks, kv_ref[pl.ds(ks, 2 * BLK), :]


def _attn_fwd(q, kv, sinks, x3, w_o, post_g, rider=None):
    s_len = q.shape[0]

    def body(q_ref, kv_ref, sk_ref, x_ref, wo_ref, g_ref, a_ref, y_ref, x4_ref):
        i = pl.program_id(0)
        _, kvw = _kv_window(kv_ref, i)
        heads = _attn_block(i, q_ref[...], kvw, sk_ref)
        attn = jnp.concatenate([_dot(p.astype(BF), v) for p, _, _, _, v in heads], axis=1)
        a_ref[...] = attn
        y = _dot(attn.astype(BF), wo_ref[...])
        y_ref[...] = y
        x4_ref[...] = x_ref[...] + _rms_fwd(y, g_ref[1:2, :])

    return _call(body, name="attn_fwd", grid=(s_len // BLK,),
                 in_specs=[_row_spec(BLK), VSPEC, SSPEC, _row_spec(BLK), VSPEC, VSPEC],
                 out_specs=[_row_spec(BLK)] * 3, out_shape=[_sds((s_len, D))] * 3,
                 args=[q, kv, sinks, x3, w_o, post_g], rider=rider)


def _attn_bwd(dx4, y, attn, q, kv, sinks, w_o, post_g, rider=None):
    s_len = q.shape[0]
    n = s_len // BLK

    def body(dx_ref, y_ref, a_ref, q_ref, kv_ref, sk_ref, wo_ref, g_ref,
             dq_ref, dkv_ref, dwo_ref, dg_ref, dsk_ref, wacc):
        i = pl.program_id(0)

        @pl.when(i == 0)
        def _():
            dkv_ref[...] = jnp.zeros_like(dkv_ref)
            wacc[...] = jnp.zeros_like(wacc)
            dg_ref[...] = jnp.zeros_like(dg_ref)
            dsk_ref[...] = jnp.zeros_like(dsk_ref)

        dy, prod = _rms_bwd(y_ref[...], g_ref[1:2, :], dx_ref[...])
        dg_ref[...] += _rowsum(prod)
        dyb = dy.astype(BF)
        attn = a_ref[...]
        wacc[...] += _dot_tn(attn.astype(BF), dyb)
        d_o = _dot_nt(dyb, wo_ref[...])
        dod = d_o * attn
        ks, kvw = _kv_window(kv_ref, i)
        heads = _attn_block(i, q_ref[...], kvw, sk_ref)
        lane = lax.broadcasted_iota(jnp.int32, (1, D), 1)
        dqs = []
        dks = [None] * N_KV_HEADS
        dvs = [None] * N_KV_HEADS
        dsk = jnp.zeros((1, D), F32)
        for h, (p, ps, qh, k, v) in enumerate(heads):
            kh = h // GQA
            hs = slice(h * HEAD_DIM, (h + 1) * HEAD_DIM)
            do_h = d_o[:, hs].astype(BF)
            dsum = jnp.sum(dod[:, hs], axis=-1, keepdims=True)
            dp = _dot_nt(do_h, v)
            dsb = (p * (dp - dsum) * ATT_SCALE).astype(BF)
            dsk = dsk + jnp.where(lane == h, -_rowsum(ps * dsum), 0.0)
            dqs.append(_dot(dsb, k))
            dk = _dot_tn(dsb, qh)
            dv = _dot_tn(p.astype(BF), do_h)
            dks[kh] = dk if dks[kh] is None else dks[kh] + dk
            dvs[kh] = dv if dvs[kh] is None else dvs[kh] + dv
        dsk_ref[...] += dsk
        dq_ref[...] = jnp.concatenate(dqs, axis=1).astype(BF)
        dkv_ref[pl.ds(ks, 2 * BLK), :] += jnp.concatenate(dks + dvs, axis=1)

        @pl.when(i == n - 1)
        def _():
            dwo_ref[...] = wacc[...].astype(BF)

    return _call(
        body, name="attn_bwd", grid=(n,),
        in_specs=[_row_spec(BLK), _row_spec(BLK), _row_spec(BLK), _row_spec(BLK), VSPEC, SSPEC, VSPEC, VSPEC],
        out_specs=[_row_spec(BLK), _const_spec((s_len, 2 * KVD)), _const_spec((D, D)),
                   _const_spec((1, D)), _const_spec((1, D))],
        out_shape=[_sds((s_len, D), BF), _sds((s_len, 2 * KVD)), _sds((D, D), BF), _sds((1, D)), _sds((1, D))],
        scratch_shapes=[pltpu.VMEM((D, D), F32)],
        args=[dx4, y, attn, q, kv, sinks, w_o, post_g], rider=rider)


Big = collections.namedtuple("Big", "name src layer L A R C rb")


def _bigs():
    out = {"pool_w": Big("pool_w", "pool_w", None, 4, 4, POOL_G // N_CHIPS, POOL_G, 32)}
    for l in range(2):
        out[f"w_gu{l}"] = Big(f"w_gu{l}", "w_gu", l, 1, 2, D, FF_HALF, 256)
        out[f"w_down{l}"] = Big(f"w_down{l}", "w_down", l, 1, 4, FF // N_CHIPS, D, 352)
        out[f"w_ple_gate{l}"] = Big(f"w_ple_gate{l}", "w_ple_gate", l, 1, 4, D // N_CHIPS, D, 128)
        out[f"w_ple_proj{l}"] = Big(f"w_ple_proj{l}", "w_ple_proj", l, 1, 1, PLE, D // N_CHIPS, 128)
    out["w_q"] = Big("w_q", "w_q", None, 1, 4, D // N_CHIPS, D, 128)
    out["w_o"] = Big("w_o", "w_o", None, 1, 4, D // N_CHIPS, D, 128)
    out["w_kv"] = Big("w_kv", "w_kv", None, 1, 4, D // N_CHIPS, 2 * KVD, 128)
    return out


BIGS = _bigs()
POOL_SCALE = Big("pool_scale", "pool_scale", None, 1, 1, 1, D // N_CHIPS, 1)
BIG_SOURCES = ("w_gu", "w_down", "w_ple_gate", "w_ple_proj", "w_q", "w_o", "w_kv", "pool_w")


def _ncb(t):
    return N_CHIPS // t.A


def _full_shape(t, rows=None):
    return (t.L, t.A, t.R if rows is None else rows, _ncb(t) * t.C)


def _slot_index(t, k):
    return k // _ncb(t), k % _ncb(t)


def _slot(ref, t, k, row0, rows):
    a, cb = _slot_index(t, k)
    return ref.at[:, a, pl.ds(row0, rows), pl.ds(pl.multiple_of(cb * t.C, 128), t.C)]


def _place(t, w, kc, out_dtype):
    rb = min(t.R, 2 * t.rb)

    def body(kc_ref, w_ref, o_ref):
        del kc_ref
        o_ref[...] = w_ref[...].astype(out_dtype)

    def in_map(l, j, kc_ref):
        return (l if t.layer is None else t.layer, j, 0)

    def out_map(l, j, kc_ref):
        a, cb = _slot_index(t, kc_ref[0])
        return (l, a, j, cb)

    return pl.pallas_call(
        body, name=f"place_{t.name}",
        grid_spec=pltpu.PrefetchScalarGridSpec(
            num_scalar_prefetch=1, grid=(t.L, t.R // rb),
            in_specs=[pl.BlockSpec((None, rb, t.C), in_map)],
            out_specs=pl.BlockSpec((None, None, rb, t.C), out_map)),
        out_shape=_sds(_full_shape(t), out_dtype),
        compiler_params=_params(2),
    )(kc, w)


def _mesh_position():
    x, y, c = lax.axis_index("x"), lax.axis_index("y"), lax.axis_index("c")
    chips = [(1 - x, y), (x, 1 - y), (1 - x, 1 - y)]
    return x, y, c, chips


def _gather_rider(parts, fulls):
    nt = len(parts)
    TO_X, TO_Y, FWD_X, FWD_Y, SIB_X, SIB_Y, SIB_D = range(7)

    def rows_of(ti, core):
        t, r0, r1 = parts[ti]
        h = (r1 - r0) // 2
        return r0 + core * h, h

    def copy(outs, sems, kind, ti, k_src, row0, rows, dev):
        region = _slot(outs[ti], parts[ti][0], k_src, row0, rows)
        return pltpu.make_async_remote_copy(region, region, sems[0].at[ti, kind], sems[1].at[ti, kind],
                                            device_id=dev, device_id_type=MESH)

    def plan(outs, sems):
        x, y, c, _ = _mesh_position()
        me, kx, ky, kd = 2 * x + y, 2 * (1 - x) + y, 2 * x + (1 - y), 2 * (1 - x) + (1 - y)
        dev_x, dev_y, dev_d, sib = (1 - x, y, c), (x, 1 - y, c), (1 - x, 1 - y, c), (x, y, 1 - c)

        def whole(ti):
            return 0, parts[ti][0].R

        def mk(kind, k_send, k_recv, dev, send_rows, recv_rows):
            def build(ti, side):
                k_src = k_send if side == "s" else k_recv
                row0, rows = (send_rows if side == "s" else recv_rows)(ti)
                return copy(outs, sems, kind, ti, k_src, row0, rows, dev)
            return build

        def first_half(core):
            return lambda ti: (rows_of(ti, core)[0], rows_of(ti, core)[1] // 2)

        def second_half(core):
            return lambda ti: (rows_of(ti, core)[0] + rows_of(ti, core)[1] // 2, rows_of(ti, core)[1] // 2)

        mine = lambda ti: rows_of(ti, c)
        theirs = lambda ti: rows_of(ti, 1 - c)
        split = {
            TO_X: mk(TO_X, me, kx, dev_x, mine, mine),
            TO_Y: mk(TO_Y, me, ky, dev_y, mine, mine),
            FWD_X: mk(FWD_X, ky, kd, dev_x, first_half(c), first_half(c)),
            FWD_Y: mk(FWD_Y, kx, kd, dev_y, second_half(c), second_half(c)),
            SIB_X: mk(SIB_X, kx, kx, sib, mine, theirs),
            SIB_Y: mk(SIB_Y, ky, ky, sib, mine, theirs),
            SIB_D: mk(SIB_D, kd, kd, sib, mine, theirs),
        }
        direct = {
            TO_X: mk(TO_X, me, kx, dev_x, whole, whole),
            TO_Y: mk(TO_Y, me, ky, dev_y, whole, whole),
            FWD_X: mk(FWD_X, me, kd, dev_d, whole, whole),
        }
        return split, direct

    is_split = [t.R > 1 for t, _, _ in parts]

    def start(ins, outs, sems):
        split, direct = plan(outs, sems)
        for ti in range(nt):
            kinds = split if is_split[ti] else direct
            kinds[TO_X](ti, "s").start()
            kinds[TO_Y](ti, "s").start()
            if not is_split[ti]:
                kinds[FWD_X](ti, "s").start()

    def mid(ins, outs, sems):
        split, _ = plan(outs, sems)
        for ti in range(nt):
            if is_split[ti]:
                split[TO_Y](ti, "r").wait_recv()
                split[FWD_X](ti, "s").start()
                split[SIB_Y](ti, "s").start()
        for ti in range(nt):
            if is_split[ti]:
                split[TO_X](ti, "r").wait_recv()
                split[FWD_Y](ti, "s").start()
                split[SIB_X](ti, "s").start()

    def finish(ins, outs, sems):
        split, direct = plan(outs, sems)
        for ti in range(nt):
            if is_split[ti]:
                split[FWD_X](ti, "r").wait_recv()
                split[FWD_Y](ti, "r").wait_recv()
                split[SIB_D](ti, "s").start()
            else:
                for kind in (TO_X, TO_Y, FWD_X):
                    direct[kind](ti, "r").wait_recv()
        for ti in range(nt):
            if is_split[ti]:
                for kind in (SIB_X, SIB_Y, SIB_D):
                    split[kind](ti, "r").wait_recv()
        for ti in range(nt):
            kinds = split if is_split[ti] else direct
            for kind in kinds:
                kinds[kind](ti, "s").wait_send()

    sems = pltpu.SemaphoreType.DMA((nt, 7))
    return Rider(list(fulls), [_sds(a.shape, a.dtype) for a in fulls], {i: i for i in range(nt)},
                 [sems, sems], start, mid, finish)


def _pair_exchange(name, specs, grads):
    nt = len(specs)

    def body(*refs):
        gs = refs[:nt]
        lands = refs[nt:2 * nt]
        send, recv = refs[2 * nt:]
        x, y, c, _ = _mesh_position()
        cps = []
        for ti, t in enumerate(specs):
            half = t.R // 2
            cp = pltpu.make_async_remote_copy(gs[ti].at[:, :, pl.ds((1 - c) * half, half), :], lands[ti],
                                              send.at[ti], recv.at[ti],
                                              device_id=(x, y, 1 - c), device_id_type=MESH)
            cp.start()
            cps.append(cp)
        for cp in cps:
            cp.wait()

    return pl.pallas_call(
        body, name=name,
        in_specs=[ANYSPEC] * nt, out_specs=[ANYSPEC] * nt,
        out_shape=[_sds(_full_shape(t, t.R // 2), BF) for t in specs],
        scratch_shapes=[pltpu.SemaphoreType.DMA((nt,)), pltpu.SemaphoreType.DMA((nt,))],
        compiler_params=_params(),
    )(*grads)


def _pair_sum(t, g, land, kc):
    half = t.R // 2
    nj = half // t.rb
    w = _ncb(t) * t.C

    def body(kc_ref, g_ref, l_ref, o_ref):
        del kc_ref
        o_ref[...] = (g_ref[...].astype(F32) + l_ref[...].astype(F32)).astype(BF)

    return pl.pallas_call(
        body, name=f"pair_sum_{t.name}",
        grid_spec=pltpu.PrefetchScalarGridSpec(
            num_scalar_prefetch=1, grid=(t.L, nj),
            in_specs=[pl.BlockSpec((None, t.A, t.rb, w), lambda l, j, kc_ref: (l, 0, kc_ref[1] * nj + j, 0)),
                      pl.BlockSpec((None, t.A, t.rb, w), lambda l, j, kc_ref: (l, 0, j, 0))],
            out_specs=pl.BlockSpec((None, t.A, t.rb, w), lambda l, j, kc_ref: (l, 0, j, 0))),
        out_shape=_sds(_full_shape(t, half), BF),
        compiler_params=_params(2),
    )(kc, g, land)


def _scatter_rider(specs, sums):
    nt = len(specs)

    def copy(ins, outs, sems, ti, j, chip, c):
        t = specs[ti]
        cx, cy = chip
        return pltpu.make_async_remote_copy(_slot(ins[ti], t, 2 * cx + cy, 0, t.R // 2), outs[ti].at[j],
                                            sems[0].at[ti, j], sems[1].at[ti, j],
                                            device_id=(cx, cy, c), device_id_type=MESH)

    def start(ins, outs, sems):
        _, _, c, chips = _mesh_position()
        for j, chip in enumerate(chips):
            for ti in range(nt):
                copy(ins, outs, sems, ti, j, chip, c).start()

    def finish(ins, outs, sems):
        _, _, c, chips = _mesh_position()
        for j, chip in enumerate(chips):
            for ti in range(nt):
                copy(ins, outs, sems, ti, j, chip, c).wait()

    sems = pltpu.SemaphoreType.DMA((nt, N_CHIPS - 1))
    return Rider(list(sums), [_sds((N_CHIPS - 1, t.L, t.R // 2, t.C), BF) for t in specs], {}, [sems, sems],
                 start, None, finish)


def _chip_sum(t, s, land, kc, n_layers, prev):
    half = t.R // 2
    nj = half // t.rb

    def body(*refs):
        s_ref, l_ref, o_ref = refs[1], refs[2], refs[-1]
        acc = s_ref[...].astype(F32)
        for j in range(N_CHIPS - 1):
            acc = acc + l_ref[j].astype(F32)
        o_ref[...] = acc

    def own_map(l, j, kc_ref):
        a, cb = _slot_index(t, kc_ref[0])
        return (l, a, j, cb)

    def out_map(l, j, kc_ref):
        return (l if t.layer is None else t.layer, kc_ref[1] * nj + j, 0)

    in_specs = [pl.BlockSpec((None, None, t.rb, t.C), own_map),
                pl.BlockSpec((N_CHIPS - 1, None, t.rb, t.C), lambda l, j, kc_ref: (0, l, j, 0))]
    args = [kc, s, land]
    aliases = {}
    if prev is not None:
        in_specs.append(ANYSPEC)
        args.append(prev)
        aliases = {3: 0}
    return pl.pallas_call(
        body, name=f"chip_sum_{t.name}",
        grid_spec=pltpu.PrefetchScalarGridSpec(
            num_scalar_prefetch=1, grid=(t.L, nj), in_specs=in_specs,
            out_specs=pl.BlockSpec((None, t.rb, t.C), out_map)),
        out_shape=_sds((n_layers, t.R, t.C)),
        input_output_aliases=aliases,
        compiler_params=_params(2),
    )(*args)


def _chip_sum_fused(t, own, land, kc, n_layers, prev, by_cols):
    if by_cols:
        rows, cols = own.shape
    else:
        nb, rows, bw = own.shape
        cols = nb * bw
    nj = rows // t.rb

    def body(*refs):
        o_ref, l_ref, out_ref = refs[1], refs[2], refs[-1]
        acc = o_ref[...].astype(F32)
        for j in range(N_CHIPS - 1):
            acc = acc + l_ref[j].astype(F32)
        out_ref[...] = acc if by_cols else jnp.concatenate([acc[b] for b in range(nb)], axis=1)

    def out_map(j, kc_ref):
        return (t.layer, j, kc_ref[1]) if by_cols else (t.layer, kc_ref[1] * nj + j, 0)

    if by_cols:
        in_specs = [pl.BlockSpec((t.rb, cols), lambda j, kc_ref: (j, 0)),
                    pl.BlockSpec((N_CHIPS - 1, t.rb, cols), lambda j, kc_ref: (0, j, 0))]
    else:
        in_specs = [pl.BlockSpec((nb, t.rb, bw), lambda j, kc_ref: (0, j, 0)),
                    pl.BlockSpec((N_CHIPS - 1, nb, t.rb, bw), lambda j, kc_ref: (0, 0, j, 0))]
    args = [kc, own, land]
    aliases = {}
    if prev is not None:
        in_specs.append(ANYSPEC)
        args.append(prev)
        aliases = {3: 0}
    return pl.pallas_call(
        body, name=f"chip_sum_{t.name}",
        grid_spec=pltpu.PrefetchScalarGridSpec(
            num_scalar_prefetch=1, grid=(nj,), in_specs=in_specs,
            out_specs=pl.BlockSpec((None, t.rb, cols), out_map)),
        out_shape=_sds((n_layers, t.R, t.C)),
        input_output_aliases=aliases,
        compiler_params=_params(1),
    )(*args)


def _pair_share(halves, by_cols):
    nt = len(halves)

    def part(ref, ti, core):
        axis = 2 if by_cols[ti] else 1
        half = halves[ti].shape[axis] // 2
        piece = pl.ds(pl.multiple_of(core * half, 128 if by_cols[ti] else 8), half)
        return ref.at[:, :, piece] if by_cols[ti] else ref.at[:, piece, :]

    def body(*refs):
        outs = refs[nt:2 * nt]
        send, recv = refs[2 * nt:]
        x, y, c, _ = _mesh_position()
        cps = []
        for ti in range(nt):
            mine = part(outs[ti], ti, c)
            cp = pltpu.make_async_remote_copy(mine, mine, send.at[ti], recv.at[ti],
                                              device_id=(x, y, 1 - c), device_id_type=MESH)
            cp.start()
            cps.append(cp)
        for ti in range(nt):
            theirs = part(outs[ti], ti, 1 - c)
            pltpu.make_async_remote_copy(theirs, theirs, send.at[ti], recv.at[ti],
                                         device_id=(x, y, 1 - c), device_id_type=MESH).wait_recv()
        for cp in cps:
            cp.wait_send()

    return pl.pallas_call(
        body, name="grads_pair_share",
        in_specs=[ANYSPEC] * nt, out_specs=[ANYSPEC] * nt,
        out_shape=[_sds(a.shape, a.dtype) for a in halves],
        scratch_shapes=[pltpu.SemaphoreType.DMA((nt,)), pltpu.SemaphoreType.DMA((nt,))],
        input_output_aliases={i: i for i in range(nt)},
        compiler_params=_params(),
    )(*halves)


def _adamw_math(w, g, m, v):
    m = B1 * m + (1.0 - B1) * g
    v = B2 * v + (1.0 - B2) * (g * g)
    delta = -LR * ((m / BC1) / (jnp.sqrt(v / BC2) + AEPS) + WD * w)
    return delta, m, v


def _adamw(name, rb, w, g, m, v):
    n_layers, r, c = w.shape

    def body(w_ref, g_ref, m_ref, v_ref, go_ref, d_ref, nm_ref, nv_ref):
        g_v = g_ref[...]
        go_ref[...] = g_v
        d_ref[...], nm_ref[...], nv_ref[...] = _adamw_math(w_ref[...], g_v, m_ref[...], v_ref[...])

    spec = pl.BlockSpec((None, rb, c), lambda l, j: (l, j, 0))
    return pl.pallas_call(
        body, name=f"adamw_{name}", grid=(n_layers, r // rb),
        in_specs=[spec] * 4, out_specs=[spec] * 4, out_shape=[_sds(w.shape)] * 4,
        compiler_params=_params(2),
    )(w, g, m, v)


GAIN_ROWS = {"pre_mix_g": 0, "post_mix_g": 2, "pre_ffn_g": 4, "post_ffn_g": 6, "ple_g": 8, "ple_post_g": 10}
ROW_KV_G, ROW_POOL_SCALE, ROW_SINKS, ROW_LOSS, PACK_ROWS = 12, 13, 14, 15, 16
SMALL_NAMES = tuple(GAIN_ROWS) + ("kv_g", "pool_scale", "sinks")


def _small_all_reduce(rows, dpool):
    ng, pr = len(WINDOWS), POOL_G // N_CHIPS

    def body(*refs):
        row_refs = refs[:PACK_ROWS]
        dpool_ref, tot_ref, gpool_ref, pack, land, pland, send, recv, psend, precv = refs[PACK_ROWS:]
        x, y, c, _ = _mesh_position()
        me = 4 * x + 2 * y + c
        for r in range(PACK_ROWS):
            pack[r:r + 1, :] = row_refs[r][...]

        def shard_of(k):
            return dpool_ref.at[:, pl.ds(pl.multiple_of(k * pr, pr), pr), :]

        cps = []
        for j in range(1, N_DEV):
            px, py, pc = x ^ (j >> 2), y ^ ((j >> 1) & 1), c ^ (j & 1)
            cps.append(pltpu.make_async_remote_copy(pack, land.at[me], send.at[j], recv.at[j],
                                                    device_id=(px, py, pc), device_id_type=MESH))
            cps.append(pltpu.make_async_remote_copy(shard_of(2 * px + py), pland.at[me], psend.at[j], precv.at[j],
                                                    device_id=(px, py, pc), device_id_type=MESH))
        for cp in cps:
            cp.start()
        land[me] = pack[...]
        pland[me] = dpool_ref[:, pl.ds(pl.multiple_of((2 * x + y) * pr, pr), pr), :]
        for j in range(1, N_DEV):
            pltpu.make_async_remote_copy(pack, land.at[me ^ j], send.at[j], recv.at[j],
                                         device_id=(x, y, c), device_id_type=MESH).wait_recv()
            pltpu.make_async_remote_copy(shard_of(0), pland.at[me ^ j], psend.at[j], precv.at[j],
                                         device_id=(x, y, c), device_id_type=MESH).wait_recv()
        for cp in cps:
            cp.wait_send()
        tot = land[0]
        gp = pland[0].astype(F32)
        for d in range(1, N_DEV):
            tot = tot + land[d]
            gp = gp + pland[d].astype(F32)
        tot_ref[...] = tot
        gpool_ref[...] = gp

    sems = pltpu.SemaphoreType.DMA((N_DEV,))
    return pl.pallas_call(
        body, name="small_all_reduce",
        in_specs=[VSPEC] * (PACK_ROWS + 1), out_specs=[VSPEC, VSPEC],
        out_shape=[_sds((PACK_ROWS, D)), _sds((ng, pr, POOL_G))],
        scratch_shapes=[pltpu.VMEM((PACK_ROWS, D), F32), pltpu.VMEM((N_DEV, PACK_ROWS, D), F32),
                        pltpu.VMEM((N_DEV, ng, pr, POOL_G), BF), sems, sems, sems, sems],
        compiler_params=_params(),
    )(*rows, dpool)


def _small_adamw(tot, kc, small_w, small_m, small_v):
    names = SMALL_NAMES
    n = len(names)

    def body(*refs):
        tot_ref, kc_ref = refs[0], refs[1]
        w_refs = dict(zip(names, refs[2:2 + n]))
        m_refs = dict(zip(names, refs[2 + n:2 + 2 * n]))
        v_refs = dict(zip(names, refs[2 + 2 * n:2 + 3 * n]))
        loss_ref = refs[2 + 3 * n]
        out_refs = {nm: refs[3 + 3 * n + 4 * k: 7 + 3 * n + 4 * k] for k, nm in enumerate(names)}
        tot = tot_ref[...]
        loss_ref[...] = 0.5 * jnp.sum(tot[ROW_LOSS:ROW_LOSS + 1, :], axis=-1, keepdims=True) * (1.0 / D)

        def update(nm, g):
            g_ref, d_ref, nm_ref, nv_ref = out_refs[nm]
            g_ref[...] = g
            d_ref[...], nm_ref[...], nv_ref[...] = _adamw_math(w_refs[nm][...], g, m_refs[nm][...], v_refs[nm][...])

        for nm, r in GAIN_ROWS.items():
            update(nm, tot[r:r + 2, :])
        update("kv_g", tot[ROW_KV_G:ROW_KV_G + 1, :])
        k = kc_ref[0]
        width = D // N_CHIPS
        g_scale = jnp.zeros((1, width), F32)
        for kk in range(N_CHIPS):
            g_scale = g_scale + jnp.where(k == kk, tot[ROW_POOL_SCALE:ROW_POOL_SCALE + 1, kk * width:(kk + 1) * width], 0.0)
        update("pool_scale", g_scale)
        update("sinks", tot[ROW_SINKS:ROW_SINKS + 1, 0:N_HEADS])

    ins = [tot, kc] + [small_w[nm] for nm in names] + [small_m[nm] for nm in names] + [small_v[nm] for nm in names]
    out_shape = [_sds((1, 1))]
    for nm in names:
        out_shape += [_sds(small_w[nm].shape)] * 4
    outs = pl.pallas_call(
        body, name="small_adamw",
        in_specs=[VSPEC, SSPEC] + [VSPEC] * (3 * n), out_specs=[VSPEC] * len(out_shape), out_shape=out_shape,
        compiler_params=_params(),
    )(*ins)
    return outs[0], {nm: outs[1 + 4 * k: 5 + 4 * k] for k, nm in enumerate(names)}


def _compute_layout(t, full):
    if t.src == "w_gu":
        return full.reshape(2, D, FF)
    if t.src == "pool_w":
        return full.reshape(len(WINDOWS), POOL_G, POOL_G)
    if t.src == "pool_scale":
        return full.reshape(1, D)
    return full.reshape(t.A * t.R, _ncb(t) * t.C)


def kernel(x, p, pre_mix_g, post_mix_g, pre_ffn_g, post_ffn_g, pool_w, pool_scale, kv_g, w_kv, w_q, sinks, w_o, w_gu, w_down, ple_g, w_ple_gate, w_ple_proj, ple_post_g, loss_target, m_pre_mix_g, m_post_mix_g, m_pre_ffn_g, m_post_ffn_g, m_pool_w, m_pool_scale, m_kv_g, m_w_kv, m_w_q, m_sinks, m_w_o, m_w_gu, m_w_down, m_ple_g, m_w_ple_gate, m_w_ple_proj, m_ple_post_g, v_pre_mix_g, v_post_mix_g, v_pre_ffn_g, v_post_ffn_g, v_pool_w, v_pool_scale, v_kv_g, v_w_kv, v_w_q, v_sinks, v_w_o, v_w_gu, v_w_down, v_ple_g, v_w_ple_gate, v_w_ple_proj, v_ple_post_g):
    weights = dict(pre_mix_g=pre_mix_g, post_mix_g=post_mix_g, pre_ffn_g=pre_ffn_g, post_ffn_g=post_ffn_g,
                   pool_w=pool_w, pool_scale=pool_scale, kv_g=kv_g, w_kv=w_kv, w_q=w_q, sinks=sinks, w_o=w_o,
                   w_gu=w_gu, w_down=w_down, ple_g=ple_g, w_ple_gate=w_ple_gate, w_ple_proj=w_ple_proj,
                   ple_post_g=ple_post_g)
    m_in = dict(pre_mix_g=m_pre_mix_g, post_mix_g=m_post_mix_g, pre_ffn_g=m_pre_ffn_g, post_ffn_g=m_post_ffn_g,
                pool_w=m_pool_w, pool_scale=m_pool_scale, kv_g=m_kv_g, w_kv=m_w_kv, w_q=m_w_q, sinks=m_sinks,
                w_o=m_w_o, w_gu=m_w_gu, w_down=m_w_down, ple_g=m_ple_g, w_ple_gate=m_w_ple_gate,
                w_ple_proj=m_w_ple_proj, ple_post_g=m_ple_post_g)
    v_in = dict(pre_mix_g=v_pre_mix_g, post_mix_g=v_post_mix_g, pre_ffn_g=v_pre_ffn_g, post_ffn_g=v_post_ffn_g,
                pool_w=v_pool_w, pool_scale=v_pool_scale, kv_g=v_kv_g, w_kv=v_w_kv, w_q=v_w_q, sinks=v_sinks,
                w_o=v_w_o, w_gu=v_w_gu, w_down=v_w_down, ple_g=v_ple_g, w_ple_gate=v_w_ple_gate,
                w_ple_proj=v_w_ple_proj, ple_post_g=v_ple_post_g)
    order = ["pre_mix_g", "post_mix_g", "pre_ffn_g", "post_ffn_g", "pool_w", "pool_scale", "kv_g", "w_kv", "w_q",
             "sinks", "w_o", "w_gu", "w_down", "ple_g", "w_ple_gate", "w_ple_proj", "ple_post_g"]

    kc = jnp.stack([2 * lax.axis_index("x") + lax.axis_index("y"), lax.axis_index("c")]).astype(jnp.int32)
    s_len = x.shape[1]
    x2d = x.reshape(s_len, D)
    p3d = p.reshape(2, s_len, PLE)
    target = loss_target.reshape(s_len, D)
    kv_g2d = kv_g.reshape(1, D)
    gains = {nm: weights[nm] for nm in GAIN_ROWS}

    def shard_view(src, a):
        t = next(t for t in BIGS.values() if t.src == src)
        return a.reshape(-1, t.R, t.C)

    placed = {nm: _place(t, shard_view(t.src, weights[t.src]), kc, BF) for nm, t in BIGS.items()}
    placed["pool_scale"] = _place(POOL_SCALE, pool_scale.reshape(1, 1, D // N_CHIPS), kc, F32)
    specs = dict(BIGS, pool_scale=POOL_SCALE)

    def gather(names, rows=None):
        rows = rows or {}
        parts = [(specs[nm],) + tuple(rows.get(nm, (0, specs[nm].R))) for nm in names]
        return _gather_rider(parts, [placed[nm] for nm in names])

    def take(names, results):
        for nm, a in zip(names, results):
            placed[nm] = a

    def weight(nm):
        return _compute_layout(specs[nm], placed[nm])

    first = ["pool_w", "pool_scale", "w_gu0", "w_down0"]
    take(first, _run("weights_gather_first", gather(first)))

    y0, x1 = _mixa_fwd(x2d, gains["pre_mix_g"], weight("pool_w"), weight("pool_scale"), gains["post_mix_g"])

    ride = ["w_ple_gate0", "w_ple_proj0", "w_q", "w_kv", "w_o", "w_gu1"]
    (f0, x2), got = _ffn_fwd(0, x1, gains["pre_ffn_g"], weight("w_gu0"), weight("w_down0"), gains["post_ffn_g"],
                             rider=gather(ride, {"w_gu1": (0, 320)}))
    take(ride, got)

    ride = ["w_ple_gate1", "w_ple_proj1", "w_gu1"]
    (z0, pe0, x3), got = _ple_fwd(0, x2, p3d, gains["ple_g"], weight("w_ple_gate0"), weight("w_ple_proj0"),
                                  gains["ple_post_g"], rider=gather(ride, {"w_gu1": (320, 448)}))
    take(ride, got)

    ride = ["w_gu1"]
    (q, kv), got = _qkv_fwd(x3, gains["pre_mix_g"], kv_g2d, weight("w_q"), weight("w_kv"),
                            rider=gather(ride, {"w_gu1": (448, 704)}))
    take(ride, got)

    ride = ["w_down1", "w_gu1"]
    (attn, y1, x4), got = _attn_fwd(q, kv, sinks, x3, weight("w_o"), gains["post_mix_g"],
                                    rider=gather(ride, {"w_gu1": (704, D)}))
    take(ride, got)

    (f1, x5), _ = _ffn_fwd(1, x4, gains["pre_ffn_g"], weight("w_gu1"), weight("w_down1"), gains["post_ffn_g"])
    (z1, pe1, dx6, loss_row), _ = _ple_fwd(1, x5, p3d, gains["ple_g"], weight("w_ple_gate1"), weight("w_ple_proj1"),
                                           gains["ple_post_g"], target=target)

    local = {}
    landed = {}
    fused = {}

    def pair_stage(tag, names):
        ts = [BIGS[nm] for nm in names]
        gs = [local[nm].reshape(_full_shape(t)) for nm, t in zip(names, ts)]
        lands = _pair_exchange(f"grads_pair_exchange_{tag}", ts, gs)
        return [_pair_sum(t, g, l, kc) for t, g, l in zip(ts, gs, lands)]

    def scatter(names, sums):
        return _scatter_rider([BIGS[nm] for nm in names], sums)

    def keep(names, sums, got):
        for nm, s, l in zip(names, sums, got):
            landed[nm] = (s, l)

    (dx5, local["w_ple_gate1"], local["w_ple_proj1"], d_ple1, d_plepost1), _ = _ple_bwd(
        1, dx6, x5, z1, pe1, p3d, gains["ple_g"], weight("w_ple_gate1"), gains["ple_post_g"])

    group_a = ["w_ple_gate1", "w_ple_proj1"]
    sums_a = pair_stage("a", group_a)
    (dx4, d_preffn1, d_postffn1, *scattered), got = _ffn_bwd(
        1, dx5, x4, f1, gains["pre_ffn_g"], weight("w_gu1"), weight("w_down1"), gains["post_ffn_g"], kc,
        rider=scatter(group_a, sums_a))
    fused["w_gu1"], fused["w_down1"] = scattered[0:2], scattered[2:4]
    keep(group_a, sums_a, got)

    (dq, dkv, local["w_o"], d_postmix1, d_sinks), _ = _attn_bwd(
        dx4, y1, attn, q, kv, sinks, weight("w_o"), gains["post_mix_g"])
    dx3, local["w_q"], local["w_kv"], d_premix1, d_kvg = _qkv_bwd(
        dq, dkv, x3, dx4, gains["pre_mix_g"], kv_g2d, weight("w_q"), weight("w_kv"))

    group_b = ["w_o", "w_q", "w_kv"]
    sums_b = pair_stage("b", group_b)
    (dx2, local["w_ple_gate0"], local["w_ple_proj0"], d_ple0, d_plepost0), got = _ple_bwd(
        0, dx3, x2, z0, pe0, p3d, gains["ple_g"], weight("w_ple_gate0"), gains["ple_post_g"],
        rider=scatter(group_b, sums_b))
    keep(group_b, sums_b, got)

    group_c = ["w_ple_gate0", "w_ple_proj0"]
    sums_c = pair_stage("c", group_c)
    (dx1, d_preffn0, d_postffn0, *scattered), got = _ffn_bwd(
        0, dx2, x1, f0, gains["pre_ffn_g"], weight("w_gu0"), weight("w_down0"), gains["post_ffn_g"], kc,
        rider=scatter(group_c, sums_c))
    fused["w_gu0"], fused["w_down0"] = scattered[0:2], scattered[2:4]
    keep(group_c, sums_c, got)

    dx0, d_pool, d_scale, d_postmix0, d_premix0 = _mixa_bwd(
        dx1, x2d, y0, gains["pre_mix_g"], weight("pool_w"), weight("pool_scale"), gains["post_mix_g"])

    rows = [d_premix0, d_premix1, d_postmix0, d_postmix1, d_preffn0, d_preffn1, d_postffn0, d_postffn1,
            d_ple0, d_ple1, d_plepost0, d_plepost1, d_kvg, d_scale, d_sinks, loss_row]
    as2d = lambda a: a.reshape(1, D) if a.ndim == 1 else a
    tot, g_pool = _small_all_reduce(rows, d_pool)
    loss, small = _small_adamw(tot, kc, {nm: as2d(weights[nm]) for nm in SMALL_NAMES},
                               {nm: as2d(m_in[nm]) for nm in SMALL_NAMES},
                               {nm: as2d(v_in[nm]) for nm in SMALL_NAMES})

    shared = [src for src in BIG_SOURCES if src != "pool_w"]
    halves = []
    for src in shared:
        n_layers = shard_view(src, weights[src]).shape[0]
        acc = None
        for t in [t for t in BIGS.values() if t.src == src]:
            if t.name in fused:
                own, land = fused[t.name]
                acc = _chip_sum_fused(t, own, land, kc, n_layers, acc, by_cols=src == "w_down")
            else:
                s, l = landed[t.name]
                acc = _chip_sum(t, s, l, kc, n_layers, acc)
        halves.append(acc)
    full_grads = dict(zip(shared, _pair_share(halves, [src == "w_down" for src in shared])))
    full_grads["pool_w"] = g_pool

    out = {"grad": {}, "delta": {}, "new_m": {}, "new_v": {}}
    for src in BIG_SOURCES:
        g = full_grads[src]
        t = next(t for t in BIGS.values() if t.src == src)
        res = _adamw(src, t.rb, shard_view(src, weights[src]), g, shard_view(src, m_in[src]), shard_view(src, v_in[src]))
        shape = weights[src].shape
        for kind, a in zip(("grad", "delta", "new_m", "new_v"), res):
            out[kind][src] = a.reshape(shape)
    for nm in SMALL_NAMES:
        shape = weights[nm].shape
        for kind, a in zip(("grad", "delta", "new_m", "new_v"), small[nm]):
            out[kind][nm] = a.reshape(shape)

    return (loss.reshape(()), dx0.reshape(x.shape),
            *[out["grad"][nm] for nm in order], *[out["delta"][nm] for nm in order],
            *[out["new_m"][nm] for nm in order], *[out["new_v"][nm] for nm in order])
```

```python
import collections

import jax
import jax.numpy as jnp
from jax import lax
from jax.experimental import pallas as pl
from jax.experimental.pallas import tpu as pltpu

D = 1024
FF = 2816
N_HEADS = 16
HEAD_DIM = 64
N_KV_HEADS = 4
GQA = N_HEADS // N_KV_HEADS
KVD = N_KV_HEADS * HEAD_DIM
PLE = 256
BLK = 128
WINDOWS = (2, 4, 8, 16)
POOL_G = 256
HALO = 16
EPS = 1e-6
NEG_INF = -1e30
ATT_SCALE = HEAD_DIM ** -0.5
SLOPES = tuple(2.0 ** (-8.0 * (h + 1) / N_HEADS) for h in range(N_HEADS))
N_CHIPS = 4
N_DEV = 8

LR, B1, B2, AEPS, WD, STEP = 0.001, 0.9, 0.999, 1e-08, 0.01, 10
BC1 = 1.0 - B1 ** STEP
BC2 = 1.0 - B2 ** STEP

BF = jnp.bfloat16
F32 = jnp.float32
MESH = pl.DeviceIdType.MESH
VMEM_LIMIT_V7X = 58 * 1024 * 1024
TM = 256
TM_FFN_BWD = 512
FF_CHUNK = 256
FF_HALF = FF // 2

VSPEC = pl.BlockSpec(memory_space=pltpu.VMEM)
SSPEC = pl.BlockSpec(memory_space=pltpu.SMEM)
ANYSPEC = pl.BlockSpec(memory_space=pl.ANY)


def _params(n_grid=0):
    sem = ("arbitrary",) * n_grid if n_grid else None
    return pltpu.CompilerParams(dimension_semantics=sem, vmem_limit_bytes=VMEM_LIMIT_V7X)


def _sds(shape, dtype=F32):
    return jax.ShapeDtypeStruct(tuple(shape), dtype)


Rider = collections.namedtuple("Rider", "arrays out_shapes aliases scratch start mid finish")
MID_NUM, MID_DEN = 5, 8


def _call(body, *, name, grid, in_specs, out_specs, out_shape, args, scratch_shapes=(), rider=None, prefetch=None):
    ni, no, ns = len(in_specs), len(out_specs), len(scratch_shapes)
    npre = 0 if prefetch is None else 1
    pre = [] if prefetch is None else [prefetch]
    if rider is None:
        rider = Rider([], [], {}, [], None, None, None)
    ri, ro = len(rider.arrays), len(rider.out_shapes)

    def full(*refs):
        pre_refs, refs = refs[:npre], refs[npre:]
        ins, refs = refs[:ni], refs[ni:]
        rins, refs = refs[:ri], refs[ri:]
        outs, refs = refs[:no], refs[no:]
        routs, refs = refs[:ro], refs[ro:]
        scr, rscr = refs[:ns], refs[ns:]
        ids = [pl.program_id(a) for a in range(len(grid))]
        first = ids[0] == 0
        last = ids[0] == grid[0] - 1
        for a in range(1, len(grid)):
            first = first & (ids[a] == 0)
            last = last & (ids[a] == grid[a] - 1)

        if rider.start is not None:
            @pl.when(first)
            def _():
                rider.start(rins, routs, rscr)

        if rider.mid is not None:
            assert len(grid) == 1

            @pl.when(ids[0] == (grid[0] * MID_NUM) // MID_DEN)
            def _():
                rider.mid(rins, routs, rscr)

        body(*pre_refs, *ins, *outs, *scr)

        if rider.finish is not None:
            @pl.when(last)
            def _():
                rider.finish(rins, routs, rscr)

    outs = pl.pallas_call(
        full, name=name,
        grid_spec=pltpu.PrefetchScalarGridSpec(
            num_scalar_prefetch=npre, grid=grid,
            in_specs=list(in_specs) + [ANYSPEC] * ri, out_specs=list(out_specs) + [ANYSPEC] * ro,
            scratch_shapes=list(scratch_shapes) + list(rider.scratch)),
        out_shape=list(out_shape) + list(rider.out_shapes),
        input_output_aliases={npre + ni + a: no + b for a, b in rider.aliases.items()},
        compiler_params=_params(len(grid)))(*pre, *args, *rider.arrays)
    return list(outs[:no]), list(outs[no:])


def _run(name, rider):
    ri = len(rider.arrays)

    def body(*refs):
        rins, routs, rscr = refs[:ri], refs[ri:ri + len(rider.out_shapes)], refs[ri + len(rider.out_shapes):]
        rider.start(rins, routs, rscr)
        if rider.mid is not None:
            rider.mid(rins, routs, rscr)
        rider.finish(rins, routs, rscr)

    return pl.pallas_call(
        body, name=name, in_specs=[ANYSPEC] * ri, out_specs=[ANYSPEC] * len(rider.out_shapes),
        out_shape=list(rider.out_shapes), scratch_shapes=list(rider.scratch),
        input_output_aliases=dict(rider.aliases), compiler_params=_params())(*rider.arrays)


def _rms_fwd(x, g):
    r = lax.rsqrt(jnp.mean(x * x, axis=-1, keepdims=True) + EPS)
    return x * r * g


def _rms_bwd(x, g, dy):
    r = lax.rsqrt(jnp.mean(x * x, axis=-1, keepdims=True) + EPS)
    xn = x * r
    dxn = dy * g
    dx = r * (dxn - xn * jnp.mean(dxn * xn, axis=-1, keepdims=True))
    return dx, dy * xn


def _rowsum(a):
    return jnp.sum(a, axis=0, keepdims=True)


def _sigmoid(z):
    return 1.0 / (1.0 + jnp.exp(-z))


def _dot(a, b):
    return jnp.dot(a, b, preferred_element_type=F32)


def _dot_nt(a, b):
    return lax.dot_general(a, b, (((1,), (1,)), ((), ())), preferred_element_type=F32)


def _dot_tn(a, b):
    return lax.dot_general(a, b, (((0,), (0,)), ((), ())), preferred_element_type=F32)


def _row_spec(tm, width=D):
    return pl.BlockSpec((tm, width), lambda i: (i, 0))


def _const_spec(shape):
    zeros = (0,) * len(shape)
    return pl.BlockSpec(tuple(shape), lambda *_: zeros)


def _pool_delta(he, pos):
    out = []
    for gi, w in enumerate(WINDOWS):
        hg = he[:, gi * POOL_G:(gi + 1) * POOL_G]
        s = hg
        k = 1
        while k < w:
            s = s + pltpu.roll(s, k, 0)
            k *= 2
        cnt = jnp.maximum(jnp.minimum(pos + 1, w), 1).astype(F32)
        out.append(s / cnt - hg)
    return out


def _load_with_halo_before(x_ref, i, tm):
    r0 = pl.multiple_of(i * tm, tm)
    hs = pl.multiple_of(jnp.maximum(i * tm - HALO, 0), 8)
    xh = jnp.where(i > 0, x_ref[pl.ds(hs, HALO), :], 0.0)
    xt = x_ref[pl.ds(r0, tm), :]
    return xt, jnp.concatenate([xh, xt], axis=0)


def _mixa_fwd(x, pre_g, pool_w, pool_scale, post_g):
    s_len = x.shape[0]
    n = s_len // TM

    def body(x_ref, pg_ref, w_ref, sc_ref, qg_ref, y_ref, x1_ref):
        i = pl.program_id(0)
        xt, xe = _load_with_halo_before(x_ref, i, TM)
        he = _rms_fwd(xe, pg_ref[0:1, :])
        pos = i * TM - HALO + lax.broadcasted_iota(jnp.int32, (TM + HALO, 1), 0)
        ds = _pool_delta(he, pos)
        ys = [_dot(ds[gi][HALO:, :].astype(BF), w_ref[gi]) for gi in range(len(WINDOWS))]
        y = jnp.concatenate(ys, axis=1) * sc_ref[...]
        y_ref[...] = y
        x1_ref[...] = xt + _rms_fwd(y, qg_ref[0:1, :])

    outs, _ = _call(body, name="mixa_fwd", grid=(n,),
                    in_specs=[VSPEC] * 5, out_specs=[_row_spec(TM), _row_spec(TM)],
                    out_shape=[_sds((s_len, D)), _sds((s_len, D))],
                    args=[x, pre_g, pool_w, pool_scale, post_g])
    return outs


def _mixa_bwd(dx1, x, y, pre_g, pool_w, pool_scale, post_g):
    s_len = x.shape[0]
    n = s_len // TM
    ng = len(WINDOWS)

    def body(dx_ref, x_ref, y_ref, pg_ref, w_ref, sc_ref, qg_ref,
             dx0_ref, dw_ref, dsc_ref, dqg_ref, dpg_ref, wacc):
        i = pl.program_id(0)

        @pl.when(i == 0)
        def _():
            wacc[...] = jnp.zeros_like(wacc)
            dsc_ref[...] = jnp.zeros_like(dsc_ref)
            dqg_ref[...] = jnp.zeros_like(dqg_ref)
            dpg_ref[...] = jnp.zeros_like(dpg_ref)

        r0 = pl.multiple_of(i * TM, TM)
        xt, xe = _load_with_halo_before(x_ref, i, TM)
        he = _rms_fwd(xe, pg_ref[0:1, :])
        pos_b = i * TM - HALO + lax.broadcasted_iota(jnp.int32, (TM + HALO, 1), 0)
        ds = _pool_delta(he, pos_b)

        last = i == n - 1
        a0 = pl.multiple_of(jnp.minimum(i * TM + TM, s_len - HALO), 8)
        ye = jnp.concatenate([y_ref[pl.ds(r0, TM), :], y_ref[pl.ds(a0, HALO), :]], axis=0)
        dt = dx_ref[pl.ds(r0, TM), :]
        de = jnp.concatenate([dt, jnp.where(last, 0.0, dx_ref[pl.ds(a0, HALO), :])], axis=0)
        dye, prod = _rms_bwd(ye, qg_ref[0:1, :], de)
        dqg_ref[...] += _rowsum(prod[:TM, :])
        dys = dye * sc_ref[...]
        pos_a = i * TM + lax.broadcasted_iota(jnp.int32, (TM + HALO, 1), 0)

        dhs, dscs = [], []
        for gi, w in enumerate(WINDOWS):
            sl = slice(gi * POOL_G, (gi + 1) * POOL_G)
            wg = w_ref[gi]
            dys_g = dys[:, sl].astype(BF)
            d_g = ds[gi][HALO:, :].astype(BF)
            ypre = _dot(d_g, wg)
            dscs.append(_rowsum(dye[:TM, sl] * ypre))
            wacc[gi] += _dot_tn(d_g, dys_g[:TM, :])
            dd = _dot_nt(dys_g, wg)
            cnt = jnp.minimum(pos_a + 1, w).astype(F32)
            a = dd / cnt
            k = 1
            while k < w:
                a = a + pltpu.roll(a, TM + HALO - k, 0)
                k *= 2
            dhs.append(a[:TM, :] - dd[:TM, :])
        dsc_ref[...] += jnp.concatenate(dscs, axis=1)
        dh = jnp.concatenate(dhs, axis=1)
        dxp, prod2 = _rms_bwd(xt, pg_ref[0:1, :], dh)
        dpg_ref[...] += _rowsum(prod2)
        dx0_ref[...] = dt + dxp

        @pl.when(last)
        def _():
            dw_ref[...] = wacc[...].astype(BF)

    outs, _ = _call(
        body, name="mixa_bwd", grid=(n,), in_specs=[VSPEC] * 7,
        out_specs=[_row_spec(TM), _const_spec((ng, POOL_G, POOL_G)), _const_spec((1, D)),
                   _const_spec((1, D)), _const_spec((1, D))],
        out_shape=[_sds((s_len, D)), _sds((ng, POOL_G, POOL_G), BF), _sds((1, D)), _sds((1, D)), _sds((1, D))],
        scratch_shapes=[pltpu.VMEM((ng, POOL_G, POOL_G), F32)],
        args=[dx1, x, y, pre_g, pool_w, pool_scale, post_g])
    return outs


def _ffn_fwd(layer, x1, pre_g, wgu, wd, post_g, rider=None):
    s_len = x1.shape[0]

    def body(x_ref, pg_ref, wgu_ref, wd_ref, qg_ref, f_ref, x2_ref):
        x = x_ref[...]
        h = _rms_fwd(x, pg_ref[layer:layer + 1, :]).astype(BF)
        f = jnp.zeros((TM, D), F32)
        for c in range(FF // FF_HALF):
            cols = slice(c * FF_HALF, (c + 1) * FF_HALF)
            g = _dot(h, wgu_ref[0, :, cols])
            u = _dot(h, wgu_ref[1, :, cols])
            act = g * _sigmoid(g) * u
            f = f + _dot(act.astype(BF), wd_ref[cols, :])
        f_ref[...] = f
        x2_ref[...] = x + _rms_fwd(f, qg_ref[layer:layer + 1, :])

    return _call(body, name=f"ffn_fwd{layer}", grid=(s_len // TM,),
                 in_specs=[_row_spec(TM), VSPEC, VSPEC, VSPEC, VSPEC],
                 out_specs=[_row_spec(TM), _row_spec(TM)],
                 out_shape=[_sds((s_len, D)), _sds((s_len, D))],
                 args=[x1, pre_g, wgu, wd, post_g], rider=rider)


GU_PIECE = 128
DN_PIECE = 64
DN_SLOT = FF // N_CHIPS
HALF_D = D // 2


def _ffn_bwd(layer, dx2, x1, f, pre_g, wgu, wd, post_g, kc, rider=None):
    s_len = x1.shape[0]
    tm = TM_FFN_BWD
    n = s_len // tm
    nc = FF // FF_CHUNK
    n_gu, n_dn = FF_CHUNK // GU_PIECE, FF_CHUNK // DN_PIECE
    n_pieces = 2 * n_gu + n_dn
    n_blk = FF_HALF // GU_PIECE

    def edge_rows(c, i, kc_ref):
        return (jnp.where((c == 0) | (c == nc - 1), i, n - 1), 0)

    def chunk_at(c, kc_ref):
        return (c + (kc_ref[0] * nc) // N_CHIPS) % nc

    def exchange(kc_ref, c, accg, accu, accd, own_gu_ref, land_gu_ref, own_dn_ref, land_dn_ref,
                 pl_gu, pl_dn, sib_gu, sib_dn, mine_gu, mine_dn, sum_gu, sum_dn,
                 psend, precv, ssend, lsem, rrecv):
        x, y, core = lax.axis_index("x"), lax.axis_index("y"), lax.axis_index("c")
        lower = core == 0

        def pair_copy(cc, part):
            p = cc % 2
            src, dst = ((sib_gu, pl_gu), (sib_dn, pl_dn))[part]
            return pltpu.make_async_remote_copy(src.at[p], dst.at[cc], psend.at[p, part], precv.at[cc, part],
                                                device_id=(x, y, 1 - core), device_id_type=MESH)

        def scatter(cc, wait):
            p = cc % 2
            jobs = []
            for gu in range(2):
                for hc in range(n_gu):
                    hidden = chunk_at(cc, kc_ref) * FF_CHUNK + hc * GU_PIECE
                    k = hidden // FF_HALF
                    jobs.append((sum_gu.at[p, gu, hc], k + 2 * gu, 0,
                                 own_gu_ref, land_gu_ref, ((hidden - k * FF_HALF) // GU_PIECE,)))
            for q in range(n_dn):
                hidden = chunk_at(cc, kc_ref) * FF_CHUNK + q * DN_PIECE
                k = hidden // DN_SLOT
                off = pl.multiple_of(hidden - k * DN_SLOT, DN_PIECE)
                jobs.append((sum_dn.at[p, pl.ds(q * DN_PIECE, DN_PIECE), :], k, 1,
                             own_dn_ref, land_dn_ref, (pl.ds(off, DN_PIECE), slice(None))))
            for pi, (src, k, t, own_ref, land_ref, where) in enumerate(jobs):
                kx, ky = k // 2, k % 2
                fx, fy = (kx != x).astype(jnp.int32), (ky != y).astype(jnp.int32)
                local = (fx + fy) == 0
                j = jnp.maximum(fx + 2 * fy - 1, 0)

                @pl.when(local)
                def _():
                    cp = pltpu.make_async_copy(src, own_ref.at[where], lsem.at[p, pi])
                    if wait:
                        cp.wait()
                    else:
                        cp.start()

                @pl.when(jnp.logical_not(local))
                def _():
                    cp = pltpu.make_async_remote_copy(src, land_ref.at[(j,) + where], ssend.at[p, pi],
                                                      rrecv.at[t, j], device_id=(kx, ky, core), device_id_type=MESH)
                    if wait:
                        cp.wait_send()
                    else:
                        cp.start()

        def add_and_scatter(cc):
            p = cc % 2
            pair_copy(cc, 0).wait_recv()
            pair_copy(cc, 1).wait_recv()
            s_gu = (mine_gu[...] + pl_gu[cc].astype(F32)).astype(BF)
            for hc in range(n_gu):
                sum_gu[p, :, hc] = s_gu[:, :, hc * GU_PIECE:(hc + 1) * GU_PIECE]
            sum_dn[p] = (mine_dn[...] + pl_dn[cc].astype(F32)).astype(BF)
            scatter(cc, wait=False)

        @pl.when(c >= 1)
        def _():
            @pl.when(c >= 3)
            def _():
                scatter(c - 3, wait=True)
            add_and_scatter(c - 1)

        @pl.when(c >= 2)
        def _():
            pair_copy(c - 2, 0).wait_send()
            pair_copy(c - 2, 1).wait_send()

        p = c % 2
        g_v, u_v, d_v = accg[...], accu[...], accd[...]
        sib_gu[p, 0] = jnp.where(lower, g_v[HALF_D:, :], g_v[:HALF_D, :]).astype(BF)
        sib_gu[p, 1] = jnp.where(lower, u_v[HALF_D:, :], u_v[:HALF_D, :]).astype(BF)
        sib_dn[p] = jnp.where(lower, d_v[:, HALF_D:], d_v[:, :HALF_D]).astype(BF)
        mine_gu[0] = jnp.where(lower, g_v[:HALF_D, :], g_v[HALF_D:, :])
        mine_gu[1] = jnp.where(lower, u_v[:HALF_D, :], u_v[HALF_D:, :])
        mine_dn[...] = jnp.where(lower, d_v[:, :HALF_D], d_v[:, HALF_D:])
        pair_copy(c, 0).start()
        pair_copy(c, 1).start()

        @pl.when(c == nc - 1)
        def _():
            scatter(nc - 3, wait=True)
            add_and_scatter(nc - 1)
            for cc in (nc - 2, nc - 1):
                pair_copy(cc, 0).wait_send()
                pair_copy(cc, 1).wait_send()
                scatter(cc, wait=True)
            for t, land_ref in enumerate((land_gu_ref, land_dn_ref)):
                for j in range(N_CHIPS - 1):
                    pltpu.make_async_remote_copy(land_ref.at[j], land_ref.at[j], ssend.at[0, 0], rrecv.at[t, j],
                                                 device_id=(x, y, core), device_id_type=MESH).wait_recv()

    def body(kc_ref, dx_ref, x_ref, f_ref, pg_ref, wgu_ref, wd_ref, qg_ref,
             dx1_ref, dpg_ref, dqg_ref, own_gu_ref, land_gu_ref, own_dn_ref, land_dn_ref,
             h_s, df_s, dh_s, accg, accu, accd, *comm):
        c = pl.program_id(0)
        i = pl.program_id(1)
        rows = pl.ds(pl.multiple_of(i * tm, tm), tm)
        pg = pg_ref[layer:layer + 1, :]

        @pl.when((c == 0) & (i == 0))
        def _():
            dpg_ref[...] = jnp.zeros_like(dpg_ref)
            dqg_ref[...] = jnp.zeros_like(dqg_ref)

        @pl.when(c == 0)
        def _():
            h_s[rows, :] = _rms_fwd(x_ref[...], pg).astype(BF)
            df, prod = _rms_bwd(f_ref[...], qg_ref[layer:layer + 1, :], dx_ref[...])
            df_s[rows, :] = df.astype(BF)
            dqg_ref[...] += _rowsum(prod)

        @pl.when(i == 0)
        def _():
            accg[...] = jnp.zeros_like(accg)
            accu[...] = jnp.zeros_like(accu)
            accd[...] = jnp.zeros_like(accd)

        h = h_s[rows, :]
        df = df_s[rows, :]
        wg = wgu_ref[0]
        wu = wgu_ref[1]
        g = _dot(h, wg)
        u = _dot(h, wu)
        sg = _sigmoid(g)
        a = g * sg
        dact = _dot_nt(df, wd_ref[...])
        accd[...] += _dot_tn((a * u).astype(BF), df)
        du = (dact * a).astype(BF)
        dg = (dact * u * (sg * (1.0 + g * (1.0 - sg)))).astype(BF)
        accg[...] += _dot_tn(h, dg)
        accu[...] += _dot_tn(h, du)
        dh = _dot_nt(dg, wg) + _dot_nt(du, wu)

        @pl.when(c == 0)
        def _():
            dh_s[rows, :] = dh

        @pl.when((c > 0) & (c < nc - 1))
        def _():
            dh_s[rows, :] += dh

        @pl.when(c == nc - 1)
        def _():
            dxp, prod = _rms_bwd(x_ref[...], pg, dh_s[rows, :] + dh)
            dpg_ref[...] += _rowsum(prod)
            dx1_ref[...] = dx_ref[...] + dxp

        @pl.when(i == n - 1)
        def _():
            exchange(kc_ref, c, accg, accu, accd, own_gu_ref, land_gu_ref, own_dn_ref, land_dn_ref, *comm)

    dma = pltpu.SemaphoreType.DMA
    return _call(
        body, name=f"ffn_bwd{layer}", grid=(nc, n),
        in_specs=[pl.BlockSpec((tm, D), edge_rows), pl.BlockSpec((tm, D), edge_rows),
                  pl.BlockSpec((tm, D), lambda c, i, kc_ref: (jnp.where(c == 0, i, n - 1), 0),
                               pipeline_mode=pl.Buffered(1)),
                  VSPEC,
                  pl.BlockSpec((2, D, FF_CHUNK), lambda c, i, kc_ref: (0, 0, chunk_at(c, kc_ref))),
                  pl.BlockSpec((FF_CHUNK, D), lambda c, i, kc_ref: (chunk_at(c, kc_ref), 0)),
                  VSPEC],
        out_specs=[pl.BlockSpec((tm, D), lambda c, i, kc_ref: (jnp.where(c == nc - 1, i, 0), 0)),
                   _const_spec((1, D)), _const_spec((1, D)), ANYSPEC, ANYSPEC, ANYSPEC, ANYSPEC],
        out_shape=[_sds((s_len, D)), _sds((1, D)), _sds((1, D)),
                   _sds((n_blk, HALF_D, GU_PIECE), BF), _sds((N_CHIPS - 1, n_blk, HALF_D, GU_PIECE), BF),
                   _sds((DN_SLOT, HALF_D), BF), _sds((N_CHIPS - 1, DN_SLOT, HALF_D), BF)],
        scratch_shapes=[pltpu.VMEM((s_len, D), BF), pltpu.VMEM((s_len, D), BF), pltpu.VMEM((s_len, D), F32),
                        pltpu.VMEM((D, FF_CHUNK), F32), pltpu.VMEM((D, FF_CHUNK), F32),
                        pltpu.VMEM((FF_CHUNK, D), F32),
                        pltpu.VMEM((nc, 2, HALF_D, FF_CHUNK), BF), pltpu.VMEM((nc, FF_CHUNK, HALF_D), BF),
                        pltpu.VMEM((2, 2, HALF_D, FF_CHUNK), BF), pltpu.VMEM((2, FF_CHUNK, HALF_D), BF),
                        pltpu.VMEM((2, HALF_D, FF_CHUNK), F32), pltpu.VMEM((FF_CHUNK, HALF_D), F32),
                        pltpu.VMEM((2, 2, n_gu, HALF_D, GU_PIECE), BF), pltpu.VMEM((2, FF_CHUNK, HALF_D), BF),
                        dma((2, 2)), dma((nc, 2)), dma((2, n_pieces)), dma((2, n_pieces)), dma((2, N_CHIPS - 1))],
        args=[dx2, x1, f, pre_g, wgu, wd, post_g], rider=rider, prefetch=kc)


def _ple_fwd(layer, x2, p, ple_g, w_gate, w_proj, post_g, target=None, rider=None):
    s_len = x2.shape[0]
    final = target is not None

    def body(*refs):
        if final:
            x_ref, p_ref, g_ref, wg_ref, wp_ref, qg_ref, t_ref, z_ref, pe_ref, dx_ref, lv_ref = refs
        else:
            x_ref, p_ref, g_ref, wg_ref, wp_ref, qg_ref, z_ref, pe_ref, x3_ref = refs
        x = x_ref[...]
        r = _rms_fwd(x, g_ref[layer:layer + 1, :]).astype(BF)
        z = _dot(r, wg_ref[...])
        pe = _dot(p_ref[...].astype(BF), wp_ref[...])
        z_ref[...] = z
        pe_ref[...] = pe
        x3 = x + _rms_fwd(pe * _sigmoid(z), qg_ref[layer:layer + 1, :])
        if final:
            @pl.when(pl.program_id(0) == 0)
            def _():
                lv_ref[...] = jnp.zeros_like(lv_ref)
            err = x3 - t_ref[...]
            dx_ref[...] = err * (1.0 / D)
            lv_ref[...] += _rowsum(err * err)
        else:
            x3_ref[...] = x3

    p_spec = pl.BlockSpec((None, TM, PLE), lambda i: (layer, i, 0))
    in_specs = [_row_spec(TM), p_spec, VSPEC, VSPEC, VSPEC, VSPEC]
    args = [x2, p, ple_g, w_gate, w_proj, post_g]
    out_specs = [_row_spec(TM), _row_spec(TM), _row_spec(TM)]
    out_shape = [_sds((s_len, D))] * 3
    if final:
        in_specs.append(_row_spec(TM))
        args.append(target)
        out_specs.append(_const_spec((1, D)))
        out_shape.append(_sds((1, D)))
    return _call(body, name=f"ple_fwd{layer}", grid=(s_len // TM,), in_specs=in_specs, out_specs=out_specs,
                 out_shape=out_shape, args=args, rider=rider)


def _ple_bwd(layer, dx3, x2, z, pe, p, ple_g, w_gate, post_g, rider=None):
    s_len = x2.shape[0]
    n = s_len // TM

    def body(dx_ref, x_ref, z_ref, pe_ref, p_ref, g_ref, wg_ref, qg_ref,
             dx2_ref, dwg_ref, dwp_ref, dg_ref, dqg_ref, gacc, pacc):
        i = pl.program_id(0)

        @pl.when(i == 0)
        def _():
            gacc[...] = jnp.zeros_like(gacc)
            pacc[...] = jnp.zeros_like(pacc)
            dg_ref[...] = jnp.zeros_like(dg_ref)
            dqg_ref[...] = jnp.zeros_like(dqg_ref)

        dx = dx_ref[...]
        x = x_ref[...]
        pe_v = pe_ref[...]
        gate = _sigmoid(z_ref[...])
        de, prod = _rms_bwd(pe_v * gate, qg_ref[layer:layer + 1, :], dx)
        dqg_ref[...] += _rowsum(prod)
        dpe = (de * gate).astype(BF)
        dz = (de * pe_v * gate * (1.0 - gate)).astype(BF)
        pacc[...] += _dot_tn(p_ref[...].astype(BF), dpe)
        g = g_ref[layer:layer + 1, :]
        r = _rms_fwd(x, g).astype(BF)
        gacc[...] += _dot_tn(r, dz)
        dr = _dot_nt(dz, wg_ref[...])
        dxp, prod2 = _rms_bwd(x, g, dr)
        dg_ref[...] += _rowsum(prod2)
        dx2_ref[...] = dx + dxp

        @pl.when(i == n - 1)
        def _():
            dwg_ref[...] = gacc[...].astype(BF)
            dwp_ref[...] = pacc[...].astype(BF)

    p_spec = pl.BlockSpec((None, TM, PLE), lambda i: (layer, i, 0))
    return _call(
        body, name=f"ple_bwd{layer}", grid=(n,),
        in_specs=[_row_spec(TM), _row_spec(TM), _row_spec(TM), _row_spec(TM), p_spec, VSPEC, VSPEC, VSPEC],
        out_specs=[_row_spec(TM), _const_spec((D, D)), _const_spec((PLE, D)), _const_spec((1, D)), _const_spec((1, D))],
        out_shape=[_sds((s_len, D)), _sds((D, D), BF), _sds((PLE, D), BF), _sds((1, D)), _sds((1, D))],
        scratch_shapes=[pltpu.VMEM((D, D), F32), pltpu.VMEM((PLE, D), F32)],
        args=[dx3, x2, z, pe, p, ple_g, w_gate, post_g], rider=rider)


def _qkv_fwd(x3, q_g, kv_g, w_q, w_kv, rider=None):
    s_len = x3.shape[0]

    def body(x_ref, qg_ref, kg_ref, wq_ref, wkv_ref, q_ref, kv_ref):
        x = x_ref[...]
        q_ref[...] = _dot(_rms_fwd(x, qg_ref[1:2, :]).astype(BF), wq_ref[...]).astype(BF)
        kv_ref[...] = _dot(_rms_fwd(x, kg_ref[...]).astype(BF), wkv_ref[...]).astype(BF)

    return _call(body, name="qkv_fwd", grid=(s_len // TM,),
                 in_specs=[_row_spec(TM), VSPEC, VSPEC, VSPEC, VSPEC],
                 out_specs=[_row_spec(TM), _row_spec(TM, 2 * KVD)],
                 out_shape=[_sds((s_len, D), BF), _sds((s_len, 2 * KVD), BF)],
                 args=[x3, q_g, kv_g, w_q, w_kv], rider=rider)


def _qkv_bwd(dq, dkv, x3, dx4, q_g, kv_g, w_q, w_kv):
    s_len = x3.shape[0]
    n = s_len // TM

    def body(dq_ref, dkv_ref, x_ref, dx_ref, qg_ref, kg_ref, wq_ref, wkv_ref,
             dx3_ref, dwq_ref, dwkv_ref, dqg_ref, dkg_ref, qacc, kacc):
        i = pl.program_id(0)

        @pl.when(i == 0)
        def _():
            qacc[...] = jnp.zeros_like(qacc)
            kacc[...] = jnp.zeros_like(kacc)
            dqg_ref[...] = jnp.zeros_like(dqg_ref)
            dkg_ref[...] = jnp.zeros_like(dkg_ref)

        x = x_ref[...]
        qg = qg_ref[1:2, :]
        kg = kg_ref[...]
        dq_v = dq_ref[...]
        dkv_v = dkv_ref[...].astype(BF)
        qacc[...] += _dot_tn(_rms_fwd(x, qg).astype(BF), dq_v)
        kacc[...] += _dot_tn(_rms_fwd(x, kg).astype(BF), dkv_v)
        dxq, prod_q = _rms_bwd(x, qg, _dot_nt(dq_v, wq_ref[...]))
        dxk, prod_k = _rms_bwd(x, kg, _dot_nt(dkv_v, wkv_ref[...]))
        dqg_ref[...] += _rowsum(prod_q)
        dkg_ref[...] += _rowsum(prod_k)
        dx3_ref[...] = dx_ref[...] + dxq + dxk

        @pl.when(i == n - 1)
        def _():
            dwq_ref[...] = qacc[...].astype(BF)
            dwkv_ref[...] = kacc[...].astype(BF)

    outs, _ = _call(
        body, name="qkv_bwd", grid=(n,),
        in_specs=[_row_spec(TM), _row_spec(TM, 2 * KVD), _row_spec(TM), _row_spec(TM), VSPEC, VSPEC, VSPEC, VSPEC],
        out_specs=[_row_spec(TM), _const_spec((D, D)), _const_spec((D, 2 * KVD)),
                   _const_spec((1, D)), _const_spec((1, D))],
        out_shape=[_sds((s_len, D)), _sds((D, D), BF), _sds((D, 2 * KVD), BF), _sds((1, D)), _sds((1, D))],
        scratch_shapes=[pltpu.VMEM((D, D), F32), pltpu.VMEM((D, 2 * KVD), F32)],
        args=[dq, dkv, x3, dx4, q_g, kv_g, w_q, w_kv])
    return outs


def _attn_group(i, q, kvw, sink_ref, g):
    rows = GQA * BLK
    heads = [GQA * g + j for j in range(GQA)]
    off = jnp.where(i > 0, BLK, 0)
    row = lax.broadcasted_iota(jnp.int32, (rows, 2 * BLK), 0)
    rel = (row % BLK) - lax.broadcasted_iota(jnp.int32, (rows, 2 * BLK), 1) + off
    valid = (rel >= 0) & (rel < BLK)
    head_of_row = lax.broadcasted_iota(jnp.int32, (rows, 1), 0) // BLK
    slope = jnp.zeros((rows, 1), F32)
    sink = jnp.zeros((rows, 1), F32)
    for j, h in enumerate(heads):
        slope = jnp.where(head_of_row == j, SLOPES[h], slope)
        sink = jnp.where(head_of_row == j, sink_ref[0, h], sink)
    qs = jnp.concatenate([q[:, h * HEAD_DIM:(h + 1) * HEAD_DIM] for h in heads], axis=0)
    k = kvw[:, g * HEAD_DIM:(g + 1) * HEAD_DIM]
    v = kvw[:, KVD + g * HEAD_DIM:KVD + (g + 1) * HEAD_DIM]
    s = _dot_nt(qs, k) * ATT_SCALE - slope * rel.astype(F32)
    s = jnp.where(valid, s, NEG_INF)
    m = jnp.maximum(jnp.max(s, axis=-1, keepdims=True), sink)
    e = jnp.exp(s - m)
    es = jnp.exp(sink - m)
    inv = 1.0 / (jnp.sum(e, axis=-1, keepdims=True) + es)
    return e * inv, es * inv, qs, k, v


def _unstack_heads(stacked):
    return [stacked[j * BLK:(j + 1) * BLK, :] for j in range(GQA)]


def _kv_window(kv_ref, i):
    ks = pl.multiple_of(jnp.maximum(i * BLK - BLK, 0), BLK)
    return ks, kv_ref[pl.ds(ks, 2 * BLK), :]


def _attn_fwd(q, kv, sinks, x3, w_o, post_g, rider=None):
    s_len = q.shape[0]

    def body(q_ref, kv_ref, sk_ref, x_ref, wo_ref, g_ref, a_ref, y_ref, x4_ref):
        i = pl.program_id(0)
        _, kvw = _kv_window(kv_ref, i)
        q = q_ref[...]
        outs = []
        for g in range(N_KV_HEADS):
            p, _, _, _, v = _attn_group(i, q, kvw, sk_ref, g)
            outs += _unstack_heads(_dot(p.astype(BF), v))
        attn = jnp.concatenate(outs, axis=1)
        a_ref[...] = attn
        y = _dot(attn.astype(BF), wo_ref[...])
        y_ref[...] = y
        x4_ref[...] = x_ref[...] + _rms_fwd(y, g_ref[1:2, :])

    return _call(body, name="attn_fwd", grid=(s_len // BLK,),
                 in_specs=[_row_spec(BLK), VSPEC, SSPEC, _row_spec(BLK), VSPEC, VSPEC],
                 out_specs=[_row_spec(BLK)] * 3, out_shape=[_sds((s_len, D))] * 3,
                 args=[q, kv, sinks, x3, w_o, post_g], rider=rider)


def _attn_bwd(dx4, y, attn, q, kv, sinks, w_o, post_g, rider=None):
    s_len = q.shape[0]
    n = s_len // BLK

    def body(dx_ref, y_ref, a_ref, q_ref, kv_ref, sk_ref, wo_ref, g_ref,
             dq_ref, dkv_ref, dwo_ref, dg_ref, dsk_ref, wacc):
        i = pl.program_id(0)

        @pl.when(i == 0)
        def _():
            dkv_ref[...] = jnp.zeros_like(dkv_ref)
            wacc[...] = jnp.zeros_like(wacc)
            dg_ref[...] = jnp.zeros_like(dg_ref)
            dsk_ref[...] = jnp.zeros_like(dsk_ref)

        dy, prod = _rms_bwd(y_ref[...], g_ref[1:2, :], dx_ref[...])
        dg_ref[...] += _rowsum(prod)
        dyb = dy.astype(BF)
        attn = a_ref[...]
        wacc[...] += _dot_tn(attn.astype(BF), dyb)
        d_o = _dot_nt(dyb, wo_ref[...])
        dod = d_o * attn
        ks, kvw = _kv_window(kv_ref, i)
        q = q_ref[...]
        lane = lax.broadcasted_iota(jnp.int32, (1, D), 1)
        dqs, dks, dvs = [], [], []
        dsk = jnp.zeros((1, D), F32)
        for g in range(N_KV_HEADS):
            p, ps, qs, k, v = _attn_group(i, q, kvw, sk_ref, g)
            cols = [slice((GQA * g + j) * HEAD_DIM, (GQA * g + j + 1) * HEAD_DIM) for j in range(GQA)]
            do_s = jnp.concatenate([d_o[:, c] for c in cols], axis=0).astype(BF)
            dsum = jnp.concatenate([jnp.sum(dod[:, c], axis=-1, keepdims=True) for c in cols], axis=0)
            dp = _dot_nt(do_s, v)
            dsb = (p * (dp - dsum) * ATT_SCALE).astype(BF)
            sink_part = ps * dsum
            for j in range(GQA):
                dsk = dsk + jnp.where(lane == GQA * g + j, -_rowsum(sink_part[j * BLK:(j + 1) * BLK, :]), 0.0)
            dqs += _unstack_heads(_dot(dsb, k))
            dks.append(_dot_tn(dsb, qs))
            dvs.append(_dot_tn(p.astype(BF), do_s))
        dsk_ref[...] += dsk
        dq_ref[...] = jnp.concatenate(dqs, axis=1).astype(BF)
        dkv_ref[pl.ds(ks, 2 * BLK), :] += jnp.concatenate(dks + dvs, axis=1)

        @pl.when(i == n - 1)
        def _():
            dwo_ref[...] = wacc[...].astype(BF)

    return _call(
        body, name="attn_bwd", grid=(n,),
        in_specs=[_row_spec(BLK), _row_spec(BLK), _row_spec(BLK), _row_spec(BLK), VSPEC, SSPEC, VSPEC, VSPEC],
        out_specs=[_row_spec(BLK), _const_spec((s_len, 2 * KVD)), _const_spec((D, D)),
                   _const_spec((1, D)), _const_spec((1, D))],
        out_shape=[_sds((s_len, D), BF), _sds((s_len, 2 * KVD)), _sds((D, D), BF), _sds((1, D)), _sds((1, D))],
        scratch_shapes=[pltpu.VMEM((D, D), F32)],
        args=[dx4, y, attn, q, kv, sinks, w_o, post_g], rider=rider)


Big = collections.namedtuple("Big", "name src layer L A R C rb")


def _bigs():
    out = {"pool_w": Big("pool_w", "pool_w", None, 4, 4, POOL_G // N_CHIPS, POOL_G, 32)}
    for l in range(2):
        out[f"w_gu{l}"] = Big(f"w_gu{l}", "w_gu", l, 1, 2, D, FF_HALF, 256)
        out[f"w_down{l}"] = Big(f"w_down{l}", "w_down", l, 1, 4, FF // N_CHIPS, D, 352)
        out[f"w_ple_gate{l}"] = Big(f"w_ple_gate{l}", "w_ple_gate", l, 1, 4, D // N_CHIPS, D, 128)
        out[f"w_ple_proj{l}"] = Big(f"w_ple_proj{l}", "w_ple_proj", l, 1, 1, PLE, D // N_CHIPS, 128)
    out["w_q"] = Big("w_q", "w_q", None, 1, 4, D // N_CHIPS, D, 128)
    out["w_o"] = Big("w_o", "w_o", None, 1, 4, D // N_CHIPS, D, 128)
    out["w_kv"] = Big("w_kv", "w_kv", None, 1, 4, D // N_CHIPS, 2 * KVD, 128)
    return out


BIGS = _bigs()
POOL_SCALE = Big("pool_scale", "pool_scale", None, 1, 1, 1, D // N_CHIPS, 1)
BIG_SOURCES = ("w_gu", "w_down", "w_ple_gate", "w_ple_proj", "w_q", "w_o", "w_kv", "pool_w")


def _ncb(t):
    return N_CHIPS // t.A


def _full_shape(t, rows=None):
    return (t.L, t.A, t.R if rows is None else rows, _ncb(t) * t.C)


def _slot_index(t, k):
    return k // _ncb(t), k % _ncb(t)


def _slot(ref, t, k, row0, rows):
    a, cb = _slot_index(t, k)
    return ref.at[:, a, pl.ds(row0, rows), pl.ds(pl.multiple_of(cb * t.C, 128), t.C)]


def _place(t, w, kc, out_dtype):
    rb = min(t.R, 2 * t.rb)

    def body(kc_ref, w_ref, o_ref):
        del kc_ref
        o_ref[...] = w_ref[...].astype(out_dtype)

    def in_map(l, j, kc_ref):
        return (l if t.layer is None else t.layer, j, 0)

    def out_map(l, j, kc_ref):
        a, cb = _slot_index(t, kc_ref[0])
        return (l, a, j, cb)

    return pl.pallas_call(
        body, name=f"place_{t.name}",
        grid_spec=pltpu.PrefetchScalarGridSpec(
            num_scalar_prefetch=1, grid=(t.L, t.R // rb),
            in_specs=[pl.BlockSpec((None, rb, t.C), in_map)],
            out_specs=pl.BlockSpec((None, None, rb, t.C), out_map)),
        out_shape=_sds(_full_shape(t), out_dtype),
        compiler_params=_params(2),
    )(kc, w)


def _mesh_position():
    x, y, c = lax.axis_index("x"), lax.axis_index("y"), lax.axis_index("c")
    chips = [(1 - x, y), (x, 1 - y), (1 - x, 1 - y)]
    return x, y, c, chips


def _gather_rider(parts, fulls):
    nt = len(parts)
    TO_X, TO_Y, FWD_X, FWD_Y, SIB_X, SIB_Y, SIB_D = range(7)

    def rows_of(ti, core):
        t, r0, r1 = parts[ti]
        h = (r1 - r0) // 2
        return r0 + core * h, h

    def copy(outs, sems, kind, ti, k_src, row0, rows, dev):
        region = _slot(outs[ti], parts[ti][0], k_src, row0, rows)
        return pltpu.make_async_remote_copy(region, region, sems[0].at[ti, kind], sems[1].at[ti, kind],
                                            device_id=dev, device_id_type=MESH)

    def plan(outs, sems):
        x, y, c, _ = _mesh_position()
        me, kx, ky, kd = 2 * x + y, 2 * (1 - x) + y, 2 * x + (1 - y), 2 * (1 - x) + (1 - y)
        dev_x, dev_y, dev_d, sib = (1 - x, y, c), (x, 1 - y, c), (1 - x, 1 - y, c), (x, y, 1 - c)

        def whole(ti):
            return 0, parts[ti][0].R

        def mk(kind, k_send, k_recv, dev, send_rows, recv_rows):
            def build(ti, side):
                k_src = k_send if side == "s" else k_recv
                row0, rows = (send_rows if side == "s" else recv_rows)(ti)
                return copy(outs, sems, kind, ti, k_src, row0, rows, dev)
            return build

        def first_half(core):
            return lambda ti: (rows_of(ti, core)[0], rows_of(ti, core)[1] // 2)

        def second_half(core):
            return lambda ti: (rows_of(ti, core)[0] + rows_of(ti, core)[1] // 2, rows_of(ti, core)[1] // 2)

        mine = lambda ti: rows_of(ti, c)
        theirs = lambda ti: rows_of(ti, 1 - c)
        split = {
            TO_X: mk(TO_X, me, kx, dev_x, mine, mine),
            TO_Y: mk(TO_Y, me, ky, dev_y, mine, mine),
            FWD_X: mk(FWD_X, ky, kd, dev_x, first_half(c), first_half(c)),
            FWD_Y: mk(FWD_Y, kx, kd, dev_y, second_half(c), second_half(c)),
            SIB_X: mk(SIB_X, kx, kx, sib, mine, theirs),
            SIB_Y: mk(SIB_Y, ky, ky, sib, mine, theirs),
            SIB_D: mk(SIB_D, kd, kd, sib, mine, theirs),
        }
        direct = {
            TO_X: mk(TO_X, me, kx, dev_x, whole, whole),
            TO_Y: mk(TO_Y, me, ky, dev_y, whole, whole),
            FWD_X: mk(FWD_X, me, kd, dev_d, whole, whole),
        }
        return split, direct

    is_split = [t.R > 1 for t, _, _ in parts]

    def start(ins, outs, sems):
        split, direct = plan(outs, sems)
        for ti in range(nt):
            kinds = split if is_split[ti] else direct
            kinds[TO_X](ti, "s").start()
            kinds[TO_Y](ti, "s").start()
            if not is_split[ti]:
                kinds[FWD_X](ti, "s").start()

    def mid(ins, outs, sems):
        split, _ = plan(outs, sems)
        for ti in range(nt):
            if is_split[ti]:
                split[TO_Y](ti, "r").wait_recv()
                split[FWD_X](ti, "s").start()
                split[SIB_Y](ti, "s").start()
        for ti in range(nt):
            if is_split[ti]:
                split[TO_X](ti, "r").wait_recv()
                split[FWD_Y](ti, "s").start()
                split[SIB_X](ti, "s").start()

    def finish(ins, outs, sems):
        split, direct = plan(outs, sems)
        for ti in range(nt):
            if is_split[ti]:
                split[FWD_X](ti, "r").wait_recv()
                split[FWD_Y](ti, "r").wait_recv()
                split[SIB_D](ti, "s").start()
            else:
                for kind in (TO_X, TO_Y, FWD_X):
                    direct[kind](ti, "r").wait_recv()
        for ti in range(nt):
            if is_split[ti]:
                for kind in (SIB_X, SIB_Y, SIB_D):
                    split[kind](ti, "r").wait_recv()
        for ti in range(nt):
            kinds = split if is_split[ti] else direct
            for kind in kinds:
                kinds[kind](ti, "s").wait_send()

    sems = pltpu.SemaphoreType.DMA((nt, 7))
    return Rider(list(fulls), [_sds(a.shape, a.dtype) for a in fulls], {i: i for i in range(nt)},
                 [sems, sems], start, mid, finish)


def _pair_exchange(name, specs, grads):
    nt = len(specs)

    def body(*refs):
        gs = refs[:nt]
        lands = refs[nt:2 * nt]
        send, recv = refs[2 * nt:]
        x, y, c, _ = _mesh_position()
        cps = []
        for ti, t in enumerate(specs):
            half = t.R // 2
            cp = pltpu.make_async_remote_copy(gs[ti].at[:, :, pl.ds((1 - c) * half, half), :], lands[ti],
                                              send.at[ti], recv.at[ti],
                                              device_id=(x, y, 1 - c), device_id_type=MESH)
            cp.start()
            cps.append(cp)
        for cp in cps:
            cp.wait()

    return pl.pallas_call(
        body, name=name,
        in_specs=[ANYSPEC] * nt, out_specs=[ANYSPEC] * nt,
        out_shape=[_sds(_full_shape(t, t.R // 2), BF) for t in specs],
        scratch_shapes=[pltpu.SemaphoreType.DMA((nt,)), pltpu.SemaphoreType.DMA((nt,))],
        compiler_params=_params(),
    )(*grads)


def _pair_sum(t, g, land, kc):
    half = t.R // 2
    nj = half // t.rb
    w = _ncb(t) * t.C

    def body(kc_ref, g_ref, l_ref, o_ref):
        del kc_ref
        o_ref[...] = (g_ref[...].astype(F32) + l_ref[...].astype(F32)).astype(BF)

    return pl.pallas_call(
        body, name=f"pair_sum_{t.name}",
        grid_spec=pltpu.PrefetchScalarGridSpec(
            num_scalar_prefetch=1, grid=(t.L, nj),
            in_specs=[pl.BlockSpec((None, t.A, t.rb, w), lambda l, j, kc_ref: (l, 0, kc_ref[1] * nj + j, 0)),
                      pl.BlockSpec((None, t.A, t.rb, w), lambda l, j, kc_ref: (l, 0, j, 0))],
            out_specs=pl.BlockSpec((None, t.A, t.rb, w), lambda l, j, kc_ref: (l, 0, j, 0))),
        out_shape=_sds(_full_shape(t, half), BF),
        compiler_params=_params(2),
    )(kc, g, land)


def _scatter_rider(specs, sums):
    nt = len(specs)

    def copy(ins, outs, sems, ti, j, chip, c):
        t = specs[ti]
        cx, cy = chip
        return pltpu.make_async_remote_copy(_slot(ins[ti], t, 2 * cx + cy, 0, t.R // 2), outs[ti].at[j],
                                            sems[0].at[ti, j], sems[1].at[ti, j],
                                            device_id=(cx, cy, c), device_id_type=MESH)

    def start(ins, outs, sems):
        _, _, c, chips = _mesh_position()
        for j, chip in enumerate(chips):
            for ti in range(nt):
                copy(ins, outs, sems, ti, j, chip, c).start()

    def finish(ins, outs, sems):
        _, _, c, chips = _mesh_position()
        for j, chip in enumerate(chips):
            for ti in range(nt):
                copy(ins, outs, sems, ti, j, chip, c).wait()

    sems = pltpu.SemaphoreType.DMA((nt, N_CHIPS - 1))
    return Rider(list(sums), [_sds((N_CHIPS - 1, t.L, t.R // 2, t.C), BF) for t in specs], {}, [sems, sems],
                 start, None, finish)


def _chip_sum(t, s, land, kc, n_layers, prev):
    half = t.R // 2
    nj = half // t.rb

    def body(*refs):
        s_ref, l_ref, o_ref = refs[1], refs[2], refs[-1]
        acc = s_ref[...].astype(F32)
        for j in range(N_CHIPS - 1):
            acc = acc + l_ref[j].astype(F32)
        o_ref[...] = acc

    def own_map(l, j, kc_ref):
        a, cb = _slot_index(t, kc_ref[0])
        return (l, a, j, cb)

    def out_map(l, j, kc_ref):
        return (l if t.layer is None else t.layer, kc_ref[1] * nj + j, 0)

    in_specs = [pl.BlockSpec((None, None, t.rb, t.C), own_map),
                pl.BlockSpec((N_CHIPS - 1, None, t.rb, t.C), lambda l, j, kc_ref: (0, l, j, 0))]
    args = [kc, s, land]
    aliases = {}
    if prev is not None:
        in_specs.append(ANYSPEC)
        args.append(prev)
        aliases = {3: 0}
    return pl.pallas_call(
        body, name=f"chip_sum_{t.name}",
        grid_spec=pltpu.PrefetchScalarGridSpec(
            num_scalar_prefetch=1, grid=(t.L, nj), in_specs=in_specs,
            out_specs=pl.BlockSpec((None, t.rb, t.C), out_map)),
        out_shape=_sds((n_layers, t.R, t.C)),
        input_output_aliases=aliases,
        compiler_params=_params(2),
    )(*args)


def _chip_sum_fused(t, own, land, kc, n_layers, prev, by_cols):
    if by_cols:
        rows, cols = own.shape
    else:
        nb, rows, bw = own.shape
        cols = nb * bw
    nj = rows // t.rb

    def body(*refs):
        o_ref, l_ref, out_ref = refs[1], refs[2], refs[-1]
        acc = o_ref[...].astype(F32)
        for j in range(N_CHIPS - 1):
            acc = acc + l_ref[j].astype(F32)
        out_ref[...] = acc if by_cols else jnp.concatenate([acc[b] for b in range(nb)], axis=1)

    def out_map(j, kc_ref):
        return (t.layer, j, kc_ref[1]) if by_cols else (t.layer, kc_ref[1] * nj + j, 0)

    if by_cols:
        in_specs = [pl.BlockSpec((t.rb, cols), lambda j, kc_ref: (j, 0)),
                    pl.BlockSpec((N_CHIPS - 1, t.rb, cols), lambda j, kc_ref: (0, j, 0))]
    else:
        in_specs = [pl.BlockSpec((nb, t.rb, bw), lambda j, kc_ref: (0, j, 0)),
                    pl.BlockSpec((N_CHIPS - 1, nb, t.rb, bw), lambda j, kc_ref: (0, 0, j, 0))]
    args = [kc, own, land]
    aliases = {}
    if prev is not None:
        in_specs.append(ANYSPEC)
        args.append(prev)
        aliases = {3: 0}
    return pl.pallas_call(
        body, name=f"chip_sum_{t.name}",
        grid_spec=pltpu.PrefetchScalarGridSpec(
            num_scalar_prefetch=1, grid=(nj,), in_specs=in_specs,
            out_specs=pl.BlockSpec((None, t.rb, cols), out_map)),
        out_shape=_sds((n_layers, t.R, t.C)),
        input_output_aliases=aliases,
        compiler_params=_params(1),
    )(*args)


def _pair_share(halves, by_cols):
    nt = len(halves)

    def part(ref, ti, core):
        axis = 2 if by_cols[ti] else 1
        half = halves[ti].shape[axis] // 2
        piece = pl.ds(pl.multiple_of(core * half, 128 if by_cols[ti] else 8), half)
        return ref.at[:, :, piece] if by_cols[ti] else ref.at[:, piece, :]

    def body(*refs):
        outs = refs[nt:2 * nt]
        send, recv = refs[2 * nt:]
        x, y, c, _ = _mesh_position()
        cps = []
        for ti in range(nt):
            mine = part(outs[ti], ti, c)
            cp = pltpu.make_async_remote_copy(mine, mine, send.at[ti], recv.at[ti],
                                              device_id=(x, y, 1 - c), device_id_type=MESH)
            cp.start()
            cps.append(cp)
        for ti in range(nt):
            theirs = part(outs[ti], ti, 1 - c)
            pltpu.make_async_remote_copy(theirs, theirs, send.at[ti], recv.at[ti],
                                         device_id=(x, y, 1 - c), device_id_type=MESH).wait_recv()
        for cp in cps:
            cp.wait_send()

    return pl.pallas_call(
        body, name="grads_pair_share",
        in_specs=[ANYSPEC] * nt, out_specs=[ANYSPEC] * nt,
        out_shape=[_sds(a.shape, a.dtype) for a in halves],
        scratch_shapes=[pltpu.SemaphoreType.DMA((nt,)), pltpu.SemaphoreType.DMA((nt,))],
        input_output_aliases={i: i for i in range(nt)},
        compiler_params=_params(),
    )(*halves)


def _adamw_math(w, g, m, v):
    m = B1 * m + (1.0 - B1) * g
    v = B2 * v + (1.0 - B2) * (g * g)
    delta = -LR * ((m / BC1) / (jnp.sqrt(v / BC2) + AEPS) + WD * w)
    return delta, m, v


def _adamw(name, rb, w, g, m, v):
    n_layers, r, c = w.shape

    def body(w_ref, g_ref, m_ref, v_ref, go_ref, d_ref, nm_ref, nv_ref):
        g_v = g_ref[...]
        go_ref[...] = g_v
        d_ref[...], nm_ref[...], nv_ref[...] = _adamw_math(w_ref[...], g_v, m_ref[...], v_ref[...])

    spec = pl.BlockSpec((None, rb, c), lambda l, j: (l, j, 0))
    return pl.pallas_call(
        body, name=f"adamw_{name}", grid=(n_layers, r // rb),
        in_specs=[spec] * 4, out_specs=[spec] * 4, out_shape=[_sds(w.shape)] * 4,
        compiler_params=_params(2),
    )(w, g, m, v)


GAIN_ROWS = {"pre_mix_g": 0, "post_mix_g": 2, "pre_ffn_g": 4, "post_ffn_g": 6, "ple_g": 8, "ple_post_g": 10}
ROW_KV_G, ROW_POOL_SCALE, ROW_SINKS, ROW_LOSS, PACK_ROWS = 12, 13, 14, 15, 16
SMALL_NAMES = tuple(GAIN_ROWS) + ("kv_g", "pool_scale", "sinks")


def _small_all_reduce(rows, dpool):
    ng, pr = len(WINDOWS), POOL_G // N_CHIPS

    def body(*refs):
        row_refs = refs[:PACK_ROWS]
        dpool_ref, tot_ref, gpool_ref, pack, land, pland, send, recv, psend, precv = refs[PACK_ROWS:]
        x, y, c, _ = _mesh_position()
        me = 4 * x + 2 * y + c
        for r in range(PACK_ROWS):
            pack[r:r + 1, :] = row_refs[r][...]

        def shard_of(k):
            return dpool_ref.at[:, pl.ds(pl.multiple_of(k * pr, pr), pr), :]

        cps = []
        for j in range(1, N_DEV):
            px, py, pc = x ^ (j >> 2), y ^ ((j >> 1) & 1), c ^ (j & 1)
            cps.append(pltpu.make_async_remote_copy(pack, land.at[me], send.at[j], recv.at[j],
                                                    device_id=(px, py, pc), device_id_type=MESH))
            cps.append(pltpu.make_async_remote_copy(shard_of(2 * px + py), pland.at[me], psend.at[j], precv.at[j],
                                                    device_id=(px, py, pc), device_id_type=MESH))
        for cp in cps:
            cp.start()
        land[me] = pack[...]
        pland[me] = dpool_ref[:, pl.ds(pl.multiple_of((2 * x + y) * pr, pr), pr), :]
        for j in range(1, N_DEV):
            pltpu.make_async_remote_copy(pack, land.at[me ^ j], send.at[j], recv.at[j],
                                         device_id=(x, y, c), device_id_type=MESH).wait_recv()
            pltpu.make_async_remote_copy(shard_of(0), pland.at[me ^ j], psend.at[j], precv.at[j],
                                         device_id=(x, y, c), device_id_type=MESH).wait_recv()
        for cp in cps:
            cp.wait_send()
        tot = land[0]
        gp = pland[0].astype(F32)
        for d in range(1, N_DEV):
            tot = tot + land[d]
            gp = gp + pland[d].astype(F32)
        tot_ref[...] = tot
        gpool_ref[...] = gp

    sems = pltpu.SemaphoreType.DMA((N_DEV,))
    return pl.pallas_call(
        body, name="small_all_reduce",
        in_specs=[VSPEC] * (PACK_ROWS + 1), out_specs=[VSPEC, VSPEC],
        out_shape=[_sds((PACK_ROWS, D)), _sds((ng, pr, POOL_G))],
        scratch_shapes=[pltpu.VMEM((PACK_ROWS, D), F32), pltpu.VMEM((N_DEV, PACK_ROWS, D), F32),
                        pltpu.VMEM((N_DEV, ng, pr, POOL_G), BF), sems, sems, sems, sems],
        compiler_params=_params(),
    )(*rows, dpool)


def _small_adamw(tot, kc, small_w, small_m, small_v):
    names = SMALL_NAMES
    n = len(names)

    def body(*refs):
        tot_ref, kc_ref = refs[0], refs[1]
        w_refs = dict(zip(names, refs[2:2 + n]))
        m_refs = dict(zip(names, refs[2 + n:2 + 2 * n]))
        v_refs = dict(zip(names, refs[2 + 2 * n:2 + 3 * n]))
        loss_ref = refs[2 + 3 * n]
        out_refs = {nm: refs[3 + 3 * n + 4 * k: 7 + 3 * n + 4 * k] for k, nm in enumerate(names)}
        tot = tot_ref[...]
        loss_ref[...] = 0.5 * jnp.sum(tot[ROW_LOSS:ROW_LOSS + 1, :], axis=-1, keepdims=True) * (1.0 / D)

        def update(nm, g):
            g_ref, d_ref, nm_ref, nv_ref = out_refs[nm]
            g_ref[...] = g
            d_ref[...], nm_ref[...], nv_ref[...] = _adamw_math(w_refs[nm][...], g, m_refs[nm][...], v_refs[nm][...])

        for nm, r in GAIN_ROWS.items():
            update(nm, tot[r:r + 2, :])
        update("kv_g", tot[ROW_KV_G:ROW_KV_G + 1, :])
        k = kc_ref[0]
        width = D // N_CHIPS
        g_scale = jnp.zeros((1, width), F32)
        for kk in range(N_CHIPS):
            g_scale = g_scale + jnp.where(k == kk, tot[ROW_POOL_SCALE:ROW_POOL_SCALE + 1, kk * width:(kk + 1) * width], 0.0)
        update("pool_scale", g_scale)
        update("sinks", tot[ROW_SINKS:ROW_SINKS + 1, 0:N_HEADS])

    ins = [tot, kc] + [small_w[nm] for nm in names] + [small_m[nm] for nm in names] + [small_v[nm] for nm in names]
    out_shape = [_sds((1, 1))]
    for nm in names:
        out_shape += [_sds(small_w[nm].shape)] * 4
    outs = pl.pallas_call(
        body, name="small_adamw",
        in_specs=[VSPEC, SSPEC] + [VSPEC] * (3 * n), out_specs=[VSPEC] * len(out_shape), out_shape=out_shape,
        compiler_params=_params(),
    )(*ins)
    return outs[0], {nm: outs[1 + 4 * k: 5 + 4 * k] for k, nm in enumerate(names)}


def _compute_layout(t, full):
    if t.src == "w_gu":
        return full.reshape(2, D, FF)
    if t.src == "pool_w":
        return full.reshape(len(WINDOWS), POOL_G, POOL_G)
    if t.src == "pool_scale":
        return full.reshape(1, D)
    return full.reshape(t.A * t.R, _ncb(t) * t.C)


def kernel(x, p, pre_mix_g, post_mix_g, pre_ffn_g, post_ffn_g, pool_w, pool_scale, kv_g, w_kv, w_q, sinks, w_o, w_gu, w_down, ple_g, w_ple_gate, w_ple_proj, ple_post_g, loss_target, m_pre_mix_g, m_post_mix_g, m_pre_ffn_g, m_post_ffn_g, m_pool_w, m_pool_scale, m_kv_g, m_w_kv, m_w_q, m_sinks, m_w_o, m_w_gu, m_w_down, m_ple_g, m_w_ple_gate, m_w_ple_proj, m_ple_post_g, v_pre_mix_g, v_post_mix_g, v_pre_ffn_g, v_post_ffn_g, v_pool_w, v_pool_scale, v_kv_g, v_w_kv, v_w_q, v_sinks, v_w_o, v_w_gu, v_w_down, v_ple_g, v_w_ple_gate, v_w_ple_proj, v_ple_post_g):
    weights = dict(pre_mix_g=pre_mix_g, post_mix_g=post_mix_g, pre_ffn_g=pre_ffn_g, post_ffn_g=post_ffn_g,
                   pool_w=pool_w, pool_scale=pool_scale, kv_g=kv_g, w_kv=w_kv, w_q=w_q, sinks=sinks, w_o=w_o,
                   w_gu=w_gu, w_down=w_down, ple_g=ple_g, w_ple_gate=w_ple_gate, w_ple_proj=w_ple_proj,
                   ple_post_g=ple_post_g)
    m_in = dict(pre_mix_g=m_pre_mix_g, post_mix_g=m_post_mix_g, pre_ffn_g=m_pre_ffn_g, post_ffn_g=m_post_ffn_g,
                pool_w=m_pool_w, pool_scale=m_pool_scale, kv_g=m_kv_g, w_kv=m_w_kv, w_q=m_w_q, sinks=m_sinks,
                w_o=m_w_o, w_gu=m_w_gu, w_down=m_w_down, ple_g=m_ple_g, w_ple_gate=m_w_ple_gate,
                w_ple_proj=m_w_ple_proj, ple_post_g=m_ple_post_g)
    v_in = dict(pre_mix_g=v_pre_mix_g, post_mix_g=v_post_mix_g, pre_ffn_g=v_pre_ffn_g, post_ffn_g=v_post_ffn_g,
                pool_w=v_pool_w, pool_scale=v_pool_scale, kv_g=v_kv_g, w_kv=v_w_kv, w_q=v_w_q, sinks=v_sinks,
                w_o=v_w_o, w_gu=v_w_gu, w_down=v_w_down, ple_g=v_ple_g, w_ple_gate=v_w_ple_gate,
                w_ple_proj=v_w_ple_proj, ple_post_g=v_ple_post_g)
    order = ["pre_mix_g", "post_mix_g", "pre_ffn_g", "post_ffn_g", "pool_w", "pool_scale", "kv_g", "w_kv", "w_q",
             "sinks", "w_o", "w_gu", "w_down", "ple_g", "w_ple_gate", "w_ple_proj", "ple_post_g"]

    kc = jnp.stack([2 * lax.axis_index("x") + lax.axis_index("y"), lax.axis_index("c")]).astype(jnp.int32)
    s_len = x.shape[1]
    x2d = x.reshape(s_len, D)
    p3d = p.reshape(2, s_len, PLE)
    target = loss_target.reshape(s_len, D)
    kv_g2d = kv_g.reshape(1, D)
    gains = {nm: weights[nm] for nm in GAIN_ROWS}

    def shard_view(src, a):
        t = next(t for t in BIGS.values() if t.src == src)
        return a.reshape(-1, t.R, t.C)

    placed = {nm: _place(t, shard_view(t.src, weights[t.src]), kc, BF) for nm, t in BIGS.items()}
    placed["pool_scale"] = _place(POOL_SCALE, pool_scale.reshape(1, 1, D // N_CHIPS), kc, F32)
    specs = dict(BIGS, pool_scale=POOL_SCALE)

    def gather(names, rows=None):
        rows = rows or {}
        parts = [(specs[nm],) + tuple(rows.get(nm, (0, specs[nm].R))) for nm in names]
        return _gather_rider(parts, [placed[nm] for nm in names])

    def take(names, results):
        for nm, a in zip(names, results):
            placed[nm] = a

    def weight(nm):
        return _compute_layout(specs[nm], placed[nm])

    first = ["pool_w", "pool_scale", "w_gu0", "w_down0"]
    take(first, _run("weights_gather_first", gather(first)))

    y0, x1 = _mixa_fwd(x2d, gains["pre_mix_g"], weight("pool_w"), weight("pool_scale"), gains["post_mix_g"])

    ride = ["w_ple_gate0", "w_ple_proj0", "w_q", "w_kv", "w_o", "w_gu1"]
    (f0, x2), got = _ffn_fwd(0, x1, gains["pre_ffn_g"], weight("w_gu0"), weight("w_down0"), gains["post_ffn_g"],
                             rider=gather(ride, {"w_gu1": (0, 320)}))
    take(ride, got)

    ride = ["w_ple_gate1", "w_ple_proj1", "w_gu1"]
    (z0, pe0, x3), got = _ple_fwd(0, x2, p3d, gains["ple_g"], weight("w_ple_gate0"), weight("w_ple_proj0"),
                                  gains["ple_post_g"], rider=gather(ride, {"w_gu1": (320, 448)}))
    take(ride, got)

    ride = ["w_gu1"]
    (q, kv), got = _qkv_fwd(x3, gains["pre_mix_g"], kv_g2d, weight("w_q"), weight("w_kv"),
                            rider=gather(ride, {"w_gu1": (448, 704)}))
    take(ride, got)

    ride = ["w_down1", "w_gu1"]
    (attn, y1, x4), got = _attn_fwd(q, kv, sinks, x3, weight("w_o"), gains["post_mix_g"],
                                    rider=gather(ride, {"w_gu1": (704, D)}))
    take(ride, got)

    (f1, x5), _ = _ffn_fwd(1, x4, gains["pre_ffn_g"], weight("w_gu1"), weight("w_down1"), gains["post_ffn_g"])
    (z1, pe1, dx6, loss_row), _ = _ple_fwd(1, x5, p3d, gains["ple_g"], weight("w_ple_gate1"), weight("w_ple_proj1"),
                                           gains["ple_post_g"], target=target)

    local = {}
    landed = {}
    fused = {}

    def pair_stage(tag, names):
        ts = [BIGS[nm] for nm in names]
        gs = [local[nm].reshape(_full_shape(t)) for nm, t in zip(names, ts)]
        lands = _pair_exchange(f"grads_pair_exchange_{tag}", ts, gs)
        return [_pair_sum(t, g, l, kc) for t, g, l in zip(ts, gs, lands)]

    def scatter(names, sums):
        return _scatter_rider([BIGS[nm] for nm in names], sums)

    def keep(names, sums, got):
        for nm, s, l in zip(names, sums, got):
            landed[nm] = (s, l)

    (dx5, local["w_ple_gate1"], local["w_ple_proj1"], d_ple1, d_plepost1), _ = _ple_bwd(
        1, dx6, x5, z1, pe1, p3d, gains["ple_g"], weight("w_ple_gate1"), gains["ple_post_g"])

    group_a = ["w_ple_gate1", "w_ple_proj1"]
    sums_a = pair_stage("a", group_a)
    (dx4, d_preffn1, d_postffn1, *scattered), got = _ffn_bwd(
        1, dx5, x4, f1, gains["pre_ffn_g"], weight("w_gu1"), weight("w_down1"), gains["post_ffn_g"], kc,
        rider=scatter(group_a, sums_a))
    fused["w_gu1"], fused["w_down1"] = scattered[0:2], scattered[2:4]
    keep(group_a, sums_a, got)

    (dq, dkv, local["w_o"], d_postmix1, d_sinks), _ = _attn_bwd(
        dx4, y1, attn, q, kv, sinks, weight("w_o"), gains["post_mix_g"])
    dx3, local["w_q"], local["w_kv"], d_premix1, d_kvg = _qkv_bwd(
        dq, dkv, x3, dx4, gains["pre_mix_g"], kv_g2d, weight("w_q"), weight("w_kv"))

    group_b = ["w_o", "w_q", "w_kv"]
    sums_b = pair_stage("b", group_b)
    (dx2, local["w_ple_gate0"], local["w_ple_proj0"], d_ple0, d_plepost0), got = _ple_bwd(
        0, dx3, x2, z0, pe0, p3d, gains["ple_g"], weight("w_ple_gate0"), gains["ple_post_g"],
        rider=scatter(group_b, sums_b))
    keep(group_b, sums_b, got)

    group_c = ["w_ple_gate0", "w_ple_proj0"]
    sums_c = pair_stage("c", group_c)
    (dx1, d_preffn0, d_postffn0, *scattered), got = _ffn_bwd(
        0, dx2, x1, f0, gains["pre_ffn_g"], weight("w_gu0"), weight("w_down0"), gains["post_ffn_g"], kc,
        rider=scatter(group_c, sums_c))
    fused["w_gu0"], fused["w_down0"] = scattered[0:2], scattered[2:4]
    keep(group_c, sums_c, got)

    dx0, d_pool, d_scale, d_postmix0, d_premix0 = _mixa_bwd(
        dx1, x2d, y0, gains["pre_mix_g"], weight("pool_w"), weight("pool_scale"), gains["post_mix_g"])

    rows = [d_premix0, d_premix1, d_postmix0, d_postmix1, d_preffn0, d_preffn1, d_postffn0, d_postffn1,
            d_ple0, d_ple1, d_plepost0, d_plepost1, d_kvg, d_scale, d_sinks, loss_row]
    as2d = lambda a: a.reshape(1, D) if a.ndim == 1 else a
    tot, g_pool = _small_all_reduce(rows, d_pool)
    loss, small = _small_adamw(tot, kc, {nm: as2d(weights[nm]) for nm in SMALL_NAMES},
                               {nm: as2d(m_in[nm]) for nm in SMALL_NAMES},
                               {nm: as2d(v_in[nm]) for nm in SMALL_NAMES})

    shared = [src for src in BIG_SOURCES if src != "pool_w"]
    halves = []
    for src in shared:
        n_layers = shard_view(src, weights[src]).shape[0]
        acc = None
        for t in [t for t in BIGS.values() if t.src == src]:
            if t.name in fused:
                own, land = fused[t.name]
                acc = _chip_sum_fused(t, own, land, kc, n_layers, acc, by_cols=src == "w_down")
            else:
                s, l = landed[t.name]
                acc = _chip_sum(t, s, l, kc, n_layers, acc)
        halves.append(acc)
    full_grads = dict(zip(shared, _pair_share(halves, [src == "w_down" for src in shared])))
    full_grads["pool_w"] = g_pool

    out = {"grad": {}, "delta": {}, "new_m": {}, "new_v": {}}
    for src in BIG_SOURCES:
        g = full_grads[src]
        t = next(t for t in BIGS.values() if t.src == src)
        res = _adamw(src, t.rb, shard_view(src, weights[src]), g, shard_view(src, m_in[src]), shard_view(src, v_in[src]))
        shape = weights[src].shape
        for kind, a in zip(("grad", "delta", "new_m", "new_v"), res):
            out[kind][src] = a.reshape(shape)
    for nm in SMALL_NAMES:
        shape = weights[nm].shape
        for kind, a in zip(("grad", "delta", "new_m", "new_v"), small[nm]):
            out[kind][nm] = a.reshape(shape)

    return (loss.reshape(()), dx0.reshape(x.shape),
            *[out["grad"][nm] for nm in order], *[out["delta"][nm] for nm in order],
            *[out["new_m"][nm] for nm in order], *[out["new_v"][nm] for nm in order])
```

```python
import collections

import jax
import jax.numpy as jnp
from jax import lax
from jax.experimental import pallas as pl
from jax.experimental.pallas import tpu as pltpu

D = 1024
FF = 2816
N_HEADS = 16
HEAD_DIM = 64
N_KV_HEADS = 4
GQA = N_HEADS // N_KV_HEADS
KVD = N_KV_HEADS * HEAD_DIM
PLE = 256
BLK = 128
WINDOWS = (2, 4, 8, 16)
POOL_G = 256
HALO = 16
EPS = 1e-6
NEG_INF = -1e30
ATT_SCALE = HEAD_DIM ** -0.5
SLOPES = tuple(2.0 ** (-8.0 * (h + 1) / N_HEADS) for h in range(N_HEADS))
N_CHIPS = 4
N_DEV = 8

LR, B1, B2, AEPS, WD, STEP = 0.001, 0.9, 0.999, 1e-08, 0.01, 10
BC1 = 1.0 - B1 ** STEP
BC2 = 1.0 - B2 ** STEP

BF = jnp.bfloat16
F32 = jnp.float32
MESH = pl.DeviceIdType.MESH
VMEM_LIMIT_V7X = 58 * 1024 * 1024
TM = 256
TM_FFN_BWD = 512
FF_CHUNK = 256
FF_HALF = FF // 2

VSPEC = pl.BlockSpec(memory_space=pltpu.VMEM)
SSPEC = pl.BlockSpec(memory_space=pltpu.SMEM)
ANYSPEC = pl.BlockSpec(memory_space=pl.ANY)


def _params(n_grid=0):
    sem = ("arbitrary",) * n_grid if n_grid else None
    return pltpu.CompilerParams(dimension_semantics=sem, vmem_limit_bytes=VMEM_LIMIT_V7X)


def _sds(shape, dtype=F32):
    return jax.ShapeDtypeStruct(tuple(shape), dtype)


Rider = collections.namedtuple("Rider", "arrays out_shapes aliases scratch start mid finish")
MID_NUM, MID_DEN = 5, 8


def _call(body, *, name, grid, in_specs, out_specs, out_shape, args, scratch_shapes=(), rider=None, prefetch=None):
    ni, no, ns = len(in_specs), len(out_specs), len(scratch_shapes)
    npre = 0 if prefetch is None else 1
    pre = [] if prefetch is None else [prefetch]
    if rider is None:
        rider = Rider([], [], {}, [], None, None, None)
    ri, ro = len(rider.arrays), len(rider.out_shapes)

    def full(*refs):
        pre_refs, refs = refs[:npre], refs[npre:]
        ins, refs = refs[:ni], refs[ni:]
        rins, refs = refs[:ri], refs[ri:]
        outs, refs = refs[:no], refs[no:]
        routs, refs = refs[:ro], refs[ro:]
        scr, rscr = refs[:ns], refs[ns:]
        ids = [pl.program_id(a) for a in range(len(grid))]
        first = ids[0] == 0
        last = ids[0] == grid[0] - 1
        for a in range(1, len(grid)):
            first = first & (ids[a] == 0)
            last = last & (ids[a] == grid[a] - 1)

        if rider.start is not None:
            @pl.when(first)
            def _():
                rider.start(rins, routs, rscr)

        if rider.mid is not None:
            assert len(grid) == 1

            @pl.when(ids[0] == (grid[0] * MID_NUM) // MID_DEN)
            def _():
                rider.mid(rins, routs, rscr)

        body(*pre_refs, *ins, *outs, *scr)

        if rider.finish is not None:
            @pl.when(last)
            def _():
                rider.finish(rins, routs, rscr)

    outs = pl.pallas_call(
        full, name=name,
        grid_spec=pltpu.PrefetchScalarGridSpec(
            num_scalar_prefetch=npre, grid=grid,
            in_specs=list(in_specs) + [ANYSPEC] * ri, out_specs=list(out_specs) + [ANYSPEC] * ro,
            scratch_shapes=list(scratch_shapes) + list(rider.scratch)),
        out_shape=list(out_shape) + list(rider.out_shapes),
        input_output_aliases={npre + ni + a: no + b for a, b in rider.aliases.items()},
        compiler_params=_params(len(grid)))(*pre, *args, *rider.arrays)
    return list(outs[:no]), list(outs[no:])


def _run(name, rider):
    ri = len(rider.arrays)

    def body(*refs):
        rins, routs, rscr = refs[:ri], refs[ri:ri + len(rider.out_shapes)], refs[ri + len(rider.out_shapes):]
        rider.start(rins, routs, rscr)
        if rider.mid is not None:
            rider.mid(rins, routs, rscr)
        rider.finish(rins, routs, rscr)

    return pl.pallas_call(
        body, name=name, in_specs=[ANYSPEC] * ri, out_specs=[ANYSPEC] * len(rider.out_shapes),
        out_shape=list(rider.out_shapes), scratch_shapes=list(rider.scratch),
        input_output_aliases=dict(rider.aliases), compiler_params=_params())(*rider.arrays)


def _rms_fwd(x, g):
    r = lax.rsqrt(jnp.mean(x * x, axis=-1, keepdims=True) + EPS)
    return x * r * g


def _rms_bwd(x, g, dy):
    r = lax.rsqrt(jnp.mean(x * x, axis=-1, keepdims=True) + EPS)
    xn = x * r
    dxn = dy * g
    dx = r * (dxn - xn * jnp.mean(dxn * xn, axis=-1, keepdims=True))
    return dx, dy * xn


def _rowsum(a):
    return jnp.sum(a, axis=0, keepdims=True)


def _sigmoid(z):
    return 1.0 / (1.0 + jnp.exp(-z))


def _dot(a, b):
    return jnp.dot(a, b, preferred_element_type=F32)


def _dot_nt(a, b):
    return lax.dot_general(a, b, (((1,), (1,)), ((), ())), preferred_element_type=F32)


def _dot_tn(a, b):
    return lax.dot_general(a, b, (((0,), (0,)), ((), ())), preferred_element_type=F32)


def _row_spec(tm, width=D):
    return pl.BlockSpec((tm, width), lambda i: (i, 0))


def _const_spec(shape):
    zeros = (0,) * len(shape)
    return pl.BlockSpec(tuple(shape), lambda *_: zeros)


def _pool_delta(he, pos):
    out = []
    for gi, w in enumerate(WINDOWS):
        hg = he[:, gi * POOL_G:(gi + 1) * POOL_G]
        s = hg
        k = 1
        while k < w:
            s = s + pltpu.roll(s, k, 0)
            k *= 2
        cnt = jnp.maximum(jnp.minimum(pos + 1, w), 1).astype(F32)
        out.append(s / cnt - hg)
    return out


def _load_with_halo_before(x_ref, i, tm):
    r0 = pl.multiple_of(i * tm, tm)
    hs = pl.multiple_of(jnp.maximum(i * tm - HALO, 0), 8)
    xh = jnp.where(i > 0, x_ref[pl.ds(hs, HALO), :], 0.0)
    xt = x_ref[pl.ds(r0, tm), :]
    return xt, jnp.concatenate([xh, xt], axis=0)


def _mixa_fwd(x, pre_g, pool_w, pool_scale, post_g):
    s_len = x.shape[0]
    n = s_len // TM

    def body(x_ref, pg_ref, w_ref, sc_ref, qg_ref, y_ref, x1_ref):
        i = pl.program_id(0)
        xt, xe = _load_with_halo_before(x_ref, i, TM)
        he = _rms_fwd(xe, pg_ref[0:1, :])
        pos = i * TM - HALO + lax.broadcasted_iota(jnp.int32, (TM + HALO, 1), 0)
        ds = _pool_delta(he, pos)
        ys = [_dot(ds[gi][HALO:, :].astype(BF), w_ref[gi]) for gi in range(len(WINDOWS))]
        y = jnp.concatenate(ys, axis=1) * sc_ref[...]
        y_ref[...] = y
        x1_ref[...] = xt + _rms_fwd(y, qg_ref[0:1, :])

    outs, _ = _call(body, name="mixa_fwd", grid=(n,),
                    in_specs=[VSPEC] * 5, out_specs=[_row_spec(TM), _row_spec(TM)],
                    out_shape=[_sds((s_len, D)), _sds((s_len, D))],
                    args=[x, pre_g, pool_w, pool_scale, post_g])
    return outs


def _mixa_bwd(dx1, x, y, pre_g, pool_w, pool_scale, post_g, rider=None):
    s_len = x.shape[0]
    n = s_len // TM
    ng = len(WINDOWS)

    def body(dx_ref, x_ref, y_ref, pg_ref, w_ref, sc_ref, qg_ref,
             dx0_ref, dw_ref, dsc_ref, dqg_ref, dpg_ref, wacc):
        i = pl.program_id(0)

        @pl.when(i == 0)
        def _():
            wacc[...] = jnp.zeros_like(wacc)
            dsc_ref[...] = jnp.zeros_like(dsc_ref)
            dqg_ref[...] = jnp.zeros_like(dqg_ref)
            dpg_ref[...] = jnp.zeros_like(dpg_ref)

        r0 = pl.multiple_of(i * TM, TM)
        xt, xe = _load_with_halo_before(x_ref, i, TM)
        he = _rms_fwd(xe, pg_ref[0:1, :])
        pos_b = i * TM - HALO + lax.broadcasted_iota(jnp.int32, (TM + HALO, 1), 0)
        ds = _pool_delta(he, pos_b)

        last = i == n - 1
        a0 = pl.multiple_of(jnp.minimum(i * TM + TM, s_len - HALO), 8)
        ye = jnp.concatenate([y_ref[pl.ds(r0, TM), :], y_ref[pl.ds(a0, HALO), :]], axis=0)
        dt = dx_ref[pl.ds(r0, TM), :]
        de = jnp.concatenate([dt, jnp.where(last, 0.0, dx_ref[pl.ds(a0, HALO), :])], axis=0)
        dye, prod = _rms_bwd(ye, qg_ref[0:1, :], de)
        dqg_ref[...] += _rowsum(prod[:TM, :])
        dys = dye * sc_ref[...]
        pos_a = i * TM + lax.broadcasted_iota(jnp.int32, (TM + HALO, 1), 0)

        dhs, dscs = [], []
        for gi, w in enumerate(WINDOWS):
            sl = slice(gi * POOL_G, (gi + 1) * POOL_G)
            wg = w_ref[gi]
            dys_g = dys[:, sl].astype(BF)
            d_g = ds[gi][HALO:, :].astype(BF)
            ypre = _dot(d_g, wg)
            dscs.append(_rowsum(dye[:TM, sl] * ypre))
            wacc[gi] += _dot_tn(d_g, dys_g[:TM, :])
            dd = _dot_nt(dys_g, wg)
            cnt = jnp.minimum(pos_a + 1, w).astype(F32)
            a = dd / cnt
            k = 1
            while k < w:
                a = a + pltpu.roll(a, TM + HALO - k, 0)
                k *= 2
            dhs.append(a[:TM, :] - dd[:TM, :])
        dsc_ref[...] += jnp.concatenate(dscs, axis=1)
        dh = jnp.concatenate(dhs, axis=1)
        dxp, prod2 = _rms_bwd(xt, pg_ref[0:1, :], dh)
        dpg_ref[...] += _rowsum(prod2)
        dx0_ref[...] = dt + dxp

        @pl.when(last)
        def _():
            dw_ref[...] = wacc[...].astype(BF)

    return _call(
        body, name="mixa_bwd", grid=(n,), in_specs=[VSPEC] * 7,
        out_specs=[_row_spec(TM), _const_spec((ng, POOL_G, POOL_G)), _const_spec((1, D)),
                   _const_spec((1, D)), _const_spec((1, D))],
        out_shape=[_sds((s_len, D)), _sds((ng, POOL_G, POOL_G), BF), _sds((1, D)), _sds((1, D)), _sds((1, D))],
        scratch_shapes=[pltpu.VMEM((ng, POOL_G, POOL_G), F32)],
        args=[dx1, x, y, pre_g, pool_w, pool_scale, post_g], rider=rider)


def _ffn_fwd(layer, x1, pre_g, wgu, wd, post_g, rider=None):
    s_len = x1.shape[0]

    def body(x_ref, pg_ref, wgu_ref, wd_ref, qg_ref, f_ref, x2_ref):
        x = x_ref[...]
        h = _rms_fwd(x, pg_ref[layer:layer + 1, :]).astype(BF)
        f = jnp.zeros((TM, D), F32)
        for c in range(FF // FF_HALF):
            cols = slice(c * FF_HALF, (c + 1) * FF_HALF)
            g = _dot(h, wgu_ref[0, :, cols])
            u = _dot(h, wgu_ref[1, :, cols])
            act = g * _sigmoid(g) * u
            f = f + _dot(act.astype(BF), wd_ref[cols, :])
        f_ref[...] = f
        x2_ref[...] = x + _rms_fwd(f, qg_ref[layer:layer + 1, :])

    return _call(body, name=f"ffn_fwd{layer}", grid=(s_len // TM,),
                 in_specs=[_row_spec(TM), VSPEC, VSPEC, VSPEC, VSPEC],
                 out_specs=[_row_spec(TM), _row_spec(TM)],
                 out_shape=[_sds((s_len, D)), _sds((s_len, D))],
                 args=[x1, pre_g, wgu, wd, post_g], rider=rider)


GU_PIECE = 128
DN_PIECE = 64
DN_SLOT = FF // N_CHIPS
HALF_D = D // 2


def _ffn_bwd(layer, dx2, x1, f, pre_g, wgu, wd, post_g, kc, rider=None):
    s_len = x1.shape[0]
    tm = TM_FFN_BWD
    n = s_len // tm
    nc = FF // FF_CHUNK
    n_gu, n_dn = FF_CHUNK // GU_PIECE, FF_CHUNK // DN_PIECE
    n_pieces = 2 * n_gu + n_dn
    n_blk = FF_HALF // GU_PIECE

    def edge_rows(c, i, kc_ref):
        return (jnp.where((c == 0) | (c == nc - 1), i, n - 1), 0)

    def chunk_at(c, kc_ref):
        return (c + (((kc_ref[0] + 1) % N_CHIPS) * nc) // N_CHIPS) % nc

    def exchange(kc_ref, c, accg, accu, accd, own_gu_ref, land_gu_ref, own_dn_ref, land_dn_ref,
                 pl_gu, pl_dn, sib_gu, sib_dn, mine_gu, mine_dn, sum_gu, sum_dn,
                 psend, precv, ssend, lsem, rrecv):
        x, y, core = lax.axis_index("x"), lax.axis_index("y"), lax.axis_index("c")
        lower = core == 0

        def pair_copy(cc, part):
            p = cc % 2
            src, dst = ((sib_gu, pl_gu), (sib_dn, pl_dn))[part]
            return pltpu.make_async_remote_copy(src.at[p], dst.at[cc], psend.at[p, part], precv.at[cc, part],
                                                device_id=(x, y, 1 - core), device_id_type=MESH)

        def scatter(cc, wait):
            p = cc % 2
            jobs = []
            for gu in range(2):
                for hc in range(n_gu):
                    hidden = chunk_at(cc, kc_ref) * FF_CHUNK + hc * GU_PIECE
                    k = hidden // FF_HALF
                    jobs.append((sum_gu.at[p, gu, hc], k + 2 * gu, 0,
                                 own_gu_ref, land_gu_ref, ((hidden - k * FF_HALF) // GU_PIECE,)))
            for q in range(n_dn):
                hidden = chunk_at(cc, kc_ref) * FF_CHUNK + q * DN_PIECE
                k = hidden // DN_SLOT
                off = pl.multiple_of(hidden - k * DN_SLOT, DN_PIECE)
                jobs.append((sum_dn.at[p, pl.ds(q * DN_PIECE, DN_PIECE), :], k, 1,
                             own_dn_ref, land_dn_ref, (pl.ds(off, DN_PIECE), slice(None))))
            for pi, (src, k, t, own_ref, land_ref, where) in enumerate(jobs):
                kx, ky = k // 2, k % 2
                fx, fy = (kx != x).astype(jnp.int32), (ky != y).astype(jnp.int32)
                local = (fx + fy) == 0
                j = jnp.maximum(fx + 2 * fy - 1, 0)

                @pl.when(local)
                def _():
                    cp = pltpu.make_async_copy(src, own_ref.at[where], lsem.at[p, pi])
                    if wait:
                        cp.wait()
                    else:
                        cp.start()

                @pl.when(jnp.logical_not(local))
                def _():
                    cp = pltpu.make_async_remote_copy(src, land_ref.at[(j,) + where], ssend.at[p, pi],
                                                      rrecv.at[t, j], device_id=(kx, ky, core), device_id_type=MESH)
                    if wait:
                        cp.wait_send()
                    else:
                        cp.start()

        def add_and_scatter(cc):
            p = cc % 2
            pair_copy(cc, 0).wait_recv()
            pair_copy(cc, 1).wait_recv()
            s_gu = (mine_gu[...] + pl_gu[cc].astype(F32)).astype(BF)
            for hc in range(n_gu):
                sum_gu[p, :, hc] = s_gu[:, :, hc * GU_PIECE:(hc + 1) * GU_PIECE]
            sum_dn[p] = (mine_dn[...] + pl_dn[cc].astype(F32)).astype(BF)
            scatter(cc, wait=False)

        @pl.when(c >= 1)
        def _():
            @pl.when(c >= 3)
            def _():
                scatter(c - 3, wait=True)
            add_and_scatter(c - 1)

        @pl.when(c >= 2)
        def _():
            pair_copy(c - 2, 0).wait_send()
            pair_copy(c - 2, 1).wait_send()

        p = c % 2
        my_rows = pl.ds(pl.multiple_of(core * HALF_D, HALF_D), HALF_D)
        sib_rows = pl.ds(pl.multiple_of((1 - core) * HALF_D, HALF_D), HALF_D)
        d_v = accd[...]
        sib_gu[p, 0] = accg[sib_rows, :].astype(BF)
        sib_gu[p, 1] = accu[sib_rows, :].astype(BF)
        sib_dn[p] = jnp.where(lower, d_v[:, HALF_D:], d_v[:, :HALF_D]).astype(BF)
        mine_gu[0] = accg[my_rows, :]
        mine_gu[1] = accu[my_rows, :]
        mine_dn[...] = jnp.where(lower, d_v[:, :HALF_D], d_v[:, HALF_D:])
        pair_copy(c, 0).start()
        pair_copy(c, 1).start()

        @pl.when(c == nc - 1)
        def _():
            scatter(nc - 3, wait=True)
            add_and_scatter(nc - 1)
            for cc in (nc - 2, nc - 1):
                pair_copy(cc, 0).wait_send()
                pair_copy(cc, 1).wait_send()
                scatter(cc, wait=True)
            for t, land_ref in enumerate((land_gu_ref, land_dn_ref)):
                for j in range(N_CHIPS - 1):
                    pltpu.make_async_remote_copy(land_ref.at[j], land_ref.at[j], ssend.at[0, 0], rrecv.at[t, j],
                                                 device_id=(x, y, core), device_id_type=MESH).wait_recv()

    def body(kc_ref, dx_ref, x_ref, f_ref, pg_ref, wgu_ref, wd_ref, qg_ref,
             dx1_ref, dpg_ref, dqg_ref, own_gu_ref, land_gu_ref, own_dn_ref, land_dn_ref,
             h_s, df_s, dh_s, accg, accu, accd, *comm):
        c = pl.program_id(0)
        i = pl.program_id(1)
        rows = pl.ds(pl.multiple_of(i * tm, tm), tm)
        pg = pg_ref[layer:layer + 1, :]

        @pl.when((c == 0) & (i == 0))
        def _():
            dpg_ref[...] = jnp.zeros_like(dpg_ref)
            dqg_ref[...] = jnp.zeros_like(dqg_ref)

        @pl.when(c == 0)
        def _():
            h_s[rows, :] = _rms_fwd(x_ref[...], pg).astype(BF)
            df, prod = _rms_bwd(f_ref[...], qg_ref[layer:layer + 1, :], dx_ref[...])
            df_s[rows, :] = df.astype(BF)
            dqg_ref[...] += _rowsum(prod)

        @pl.when(i == 0)
        def _():
            accg[...] = jnp.zeros_like(accg)
            accu[...] = jnp.zeros_like(accu)
            accd[...] = jnp.zeros_like(accd)

        h = h_s[rows, :]
        df = df_s[rows, :]
        wg = wgu_ref[0]
        wu = wgu_ref[1]
        g = _dot(h, wg)
        u = _dot(h, wu)
        sg = _sigmoid(g)
        a = g * sg
        dact = _dot_nt(df, wd_ref[...])
        accd[...] += _dot_tn((a * u).astype(BF), df)
        du = (dact * a).astype(BF)
        dg = (dact * u * (sg * (1.0 + g * (1.0 - sg)))).astype(BF)
        accg[...] += _dot_tn(h, dg)
        accu[...] += _dot_tn(h, du)
        dh = _dot_nt(dg, wg) + _dot_nt(du, wu)

        @pl.when(c == 0)
        def _():
            dh_s[rows, :] = dh

        @pl.when((c > 0) & (c < nc - 1))
        def _():
            dh_s[rows, :] += dh

        @pl.when(c == nc - 1)
        def _():
            dxp, prod = _rms_bwd(x_ref[...], pg, dh_s[rows, :] + dh)
            dpg_ref[...] += _rowsum(prod)
            dx1_ref[...] = dx_ref[...] + dxp

        @pl.when(i == n - 1)
        def _():
            exchange(kc_ref, c, accg, accu, accd, own_gu_ref, land_gu_ref, own_dn_ref, land_dn_ref, *comm)

    dma = pltpu.SemaphoreType.DMA
    return _call(
        body, name=f"ffn_bwd{layer}", grid=(nc, n),
        in_specs=[pl.BlockSpec((tm, D), edge_rows), pl.BlockSpec((tm, D), edge_rows),
                  pl.BlockSpec((tm, D), lambda c, i, kc_ref: (jnp.where(c == 0, i, n - 1), 0),
                               pipeline_mode=pl.Buffered(1)),
                  VSPEC,
                  pl.BlockSpec((2, D, FF_CHUNK), lambda c, i, kc_ref: (0, 0, chunk_at(c, kc_ref))),
                  pl.BlockSpec((FF_CHUNK, D), lambda c, i, kc_ref: (chunk_at(c, kc_ref), 0)),
                  VSPEC],
        out_specs=[pl.BlockSpec((tm, D), lambda c, i, kc_ref: (jnp.where(c == nc - 1, i, 0), 0)),
                   _const_spec((1, D)), _const_spec((1, D)), ANYSPEC, ANYSPEC, ANYSPEC, ANYSPEC],
        out_shape=[_sds((s_len, D)), _sds((1, D)), _sds((1, D)),
                   _sds((n_blk, HALF_D, GU_PIECE), BF), _sds((N_CHIPS - 1, n_blk, HALF_D, GU_PIECE), BF),
                   _sds((DN_SLOT, HALF_D), BF), _sds((N_CHIPS - 1, DN_SLOT, HALF_D), BF)],
        scratch_shapes=[pltpu.VMEM((s_len, D), BF), pltpu.VMEM((s_len, D), BF), pltpu.VMEM((s_len, D), F32),
                        pltpu.VMEM((D, FF_CHUNK), F32), pltpu.VMEM((D, FF_CHUNK), F32),
                        pltpu.VMEM((FF_CHUNK, D), F32),
                        pltpu.VMEM((nc, 2, HALF_D, FF_CHUNK), BF), pltpu.VMEM((nc, FF_CHUNK, HALF_D), BF),
                        pltpu.VMEM((2, 2, HALF_D, FF_CHUNK), BF), pltpu.VMEM((2, FF_CHUNK, HALF_D), BF),
                        pltpu.VMEM((2, HALF_D, FF_CHUNK), F32), pltpu.VMEM((FF_CHUNK, HALF_D), F32),
                        pltpu.VMEM((2, 2, n_gu, HALF_D, GU_PIECE), BF), pltpu.VMEM((2, FF_CHUNK, HALF_D), BF),
                        dma((2, 2)), dma((nc, 2)), dma((2, n_pieces)), dma((2, n_pieces)), dma((2, N_CHIPS - 1))],
        args=[dx2, x1, f, pre_g, wgu, wd, post_g], rider=rider, prefetch=kc)


def _ple_fwd(layer, x2, p, ple_g, w_gate, w_proj, post_g, target=None, rider=None):
    s_len = x2.shape[0]
    final = target is not None

    def body(*refs):
        if final:
            x_ref, p_ref, g_ref, wg_ref, wp_ref, qg_ref, t_ref, z_ref, pe_ref, dx_ref, lv_ref = refs
        else:
            x_ref, p_ref, g_ref, wg_ref, wp_ref, qg_ref, z_ref, pe_ref, x3_ref = refs
        x = x_ref[...]
        r = _rms_fwd(x, g_ref[layer:layer + 1, :]).astype(BF)
        z = _dot(r, wg_ref[...])
        pe = _dot(p_ref[...].astype(BF), wp_ref[...])
        z_ref[...] = z
        pe_ref[...] = pe
        x3 = x + _rms_fwd(pe * _sigmoid(z), qg_ref[layer:layer + 1, :])
        if final:
            @pl.when(pl.program_id(0) == 0)
            def _():
                lv_ref[...] = jnp.zeros_like(lv_ref)
            err = x3 - t_ref[...]
            dx_ref[...] = err * (1.0 / D)
            lv_ref[...] += _rowsum(err * err)
        else:
            x3_ref[...] = x3

    p_spec = pl.BlockSpec((None, TM, PLE), lambda i: (layer, i, 0))
    in_specs = [_row_spec(TM), p_spec, VSPEC, VSPEC, VSPEC, VSPEC]
    args = [x2, p, ple_g, w_gate, w_proj, post_g]
    out_specs = [_row_spec(TM), _row_spec(TM), _row_spec(TM)]
    out_shape = [_sds((s_len, D))] * 3
    if final:
        in_specs.append(_row_spec(TM))
        args.append(target)
        out_specs.append(_const_spec((1, D)))
        out_shape.append(_sds((1, D)))
    return _call(body, name=f"ple_fwd{layer}", grid=(s_len // TM,), in_specs=in_specs, out_specs=out_specs,
                 out_shape=out_shape, args=args, rider=rider)


def _ple_bwd(layer, dx3, x2, z, pe, p, ple_g, w_gate, post_g, rider=None):
    s_len = x2.shape[0]
    n = s_len // TM

    def body(dx_ref, x_ref, z_ref, pe_ref, p_ref, g_ref, wg_ref, qg_ref,
             dx2_ref, dwg_ref, dwp_ref, dg_ref, dqg_ref, gacc, pacc):
        i = pl.program_id(0)

        @pl.when(i == 0)
        def _():
            gacc[...] = jnp.zeros_like(gacc)
            pacc[...] = jnp.zeros_like(pacc)
            dg_ref[...] = jnp.zeros_like(dg_ref)
            dqg_ref[...] = jnp.zeros_like(dqg_ref)

        dx = dx_ref[...]
        x = x_ref[...]
        pe_v = pe_ref[...]
        gate = _sigmoid(z_ref[...])
        de, prod = _rms_bwd(pe_v * gate, qg_ref[layer:layer + 1, :], dx)
        dqg_ref[...] += _rowsum(prod)
        dpe = (de * gate).astype(BF)
        dz = (de * pe_v * gate * (1.0 - gate)).astype(BF)
        pacc[...] += _dot_tn(p_ref[...].astype(BF), dpe)
        g = g_ref[layer:layer + 1, :]
        r = _rms_fwd(x, g).astype(BF)
        gacc[...] += _dot_tn(r, dz)
        dr = _dot_nt(dz, wg_ref[...])
        dxp, prod2 = _rms_bwd(x, g, dr)
        dg_ref[...] += _rowsum(prod2)
        dx2_ref[...] = dx + dxp

        @pl.when(i == n - 1)
        def _():
            dwg_ref[...] = gacc[...].astype(BF)
            dwp_ref[...] = pacc[...].astype(BF)

    p_spec = pl.BlockSpec((None, TM, PLE), lambda i: (layer, i, 0))
    return _call(
        body, name=f"ple_bwd{layer}", grid=(n,),
        in_specs=[_row_spec(TM), _row_spec(TM), _row_spec(TM), _row_spec(TM), p_spec, VSPEC, VSPEC, VSPEC],
        out_specs=[_row_spec(TM), _const_spec((D, D)), _const_spec((PLE, D)), _const_spec((1, D)), _const_spec((1, D))],
        out_shape=[_sds((s_len, D)), _sds((D, D), BF), _sds((PLE, D), BF), _sds((1, D)), _sds((1, D))],
        scratch_shapes=[pltpu.VMEM((D, D), F32), pltpu.VMEM((PLE, D), F32)],
        args=[dx3, x2, z, pe, p, ple_g, w_gate, post_g], rider=rider)


def _qkv_fwd(x3, q_g, kv_g, w_q, w_kv, rider=None):
    s_len = x3.shape[0]

    def body(x_ref, qg_ref, kg_ref, wq_ref, wkv_ref, q_ref, kv_ref):
        x = x_ref[...]
        q_ref[...] = _dot(_rms_fwd(x, qg_ref[1:2, :]).astype(BF), wq_ref[...]).astype(BF)
        kv_ref[...] = _dot(_rms_fwd(x, kg_ref[...]).astype(BF), wkv_ref[...]).astype(BF)

    return _call(body, name="qkv_fwd", grid=(s_len // TM,),
                 in_specs=[_row_spec(TM), VSPEC, VSPEC, VSPEC, VSPEC],
                 out_specs=[_row_spec(TM), _row_spec(TM, 2 * KVD)],
                 out_shape=[_sds((s_len, D), BF), _sds((s_len, 2 * KVD), BF)],
                 args=[x3, q_g, kv_g, w_q, w_kv], rider=rider)


def _qkv_bwd(dq, dkv, x3, dx4, q_g, kv_g, w_q, w_kv):
    s_len = x3.shape[0]
    n = s_len // TM

    def body(dq_ref, dkv_ref, x_ref, dx_ref, qg_ref, kg_ref, wq_ref, wkv_ref,
             dx3_ref, dwq_ref, dwkv_ref, dqg_ref, dkg_ref, qacc, kacc):
        i = pl.program_id(0)

        @pl.when(i == 0)
        def _():
            qacc[...] = jnp.zeros_like(qacc)
            kacc[...] = jnp.zeros_like(kacc)
            dqg_ref[...] = jnp.zeros_like(dqg_ref)
            dkg_ref[...] = jnp.zeros_like(dkg_ref)

        x = x_ref[...]
        qg = qg_ref[1:2, :]
        kg = kg_ref[...]
        dq_v = dq_ref[...]
        dkv_v = dkv_ref[...].astype(BF)
        qacc[...] += _dot_tn(_rms_fwd(x, qg).astype(BF), dq_v)
        kacc[...] += _dot_tn(_rms_fwd(x, kg).astype(BF), dkv_v)
        dxq, prod_q = _rms_bwd(x, qg, _dot_nt(dq_v, wq_ref[...]))
        dxk, prod_k = _rms_bwd(x, kg, _dot_nt(dkv_v, wkv_ref[...]))
        dqg_ref[...] += _rowsum(prod_q)
        dkg_ref[...] += _rowsum(prod_k)
        dx3_ref[...] = dx_ref[...] + dxq + dxk

        @pl.when(i == n - 1)
        def _():
            dwq_ref[...] = qacc[...].astype(BF)
            dwkv_ref[...] = kacc[...].astype(BF)

    outs, _ = _call(
        body, name="qkv_bwd", grid=(n,),
        in_specs=[_row_spec(TM), _row_spec(TM, 2 * KVD), _row_spec(TM), _row_spec(TM), VSPEC, VSPEC, VSPEC, VSPEC],
        out_specs=[_row_spec(TM), _const_spec((D, D)), _const_spec((D, 2 * KVD)),
                   _const_spec((1, D)), _const_spec((1, D))],
        out_shape=[_sds((s_len, D)), _sds((D, D), BF), _sds((D, 2 * KVD), BF), _sds((1, D)), _sds((1, D))],
        scratch_shapes=[pltpu.VMEM((D, D), F32), pltpu.VMEM((D, 2 * KVD), F32)],
        args=[dq, dkv, x3, dx4, q_g, kv_g, w_q, w_kv])
    return outs


def _attn_group(i, q, kvw, sink_ref, g):
    rows = GQA * BLK
    heads = [GQA * g + j for j in range(GQA)]
    off = jnp.where(i > 0, BLK, 0)
    row = lax.broadcasted_iota(jnp.int32, (rows, 2 * BLK), 0)
    rel = (row % BLK) - lax.broadcasted_iota(jnp.int32, (rows, 2 * BLK), 1) + off
    valid = (rel >= 0) & (rel < BLK)
    head_of_row = lax.broadcasted_iota(jnp.int32, (rows, 1), 0) // BLK
    slope = jnp.zeros((rows, 1), F32)
    sink = jnp.zeros((rows, 1), F32)
    for j, h in enumerate(heads):
        slope = jnp.where(head_of_row == j, SLOPES[h], slope)
        sink = jnp.where(head_of_row == j, sink_ref[0, h], sink)
    qs = jnp.concatenate([q[:, h * HEAD_DIM:(h + 1) * HEAD_DIM] for h in heads], axis=0)
    k = kvw[:, g * HEAD_DIM:(g + 1) * HEAD_DIM]
    v = kvw[:, KVD + g * HEAD_DIM:KVD + (g + 1) * HEAD_DIM]
    s = _dot_nt(qs, k) * ATT_SCALE - slope * rel.astype(F32)
    s = jnp.where(valid, s, NEG_INF)
    m = jnp.maximum(jnp.max(s, axis=-1, keepdims=True), sink)
    e = jnp.exp(s - m)
    es = jnp.exp(sink - m)
    inv = 1.0 / (jnp.sum(e, axis=-1, keepdims=True) + es)
    return e * inv, es * inv, qs, k, v


def _unstack_heads(stacked):
    return [stacked[j * BLK:(j + 1) * BLK, :] for j in range(GQA)]


def _kv_window(kv_ref, i):
    ks = pl.multiple_of(jnp.maximum(i * BLK - BLK, 0), BLK)
    return ks, kv_ref[pl.ds(ks, 2 * BLK), :]


def _attn_fwd(q, kv, sinks, x3, w_o, post_g, rider=None):
    s_len = q.shape[0]

    def body(q_ref, kv_ref, sk_ref, x_ref, wo_ref, g_ref, a_ref, y_ref, x4_ref):
        i = pl.program_id(0)
        _, kvw = _kv_window(kv_ref, i)
        q = q_ref[...]
        outs = []
        for g in range(N_KV_HEADS):
            p, _, _, _, v = _attn_group(i, q, kvw, sk_ref, g)
            outs += _unstack_heads(_dot(p.astype(BF), v))
        attn = jnp.concatenate(outs, axis=1)
        a_ref[...] = attn
        y = _dot(attn.astype(BF), wo_ref[...])
        y_ref[...] = y
        x4_ref[...] = x_ref[...] + _rms_fwd(y, g_ref[1:2, :])

    return _call(body, name="attn_fwd", grid=(s_len // BLK,),
                 in_specs=[_row_spec(BLK), VSPEC, SSPEC, _row_spec(BLK), VSPEC, VSPEC],
                 out_specs=[_row_spec(BLK)] * 3, out_shape=[_sds((s_len, D))] * 3,
                 args=[q, kv, sinks, x3, w_o, post_g], rider=rider)


def _attn_bwd(dx4, y, attn, q, kv, sinks, w_o, post_g, rider=None):
    s_len = q.shape[0]
    n = s_len // BLK

    def body(dx_ref, y_ref, a_ref, q_ref, kv_ref, sk_ref, wo_ref, g_ref,
             dq_ref, dkv_ref, dwo_ref, dg_ref, dsk_ref, wacc):
        i = pl.program_id(0)

        @pl.when(i == 0)
        def _():
            dkv_ref[...] = jnp.zeros_like(dkv_ref)
            wacc[...] = jnp.zeros_like(wacc)
            dg_ref[...] = jnp.zeros_like(dg_ref)
            dsk_ref[...] = jnp.zeros_like(dsk_ref)

        dy, prod = _rms_bwd(y_ref[...], g_ref[1:2, :], dx_ref[...])
        dg_ref[...] += _rowsum(prod)
        dyb = dy.astype(BF)
        attn = a_ref[...]
        wacc[...] += _dot_tn(attn.astype(BF), dyb)
        d_o = _dot_nt(dyb, wo_ref[...])
        dod = d_o * attn
        ks, kvw = _kv_window(kv_ref, i)
        q = q_ref[...]
        lane = lax.broadcasted_iota(jnp.int32, (1, D), 1)
        dqs, dks, dvs = [], [], []
        dsk = jnp.zeros((1, D), F32)
        for g in range(N_KV_HEADS):
            p, ps, qs, k, v = _attn_group(i, q, kvw, sk_ref, g)
            cols = [slice((GQA * g + j) * HEAD_DIM, (GQA * g + j + 1) * HEAD_DIM) for j in range(GQA)]
            do_s = jnp.concatenate([d_o[:, c] for c in cols], axis=0).astype(BF)
            dsum = jnp.concatenate([jnp.sum(dod[:, c], axis=-1, keepdims=True) for c in cols], axis=0)
            dp = _dot_nt(do_s, v)
            dsb = (p * (dp - dsum) * ATT_SCALE).astype(BF)
            sink_part = ps * dsum
            for j in range(GQA):
                dsk = dsk + jnp.where(lane == GQA * g + j, -_rowsum(sink_part[j * BLK:(j + 1) * BLK, :]), 0.0)
            dqs += _unstack_heads(_dot(dsb, k))
            dks.append(_dot_tn(dsb, qs))
            dvs.append(_dot_tn(p.astype(BF), do_s))
        dsk_ref[...] += dsk
        dq_ref[...] = jnp.concatenate(dqs, axis=1).astype(BF)
        dkv_ref[pl.ds(ks, 2 * BLK), :] += jnp.concatenate(dks + dvs, axis=1)

        @pl.when(i == n - 1)
        def _():
            dwo_ref[...] = wacc[...].astype(BF)

    return _call(
        body, name="attn_bwd", grid=(n,),
        in_specs=[_row_spec(BLK), _row_spec(BLK), _row_spec(BLK), _row_spec(BLK), VSPEC, SSPEC, VSPEC, VSPEC],
        out_specs=[_row_spec(BLK), _const_spec((s_len, 2 * KVD)), _const_spec((D, D)),
                   _const_spec((1, D)), _const_spec((1, D))],
        out_shape=[_sds((s_len, D), BF), _sds((s_len, 2 * KVD)), _sds((D, D), BF), _sds((1, D)), _sds((1, D))],
        scratch_shapes=[pltpu.VMEM((D, D), F32)],
        args=[dx4, y, attn, q, kv, sinks, w_o, post_g], rider=rider)


Big = collections.namedtuple("Big", "name src layer L A R C rb")


def _bigs():
    out = {"pool_w": Big("pool_w", "pool_w", None, 4, 4, POOL_G // N_CHIPS, POOL_G, 32)}
    for l in range(2):
        out[f"w_gu{l}"] = Big(f"w_gu{l}", "w_gu", l, 1, 2, D, FF_HALF, 256)
        out[f"w_down{l}"] = Big(f"w_down{l}", "w_down", l, 1, 4, FF // N_CHIPS, D, 352)
        out[f"w_ple_gate{l}"] = Big(f"w_ple_gate{l}", "w_ple_gate", l, 1, 4, D // N_CHIPS, D, 128)
        out[f"w_ple_proj{l}"] = Big(f"w_ple_proj{l}", "w_ple_proj", l, 1, 1, PLE, D // N_CHIPS, 128)
    out["w_q"] = Big("w_q", "w_q", None, 1, 4, D // N_CHIPS, D, 128)
    out["w_o"] = Big("w_o", "w_o", None, 1, 4, D // N_CHIPS, D, 128)
    out["w_kv"] = Big("w_kv", "w_kv", None, 1, 4, D // N_CHIPS, 2 * KVD, 128)
    return out


BIGS = _bigs()
POOL_SCALE = Big("pool_scale", "pool_scale", None, 1, 1, 1, D // N_CHIPS, 1)
BIG_SOURCES = ("w_gu", "w_down", "w_ple_gate", "w_ple_proj", "w_q", "w_o", "w_kv", "pool_w")


def _ncb(t):
    return N_CHIPS // t.A


def _full_shape(t, rows=None):
    return (t.L, t.A, t.R if rows is None else rows, _ncb(t) * t.C)


def _slot_index(t, k):
    return k // _ncb(t), k % _ncb(t)


def _slot(ref, t, k, row0, rows):
    a, cb = _slot_index(t, k)
    return ref.at[:, a, pl.ds(row0, rows), pl.ds(pl.multiple_of(cb * t.C, 128), t.C)]


def _place(t, w, kc, out_dtype):
    rb = min(t.R, 2 * t.rb)

    def body(kc_ref, w_ref, o_ref):
        del kc_ref
        o_ref[...] = w_ref[...].astype(out_dtype)

    def in_map(l, j, kc_ref):
        return (l if t.layer is None else t.layer, j, 0)

    def out_map(l, j, kc_ref):
        a, cb = _slot_index(t, kc_ref[0])
        return (l, a, j, cb)

    return pl.pallas_call(
        body, name=f"place_{t.name}",
        grid_spec=pltpu.PrefetchScalarGridSpec(
            num_scalar_prefetch=1, grid=(t.L, t.R // rb),
            in_specs=[pl.BlockSpec((None, rb, t.C), in_map)],
            out_specs=pl.BlockSpec((None, None, rb, t.C), out_map)),
        out_shape=_sds(_full_shape(t), out_dtype),
        compiler_params=_params(2),
    )(kc, w)


def _mesh_position():
    x, y, c = lax.axis_index("x"), lax.axis_index("y"), lax.axis_index("c")
    chips = [(1 - x, y), (x, 1 - y), (1 - x, 1 - y)]
    return x, y, c, chips


def _gather_rider(parts, fulls):
    nt = len(parts)
    TO_X, TO_Y, FWD_X, FWD_Y, SIB_X, SIB_Y, SIB_D = range(7)

    def rows_of(ti, core):
        t, r0, r1 = parts[ti]
        h = (r1 - r0) // 2
        return r0 + core * h, h

    def copy(outs, sems, kind, ti, k_src, row0, rows, dev):
        region = _slot(outs[ti], parts[ti][0], k_src, row0, rows)
        return pltpu.make_async_remote_copy(region, region, sems[0].at[ti, kind], sems[1].at[ti, kind],
                                            device_id=dev, device_id_type=MESH)

    def plan(outs, sems):
        x, y, c, _ = _mesh_position()
        me, kx, ky, kd = 2 * x + y, 2 * (1 - x) + y, 2 * x + (1 - y), 2 * (1 - x) + (1 - y)
        dev_x, dev_y, dev_d, sib = (1 - x, y, c), (x, 1 - y, c), (1 - x, 1 - y, c), (x, y, 1 - c)

        def whole(ti):
            return 0, parts[ti][0].R

        def mk(kind, k_send, k_recv, dev, send_rows, recv_rows):
            def build(ti, side):
                k_src = k_send if side == "s" else k_recv
                row0, rows = (send_rows if side == "s" else recv_rows)(ti)
                return copy(outs, sems, kind, ti, k_src, row0, rows, dev)
            return build

        def first_half(core):
            return lambda ti: (rows_of(ti, core)[0], rows_of(ti, core)[1] // 2)

        def second_half(core):
            return lambda ti: (rows_of(ti, core)[0] + rows_of(ti, core)[1] // 2, rows_of(ti, core)[1] // 2)

        mine = lambda ti: rows_of(ti, c)
        theirs = lambda ti: rows_of(ti, 1 - c)
        split = {
            TO_X: mk(TO_X, me, kx, dev_x, mine, mine),
            TO_Y: mk(TO_Y, me, ky, dev_y, mine, mine),
            FWD_X: mk(FWD_X, ky, kd, dev_x, first_half(c), first_half(c)),
            FWD_Y: mk(FWD_Y, kx, kd, dev_y, second_half(c), second_half(c)),
            SIB_X: mk(SIB_X, kx, kx, sib, mine, theirs),
            SIB_Y: mk(SIB_Y, ky, ky, sib, mine, theirs),
            SIB_D: mk(SIB_D, kd, kd, sib, mine, theirs),
        }
        direct = {
            TO_X: mk(TO_X, me, kx, dev_x, whole, whole),
            TO_Y: mk(TO_Y, me, ky, dev_y, whole, whole),
            FWD_X: mk(FWD_X, me, kd, dev_d, whole, whole),
        }
        return split, direct

    is_split = [t.R > 1 for t, _, _ in parts]

    def start(ins, outs, sems):
        split, direct = plan(outs, sems)
        for ti in range(nt):
            kinds = split if is_split[ti] else direct
            kinds[TO_X](ti, "s").start()
            kinds[TO_Y](ti, "s").start()
            if not is_split[ti]:
                kinds[FWD_X](ti, "s").start()

    def mid(ins, outs, sems):
        split, _ = plan(outs, sems)
        for ti in range(nt):
            if is_split[ti]:
                split[TO_Y](ti, "r").wait_recv()
                split[FWD_X](ti, "s").start()
                split[SIB_Y](ti, "s").start()
        for ti in range(nt):
            if is_split[ti]:
                split[TO_X](ti, "r").wait_recv()
                split[FWD_Y](ti, "s").start()
                split[SIB_X](ti, "s").start()

    def finish(ins, outs, sems):
        split, direct = plan(outs, sems)
        for ti in range(nt):
            if is_split[ti]:
                split[FWD_X](ti, "r").wait_recv()
                split[FWD_Y](ti, "r").wait_recv()
                split[SIB_D](ti, "s").start()
            else:
                for kind in (TO_X, TO_Y, FWD_X):
                    direct[kind](ti, "r").wait_recv()
        for ti in range(nt):
            if is_split[ti]:
                for kind in (SIB_X, SIB_Y, SIB_D):
                    split[kind](ti, "r").wait_recv()
        for ti in range(nt):
            kinds = split if is_split[ti] else direct
            for kind in kinds:
                kinds[kind](ti, "s").wait_send()

    sems = pltpu.SemaphoreType.DMA((nt, 7))
    return Rider(list(fulls), [_sds(a.shape, a.dtype) for a in fulls], {i: i for i in range(nt)},
                 [sems, sems], start, mid, finish)


def _pair_exchange(name, specs, grads):
    nt = len(specs)

    def body(*refs):
        gs = refs[:nt]
        lands = refs[nt:2 * nt]
        send, recv = refs[2 * nt:]
        x, y, c, _ = _mesh_position()
        cps = []
        for ti, t in enumerate(specs):
            half = t.R // 2
            cp = pltpu.make_async_remote_copy(gs[ti].at[:, :, pl.ds((1 - c) * half, half), :], lands[ti],
                                              send.at[ti], recv.at[ti],
                                              device_id=(x, y, 1 - c), device_id_type=MESH)
            cp.start()
            cps.append(cp)
        for cp in cps:
            cp.wait()

    return pl.pallas_call(
        body, name=name,
        in_specs=[ANYSPEC] * nt, out_specs=[ANYSPEC] * nt,
        out_shape=[_sds(_full_shape(t, t.R // 2), BF) for t in specs],
        scratch_shapes=[pltpu.SemaphoreType.DMA((nt,)), pltpu.SemaphoreType.DMA((nt,))],
        compiler_params=_params(),
    )(*grads)


def _pair_sum(t, g, land, kc):
    half = t.R // 2
    nj = half // t.rb
    w = _ncb(t) * t.C

    def body(kc_ref, g_ref, l_ref, o_ref):
        del kc_ref
        o_ref[...] = (g_ref[...].astype(F32) + l_ref[...].astype(F32)).astype(BF)

    return pl.pallas_call(
        body, name=f"pair_sum_{t.name}",
        grid_spec=pltpu.PrefetchScalarGridSpec(
            num_scalar_prefetch=1, grid=(t.L, nj),
            in_specs=[pl.BlockSpec((None, t.A, t.rb, w), lambda l, j, kc_ref: (l, 0, kc_ref[1] * nj + j, 0)),
                      pl.BlockSpec((None, t.A, t.rb, w), lambda l, j, kc_ref: (l, 0, j, 0))],
            out_specs=pl.BlockSpec((None, t.A, t.rb, w), lambda l, j, kc_ref: (l, 0, j, 0))),
        out_shape=_sds(_full_shape(t, half), BF),
        compiler_params=_params(2),
    )(kc, g, land)


def _scatter_rider(specs, sums):
    nt = len(specs)

    def copy(ins, outs, sems, ti, j, chip, c):
        t = specs[ti]
        cx, cy = chip
        return pltpu.make_async_remote_copy(_slot(ins[ti], t, 2 * cx + cy, 0, t.R // 2), outs[ti].at[j],
                                            sems[0].at[ti, j], sems[1].at[ti, j],
                                            device_id=(cx, cy, c), device_id_type=MESH)

    def start(ins, outs, sems):
        _, _, c, chips = _mesh_position()
        for j, chip in enumerate(chips):
            for ti in range(nt):
                copy(ins, outs, sems, ti, j, chip, c).start()

    def finish(ins, outs, sems):
        _, _, c, chips = _mesh_position()
        for j, chip in enumerate(chips):
            for ti in range(nt):
                copy(ins, outs, sems, ti, j, chip, c).wait()

    sems = pltpu.SemaphoreType.DMA((nt, N_CHIPS - 1))
    return Rider(list(sums), [_sds((N_CHIPS - 1, t.L, t.R // 2, t.C), BF) for t in specs], {}, [sems, sems],
                 start, None, finish)


def _chip_sum(t, s, land, kc, n_layers, prev):
    half = t.R // 2
    nj = half // t.rb

    def body(*refs):
        s_ref, l_ref, o_ref = refs[1], refs[2], refs[-1]
        acc = s_ref[...].astype(F32)
        for j in range(N_CHIPS - 1):
            acc = acc + l_ref[j].astype(F32)
        o_ref[...] = acc

    def own_map(l, j, kc_ref):
        a, cb = _slot_index(t, kc_ref[0])
        return (l, a, j, cb)

    def out_map(l, j, kc_ref):
        return (l if t.layer is None else t.layer, kc_ref[1] * nj + j, 0)

    in_specs = [pl.BlockSpec((None, None, t.rb, t.C), own_map),
                pl.BlockSpec((N_CHIPS - 1, None, t.rb, t.C), lambda l, j, kc_ref: (0, l, j, 0))]
    args = [kc, s, land]
    aliases = {}
    if prev is not None:
        in_specs.append(ANYSPEC)
        args.append(prev)
        aliases = {3: 0}
    return pl.pallas_call(
        body, name=f"chip_sum_{t.name}",
        grid_spec=pltpu.PrefetchScalarGridSpec(
            num_scalar_prefetch=1, grid=(t.L, nj), in_specs=in_specs,
            out_specs=pl.BlockSpec((None, t.rb, t.C), out_map)),
        out_shape=_sds((n_layers, t.R, t.C)),
        input_output_aliases=aliases,
        compiler_params=_params(2),
    )(*args)


def _chip_sum_fused(t, own, land, kc, n_layers, prev, by_cols):
    if by_cols:
        rows, cols = own.shape
    else:
        nb, rows, bw = own.shape
        cols = nb * bw
    nj = rows // t.rb

    def body(*refs):
        o_ref, l_ref, out_ref = refs[1], refs[2], refs[-1]
        acc = o_ref[...].astype(F32)
        for j in range(N_CHIPS - 1):
            acc = acc + l_ref[j].astype(F32)
        out_ref[...] = acc if by_cols else jnp.concatenate([acc[b] for b in range(nb)], axis=1)

    def out_map(j, kc_ref):
        return (t.layer, j, kc_ref[1]) if by_cols else (t.layer, kc_ref[1] * nj + j, 0)

    if by_cols:
        in_specs = [pl.BlockSpec((t.rb, cols), lambda j, kc_ref: (j, 0)),
                    pl.BlockSpec((N_CHIPS - 1, t.rb, cols), lambda j, kc_ref: (0, j, 0))]
    else:
        in_specs = [pl.BlockSpec((nb, t.rb, bw), lambda j, kc_ref: (0, j, 0)),
                    pl.BlockSpec((N_CHIPS - 1, nb, t.rb, bw), lambda j, kc_ref: (0, 0, j, 0))]
    args = [kc, own, land]
    aliases = {}
    if prev is not None:
        in_specs.append(ANYSPEC)
        args.append(prev)
        aliases = {3: 0}
    return pl.pallas_call(
        body, name=f"chip_sum_{t.name}",
        grid_spec=pltpu.PrefetchScalarGridSpec(
            num_scalar_prefetch=1, grid=(nj,), in_specs=in_specs,
            out_specs=pl.BlockSpec((None, t.rb, cols), out_map)),
        out_shape=_sds((n_layers, t.R, t.C)),
        input_output_aliases=aliases,
        compiler_params=_params(1),
    )(*args)


def _pair_share(halves, by_cols):
    nt = len(halves)

    def part(ref, ti, core):
        axis = 2 if by_cols[ti] else 1
        half = halves[ti].shape[axis] // 2
        piece = pl.ds(pl.multiple_of(core * half, 128 if by_cols[ti] else 8), half)
        return ref.at[:, :, piece] if by_cols[ti] else ref.at[:, piece, :]

    def body(*refs):
        outs = refs[nt:2 * nt]
        send, recv = refs[2 * nt:]
        x, y, c, _ = _mesh_position()
        cps = []
        for ti in range(nt):
            mine = part(outs[ti], ti, c)
            cp = pltpu.make_async_remote_copy(mine, mine, send.at[ti], recv.at[ti],
                                              device_id=(x, y, 1 - c), device_id_type=MESH)
            cp.start()
            cps.append(cp)
        for ti in range(nt):
            theirs = part(outs[ti], ti, 1 - c)
            pltpu.make_async_remote_copy(theirs, theirs, send.at[ti], recv.at[ti],
                                         device_id=(x, y, 1 - c), device_id_type=MESH).wait_recv()
        for cp in cps:
            cp.wait_send()

    return pl.pallas_call(
        body, name="grads_pair_share",
        in_specs=[ANYSPEC] * nt, out_specs=[ANYSPEC] * nt,
        out_shape=[_sds(a.shape, a.dtype) for a in halves],
        scratch_shapes=[pltpu.SemaphoreType.DMA((nt,)), pltpu.SemaphoreType.DMA((nt,))],
        input_output_aliases={i: i for i in range(nt)},
        compiler_params=_params(),
    )(*halves)


def _adamw_math(w, g, m, v):
    m = B1 * m + (1.0 - B1) * g
    v = B2 * v + (1.0 - B2) * (g * g)
    delta = -LR * ((m / BC1) / (jnp.sqrt(v / BC2) + AEPS) + WD * w)
    return delta, m, v


def _adamw(name, rb, w, g, m, v):
    n_layers, r, c = w.shape

    def body(w_ref, g_ref, m_ref, v_ref, go_ref, d_ref, nm_ref, nv_ref):
        g_v = g_ref[...]
        go_ref[...] = g_v
        d_ref[...], nm_ref[...], nv_ref[...] = _adamw_math(w_ref[...], g_v, m_ref[...], v_ref[...])

    spec = pl.BlockSpec((None, rb, c), lambda l, j: (l, j, 0))
    return pl.pallas_call(
        body, name=f"adamw_{name}", grid=(n_layers, r // rb),
        in_specs=[spec] * 4, out_specs=[spec] * 4, out_shape=[_sds(w.shape)] * 4,
        compiler_params=_params(2),
    )(w, g, m, v)


GAIN_ROWS = {"pre_mix_g": 0, "post_mix_g": 2, "pre_ffn_g": 4, "post_ffn_g": 6, "ple_g": 8, "ple_post_g": 10}
ROW_KV_G, ROW_POOL_SCALE, ROW_SINKS, ROW_LOSS, PACK_ROWS = 12, 13, 14, 15, 16
SMALL_NAMES = tuple(GAIN_ROWS) + ("kv_g", "pool_scale", "sinks")


def _small_all_reduce(rows, dpool):
    ng, pr = len(WINDOWS), POOL_G // N_CHIPS

    def body(*refs):
        row_refs = refs[:PACK_ROWS]
        dpool_ref, tot_ref, gpool_ref, pack, land, pland, send, recv, psend, precv = refs[PACK_ROWS:]
        x, y, c, _ = _mesh_position()
        me = 4 * x + 2 * y + c
        for r in range(PACK_ROWS):
            pack[r:r + 1, :] = row_refs[r][...]

        def shard_of(k):
            return dpool_ref.at[:, pl.ds(pl.multiple_of(k * pr, pr), pr), :]

        cps = []
        for j in range(1, N_DEV):
            px, py, pc = x ^ (j >> 2), y ^ ((j >> 1) & 1), c ^ (j & 1)
            cps.append(pltpu.make_async_remote_copy(pack, land.at[me], send.at[j], recv.at[j],
                                                    device_id=(px, py, pc), device_id_type=MESH))
            cps.append(pltpu.make_async_remote_copy(shard_of(2 * px + py), pland.at[me], psend.at[j], precv.at[j],
                                                    device_id=(px, py, pc), device_id_type=MESH))
        for cp in cps:
            cp.start()
        land[me] = pack[...]
        pland[me] = dpool_ref[:, pl.ds(pl.multiple_of((2 * x + y) * pr, pr), pr), :]
        for j in range(1, N_DEV):
            pltpu.make_async_remote_copy(pack, land.at[me ^ j], send.at[j], recv.at[j],
                                         device_id=(x, y, c), device_id_type=MESH).wait_recv()
            pltpu.make_async_remote_copy(shard_of(0), pland.at[me ^ j], psend.at[j], precv.at[j],
                                         device_id=(x, y, c), device_id_type=MESH).wait_recv()
        for cp in cps:
            cp.wait_send()
        tot = land[0]
        gp = pland[0].astype(F32)
        for d in range(1, N_DEV):
            tot = tot + land[d]
            gp = gp + pland[d].astype(F32)
        tot_ref[...] = tot
        gpool_ref[...] = gp

    sems = pltpu.SemaphoreType.DMA((N_DEV,))
    return pl.pallas_call(
        body, name="small_all_reduce",
        in_specs=[VSPEC] * (PACK_ROWS + 1), out_specs=[VSPEC, VSPEC],
        out_shape=[_sds((PACK_ROWS, D)), _sds((ng, pr, POOL_G))],
        scratch_shapes=[pltpu.VMEM((PACK_ROWS, D), F32), pltpu.VMEM((N_DEV, PACK_ROWS, D), F32),
                        pltpu.VMEM((N_DEV, ng, pr, POOL_G), BF), sems, sems, sems, sems],
        compiler_params=_params(),
    )(*rows, dpool)


def _small_adamw(tot, kc, small_w, small_m, small_v):
    names = SMALL_NAMES
    n = len(names)

    def body(*refs):
        tot_ref, kc_ref = refs[0], refs[1]
        w_refs = dict(zip(names, refs[2:2 + n]))
        m_refs = dict(zip(names, refs[2 + n:2 + 2 * n]))
        v_refs = dict(zip(names, refs[2 + 2 * n:2 + 3 * n]))
        loss_ref = refs[2 + 3 * n]
        out_refs = {nm: refs[3 + 3 * n + 4 * k: 7 + 3 * n + 4 * k] for k, nm in enumerate(names)}
        tot = tot_ref[...]
        loss_ref[...] = 0.5 * jnp.sum(tot[ROW_LOSS:ROW_LOSS + 1, :], axis=-1, keepdims=True) * (1.0 / D)

        def update(nm, g):
            g_ref, d_ref, nm_ref, nv_ref = out_refs[nm]
            g_ref[...] = g
            d_ref[...], nm_ref[...], nv_ref[...] = _adamw_math(w_refs[nm][...], g, m_refs[nm][...], v_refs[nm][...])

        for nm, r in GAIN_ROWS.items():
            update(nm, tot[r:r + 2, :])
        update("kv_g", tot[ROW_KV_G:ROW_KV_G + 1, :])
        k = kc_ref[0]
        width = D // N_CHIPS
        g_scale = jnp.zeros((1, width), F32)
        for kk in range(N_CHIPS):
            g_scale = g_scale + jnp.where(k == kk, tot[ROW_POOL_SCALE:ROW_POOL_SCALE + 1, kk * width:(kk + 1) * width], 0.0)
        update("pool_scale", g_scale)
        update("sinks", tot[ROW_SINKS:ROW_SINKS + 1, 0:N_HEADS])

    ins = [tot, kc] + [small_w[nm] for nm in names] + [small_m[nm] for nm in names] + [small_v[nm] for nm in names]
    out_shape = [_sds((1, 1))]
    for nm in names:
        out_shape += [_sds(small_w[nm].shape)] * 4
    outs = pl.pallas_call(
        body, name="small_adamw",
        in_specs=[VSPEC, SSPEC] + [VSPEC] * (3 * n), out_specs=[VSPEC] * len(out_shape), out_shape=out_shape,
        compiler_params=_params(),
    )(*ins)
    return outs[0], {nm: outs[1 + 4 * k: 5 + 4 * k] for k, nm in enumerate(names)}


def _compute_layout(t, full):
    if t.src == "w_gu":
        return full.reshape(2, D, FF)
    if t.src == "pool_w":
        return full.reshape(len(WINDOWS), POOL_G, POOL_G)
    if t.src == "pool_scale":
        return full.reshape(1, D)
    return full.reshape(t.A * t.R, _ncb(t) * t.C)


def kernel(x, p, pre_mix_g, post_mix_g, pre_ffn_g, post_ffn_g, pool_w, pool_scale, kv_g, w_kv, w_q, sinks, w_o, w_gu, w_down, ple_g, w_ple_gate, w_ple_proj, ple_post_g, loss_target, m_pre_mix_g, m_post_mix_g, m_pre_ffn_g, m_post_ffn_g, m_pool_w, m_pool_scale, m_kv_g, m_w_kv, m_w_q, m_sinks, m_w_o, m_w_gu, m_w_down, m_ple_g, m_w_ple_gate, m_w_ple_proj, m_ple_post_g, v_pre_mix_g, v_post_mix_g, v_pre_ffn_g, v_post_ffn_g, v_pool_w, v_pool_scale, v_kv_g, v_w_kv, v_w_q, v_sinks, v_w_o, v_w_gu, v_w_down, v_ple_g, v_w_ple_gate, v_w_ple_proj, v_ple_post_g):
    weights = dict(pre_mix_g=pre_mix_g, post_mix_g=post_mix_g, pre_ffn_g=pre_ffn_g, post_ffn_g=post_ffn_g,
                   pool_w=pool_w, pool_scale=pool_scale, kv_g=kv_g, w_kv=w_kv, w_q=w_q, sinks=sinks, w_o=w_o,
                   w_gu=w_gu, w_down=w_down, ple_g=ple_g, w_ple_gate=w_ple_gate, w_ple_proj=w_ple_proj,
                   ple_post_g=ple_post_g)
    m_in = dict(pre_mix_g=m_pre_mix_g, post_mix_g=m_post_mix_g, pre_ffn_g=m_pre_ffn_g, post_ffn_g=m_post_ffn_g,
                pool_w=m_pool_w, pool_scale=m_pool_scale, kv_g=m_kv_g, w_kv=m_w_kv, w_q=m_w_q, sinks=m_sinks,
                w_o=m_w_o, w_gu=m_w_gu, w_down=m_w_down, ple_g=m_ple_g, w_ple_gate=m_w_ple_gate,
                w_ple_proj=m_w_ple_proj, ple_post_g=m_ple_post_g)
    v_in = dict(pre_mix_g=v_pre_mix_g, post_mix_g=v_post_mix_g, pre_ffn_g=v_pre_ffn_g, post_ffn_g=v_post_ffn_g,
                pool_w=v_pool_w, pool_scale=v_pool_scale, kv_g=v_kv_g, w_kv=v_w_kv, w_q=v_w_q, sinks=v_sinks,
                w_o=v_w_o, w_gu=v_w_gu, w_down=v_w_down, ple_g=v_ple_g, w_ple_gate=v_w_ple_gate,
                w_ple_proj=v_w_ple_proj, ple_post_g=v_ple_post_g)
    order = ["pre_mix_g", "post_mix_g", "pre_ffn_g", "post_ffn_g", "pool_w", "pool_scale", "kv_g", "w_kv", "w_q",
             "sinks", "w_o", "w_gu", "w_down", "ple_g", "w_ple_gate", "w_ple_proj", "ple_post_g"]

    kc = jnp.stack([2 * lax.axis_index("x") + lax.axis_index("y"), lax.axis_index("c")]).astype(jnp.int32)
    s_len = x.shape[1]
    x2d = x.reshape(s_len, D)
    p3d = p.reshape(2, s_len, PLE)
    target = loss_target.reshape(s_len, D)
    kv_g2d = kv_g.reshape(1, D)
    gains = {nm: weights[nm] for nm in GAIN_ROWS}

    def shard_view(src, a):
        t = next(t for t in BIGS.values() if t.src == src)
        return a.reshape(-1, t.R, t.C)

    placed = {nm: _place(t, shard_view(t.src, weights[t.src]), kc, BF) for nm, t in BIGS.items()}
    placed["pool_scale"] = _place(POOL_SCALE, pool_scale.reshape(1, 1, D // N_CHIPS), kc, F32)
    specs = dict(BIGS, pool_scale=POOL_SCALE)

    def gather(names, rows=None):
        rows = rows or {}
        parts = [(specs[nm],) + tuple(rows.get(nm, (0, specs[nm].R))) for nm in names]
        return _gather_rider(parts, [placed[nm] for nm in names])

    def take(names, results):
        for nm, a in zip(names, results):
            placed[nm] = a

    def weight(nm):
        return _compute_layout(specs[nm], placed[nm])

    first = ["pool_w", "pool_scale", "w_gu0", "w_down0"]
    take(first, _run("weights_gather_first", gather(first)))

    y0, x1 = _mixa_fwd(x2d, gains["pre_mix_g"], weight("pool_w"), weight("pool_scale"), gains["post_mix_g"])

    ride = ["w_ple_gate0", "w_ple_proj0", "w_q", "w_kv", "w_o", "w_gu1"]
    (f0, x2), got = _ffn_fwd(0, x1, gains["pre_ffn_g"], weight("w_gu0"), weight("w_down0"), gains["post_ffn_g"],
                             rider=gather(ride, {"w_gu1": (0, 320)}))
    take(ride, got)

    ride = ["w_ple_gate1", "w_ple_proj1", "w_gu1"]
    (z0, pe0, x3), got = _ple_fwd(0, x2, p3d, gains["ple_g"], weight("w_ple_gate0"), weight("w_ple_proj0"),
                                  gains["ple_post_g"], rider=gather(ride, {"w_gu1": (320, 448)}))
    take(ride, got)

    ride = ["w_gu1"]
    (q, kv), got = _qkv_fwd(x3, gains["pre_mix_g"], kv_g2d, weight("w_q"), weight("w_kv"),
                            rider=gather(ride, {"w_gu1": (448, 704)}))
    take(ride, got)

    ride = ["w_down1", "w_gu1"]
    (attn, y1, x4), got = _attn_fwd(q, kv, sinks, x3, weight("w_o"), gains["post_mix_g"],
                                    rider=gather(ride, {"w_gu1": (704, D)}))
    take(ride, got)

    (f1, x5), _ = _ffn_fwd(1, x4, gains["pre_ffn_g"], weight("w_gu1"), weight("w_down1"), gains["post_ffn_g"])
    (z1, pe1, dx6, loss_row), _ = _ple_fwd(1, x5, p3d, gains["ple_g"], weight("w_ple_gate1"), weight("w_ple_proj1"),
                                           gains["ple_post_g"], target=target)

    local = {}
    landed = {}
    fused = {}

    def pair_stage(tag, names):
        ts = [BIGS[nm] for nm in names]
        gs = [local[nm].reshape(_full_shape(t)) for nm, t in zip(names, ts)]
        lands = _pair_exchange(f"grads_pair_exchange_{tag}", ts, gs)
        return [_pair_sum(t, g, l, kc) for t, g, l in zip(ts, gs, lands)]

    def scatter(names, sums):
        return _scatter_rider([BIGS[nm] for nm in names], sums)

    def keep(names, sums, got):
        for nm, s, l in zip(names, sums, got):
            landed[nm] = (s, l)

    (dx5, local["w_ple_gate1"], local["w_ple_proj1"], d_ple1, d_plepost1), _ = _ple_bwd(
        1, dx6, x5, z1, pe1, p3d, gains["ple_g"], weight("w_ple_gate1"), gains["ple_post_g"])

    group_a = ["w_ple_gate1", "w_ple_proj1"]
    sums_a = pair_stage("a", group_a)
    (dx4, d_preffn1, d_postffn1, *scattered), _ = _ffn_bwd(
        1, dx5, x4, f1, gains["pre_ffn_g"], weight("w_gu1"), weight("w_down1"), gains["post_ffn_g"], kc)
    fused["w_gu1"], fused["w_down1"] = scattered[0:2], scattered[2:4]

    (dq, dkv, local["w_o"], d_postmix1, d_sinks), got = _attn_bwd(
        dx4, y1, attn, q, kv, sinks, weight("w_o"), gains["post_mix_g"], rider=scatter(group_a, sums_a))
    keep(group_a, sums_a, got)
    dx3, local["w_q"], local["w_kv"], d_premix1, d_kvg = _qkv_bwd(
        dq, dkv, x3, dx4, gains["pre_mix_g"], kv_g2d, weight("w_q"), weight("w_kv"))

    group_b = ["w_o", "w_q", "w_kv"]
    sums_b = pair_stage("b", group_b)
    (dx2, local["w_ple_gate0"], local["w_ple_proj0"], d_ple0, d_plepost0), got = _ple_bwd(
        0, dx3, x2, z0, pe0, p3d, gains["ple_g"], weight("w_ple_gate0"), gains["ple_post_g"],
        rider=scatter(group_b, sums_b))
    keep(group_b, sums_b, got)

    group_c = ["w_ple_gate0", "w_ple_proj0"]
    sums_c = pair_stage("c", group_c)
    (dx1, d_preffn0, d_postffn0, *scattered), _ = _ffn_bwd(
        0, dx2, x1, f0, gains["pre_ffn_g"], weight("w_gu0"), weight("w_down0"), gains["post_ffn_g"], kc)
    fused["w_gu0"], fused["w_down0"] = scattered[0:2], scattered[2:4]

    (dx0, d_pool, d_scale, d_postmix0, d_premix0), got = _mixa_bwd(
        dx1, x2d, y0, gains["pre_mix_g"], weight("pool_w"), weight("pool_scale"), gains["post_mix_g"],
        rider=scatter(group_c, sums_c))
    keep(group_c, sums_c, got)

    rows = [d_premix0, d_premix1, d_postmix0, d_postmix1, d_preffn0, d_preffn1, d_postffn0, d_postffn1,
            d_ple0, d_ple1, d_plepost0, d_plepost1, d_kvg, d_scale, d_sinks, loss_row]
    as2d = lambda a: a.reshape(1, D) if a.ndim == 1 else a
    tot, g_pool = _small_all_reduce(rows, d_pool)
    loss, small = _small_adamw(tot, kc, {nm: as2d(weights[nm]) for nm in SMALL_NAMES},
                               {nm: as2d(m_in[nm]) for nm in SMALL_NAMES},
                               {nm: as2d(v_in[nm]) for nm in SMALL_NAMES})

    shared = [src for src in BIG_SOURCES if src != "pool_w"]
    halves = []
    for src in shared:
        n_layers = shard_view(src, weights[src]).shape[0]
        acc = None
        for t in [t for t in BIGS.values() if t.src == src]:
            if t.name in fused:
                own, land = fused[t.name]
                acc = _chip_sum_fused(t, own, land, kc, n_layers, acc, by_cols=src == "w_down")
            else:
                s, l = landed[t.name]
                acc = _chip_sum(t, s, l, kc, n_layers, acc)
        halves.append(acc)
    full_grads = dict(zip(shared, _pair_share(halves, [src == "w_down" for src in shared])))
    full_grads["pool_w"] = g_pool

    out = {"grad": {}, "delta": {}, "new_m": {}, "new_v": {}}
    for src in BIG_SOURCES:
        g = full_grads[src]
        t = next(t for t in BIGS.values() if t.src == src)
        res = _adamw(src, t.rb, shard_view(src, weights[src]), g, shard_view(src, m_in[src]), shard_view(src, v_in[src]))
        shape = weights[src].shape
        for kind, a in zip(("grad", "delta", "new_m", "new_v"), res):
            out[kind][src] = a.reshape(shape)
    for nm in SMALL_NAMES:
        shape = weights[nm].shape
        for kind, a in zip(("grad", "delta", "new_m", "new_v"), small[nm]):
            out[kind][nm] = a.reshape(shape)

    return (loss.reshape(()), dx0.reshape(x.shape),
            *[out["grad"][nm] for nm in order], *[out["delta"][nm] for nm in order],
            *[out["new_m"][nm] for nm in order], *[out["new_v"][nm] for nm in order])
```

```python
import collections

import jax
import jax.numpy as jnp
from jax import lax
from jax.experimental import pallas as pl
from jax.experimental.pallas import tpu as pltpu

D = 1024
FF = 2816
N_HEADS = 16
HEAD_DIM = 64
N_KV_HEADS = 4
GQA = N_HEADS // N_KV_HEADS
KVD = N_KV_HEADS * HEAD_DIM
PLE = 256
BLK = 128
WINDOWS = (2, 4, 8, 16)
POOL_G = 256
HALO = 16
EPS = 1e-6
NEG_INF = -1e30
ATT_SCALE = HEAD_DIM ** -0.5
SLOPES = tuple(2.0 ** (-8.0 * (h + 1) / N_HEADS) for h in range(N_HEADS))
N_CHIPS = 4
N_DEV = 8

LR, B1, B2, AEPS, WD, STEP = 0.001, 0.9, 0.999, 1e-08, 0.01, 10
BC1 = 1.0 - B1 ** STEP
BC2 = 1.0 - B2 ** STEP

BF = jnp.bfloat16
F32 = jnp.float32
MESH = pl.DeviceIdType.MESH
VMEM_LIMIT_V7X = 58 * 1024 * 1024
TM = 256
TM_FFN_BWD = 512
FF_CHUNK = 256
FF_HALF = FF // 2

VSPEC = pl.BlockSpec(memory_space=pltpu.VMEM)
SSPEC = pl.BlockSpec(memory_space=pltpu.SMEM)
ANYSPEC = pl.BlockSpec(memory_space=pl.ANY)


def _params(n_grid=0):
    sem = ("arbitrary",) * n_grid if n_grid else None
    return pltpu.CompilerParams(dimension_semantics=sem, vmem_limit_bytes=VMEM_LIMIT_V7X)


def _sds(shape, dtype=F32):
    return jax.ShapeDtypeStruct(tuple(shape), dtype)


Rider = collections.namedtuple("Rider", "arrays out_shapes aliases scratch start mid finish")
MID_NUM, MID_DEN = 5, 8


def _call(body, *, name, grid, in_specs, out_specs, out_shape, args, scratch_shapes=(), rider=None, prefetch=None):
    ni, no, ns = len(in_specs), len(out_specs), len(scratch_shapes)
    npre = 0 if prefetch is None else 1
    pre = [] if prefetch is None else [prefetch]
    if rider is None:
        rider = Rider([], [], {}, [], None, None, None)
    ri, ro = len(rider.arrays), len(rider.out_shapes)

    def full(*refs):
        pre_refs, refs = refs[:npre], refs[npre:]
        ins, refs = refs[:ni], refs[ni:]
        rins, refs = refs[:ri], refs[ri:]
        outs, refs = refs[:no], refs[no:]
        routs, refs = refs[:ro], refs[ro:]
        scr, rscr = refs[:ns], refs[ns:]
        ids = [pl.program_id(a) for a in range(len(grid))]
        first = ids[0] == 0
        last = ids[0] == grid[0] - 1
        for a in range(1, len(grid)):
            first = first & (ids[a] == 0)
            last = last & (ids[a] == grid[a] - 1)

        if rider.start is not None:
            @pl.when(first)
            def _():
                rider.start(rins, routs, rscr)

        if rider.mid is not None:
            assert len(grid) == 1

            @pl.when(ids[0] == (grid[0] * MID_NUM) // MID_DEN)
            def _():
                rider.mid(rins, routs, rscr)

        body(*pre_refs, *ins, *outs, *scr)

        if rider.finish is not None:
            @pl.when(last)
            def _():
                rider.finish(rins, routs, rscr)

    outs = pl.pallas_call(
        full, name=name,
        grid_spec=pltpu.PrefetchScalarGridSpec(
            num_scalar_prefetch=npre, grid=grid,
            in_specs=list(in_specs) + [ANYSPEC] * ri, out_specs=list(out_specs) + [ANYSPEC] * ro,
            scratch_shapes=list(scratch_shapes) + list(rider.scratch)),
        out_shape=list(out_shape) + list(rider.out_shapes),
        input_output_aliases={npre + ni + a: no + b for a, b in rider.aliases.items()},
        compiler_params=_params(len(grid)))(*pre, *args, *rider.arrays)
    return list(outs[:no]), list(outs[no:])


def _run(name, rider):
    ri = len(rider.arrays)

    def body(*refs):
        rins, routs, rscr = refs[:ri], refs[ri:ri + len(rider.out_shapes)], refs[ri + len(rider.out_shapes):]
        rider.start(rins, routs, rscr)
        if rider.mid is not None:
            rider.mid(rins, routs, rscr)
        rider.finish(rins, routs, rscr)

    return pl.pallas_call(
        body, name=name, in_specs=[ANYSPEC] * ri, out_specs=[ANYSPEC] * len(rider.out_shapes),
        out_shape=list(rider.out_shapes), scratch_shapes=list(rider.scratch),
        input_output_aliases=dict(rider.aliases), compiler_params=_params())(*rider.arrays)


def _rms_fwd(x, g):
    r = lax.rsqrt(jnp.mean(x * x, axis=-1, keepdims=True) + EPS)
    return x * r * g


def _rms_bwd(x, g, dy):
    r = lax.rsqrt(jnp.mean(x * x, axis=-1, keepdims=True) + EPS)
    xn = x * r
    dxn = dy * g
    dx = r * (dxn - xn * jnp.mean(dxn * xn, axis=-1, keepdims=True))
    return dx, dy * xn


def _rowsum(a):
    return jnp.sum(a, axis=0, keepdims=True)


def _sigmoid(z):
    return 1.0 / (1.0 + jnp.exp(-z))


def _dot(a, b):
    return jnp.dot(a, b, preferred_element_type=F32)


def _dot_nt(a, b):
    return lax.dot_general(a, b, (((1,), (1,)), ((), ())), preferred_element_type=F32)


def _dot_tn(a, b):
    return lax.dot_general(a, b, (((0,), (0,)), ((), ())), preferred_element_type=F32)


def _row_spec(tm, width=D):
    return pl.BlockSpec((tm, width), lambda i: (i, 0))


def _const_spec(shape):
    zeros = (0,) * len(shape)
    return pl.BlockSpec(tuple(shape), lambda *_: zeros)


def _pool_delta(he, pos):
    out = []
    for gi, w in enumerate(WINDOWS):
        hg = he[:, gi * POOL_G:(gi + 1) * POOL_G]
        s = hg
        k = 1
        while k < w:
            s = s + pltpu.roll(s, k, 0)
            k *= 2
        cnt = jnp.maximum(jnp.minimum(pos + 1, w), 1).astype(F32)
        out.append(s / cnt - hg)
    return out


def _load_with_halo_before(x_ref, i, tm):
    r0 = pl.multiple_of(i * tm, tm)
    hs = pl.multiple_of(jnp.maximum(i * tm - HALO, 0), 8)
    xh = jnp.where(i > 0, x_ref[pl.ds(hs, HALO), :], 0.0)
    xt = x_ref[pl.ds(r0, tm), :]
    return xt, jnp.concatenate([xh, xt], axis=0)


def _mixa_fwd(x, pre_g, pool_w, pool_scale, post_g):
    s_len = x.shape[0]
    n = s_len // TM

    def body(x_ref, pg_ref, w_ref, sc_ref, qg_ref, y_ref, x1_ref):
        i = pl.program_id(0)
        xt, xe = _load_with_halo_before(x_ref, i, TM)
        he = _rms_fwd(xe, pg_ref[0:1, :])
        pos = i * TM - HALO + lax.broadcasted_iota(jnp.int32, (TM + HALO, 1), 0)
        ds = _pool_delta(he, pos)
        ys = [_dot(ds[gi][HALO:, :].astype(BF), w_ref[gi]) for gi in range(len(WINDOWS))]
        y = jnp.concatenate(ys, axis=1) * sc_ref[...]
        y_ref[...] = y
        x1_ref[...] = xt + _rms_fwd(y, qg_ref[0:1, :])

    outs, _ = _call(body, name="mixa_fwd", grid=(n,),
                    in_specs=[VSPEC] * 5, out_specs=[_row_spec(TM), _row_spec(TM)],
                    out_shape=[_sds((s_len, D)), _sds((s_len, D))],
                    args=[x, pre_g, pool_w, pool_scale, post_g])
    return outs


def _mixa_bwd(dx1, x, y, pre_g, pool_w, pool_scale, post_g, rider=None):
    s_len = x.shape[0]
    n = s_len // TM
    ng = len(WINDOWS)

    def body(dx_ref, x_ref, y_ref, pg_ref, w_ref, sc_ref, qg_ref,
             dx0_ref, dw_ref, dsc_ref, dqg_ref, dpg_ref, wacc):
        i = pl.program_id(0)

        @pl.when(i == 0)
        def _():
            wacc[...] = jnp.zeros_like(wacc)
            dsc_ref[...] = jnp.zeros_like(dsc_ref)
            dqg_ref[...] = jnp.zeros_like(dqg_ref)
            dpg_ref[...] = jnp.zeros_like(dpg_ref)

        r0 = pl.multiple_of(i * TM, TM)
        xt, xe = _load_with_halo_before(x_ref, i, TM)
        he = _rms_fwd(xe, pg_ref[0:1, :])
        pos_b = i * TM - HALO + lax.broadcasted_iota(jnp.int32, (TM + HALO, 1), 0)
        ds = _pool_delta(he, pos_b)

        last = i == n - 1
        a0 = pl.multiple_of(jnp.minimum(i * TM + TM, s_len - HALO), 8)
        ye = jnp.concatenate([y_ref[pl.ds(r0, TM), :], y_ref[pl.ds(a0, HALO), :]], axis=0)
        dt = dx_ref[pl.ds(r0, TM), :]
        de = jnp.concatenate([dt, jnp.where(last, 0.0, dx_ref[pl.ds(a0, HALO), :])], axis=0)
        dye, prod = _rms_bwd(ye, qg_ref[0:1, :], de)
        dqg_ref[...] += _rowsum(prod[:TM, :])
        dys = dye * sc_ref[...]
        pos_a = i * TM + lax.broadcasted_iota(jnp.int32, (TM + HALO, 1), 0)

        dhs, dscs = [], []
        for gi, w in enumerate(WINDOWS):
            sl = slice(gi * POOL_G, (gi + 1) * POOL_G)
            wg = w_ref[gi]
            dys_g = dys[:, sl].astype(BF)
            d_g = ds[gi][HALO:, :].astype(BF)
            ypre = _dot(d_g, wg)
            dscs.append(_rowsum(dye[:TM, sl] * ypre))
            wacc[gi] += _dot_tn(d_g, dys_g[:TM, :])
            dd = _dot_nt(dys_g, wg)
            cnt = jnp.minimum(pos_a + 1, w).astype(F32)
            a = dd / cnt
            k = 1
            while k < w:
                a = a + pltpu.roll(a, TM + HALO - k, 0)
                k *= 2
            dhs.append(a[:TM, :] - dd[:TM, :])
        dsc_ref[...] += jnp.concatenate(dscs, axis=1)
        dh = jnp.concatenate(dhs, axis=1)
        dxp, prod2 = _rms_bwd(xt, pg_ref[0:1, :], dh)
        dpg_ref[...] += _rowsum(prod2)
        dx0_ref[...] = dt + dxp

        @pl.when(last)
        def _():
            dw_ref[...] = wacc[...].astype(BF)

    return _call(
        body, name="mixa_bwd", grid=(n,), in_specs=[VSPEC] * 7,
        out_specs=[_row_spec(TM), _const_spec((ng, POOL_G, POOL_G)), _const_spec((1, D)),
                   _const_spec((1, D)), _const_spec((1, D))],
        out_shape=[_sds((s_len, D)), _sds((ng, POOL_G, POOL_G), BF), _sds((1, D)), _sds((1, D)), _sds((1, D))],
        scratch_shapes=[pltpu.VMEM((ng, POOL_G, POOL_G), F32)],
        args=[dx1, x, y, pre_g, pool_w, pool_scale, post_g], rider=rider)


def _ffn_fwd(layer, x1, pre_g, wgu, wd, post_g, rider=None):
    s_len = x1.shape[0]

    def body(x_ref, pg_ref, wgu_ref, wd_ref, qg_ref, f_ref, x2_ref):
        x = x_ref[...]
        h = _rms_fwd(x, pg_ref[layer:layer + 1, :]).astype(BF)
        f = jnp.zeros((TM, D), F32)
        for c in range(FF // FF_HALF):
            cols = slice(c * FF_HALF, (c + 1) * FF_HALF)
            g = _dot(h, wgu_ref[0, :, cols])
            u = _dot(h, wgu_ref[1, :, cols])
            act = g * _sigmoid(g) * u
            f = f + _dot(act.astype(BF), wd_ref[cols, :])
        f_ref[...] = f
        x2_ref[...] = x + _rms_fwd(f, qg_ref[layer:layer + 1, :])

    return _call(body, name=f"ffn_fwd{layer}", grid=(s_len // TM,),
                 in_specs=[_row_spec(TM), VSPEC, VSPEC, VSPEC, VSPEC],
                 out_specs=[_row_spec(TM), _row_spec(TM)],
                 out_shape=[_sds((s_len, D)), _sds((s_len, D))],
                 args=[x1, pre_g, wgu, wd, post_g], rider=rider)


GU_PIECE = 128
DN_PIECE = 64
DN_SLOT = FF // N_CHIPS
HALF_D = D // 2


def _ffn_bwd(layer, dx2, x1, f, pre_g, wgu, wd, post_g, kc, rider=None):
    s_len = x1.shape[0]
    tm = TM_FFN_BWD
    n = s_len // tm
    nc = FF // FF_CHUNK
    n_gu, n_dn = FF_CHUNK // GU_PIECE, FF_CHUNK // DN_PIECE
    n_pieces = 2 * n_gu + n_dn
    n_blk = FF_HALF // GU_PIECE

    def edge_rows(c, i, kc_ref):
        return (jnp.where((c == 0) | (c == nc - 1), i, n - 1), 0)

    def chunk_at(c, kc_ref):
        return (c + (((kc_ref[0] + 1) % N_CHIPS) * nc) // N_CHIPS) % nc

    def exchange(kc_ref, c, accg, accu, accd, own_gu_ref, land_gu_ref, own_dn_ref, land_dn_ref,
                 pl_gu, pl_dn, sib_gu, sib_dn, mine_gu, mine_dn, sum_gu, sum_dn,
                 psend, precv, ssend, lsem, rrecv):
        x, y, core = lax.axis_index("x"), lax.axis_index("y"), lax.axis_index("c")
        lower = core == 0

        def pair_copy(cc, part):
            p = cc % 2
            src, dst = ((sib_gu, pl_gu), (sib_dn, pl_dn))[part]
            return pltpu.make_async_remote_copy(src.at[p], dst.at[cc], psend.at[p, part], precv.at[cc, part],
                                                device_id=(x, y, 1 - core), device_id_type=MESH)

        def scatter(cc, wait):
            p = cc % 2
            jobs = []
            for gu in range(2):
                for hc in range(n_gu):
                    hidden = chunk_at(cc, kc_ref) * FF_CHUNK + hc * GU_PIECE
                    k = hidden // FF_HALF
                    jobs.append((sum_gu.at[p, gu, hc], k + 2 * gu, 0,
                                 own_gu_ref, land_gu_ref, ((hidden - k * FF_HALF) // GU_PIECE,)))
            for q in range(n_dn):
                hidden = chunk_at(cc, kc_ref) * FF_CHUNK + q * DN_PIECE
                k = hidden // DN_SLOT
                off = pl.multiple_of(hidden - k * DN_SLOT, DN_PIECE)
                jobs.append((sum_dn.at[p, pl.ds(q * DN_PIECE, DN_PIECE), :], k, 1,
                             own_dn_ref, land_dn_ref, (pl.ds(off, DN_PIECE), slice(None))))
            for pi, (src, k, t, own_ref, land_ref, where) in enumerate(jobs):
                kx, ky = k // 2, k % 2
                fx, fy = (kx != x).astype(jnp.int32), (ky != y).astype(jnp.int32)
                local = (fx + fy) == 0
                j = jnp.maximum(fx + 2 * fy - 1, 0)

                @pl.when(local)
                def _():
                    cp = pltpu.make_async_copy(src, own_ref.at[where], lsem.at[p, pi])
                    if wait:
                        cp.wait()
                    else:
                        cp.start()

                @pl.when(jnp.logical_not(local))
                def _():
                    cp = pltpu.make_async_remote_copy(src, land_ref.at[(j,) + where], ssend.at[p, pi],
                                                      rrecv.at[t, j], device_id=(kx, ky, core), device_id_type=MESH)
                    if wait:
                        cp.wait_send()
                    else:
                        cp.start()

        def add_and_scatter(cc):
            p = cc % 2
            pair_copy(cc, 0).wait_recv()
            pair_copy(cc, 1).wait_recv()
            s_gu = (mine_gu[...] + pl_gu[cc].astype(F32)).astype(BF)
            for hc in range(n_gu):
                sum_gu[p, :, hc] = s_gu[:, :, hc * GU_PIECE:(hc + 1) * GU_PIECE]
            sum_dn[p] = (mine_dn[...] + pl_dn[cc].astype(F32)).astype(BF)
            scatter(cc, wait=False)

        @pl.when(c >= 1)
        def _():
            @pl.when(c >= 3)
            def _():
                scatter(c - 3, wait=True)
            add_and_scatter(c - 1)

        @pl.when(c >= 2)
        def _():
            pair_copy(c - 2, 0).wait_send()
            pair_copy(c - 2, 1).wait_send()

        p = c % 2
        my_rows = pl.ds(pl.multiple_of(core * HALF_D, HALF_D), HALF_D)
        sib_rows = pl.ds(pl.multiple_of((1 - core) * HALF_D, HALF_D), HALF_D)
        d_v = accd[...]
        sib_gu[p, 0] = accg[sib_rows, :].astype(BF)
        sib_gu[p, 1] = accu[sib_rows, :].astype(BF)
        sib_dn[p] = jnp.where(lower, d_v[:, HALF_D:], d_v[:, :HALF_D]).astype(BF)
        mine_gu[0] = accg[my_rows, :]
        mine_gu[1] = accu[my_rows, :]
        mine_dn[...] = jnp.where(lower, d_v[:, :HALF_D], d_v[:, HALF_D:])
        pair_copy(c, 0).start()
        pair_copy(c, 1).start()

        @pl.when(c == nc - 1)
        def _():
            scatter(nc - 3, wait=True)
            add_and_scatter(nc - 1)
            for cc in (nc - 2, nc - 1):
                pair_copy(cc, 0).wait_send()
                pair_copy(cc, 1).wait_send()
                scatter(cc, wait=True)
            for t, land_ref in enumerate((land_gu_ref, land_dn_ref)):
                for j in range(N_CHIPS - 1):
                    pltpu.make_async_remote_copy(land_ref.at[j], land_ref.at[j], ssend.at[0, 0], rrecv.at[t, j],
                                                 device_id=(x, y, core), device_id_type=MESH).wait_recv()

    def body(kc_ref, dx_ref, x_ref, f_ref, pg_ref, wgu_ref, wd_ref, qg_ref,
             dx1_ref, dpg_ref, dqg_ref, own_gu_ref, land_gu_ref, own_dn_ref, land_dn_ref,
             h_s, df_s, dh_s, accg, accu, accd, *comm):
        c = pl.program_id(0)
        i = pl.program_id(1)
        rows = pl.ds(pl.multiple_of(i * tm, tm), tm)
        pg = pg_ref[layer:layer + 1, :]

        @pl.when((c == 0) & (i == 0))
        def _():
            dpg_ref[...] = jnp.zeros_like(dpg_ref)
            dqg_ref[...] = jnp.zeros_like(dqg_ref)

        @pl.when(c == 0)
        def _():
            h_s[rows, :] = _rms_fwd(x_ref[...], pg).astype(BF)
            df, prod = _rms_bwd(f_ref[...], qg_ref[layer:layer + 1, :], dx_ref[...])
            df_s[rows, :] = df.astype(BF)
            dqg_ref[...] += _rowsum(prod)

        @pl.when(i == 0)
        def _():
            accg[...] = jnp.zeros_like(accg)
            accu[...] = jnp.zeros_like(accu)
            accd[...] = jnp.zeros_like(accd)

        h = h_s[rows, :]
        df = df_s[rows, :]
        wg = wgu_ref[0]
        wu = wgu_ref[1]
        g = _dot(h, wg)
        u = _dot(h, wu)
        sg = _sigmoid(g)
        a = g * sg
        dact = _dot_nt(df, wd_ref[...])
        accd[...] += _dot_tn((a * u).astype(BF), df)
        du = (dact * a).astype(BF)
        dg = (dact * u * (sg * (1.0 + g * (1.0 - sg)))).astype(BF)
        accg[...] += _dot_tn(h, dg)
        accu[...] += _dot_tn(h, du)
        dh = _dot_nt(dg, wg) + _dot_nt(du, wu)

        @pl.when(c == 0)
        def _():
            dh_s[rows, :] = dh

        @pl.when((c > 0) & (c < nc - 1))
        def _():
            dh_s[rows, :] += dh

        @pl.when(c == nc - 1)
        def _():
            dxp, prod = _rms_bwd(x_ref[...], pg, dh_s[rows, :] + dh)
            dpg_ref[...] += _rowsum(prod)
            dx1_ref[...] = dx_ref[...] + dxp

        @pl.when(i == n - 1)
        def _():
            exchange(kc_ref, c, accg, accu, accd, own_gu_ref, land_gu_ref, own_dn_ref, land_dn_ref, *comm)

    dma = pltpu.SemaphoreType.DMA
    return _call(
        body, name=f"ffn_bwd{layer}", grid=(nc, n),
        in_specs=[pl.BlockSpec((tm, D), edge_rows), pl.BlockSpec((tm, D), edge_rows),
                  pl.BlockSpec((tm, D), lambda c, i, kc_ref: (jnp.where(c == 0, i, n - 1), 0),
                               pipeline_mode=pl.Buffered(1)),
                  VSPEC,
                  pl.BlockSpec((2, D, FF_CHUNK), lambda c, i, kc_ref: (0, 0, chunk_at(c, kc_ref))),
                  pl.BlockSpec((FF_CHUNK, D), lambda c, i, kc_ref: (chunk_at(c, kc_ref), 0)),
                  VSPEC],
        out_specs=[pl.BlockSpec((tm, D), lambda c, i, kc_ref: (jnp.where(c == nc - 1, i, 0), 0)),
                   _const_spec((1, D)), _const_spec((1, D)), ANYSPEC, ANYSPEC, ANYSPEC, ANYSPEC],
        out_shape=[_sds((s_len, D)), _sds((1, D)), _sds((1, D)),
                   _sds((n_blk, HALF_D, GU_PIECE), BF), _sds((N_CHIPS - 1, n_blk, HALF_D, GU_PIECE), BF),
                   _sds((DN_SLOT, HALF_D), BF), _sds((N_CHIPS - 1, DN_SLOT, HALF_D), BF)],
        scratch_shapes=[pltpu.VMEM((s_len, D), BF), pltpu.VMEM((s_len, D), BF), pltpu.VMEM((s_len, D), F32),
                        pltpu.VMEM((D, FF_CHUNK), F32), pltpu.VMEM((D, FF_CHUNK), F32),
                        pltpu.VMEM((FF_CHUNK, D), F32),
                        pltpu.VMEM((nc, 2, HALF_D, FF_CHUNK), BF), pltpu.VMEM((nc, FF_CHUNK, HALF_D), BF),
                        pltpu.VMEM((2, 2, HALF_D, FF_CHUNK), BF), pltpu.VMEM((2, FF_CHUNK, HALF_D), BF),
                        pltpu.VMEM((2, HALF_D, FF_CHUNK), F32), pltpu.VMEM((FF_CHUNK, HALF_D), F32),
                        pltpu.VMEM((2, 2, n_gu, HALF_D, GU_PIECE), BF), pltpu.VMEM((2, FF_CHUNK, HALF_D), BF),
                        dma((2, 2)), dma((nc, 2)), dma((2, n_pieces)), dma((2, n_pieces)), dma((2, N_CHIPS - 1))],
        args=[dx2, x1, f, pre_g, wgu, wd, post_g], rider=rider, prefetch=kc)


def _ple_fwd(layer, x2, p, ple_g, w_gate, w_proj, post_g, target=None, qkv=None, rider=None):
    s_len = x2.shape[0]
    final = target is not None
    assert not (final and qkv)

    def body(*refs):
        if final:
            x_ref, p_ref, g_ref, wg_ref, wp_ref, qg_ref, t_ref, z_ref, pe_ref, dx_ref, lv_ref = refs
        elif qkv:
            (x_ref, p_ref, g_ref, wg_ref, wp_ref, qg_ref, ng_ref, kg_ref, wq_ref, wkv_ref,
             z_ref, pe_ref, x3_ref, q_ref, kv_ref) = refs
        else:
            x_ref, p_ref, g_ref, wg_ref, wp_ref, qg_ref, z_ref, pe_ref, x3_ref = refs
        x = x_ref[...]
        r = _rms_fwd(x, g_ref[layer:layer + 1, :]).astype(BF)
        z = _dot(r, wg_ref[...])
        pe = _dot(p_ref[...].astype(BF), wp_ref[...])
        z_ref[...] = z
        pe_ref[...] = pe
        x3 = x + _rms_fwd(pe * _sigmoid(z), qg_ref[layer:layer + 1, :])
        if final:
            @pl.when(pl.program_id(0) == 0)
            def _():
                lv_ref[...] = jnp.zeros_like(lv_ref)
            err = x3 - t_ref[...]
            dx_ref[...] = err * (1.0 / D)
            lv_ref[...] += _rowsum(err * err)
        else:
            x3_ref[...] = x3
        if qkv:
            q_ref[...] = _dot(_rms_fwd(x3, ng_ref[layer + 1:layer + 2, :]).astype(BF), wq_ref[...]).astype(BF)
            kv_ref[...] = _dot(_rms_fwd(x3, kg_ref[...]).astype(BF), wkv_ref[...]).astype(BF)

    p_spec = pl.BlockSpec((None, TM, PLE), lambda i: (layer, i, 0))
    in_specs = [_row_spec(TM), p_spec, VSPEC, VSPEC, VSPEC, VSPEC]
    args = [x2, p, ple_g, w_gate, w_proj, post_g]
    out_specs = [_row_spec(TM), _row_spec(TM), _row_spec(TM)]
    out_shape = [_sds((s_len, D))] * 3
    if qkv:
        in_specs += [VSPEC] * 4
        args += list(qkv)
        out_specs += [_row_spec(TM), _row_spec(TM, 2 * KVD)]
        out_shape += [_sds((s_len, D), BF), _sds((s_len, 2 * KVD), BF)]
    if final:
        in_specs.append(_row_spec(TM))
        args.append(target)
        out_specs.append(_const_spec((1, D)))
        out_shape.append(_sds((1, D)))
    return _call(body, name=f"ple_fwd{layer}", grid=(s_len // TM,), in_specs=in_specs, out_specs=out_specs,
                 out_shape=out_shape, args=args, rider=rider)


def _ple_bwd(layer, dx3, x2, z, pe, p, ple_g, w_gate, post_g, rider=None):
    s_len = x2.shape[0]
    n = s_len // TM

    def body(dx_ref, x_ref, z_ref, pe_ref, p_ref, g_ref, wg_ref, qg_ref,
             dx2_ref, dwg_ref, dwp_ref, dg_ref, dqg_ref, gacc, pacc):
        i = pl.program_id(0)

        @pl.when(i == 0)
        def _():
            gacc[...] = jnp.zeros_like(gacc)
            pacc[...] = jnp.zeros_like(pacc)
            dg_ref[...] = jnp.zeros_like(dg_ref)
            dqg_ref[...] = jnp.zeros_like(dqg_ref)

        dx = dx_ref[...]
        x = x_ref[...]
        pe_v = pe_ref[...]
        gate = _sigmoid(z_ref[...])
        de, prod = _rms_bwd(pe_v * gate, qg_ref[layer:layer + 1, :], dx)
        dqg_ref[...] += _rowsum(prod)
        dpe = (de * gate).astype(BF)
        dz = (de * pe_v * gate * (1.0 - gate)).astype(BF)
        pacc[...] += _dot_tn(p_ref[...].astype(BF), dpe)
        g = g_ref[layer:layer + 1, :]
        r = _rms_fwd(x, g).astype(BF)
        gacc[...] += _dot_tn(r, dz)
        dr = _dot_nt(dz, wg_ref[...])
        dxp, prod2 = _rms_bwd(x, g, dr)
        dg_ref[...] += _rowsum(prod2)
        dx2_ref[...] = dx + dxp

        @pl.when(i == n - 1)
        def _():
            dwg_ref[...] = gacc[...].astype(BF)
            dwp_ref[...] = pacc[...].astype(BF)

    p_spec = pl.BlockSpec((None, TM, PLE), lambda i: (layer, i, 0))
    return _call(
        body, name=f"ple_bwd{layer}", grid=(n,),
        in_specs=[_row_spec(TM), _row_spec(TM), _row_spec(TM), _row_spec(TM), p_spec, VSPEC, VSPEC, VSPEC],
        out_specs=[_row_spec(TM), _const_spec((D, D)), _const_spec((PLE, D)), _const_spec((1, D)), _const_spec((1, D))],
        out_shape=[_sds((s_len, D)), _sds((D, D), BF), _sds((PLE, D), BF), _sds((1, D)), _sds((1, D))],
        scratch_shapes=[pltpu.VMEM((D, D), F32), pltpu.VMEM((PLE, D), F32)],
        args=[dx3, x2, z, pe, p, ple_g, w_gate, post_g], rider=rider)


def _qkv_bwd(dq, dkv, x3, dx4, q_g, kv_g, w_q, w_kv):
    s_len = x3.shape[0]
    n = s_len // TM

    def body(dq_ref, dkv_ref, x_ref, dx_ref, qg_ref, kg_ref, wq_ref, wkv_ref,
             dx3_ref, dwq_ref, dwkv_ref, dqg_ref, dkg_ref, qacc, kacc):
        i = pl.program_id(0)

        @pl.when(i == 0)
        def _():
            qacc[...] = jnp.zeros_like(qacc)
            kacc[...] = jnp.zeros_like(kacc)
            dqg_ref[...] = jnp.zeros_like(dqg_ref)
            dkg_ref[...] = jnp.zeros_like(dkg_ref)

        x = x_ref[...]
        qg = qg_ref[1:2, :]
        kg = kg_ref[...]
        dq_v = dq_ref[...]
        dkv_v = dkv_ref[...].astype(BF)
        qacc[...] += _dot_tn(_rms_fwd(x, qg).astype(BF), dq_v)
        kacc[...] += _dot_tn(_rms_fwd(x, kg).astype(BF), dkv_v)
        dxq, prod_q = _rms_bwd(x, qg, _dot_nt(dq_v, wq_ref[...]))
        dxk, prod_k = _rms_bwd(x, kg, _dot_nt(dkv_v, wkv_ref[...]))
        dqg_ref[...] += _rowsum(prod_q)
        dkg_ref[...] += _rowsum(prod_k)
        dx3_ref[...] = dx_ref[...] + dxq + dxk

        @pl.when(i == n - 1)
        def _():
            dwq_ref[...] = qacc[...].astype(BF)
            dwkv_ref[...] = kacc[...].astype(BF)

    outs, _ = _call(
        body, name="qkv_bwd", grid=(n,),
        in_specs=[_row_spec(TM), _row_spec(TM, 2 * KVD), _row_spec(TM), _row_spec(TM), VSPEC, VSPEC, VSPEC, VSPEC],
        out_specs=[_row_spec(TM), _const_spec((D, D)), _const_spec((D, 2 * KVD)),
                   _const_spec((1, D)), _const_spec((1, D))],
        out_shape=[_sds((s_len, D)), _sds((D, D), BF), _sds((D, 2 * KVD), BF), _sds((1, D)), _sds((1, D))],
        scratch_shapes=[pltpu.VMEM((D, D), F32), pltpu.VMEM((D, 2 * KVD), F32)],
        args=[dq, dkv, x3, dx4, q_g, kv_g, w_q, w_kv])
    return outs


def _attn_group(i, q, kvw, sink_ref, g):
    rows = GQA * BLK
    heads = [GQA * g + j for j in range(GQA)]
    off = jnp.where(i > 0, BLK, 0)
    row = lax.broadcasted_iota(jnp.int32, (rows, 2 * BLK), 0)
    rel = (row % BLK) - lax.broadcasted_iota(jnp.int32, (rows, 2 * BLK), 1) + off
    valid = (rel >= 0) & (rel < BLK)
    head_of_row = lax.broadcasted_iota(jnp.int32, (rows, 1), 0) // BLK
    slope = jnp.zeros((rows, 1), F32)
    sink = jnp.zeros((rows, 1), F32)
    for j, h in enumerate(heads):
        slope = jnp.where(head_of_row == j, SLOPES[h], slope)
        sink = jnp.where(head_of_row == j, sink_ref[0, h], sink)
    qs = jnp.concatenate([q[:, h * HEAD_DIM:(h + 1) * HEAD_DIM] for h in heads], axis=0)
    k = kvw[:, g * HEAD_DIM:(g + 1) * HEAD_DIM]
    v = kvw[:, KVD + g * HEAD_DIM:KVD + (g + 1) * HEAD_DIM]
    s = _dot_nt(qs, k) * ATT_SCALE - slope * rel.astype(F32)
    s = jnp.where(valid, s, NEG_INF)
    m = jnp.maximum(jnp.max(s, axis=-1, keepdims=True), sink)
    e = jnp.exp(s - m)
    es = jnp.exp(sink - m)
    inv = 1.0 / (jnp.sum(e, axis=-1, keepdims=True) + es)
    return e * inv, es * inv, qs, k, v


def _unstack_heads(stacked):
    return [stacked[j * BLK:(j + 1) * BLK, :] for j in range(GQA)]


def _kv_window(kv_ref, i):
    ks = pl.multiple_of(jnp.maximum(i * BLK - BLK, 0), BLK)
    return ks, kv_ref[pl.ds(ks, 2 * BLK), :]


def _attn_fwd(q, kv, sinks, x3, w_o, post_g, rider=None):
    s_len = q.shape[0]

    def body(q_ref, kv_ref, sk_ref, x_ref, wo_ref, g_ref, a_ref, y_ref, x4_ref):
        i = pl.program_id(0)
        _, kvw = _kv_window(kv_ref, i)
        q = q_ref[...]
        outs = []
        for g in range(N_KV_HEADS):
            p, _, _, _, v = _attn_group(i, q, kvw, sk_ref, g)
            outs += _unstack_heads(_dot(p.astype(BF), v))
        attn = jnp.concatenate(outs, axis=1)
        a_ref[...] = attn
        y = _dot(attn.astype(BF), wo_ref[...])
        y_ref[...] = y
        x4_ref[...] = x_ref[...] + _rms_fwd(y, g_ref[1:2, :])

    return _call(body, name="attn_fwd", grid=(s_len // BLK,),
                 in_specs=[_row_spec(BLK), VSPEC, SSPEC, _row_spec(BLK), VSPEC, VSPEC],
                 out_specs=[_row_spec(BLK)] * 3, out_shape=[_sds((s_len, D))] * 3,
                 args=[q, kv, sinks, x3, w_o, post_g], rider=rider)


def _attn_bwd(dx4, y, attn, q, kv, sinks, w_o, post_g, rider=None):
    s_len = q.shape[0]
    n = s_len // BLK

    def body(dx_ref, y_ref, a_ref, q_ref, kv_ref, sk_ref, wo_ref, g_ref,
             dq_ref, dkv_ref, dwo_ref, dg_ref, dsk_ref, wacc):
        i = pl.program_id(0)

        @pl.when(i == 0)
        def _():
            dkv_ref[...] = jnp.zeros_like(dkv_ref)
            wacc[...] = jnp.zeros_like(wacc)
            dg_ref[...] = jnp.zeros_like(dg_ref)
            dsk_ref[...] = jnp.zeros_like(dsk_ref)

        dy, prod = _rms_bwd(y_ref[...], g_ref[1:2, :], dx_ref[...])
        dg_ref[...] += _rowsum(prod)
        dyb = dy.astype(BF)
        attn = a_ref[...]
        wacc[...] += _dot_tn(attn.astype(BF), dyb)
        d_o = _dot_nt(dyb, wo_ref[...])
        dod = d_o * attn
        ks, kvw = _kv_window(kv_ref, i)
        q = q_ref[...]
        lane = lax.broadcasted_iota(jnp.int32, (1, D), 1)
        dqs, dks, dvs = [], [], []
        dsk = jnp.zeros((1, D), F32)
        for g in range(N_KV_HEADS):
            p, ps, qs, k, v = _attn_group(i, q, kvw, sk_ref, g)
            cols = [slice((GQA * g + j) * HEAD_DIM, (GQA * g + j + 1) * HEAD_DIM) for j in range(GQA)]
            do_s = jnp.concatenate([d_o[:, c] for c in cols], axis=0).astype(BF)
            dsum = jnp.concatenate([jnp.sum(dod[:, c], axis=-1, keepdims=True) for c in cols], axis=0)
            dp = _dot_nt(do_s, v)
            dsb = (p * (dp - dsum) * ATT_SCALE).astype(BF)
            sink_part = ps * dsum
            for j in range(GQA):
                dsk = dsk + jnp.where(lane == GQA * g + j, -_rowsum(sink_part[j * BLK:(j + 1) * BLK, :]), 0.0)
            dqs += _unstack_heads(_dot(dsb, k))
            dks.append(_dot_tn(dsb, qs))
            dvs.append(_dot_tn(p.astype(BF), do_s))
        dsk_ref[...] += dsk
        dq_ref[...] = jnp.concatenate(dqs, axis=1).astype(BF)
        dkv_ref[pl.ds(ks, 2 * BLK), :] += jnp.concatenate(dks + dvs, axis=1)

        @pl.when(i == n - 1)
        def _():
            dwo_ref[...] = wacc[...].astype(BF)

    return _call(
        body, name="attn_bwd", grid=(n,),
        in_specs=[_row_spec(BLK), _row_spec(BLK), _row_spec(BLK), _row_spec(BLK), VSPEC, SSPEC, VSPEC, VSPEC],
        out_specs=[_row_spec(BLK), _const_spec((s_len, 2 * KVD)), _const_spec((D, D)),
                   _const_spec((1, D)), _const_spec((1, D))],
        out_shape=[_sds((s_len, D), BF), _sds((s_len, 2 * KVD)), _sds((D, D), BF), _sds((1, D)), _sds((1, D))],
        scratch_shapes=[pltpu.VMEM((D, D), F32)],
        args=[dx4, y, attn, q, kv, sinks, w_o, post_g], rider=rider)


Big = collections.namedtuple("Big", "name src layer L A R C rb")


def _bigs():
    out = {"pool_w": Big("pool_w", "pool_w", None, 4, 4, POOL_G // N_CHIPS, POOL_G, 32)}
    for l in range(2):
        out[f"w_gu{l}"] = Big(f"w_gu{l}", "w_gu", l, 1, 2, D, FF_HALF, 256)
        out[f"w_down{l}"] = Big(f"w_down{l}", "w_down", l, 1, 4, FF // N_CHIPS, D, 352)
        out[f"w_ple_gate{l}"] = Big(f"w_ple_gate{l}", "w_ple_gate", l, 1, 4, D // N_CHIPS, D, 128)
        out[f"w_ple_proj{l}"] = Big(f"w_ple_proj{l}", "w_ple_proj", l, 1, 1, PLE, D // N_CHIPS, 128)
    out["w_q"] = Big("w_q", "w_q", None, 1, 4, D // N_CHIPS, D, 128)
    out["w_o"] = Big("w_o", "w_o", None, 1, 4, D // N_CHIPS, D, 128)
    out["w_kv"] = Big("w_kv", "w_kv", None, 1, 4, D // N_CHIPS, 2 * KVD, 128)
    return out


BIGS = _bigs()
POOL_SCALE = Big("pool_scale", "pool_scale", None, 1, 1, 1, D // N_CHIPS, 1)
BIG_SOURCES = ("w_gu", "w_down", "w_ple_gate", "w_ple_proj", "w_q", "w_o", "w_kv", "pool_w")


def _ncb(t):
    return N_CHIPS // t.A


def _full_shape(t, rows=None):
    return (t.L, t.A, t.R if rows is None else rows, _ncb(t) * t.C)


def _slot_index(t, k):
    return k // _ncb(t), k % _ncb(t)


def _slot(ref, t, k, row0, rows):
    a, cb = _slot_index(t, k)
    return ref.at[:, a, pl.ds(row0, rows), pl.ds(pl.multiple_of(cb * t.C, 128), t.C)]


def _place(t, w, kc, out_dtype):
    rb = min(t.R, 2 * t.rb)

    def body(kc_ref, w_ref, o_ref):
        del kc_ref
        o_ref[...] = w_ref[...].astype(out_dtype)

    def in_map(l, j, kc_ref):
        return (l if t.layer is None else t.layer, j, 0)

    def out_map(l, j, kc_ref):
        a, cb = _slot_index(t, kc_ref[0])
        return (l, a, j, cb)

    return pl.pallas_call(
        body, name=f"place_{t.name}",
        grid_spec=pltpu.PrefetchScalarGridSpec(
            num_scalar_prefetch=1, grid=(t.L, t.R // rb),
            in_specs=[pl.BlockSpec((None, rb, t.C), in_map)],
            out_specs=pl.BlockSpec((None, None, rb, t.C), out_map)),
        out_shape=_sds(_full_shape(t), out_dtype),
        compiler_params=_params(2),
    )(kc, w)


def _mesh_position():
    x, y, c = lax.axis_index("x"), lax.axis_index("y"), lax.axis_index("c")
    chips = [(1 - x, y), (x, 1 - y), (1 - x, 1 - y)]
    return x, y, c, chips


def _gather_rider(parts, fulls):
    nt = len(parts)
    TO_X, TO_Y, FWD_X, FWD_Y, SIB_X, SIB_Y, SIB_D = range(7)

    def rows_of(ti, core):
        t, r0, r1 = parts[ti]
        h = (r1 - r0) // 2
        return r0 + core * h, h

    def copy(outs, sems, kind, ti, k_src, row0, rows, dev):
        region = _slot(outs[ti], parts[ti][0], k_src, row0, rows)
        return pltpu.make_async_remote_copy(region, region, sems[0].at[ti, kind], sems[1].at[ti, kind],
                                            device_id=dev, device_id_type=MESH)

    def plan(outs, sems):
        x, y, c, _ = _mesh_position()
        me, kx, ky, kd = 2 * x + y, 2 * (1 - x) + y, 2 * x + (1 - y), 2 * (1 - x) + (1 - y)
        dev_x, dev_y, dev_d, sib = (1 - x, y, c), (x, 1 - y, c), (1 - x, 1 - y, c), (x, y, 1 - c)

        def whole(ti):
            return 0, parts[ti][0].R

        def mk(kind, k_send, k_recv, dev, send_rows, recv_rows):
            def build(ti, side):
                k_src = k_send if side == "s" else k_recv
                row0, rows = (send_rows if side == "s" else recv_rows)(ti)
                return copy(outs, sems, kind, ti, k_src, row0, rows, dev)
            return build

        def first_half(core):
            return lambda ti: (rows_of(ti, core)[0], rows_of(ti, core)[1] // 2)

        def second_half(core):
            return lambda ti: (rows_of(ti, core)[0] + rows_of(ti, core)[1] // 2, rows_of(ti, core)[1] // 2)

        mine = lambda ti: rows_of(ti, c)
        theirs = lambda ti: rows_of(ti, 1 - c)
        split = {
            TO_X: mk(TO_X, me, kx, dev_x, mine, mine),
            TO_Y: mk(TO_Y, me, ky, dev_y, mine, mine),
            FWD_X: mk(FWD_X, ky, kd, dev_x, first_half(c), first_half(c)),
            FWD_Y: mk(FWD_Y, kx, kd, dev_y, second_half(c), second_half(c)),
            SIB_X: mk(SIB_X, kx, kx, sib, mine, theirs),
            SIB_Y: mk(SIB_Y, ky, ky, sib, mine, theirs),
            SIB_D: mk(SIB_D, kd, kd, sib, mine, theirs),
        }
        direct = {
            TO_X: mk(TO_X, me, kx, dev_x, whole, whole),
            TO_Y: mk(TO_Y, me, ky, dev_y, whole, whole),
            FWD_X: mk(FWD_X, me, kd, dev_d, whole, whole),
        }
        return split, direct

    is_split = [t.R > 1 for t, _, _ in parts]

    def start(ins, outs, sems):
        split, direct = plan(outs, sems)
        for ti in range(nt):
            kinds = split if is_split[ti] else direct
            kinds[TO_X](ti, "s").start()
            kinds[TO_Y](ti, "s").start()
            if not is_split[ti]:
                kinds[FWD_X](ti, "s").start()

    def mid(ins, outs, sems):
        split, _ = plan(outs, sems)
        for ti in range(nt):
            if is_split[ti]:
                split[TO_Y](ti, "r").wait_recv()
                split[FWD_X](ti, "s").start()
                split[SIB_Y](ti, "s").start()
        for ti in range(nt):
            if is_split[ti]:
                split[TO_X](ti, "r").wait_recv()
                split[FWD_Y](ti, "s").start()
                split[SIB_X](ti, "s").start()

    def finish(ins, outs, sems):
        split, direct = plan(outs, sems)
        for ti in range(nt):
            if is_split[ti]:
                split[FWD_X](ti, "r").wait_recv()
                split[FWD_Y](ti, "r").wait_recv()
                split[SIB_D](ti, "s").start()
            else:
                for kind in (TO_X, TO_Y, FWD_X):
                    direct[kind](ti, "r").wait_recv()
        for ti in range(nt):
            if is_split[ti]:
                for kind in (SIB_X, SIB_Y, SIB_D):
                    split[kind](ti, "r").wait_recv()
        for ti in range(nt):
            kinds = split if is_split[ti] else direct
            for kind in kinds:
                kinds[kind](ti, "s").wait_send()

    sems = pltpu.SemaphoreType.DMA((nt, 7))
    return Rider(list(fulls), [_sds(a.shape, a.dtype) for a in fulls], {i: i for i in range(nt)},
                 [sems, sems], start, mid, finish)


def _pair_exchange(name, specs, grads):
    nt = len(specs)

    def body(*refs):
        gs = refs[:nt]
        lands = refs[nt:2 * nt]
        send, recv = refs[2 * nt:]
        x, y, c, _ = _mesh_position()
        cps = []
        for ti, t in enumerate(specs):
            half = t.R // 2
            cp = pltpu.make_async_remote_copy(gs[ti].at[:, :, pl.ds((1 - c) * half, half), :], lands[ti],
                                              send.at[ti], recv.at[ti],
                                              device_id=(x, y, 1 - c), device_id_type=MESH)
            cp.start()
            cps.append(cp)
        for cp in cps:
            cp.wait()

    return pl.pallas_call(
        body, name=name,
        in_specs=[ANYSPEC] * nt, out_specs=[ANYSPEC] * nt,
        out_shape=[_sds(_full_shape(t, t.R // 2), BF) for t in specs],
        scratch_shapes=[pltpu.SemaphoreType.DMA((nt,)), pltpu.SemaphoreType.DMA((nt,))],
        compiler_params=_params(),
    )(*grads)


def _pair_sum(t, g, land, kc):
    half = t.R // 2
    nj = half // t.rb
    w = _ncb(t) * t.C

    def body(kc_ref, g_ref, l_ref, o_ref):
        del kc_ref
        o_ref[...] = (g_ref[...].astype(F32) + l_ref[...].astype(F32)).astype(BF)

    return pl.pallas_call(
        body, name=f"pair_sum_{t.name}",
        grid_spec=pltpu.PrefetchScalarGridSpec(
            num_scalar_prefetch=1, grid=(t.L, nj),
            in_specs=[pl.BlockSpec((None, t.A, t.rb, w), lambda l, j, kc_ref: (l, 0, kc_ref[1] * nj + j, 0)),
                      pl.BlockSpec((None, t.A, t.rb, w), lambda l, j, kc_ref: (l, 0, j, 0))],
            out_specs=pl.BlockSpec((None, t.A, t.rb, w), lambda l, j, kc_ref: (l, 0, j, 0))),
        out_shape=_sds(_full_shape(t, half), BF),
        compiler_params=_params(2),
    )(kc, g, land)


def _scatter_rider(specs, sums):
    nt = len(specs)

    def copy(ins, outs, sems, ti, j, chip, c):
        t = specs[ti]
        cx, cy = chip
        return pltpu.make_async_remote_copy(_slot(ins[ti], t, 2 * cx + cy, 0, t.R // 2), outs[ti].at[j],
                                            sems[0].at[ti, j], sems[1].at[ti, j],
                                            device_id=(cx, cy, c), device_id_type=MESH)

    def start(ins, outs, sems):
        _, _, c, chips = _mesh_position()
        for j, chip in enumerate(chips):
            for ti in range(nt):
                copy(ins, outs, sems, ti, j, chip, c).start()

    def finish(ins, outs, sems):
        _, _, c, chips = _mesh_position()
        for j, chip in enumerate(chips):
            for ti in range(nt):
                copy(ins, outs, sems, ti, j, chip, c).wait()

    sems = pltpu.SemaphoreType.DMA((nt, N_CHIPS - 1))
    return Rider(list(sums), [_sds((N_CHIPS - 1, t.L, t.R // 2, t.C), BF) for t in specs], {}, [sems, sems],
                 start, None, finish)


def _chip_sum(t, s, land, kc, n_layers, prev):
    half = t.R // 2
    nj = half // t.rb

    def body(*refs):
        s_ref, l_ref, o_ref = refs[1], refs[2], refs[-1]
        acc = s_ref[...].astype(F32)
        for j in range(N_CHIPS - 1):
            acc = acc + l_ref[j].astype(F32)
        o_ref[...] = acc

    def own_map(l, j, kc_ref):
        a, cb = _slot_index(t, kc_ref[0])
        return (l, a, j, cb)

    def out_map(l, j, kc_ref):
        return (l if t.layer is None else t.layer, kc_ref[1] * nj + j, 0)

    in_specs = [pl.BlockSpec((None, None, t.rb, t.C), own_map),
                pl.BlockSpec((N_CHIPS - 1, None, t.rb, t.C), lambda l, j, kc_ref: (0, l, j, 0))]
    args = [kc, s, land]
    aliases = {}
    if prev is not None:
        in_specs.append(ANYSPEC)
        args.append(prev)
        aliases = {3: 0}
    return pl.pallas_call(
        body, name=f"chip_sum_{t.name}",
        grid_spec=pltpu.PrefetchScalarGridSpec(
            num_scalar_prefetch=1, grid=(t.L, nj), in_specs=in_specs,
            out_specs=pl.BlockSpec((None, t.rb, t.C), out_map)),
        out_shape=_sds((n_layers, t.R, t.C)),
        input_output_aliases=aliases,
        compiler_params=_params(2),
    )(*args)


def _chip_sum_fused(t, own, land, kc, n_layers, prev, by_cols):
    if by_cols:
        rows, cols = own.shape
    else:
        nb, rows, bw = own.shape
        cols = nb * bw
    nj = rows // t.rb

    def body(*refs):
        o_ref, l_ref, out_ref = refs[1], refs[2], refs[-1]
        acc = o_ref[...].astype(F32)
        for j in range(N_CHIPS - 1):
            acc = acc + l_ref[j].astype(F32)
        out_ref[...] = acc if by_cols else jnp.concatenate([acc[b] for b in range(nb)], axis=1)

    def out_map(j, kc_ref):
        return (t.layer, j, kc_ref[1]) if by_cols else (t.layer, kc_ref[1] * nj + j, 0)

    if by_cols:
        in_specs = [pl.BlockSpec((t.rb, cols), lambda j, kc_ref: (j, 0)),
                    pl.BlockSpec((N_CHIPS - 1, t.rb, cols), lambda j, kc_ref: (0, j, 0))]
    else:
        in_specs = [pl.BlockSpec((nb, t.rb, bw), lambda j, kc_ref: (0, j, 0)),
                    pl.BlockSpec((N_CHIPS - 1, nb, t.rb, bw), lambda j, kc_ref: (0, 0, j, 0))]
    args = [kc, own, land]
    aliases = {}
    if prev is not None:
        in_specs.append(ANYSPEC)
        args.append(prev)
        aliases = {3: 0}
    return pl.pallas_call(
        body, name=f"chip_sum_{t.name}",
        grid_spec=pltpu.PrefetchScalarGridSpec(
            num_scalar_prefetch=1, grid=(nj,), in_specs=in_specs,
            out_specs=pl.BlockSpec((None, t.rb, cols), out_map)),
        out_shape=_sds((n_layers, t.R, t.C)),
        input_output_aliases=aliases,
        compiler_params=_params(1),
    )(*args)


def _pair_share(halves, by_cols):
    nt = len(halves)

    def part(ref, ti, core):
        axis = 2 if by_cols[ti] else 1
        half = halves[ti].shape[axis] // 2
        piece = pl.ds(pl.multiple_of(core * half, 128 if by_cols[ti] else 8), half)
        return ref.at[:, :, piece] if by_cols[ti] else ref.at[:, piece, :]

    def body(*refs):
        outs = refs[nt:2 * nt]
        send, recv = refs[2 * nt:]
        x, y, c, _ = _mesh_position()
        cps = []
        for ti in range(nt):
            mine = part(outs[ti], ti, c)
            cp = pltpu.make_async_remote_copy(mine, mine, send.at[ti], recv.at[ti],
                                              device_id=(x, y, 1 - c), device_id_type=MESH)
            cp.start()
            cps.append(cp)
        for ti in range(nt):
            theirs = part(outs[ti], ti, 1 - c)
            pltpu.make_async_remote_copy(theirs, theirs, send.at[ti], recv.at[ti],
                                         device_id=(x, y, 1 - c), device_id_type=MESH).wait_recv()
        for cp in cps:
            cp.wait_send()

    return pl.pallas_call(
        body, name="grads_pair_share",
        in_specs=[ANYSPEC] * nt, out_specs=[ANYSPEC] * nt,
        out_shape=[_sds(a.shape, a.dtype) for a in halves],
        scratch_shapes=[pltpu.SemaphoreType.DMA((nt,)), pltpu.SemaphoreType.DMA((nt,))],
        input_output_aliases={i: i for i in range(nt)},
        compiler_params=_params(),
    )(*halves)


def _adamw_math(w, g, m, v):
    m = B1 * m + (1.0 - B1) * g
    v = B2 * v + (1.0 - B2) * (g * g)
    delta = -LR * ((m / BC1) / (jnp.sqrt(v / BC2) + AEPS) + WD * w)
    return delta, m, v


def _adamw(name, rb, w, g, m, v):
    n_layers, r, c = w.shape

    def body(w_ref, g_ref, m_ref, v_ref, go_ref, d_ref, nm_ref, nv_ref):
        g_v = g_ref[...]
        go_ref[...] = g_v
        d_ref[...], nm_ref[...], nv_ref[...] = _adamw_math(w_ref[...], g_v, m_ref[...], v_ref[...])

    spec = pl.BlockSpec((None, rb, c), lambda l, j: (l, j, 0))
    return pl.pallas_call(
        body, name=f"adamw_{name}", grid=(n_layers, r // rb),
        in_specs=[spec] * 4, out_specs=[spec] * 4, out_shape=[_sds(w.shape)] * 4,
        compiler_params=_params(2),
    )(w, g, m, v)


GAIN_ROWS = {"pre_mix_g": 0, "post_mix_g": 2, "pre_ffn_g": 4, "post_ffn_g": 6, "ple_g": 8, "ple_post_g": 10}
ROW_KV_G, ROW_POOL_SCALE, ROW_SINKS, ROW_LOSS, PACK_ROWS = 12, 13, 14, 15, 16
SMALL_NAMES = tuple(GAIN_ROWS) + ("kv_g", "pool_scale", "sinks")


def _small_all_reduce(rows, dpool, rider=None):
    ng, pr = len(WINDOWS), POOL_G // N_CHIPS

    def body(*refs):
        row_refs = refs[:PACK_ROWS]
        dpool_ref, tot_ref, gpool_ref, pack, land, pland, send, recv, psend, precv = refs[PACK_ROWS:]
        x, y, c, _ = _mesh_position()
        me = 4 * x + 2 * y + c
        for r in range(PACK_ROWS):
            pack[r:r + 1, :] = row_refs[r][...]

        def shard_of(k):
            return dpool_ref.at[:, pl.ds(pl.multiple_of(k * pr, pr), pr), :]

        cps = []
        for j in range(1, N_DEV):
            px, py, pc = x ^ (j >> 2), y ^ ((j >> 1) & 1), c ^ (j & 1)
            cps.append(pltpu.make_async_remote_copy(pack, land.at[me], send.at[j], recv.at[j],
                                                    device_id=(px, py, pc), device_id_type=MESH))
            cps.append(pltpu.make_async_remote_copy(shard_of(2 * px + py), pland.at[me], psend.at[j], precv.at[j],
                                                    device_id=(px, py, pc), device_id_type=MESH))
        for cp in cps:
            cp.start()
        land[me] = pack[...]
        pland[me] = dpool_ref[:, pl.ds(pl.multiple_of((2 * x + y) * pr, pr), pr), :]
        for j in range(1, N_DEV):
            pltpu.make_async_remote_copy(pack, land.at[me ^ j], send.at[j], recv.at[j],
                                         device_id=(x, y, c), device_id_type=MESH).wait_recv()
            pltpu.make_async_remote_copy(shard_of(0), pland.at[me ^ j], psend.at[j], precv.at[j],
                                         device_id=(x, y, c), device_id_type=MESH).wait_recv()
        for cp in cps:
            cp.wait_send()
        tot = land[0]
        gp = pland[0].astype(F32)
        for d in range(1, N_DEV):
            tot = tot + land[d]
            gp = gp + pland[d].astype(F32)
        tot_ref[...] = tot
        gpool_ref[...] = gp

    sems = pltpu.SemaphoreType.DMA((N_DEV,))
    return _call(
        body, name="small_all_reduce", grid=(1,),
        in_specs=[VSPEC] * (PACK_ROWS + 1), out_specs=[VSPEC, VSPEC],
        out_shape=[_sds((PACK_ROWS, D)), _sds((ng, pr, POOL_G))],
        scratch_shapes=[pltpu.VMEM((PACK_ROWS, D), F32), pltpu.VMEM((N_DEV, PACK_ROWS, D), F32),
                        pltpu.VMEM((N_DEV, ng, pr, POOL_G), BF), sems, sems, sems, sems],
        args=[*rows, dpool], rider=rider)


def _small_adamw(tot, kc, small_w, small_m, small_v):
    names = SMALL_NAMES
    n = len(names)

    def body(*refs):
        tot_ref, kc_ref = refs[0], refs[1]
        w_refs = dict(zip(names, refs[2:2 + n]))
        m_refs = dict(zip(names, refs[2 + n:2 + 2 * n]))
        v_refs = dict(zip(names, refs[2 + 2 * n:2 + 3 * n]))
        loss_ref = refs[2 + 3 * n]
        out_refs = {nm: refs[3 + 3 * n + 4 * k: 7 + 3 * n + 4 * k] for k, nm in enumerate(names)}
        tot = tot_ref[...]
        loss_ref[...] = 0.5 * jnp.sum(tot[ROW_LOSS:ROW_LOSS + 1, :], axis=-1, keepdims=True) * (1.0 / D)

        def update(nm, g):
            g_ref, d_ref, nm_ref, nv_ref = out_refs[nm]
            g_ref[...] = g
            d_ref[...], nm_ref[...], nv_ref[...] = _adamw_math(w_refs[nm][...], g, m_refs[nm][...], v_refs[nm][...])

        for nm, r in GAIN_ROWS.items():
            update(nm, tot[r:r + 2, :])
        update("kv_g", tot[ROW_KV_G:ROW_KV_G + 1, :])
        k = kc_ref[0]
        width = D // N_CHIPS
        g_scale = jnp.zeros((1, width), F32)
        for kk in range(N_CHIPS):
            g_scale = g_scale + jnp.where(k == kk, tot[ROW_POOL_SCALE:ROW_POOL_SCALE + 1, kk * width:(kk + 1) * width], 0.0)
        update("pool_scale", g_scale)
        update("sinks", tot[ROW_SINKS:ROW_SINKS + 1, 0:N_HEADS])

    ins = [tot, kc] + [small_w[nm] for nm in names] + [small_m[nm] for nm in names] + [small_v[nm] for nm in names]
    out_shape = [_sds((1, 1))]
    for nm in names:
        out_shape += [_sds(small_w[nm].shape)] * 4
    outs = pl.pallas_call(
        body, name="small_adamw",
        in_specs=[VSPEC, SSPEC] + [VSPEC] * (3 * n), out_specs=[VSPEC] * len(out_shape), out_shape=out_shape,
        compiler_params=_params(),
    )(*ins)
    return outs[0], {nm: outs[1 + 4 * k: 5 + 4 * k] for k, nm in enumerate(names)}


def _compute_layout(t, full):
    if t.src == "w_gu":
        return full.reshape(2, D, FF)
    if t.src == "pool_w":
        return full.reshape(len(WINDOWS), POOL_G, POOL_G)
    if t.src == "pool_scale":
        return full.reshape(1, D)
    return full.reshape(t.A * t.R, _ncb(t) * t.C)


def kernel(x, p, pre_mix_g, post_mix_g, pre_ffn_g, post_ffn_g, pool_w, pool_scale, kv_g, w_kv, w_q, sinks, w_o, w_gu, w_down, ple_g, w_ple_gate, w_ple_proj, ple_post_g, loss_target, m_pre_mix_g, m_post_mix_g, m_pre_ffn_g, m_post_ffn_g, m_pool_w, m_pool_scale, m_kv_g, m_w_kv, m_w_q, m_sinks, m_w_o, m_w_gu, m_w_down, m_ple_g, m_w_ple_gate, m_w_ple_proj, m_ple_post_g, v_pre_mix_g, v_post_mix_g, v_pre_ffn_g, v_post_ffn_g, v_pool_w, v_pool_scale, v_kv_g, v_w_kv, v_w_q, v_sinks, v_w_o, v_w_gu, v_w_down, v_ple_g, v_w_ple_gate, v_w_ple_proj, v_ple_post_g):
    weights = dict(pre_mix_g=pre_mix_g, post_mix_g=post_mix_g, pre_ffn_g=pre_ffn_g, post_ffn_g=post_ffn_g,
                   pool_w=pool_w, pool_scale=pool_scale, kv_g=kv_g, w_kv=w_kv, w_q=w_q, sinks=sinks, w_o=w_o,
                   w_gu=w_gu, w_down=w_down, ple_g=ple_g, w_ple_gate=w_ple_gate, w_ple_proj=w_ple_proj,
                   ple_post_g=ple_post_g)
    m_in = dict(pre_mix_g=m_pre_mix_g, post_mix_g=m_post_mix_g, pre_ffn_g=m_pre_ffn_g, post_ffn_g=m_post_ffn_g,
                pool_w=m_pool_w, pool_scale=m_pool_scale, kv_g=m_kv_g, w_kv=m_w_kv, w_q=m_w_q, sinks=m_sinks,
                w_o=m_w_o, w_gu=m_w_gu, w_down=m_w_down, ple_g=m_ple_g, w_ple_gate=m_w_ple_gate,
                w_ple_proj=m_w_ple_proj, ple_post_g=m_ple_post_g)
    v_in = dict(pre_mix_g=v_pre_mix_g, post_mix_g=v_post_mix_g, pre_ffn_g=v_pre_ffn_g, post_ffn_g=v_post_ffn_g,
                pool_w=v_pool_w, pool_scale=v_pool_scale, kv_g=v_kv_g, w_kv=v_w_kv, w_q=v_w_q, sinks=v_sinks,
                w_o=v_w_o, w_gu=v_w_gu, w_down=v_w_down, ple_g=v_ple_g, w_ple_gate=v_w_ple_gate,
                w_ple_proj=v_w_ple_proj, ple_post_g=v_ple_post_g)
    order = ["pre_mix_g", "post_mix_g", "pre_ffn_g", "post_ffn_g", "pool_w", "pool_scale", "kv_g", "w_kv", "w_q",
             "sinks", "w_o", "w_gu", "w_down", "ple_g", "w_ple_gate", "w_ple_proj", "ple_post_g"]

    kc = jnp.stack([2 * lax.axis_index("x") + lax.axis_index("y"), lax.axis_index("c")]).astype(jnp.int32)
    s_len = x.shape[1]
    x2d = x.reshape(s_len, D)
    p3d = p.reshape(2, s_len, PLE)
    target = loss_target.reshape(s_len, D)
    kv_g2d = kv_g.reshape(1, D)
    gains = {nm: weights[nm] for nm in GAIN_ROWS}

    def shard_view(src, a):
        t = next(t for t in BIGS.values() if t.src == src)
        return a.reshape(-1, t.R, t.C)

    placed = {nm: _place(t, shard_view(t.src, weights[t.src]), kc, BF) for nm, t in BIGS.items()}
    placed["pool_scale"] = _place(POOL_SCALE, pool_scale.reshape(1, 1, D // N_CHIPS), kc, F32)
    specs = dict(BIGS, pool_scale=POOL_SCALE)

    def gather(names, rows=None):
        rows = rows or {}
        parts = [(specs[nm],) + tuple(rows.get(nm, (0, specs[nm].R))) for nm in names]
        return _gather_rider(parts, [placed[nm] for nm in names])

    def take(names, results):
        for nm, a in zip(names, results):
            placed[nm] = a

    def weight(nm):
        return _compute_layout(specs[nm], placed[nm])

    first = ["pool_w", "pool_scale", "w_gu0", "w_down0"]
    take(first, _run("weights_gather_first", gather(first)))

    y0, x1 = _mixa_fwd(x2d, gains["pre_mix_g"], weight("pool_w"), weight("pool_scale"), gains["post_mix_g"])

    ride = ["w_ple_gate0", "w_ple_proj0", "w_q", "w_kv", "w_o", "w_gu1"]
    (f0, x2), got = _ffn_fwd(0, x1, gains["pre_ffn_g"], weight("w_gu0"), weight("w_down0"), gains["post_ffn_g"],
                             rider=gather(ride, {"w_gu1": (0, 320)}))
    take(ride, got)

    ride = ["w_ple_gate1", "w_ple_proj1", "w_gu1"]
    (z0, pe0, x3, q, kv), got = _ple_fwd(
        0, x2, p3d, gains["ple_g"], weight("w_ple_gate0"), weight("w_ple_proj0"), gains["ple_post_g"],
        qkv=(gains["pre_mix_g"], kv_g2d, weight("w_q"), weight("w_kv")),
        rider=gather(ride, {"w_gu1": (320, 704)}))
    take(ride, got)

    ride = ["w_down1", "w_gu1"]
    (attn, y1, x4), got = _attn_fwd(q, kv, sinks, x3, weight("w_o"), gains["post_mix_g"],
                                    rider=gather(ride, {"w_gu1": (704, D)}))
    take(ride, got)

    (f1, x5), _ = _ffn_fwd(1, x4, gains["pre_ffn_g"], weight("w_gu1"), weight("w_down1"), gains["post_ffn_g"])
    (z1, pe1, dx6, loss_row), _ = _ple_fwd(1, x5, p3d, gains["ple_g"], weight("w_ple_gate1"), weight("w_ple_proj1"),
                                           gains["ple_post_g"], target=target)

    local = {}
    landed = {}
    fused = {}

    def pair_stage(tag, names):
        ts = [BIGS[nm] for nm in names]
        gs = [local[nm].reshape(_full_shape(t)) for nm, t in zip(names, ts)]
        lands = _pair_exchange(f"grads_pair_exchange_{tag}", ts, gs)
        return [_pair_sum(t, g, l, kc) for t, g, l in zip(ts, gs, lands)]

    def scatter(names, sums):
        return _scatter_rider([BIGS[nm] for nm in names], sums)

    def keep(names, sums, got):
        for nm, s, l in zip(names, sums, got):
            landed[nm] = (s, l)

    (dx5, local["w_ple_gate1"], local["w_ple_proj1"], d_ple1, d_plepost1), _ = _ple_bwd(
        1, dx6, x5, z1, pe1, p3d, gains["ple_g"], weight("w_ple_gate1"), gains["ple_post_g"])

    group_a = ["w_ple_gate1", "w_ple_proj1"]
    sums_a = pair_stage("a", group_a)
    (dx4, d_preffn1, d_postffn1, *scattered), _ = _ffn_bwd(
        1, dx5, x4, f1, gains["pre_ffn_g"], weight("w_gu1"), weight("w_down1"), gains["post_ffn_g"], kc)
    fused["w_gu1"], fused["w_down1"] = scattered[0:2], scattered[2:4]

    (dq, dkv, local["w_o"], d_postmix1, d_sinks), got = _attn_bwd(
        dx4, y1, attn, q, kv, sinks, weight("w_o"), gains["post_mix_g"], rider=scatter(group_a, sums_a))
    keep(group_a, sums_a, got)
    dx3, local["w_q"], local["w_kv"], d_premix1, d_kvg = _qkv_bwd(
        dq, dkv, x3, dx4, gains["pre_mix_g"], kv_g2d, weight("w_q"), weight("w_kv"))

    group_b = ["w_o", "w_q", "w_kv"]
    sums_b = pair_stage("b", group_b)
    (dx2, local["w_ple_gate0"], local["w_ple_proj0"], d_ple0, d_plepost0), got = _ple_bwd(
        0, dx3, x2, z0, pe0, p3d, gains["ple_g"], weight("w_ple_gate0"), gains["ple_post_g"],
        rider=scatter(group_b, sums_b))
    keep(group_b, sums_b, got)

    group_c = ["w_ple_gate0", "w_ple_proj0"]
    sums_c = pair_stage("c", group_c)
    (dx1, d_preffn0, d_postffn0, *scattered), _ = _ffn_bwd(
        0, dx2, x1, f0, gains["pre_ffn_g"], weight("w_gu0"), weight("w_down0"), gains["post_ffn_g"], kc)
    fused["w_gu0"], fused["w_down0"] = scattered[0:2], scattered[2:4]

    (dx0, d_pool, d_scale, d_postmix0, d_premix0), _ = _mixa_bwd(
        dx1, x2d, y0, gains["pre_mix_g"], weight("pool_w"), weight("pool_scale"), gains["post_mix_g"])

    rows = [d_premix0, d_premix1, d_postmix0, d_postmix1, d_preffn0, d_preffn1, d_postffn0, d_postffn1,
            d_ple0, d_ple1, d_plepost0, d_plepost1, d_kvg, d_scale, d_sinks, loss_row]
    as2d = lambda a: a.reshape(1, D) if a.ndim == 1 else a
    (tot, g_pool), got = _small_all_reduce(rows, d_pool, rider=scatter(group_c, sums_c))
    keep(group_c, sums_c, got)
    loss, small = _small_adamw(tot, kc, {nm: as2d(weights[nm]) for nm in SMALL_NAMES},
                               {nm: as2d(m_in[nm]) for nm in SMALL_NAMES},
                               {nm: as2d(v_in[nm]) for nm in SMALL_NAMES})

    shared = [src for src in BIG_SOURCES if src != "pool_w"]
    halves = []
    for src in shared:
        n_layers = shard_view(src, weights[src]).shape[0]
        acc = None
        for t in [t for t in BIGS.values() if t.src == src]:
            if t.name in fused:
                own, land = fused[t.name]
                acc = _chip_sum_fused(t, own, land, kc, n_layers, acc, by_cols=src == "w_down")
            else:
                s, l = landed[t.name]
                acc = _chip_sum(t, s, l, kc, n_layers, acc)
        halves.append(acc)
    full_grads = dict(zip(shared, _pair_share(halves, [src == "w_down" for src in shared])))
    full_grads["pool_w"] = g_pool

    out = {"grad": {}, "delta": {}, "new_m": {}, "new_v": {}}
    for src in BIG_SOURCES:
        g = full_grads[src]
        t = next(t for t in BIGS.values() if t.src == src)
        res = _adamw(src, t.rb, shard_view(src, weights[src]), g, shard_view(src, m_in[src]), shard_view(src, v_in[src]))
        shape = weights[src].shape
        for kind, a in zip(("grad", "delta", "new_m", "new_v"), res):
            out[kind][src] = a.reshape(shape)
    for nm in SMALL_NAMES:
        shape = weights[nm].shape
        for kind, a in zip(("grad", "delta", "new_m", "new_v"), small[nm]):
            out[kind][nm] = a.reshape(shape)

    return (loss.reshape(()), dx0.reshape(x.shape),
            *[out["grad"][nm] for nm in order], *[out["delta"][nm] for nm in order],
            *[out["new_m"][nm] for nm in order], *[out["new_v"][nm] for nm in order])
```

```python
import collections

import jax
import jax.numpy as jnp
from jax import lax
from jax.experimental import pallas as pl
from jax.experimental.pallas import tpu as pltpu

D = 1024
FF = 2816
N_HEADS = 16
HEAD_DIM = 64
N_KV_HEADS = 4
GQA = N_HEADS // N_KV_HEADS
KVD = N_KV_HEADS * HEAD_DIM
PLE = 256
BLK = 128
WINDOWS = (2, 4, 8, 16)
POOL_G = 256
HALO = 16
EPS = 1e-6
NEG_INF = -1e30
ATT_SCALE = HEAD_DIM ** -0.5
SLOPES = tuple(2.0 ** (-8.0 * (h + 1) / N_HEADS) for h in range(N_HEADS))
N_CHIPS = 4
N_DEV = 8

LR, B1, B2, AEPS, WD, STEP = 0.001, 0.9, 0.999, 1e-08, 0.01, 10
BC1 = 1.0 - B1 ** STEP
BC2 = 1.0 - B2 ** STEP

BF = jnp.bfloat16
F32 = jnp.float32
MESH = pl.DeviceIdType.MESH
VMEM_LIMIT_V7X = 58 * 1024 * 1024
TM = 256
TM_FFN_BWD = 512
FF_CHUNK = 256
FF_HALF = FF // 2

VSPEC = pl.BlockSpec(memory_space=pltpu.VMEM)
SSPEC = pl.BlockSpec(memory_space=pltpu.SMEM)
ANYSPEC = pl.BlockSpec(memory_space=pl.ANY)


def _params(n_grid=0):
    sem = ("arbitrary",) * n_grid if n_grid else None
    return pltpu.CompilerParams(dimension_semantics=sem, vmem_limit_bytes=VMEM_LIMIT_V7X)


def _sds(shape, dtype=F32):
    return jax.ShapeDtypeStruct(tuple(shape), dtype)


Rider = collections.namedtuple("Rider", "arrays out_shapes aliases scratch start mid finish")
MID_NUM, MID_DEN = 5, 8


def _call(body, *, name, grid, in_specs, out_specs, out_shape, args, scratch_shapes=(), rider=None, prefetch=None):
    ni, no, ns = len(in_specs), len(out_specs), len(scratch_shapes)
    npre = 0 if prefetch is None else 1
    pre = [] if prefetch is None else [prefetch]
    if rider is None:
        rider = Rider([], [], {}, [], None, None, None)
    ri, ro = len(rider.arrays), len(rider.out_shapes)

    def full(*refs):
        pre_refs, refs = refs[:npre], refs[npre:]
        ins, refs = refs[:ni], refs[ni:]
        rins, refs = refs[:ri], refs[ri:]
        outs, refs = refs[:no], refs[no:]
        routs, refs = refs[:ro], refs[ro:]
        scr, rscr = refs[:ns], refs[ns:]
        ids = [pl.program_id(a) for a in range(len(grid))]
        first = ids[0] == 0
        last = ids[0] == grid[0] - 1
        for a in range(1, len(grid)):
            first = first & (ids[a] == 0)
            last = last & (ids[a] == grid[a] - 1)

        if rider.start is not None:
            @pl.when(first)
            def _():
                rider.start(rins, routs, rscr)

        if rider.mid is not None:
            assert len(grid) == 1

            @pl.when(ids[0] == (grid[0] * MID_NUM) // MID_DEN)
            def _():
                rider.mid(rins, routs, rscr)

        body(*pre_refs, *ins, *outs, *scr)

        if rider.finish is not None:
            @pl.when(last)
            def _():
                rider.finish(rins, routs, rscr)

    outs = pl.pallas_call(
        full, name=name,
        grid_spec=pltpu.PrefetchScalarGridSpec(
            num_scalar_prefetch=npre, grid=grid,
            in_specs=list(in_specs) + [ANYSPEC] * ri, out_specs=list(out_specs) + [ANYSPEC] * ro,
            scratch_shapes=list(scratch_shapes) + list(rider.scratch)),
        out_shape=list(out_shape) + list(rider.out_shapes),
        input_output_aliases={npre + ni + a: no + b for a, b in rider.aliases.items()},
        compiler_params=_params(len(grid)))(*pre, *args, *rider.arrays)
    return list(outs[:no]), list(outs[no:])


def _run(name, rider):
    ri = len(rider.arrays)

    def body(*refs):
        rins, routs, rscr = refs[:ri], refs[ri:ri + len(rider.out_shapes)], refs[ri + len(rider.out_shapes):]
        rider.start(rins, routs, rscr)
        if rider.mid is not None:
            rider.mid(rins, routs, rscr)
        rider.finish(rins, routs, rscr)

    return pl.pallas_call(
        body, name=name, in_specs=[ANYSPEC] * ri, out_specs=[ANYSPEC] * len(rider.out_shapes),
        out_shape=list(rider.out_shapes), scratch_shapes=list(rider.scratch),
        input_output_aliases=dict(rider.aliases), compiler_params=_params())(*rider.arrays)


def _rms_fwd(x, g):
    r = lax.rsqrt(jnp.mean(x * x, axis=-1, keepdims=True) + EPS)
    return x * r * g


def _rms_bwd(x, g, dy):
    r = lax.rsqrt(jnp.mean(x * x, axis=-1, keepdims=True) + EPS)
    xn = x * r
    dxn = dy * g
    dx = r * (dxn - xn * jnp.mean(dxn * xn, axis=-1, keepdims=True))
    return dx, dy * xn


def _rowsum(a):
    return jnp.sum(a, axis=0, keepdims=True)


def _sigmoid(z):
    return 1.0 / (1.0 + jnp.exp(-z))


def _dot(a, b):
    return jnp.dot(a, b, preferred_element_type=F32)


def _dot_nt(a, b):
    return lax.dot_general(a, b, (((1,), (1,)), ((), ())), preferred_element_type=F32)


def _dot_tn(a, b):
    return lax.dot_general(a, b, (((0,), (0,)), ((), ())), preferred_element_type=F32)


def _row_spec(tm, width=D):
    return pl.BlockSpec((tm, width), lambda i: (i, 0))


def _const_spec(shape):
    zeros = (0,) * len(shape)
    return pl.BlockSpec(tuple(shape), lambda *_: zeros)


def _pool_delta(he, pos):
    out = []
    for gi, w in enumerate(WINDOWS):
        hg = he[:, gi * POOL_G:(gi + 1) * POOL_G]
        s = hg
        k = 1
        while k < w:
            s = s + pltpu.roll(s, k, 0)
            k *= 2
        cnt = jnp.maximum(jnp.minimum(pos + 1, w), 1).astype(F32)
        out.append(s / cnt - hg)
    return out


def _load_with_halo_before(x_ref, i, tm):
    r0 = pl.multiple_of(i * tm, tm)
    hs = pl.multiple_of(jnp.maximum(i * tm - HALO, 0), 8)
    xh = jnp.where(i > 0, x_ref[pl.ds(hs, HALO), :], 0.0)
    xt = x_ref[pl.ds(r0, tm), :]
    return xt, jnp.concatenate([xh, xt], axis=0)


def _mixa_fwd(x, pre_g, pool_w, pool_scale, post_g):
    s_len = x.shape[0]
    n = s_len // TM

    def body(x_ref, pg_ref, w_ref, sc_ref, qg_ref, y_ref, x1_ref):
        i = pl.program_id(0)
        xt, xe = _load_with_halo_before(x_ref, i, TM)
        he = _rms_fwd(xe, pg_ref[0:1, :])
        pos = i * TM - HALO + lax.broadcasted_iota(jnp.int32, (TM + HALO, 1), 0)
        ds = _pool_delta(he, pos)
        ys = [_dot(ds[gi][HALO:, :].astype(BF), w_ref[gi]) for gi in range(len(WINDOWS))]
        y = jnp.concatenate(ys, axis=1) * sc_ref[...]
        y_ref[...] = y
        x1_ref[...] = xt + _rms_fwd(y, qg_ref[0:1, :])

    outs, _ = _call(body, name="mixa_fwd", grid=(n,),
                    in_specs=[VSPEC] * 5, out_specs=[_row_spec(TM), _row_spec(TM)],
                    out_shape=[_sds((s_len, D)), _sds((s_len, D))],
                    args=[x, pre_g, pool_w, pool_scale, post_g])
    return outs


def _mixa_bwd(dx1, x, y, pre_g, pool_w, pool_scale, post_g, rider=None):
    s_len = x.shape[0]
    n = s_len // TM
    ng = len(WINDOWS)

    def body(dx_ref, x_ref, y_ref, pg_ref, w_ref, sc_ref, qg_ref,
             dx0_ref, dw_ref, dsc_ref, dqg_ref, dpg_ref, wacc):
        i = pl.program_id(0)

        @pl.when(i == 0)
        def _():
            wacc[...] = jnp.zeros_like(wacc)
            dsc_ref[...] = jnp.zeros_like(dsc_ref)
            dqg_ref[...] = jnp.zeros_like(dqg_ref)
            dpg_ref[...] = jnp.zeros_like(dpg_ref)

        r0 = pl.multiple_of(i * TM, TM)
        xt, xe = _load_with_halo_before(x_ref, i, TM)
        he = _rms_fwd(xe, pg_ref[0:1, :])
        pos_b = i * TM - HALO + lax.broadcasted_iota(jnp.int32, (TM + HALO, 1), 0)
        ds = _pool_delta(he, pos_b)

        last = i == n - 1
        a0 = pl.multiple_of(jnp.minimum(i * TM + TM, s_len - HALO), 8)
        ye = jnp.concatenate([y_ref[pl.ds(r0, TM), :], y_ref[pl.ds(a0, HALO), :]], axis=0)
        dt = dx_ref[pl.ds(r0, TM), :]
        de = jnp.concatenate([dt, jnp.where(last, 0.0, dx_ref[pl.ds(a0, HALO), :])], axis=0)
        dye, prod = _rms_bwd(ye, qg_ref[0:1, :], de)
        dqg_ref[...] += _rowsum(prod[:TM, :])
        dys = dye * sc_ref[...]
        pos_a = i * TM + lax.broadcasted_iota(jnp.int32, (TM + HALO, 1), 0)

        dhs, dscs = [], []
        for gi, w in enumerate(WINDOWS):
            sl = slice(gi * POOL_G, (gi + 1) * POOL_G)
            wg = w_ref[gi]
            dys_g = dys[:, sl].astype(BF)
            d_g = ds[gi][HALO:, :].astype(BF)
            ypre = _dot(d_g, wg)
            dscs.append(_rowsum(dye[:TM, sl] * ypre))
            wacc[gi] += _dot_tn(d_g, dys_g[:TM, :])
            dd = _dot_nt(dys_g, wg)
            cnt = jnp.minimum(pos_a + 1, w).astype(F32)
            a = dd / cnt
            k = 1
            while k < w:
                a = a + pltpu.roll(a, TM + HALO - k, 0)
                k *= 2
            dhs.append(a[:TM, :] - dd[:TM, :])
        dsc_ref[...] += jnp.concatenate(dscs, axis=1)
        dh = jnp.concatenate(dhs, axis=1)
        dxp, prod2 = _rms_bwd(xt, pg_ref[0:1, :], dh)
        dpg_ref[...] += _rowsum(prod2)
        dx0_ref[...] = dt + dxp

        @pl.when(last)
        def _():
            dw_ref[...] = wacc[...].astype(BF)

    return _call(
        body, name="mixa_bwd", grid=(n,), in_specs=[VSPEC] * 7,
        out_specs=[_row_spec(TM), _const_spec((ng, POOL_G, POOL_G)), _const_spec((1, D)),
                   _const_spec((1, D)), _const_spec((1, D))],
        out_shape=[_sds((s_len, D)), _sds((ng, POOL_G, POOL_G), BF), _sds((1, D)), _sds((1, D)), _sds((1, D))],
        scratch_shapes=[pltpu.VMEM((ng, POOL_G, POOL_G), F32)],
        args=[dx1, x, y, pre_g, pool_w, pool_scale, post_g], rider=rider)


def _ffn_fwd(layer, x1, pre_g, wgu, wd, post_g, rider=None):
    s_len = x1.shape[0]

    def body(x_ref, pg_ref, wgu_ref, wd_ref, qg_ref, f_ref, x2_ref):
        x = x_ref[...]
        h = _rms_fwd(x, pg_ref[layer:layer + 1, :]).astype(BF)
        f = jnp.zeros((TM, D), F32)
        for c in range(FF // FF_HALF):
            cols = slice(c * FF_HALF, (c + 1) * FF_HALF)
            g = _dot(h, wgu_ref[0, :, cols])
            u = _dot(h, wgu_ref[1, :, cols])
            act = g * _sigmoid(g) * u
            f = f + _dot(act.astype(BF), wd_ref[cols, :])
        f_ref[...] = f
        x2_ref[...] = x + _rms_fwd(f, qg_ref[layer:layer + 1, :])

    return _call(body, name=f"ffn_fwd{layer}", grid=(s_len // TM,),
                 in_specs=[_row_spec(TM), VSPEC, VSPEC, VSPEC, VSPEC],
                 out_specs=[_row_spec(TM), _row_spec(TM)],
                 out_shape=[_sds((s_len, D)), _sds((s_len, D))],
                 args=[x1, pre_g, wgu, wd, post_g], rider=rider)


GU_PIECE = 128
DN_PIECE = 64
DN_SLOT = FF // N_CHIPS
HALF_D = D // 2


def _ffn_bwd(layer, dx2, x1, f, pre_g, wgu, wd, post_g, kc, rider=None):
    s_len = x1.shape[0]
    tm = TM_FFN_BWD
    n = s_len // tm
    nc = FF // FF_CHUNK
    n_gu, n_dn = FF_CHUNK // GU_PIECE, FF_CHUNK // DN_PIECE
    n_pieces = 2 * n_gu + n_dn
    n_blk = FF_HALF // GU_PIECE

    def edge_rows(c, i, kc_ref):
        return (jnp.where((c == 0) | (c == nc - 1), i, n - 1), 0)

    def chunk_at(c, kc_ref):
        return (c + (((kc_ref[0] + 1) % N_CHIPS) * nc) // N_CHIPS) % nc

    def exchange(kc_ref, c, accg, accu, accd, own_gu_ref, land_gu_ref, own_dn_ref, land_dn_ref,
                 pl_gu, pl_dn, sib_gu, sib_dn, mine_gu, mine_dn, sum_gu, sum_dn,
                 psend, precv, ssend, lsem, rrecv):
        x, y, core = lax.axis_index("x"), lax.axis_index("y"), lax.axis_index("c")
        lower = core == 0

        def pair_copy(cc, part):
            p = cc % 2
            src, dst = ((sib_gu, pl_gu), (sib_dn, pl_dn))[part]
            return pltpu.make_async_remote_copy(src.at[p], dst.at[cc], psend.at[p, part], precv.at[cc, part],
                                                device_id=(x, y, 1 - core), device_id_type=MESH)

        def scatter(cc, wait):
            p = cc % 2
            jobs = []
            for gu in range(2):
                for hc in range(n_gu):
                    hidden = chunk_at(cc, kc_ref) * FF_CHUNK + hc * GU_PIECE
                    k = hidden // FF_HALF
                    jobs.append((sum_gu.at[p, gu, hc], k + 2 * gu, 0,
                                 own_gu_ref, land_gu_ref, ((hidden - k * FF_HALF) // GU_PIECE,)))
            for q in range(n_dn):
                hidden = chunk_at(cc, kc_ref) * FF_CHUNK + q * DN_PIECE
                k = hidden // DN_SLOT
                off = pl.multiple_of(hidden - k * DN_SLOT, DN_PIECE)
                jobs.append((sum_dn.at[p, pl.ds(q * DN_PIECE, DN_PIECE), :], k, 1,
                             own_dn_ref, land_dn_ref, (pl.ds(off, DN_PIECE), slice(None))))
            for pi, (src, k, t, own_ref, land_ref, where) in enumerate(jobs):
                kx, ky = k // 2, k % 2
                fx, fy = (kx != x).astype(jnp.int32), (ky != y).astype(jnp.int32)
                local = (fx + fy) == 0
                j = jnp.maximum(fx + 2 * fy - 1, 0)

                @pl.when(local)
                def _():
                    cp = pltpu.make_async_copy(src, own_ref.at[where], lsem.at[p, pi])
                    if wait:
                        cp.wait()
                    else:
                        cp.start()

                @pl.when(jnp.logical_not(local))
                def _():
                    cp = pltpu.make_async_remote_copy(src, land_ref.at[(j,) + where], ssend.at[p, pi],
                                                      rrecv.at[t, j], device_id=(kx, ky, core), device_id_type=MESH)
                    if wait:
                        cp.wait_send()
                    else:
                        cp.start()

        def add_and_scatter(cc):
            p = cc % 2
            pair_copy(cc, 0).wait_recv()
            pair_copy(cc, 1).wait_recv()
            s_gu = (mine_gu[...] + pl_gu[cc].astype(F32)).astype(BF)
            for hc in range(n_gu):
                sum_gu[p, :, hc] = s_gu[:, :, hc * GU_PIECE:(hc + 1) * GU_PIECE]
            sum_dn[p] = (mine_dn[...] + pl_dn[cc].astype(F32)).astype(BF)
            scatter(cc, wait=False)

        @pl.when(c >= 1)
        def _():
            @pl.when(c >= 3)
            def _():
                scatter(c - 3, wait=True)
            add_and_scatter(c - 1)

        @pl.when(c >= 2)
        def _():
            pair_copy(c - 2, 0).wait_send()
            pair_copy(c - 2, 1).wait_send()

        p = c % 2
        my_rows = pl.ds(pl.multiple_of(core * HALF_D, HALF_D), HALF_D)
        sib_rows = pl.ds(pl.multiple_of((1 - core) * HALF_D, HALF_D), HALF_D)
        d_v = accd[...]
        sib_gu[p, 0] = accg[sib_rows, :].astype(BF)
        sib_gu[p, 1] = accu[sib_rows, :].astype(BF)
        sib_dn[p] = jnp.where(lower, d_v[:, HALF_D:], d_v[:, :HALF_D]).astype(BF)
        mine_gu[0] = accg[my_rows, :]
        mine_gu[1] = accu[my_rows, :]
        mine_dn[...] = jnp.where(lower, d_v[:, :HALF_D], d_v[:, HALF_D:])
        pair_copy(c, 0).start()
        pair_copy(c, 1).start()

        @pl.when(c == nc - 1)
        def _():
            scatter(nc - 3, wait=True)
            add_and_scatter(nc - 1)
            for cc in (nc - 2, nc - 1):
                pair_copy(cc, 0).wait_send()
                pair_copy(cc, 1).wait_send()
                scatter(cc, wait=True)
            for t, land_ref in enumerate((land_gu_ref, land_dn_ref)):
                for j in range(N_CHIPS - 1):
                    pltpu.make_async_remote_copy(land_ref.at[j], land_ref.at[j], ssend.at[0, 0], rrecv.at[t, j],
                                                 device_id=(x, y, core), device_id_type=MESH).wait_recv()

    def body(kc_ref, dx_ref, x_ref, f_ref, pg_ref, wgu_ref, wd_ref, qg_ref,
             dx1_ref, dpg_ref, dqg_ref, own_gu_ref, land_gu_ref, own_dn_ref, land_dn_ref,
             h_s, df_s, dh_s, accg, accu, accd, *comm):
        c = pl.program_id(0)
        i = pl.program_id(1)
        rows = pl.ds(pl.multiple_of(i * tm, tm), tm)
        pg = pg_ref[layer:layer + 1, :]

        @pl.when((c == 0) & (i == 0))
        def _():
            dpg_ref[...] = jnp.zeros_like(dpg_ref)
            dqg_ref[...] = jnp.zeros_like(dqg_ref)

        @pl.when(c == 0)
        def _():
            h_s[rows, :] = _rms_fwd(x_ref[...], pg).astype(BF)
            df, prod = _rms_bwd(f_ref[...], qg_ref[layer:layer + 1, :], dx_ref[...])
            df_s[rows, :] = df.astype(BF)
            dqg_ref[...] += _rowsum(prod)

        @pl.when(i == 0)
        def _():
            accg[...] = jnp.zeros_like(accg)
            accu[...] = jnp.zeros_like(accu)
            accd[...] = jnp.zeros_like(accd)

        h = h_s[rows, :]
        df = df_s[rows, :]
        wg = wgu_ref[0]
        wu = wgu_ref[1]
        g = _dot(h, wg)
        u = _dot(h, wu)
        sg = _sigmoid(g)
        a = g * sg
        dact = _dot_nt(df, wd_ref[...])
        accd[...] += _dot_tn((a * u).astype(BF), df)
        du = (dact * a).astype(BF)
        dg = (dact * u * (sg * (1.0 + g * (1.0 - sg)))).astype(BF)
        accg[...] += _dot_tn(h, dg)
        accu[...] += _dot_tn(h, du)
        dh = _dot_nt(dg, wg) + _dot_nt(du, wu)

        @pl.when(c == 0)
        def _():
            dh_s[rows, :] = dh

        @pl.when((c > 0) & (c < nc - 1))
        def _():
            dh_s[rows, :] += dh

        @pl.when(c == nc - 1)
        def _():
            dxp, prod = _rms_bwd(x_ref[...], pg, dh_s[rows, :] + dh)
            dpg_ref[...] += _rowsum(prod)
            dx1_ref[...] = dx_ref[...] + dxp

        @pl.when(i == n - 1)
        def _():
            exchange(kc_ref, c, accg, accu, accd, own_gu_ref, land_gu_ref, own_dn_ref, land_dn_ref, *comm)

    dma = pltpu.SemaphoreType.DMA
    return _call(
        body, name=f"ffn_bwd{layer}", grid=(nc, n),
        in_specs=[pl.BlockSpec((tm, D), edge_rows), pl.BlockSpec((tm, D), edge_rows),
                  pl.BlockSpec((tm, D), lambda c, i, kc_ref: (jnp.where(c == 0, i, n - 1), 0),
                               pipeline_mode=pl.Buffered(1)),
                  VSPEC,
                  pl.BlockSpec((2, D, FF_CHUNK), lambda c, i, kc_ref: (0, 0, chunk_at(c, kc_ref))),
                  pl.BlockSpec((FF_CHUNK, D), lambda c, i, kc_ref: (chunk_at(c, kc_ref), 0)),
                  VSPEC],
        out_specs=[pl.BlockSpec((tm, D), lambda c, i, kc_ref: (jnp.where(c == nc - 1, i, 0), 0)),
                   _const_spec((1, D)), _const_spec((1, D)), ANYSPEC, ANYSPEC, ANYSPEC, ANYSPEC],
        out_shape=[_sds((s_len, D)), _sds((1, D)), _sds((1, D)),
                   _sds((n_blk, HALF_D, GU_PIECE), BF), _sds((N_CHIPS - 1, n_blk, HALF_D, GU_PIECE), BF),
                   _sds((DN_SLOT, HALF_D), BF), _sds((N_CHIPS - 1, DN_SLOT, HALF_D), BF)],
        scratch_shapes=[pltpu.VMEM((s_len, D), BF), pltpu.VMEM((s_len, D), BF), pltpu.VMEM((s_len, D), F32),
                        pltpu.VMEM((D, FF_CHUNK), F32), pltpu.VMEM((D, FF_CHUNK), F32),
                        pltpu.VMEM((FF_CHUNK, D), F32),
                        pltpu.VMEM((nc, 2, HALF_D, FF_CHUNK), BF), pltpu.VMEM((nc, FF_CHUNK, HALF_D), BF),
                        pltpu.VMEM((2, 2, HALF_D, FF_CHUNK), BF), pltpu.VMEM((2, FF_CHUNK, HALF_D), BF),
                        pltpu.VMEM((2, HALF_D, FF_CHUNK), F32), pltpu.VMEM((FF_CHUNK, HALF_D), F32),
                        pltpu.VMEM((2, 2, n_gu, HALF_D, GU_PIECE), BF), pltpu.VMEM((2, FF_CHUNK, HALF_D), BF),
                        dma((2, 2)), dma((nc, 2)), dma((2, n_pieces)), dma((2, n_pieces)), dma((2, N_CHIPS - 1))],
        args=[dx2, x1, f, pre_g, wgu, wd, post_g], rider=rider, prefetch=kc)


def _ple_fwd(layer, x2, p, ple_g, w_gate, w_proj, post_g, target=None, qkv=None, rider=None):
    s_len = x2.shape[0]
    final = target is not None
    assert not (final and qkv)

    def body(*refs):
        if final:
            x_ref, p_ref, g_ref, wg_ref, wp_ref, qg_ref, t_ref, z_ref, pe_ref, dx_ref, lv_ref = refs
        elif qkv:
            (x_ref, p_ref, g_ref, wg_ref, wp_ref, qg_ref, ng_ref, kg_ref, wq_ref, wkv_ref,
             z_ref, pe_ref, x3_ref, q_ref, kv_ref) = refs
        else:
            x_ref, p_ref, g_ref, wg_ref, wp_ref, qg_ref, z_ref, pe_ref, x3_ref = refs
        x = x_ref[...]
        r = _rms_fwd(x, g_ref[layer:layer + 1, :]).astype(BF)
        z = _dot(r, wg_ref[...])
        pe = _dot(p_ref[...].astype(BF), wp_ref[...])
        z_ref[...] = z
        pe_ref[...] = pe
        x3 = x + _rms_fwd(pe * _sigmoid(z), qg_ref[layer:layer + 1, :])
        if final:
            @pl.when(pl.program_id(0) == 0)
            def _():
                lv_ref[...] = jnp.zeros_like(lv_ref)
            err = x3 - t_ref[...]
            dx_ref[...] = err * (1.0 / D)
            lv_ref[...] += _rowsum(err * err)
        else:
            x3_ref[...] = x3
        if qkv:
            q_ref[...] = _dot(_rms_fwd(x3, ng_ref[layer + 1:layer + 2, :]).astype(BF), wq_ref[...]).astype(BF)
            kv_ref[...] = _dot(_rms_fwd(x3, kg_ref[...]).astype(BF), wkv_ref[...]).astype(BF)

    p_spec = pl.BlockSpec((None, TM, PLE), lambda i: (layer, i, 0))
    in_specs = [_row_spec(TM), p_spec, VSPEC, VSPEC, VSPEC, VSPEC]
    args = [x2, p, ple_g, w_gate, w_proj, post_g]
    out_specs = [_row_spec(TM), _row_spec(TM), _row_spec(TM)]
    out_shape = [_sds((s_len, D))] * 3
    if qkv:
        in_specs += [VSPEC] * 4
        args += list(qkv)
        out_specs += [_row_spec(TM), _row_spec(TM, 2 * KVD)]
        out_shape += [_sds((s_len, D), BF), _sds((s_len, 2 * KVD), BF)]
    if final:
        in_specs.append(_row_spec(TM))
        args.append(target)
        out_specs.append(_const_spec((1, D)))
        out_shape.append(_sds((1, D)))
    return _call(body, name=f"ple_fwd{layer}", grid=(s_len // TM,), in_specs=in_specs, out_specs=out_specs,
                 out_shape=out_shape, args=args, rider=rider)


def _ple_bwd(layer, dx3, x2, z, pe, p, ple_g, w_gate, post_g, rider=None):
    s_len = x2.shape[0]
    n = s_len // TM

    def body(dx_ref, x_ref, z_ref, pe_ref, p_ref, g_ref, wg_ref, qg_ref,
             dx2_ref, dwg_ref, dwp_ref, dg_ref, dqg_ref, gacc, pacc):
        i = pl.program_id(0)

        @pl.when(i == 0)
        def _():
            gacc[...] = jnp.zeros_like(gacc)
            pacc[...] = jnp.zeros_like(pacc)
            dg_ref[...] = jnp.zeros_like(dg_ref)
            dqg_ref[...] = jnp.zeros_like(dqg_ref)

        dx = dx_ref[...]
        x = x_ref[...]
        pe_v = pe_ref[...]
        gate = _sigmoid(z_ref[...])
        de, prod = _rms_bwd(pe_v * gate, qg_ref[layer:layer + 1, :], dx)
        dqg_ref[...] += _rowsum(prod)
        dpe = (de * gate).astype(BF)
        dz = (de * pe_v * gate * (1.0 - gate)).astype(BF)
        pacc[...] += _dot_tn(p_ref[...].astype(BF), dpe)
        g = g_ref[layer:layer + 1, :]
        r = _rms_fwd(x, g).astype(BF)
        gacc[...] += _dot_tn(r, dz)
        dr = _dot_nt(dz, wg_ref[...])
        dxp, prod2 = _rms_bwd(x, g, dr)
        dg_ref[...] += _rowsum(prod2)
        dx2_ref[...] = dx + dxp

        @pl.when(i == n - 1)
        def _():
            dwg_ref[...] = gacc[...].astype(BF)
            dwp_ref[...] = pacc[...].astype(BF)

    p_spec = pl.BlockSpec((None, TM, PLE), lambda i: (layer, i, 0))
    return _call(
        body, name=f"ple_bwd{layer}", grid=(n,),
        in_specs=[_row_spec(TM), _row_spec(TM), _row_spec(TM), _row_spec(TM), p_spec, VSPEC, VSPEC, VSPEC],
        out_specs=[_row_spec(TM), _const_spec((D, D)), _const_spec((PLE, D)), _const_spec((1, D)), _const_spec((1, D))],
        out_shape=[_sds((s_len, D)), _sds((D, D), BF), _sds((PLE, D), BF), _sds((1, D)), _sds((1, D))],
        scratch_shapes=[pltpu.VMEM((D, D), F32), pltpu.VMEM((PLE, D), F32)],
        args=[dx3, x2, z, pe, p, ple_g, w_gate, post_g], rider=rider)


def _qkv_bwd(dq, dkv, x3, dx4, q_g, kv_g, w_q, w_kv):
    s_len = x3.shape[0]
    n = s_len // TM

    def body(dq_ref, dkv_ref, x_ref, dx_ref, qg_ref, kg_ref, wq_ref, wkv_ref,
             dx3_ref, dwq_ref, dwkv_ref, dqg_ref, dkg_ref, qacc, kacc):
        i = pl.program_id(0)

        @pl.when(i == 0)
        def _():
            qacc[...] = jnp.zeros_like(qacc)
            kacc[...] = jnp.zeros_like(kacc)
            dqg_ref[...] = jnp.zeros_like(dqg_ref)
            dkg_ref[...] = jnp.zeros_like(dkg_ref)

        x = x_ref[...]
        qg = qg_ref[1:2, :]
        kg = kg_ref[...]
        dq_v = dq_ref[...]
        dkv_v = dkv_ref[...].astype(BF)
        qacc[...] += _dot_tn(_rms_fwd(x, qg).astype(BF), dq_v)
        kacc[...] += _dot_tn(_rms_fwd(x, kg).astype(BF), dkv_v)
        dxq, prod_q = _rms_bwd(x, qg, _dot_nt(dq_v, wq_ref[...]))
        dxk, prod_k = _rms_bwd(x, kg, _dot_nt(dkv_v, wkv_ref[...]))
        dqg_ref[...] += _rowsum(prod_q)
        dkg_ref[...] += _rowsum(prod_k)
        dx3_ref[...] = dx_ref[...] + dxq + dxk

        @pl.when(i == n - 1)
        def _():
            dwq_ref[...] = qacc[...].astype(BF)
            dwkv_ref[...] = kacc[...].astype(BF)

    outs, _ = _call(
        body, name="qkv_bwd", grid=(n,),
        in_specs=[_row_spec(TM), _row_spec(TM, 2 * KVD), _row_spec(TM), _row_spec(TM), VSPEC, VSPEC, VSPEC, VSPEC],
        out_specs=[_row_spec(TM), _const_spec((D, D)), _const_spec((D, 2 * KVD)),
                   _const_spec((1, D)), _const_spec((1, D))],
        out_shape=[_sds((s_len, D)), _sds((D, D), BF), _sds((D, 2 * KVD), BF), _sds((1, D)), _sds((1, D))],
        scratch_shapes=[pltpu.VMEM((D, D), F32), pltpu.VMEM((D, 2 * KVD), F32)],
        args=[dq, dkv, x3, dx4, q_g, kv_g, w_q, w_kv])
    return outs


def _attn_group(i, q, kvw, sink_ref, g):
    rows = GQA * BLK
    heads = [GQA * g + j for j in range(GQA)]
    off = jnp.where(i > 0, BLK, 0)
    row = lax.broadcasted_iota(jnp.int32, (rows, 2 * BLK), 0)
    rel = (row % BLK) - lax.broadcasted_iota(jnp.int32, (rows, 2 * BLK), 1) + off
    valid = (rel >= 0) & (rel < BLK)
    head_of_row = lax.broadcasted_iota(jnp.int32, (rows, 1), 0) // BLK
    slope = jnp.zeros((rows, 1), F32)
    sink = jnp.zeros((rows, 1), F32)
    for j, h in enumerate(heads):
        slope = jnp.where(head_of_row == j, SLOPES[h], slope)
        sink = jnp.where(head_of_row == j, sink_ref[0, h], sink)
    qs = jnp.concatenate([q[:, h * HEAD_DIM:(h + 1) * HEAD_DIM] for h in heads], axis=0)
    k = kvw[:, g * HEAD_DIM:(g + 1) * HEAD_DIM]
    v = kvw[:, KVD + g * HEAD_DIM:KVD + (g + 1) * HEAD_DIM]
    s = _dot_nt(qs, k) * ATT_SCALE - slope * rel.astype(F32)
    s = jnp.where(valid, s, NEG_INF)
    m = jnp.maximum(jnp.max(s, axis=-1, keepdims=True), sink)
    e = jnp.exp(s - m)
    es = jnp.exp(sink - m)
    inv = 1.0 / (jnp.sum(e, axis=-1, keepdims=True) + es)
    return e * inv, es * inv, qs, k, v


def _unstack_heads(stacked):
    return [stacked[j * BLK:(j + 1) * BLK, :] for j in range(GQA)]


def _kv_window(kv_ref, i):
    ks = pl.multiple_of(jnp.maximum(i * BLK - BLK, 0), BLK)
    return ks, kv_ref[pl.ds(ks, 2 * BLK), :]


def _attn_fwd(q, kv, sinks, x3, w_o, post_g, rider=None):
    s_len = q.shape[0]

    def body(q_ref, kv_ref, sk_ref, x_ref, wo_ref, g_ref, a_ref, y_ref, x4_ref):
        i = pl.program_id(0)
        _, kvw = _kv_window(kv_ref, i)
        q = q_ref[...]
        outs = []
        for g in range(N_KV_HEADS):
            p, _, _, _, v = _attn_group(i, q, kvw, sk_ref, g)
            outs += _unstack_heads(_dot(p.astype(BF), v))
        attn = jnp.concatenate(outs, axis=1)
        a_ref[...] = attn
        y = _dot(attn.astype(BF), wo_ref[...])
        y_ref[...] = y
        x4_ref[...] = x_ref[...] + _rms_fwd(y, g_ref[1:2, :])

    return _call(body, name="attn_fwd", grid=(s_len // BLK,),
                 in_specs=[_row_spec(BLK), VSPEC, SSPEC, _row_spec(BLK), VSPEC, VSPEC],
                 out_specs=[_row_spec(BLK)] * 3, out_shape=[_sds((s_len, D))] * 3,
                 args=[q, kv, sinks, x3, w_o, post_g], rider=rider)


def _attn_bwd(dx4, y, attn, q, kv, sinks, w_o, post_g, rider=None):
    s_len = q.shape[0]
    n = s_len // BLK

    def body(dx_ref, y_ref, a_ref, q_ref, kv_ref, sk_ref, wo_ref, g_ref,
             dq_ref, dkv_ref, dwo_ref, dg_ref, dsk_ref, wacc):
        i = pl.program_id(0)

        @pl.when(i == 0)
        def _():
            dkv_ref[...] = jnp.zeros_like(dkv_ref)
            wacc[...] = jnp.zeros_like(wacc)
            dg_ref[...] = jnp.zeros_like(dg_ref)
            dsk_ref[...] = jnp.zeros_like(dsk_ref)

        dy, prod = _rms_bwd(y_ref[...], g_ref[1:2, :], dx_ref[...])
        dg_ref[...] += _rowsum(prod)
        dyb = dy.astype(BF)
        attn = a_ref[...]
        wacc[...] += _dot_tn(attn.astype(BF), dyb)
        d_o = _dot_nt(dyb, wo_ref[...])
        dod = d_o * attn
        ks, kvw = _kv_window(kv_ref, i)
        q = q_ref[...]
        lane = lax.broadcasted_iota(jnp.int32, (1, D), 1)
        dqs, dks, dvs = [], [], []
        dsk = jnp.zeros((1, D), F32)
        for g in range(N_KV_HEADS):
            p, ps, qs, k, v = _attn_group(i, q, kvw, sk_ref, g)
            cols = [slice((GQA * g + j) * HEAD_DIM, (GQA * g + j + 1) * HEAD_DIM) for j in range(GQA)]
            do_s = jnp.concatenate([d_o[:, c] for c in cols], axis=0).astype(BF)
            dsum = jnp.concatenate([jnp.sum(dod[:, c], axis=-1, keepdims=True) for c in cols], axis=0)
            dp = _dot_nt(do_s, v)
            dsb = (p * (dp - dsum) * ATT_SCALE).astype(BF)
            sink_part = ps * dsum
            for j in range(GQA):
                dsk = dsk + jnp.where(lane == GQA * g + j, -_rowsum(sink_part[j * BLK:(j + 1) * BLK, :]), 0.0)
            dqs += _unstack_heads(_dot(dsb, k))
            dks.append(_dot_tn(dsb, qs))
            dvs.append(_dot_tn(p.astype(BF), do_s))
        dsk_ref[...] += dsk
        dq_ref[...] = jnp.concatenate(dqs, axis=1).astype(BF)
        dkv_ref[pl.ds(ks, 2 * BLK), :] += jnp.concatenate(dks + dvs, axis=1)

        @pl.when(i == n - 1)
        def _():
            dwo_ref[...] = wacc[...].astype(BF)

    return _call(
        body, name="attn_bwd", grid=(n,),
        in_specs=[_row_spec(BLK), _row_spec(BLK), _row_spec(BLK), _row_spec(BLK), VSPEC, SSPEC, VSPEC, VSPEC],
        out_specs=[_row_spec(BLK), _const_spec((s_len, 2 * KVD)), _const_spec((D, D)),
                   _const_spec((1, D)), _const_spec((1, D))],
        out_shape=[_sds((s_len, D), BF), _sds((s_len, 2 * KVD)), _sds((D, D), BF), _sds((1, D)), _sds((1, D))],
        scratch_shapes=[pltpu.VMEM((D, D), F32)],
        args=[dx4, y, attn, q, kv, sinks, w_o, post_g], rider=rider)


Big = collections.namedtuple("Big", "name src layer L A R C rb")


def _bigs():
    out = {"pool_w": Big("pool_w", "pool_w", None, 4, 4, POOL_G // N_CHIPS, POOL_G, 32)}
    for l in range(2):
        out[f"w_gu{l}"] = Big(f"w_gu{l}", "w_gu", l, 1, 2, D, FF_HALF, 256)
        out[f"w_down{l}"] = Big(f"w_down{l}", "w_down", l, 1, 4, FF // N_CHIPS, D, 352)
        out[f"w_ple_gate{l}"] = Big(f"w_ple_gate{l}", "w_ple_gate", l, 1, 4, D // N_CHIPS, D, 128)
        out[f"w_ple_proj{l}"] = Big(f"w_ple_proj{l}", "w_ple_proj", l, 1, 1, PLE, D // N_CHIPS, 128)
    out["w_q"] = Big("w_q", "w_q", None, 1, 4, D // N_CHIPS, D, 128)
    out["w_o"] = Big("w_o", "w_o", None, 1, 4, D // N_CHIPS, D, 128)
    out["w_kv"] = Big("w_kv", "w_kv", None, 1, 4, D // N_CHIPS, 2 * KVD, 128)
    return out


BIGS = _bigs()
POOL_SCALE = Big("pool_scale", "pool_scale", None, 1, 1, 1, D // N_CHIPS, 1)
BIG_SOURCES = ("w_gu", "w_down", "w_ple_gate", "w_ple_proj", "w_q", "w_o", "w_kv", "pool_w")


def _ncb(t):
    return N_CHIPS // t.A


def _full_shape(t, rows=None):
    return (t.L, t.A, t.R if rows is None else rows, _ncb(t) * t.C)


def _slot_index(t, k):
    return k // _ncb(t), k % _ncb(t)


def _slot(ref, t, k, row0, rows):
    a, cb = _slot_index(t, k)
    return ref.at[:, a, pl.ds(row0, rows), pl.ds(pl.multiple_of(cb * t.C, 128), t.C)]


def _place(t, w, kc, out_dtype):
    rb = min(t.R, 2 * t.rb)

    def body(kc_ref, w_ref, o_ref):
        del kc_ref
        o_ref[...] = w_ref[...].astype(out_dtype)

    def in_map(l, j, kc_ref):
        return (l if t.layer is None else t.layer, j, 0)

    def out_map(l, j, kc_ref):
        a, cb = _slot_index(t, kc_ref[0])
        return (l, a, j, cb)

    return pl.pallas_call(
        body, name=f"place_{t.name}",
        grid_spec=pltpu.PrefetchScalarGridSpec(
            num_scalar_prefetch=1, grid=(t.L, t.R // rb),
            in_specs=[pl.BlockSpec((None, rb, t.C), in_map)],
            out_specs=pl.BlockSpec((None, None, rb, t.C), out_map)),
        out_shape=_sds(_full_shape(t), out_dtype),
        compiler_params=_params(2),
    )(kc, w)


def _place_many(ts, ws, kc, rider):
    n = 8

    def blocks_of(t):
        return next(nb for nb in (8, 4, 2, 1) if t.R % (16 * nb) == 0)

    def body(kc_ref, *refs):
        del kc_ref
        s = pl.program_id(0)
        for ti, t in enumerate(ts):
            @pl.when(s < blocks_of(t))
            def _():
                refs[len(ts) + ti][...] = refs[ti][...].astype(BF)

    in_specs, out_specs = [], []
    for t in ts:
        assert t.L == 1
        nb = blocks_of(t)
        rb = t.R // nb

        def in_map(s, kc_ref, t=t, nb=nb):
            return (0 if t.layer is None else t.layer, jnp.minimum(s, nb - 1), 0)

        def out_map(s, kc_ref, t=t, nb=nb):
            a, cb = _slot_index(t, kc_ref[0])
            return (0, a, jnp.minimum(s, nb - 1), cb)

        in_specs.append(pl.BlockSpec((None, rb, t.C), in_map))
        out_specs.append(pl.BlockSpec((None, None, rb, t.C), out_map))
    return _call(body, name="place_rest", grid=(n,), in_specs=in_specs, out_specs=out_specs,
                 out_shape=[_sds(_full_shape(t), BF) for t in ts], args=list(ws), rider=rider, prefetch=kc)


def _mesh_position():
    x, y, c = lax.axis_index("x"), lax.axis_index("y"), lax.axis_index("c")
    chips = [(1 - x, y), (x, 1 - y), (1 - x, 1 - y)]
    return x, y, c, chips


def _gather_rider(parts, fulls):
    nt = len(parts)
    TO_X, TO_Y, FWD_X, FWD_Y, SIB_X, SIB_Y, SIB_D = range(7)

    def rows_of(ti, core):
        t, r0, r1 = parts[ti]
        h = (r1 - r0) // 2
        return r0 + core * h, h

    def copy(outs, sems, kind, ti, k_src, row0, rows, dev):
        region = _slot(outs[ti], parts[ti][0], k_src, row0, rows)
        return pltpu.make_async_remote_copy(region, region, sems[0].at[ti, kind], sems[1].at[ti, kind],
                                            device_id=dev, device_id_type=MESH)

    def plan(outs, sems):
        x, y, c, _ = _mesh_position()
        me, kx, ky, kd = 2 * x + y, 2 * (1 - x) + y, 2 * x + (1 - y), 2 * (1 - x) + (1 - y)
        dev_x, dev_y, dev_d, sib = (1 - x, y, c), (x, 1 - y, c), (1 - x, 1 - y, c), (x, y, 1 - c)

        def whole(ti):
            return 0, parts[ti][0].R

        def mk(kind, k_send, k_recv, dev, send_rows, recv_rows):
            def build(ti, side):
                k_src = k_send if side == "s" else k_recv
                row0, rows = (send_rows if side == "s" else recv_rows)(ti)
                return copy(outs, sems, kind, ti, k_src, row0, rows, dev)
            return build

        def first_half(core):
            return lambda ti: (rows_of(ti, core)[0], rows_of(ti, core)[1] // 2)

        def second_half(core):
            return lambda ti: (rows_of(ti, core)[0] + rows_of(ti, core)[1] // 2, rows_of(ti, core)[1] // 2)

        mine = lambda ti: rows_of(ti, c)
        theirs = lambda ti: rows_of(ti, 1 - c)
        split = {
            TO_X: mk(TO_X, me, kx, dev_x, mine, mine),
            TO_Y: mk(TO_Y, me, ky, dev_y, mine, mine),
            FWD_X: mk(FWD_X, ky, kd, dev_x, first_half(c), first_half(c)),
            FWD_Y: mk(FWD_Y, kx, kd, dev_y, second_half(c), second_half(c)),
            SIB_X: mk(SIB_X, kx, kx, sib, mine, theirs),
            SIB_Y: mk(SIB_Y, ky, ky, sib, mine, theirs),
            SIB_D: mk(SIB_D, kd, kd, sib, mine, theirs),
        }
        direct = {
            TO_X: mk(TO_X, me, kx, dev_x, whole, whole),
            TO_Y: mk(TO_Y, me, ky, dev_y, whole, whole),
            FWD_X: mk(FWD_X, me, kd, dev_d, whole, whole),
        }
        return split, direct

    is_split = [t.R > 1 for t, _, _ in parts]

    def start(ins, outs, sems):
        split, direct = plan(outs, sems)
        for ti in range(nt):
            kinds = split if is_split[ti] else direct
            kinds[TO_X](ti, "s").start()
            kinds[TO_Y](ti, "s").start()
            if not is_split[ti]:
                kinds[FWD_X](ti, "s").start()

    def mid(ins, outs, sems):
        split, _ = plan(outs, sems)
        for ti in range(nt):
            if is_split[ti]:
                split[TO_Y](ti, "r").wait_recv()
                split[FWD_X](ti, "s").start()
                split[SIB_Y](ti, "s").start()
        for ti in range(nt):
            if is_split[ti]:
                split[TO_X](ti, "r").wait_recv()
                split[FWD_Y](ti, "s").start()
                split[SIB_X](ti, "s").start()

    def finish(ins, outs, sems):
        split, direct = plan(outs, sems)
        for ti in range(nt):
            if is_split[ti]:
                split[FWD_X](ti, "r").wait_recv()
                split[FWD_Y](ti, "r").wait_recv()
                split[SIB_D](ti, "s").start()
            else:
                for kind in (TO_X, TO_Y, FWD_X):
                    direct[kind](ti, "r").wait_recv()
        for ti in range(nt):
            if is_split[ti]:
                for kind in (SIB_X, SIB_Y, SIB_D):
                    split[kind](ti, "r").wait_recv()
        for ti in range(nt):
            kinds = split if is_split[ti] else direct
            for kind in kinds:
                kinds[kind](ti, "s").wait_send()

    sems = pltpu.SemaphoreType.DMA((nt, 7))
    return Rider(list(fulls), [_sds(a.shape, a.dtype) for a in fulls], {i: i for i in range(nt)},
                 [sems, sems], start, mid, finish)


def _pair_exchange(name, specs, grads):
    nt = len(specs)

    def body(*refs):
        gs = refs[:nt]
        lands = refs[nt:2 * nt]
        send, recv = refs[2 * nt:]
        x, y, c, _ = _mesh_position()
        cps = []
        for ti, t in enumerate(specs):
            half = t.R // 2
            cp = pltpu.make_async_remote_copy(gs[ti].at[:, :, pl.ds((1 - c) * half, half), :], lands[ti],
                                              send.at[ti], recv.at[ti],
                                              device_id=(x, y, 1 - c), device_id_type=MESH)
            cp.start()
            cps.append(cp)
        for cp in cps:
            cp.wait()

    return pl.pallas_call(
        body, name=name,
        in_specs=[ANYSPEC] * nt, out_specs=[ANYSPEC] * nt,
        out_shape=[_sds(_full_shape(t, t.R // 2), BF) for t in specs],
        scratch_shapes=[pltpu.SemaphoreType.DMA((nt,)), pltpu.SemaphoreType.DMA((nt,))],
        compiler_params=_params(),
    )(*grads)


def _pair_sum(t, g, land, kc):
    half = t.R // 2
    nj = half // t.rb
    w = _ncb(t) * t.C

    def body(kc_ref, g_ref, l_ref, o_ref):
        del kc_ref
        o_ref[...] = (g_ref[...].astype(F32) + l_ref[...].astype(F32)).astype(BF)

    return pl.pallas_call(
        body, name=f"pair_sum_{t.name}",
        grid_spec=pltpu.PrefetchScalarGridSpec(
            num_scalar_prefetch=1, grid=(t.L, nj),
            in_specs=[pl.BlockSpec((None, t.A, t.rb, w), lambda l, j, kc_ref: (l, 0, kc_ref[1] * nj + j, 0)),
                      pl.BlockSpec((None, t.A, t.rb, w), lambda l, j, kc_ref: (l, 0, j, 0))],
            out_specs=pl.BlockSpec((None, t.A, t.rb, w), lambda l, j, kc_ref: (l, 0, j, 0))),
        out_shape=_sds(_full_shape(t, half), BF),
        compiler_params=_params(2),
    )(kc, g, land)


def _scatter_rider(specs, sums):
    nt = len(specs)

    def copy(ins, outs, sems, ti, j, chip, c):
        t = specs[ti]
        cx, cy = chip
        return pltpu.make_async_remote_copy(_slot(ins[ti], t, 2 * cx + cy, 0, t.R // 2), outs[ti].at[j],
                                            sems[0].at[ti, j], sems[1].at[ti, j],
                                            device_id=(cx, cy, c), device_id_type=MESH)

    def start(ins, outs, sems):
        _, _, c, chips = _mesh_position()
        for j, chip in enumerate(chips):
            for ti in range(nt):
                copy(ins, outs, sems, ti, j, chip, c).start()

    def finish(ins, outs, sems):
        _, _, c, chips = _mesh_position()
        for j, chip in enumerate(chips):
            for ti in range(nt):
                copy(ins, outs, sems, ti, j, chip, c).wait()

    sems = pltpu.SemaphoreType.DMA((nt, N_CHIPS - 1))
    return Rider(list(sums), [_sds((N_CHIPS - 1, t.L, t.R // 2, t.C), BF) for t in specs], {}, [sems, sems],
                 start, None, finish)


def _chip_sum(t, s, land, kc, n_layers, prev):
    half = t.R // 2
    nj = half // t.rb

    def body(*refs):
        s_ref, l_ref, o_ref = refs[1], refs[2], refs[-1]
        acc = s_ref[...].astype(F32)
        for j in range(N_CHIPS - 1):
            acc = acc + l_ref[j].astype(F32)
        o_ref[...] = acc

    def own_map(l, j, kc_ref):
        a, cb = _slot_index(t, kc_ref[0])
        return (l, a, j, cb)

    def out_map(l, j, kc_ref):
        return (l if t.layer is None else t.layer, kc_ref[1] * nj + j, 0)

    in_specs = [pl.BlockSpec((None, None, t.rb, t.C), own_map),
                pl.BlockSpec((N_CHIPS - 1, None, t.rb, t.C), lambda l, j, kc_ref: (0, l, j, 0))]
    args = [kc, s, land]
    aliases = {}
    if prev is not None:
        in_specs.append(ANYSPEC)
        args.append(prev)
        aliases = {3: 0}
    return pl.pallas_call(
        body, name=f"chip_sum_{t.name}",
        grid_spec=pltpu.PrefetchScalarGridSpec(
            num_scalar_prefetch=1, grid=(t.L, nj), in_specs=in_specs,
            out_specs=pl.BlockSpec((None, t.rb, t.C), out_map)),
        out_shape=_sds((n_layers, t.R, t.C)),
        input_output_aliases=aliases,
        compiler_params=_params(2),
    )(*args)


def _chip_sum_fused(t, own, land, kc, n_layers, prev, by_cols):
    if by_cols:
        rows, cols = own.shape
    else:
        nb, rows, bw = own.shape
        cols = nb * bw
    nj = rows // t.rb

    def body(*refs):
        o_ref, l_ref, out_ref = refs[1], refs[2], refs[-1]
        acc = o_ref[...].astype(F32)
        for j in range(N_CHIPS - 1):
            acc = acc + l_ref[j].astype(F32)
        out_ref[...] = acc if by_cols else jnp.concatenate([acc[b] for b in range(nb)], axis=1)

    def out_map(j, kc_ref):
        return (t.layer, j, kc_ref[1]) if by_cols else (t.layer, kc_ref[1] * nj + j, 0)

    if by_cols:
        in_specs = [pl.BlockSpec((t.rb, cols), lambda j, kc_ref: (j, 0)),
                    pl.BlockSpec((N_CHIPS - 1, t.rb, cols), lambda j, kc_ref: (0, j, 0))]
    else:
        in_specs = [pl.BlockSpec((nb, t.rb, bw), lambda j, kc_ref: (0, j, 0)),
                    pl.BlockSpec((N_CHIPS - 1, nb, t.rb, bw), lambda j, kc_ref: (0, 0, j, 0))]
    args = [kc, own, land]
    aliases = {}
    if prev is not None:
        in_specs.append(ANYSPEC)
        args.append(prev)
        aliases = {3: 0}
    return pl.pallas_call(
        body, name=f"chip_sum_{t.name}",
        grid_spec=pltpu.PrefetchScalarGridSpec(
            num_scalar_prefetch=1, grid=(nj,), in_specs=in_specs,
            out_specs=pl.BlockSpec((None, t.rb, cols), out_map)),
        out_shape=_sds((n_layers, t.R, t.C)),
        input_output_aliases=aliases,
        compiler_params=_params(1),
    )(*args)


def _pair_share(halves, by_cols):
    nt = len(halves)

    def part(ref, ti, core):
        axis = 2 if by_cols[ti] else 1
        half = halves[ti].shape[axis] // 2
        piece = pl.ds(pl.multiple_of(core * half, 128 if by_cols[ti] else 8), half)
        return ref.at[:, :, piece] if by_cols[ti] else ref.at[:, piece, :]

    def body(*refs):
        outs = refs[nt:2 * nt]
        send, recv = refs[2 * nt:]
        x, y, c, _ = _mesh_position()
        cps = []
        for ti in range(nt):
            mine = part(outs[ti], ti, c)
            cp = pltpu.make_async_remote_copy(mine, mine, send.at[ti], recv.at[ti],
                                              device_id=(x, y, 1 - c), device_id_type=MESH)
            cp.start()
            cps.append(cp)
        for ti in range(nt):
            theirs = part(outs[ti], ti, 1 - c)
            pltpu.make_async_remote_copy(theirs, theirs, send.at[ti], recv.at[ti],
                                         device_id=(x, y, 1 - c), device_id_type=MESH).wait_recv()
        for cp in cps:
            cp.wait_send()

    return pl.pallas_call(
        body, name="grads_pair_share",
        in_specs=[ANYSPEC] * nt, out_specs=[ANYSPEC] * nt,
        out_shape=[_sds(a.shape, a.dtype) for a in halves],
        scratch_shapes=[pltpu.SemaphoreType.DMA((nt,)), pltpu.SemaphoreType.DMA((nt,))],
        input_output_aliases={i: i for i in range(nt)},
        compiler_params=_params(),
    )(*halves)


def _adamw_math(w, g, m, v):
    m = B1 * m + (1.0 - B1) * g
    v = B2 * v + (1.0 - B2) * (g * g)
    delta = -LR * ((m / BC1) / (jnp.sqrt(v / BC2) + AEPS) + WD * w)
    return delta, m, v


def _adamw(name, rb, w, g, m, v):
    n_layers, r, c = w.shape

    def body(w_ref, g_ref, m_ref, v_ref, go_ref, d_ref, nm_ref, nv_ref):
        g_v = g_ref[...]
        go_ref[...] = g_v
        d_ref[...], nm_ref[...], nv_ref[...] = _adamw_math(w_ref[...], g_v, m_ref[...], v_ref[...])

    spec = pl.BlockSpec((None, rb, c), lambda l, j: (l, j, 0))
    return pl.pallas_call(
        body, name=f"adamw_{name}", grid=(n_layers, r // rb),
        in_specs=[spec] * 4, out_specs=[spec] * 4, out_shape=[_sds(w.shape)] * 4,
        compiler_params=_params(2),
    )(w, g, m, v)


GAIN_ROWS = {"pre_mix_g": 0, "post_mix_g": 2, "pre_ffn_g": 4, "post_ffn_g": 6, "ple_g": 8, "ple_post_g": 10}
ROW_KV_G, ROW_POOL_SCALE, ROW_SINKS, ROW_LOSS, PACK_ROWS = 12, 13, 14, 15, 16
SMALL_NAMES = tuple(GAIN_ROWS) + ("kv_g", "pool_scale", "sinks")


def _small_all_reduce(rows, dpool, rider=None):
    ng, pr = len(WINDOWS), POOL_G // N_CHIPS

    def body(*refs):
        row_refs = refs[:PACK_ROWS]
        dpool_ref, tot_ref, gpool_ref, pack, land, pland, send, recv, psend, precv = refs[PACK_ROWS:]
        x, y, c, _ = _mesh_position()
        me = 4 * x + 2 * y + c
        for r in range(PACK_ROWS):
            pack[r:r + 1, :] = row_refs[r][...]

        def shard_of(k):
            return dpool_ref.at[:, pl.ds(pl.multiple_of(k * pr, pr), pr), :]

        cps = []
        for j in range(1, N_DEV):
            px, py, pc = x ^ (j >> 2), y ^ ((j >> 1) & 1), c ^ (j & 1)
            cps.append(pltpu.make_async_remote_copy(pack, land.at[me], send.at[j], recv.at[j],
                                                    device_id=(px, py, pc), device_id_type=MESH))
            cps.append(pltpu.make_async_remote_copy(shard_of(2 * px + py), pland.at[me], psend.at[j], precv.at[j],
                                                    device_id=(px, py, pc), device_id_type=MESH))
        for cp in cps:
            cp.start()
        land[me] = pack[...]
        pland[me] = dpool_ref[:, pl.ds(pl.multiple_of((2 * x + y) * pr, pr), pr), :]
        for j in range(1, N_DEV):
            pltpu.make_async_remote_copy(pack, land.at[me ^ j], send.at[j], recv.at[j],
                                         device_id=(x, y, c), device_id_type=MESH).wait_recv()
            pltpu.make_async_remote_copy(shard_of(0), pland.at[me ^ j], psend.at[j], precv.at[j],
                                         device_id=(x, y, c), device_id_type=MESH).wait_recv()
        for cp in cps:
            cp.wait_send()
        tot = land[0]
        gp = pland[0].astype(F32)
        for d in range(1, N_DEV):
            tot = tot + land[d]
            gp = gp + pland[d].astype(F32)
        tot_ref[...] = tot
        gpool_ref[...] = gp

    sems = pltpu.SemaphoreType.DMA((N_DEV,))
    return _call(
        body, name="small_all_reduce", grid=(1,),
        in_specs=[VSPEC] * (PACK_ROWS + 1), out_specs=[VSPEC, VSPEC],
        out_shape=[_sds((PACK_ROWS, D)), _sds((ng, pr, POOL_G))],
        scratch_shapes=[pltpu.VMEM((PACK_ROWS, D), F32), pltpu.VMEM((N_DEV, PACK_ROWS, D), F32),
                        pltpu.VMEM((N_DEV, ng, pr, POOL_G), BF), sems, sems, sems, sems],
        args=[*rows, dpool], rider=rider)


def _small_adamw(tot, kc, small_w, small_m, small_v):
    names = SMALL_NAMES
    n = len(names)

    def body(*refs):
        tot_ref, kc_ref = refs[0], refs[1]
        w_refs = dict(zip(names, refs[2:2 + n]))
        m_refs = dict(zip(names, refs[2 + n:2 + 2 * n]))
        v_refs = dict(zip(names, refs[2 + 2 * n:2 + 3 * n]))
        loss_ref = refs[2 + 3 * n]
        out_refs = {nm: refs[3 + 3 * n + 4 * k: 7 + 3 * n + 4 * k] for k, nm in enumerate(names)}
        tot = tot_ref[...]
        loss_ref[...] = 0.5 * jnp.sum(tot[ROW_LOSS:ROW_LOSS + 1, :], axis=-1, keepdims=True) * (1.0 / D)

        def update(nm, g):
            g_ref, d_ref, nm_ref, nv_ref = out_refs[nm]
            g_ref[...] = g
            d_ref[...], nm_ref[...], nv_ref[...] = _adamw_math(w_refs[nm][...], g, m_refs[nm][...], v_refs[nm][...])

        for nm, r in GAIN_ROWS.items():
            update(nm, tot[r:r + 2, :])
        update("kv_g", tot[ROW_KV_G:ROW_KV_G + 1, :])
        k = kc_ref[0]
        width = D // N_CHIPS
        g_scale = jnp.zeros((1, width), F32)
        for kk in range(N_CHIPS):
            g_scale = g_scale + jnp.where(k == kk, tot[ROW_POOL_SCALE:ROW_POOL_SCALE + 1, kk * width:(kk + 1) * width], 0.0)
        update("pool_scale", g_scale)
        update("sinks", tot[ROW_SINKS:ROW_SINKS + 1, 0:N_HEADS])

    ins = [tot, kc] + [small_w[nm] for nm in names] + [small_m[nm] for nm in names] + [small_v[nm] for nm in names]
    out_shape = [_sds((1, 1))]
    for nm in names:
        out_shape += [_sds(small_w[nm].shape)] * 4
    outs = pl.pallas_call(
        body, name="small_adamw",
        in_specs=[VSPEC, SSPEC] + [VSPEC] * (3 * n), out_specs=[VSPEC] * len(out_shape), out_shape=out_shape,
        compiler_params=_params(),
    )(*ins)
    return outs[0], {nm: outs[1 + 4 * k: 5 + 4 * k] for k, nm in enumerate(names)}


def _compute_layout(t, full):
    if t.src == "w_gu":
        return full.reshape(2, D, FF)
    if t.src == "pool_w":
        return full.reshape(len(WINDOWS), POOL_G, POOL_G)
    if t.src == "pool_scale":
        return full.reshape(1, D)
    return full.reshape(t.A * t.R, _ncb(t) * t.C)


def kernel(x, p, pre_mix_g, post_mix_g, pre_ffn_g, post_ffn_g, pool_w, pool_scale, kv_g, w_kv, w_q, sinks, w_o, w_gu, w_down, ple_g, w_ple_gate, w_ple_proj, ple_post_g, loss_target, m_pre_mix_g, m_post_mix_g, m_pre_ffn_g, m_post_ffn_g, m_pool_w, m_pool_scale, m_kv_g, m_w_kv, m_w_q, m_sinks, m_w_o, m_w_gu, m_w_down, m_ple_g, m_w_ple_gate, m_w_ple_proj, m_ple_post_g, v_pre_mix_g, v_post_mix_g, v_pre_ffn_g, v_post_ffn_g, v_pool_w, v_pool_scale, v_kv_g, v_w_kv, v_w_q, v_sinks, v_w_o, v_w_gu, v_w_down, v_ple_g, v_w_ple_gate, v_w_ple_proj, v_ple_post_g):
    weights = dict(pre_mix_g=pre_mix_g, post_mix_g=post_mix_g, pre_ffn_g=pre_ffn_g, post_ffn_g=post_ffn_g,
                   pool_w=pool_w, pool_scale=pool_scale, kv_g=kv_g, w_kv=w_kv, w_q=w_q, sinks=sinks, w_o=w_o,
                   w_gu=w_gu, w_down=w_down, ple_g=ple_g, w_ple_gate=w_ple_gate, w_ple_proj=w_ple_proj,
                   ple_post_g=ple_post_g)
    m_in = dict(pre_mix_g=m_pre_mix_g, post_mix_g=m_post_mix_g, pre_ffn_g=m_pre_ffn_g, post_ffn_g=m_post_ffn_g,
                pool_w=m_pool_w, pool_scale=m_pool_scale, kv_g=m_kv_g, w_kv=m_w_kv, w_q=m_w_q, sinks=m_sinks,
                w_o=m_w_o, w_gu=m_w_gu, w_down=m_w_down, ple_g=m_ple_g, w_ple_gate=m_w_ple_gate,
                w_ple_proj=m_w_ple_proj, ple_post_g=m_ple_post_g)
    v_in = dict(pre_mix_g=v_pre_mix_g, post_mix_g=v_post_mix_g, pre_ffn_g=v_pre_ffn_g, post_ffn_g=v_post_ffn_g,
                pool_w=v_pool_w, pool_scale=v_pool_scale, kv_g=v_kv_g, w_kv=v_w_kv, w_q=v_w_q, sinks=v_sinks,
                w_o=v_w_o, w_gu=v_w_gu, w_down=v_w_down, ple_g=v_ple_g, w_ple_gate=v_w_ple_gate,
                w_ple_proj=v_w_ple_proj, ple_post_g=v_ple_post_g)
    order = ["pre_mix_g", "post_mix_g", "pre_ffn_g", "post_ffn_g", "pool_w", "pool_scale", "kv_g", "w_kv", "w_q",
             "sinks", "w_o", "w_gu", "w_down", "ple_g", "w_ple_gate", "w_ple_proj", "ple_post_g"]

    kc = jnp.stack([2 * lax.axis_index("x") + lax.axis_index("y"), lax.axis_index("c")]).astype(jnp.int32)
    s_len = x.shape[1]
    x2d = x.reshape(s_len, D)
    p3d = p.reshape(2, s_len, PLE)
    target = loss_target.reshape(s_len, D)
    kv_g2d = kv_g.reshape(1, D)
    gains = {nm: weights[nm] for nm in GAIN_ROWS}

    def shard_view(src, a):
        t = next(t for t in BIGS.values() if t.src == src)
        return a.reshape(-1, t.R, t.C)

    first = ["pool_w", "pool_scale", "w_gu0", "w_down0"]
    rest = [nm for nm in BIGS if nm not in first]
    specs = dict(BIGS, pool_scale=POOL_SCALE)
    placed = {nm: _place(BIGS[nm], shard_view(BIGS[nm].src, weights[BIGS[nm].src]), kc, BF)
              for nm in first if nm in BIGS}
    placed["pool_scale"] = _place(POOL_SCALE, pool_scale.reshape(1, 1, D // N_CHIPS), kc, F32)

    def gather(names, rows=None):
        rows = rows or {}
        parts = [(specs[nm],) + tuple(rows.get(nm, (0, specs[nm].R))) for nm in names]
        return _gather_rider(parts, [placed[nm] for nm in names])

    def take(names, results):
        for nm, a in zip(names, results):
            placed[nm] = a

    def weight(nm):
        return _compute_layout(specs[nm], placed[nm])

    cast, got = _place_many([BIGS[nm] for nm in rest], [shard_view(BIGS[nm].src, weights[BIGS[nm].src]) for nm in rest],
                            kc, rider=gather(first))
    take(rest, cast)
    take(first, got)


    y0, x1 = _mixa_fwd(x2d, gains["pre_mix_g"], weight("pool_w"), weight("pool_scale"), gains["post_mix_g"])

    ride = ["w_ple_gate0", "w_ple_proj0", "w_q", "w_kv", "w_o", "w_gu1"]
    (f0, x2), got = _ffn_fwd(0, x1, gains["pre_ffn_g"], weight("w_gu0"), weight("w_down0"), gains["post_ffn_g"],
                             rider=gather(ride, {"w_gu1": (0, 320)}))
    take(ride, got)

    ride = ["w_ple_gate1", "w_ple_proj1", "w_gu1"]
    (z0, pe0, x3, q, kv), got = _ple_fwd(
        0, x2, p3d, gains["ple_g"], weight("w_ple_gate0"), weight("w_ple_proj0"), gains["ple_post_g"],
        qkv=(gains["pre_mix_g"], kv_g2d, weight("w_q"), weight("w_kv")),
        rider=gather(ride, {"w_gu1": (320, 704)}))
    take(ride, got)

    ride = ["w_down1", "w_gu1"]
    (attn, y1, x4), got = _attn_fwd(q, kv, sinks, x3, weight("w_o"), gains["post_mix_g"],
                                    rider=gather(ride, {"w_gu1": (704, D)}))
    take(ride, got)

    (f1, x5), _ = _ffn_fwd(1, x4, gains["pre_ffn_g"], weight("w_gu1"), weight("w_down1"), gains["post_ffn_g"])
    (z1, pe1, dx6, loss_row), _ = _ple_fwd(1, x5, p3d, gains["ple_g"], weight("w_ple_gate1"), weight("w_ple_proj1"),
                                           gains["ple_post_g"], target=target)

    local = {}
    landed = {}
    fused = {}

    def pair_stage(tag, names):
        ts = [BIGS[nm] for nm in names]
        gs = [local[nm].reshape(_full_shape(t)) for nm, t in zip(names, ts)]
        lands = _pair_exchange(f"grads_pair_exchange_{tag}", ts, gs)
        return [_pair_sum(t, g, l, kc) for t, g, l in zip(ts, gs, lands)]

    def scatter(names, sums):
        return _scatter_rider([BIGS[nm] for nm in names], sums)

    def keep(names, sums, got):
        for nm, s, l in zip(names, sums, got):
            landed[nm] = (s, l)

    (dx5, local["w_ple_gate1"], local["w_ple_proj1"], d_ple1, d_plepost1), _ = _ple_bwd(
        1, dx6, x5, z1, pe1, p3d, gains["ple_g"], weight("w_ple_gate1"), gains["ple_post_g"])

    group_a = ["w_ple_gate1", "w_ple_proj1"]
    sums_a = pair_stage("a", group_a)
    (dx4, d_preffn1, d_postffn1, *scattered), _ = _ffn_bwd(
        1, dx5, x4, f1, gains["pre_ffn_g"], weight("w_gu1"), weight("w_down1"), gains["post_ffn_g"], kc)
    fused["w_gu1"], fused["w_down1"] = scattered[0:2], scattered[2:4]

    (dq, dkv, local["w_o"], d_postmix1, d_sinks), got = _attn_bwd(
        dx4, y1, attn, q, kv, sinks, weight("w_o"), gains["post_mix_g"], rider=scatter(group_a, sums_a))
    keep(group_a, sums_a, got)
    dx3, local["w_q"], local["w_kv"], d_premix1, d_kvg = _qkv_bwd(
        dq, dkv, x3, dx4, gains["pre_mix_g"], kv_g2d, weight("w_q"), weight("w_kv"))

    group_b = ["w_o", "w_q", "w_kv"]
    sums_b = pair_stage("b", group_b)
    (dx2, local["w_ple_gate0"], local["w_ple_proj0"], d_ple0, d_plepost0), got = _ple_bwd(
        0, dx3, x2, z0, pe0, p3d, gains["ple_g"], weight("w_ple_gate0"), gains["ple_post_g"],
        rider=scatter(group_b, sums_b))
    keep(group_b, sums_b, got)

    group_c = ["w_ple_gate0", "w_ple_proj0"]
    sums_c = pair_stage("c", group_c)
    (dx1, d_preffn0, d_postffn0, *scattered), _ = _ffn_bwd(
        0, dx2, x1, f0, gains["pre_ffn_g"], weight("w_gu0"), weight("w_down0"), gains["post_ffn_g"], kc)
    fused["w_gu0"], fused["w_down0"] = scattered[0:2], scattered[2:4]

    (dx0, d_pool, d_scale, d_postmix0, d_premix0), _ = _mixa_bwd(
        dx1, x2d, y0, gains["pre_mix_g"], weight("pool_w"), weight("pool_scale"), gains["post_mix_g"])

    rows = [d_premix0, d_premix1, d_postmix0, d_postmix1, d_preffn0, d_preffn1, d_postffn0, d_postffn1,
            d_ple0, d_ple1, d_plepost0, d_plepost1, d_kvg, d_scale, d_sinks, loss_row]
    as2d = lambda a: a.reshape(1, D) if a.ndim == 1 else a
    (tot, g_pool), got = _small_all_reduce(rows, d_pool, rider=scatter(group_c, sums_c))
    keep(group_c, sums_c, got)
    loss, small = _small_adamw(tot, kc, {nm: as2d(weights[nm]) for nm in SMALL_NAMES},
                               {nm: as2d(m_in[nm]) for nm in SMALL_NAMES},
                               {nm: as2d(v_in[nm]) for nm in SMALL_NAMES})

    shared = [src for src in BIG_SOURCES if src != "pool_w"]
    halves = []
    for src in shared:
        n_layers = shard_view(src, weights[src]).shape[0]
        acc = None
        for t in [t for t in BIGS.values() if t.src == src]:
            if t.name in fused:
                own, land = fused[t.name]
                acc = _chip_sum_fused(t, own, land, kc, n_layers, acc, by_cols=src == "w_down")
            else:
                s, l = landed[t.name]
                acc = _chip_sum(t, s, l, kc, n_layers, acc)
        halves.append(acc)
    full_grads = dict(zip(shared, _pair_share(halves, [src == "w_down" for src in shared])))
    full_grads["pool_w"] = g_pool

    out = {"grad": {}, "delta": {}, "new_m": {}, "new_v": {}}
    for src in BIG_SOURCES:
        g = full_grads[src]
        t = next(t for t in BIGS.values() if t.src == src)
        res = _adamw(src, t.rb, shard_view(src, weights[src]), g, shard_view(src, m_in[src]), shard_view(src, v_in[src]))
        shape = weights[src].shape
        for kind, a in zip(("grad", "delta", "new_m", "new_v"), res):
            out[kind][src] = a.reshape(shape)
    for nm in SMALL_NAMES:
        shape = weights[nm].shape
        for kind, a in zip(("grad", "delta", "new_m", "new_v"), small[nm]):
            out[kind][nm] = a.reshape(shape)

    return (loss.reshape(()), dx0.reshape(x.shape),
            *[out["grad"][nm] for nm in order], *[out["delta"][nm] for nm in order],
            *[out["new_m"][nm] for nm in order], *[out["new_v"][nm] for nm in order])
```

```python
import collections

import jax
import jax.numpy as jnp
from jax import lax
from jax.experimental import pallas as pl
from jax.experimental.pallas import tpu as pltpu

D = 1024
FF = 2816
N_HEADS = 16
HEAD_DIM = 64
N_KV_HEADS = 4
GQA = N_HEADS // N_KV_HEADS
KVD = N_KV_HEADS * HEAD_DIM
PLE = 256
BLK = 128
WINDOWS = (2, 4, 8, 16)
POOL_G = 256
HALO = 16
EPS = 1e-6
NEG_INF = -1e30
ATT_SCALE = HEAD_DIM ** -0.5
SLOPES = tuple(2.0 ** (-8.0 * (h + 1) / N_HEADS) for h in range(N_HEADS))
N_CHIPS = 4
N_DEV = 8

LR, B1, B2, AEPS, WD, STEP = 0.001, 0.9, 0.999, 1e-08, 0.01, 10
BC1 = 1.0 - B1 ** STEP
BC2 = 1.0 - B2 ** STEP

BF = jnp.bfloat16
F32 = jnp.float32
MESH = pl.DeviceIdType.MESH
VMEM_LIMIT_V7X = 58 * 1024 * 1024
TM = 256
TM_FFN_BWD = 512
FF_CHUNK = 256
FF_HALF = FF // 2

VSPEC = pl.BlockSpec(memory_space=pltpu.VMEM)
SSPEC = pl.BlockSpec(memory_space=pltpu.SMEM)
ANYSPEC = pl.BlockSpec(memory_space=pl.ANY)


def _params(n_grid=0):
    sem = ("arbitrary",) * n_grid if n_grid else None
    return pltpu.CompilerParams(dimension_semantics=sem, vmem_limit_bytes=VMEM_LIMIT_V7X)


def _sds(shape, dtype=F32):
    return jax.ShapeDtypeStruct(tuple(shape), dtype)


Rider = collections.namedtuple("Rider", "arrays out_shapes aliases scratch start mid finish")
MID_NUM, MID_DEN = 5, 8


def _call(body, *, name, grid, in_specs, out_specs, out_shape, args, scratch_shapes=(), rider=None, prefetch=None):
    ni, no, ns = len(in_specs), len(out_specs), len(scratch_shapes)
    npre = 0 if prefetch is None else 1
    pre = [] if prefetch is None else [prefetch]
    if rider is None:
        rider = Rider([], [], {}, [], None, None, None)
    ri, ro = len(rider.arrays), len(rider.out_shapes)

    def full(*refs):
        pre_refs, refs = refs[:npre], refs[npre:]
        ins, refs = refs[:ni], refs[ni:]
        rins, refs = refs[:ri], refs[ri:]
        outs, refs = refs[:no], refs[no:]
        routs, refs = refs[:ro], refs[ro:]
        scr, rscr = refs[:ns], refs[ns:]
        ids = [pl.program_id(a) for a in range(len(grid))]
        first = ids[0] == 0
        last = ids[0] == grid[0] - 1
        for a in range(1, len(grid)):
            first = first & (ids[a] == 0)
            last = last & (ids[a] == grid[a] - 1)

        if rider.start is not None:
            @pl.when(first)
            def _():
                rider.start(rins, routs, rscr)

        if rider.mid is not None:
            assert len(grid) == 1

            @pl.when(ids[0] == (grid[0] * MID_NUM) // MID_DEN)
            def _():
                rider.mid(rins, routs, rscr)

        body(*pre_refs, *ins, *outs, *scr)

        if rider.finish is not None:
            @pl.when(last)
            def _():
                rider.finish(rins, routs, rscr)

    outs = pl.pallas_call(
        full, name=name,
        grid_spec=pltpu.PrefetchScalarGridSpec(
            num_scalar_prefetch=npre, grid=grid,
            in_specs=list(in_specs) + [ANYSPEC] * ri, out_specs=list(out_specs) + [ANYSPEC] * ro,
            scratch_shapes=list(scratch_shapes) + list(rider.scratch)),
        out_shape=list(out_shape) + list(rider.out_shapes),
        input_output_aliases={npre + ni + a: no + b for a, b in rider.aliases.items()},
        compiler_params=_params(len(grid)))(*pre, *args, *rider.arrays)
    return list(outs[:no]), list(outs[no:])


def _run(name, rider):
    ri = len(rider.arrays)

    def body(*refs):
        rins, routs, rscr = refs[:ri], refs[ri:ri + len(rider.out_shapes)], refs[ri + len(rider.out_shapes):]
        rider.start(rins, routs, rscr)
        if rider.mid is not None:
            rider.mid(rins, routs, rscr)
        rider.finish(rins, routs, rscr)

    return pl.pallas_call(
        body, name=name, in_specs=[ANYSPEC] * ri, out_specs=[ANYSPEC] * len(rider.out_shapes),
        out_shape=list(rider.out_shapes), scratch_shapes=list(rider.scratch),
        input_output_aliases=dict(rider.aliases), compiler_params=_params())(*rider.arrays)


def _rms_fwd(x, g):
    r = lax.rsqrt(jnp.mean(x * x, axis=-1, keepdims=True) + EPS)
    return x * r * g


def _rms_bwd(x, g, dy):
    r = lax.rsqrt(jnp.mean(x * x, axis=-1, keepdims=True) + EPS)
    xn = x * r
    dxn = dy * g
    dx = r * (dxn - xn * jnp.mean(dxn * xn, axis=-1, keepdims=True))
    return dx, dy * xn


def _rowsum(a):
    return jnp.sum(a, axis=0, keepdims=True)


def _sigmoid(z):
    return 1.0 / (1.0 + jnp.exp(-z))


def _dot(a, b):
    return jnp.dot(a, b, preferred_element_type=F32)


def _dot_nt(a, b):
    return lax.dot_general(a, b, (((1,), (1,)), ((), ())), preferred_element_type=F32)


def _dot_tn(a, b):
    return lax.dot_general(a, b, (((0,), (0,)), ((), ())), preferred_element_type=F32)


def _row_spec(tm, width=D):
    return pl.BlockSpec((tm, width), lambda i: (i, 0))


def _const_spec(shape):
    zeros = (0,) * len(shape)
    return pl.BlockSpec(tuple(shape), lambda *_: zeros)


def _pool_delta(he, pos):
    out = []
    for gi, w in enumerate(WINDOWS):
        hg = he[:, gi * POOL_G:(gi + 1) * POOL_G]
        s = hg
        k = 1
        while k < w:
            s = s + pltpu.roll(s, k, 0)
            k *= 2
        cnt = jnp.maximum(jnp.minimum(pos + 1, w), 1).astype(F32)
        out.append(s / cnt - hg)
    return out


def _load_with_halo_before(x_ref, i, tm):
    r0 = pl.multiple_of(i * tm, tm)
    hs = pl.multiple_of(jnp.maximum(i * tm - HALO, 0), 8)
    xh = jnp.where(i > 0, x_ref[pl.ds(hs, HALO), :], 0.0)
    xt = x_ref[pl.ds(r0, tm), :]
    return xt, jnp.concatenate([xh, xt], axis=0)


def _mixa_fwd(x, pre_g, pool_w, pool_scale, post_g):
    s_len = x.shape[0]
    n = s_len // TM

    def body(x_ref, pg_ref, w_ref, sc_ref, qg_ref, y_ref, x1_ref):
        i = pl.program_id(0)
        xt, xe = _load_with_halo_before(x_ref, i, TM)
        he = _rms_fwd(xe, pg_ref[0:1, :])
        pos = i * TM - HALO + lax.broadcasted_iota(jnp.int32, (TM + HALO, 1), 0)
        ds = _pool_delta(he, pos)
        ys = [_dot(ds[gi][HALO:, :].astype(BF), w_ref[gi]) for gi in range(len(WINDOWS))]
        y = jnp.concatenate(ys, axis=1) * sc_ref[...]
        y_ref[...] = y
        x1_ref[...] = xt + _rms_fwd(y, qg_ref[0:1, :])

    outs, _ = _call(body, name="mixa_fwd", grid=(n,),
                    in_specs=[VSPEC] * 5, out_specs=[_row_spec(TM), _row_spec(TM)],
                    out_shape=[_sds((s_len, D)), _sds((s_len, D))],
                    args=[x, pre_g, pool_w, pool_scale, post_g])
    return outs


def _mixa_bwd(dx1, x, y, pre_g, pool_w, pool_scale, post_g, rider=None):
    s_len = x.shape[0]
    n = s_len // TM
    ng = len(WINDOWS)

    def body(dx_ref, x_ref, y_ref, pg_ref, w_ref, sc_ref, qg_ref,
             dx0_ref, dw_ref, dsc_ref, dqg_ref, dpg_ref, wacc):
        i = pl.program_id(0)

        @pl.when(i == 0)
        def _():
            wacc[...] = jnp.zeros_like(wacc)
            dsc_ref[...] = jnp.zeros_like(dsc_ref)
            dqg_ref[...] = jnp.zeros_like(dqg_ref)
            dpg_ref[...] = jnp.zeros_like(dpg_ref)

        r0 = pl.multiple_of(i * TM, TM)
        xt, xe = _load_with_halo_before(x_ref, i, TM)
        he = _rms_fwd(xe, pg_ref[0:1, :])
        pos_b = i * TM - HALO + lax.broadcasted_iota(jnp.int32, (TM + HALO, 1), 0)
        ds = _pool_delta(he, pos_b)

        last = i == n - 1
        a0 = pl.multiple_of(jnp.minimum(i * TM + TM, s_len - HALO), 8)
        ye = jnp.concatenate([y_ref[pl.ds(r0, TM), :], y_ref[pl.ds(a0, HALO), :]], axis=0)
        dt = dx_ref[pl.ds(r0, TM), :]
        de = jnp.concatenate([dt, jnp.where(last, 0.0, dx_ref[pl.ds(a0, HALO), :])], axis=0)
        dye, prod = _rms_bwd(ye, qg_ref[0:1, :], de)
        dqg_ref[...] += _rowsum(prod[:TM, :])
        dys = dye * sc_ref[...]
        pos_a = i * TM + lax.broadcasted_iota(jnp.int32, (TM + HALO, 1), 0)

        dhs, dscs = [], []
        for gi, w in enumerate(WINDOWS):
            sl = slice(gi * POOL_G, (gi + 1) * POOL_G)
            wg = w_ref[gi]
            dys_g = dys[:, sl].astype(BF)
            d_g = ds[gi][HALO:, :].astype(BF)
            ypre = _dot(d_g, wg)
            dscs.append(_rowsum(dye[:TM, sl] * ypre))
            wacc[gi] += _dot_tn(d_g, dys_g[:TM, :])
            dd = _dot_nt(dys_g, wg)
            cnt = jnp.minimum(pos_a + 1, w).astype(F32)
            a = dd / cnt
            k = 1
            while k < w:
                a = a + pltpu.roll(a, TM + HALO - k, 0)
                k *= 2
            dhs.append(a[:TM, :] - dd[:TM, :])
        dsc_ref[...] += jnp.concatenate(dscs, axis=1)
        dh = jnp.concatenate(dhs, axis=1)
        dxp, prod2 = _rms_bwd(xt, pg_ref[0:1, :], dh)
        dpg_ref[...] += _rowsum(prod2)
        dx0_ref[...] = dt + dxp

        @pl.when(last)
        def _():
            dw_ref[...] = wacc[...].astype(BF)

    return _call(
        body, name="mixa_bwd", grid=(n,), in_specs=[VSPEC] * 7,
        out_specs=[_row_spec(TM), _const_spec((ng, POOL_G, POOL_G)), _const_spec((1, D)),
                   _const_spec((1, D)), _const_spec((1, D))],
        out_shape=[_sds((s_len, D)), _sds((ng, POOL_G, POOL_G), BF), _sds((1, D)), _sds((1, D)), _sds((1, D))],
        scratch_shapes=[pltpu.VMEM((ng, POOL_G, POOL_G), F32)],
        args=[dx1, x, y, pre_g, pool_w, pool_scale, post_g], rider=rider)


def _ffn_fwd(layer, x1, pre_g, wgu, wd, post_g, rider=None):
    s_len = x1.shape[0]

    def body(x_ref, pg_ref, wgu_ref, wd_ref, qg_ref, f_ref, x2_ref):
        x = x_ref[...]
        h = _rms_fwd(x, pg_ref[layer:layer + 1, :]).astype(BF)
        f = jnp.zeros((TM, D), F32)
        for c in range(FF // FF_HALF):
            cols = slice(c * FF_HALF, (c + 1) * FF_HALF)
            g = _dot(h, wgu_ref[0, :, cols])
            u = _dot(h, wgu_ref[1, :, cols])
            act = g * _sigmoid(g) * u
            f = f + _dot(act.astype(BF), wd_ref[cols, :])
        f_ref[...] = f
        x2_ref[...] = x + _rms_fwd(f, qg_ref[layer:layer + 1, :])

    return _call(body, name=f"ffn_fwd{layer}", grid=(s_len // TM,),
                 in_specs=[_row_spec(TM), VSPEC, VSPEC, VSPEC, VSPEC],
                 out_specs=[_row_spec(TM), _row_spec(TM)],
                 out_shape=[_sds((s_len, D)), _sds((s_len, D))],
                 args=[x1, pre_g, wgu, wd, post_g], rider=rider)


GU_PIECE = 128
DN_PIECE = 64
DN_SLOT = FF // N_CHIPS
HALF_D = D // 2


def _ffn_bwd(layer, dx2, x1, f, pre_g, wgu, wd, post_g, kc, rider=None):
    s_len = x1.shape[0]
    tm = TM_FFN_BWD
    n = s_len // tm
    nc = FF // FF_CHUNK
    n_gu, n_dn = FF_CHUNK // GU_PIECE, FF_CHUNK // DN_PIECE
    n_pieces = 2 * n_gu + n_dn
    n_blk = FF_HALF // GU_PIECE

    def edge_rows(c, i, kc_ref):
        return (jnp.where((c == 0) | (c == nc - 1), i, n - 1), 0)

    def chunk_at(c, kc_ref):
        return (c + (((kc_ref[0] + 1) % N_CHIPS) * nc) // N_CHIPS) % nc

    def exchange(kc_ref, c, accg, accu, accd, own_gu_ref, land_gu_ref, own_dn_ref, land_dn_ref,
                 pl_gu, pl_dn, sib_gu, sib_dn, mine_gu, mine_dn, sum_gu, sum_dn,
                 psend, precv, ssend, lsem, rrecv):
        x, y, core = lax.axis_index("x"), lax.axis_index("y"), lax.axis_index("c")
        lower = core == 0

        def pair_copy(cc, part):
            p = cc % 2
            src, dst = ((sib_gu, pl_gu), (sib_dn, pl_dn))[part]
            return pltpu.make_async_remote_copy(src.at[p], dst.at[cc], psend.at[p, part], precv.at[cc, part],
                                                device_id=(x, y, 1 - core), device_id_type=MESH)

        def scatter(cc, wait):
            p = cc % 2
            jobs = []
            for gu in range(2):
                for hc in range(n_gu):
                    hidden = chunk_at(cc, kc_ref) * FF_CHUNK + hc * GU_PIECE
                    k = hidden // FF_HALF
                    jobs.append((sum_gu.at[p, gu, hc], k + 2 * gu, 0,
                                 own_gu_ref, land_gu_ref, ((hidden - k * FF_HALF) // GU_PIECE,)))
            for q in range(n_dn):
                hidden = chunk_at(cc, kc_ref) * FF_CHUNK + q * DN_PIECE
                k = hidden // DN_SLOT
                off = pl.multiple_of(hidden - k * DN_SLOT, DN_PIECE)
                jobs.append((sum_dn.at[p, pl.ds(q * DN_PIECE, DN_PIECE), :], k, 1,
                             own_dn_ref, land_dn_ref, (pl.ds(off, DN_PIECE), slice(None))))
            for pi, (src, k, t, own_ref, land_ref, where) in enumerate(jobs):
                kx, ky = k // 2, k % 2
                fx, fy = (kx != x).astype(jnp.int32), (ky != y).astype(jnp.int32)
                local = (fx + fy) == 0
                j = jnp.maximum(fx + 2 * fy - 1, 0)

                @pl.when(local)
                def _():
                    cp = pltpu.make_async_copy(src, own_ref.at[where], lsem.at[p, pi])
                    if wait:
                        cp.wait()
                    else:
                        cp.start()

                @pl.when(jnp.logical_not(local))
                def _():
                    cp = pltpu.make_async_remote_copy(src, land_ref.at[(j,) + where], ssend.at[p, pi],
                                                      rrecv.at[t, j], device_id=(kx, ky, core), device_id_type=MESH)
                    if wait:
                        cp.wait_send()
                    else:
                        cp.start()

        def add_and_scatter(cc):
            p = cc % 2
            pair_copy(cc, 0).wait_recv()
            pair_copy(cc, 1).wait_recv()
            s_gu = (mine_gu[...] + pl_gu[cc].astype(F32)).astype(BF)
            for hc in range(n_gu):
                sum_gu[p, :, hc] = s_gu[:, :, hc * GU_PIECE:(hc + 1) * GU_PIECE]
            sum_dn[p] = (mine_dn[...] + pl_dn[cc].astype(F32)).astype(BF)
            scatter(cc, wait=False)

        @pl.when(c >= 1)
        def _():
            @pl.when(c >= 3)
            def _():
                scatter(c - 3, wait=True)
            add_and_scatter(c - 1)

        @pl.when(c >= 2)
        def _():
            pair_copy(c - 2, 0).wait_send()
            pair_copy(c - 2, 1).wait_send()

        p = c % 2
        my_rows = pl.ds(pl.multiple_of(core * HALF_D, HALF_D), HALF_D)
        sib_rows = pl.ds(pl.multiple_of((1 - core) * HALF_D, HALF_D), HALF_D)
        d_v = accd[...]
        sib_gu[p, 0] = accg[sib_rows, :].astype(BF)
        sib_gu[p, 1] = accu[sib_rows, :].astype(BF)
        sib_dn[p] = jnp.where(lower, d_v[:, HALF_D:], d_v[:, :HALF_D]).astype(BF)
        mine_gu[0] = accg[my_rows, :]
        mine_gu[1] = accu[my_rows, :]
        mine_dn[...] = jnp.where(lower, d_v[:, :HALF_D], d_v[:, HALF_D:])
        pair_copy(c, 0).start()
        pair_copy(c, 1).start()

        @pl.when(c == nc - 1)
        def _():
            scatter(nc - 3, wait=True)
            add_and_scatter(nc - 1)
            for cc in (nc - 2, nc - 1):
                pair_copy(cc, 0).wait_send()
                pair_copy(cc, 1).wait_send()
                scatter(cc, wait=True)
            for t, land_ref in enumerate((land_gu_ref, land_dn_ref)):
                for j in range(N_CHIPS - 1):
                    pltpu.make_async_remote_copy(land_ref.at[j], land_ref.at[j], ssend.at[0, 0], rrecv.at[t, j],
                                                 device_id=(x, y, core), device_id_type=MESH).wait_recv()

    def body(kc_ref, dx_ref, x_ref, f_ref, pg_ref, wgu_ref, wd_ref, qg_ref,
             dx1_ref, dpg_ref, dqg_ref, own_gu_ref, land_gu_ref, own_dn_ref, land_dn_ref,
             h_s, df_s, dh_s, accg, accu, accd, *comm):
        c = pl.program_id(0)
        i = pl.program_id(1)
        rows = pl.ds(pl.multiple_of(i * tm, tm), tm)
        pg = pg_ref[layer:layer + 1, :]

        @pl.when((c == 0) & (i == 0))
        def _():
            dpg_ref[...] = jnp.zeros_like(dpg_ref)
            dqg_ref[...] = jnp.zeros_like(dqg_ref)

        @pl.when(c == 0)
        def _():
            h_s[rows, :] = _rms_fwd(x_ref[...], pg).astype(BF)
            df, prod = _rms_bwd(f_ref[...], qg_ref[layer:layer + 1, :], dx_ref[...])
            df_s[rows, :] = df.astype(BF)
            dqg_ref[...] += _rowsum(prod)

        @pl.when(i == 0)
        def _():
            accg[...] = jnp.zeros_like(accg)
            accu[...] = jnp.zeros_like(accu)
            accd[...] = jnp.zeros_like(accd)

        h = h_s[rows, :]
        df = df_s[rows, :]
        wg = wgu_ref[0]
        wu = wgu_ref[1]
        g = _dot(h, wg)
        u = _dot(h, wu)
        sg = _sigmoid(g)
        a = g * sg
        dact = _dot_nt(df, wd_ref[...])
        accd[...] += _dot_tn((a * u).astype(BF), df)
        du = (dact * a).astype(BF)
        dg = (dact * u * (sg * (1.0 + g * (1.0 - sg)))).astype(BF)
        accg[...] += _dot_tn(h, dg)
        accu[...] += _dot_tn(h, du)
        dh = _dot_nt(dg, wg) + _dot_nt(du, wu)

        @pl.when(c == 0)
        def _():
            dh_s[rows, :] = dh

        @pl.when((c > 0) & (c < nc - 1))
        def _():
            dh_s[rows, :] += dh

        @pl.when(c == nc - 1)
        def _():
            dxp, prod = _rms_bwd(x_ref[...], pg, dh_s[rows, :] + dh)
            dpg_ref[...] += _rowsum(prod)
            dx1_ref[...] = dx_ref[...] + dxp

        @pl.when(i == n - 1)
        def _():
            exchange(kc_ref, c, accg, accu, accd, own_gu_ref, land_gu_ref, own_dn_ref, land_dn_ref, *comm)

    dma = pltpu.SemaphoreType.DMA
    return _call(
        body, name=f"ffn_bwd{layer}", grid=(nc, n),
        in_specs=[pl.BlockSpec((tm, D), edge_rows), pl.BlockSpec((tm, D), edge_rows),
                  pl.BlockSpec((tm, D), lambda c, i, kc_ref: (jnp.where(c == 0, i, n - 1), 0),
                               pipeline_mode=pl.Buffered(1)),
                  VSPEC,
                  pl.BlockSpec((2, D, FF_CHUNK), lambda c, i, kc_ref: (0, 0, chunk_at(c, kc_ref))),
                  pl.BlockSpec((FF_CHUNK, D), lambda c, i, kc_ref: (chunk_at(c, kc_ref), 0)),
                  VSPEC],
        out_specs=[pl.BlockSpec((tm, D), lambda c, i, kc_ref: (jnp.where(c == nc - 1, i, 0), 0)),
                   _const_spec((1, D)), _const_spec((1, D)), ANYSPEC, ANYSPEC, ANYSPEC, ANYSPEC],
        out_shape=[_sds((s_len, D)), _sds((1, D)), _sds((1, D)),
                   _sds((n_blk, HALF_D, GU_PIECE), BF), _sds((N_CHIPS - 1, n_blk, HALF_D, GU_PIECE), BF),
                   _sds((DN_SLOT, HALF_D), BF), _sds((N_CHIPS - 1, DN_SLOT, HALF_D), BF)],
        scratch_shapes=[pltpu.VMEM((s_len, D), BF), pltpu.VMEM((s_len, D), BF), pltpu.VMEM((s_len, D), F32),
                        pltpu.VMEM((D, FF_CHUNK), F32), pltpu.VMEM((D, FF_CHUNK), F32),
                        pltpu.VMEM((FF_CHUNK, D), F32),
                        pltpu.VMEM((nc, 2, HALF_D, FF_CHUNK), BF), pltpu.VMEM((nc, FF_CHUNK, HALF_D), BF),
                        pltpu.VMEM((2, 2, HALF_D, FF_CHUNK), BF), pltpu.VMEM((2, FF_CHUNK, HALF_D), BF),
                        pltpu.VMEM((2, HALF_D, FF_CHUNK), F32), pltpu.VMEM((FF_CHUNK, HALF_D), F32),
                        pltpu.VMEM((2, 2, n_gu, HALF_D, GU_PIECE), BF), pltpu.VMEM((2, FF_CHUNK, HALF_D), BF),
                        dma((2, 2)), dma((nc, 2)), dma((2, n_pieces)), dma((2, n_pieces)), dma((2, N_CHIPS - 1))],
        args=[dx2, x1, f, pre_g, wgu, wd, post_g], rider=rider, prefetch=kc)


def _ple_fwd(layer, x2, p, ple_g, w_gate, w_proj, post_g, target=None, qkv=None, rider=None):
    s_len = x2.shape[0]
    final = target is not None
    assert not (final and qkv)

    def body(*refs):
        if final:
            x_ref, p_ref, g_ref, wg_ref, wp_ref, qg_ref, t_ref, z_ref, pe_ref, dx_ref, lv_ref = refs
        elif qkv:
            (x_ref, p_ref, g_ref, wg_ref, wp_ref, qg_ref, ng_ref, kg_ref, wq_ref, wkv_ref,
             z_ref, pe_ref, x3_ref, q_ref, kv_ref) = refs
        else:
            x_ref, p_ref, g_ref, wg_ref, wp_ref, qg_ref, z_ref, pe_ref, x3_ref = refs
        x = x_ref[...]
        r = _rms_fwd(x, g_ref[layer:layer + 1, :]).astype(BF)
        z = _dot(r, wg_ref[...])
        pe = _dot(p_ref[...].astype(BF), wp_ref[...])
        z_ref[...] = z
        pe_ref[...] = pe
        x3 = x + _rms_fwd(pe * _sigmoid(z), qg_ref[layer:layer + 1, :])
        if final:
            @pl.when(pl.program_id(0) == 0)
            def _():
                lv_ref[...] = jnp.zeros_like(lv_ref)
            err = x3 - t_ref[...]
            dx_ref[...] = err * (1.0 / D)
            lv_ref[...] += _rowsum(err * err)
        else:
            x3_ref[...] = x3
        if qkv:
            q_ref[...] = _dot(_rms_fwd(x3, ng_ref[layer + 1:layer + 2, :]).astype(BF), wq_ref[...]).astype(BF)
            kv_ref[...] = _dot(_rms_fwd(x3, kg_ref[...]).astype(BF), wkv_ref[...]).astype(BF)

    p_spec = pl.BlockSpec((None, TM, PLE), lambda i: (layer, i, 0))
    in_specs = [_row_spec(TM), p_spec, VSPEC, VSPEC, VSPEC, VSPEC]
    args = [x2, p, ple_g, w_gate, w_proj, post_g]
    out_specs = [_row_spec(TM), _row_spec(TM), _row_spec(TM)]
    out_shape = [_sds((s_len, D))] * 3
    if qkv:
        in_specs += [VSPEC] * 4
        args += list(qkv)
        out_specs += [_row_spec(TM), _row_spec(TM, 2 * KVD)]
        out_shape += [_sds((s_len, D), BF), _sds((s_len, 2 * KVD), BF)]
    if final:
        in_specs.append(_row_spec(TM))
        args.append(target)
        out_specs.append(_const_spec((1, D)))
        out_shape.append(_sds((1, D)))
    return _call(body, name=f"ple_fwd{layer}", grid=(s_len // TM,), in_specs=in_specs, out_specs=out_specs,
                 out_shape=out_shape, args=args, rider=rider)


def _ple_bwd(layer, dx3, x2, z, pe, p, ple_g, w_gate, post_g, rider=None):
    s_len = x2.shape[0]
    n = s_len // TM

    def body(dx_ref, x_ref, z_ref, pe_ref, p_ref, g_ref, wg_ref, qg_ref,
             dx2_ref, dwg_ref, dwp_ref, dg_ref, dqg_ref, gacc, pacc):
        i = pl.program_id(0)

        @pl.when(i == 0)
        def _():
            gacc[...] = jnp.zeros_like(gacc)
            pacc[...] = jnp.zeros_like(pacc)
            dg_ref[...] = jnp.zeros_like(dg_ref)
            dqg_ref[...] = jnp.zeros_like(dqg_ref)

        dx = dx_ref[...]
        x = x_ref[...]
        pe_v = pe_ref[...]
        gate = _sigmoid(z_ref[...])
        de, prod = _rms_bwd(pe_v * gate, qg_ref[layer:layer + 1, :], dx)
        dqg_ref[...] += _rowsum(prod)
        dpe = (de * gate).astype(BF)
        dz = (de * pe_v * gate * (1.0 - gate)).astype(BF)
        pacc[...] += _dot_tn(p_ref[...].astype(BF), dpe)
        g = g_ref[layer:layer + 1, :]
        r = _rms_fwd(x, g).astype(BF)
        gacc[...] += _dot_tn(r, dz)
        dr = _dot_nt(dz, wg_ref[...])
        dxp, prod2 = _rms_bwd(x, g, dr)
        dg_ref[...] += _rowsum(prod2)
        dx2_ref[...] = dx + dxp

        @pl.when(i == n - 1)
        def _():
            dwg_ref[...] = gacc[...].astype(BF)
            dwp_ref[...] = pacc[...].astype(BF)

    p_spec = pl.BlockSpec((None, TM, PLE), lambda i: (layer, i, 0))
    return _call(
        body, name=f"ple_bwd{layer}", grid=(n,),
        in_specs=[_row_spec(TM), _row_spec(TM), _row_spec(TM), _row_spec(TM), p_spec, VSPEC, VSPEC, VSPEC],
        out_specs=[_row_spec(TM), _const_spec((D, D)), _const_spec((PLE, D)), _const_spec((1, D)), _const_spec((1, D))],
        out_shape=[_sds((s_len, D)), _sds((D, D), BF), _sds((PLE, D), BF), _sds((1, D)), _sds((1, D))],
        scratch_shapes=[pltpu.VMEM((D, D), F32), pltpu.VMEM((PLE, D), F32)],
        args=[dx3, x2, z, pe, p, ple_g, w_gate, post_g], rider=rider)


def _qkv_bwd(dq, dkv, x3, dx4, q_g, kv_g, w_q, w_kv):
    s_len = x3.shape[0]
    n = s_len // TM

    def body(dq_ref, dkv_ref, x_ref, dx_ref, qg_ref, kg_ref, wq_ref, wkv_ref,
             dx3_ref, dwq_ref, dwkv_ref, dqg_ref, dkg_ref, qacc, kacc):
        i = pl.program_id(0)

        @pl.when(i == 0)
        def _():
            qacc[...] = jnp.zeros_like(qacc)
            kacc[...] = jnp.zeros_like(kacc)
            dqg_ref[...] = jnp.zeros_like(dqg_ref)
            dkg_ref[...] = jnp.zeros_like(dkg_ref)

        x = x_ref[...]
        qg = qg_ref[1:2, :]
        kg = kg_ref[...]
        dq_v = dq_ref[...]
        dkv_v = dkv_ref[...].astype(BF)
        qacc[...] += _dot_tn(_rms_fwd(x, qg).astype(BF), dq_v)
        kacc[...] += _dot_tn(_rms_fwd(x, kg).astype(BF), dkv_v)
        dxq, prod_q = _rms_bwd(x, qg, _dot_nt(dq_v, wq_ref[...]))
        dxk, prod_k = _rms_bwd(x, kg, _dot_nt(dkv_v, wkv_ref[...]))
        dqg_ref[...] += _rowsum(prod_q)
        dkg_ref[...] += _rowsum(prod_k)
        dx3_ref[...] = dx_ref[...] + dxq + dxk

        @pl.when(i == n - 1)
        def _():
            dwq_ref[...] = qacc[...].astype(BF)
            dwkv_ref[...] = kacc[...].astype(BF)

    outs, _ = _call(
        body, name="qkv_bwd", grid=(n,),
        in_specs=[_row_spec(TM), _row_spec(TM, 2 * KVD), _row_spec(TM), _row_spec(TM), VSPEC, VSPEC, VSPEC, VSPEC],
        out_specs=[_row_spec(TM), _const_spec((D, D)), _const_spec((D, 2 * KVD)),
                   _const_spec((1, D)), _const_spec((1, D))],
        out_shape=[_sds((s_len, D)), _sds((D, D), BF), _sds((D, 2 * KVD), BF), _sds((1, D)), _sds((1, D))],
        scratch_shapes=[pltpu.VMEM((D, D), F32), pltpu.VMEM((D, 2 * KVD), F32)],
        args=[dq, dkv, x3, dx4, q_g, kv_g, w_q, w_kv])
    return outs


def _attn_group(i, q, kvw, sink_ref, g):
    rows = GQA * BLK
    heads = [GQA * g + j for j in range(GQA)]
    off = jnp.where(i > 0, BLK, 0)
    row = lax.broadcasted_iota(jnp.int32, (rows, 2 * BLK), 0)
    rel = (row % BLK) - lax.broadcasted_iota(jnp.int32, (rows, 2 * BLK), 1) + off
    valid = (rel >= 0) & (rel < BLK)
    head_of_row = lax.broadcasted_iota(jnp.int32, (rows, 1), 0) // BLK
    slope = jnp.zeros((rows, 1), F32)
    sink = jnp.zeros((rows, 1), F32)
    for j, h in enumerate(heads):
        slope = jnp.where(head_of_row == j, SLOPES[h], slope)
        sink = jnp.where(head_of_row == j, sink_ref[0, h], sink)
    qs = jnp.concatenate([q[:, h * HEAD_DIM:(h + 1) * HEAD_DIM] for h in heads], axis=0)
    k = kvw[:, g * HEAD_DIM:(g + 1) * HEAD_DIM]
    v = kvw[:, KVD + g * HEAD_DIM:KVD + (g + 1) * HEAD_DIM]
    s = _dot_nt(qs, k) * ATT_SCALE - slope * rel.astype(F32)
    s = jnp.where(valid, s, NEG_INF)
    m = jnp.maximum(jnp.max(s, axis=-1, keepdims=True), sink)
    e = jnp.exp(s - m)
    es = jnp.exp(sink - m)
    inv = 1.0 / (jnp.sum(e, axis=-1, keepdims=True) + es)
    return e * inv, es * inv, qs, k, v


def _unstack_heads(stacked):
    return [stacked[j * BLK:(j + 1) * BLK, :] for j in range(GQA)]


def _kv_window(kv_ref, i):
    ks = pl.multiple_of(jnp.maximum(i * BLK - BLK, 0), BLK)
    return ks, kv_ref[pl.ds(ks, 2 * BLK), :]


def _attn_fwd(q, kv, sinks, x3, w_o, post_g, rider=None):
    s_len = q.shape[0]

    def body(q_ref, kv_ref, sk_ref, x_ref, wo_ref, g_ref, a_ref, y_ref, x4_ref):
        i = pl.program_id(0)
        _, kvw = _kv_window(kv_ref, i)
        q = q_ref[...]
        outs = []
        for g in range(N_KV_HEADS):
            p, _, _, _, v = _attn_group(i, q, kvw, sk_ref, g)
            outs += _unstack_heads(_dot(p.astype(BF), v))
        attn = jnp.concatenate(outs, axis=1)
        a_ref[...] = attn
        y = _dot(attn.astype(BF), wo_ref[...])
        y_ref[...] = y
        x4_ref[...] = x_ref[...] + _rms_fwd(y, g_ref[1:2, :])

    return _call(body, name="attn_fwd", grid=(s_len // BLK,),
                 in_specs=[_row_spec(BLK), VSPEC, SSPEC, _row_spec(BLK), VSPEC, VSPEC],
                 out_specs=[_row_spec(BLK)] * 3, out_shape=[_sds((s_len, D))] * 3,
                 args=[q, kv, sinks, x3, w_o, post_g], rider=rider)


def _attn_bwd(dx4, y, attn, q, kv, sinks, w_o, post_g, rider=None):
    s_len = q.shape[0]
    n = s_len // BLK

    def body(dx_ref, y_ref, a_ref, q_ref, kv_ref, sk_ref, wo_ref, g_ref,
             dq_ref, dkv_ref, dwo_ref, dg_ref, dsk_ref, wacc):
        i = pl.program_id(0)

        @pl.when(i == 0)
        def _():
            dkv_ref[...] = jnp.zeros_like(dkv_ref)
            wacc[...] = jnp.zeros_like(wacc)
            dg_ref[...] = jnp.zeros_like(dg_ref)
            dsk_ref[...] = jnp.zeros_like(dsk_ref)

        dy, prod = _rms_bwd(y_ref[...], g_ref[1:2, :], dx_ref[...])
        dg_ref[...] += _rowsum(prod)
        dyb = dy.astype(BF)
        attn = a_ref[...]
        wacc[...] += _dot_tn(attn.astype(BF), dyb)
        d_o = _dot_nt(dyb, wo_ref[...])
        dod = d_o * attn
        ks, kvw = _kv_window(kv_ref, i)
        q = q_ref[...]
        lane = lax.broadcasted_iota(jnp.int32, (1, D), 1)
        dqs, dks, dvs = [], [], []
        dsk = jnp.zeros((1, D), F32)
        for g in range(N_KV_HEADS):
            p, ps, qs, k, v = _attn_group(i, q, kvw, sk_ref, g)
            cols = [slice((GQA * g + j) * HEAD_DIM, (GQA * g + j + 1) * HEAD_DIM) for j in range(GQA)]
            do_s = jnp.concatenate([d_o[:, c] for c in cols], axis=0).astype(BF)
            dsum = jnp.concatenate([jnp.sum(dod[:, c], axis=-1, keepdims=True) for c in cols], axis=0)
            dp = _dot_nt(do_s, v)
            dsb = (p * (dp - dsum) * ATT_SCALE).astype(BF)
            sink_part = ps * dsum
            for j in range(GQA):
                dsk = dsk + jnp.where(lane == GQA * g + j, -_rowsum(sink_part[j * BLK:(j + 1) * BLK, :]), 0.0)
            dqs += _unstack_heads(_dot(dsb, k))
            dks.append(_dot_tn(dsb, qs))
            dvs.append(_dot_tn(p.astype(BF), do_s))
        dsk_ref[...] += dsk
        dq_ref[...] = jnp.concatenate(dqs, axis=1).astype(BF)
        dkv_ref[pl.ds(ks, 2 * BLK), :] += jnp.concatenate(dks + dvs, axis=1)

        @pl.when(i == n - 1)
        def _():
            dwo_ref[...] = wacc[...].astype(BF)

    return _call(
        body, name="attn_bwd", grid=(n,),
        in_specs=[_row_spec(BLK), _row_spec(BLK), _row_spec(BLK), _row_spec(BLK), VSPEC, SSPEC, VSPEC, VSPEC],
        out_specs=[_row_spec(BLK), _const_spec((s_len, 2 * KVD)), _const_spec((D, D)),
                   _const_spec((1, D)), _const_spec((1, D))],
        out_shape=[_sds((s_len, D), BF), _sds((s_len, 2 * KVD)), _sds((D, D), BF), _sds((1, D)), _sds((1, D))],
        scratch_shapes=[pltpu.VMEM((D, D), F32)],
        args=[dx4, y, attn, q, kv, sinks, w_o, post_g], rider=rider)


Big = collections.namedtuple("Big", "name src layer L A R C rb")


def _bigs():
    out = {"pool_w": Big("pool_w", "pool_w", None, 4, 4, POOL_G // N_CHIPS, POOL_G, 32)}
    for l in range(2):
        out[f"w_gu{l}"] = Big(f"w_gu{l}", "w_gu", l, 1, 2, D, FF_HALF, 256)
        out[f"w_down{l}"] = Big(f"w_down{l}", "w_down", l, 1, 4, FF // N_CHIPS, D, 352)
        out[f"w_ple_gate{l}"] = Big(f"w_ple_gate{l}", "w_ple_gate", l, 1, 4, D // N_CHIPS, D, 128)
        out[f"w_ple_proj{l}"] = Big(f"w_ple_proj{l}", "w_ple_proj", l, 1, 1, PLE, D // N_CHIPS, 128)
    out["w_q"] = Big("w_q", "w_q", None, 1, 4, D // N_CHIPS, D, 128)
    out["w_o"] = Big("w_o", "w_o", None, 1, 4, D // N_CHIPS, D, 128)
    out["w_kv"] = Big("w_kv", "w_kv", None, 1, 4, D // N_CHIPS, 2 * KVD, 128)
    return out


BIGS = _bigs()
POOL_SCALE = Big("pool_scale", "pool_scale", None, 1, 1, 1, D // N_CHIPS, 1)
BIG_SOURCES = ("w_gu", "w_down", "w_ple_gate", "w_ple_proj", "w_q", "w_o", "w_kv", "pool_w")


def _ncb(t):
    return N_CHIPS // t.A


def _full_shape(t, rows=None):
    return (t.L, t.A, t.R if rows is None else rows, _ncb(t) * t.C)


def _slot_index(t, k):
    return k // _ncb(t), k % _ncb(t)


def _slot(ref, t, k, row0, rows):
    a, cb = _slot_index(t, k)
    return ref.at[:, a, pl.ds(row0, rows), pl.ds(pl.multiple_of(cb * t.C, 128), t.C)]


def _place(t, w, kc, out_dtype):
    rb = min(t.R, 2 * t.rb)

    def body(kc_ref, w_ref, o_ref):
        del kc_ref
        o_ref[...] = w_ref[...].astype(out_dtype)

    def in_map(l, j, kc_ref):
        return (l if t.layer is None else t.layer, j, 0)

    def out_map(l, j, kc_ref):
        a, cb = _slot_index(t, kc_ref[0])
        return (l, a, j, cb)

    return pl.pallas_call(
        body, name=f"place_{t.name}",
        grid_spec=pltpu.PrefetchScalarGridSpec(
            num_scalar_prefetch=1, grid=(t.L, t.R // rb),
            in_specs=[pl.BlockSpec((None, rb, t.C), in_map)],
            out_specs=pl.BlockSpec((None, None, rb, t.C), out_map)),
        out_shape=_sds(_full_shape(t), out_dtype),
        compiler_params=_params(2),
    )(kc, w)


def _place_many(ts, ws, kc, rider):
    n = 8

    def blocks_of(t):
        return next(nb for nb in (8, 4, 2, 1) if t.R % (16 * nb) == 0)

    def body(kc_ref, *refs):
        del kc_ref
        s = pl.program_id(0)
        for ti, t in enumerate(ts):
            @pl.when(s < blocks_of(t))
            def _():
                refs[len(ts) + ti][...] = refs[ti][...].astype(BF)

    in_specs, out_specs = [], []
    for t in ts:
        assert t.L == 1
        nb = blocks_of(t)
        rb = t.R // nb

        def in_map(s, kc_ref, t=t, nb=nb):
            return (0 if t.layer is None else t.layer, jnp.minimum(s, nb - 1), 0)

        def out_map(s, kc_ref, t=t, nb=nb):
            a, cb = _slot_index(t, kc_ref[0])
            return (0, a, jnp.minimum(s, nb - 1), cb)

        in_specs.append(pl.BlockSpec((None, rb, t.C), in_map))
        out_specs.append(pl.BlockSpec((None, None, rb, t.C), out_map))
    return _call(body, name="place_rest", grid=(n,), in_specs=in_specs, out_specs=out_specs,
                 out_shape=[_sds(_full_shape(t), BF) for t in ts], args=list(ws), rider=rider, prefetch=kc)


def _mesh_position():
    x, y, c = lax.axis_index("x"), lax.axis_index("y"), lax.axis_index("c")
    chips = [(1 - x, y), (x, 1 - y), (1 - x, 1 - y)]
    return x, y, c, chips


def _gather_rider(parts, fulls):
    nt = len(parts)
    TO_X, TO_Y, FWD_X, FWD_Y, SIB_X, SIB_Y, SIB_D = range(7)

    def rows_of(ti, core):
        t, r0, r1 = parts[ti]
        h = (r1 - r0) // 2
        return r0 + core * h, h

    def copy(outs, sems, kind, ti, k_src, row0, rows, dev):
        region = _slot(outs[ti], parts[ti][0], k_src, row0, rows)
        return pltpu.make_async_remote_copy(region, region, sems[0].at[ti, kind], sems[1].at[ti, kind],
                                            device_id=dev, device_id_type=MESH)

    def plan(outs, sems):
        x, y, c, _ = _mesh_position()
        me, kx, ky, kd = 2 * x + y, 2 * (1 - x) + y, 2 * x + (1 - y), 2 * (1 - x) + (1 - y)
        dev_x, dev_y, dev_d, sib = (1 - x, y, c), (x, 1 - y, c), (1 - x, 1 - y, c), (x, y, 1 - c)

        def whole(ti):
            return 0, parts[ti][0].R

        def mk(kind, k_send, k_recv, dev, send_rows, recv_rows):
            def build(ti, side):
                k_src = k_send if side == "s" else k_recv
                row0, rows = (send_rows if side == "s" else recv_rows)(ti)
                return copy(outs, sems, kind, ti, k_src, row0, rows, dev)
            return build

        def first_half(core):
            return lambda ti: (rows_of(ti, core)[0], rows_of(ti, core)[1] // 2)

        def second_half(core):
            return lambda ti: (rows_of(ti, core)[0] + rows_of(ti, core)[1] // 2, rows_of(ti, core)[1] // 2)

        mine = lambda ti: rows_of(ti, c)
        theirs = lambda ti: rows_of(ti, 1 - c)
        split = {
            TO_X: mk(TO_X, me, kx, dev_x, mine, mine),
            TO_Y: mk(TO_Y, me, ky, dev_y, mine, mine),
            FWD_X: mk(FWD_X, ky, kd, dev_x, first_half(c), first_half(c)),
            FWD_Y: mk(FWD_Y, kx, kd, dev_y, second_half(c), second_half(c)),
            SIB_X: mk(SIB_X, kx, kx, sib, mine, theirs),
            SIB_Y: mk(SIB_Y, ky, ky, sib, mine, theirs),
            SIB_D: mk(SIB_D, kd, kd, sib, mine, theirs),
        }
        direct = {
            TO_X: mk(TO_X, me, kx, dev_x, whole, whole),
            TO_Y: mk(TO_Y, me, ky, dev_y, whole, whole),
            FWD_X: mk(FWD_X, me, kd, dev_d, whole, whole),
        }
        return split, direct

    is_split = [t.R > 1 for t, _, _ in parts]

    def start(ins, outs, sems):
        split, direct = plan(outs, sems)
        for ti in range(nt):
            kinds = split if is_split[ti] else direct
            kinds[TO_X](ti, "s").start()
            kinds[TO_Y](ti, "s").start()
            if not is_split[ti]:
                kinds[FWD_X](ti, "s").start()

    def mid(ins, outs, sems):
        split, _ = plan(outs, sems)
        for ti in range(nt):
            if is_split[ti]:
                split[TO_Y](ti, "r").wait_recv()
                split[FWD_X](ti, "s").start()
                split[SIB_Y](ti, "s").start()
        for ti in range(nt):
            if is_split[ti]:
                split[TO_X](ti, "r").wait_recv()
                split[FWD_Y](ti, "s").start()
                split[SIB_X](ti, "s").start()

    def finish(ins, outs, sems):
        split, direct = plan(outs, sems)
        for ti in range(nt):
            if is_split[ti]:
                split[FWD_X](ti, "r").wait_recv()
                split[FWD_Y](ti, "r").wait_recv()
                split[SIB_D](ti, "s").start()
            else:
                for kind in (TO_X, TO_Y, FWD_X):
                    direct[kind](ti, "r").wait_recv()
        for ti in range(nt):
            if is_split[ti]:
                for kind in (SIB_X, SIB_Y, SIB_D):
                    split[kind](ti, "r").wait_recv()
        for ti in range(nt):
            kinds = split if is_split[ti] else direct
            for kind in kinds:
                kinds[kind](ti, "s").wait_send()

    sems = pltpu.SemaphoreType.DMA((nt, 7))
    return Rider(list(fulls), [_sds(a.shape, a.dtype) for a in fulls], {i: i for i in range(nt)},
                 [sems, sems], start, mid, finish)


def _pair_exchange(name, specs, grads):
    nt = len(specs)

    def body(*refs):
        gs = refs[:nt]
        lands = refs[nt:2 * nt]
        send, recv = refs[2 * nt:]
        x, y, c, _ = _mesh_position()
        cps = []
        for ti, t in enumerate(specs):
            half = t.R // 2
            cp = pltpu.make_async_remote_copy(gs[ti].at[:, :, pl.ds((1 - c) * half, half), :], lands[ti],
                                              send.at[ti], recv.at[ti],
                                              device_id=(x, y, 1 - c), device_id_type=MESH)
            cp.start()
            cps.append(cp)
        for cp in cps:
            cp.wait()

    return pl.pallas_call(
        body, name=name,
        in_specs=[ANYSPEC] * nt, out_specs=[ANYSPEC] * nt,
        out_shape=[_sds(_full_shape(t, t.R // 2), BF) for t in specs],
        scratch_shapes=[pltpu.SemaphoreType.DMA((nt,)), pltpu.SemaphoreType.DMA((nt,))],
        compiler_params=_params(),
    )(*grads)


def _pair_sum(t, g, land, kc):
    half = t.R // 2
    nj = half // t.rb
    w = _ncb(t) * t.C

    def body(kc_ref, g_ref, l_ref, o_ref):
        del kc_ref
        o_ref[...] = (g_ref[...].astype(F32) + l_ref[...].astype(F32)).astype(BF)

    return pl.pallas_call(
        body, name=f"pair_sum_{t.name}",
        grid_spec=pltpu.PrefetchScalarGridSpec(
            num_scalar_prefetch=1, grid=(t.L, nj),
            in_specs=[pl.BlockSpec((None, t.A, t.rb, w), lambda l, j, kc_ref: (l, 0, kc_ref[1] * nj + j, 0)),
                      pl.BlockSpec((None, t.A, t.rb, w), lambda l, j, kc_ref: (l, 0, j, 0))],
            out_specs=pl.BlockSpec((None, t.A, t.rb, w), lambda l, j, kc_ref: (l, 0, j, 0))),
        out_shape=_sds(_full_shape(t, half), BF),
        compiler_params=_params(2),
    )(kc, g, land)


def _scatter_rider(specs, sums):
    nt = len(specs)

    def copy(ins, outs, sems, ti, j, chip, c):
        t = specs[ti]
        cx, cy = chip
        return pltpu.make_async_remote_copy(_slot(ins[ti], t, 2 * cx + cy, 0, t.R // 2), outs[ti].at[j],
                                            sems[0].at[ti, j], sems[1].at[ti, j],
                                            device_id=(cx, cy, c), device_id_type=MESH)

    def start(ins, outs, sems):
        _, _, c, chips = _mesh_position()
        for j, chip in enumerate(chips):
            for ti in range(nt):
                copy(ins, outs, sems, ti, j, chip, c).start()

    def finish(ins, outs, sems):
        _, _, c, chips = _mesh_position()
        for j, chip in enumerate(chips):
            for ti in range(nt):
                copy(ins, outs, sems, ti, j, chip, c).wait()

    sems = pltpu.SemaphoreType.DMA((nt, N_CHIPS - 1))
    return Rider(list(sums), [_sds((N_CHIPS - 1, t.L, t.R // 2, t.C), BF) for t in specs], {}, [sems, sems],
                 start, None, finish)


def _chip_sum(t, s, land, kc, n_layers, prev):
    half = t.R // 2
    nj = half // t.rb

    def body(*refs):
        s_ref, l_ref, o_ref = refs[1], refs[2], refs[-1]
        acc = s_ref[...].astype(F32)
        for j in range(N_CHIPS - 1):
            acc = acc + l_ref[j].astype(F32)
        o_ref[...] = acc

    def own_map(l, j, kc_ref):
        a, cb = _slot_index(t, kc_ref[0])
        return (l, a, j, cb)

    def out_map(l, j, kc_ref):
        return (l if t.layer is None else t.layer, kc_ref[1] * nj + j, 0)

    in_specs = [pl.BlockSpec((None, None, t.rb, t.C), own_map),
                pl.BlockSpec((N_CHIPS - 1, None, t.rb, t.C), lambda l, j, kc_ref: (0, l, j, 0))]
    args = [kc, s, land]
    aliases = {}
    if prev is not None:
        in_specs.append(ANYSPEC)
        args.append(prev)
        aliases = {3: 0}
    return pl.pallas_call(
        body, name=f"chip_sum_{t.name}",
        grid_spec=pltpu.PrefetchScalarGridSpec(
            num_scalar_prefetch=1, grid=(t.L, nj), in_specs=in_specs,
            out_specs=pl.BlockSpec((None, t.rb, t.C), out_map)),
        out_shape=_sds((n_layers, t.R, t.C)),
        input_output_aliases=aliases,
        compiler_params=_params(2),
    )(*args)


def _chip_sum_fused(t, own, land, kc, n_layers, prev, by_cols):
    if by_cols:
        rows, cols = own.shape
    else:
        nb, rows, bw = own.shape
        cols = nb * bw
    nj = rows // t.rb

    def body(*refs):
        o_ref, l_ref, out_ref = refs[1], refs[2], refs[-1]
        acc = o_ref[...].astype(F32)
        for j in range(N_CHIPS - 1):
            acc = acc + l_ref[j].astype(F32)
        out_ref[...] = acc if by_cols else jnp.concatenate([acc[b] for b in range(nb)], axis=1)

    def out_map(j, kc_ref):
        return (t.layer, j, kc_ref[1]) if by_cols else (t.layer, kc_ref[1] * nj + j, 0)

    if by_cols:
        in_specs = [pl.BlockSpec((t.rb, cols), lambda j, kc_ref: (j, 0)),
                    pl.BlockSpec((N_CHIPS - 1, t.rb, cols), lambda j, kc_ref: (0, j, 0))]
    else:
        in_specs = [pl.BlockSpec((nb, t.rb, bw), lambda j, kc_ref: (0, j, 0)),
                    pl.BlockSpec((N_CHIPS - 1, nb, t.rb, bw), lambda j, kc_ref: (0, 0, j, 0))]
    args = [kc, own, land]
    aliases = {}
    if prev is not None:
        in_specs.append(ANYSPEC)
        args.append(prev)
        aliases = {3: 0}
    return pl.pallas_call(
        body, name=f"chip_sum_{t.name}",
        grid_spec=pltpu.PrefetchScalarGridSpec(
            num_scalar_prefetch=1, grid=(nj,), in_specs=in_specs,
            out_specs=pl.BlockSpec((None, t.rb, cols), out_map)),
        out_shape=_sds((n_layers, t.R, t.C)),
        input_output_aliases=aliases,
        compiler_params=_params(1),
    )(*args)


def _pair_share(halves, by_cols):
    nt = len(halves)

    def part(ref, ti, core):
        axis = 2 if by_cols[ti] else 1
        half = halves[ti].shape[axis] // 2
        piece = pl.ds(pl.multiple_of(core * half, 128 if by_cols[ti] else 8), half)
        return ref.at[:, :, piece] if by_cols[ti] else ref.at[:, piece, :]

    def body(*refs):
        outs = refs[nt:2 * nt]
        send, recv = refs[2 * nt:]
        x, y, c, _ = _mesh_position()
        cps = []
        for ti in range(nt):
            mine = part(outs[ti], ti, c)
            cp = pltpu.make_async_remote_copy(mine, mine, send.at[ti], recv.at[ti],
                                              device_id=(x, y, 1 - c), device_id_type=MESH)
            cp.start()
            cps.append(cp)
        for ti in range(nt):
            theirs = part(outs[ti], ti, 1 - c)
            pltpu.make_async_remote_copy(theirs, theirs, send.at[ti], recv.at[ti],
                                         device_id=(x, y, 1 - c), device_id_type=MESH).wait_recv()
        for cp in cps:
            cp.wait_send()

    return pl.pallas_call(
        body, name="grads_pair_share",
        in_specs=[ANYSPEC] * nt, out_specs=[ANYSPEC] * nt,
        out_shape=[_sds(a.shape, a.dtype) for a in halves],
        scratch_shapes=[pltpu.SemaphoreType.DMA((nt,)), pltpu.SemaphoreType.DMA((nt,))],
        input_output_aliases={i: i for i in range(nt)},
        compiler_params=_params(),
    )(*halves)


def _adamw_math(w, g, m, v):
    m = B1 * m + (1.0 - B1) * g
    v = B2 * v + (1.0 - B2) * (g * g)
    delta = -LR * ((m / BC1) / (jnp.sqrt(v / BC2) + AEPS) + WD * w)
    return delta, m, v


def _adamw(name, rb, w, g, m, v):
    n_layers, r, c = w.shape

    def body(w_ref, g_ref, m_ref, v_ref, go_ref, d_ref, nm_ref, nv_ref):
        g_v = g_ref[...]
        go_ref[...] = g_v
        d_ref[...], nm_ref[...], nv_ref[...] = _adamw_math(w_ref[...], g_v, m_ref[...], v_ref[...])

    spec = pl.BlockSpec((None, rb, c), lambda l, j: (l, j, 0))
    return pl.pallas_call(
        body, name=f"adamw_{name}", grid=(n_layers, r // rb),
        in_specs=[spec] * 4, out_specs=[spec] * 4, out_shape=[_sds(w.shape)] * 4,
        compiler_params=_params(2),
    )(w, g, m, v)


GAIN_ROWS = {"pre_mix_g": 0, "post_mix_g": 2, "pre_ffn_g": 4, "post_ffn_g": 6, "ple_g": 8, "ple_post_g": 10}
ROW_KV_G, ROW_POOL_SCALE, ROW_SINKS, ROW_LOSS, PACK_ROWS = 12, 13, 14, 15, 16
SMALL_NAMES = tuple(GAIN_ROWS) + ("kv_g", "pool_scale", "sinks")


def _small_all_reduce(rows, dpool, rider=None):
    ng, pr = len(WINDOWS), POOL_G // N_CHIPS

    def body(*refs):
        row_refs = refs[:PACK_ROWS]
        dpool_ref, tot_ref, gpool_ref, pack, land, pland, send, recv, psend, precv = refs[PACK_ROWS:]
        x, y, c, _ = _mesh_position()
        me = 4 * x + 2 * y + c
        for r in range(PACK_ROWS):
            pack[r:r + 1, :] = row_refs[r][...]

        def shard_of(k):
            return dpool_ref.at[:, pl.ds(pl.multiple_of(k * pr, pr), pr), :]

        cps = []
        for j in range(1, N_DEV):
            px, py, pc = x ^ (j >> 2), y ^ ((j >> 1) & 1), c ^ (j & 1)
            cps.append(pltpu.make_async_remote_copy(pack, land.at[me], send.at[j], recv.at[j],
                                                    device_id=(px, py, pc), device_id_type=MESH))
            cps.append(pltpu.make_async_remote_copy(shard_of(2 * px + py), pland.at[me], psend.at[j], precv.at[j],
                                                    device_id=(px, py, pc), device_id_type=MESH))
        for cp in cps:
            cp.start()
        land[me] = pack[...]
        pland[me] = dpool_ref[:, pl.ds(pl.multiple_of((2 * x + y) * pr, pr), pr), :]
        for j in range(1, N_DEV):
            pltpu.make_async_remote_copy(pack, land.at[me ^ j], send.at[j], recv.at[j],
                                         device_id=(x, y, c), device_id_type=MESH).wait_recv()
            pltpu.make_async_remote_copy(shard_of(0), pland.at[me ^ j], psend.at[j], precv.at[j],
                                         device_id=(x, y, c), device_id_type=MESH).wait_recv()
        for cp in cps:
            cp.wait_send()
        tot = land[0]
        gp = pland[0].astype(F32)
        for d in range(1, N_DEV):
            tot = tot + land[d]
            gp = gp + pland[d].astype(F32)
        tot_ref[...] = tot
        gpool_ref[...] = gp

    sems = pltpu.SemaphoreType.DMA((N_DEV,))
    return _call(
        body, name="small_all_reduce", grid=(1,),
        in_specs=[VSPEC] * (PACK_ROWS + 1), out_specs=[VSPEC, VSPEC],
        out_shape=[_sds((PACK_ROWS, D)), _sds((ng, pr, POOL_G))],
        scratch_shapes=[pltpu.VMEM((PACK_ROWS, D), F32), pltpu.VMEM((N_DEV, PACK_ROWS, D), F32),
                        pltpu.VMEM((N_DEV, ng, pr, POOL_G), BF), sems, sems, sems, sems],
        args=[*rows, dpool], rider=rider)


def _small_adamw(tot, kc, small_w, small_m, small_v):
    names = SMALL_NAMES
    n = len(names)

    def body(*refs):
        tot_ref, kc_ref = refs[0], refs[1]
        w_refs = dict(zip(names, refs[2:2 + n]))
        m_refs = dict(zip(names, refs[2 + n:2 + 2 * n]))
        v_refs = dict(zip(names, refs[2 + 2 * n:2 + 3 * n]))
        loss_ref = refs[2 + 3 * n]
        out_refs = {nm: refs[3 + 3 * n + 4 * k: 7 + 3 * n + 4 * k] for k, nm in enumerate(names)}
        tot = tot_ref[...]
        loss_ref[...] = 0.5 * jnp.sum(tot[ROW_LOSS:ROW_LOSS + 1, :], axis=-1, keepdims=True) * (1.0 / D)

        def update(nm, g):
            g_ref, d_ref, nm_ref, nv_ref = out_refs[nm]
            g_ref[...] = g
            d_ref[...], nm_ref[...], nv_ref[...] = _adamw_math(w_refs[nm][...], g, m_refs[nm][...], v_refs[nm][...])

        for nm, r in GAIN_ROWS.items():
            update(nm, tot[r:r + 2, :])
        update("kv_g", tot[ROW_KV_G:ROW_KV_G + 1, :])
        k = kc_ref[0]
        width = D // N_CHIPS
        g_scale = jnp.zeros((1, width), F32)
        for kk in range(N_CHIPS):
            g_scale = g_scale + jnp.where(k == kk, tot[ROW_POOL_SCALE:ROW_POOL_SCALE + 1, kk * width:(kk + 1) * width], 0.0)
        update("pool_scale", g_scale)
        update("sinks", tot[ROW_SINKS:ROW_SINKS + 1, 0:N_HEADS])

    ins = [tot, kc] + [small_w[nm] for nm in names] + [small_m[nm] for nm in names] + [small_v[nm] for nm in names]
    out_shape = [_sds((1, 1))]
    for nm in names:
        out_shape += [_sds(small_w[nm].shape)] * 4
    outs = pl.pallas_call(
        body, name="small_adamw",
        in_specs=[VSPEC, SSPEC] + [VSPEC] * (3 * n), out_specs=[VSPEC] * len(out_shape), out_shape=out_shape,
        compiler_params=_params(),
    )(*ins)
    return outs[0], {nm: outs[1 + 4 * k: 5 + 4 * k] for k, nm in enumerate(names)}


def _compute_layout(t, full):
    if t.src == "w_gu":
        return full.reshape(2, D, FF)
    if t.src == "pool_w":
        return full.reshape(len(WINDOWS), POOL_G, POOL_G)
    if t.src == "pool_scale":
        return full.reshape(1, D)
    return full.reshape(t.A * t.R, _ncb(t) * t.C)


def kernel(x, p, pre_mix_g, post_mix_g, pre_ffn_g, post_ffn_g, pool_w, pool_scale, kv_g, w_kv, w_q, sinks, w_o, w_gu, w_down, ple_g, w_ple_gate, w_ple_proj, ple_post_g, loss_target, m_pre_mix_g, m_post_mix_g, m_pre_ffn_g, m_post_ffn_g, m_pool_w, m_pool_scale, m_kv_g, m_w_kv, m_w_q, m_sinks, m_w_o, m_w_gu, m_w_down, m_ple_g, m_w_ple_gate, m_w_ple_proj, m_ple_post_g, v_pre_mix_g, v_post_mix_g, v_pre_ffn_g, v_post_ffn_g, v_pool_w, v_pool_scale, v_kv_g, v_w_kv, v_w_q, v_sinks, v_w_o, v_w_gu, v_w_down, v_ple_g, v_w_ple_gate, v_w_ple_proj, v_ple_post_g):
    weights = dict(pre_mix_g=pre_mix_g, post_mix_g=post_mix_g, pre_ffn_g=pre_ffn_g, post_ffn_g=post_ffn_g,
                   pool_w=pool_w, pool_scale=pool_scale, kv_g=kv_g, w_kv=w_kv, w_q=w_q, sinks=sinks, w_o=w_o,
                   w_gu=w_gu, w_down=w_down, ple_g=ple_g, w_ple_gate=w_ple_gate, w_ple_proj=w_ple_proj,
                   ple_post_g=ple_post_g)
    m_in = dict(pre_mix_g=m_pre_mix_g, post_mix_g=m_post_mix_g, pre_ffn_g=m_pre_ffn_g, post_ffn_g=m_post_ffn_g,
                pool_w=m_pool_w, pool_scale=m_pool_scale, kv_g=m_kv_g, w_kv=m_w_kv, w_q=m_w_q, sinks=m_sinks,
                w_o=m_w_o, w_gu=m_w_gu, w_down=m_w_down, ple_g=m_ple_g, w_ple_gate=m_w_ple_gate,
                w_ple_proj=m_w_ple_proj, ple_post_g=m_ple_post_g)
    v_in = dict(pre_mix_g=v_pre_mix_g, post_mix_g=v_post_mix_g, pre_ffn_g=v_pre_ffn_g, post_ffn_g=v_post_ffn_g,
                pool_w=v_pool_w, pool_scale=v_pool_scale, kv_g=v_kv_g, w_kv=v_w_kv, w_q=v_w_q, sinks=v_sinks,
                w_o=v_w_o, w_gu=v_w_gu, w_down=v_w_down, ple_g=v_ple_g, w_ple_gate=v_w_ple_gate,
                w_ple_proj=v_w_ple_proj, ple_post_g=v_ple_post_g)
    order = ["pre_mix_g", "post_mix_g", "pre_ffn_g", "post_ffn_g", "pool_w", "pool_scale", "kv_g", "w_kv", "w_q",
             "sinks", "w_o", "w_gu", "w_down", "ple_g", "w_ple_gate", "w_ple_proj", "ple_post_g"]

    kc = jnp.stack([2 * lax.axis_index("x") + lax.axis_index("y"), lax.axis_index("c")]).astype(jnp.int32)
    s_len = x.shape[1]
    x2d = x.reshape(s_len, D)
    p3d = p.reshape(2, s_len, PLE)
    target = loss_target.reshape(s_len, D)
    kv_g2d = kv_g.reshape(1, D)
    gains = {nm: weights[nm] for nm in GAIN_ROWS}

    def shard_view(src, a):
        t = next(t for t in BIGS.values() if t.src == src)
        return a.reshape(-1, t.R, t.C)

    first = ["pool_w", "pool_scale", "w_gu0", "w_down0"]
    rest = [nm for nm in BIGS if nm not in first]
    specs = dict(BIGS, pool_scale=POOL_SCALE)
    placed = {nm: _place(BIGS[nm], shard_view(BIGS[nm].src, weights[BIGS[nm].src]), kc, BF)
              for nm in first if nm in BIGS}
    placed["pool_scale"] = _place(POOL_SCALE, pool_scale.reshape(1, 1, D // N_CHIPS), kc, F32)

    def gather(names, rows=None):
        rows = rows or {}
        parts = [(specs[nm],) + tuple(rows.get(nm, (0, specs[nm].R))) for nm in names]
        return _gather_rider(parts, [placed[nm] for nm in names])

    def take(names, results):
        for nm, a in zip(names, results):
            placed[nm] = a

    def weight(nm):
        return _compute_layout(specs[nm], placed[nm])

    cast, got = _place_many([BIGS[nm] for nm in rest], [shard_view(BIGS[nm].src, weights[BIGS[nm].src]) for nm in rest],
                            kc, rider=gather(first))
    take(rest, cast)
    take(first, got)


    y0, x1 = _mixa_fwd(x2d, gains["pre_mix_g"], weight("pool_w"), weight("pool_scale"), gains["post_mix_g"])

    ride = ["w_ple_gate0", "w_ple_proj0", "w_q", "w_kv", "w_o", "w_gu1"]
    (f0, x2), got = _ffn_fwd(0, x1, gains["pre_ffn_g"], weight("w_gu0"), weight("w_down0"), gains["post_ffn_g"],
                             rider=gather(ride, {"w_gu1": (0, 320)}))
    take(ride, got)

    ride = ["w_ple_gate1", "w_ple_proj1", "w_gu1"]
    (z0, pe0, x3, q, kv), got = _ple_fwd(
        0, x2, p3d, gains["ple_g"], weight("w_ple_gate0"), weight("w_ple_proj0"), gains["ple_post_g"],
        qkv=(gains["pre_mix_g"], kv_g2d, weight("w_q"), weight("w_kv")),
        rider=gather(ride, {"w_gu1": (320, 704)}))
    take(ride, got)

    ride = ["w_down1", "w_gu1"]
    (attn, y1, x4), got = _attn_fwd(q, kv, sinks, x3, weight("w_o"), gains["post_mix_g"],
                                    rider=gather(ride, {"w_gu1": (704, D)}))
    take(ride, got)

    (f1, x5), _ = _ffn_fwd(1, x4, gains["pre_ffn_g"], weight("w_gu1"), weight("w_down1"), gains["post_ffn_g"])
    (z1, pe1, dx6, loss_row), _ = _ple_fwd(1, x5, p3d, gains["ple_g"], weight("w_ple_gate1"), weight("w_ple_proj1"),
                                           gains["ple_post_g"], target=target)

    local = {}
    landed = {}
    fused = {}

    def pair_stage(tag, names):
        ts = [BIGS[nm] for nm in names]
        gs = [local[nm].reshape(_full_shape(t)) for nm, t in zip(names, ts)]
        lands = _pair_exchange(f"grads_pair_exchange_{tag}", ts, gs)
        return [_pair_sum(t, g, l, kc) for t, g, l in zip(ts, gs, lands)]

    def scatter(names, sums):
        return _scatter_rider([BIGS[nm] for nm in names], sums)

    def keep(names, sums, got):
        for nm, s, l in zip(names, sums, got):
            landed[nm] = (s, l)

    (dx5, local["w_ple_gate1"], local["w_ple_proj1"], d_ple1, d_plepost1), _ = _ple_bwd(
        1, dx6, x5, z1, pe1, p3d, gains["ple_g"], weight("w_ple_gate1"), gains["ple_post_g"])

    group_a = ["w_ple_gate1", "w_ple_proj1"]
    sums_a = pair_stage("a", group_a)
    (dx4, d_preffn1, d_postffn1, *scattered), _ = _ffn_bwd(
        1, dx5, x4, f1, gains["pre_ffn_g"], weight("w_gu1"), weight("w_down1"), gains["post_ffn_g"], kc)
    fused["w_gu1"], fused["w_down1"] = scattered[0:2], scattered[2:4]

    (dq, dkv, local["w_o"], d_postmix1, d_sinks), got = _attn_bwd(
        dx4, y1, attn, q, kv, sinks, weight("w_o"), gains["post_mix_g"], rider=scatter(group_a, sums_a))
    keep(group_a, sums_a, got)
    dx3, local["w_q"], local["w_kv"], d_premix1, d_kvg = _qkv_bwd(
        dq, dkv, x3, dx4, gains["pre_mix_g"], kv_g2d, weight("w_q"), weight("w_kv"))

    (dx2, local["w_ple_gate0"], local["w_ple_proj0"], d_ple0, d_plepost0), _ = _ple_bwd(
        0, dx3, x2, z0, pe0, p3d, gains["ple_g"], weight("w_ple_gate0"), gains["ple_post_g"])

    group_b = ["w_o", "w_q", "w_kv", "w_ple_gate0", "w_ple_proj0"]
    sums_b = pair_stage("b", group_b)
    (dx1, d_preffn0, d_postffn0, *scattered), got = _ffn_bwd(
        0, dx2, x1, f0, gains["pre_ffn_g"], weight("w_gu0"), weight("w_down0"), gains["post_ffn_g"], kc,
        rider=scatter(group_b, sums_b))
    fused["w_gu0"], fused["w_down0"] = scattered[0:2], scattered[2:4]
    keep(group_b, sums_b, got)

    (dx0, d_pool, d_scale, d_postmix0, d_premix0), _ = _mixa_bwd(
        dx1, x2d, y0, gains["pre_mix_g"], weight("pool_w"), weight("pool_scale"), gains["post_mix_g"])

    rows = [d_premix0, d_premix1, d_postmix0, d_postmix1, d_preffn0, d_preffn1, d_postffn0, d_postffn1,
            d_ple0, d_ple1, d_plepost0, d_plepost1, d_kvg, d_scale, d_sinks, loss_row]
    as2d = lambda a: a.reshape(1, D) if a.ndim == 1 else a
    (tot, g_pool), _ = _small_all_reduce(rows, d_pool)
    loss, small = _small_adamw(tot, kc, {nm: as2d(weights[nm]) for nm in SMALL_NAMES},
                               {nm: as2d(m_in[nm]) for nm in SMALL_NAMES},
                               {nm: as2d(v_in[nm]) for nm in SMALL_NAMES})

    shared = [src for src in BIG_SOURCES if src != "pool_w"]
    halves = []
    for src in shared:
        n_layers = shard_view(src, weights[src]).shape[0]
        acc = None
        for t in [t for t in BIGS.values() if t.src == src]:
            if t.name in fused:
                own, land = fused[t.name]
                acc = _chip_sum_fused(t, own, land, kc, n_layers, acc, by_cols=src == "w_down")
            else:
                s, l = landed[t.name]
                acc = _chip_sum(t, s, l, kc, n_layers, acc)
        halves.append(acc)
    full_grads = dict(zip(shared, _pair_share(halves, [src == "w_down" for src in shared])))
    full_grads["pool_w"] = g_pool

    out = {"grad": {}, "delta": {}, "new_m": {}, "new_v": {}}
    for src in BIG_SOURCES:
        g = full_grads[src]
        t = next(t for t in BIGS.values() if t.src == src)
        res = _adamw(src, t.rb, shard_view(src, weights[src]), g, shard_view(src, m_in[src]), shard_view(src, v_in[src]))
        shape = weights[src].shape
        for kind, a in zip(("grad", "delta", "new_m", "new_v"), res):
            out[kind][src] = a.reshape(shape)
    for nm in SMALL_NAMES:
        shape = weights[nm].shape
        for kind, a in zip(("grad", "delta", "new_m", "new_v"), small[nm]):
            out[kind][nm] = a.reshape(shape)

    return (loss.reshape(()), dx0.reshape(x.shape),
            *[out["grad"][nm] for nm in order], *[out["delta"][nm] for nm in order],
            *[out["new_m"][nm] for nm in order], *[out["new_v"][nm] for nm in order])
```

```python
import collections

import jax
import jax.numpy as jnp
from jax import lax
from jax.experimental import pallas as pl
from jax.experimental.pallas import tpu as pltpu

D = 1024
FF = 2816
N_HEADS = 16
HEAD_DIM = 64
N_KV_HEADS = 4
GQA = N_HEADS // N_KV_HEADS
KVD = N_KV_HEADS * HEAD_DIM
PLE = 256
BLK = 128
WINDOWS = (2, 4, 8, 16)
POOL_G = 256
HALO = 16
EPS = 1e-6
NEG_INF = -1e30
ATT_SCALE = HEAD_DIM ** -0.5
SLOPES = tuple(2.0 ** (-8.0 * (h + 1) / N_HEADS) for h in range(N_HEADS))
N_CHIPS = 4
N_DEV = 8

LR, B1, B2, AEPS, WD, STEP = 0.001, 0.9, 0.999, 1e-08, 0.01, 10
BC1 = 1.0 - B1 ** STEP
BC2 = 1.0 - B2 ** STEP

BF = jnp.bfloat16
F32 = jnp.float32
MESH = pl.DeviceIdType.MESH
VMEM_LIMIT_V7X = 58 * 1024 * 1024
TM = 256
TM_FFN_BWD = 512
FF_CHUNK = 256
FF_HALF = FF // 2

VSPEC = pl.BlockSpec(memory_space=pltpu.VMEM)
SSPEC = pl.BlockSpec(memory_space=pltpu.SMEM)
ANYSPEC = pl.BlockSpec(memory_space=pl.ANY)


def _params(n_grid=0):
    sem = ("arbitrary",) * n_grid if n_grid else None
    return pltpu.CompilerParams(dimension_semantics=sem, vmem_limit_bytes=VMEM_LIMIT_V7X)


def _sds(shape, dtype=F32):
    return jax.ShapeDtypeStruct(tuple(shape), dtype)


Rider = collections.namedtuple("Rider", "arrays out_shapes aliases scratch start mid finish")
MID_NUM, MID_DEN = 5, 8


def _call(body, *, name, grid, in_specs, out_specs, out_shape, args, scratch_shapes=(), rider=None, prefetch=None):
    ni, no, ns = len(in_specs), len(out_specs), len(scratch_shapes)
    npre = 0 if prefetch is None else 1
    pre = [] if prefetch is None else [prefetch]
    if rider is None:
        rider = Rider([], [], {}, [], None, None, None)
    ri, ro = len(rider.arrays), len(rider.out_shapes)

    def full(*refs):
        pre_refs, refs = refs[:npre], refs[npre:]
        ins, refs = refs[:ni], refs[ni:]
        rins, refs = refs[:ri], refs[ri:]
        outs, refs = refs[:no], refs[no:]
        routs, refs = refs[:ro], refs[ro:]
        scr, rscr = refs[:ns], refs[ns:]
        ids = [pl.program_id(a) for a in range(len(grid))]
        first = ids[0] == 0
        last = ids[0] == grid[0] - 1
        for a in range(1, len(grid)):
            first = first & (ids[a] == 0)
            last = last & (ids[a] == grid[a] - 1)

        if rider.start is not None:
            @pl.when(first)
            def _():
                rider.start(rins, routs, rscr)

        if rider.mid is not None:
            assert len(grid) == 1

            @pl.when(ids[0] == (grid[0] * MID_NUM) // MID_DEN)
            def _():
                rider.mid(rins, routs, rscr)

        body(*pre_refs, *ins, *outs, *scr)

        if rider.finish is not None:
            @pl.when(last)
            def _():
                rider.finish(rins, routs, rscr)

    outs = pl.pallas_call(
        full, name=name,
        grid_spec=pltpu.PrefetchScalarGridSpec(
            num_scalar_prefetch=npre, grid=grid,
            in_specs=list(in_specs) + [ANYSPEC] * ri, out_specs=list(out_specs) + [ANYSPEC] * ro,
            scratch_shapes=list(scratch_shapes) + list(rider.scratch)),
        out_shape=list(out_shape) + list(rider.out_shapes),
        input_output_aliases={npre + ni + a: no + b for a, b in rider.aliases.items()},
        compiler_params=_params(len(grid)))(*pre, *args, *rider.arrays)
    return list(outs[:no]), list(outs[no:])


def _run(name, rider):
    ri = len(rider.arrays)

    def body(*refs):
        rins, routs, rscr = refs[:ri], refs[ri:ri + len(rider.out_shapes)], refs[ri + len(rider.out_shapes):]
        rider.start(rins, routs, rscr)
        if rider.mid is not None:
            rider.mid(rins, routs, rscr)
        rider.finish(rins, routs, rscr)

    return pl.pallas_call(
        body, name=name, in_specs=[ANYSPEC] * ri, out_specs=[ANYSPEC] * len(rider.out_shapes),
        out_shape=list(rider.out_shapes), scratch_shapes=list(rider.scratch),
        input_output_aliases=dict(rider.aliases), compiler_params=_params())(*rider.arrays)


Job = collections.namedtuple("Job", "steps ins outs fn")


def _multi_call(name, jobs, kc):
    n = max(job.steps for job in jobs)

    def clamped(index, steps):
        return lambda s, kc_ref: index(jnp.minimum(s, steps - 1), kc_ref)

    in_specs, out_specs, out_shape, args = [], [], [], []
    for job in jobs:
        for arr, block, index in job.ins:
            in_specs.append(pl.BlockSpec(block, clamped(index, job.steps)))
            args.append(arr)
        for sds, block, index in job.outs:
            out_specs.append(pl.BlockSpec(block, clamped(index, job.steps)))
            out_shape.append(sds)
    n_in = len(args)

    def body(kc_ref, *refs):
        s = pl.program_id(0)
        i0, o0 = 0, n_in
        for job in jobs:
            ins, outs = refs[i0:i0 + len(job.ins)], refs[o0:o0 + len(job.outs)]
            i0, o0 = i0 + len(job.ins), o0 + len(job.outs)

            @pl.when(s < job.steps)
            def _():
                job.fn(s, kc_ref, ins, outs)

    outs, _ = _call(body, name=name, grid=(n,), in_specs=in_specs, out_specs=out_specs, out_shape=out_shape,
                    args=args, prefetch=kc)
    res, o0 = [], 0
    for job in jobs:
        res.append(outs[o0:o0 + len(job.outs)])
        o0 += len(job.outs)
    return res


def _rms_fwd(x, g):
    r = lax.rsqrt(jnp.mean(x * x, axis=-1, keepdims=True) + EPS)
    return x * r * g


def _rms_bwd(x, g, dy):
    r = lax.rsqrt(jnp.mean(x * x, axis=-1, keepdims=True) + EPS)
    xn = x * r
    dxn = dy * g
    dx = r * (dxn - xn * jnp.mean(dxn * xn, axis=-1, keepdims=True))
    return dx, dy * xn


def _rowsum(a):
    return jnp.sum(a, axis=0, keepdims=True)


def _sigmoid(z):
    return 1.0 / (1.0 + jnp.exp(-z))


def _dot(a, b):
    return jnp.dot(a, b, preferred_element_type=F32)


def _dot_nt(a, b):
    return lax.dot_general(a, b, (((1,), (1,)), ((), ())), preferred_element_type=F32)


def _dot_tn(a, b):
    return lax.dot_general(a, b, (((0,), (0,)), ((), ())), preferred_element_type=F32)


def _row_spec(tm, width=D):
    return pl.BlockSpec((tm, width), lambda i: (i, 0))


def _const_spec(shape):
    zeros = (0,) * len(shape)
    return pl.BlockSpec(tuple(shape), lambda *_: zeros)


def _pool_delta(he, pos):
    out = []
    for gi, w in enumerate(WINDOWS):
        hg = he[:, gi * POOL_G:(gi + 1) * POOL_G]
        s = hg
        k = 1
        while k < w:
            s = s + pltpu.roll(s, k, 0)
            k *= 2
        cnt = jnp.maximum(jnp.minimum(pos + 1, w), 1).astype(F32)
        out.append(s / cnt - hg)
    return out


def _load_with_halo_before(x_ref, i, tm):
    r0 = pl.multiple_of(i * tm, tm)
    hs = pl.multiple_of(jnp.maximum(i * tm - HALO, 0), 8)
    xh = jnp.where(i > 0, x_ref[pl.ds(hs, HALO), :], 0.0)
    xt = x_ref[pl.ds(r0, tm), :]
    return xt, jnp.concatenate([xh, xt], axis=0)


def _mixa_fwd(x, pre_g, pool_w, pool_scale, post_g):
    s_len = x.shape[0]
    n = s_len // TM

    def body(x_ref, pg_ref, w_ref, sc_ref, qg_ref, y_ref, x1_ref):
        i = pl.program_id(0)
        xt, xe = _load_with_halo_before(x_ref, i, TM)
        he = _rms_fwd(xe, pg_ref[0:1, :])
        pos = i * TM - HALO + lax.broadcasted_iota(jnp.int32, (TM + HALO, 1), 0)
        ds = _pool_delta(he, pos)
        ys = [_dot(ds[gi][HALO:, :].astype(BF), w_ref[gi]) for gi in range(len(WINDOWS))]
        y = jnp.concatenate(ys, axis=1) * sc_ref[...]
        y_ref[...] = y
        x1_ref[...] = xt + _rms_fwd(y, qg_ref[0:1, :])

    outs, _ = _call(body, name="mixa_fwd", grid=(n,),
                    in_specs=[VSPEC] * 5, out_specs=[_row_spec(TM), _row_spec(TM)],
                    out_shape=[_sds((s_len, D)), _sds((s_len, D))],
                    args=[x, pre_g, pool_w, pool_scale, post_g])
    return outs


def _mixa_bwd(dx1, x, y, pre_g, pool_w, pool_scale, post_g, rider=None):
    s_len = x.shape[0]
    n = s_len // TM
    ng = len(WINDOWS)

    def body(dx_ref, x_ref, y_ref, pg_ref, w_ref, sc_ref, qg_ref,
             dx0_ref, dw_ref, dsc_ref, dqg_ref, dpg_ref, wacc):
        i = pl.program_id(0)

        @pl.when(i == 0)
        def _():
            wacc[...] = jnp.zeros_like(wacc)
            dsc_ref[...] = jnp.zeros_like(dsc_ref)
            dqg_ref[...] = jnp.zeros_like(dqg_ref)
            dpg_ref[...] = jnp.zeros_like(dpg_ref)

        r0 = pl.multiple_of(i * TM, TM)
        xt, xe = _load_with_halo_before(x_ref, i, TM)
        he = _rms_fwd(xe, pg_ref[0:1, :])
        pos_b = i * TM - HALO + lax.broadcasted_iota(jnp.int32, (TM + HALO, 1), 0)
        ds = _pool_delta(he, pos_b)

        last = i == n - 1
        a0 = pl.multiple_of(jnp.minimum(i * TM + TM, s_len - HALO), 8)
        ye = jnp.concatenate([y_ref[pl.ds(r0, TM), :], y_ref[pl.ds(a0, HALO), :]], axis=0)
        dt = dx_ref[pl.ds(r0, TM), :]
        de = jnp.concatenate([dt, jnp.where(last, 0.0, dx_ref[pl.ds(a0, HALO), :])], axis=0)
        dye, prod = _rms_bwd(ye, qg_ref[0:1, :], de)
        dqg_ref[...] += _rowsum(prod[:TM, :])
        dys = dye * sc_ref[...]
        pos_a = i * TM + lax.broadcasted_iota(jnp.int32, (TM + HALO, 1), 0)

        dhs, dscs = [], []
        for gi, w in enumerate(WINDOWS):
            sl = slice(gi * POOL_G, (gi + 1) * POOL_G)
            wg = w_ref[gi]
            dys_g = dys[:, sl].astype(BF)
            d_g = ds[gi][HALO:, :].astype(BF)
            ypre = _dot(d_g, wg)
            dscs.append(_rowsum(dye[:TM, sl] * ypre))
            wacc[gi] += _dot_tn(d_g, dys_g[:TM, :])
            dd = _dot_nt(dys_g, wg)
            cnt = jnp.minimum(pos_a + 1, w).astype(F32)
            a = dd / cnt
            k = 1
            while k < w:
                a = a + pltpu.roll(a, TM + HALO - k, 0)
                k *= 2
            dhs.append(a[:TM, :] - dd[:TM, :])
        dsc_ref[...] += jnp.concatenate(dscs, axis=1)
        dh = jnp.concatenate(dhs, axis=1)
        dxp, prod2 = _rms_bwd(xt, pg_ref[0:1, :], dh)
        dpg_ref[...] += _rowsum(prod2)
        dx0_ref[...] = dt + dxp

        @pl.when(last)
        def _():
            dw_ref[...] = wacc[...].astype(BF)

    return _call(
        body, name="mixa_bwd", grid=(n,), in_specs=[VSPEC] * 7,
        out_specs=[_row_spec(TM), _const_spec((ng, POOL_G, POOL_G)), _const_spec((1, D)),
                   _const_spec((1, D)), _const_spec((1, D))],
        out_shape=[_sds((s_len, D)), _sds((ng, POOL_G, POOL_G), BF), _sds((1, D)), _sds((1, D)), _sds((1, D))],
        scratch_shapes=[pltpu.VMEM((ng, POOL_G, POOL_G), F32)],
        args=[dx1, x, y, pre_g, pool_w, pool_scale, post_g], rider=rider)


def _ffn_fwd(layer, x1, pre_g, wgu, wd, post_g, rider=None):
    s_len = x1.shape[0]

    def body(x_ref, pg_ref, wgu_ref, wd_ref, qg_ref, f_ref, x2_ref):
        x = x_ref[...]
        h = _rms_fwd(x, pg_ref[layer:layer + 1, :]).astype(BF)
        f = jnp.zeros((TM, D), F32)
        for c in range(FF // FF_HALF):
            cols = slice(c * FF_HALF, (c + 1) * FF_HALF)
            g = _dot(h, wgu_ref[0, :, cols])
            u = _dot(h, wgu_ref[1, :, cols])
            act = g * _sigmoid(g) * u
            f = f + _dot(act.astype(BF), wd_ref[cols, :])
        f_ref[...] = f
        x2_ref[...] = x + _rms_fwd(f, qg_ref[layer:layer + 1, :])

    return _call(body, name=f"ffn_fwd{layer}", grid=(s_len // TM,),
                 in_specs=[_row_spec(TM), VSPEC, VSPEC, VSPEC, VSPEC],
                 out_specs=[_row_spec(TM), _row_spec(TM)],
                 out_shape=[_sds((s_len, D)), _sds((s_len, D))],
                 args=[x1, pre_g, wgu, wd, post_g], rider=rider)


GU_PIECE = 128
DN_PIECE = 64
DN_SLOT = FF // N_CHIPS
HALF_D = D // 2


def _ffn_bwd(layer, dx2, x1, f, pre_g, wgu, wd, post_g, kc, rider=None):
    s_len = x1.shape[0]
    tm = TM_FFN_BWD
    n = s_len // tm
    nc = FF // FF_CHUNK
    n_gu, n_dn = FF_CHUNK // GU_PIECE, FF_CHUNK // DN_PIECE
    n_pieces = 2 * n_gu + n_dn
    n_blk = FF_HALF // GU_PIECE

    def edge_rows(c, i, kc_ref):
        return (jnp.where((c == 0) | (c == nc - 1), i, n - 1), 0)

    def chunk_at(c, kc_ref):
        return (c + (((kc_ref[0] + 1) % N_CHIPS) * nc) // N_CHIPS) % nc

    def exchange(kc_ref, c, accg, accu, accd, own_gu_ref, land_gu_ref, own_dn_ref, land_dn_ref,
                 pl_gu, pl_dn, sib_gu, sib_dn, mine_gu, mine_dn, sum_gu, sum_dn,
                 psend, precv, ssend, lsem, rrecv):
        x, y, core = lax.axis_index("x"), lax.axis_index("y"), lax.axis_index("c")
        lower = core == 0

        def pair_copy(cc, part):
            p = cc % 2
            src, dst = ((sib_gu, pl_gu), (sib_dn, pl_dn))[part]
            return pltpu.make_async_remote_copy(src.at[p], dst.at[cc], psend.at[p, part], precv.at[cc, part],
                                                device_id=(x, y, 1 - core), device_id_type=MESH)

        def scatter(cc, wait):
            p = cc % 2
            jobs = []
            for gu in range(2):
                for hc in range(n_gu):
                    hidden = chunk_at(cc, kc_ref) * FF_CHUNK + hc * GU_PIECE
                    k = hidden // FF_HALF
                    jobs.append((sum_gu.at[p, gu, hc], k + 2 * gu, 0,
                                 own_gu_ref, land_gu_ref, ((hidden - k * FF_HALF) // GU_PIECE,)))
            for q in range(n_dn):
                hidden = chunk_at(cc, kc_ref) * FF_CHUNK + q * DN_PIECE
                k = hidden // DN_SLOT
                off = pl.multiple_of(hidden - k * DN_SLOT, DN_PIECE)
                jobs.append((sum_dn.at[p, pl.ds(q * DN_PIECE, DN_PIECE), :], k, 1,
                             own_dn_ref, land_dn_ref, (pl.ds(off, DN_PIECE), slice(None))))
            for pi, (src, k, t, own_ref, land_ref, where) in enumerate(jobs):
                kx, ky = k // 2, k % 2
                fx, fy = (kx != x).astype(jnp.int32), (ky != y).astype(jnp.int32)
                local = (fx + fy) == 0
                j = jnp.maximum(fx + 2 * fy - 1, 0)

                @pl.when(local)
                def _():
                    cp = pltpu.make_async_copy(src, own_ref.at[where], lsem.at[p, pi])
                    if wait:
                        cp.wait()
                    else:
                        cp.start()

                @pl.when(jnp.logical_not(local))
                def _():
                    cp = pltpu.make_async_remote_copy(src, land_ref.at[(j,) + where], ssend.at[p, pi],
                                                      rrecv.at[t, j], device_id=(kx, ky, core), device_id_type=MESH)
                    if wait:
                        cp.wait_send()
                    else:
                        cp.start()

        def add_and_scatter(cc):
            p = cc % 2
            pair_copy(cc, 0).wait_recv()
            pair_copy(cc, 1).wait_recv()
            s_gu = (mine_gu[...] + pl_gu[cc].astype(F32)).astype(BF)
            for hc in range(n_gu):
                sum_gu[p, :, hc] = s_gu[:, :, hc * GU_PIECE:(hc + 1) * GU_PIECE]
            sum_dn[p] = (mine_dn[...] + pl_dn[cc].astype(F32)).astype(BF)
            scatter(cc, wait=False)

        @pl.when(c >= 1)
        def _():
            @pl.when(c >= 3)
            def _():
                scatter(c - 3, wait=True)
            add_and_scatter(c - 1)

        @pl.when(c >= 2)
        def _():
            pair_copy(c - 2, 0).wait_send()
            pair_copy(c - 2, 1).wait_send()

        p = c % 2
        my_rows = pl.ds(pl.multiple_of(core * HALF_D, HALF_D), HALF_D)
        sib_rows = pl.ds(pl.multiple_of((1 - core) * HALF_D, HALF_D), HALF_D)
        d_v = accd[...]
        sib_gu[p, 0] = accg[sib_rows, :].astype(BF)
        sib_gu[p, 1] = accu[sib_rows, :].astype(BF)
        sib_dn[p] = jnp.where(lower, d_v[:, HALF_D:], d_v[:, :HALF_D]).astype(BF)
        mine_gu[0] = accg[my_rows, :]
        mine_gu[1] = accu[my_rows, :]
        mine_dn[...] = jnp.where(lower, d_v[:, :HALF_D], d_v[:, HALF_D:])
        pair_copy(c, 0).start()
        pair_copy(c, 1).start()

        @pl.when(c == nc - 1)
        def _():
            scatter(nc - 3, wait=True)
            add_and_scatter(nc - 1)
            for cc in (nc - 2, nc - 1):
                pair_copy(cc, 0).wait_send()
                pair_copy(cc, 1).wait_send()
                scatter(cc, wait=True)
            for t, land_ref in enumerate((land_gu_ref, land_dn_ref)):
                for j in range(N_CHIPS - 1):
                    pltpu.make_async_remote_copy(land_ref.at[j], land_ref.at[j], ssend.at[0, 0], rrecv.at[t, j],
                                                 device_id=(x, y, core), device_id_type=MESH).wait_recv()

    def body(kc_ref, dx_ref, x_ref, f_ref, pg_ref, wgu_ref, wd_ref, qg_ref,
             dx1_ref, dpg_ref, dqg_ref, own_gu_ref, land_gu_ref, own_dn_ref, land_dn_ref,
             h_s, df_s, dh_s, accg, accu, accd, *comm):
        c = pl.program_id(0)
        i = pl.program_id(1)
        rows = pl.ds(pl.multiple_of(i * tm, tm), tm)
        pg = pg_ref[layer:layer + 1, :]

        @pl.when((c == 0) & (i == 0))
        def _():
            dpg_ref[...] = jnp.zeros_like(dpg_ref)
            dqg_ref[...] = jnp.zeros_like(dqg_ref)

        @pl.when(c == 0)
        def _():
            h_s[rows, :] = _rms_fwd(x_ref[...], pg).astype(BF)
            df, prod = _rms_bwd(f_ref[...], qg_ref[layer:layer + 1, :], dx_ref[...])
            df_s[rows, :] = df.astype(BF)
            dqg_ref[...] += _rowsum(prod)

        @pl.when(i == 0)
        def _():
            accg[...] = jnp.zeros_like(accg)
            accu[...] = jnp.zeros_like(accu)
            accd[...] = jnp.zeros_like(accd)

        h = h_s[rows, :]
        df = df_s[rows, :]
        wg = wgu_ref[0]
        wu = wgu_ref[1]
        g = _dot(h, wg)
        u = _dot(h, wu)
        sg = _sigmoid(g)
        a = g * sg
        dact = _dot_nt(df, wd_ref[...])
        accd[...] += _dot_tn((a * u).astype(BF), df)
        du = (dact * a).astype(BF)
        dg = (dact * u * (sg * (1.0 + g * (1.0 - sg)))).astype(BF)
        accg[...] += _dot_tn(h, dg)
        accu[...] += _dot_tn(h, du)
        dh = _dot_nt(dg, wg) + _dot_nt(du, wu)

        @pl.when(c == 0)
        def _():
            dh_s[rows, :] = dh

        @pl.when((c > 0) & (c < nc - 1))
        def _():
            dh_s[rows, :] += dh

        @pl.when(c == nc - 1)
        def _():
            dxp, prod = _rms_bwd(x_ref[...], pg, dh_s[rows, :] + dh)
            dpg_ref[...] += _rowsum(prod)
            dx1_ref[...] = dx_ref[...] + dxp

        @pl.when(i == n - 1)
        def _():
            exchange(kc_ref, c, accg, accu, accd, own_gu_ref, land_gu_ref, own_dn_ref, land_dn_ref, *comm)

    dma = pltpu.SemaphoreType.DMA
    return _call(
        body, name=f"ffn_bwd{layer}", grid=(nc, n),
        in_specs=[pl.BlockSpec((tm, D), edge_rows), pl.BlockSpec((tm, D), edge_rows),
                  pl.BlockSpec((tm, D), lambda c, i, kc_ref: (jnp.where(c == 0, i, n - 1), 0),
                               pipeline_mode=pl.Buffered(1)),
                  VSPEC,
                  pl.BlockSpec((2, D, FF_CHUNK), lambda c, i, kc_ref: (0, 0, chunk_at(c, kc_ref))),
                  pl.BlockSpec((FF_CHUNK, D), lambda c, i, kc_ref: (chunk_at(c, kc_ref), 0)),
                  VSPEC],
        out_specs=[pl.BlockSpec((tm, D), lambda c, i, kc_ref: (jnp.where(c == nc - 1, i, 0), 0)),
                   _const_spec((1, D)), _const_spec((1, D)), ANYSPEC, ANYSPEC, ANYSPEC, ANYSPEC],
        out_shape=[_sds((s_len, D)), _sds((1, D)), _sds((1, D)),
                   _sds((n_blk, HALF_D, GU_PIECE), BF), _sds((N_CHIPS - 1, n_blk, HALF_D, GU_PIECE), BF),
                   _sds((DN_SLOT, HALF_D), BF), _sds((N_CHIPS - 1, DN_SLOT, HALF_D), BF)],
        scratch_shapes=[pltpu.VMEM((s_len, D), BF), pltpu.VMEM((s_len, D), BF), pltpu.VMEM((s_len, D), F32),
                        pltpu.VMEM((D, FF_CHUNK), F32), pltpu.VMEM((D, FF_CHUNK), F32),
                        pltpu.VMEM((FF_CHUNK, D), F32),
                        pltpu.VMEM((nc, 2, HALF_D, FF_CHUNK), BF), pltpu.VMEM((nc, FF_CHUNK, HALF_D), BF),
                        pltpu.VMEM((2, 2, HALF_D, FF_CHUNK), BF), pltpu.VMEM((2, FF_CHUNK, HALF_D), BF),
                        pltpu.VMEM((2, HALF_D, FF_CHUNK), F32), pltpu.VMEM((FF_CHUNK, HALF_D), F32),
                        pltpu.VMEM((2, 2, n_gu, HALF_D, GU_PIECE), BF), pltpu.VMEM((2, FF_CHUNK, HALF_D), BF),
                        dma((2, 2)), dma((nc, 2)), dma((2, n_pieces)), dma((2, n_pieces)), dma((2, N_CHIPS - 1))],
        args=[dx2, x1, f, pre_g, wgu, wd, post_g], rider=rider, prefetch=kc)


def _ple_fwd(layer, x2, p, ple_g, w_gate, w_proj, post_g, target=None, qkv=None, rider=None):
    s_len = x2.shape[0]
    final = target is not None
    assert not (final and qkv)

    def body(*refs):
        if final:
            x_ref, p_ref, g_ref, wg_ref, wp_ref, qg_ref, t_ref, z_ref, pe_ref, dx_ref, lv_ref = refs
        elif qkv:
            (x_ref, p_ref, g_ref, wg_ref, wp_ref, qg_ref, ng_ref, kg_ref, wq_ref, wkv_ref,
             z_ref, pe_ref, x3_ref, q_ref, kv_ref) = refs
        else:
            x_ref, p_ref, g_ref, wg_ref, wp_ref, qg_ref, z_ref, pe_ref, x3_ref = refs
        x = x_ref[...]
        r = _rms_fwd(x, g_ref[layer:layer + 1, :]).astype(BF)
        z = _dot(r, wg_ref[...])
        pe = _dot(p_ref[...].astype(BF), wp_ref[...])
        z_ref[...] = z
        pe_ref[...] = pe
        x3 = x + _rms_fwd(pe * _sigmoid(z), qg_ref[layer:layer + 1, :])
        if final:
            @pl.when(pl.program_id(0) == 0)
            def _():
                lv_ref[...] = jnp.zeros_like(lv_ref)
            err = x3 - t_ref[...]
            dx_ref[...] = err * (1.0 / D)
            lv_ref[...] += _rowsum(err * err)
        else:
            x3_ref[...] = x3
        if qkv:
            q_ref[...] = _dot(_rms_fwd(x3, ng_ref[layer + 1:layer + 2, :]).astype(BF), wq_ref[...]).astype(BF)
            kv_ref[...] = _dot(_rms_fwd(x3, kg_ref[...]).astype(BF), wkv_ref[...]).astype(BF)

    p_spec = pl.BlockSpec((None, TM, PLE), lambda i: (layer, i, 0))
    in_specs = [_row_spec(TM), p_spec, VSPEC, VSPEC, VSPEC, VSPEC]
    args = [x2, p, ple_g, w_gate, w_proj, post_g]
    out_specs = [_row_spec(TM), _row_spec(TM), _row_spec(TM)]
    out_shape = [_sds((s_len, D))] * 3
    if qkv:
        in_specs += [VSPEC] * 4
        args += list(qkv)
        out_specs += [_row_spec(TM), _row_spec(TM, 2 * KVD)]
        out_shape += [_sds((s_len, D), BF), _sds((s_len, 2 * KVD), BF)]
    if final:
        in_specs.append(_row_spec(TM))
        args.append(target)
        out_specs.append(_const_spec((1, D)))
        out_shape.append(_sds((1, D)))
    return _call(body, name=f"ple_fwd{layer}", grid=(s_len // TM,), in_specs=in_specs, out_specs=out_specs,
                 out_shape=out_shape, args=args, rider=rider)


def _ple_bwd(layer, dx3, x2, z, pe, p, ple_g, w_gate, post_g, rider=None):
    s_len = x2.shape[0]
    n = s_len // TM

    def body(dx_ref, x_ref, z_ref, pe_ref, p_ref, g_ref, wg_ref, qg_ref,
             dx2_ref, dwg_ref, dwp_ref, dg_ref, dqg_ref, gacc, pacc):
        i = pl.program_id(0)

        @pl.when(i == 0)
        def _():
            gacc[...] = jnp.zeros_like(gacc)
            pacc[...] = jnp.zeros_like(pacc)
            dg_ref[...] = jnp.zeros_like(dg_ref)
            dqg_ref[...] = jnp.zeros_like(dqg_ref)

        dx = dx_ref[...]
        x = x_ref[...]
        pe_v = pe_ref[...]
        gate = _sigmoid(z_ref[...])
        de, prod = _rms_bwd(pe_v * gate, qg_ref[layer:layer + 1, :], dx)
        dqg_ref[...] += _rowsum(prod)
        dpe = (de * gate).astype(BF)
        dz = (de * pe_v * gate * (1.0 - gate)).astype(BF)
        pacc[...] += _dot_tn(p_ref[...].astype(BF), dpe)
        g = g_ref[layer:layer + 1, :]
        r = _rms_fwd(x, g).astype(BF)
        gacc[...] += _dot_tn(r, dz)
        dr = _dot_nt(dz, wg_ref[...])
        dxp, prod2 = _rms_bwd(x, g, dr)
        dg_ref[...] += _rowsum(prod2)
        dx2_ref[...] = dx + dxp

        @pl.when(i == n - 1)
        def _():
            dwg_ref[...] = gacc[...].astype(BF)
            dwp_ref[...] = pacc[...].astype(BF)

    p_spec = pl.BlockSpec((None, TM, PLE), lambda i: (layer, i, 0))
    return _call(
        body, name=f"ple_bwd{layer}", grid=(n,),
        in_specs=[_row_spec(TM), _row_spec(TM), _row_spec(TM), _row_spec(TM), p_spec, VSPEC, VSPEC, VSPEC],
        out_specs=[_row_spec(TM), _const_spec((D, D)), _const_spec((PLE, D)), _const_spec((1, D)), _const_spec((1, D))],
        out_shape=[_sds((s_len, D)), _sds((D, D), BF), _sds((PLE, D), BF), _sds((1, D)), _sds((1, D))],
        scratch_shapes=[pltpu.VMEM((D, D), F32), pltpu.VMEM((PLE, D), F32)],
        args=[dx3, x2, z, pe, p, ple_g, w_gate, post_g], rider=rider)


def _qkv_bwd(dq, dkv, x3, dx4, q_g, kv_g, w_q, w_kv):
    s_len = x3.shape[0]
    n = s_len // TM

    def body(dq_ref, dkv_ref, x_ref, dx_ref, qg_ref, kg_ref, wq_ref, wkv_ref,
             dx3_ref, dwq_ref, dwkv_ref, dqg_ref, dkg_ref, qacc, kacc):
        i = pl.program_id(0)

        @pl.when(i == 0)
        def _():
            qacc[...] = jnp.zeros_like(qacc)
            kacc[...] = jnp.zeros_like(kacc)
            dqg_ref[...] = jnp.zeros_like(dqg_ref)
            dkg_ref[...] = jnp.zeros_like(dkg_ref)

        x = x_ref[...]
        qg = qg_ref[1:2, :]
        kg = kg_ref[...]
        dq_v = dq_ref[...]
        dkv_v = dkv_ref[...].astype(BF)
        qacc[...] += _dot_tn(_rms_fwd(x, qg).astype(BF), dq_v)
        kacc[...] += _dot_tn(_rms_fwd(x, kg).astype(BF), dkv_v)
        dxq, prod_q = _rms_bwd(x, qg, _dot_nt(dq_v, wq_ref[...]))
        dxk, prod_k = _rms_bwd(x, kg, _dot_nt(dkv_v, wkv_ref[...]))
        dqg_ref[...] += _rowsum(prod_q)
        dkg_ref[...] += _rowsum(prod_k)
        dx3_ref[...] = dx_ref[...] + dxq + dxk

        @pl.when(i == n - 1)
        def _():
            dwq_ref[...] = qacc[...].astype(BF)
            dwkv_ref[...] = kacc[...].astype(BF)

    outs, _ = _call(
        body, name="qkv_bwd", grid=(n,),
        in_specs=[_row_spec(TM), _row_spec(TM, 2 * KVD), _row_spec(TM), _row_spec(TM), VSPEC, VSPEC, VSPEC, VSPEC],
        out_specs=[_row_spec(TM), _const_spec((D, D)), _const_spec((D, 2 * KVD)),
                   _const_spec((1, D)), _const_spec((1, D))],
        out_shape=[_sds((s_len, D)), _sds((D, D), BF), _sds((D, 2 * KVD), BF), _sds((1, D)), _sds((1, D))],
        scratch_shapes=[pltpu.VMEM((D, D), F32), pltpu.VMEM((D, 2 * KVD), F32)],
        args=[dq, dkv, x3, dx4, q_g, kv_g, w_q, w_kv])
    return outs


def _attn_group(i, q, kvw, sink_ref, g):
    rows = GQA * BLK
    heads = [GQA * g + j for j in range(GQA)]
    off = jnp.where(i > 0, BLK, 0)
    row = lax.broadcasted_iota(jnp.int32, (rows, 2 * BLK), 0)
    rel = (row % BLK) - lax.broadcasted_iota(jnp.int32, (rows, 2 * BLK), 1) + off
    valid = (rel >= 0) & (rel < BLK)
    head_of_row = lax.broadcasted_iota(jnp.int32, (rows, 1), 0) // BLK
    slope = jnp.zeros((rows, 1), F32)
    sink = jnp.zeros((rows, 1), F32)
    for j, h in enumerate(heads):
        slope = jnp.where(head_of_row == j, SLOPES[h], slope)
        sink = jnp.where(head_of_row == j, sink_ref[0, h], sink)
    qs = jnp.concatenate([q[:, h * HEAD_DIM:(h + 1) * HEAD_DIM] for h in heads], axis=0)
    k = kvw[:, g * HEAD_DIM:(g + 1) * HEAD_DIM]
    v = kvw[:, KVD + g * HEAD_DIM:KVD + (g + 1) * HEAD_DIM]
    s = _dot_nt(qs, k) * ATT_SCALE - slope * rel.astype(F32)
    s = jnp.where(valid, s, NEG_INF)
    m = jnp.maximum(jnp.max(s, axis=-1, keepdims=True), sink)
    e = jnp.exp(s - m)
    es = jnp.exp(sink - m)
    inv = 1.0 / (jnp.sum(e, axis=-1, keepdims=True) + es)
    return e * inv, es * inv, qs, k, v


def _unstack_heads(stacked):
    return [stacked[j * BLK:(j + 1) * BLK, :] for j in range(GQA)]


def _kv_window(kv_ref, i):
    ks = pl.multiple_of(jnp.maximum(i * BLK - BLK, 0), BLK)
    return ks, kv_ref[pl.ds(ks, 2 * BLK), :]


def _attn_fwd(q, kv, sinks, x3, w_o, post_g, rider=None):
    s_len = q.shape[0]

    def body(q_ref, kv_ref, sk_ref, x_ref, wo_ref, g_ref, a_ref, y_ref, x4_ref):
        i = pl.program_id(0)
        _, kvw = _kv_window(kv_ref, i)
        q = q_ref[...]
        outs = []
        for g in range(N_KV_HEADS):
            p, _, _, _, v = _attn_group(i, q, kvw, sk_ref, g)
            outs += _unstack_heads(_dot(p.astype(BF), v))
        attn = jnp.concatenate(outs, axis=1)
        a_ref[...] = attn
        y = _dot(attn.astype(BF), wo_ref[...])
        y_ref[...] = y
        x4_ref[...] = x_ref[...] + _rms_fwd(y, g_ref[1:2, :])

    return _call(body, name="attn_fwd", grid=(s_len // BLK,),
                 in_specs=[_row_spec(BLK), VSPEC, SSPEC, _row_spec(BLK), VSPEC, VSPEC],
                 out_specs=[_row_spec(BLK)] * 3, out_shape=[_sds((s_len, D))] * 3,
                 args=[q, kv, sinks, x3, w_o, post_g], rider=rider)


def _attn_bwd(dx4, y, attn, q, kv, sinks, w_o, post_g, rider=None):
    s_len = q.shape[0]
    n = s_len // BLK

    def body(dx_ref, y_ref, a_ref, q_ref, kv_ref, sk_ref, wo_ref, g_ref,
             dq_ref, dkv_ref, dwo_ref, dg_ref, dsk_ref, wacc):
        i = pl.program_id(0)

        @pl.when(i == 0)
        def _():
            dkv_ref[...] = jnp.zeros_like(dkv_ref)
            wacc[...] = jnp.zeros_like(wacc)
            dg_ref[...] = jnp.zeros_like(dg_ref)
            dsk_ref[...] = jnp.zeros_like(dsk_ref)

        dy, prod = _rms_bwd(y_ref[...], g_ref[1:2, :], dx_ref[...])
        dg_ref[...] += _rowsum(prod)
        dyb = dy.astype(BF)
        attn = a_ref[...]
        wacc[...] += _dot_tn(attn.astype(BF), dyb)
        d_o = _dot_nt(dyb, wo_ref[...])
        dod = d_o * attn
        ks, kvw = _kv_window(kv_ref, i)
        q = q_ref[...]
        lane = lax.broadcasted_iota(jnp.int32, (1, D), 1)
        dqs, dks, dvs = [], [], []
        dsk = jnp.zeros((1, D), F32)
        for g in range(N_KV_HEADS):
            p, ps, qs, k, v = _attn_group(i, q, kvw, sk_ref, g)
            cols = [slice((GQA * g + j) * HEAD_DIM, (GQA * g + j + 1) * HEAD_DIM) for j in range(GQA)]
            do_s = jnp.concatenate([d_o[:, c] for c in cols], axis=0).astype(BF)
            dsum = jnp.concatenate([jnp.sum(dod[:, c], axis=-1, keepdims=True) for c in cols], axis=0)
            dp = _dot_nt(do_s, v)
            dsb = (p * (dp - dsum) * ATT_SCALE).astype(BF)
            sink_part = ps * dsum
            for j in range(GQA):
                dsk = dsk + jnp.where(lane == GQA * g + j, -_rowsum(sink_part[j * BLK:(j + 1) * BLK, :]), 0.0)
            dqs += _unstack_heads(_dot(dsb, k))
            dks.append(_dot_tn(dsb, qs))
            dvs.append(_dot_tn(p.astype(BF), do_s))
        dsk_ref[...] += dsk
        dq_ref[...] = jnp.concatenate(dqs, axis=1).astype(BF)
        dkv_ref[pl.ds(ks, 2 * BLK), :] += jnp.concatenate(dks + dvs, axis=1)

        @pl.when(i == n - 1)
        def _():
            dwo_ref[...] = wacc[...].astype(BF)

    return _call(
        body, name="attn_bwd", grid=(n,),
        in_specs=[_row_spec(BLK), _row_spec(BLK), _row_spec(BLK), _row_spec(BLK), VSPEC, SSPEC, VSPEC, VSPEC],
        out_specs=[_row_spec(BLK), _const_spec((s_len, 2 * KVD)), _const_spec((D, D)),
                   _const_spec((1, D)), _const_spec((1, D))],
        out_shape=[_sds((s_len, D), BF), _sds((s_len, 2 * KVD)), _sds((D, D), BF), _sds((1, D)), _sds((1, D))],
        scratch_shapes=[pltpu.VMEM((D, D), F32)],
        args=[dx4, y, attn, q, kv, sinks, w_o, post_g], rider=rider)


Big = collections.namedtuple("Big", "name src layer L A R C rb")


def _bigs():
    out = {"pool_w": Big("pool_w", "pool_w", None, 4, 4, POOL_G // N_CHIPS, POOL_G, 32)}
    for l in range(2):
        out[f"w_gu{l}"] = Big(f"w_gu{l}", "w_gu", l, 1, 2, D, FF_HALF, 256)
        out[f"w_down{l}"] = Big(f"w_down{l}", "w_down", l, 1, 4, FF // N_CHIPS, D, 352)
        out[f"w_ple_gate{l}"] = Big(f"w_ple_gate{l}", "w_ple_gate", l, 1, 4, D // N_CHIPS, D, 128)
        out[f"w_ple_proj{l}"] = Big(f"w_ple_proj{l}", "w_ple_proj", l, 1, 1, PLE, D // N_CHIPS, 128)
    out["w_q"] = Big("w_q", "w_q", None, 1, 4, D // N_CHIPS, D, 128)
    out["w_o"] = Big("w_o", "w_o", None, 1, 4, D // N_CHIPS, D, 128)
    out["w_kv"] = Big("w_kv", "w_kv", None, 1, 4, D // N_CHIPS, 2 * KVD, 128)
    return out


BIGS = _bigs()
POOL_SCALE = Big("pool_scale", "pool_scale", None, 1, 1, 1, D // N_CHIPS, 1)
BIG_SOURCES = ("w_gu", "w_down", "w_ple_gate", "w_ple_proj", "w_q", "w_o", "w_kv", "pool_w")


def _ncb(t):
    return N_CHIPS // t.A


def _full_shape(t, rows=None):
    return (t.L, t.A, t.R if rows is None else rows, _ncb(t) * t.C)


def _slot_index(t, k):
    return k // _ncb(t), k % _ncb(t)


def _slot(ref, t, k, row0, rows):
    a, cb = _slot_index(t, k)
    return ref.at[:, a, pl.ds(row0, rows), pl.ds(pl.multiple_of(cb * t.C, 128), t.C)]


def _place(t, w, kc, out_dtype):
    rb = min(t.R, 2 * t.rb)

    def body(kc_ref, w_ref, o_ref):
        del kc_ref
        o_ref[...] = w_ref[...].astype(out_dtype)

    def in_map(l, j, kc_ref):
        return (l if t.layer is None else t.layer, j, 0)

    def out_map(l, j, kc_ref):
        a, cb = _slot_index(t, kc_ref[0])
        return (l, a, j, cb)

    return pl.pallas_call(
        body, name=f"place_{t.name}",
        grid_spec=pltpu.PrefetchScalarGridSpec(
            num_scalar_prefetch=1, grid=(t.L, t.R // rb),
            in_specs=[pl.BlockSpec((None, rb, t.C), in_map)],
            out_specs=pl.BlockSpec((None, None, rb, t.C), out_map)),
        out_shape=_sds(_full_shape(t), out_dtype),
        compiler_params=_params(2),
    )(kc, w)


def _place_many(ts, ws, kc, rider):
    n = 8

    def blocks_of(t):
        return next(nb for nb in (8, 4, 2, 1) if t.R % (16 * nb) == 0)

    def body(kc_ref, *refs):
        del kc_ref
        s = pl.program_id(0)
        for ti, t in enumerate(ts):
            @pl.when(s < blocks_of(t))
            def _():
                refs[len(ts) + ti][...] = refs[ti][...].astype(BF)

    in_specs, out_specs = [], []
    for t in ts:
        assert t.L == 1
        nb = blocks_of(t)
        rb = t.R // nb

        def in_map(s, kc_ref, t=t, nb=nb):
            return (0 if t.layer is None else t.layer, jnp.minimum(s, nb - 1), 0)

        def out_map(s, kc_ref, t=t, nb=nb):
            a, cb = _slot_index(t, kc_ref[0])
            return (0, a, jnp.minimum(s, nb - 1), cb)

        in_specs.append(pl.BlockSpec((None, rb, t.C), in_map))
        out_specs.append(pl.BlockSpec((None, None, rb, t.C), out_map))
    return _call(body, name="place_rest", grid=(n,), in_specs=in_specs, out_specs=out_specs,
                 out_shape=[_sds(_full_shape(t), BF) for t in ts], args=list(ws), rider=rider, prefetch=kc)


def _mesh_position():
    x, y, c = lax.axis_index("x"), lax.axis_index("y"), lax.axis_index("c")
    chips = [(1 - x, y), (x, 1 - y), (1 - x, 1 - y)]
    return x, y, c, chips


def _gather_rider(parts, fulls):
    nt = len(parts)
    TO_X, TO_Y, FWD_X, FWD_Y, SIB_X, SIB_Y, SIB_D = range(7)

    def rows_of(ti, core):
        t, r0, r1 = parts[ti]
        h = (r1 - r0) // 2
        return r0 + core * h, h

    def copy(outs, sems, kind, ti, k_src, row0, rows, dev):
        region = _slot(outs[ti], parts[ti][0], k_src, row0, rows)
        return pltpu.make_async_remote_copy(region, region, sems[0].at[ti, kind], sems[1].at[ti, kind],
                                            device_id=dev, device_id_type=MESH)

    def plan(outs, sems):
        x, y, c, _ = _mesh_position()
        me, kx, ky, kd = 2 * x + y, 2 * (1 - x) + y, 2 * x + (1 - y), 2 * (1 - x) + (1 - y)
        dev_x, dev_y, dev_d, sib = (1 - x, y, c), (x, 1 - y, c), (1 - x, 1 - y, c), (x, y, 1 - c)

        def whole(ti):
            return 0, parts[ti][0].R

        def mk(kind, k_send, k_recv, dev, send_rows, recv_rows):
            def build(ti, side):
                k_src = k_send if side == "s" else k_recv
                row0, rows = (send_rows if side == "s" else recv_rows)(ti)
                return copy(outs, sems, kind, ti, k_src, row0, rows, dev)
            return build

        def first_half(core):
            return lambda ti: (rows_of(ti, core)[0], rows_of(ti, core)[1] // 2)

        def second_half(core):
            return lambda ti: (rows_of(ti, core)[0] + rows_of(ti, core)[1] // 2, rows_of(ti, core)[1] // 2)

        mine = lambda ti: rows_of(ti, c)
        theirs = lambda ti: rows_of(ti, 1 - c)
        split = {
            TO_X: mk(TO_X, me, kx, dev_x, mine, mine),
            TO_Y: mk(TO_Y, me, ky, dev_y, mine, mine),
            FWD_X: mk(FWD_X, ky, kd, dev_x, first_half(c), first_half(c)),
            FWD_Y: mk(FWD_Y, kx, kd, dev_y, second_half(c), second_half(c)),
            SIB_X: mk(SIB_X, kx, kx, sib, mine, theirs),
            SIB_Y: mk(SIB_Y, ky, ky, sib, mine, theirs),
            SIB_D: mk(SIB_D, kd, kd, sib, mine, theirs),
        }
        direct = {
            TO_X: mk(TO_X, me, kx, dev_x, whole, whole),
            TO_Y: mk(TO_Y, me, ky, dev_y, whole, whole),
            FWD_X: mk(FWD_X, me, kd, dev_d, whole, whole),
        }
        return split, direct

    is_split = [t.R > 1 for t, _, _ in parts]

    def start(ins, outs, sems):
        split, direct = plan(outs, sems)
        for ti in range(nt):
            kinds = split if is_split[ti] else direct
            kinds[TO_X](ti, "s").start()
            kinds[TO_Y](ti, "s").start()
            if not is_split[ti]:
                kinds[FWD_X](ti, "s").start()

    def mid(ins, outs, sems):
        split, _ = plan(outs, sems)
        for ti in range(nt):
            if is_split[ti]:
                split[TO_Y](ti, "r").wait_recv()
                split[FWD_X](ti, "s").start()
                split[SIB_Y](ti, "s").start()
        for ti in range(nt):
            if is_split[ti]:
                split[TO_X](ti, "r").wait_recv()
                split[FWD_Y](ti, "s").start()
                split[SIB_X](ti, "s").start()

    def finish(ins, outs, sems):
        split, direct = plan(outs, sems)
        for ti in range(nt):
            if is_split[ti]:
                split[FWD_X](ti, "r").wait_recv()
                split[FWD_Y](ti, "r").wait_recv()
                split[SIB_D](ti, "s").start()
            else:
                for kind in (TO_X, TO_Y, FWD_X):
                    direct[kind](ti, "r").wait_recv()
        for ti in range(nt):
            if is_split[ti]:
                for kind in (SIB_X, SIB_Y, SIB_D):
                    split[kind](ti, "r").wait_recv()
        for ti in range(nt):
            kinds = split if is_split[ti] else direct
            for kind in kinds:
                kinds[kind](ti, "s").wait_send()

    sems = pltpu.SemaphoreType.DMA((nt, 7))
    return Rider(list(fulls), [_sds(a.shape, a.dtype) for a in fulls], {i: i for i in range(nt)},
                 [sems, sems], start, mid, finish)


def _pair_exchange(name, specs, grads):
    nt = len(specs)

    def body(*refs):
        gs = refs[:nt]
        lands = refs[nt:2 * nt]
        send, recv = refs[2 * nt:]
        x, y, c, _ = _mesh_position()
        cps = []
        for ti, t in enumerate(specs):
            half = t.R // 2
            cp = pltpu.make_async_remote_copy(gs[ti].at[:, :, pl.ds((1 - c) * half, half), :], lands[ti],
                                              send.at[ti], recv.at[ti],
                                              device_id=(x, y, 1 - c), device_id_type=MESH)
            cp.start()
            cps.append(cp)
        for cp in cps:
            cp.wait()

    return pl.pallas_call(
        body, name=name,
        in_specs=[ANYSPEC] * nt, out_specs=[ANYSPEC] * nt,
        out_shape=[_sds(_full_shape(t, t.R // 2), BF) for t in specs],
        scratch_shapes=[pltpu.SemaphoreType.DMA((nt,)), pltpu.SemaphoreType.DMA((nt,))],
        compiler_params=_params(),
    )(*grads)


def _pair_sum_job(t, g, land):
    assert t.L == 1
    half = t.R // 2
    nj = half // t.rb
    block = (None, t.A, t.rb, _ncb(t) * t.C)

    def fn(j, kc_ref, ins, outs):
        outs[0][...] = (ins[0][...].astype(F32) + ins[1][...].astype(F32)).astype(BF)

    return Job(nj,
               [(g, block, lambda j, kc_ref: (0, 0, kc_ref[1] * nj + j, 0)),
                (land, block, lambda j, kc_ref: (0, 0, j, 0))],
               [(_sds(_full_shape(t, half), BF), block, lambda j, kc_ref: (0, 0, j, 0))], fn)


def _scatter_rider(specs, sums):
    nt = len(specs)

    def copy(ins, outs, sems, ti, j, chip, c):
        t = specs[ti]
        cx, cy = chip
        return pltpu.make_async_remote_copy(_slot(ins[ti], t, 2 * cx + cy, 0, t.R // 2), outs[ti].at[j],
                                            sems[0].at[ti, j], sems[1].at[ti, j],
                                            device_id=(cx, cy, c), device_id_type=MESH)

    def start(ins, outs, sems):
        _, _, c, chips = _mesh_position()
        for j, chip in enumerate(chips):
            for ti in range(nt):
                copy(ins, outs, sems, ti, j, chip, c).start()

    def finish(ins, outs, sems):
        _, _, c, chips = _mesh_position()
        for j, chip in enumerate(chips):
            for ti in range(nt):
                copy(ins, outs, sems, ti, j, chip, c).wait()

    sems = pltpu.SemaphoreType.DMA((nt, N_CHIPS - 1))
    return Rider(list(sums), [_sds((N_CHIPS - 1, t.L, t.R // 2, t.C), BF) for t in specs], {}, [sems, sems],
                 start, None, finish)


def _chip_sum_job(ts, landed):
    t0 = ts[0]
    assert t0.L == 1
    half = t0.R // 2
    nj = half // t0.rb

    def local(j, li):
        return jnp.clip(j - li * nj, 0, nj - 1)

    ins = []
    for li, t in enumerate(ts):
        s, land = landed[t.name]

        def own_map(j, kc_ref, li=li, t=t):
            a, cb = _slot_index(t, kc_ref[0])
            return (0, a, local(j, li), cb)

        ins.append((s, (None, None, t.rb, t.C), own_map))
        ins.append((land, (N_CHIPS - 1, None, t.rb, t.C), lambda j, kc_ref, li=li: (0, 0, local(j, li), 0)))

    def fn(j, kc_ref, in_refs, outs):
        for li in range(len(ts)):
            @pl.when(j // nj == li)
            def _():
                acc = in_refs[2 * li][...].astype(F32)
                for k in range(N_CHIPS - 1):
                    acc = acc + in_refs[2 * li + 1][k].astype(F32)
                outs[0][...] = acc

    return Job(len(ts) * nj, ins,
               [(_sds((len(ts), t0.R, t0.C)), (None, t0.rb, t0.C),
                 lambda j, kc_ref: (j // nj, kc_ref[1] * nj + j % nj, 0))], fn)


def _adamw_job(rb, w, g, m, v):
    n_layers, r, c = w.shape
    nb = r // rb
    block = (None, rb, c)
    index = lambda j, kc_ref: (j // nb, j % nb, 0)

    def fn(j, kc_ref, ins, outs):
        g_v = ins[1][...]
        outs[0][...] = g_v
        outs[1][...], outs[2][...], outs[3][...] = _adamw_math(ins[0][...], g_v, ins[2][...], ins[3][...])

    return Job(n_layers * nb, [(a, block, index) for a in (w, g, m, v)],
               [(_sds(w.shape), block, index)] * 4, fn)


def _chip_sum_fused(t, own, land, kc, n_layers, prev, by_cols):
    if by_cols:
        rows, cols = own.shape
    else:
        nb, rows, bw = own.shape
        cols = nb * bw
    nj = rows // t.rb

    def body(*refs):
        o_ref, l_ref, out_ref = refs[1], refs[2], refs[-1]
        acc = o_ref[...].astype(F32)
        for j in range(N_CHIPS - 1):
            acc = acc + l_ref[j].astype(F32)
        out_ref[...] = acc if by_cols else jnp.concatenate([acc[b] for b in range(nb)], axis=1)

    def out_map(j, kc_ref):
        return (t.layer, j, kc_ref[1]) if by_cols else (t.layer, kc_ref[1] * nj + j, 0)

    if by_cols:
        in_specs = [pl.BlockSpec((t.rb, cols), lambda j, kc_ref: (j, 0)),
                    pl.BlockSpec((N_CHIPS - 1, t.rb, cols), lambda j, kc_ref: (0, j, 0))]
    else:
        in_specs = [pl.BlockSpec((nb, t.rb, bw), lambda j, kc_ref: (0, j, 0)),
                    pl.BlockSpec((N_CHIPS - 1, nb, t.rb, bw), lambda j, kc_ref: (0, 0, j, 0))]
    args = [kc, own, land]
    aliases = {}
    if prev is not None:
        in_specs.append(ANYSPEC)
        args.append(prev)
        aliases = {3: 0}
    return pl.pallas_call(
        body, name=f"chip_sum_{t.name}",
        grid_spec=pltpu.PrefetchScalarGridSpec(
            num_scalar_prefetch=1, grid=(nj,), in_specs=in_specs,
            out_specs=pl.BlockSpec((None, t.rb, cols), out_map)),
        out_shape=_sds((n_layers, t.R, t.C)),
        input_output_aliases=aliases,
        compiler_params=_params(1),
    )(*args)


def _pair_share(halves, by_cols):
    nt = len(halves)

    def part(ref, ti, core):
        axis = 2 if by_cols[ti] else 1
        half = halves[ti].shape[axis] // 2
        piece = pl.ds(pl.multiple_of(core * half, 128 if by_cols[ti] else 8), half)
        return ref.at[:, :, piece] if by_cols[ti] else ref.at[:, piece, :]

    def body(*refs):
        outs = refs[nt:2 * nt]
        send, recv = refs[2 * nt:]
        x, y, c, _ = _mesh_position()
        cps = []
        for ti in range(nt):
            mine = part(outs[ti], ti, c)
            cp = pltpu.make_async_remote_copy(mine, mine, send.at[ti], recv.at[ti],
                                              device_id=(x, y, 1 - c), device_id_type=MESH)
            cp.start()
            cps.append(cp)
        for ti in range(nt):
            theirs = part(outs[ti], ti, 1 - c)
            pltpu.make_async_remote_copy(theirs, theirs, send.at[ti], recv.at[ti],
                                         device_id=(x, y, 1 - c), device_id_type=MESH).wait_recv()
        for cp in cps:
            cp.wait_send()

    return pl.pallas_call(
        body, name="grads_pair_share",
        in_specs=[ANYSPEC] * nt, out_specs=[ANYSPEC] * nt,
        out_shape=[_sds(a.shape, a.dtype) for a in halves],
        scratch_shapes=[pltpu.SemaphoreType.DMA((nt,)), pltpu.SemaphoreType.DMA((nt,))],
        input_output_aliases={i: i for i in range(nt)},
        compiler_params=_params(),
    )(*halves)


def _adamw_math(w, g, m, v):
    m = B1 * m + (1.0 - B1) * g
    v = B2 * v + (1.0 - B2) * (g * g)
    delta = -LR * ((m / BC1) / (jnp.sqrt(v / BC2) + AEPS) + WD * w)
    return delta, m, v


def _adamw(name, rb, w, g, m, v):
    n_layers, r, c = w.shape

    def body(w_ref, g_ref, m_ref, v_ref, go_ref, d_ref, nm_ref, nv_ref):
        g_v = g_ref[...]
        go_ref[...] = g_v
        d_ref[...], nm_ref[...], nv_ref[...] = _adamw_math(w_ref[...], g_v, m_ref[...], v_ref[...])

    spec = pl.BlockSpec((None, rb, c), lambda l, j: (l, j, 0))
    return pl.pallas_call(
        body, name=f"adamw_{name}", grid=(n_layers, r // rb),
        in_specs=[spec] * 4, out_specs=[spec] * 4, out_shape=[_sds(w.shape)] * 4,
        compiler_params=_params(2),
    )(w, g, m, v)


GAIN_ROWS = {"pre_mix_g": 0, "post_mix_g": 2, "pre_ffn_g": 4, "post_ffn_g": 6, "ple_g": 8, "ple_post_g": 10}
ROW_KV_G, ROW_POOL_SCALE, ROW_SINKS, ROW_LOSS, PACK_ROWS = 12, 13, 14, 15, 16
SMALL_NAMES = tuple(GAIN_ROWS) + ("kv_g", "pool_scale", "sinks")


def _small_all_reduce(rows, dpool, rider=None):
    ng, pr = len(WINDOWS), POOL_G // N_CHIPS

    def body(*refs):
        row_refs = refs[:PACK_ROWS]
        dpool_ref, tot_ref, gpool_ref, pack, land, pland, send, recv, psend, precv = refs[PACK_ROWS:]
        x, y, c, _ = _mesh_position()
        me = 4 * x + 2 * y + c
        for r in range(PACK_ROWS):
            pack[r:r + 1, :] = row_refs[r][...]

        def shard_of(k):
            return dpool_ref.at[:, pl.ds(pl.multiple_of(k * pr, pr), pr), :]

        cps = []
        for j in range(1, N_DEV):
            px, py, pc = x ^ (j >> 2), y ^ ((j >> 1) & 1), c ^ (j & 1)
            cps.append(pltpu.make_async_remote_copy(pack, land.at[me], send.at[j], recv.at[j],
                                                    device_id=(px, py, pc), device_id_type=MESH))
            cps.append(pltpu.make_async_remote_copy(shard_of(2 * px + py), pland.at[me], psend.at[j], precv.at[j],
                                                    device_id=(px, py, pc), device_id_type=MESH))
        for cp in cps:
            cp.start()
        land[me] = pack[...]
        pland[me] = dpool_ref[:, pl.ds(pl.multiple_of((2 * x + y) * pr, pr), pr), :]
        for j in range(1, N_DEV):
            pltpu.make_async_remote_copy(pack, land.at[me ^ j], send.at[j], recv.at[j],
                                         device_id=(x, y, c), device_id_type=MESH).wait_recv()
            pltpu.make_async_remote_copy(shard_of(0), pland.at[me ^ j], psend.at[j], precv.at[j],
                                         device_id=(x, y, c), device_id_type=MESH).wait_recv()
        for cp in cps:
            cp.wait_send()
        tot = land[0]
        gp = pland[0].astype(F32)
        for d in range(1, N_DEV):
            tot = tot + land[d]
            gp = gp + pland[d].astype(F32)
        tot_ref[...] = tot
        gpool_ref[...] = gp

    sems = pltpu.SemaphoreType.DMA((N_DEV,))
    return _call(
        body, name="small_all_reduce", grid=(1,),
        in_specs=[VSPEC] * (PACK_ROWS + 1), out_specs=[VSPEC, VSPEC],
        out_shape=[_sds((PACK_ROWS, D)), _sds((ng, pr, POOL_G))],
        scratch_shapes=[pltpu.VMEM((PACK_ROWS, D), F32), pltpu.VMEM((N_DEV, PACK_ROWS, D), F32),
                        pltpu.VMEM((N_DEV, ng, pr, POOL_G), BF), sems, sems, sems, sems],
        args=[*rows, dpool], rider=rider)


def _small_adamw(tot, kc, small_w, small_m, small_v):
    names = SMALL_NAMES
    n = len(names)

    def body(*refs):
        tot_ref, kc_ref = refs[0], refs[1]
        w_refs = dict(zip(names, refs[2:2 + n]))
        m_refs = dict(zip(names, refs[2 + n:2 + 2 * n]))
        v_refs = dict(zip(names, refs[2 + 2 * n:2 + 3 * n]))
        loss_ref = refs[2 + 3 * n]
        out_refs = {nm: refs[3 + 3 * n + 4 * k: 7 + 3 * n + 4 * k] for k, nm in enumerate(names)}
        tot = tot_ref[...]
        loss_ref[...] = 0.5 * jnp.sum(tot[ROW_LOSS:ROW_LOSS + 1, :], axis=-1, keepdims=True) * (1.0 / D)

        def update(nm, g):
            g_ref, d_ref, nm_ref, nv_ref = out_refs[nm]
            g_ref[...] = g
            d_ref[...], nm_ref[...], nv_ref[...] = _adamw_math(w_refs[nm][...], g, m_refs[nm][...], v_refs[nm][...])

        for nm, r in GAIN_ROWS.items():
            update(nm, tot[r:r + 2, :])
        update("kv_g", tot[ROW_KV_G:ROW_KV_G + 1, :])
        k = kc_ref[0]
        width = D // N_CHIPS
        g_scale = jnp.zeros((1, width), F32)
        for kk in range(N_CHIPS):
            g_scale = g_scale + jnp.where(k == kk, tot[ROW_POOL_SCALE:ROW_POOL_SCALE + 1, kk * width:(kk + 1) * width], 0.0)
        update("pool_scale", g_scale)
        update("sinks", tot[ROW_SINKS:ROW_SINKS + 1, 0:N_HEADS])

    ins = [tot, kc] + [small_w[nm] for nm in names] + [small_m[nm] for nm in names] + [small_v[nm] for nm in names]
    out_shape = [_sds((1, 1))]
    for nm in names:
        out_shape += [_sds(small_w[nm].shape)] * 4
    outs = pl.pallas_call(
        body, name="small_adamw",
        in_specs=[VSPEC, SSPEC] + [VSPEC] * (3 * n), out_specs=[VSPEC] * len(out_shape), out_shape=out_shape,
        compiler_params=_params(),
    )(*ins)
    return outs[0], {nm: outs[1 + 4 * k: 5 + 4 * k] for k, nm in enumerate(names)}


def _compute_layout(t, full):
    if t.src == "w_gu":
        return full.reshape(2, D, FF)
    if t.src == "pool_w":
        return full.reshape(len(WINDOWS), POOL_G, POOL_G)
    if t.src == "pool_scale":
        return full.reshape(1, D)
    return full.reshape(t.A * t.R, _ncb(t) * t.C)


def kernel(x, p, pre_mix_g, post_mix_g, pre_ffn_g, post_ffn_g, pool_w, pool_scale, kv_g, w_kv, w_q, sinks, w_o, w_gu, w_down, ple_g, w_ple_gate, w_ple_proj, ple_post_g, loss_target, m_pre_mix_g, m_post_mix_g, m_pre_ffn_g, m_post_ffn_g, m_pool_w, m_pool_scale, m_kv_g, m_w_kv, m_w_q, m_sinks, m_w_o, m_w_gu, m_w_down, m_ple_g, m_w_ple_gate, m_w_ple_proj, m_ple_post_g, v_pre_mix_g, v_post_mix_g, v_pre_ffn_g, v_post_ffn_g, v_pool_w, v_pool_scale, v_kv_g, v_w_kv, v_w_q, v_sinks, v_w_o, v_w_gu, v_w_down, v_ple_g, v_w_ple_gate, v_w_ple_proj, v_ple_post_g):
    weights = dict(pre_mix_g=pre_mix_g, post_mix_g=post_mix_g, pre_ffn_g=pre_ffn_g, post_ffn_g=post_ffn_g,
                   pool_w=pool_w, pool_scale=pool_scale, kv_g=kv_g, w_kv=w_kv, w_q=w_q, sinks=sinks, w_o=w_o,
                   w_gu=w_gu, w_down=w_down, ple_g=ple_g, w_ple_gate=w_ple_gate, w_ple_proj=w_ple_proj,
                   ple_post_g=ple_post_g)
    m_in = dict(pre_mix_g=m_pre_mix_g, post_mix_g=m_post_mix_g, pre_ffn_g=m_pre_ffn_g, post_ffn_g=m_post_ffn_g,
                pool_w=m_pool_w, pool_scale=m_pool_scale, kv_g=m_kv_g, w_kv=m_w_kv, w_q=m_w_q, sinks=m_sinks,
                w_o=m_w_o, w_gu=m_w_gu, w_down=m_w_down, ple_g=m_ple_g, w_ple_gate=m_w_ple_gate,
                w_ple_proj=m_w_ple_proj, ple_post_g=m_ple_post_g)
    v_in = dict(pre_mix_g=v_pre_mix_g, post_mix_g=v_post_mix_g, pre_ffn_g=v_pre_ffn_g, post_ffn_g=v_post_ffn_g,
                pool_w=v_pool_w, pool_scale=v_pool_scale, kv_g=v_kv_g, w_kv=v_w_kv, w_q=v_w_q, sinks=v_sinks,
                w_o=v_w_o, w_gu=v_w_gu, w_down=v_w_down, ple_g=v_ple_g, w_ple_gate=v_w_ple_gate,
                w_ple_proj=v_w_ple_proj, ple_post_g=v_ple_post_g)
    order = ["pre_mix_g", "post_mix_g", "pre_ffn_g", "post_ffn_g", "pool_w", "pool_scale", "kv_g", "w_kv", "w_q",
             "sinks", "w_o", "w_gu", "w_down", "ple_g", "w_ple_gate", "w_ple_proj", "ple_post_g"]

    kc = jnp.stack([2 * lax.axis_index("x") + lax.axis_index("y"), lax.axis_index("c")]).astype(jnp.int32)
    s_len = x.shape[1]
    x2d = x.reshape(s_len, D)
    p3d = p.reshape(2, s_len, PLE)
    target = loss_target.reshape(s_len, D)
    kv_g2d = kv_g.reshape(1, D)
    gains = {nm: weights[nm] for nm in GAIN_ROWS}

    def shard_view(src, a):
        t = next(t for t in BIGS.values() if t.src == src)
        return a.reshape(-1, t.R, t.C)

    first = ["pool_w", "pool_scale", "w_gu0", "w_down0"]
    rest = [nm for nm in BIGS if nm not in first]
    specs = dict(BIGS, pool_scale=POOL_SCALE)
    placed = {nm: _place(BIGS[nm], shard_view(BIGS[nm].src, weights[BIGS[nm].src]), kc, BF)
              for nm in first if nm in BIGS}
    placed["pool_scale"] = _place(POOL_SCALE, pool_scale.reshape(1, 1, D // N_CHIPS), kc, F32)

    def gather(names, rows=None):
        rows = rows or {}
        parts = [(specs[nm],) + tuple(rows.get(nm, (0, specs[nm].R))) for nm in names]
        return _gather_rider(parts, [placed[nm] for nm in names])

    def take(names, results):
        for nm, a in zip(names, results):
            placed[nm] = a

    def weight(nm):
        return _compute_layout(specs[nm], placed[nm])

    cast, got = _place_many([BIGS[nm] for nm in rest], [shard_view(BIGS[nm].src, weights[BIGS[nm].src]) for nm in rest],
                            kc, rider=gather(first))
    take(rest, cast)
    take(first, got)


    y0, x1 = _mixa_fwd(x2d, gains["pre_mix_g"], weight("pool_w"), weight("pool_scale"), gains["post_mix_g"])

    ride = ["w_ple_gate0", "w_ple_proj0", "w_q", "w_kv", "w_o", "w_gu1"]
    (f0, x2), got = _ffn_fwd(0, x1, gains["pre_ffn_g"], weight("w_gu0"), weight("w_down0"), gains["post_ffn_g"],
                             rider=gather(ride, {"w_gu1": (0, 320)}))
    take(ride, got)

    ride = ["w_ple_gate1", "w_ple_proj1", "w_gu1"]
    (z0, pe0, x3, q, kv), got = _ple_fwd(
        0, x2, p3d, gains["ple_g"], weight("w_ple_gate0"), weight("w_ple_proj0"), gains["ple_post_g"],
        qkv=(gains["pre_mix_g"], kv_g2d, weight("w_q"), weight("w_kv")),
        rider=gather(ride, {"w_gu1": (320, 704)}))
    take(ride, got)

    ride = ["w_down1", "w_gu1"]
    (attn, y1, x4), got = _attn_fwd(q, kv, sinks, x3, weight("w_o"), gains["post_mix_g"],
                                    rider=gather(ride, {"w_gu1": (704, D)}))
    take(ride, got)

    (f1, x5), _ = _ffn_fwd(1, x4, gains["pre_ffn_g"], weight("w_gu1"), weight("w_down1"), gains["post_ffn_g"])
    (z1, pe1, dx6, loss_row), _ = _ple_fwd(1, x5, p3d, gains["ple_g"], weight("w_ple_gate1"), weight("w_ple_proj1"),
                                           gains["ple_post_g"], target=target)

    local = {}
    landed = {}
    fused = {}

    def pair_stage(tag, names):
        ts = [BIGS[nm] for nm in names]
        gs = [local[nm].reshape(_full_shape(t)) for nm, t in zip(names, ts)]
        lands = _pair_exchange(f"grads_pair_exchange_{tag}", ts, gs)
        jobs = [_pair_sum_job(t, g, l) for t, g, l in zip(ts, gs, lands)]
        return [r[0] for r in _multi_call(f"pair_sum_{tag}", jobs, kc)]

    def scatter(names, sums):
        return _scatter_rider([BIGS[nm] for nm in names], sums)

    def keep(names, sums, got):
        for nm, s, l in zip(names, sums, got):
            landed[nm] = (s, l)

    (dx5, local["w_ple_gate1"], local["w_ple_proj1"], d_ple1, d_plepost1), _ = _ple_bwd(
        1, dx6, x5, z1, pe1, p3d, gains["ple_g"], weight("w_ple_gate1"), gains["ple_post_g"])

    group_a = ["w_ple_gate1", "w_ple_proj1"]
    sums_a = pair_stage("a", group_a)
    (dx4, d_preffn1, d_postffn1, *scattered), _ = _ffn_bwd(
        1, dx5, x4, f1, gains["pre_ffn_g"], weight("w_gu1"), weight("w_down1"), gains["post_ffn_g"], kc)
    fused["w_gu1"], fused["w_down1"] = scattered[0:2], scattered[2:4]

    (dq, dkv, local["w_o"], d_postmix1, d_sinks), got = _attn_bwd(
        dx4, y1, attn, q, kv, sinks, weight("w_o"), gains["post_mix_g"], rider=scatter(group_a, sums_a))
    keep(group_a, sums_a, got)
    dx3, local["w_q"], local["w_kv"], d_premix1, d_kvg = _qkv_bwd(
        dq, dkv, x3, dx4, gains["pre_mix_g"], kv_g2d, weight("w_q"), weight("w_kv"))

    group_b = ["w_o", "w_q", "w_kv"]
    sums_b = pair_stage("b", group_b)
    (dx2, local["w_ple_gate0"], local["w_ple_proj0"], d_ple0, d_plepost0), got = _ple_bwd(
        0, dx3, x2, z0, pe0, p3d, gains["ple_g"], weight("w_ple_gate0"), gains["ple_post_g"],
        rider=scatter(group_b, sums_b))
    keep(group_b, sums_b, got)

    group_c = ["w_ple_gate0", "w_ple_proj0"]
    sums_c = pair_stage("c", group_c)
    (dx1, d_preffn0, d_postffn0, *scattered), _ = _ffn_bwd(
        0, dx2, x1, f0, gains["pre_ffn_g"], weight("w_gu0"), weight("w_down0"), gains["post_ffn_g"], kc)
    fused["w_gu0"], fused["w_down0"] = scattered[0:2], scattered[2:4]

    (dx0, d_pool, d_scale, d_postmix0, d_premix0), _ = _mixa_bwd(
        dx1, x2d, y0, gains["pre_mix_g"], weight("pool_w"), weight("pool_scale"), gains["post_mix_g"])

    rows = [d_premix0, d_premix1, d_postmix0, d_postmix1, d_preffn0, d_preffn1, d_postffn0, d_postffn1,
            d_ple0, d_ple1, d_plepost0, d_plepost1, d_kvg, d_scale, d_sinks, loss_row]
    as2d = lambda a: a.reshape(1, D) if a.ndim == 1 else a
    (tot, g_pool), got = _small_all_reduce(rows, d_pool, rider=scatter(group_c, sums_c))
    keep(group_c, sums_c, got)
    loss, small = _small_adamw(tot, kc, {nm: as2d(weights[nm]) for nm in SMALL_NAMES},
                               {nm: as2d(m_in[nm]) for nm in SMALL_NAMES},
                               {nm: as2d(v_in[nm]) for nm in SMALL_NAMES})

    layers_of = lambda src: [t for t in BIGS.values() if t.src == src]
    own_scatter = ["w_gu", "w_down"]
    others = [src for src in BIG_SOURCES if src not in own_scatter and src != "pool_w"]
    halves = {}
    for src in own_scatter:
        acc = None
        for t in layers_of(src):
            own, land = fused[t.name]
            acc = _chip_sum_fused(t, own, land, kc, len(layers_of(src)), acc, by_cols=src == "w_down")
        halves[src] = acc
    sums = _multi_call("chip_sum_rest", [_chip_sum_job(layers_of(src), landed) for src in others], kc)
    halves.update({src: r[0] for src, r in zip(others, sums)})
    shared = own_scatter + others
    full_grads = dict(zip(shared, _pair_share([halves[src] for src in shared], [src == "w_down" for src in shared])))
    full_grads["pool_w"] = g_pool

    def adam_args(src):
        return (layers_of(src)[0].rb, shard_view(src, weights[src]), full_grads[src],
                shard_view(src, m_in[src]), shard_view(src, v_in[src]))

    out = {"grad": {}, "delta": {}, "new_m": {}, "new_v": {}}
    results = {src: _adamw(src, *adam_args(src)) for src in own_scatter}
    rest_srcs = others + ["pool_w"]
    results.update(zip(rest_srcs, _multi_call("adamw_rest", [_adamw_job(*adam_args(src)) for src in rest_srcs], kc)))
    for src in BIG_SOURCES:
        shape = weights[src].shape
        for kind, a in zip(("grad", "delta", "new_m", "new_v"), results[src]):
            out[kind][src] = a.reshape(shape)
    for nm in SMALL_NAMES:
        shape = weights[nm].shape
        for kind, a in zip(("grad", "delta", "new_m", "new_v"), small[nm]):
            out[kind][nm] = a.reshape(shape)

    return (loss.reshape(()), dx0.reshape(x.shape),
            *[out["grad"][nm] for nm in order], *[out["delta"][nm] for nm in order],
            *[out["new_m"][nm] for nm in order], *[out["new_v"][nm] for nm in order])
```

```python
import collections

import jax
import jax.numpy as jnp
from jax import lax
from jax.experimental import pallas as pl
from jax.experimental.pallas import tpu as pltpu

D = 1024
FF = 2816
N_HEADS = 16
HEAD_DIM = 64
N_KV_HEADS = 4
GQA = N_HEADS // N_KV_HEADS
KVD = N_KV_HEADS * HEAD_DIM
PLE = 256
BLK = 128
WINDOWS = (2, 4, 8, 16)
POOL_G = 256
HALO = 16
EPS = 1e-6
NEG_INF = -1e30
ATT_SCALE = HEAD_DIM ** -0.5
SLOPES = tuple(2.0 ** (-8.0 * (h + 1) / N_HEADS) for h in range(N_HEADS))
N_CHIPS = 4
N_DEV = 8

LR, B1, B2, AEPS, WD, STEP = 0.001, 0.9, 0.999, 1e-08, 0.01, 10
BC1 = 1.0 - B1 ** STEP
BC2 = 1.0 - B2 ** STEP

BF = jnp.bfloat16
F32 = jnp.float32
MESH = pl.DeviceIdType.MESH
VMEM_LIMIT_V7X = 58 * 1024 * 1024
TM = 512
TM_FFN_BWD = 512
FF_CHUNK = 256
FF_HALF = FF // 2

VSPEC = pl.BlockSpec(memory_space=pltpu.VMEM)
SSPEC = pl.BlockSpec(memory_space=pltpu.SMEM)
ANYSPEC = pl.BlockSpec(memory_space=pl.ANY)


def _params(n_grid=0):
    sem = ("arbitrary",) * n_grid if n_grid else None
    return pltpu.CompilerParams(dimension_semantics=sem, vmem_limit_bytes=VMEM_LIMIT_V7X)


def _sds(shape, dtype=F32):
    return jax.ShapeDtypeStruct(tuple(shape), dtype)


Rider = collections.namedtuple("Rider", "arrays out_shapes aliases scratch start mid finish")
MID_NUM, MID_DEN = 5, 8


def _call(body, *, name, grid, in_specs, out_specs, out_shape, args, scratch_shapes=(), rider=None, prefetch=None):
    ni, no, ns = len(in_specs), len(out_specs), len(scratch_shapes)
    npre = 0 if prefetch is None else 1
    pre = [] if prefetch is None else [prefetch]
    if rider is None:
        rider = Rider([], [], {}, [], None, None, None)
    ri, ro = len(rider.arrays), len(rider.out_shapes)

    def full(*refs):
        pre_refs, refs = refs[:npre], refs[npre:]
        ins, refs = refs[:ni], refs[ni:]
        rins, refs = refs[:ri], refs[ri:]
        outs, refs = refs[:no], refs[no:]
        routs, refs = refs[:ro], refs[ro:]
        scr, rscr = refs[:ns], refs[ns:]
        ids = [pl.program_id(a) for a in range(len(grid))]
        first = ids[0] == 0
        last = ids[0] == grid[0] - 1
        for a in range(1, len(grid)):
            first = first & (ids[a] == 0)
            last = last & (ids[a] == grid[a] - 1)

        if rider.start is not None:
            @pl.when(first)
            def _():
                rider.start(rins, routs, rscr)

        if rider.mid is not None:
            assert len(grid) == 1

            @pl.when(ids[0] == (grid[0] * MID_NUM) // MID_DEN)
            def _():
                rider.mid(rins, routs, rscr)

        body(*pre_refs, *ins, *outs, *scr)

        if rider.finish is not None:
            @pl.when(last)
            def _():
                rider.finish(rins, routs, rscr)

    outs = pl.pallas_call(
        full, name=name,
        grid_spec=pltpu.PrefetchScalarGridSpec(
            num_scalar_prefetch=npre, grid=grid,
            in_specs=list(in_specs) + [ANYSPEC] * ri, out_specs=list(out_specs) + [ANYSPEC] * ro,
            scratch_shapes=list(scratch_shapes) + list(rider.scratch)),
        out_shape=list(out_shape) + list(rider.out_shapes),
        input_output_aliases={npre + ni + a: no + b for a, b in rider.aliases.items()},
        compiler_params=_params(len(grid)))(*pre, *args, *rider.arrays)
    return list(outs[:no]), list(outs[no:])


def _run(name, rider):
    ri = len(rider.arrays)

    def body(*refs):
        rins, routs, rscr = refs[:ri], refs[ri:ri + len(rider.out_shapes)], refs[ri + len(rider.out_shapes):]
        rider.start(rins, routs, rscr)
        if rider.mid is not None:
            rider.mid(rins, routs, rscr)
        rider.finish(rins, routs, rscr)

    return pl.pallas_call(
        body, name=name, in_specs=[ANYSPEC] * ri, out_specs=[ANYSPEC] * len(rider.out_shapes),
        out_shape=list(rider.out_shapes), scratch_shapes=list(rider.scratch),
        input_output_aliases=dict(rider.aliases), compiler_params=_params())(*rider.arrays)


Job = collections.namedtuple("Job", "steps ins outs fn")


def _multi_call(name, jobs, kc):
    n = max(job.steps for job in jobs)

    def clamped(index, steps):
        return lambda s, kc_ref: index(jnp.minimum(s, steps - 1), kc_ref)

    in_specs, out_specs, out_shape, args = [], [], [], []
    for job in jobs:
        for arr, block, index in job.ins:
            in_specs.append(pl.BlockSpec(block, clamped(index, job.steps)))
            args.append(arr)
        for sds, block, index in job.outs:
            out_specs.append(pl.BlockSpec(block, clamped(index, job.steps)))
            out_shape.append(sds)
    n_in = len(args)

    def body(kc_ref, *refs):
        s = pl.program_id(0)
        i0, o0 = 0, n_in
        for job in jobs:
            ins, outs = refs[i0:i0 + len(job.ins)], refs[o0:o0 + len(job.outs)]
            i0, o0 = i0 + len(job.ins), o0 + len(job.outs)

            @pl.when(s < job.steps)
            def _():
                job.fn(s, kc_ref, ins, outs)

    outs, _ = _call(body, name=name, grid=(n,), in_specs=in_specs, out_specs=out_specs, out_shape=out_shape,
                    args=args, prefetch=kc)
    res, o0 = [], 0
    for job in jobs:
        res.append(outs[o0:o0 + len(job.outs)])
        o0 += len(job.outs)
    return res


def _rms_fwd(x, g):
    r = lax.rsqrt(jnp.mean(x * x, axis=-1, keepdims=True) + EPS)
    return x * r * g


def _rms_bwd(x, g, dy):
    r = lax.rsqrt(jnp.mean(x * x, axis=-1, keepdims=True) + EPS)
    xn = x * r
    dxn = dy * g
    dx = r * (dxn - xn * jnp.mean(dxn * xn, axis=-1, keepdims=True))
    return dx, dy * xn


def _rowsum(a):
    return jnp.sum(a, axis=0, keepdims=True)


def _sigmoid(z):
    return 1.0 / (1.0 + jnp.exp(-z))


def _dot(a, b):
    return jnp.dot(a, b, preferred_element_type=F32)


def _dot_nt(a, b):
    return lax.dot_general(a, b, (((1,), (1,)), ((), ())), preferred_element_type=F32)


def _dot_tn(a, b):
    return lax.dot_general(a, b, (((0,), (0,)), ((), ())), preferred_element_type=F32)


def _row_spec(tm, width=D):
    return pl.BlockSpec((tm, width), lambda i: (i, 0))


def _const_spec(shape):
    zeros = (0,) * len(shape)
    return pl.BlockSpec(tuple(shape), lambda *_: zeros)


def _pool_delta(he, pos):
    out = []
    for gi, w in enumerate(WINDOWS):
        hg = he[:, gi * POOL_G:(gi + 1) * POOL_G]
        s = hg
        k = 1
        while k < w:
            s = s + pltpu.roll(s, k, 0)
            k *= 2
        cnt = jnp.maximum(jnp.minimum(pos + 1, w), 1).astype(F32)
        out.append(s / cnt - hg)
    return out


def _load_with_halo_before(x_ref, i, tm):
    r0 = pl.multiple_of(i * tm, tm)
    hs = pl.multiple_of(jnp.maximum(i * tm - HALO, 0), 8)
    xh = jnp.where(i > 0, x_ref[pl.ds(hs, HALO), :], 0.0)
    xt = x_ref[pl.ds(r0, tm), :]
    return xt, jnp.concatenate([xh, xt], axis=0)


def _mixa_fwd(x, pre_g, pool_w, pool_scale, post_g):
    s_len = x.shape[0]
    n = s_len // TM

    def body(x_ref, pg_ref, w_ref, sc_ref, qg_ref, y_ref, x1_ref):
        i = pl.program_id(0)
        xt, xe = _load_with_halo_before(x_ref, i, TM)
        he = _rms_fwd(xe, pg_ref[0:1, :])
        pos = i * TM - HALO + lax.broadcasted_iota(jnp.int32, (TM + HALO, 1), 0)
        ds = _pool_delta(he, pos)
        ys = [_dot(ds[gi][HALO:, :].astype(BF), w_ref[gi]) for gi in range(len(WINDOWS))]
        y = jnp.concatenate(ys, axis=1) * sc_ref[...]
        y_ref[...] = y
        x1_ref[...] = xt + _rms_fwd(y, qg_ref[0:1, :])

    outs, _ = _call(body, name="mixa_fwd", grid=(n,),
                    in_specs=[VSPEC] * 5, out_specs=[_row_spec(TM), _row_spec(TM)],
                    out_shape=[_sds((s_len, D)), _sds((s_len, D))],
                    args=[x, pre_g, pool_w, pool_scale, post_g])
    return outs


def _mixa_bwd(dx1, x, y, pre_g, pool_w, pool_scale, post_g, rider=None):
    s_len = x.shape[0]
    n = s_len // TM
    ng = len(WINDOWS)

    def body(dx_ref, x_ref, y_ref, pg_ref, w_ref, sc_ref, qg_ref,
             dx0_ref, dw_ref, dsc_ref, dqg_ref, dpg_ref, wacc):
        i = pl.program_id(0)

        @pl.when(i == 0)
        def _():
            wacc[...] = jnp.zeros_like(wacc)
            dsc_ref[...] = jnp.zeros_like(dsc_ref)
            dqg_ref[...] = jnp.zeros_like(dqg_ref)
            dpg_ref[...] = jnp.zeros_like(dpg_ref)

        r0 = pl.multiple_of(i * TM, TM)
        xt, xe = _load_with_halo_before(x_ref, i, TM)
        he = _rms_fwd(xe, pg_ref[0:1, :])
        pos_b = i * TM - HALO + lax.broadcasted_iota(jnp.int32, (TM + HALO, 1), 0)
        ds = _pool_delta(he, pos_b)

        last = i == n - 1
        a0 = pl.multiple_of(jnp.minimum(i * TM + TM, s_len - HALO), 8)
        ye = jnp.concatenate([y_ref[pl.ds(r0, TM), :], y_ref[pl.ds(a0, HALO), :]], axis=0)
        dt = dx_ref[pl.ds(r0, TM), :]
        de = jnp.concatenate([dt, jnp.where(last, 0.0, dx_ref[pl.ds(a0, HALO), :])], axis=0)
        dye, prod = _rms_bwd(ye, qg_ref[0:1, :], de)
        dqg_ref[...] += _rowsum(prod[:TM, :])
        dys = dye * sc_ref[...]
        pos_a = i * TM + lax.broadcasted_iota(jnp.int32, (TM + HALO, 1), 0)

        dhs, dscs = [], []
        for gi, w in enumerate(WINDOWS):
            sl = slice(gi * POOL_G, (gi + 1) * POOL_G)
            wg = w_ref[gi]
            dys_g = dys[:, sl].astype(BF)
            d_g = ds[gi][HALO:, :].astype(BF)
            ypre = _dot(d_g, wg)
            dscs.append(_rowsum(dye[:TM, sl] * ypre))
            wacc[gi] += _dot_tn(d_g, dys_g[:TM, :])
            dd = _dot_nt(dys_g, wg)
            cnt = jnp.minimum(pos_a + 1, w).astype(F32)
            a = dd / cnt
            k = 1
            while k < w:
                a = a + pltpu.roll(a, TM + HALO - k, 0)
                k *= 2
            dhs.append(a[:TM, :] - dd[:TM, :])
        dsc_ref[...] += jnp.concatenate(dscs, axis=1)
        dh = jnp.concatenate(dhs, axis=1)
        dxp, prod2 = _rms_bwd(xt, pg_ref[0:1, :], dh)
        dpg_ref[...] += _rowsum(prod2)
        dx0_ref[...] = dt + dxp

        @pl.when(last)
        def _():
            dw_ref[...] = wacc[...].astype(BF)

    return _call(
        body, name="mixa_bwd", grid=(n,), in_specs=[VSPEC] * 7,
        out_specs=[_row_spec(TM), _const_spec((ng, POOL_G, POOL_G)), _const_spec((1, D)),
                   _const_spec((1, D)), _const_spec((1, D))],
        out_shape=[_sds((s_len, D)), _sds((ng, POOL_G, POOL_G), BF), _sds((1, D)), _sds((1, D)), _sds((1, D))],
        scratch_shapes=[pltpu.VMEM((ng, POOL_G, POOL_G), F32)],
        args=[dx1, x, y, pre_g, pool_w, pool_scale, post_g], rider=rider)


def _ffn_fwd(layer, x1, pre_g, wgu, wd, post_g, rider=None):
    s_len = x1.shape[0]

    def body(x_ref, pg_ref, wgu_ref, wd_ref, qg_ref, f_ref, x2_ref):
        x = x_ref[...]
        h = _rms_fwd(x, pg_ref[layer:layer + 1, :]).astype(BF)
        f = jnp.zeros((TM, D), F32)
        for c in range(FF // FF_HALF):
            cols = slice(c * FF_HALF, (c + 1) * FF_HALF)
            g = _dot(h, wgu_ref[0, :, cols])
            u = _dot(h, wgu_ref[1, :, cols])
            act = g * _sigmoid(g) * u
            f = f + _dot(act.astype(BF), wd_ref[cols, :])
        f_ref[...] = f
        x2_ref[...] = x + _rms_fwd(f, qg_ref[layer:layer + 1, :])

    return _call(body, name=f"ffn_fwd{layer}", grid=(s_len // TM,),
                 in_specs=[_row_spec(TM), VSPEC, VSPEC, VSPEC, VSPEC],
                 out_specs=[_row_spec(TM), _row_spec(TM)],
                 out_shape=[_sds((s_len, D)), _sds((s_len, D))],
                 args=[x1, pre_g, wgu, wd, post_g], rider=rider)


GU_PIECE = 128
DN_PIECE = 64
DN_SLOT = FF // N_CHIPS
HALF_D = D // 2


def _ffn_bwd(layer, dx2, x1, f, pre_g, wgu, wd, post_g, kc, rider=None):
    s_len = x1.shape[0]
    tm = TM_FFN_BWD
    n = s_len // tm
    nc = FF // FF_CHUNK
    n_gu, n_dn = FF_CHUNK // GU_PIECE, FF_CHUNK // DN_PIECE
    n_pieces = 2 * n_gu + n_dn
    n_blk = FF_HALF // GU_PIECE

    def edge_rows(c, i, kc_ref):
        return (jnp.where((c == 0) | (c == nc - 1), i, n - 1), 0)

    def chunk_at(c, kc_ref):
        return (c + (((kc_ref[0] + 1) % N_CHIPS) * nc) // N_CHIPS) % nc

    def exchange(kc_ref, c, accg, accu, accd, own_gu_ref, land_gu_ref, own_dn_ref, land_dn_ref,
                 pl_gu, pl_dn, sib_gu, sib_dn, mine_gu, mine_dn, sum_gu, sum_dn,
                 psend, precv, ssend, lsem, rrecv):
        x, y, core = lax.axis_index("x"), lax.axis_index("y"), lax.axis_index("c")
        lower = core == 0

        def pair_copy(cc, part):
            p = cc % 2
            src, dst = ((sib_gu, pl_gu), (sib_dn, pl_dn))[part]
            return pltpu.make_async_remote_copy(src.at[p], dst.at[cc], psend.at[p, part], precv.at[cc, part],
                                                device_id=(x, y, 1 - core), device_id_type=MESH)

        def scatter(cc, wait):
            p = cc % 2
            jobs = []
            for gu in range(2):
                for hc in range(n_gu):
                    hidden = chunk_at(cc, kc_ref) * FF_CHUNK + hc * GU_PIECE
                    k = hidden // FF_HALF
                    jobs.append((sum_gu.at[p, gu, hc], k + 2 * gu, 0,
                                 own_gu_ref, land_gu_ref, ((hidden - k * FF_HALF) // GU_PIECE,)))
            for q in range(n_dn):
                hidden = chunk_at(cc, kc_ref) * FF_CHUNK + q * DN_PIECE
                k = hidden // DN_SLOT
                off = pl.multiple_of(hidden - k * DN_SLOT, DN_PIECE)
                jobs.append((sum_dn.at[p, pl.ds(q * DN_PIECE, DN_PIECE), :], k, 1,
                             own_dn_ref, land_dn_ref, (pl.ds(off, DN_PIECE), slice(None))))
            for pi, (src, k, t, own_ref, land_ref, where) in enumerate(jobs):
                kx, ky = k // 2, k % 2
                fx, fy = (kx != x).astype(jnp.int32), (ky != y).astype(jnp.int32)
                local = (fx + fy) == 0
                j = jnp.maximum(fx + 2 * fy - 1, 0)

                @pl.when(local)
                def _():
                    cp = pltpu.make_async_copy(src, own_ref.at[where], lsem.at[p, pi])
                    if wait:
                        cp.wait()
                    else:
                        cp.start()

                @pl.when(jnp.logical_not(local))
                def _():
                    cp = pltpu.make_async_remote_copy(src, land_ref.at[(j,) + where], ssend.at[p, pi],
                                                      rrecv.at[t, j], device_id=(kx, ky, core), device_id_type=MESH)
                    if wait:
                        cp.wait_send()
                    else:
                        cp.start()

        def add_and_scatter(cc):
            p = cc % 2
            pair_copy(cc, 0).wait_recv()
            pair_copy(cc, 1).wait_recv()
            s_gu = (mine_gu[...] + pl_gu[cc].astype(F32)).astype(BF)
            for hc in range(n_gu):
                sum_gu[p, :, hc] = s_gu[:, :, hc * GU_PIECE:(hc + 1) * GU_PIECE]
            sum_dn[p] = (mine_dn[...] + pl_dn[cc].astype(F32)).astype(BF)
            scatter(cc, wait=False)

        @pl.when(c >= 1)
        def _():
            @pl.when(c >= 3)
            def _():
                scatter(c - 3, wait=True)
            add_and_scatter(c - 1)

        @pl.when(c >= 2)
        def _():
            pair_copy(c - 2, 0).wait_send()
            pair_copy(c - 2, 1).wait_send()

        p = c % 2
        my_rows = pl.ds(pl.multiple_of(core * HALF_D, HALF_D), HALF_D)
        sib_rows = pl.ds(pl.multiple_of((1 - core) * HALF_D, HALF_D), HALF_D)
        d_v = accd[...]
        sib_gu[p, 0] = accg[sib_rows, :].astype(BF)
        sib_gu[p, 1] = accu[sib_rows, :].astype(BF)
        sib_dn[p] = jnp.where(lower, d_v[:, HALF_D:], d_v[:, :HALF_D]).astype(BF)
        mine_gu[0] = accg[my_rows, :]
        mine_gu[1] = accu[my_rows, :]
        mine_dn[...] = jnp.where(lower, d_v[:, :HALF_D], d_v[:, HALF_D:])
        pair_copy(c, 0).start()
        pair_copy(c, 1).start()

        @pl.when(c == nc - 1)
        def _():
            scatter(nc - 3, wait=True)
            add_and_scatter(nc - 1)
            for cc in (nc - 2, nc - 1):
                pair_copy(cc, 0).wait_send()
                pair_copy(cc, 1).wait_send()
                scatter(cc, wait=True)
            for t, land_ref in enumerate((land_gu_ref, land_dn_ref)):
                for j in range(N_CHIPS - 1):
                    pltpu.make_async_remote_copy(land_ref.at[j], land_ref.at[j], ssend.at[0, 0], rrecv.at[t, j],
                                                 device_id=(x, y, core), device_id_type=MESH).wait_recv()

    def body(kc_ref, dx_ref, x_ref, f_ref, pg_ref, wgu_ref, wd_ref, qg_ref,
             dx1_ref, dpg_ref, dqg_ref, own_gu_ref, land_gu_ref, own_dn_ref, land_dn_ref,
             h_s, df_s, dh_s, accg, accu, accd, *comm):
        c = pl.program_id(0)
        i = pl.program_id(1)
        rows = pl.ds(pl.multiple_of(i * tm, tm), tm)
        pg = pg_ref[layer:layer + 1, :]

        @pl.when((c == 0) & (i == 0))
        def _():
            dpg_ref[...] = jnp.zeros_like(dpg_ref)
            dqg_ref[...] = jnp.zeros_like(dqg_ref)

        @pl.when(c == 0)
        def _():
            h_s[rows, :] = _rms_fwd(x_ref[...], pg).astype(BF)
            df, prod = _rms_bwd(f_ref[...], qg_ref[layer:layer + 1, :], dx_ref[...])
            df_s[rows, :] = df.astype(BF)
            dqg_ref[...] += _rowsum(prod)

        @pl.when(i == 0)
        def _():
            accg[...] = jnp.zeros_like(accg)
            accu[...] = jnp.zeros_like(accu)
            accd[...] = jnp.zeros_like(accd)

        h = h_s[rows, :]
        df = df_s[rows, :]
        wg = wgu_ref[0]
        wu = wgu_ref[1]
        g = _dot(h, wg)
        u = _dot(h, wu)
        sg = _sigmoid(g)
        a = g * sg
        dact = _dot_nt(df, wd_ref[...])
        accd[...] += _dot_tn((a * u).astype(BF), df)
        du = (dact * a).astype(BF)
        dg = (dact * u * (sg * (1.0 + g * (1.0 - sg)))).astype(BF)
        accg[...] += _dot_tn(h, dg)
        accu[...] += _dot_tn(h, du)
        dh = _dot_nt(dg, wg) + _dot_nt(du, wu)

        @pl.when(c == 0)
        def _():
            dh_s[rows, :] = dh

        @pl.when((c > 0) & (c < nc - 1))
        def _():
            dh_s[rows, :] += dh

        @pl.when(c == nc - 1)
        def _():
            dxp, prod = _rms_bwd(x_ref[...], pg, dh_s[rows, :] + dh)
            dpg_ref[...] += _rowsum(prod)
            dx1_ref[...] = dx_ref[...] + dxp

        @pl.when(i == n - 1)
        def _():
            exchange(kc_ref, c, accg, accu, accd, own_gu_ref, land_gu_ref, own_dn_ref, land_dn_ref, *comm)

    dma = pltpu.SemaphoreType.DMA
    return _call(
        body, name=f"ffn_bwd{layer}", grid=(nc, n),
        in_specs=[pl.BlockSpec((tm, D), edge_rows), pl.BlockSpec((tm, D), edge_rows),
                  pl.BlockSpec((tm, D), lambda c, i, kc_ref: (jnp.where(c == 0, i, n - 1), 0),
                               pipeline_mode=pl.Buffered(1)),
                  VSPEC,
                  pl.BlockSpec((2, D, FF_CHUNK), lambda c, i, kc_ref: (0, 0, chunk_at(c, kc_ref))),
                  pl.BlockSpec((FF_CHUNK, D), lambda c, i, kc_ref: (chunk_at(c, kc_ref), 0)),
                  VSPEC],
        out_specs=[pl.BlockSpec((tm, D), lambda c, i, kc_ref: (jnp.where(c == nc - 1, i, 0), 0)),
                   _const_spec((1, D)), _const_spec((1, D)), ANYSPEC, ANYSPEC, ANYSPEC, ANYSPEC],
        out_shape=[_sds((s_len, D)), _sds((1, D)), _sds((1, D)),
                   _sds((n_blk, HALF_D, GU_PIECE), BF), _sds((N_CHIPS - 1, n_blk, HALF_D, GU_PIECE), BF),
                   _sds((DN_SLOT, HALF_D), BF), _sds((N_CHIPS - 1, DN_SLOT, HALF_D), BF)],
        scratch_shapes=[pltpu.VMEM((s_len, D), BF), pltpu.VMEM((s_len, D), BF), pltpu.VMEM((s_len, D), F32),
                        pltpu.VMEM((D, FF_CHUNK), F32), pltpu.VMEM((D, FF_CHUNK), F32),
                        pltpu.VMEM((FF_CHUNK, D), F32),
                        pltpu.VMEM((nc, 2, HALF_D, FF_CHUNK), BF), pltpu.VMEM((nc, FF_CHUNK, HALF_D), BF),
                        pltpu.VMEM((2, 2, HALF_D, FF_CHUNK), BF), pltpu.VMEM((2, FF_CHUNK, HALF_D), BF),
                        pltpu.VMEM((2, HALF_D, FF_CHUNK), F32), pltpu.VMEM((FF_CHUNK, HALF_D), F32),
                        pltpu.VMEM((2, 2, n_gu, HALF_D, GU_PIECE), BF), pltpu.VMEM((2, FF_CHUNK, HALF_D), BF),
                        dma((2, 2)), dma((nc, 2)), dma((2, n_pieces)), dma((2, n_pieces)), dma((2, N_CHIPS - 1))],
        args=[dx2, x1, f, pre_g, wgu, wd, post_g], rider=rider, prefetch=kc)


def _ple_fwd(layer, x2, p, ple_g, w_gate, w_proj, post_g, target=None, qkv=None, rider=None):
    s_len = x2.shape[0]
    final = target is not None
    assert not (final and qkv)

    def body(*refs):
        if final:
            x_ref, p_ref, g_ref, wg_ref, wp_ref, qg_ref, t_ref, z_ref, pe_ref, dx_ref, lv_ref = refs
        elif qkv:
            (x_ref, p_ref, g_ref, wg_ref, wp_ref, qg_ref, ng_ref, kg_ref, wq_ref, wkv_ref,
             z_ref, pe_ref, x3_ref, q_ref, kv_ref) = refs
        else:
            x_ref, p_ref, g_ref, wg_ref, wp_ref, qg_ref, z_ref, pe_ref, x3_ref = refs
        x = x_ref[...]
        r = _rms_fwd(x, g_ref[layer:layer + 1, :]).astype(BF)
        z = _dot(r, wg_ref[...])
        pe = _dot(p_ref[...].astype(BF), wp_ref[...])
        z_ref[...] = z
        pe_ref[...] = pe
        x3 = x + _rms_fwd(pe * _sigmoid(z), qg_ref[layer:layer + 1, :])
        if final:
            @pl.when(pl.program_id(0) == 0)
            def _():
                lv_ref[...] = jnp.zeros_like(lv_ref)
            err = x3 - t_ref[...]
            dx_ref[...] = err * (1.0 / D)
            lv_ref[...] += _rowsum(err * err)
        else:
            x3_ref[...] = x3
        if qkv:
            q_ref[...] = _dot(_rms_fwd(x3, ng_ref[layer + 1:layer + 2, :]).astype(BF), wq_ref[...]).astype(BF)
            kv_ref[...] = _dot(_rms_fwd(x3, kg_ref[...]).astype(BF), wkv_ref[...]).astype(BF)

    p_spec = pl.BlockSpec((None, TM, PLE), lambda i: (layer, i, 0))
    in_specs = [_row_spec(TM), p_spec, VSPEC, VSPEC, VSPEC, VSPEC]
    args = [x2, p, ple_g, w_gate, w_proj, post_g]
    out_specs = [_row_spec(TM), _row_spec(TM), _row_spec(TM)]
    out_shape = [_sds((s_len, D))] * 3
    if qkv:
        in_specs += [VSPEC] * 4
        args += list(qkv)
        out_specs += [_row_spec(TM), _row_spec(TM, 2 * KVD)]
        out_shape += [_sds((s_len, D), BF), _sds((s_len, 2 * KVD), BF)]
    if final:
        in_specs.append(_row_spec(TM))
        args.append(target)
        out_specs.append(_const_spec((1, D)))
        out_shape.append(_sds((1, D)))
    return _call(body, name=f"ple_fwd{layer}", grid=(s_len // TM,), in_specs=in_specs, out_specs=out_specs,
                 out_shape=out_shape, args=args, rider=rider)


def _ple_bwd(layer, dx3, x2, z, pe, p, ple_g, w_gate, post_g, rider=None):
    s_len = x2.shape[0]
    n = s_len // TM

    def body(dx_ref, x_ref, z_ref, pe_ref, p_ref, g_ref, wg_ref, qg_ref,
             dx2_ref, dwg_ref, dwp_ref, dg_ref, dqg_ref, gacc, pacc):
        i = pl.program_id(0)

        @pl.when(i == 0)
        def _():
            gacc[...] = jnp.zeros_like(gacc)
            pacc[...] = jnp.zeros_like(pacc)
            dg_ref[...] = jnp.zeros_like(dg_ref)
            dqg_ref[...] = jnp.zeros_like(dqg_ref)

        dx = dx_ref[...]
        x = x_ref[...]
        pe_v = pe_ref[...]
        gate = _sigmoid(z_ref[...])
        de, prod = _rms_bwd(pe_v * gate, qg_ref[layer:layer + 1, :], dx)
        dqg_ref[...] += _rowsum(prod)
        dpe = (de * gate).astype(BF)
        dz = (de * pe_v * gate * (1.0 - gate)).astype(BF)
        pacc[...] += _dot_tn(p_ref[...].astype(BF), dpe)
        g = g_ref[layer:layer + 1, :]
        r = _rms_fwd(x, g).astype(BF)
        gacc[...] += _dot_tn(r, dz)
        dr = _dot_nt(dz, wg_ref[...])
        dxp, prod2 = _rms_bwd(x, g, dr)
        dg_ref[...] += _rowsum(prod2)
        dx2_ref[...] = dx + dxp

        @pl.when(i == n - 1)
        def _():
            dwg_ref[...] = gacc[...].astype(BF)
            dwp_ref[...] = pacc[...].astype(BF)

    p_spec = pl.BlockSpec((None, TM, PLE), lambda i: (layer, i, 0))
    return _call(
        body, name=f"ple_bwd{layer}", grid=(n,),
        in_specs=[_row_spec(TM), _row_spec(TM), _row_spec(TM), _row_spec(TM), p_spec, VSPEC, VSPEC, VSPEC],
        out_specs=[_row_spec(TM), _const_spec((D, D)), _const_spec((PLE, D)), _const_spec((1, D)), _const_spec((1, D))],
        out_shape=[_sds((s_len, D)), _sds((D, D), BF), _sds((PLE, D), BF), _sds((1, D)), _sds((1, D))],
        scratch_shapes=[pltpu.VMEM((D, D), F32), pltpu.VMEM((PLE, D), F32)],
        args=[dx3, x2, z, pe, p, ple_g, w_gate, post_g], rider=rider)


def _qkv_bwd(dq, dkv, x3, dx4, q_g, kv_g, w_q, w_kv):
    s_len = x3.shape[0]
    n = s_len // TM

    def body(dq_ref, dkv_ref, x_ref, dx_ref, qg_ref, kg_ref, wq_ref, wkv_ref,
             dx3_ref, dwq_ref, dwkv_ref, dqg_ref, dkg_ref, qacc, kacc):
        i = pl.program_id(0)

        @pl.when(i == 0)
        def _():
            qacc[...] = jnp.zeros_like(qacc)
            kacc[...] = jnp.zeros_like(kacc)
            dqg_ref[...] = jnp.zeros_like(dqg_ref)
            dkg_ref[...] = jnp.zeros_like(dkg_ref)

        x = x_ref[...]
        qg = qg_ref[1:2, :]
        kg = kg_ref[...]
        dq_v = dq_ref[...]
        dkv_v = dkv_ref[...].astype(BF)
        qacc[...] += _dot_tn(_rms_fwd(x, qg).astype(BF), dq_v)
        kacc[...] += _dot_tn(_rms_fwd(x, kg).astype(BF), dkv_v)
        dxq, prod_q = _rms_bwd(x, qg, _dot_nt(dq_v, wq_ref[...]))
        dxk, prod_k = _rms_bwd(x, kg, _dot_nt(dkv_v, wkv_ref[...]))
        dqg_ref[...] += _rowsum(prod_q)
        dkg_ref[...] += _rowsum(prod_k)
        dx3_ref[...] = dx_ref[...] + dxq + dxk

        @pl.when(i == n - 1)
        def _():
            dwq_ref[...] = qacc[...].astype(BF)
            dwkv_ref[...] = kacc[...].astype(BF)

    outs, _ = _call(
        body, name="qkv_bwd", grid=(n,),
        in_specs=[_row_spec(TM), _row_spec(TM, 2 * KVD), _row_spec(TM), _row_spec(TM), VSPEC, VSPEC, VSPEC, VSPEC],
        out_specs=[_row_spec(TM), _const_spec((D, D)), _const_spec((D, 2 * KVD)),
                   _const_spec((1, D)), _const_spec((1, D))],
        out_shape=[_sds((s_len, D)), _sds((D, D), BF), _sds((D, 2 * KVD), BF), _sds((1, D)), _sds((1, D))],
        scratch_shapes=[pltpu.VMEM((D, D), F32), pltpu.VMEM((D, 2 * KVD), F32)],
        args=[dq, dkv, x3, dx4, q_g, kv_g, w_q, w_kv])
    return outs


def _attn_group(i, q, kvw, sink_ref, g):
    rows = GQA * BLK
    heads = [GQA * g + j for j in range(GQA)]
    off = jnp.where(i > 0, BLK, 0)
    row = lax.broadcasted_iota(jnp.int32, (rows, 2 * BLK), 0)
    rel = (row % BLK) - lax.broadcasted_iota(jnp.int32, (rows, 2 * BLK), 1) + off
    valid = (rel >= 0) & (rel < BLK)
    head_of_row = lax.broadcasted_iota(jnp.int32, (rows, 1), 0) // BLK
    slope = jnp.zeros((rows, 1), F32)
    sink = jnp.zeros((rows, 1), F32)
    for j, h in enumerate(heads):
        slope = jnp.where(head_of_row == j, SLOPES[h], slope)
        sink = jnp.where(head_of_row == j, sink_ref[0, h], sink)
    qs = jnp.concatenate([q[:, h * HEAD_DIM:(h + 1) * HEAD_DIM] for h in heads], axis=0)
    k = kvw[:, g * HEAD_DIM:(g + 1) * HEAD_DIM]
    v = kvw[:, KVD + g * HEAD_DIM:KVD + (g + 1) * HEAD_DIM]
    s = _dot_nt(qs, k) * ATT_SCALE - slope * rel.astype(F32)
    s = jnp.where(valid, s, NEG_INF)
    m = jnp.maximum(jnp.max(s, axis=-1, keepdims=True), sink)
    e = jnp.exp(s - m)
    es = jnp.exp(sink - m)
    inv = 1.0 / (jnp.sum(e, axis=-1, keepdims=True) + es)
    return e * inv, es * inv, qs, k, v


def _unstack_heads(stacked):
    return [stacked[j * BLK:(j + 1) * BLK, :] for j in range(GQA)]


def _kv_window(kv_ref, i):
    ks = pl.multiple_of(jnp.maximum(i * BLK - BLK, 0), BLK)
    return ks, kv_ref[pl.ds(ks, 2 * BLK), :]


def _attn_fwd(q, kv, sinks, x3, w_o, post_g, rider=None):
    s_len = q.shape[0]

    def body(q_ref, kv_ref, sk_ref, x_ref, wo_ref, g_ref, a_ref, y_ref, x4_ref):
        i = pl.program_id(0)
        _, kvw = _kv_window(kv_ref, i)
        q = q_ref[...]
        outs = []
        for g in range(N_KV_HEADS):
            p, _, _, _, v = _attn_group(i, q, kvw, sk_ref, g)
            outs += _unstack_heads(_dot(p.astype(BF), v))
        attn = jnp.concatenate(outs, axis=1)
        a_ref[...] = attn
        y = _dot(attn.astype(BF), wo_ref[...])
        y_ref[...] = y
        x4_ref[...] = x_ref[...] + _rms_fwd(y, g_ref[1:2, :])

    return _call(body, name="attn_fwd", grid=(s_len // BLK,),
                 in_specs=[_row_spec(BLK), VSPEC, SSPEC, _row_spec(BLK), VSPEC, VSPEC],
                 out_specs=[_row_spec(BLK)] * 3, out_shape=[_sds((s_len, D))] * 3,
                 args=[q, kv, sinks, x3, w_o, post_g], rider=rider)


def _attn_bwd(dx4, y, attn, q, kv, sinks, w_o, post_g, rider=None):
    s_len = q.shape[0]
    n = s_len // BLK

    def body(dx_ref, y_ref, a_ref, q_ref, kv_ref, sk_ref, wo_ref, g_ref,
             dq_ref, dkv_ref, dwo_ref, dg_ref, dsk_ref, wacc):
        i = pl.program_id(0)

        @pl.when(i == 0)
        def _():
            dkv_ref[...] = jnp.zeros_like(dkv_ref)
            wacc[...] = jnp.zeros_like(wacc)
            dg_ref[...] = jnp.zeros_like(dg_ref)
            dsk_ref[...] = jnp.zeros_like(dsk_ref)

        dy, prod = _rms_bwd(y_ref[...], g_ref[1:2, :], dx_ref[...])
        dg_ref[...] += _rowsum(prod)
        dyb = dy.astype(BF)
        attn = a_ref[...]
        wacc[...] += _dot_tn(attn.astype(BF), dyb)
        d_o = _dot_nt(dyb, wo_ref[...])
        dod = d_o * attn
        ks, kvw = _kv_window(kv_ref, i)
        q = q_ref[...]
        lane = lax.broadcasted_iota(jnp.int32, (1, D), 1)
        dqs, dks, dvs = [], [], []
        dsk = jnp.zeros((1, D), F32)
        for g in range(N_KV_HEADS):
            p, ps, qs, k, v = _attn_group(i, q, kvw, sk_ref, g)
            cols = [slice((GQA * g + j) * HEAD_DIM, (GQA * g + j + 1) * HEAD_DIM) for j in range(GQA)]
            do_s = jnp.concatenate([d_o[:, c] for c in cols], axis=0).astype(BF)
            dsum = jnp.concatenate([jnp.sum(dod[:, c], axis=-1, keepdims=True) for c in cols], axis=0)
            dp = _dot_nt(do_s, v)
            dsb = (p * (dp - dsum) * ATT_SCALE).astype(BF)
            sink_part = ps * dsum
            for j in range(GQA):
                dsk = dsk + jnp.where(lane == GQA * g + j, -_rowsum(sink_part[j * BLK:(j + 1) * BLK, :]), 0.0)
            dqs += _unstack_heads(_dot(dsb, k))
            dks.append(_dot_tn(dsb, qs))
            dvs.append(_dot_tn(p.astype(BF), do_s))
        dsk_ref[...] += dsk
        dq_ref[...] = jnp.concatenate(dqs, axis=1).astype(BF)
        dkv_ref[pl.ds(ks, 2 * BLK), :] += jnp.concatenate(dks + dvs, axis=1)

        @pl.when(i == n - 1)
        def _():
            dwo_ref[...] = wacc[...].astype(BF)

    return _call(
        body, name="attn_bwd", grid=(n,),
        in_specs=[_row_spec(BLK), _row_spec(BLK), _row_spec(BLK), _row_spec(BLK), VSPEC, SSPEC, VSPEC, VSPEC],
        out_specs=[_row_spec(BLK), _const_spec((s_len, 2 * KVD)), _const_spec((D, D)),
                   _const_spec((1, D)), _const_spec((1, D))],
        out_shape=[_sds((s_len, D), BF), _sds((s_len, 2 * KVD)), _sds((D, D), BF), _sds((1, D)), _sds((1, D))],
        scratch_shapes=[pltpu.VMEM((D, D), F32)],
        args=[dx4, y, attn, q, kv, sinks, w_o, post_g], rider=rider)


Big = collections.namedtuple("Big", "name src layer L A R C rb")


def _bigs():
    out = {"pool_w": Big("pool_w", "pool_w", None, 4, 4, POOL_G // N_CHIPS, POOL_G, 32)}
    for l in range(2):
        out[f"w_gu{l}"] = Big(f"w_gu{l}", "w_gu", l, 1, 2, D, FF_HALF, 256)
        out[f"w_down{l}"] = Big(f"w_down{l}", "w_down", l, 1, 4, FF // N_CHIPS, D, 352)
        out[f"w_ple_gate{l}"] = Big(f"w_ple_gate{l}", "w_ple_gate", l, 1, 4, D // N_CHIPS, D, 128)
        out[f"w_ple_proj{l}"] = Big(f"w_ple_proj{l}", "w_ple_proj", l, 1, 1, PLE, D // N_CHIPS, 128)
    out["w_q"] = Big("w_q", "w_q", None, 1, 4, D // N_CHIPS, D, 128)
    out["w_o"] = Big("w_o", "w_o", None, 1, 4, D // N_CHIPS, D, 128)
    out["w_kv"] = Big("w_kv", "w_kv", None, 1, 4, D // N_CHIPS, 2 * KVD, 128)
    return out


BIGS = _bigs()
POOL_SCALE = Big("pool_scale", "pool_scale", None, 1, 1, 1, D // N_CHIPS, 1)
BIG_SOURCES = ("w_gu", "w_down", "w_ple_gate", "w_ple_proj", "w_q", "w_o", "w_kv", "pool_w")


def _ncb(t):
    return N_CHIPS // t.A


def _full_shape(t, rows=None):
    return (t.L, t.A, t.R if rows is None else rows, _ncb(t) * t.C)


def _slot_index(t, k):
    return k // _ncb(t), k % _ncb(t)


def _slot(ref, t, k, row0, rows):
    a, cb = _slot_index(t, k)
    return ref.at[:, a, pl.ds(row0, rows), pl.ds(pl.multiple_of(cb * t.C, 128), t.C)]


def _place(t, w, kc, out_dtype):
    rb = min(t.R, 2 * t.rb)

    def body(kc_ref, w_ref, o_ref):
        del kc_ref
        o_ref[...] = w_ref[...].astype(out_dtype)

    def in_map(l, j, kc_ref):
        return (l if t.layer is None else t.layer, j, 0)

    def out_map(l, j, kc_ref):
        a, cb = _slot_index(t, kc_ref[0])
        return (l, a, j, cb)

    return pl.pallas_call(
        body, name=f"place_{t.name}",
        grid_spec=pltpu.PrefetchScalarGridSpec(
            num_scalar_prefetch=1, grid=(t.L, t.R // rb),
            in_specs=[pl.BlockSpec((None, rb, t.C), in_map)],
            out_specs=pl.BlockSpec((None, None, rb, t.C), out_map)),
        out_shape=_sds(_full_shape(t), out_dtype),
        compiler_params=_params(2),
    )(kc, w)


def _place_many(ts, ws, kc, rider):
    n = 8

    def blocks_of(t):
        return next(nb for nb in (8, 4, 2, 1) if t.R % (16 * nb) == 0)

    def body(kc_ref, *refs):
        del kc_ref
        s = pl.program_id(0)
        for ti, t in enumerate(ts):
            @pl.when(s < blocks_of(t))
            def _():
                refs[len(ts) + ti][...] = refs[ti][...].astype(BF)

    in_specs, out_specs = [], []
    for t in ts:
        assert t.L == 1
        nb = blocks_of(t)
        rb = t.R // nb

        def in_map(s, kc_ref, t=t, nb=nb):
            return (0 if t.layer is None else t.layer, jnp.minimum(s, nb - 1), 0)

        def out_map(s, kc_ref, t=t, nb=nb):
            a, cb = _slot_index(t, kc_ref[0])
            return (0, a, jnp.minimum(s, nb - 1), cb)

        in_specs.append(pl.BlockSpec((None, rb, t.C), in_map))
        out_specs.append(pl.BlockSpec((None, None, rb, t.C), out_map))
    return _call(body, name="place_rest", grid=(n,), in_specs=in_specs, out_specs=out_specs,
                 out_shape=[_sds(_full_shape(t), BF) for t in ts], args=list(ws), rider=rider, prefetch=kc)


def _mesh_position():
    x, y, c = lax.axis_index("x"), lax.axis_index("y"), lax.axis_index("c")
    chips = [(1 - x, y), (x, 1 - y), (1 - x, 1 - y)]
    return x, y, c, chips


def _gather_rider(parts, fulls):
    nt = len(parts)
    TO_X, TO_Y, FWD_X, FWD_Y, SIB_X, SIB_Y, SIB_D = range(7)

    def rows_of(ti, core):
        t, r0, r1 = parts[ti]
        h = (r1 - r0) // 2
        return r0 + core * h, h

    def copy(outs, sems, kind, ti, k_src, row0, rows, dev):
        region = _slot(outs[ti], parts[ti][0], k_src, row0, rows)
        return pltpu.make_async_remote_copy(region, region, sems[0].at[ti, kind], sems[1].at[ti, kind],
                                            device_id=dev, device_id_type=MESH)

    def plan(outs, sems):
        x, y, c, _ = _mesh_position()
        me, kx, ky, kd = 2 * x + y, 2 * (1 - x) + y, 2 * x + (1 - y), 2 * (1 - x) + (1 - y)
        dev_x, dev_y, dev_d, sib = (1 - x, y, c), (x, 1 - y, c), (1 - x, 1 - y, c), (x, y, 1 - c)

        def whole(ti):
            return 0, parts[ti][0].R

        def mk(kind, k_send, k_recv, dev, send_rows, recv_rows):
            def build(ti, side):
                k_src = k_send if side == "s" else k_recv
                row0, rows = (send_rows if side == "s" else recv_rows)(ti)
                return copy(outs, sems, kind, ti, k_src, row0, rows, dev)
            return build

        def first_half(core):
            return lambda ti: (rows_of(ti, core)[0], rows_of(ti, core)[1] // 2)

        def second_half(core):
            return lambda ti: (rows_of(ti, core)[0] + rows_of(ti, core)[1] // 2, rows_of(ti, core)[1] // 2)

        mine = lambda ti: rows_of(ti, c)
        theirs = lambda ti: rows_of(ti, 1 - c)
        split = {
            TO_X: mk(TO_X, me, kx, dev_x, mine, mine),
            TO_Y: mk(TO_Y, me, ky, dev_y, mine, mine),
            FWD_X: mk(FWD_X, ky, kd, dev_x, first_half(c), first_half(c)),
            FWD_Y: mk(FWD_Y, kx, kd, dev_y, second_half(c), second_half(c)),
            SIB_X: mk(SIB_X, kx, kx, sib, mine, theirs),
            SIB_Y: mk(SIB_Y, ky, ky, sib, mine, theirs),
            SIB_D: mk(SIB_D, kd, kd, sib, mine, theirs),
        }
        direct = {
            TO_X: mk(TO_X, me, kx, dev_x, whole, whole),
            TO_Y: mk(TO_Y, me, ky, dev_y, whole, whole),
            FWD_X: mk(FWD_X, me, kd, dev_d, whole, whole),
        }
        return split, direct

    is_split = [t.R > 1 for t, _, _ in parts]

    def start(ins, outs, sems):
        split, direct = plan(outs, sems)
        for ti in range(nt):
            kinds = split if is_split[ti] else direct
            kinds[TO_X](ti, "s").start()
            kinds[TO_Y](ti, "s").start()
            if not is_split[ti]:
                kinds[FWD_X](ti, "s").start()

    def mid(ins, outs, sems):
        split, _ = plan(outs, sems)
        for ti in range(nt):
            if is_split[ti]:
                split[TO_Y](ti, "r").wait_recv()
                split[FWD_X](ti, "s").start()
                split[SIB_Y](ti, "s").start()
        for ti in range(nt):
            if is_split[ti]:
                split[TO_X](ti, "r").wait_recv()
                split[FWD_Y](ti, "s").start()
                split[SIB_X](ti, "s").start()

    def finish(ins, outs, sems):
        split, direct = plan(outs, sems)
        for ti in range(nt):
            if is_split[ti]:
                split[FWD_X](ti, "r").wait_recv()
                split[FWD_Y](ti, "r").wait_recv()
                split[SIB_D](ti, "s").start()
            else:
                for kind in (TO_X, TO_Y, FWD_X):
                    direct[kind](ti, "r").wait_recv()
        for ti in range(nt):
            if is_split[ti]:
                for kind in (SIB_X, SIB_Y, SIB_D):
                    split[kind](ti, "r").wait_recv()
        for ti in range(nt):
            kinds = split if is_split[ti] else direct
            for kind in kinds:
                kinds[kind](ti, "s").wait_send()

    sems = pltpu.SemaphoreType.DMA((nt, 7))
    return Rider(list(fulls), [_sds(a.shape, a.dtype) for a in fulls], {i: i for i in range(nt)},
                 [sems, sems], start, mid, finish)


def _pair_exchange(name, specs, grads):
    nt = len(specs)

    def body(*refs):
        gs = refs[:nt]
        lands = refs[nt:2 * nt]
        send, recv = refs[2 * nt:]
        x, y, c, _ = _mesh_position()
        cps = []
        for ti, t in enumerate(specs):
            half = t.R // 2
            cp = pltpu.make_async_remote_copy(gs[ti].at[:, :, pl.ds((1 - c) * half, half), :], lands[ti],
                                              send.at[ti], recv.at[ti],
                                              device_id=(x, y, 1 - c), device_id_type=MESH)
            cp.start()
            cps.append(cp)
        for cp in cps:
            cp.wait()

    return pl.pallas_call(
        body, name=name,
        in_specs=[ANYSPEC] * nt, out_specs=[ANYSPEC] * nt,
        out_shape=[_sds(_full_shape(t, t.R // 2), BF) for t in specs],
        scratch_shapes=[pltpu.SemaphoreType.DMA((nt,)), pltpu.SemaphoreType.DMA((nt,))],
        compiler_params=_params(),
    )(*grads)


def _pair_sum_job(t, g, land):
    assert t.L == 1
    half = t.R // 2
    nj = half // t.rb
    block = (None, t.A, t.rb, _ncb(t) * t.C)

    def fn(j, kc_ref, ins, outs):
        outs[0][...] = (ins[0][...].astype(F32) + ins[1][...].astype(F32)).astype(BF)

    return Job(nj,
               [(g, block, lambda j, kc_ref: (0, 0, kc_ref[1] * nj + j, 0)),
                (land, block, lambda j, kc_ref: (0, 0, j, 0))],
               [(_sds(_full_shape(t, half), BF), block, lambda j, kc_ref: (0, 0, j, 0))], fn)


def _scatter_rider(specs, sums):
    nt = len(specs)

    def copy(ins, outs, sems, ti, j, chip, c):
        t = specs[ti]
        cx, cy = chip
        return pltpu.make_async_remote_copy(_slot(ins[ti], t, 2 * cx + cy, 0, t.R // 2), outs[ti].at[j],
                                            sems[0].at[ti, j], sems[1].at[ti, j],
                                            device_id=(cx, cy, c), device_id_type=MESH)

    def start(ins, outs, sems):
        _, _, c, chips = _mesh_position()
        for j, chip in enumerate(chips):
            for ti in range(nt):
                copy(ins, outs, sems, ti, j, chip, c).start()

    def finish(ins, outs, sems):
        _, _, c, chips = _mesh_position()
        for j, chip in enumerate(chips):
            for ti in range(nt):
                copy(ins, outs, sems, ti, j, chip, c).wait()

    sems = pltpu.SemaphoreType.DMA((nt, N_CHIPS - 1))
    return Rider(list(sums), [_sds((N_CHIPS - 1, t.L, t.R // 2, t.C), BF) for t in specs], {}, [sems, sems],
                 start, None, finish)


def _chip_sum_job(ts, landed):
    t0 = ts[0]
    assert t0.L == 1
    half = t0.R // 2
    nj = half // t0.rb

    def local(j, li):
        return jnp.clip(j - li * nj, 0, nj - 1)

    ins = []
    for li, t in enumerate(ts):
        s, land = landed[t.name]

        def own_map(j, kc_ref, li=li, t=t):
            a, cb = _slot_index(t, kc_ref[0])
            return (0, a, local(j, li), cb)

        ins.append((s, (None, None, t.rb, t.C), own_map))
        ins.append((land, (N_CHIPS - 1, None, t.rb, t.C), lambda j, kc_ref, li=li: (0, 0, local(j, li), 0)))

    def fn(j, kc_ref, in_refs, outs):
        for li in range(len(ts)):
            @pl.when(j // nj == li)
            def _():
                acc = in_refs[2 * li][...].astype(F32)
                for k in range(N_CHIPS - 1):
                    acc = acc + in_refs[2 * li + 1][k].astype(F32)
                outs[0][...] = acc

    return Job(len(ts) * nj, ins,
               [(_sds((len(ts), t0.R, t0.C)), (None, t0.rb, t0.C),
                 lambda j, kc_ref: (j // nj, kc_ref[1] * nj + j % nj, 0))], fn)


def _adamw_job(rb, w, g, m, v):
    n_layers, r, c = w.shape
    nb = r // rb
    block = (None, rb, c)
    index = lambda j, kc_ref: (j // nb, j % nb, 0)

    def fn(j, kc_ref, ins, outs):
        g_v = ins[1][...]
        outs[0][...] = g_v
        outs[1][...], outs[2][...], outs[3][...] = _adamw_math(ins[0][...], g_v, ins[2][...], ins[3][...])

    return Job(n_layers * nb, [(a, block, index) for a in (w, g, m, v)],
               [(_sds(w.shape), block, index)] * 4, fn)


def _chip_sum_fused(t, own, land, kc, n_layers, prev, by_cols):
    if by_cols:
        rows, cols = own.shape
    else:
        nb, rows, bw = own.shape
        cols = nb * bw
    nj = rows // t.rb

    def body(*refs):
        o_ref, l_ref, out_ref = refs[1], refs[2], refs[-1]
        acc = o_ref[...].astype(F32)
        for j in range(N_CHIPS - 1):
            acc = acc + l_ref[j].astype(F32)
        out_ref[...] = acc if by_cols else jnp.concatenate([acc[b] for b in range(nb)], axis=1)

    def out_map(j, kc_ref):
        return (t.layer, j, kc_ref[1]) if by_cols else (t.layer, kc_ref[1] * nj + j, 0)

    if by_cols:
        in_specs = [pl.BlockSpec((t.rb, cols), lambda j, kc_ref: (j, 0)),
                    pl.BlockSpec((N_CHIPS - 1, t.rb, cols), lambda j, kc_ref: (0, j, 0))]
    else:
        in_specs = [pl.BlockSpec((nb, t.rb, bw), lambda j, kc_ref: (0, j, 0)),
                    pl.BlockSpec((N_CHIPS - 1, nb, t.rb, bw), lambda j, kc_ref: (0, 0, j, 0))]
    args = [kc, own, land]
    aliases = {}
    if prev is not None:
        in_specs.append(ANYSPEC)
        args.append(prev)
        aliases = {3: 0}
    return pl.pallas_call(
        body, name=f"chip_sum_{t.name}",
        grid_spec=pltpu.PrefetchScalarGridSpec(
            num_scalar_prefetch=1, grid=(nj,), in_specs=in_specs,
            out_specs=pl.BlockSpec((None, t.rb, cols), out_map)),
        out_shape=_sds((n_layers, t.R, t.C)),
        input_output_aliases=aliases,
        compiler_params=_params(1),
    )(*args)


def _pair_share(halves, by_cols):
    nt = len(halves)

    def part(ref, ti, core):
        axis = 2 if by_cols[ti] else 1
        half = halves[ti].shape[axis] // 2
        piece = pl.ds(pl.multiple_of(core * half, 128 if by_cols[ti] else 8), half)
        return ref.at[:, :, piece] if by_cols[ti] else ref.at[:, piece, :]

    def body(*refs):
        outs = refs[nt:2 * nt]
        send, recv = refs[2 * nt:]
        x, y, c, _ = _mesh_position()
        cps = []
        for ti in range(nt):
            mine = part(outs[ti], ti, c)
            cp = pltpu.make_async_remote_copy(mine, mine, send.at[ti], recv.at[ti],
                                              device_id=(x, y, 1 - c), device_id_type=MESH)
            cp.start()
            cps.append(cp)
        for ti in range(nt):
            theirs = part(outs[ti], ti, 1 - c)
            pltpu.make_async_remote_copy(theirs, theirs, send.at[ti], recv.at[ti],
                                         device_id=(x, y, 1 - c), device_id_type=MESH).wait_recv()
        for cp in cps:
            cp.wait_send()

    return pl.pallas_call(
        body, name="grads_pair_share",
        in_specs=[ANYSPEC] * nt, out_specs=[ANYSPEC] * nt,
        out_shape=[_sds(a.shape, a.dtype) for a in halves],
        scratch_shapes=[pltpu.SemaphoreType.DMA((nt,)), pltpu.SemaphoreType.DMA((nt,))],
        input_output_aliases={i: i for i in range(nt)},
        compiler_params=_params(),
    )(*halves)


def _adamw_math(w, g, m, v):
    m = B1 * m + (1.0 - B1) * g
    v = B2 * v + (1.0 - B2) * (g * g)
    delta = -LR * ((m / BC1) / (jnp.sqrt(v / BC2) + AEPS) + WD * w)
    return delta, m, v


def _adamw(name, rb, w, g, m, v):
    n_layers, r, c = w.shape

    def body(w_ref, g_ref, m_ref, v_ref, go_ref, d_ref, nm_ref, nv_ref):
        g_v = g_ref[...]
        go_ref[...] = g_v
        d_ref[...], nm_ref[...], nv_ref[...] = _adamw_math(w_ref[...], g_v, m_ref[...], v_ref[...])

    spec = pl.BlockSpec((None, rb, c), lambda l, j: (l, j, 0))
    return pl.pallas_call(
        body, name=f"adamw_{name}", grid=(n_layers, r // rb),
        in_specs=[spec] * 4, out_specs=[spec] * 4, out_shape=[_sds(w.shape)] * 4,
        compiler_params=_params(2),
    )(w, g, m, v)


GAIN_ROWS = {"pre_mix_g": 0, "post_mix_g": 2, "pre_ffn_g": 4, "post_ffn_g": 6, "ple_g": 8, "ple_post_g": 10}
ROW_KV_G, ROW_POOL_SCALE, ROW_SINKS, ROW_LOSS, PACK_ROWS = 12, 13, 14, 15, 16
SMALL_NAMES = tuple(GAIN_ROWS) + ("kv_g", "pool_scale", "sinks")


def _small_all_reduce(rows, dpool, rider=None):
    ng, pr = len(WINDOWS), POOL_G // N_CHIPS

    def body(*refs):
        row_refs = refs[:PACK_ROWS]
        dpool_ref, tot_ref, gpool_ref, pack, land, pland, send, recv, psend, precv = refs[PACK_ROWS:]
        x, y, c, _ = _mesh_position()
        me = 4 * x + 2 * y + c
        for r in range(PACK_ROWS):
            pack[r:r + 1, :] = row_refs[r][...]

        def shard_of(k):
            return dpool_ref.at[:, pl.ds(pl.multiple_of(k * pr, pr), pr), :]

        cps = []
        for j in range(1, N_DEV):
            px, py, pc = x ^ (j >> 2), y ^ ((j >> 1) & 1), c ^ (j & 1)
            cps.append(pltpu.make_async_remote_copy(pack, land.at[me], send.at[j], recv.at[j],
                                                    device_id=(px, py, pc), device_id_type=MESH))
            cps.append(pltpu.make_async_remote_copy(shard_of(2 * px + py), pland.at[me], psend.at[j], precv.at[j],
                                                    device_id=(px, py, pc), device_id_type=MESH))
        for cp in cps:
            cp.start()
        land[me] = pack[...]
        pland[me] = dpool_ref[:, pl.ds(pl.multiple_of((2 * x + y) * pr, pr), pr), :]
        for j in range(1, N_DEV):
            pltpu.make_async_remote_copy(pack, land.at[me ^ j], send.at[j], recv.at[j],
                                         device_id=(x, y, c), device_id_type=MESH).wait_recv()
            pltpu.make_async_remote_copy(shard_of(0), pland.at[me ^ j], psend.at[j], precv.at[j],
                                         device_id=(x, y, c), device_id_type=MESH).wait_recv()
        for cp in cps:
            cp.wait_send()
        tot = land[0]
        gp = pland[0].astype(F32)
        for d in range(1, N_DEV):
            tot = tot + land[d]
            gp = gp + pland[d].astype(F32)
        tot_ref[...] = tot
        gpool_ref[...] = gp

    sems = pltpu.SemaphoreType.DMA((N_DEV,))
    return _call(
        body, name="small_all_reduce", grid=(1,),
        in_specs=[VSPEC] * (PACK_ROWS + 1), out_specs=[VSPEC, VSPEC],
        out_shape=[_sds((PACK_ROWS, D)), _sds((ng, pr, POOL_G))],
        scratch_shapes=[pltpu.VMEM((PACK_ROWS, D), F32), pltpu.VMEM((N_DEV, PACK_ROWS, D), F32),
                        pltpu.VMEM((N_DEV, ng, pr, POOL_G), BF), sems, sems, sems, sems],
        args=[*rows, dpool], rider=rider)


def _small_adamw(tot, kc, small_w, small_m, small_v):
    names = SMALL_NAMES
    n = len(names)

    def body(*refs):
        tot_ref, kc_ref = refs[0], refs[1]
        w_refs = dict(zip(names, refs[2:2 + n]))
        m_refs = dict(zip(names, refs[2 + n:2 + 2 * n]))
        v_refs = dict(zip(names, refs[2 + 2 * n:2 + 3 * n]))
        loss_ref = refs[2 + 3 * n]
        out_refs = {nm: refs[3 + 3 * n + 4 * k: 7 + 3 * n + 4 * k] for k, nm in enumerate(names)}
        tot = tot_ref[...]
        loss_ref[...] = 0.5 * jnp.sum(tot[ROW_LOSS:ROW_LOSS + 1, :], axis=-1, keepdims=True) * (1.0 / D)

        def update(nm, g):
            g_ref, d_ref, nm_ref, nv_ref = out_refs[nm]
            g_ref[...] = g
            d_ref[...], nm_ref[...], nv_ref[...] = _adamw_math(w_refs[nm][...], g, m_refs[nm][...], v_refs[nm][...])

        for nm, r in GAIN_ROWS.items():
            update(nm, tot[r:r + 2, :])
        update("kv_g", tot[ROW_KV_G:ROW_KV_G + 1, :])
        k = kc_ref[0]
        width = D // N_CHIPS
        g_scale = jnp.zeros((1, width), F32)
        for kk in range(N_CHIPS):
            g_scale = g_scale + jnp.where(k == kk, tot[ROW_POOL_SCALE:ROW_POOL_SCALE + 1, kk * width:(kk + 1) * width], 0.0)
        update("pool_scale", g_scale)
        update("sinks", tot[ROW_SINKS:ROW_SINKS + 1, 0:N_HEADS])

    ins = [tot, kc] + [small_w[nm] for nm in names] + [small_m[nm] for nm in names] + [small_v[nm] for nm in names]
    out_shape = [_sds((1, 1))]
    for nm in names:
        out_shape += [_sds(small_w[nm].shape)] * 4
    outs = pl.pallas_call(
        body, name="small_adamw",
        in_specs=[VSPEC, SSPEC] + [VSPEC] * (3 * n), out_specs=[VSPEC] * len(out_shape), out_shape=out_shape,
        compiler_params=_params(),
    )(*ins)
    return outs[0], {nm: outs[1 + 4 * k: 5 + 4 * k] for k, nm in enumerate(names)}


def _compute_layout(t, full):
    if t.src == "w_gu":
        return full.reshape(2, D, FF)
    if t.src == "pool_w":
        return full.reshape(len(WINDOWS), POOL_G, POOL_G)
    if t.src == "pool_scale":
        return full.reshape(1, D)
    return full.reshape(t.A * t.R, _ncb(t) * t.C)


def kernel(x, p, pre_mix_g, post_mix_g, pre_ffn_g, post_ffn_g, pool_w, pool_scale, kv_g, w_kv, w_q, sinks, w_o, w_gu, w_down, ple_g, w_ple_gate, w_ple_proj, ple_post_g, loss_target, m_pre_mix_g, m_post_mix_g, m_pre_ffn_g, m_post_ffn_g, m_pool_w, m_pool_scale, m_kv_g, m_w_kv, m_w_q, m_sinks, m_w_o, m_w_gu, m_w_down, m_ple_g, m_w_ple_gate, m_w_ple_proj, m_ple_post_g, v_pre_mix_g, v_post_mix_g, v_pre_ffn_g, v_post_ffn_g, v_pool_w, v_pool_scale, v_kv_g, v_w_kv, v_w_q, v_sinks, v_w_o, v_w_gu, v_w_down, v_ple_g, v_w_ple_gate, v_w_ple_proj, v_ple_post_g):
    weights = dict(pre_mix_g=pre_mix_g, post_mix_g=post_mix_g, pre_ffn_g=pre_ffn_g, post_ffn_g=post_ffn_g,
                   pool_w=pool_w, pool_scale=pool_scale, kv_g=kv_g, w_kv=w_kv, w_q=w_q, sinks=sinks, w_o=w_o,
                   w_gu=w_gu, w_down=w_down, ple_g=ple_g, w_ple_gate=w_ple_gate, w_ple_proj=w_ple_proj,
                   ple_post_g=ple_post_g)
    m_in = dict(pre_mix_g=m_pre_mix_g, post_mix_g=m_post_mix_g, pre_ffn_g=m_pre_ffn_g, post_ffn_g=m_post_ffn_g,
                pool_w=m_pool_w, pool_scale=m_pool_scale, kv_g=m_kv_g, w_kv=m_w_kv, w_q=m_w_q, sinks=m_sinks,
                w_o=m_w_o, w_gu=m_w_gu, w_down=m_w_down, ple_g=m_ple_g, w_ple_gate=m_w_ple_gate,
                w_ple_proj=m_w_ple_proj, ple_post_g=m_ple_post_g)
    v_in = dict(pre_mix_g=v_pre_mix_g, post_mix_g=v_post_mix_g, pre_ffn_g=v_pre_ffn_g, post_ffn_g=v_post_ffn_g,
                pool_w=v_pool_w, pool_scale=v_pool_scale, kv_g=v_kv_g, w_kv=v_w_kv, w_q=v_w_q, sinks=v_sinks,
                w_o=v_w_o, w_gu=v_w_gu, w_down=v_w_down, ple_g=v_ple_g, w_ple_gate=v_w_ple_gate,
                w_ple_proj=v_w_ple_proj, ple_post_g=v_ple_post_g)
    order = ["pre_mix_g", "post_mix_g", "pre_ffn_g", "post_ffn_g", "pool_w", "pool_scale", "kv_g", "w_kv", "w_q",
             "sinks", "w_o", "w_gu", "w_down", "ple_g", "w_ple_gate", "w_ple_proj", "ple_post_g"]

    kc = jnp.stack([2 * lax.axis_index("x") + lax.axis_index("y"), lax.axis_index("c")]).astype(jnp.int32)
    s_len = x.shape[1]
    x2d = x.reshape(s_len, D)
    p3d = p.reshape(2, s_len, PLE)
    target = loss_target.reshape(s_len, D)
    kv_g2d = kv_g.reshape(1, D)
    gains = {nm: weights[nm] for nm in GAIN_ROWS}

    def shard_view(src, a):
        t = next(t for t in BIGS.values() if t.src == src)
        return a.reshape(-1, t.R, t.C)

    first = ["pool_w", "pool_scale", "w_gu0", "w_down0"]
    rest = [nm for nm in BIGS if nm not in first]
    specs = dict(BIGS, pool_scale=POOL_SCALE)
    placed = {nm: _place(BIGS[nm], shard_view(BIGS[nm].src, weights[BIGS[nm].src]), kc, BF)
              for nm in first if nm in BIGS}
    placed["pool_scale"] = _place(POOL_SCALE, pool_scale.reshape(1, 1, D // N_CHIPS), kc, F32)

    def gather(names, rows=None):
        rows = rows or {}
        parts = [(specs[nm],) + tuple(rows.get(nm, (0, specs[nm].R))) for nm in names]
        return _gather_rider(parts, [placed[nm] for nm in names])

    def take(names, results):
        for nm, a in zip(names, results):
            placed[nm] = a

    def weight(nm):
        return _compute_layout(specs[nm], placed[nm])

    cast, got = _place_many([BIGS[nm] for nm in rest], [shard_view(BIGS[nm].src, weights[BIGS[nm].src]) for nm in rest],
                            kc, rider=gather(first))
    take(rest, cast)
    take(first, got)


    y0, x1 = _mixa_fwd(x2d, gains["pre_mix_g"], weight("pool_w"), weight("pool_scale"), gains["post_mix_g"])

    ride = ["w_ple_gate0", "w_ple_proj0", "w_q", "w_kv", "w_o", "w_gu1"]
    (f0, x2), got = _ffn_fwd(0, x1, gains["pre_ffn_g"], weight("w_gu0"), weight("w_down0"), gains["post_ffn_g"],
                             rider=gather(ride, {"w_gu1": (0, 320)}))
    take(ride, got)

    ride = ["w_ple_gate1", "w_ple_proj1", "w_gu1"]
    (z0, pe0, x3, q, kv), got = _ple_fwd(
        0, x2, p3d, gains["ple_g"], weight("w_ple_gate0"), weight("w_ple_proj0"), gains["ple_post_g"],
        qkv=(gains["pre_mix_g"], kv_g2d, weight("w_q"), weight("w_kv")),
        rider=gather(ride, {"w_gu1": (320, 704)}))
    take(ride, got)

    ride = ["w_down1", "w_gu1"]
    (attn, y1, x4), got = _attn_fwd(q, kv, sinks, x3, weight("w_o"), gains["post_mix_g"],
                                    rider=gather(ride, {"w_gu1": (704, D)}))
    take(ride, got)

    (f1, x5), _ = _ffn_fwd(1, x4, gains["pre_ffn_g"], weight("w_gu1"), weight("w_down1"), gains["post_ffn_g"])
    (z1, pe1, dx6, loss_row), _ = _ple_fwd(1, x5, p3d, gains["ple_g"], weight("w_ple_gate1"), weight("w_ple_proj1"),
                                           gains["ple_post_g"], target=target)

    local = {}
    landed = {}
    fused = {}

    def pair_stage(tag, names):
        ts = [BIGS[nm] for nm in names]
        gs = [local[nm].reshape(_full_shape(t)) for nm, t in zip(names, ts)]
        lands = _pair_exchange(f"grads_pair_exchange_{tag}", ts, gs)
        jobs = [_pair_sum_job(t, g, l) for t, g, l in zip(ts, gs, lands)]
        return [r[0] for r in _multi_call(f"pair_sum_{tag}", jobs, kc)]

    def scatter(names, sums):
        return _scatter_rider([BIGS[nm] for nm in names], sums)

    def keep(names, sums, got):
        for nm, s, l in zip(names, sums, got):
            landed[nm] = (s, l)

    (dx5, local["w_ple_gate1"], local["w_ple_proj1"], d_ple1, d_plepost1), _ = _ple_bwd(
        1, dx6, x5, z1, pe1, p3d, gains["ple_g"], weight("w_ple_gate1"), gains["ple_post_g"])

    group_a = ["w_ple_gate1", "w_ple_proj1"]
    sums_a = pair_stage("a", group_a)
    (dx4, d_preffn1, d_postffn1, *scattered), _ = _ffn_bwd(
        1, dx5, x4, f1, gains["pre_ffn_g"], weight("w_gu1"), weight("w_down1"), gains["post_ffn_g"], kc)
    fused["w_gu1"], fused["w_down1"] = scattered[0:2], scattered[2:4]

    (dq, dkv, local["w_o"], d_postmix1, d_sinks), got = _attn_bwd(
        dx4, y1, attn, q, kv, sinks, weight("w_o"), gains["post_mix_g"], rider=scatter(group_a, sums_a))
    keep(group_a, sums_a, got)
    dx3, local["w_q"], local["w_kv"], d_premix1, d_kvg = _qkv_bwd(
        dq, dkv, x3, dx4, gains["pre_mix_g"], kv_g2d, weight("w_q"), weight("w_kv"))

    group_b = ["w_o", "w_q", "w_kv"]
    sums_b = pair_stage("b", group_b)
    (dx2, local["w_ple_gate0"], local["w_ple_proj0"], d_ple0, d_plepost0), got = _ple_bwd(
        0, dx3, x2, z0, pe0, p3d, gains["ple_g"], weight("w_ple_gate0"), gains["ple_post_g"],
        rider=scatter(group_b, sums_b))
    keep(group_b, sums_b, got)

    group_c = ["w_ple_gate0", "w_ple_proj0"]
    sums_c = pair_stage("c", group_c)
    (dx1, d_preffn0, d_postffn0, *scattered), _ = _ffn_bwd(
        0, dx2, x1, f0, gains["pre_ffn_g"], weight("w_gu0"), weight("w_down0"), gains["post_ffn_g"], kc)
    fused["w_gu0"], fused["w_down0"] = scattered[0:2], scattered[2:4]

    (dx0, d_pool, d_scale, d_postmix0, d_premix0), _ = _mixa_bwd(
        dx1, x2d, y0, gains["pre_mix_g"], weight("pool_w"), weight("pool_scale"), gains["post_mix_g"])

    rows = [d_premix0, d_premix1, d_postmix0, d_postmix1, d_preffn0, d_preffn1, d_postffn0, d_postffn1,
            d_ple0, d_ple1, d_plepost0, d_plepost1, d_kvg, d_scale, d_sinks, loss_row]
    as2d = lambda a: a.reshape(1, D) if a.ndim == 1 else a
    (tot, g_pool), got = _small_all_reduce(rows, d_pool, rider=scatter(group_c, sums_c))
    keep(group_c, sums_c, got)
    loss, small = _small_adamw(tot, kc, {nm: as2d(weights[nm]) for nm in SMALL_NAMES},
                               {nm: as2d(m_in[nm]) for nm in SMALL_NAMES},
                               {nm: as2d(v_in[nm]) for nm in SMALL_NAMES})

    layers_of = lambda src: [t for t in BIGS.values() if t.src == src]
    own_scatter = ["w_gu", "w_down"]
    others = [src for src in BIG_SOURCES if src not in own_scatter and src != "pool_w"]
    halves = {}
    for src in own_scatter:
        acc = None
        for t in layers_of(src):
            own, land = fused[t.name]
            acc = _chip_sum_fused(t, own, land, kc, len(layers_of(src)), acc, by_cols=src == "w_down")
        halves[src] = acc
    sums = _multi_call("chip_sum_rest", [_chip_sum_job(layers_of(src), landed) for src in others], kc)
    halves.update({src: r[0] for src, r in zip(others, sums)})
    shared = own_scatter + others
    full_grads = dict(zip(shared, _pair_share([halves[src] for src in shared], [src == "w_down" for src in shared])))
    full_grads["pool_w"] = g_pool

    def adam_args(src):
        return (layers_of(src)[0].rb, shard_view(src, weights[src]), full_grads[src],
                shard_view(src, m_in[src]), shard_view(src, v_in[src]))

    out = {"grad": {}, "delta": {}, "new_m": {}, "new_v": {}}
    results = {src: _adamw(src, *adam_args(src)) for src in own_scatter}
    rest_srcs = others + ["pool_w"]
    results.update(zip(rest_srcs, _multi_call("adamw_rest", [_adamw_job(*adam_args(src)) for src in rest_srcs], kc)))
    for src in BIG_SOURCES:
        shape = weights[src].shape
        for kind, a in zip(("grad", "delta", "new_m", "new_v"), results[src]):
            out[kind][src] = a.reshape(shape)
    for nm in SMALL_NAMES:
        shape = weights[nm].shape
        for kind, a in zip(("grad", "delta", "new_m", "new_v"), small[nm]):
            out[kind][nm] = a.reshape(shape)

    return (loss.reshape(()), dx0.reshape(x.shape),
            *[out["grad"][nm] for nm in order], *[out["delta"][nm] for nm in order],
            *[out["new_m"][nm] for nm in order], *[out["new_v"][nm] for nm in order])
```

```python
import collections

import jax
import jax.numpy as jnp
from jax import lax
from jax.experimental import pallas as pl
from jax.experimental.pallas import tpu as pltpu

D = 1024
FF = 2816
N_HEADS = 16
HEAD_DIM = 64
N_KV_HEADS = 4
GQA = N_HEADS // N_KV_HEADS
KVD = N_KV_HEADS * HEAD_DIM
PLE = 256
BLK = 128
WINDOWS = (2, 4, 8, 16)
POOL_G = 256
HALO = 16
EPS = 1e-6
NEG_INF = -1e30
ATT_SCALE = HEAD_DIM ** -0.5
SLOPES = tuple(2.0 ** (-8.0 * (h + 1) / N_HEADS) for h in range(N_HEADS))
N_CHIPS = 4
N_DEV = 8

LR, B1, B2, AEPS, WD, STEP = 0.001, 0.9, 0.999, 1e-08, 0.01, 10
BC1 = 1.0 - B1 ** STEP
BC2 = 1.0 - B2 ** STEP

BF = jnp.bfloat16
F32 = jnp.float32
MESH = pl.DeviceIdType.MESH
VMEM_LIMIT_V7X = 58 * 1024 * 1024
TM = 256
TM_FFN_BWD = 512
FF_CHUNK = 256
FF_HALF = FF // 2

VSPEC = pl.BlockSpec(memory_space=pltpu.VMEM)
SSPEC = pl.BlockSpec(memory_space=pltpu.SMEM)
ANYSPEC = pl.BlockSpec(memory_space=pl.ANY)


def _params(n_grid=0):
    sem = ("arbitrary",) * n_grid if n_grid else None
    return pltpu.CompilerParams(dimension_semantics=sem, vmem_limit_bytes=VMEM_LIMIT_V7X)


def _sds(shape, dtype=F32):
    return jax.ShapeDtypeStruct(tuple(shape), dtype)


Rider = collections.namedtuple("Rider", "arrays out_shapes aliases scratch start mid finish")
MID_NUM, MID_DEN = 5, 8


def _call(body, *, name, grid, in_specs, out_specs, out_shape, args, scratch_shapes=(), rider=None, prefetch=None):
    ni, no, ns = len(in_specs), len(out_specs), len(scratch_shapes)
    npre = 0 if prefetch is None else 1
    pre = [] if prefetch is None else [prefetch]
    if rider is None:
        rider = Rider([], [], {}, [], None, None, None)
    ri, ro = len(rider.arrays), len(rider.out_shapes)

    def full(*refs):
        pre_refs, refs = refs[:npre], refs[npre:]
        ins, refs = refs[:ni], refs[ni:]
        rins, refs = refs[:ri], refs[ri:]
        outs, refs = refs[:no], refs[no:]
        routs, refs = refs[:ro], refs[ro:]
        scr, rscr = refs[:ns], refs[ns:]
        ids = [pl.program_id(a) for a in range(len(grid))]
        first = ids[0] == 0
        last = ids[0] == grid[0] - 1
        for a in range(1, len(grid)):
            first = first & (ids[a] == 0)
            last = last & (ids[a] == grid[a] - 1)

        if rider.start is not None:
            @pl.when(first)
            def _():
                rider.start(rins, routs, rscr)

        if rider.mid is not None:
            assert len(grid) == 1

            @pl.when(ids[0] == (grid[0] * MID_NUM) // MID_DEN)
            def _():
                rider.mid(rins, routs, rscr)

        body(*pre_refs, *ins, *outs, *scr)

        if rider.finish is not None:
            @pl.when(last)
            def _():
                rider.finish(rins, routs, rscr)

    outs = pl.pallas_call(
        full, name=name,
        grid_spec=pltpu.PrefetchScalarGridSpec(
            num_scalar_prefetch=npre, grid=grid,
            in_specs=list(in_specs) + [ANYSPEC] * ri, out_specs=list(out_specs) + [ANYSPEC] * ro,
            scratch_shapes=list(scratch_shapes) + list(rider.scratch)),
        out_shape=list(out_shape) + list(rider.out_shapes),
        input_output_aliases={npre + ni + a: no + b for a, b in rider.aliases.items()},
        compiler_params=_params(len(grid)))(*pre, *args, *rider.arrays)
    return list(outs[:no]), list(outs[no:])


def _run(name, rider):
    ri = len(rider.arrays)

    def body(*refs):
        rins, routs, rscr = refs[:ri], refs[ri:ri + len(rider.out_shapes)], refs[ri + len(rider.out_shapes):]
        rider.start(rins, routs, rscr)
        if rider.mid is not None:
            rider.mid(rins, routs, rscr)
        rider.finish(rins, routs, rscr)

    return pl.pallas_call(
        body, name=name, in_specs=[ANYSPEC] * ri, out_specs=[ANYSPEC] * len(rider.out_shapes),
        out_shape=list(rider.out_shapes), scratch_shapes=list(rider.scratch),
        input_output_aliases=dict(rider.aliases), compiler_params=_params())(*rider.arrays)


Job = collections.namedtuple("Job", "steps ins outs fn")


def _multi_call(name, jobs, kc):
    n = max(job.steps for job in jobs)

    def clamped(index, steps):
        return lambda s, kc_ref: index(jnp.minimum(s, steps - 1), kc_ref)

    in_specs, out_specs, out_shape, args = [], [], [], []
    for job in jobs:
        for arr, block, index in job.ins:
            in_specs.append(pl.BlockSpec(block, clamped(index, job.steps)))
            args.append(arr)
        for sds, block, index in job.outs:
            out_specs.append(pl.BlockSpec(block, clamped(index, job.steps)))
            out_shape.append(sds)
    n_in = len(args)

    def body(kc_ref, *refs):
        s = pl.program_id(0)
        i0, o0 = 0, n_in
        for job in jobs:
            ins, outs = refs[i0:i0 + len(job.ins)], refs[o0:o0 + len(job.outs)]
            i0, o0 = i0 + len(job.ins), o0 + len(job.outs)

            @pl.when(s < job.steps)
            def _():
                job.fn(s, kc_ref, ins, outs)

    outs, _ = _call(body, name=name, grid=(n,), in_specs=in_specs, out_specs=out_specs, out_shape=out_shape,
                    args=args, prefetch=kc)
    res, o0 = [], 0
    for job in jobs:
        res.append(outs[o0:o0 + len(job.outs)])
        o0 += len(job.outs)
    return res


def _rms_fwd(x, g):
    r = lax.rsqrt(jnp.mean(x * x, axis=-1, keepdims=True) + EPS)
    return x * r * g


def _rms_bwd(x, g, dy):
    r = lax.rsqrt(jnp.mean(x * x, axis=-1, keepdims=True) + EPS)
    xn = x * r
    dxn = dy * g
    dx = r * (dxn - xn * jnp.mean(dxn * xn, axis=-1, keepdims=True))
    return dx, dy * xn


def _rowsum(a):
    return jnp.sum(a, axis=0, keepdims=True)


def _sigmoid(z):
    return 1.0 / (1.0 + jnp.exp(-z))


def _dot(a, b):
    return jnp.dot(a, b, preferred_element_type=F32)


def _dot_nt(a, b):
    return lax.dot_general(a, b, (((1,), (1,)), ((), ())), preferred_element_type=F32)


def _dot_tn(a, b):
    return lax.dot_general(a, b, (((0,), (0,)), ((), ())), preferred_element_type=F32)


def _row_spec(tm, width=D):
    return pl.BlockSpec((tm, width), lambda i: (i, 0))


def _const_spec(shape):
    zeros = (0,) * len(shape)
    return pl.BlockSpec(tuple(shape), lambda *_: zeros)


def _pool_delta(he, pos):
    out = []
    for gi, w in enumerate(WINDOWS):
        hg = he[:, gi * POOL_G:(gi + 1) * POOL_G]
        s = hg
        k = 1
        while k < w:
            s = s + pltpu.roll(s, k, 0)
            k *= 2
        cnt = jnp.maximum(jnp.minimum(pos + 1, w), 1).astype(F32)
        out.append(s / cnt - hg)
    return out


def _load_with_halo_before(x_ref, i, tm):
    r0 = pl.multiple_of(i * tm, tm)
    hs = pl.multiple_of(jnp.maximum(i * tm - HALO, 0), 8)
    xh = jnp.where(i > 0, x_ref[pl.ds(hs, HALO), :], 0.0)
    xt = x_ref[pl.ds(r0, tm), :]
    return xt, jnp.concatenate([xh, xt], axis=0)


def _mixa_fwd(x, pre_g, pool_w, pool_scale, post_g, rider=None):
    s_len = x.shape[0]
    n = s_len // TM

    def body(x_ref, pg_ref, w_ref, sc_ref, qg_ref, y_ref, x1_ref):
        i = pl.program_id(0)
        xt, xe = _load_with_halo_before(x_ref, i, TM)
        he = _rms_fwd(xe, pg_ref[0:1, :])
        pos = i * TM - HALO + lax.broadcasted_iota(jnp.int32, (TM + HALO, 1), 0)
        ds = _pool_delta(he, pos)
        ys = [_dot(ds[gi][HALO:, :].astype(BF), w_ref[gi]) for gi in range(len(WINDOWS))]
        y = jnp.concatenate(ys, axis=1) * sc_ref[...]
        y_ref[...] = y
        x1_ref[...] = xt + _rms_fwd(y, qg_ref[0:1, :])

    return _call(body, name="mixa_fwd", grid=(n,),
                 in_specs=[VSPEC] * 5, out_specs=[_row_spec(TM), _row_spec(TM)],
                 out_shape=[_sds((s_len, D)), _sds((s_len, D))],
                 args=[x, pre_g, pool_w, pool_scale, post_g], rider=rider)


def _mixa_bwd(dx1, x, y, pre_g, pool_w, pool_scale, post_g, rider=None):
    s_len = x.shape[0]
    n = s_len // TM
    ng = len(WINDOWS)

    def body(dx_ref, x_ref, y_ref, pg_ref, w_ref, sc_ref, qg_ref,
             dx0_ref, dw_ref, dsc_ref, dqg_ref, dpg_ref, wacc):
        i = pl.program_id(0)

        @pl.when(i == 0)
        def _():
            wacc[...] = jnp.zeros_like(wacc)
            dsc_ref[...] = jnp.zeros_like(dsc_ref)
            dqg_ref[...] = jnp.zeros_like(dqg_ref)
            dpg_ref[...] = jnp.zeros_like(dpg_ref)

        r0 = pl.multiple_of(i * TM, TM)
        xt, xe = _load_with_halo_before(x_ref, i, TM)
        he = _rms_fwd(xe, pg_ref[0:1, :])
        pos_b = i * TM - HALO + lax.broadcasted_iota(jnp.int32, (TM + HALO, 1), 0)
        ds = _pool_delta(he, pos_b)

        last = i == n - 1
        a0 = pl.multiple_of(jnp.minimum(i * TM + TM, s_len - HALO), 8)
        ye = jnp.concatenate([y_ref[pl.ds(r0, TM), :], y_ref[pl.ds(a0, HALO), :]], axis=0)
        dt = dx_ref[pl.ds(r0, TM), :]
        de = jnp.concatenate([dt, jnp.where(last, 0.0, dx_ref[pl.ds(a0, HALO), :])], axis=0)
        dye, prod = _rms_bwd(ye, qg_ref[0:1, :], de)
        dqg_ref[...] += _rowsum(prod[:TM, :])
        dys = dye * sc_ref[...]
        pos_a = i * TM + lax.broadcasted_iota(jnp.int32, (TM + HALO, 1), 0)

        dhs, dscs = [], []
        for gi, w in enumerate(WINDOWS):
            sl = slice(gi * POOL_G, (gi + 1) * POOL_G)
            wg = w_ref[gi]
            dys_g = dys[:, sl].astype(BF)
            d_g = ds[gi][HALO:, :].astype(BF)
            ypre = _dot(d_g, wg)
            dscs.append(_rowsum(dye[:TM, sl] * ypre))
            wacc[gi] += _dot_tn(d_g, dys_g[:TM, :])
            dd = _dot_nt(dys_g, wg)
            cnt = jnp.minimum(pos_a + 1, w).astype(F32)
            a = dd / cnt
            k = 1
            while k < w:
                a = a + pltpu.roll(a, TM + HALO - k, 0)
                k *= 2
            dhs.append(a[:TM, :] - dd[:TM, :])
        dsc_ref[...] += jnp.concatenate(dscs, axis=1)
        dh = jnp.concatenate(dhs, axis=1)
        dxp, prod2 = _rms_bwd(xt, pg_ref[0:1, :], dh)
        dpg_ref[...] += _rowsum(prod2)
        dx0_ref[...] = dt + dxp

        @pl.when(last)
        def _():
            dw_ref[...] = wacc[...].astype(BF)

    return _call(
        body, name="mixa_bwd", grid=(n,), in_specs=[VSPEC] * 7,
        out_specs=[_row_spec(TM), _const_spec((ng, POOL_G, POOL_G)), _const_spec((1, D)),
                   _const_spec((1, D)), _const_spec((1, D))],
        out_shape=[_sds((s_len, D)), _sds((ng, POOL_G, POOL_G), BF), _sds((1, D)), _sds((1, D)), _sds((1, D))],
        scratch_shapes=[pltpu.VMEM((ng, POOL_G, POOL_G), F32)],
        args=[dx1, x, y, pre_g, pool_w, pool_scale, post_g], rider=rider)


def _ffn_fwd(layer, x1, pre_g, wgu, wd, post_g, rider=None):
    s_len = x1.shape[0]

    def body(x_ref, pg_ref, wgu_ref, wd_ref, qg_ref, f_ref, x2_ref):
        x = x_ref[...]
        h = _rms_fwd(x, pg_ref[layer:layer + 1, :]).astype(BF)
        f = jnp.zeros((TM, D), F32)
        for c in range(FF // FF_HALF):
            cols = slice(c * FF_HALF, (c + 1) * FF_HALF)
            g = _dot(h, wgu_ref[0, :, cols])
            u = _dot(h, wgu_ref[1, :, cols])
            act = g * _sigmoid(g) * u
            f = f + _dot(act.astype(BF), wd_ref[cols, :])
        f_ref[...] = f
        x2_ref[...] = x + _rms_fwd(f, qg_ref[layer:layer + 1, :])

    return _call(body, name=f"ffn_fwd{layer}", grid=(s_len // TM,),
                 in_specs=[_row_spec(TM), VSPEC, VSPEC, VSPEC, VSPEC],
                 out_specs=[_row_spec(TM), _row_spec(TM)],
                 out_shape=[_sds((s_len, D)), _sds((s_len, D))],
                 args=[x1, pre_g, wgu, wd, post_g], rider=rider)


GU_PIECE = 128
DN_PIECE = 64
DN_SLOT = FF // N_CHIPS
HALF_D = D // 2


def _ffn_bwd(layer, dx2, x1, f, pre_g, wgu, wd, post_g, kc, rider=None):
    s_len = x1.shape[0]
    tm = TM_FFN_BWD
    n = s_len // tm
    nc = FF // FF_CHUNK
    n_gu, n_dn = FF_CHUNK // GU_PIECE, FF_CHUNK // DN_PIECE
    n_pieces = 2 * n_gu + n_dn
    n_blk = FF_HALF // GU_PIECE

    def edge_rows(c, i, kc_ref):
        return (jnp.where((c == 0) | (c == nc - 1), i, n - 1), 0)

    def chunk_at(c, kc_ref):
        return (c + (((kc_ref[0] + 1) % N_CHIPS) * nc) // N_CHIPS) % nc

    def exchange(kc_ref, c, accg, accu, accd, own_gu_ref, land_gu_ref, own_dn_ref, land_dn_ref,
                 pl_gu, pl_dn, sib_gu, sib_dn, mine_gu, mine_dn, sum_gu, sum_dn,
                 psend, precv, ssend, lsem, rrecv):
        x, y, core = lax.axis_index("x"), lax.axis_index("y"), lax.axis_index("c")
        lower = core == 0

        def pair_copy(cc, part):
            p = cc % 2
            src, dst = ((sib_gu, pl_gu), (sib_dn, pl_dn))[part]
            return pltpu.make_async_remote_copy(src.at[p], dst.at[cc], psend.at[p, part], precv.at[cc, part],
                                                device_id=(x, y, 1 - core), device_id_type=MESH)

        def scatter(cc, wait):
            p = cc % 2
            jobs = []
            for gu in range(2):
                for hc in range(n_gu):
                    hidden = chunk_at(cc, kc_ref) * FF_CHUNK + hc * GU_PIECE
                    k = hidden // FF_HALF
                    jobs.append((sum_gu.at[p, gu, hc], k + 2 * gu, 0,
                                 own_gu_ref, land_gu_ref, ((hidden - k * FF_HALF) // GU_PIECE,)))
            for q in range(n_dn):
                hidden = chunk_at(cc, kc_ref) * FF_CHUNK + q * DN_PIECE
                k = hidden // DN_SLOT
                off = pl.multiple_of(hidden - k * DN_SLOT, DN_PIECE)
                jobs.append((sum_dn.at[p, pl.ds(q * DN_PIECE, DN_PIECE), :], k, 1,
                             own_dn_ref, land_dn_ref, (pl.ds(off, DN_PIECE), slice(None))))
            for pi, (src, k, t, own_ref, land_ref, where) in enumerate(jobs):
                kx, ky = k // 2, k % 2
                fx, fy = (kx != x).astype(jnp.int32), (ky != y).astype(jnp.int32)
                local = (fx + fy) == 0
                j = jnp.maximum(fx + 2 * fy - 1, 0)

                @pl.when(local)
                def _():
                    cp = pltpu.make_async_copy(src, own_ref.at[where], lsem.at[p, pi])
                    if wait:
                        cp.wait()
                    else:
                        cp.start()

                @pl.when(jnp.logical_not(local))
                def _():
                    cp = pltpu.make_async_remote_copy(src, land_ref.at[(j,) + where], ssend.at[p, pi],
                                                      rrecv.at[t, j], device_id=(kx, ky, core), device_id_type=MESH)
                    if wait:
                        cp.wait_send()
                    else:
                        cp.start()

        def add_and_scatter(cc):
            p = cc % 2
            pair_copy(cc, 0).wait_recv()
            pair_copy(cc, 1).wait_recv()
            s_gu = (mine_gu[...] + pl_gu[cc].astype(F32)).astype(BF)
            for hc in range(n_gu):
                sum_gu[p, :, hc] = s_gu[:, :, hc * GU_PIECE:(hc + 1) * GU_PIECE]
            sum_dn[p] = (mine_dn[...] + pl_dn[cc].astype(F32)).astype(BF)
            scatter(cc, wait=False)

        @pl.when(c >= 1)
        def _():
            @pl.when(c >= 3)
            def _():
                scatter(c - 3, wait=True)
            add_and_scatter(c - 1)

        @pl.when(c >= 2)
        def _():
            pair_copy(c - 2, 0).wait_send()
            pair_copy(c - 2, 1).wait_send()

        p = c % 2
        my_rows = pl.ds(pl.multiple_of(core * HALF_D, HALF_D), HALF_D)
        sib_rows = pl.ds(pl.multiple_of((1 - core) * HALF_D, HALF_D), HALF_D)
        d_v = accd[...]
        sib_gu[p, 0] = accg[sib_rows, :].astype(BF)
        sib_gu[p, 1] = accu[sib_rows, :].astype(BF)
        sib_dn[p] = jnp.where(lower, d_v[:, HALF_D:], d_v[:, :HALF_D]).astype(BF)
        mine_gu[0] = accg[my_rows, :]
        mine_gu[1] = accu[my_rows, :]
        mine_dn[...] = jnp.where(lower, d_v[:, :HALF_D], d_v[:, HALF_D:])
        pair_copy(c, 0).start()
        pair_copy(c, 1).start()

        @pl.when(c == nc - 1)
        def _():
            scatter(nc - 3, wait=True)
            add_and_scatter(nc - 1)
            for cc in (nc - 2, nc - 1):
                pair_copy(cc, 0).wait_send()
                pair_copy(cc, 1).wait_send()
                scatter(cc, wait=True)
            for t, land_ref in enumerate((land_gu_ref, land_dn_ref)):
                for j in range(N_CHIPS - 1):
                    pltpu.make_async_remote_copy(land_ref.at[j], land_ref.at[j], ssend.at[0, 0], rrecv.at[t, j],
                                                 device_id=(x, y, core), device_id_type=MESH).wait_recv()

    def body(kc_ref, dx_ref, x_ref, f_ref, pg_ref, wgu_ref, wd_ref, qg_ref,
             dx1_ref, dpg_ref, dqg_ref, own_gu_ref, land_gu_ref, own_dn_ref, land_dn_ref,
             h_s, df_s, dh_s, accg, accu, accd, *comm):
        c = pl.program_id(0)
        i = pl.program_id(1)
        rows = pl.ds(pl.multiple_of(i * tm, tm), tm)
        pg = pg_ref[layer:layer + 1, :]

        @pl.when((c == 0) & (i == 0))
        def _():
            dpg_ref[...] = jnp.zeros_like(dpg_ref)
            dqg_ref[...] = jnp.zeros_like(dqg_ref)

        @pl.when(c == 0)
        def _():
            h_s[rows, :] = _rms_fwd(x_ref[...], pg).astype(BF)
            df, prod = _rms_bwd(f_ref[...], qg_ref[layer:layer + 1, :], dx_ref[...])
            df_s[rows, :] = df.astype(BF)
            dqg_ref[...] += _rowsum(prod)

        @pl.when(i == 0)
        def _():
            accg[...] = jnp.zeros_like(accg)
            accu[...] = jnp.zeros_like(accu)
            accd[...] = jnp.zeros_like(accd)

        h = h_s[rows, :]
        df = df_s[rows, :]
        wg = wgu_ref[0]
        wu = wgu_ref[1]
        g = _dot(h, wg)
        u = _dot(h, wu)
        sg = _sigmoid(g)
        a = g * sg
        dact = _dot_nt(df, wd_ref[...])
        accd[...] += _dot_tn((a * u).astype(BF), df)
        du = (dact * a).astype(BF)
        dg = (dact * u * (sg * (1.0 + g * (1.0 - sg)))).astype(BF)
        accg[...] += _dot_tn(h, dg)
        accu[...] += _dot_tn(h, du)
        dh = _dot_nt(dg, wg) + _dot_nt(du, wu)

        @pl.when(c == 0)
        def _():
            dh_s[rows, :] = dh

        @pl.when((c > 0) & (c < nc - 1))
        def _():
            dh_s[rows, :] += dh

        @pl.when(c == nc - 1)
        def _():
            dxp, prod = _rms_bwd(x_ref[...], pg, dh_s[rows, :] + dh)
            dpg_ref[...] += _rowsum(prod)
            dx1_ref[...] = dx_ref[...] + dxp

        @pl.when(i == n - 1)
        def _():
            exchange(kc_ref, c, accg, accu, accd, own_gu_ref, land_gu_ref, own_dn_ref, land_dn_ref, *comm)

    dma = pltpu.SemaphoreType.DMA
    return _call(
        body, name=f"ffn_bwd{layer}", grid=(nc, n),
        in_specs=[pl.BlockSpec((tm, D), edge_rows), pl.BlockSpec((tm, D), edge_rows),
                  pl.BlockSpec((tm, D), lambda c, i, kc_ref: (jnp.where(c == 0, i, n - 1), 0),
                               pipeline_mode=pl.Buffered(1)),
                  VSPEC,
                  pl.BlockSpec((2, D, FF_CHUNK), lambda c, i, kc_ref: (0, 0, chunk_at(c, kc_ref))),
                  pl.BlockSpec((FF_CHUNK, D), lambda c, i, kc_ref: (chunk_at(c, kc_ref), 0)),
                  VSPEC],
        out_specs=[pl.BlockSpec((tm, D), lambda c, i, kc_ref: (jnp.where(c == nc - 1, i, 0), 0)),
                   _const_spec((1, D)), _const_spec((1, D)), ANYSPEC, ANYSPEC, ANYSPEC, ANYSPEC],
        out_shape=[_sds((s_len, D)), _sds((1, D)), _sds((1, D)),
                   _sds((n_blk, HALF_D, GU_PIECE), BF), _sds((N_CHIPS - 1, n_blk, HALF_D, GU_PIECE), BF),
                   _sds((DN_SLOT, HALF_D), BF), _sds((N_CHIPS - 1, DN_SLOT, HALF_D), BF)],
        scratch_shapes=[pltpu.VMEM((s_len, D), BF), pltpu.VMEM((s_len, D), BF), pltpu.VMEM((s_len, D), F32),
                        pltpu.VMEM((D, FF_CHUNK), F32), pltpu.VMEM((D, FF_CHUNK), F32),
                        pltpu.VMEM((FF_CHUNK, D), F32),
                        pltpu.VMEM((nc, 2, HALF_D, FF_CHUNK), BF), pltpu.VMEM((nc, FF_CHUNK, HALF_D), BF),
                        pltpu.VMEM((2, 2, HALF_D, FF_CHUNK), BF), pltpu.VMEM((2, FF_CHUNK, HALF_D), BF),
                        pltpu.VMEM((2, HALF_D, FF_CHUNK), F32), pltpu.VMEM((FF_CHUNK, HALF_D), F32),
                        pltpu.VMEM((2, 2, n_gu, HALF_D, GU_PIECE), BF), pltpu.VMEM((2, FF_CHUNK, HALF_D), BF),
                        dma((2, 2)), dma((nc, 2)), dma((2, n_pieces)), dma((2, n_pieces)), dma((2, N_CHIPS - 1))],
        args=[dx2, x1, f, pre_g, wgu, wd, post_g], rider=rider, prefetch=kc)


def _ple_fwd(layer, x2, p, ple_g, w_gate, w_proj, post_g, target=None, qkv=None, rider=None):
    s_len = x2.shape[0]
    final = target is not None
    assert not (final and qkv)

    def body(*refs):
        if final:
            x_ref, p_ref, g_ref, wg_ref, wp_ref, qg_ref, t_ref, z_ref, pe_ref, dx_ref, lv_ref = refs
        elif qkv:
            (x_ref, p_ref, g_ref, wg_ref, wp_ref, qg_ref, ng_ref, kg_ref, wq_ref, wkv_ref,
             z_ref, pe_ref, x3_ref, q_ref, kv_ref) = refs
        else:
            x_ref, p_ref, g_ref, wg_ref, wp_ref, qg_ref, z_ref, pe_ref, x3_ref = refs
        x = x_ref[...]
        r = _rms_fwd(x, g_ref[layer:layer + 1, :]).astype(BF)
        z = _dot(r, wg_ref[...])
        pe = _dot(p_ref[...].astype(BF), wp_ref[...])
        z_ref[...] = z
        pe_ref[...] = pe
        x3 = x + _rms_fwd(pe * _sigmoid(z), qg_ref[layer:layer + 1, :])
        if final:
            @pl.when(pl.program_id(0) == 0)
            def _():
                lv_ref[...] = jnp.zeros_like(lv_ref)
            err = x3 - t_ref[...]
            dx_ref[...] = err * (1.0 / D)
            lv_ref[...] += _rowsum(err * err)
        else:
            x3_ref[...] = x3
        if qkv:
            q_ref[...] = _dot(_rms_fwd(x3, ng_ref[layer + 1:layer + 2, :]).astype(BF), wq_ref[...]).astype(BF)
            kv_ref[...] = _dot(_rms_fwd(x3, kg_ref[...]).astype(BF), wkv_ref[...]).astype(BF)

    p_spec = pl.BlockSpec((None, TM, PLE), lambda i: (layer, i, 0))
    in_specs = [_row_spec(TM), p_spec, VSPEC, VSPEC, VSPEC, VSPEC]
    args = [x2, p, ple_g, w_gate, w_proj, post_g]
    out_specs = [_row_spec(TM), _row_spec(TM), _row_spec(TM)]
    out_shape = [_sds((s_len, D))] * 3
    if qkv:
        in_specs += [VSPEC] * 4
        args += list(qkv)
        out_specs += [_row_spec(TM), _row_spec(TM, 2 * KVD)]
        out_shape += [_sds((s_len, D), BF), _sds((s_len, 2 * KVD), BF)]
    if final:
        in_specs.append(_row_spec(TM))
        args.append(target)
        out_specs.append(_const_spec((1, D)))
        out_shape.append(_sds((1, D)))
    return _call(body, name=f"ple_fwd{layer}", grid=(s_len // TM,), in_specs=in_specs, out_specs=out_specs,
                 out_shape=out_shape, args=args, rider=rider)


def _ple_bwd(layer, dx3, x2, z, pe, p, ple_g, w_gate, post_g, rider=None):
    s_len = x2.shape[0]
    n = s_len // TM

    def body(dx_ref, x_ref, z_ref, pe_ref, p_ref, g_ref, wg_ref, qg_ref,
             dx2_ref, dwg_ref, dwp_ref, dg_ref, dqg_ref, gacc, pacc):
        i = pl.program_id(0)

        @pl.when(i == 0)
        def _():
            gacc[...] = jnp.zeros_like(gacc)
            pacc[...] = jnp.zeros_like(pacc)
            dg_ref[...] = jnp.zeros_like(dg_ref)
            dqg_ref[...] = jnp.zeros_like(dqg_ref)

        dx = dx_ref[...]
        x = x_ref[...]
        pe_v = pe_ref[...]
        gate = _sigmoid(z_ref[...])
        de, prod = _rms_bwd(pe_v * gate, qg_ref[layer:layer + 1, :], dx)
        dqg_ref[...] += _rowsum(prod)
        dpe = (de * gate).astype(BF)
        dz = (de * pe_v * gate * (1.0 - gate)).astype(BF)
        pacc[...] += _dot_tn(p_ref[...].astype(BF), dpe)
        g = g_ref[layer:layer + 1, :]
        r = _rms_fwd(x, g).astype(BF)
        gacc[...] += _dot_tn(r, dz)
        dr = _dot_nt(dz, wg_ref[...])
        dxp, prod2 = _rms_bwd(x, g, dr)
        dg_ref[...] += _rowsum(prod2)
        dx2_ref[...] = dx + dxp

        @pl.when(i == n - 1)
        def _():
            dwg_ref[...] = gacc[...].astype(BF)
            dwp_ref[...] = pacc[...].astype(BF)

    p_spec = pl.BlockSpec((None, TM, PLE), lambda i: (layer, i, 0))
    return _call(
        body, name=f"ple_bwd{layer}", grid=(n,),
        in_specs=[_row_spec(TM), _row_spec(TM), _row_spec(TM), _row_spec(TM), p_spec, VSPEC, VSPEC, VSPEC],
        out_specs=[_row_spec(TM), _const_spec((D, D)), _const_spec((PLE, D)), _const_spec((1, D)), _const_spec((1, D))],
        out_shape=[_sds((s_len, D)), _sds((D, D), BF), _sds((PLE, D), BF), _sds((1, D)), _sds((1, D))],
        scratch_shapes=[pltpu.VMEM((D, D), F32), pltpu.VMEM((PLE, D), F32)],
        args=[dx3, x2, z, pe, p, ple_g, w_gate, post_g], rider=rider)


def _qkv_bwd(dq, dkv, x3, dx4, q_g, kv_g, w_q, w_kv):
    s_len = x3.shape[0]
    n = s_len // TM

    def body(dq_ref, dkv_ref, x_ref, dx_ref, qg_ref, kg_ref, wq_ref, wkv_ref,
             dx3_ref, dwq_ref, dwkv_ref, dqg_ref, dkg_ref, qacc, kacc):
        i = pl.program_id(0)

        @pl.when(i == 0)
        def _():
            qacc[...] = jnp.zeros_like(qacc)
            kacc[...] = jnp.zeros_like(kacc)
            dqg_ref[...] = jnp.zeros_like(dqg_ref)
            dkg_ref[...] = jnp.zeros_like(dkg_ref)

        x = x_ref[...]
        qg = qg_ref[1:2, :]
        kg = kg_ref[...]
        dq_v = dq_ref[...]
        dkv_v = dkv_ref[...].astype(BF)
        qacc[...] += _dot_tn(_rms_fwd(x, qg).astype(BF), dq_v)
        kacc[...] += _dot_tn(_rms_fwd(x, kg).astype(BF), dkv_v)
        dxq, prod_q = _rms_bwd(x, qg, _dot_nt(dq_v, wq_ref[...]))
        dxk, prod_k = _rms_bwd(x, kg, _dot_nt(dkv_v, wkv_ref[...]))
        dqg_ref[...] += _rowsum(prod_q)
        dkg_ref[...] += _rowsum(prod_k)
        dx3_ref[...] = dx_ref[...] + dxq + dxk

        @pl.when(i == n - 1)
        def _():
            dwq_ref[...] = qacc[...].astype(BF)
            dwkv_ref[...] = kacc[...].astype(BF)

    outs, _ = _call(
        body, name="qkv_bwd", grid=(n,),
        in_specs=[_row_spec(TM), _row_spec(TM, 2 * KVD), _row_spec(TM), _row_spec(TM), VSPEC, VSPEC, VSPEC, VSPEC],
        out_specs=[_row_spec(TM), _const_spec((D, D)), _const_spec((D, 2 * KVD)),
                   _const_spec((1, D)), _const_spec((1, D))],
        out_shape=[_sds((s_len, D)), _sds((D, D), BF), _sds((D, 2 * KVD), BF), _sds((1, D)), _sds((1, D))],
        scratch_shapes=[pltpu.VMEM((D, D), F32), pltpu.VMEM((D, 2 * KVD), F32)],
        args=[dq, dkv, x3, dx4, q_g, kv_g, w_q, w_kv])
    return outs


def _attn_group(i, q, kvw, sink_ref, g):
    rows = GQA * BLK
    heads = [GQA * g + j for j in range(GQA)]
    off = jnp.where(i > 0, BLK, 0)
    row = lax.broadcasted_iota(jnp.int32, (rows, 2 * BLK), 0)
    rel = (row % BLK) - lax.broadcasted_iota(jnp.int32, (rows, 2 * BLK), 1) + off
    valid = (rel >= 0) & (rel < BLK)
    head_of_row = lax.broadcasted_iota(jnp.int32, (rows, 1), 0) // BLK
    slope = jnp.zeros((rows, 1), F32)
    sink = jnp.zeros((rows, 1), F32)
    for j, h in enumerate(heads):
        slope = jnp.where(head_of_row == j, SLOPES[h], slope)
        sink = jnp.where(head_of_row == j, sink_ref[0, h], sink)
    qs = jnp.concatenate([q[:, h * HEAD_DIM:(h + 1) * HEAD_DIM] for h in heads], axis=0)
    k = kvw[:, g * HEAD_DIM:(g + 1) * HEAD_DIM]
    v = kvw[:, KVD + g * HEAD_DIM:KVD + (g + 1) * HEAD_DIM]
    s = _dot_nt(qs, k) * ATT_SCALE - slope * rel.astype(F32)
    s = jnp.where(valid, s, NEG_INF)
    m = jnp.maximum(jnp.max(s, axis=-1, keepdims=True), sink)
    e = jnp.exp(s - m)
    es = jnp.exp(sink - m)
    inv = 1.0 / (jnp.sum(e, axis=-1, keepdims=True) + es)
    return e * inv, es * inv, qs, k, v


def _unstack_heads(stacked):
    return [stacked[j * BLK:(j + 1) * BLK, :] for j in range(GQA)]


def _kv_window(kv_ref, i):
    ks = pl.multiple_of(jnp.maximum(i * BLK - BLK, 0), BLK)
    return ks, kv_ref[pl.ds(ks, 2 * BLK), :]


def _attn_fwd(q, kv, sinks, x3, w_o, post_g, rider=None):
    s_len = q.shape[0]

    def body(q_ref, kv_ref, sk_ref, x_ref, wo_ref, g_ref, a_ref, y_ref, x4_ref):
        i = pl.program_id(0)
        _, kvw = _kv_window(kv_ref, i)
        q = q_ref[...]
        outs = []
        for g in range(N_KV_HEADS):
            p, _, _, _, v = _attn_group(i, q, kvw, sk_ref, g)
            outs += _unstack_heads(_dot(p.astype(BF), v))
        attn = jnp.concatenate(outs, axis=1)
        a_ref[...] = attn
        y = _dot(attn.astype(BF), wo_ref[...])
        y_ref[...] = y
        x4_ref[...] = x_ref[...] + _rms_fwd(y, g_ref[1:2, :])

    return _call(body, name="attn_fwd", grid=(s_len // BLK,),
                 in_specs=[_row_spec(BLK), VSPEC, SSPEC, _row_spec(BLK), VSPEC, VSPEC],
                 out_specs=[_row_spec(BLK)] * 3, out_shape=[_sds((s_len, D))] * 3,
                 args=[q, kv, sinks, x3, w_o, post_g], rider=rider)


def _attn_bwd(dx4, y, attn, q, kv, sinks, w_o, post_g, rider=None):
    s_len = q.shape[0]
    n = s_len // BLK

    def body(dx_ref, y_ref, a_ref, q_ref, kv_ref, sk_ref, wo_ref, g_ref,
             dq_ref, dkv_ref, dwo_ref, dg_ref, dsk_ref, wacc):
        i = pl.program_id(0)

        @pl.when(i == 0)
        def _():
            dkv_ref[...] = jnp.zeros_like(dkv_ref)
            wacc[...] = jnp.zeros_like(wacc)
            dg_ref[...] = jnp.zeros_like(dg_ref)
            dsk_ref[...] = jnp.zeros_like(dsk_ref)

        dy, prod = _rms_bwd(y_ref[...], g_ref[1:2, :], dx_ref[...])
        dg_ref[...] += _rowsum(prod)
        dyb = dy.astype(BF)
        attn = a_ref[...]
        wacc[...] += _dot_tn(attn.astype(BF), dyb)
        d_o = _dot_nt(dyb, wo_ref[...])
        dod = d_o * attn
        ks, kvw = _kv_window(kv_ref, i)
        q = q_ref[...]
        lane = lax.broadcasted_iota(jnp.int32, (1, D), 1)
        dqs, dks, dvs = [], [], []
        dsk = jnp.zeros((1, D), F32)
        for g in range(N_KV_HEADS):
            p, ps, qs, k, v = _attn_group(i, q, kvw, sk_ref, g)
            cols = [slice((GQA * g + j) * HEAD_DIM, (GQA * g + j + 1) * HEAD_DIM) for j in range(GQA)]
            do_s = jnp.concatenate([d_o[:, c] for c in cols], axis=0).astype(BF)
            dsum = jnp.concatenate([jnp.sum(dod[:, c], axis=-1, keepdims=True) for c in cols], axis=0)
            dp = _dot_nt(do_s, v)
            dsb = (p * (dp - dsum) * ATT_SCALE).astype(BF)
            sink_part = ps * dsum
            for j in range(GQA):
                dsk = dsk + jnp.where(lane == GQA * g + j, -_rowsum(sink_part[j * BLK:(j + 1) * BLK, :]), 0.0)
            dqs += _unstack_heads(_dot(dsb, k))
            dks.append(_dot_tn(dsb, qs))
            dvs.append(_dot_tn(p.astype(BF), do_s))
        dsk_ref[...] += dsk
        dq_ref[...] = jnp.concatenate(dqs, axis=1).astype(BF)
        dkv_ref[pl.ds(ks, 2 * BLK), :] += jnp.concatenate(dks + dvs, axis=1)

        @pl.when(i == n - 1)
        def _():
            dwo_ref[...] = wacc[...].astype(BF)

    return _call(
        body, name="attn_bwd", grid=(n,),
        in_specs=[_row_spec(BLK), _row_spec(BLK), _row_spec(BLK), _row_spec(BLK), VSPEC, SSPEC, VSPEC, VSPEC],
        out_specs=[_row_spec(BLK), _const_spec((s_len, 2 * KVD)), _const_spec((D, D)),
                   _const_spec((1, D)), _const_spec((1, D))],
        out_shape=[_sds((s_len, D), BF), _sds((s_len, 2 * KVD)), _sds((D, D), BF), _sds((1, D)), _sds((1, D))],
        scratch_shapes=[pltpu.VMEM((D, D), F32)],
        args=[dx4, y, attn, q, kv, sinks, w_o, post_g], rider=rider)


Big = collections.namedtuple("Big", "name src layer L A R C rb")


def _bigs():
    out = {"pool_w": Big("pool_w", "pool_w", None, 4, 4, POOL_G // N_CHIPS, POOL_G, 32)}
    for l in range(2):
        out[f"w_gu{l}"] = Big(f"w_gu{l}", "w_gu", l, 1, 2, D, FF_HALF, 256)
        out[f"w_down{l}"] = Big(f"w_down{l}", "w_down", l, 1, 4, FF // N_CHIPS, D, 352)
        out[f"w_ple_gate{l}"] = Big(f"w_ple_gate{l}", "w_ple_gate", l, 1, 4, D // N_CHIPS, D, 128)
        out[f"w_ple_proj{l}"] = Big(f"w_ple_proj{l}", "w_ple_proj", l, 1, 1, PLE, D // N_CHIPS, 128)
    out["w_q"] = Big("w_q", "w_q", None, 1, 4, D // N_CHIPS, D, 128)
    out["w_o"] = Big("w_o", "w_o", None, 1, 4, D // N_CHIPS, D, 128)
    out["w_kv"] = Big("w_kv", "w_kv", None, 1, 4, D // N_CHIPS, 2 * KVD, 128)
    return out


BIGS = _bigs()
POOL_SCALE = Big("pool_scale", "pool_scale", None, 1, 1, 1, D // N_CHIPS, 1)
BIG_SOURCES = ("w_gu", "w_down", "w_ple_gate", "w_ple_proj", "w_q", "w_o", "w_kv", "pool_w")


def _ncb(t):
    return N_CHIPS // t.A


def _full_shape(t, rows=None):
    return (t.L, t.A, t.R if rows is None else rows, _ncb(t) * t.C)


def _slot_index(t, k):
    return k // _ncb(t), k % _ncb(t)


def _slot(ref, t, k, row0, rows):
    a, cb = _slot_index(t, k)
    return ref.at[:, a, pl.ds(row0, rows), pl.ds(pl.multiple_of(cb * t.C, 128), t.C)]


def _place(t, w, kc, out_dtype):
    rb = min(t.R, 2 * t.rb)

    def body(kc_ref, w_ref, o_ref):
        del kc_ref
        o_ref[...] = w_ref[...].astype(out_dtype)

    def in_map(l, j, kc_ref):
        return (l if t.layer is None else t.layer, j, 0)

    def out_map(l, j, kc_ref):
        a, cb = _slot_index(t, kc_ref[0])
        return (l, a, j, cb)

    return pl.pallas_call(
        body, name=f"place_{t.name}",
        grid_spec=pltpu.PrefetchScalarGridSpec(
            num_scalar_prefetch=1, grid=(t.L, t.R // rb),
            in_specs=[pl.BlockSpec((None, rb, t.C), in_map)],
            out_specs=pl.BlockSpec((None, None, rb, t.C), out_map)),
        out_shape=_sds(_full_shape(t), out_dtype),
        compiler_params=_params(2),
    )(kc, w)


def _place_many(ts, ws, kc, rider):
    n = 8

    def blocks_of(t):
        return next(nb for nb in (8, 4, 2, 1) if t.R % (16 * nb) == 0)

    def body(kc_ref, *refs):
        del kc_ref
        s = pl.program_id(0)
        for ti, t in enumerate(ts):
            @pl.when(s < blocks_of(t))
            def _():
                refs[len(ts) + ti][...] = refs[ti][...].astype(BF)

    in_specs, out_specs = [], []
    for t in ts:
        assert t.L == 1
        nb = blocks_of(t)
        rb = t.R // nb

        def in_map(s, kc_ref, t=t, nb=nb):
            return (0 if t.layer is None else t.layer, jnp.minimum(s, nb - 1), 0)

        def out_map(s, kc_ref, t=t, nb=nb):
            a, cb = _slot_index(t, kc_ref[0])
            return (0, a, jnp.minimum(s, nb - 1), cb)

        in_specs.append(pl.BlockSpec((None, rb, t.C), in_map))
        out_specs.append(pl.BlockSpec((None, None, rb, t.C), out_map))
    return _call(body, name="place_rest", grid=(n,), in_specs=in_specs, out_specs=out_specs,
                 out_shape=[_sds(_full_shape(t), BF) for t in ts], args=list(ws), rider=rider, prefetch=kc)


def _mesh_position():
    x, y, c = lax.axis_index("x"), lax.axis_index("y"), lax.axis_index("c")
    chips = [(1 - x, y), (x, 1 - y), (1 - x, 1 - y)]
    return x, y, c, chips


def _gather_rider(parts, fulls):
    nt = len(parts)
    TO_X, TO_Y, FWD_X, FWD_Y, SIB_X, SIB_Y, SIB_D = range(7)

    def rows_of(ti, core):
        t, r0, r1 = parts[ti]
        h = (r1 - r0) // 2
        return r0 + core * h, h

    def copy(outs, sems, kind, ti, k_src, row0, rows, dev):
        region = _slot(outs[ti], parts[ti][0], k_src, row0, rows)
        return pltpu.make_async_remote_copy(region, region, sems[0].at[ti, kind], sems[1].at[ti, kind],
                                            device_id=dev, device_id_type=MESH)

    def plan(outs, sems):
        x, y, c, _ = _mesh_position()
        me, kx, ky, kd = 2 * x + y, 2 * (1 - x) + y, 2 * x + (1 - y), 2 * (1 - x) + (1 - y)
        dev_x, dev_y, dev_d, sib = (1 - x, y, c), (x, 1 - y, c), (1 - x, 1 - y, c), (x, y, 1 - c)

        def whole(ti):
            return 0, parts[ti][0].R

        def mk(kind, k_send, k_recv, dev, send_rows, recv_rows):
            def build(ti, side):
                k_src = k_send if side == "s" else k_recv
                row0, rows = (send_rows if side == "s" else recv_rows)(ti)
                return copy(outs, sems, kind, ti, k_src, row0, rows, dev)
            return build

        def first_half(core):
            return lambda ti: (rows_of(ti, core)[0], rows_of(ti, core)[1] // 2)

        def second_half(core):
            return lambda ti: (rows_of(ti, core)[0] + rows_of(ti, core)[1] // 2, rows_of(ti, core)[1] // 2)

        mine = lambda ti: rows_of(ti, c)
        theirs = lambda ti: rows_of(ti, 1 - c)
        split = {
            TO_X: mk(TO_X, me, kx, dev_x, mine, mine),
            TO_Y: mk(TO_Y, me, ky, dev_y, mine, mine),
            FWD_X: mk(FWD_X, ky, kd, dev_x, first_half(c), first_half(c)),
            FWD_Y: mk(FWD_Y, kx, kd, dev_y, second_half(c), second_half(c)),
            SIB_X: mk(SIB_X, kx, kx, sib, mine, theirs),
            SIB_Y: mk(SIB_Y, ky, ky, sib, mine, theirs),
            SIB_D: mk(SIB_D, kd, kd, sib, mine, theirs),
        }
        direct = {
            TO_X: mk(TO_X, me, kx, dev_x, whole, whole),
            TO_Y: mk(TO_Y, me, ky, dev_y, whole, whole),
            FWD_X: mk(FWD_X, me, kd, dev_d, whole, whole),
        }
        return split, direct

    is_split = [t.R > 1 for t, _, _ in parts]

    def start(ins, outs, sems):
        split, direct = plan(outs, sems)
        for ti in range(nt):
            kinds = split if is_split[ti] else direct
            kinds[TO_X](ti, "s").start()
            kinds[TO_Y](ti, "s").start()
            if not is_split[ti]:
                kinds[FWD_X](ti, "s").start()

    def mid(ins, outs, sems):
        split, _ = plan(outs, sems)
        for ti in range(nt):
            if is_split[ti]:
                split[TO_Y](ti, "r").wait_recv()
                split[FWD_X](ti, "s").start()
                split[SIB_Y](ti, "s").start()
        for ti in range(nt):
            if is_split[ti]:
                split[TO_X](ti, "r").wait_recv()
                split[FWD_Y](ti, "s").start()
                split[SIB_X](ti, "s").start()

    def finish(ins, outs, sems):
        split, direct = plan(outs, sems)
        for ti in range(nt):
            if is_split[ti]:
                split[FWD_X](ti, "r").wait_recv()
                split[FWD_Y](ti, "r").wait_recv()
                split[SIB_D](ti, "s").start()
            else:
                for kind in (TO_X, TO_Y, FWD_X):
                    direct[kind](ti, "r").wait_recv()
        for ti in range(nt):
            if is_split[ti]:
                for kind in (SIB_X, SIB_Y, SIB_D):
                    split[kind](ti, "r").wait_recv()
        for ti in range(nt):
            kinds = split if is_split[ti] else direct
            for kind in kinds:
                kinds[kind](ti, "s").wait_send()

    sems = pltpu.SemaphoreType.DMA((nt, 7))
    return Rider(list(fulls), [_sds(a.shape, a.dtype) for a in fulls], {i: i for i in range(nt)},
                 [sems, sems], start, mid, finish)


def _pair_exchange(name, specs, grads):
    nt = len(specs)

    def body(*refs):
        gs = refs[:nt]
        lands = refs[nt:2 * nt]
        send, recv = refs[2 * nt:]
        x, y, c, _ = _mesh_position()
        cps = []
        for ti, t in enumerate(specs):
            half = t.R // 2
            cp = pltpu.make_async_remote_copy(gs[ti].at[:, :, pl.ds((1 - c) * half, half), :], lands[ti],
                                              send.at[ti], recv.at[ti],
                                              device_id=(x, y, 1 - c), device_id_type=MESH)
            cp.start()
            cps.append(cp)
        for cp in cps:
            cp.wait()

    return pl.pallas_call(
        body, name=name,
        in_specs=[ANYSPEC] * nt, out_specs=[ANYSPEC] * nt,
        out_shape=[_sds(_full_shape(t, t.R // 2), BF) for t in specs],
        scratch_shapes=[pltpu.SemaphoreType.DMA((nt,)), pltpu.SemaphoreType.DMA((nt,))],
        compiler_params=_params(),
    )(*grads)


def _pair_sum_job(t, g, land):
    assert t.L == 1
    half = t.R // 2
    nj = half // t.rb
    block = (None, t.A, t.rb, _ncb(t) * t.C)

    def fn(j, kc_ref, ins, outs):
        outs[0][...] = (ins[0][...].astype(F32) + ins[1][...].astype(F32)).astype(BF)

    return Job(nj,
               [(g, block, lambda j, kc_ref: (0, 0, kc_ref[1] * nj + j, 0)),
                (land, block, lambda j, kc_ref: (0, 0, j, 0))],
               [(_sds(_full_shape(t, half), BF), block, lambda j, kc_ref: (0, 0, j, 0))], fn)


def _scatter_rider(specs, sums):
    nt = len(specs)

    def copy(ins, outs, sems, ti, j, chip, c):
        t = specs[ti]
        cx, cy = chip
        return pltpu.make_async_remote_copy(_slot(ins[ti], t, 2 * cx + cy, 0, t.R // 2), outs[ti].at[j],
                                            sems[0].at[ti, j], sems[1].at[ti, j],
                                            device_id=(cx, cy, c), device_id_type=MESH)

    def start(ins, outs, sems):
        _, _, c, chips = _mesh_position()
        for j, chip in enumerate(chips):
            for ti in range(nt):
                copy(ins, outs, sems, ti, j, chip, c).start()

    def finish(ins, outs, sems):
        _, _, c, chips = _mesh_position()
        for j, chip in enumerate(chips):
            for ti in range(nt):
                copy(ins, outs, sems, ti, j, chip, c).wait()

    sems = pltpu.SemaphoreType.DMA((nt, N_CHIPS - 1))
    return Rider(list(sums), [_sds((N_CHIPS - 1, t.L, t.R // 2, t.C), BF) for t in specs], {}, [sems, sems],
                 start, None, finish)


def _chip_sum_job(ts, landed):
    t0 = ts[0]
    assert t0.L == 1
    half = t0.R // 2
    nj = half // t0.rb

    def local(j, li):
        return jnp.clip(j - li * nj, 0, nj - 1)

    ins = []
    for li, t in enumerate(ts):
        s, land = landed[t.name]

        def own_map(j, kc_ref, li=li, t=t):
            a, cb = _slot_index(t, kc_ref[0])
            return (0, a, local(j, li), cb)

        ins.append((s, (None, None, t.rb, t.C), own_map))
        ins.append((land, (N_CHIPS - 1, None, t.rb, t.C), lambda j, kc_ref, li=li: (0, 0, local(j, li), 0)))

    def fn(j, kc_ref, in_refs, outs):
        for li in range(len(ts)):
            @pl.when(j // nj == li)
            def _():
                acc = in_refs[2 * li][...].astype(F32)
                for k in range(N_CHIPS - 1):
                    acc = acc + in_refs[2 * li + 1][k].astype(F32)
                outs[0][...] = acc

    return Job(len(ts) * nj, ins,
               [(_sds((len(ts), t0.R, t0.C)), (None, t0.rb, t0.C),
                 lambda j, kc_ref: (j // nj, kc_ref[1] * nj + j % nj, 0))], fn)


def _adamw_job(rb, w, g, m, v):
    n_layers, r, c = w.shape
    nb = r // rb
    block = (None, rb, c)
    index = lambda j, kc_ref: (j // nb, j % nb, 0)

    def fn(j, kc_ref, ins, outs):
        g_v = ins[1][...]
        outs[0][...] = g_v
        outs[1][...], outs[2][...], outs[3][...] = _adamw_math(ins[0][...], g_v, ins[2][...], ins[3][...])

    return Job(n_layers * nb, [(a, block, index) for a in (w, g, m, v)],
               [(_sds(w.shape), block, index)] * 4, fn)


def _chip_sum_fused_job(ts, fused, by_cols):
    t0 = ts[0]
    own0 = fused[t0.name][0]
    if by_cols:
        rows, cols = own0.shape
    else:
        nb, rows, bw = own0.shape
        cols = nb * bw
    nj = rows // t0.rb

    def local(j, li):
        return jnp.clip(j - li * nj, 0, nj - 1)

    ins = []
    for li, t in enumerate(ts):
        own, land = fused[t.name]
        if by_cols:
            ins.append((own, (t.rb, cols), lambda j, kc_ref, li=li: (local(j, li), 0)))
            ins.append((land, (N_CHIPS - 1, t.rb, cols), lambda j, kc_ref, li=li: (0, local(j, li), 0)))
        else:
            ins.append((own, (nb, t.rb, bw), lambda j, kc_ref, li=li: (0, local(j, li), 0)))
            ins.append((land, (N_CHIPS - 1, nb, t.rb, bw), lambda j, kc_ref, li=li: (0, 0, local(j, li), 0)))

    def fn(j, kc_ref, in_refs, outs):
        for li in range(len(ts)):
            @pl.when(j // nj == li)
            def _():
                acc = in_refs[2 * li][...].astype(F32)
                for k in range(N_CHIPS - 1):
                    acc = acc + in_refs[2 * li + 1][k].astype(F32)
                outs[0][...] = acc if by_cols else jnp.concatenate([acc[b] for b in range(nb)], axis=1)

    def out_map(j, kc_ref):
        return (j // nj, j % nj, kc_ref[1]) if by_cols else (j // nj, kc_ref[1] * nj + j % nj, 0)

    return Job(len(ts) * nj, ins, [(_sds((len(ts), t0.R, t0.C)), (None, t0.rb, cols), out_map)], fn)


def _pair_share(halves, by_cols):
    nt = len(halves)

    def part(ref, ti, core):
        axis = 2 if by_cols[ti] else 1
        half = halves[ti].shape[axis] // 2
        piece = pl.ds(pl.multiple_of(core * half, 128 if by_cols[ti] else 8), half)
        return ref.at[:, :, piece] if by_cols[ti] else ref.at[:, piece, :]

    def body(*refs):
        outs = refs[nt:2 * nt]
        send, recv = refs[2 * nt:]
        x, y, c, _ = _mesh_position()
        cps = []
        for ti in range(nt):
            mine = part(outs[ti], ti, c)
            cp = pltpu.make_async_remote_copy(mine, mine, send.at[ti], recv.at[ti],
                                              device_id=(x, y, 1 - c), device_id_type=MESH)
            cp.start()
            cps.append(cp)
        for ti in range(nt):
            theirs = part(outs[ti], ti, 1 - c)
            pltpu.make_async_remote_copy(theirs, theirs, send.at[ti], recv.at[ti],
                                         device_id=(x, y, 1 - c), device_id_type=MESH).wait_recv()
        for cp in cps:
            cp.wait_send()

    return pl.pallas_call(
        body, name="grads_pair_share",
        in_specs=[ANYSPEC] * nt, out_specs=[ANYSPEC] * nt,
        out_shape=[_sds(a.shape, a.dtype) for a in halves],
        scratch_shapes=[pltpu.SemaphoreType.DMA((nt,)), pltpu.SemaphoreType.DMA((nt,))],
        input_output_aliases={i: i for i in range(nt)},
        compiler_params=_params(),
    )(*halves)


def _adamw_math(w, g, m, v):
    m = B1 * m + (1.0 - B1) * g
    v = B2 * v + (1.0 - B2) * (g * g)
    delta = -LR * ((m / BC1) / (jnp.sqrt(v / BC2) + AEPS) + WD * w)
    return delta, m, v


def _adamw(name, rb, w, g, m, v):
    n_layers, r, c = w.shape

    def body(w_ref, g_ref, m_ref, v_ref, go_ref, d_ref, nm_ref, nv_ref):
        g_v = g_ref[...]
        go_ref[...] = g_v
        d_ref[...], nm_ref[...], nv_ref[...] = _adamw_math(w_ref[...], g_v, m_ref[...], v_ref[...])

    spec = pl.BlockSpec((None, rb, c), lambda l, j: (l, j, 0))
    return pl.pallas_call(
        body, name=f"adamw_{name}", grid=(n_layers, r // rb),
        in_specs=[spec] * 4, out_specs=[spec] * 4, out_shape=[_sds(w.shape)] * 4,
        compiler_params=_params(2),
    )(w, g, m, v)


GAIN_ROWS = {"pre_mix_g": 0, "post_mix_g": 2, "pre_ffn_g": 4, "post_ffn_g": 6, "ple_g": 8, "ple_post_g": 10}
ROW_KV_G, ROW_POOL_SCALE, ROW_SINKS, ROW_LOSS, PACK_ROWS = 12, 13, 14, 15, 16
SMALL_NAMES = tuple(GAIN_ROWS) + ("kv_g", "pool_scale", "sinks")


def _small_all_reduce(rows, dpool, rider=None):
    ng, pr = len(WINDOWS), POOL_G // N_CHIPS

    def body(*refs):
        row_refs = refs[:PACK_ROWS]
        dpool_ref, tot_ref, gpool_ref, pack, land, pland, send, recv, psend, precv = refs[PACK_ROWS:]
        x, y, c, _ = _mesh_position()
        me = 4 * x + 2 * y + c
        for r in range(PACK_ROWS):
            pack[r:r + 1, :] = row_refs[r][...]

        def shard_of(k):
            return dpool_ref.at[:, pl.ds(pl.multiple_of(k * pr, pr), pr), :]

        cps = []
        for j in range(1, N_DEV):
            px, py, pc = x ^ (j >> 2), y ^ ((j >> 1) & 1), c ^ (j & 1)
            cps.append(pltpu.make_async_remote_copy(pack, land.at[me], send.at[j], recv.at[j],
                                                    device_id=(px, py, pc), device_id_type=MESH))
            cps.append(pltpu.make_async_remote_copy(shard_of(2 * px + py), pland.at[me], psend.at[j], precv.at[j],
                                                    device_id=(px, py, pc), device_id_type=MESH))
        for cp in cps:
            cp.start()
        land[me] = pack[...]
        pland[me] = dpool_ref[:, pl.ds(pl.multiple_of((2 * x + y) * pr, pr), pr), :]
        for j in range(1, N_DEV):
            pltpu.make_async_remote_copy(pack, land.at[me ^ j], send.at[j], recv.at[j],
                                         device_id=(x, y, c), device_id_type=MESH).wait_recv()
            pltpu.make_async_remote_copy(shard_of(0), pland.at[me ^ j], psend.at[j], precv.at[j],
                                         device_id=(x, y, c), device_id_type=MESH).wait_recv()
        for cp in cps:
            cp.wait_send()
        tot = land[0]
        gp = pland[0].astype(F32)
        for d in range(1, N_DEV):
            tot = tot + land[d]
            gp = gp + pland[d].astype(F32)
        tot_ref[...] = tot
        gpool_ref[...] = gp

    sems = pltpu.SemaphoreType.DMA((N_DEV,))
    return _call(
        body, name="small_all_reduce", grid=(1,),
        in_specs=[VSPEC] * (PACK_ROWS + 1), out_specs=[VSPEC, VSPEC],
        out_shape=[_sds((PACK_ROWS, D)), _sds((ng, pr, POOL_G))],
        scratch_shapes=[pltpu.VMEM((PACK_ROWS, D), F32), pltpu.VMEM((N_DEV, PACK_ROWS, D), F32),
                        pltpu.VMEM((N_DEV, ng, pr, POOL_G), BF), sems, sems, sems, sems],
        args=[*rows, dpool], rider=rider)


def _small_adamw(tot, kc, small_w, small_m, small_v):
    names = SMALL_NAMES
    n = len(names)

    def body(*refs):
        tot_ref, kc_ref = refs[0], refs[1]
        w_refs = dict(zip(names, refs[2:2 + n]))
        m_refs = dict(zip(names, refs[2 + n:2 + 2 * n]))
        v_refs = dict(zip(names, refs[2 + 2 * n:2 + 3 * n]))
        loss_ref = refs[2 + 3 * n]
        out_refs = {nm: refs[3 + 3 * n + 4 * k: 7 + 3 * n + 4 * k] for k, nm in enumerate(names)}
        tot = tot_ref[...]
        loss_ref[...] = 0.5 * jnp.sum(tot[ROW_LOSS:ROW_LOSS + 1, :], axis=-1, keepdims=True) * (1.0 / D)

        def update(nm, g):
            g_ref, d_ref, nm_ref, nv_ref = out_refs[nm]
            g_ref[...] = g
            d_ref[...], nm_ref[...], nv_ref[...] = _adamw_math(w_refs[nm][...], g, m_refs[nm][...], v_refs[nm][...])

        for nm, r in GAIN_ROWS.items():
            update(nm, tot[r:r + 2, :])
        update("kv_g", tot[ROW_KV_G:ROW_KV_G + 1, :])
        k = kc_ref[0]
        width = D // N_CHIPS
        g_scale = jnp.zeros((1, width), F32)
        for kk in range(N_CHIPS):
            g_scale = g_scale + jnp.where(k == kk, tot[ROW_POOL_SCALE:ROW_POOL_SCALE + 1, kk * width:(kk + 1) * width], 0.0)
        update("pool_scale", g_scale)
        update("sinks", tot[ROW_SINKS:ROW_SINKS + 1, 0:N_HEADS])

    ins = [tot, kc] + [small_w[nm] for nm in names] + [small_m[nm] for nm in names] + [small_v[nm] for nm in names]
    out_shape = [_sds((1, 1))]
    for nm in names:
        out_shape += [_sds(small_w[nm].shape)] * 4
    outs = pl.pallas_call(
        body, name="small_adamw",
        in_specs=[VSPEC, SSPEC] + [VSPEC] * (3 * n), out_specs=[VSPEC] * len(out_shape), out_shape=out_shape,
        compiler_params=_params(),
    )(*ins)
    return outs[0], {nm: outs[1 + 4 * k: 5 + 4 * k] for k, nm in enumerate(names)}


def _compute_layout(t, full):
    if t.src == "w_gu":
        return full.reshape(2, D, FF)
    if t.src == "pool_w":
        return full.reshape(len(WINDOWS), POOL_G, POOL_G)
    if t.src == "pool_scale":
        return full.reshape(1, D)
    return full.reshape(t.A * t.R, _ncb(t) * t.C)


def kernel(x, p, pre_mix_g, post_mix_g, pre_ffn_g, post_ffn_g, pool_w, pool_scale, kv_g, w_kv, w_q, sinks, w_o, w_gu, w_down, ple_g, w_ple_gate, w_ple_proj, ple_post_g, loss_target, m_pre_mix_g, m_post_mix_g, m_pre_ffn_g, m_post_ffn_g, m_pool_w, m_pool_scale, m_kv_g, m_w_kv, m_w_q, m_sinks, m_w_o, m_w_gu, m_w_down, m_ple_g, m_w_ple_gate, m_w_ple_proj, m_ple_post_g, v_pre_mix_g, v_post_mix_g, v_pre_ffn_g, v_post_ffn_g, v_pool_w, v_pool_scale, v_kv_g, v_w_kv, v_w_q, v_sinks, v_w_o, v_w_gu, v_w_down, v_ple_g, v_w_ple_gate, v_w_ple_proj, v_ple_post_g):
    weights = dict(pre_mix_g=pre_mix_g, post_mix_g=post_mix_g, pre_ffn_g=pre_ffn_g, post_ffn_g=post_ffn_g,
                   pool_w=pool_w, pool_scale=pool_scale, kv_g=kv_g, w_kv=w_kv, w_q=w_q, sinks=sinks, w_o=w_o,
                   w_gu=w_gu, w_down=w_down, ple_g=ple_g, w_ple_gate=w_ple_gate, w_ple_proj=w_ple_proj,
                   ple_post_g=ple_post_g)
    m_in = dict(pre_mix_g=m_pre_mix_g, post_mix_g=m_post_mix_g, pre_ffn_g=m_pre_ffn_g, post_ffn_g=m_post_ffn_g,
                pool_w=m_pool_w, pool_scale=m_pool_scale, kv_g=m_kv_g, w_kv=m_w_kv, w_q=m_w_q, sinks=m_sinks,
                w_o=m_w_o, w_gu=m_w_gu, w_down=m_w_down, ple_g=m_ple_g, w_ple_gate=m_w_ple_gate,
                w_ple_proj=m_w_ple_proj, ple_post_g=m_ple_post_g)
    v_in = dict(pre_mix_g=v_pre_mix_g, post_mix_g=v_post_mix_g, pre_ffn_g=v_pre_ffn_g, post_ffn_g=v_post_ffn_g,
                pool_w=v_pool_w, pool_scale=v_pool_scale, kv_g=v_kv_g, w_kv=v_w_kv, w_q=v_w_q, sinks=v_sinks,
                w_o=v_w_o, w_gu=v_w_gu, w_down=v_w_down, ple_g=v_ple_g, w_ple_gate=v_w_ple_gate,
                w_ple_proj=v_w_ple_proj, ple_post_g=v_ple_post_g)
    order = ["pre_mix_g", "post_mix_g", "pre_ffn_g", "post_ffn_g", "pool_w", "pool_scale", "kv_g", "w_kv", "w_q",
             "sinks", "w_o", "w_gu", "w_down", "ple_g", "w_ple_gate", "w_ple_proj", "ple_post_g"]

    kc = jnp.stack([2 * lax.axis_index("x") + lax.axis_index("y"), lax.axis_index("c")]).astype(jnp.int32)
    s_len = x.shape[1]
    x2d = x.reshape(s_len, D)
    p3d = p.reshape(2, s_len, PLE)
    target = loss_target.reshape(s_len, D)
    kv_g2d = kv_g.reshape(1, D)
    gains = {nm: weights[nm] for nm in GAIN_ROWS}

    def shard_view(src, a):
        t = next(t for t in BIGS.values() if t.src == src)
        return a.reshape(-1, t.R, t.C)

    first, second = ["pool_w", "pool_scale"], ["w_gu0", "w_down0"]
    rest = [nm for nm in BIGS if nm not in first + second]
    specs = dict(BIGS, pool_scale=POOL_SCALE)
    placed = {nm: _place(BIGS[nm], shard_view(BIGS[nm].src, weights[BIGS[nm].src]), kc, BF)
              for nm in first + second if nm in BIGS}
    placed["pool_scale"] = _place(POOL_SCALE, pool_scale.reshape(1, 1, D // N_CHIPS), kc, F32)

    def gather(names, rows=None):
        rows = rows or {}
        parts = [(specs[nm],) + tuple(rows.get(nm, (0, specs[nm].R))) for nm in names]
        return _gather_rider(parts, [placed[nm] for nm in names])

    def take(names, results):
        for nm, a in zip(names, results):
            placed[nm] = a

    def weight(nm):
        return _compute_layout(specs[nm], placed[nm])

    cast, got = _place_many([BIGS[nm] for nm in rest], [shard_view(BIGS[nm].src, weights[BIGS[nm].src]) for nm in rest],
                            kc, rider=gather(first))
    take(rest, cast)
    take(first, got)


    (y0, x1), got = _mixa_fwd(x2d, gains["pre_mix_g"], weight("pool_w"), weight("pool_scale"), gains["post_mix_g"],
                              rider=gather(second))
    take(second, got)

    ride = ["w_ple_gate0", "w_ple_proj0", "w_q", "w_kv", "w_o", "w_gu1"]
    (f0, x2), got = _ffn_fwd(0, x1, gains["pre_ffn_g"], weight("w_gu0"), weight("w_down0"), gains["post_ffn_g"],
                             rider=gather(ride, {"w_gu1": (0, 320)}))
    take(ride, got)

    ride = ["w_ple_gate1", "w_ple_proj1", "w_gu1"]
    (z0, pe0, x3, q, kv), got = _ple_fwd(
        0, x2, p3d, gains["ple_g"], weight("w_ple_gate0"), weight("w_ple_proj0"), gains["ple_post_g"],
        qkv=(gains["pre_mix_g"], kv_g2d, weight("w_q"), weight("w_kv")),
        rider=gather(ride, {"w_gu1": (320, 704)}))
    take(ride, got)

    ride = ["w_down1", "w_gu1"]
    (attn, y1, x4), got = _attn_fwd(q, kv, sinks, x3, weight("w_o"), gains["post_mix_g"],
                                    rider=gather(ride, {"w_gu1": (704, D)}))
    take(ride, got)

    (f1, x5), _ = _ffn_fwd(1, x4, gains["pre_ffn_g"], weight("w_gu1"), weight("w_down1"), gains["post_ffn_g"])
    (z1, pe1, dx6, loss_row), _ = _ple_fwd(1, x5, p3d, gains["ple_g"], weight("w_ple_gate1"), weight("w_ple_proj1"),
                                           gains["ple_post_g"], target=target)

    local = {}
    landed = {}
    fused = {}

    def pair_stage(tag, names):
        ts = [BIGS[nm] for nm in names]
        gs = [local[nm].reshape(_full_shape(t)) for nm, t in zip(names, ts)]
        lands = _pair_exchange(f"grads_pair_exchange_{tag}", ts, gs)
        jobs = [_pair_sum_job(t, g, l) for t, g, l in zip(ts, gs, lands)]
        return [r[0] for r in _multi_call(f"pair_sum_{tag}", jobs, kc)]

    def scatter(names, sums):
        return _scatter_rider([BIGS[nm] for nm in names], sums)

    def keep(names, sums, got):
        for nm, s, l in zip(names, sums, got):
            landed[nm] = (s, l)

    (dx5, local["w_ple_gate1"], local["w_ple_proj1"], d_ple1, d_plepost1), _ = _ple_bwd(
        1, dx6, x5, z1, pe1, p3d, gains["ple_g"], weight("w_ple_gate1"), gains["ple_post_g"])

    group_a = ["w_ple_gate1", "w_ple_proj1"]
    sums_a = pair_stage("a", group_a)
    (dx4, d_preffn1, d_postffn1, *scattered), _ = _ffn_bwd(
        1, dx5, x4, f1, gains["pre_ffn_g"], weight("w_gu1"), weight("w_down1"), gains["post_ffn_g"], kc)
    fused["w_gu1"], fused["w_down1"] = scattered[0:2], scattered[2:4]

    (dq, dkv, local["w_o"], d_postmix1, d_sinks), got = _attn_bwd(
        dx4, y1, attn, q, kv, sinks, weight("w_o"), gains["post_mix_g"], rider=scatter(group_a, sums_a))
    keep(group_a, sums_a, got)
    dx3, local["w_q"], local["w_kv"], d_premix1, d_kvg = _qkv_bwd(
        dq, dkv, x3, dx4, gains["pre_mix_g"], kv_g2d, weight("w_q"), weight("w_kv"))

    group_b = ["w_o", "w_q", "w_kv"]
    sums_b = pair_stage("b", group_b)
    (dx2, local["w_ple_gate0"], local["w_ple_proj0"], d_ple0, d_plepost0), got = _ple_bwd(
        0, dx3, x2, z0, pe0, p3d, gains["ple_g"], weight("w_ple_gate0"), gains["ple_post_g"],
        rider=scatter(group_b, sums_b))
    keep(group_b, sums_b, got)

    group_c = ["w_ple_gate0", "w_ple_proj0"]
    sums_c = pair_stage("c", group_c)
    (dx1, d_preffn0, d_postffn0, *scattered), _ = _ffn_bwd(
        0, dx2, x1, f0, gains["pre_ffn_g"], weight("w_gu0"), weight("w_down0"), gains["post_ffn_g"], kc)
    fused["w_gu0"], fused["w_down0"] = scattered[0:2], scattered[2:4]

    (dx0, d_pool, d_scale, d_postmix0, d_premix0), _ = _mixa_bwd(
        dx1, x2d, y0, gains["pre_mix_g"], weight("pool_w"), weight("pool_scale"), gains["post_mix_g"])

    rows = [d_premix0, d_premix1, d_postmix0, d_postmix1, d_preffn0, d_preffn1, d_postffn0, d_postffn1,
            d_ple0, d_ple1, d_plepost0, d_plepost1, d_kvg, d_scale, d_sinks, loss_row]
    as2d = lambda a: a.reshape(1, D) if a.ndim == 1 else a
    (tot, g_pool), got = _small_all_reduce(rows, d_pool, rider=scatter(group_c, sums_c))
    keep(group_c, sums_c, got)
    loss, small = _small_adamw(tot, kc, {nm: as2d(weights[nm]) for nm in SMALL_NAMES},
                               {nm: as2d(m_in[nm]) for nm in SMALL_NAMES},
                               {nm: as2d(v_in[nm]) for nm in SMALL_NAMES})

    layers_of = lambda src: [t for t in BIGS.values() if t.src == src]
    own_scatter = ["w_gu", "w_down"]
    others = [src for src in BIG_SOURCES if src not in own_scatter and src != "pool_w"]
    shared = own_scatter + others
    jobs = [_chip_sum_fused_job(layers_of(src), fused, by_cols=src == "w_down") for src in own_scatter]
    jobs += [_chip_sum_job(layers_of(src), landed) for src in others]
    halves = {src: r[0] for src, r in zip(shared, _multi_call("chip_sum", jobs, kc))}
    full_grads = dict(zip(shared, _pair_share([halves[src] for src in shared], [src == "w_down" for src in shared])))
    full_grads["pool_w"] = g_pool

    def adam_args(src):
        return (layers_of(src)[0].rb, shard_view(src, weights[src]), full_grads[src],
                shard_view(src, m_in[src]), shard_view(src, v_in[src]))

    out = {"grad": {}, "delta": {}, "new_m": {}, "new_v": {}}
    results = {src: _adamw(src, *adam_args(src)) for src in own_scatter}
    rest_srcs = others + ["pool_w"]
    results.update(zip(rest_srcs, _multi_call("adamw_rest", [_adamw_job(*adam_args(src)) for src in rest_srcs], kc)))
    for src in BIG_SOURCES:
        shape = weights[src].shape
        for kind, a in zip(("grad", "delta", "new_m", "new_v"), results[src]):
            out[kind][src] = a.reshape(shape)
    for nm in SMALL_NAMES:
        shape = weights[nm].shape
        for kind, a in zip(("grad", "delta", "new_m", "new_v"), small[nm]):
            out[kind][nm] = a.reshape(shape)

    return (loss.reshape(()), dx0.reshape(x.shape),
            *[out["grad"][nm] for nm in order], *[out["delta"][nm] for nm in order],
            *[out["new_m"][nm] for nm in order], *[out["new_v"][nm] for nm in order])
```

```python
import collections

import jax
import jax.numpy as jnp
from jax import lax
from jax.experimental import pallas as pl
from jax.experimental.pallas import tpu as pltpu

D = 1024
FF = 2816
N_HEADS = 16
HEAD_DIM = 64
N_KV_HEADS = 4
GQA = N_HEADS // N_KV_HEADS
KVD = N_KV_HEADS * HEAD_DIM
PLE = 256
BLK = 128
WINDOWS = (2, 4, 8, 16)
POOL_G = 256
HALO = 16
EPS = 1e-6
NEG_INF = -1e30
ATT_SCALE = HEAD_DIM ** -0.5
SLOPES = tuple(2.0 ** (-8.0 * (h + 1) / N_HEADS) for h in range(N_HEADS))
N_CHIPS = 4
N_DEV = 8

LR, B1, B2, AEPS, WD, STEP = 0.001, 0.9, 0.999, 1e-08, 0.01, 10
BC1 = 1.0 - B1 ** STEP
BC2 = 1.0 - B2 ** STEP

BF = jnp.bfloat16
F32 = jnp.float32
MESH = pl.DeviceIdType.MESH
VMEM_LIMIT_V7X = 58 * 1024 * 1024
TM = 256
TM_FFN_BWD = 512
FF_CHUNK = 256
FF_HALF = FF // 2

VSPEC = pl.BlockSpec(memory_space=pltpu.VMEM)
SSPEC = pl.BlockSpec(memory_space=pltpu.SMEM)
ANYSPEC = pl.BlockSpec(memory_space=pl.ANY)


def _params(n_grid=0):
    sem = ("arbitrary",) * n_grid if n_grid else None
    return pltpu.CompilerParams(dimension_semantics=sem, vmem_limit_bytes=VMEM_LIMIT_V7X)


def _sds(shape, dtype=F32):
    return jax.ShapeDtypeStruct(tuple(shape), dtype)


Rider = collections.namedtuple("Rider", "arrays out_shapes aliases scratch start mid finish")
MID_NUM, MID_DEN = 5, 8


def _call(body, *, name, grid, in_specs, out_specs, out_shape, args, scratch_shapes=(), rider=None, prefetch=None):
    ni, no, ns = len(in_specs), len(out_specs), len(scratch_shapes)
    npre = 0 if prefetch is None else 1
    pre = [] if prefetch is None else [prefetch]
    if rider is None:
        rider = Rider([], [], {}, [], None, None, None)
    ri, ro = len(rider.arrays), len(rider.out_shapes)

    def full(*refs):
        pre_refs, refs = refs[:npre], refs[npre:]
        ins, refs = refs[:ni], refs[ni:]
        rins, refs = refs[:ri], refs[ri:]
        outs, refs = refs[:no], refs[no:]
        routs, refs = refs[:ro], refs[ro:]
        scr, rscr = refs[:ns], refs[ns:]
        ids = [pl.program_id(a) for a in range(len(grid))]
        first = ids[0] == 0
        last = ids[0] == grid[0] - 1
        for a in range(1, len(grid)):
            first = first & (ids[a] == 0)
            last = last & (ids[a] == grid[a] - 1)

        if rider.start is not None:
            @pl.when(first)
            def _():
                rider.start(rins, routs, rscr)

        if rider.mid is not None:
            assert len(grid) == 1

            @pl.when(ids[0] == (grid[0] * MID_NUM) // MID_DEN)
            def _():
                rider.mid(rins, routs, rscr)

        body(*pre_refs, *ins, *outs, *scr)

        if rider.finish is not None:
            @pl.when(last)
            def _():
                rider.finish(rins, routs, rscr)

    outs = pl.pallas_call(
        full, name=name,
        grid_spec=pltpu.PrefetchScalarGridSpec(
            num_scalar_prefetch=npre, grid=grid,
            in_specs=list(in_specs) + [ANYSPEC] * ri, out_specs=list(out_specs) + [ANYSPEC] * ro,
            scratch_shapes=list(scratch_shapes) + list(rider.scratch)),
        out_shape=list(out_shape) + list(rider.out_shapes),
        input_output_aliases={npre + ni + a: no + b for a, b in rider.aliases.items()},
        compiler_params=_params(len(grid)))(*pre, *args, *rider.arrays)
    return list(outs[:no]), list(outs[no:])


def _run(name, rider):
    ri = len(rider.arrays)

    def body(*refs):
        rins, routs, rscr = refs[:ri], refs[ri:ri + len(rider.out_shapes)], refs[ri + len(rider.out_shapes):]
        rider.start(rins, routs, rscr)
        if rider.mid is not None:
            rider.mid(rins, routs, rscr)
        rider.finish(rins, routs, rscr)

    return pl.pallas_call(
        body, name=name, in_specs=[ANYSPEC] * ri, out_specs=[ANYSPEC] * len(rider.out_shapes),
        out_shape=list(rider.out_shapes), scratch_shapes=list(rider.scratch),
        input_output_aliases=dict(rider.aliases), compiler_params=_params())(*rider.arrays)


Job = collections.namedtuple("Job", "steps ins outs fn")


def _multi_call(name, jobs, kc):
    n = max(job.steps for job in jobs)

    def clamped(index, steps):
        return lambda s, kc_ref: index(jnp.minimum(s, steps - 1), kc_ref)

    in_specs, out_specs, out_shape, args = [], [], [], []
    for job in jobs:
        for arr, block, index in job.ins:
            in_specs.append(pl.BlockSpec(block, clamped(index, job.steps)))
            args.append(arr)
        for sds, block, index in job.outs:
            out_specs.append(pl.BlockSpec(block, clamped(index, job.steps)))
            out_shape.append(sds)
    n_in = len(args)

    def body(kc_ref, *refs):
        s = pl.program_id(0)
        i0, o0 = 0, n_in
        for job in jobs:
            ins, outs = refs[i0:i0 + len(job.ins)], refs[o0:o0 + len(job.outs)]
            i0, o0 = i0 + len(job.ins), o0 + len(job.outs)

            @pl.when(s < job.steps)
            def _():
                job.fn(s, kc_ref, ins, outs)

    outs, _ = _call(body, name=name, grid=(n,), in_specs=in_specs, out_specs=out_specs, out_shape=out_shape,
                    args=args, prefetch=kc)
    res, o0 = [], 0
    for job in jobs:
        res.append(outs[o0:o0 + len(job.outs)])
        o0 += len(job.outs)
    return res


def _rms_fwd(x, g):
    r = lax.rsqrt(jnp.mean(x * x, axis=-1, keepdims=True) + EPS)
    return x * r * g


def _rms_bwd(x, g, dy):
    r = lax.rsqrt(jnp.mean(x * x, axis=-1, keepdims=True) + EPS)
    xn = x * r
    dxn = dy * g
    dx = r * (dxn - xn * jnp.mean(dxn * xn, axis=-1, keepdims=True))
    return dx, dy * xn


def _rowsum(a):
    return jnp.sum(a, axis=0, keepdims=True)


def _sigmoid(z):
    return 1.0 / (1.0 + jnp.exp(-z))


def _dot(a, b):
    return jnp.dot(a, b, preferred_element_type=F32)


def _dot_nt(a, b):
    return lax.dot_general(a, b, (((1,), (1,)), ((), ())), preferred_element_type=F32)


def _dot_tn(a, b):
    return lax.dot_general(a, b, (((0,), (0,)), ((), ())), preferred_element_type=F32)


def _row_spec(tm, width=D):
    return pl.BlockSpec((tm, width), lambda i: (i, 0))


def _const_spec(shape):
    zeros = (0,) * len(shape)
    return pl.BlockSpec(tuple(shape), lambda *_: zeros)


def _pool_delta(he, pos):
    out = []
    for gi, w in enumerate(WINDOWS):
        hg = he[:, gi * POOL_G:(gi + 1) * POOL_G]
        s = hg
        k = 1
        while k < w:
            s = s + pltpu.roll(s, k, 0)
            k *= 2
        cnt = jnp.maximum(jnp.minimum(pos + 1, w), 1).astype(F32)
        out.append(s / cnt - hg)
    return out


def _load_with_halo_before(x_ref, i, tm):
    r0 = pl.multiple_of(i * tm, tm)
    hs = pl.multiple_of(jnp.maximum(i * tm - HALO, 0), 8)
    xh = jnp.where(i > 0, x_ref[pl.ds(hs, HALO), :], 0.0)
    xt = x_ref[pl.ds(r0, tm), :]
    return xt, jnp.concatenate([xh, xt], axis=0)


def _mixa_fwd(x, pre_g, pool_w, pool_scale, post_g, rider=None):
    s_len = x.shape[0]
    n = s_len // TM

    def body(x_ref, pg_ref, w_ref, sc_ref, qg_ref, y_ref, x1_ref):
        i = pl.program_id(0)
        xt, xe = _load_with_halo_before(x_ref, i, TM)
        he = _rms_fwd(xe, pg_ref[0:1, :])
        pos = i * TM - HALO + lax.broadcasted_iota(jnp.int32, (TM + HALO, 1), 0)
        ds = _pool_delta(he, pos)
        ys = [_dot(ds[gi][HALO:, :].astype(BF), w_ref[gi]) for gi in range(len(WINDOWS))]
        y = jnp.concatenate(ys, axis=1) * sc_ref[...]
        y_ref[...] = y
        x1_ref[...] = xt + _rms_fwd(y, qg_ref[0:1, :])

    return _call(body, name="mixa_fwd", grid=(n,),
                 in_specs=[VSPEC] * 5, out_specs=[_row_spec(TM), _row_spec(TM)],
                 out_shape=[_sds((s_len, D)), _sds((s_len, D))],
                 args=[x, pre_g, pool_w, pool_scale, post_g], rider=rider)


def _mixa_bwd(dx1, x, y, pre_g, pool_w, pool_scale, post_g, rider=None):
    s_len = x.shape[0]
    n = s_len // TM
    ng = len(WINDOWS)

    def body(dx_ref, x_ref, y_ref, pg_ref, w_ref, sc_ref, qg_ref,
             dx0_ref, dw_ref, dsc_ref, dqg_ref, dpg_ref, wacc):
        i = pl.program_id(0)

        @pl.when(i == 0)
        def _():
            wacc[...] = jnp.zeros_like(wacc)
            dsc_ref[...] = jnp.zeros_like(dsc_ref)
            dqg_ref[...] = jnp.zeros_like(dqg_ref)
            dpg_ref[...] = jnp.zeros_like(dpg_ref)

        r0 = pl.multiple_of(i * TM, TM)
        xt, xe = _load_with_halo_before(x_ref, i, TM)
        he = _rms_fwd(xe, pg_ref[0:1, :])
        pos_b = i * TM - HALO + lax.broadcasted_iota(jnp.int32, (TM + HALO, 1), 0)
        ds = _pool_delta(he, pos_b)

        last = i == n - 1
        a0 = pl.multiple_of(jnp.minimum(i * TM + TM, s_len - HALO), 8)
        ye = jnp.concatenate([y_ref[pl.ds(r0, TM), :], y_ref[pl.ds(a0, HALO), :]], axis=0)
        dt = dx_ref[pl.ds(r0, TM), :]
        de = jnp.concatenate([dt, jnp.where(last, 0.0, dx_ref[pl.ds(a0, HALO), :])], axis=0)
        dye, prod = _rms_bwd(ye, qg_ref[0:1, :], de)
        dqg_ref[...] += _rowsum(prod[:TM, :])
        dys = dye * sc_ref[...]
        pos_a = i * TM + lax.broadcasted_iota(jnp.int32, (TM + HALO, 1), 0)

        dhs, dscs = [], []
        for gi, w in enumerate(WINDOWS):
            sl = slice(gi * POOL_G, (gi + 1) * POOL_G)
            wg = w_ref[gi]
            dys_g = dys[:, sl].astype(BF)
            d_g = ds[gi][HALO:, :].astype(BF)
            ypre = _dot(d_g, wg)
            dscs.append(_rowsum(dye[:TM, sl] * ypre))
            wacc[gi] += _dot_tn(d_g, dys_g[:TM, :])
            dd = _dot_nt(dys_g, wg)
            cnt = jnp.minimum(pos_a + 1, w).astype(F32)
            a = dd / cnt
            k = 1
            while k < w:
                a = a + pltpu.roll(a, TM + HALO - k, 0)
                k *= 2
            dhs.append(a[:TM, :] - dd[:TM, :])
        dsc_ref[...] += jnp.concatenate(dscs, axis=1)
        dh = jnp.concatenate(dhs, axis=1)
        dxp, prod2 = _rms_bwd(xt, pg_ref[0:1, :], dh)
        dpg_ref[...] += _rowsum(prod2)
        dx0_ref[...] = dt + dxp

        @pl.when(last)
        def _():
            dw_ref[...] = wacc[...].astype(BF)

    return _call(
        body, name="mixa_bwd", grid=(n,), in_specs=[VSPEC] * 7,
        out_specs=[_row_spec(TM), _const_spec((ng, POOL_G, POOL_G)), _const_spec((1, D)),
                   _const_spec((1, D)), _const_spec((1, D))],
        out_shape=[_sds((s_len, D)), _sds((ng, POOL_G, POOL_G), BF), _sds((1, D)), _sds((1, D)), _sds((1, D))],
        scratch_shapes=[pltpu.VMEM((ng, POOL_G, POOL_G), F32)],
        args=[dx1, x, y, pre_g, pool_w, pool_scale, post_g], rider=rider)


def _ffn_fwd(layer, x1, pre_g, wgu, wd, post_g, rider=None):
    s_len = x1.shape[0]

    def body(x_ref, pg_ref, wgu_ref, wd_ref, qg_ref, f_ref, x2_ref, g_ref, u_ref):
        x = x_ref[...]
        h = _rms_fwd(x, pg_ref[layer:layer + 1, :]).astype(BF)
        f = jnp.zeros((TM, D), F32)
        for c in range(FF // FF_HALF):
            cols = slice(c * FF_HALF, (c + 1) * FF_HALF)
            g = _dot(h, wgu_ref[0, :, cols])
            u = _dot(h, wgu_ref[1, :, cols])
            g_ref[:, cols] = g.astype(BF)
            u_ref[:, cols] = u.astype(BF)
            act = g * _sigmoid(g) * u
            f = f + _dot(act.astype(BF), wd_ref[cols, :])
        f_ref[...] = f
        x2_ref[...] = x + _rms_fwd(f, qg_ref[layer:layer + 1, :])

    return _call(body, name=f"ffn_fwd{layer}", grid=(s_len // TM,),
                 in_specs=[_row_spec(TM), VSPEC, VSPEC, VSPEC, VSPEC],
                 out_specs=[_row_spec(TM), _row_spec(TM), _row_spec(TM, FF), _row_spec(TM, FF)],
                 out_shape=[_sds((s_len, D)), _sds((s_len, D)), _sds((s_len, FF), BF), _sds((s_len, FF), BF)],
                 args=[x1, pre_g, wgu, wd, post_g], rider=rider)


GU_PIECE = 128
DN_PIECE = 64
DN_SLOT = FF // N_CHIPS
HALF_D = D // 2


def _ffn_bwd(layer, dx2, x1, f, g_pre, u_pre, pre_g, wgu, wd, post_g, kc, rider=None):
    s_len = x1.shape[0]
    tm = TM_FFN_BWD
    n = s_len // tm
    nc = FF // FF_CHUNK
    n_gu, n_dn = FF_CHUNK // GU_PIECE, FF_CHUNK // DN_PIECE
    n_pieces = 2 * n_gu + n_dn
    n_blk = FF_HALF // GU_PIECE

    def edge_rows(c, i, kc_ref):
        return (jnp.where((c == 0) | (c == nc - 1), i, n - 1), 0)

    def chunk_at(c, kc_ref):
        return (c + (((kc_ref[0] + 1) % N_CHIPS) * nc) // N_CHIPS) % nc

    def exchange(kc_ref, c, accg, accu, accd, own_gu_ref, land_gu_ref, own_dn_ref, land_dn_ref,
                 pl_gu, pl_dn, sib_gu, sib_dn, mine_gu, mine_dn, sum_gu, sum_dn,
                 psend, precv, ssend, lsem, rrecv):
        x, y, core = lax.axis_index("x"), lax.axis_index("y"), lax.axis_index("c")
        lower = core == 0

        def pair_copy(cc, part):
            p = cc % 2
            src, dst = ((sib_gu, pl_gu), (sib_dn, pl_dn))[part]
            return pltpu.make_async_remote_copy(src.at[p], dst.at[cc], psend.at[p, part], precv.at[cc, part],
                                                device_id=(x, y, 1 - core), device_id_type=MESH)

        def scatter(cc, wait):
            p = cc % 2
            jobs = []
            for gu in range(2):
                for hc in range(n_gu):
                    hidden = chunk_at(cc, kc_ref) * FF_CHUNK + hc * GU_PIECE
                    k = hidden // FF_HALF
                    jobs.append((sum_gu.at[p, gu, hc], k + 2 * gu, 0,
                                 own_gu_ref, land_gu_ref, ((hidden - k * FF_HALF) // GU_PIECE,)))
            for q in range(n_dn):
                hidden = chunk_at(cc, kc_ref) * FF_CHUNK + q * DN_PIECE
                k = hidden // DN_SLOT
                off = pl.multiple_of(hidden - k * DN_SLOT, DN_PIECE)
                jobs.append((sum_dn.at[p, pl.ds(q * DN_PIECE, DN_PIECE), :], k, 1,
                             own_dn_ref, land_dn_ref, (pl.ds(off, DN_PIECE), slice(None))))
            for pi, (src, k, t, own_ref, land_ref, where) in enumerate(jobs):
                kx, ky = k // 2, k % 2
                fx, fy = (kx != x).astype(jnp.int32), (ky != y).astype(jnp.int32)
                local = (fx + fy) == 0
                j = jnp.maximum(fx + 2 * fy - 1, 0)

                @pl.when(local)
                def _():
                    cp = pltpu.make_async_copy(src, own_ref.at[where], lsem.at[p, pi])
                    if wait:
                        cp.wait()
                    else:
                        cp.start()

                @pl.when(jnp.logical_not(local))
                def _():
                    cp = pltpu.make_async_remote_copy(src, land_ref.at[(j,) + where], ssend.at[p, pi],
                                                      rrecv.at[t, j], device_id=(kx, ky, core), device_id_type=MESH)
                    if wait:
                        cp.wait_send()
                    else:
                        cp.start()

        def add_and_scatter(cc):
            p = cc % 2
            pair_copy(cc, 0).wait_recv()
            pair_copy(cc, 1).wait_recv()
            s_gu = (mine_gu[...] + pl_gu[cc].astype(F32)).astype(BF)
            for hc in range(n_gu):
                sum_gu[p, :, hc] = s_gu[:, :, hc * GU_PIECE:(hc + 1) * GU_PIECE]
            sum_dn[p] = (mine_dn[...] + pl_dn[cc].astype(F32)).astype(BF)
            scatter(cc, wait=False)

        @pl.when(c >= 1)
        def _():
            @pl.when(c >= 3)
            def _():
                scatter(c - 3, wait=True)
            add_and_scatter(c - 1)

        @pl.when(c >= 2)
        def _():
            pair_copy(c - 2, 0).wait_send()
            pair_copy(c - 2, 1).wait_send()

        p = c % 2
        my_rows = pl.ds(pl.multiple_of(core * HALF_D, HALF_D), HALF_D)
        sib_rows = pl.ds(pl.multiple_of((1 - core) * HALF_D, HALF_D), HALF_D)
        d_v = accd[...]
        sib_gu[p, 0] = accg[sib_rows, :].astype(BF)
        sib_gu[p, 1] = accu[sib_rows, :].astype(BF)
        sib_dn[p] = jnp.where(lower, d_v[:, HALF_D:], d_v[:, :HALF_D]).astype(BF)
        mine_gu[0] = accg[my_rows, :]
        mine_gu[1] = accu[my_rows, :]
        mine_dn[...] = jnp.where(lower, d_v[:, :HALF_D], d_v[:, HALF_D:])
        pair_copy(c, 0).start()
        pair_copy(c, 1).start()

        @pl.when(c == nc - 1)
        def _():
            scatter(nc - 3, wait=True)
            add_and_scatter(nc - 1)
            for cc in (nc - 2, nc - 1):
                pair_copy(cc, 0).wait_send()
                pair_copy(cc, 1).wait_send()
                scatter(cc, wait=True)
            for t, land_ref in enumerate((land_gu_ref, land_dn_ref)):
                for j in range(N_CHIPS - 1):
                    pltpu.make_async_remote_copy(land_ref.at[j], land_ref.at[j], ssend.at[0, 0], rrecv.at[t, j],
                                                 device_id=(x, y, core), device_id_type=MESH).wait_recv()

    def body(kc_ref, dx_ref, x_ref, f_ref, gp_ref, up_ref, pg_ref, wgu_ref, wd_ref, qg_ref,
             dx1_ref, dpg_ref, dqg_ref, own_gu_ref, land_gu_ref, own_dn_ref, land_dn_ref,
             h_s, df_s, dh_s, accg, accu, accd, *comm):
        c = pl.program_id(0)
        i = pl.program_id(1)
        rows = pl.ds(pl.multiple_of(i * tm, tm), tm)
        pg = pg_ref[layer:layer + 1, :]

        @pl.when((c == 0) & (i == 0))
        def _():
            dpg_ref[...] = jnp.zeros_like(dpg_ref)
            dqg_ref[...] = jnp.zeros_like(dqg_ref)

        @pl.when(c == 0)
        def _():
            h_s[rows, :] = _rms_fwd(x_ref[...], pg).astype(BF)
            df, prod = _rms_bwd(f_ref[...], qg_ref[layer:layer + 1, :], dx_ref[...])
            df_s[rows, :] = df.astype(BF)
            dqg_ref[...] += _rowsum(prod)

        @pl.when(i == 0)
        def _():
            accg[...] = jnp.zeros_like(accg)
            accu[...] = jnp.zeros_like(accu)
            accd[...] = jnp.zeros_like(accd)

        h = h_s[rows, :]
        df = df_s[rows, :]
        wg = wgu_ref[0]
        wu = wgu_ref[1]
        g = gp_ref[...].astype(F32)
        u = up_ref[...].astype(F32)
        sg = _sigmoid(g)
        a = g * sg
        dact = _dot_nt(df, wd_ref[...])
        accd[...] += _dot_tn((a * u).astype(BF), df)
        du = (dact * a).astype(BF)
        dg = (dact * u * (sg * (1.0 + g * (1.0 - sg)))).astype(BF)
        accg[...] += _dot_tn(h, dg)
        accu[...] += _dot_tn(h, du)
        dh = _dot_nt(dg, wg) + _dot_nt(du, wu)

        @pl.when(c == 0)
        def _():
            dh_s[rows, :] = dh

        @pl.when((c > 0) & (c < nc - 1))
        def _():
            dh_s[rows, :] += dh

        @pl.when(c == nc - 1)
        def _():
            dxp, prod = _rms_bwd(x_ref[...], pg, dh_s[rows, :] + dh)
            dpg_ref[...] += _rowsum(prod)
            dx1_ref[...] = dx_ref[...] + dxp

        @pl.when(i == n - 1)
        def _():
            exchange(kc_ref, c, accg, accu, accd, own_gu_ref, land_gu_ref, own_dn_ref, land_dn_ref, *comm)

    dma = pltpu.SemaphoreType.DMA
    return _call(
        body, name=f"ffn_bwd{layer}", grid=(nc, n),
        in_specs=[pl.BlockSpec((tm, D), edge_rows), pl.BlockSpec((tm, D), edge_rows),
                  pl.BlockSpec((tm, D), lambda c, i, kc_ref: (jnp.where(c == 0, i, n - 1), 0),
                               pipeline_mode=pl.Buffered(1)),
                  pl.BlockSpec((tm, FF_CHUNK), lambda c, i, kc_ref: (i, chunk_at(c, kc_ref))),
                  pl.BlockSpec((tm, FF_CHUNK), lambda c, i, kc_ref: (i, chunk_at(c, kc_ref))),
                  VSPEC,
                  pl.BlockSpec((2, D, FF_CHUNK), lambda c, i, kc_ref: (0, 0, chunk_at(c, kc_ref))),
                  pl.BlockSpec((FF_CHUNK, D), lambda c, i, kc_ref: (chunk_at(c, kc_ref), 0)),
                  VSPEC],
        out_specs=[pl.BlockSpec((tm, D), lambda c, i, kc_ref: (jnp.where(c == nc - 1, i, 0), 0)),
                   _const_spec((1, D)), _const_spec((1, D)), ANYSPEC, ANYSPEC, ANYSPEC, ANYSPEC],
        out_shape=[_sds((s_len, D)), _sds((1, D)), _sds((1, D)),
                   _sds((n_blk, HALF_D, GU_PIECE), BF), _sds((N_CHIPS - 1, n_blk, HALF_D, GU_PIECE), BF),
                   _sds((DN_SLOT, HALF_D), BF), _sds((N_CHIPS - 1, DN_SLOT, HALF_D), BF)],
        scratch_shapes=[pltpu.VMEM((s_len, D), BF), pltpu.VMEM((s_len, D), BF), pltpu.VMEM((s_len, D), F32),
                        pltpu.VMEM((D, FF_CHUNK), F32), pltpu.VMEM((D, FF_CHUNK), F32),
                        pltpu.VMEM((FF_CHUNK, D), F32),
                        pltpu.VMEM((nc, 2, HALF_D, FF_CHUNK), BF), pltpu.VMEM((nc, FF_CHUNK, HALF_D), BF),
                        pltpu.VMEM((2, 2, HALF_D, FF_CHUNK), BF), pltpu.VMEM((2, FF_CHUNK, HALF_D), BF),
                        pltpu.VMEM((2, HALF_D, FF_CHUNK), F32), pltpu.VMEM((FF_CHUNK, HALF_D), F32),
                        pltpu.VMEM((2, 2, n_gu, HALF_D, GU_PIECE), BF), pltpu.VMEM((2, FF_CHUNK, HALF_D), BF),
                        dma((2, 2)), dma((nc, 2)), dma((2, n_pieces)), dma((2, n_pieces)), dma((2, N_CHIPS - 1))],
        args=[dx2, x1, f, g_pre, u_pre, pre_g, wgu, wd, post_g], rider=rider, prefetch=kc)


def _ple_fwd(layer, x2, p, ple_g, w_gate, w_proj, post_g, target=None, qkv=None, rider=None):
    s_len = x2.shape[0]
    final = target is not None
    assert not (final and qkv)

    def body(*refs):
        if final:
            x_ref, p_ref, g_ref, wg_ref, wp_ref, qg_ref, t_ref, z_ref, pe_ref, dx_ref, lv_ref = refs
        elif qkv:
            (x_ref, p_ref, g_ref, wg_ref, wp_ref, qg_ref, ng_ref, kg_ref, wq_ref, wkv_ref,
             z_ref, pe_ref, x3_ref, q_ref, kv_ref) = refs
        else:
            x_ref, p_ref, g_ref, wg_ref, wp_ref, qg_ref, z_ref, pe_ref, x3_ref = refs
        x = x_ref[...]
        r = _rms_fwd(x, g_ref[layer:layer + 1, :]).astype(BF)
        z = _dot(r, wg_ref[...])
        pe = _dot(p_ref[...].astype(BF), wp_ref[...])
        z_ref[...] = z
        pe_ref[...] = pe
        x3 = x + _rms_fwd(pe * _sigmoid(z), qg_ref[layer:layer + 1, :])
        if final:
            @pl.when(pl.program_id(0) == 0)
            def _():
                lv_ref[...] = jnp.zeros_like(lv_ref)
            err = x3 - t_ref[...]
            dx_ref[...] = err * (1.0 / D)
            lv_ref[...] += _rowsum(err * err)
        else:
            x3_ref[...] = x3
        if qkv:
            q_ref[...] = _dot(_rms_fwd(x3, ng_ref[layer + 1:layer + 2, :]).astype(BF), wq_ref[...]).astype(BF)
            kv_ref[...] = _dot(_rms_fwd(x3, kg_ref[...]).astype(BF), wkv_ref[...]).astype(BF)

    p_spec = pl.BlockSpec((None, TM, PLE), lambda i: (layer, i, 0))
    in_specs = [_row_spec(TM), p_spec, VSPEC, VSPEC, VSPEC, VSPEC]
    args = [x2, p, ple_g, w_gate, w_proj, post_g]
    out_specs = [_row_spec(TM), _row_spec(TM), _row_spec(TM)]
    out_shape = [_sds((s_len, D))] * 3
    if qkv:
        in_specs += [VSPEC] * 4
        args += list(qkv)
        out_specs += [_row_spec(TM), _row_spec(TM, 2 * KVD)]
        out_shape += [_sds((s_len, D), BF), _sds((s_len, 2 * KVD), BF)]
    if final:
        in_specs.append(_row_spec(TM))
        args.append(target)
        out_specs.append(_const_spec((1, D)))
        out_shape.append(_sds((1, D)))
    return _call(body, name=f"ple_fwd{layer}", grid=(s_len // TM,), in_specs=in_specs, out_specs=out_specs,
                 out_shape=out_shape, args=args, rider=rider)


def _ple_bwd(layer, dx3, x2, z, pe, p, ple_g, w_gate, post_g, rider=None):
    s_len = x2.shape[0]
    n = s_len // TM

    def body(dx_ref, x_ref, z_ref, pe_ref, p_ref, g_ref, wg_ref, qg_ref,
             dx2_ref, dwg_ref, dwp_ref, dg_ref, dqg_ref, gacc, pacc):
        i = pl.program_id(0)

        @pl.when(i == 0)
        def _():
            gacc[...] = jnp.zeros_like(gacc)
            pacc[...] = jnp.zeros_like(pacc)
            dg_ref[...] = jnp.zeros_like(dg_ref)
            dqg_ref[...] = jnp.zeros_like(dqg_ref)

        dx = dx_ref[...]
        x = x_ref[...]
        pe_v = pe_ref[...]
        gate = _sigmoid(z_ref[...])
        de, prod = _rms_bwd(pe_v * gate, qg_ref[layer:layer + 1, :], dx)
        dqg_ref[...] += _rowsum(prod)
        dpe = (de * gate).astype(BF)
        dz = (de * pe_v * gate * (1.0 - gate)).astype(BF)
        pacc[...] += _dot_tn(p_ref[...].astype(BF), dpe)
        g = g_ref[layer:layer + 1, :]
        r = _rms_fwd(x, g).astype(BF)
        gacc[...] += _dot_tn(r, dz)
        dr = _dot_nt(dz, wg_ref[...])
        dxp, prod2 = _rms_bwd(x, g, dr)
        dg_ref[...] += _rowsum(prod2)
        dx2_ref[...] = dx + dxp

        @pl.when(i == n - 1)
        def _():
            dwg_ref[...] = gacc[...].astype(BF)
            dwp_ref[...] = pacc[...].astype(BF)

    p_spec = pl.BlockSpec((None, TM, PLE), lambda i: (layer, i, 0))
    return _call(
        body, name=f"ple_bwd{layer}", grid=(n,),
        in_specs=[_row_spec(TM), _row_spec(TM), _row_spec(TM), _row_spec(TM), p_spec, VSPEC, VSPEC, VSPEC],
        out_specs=[_row_spec(TM), _const_spec((D, D)), _const_spec((PLE, D)), _const_spec((1, D)), _const_spec((1, D))],
        out_shape=[_sds((s_len, D)), _sds((D, D), BF), _sds((PLE, D), BF), _sds((1, D)), _sds((1, D))],
        scratch_shapes=[pltpu.VMEM((D, D), F32), pltpu.VMEM((PLE, D), F32)],
        args=[dx3, x2, z, pe, p, ple_g, w_gate, post_g], rider=rider)


def _qkv_bwd(dq, dkv, x3, dx4, q_g, kv_g, w_q, w_kv):
    s_len = x3.shape[0]
    n = s_len // TM

    def body(dq_ref, dkv_ref, x_ref, dx_ref, qg_ref, kg_ref, wq_ref, wkv_ref,
             dx3_ref, dwq_ref, dwkv_ref, dqg_ref, dkg_ref, qacc, kacc):
        i = pl.program_id(0)

        @pl.when(i == 0)
        def _():
            qacc[...] = jnp.zeros_like(qacc)
            kacc[...] = jnp.zeros_like(kacc)
            dqg_ref[...] = jnp.zeros_like(dqg_ref)
            dkg_ref[...] = jnp.zeros_like(dkg_ref)

        x = x_ref[...]
        qg = qg_ref[1:2, :]
        kg = kg_ref[...]
        dq_v = dq_ref[...]
        dkv_v = dkv_ref[...].astype(BF)
        qacc[...] += _dot_tn(_rms_fwd(x, qg).astype(BF), dq_v)
        kacc[...] += _dot_tn(_rms_fwd(x, kg).astype(BF), dkv_v)
        dxq, prod_q = _rms_bwd(x, qg, _dot_nt(dq_v, wq_ref[...]))
        dxk, prod_k = _rms_bwd(x, kg, _dot_nt(dkv_v, wkv_ref[...]))
        dqg_ref[...] += _rowsum(prod_q)
        dkg_ref[...] += _rowsum(prod_k)
        dx3_ref[...] = dx_ref[...] + dxq + dxk

        @pl.when(i == n - 1)
        def _():
            dwq_ref[...] = qacc[...].astype(BF)
            dwkv_ref[...] = kacc[...].astype(BF)

    outs, _ = _call(
        body, name="qkv_bwd", grid=(n,),
        in_specs=[_row_spec(TM), _row_spec(TM, 2 * KVD), _row_spec(TM), _row_spec(TM), VSPEC, VSPEC, VSPEC, VSPEC],
        out_specs=[_row_spec(TM), _const_spec((D, D)), _const_spec((D, 2 * KVD)),
                   _const_spec((1, D)), _const_spec((1, D))],
        out_shape=[_sds((s_len, D)), _sds((D, D), BF), _sds((D, 2 * KVD), BF), _sds((1, D)), _sds((1, D))],
        scratch_shapes=[pltpu.VMEM((D, D), F32), pltpu.VMEM((D, 2 * KVD), F32)],
        args=[dq, dkv, x3, dx4, q_g, kv_g, w_q, w_kv])
    return outs


def _attn_group(i, q, kvw, sink_ref, g):
    rows = GQA * BLK
    heads = [GQA * g + j for j in range(GQA)]
    off = jnp.where(i > 0, BLK, 0)
    row = lax.broadcasted_iota(jnp.int32, (rows, 2 * BLK), 0)
    rel = (row % BLK) - lax.broadcasted_iota(jnp.int32, (rows, 2 * BLK), 1) + off
    valid = (rel >= 0) & (rel < BLK)
    head_of_row = lax.broadcasted_iota(jnp.int32, (rows, 1), 0) // BLK
    slope = jnp.zeros((rows, 1), F32)
    sink = jnp.zeros((rows, 1), F32)
    for j, h in enumerate(heads):
        slope = jnp.where(head_of_row == j, SLOPES[h], slope)
        sink = jnp.where(head_of_row == j, sink_ref[0, h], sink)
    qs = jnp.concatenate([q[:, h * HEAD_DIM:(h + 1) * HEAD_DIM] for h in heads], axis=0)
    k = kvw[:, g * HEAD_DIM:(g + 1) * HEAD_DIM]
    v = kvw[:, KVD + g * HEAD_DIM:KVD + (g + 1) * HEAD_DIM]
    s = _dot_nt(qs, k) * ATT_SCALE - slope * rel.astype(F32)
    s = jnp.where(valid, s, NEG_INF)
    m = jnp.maximum(jnp.max(s, axis=-1, keepdims=True), sink)
    e = jnp.exp(s - m)
    es = jnp.exp(sink - m)
    inv = 1.0 / (jnp.sum(e, axis=-1, keepdims=True) + es)
    return e * inv, es * inv, qs, k, v


def _unstack_heads(stacked):
    return [stacked[j * BLK:(j + 1) * BLK, :] for j in range(GQA)]


def _kv_window(kv_ref, i):
    ks = pl.multiple_of(jnp.maximum(i * BLK - BLK, 0), BLK)
    return ks, kv_ref[pl.ds(ks, 2 * BLK), :]


def _attn_fwd(q, kv, sinks, x3, w_o, post_g, rider=None):
    s_len = q.shape[0]

    def body(q_ref, kv_ref, sk_ref, x_ref, wo_ref, g_ref, a_ref, y_ref, x4_ref):
        i = pl.program_id(0)
        _, kvw = _kv_window(kv_ref, i)
        q = q_ref[...]
        outs = []
        for g in range(N_KV_HEADS):
            p, _, _, _, v = _attn_group(i, q, kvw, sk_ref, g)
            outs += _unstack_heads(_dot(p.astype(BF), v))
        attn = jnp.concatenate(outs, axis=1)
        a_ref[...] = attn
        y = _dot(attn.astype(BF), wo_ref[...])
        y_ref[...] = y
        x4_ref[...] = x_ref[...] + _rms_fwd(y, g_ref[1:2, :])

    return _call(body, name="attn_fwd", grid=(s_len // BLK,),
                 in_specs=[_row_spec(BLK), VSPEC, SSPEC, _row_spec(BLK), VSPEC, VSPEC],
                 out_specs=[_row_spec(BLK)] * 3, out_shape=[_sds((s_len, D))] * 3,
                 args=[q, kv, sinks, x3, w_o, post_g], rider=rider)


def _attn_bwd(dx4, y, attn, q, kv, sinks, w_o, post_g, rider=None):
    s_len = q.shape[0]
    n = s_len // BLK

    def body(dx_ref, y_ref, a_ref, q_ref, kv_ref, sk_ref, wo_ref, g_ref,
             dq_ref, dkv_ref, dwo_ref, dg_ref, dsk_ref, wacc):
        i = pl.program_id(0)

        @pl.when(i == 0)
        def _():
            dkv_ref[...] = jnp.zeros_like(dkv_ref)
            wacc[...] = jnp.zeros_like(wacc)
            dg_ref[...] = jnp.zeros_like(dg_ref)
            dsk_ref[...] = jnp.zeros_like(dsk_ref)

        dy, prod = _rms_bwd(y_ref[...], g_ref[1:2, :], dx_ref[...])
        dg_ref[...] += _rowsum(prod)
        dyb = dy.astype(BF)
        attn = a_ref[...]
        wacc[...] += _dot_tn(attn.astype(BF), dyb)
        d_o = _dot_nt(dyb, wo_ref[...])
        dod = d_o * attn
        ks, kvw = _kv_window(kv_ref, i)
        q = q_ref[...]
        lane = lax.broadcasted_iota(jnp.int32, (1, D), 1)
        dqs, dks, dvs = [], [], []
        dsk = jnp.zeros((1, D), F32)
        for g in range(N_KV_HEADS):
            p, ps, qs, k, v = _attn_group(i, q, kvw, sk_ref, g)
            cols = [slice((GQA * g + j) * HEAD_DIM, (GQA * g + j + 1) * HEAD_DIM) for j in range(GQA)]
            do_s = jnp.concatenate([d_o[:, c] for c in cols], axis=0).astype(BF)
            dsum = jnp.concatenate([jnp.sum(dod[:, c], axis=-1, keepdims=True) for c in cols], axis=0)
            dp = _dot_nt(do_s, v)
            dsb = (p * (dp - dsum) * ATT_SCALE).astype(BF)
            sink_part = ps * dsum
            for j in range(GQA):
                dsk = dsk + jnp.where(lane == GQA * g + j, -_rowsum(sink_part[j * BLK:(j + 1) * BLK, :]), 0.0)
            dqs += _unstack_heads(_dot(dsb, k))
            dks.append(_dot_tn(dsb, qs))
            dvs.append(_dot_tn(p.astype(BF), do_s))
        dsk_ref[...] += dsk
        dq_ref[...] = jnp.concatenate(dqs, axis=1).astype(BF)
        dkv_ref[pl.ds(ks, 2 * BLK), :] += jnp.concatenate(dks + dvs, axis=1)

        @pl.when(i == n - 1)
        def _():
            dwo_ref[...] = wacc[...].astype(BF)

    return _call(
        body, name="attn_bwd", grid=(n,),
        in_specs=[_row_spec(BLK), _row_spec(BLK), _row_spec(BLK), _row_spec(BLK), VSPEC, SSPEC, VSPEC, VSPEC],
        out_specs=[_row_spec(BLK), _const_spec((s_len, 2 * KVD)), _const_spec((D, D)),
                   _const_spec((1, D)), _const_spec((1, D))],
        out_shape=[_sds((s_len, D), BF), _sds((s_len, 2 * KVD)), _sds((D, D), BF), _sds((1, D)), _sds((1, D))],
        scratch_shapes=[pltpu.VMEM((D, D), F32)],
        args=[dx4, y, attn, q, kv, sinks, w_o, post_g], rider=rider)


Big = collections.namedtuple("Big", "name src layer L A R C rb")


def _bigs():
    out = {"pool_w": Big("pool_w", "pool_w", None, 4, 4, POOL_G // N_CHIPS, POOL_G, 32)}
    for l in range(2):
        out[f"w_gu{l}"] = Big(f"w_gu{l}", "w_gu", l, 1, 2, D, FF_HALF, 256)
        out[f"w_down{l}"] = Big(f"w_down{l}", "w_down", l, 1, 4, FF // N_CHIPS, D, 352)
        out[f"w_ple_gate{l}"] = Big(f"w_ple_gate{l}", "w_ple_gate", l, 1, 4, D // N_CHIPS, D, 128)
        out[f"w_ple_proj{l}"] = Big(f"w_ple_proj{l}", "w_ple_proj", l, 1, 1, PLE, D // N_CHIPS, 128)
    out["w_q"] = Big("w_q", "w_q", None, 1, 4, D // N_CHIPS, D, 128)
    out["w_o"] = Big("w_o", "w_o", None, 1, 4, D // N_CHIPS, D, 128)
    out["w_kv"] = Big("w_kv", "w_kv", None, 1, 4, D // N_CHIPS, 2 * KVD, 128)
    return out


BIGS = _bigs()
POOL_SCALE = Big("pool_scale", "pool_scale", None, 1, 1, 1, D // N_CHIPS, 1)
BIG_SOURCES = ("w_gu", "w_down", "w_ple_gate", "w_ple_proj", "w_q", "w_o", "w_kv", "pool_w")


def _ncb(t):
    return N_CHIPS // t.A


def _full_shape(t, rows=None):
    return (t.L, t.A, t.R if rows is None else rows, _ncb(t) * t.C)


def _slot_index(t, k):
    return k // _ncb(t), k % _ncb(t)


def _slot(ref, t, k, row0, rows):
    a, cb = _slot_index(t, k)
    return ref.at[:, a, pl.ds(row0, rows), pl.ds(pl.multiple_of(cb * t.C, 128), t.C)]


def _place(t, w, kc, out_dtype):
    rb = min(t.R, 2 * t.rb)

    def body(kc_ref, w_ref, o_ref):
        del kc_ref
        o_ref[...] = w_ref[...].astype(out_dtype)

    def in_map(l, j, kc_ref):
        return (l if t.layer is None else t.layer, j, 0)

    def out_map(l, j, kc_ref):
        a, cb = _slot_index(t, kc_ref[0])
        return (l, a, j, cb)

    return pl.pallas_call(
        body, name=f"place_{t.name}",
        grid_spec=pltpu.PrefetchScalarGridSpec(
            num_scalar_prefetch=1, grid=(t.L, t.R // rb),
            in_specs=[pl.BlockSpec((None, rb, t.C), in_map)],
            out_specs=pl.BlockSpec((None, None, rb, t.C), out_map)),
        out_shape=_sds(_full_shape(t), out_dtype),
        compiler_params=_params(2),
    )(kc, w)


def _place_many(ts, ws, kc, rider):
    n = 8

    def blocks_of(t):
        return next(nb for nb in (8, 4, 2, 1) if t.R % (16 * nb) == 0)

    def body(kc_ref, *refs):
        del kc_ref
        s = pl.program_id(0)
        for ti, t in enumerate(ts):
            @pl.when(s < blocks_of(t))
            def _():
                refs[len(ts) + ti][...] = refs[ti][...].astype(BF)

    in_specs, out_specs = [], []
    for t in ts:
        assert t.L == 1
        nb = blocks_of(t)
        rb = t.R // nb

        def in_map(s, kc_ref, t=t, nb=nb):
            return (0 if t.layer is None else t.layer, jnp.minimum(s, nb - 1), 0)

        def out_map(s, kc_ref, t=t, nb=nb):
            a, cb = _slot_index(t, kc_ref[0])
            return (0, a, jnp.minimum(s, nb - 1), cb)

        in_specs.append(pl.BlockSpec((None, rb, t.C), in_map))
        out_specs.append(pl.BlockSpec((None, None, rb, t.C), out_map))
    return _call(body, name="place_rest", grid=(n,), in_specs=in_specs, out_specs=out_specs,
                 out_shape=[_sds(_full_shape(t), BF) for t in ts], args=list(ws), rider=rider, prefetch=kc)


def _mesh_position():
    x, y, c = lax.axis_index("x"), lax.axis_index("y"), lax.axis_index("c")
    chips = [(1 - x, y), (x, 1 - y), (1 - x, 1 - y)]
    return x, y, c, chips


def _gather_rider(parts, fulls):
    nt = len(parts)
    TO_X, TO_Y, FWD_X, FWD_Y, SIB_X, SIB_Y, SIB_D = range(7)

    def rows_of(ti, core):
        t, r0, r1 = parts[ti]
        h = (r1 - r0) // 2
        return r0 + core * h, h

    def copy(outs, sems, kind, ti, k_src, row0, rows, dev):
        region = _slot(outs[ti], parts[ti][0], k_src, row0, rows)
        return pltpu.make_async_remote_copy(region, region, sems[0].at[ti, kind], sems[1].at[ti, kind],
                                            device_id=dev, device_id_type=MESH)

    def plan(outs, sems):
        x, y, c, _ = _mesh_position()
        me, kx, ky, kd = 2 * x + y, 2 * (1 - x) + y, 2 * x + (1 - y), 2 * (1 - x) + (1 - y)
        dev_x, dev_y, dev_d, sib = (1 - x, y, c), (x, 1 - y, c), (1 - x, 1 - y, c), (x, y, 1 - c)

        def whole(ti):
            return 0, parts[ti][0].R

        def mk(kind, k_send, k_recv, dev, send_rows, recv_rows):
            def build(ti, side):
                k_src = k_send if side == "s" else k_recv
                row0, rows = (send_rows if side == "s" else recv_rows)(ti)
                return copy(outs, sems, kind, ti, k_src, row0, rows, dev)
            return build

        def first_half(core):
            return lambda ti: (rows_of(ti, core)[0], rows_of(ti, core)[1] // 2)

        def second_half(core):
            return lambda ti: (rows_of(ti, core)[0] + rows_of(ti, core)[1] // 2, rows_of(ti, core)[1] // 2)

        mine = lambda ti: rows_of(ti, c)
        theirs = lambda ti: rows_of(ti, 1 - c)
        split = {
            TO_X: mk(TO_X, me, kx, dev_x, mine, mine),
            TO_Y: mk(TO_Y, me, ky, dev_y, mine, mine),
            FWD_X: mk(FWD_X, ky, kd, dev_x, first_half(c), first_half(c)),
            FWD_Y: mk(FWD_Y, kx, kd, dev_y, second_half(c), second_half(c)),
            SIB_X: mk(SIB_X, kx, kx, sib, mine, theirs),
            SIB_Y: mk(SIB_Y, ky, ky, sib, mine, theirs),
            SIB_D: mk(SIB_D, kd, kd, sib, mine, theirs),
        }
        direct = {
            TO_X: mk(TO_X, me, kx, dev_x, whole, whole),
            TO_Y: mk(TO_Y, me, ky, dev_y, whole, whole),
            FWD_X: mk(FWD_X, me, kd, dev_d, whole, whole),
        }
        return split, direct

    is_split = [t.R > 1 for t, _, _ in parts]

    def start(ins, outs, sems):
        split, direct = plan(outs, sems)
        for ti in range(nt):
            kinds = split if is_split[ti] else direct
            kinds[TO_X](ti, "s").start()
            kinds[TO_Y](ti, "s").start()
            if not is_split[ti]:
                kinds[FWD_X](ti, "s").start()

    def mid(ins, outs, sems):
        split, _ = plan(outs, sems)
        for ti in range(nt):
            if is_split[ti]:
                split[TO_Y](ti, "r").wait_recv()
                split[FWD_X](ti, "s").start()
                split[SIB_Y](ti, "s").start()
        for ti in range(nt):
            if is_split[ti]:
                split[TO_X](ti, "r").wait_recv()
                split[FWD_Y](ti, "s").start()
                split[SIB_X](ti, "s").start()

    def finish(ins, outs, sems):
        split, direct = plan(outs, sems)
        for ti in range(nt):
            if is_split[ti]:
                split[FWD_X](ti, "r").wait_recv()
                split[FWD_Y](ti, "r").wait_recv()
                split[SIB_D](ti, "s").start()
            else:
                for kind in (TO_X, TO_Y, FWD_X):
                    direct[kind](ti, "r").wait_recv()
        for ti in range(nt):
            if is_split[ti]:
                for kind in (SIB_X, SIB_Y, SIB_D):
                    split[kind](ti, "r").wait_recv()
        for ti in range(nt):
            kinds = split if is_split[ti] else direct
            for kind in kinds:
                kinds[kind](ti, "s").wait_send()

    sems = pltpu.SemaphoreType.DMA((nt, 7))
    return Rider(list(fulls), [_sds(a.shape, a.dtype) for a in fulls], {i: i for i in range(nt)},
                 [sems, sems], start, mid, finish)


def _pair_exchange(name, specs, grads):
    nt = len(specs)

    def body(*refs):
        gs = refs[:nt]
        lands = refs[nt:2 * nt]
        send, recv = refs[2 * nt:]
        x, y, c, _ = _mesh_position()
        cps = []
        for ti, t in enumerate(specs):
            half = t.R // 2
            cp = pltpu.make_async_remote_copy(gs[ti].at[:, :, pl.ds((1 - c) * half, half), :], lands[ti],
                                              send.at[ti], recv.at[ti],
                                              device_id=(x, y, 1 - c), device_id_type=MESH)
            cp.start()
            cps.append(cp)
        for cp in cps:
            cp.wait()

    return pl.pallas_call(
        body, name=name,
        in_specs=[ANYSPEC] * nt, out_specs=[ANYSPEC] * nt,
        out_shape=[_sds(_full_shape(t, t.R // 2), BF) for t in specs],
        scratch_shapes=[pltpu.SemaphoreType.DMA((nt,)), pltpu.SemaphoreType.DMA((nt,))],
        compiler_params=_params(),
    )(*grads)


def _pair_sum_job(t, g, land):
    assert t.L == 1
    half = t.R // 2
    nj = half // t.rb
    block = (None, t.A, t.rb, _ncb(t) * t.C)

    def fn(j, kc_ref, ins, outs):
        outs[0][...] = (ins[0][...].astype(F32) + ins[1][...].astype(F32)).astype(BF)

    return Job(nj,
               [(g, block, lambda j, kc_ref: (0, 0, kc_ref[1] * nj + j, 0)),
                (land, block, lambda j, kc_ref: (0, 0, j, 0))],
               [(_sds(_full_shape(t, half), BF), block, lambda j, kc_ref: (0, 0, j, 0))], fn)


def _scatter_rider(specs, sums):
    nt = len(specs)

    def copy(ins, outs, sems, ti, j, chip, c):
        t = specs[ti]
        cx, cy = chip
        return pltpu.make_async_remote_copy(_slot(ins[ti], t, 2 * cx + cy, 0, t.R // 2), outs[ti].at[j],
                                            sems[0].at[ti, j], sems[1].at[ti, j],
                                            device_id=(cx, cy, c), device_id_type=MESH)

    def start(ins, outs, sems):
        _, _, c, chips = _mesh_position()
        for j, chip in enumerate(chips):
            for ti in range(nt):
                copy(ins, outs, sems, ti, j, chip, c).start()

    def finish(ins, outs, sems):
        _, _, c, chips = _mesh_position()
        for j, chip in enumerate(chips):
            for ti in range(nt):
                copy(ins, outs, sems, ti, j, chip, c).wait()

    sems = pltpu.SemaphoreType.DMA((nt, N_CHIPS - 1))
    return Rider(list(sums), [_sds((N_CHIPS - 1, t.L, t.R // 2, t.C), BF) for t in specs], {}, [sems, sems],
                 start, None, finish)


def _chip_sum_job(ts, landed):
    t0 = ts[0]
    assert t0.L == 1
    half = t0.R // 2
    nj = half // t0.rb

    def local(j, li):
        return jnp.clip(j - li * nj, 0, nj - 1)

    ins = []
    for li, t in enumerate(ts):
        s, land = landed[t.name]

        def own_map(j, kc_ref, li=li, t=t):
            a, cb = _slot_index(t, kc_ref[0])
            return (0, a, local(j, li), cb)

        ins.append((s, (None, None, t.rb, t.C), own_map))
        ins.append((land, (N_CHIPS - 1, None, t.rb, t.C), lambda j, kc_ref, li=li: (0, 0, local(j, li), 0)))

    def fn(j, kc_ref, in_refs, outs):
        for li in range(len(ts)):
            @pl.when(j // nj == li)
            def _():
                acc = in_refs[2 * li][...].astype(F32)
                for k in range(N_CHIPS - 1):
                    acc = acc + in_refs[2 * li + 1][k].astype(F32)
                outs[0][...] = acc

    return Job(len(ts) * nj, ins,
               [(_sds((len(ts), t0.R, t0.C)), (None, t0.rb, t0.C),
                 lambda j, kc_ref: (j // nj, kc_ref[1] * nj + j % nj, 0))], fn)


def _adamw_job(rb, w, g, m, v):
    n_layers, r, c = w.shape
    nb = r // rb
    block = (None, rb, c)
    index = lambda j, kc_ref: (j // nb, j % nb, 0)

    def fn(j, kc_ref, ins, outs):
        g_v = ins[1][...]
        outs[0][...] = g_v
        outs[1][...], outs[2][...], outs[3][...] = _adamw_math(ins[0][...], g_v, ins[2][...], ins[3][...])

    return Job(n_layers * nb, [(a, block, index) for a in (w, g, m, v)],
               [(_sds(w.shape), block, index)] * 4, fn)


def _chip_sum_fused_job(ts, fused, by_cols):
    t0 = ts[0]
    own0 = fused[t0.name][0]
    if by_cols:
        rows, cols = own0.shape
    else:
        nb, rows, bw = own0.shape
        cols = nb * bw
    nj = rows // t0.rb

    def local(j, li):
        return jnp.clip(j - li * nj, 0, nj - 1)

    ins = []
    for li, t in enumerate(ts):
        own, land = fused[t.name]
        if by_cols:
            ins.append((own, (t.rb, cols), lambda j, kc_ref, li=li: (local(j, li), 0)))
            ins.append((land, (N_CHIPS - 1, t.rb, cols), lambda j, kc_ref, li=li: (0, local(j, li), 0)))
        else:
            ins.append((own, (nb, t.rb, bw), lambda j, kc_ref, li=li: (0, local(j, li), 0)))
            ins.append((land, (N_CHIPS - 1, nb, t.rb, bw), lambda j, kc_ref, li=li: (0, 0, local(j, li), 0)))

    def fn(j, kc_ref, in_refs, outs):
        for li in range(len(ts)):
            @pl.when(j // nj == li)
            def _():
                acc = in_refs[2 * li][...].astype(F32)
                for k in range(N_CHIPS - 1):
                    acc = acc + in_refs[2 * li + 1][k].astype(F32)
                outs[0][...] = acc if by_cols else jnp.concatenate([acc[b] for b in range(nb)], axis=1)

    def out_map(j, kc_ref):
        return (j // nj, j % nj, kc_ref[1]) if by_cols else (j // nj, kc_ref[1] * nj + j % nj, 0)

    return Job(len(ts) * nj, ins, [(_sds((len(ts), t0.R, t0.C)), (None, t0.rb, cols), out_map)], fn)


def _pair_share(halves, by_cols):
    nt = len(halves)

    def part(ref, ti, core):
        axis = 2 if by_cols[ti] else 1
        half = halves[ti].shape[axis] // 2
        piece = pl.ds(pl.multiple_of(core * half, 128 if by_cols[ti] else 8), half)
        return ref.at[:, :, piece] if by_cols[ti] else ref.at[:, piece, :]

    def body(*refs):
        outs = refs[nt:2 * nt]
        send, recv = refs[2 * nt:]
        x, y, c, _ = _mesh_position()
        cps = []
        for ti in range(nt):
            mine = part(outs[ti], ti, c)
            cp = pltpu.make_async_remote_copy(mine, mine, send.at[ti], recv.at[ti],
                                              device_id=(x, y, 1 - c), device_id_type=MESH)
            cp.start()
            cps.append(cp)
        for ti in range(nt):
            theirs = part(outs[ti], ti, 1 - c)
            pltpu.make_async_remote_copy(theirs, theirs, send.at[ti], recv.at[ti],
                                         device_id=(x, y, 1 - c), device_id_type=MESH).wait_recv()
        for cp in cps:
            cp.wait_send()

    return pl.pallas_call(
        body, name="grads_pair_share",
        in_specs=[ANYSPEC] * nt, out_specs=[ANYSPEC] * nt,
        out_shape=[_sds(a.shape, a.dtype) for a in halves],
        scratch_shapes=[pltpu.SemaphoreType.DMA((nt,)), pltpu.SemaphoreType.DMA((nt,))],
        input_output_aliases={i: i for i in range(nt)},
        compiler_params=_params(),
    )(*halves)


def _adamw_math(w, g, m, v):
    m = B1 * m + (1.0 - B1) * g
    v = B2 * v + (1.0 - B2) * (g * g)
    delta = -LR * ((m / BC1) / (jnp.sqrt(v / BC2) + AEPS) + WD * w)
    return delta, m, v


def _adamw(name, rb, w, g, m, v):
    n_layers, r, c = w.shape

    def body(w_ref, g_ref, m_ref, v_ref, go_ref, d_ref, nm_ref, nv_ref):
        g_v = g_ref[...]
        go_ref[...] = g_v
        d_ref[...], nm_ref[...], nv_ref[...] = _adamw_math(w_ref[...], g_v, m_ref[...], v_ref[...])

    spec = pl.BlockSpec((None, rb, c), lambda l, j: (l, j, 0))
    return pl.pallas_call(
        body, name=f"adamw_{name}", grid=(n_layers, r // rb),
        in_specs=[spec] * 4, out_specs=[spec] * 4, out_shape=[_sds(w.shape)] * 4,
        compiler_params=_params(2),
    )(w, g, m, v)


GAIN_ROWS = {"pre_mix_g": 0, "post_mix_g": 2, "pre_ffn_g": 4, "post_ffn_g": 6, "ple_g": 8, "ple_post_g": 10}
ROW_KV_G, ROW_POOL_SCALE, ROW_SINKS, ROW_LOSS, PACK_ROWS = 12, 13, 14, 15, 16
SMALL_NAMES = tuple(GAIN_ROWS) + ("kv_g", "pool_scale", "sinks")


def _small_all_reduce(rows, dpool, rider=None):
    ng, pr = len(WINDOWS), POOL_G // N_CHIPS

    def body(*refs):
        row_refs = refs[:PACK_ROWS]
        dpool_ref, tot_ref, gpool_ref, pack, land, pland, send, recv, psend, precv = refs[PACK_ROWS:]
        x, y, c, _ = _mesh_position()
        me = 4 * x + 2 * y + c
        for r in range(PACK_ROWS):
            pack[r:r + 1, :] = row_refs[r][...]

        def shard_of(k):
            return dpool_ref.at[:, pl.ds(pl.multiple_of(k * pr, pr), pr), :]

        cps = []
        for j in range(1, N_DEV):
            px, py, pc = x ^ (j >> 2), y ^ ((j >> 1) & 1), c ^ (j & 1)
            cps.append(pltpu.make_async_remote_copy(pack, land.at[me], send.at[j], recv.at[j],
                                                    device_id=(px, py, pc), device_id_type=MESH))
            cps.append(pltpu.make_async_remote_copy(shard_of(2 * px + py), pland.at[me], psend.at[j], precv.at[j],
                                                    device_id=(px, py, pc), device_id_type=MESH))
        for cp in cps:
            cp.start()
        land[me] = pack[...]
        pland[me] = dpool_ref[:, pl.ds(pl.multiple_of((2 * x + y) * pr, pr), pr), :]
        for j in range(1, N_DEV):
            pltpu.make_async_remote_copy(pack, land.at[me ^ j], send.at[j], recv.at[j],
                                         device_id=(x, y, c), device_id_type=MESH).wait_recv()
            pltpu.make_async_remote_copy(shard_of(0), pland.at[me ^ j], psend.at[j], precv.at[j],
                                         device_id=(x, y, c), device_id_type=MESH).wait_recv()
        for cp in cps:
            cp.wait_send()
        tot = land[0]
        gp = pland[0].astype(F32)
        for d in range(1, N_DEV):
            tot = tot + land[d]
            gp = gp + pland[d].astype(F32)
        tot_ref[...] = tot
        gpool_ref[...] = gp

    sems = pltpu.SemaphoreType.DMA((N_DEV,))
    return _call(
        body, name="small_all_reduce", grid=(1,),
        in_specs=[VSPEC] * (PACK_ROWS + 1), out_specs=[VSPEC, VSPEC],
        out_shape=[_sds((PACK_ROWS, D)), _sds((ng, pr, POOL_G))],
        scratch_shapes=[pltpu.VMEM((PACK_ROWS, D), F32), pltpu.VMEM((N_DEV, PACK_ROWS, D), F32),
                        pltpu.VMEM((N_DEV, ng, pr, POOL_G), BF), sems, sems, sems, sems],
        args=[*rows, dpool], rider=rider)


def _small_adamw(tot, kc, small_w, small_m, small_v):
    names = SMALL_NAMES
    n = len(names)

    def body(*refs):
        tot_ref, kc_ref = refs[0], refs[1]
        w_refs = dict(zip(names, refs[2:2 + n]))
        m_refs = dict(zip(names, refs[2 + n:2 + 2 * n]))
        v_refs = dict(zip(names, refs[2 + 2 * n:2 + 3 * n]))
        loss_ref = refs[2 + 3 * n]
        out_refs = {nm: refs[3 + 3 * n + 4 * k: 7 + 3 * n + 4 * k] for k, nm in enumerate(names)}
        tot = tot_ref[...]
        loss_ref[...] = 0.5 * jnp.sum(tot[ROW_LOSS:ROW_LOSS + 1, :], axis=-1, keepdims=True) * (1.0 / D)

        def update(nm, g):
            g_ref, d_ref, nm_ref, nv_ref = out_refs[nm]
            g_ref[...] = g
            d_ref[...], nm_ref[...], nv_ref[...] = _adamw_math(w_refs[nm][...], g, m_refs[nm][...], v_refs[nm][...])

        for nm, r in GAIN_ROWS.items():
            update(nm, tot[r:r + 2, :])
        update("kv_g", tot[ROW_KV_G:ROW_KV_G + 1, :])
        k = kc_ref[0]
        width = D // N_CHIPS
        g_scale = jnp.zeros((1, width), F32)
        for kk in range(N_CHIPS):
            g_scale = g_scale + jnp.where(k == kk, tot[ROW_POOL_SCALE:ROW_POOL_SCALE + 1, kk * width:(kk + 1) * width], 0.0)
        update("pool_scale", g_scale)
        update("sinks", tot[ROW_SINKS:ROW_SINKS + 1, 0:N_HEADS])

    ins = [tot, kc] + [small_w[nm] for nm in names] + [small_m[nm] for nm in names] + [small_v[nm] for nm in names]
    out_shape = [_sds((1, 1))]
    for nm in names:
        out_shape += [_sds(small_w[nm].shape)] * 4
    outs = pl.pallas_call(
        body, name="small_adamw",
        in_specs=[VSPEC, SSPEC] + [VSPEC] * (3 * n), out_specs=[VSPEC] * len(out_shape), out_shape=out_shape,
        compiler_params=_params(),
    )(*ins)
    return outs[0], {nm: outs[1 + 4 * k: 5 + 4 * k] for k, nm in enumerate(names)}


def _compute_layout(t, full):
    if t.src == "w_gu":
        return full.reshape(2, D, FF)
    if t.src == "pool_w":
        return full.reshape(len(WINDOWS), POOL_G, POOL_G)
    if t.src == "pool_scale":
        return full.reshape(1, D)
    return full.reshape(t.A * t.R, _ncb(t) * t.C)


def kernel(x, p, pre_mix_g, post_mix_g, pre_ffn_g, post_ffn_g, pool_w, pool_scale, kv_g, w_kv, w_q, sinks, w_o, w_gu, w_down, ple_g, w_ple_gate, w_ple_proj, ple_post_g, loss_target, m_pre_mix_g, m_post_mix_g, m_pre_ffn_g, m_post_ffn_g, m_pool_w, m_pool_scale, m_kv_g, m_w_kv, m_w_q, m_sinks, m_w_o, m_w_gu, m_w_down, m_ple_g, m_w_ple_gate, m_w_ple_proj, m_ple_post_g, v_pre_mix_g, v_post_mix_g, v_pre_ffn_g, v_post_ffn_g, v_pool_w, v_pool_scale, v_kv_g, v_w_kv, v_w_q, v_sinks, v_w_o, v_w_gu, v_w_down, v_ple_g, v_w_ple_gate, v_w_ple_proj, v_ple_post_g):
    weights = dict(pre_mix_g=pre_mix_g, post_mix_g=post_mix_g, pre_ffn_g=pre_ffn_g, post_ffn_g=post_ffn_g,
                   pool_w=pool_w, pool_scale=pool_scale, kv_g=kv_g, w_kv=w_kv, w_q=w_q, sinks=sinks, w_o=w_o,
                   w_gu=w_gu, w_down=w_down, ple_g=ple_g, w_ple_gate=w_ple_gate, w_ple_proj=w_ple_proj,
                   ple_post_g=ple_post_g)
    m_in = dict(pre_mix_g=m_pre_mix_g, post_mix_g=m_post_mix_g, pre_ffn_g=m_pre_ffn_g, post_ffn_g=m_post_ffn_g,
                pool_w=m_pool_w, pool_scale=m_pool_scale, kv_g=m_kv_g, w_kv=m_w_kv, w_q=m_w_q, sinks=m_sinks,
                w_o=m_w_o, w_gu=m_w_gu, w_down=m_w_down, ple_g=m_ple_g, w_ple_gate=m_w_ple_gate,
                w_ple_proj=m_w_ple_proj, ple_post_g=m_ple_post_g)
    v_in = dict(pre_mix_g=v_pre_mix_g, post_mix_g=v_post_mix_g, pre_ffn_g=v_pre_ffn_g, post_ffn_g=v_post_ffn_g,
                pool_w=v_pool_w, pool_scale=v_pool_scale, kv_g=v_kv_g, w_kv=v_w_kv, w_q=v_w_q, sinks=v_sinks,
                w_o=v_w_o, w_gu=v_w_gu, w_down=v_w_down, ple_g=v_ple_g, w_ple_gate=v_w_ple_gate,
                w_ple_proj=v_w_ple_proj, ple_post_g=v_ple_post_g)
    order = ["pre_mix_g", "post_mix_g", "pre_ffn_g", "post_ffn_g", "pool_w", "pool_scale", "kv_g", "w_kv", "w_q",
             "sinks", "w_o", "w_gu", "w_down", "ple_g", "w_ple_gate", "w_ple_proj", "ple_post_g"]

    kc = jnp.stack([2 * lax.axis_index("x") + lax.axis_index("y"), lax.axis_index("c")]).astype(jnp.int32)
    s_len = x.shape[1]
    x2d = x.reshape(s_len, D)
    p3d = p.reshape(2, s_len, PLE)
    target = loss_target.reshape(s_len, D)
    kv_g2d = kv_g.reshape(1, D)
    gains = {nm: weights[nm] for nm in GAIN_ROWS}

    def shard_view(src, a):
        t = next(t for t in BIGS.values() if t.src == src)
        return a.reshape(-1, t.R, t.C)

    first, second = ["pool_w", "pool_scale"], ["w_gu0", "w_down0"]
    rest = [nm for nm in BIGS if nm not in first + second]
    specs = dict(BIGS, pool_scale=POOL_SCALE)
    placed = {nm: _place(BIGS[nm], shard_view(BIGS[nm].src, weights[BIGS[nm].src]), kc, BF)
              for nm in first + second if nm in BIGS}
    placed["pool_scale"] = _place(POOL_SCALE, pool_scale.reshape(1, 1, D // N_CHIPS), kc, F32)

    def gather(names, rows=None):
        rows = rows or {}
        parts = [(specs[nm],) + tuple(rows.get(nm, (0, specs[nm].R))) for nm in names]
        return _gather_rider(parts, [placed[nm] for nm in names])

    def take(names, results):
        for nm, a in zip(names, results):
            placed[nm] = a

    def weight(nm):
        return _compute_layout(specs[nm], placed[nm])

    cast, got = _place_many([BIGS[nm] for nm in rest], [shard_view(BIGS[nm].src, weights[BIGS[nm].src]) for nm in rest],
                            kc, rider=gather(first))
    take(rest, cast)
    take(first, got)


    (y0, x1), got = _mixa_fwd(x2d, gains["pre_mix_g"], weight("pool_w"), weight("pool_scale"), gains["post_mix_g"],
                              rider=gather(second))
    take(second, got)

    ride = ["w_ple_gate0", "w_ple_proj0", "w_q", "w_kv", "w_o", "w_gu1"]
    (f0, x2, g0, u0), got = _ffn_fwd(0, x1, gains["pre_ffn_g"], weight("w_gu0"), weight("w_down0"), gains["post_ffn_g"],
                             rider=gather(ride, {"w_gu1": (0, 320)}))
    take(ride, got)

    ride = ["w_ple_gate1", "w_ple_proj1", "w_gu1"]
    (z0, pe0, x3, q, kv), got = _ple_fwd(
        0, x2, p3d, gains["ple_g"], weight("w_ple_gate0"), weight("w_ple_proj0"), gains["ple_post_g"],
        qkv=(gains["pre_mix_g"], kv_g2d, weight("w_q"), weight("w_kv")),
        rider=gather(ride, {"w_gu1": (320, 704)}))
    take(ride, got)

    ride = ["w_down1", "w_gu1"]
    (attn, y1, x4), got = _attn_fwd(q, kv, sinks, x3, weight("w_o"), gains["post_mix_g"],
                                    rider=gather(ride, {"w_gu1": (704, D)}))
    take(ride, got)

    (f1, x5, g1, u1), _ = _ffn_fwd(1, x4, gains["pre_ffn_g"], weight("w_gu1"), weight("w_down1"), gains["post_ffn_g"])
    (z1, pe1, dx6, loss_row), _ = _ple_fwd(1, x5, p3d, gains["ple_g"], weight("w_ple_gate1"), weight("w_ple_proj1"),
                                           gains["ple_post_g"], target=target)

    local = {}
    landed = {}
    fused = {}

    def pair_stage(tag, names):
        ts = [BIGS[nm] for nm in names]
        gs = [local[nm].reshape(_full_shape(t)) for nm, t in zip(names, ts)]
        lands = _pair_exchange(f"grads_pair_exchange_{tag}", ts, gs)
        jobs = [_pair_sum_job(t, g, l) for t, g, l in zip(ts, gs, lands)]
        return [r[0] for r in _multi_call(f"pair_sum_{tag}", jobs, kc)]

    def scatter(names, sums):
        return _scatter_rider([BIGS[nm] for nm in names], sums)

    def keep(names, sums, got):
        for nm, s, l in zip(names, sums, got):
            landed[nm] = (s, l)

    (dx5, local["w_ple_gate1"], local["w_ple_proj1"], d_ple1, d_plepost1), _ = _ple_bwd(
        1, dx6, x5, z1, pe1, p3d, gains["ple_g"], weight("w_ple_gate1"), gains["ple_post_g"])

    group_a = ["w_ple_gate1", "w_ple_proj1"]
    sums_a = pair_stage("a", group_a)
    (dx4, d_preffn1, d_postffn1, *scattered), _ = _ffn_bwd(
        1, dx5, x4, f1, g1, u1, gains["pre_ffn_g"], weight("w_gu1"), weight("w_down1"), gains["post_ffn_g"], kc)
    fused["w_gu1"], fused["w_down1"] = scattered[0:2], scattered[2:4]

    (dq, dkv, local["w_o"], d_postmix1, d_sinks), got = _attn_bwd(
        dx4, y1, attn, q, kv, sinks, weight("w_o"), gains["post_mix_g"], rider=scatter(group_a, sums_a))
    keep(group_a, sums_a, got)
    dx3, local["w_q"], local["w_kv"], d_premix1, d_kvg = _qkv_bwd(
        dq, dkv, x3, dx4, gains["pre_mix_g"], kv_g2d, weight("w_q"), weight("w_kv"))

    group_b = ["w_o", "w_q", "w_kv"]
    sums_b = pair_stage("b", group_b)
    (dx2, local["w_ple_gate0"], local["w_ple_proj0"], d_ple0, d_plepost0), got = _ple_bwd(
        0, dx3, x2, z0, pe0, p3d, gains["ple_g"], weight("w_ple_gate0"), gains["ple_post_g"],
        rider=scatter(group_b, sums_b))
    keep(group_b, sums_b, got)

    group_c = ["w_ple_gate0", "w_ple_proj0"]
    sums_c = pair_stage("c", group_c)
    (dx1, d_preffn0, d_postffn0, *scattered), _ = _ffn_bwd(
        0, dx2, x1, f0, g0, u0, gains["pre_ffn_g"], weight("w_gu0"), weight("w_down0"), gains["post_ffn_g"], kc)
    fused["w_gu0"], fused["w_down0"] = scattered[0:2], scattered[2:4]

    (dx0, d_pool, d_scale, d_postmix0, d_premix0), _ = _mixa_bwd(
        dx1, x2d, y0, gains["pre_mix_g"], weight("pool_w"), weight("pool_scale"), gains["post_mix_g"])

    rows = [d_premix0, d_premix1, d_postmix0, d_postmix1, d_preffn0, d_preffn1, d_postffn0, d_postffn1,
            d_ple0, d_ple1, d_plepost0, d_plepost1, d_kvg, d_scale, d_sinks, loss_row]
    as2d = lambda a: a.reshape(1, D) if a.ndim == 1 else a
    (tot, g_pool), got = _small_all_reduce(rows, d_pool, rider=scatter(group_c, sums_c))
    keep(group_c, sums_c, got)
    loss, small = _small_adamw(tot, kc, {nm: as2d(weights[nm]) for nm in SMALL_NAMES},
                               {nm: as2d(m_in[nm]) for nm in SMALL_NAMES},
                               {nm: as2d(v_in[nm]) for nm in SMALL_NAMES})

    layers_of = lambda src: [t for t in BIGS.values() if t.src == src]
    own_scatter = ["w_gu", "w_down"]
    others = [src for src in BIG_SOURCES if src not in own_scatter and src != "pool_w"]
    shared = own_scatter + others
    jobs = [_chip_sum_fused_job(layers_of(src), fused, by_cols=src == "w_down") for src in own_scatter]
    jobs += [_chip_sum_job(layers_of(src), landed) for src in others]
    halves = {src: r[0] for src, r in zip(shared, _multi_call("chip_sum", jobs, kc))}
    full_grads = dict(zip(shared, _pair_share([halves[src] for src in shared], [src == "w_down" for src in shared])))
    full_grads["pool_w"] = g_pool

    def adam_args(src):
        return (layers_of(src)[0].rb, shard_view(src, weights[src]), full_grads[src],
                shard_view(src, m_in[src]), shard_view(src, v_in[src]))

    out = {"grad": {}, "delta": {}, "new_m": {}, "new_v": {}}
    results = {src: _adamw(src, *adam_args(src)) for src in own_scatter}
    rest_srcs = others + ["pool_w"]
    results.update(zip(rest_srcs, _multi_call("adamw_rest", [_adamw_job(*adam_args(src)) for src in rest_srcs], kc)))
    for src in BIG_SOURCES:
        shape = weights[src].shape
        for kind, a in zip(("grad", "delta", "new_m", "new_v"), results[src]):
            out[kind][src] = a.reshape(shape)
    for nm in SMALL_NAMES:
        shape = weights[nm].shape
        for kind, a in zip(("grad", "delta", "new_m", "new_v"), small[nm]):
            out[kind][nm] = a.reshape(shape)

    return (loss.reshape(()), dx0.reshape(x.shape),
            *[out["grad"][nm] for nm in order], *[out["delta"][nm] for nm in order],
            *[out["new_m"][nm] for nm in order], *[out["new_v"][nm] for nm in order])
```

```python
import collections

import jax
import jax.numpy as jnp
from jax import lax
from jax.experimental import pallas as pl
from jax.experimental.pallas import tpu as pltpu

D = 1024
FF = 2816
N_HEADS = 16
HEAD_DIM = 64
N_KV_HEADS = 4
GQA = N_HEADS // N_KV_HEADS
KVD = N_KV_HEADS * HEAD_DIM
PLE = 256
BLK = 128
WINDOWS = (2, 4, 8, 16)
POOL_G = 256
HALO = 16
EPS = 1e-6
NEG_INF = -1e30
ATT_SCALE = HEAD_DIM ** -0.5
SLOPES = tuple(2.0 ** (-8.0 * (h + 1) / N_HEADS) for h in range(N_HEADS))
N_CHIPS = 4
N_DEV = 8

LR, B1, B2, AEPS, WD, STEP = 0.001, 0.9, 0.999, 1e-08, 0.01, 10
BC1 = 1.0 - B1 ** STEP
BC2 = 1.0 - B2 ** STEP

BF = jnp.bfloat16
F32 = jnp.float32
MESH = pl.DeviceIdType.MESH
VMEM_LIMIT_V7X = 58 * 1024 * 1024
TM = 256
TM_FFN_BWD = 512
FF_CHUNK = 256
FF_HALF = FF // 2

VSPEC = pl.BlockSpec(memory_space=pltpu.VMEM)
SSPEC = pl.BlockSpec(memory_space=pltpu.SMEM)
ANYSPEC = pl.BlockSpec(memory_space=pl.ANY)


def _params(n_grid=0):
    sem = ("arbitrary",) * n_grid if n_grid else None
    return pltpu.CompilerParams(dimension_semantics=sem, vmem_limit_bytes=VMEM_LIMIT_V7X)


def _sds(shape, dtype=F32):
    return jax.ShapeDtypeStruct(tuple(shape), dtype)


Rider = collections.namedtuple("Rider", "arrays out_shapes aliases scratch start mid finish")
MID_NUM, MID_DEN = 5, 8


def _call(body, *, name, grid, in_specs, out_specs, out_shape, args, scratch_shapes=(), rider=None, prefetch=None):
    ni, no, ns = len(in_specs), len(out_specs), len(scratch_shapes)
    npre = 0 if prefetch is None else 1
    pre = [] if prefetch is None else [prefetch]
    if rider is None:
        rider = Rider([], [], {}, [], None, None, None)
    ri, ro = len(rider.arrays), len(rider.out_shapes)

    def full(*refs):
        pre_refs, refs = refs[:npre], refs[npre:]
        ins, refs = refs[:ni], refs[ni:]
        rins, refs = refs[:ri], refs[ri:]
        outs, refs = refs[:no], refs[no:]
        routs, refs = refs[:ro], refs[ro:]
        scr, rscr = refs[:ns], refs[ns:]
        ids = [pl.program_id(a) for a in range(len(grid))]
        first = ids[0] == 0
        last = ids[0] == grid[0] - 1
        for a in range(1, len(grid)):
            first = first & (ids[a] == 0)
            last = last & (ids[a] == grid[a] - 1)

        if rider.start is not None:
            @pl.when(first)
            def _():
                rider.start(rins, routs, rscr)

        if rider.mid is not None:
            assert len(grid) == 1

            @pl.when(ids[0] == (grid[0] * MID_NUM) // MID_DEN)
            def _():
                rider.mid(rins, routs, rscr)

        body(*pre_refs, *ins, *outs, *scr)

        if rider.finish is not None:
            @pl.when(last)
            def _():
                rider.finish(rins, routs, rscr)

    outs = pl.pallas_call(
        full, name=name,
        grid_spec=pltpu.PrefetchScalarGridSpec(
            num_scalar_prefetch=npre, grid=grid,
            in_specs=list(in_specs) + [ANYSPEC] * ri, out_specs=list(out_specs) + [ANYSPEC] * ro,
            scratch_shapes=list(scratch_shapes) + list(rider.scratch)),
        out_shape=list(out_shape) + list(rider.out_shapes),
        input_output_aliases={npre + ni + a: no + b for a, b in rider.aliases.items()},
        compiler_params=_params(len(grid)))(*pre, *args, *rider.arrays)
    return list(outs[:no]), list(outs[no:])


def _run(name, rider):
    ri = len(rider.arrays)

    def body(*refs):
        rins, routs, rscr = refs[:ri], refs[ri:ri + len(rider.out_shapes)], refs[ri + len(rider.out_shapes):]
        rider.start(rins, routs, rscr)
        if rider.mid is not None:
            rider.mid(rins, routs, rscr)
        rider.finish(rins, routs, rscr)

    return pl.pallas_call(
        body, name=name, in_specs=[ANYSPEC] * ri, out_specs=[ANYSPEC] * len(rider.out_shapes),
        out_shape=list(rider.out_shapes), scratch_shapes=list(rider.scratch),
        input_output_aliases=dict(rider.aliases), compiler_params=_params())(*rider.arrays)


Job = collections.namedtuple("Job", "steps ins outs fn")


def _multi_call(name, jobs, kc):
    n = max(job.steps for job in jobs)

    def clamped(index, steps):
        return lambda s, kc_ref: index(jnp.minimum(s, steps - 1), kc_ref)

    in_specs, out_specs, out_shape, args = [], [], [], []
    for job in jobs:
        for arr, block, index in job.ins:
            in_specs.append(pl.BlockSpec(block, clamped(index, job.steps)))
            args.append(arr)
        for sds, block, index in job.outs:
            out_specs.append(pl.BlockSpec(block, clamped(index, job.steps)))
            out_shape.append(sds)
    n_in = len(args)

    def body(kc_ref, *refs):
        s = pl.program_id(0)
        i0, o0 = 0, n_in
        for job in jobs:
            ins, outs = refs[i0:i0 + len(job.ins)], refs[o0:o0 + len(job.outs)]
            i0, o0 = i0 + len(job.ins), o0 + len(job.outs)

            @pl.when(s < job.steps)
            def _():
                job.fn(s, kc_ref, ins, outs)

    outs, _ = _call(body, name=name, grid=(n,), in_specs=in_specs, out_specs=out_specs, out_shape=out_shape,
                    args=args, prefetch=kc)
    res, o0 = [], 0
    for job in jobs:
        res.append(outs[o0:o0 + len(job.outs)])
        o0 += len(job.outs)
    return res


def _rms_fwd(x, g):
    r = lax.rsqrt(jnp.mean(x * x, axis=-1, keepdims=True) + EPS)
    return x * r * g


def _rms_bwd(x, g, dy):
    r = lax.rsqrt(jnp.mean(x * x, axis=-1, keepdims=True) + EPS)
    xn = x * r
    dxn = dy * g
    dx = r * (dxn - xn * jnp.mean(dxn * xn, axis=-1, keepdims=True))
    return dx, dy * xn


def _rowsum(a):
    return jnp.sum(a, axis=0, keepdims=True)


def _sigmoid(z):
    return 1.0 / (1.0 + jnp.exp(-z))


def _dot(a, b):
    return jnp.dot(a, b, preferred_element_type=F32)


def _dot_nt(a, b):
    return lax.dot_general(a, b, (((1,), (1,)), ((), ())), preferred_element_type=F32)


def _dot_tn(a, b):
    return lax.dot_general(a, b, (((0,), (0,)), ((), ())), preferred_element_type=F32)


def _row_spec(tm, width=D):
    return pl.BlockSpec((tm, width), lambda i: (i, 0))


def _const_spec(shape):
    zeros = (0,) * len(shape)
    return pl.BlockSpec(tuple(shape), lambda *_: zeros)


def _pool_delta(he, pos):
    out = []
    for gi, w in enumerate(WINDOWS):
        hg = he[:, gi * POOL_G:(gi + 1) * POOL_G]
        s = hg
        k = 1
        while k < w:
            s = s + pltpu.roll(s, k, 0)
            k *= 2
        cnt = jnp.maximum(jnp.minimum(pos + 1, w), 1).astype(F32)
        out.append(s / cnt - hg)
    return out


def _load_with_halo_before(x_ref, i, tm):
    r0 = pl.multiple_of(i * tm, tm)
    hs = pl.multiple_of(jnp.maximum(i * tm - HALO, 0), 8)
    xh = jnp.where(i > 0, x_ref[pl.ds(hs, HALO), :], 0.0)
    xt = x_ref[pl.ds(r0, tm), :]
    return xt, jnp.concatenate([xh, xt], axis=0)


def _mixa_fwd(x, pre_g, pool_w, pool_scale, post_g, rider=None):
    s_len = x.shape[0]
    n = s_len // TM

    def body(x_ref, pg_ref, w_ref, sc_ref, qg_ref, y_ref, x1_ref):
        i = pl.program_id(0)
        xt, xe = _load_with_halo_before(x_ref, i, TM)
        he = _rms_fwd(xe, pg_ref[0:1, :])
        pos = i * TM - HALO + lax.broadcasted_iota(jnp.int32, (TM + HALO, 1), 0)
        ds = _pool_delta(he, pos)
        ys = [_dot(ds[gi][HALO:, :].astype(BF), w_ref[gi]) for gi in range(len(WINDOWS))]
        y = jnp.concatenate(ys, axis=1) * sc_ref[...]
        y_ref[...] = y
        x1_ref[...] = xt + _rms_fwd(y, qg_ref[0:1, :])

    return _call(body, name="mixa_fwd", grid=(n,),
                 in_specs=[VSPEC] * 5, out_specs=[_row_spec(TM), _row_spec(TM)],
                 out_shape=[_sds((s_len, D)), _sds((s_len, D))],
                 args=[x, pre_g, pool_w, pool_scale, post_g], rider=rider)


def _mixa_bwd(dx1, x, y, pre_g, pool_w, pool_scale, post_g, rider=None):
    s_len = x.shape[0]
    n = s_len // TM
    ng = len(WINDOWS)

    def body(dx_ref, x_ref, y_ref, pg_ref, w_ref, sc_ref, qg_ref,
             dx0_ref, dw_ref, dsc_ref, dqg_ref, dpg_ref, wacc):
        i = pl.program_id(0)

        @pl.when(i == 0)
        def _():
            wacc[...] = jnp.zeros_like(wacc)
            dsc_ref[...] = jnp.zeros_like(dsc_ref)
            dqg_ref[...] = jnp.zeros_like(dqg_ref)
            dpg_ref[...] = jnp.zeros_like(dpg_ref)

        r0 = pl.multiple_of(i * TM, TM)
        xt, xe = _load_with_halo_before(x_ref, i, TM)
        he = _rms_fwd(xe, pg_ref[0:1, :])
        pos_b = i * TM - HALO + lax.broadcasted_iota(jnp.int32, (TM + HALO, 1), 0)
        ds = _pool_delta(he, pos_b)

        last = i == n - 1
        a0 = pl.multiple_of(jnp.minimum(i * TM + TM, s_len - HALO), 8)
        ye = jnp.concatenate([y_ref[pl.ds(r0, TM), :], y_ref[pl.ds(a0, HALO), :]], axis=0)
        dt = dx_ref[pl.ds(r0, TM), :]
        de = jnp.concatenate([dt, jnp.where(last, 0.0, dx_ref[pl.ds(a0, HALO), :])], axis=0)
        dye, prod = _rms_bwd(ye, qg_ref[0:1, :], de)
        dqg_ref[...] += _rowsum(prod[:TM, :])
        dys = dye * sc_ref[...]
        pos_a = i * TM + lax.broadcasted_iota(jnp.int32, (TM + HALO, 1), 0)

        dhs, dscs = [], []
        for gi, w in enumerate(WINDOWS):
            sl = slice(gi * POOL_G, (gi + 1) * POOL_G)
            wg = w_ref[gi]
            dys_g = dys[:, sl].astype(BF)
            d_g = ds[gi][HALO:, :].astype(BF)
            ypre = _dot(d_g, wg)
            dscs.append(_rowsum(dye[:TM, sl] * ypre))
            wacc[gi] += _dot_tn(d_g, dys_g[:TM, :])
            dd = _dot_nt(dys_g, wg)
            cnt = jnp.minimum(pos_a + 1, w).astype(F32)
            a = dd / cnt
            k = 1
            while k < w:
                a = a + pltpu.roll(a, TM + HALO - k, 0)
                k *= 2
            dhs.append(a[:TM, :] - dd[:TM, :])
        dsc_ref[...] += jnp.concatenate(dscs, axis=1)
        dh = jnp.concatenate(dhs, axis=1)
        dxp, prod2 = _rms_bwd(xt, pg_ref[0:1, :], dh)
        dpg_ref[...] += _rowsum(prod2)
        dx0_ref[...] = dt + dxp

        @pl.when(last)
        def _():
            dw_ref[...] = wacc[...].astype(BF)

    return _call(
        body, name="mixa_bwd", grid=(n,), in_specs=[VSPEC] * 7,
        out_specs=[_row_spec(TM), _const_spec((ng, POOL_G, POOL_G)), _const_spec((1, D)),
                   _const_spec((1, D)), _const_spec((1, D))],
        out_shape=[_sds((s_len, D)), _sds((ng, POOL_G, POOL_G), BF), _sds((1, D)), _sds((1, D)), _sds((1, D))],
        scratch_shapes=[pltpu.VMEM((ng, POOL_G, POOL_G), F32)],
        args=[dx1, x, y, pre_g, pool_w, pool_scale, post_g], rider=rider)


def _ffn_fwd(layer, x1, pre_g, wgu, wd, post_g, rider=None):
    s_len = x1.shape[0]

    def body(x_ref, pg_ref, wgu_ref, wd_ref, qg_ref, f_ref, x2_ref, g_ref, u_ref):
        x = x_ref[...]
        h = _rms_fwd(x, pg_ref[layer:layer + 1, :]).astype(BF)
        f = jnp.zeros((TM, D), F32)
        for c in range(FF // FF_HALF):
            cols = slice(c * FF_HALF, (c + 1) * FF_HALF)
            g = _dot(h, wgu_ref[0, :, cols])
            u = _dot(h, wgu_ref[1, :, cols])
            g_ref[:, cols] = g.astype(BF)
            u_ref[:, cols] = u.astype(BF)
            act = g * _sigmoid(g) * u
            f = f + _dot(act.astype(BF), wd_ref[cols, :])
        f_ref[...] = f
        x2_ref[...] = x + _rms_fwd(f, qg_ref[layer:layer + 1, :])

    return _call(body, name=f"ffn_fwd{layer}", grid=(s_len // TM,),
                 in_specs=[_row_spec(TM), VSPEC, VSPEC, VSPEC, VSPEC],
                 out_specs=[_row_spec(TM), _row_spec(TM), _row_spec(TM, FF), _row_spec(TM, FF)],
                 out_shape=[_sds((s_len, D)), _sds((s_len, D)), _sds((s_len, FF), BF), _sds((s_len, FF), BF)],
                 args=[x1, pre_g, wgu, wd, post_g], rider=rider)


GU_PIECE = 128
DN_PIECE = 64
DN_SLOT = FF // N_CHIPS
HALF_D = D // 2


def _ffn_bwd(layer, dx2, x1, f, g_pre, u_pre, pre_g, wgu, wd, post_g, kc, rider=None):
    s_len = x1.shape[0]
    tm = TM_FFN_BWD
    n = s_len // tm
    nc = FF // FF_CHUNK
    n_gu, n_dn = FF_CHUNK // GU_PIECE, FF_CHUNK // DN_PIECE
    n_pieces = 2 * n_gu + n_dn
    n_blk = FF_HALF // GU_PIECE

    def edge_rows(c, i, kc_ref):
        return (jnp.where((c == 0) | (c == nc - 1), i, n - 1), 0)

    def chunk_at(c, kc_ref):
        return (c + (((kc_ref[0] + 1) % N_CHIPS) * nc) // N_CHIPS) % nc

    def exchange(kc_ref, c, accg, accu, accd, own_gu_ref, land_gu_ref, own_dn_ref, land_dn_ref,
                 pl_gu, pl_dn, sib_gu, sib_dn, mine_gu, mine_dn, sum_gu, sum_dn,
                 psend, precv, ssend, lsem, rrecv):
        x, y, core = lax.axis_index("x"), lax.axis_index("y"), lax.axis_index("c")
        lower = core == 0

        def pair_copy(cc, part):
            p = cc % 2
            src, dst = ((sib_gu, pl_gu), (sib_dn, pl_dn))[part]
            return pltpu.make_async_remote_copy(src.at[p], dst.at[cc], psend.at[p, part], precv.at[cc, part],
                                                device_id=(x, y, 1 - core), device_id_type=MESH)

        def scatter(cc, wait):
            p = cc % 2
            hidden = chunk_at(cc, kc_ref) * FF_CHUNK

            assert n_gu == 2
            k0, k1 = hidden // FF_HALF, (hidden + GU_PIECE) // FF_HALF
            blk = (hidden - k0 * FF_HALF) // GU_PIECE
            for gu in range(2):
                @pl.when(k0 == k1)
                def _():
                    piece(p, wait, 2 * gu, sum_gu.at[p, gu], k0 + 2 * gu, 0, (pl.ds(blk, 2),))

                @pl.when(k0 != k1)
                def _():
                    piece(p, wait, 2 * gu, sum_gu.at[p, gu, 0], k0 + 2 * gu, 0, (blk,))
                    piece(p, wait, 2 * gu + 1, sum_gu.at[p, gu, 1], k1 + 2 * gu, 0, (0,))

            kd = hidden // DN_SLOT
            off = pl.multiple_of(hidden - kd * DN_SLOT, DN_PIECE)
            m = jnp.minimum((DN_SLOT - off) // DN_PIECE, n_dn)
            for mm in range(1, n_dn + 1):
                @pl.when(m == mm)
                def _():
                    rows = mm * DN_PIECE
                    piece(p, wait, 2 * n_gu, sum_dn.at[p, pl.ds(0, rows), :], kd, 1, (pl.ds(off, rows), slice(None)))
                    if mm < n_dn:
                        piece(p, wait, 2 * n_gu + 1, sum_dn.at[p, pl.ds(rows, FF_CHUNK - rows), :], kd + 1, 1,
                              (pl.ds(0, FF_CHUNK - rows), slice(None)))

        def piece(p, wait, pi, src, k, t, where):
            own_ref, land_ref = ((own_gu_ref, land_gu_ref), (own_dn_ref, land_dn_ref))[t]
            kx, ky = k // 2, k % 2
            fx, fy = (kx != x).astype(jnp.int32), (ky != y).astype(jnp.int32)
            local = (fx + fy) == 0
            j = jnp.maximum(fx + 2 * fy - 1, 0)

            @pl.when(local)
            def _():
                cp = pltpu.make_async_copy(src, own_ref.at[where], lsem.at[p, pi])
                if wait:
                    cp.wait()
                else:
                    cp.start()

            @pl.when(jnp.logical_not(local))
            def _():
                cp = pltpu.make_async_remote_copy(src, land_ref.at[(j,) + where], ssend.at[p, pi],
                                                  rrecv.at[t, j], device_id=(kx, ky, core), device_id_type=MESH)
                if wait:
                    cp.wait_send()
                else:
                    cp.start()

        def add_and_scatter(cc):
            p = cc % 2
            pair_copy(cc, 0).wait_recv()
            pair_copy(cc, 1).wait_recv()
            s_gu = (mine_gu[...] + pl_gu[cc].astype(F32)).astype(BF)
            for hc in range(n_gu):
                sum_gu[p, :, hc] = s_gu[:, :, hc * GU_PIECE:(hc + 1) * GU_PIECE]
            sum_dn[p] = (mine_dn[...] + pl_dn[cc].astype(F32)).astype(BF)
            scatter(cc, wait=False)

        @pl.when(c >= 1)
        def _():
            @pl.when(c >= 3)
            def _():
                scatter(c - 3, wait=True)
            add_and_scatter(c - 1)

        @pl.when(c >= 2)
        def _():
            pair_copy(c - 2, 0).wait_send()
            pair_copy(c - 2, 1).wait_send()

        p = c % 2
        my_rows = pl.ds(pl.multiple_of(core * HALF_D, HALF_D), HALF_D)
        sib_rows = pl.ds(pl.multiple_of((1 - core) * HALF_D, HALF_D), HALF_D)
        d_v = accd[...]
        sib_gu[p, 0] = accg[sib_rows, :].astype(BF)
        sib_gu[p, 1] = accu[sib_rows, :].astype(BF)
        sib_dn[p] = jnp.where(lower, d_v[:, HALF_D:], d_v[:, :HALF_D]).astype(BF)
        mine_gu[0] = accg[my_rows, :]
        mine_gu[1] = accu[my_rows, :]
        mine_dn[...] = jnp.where(lower, d_v[:, :HALF_D], d_v[:, HALF_D:])
        pair_copy(c, 0).start()
        pair_copy(c, 1).start()

        @pl.when(c == nc - 1)
        def _():
            scatter(nc - 3, wait=True)
            add_and_scatter(nc - 1)
            for cc in (nc - 2, nc - 1):
                pair_copy(cc, 0).wait_send()
                pair_copy(cc, 1).wait_send()
                scatter(cc, wait=True)
            for t, land_ref in enumerate((land_gu_ref, land_dn_ref)):
                for j in range(N_CHIPS - 1):
                    pltpu.make_async_remote_copy(land_ref.at[j], land_ref.at[j], ssend.at[0, 0], rrecv.at[t, j],
                                                 device_id=(x, y, core), device_id_type=MESH).wait_recv()

    def body(kc_ref, dx_ref, x_ref, f_ref, gp_ref, up_ref, pg_ref, wgu_ref, wd_ref, qg_ref,
             dx1_ref, dpg_ref, dqg_ref, own_gu_ref, land_gu_ref, own_dn_ref, land_dn_ref,
             h_s, df_s, dh_s, accg, accu, accd, *comm):
        c = pl.program_id(0)
        i = pl.program_id(1)
        rows = pl.ds(pl.multiple_of(i * tm, tm), tm)
        pg = pg_ref[layer:layer + 1, :]

        @pl.when((c == 0) & (i == 0))
        def _():
            dpg_ref[...] = jnp.zeros_like(dpg_ref)
            dqg_ref[...] = jnp.zeros_like(dqg_ref)

        @pl.when(c == 0)
        def _():
            h_s[rows, :] = _rms_fwd(x_ref[...], pg).astype(BF)
            df, prod = _rms_bwd(f_ref[...], qg_ref[layer:layer + 1, :], dx_ref[...])
            df_s[rows, :] = df.astype(BF)
            dqg_ref[...] += _rowsum(prod)

        @pl.when(i == 0)
        def _():
            accg[...] = jnp.zeros_like(accg)
            accu[...] = jnp.zeros_like(accu)
            accd[...] = jnp.zeros_like(accd)

        h = h_s[rows, :]
        df = df_s[rows, :]
        wg = wgu_ref[0]
        wu = wgu_ref[1]
        g = gp_ref[...].astype(F32)
        u = up_ref[...].astype(F32)
        sg = _sigmoid(g)
        a = g * sg
        dact = _dot_nt(df, wd_ref[...])
        accd[...] += _dot_tn((a * u).astype(BF), df)
        du = (dact * a).astype(BF)
        dg = (dact * u * (sg * (1.0 + g * (1.0 - sg)))).astype(BF)
        accg[...] += _dot_tn(h, dg)
        accu[...] += _dot_tn(h, du)
        dh = _dot_nt(dg, wg) + _dot_nt(du, wu)

        @pl.when(c == 0)
        def _():
            dh_s[rows, :] = dh

        @pl.when((c > 0) & (c < nc - 1))
        def _():
            dh_s[rows, :] += dh

        @pl.when(c == nc - 1)
        def _():
            dxp, prod = _rms_bwd(x_ref[...], pg, dh_s[rows, :] + dh)
            dpg_ref[...] += _rowsum(prod)
            dx1_ref[...] = dx_ref[...] + dxp

        @pl.when(i == n - 1)
        def _():
            exchange(kc_ref, c, accg, accu, accd, own_gu_ref, land_gu_ref, own_dn_ref, land_dn_ref, *comm)

    dma = pltpu.SemaphoreType.DMA
    return _call(
        body, name=f"ffn_bwd{layer}", grid=(nc, n),
        in_specs=[pl.BlockSpec((tm, D), edge_rows), pl.BlockSpec((tm, D), edge_rows),
                  pl.BlockSpec((tm, D), lambda c, i, kc_ref: (jnp.where(c == 0, i, n - 1), 0),
                               pipeline_mode=pl.Buffered(1)),
                  pl.BlockSpec((tm, FF_CHUNK), lambda c, i, kc_ref: (i, chunk_at(c, kc_ref))),
                  pl.BlockSpec((tm, FF_CHUNK), lambda c, i, kc_ref: (i, chunk_at(c, kc_ref))),
                  VSPEC,
                  pl.BlockSpec((2, D, FF_CHUNK), lambda c, i, kc_ref: (0, 0, chunk_at(c, kc_ref))),
                  pl.BlockSpec((FF_CHUNK, D), lambda c, i, kc_ref: (chunk_at(c, kc_ref), 0)),
                  VSPEC],
        out_specs=[pl.BlockSpec((tm, D), lambda c, i, kc_ref: (jnp.where(c == nc - 1, i, 0), 0)),
                   _const_spec((1, D)), _const_spec((1, D)), ANYSPEC, ANYSPEC, ANYSPEC, ANYSPEC],
        out_shape=[_sds((s_len, D)), _sds((1, D)), _sds((1, D)),
                   _sds((n_blk, HALF_D, GU_PIECE), BF), _sds((N_CHIPS - 1, n_blk, HALF_D, GU_PIECE), BF),
                   _sds((DN_SLOT, HALF_D), BF), _sds((N_CHIPS - 1, DN_SLOT, HALF_D), BF)],
        scratch_shapes=[pltpu.VMEM((s_len, D), BF), pltpu.VMEM((s_len, D), BF), pltpu.VMEM((s_len, D), F32),
                        pltpu.VMEM((D, FF_CHUNK), F32), pltpu.VMEM((D, FF_CHUNK), F32),
                        pltpu.VMEM((FF_CHUNK, D), F32),
                        pltpu.VMEM((nc, 2, HALF_D, FF_CHUNK), BF), pltpu.VMEM((nc, FF_CHUNK, HALF_D), BF),
                        pltpu.VMEM((2, 2, HALF_D, FF_CHUNK), BF), pltpu.VMEM((2, FF_CHUNK, HALF_D), BF),
                        pltpu.VMEM((2, HALF_D, FF_CHUNK), F32), pltpu.VMEM((FF_CHUNK, HALF_D), F32),
                        pltpu.VMEM((2, 2, n_gu, HALF_D, GU_PIECE), BF), pltpu.VMEM((2, FF_CHUNK, HALF_D), BF),
                        dma((2, 2)), dma((nc, 2)), dma((2, n_pieces)), dma((2, n_pieces)), dma((2, N_CHIPS - 1))],
        args=[dx2, x1, f, g_pre, u_pre, pre_g, wgu, wd, post_g], rider=rider, prefetch=kc)


def _ple_fwd(layer, x2, p, ple_g, w_gate, w_proj, post_g, target=None, qkv=None, rider=None):
    s_len = x2.shape[0]
    final = target is not None
    assert not (final and qkv)

    def body(*refs):
        if final:
            x_ref, p_ref, g_ref, wg_ref, wp_ref, qg_ref, t_ref, z_ref, pe_ref, dx_ref, lv_ref = refs
        elif qkv:
            (x_ref, p_ref, g_ref, wg_ref, wp_ref, qg_ref, ng_ref, kg_ref, wq_ref, wkv_ref,
             z_ref, pe_ref, x3_ref, q_ref, kv_ref) = refs
        else:
            x_ref, p_ref, g_ref, wg_ref, wp_ref, qg_ref, z_ref, pe_ref, x3_ref = refs
        x = x_ref[...]
        r = _rms_fwd(x, g_ref[layer:layer + 1, :]).astype(BF)
        z = _dot(r, wg_ref[...])
        pe = _dot(p_ref[...].astype(BF), wp_ref[...])
        z_ref[...] = z
        pe_ref[...] = pe
        x3 = x + _rms_fwd(pe * _sigmoid(z), qg_ref[layer:layer + 1, :])
        if final:
            @pl.when(pl.program_id(0) == 0)
            def _():
                lv_ref[...] = jnp.zeros_like(lv_ref)
            err = x3 - t_ref[...]
            dx_ref[...] = err * (1.0 / D)
            lv_ref[...] += _rowsum(err * err)
        else:
            x3_ref[...] = x3
        if qkv:
            q_ref[...] = _dot(_rms_fwd(x3, ng_ref[layer + 1:layer + 2, :]).astype(BF), wq_ref[...]).astype(BF)
            kv_ref[...] = _dot(_rms_fwd(x3, kg_ref[...]).astype(BF), wkv_ref[...]).astype(BF)

    p_spec = pl.BlockSpec((None, TM, PLE), lambda i: (layer, i, 0))
    in_specs = [_row_spec(TM), p_spec, VSPEC, VSPEC, VSPEC, VSPEC]
    args = [x2, p, ple_g, w_gate, w_proj, post_g]
    out_specs = [_row_spec(TM), _row_spec(TM), _row_spec(TM)]
    out_shape = [_sds((s_len, D))] * 3
    if qkv:
        in_specs += [VSPEC] * 4
        args += list(qkv)
        out_specs += [_row_spec(TM), _row_spec(TM, 2 * KVD)]
        out_shape += [_sds((s_len, D), BF), _sds((s_len, 2 * KVD), BF)]
    if final:
        in_specs.append(_row_spec(TM))
        args.append(target)
        out_specs.append(_const_spec((1, D)))
        out_shape.append(_sds((1, D)))
    return _call(body, name=f"ple_fwd{layer}", grid=(s_len // TM,), in_specs=in_specs, out_specs=out_specs,
                 out_shape=out_shape, args=args, rider=rider)


def _ple_bwd(layer, dx3, x2, z, pe, p, ple_g, w_gate, post_g, rider=None):
    s_len = x2.shape[0]
    n = s_len // TM

    def body(dx_ref, x_ref, z_ref, pe_ref, p_ref, g_ref, wg_ref, qg_ref,
             dx2_ref, dwg_ref, dwp_ref, dg_ref, dqg_ref, gacc, pacc):
        i = pl.program_id(0)

        @pl.when(i == 0)
        def _():
            gacc[...] = jnp.zeros_like(gacc)
            pacc[...] = jnp.zeros_like(pacc)
            dg_ref[...] = jnp.zeros_like(dg_ref)
            dqg_ref[...] = jnp.zeros_like(dqg_ref)

        dx = dx_ref[...]
        x = x_ref[...]
        pe_v = pe_ref[...]
        gate = _sigmoid(z_ref[...])
        de, prod = _rms_bwd(pe_v * gate, qg_ref[layer:layer + 1, :], dx)
        dqg_ref[...] += _rowsum(prod)
        dpe = (de * gate).astype(BF)
        dz = (de * pe_v * gate * (1.0 - gate)).astype(BF)
        pacc[...] += _dot_tn(p_ref[...].astype(BF), dpe)
        g = g_ref[layer:layer + 1, :]
        r = _rms_fwd(x, g).astype(BF)
        gacc[...] += _dot_tn(r, dz)
        dr = _dot_nt(dz, wg_ref[...])
        dxp, prod2 = _rms_bwd(x, g, dr)
        dg_ref[...] += _rowsum(prod2)
        dx2_ref[...] = dx + dxp

        @pl.when(i == n - 1)
        def _():
            dwg_ref[...] = gacc[...].astype(BF)
            dwp_ref[...] = pacc[...].astype(BF)

    p_spec = pl.BlockSpec((None, TM, PLE), lambda i: (layer, i, 0))
    return _call(
        body, name=f"ple_bwd{layer}", grid=(n,),
        in_specs=[_row_spec(TM), _row_spec(TM), _row_spec(TM), _row_spec(TM), p_spec, VSPEC, VSPEC, VSPEC],
        out_specs=[_row_spec(TM), _const_spec((D, D)), _const_spec((PLE, D)), _const_spec((1, D)), _const_spec((1, D))],
        out_shape=[_sds((s_len, D)), _sds((D, D), BF), _sds((PLE, D), BF), _sds((1, D)), _sds((1, D))],
        scratch_shapes=[pltpu.VMEM((D, D), F32), pltpu.VMEM((PLE, D), F32)],
        args=[dx3, x2, z, pe, p, ple_g, w_gate, post_g], rider=rider)


def _qkv_bwd(dq, dkv, x3, dx4, q_g, kv_g, w_q, w_kv):
    s_len = x3.shape[0]
    n = s_len // TM

    def body(dq_ref, dkv_ref, x_ref, dx_ref, qg_ref, kg_ref, wq_ref, wkv_ref,
             dx3_ref, dwq_ref, dwkv_ref, dqg_ref, dkg_ref, qacc, kacc):
        i = pl.program_id(0)

        @pl.when(i == 0)
        def _():
            qacc[...] = jnp.zeros_like(qacc)
            kacc[...] = jnp.zeros_like(kacc)
            dqg_ref[...] = jnp.zeros_like(dqg_ref)
            dkg_ref[...] = jnp.zeros_like(dkg_ref)

        x = x_ref[...]
        qg = qg_ref[1:2, :]
        kg = kg_ref[...]
        dq_v = dq_ref[...]
        dkv_v = dkv_ref[...].astype(BF)
        qacc[...] += _dot_tn(_rms_fwd(x, qg).astype(BF), dq_v)
        kacc[...] += _dot_tn(_rms_fwd(x, kg).astype(BF), dkv_v)
        dxq, prod_q = _rms_bwd(x, qg, _dot_nt(dq_v, wq_ref[...]))
        dxk, prod_k = _rms_bwd(x, kg, _dot_nt(dkv_v, wkv_ref[...]))
        dqg_ref[...] += _rowsum(prod_q)
        dkg_ref[...] += _rowsum(prod_k)
        dx3_ref[...] = dx_ref[...] + dxq + dxk

        @pl.when(i == n - 1)
        def _():
            dwq_ref[...] = qacc[...].astype(BF)
            dwkv_ref[...] = kacc[...].astype(BF)

    outs, _ = _call(
        body, name="qkv_bwd", grid=(n,),
        in_specs=[_row_spec(TM), _row_spec(TM, 2 * KVD), _row_spec(TM), _row_spec(TM), VSPEC, VSPEC, VSPEC, VSPEC],
        out_specs=[_row_spec(TM), _const_spec((D, D)), _const_spec((D, 2 * KVD)),
                   _const_spec((1, D)), _const_spec((1, D))],
        out_shape=[_sds((s_len, D)), _sds((D, D), BF), _sds((D, 2 * KVD), BF), _sds((1, D)), _sds((1, D))],
        scratch_shapes=[pltpu.VMEM((D, D), F32), pltpu.VMEM((D, 2 * KVD), F32)],
        args=[dq, dkv, x3, dx4, q_g, kv_g, w_q, w_kv])
    return outs


def _attn_group(i, q, kvw, sink_ref, g):
    rows = GQA * BLK
    heads = [GQA * g + j for j in range(GQA)]
    off = jnp.where(i > 0, BLK, 0)
    row = lax.broadcasted_iota(jnp.int32, (rows, 2 * BLK), 0)
    rel = (row % BLK) - lax.broadcasted_iota(jnp.int32, (rows, 2 * BLK), 1) + off
    valid = (rel >= 0) & (rel < BLK)
    head_of_row = lax.broadcasted_iota(jnp.int32, (rows, 1), 0) // BLK
    slope = jnp.zeros((rows, 1), F32)
    sink = jnp.zeros((rows, 1), F32)
    for j, h in enumerate(heads):
        slope = jnp.where(head_of_row == j, SLOPES[h], slope)
        sink = jnp.where(head_of_row == j, sink_ref[0, h], sink)
    qs = jnp.concatenate([q[:, h * HEAD_DIM:(h + 1) * HEAD_DIM] for h in heads], axis=0)
    k = kvw[:, g * HEAD_DIM:(g + 1) * HEAD_DIM]
    v = kvw[:, KVD + g * HEAD_DIM:KVD + (g + 1) * HEAD_DIM]
    s = _dot_nt(qs, k) * ATT_SCALE - slope * rel.astype(F32)
    s = jnp.where(valid, s, NEG_INF)
    m = jnp.maximum(jnp.max(s, axis=-1, keepdims=True), sink)
    e = jnp.exp(s - m)
    es = jnp.exp(sink - m)
    inv = 1.0 / (jnp.sum(e, axis=-1, keepdims=True) + es)
    return e * inv, es * inv, qs, k, v


def _unstack_heads(stacked):
    return [stacked[j * BLK:(j + 1) * BLK, :] for j in range(GQA)]


def _kv_window(kv_ref, i):
    ks = pl.multiple_of(jnp.maximum(i * BLK - BLK, 0), BLK)
    return ks, kv_ref[pl.ds(ks, 2 * BLK), :]


def _attn_fwd(q, kv, sinks, x3, w_o, post_g, rider=None):
    s_len = q.shape[0]

    def body(q_ref, kv_ref, sk_ref, x_ref, wo_ref, g_ref, a_ref, y_ref, x4_ref):
        i = pl.program_id(0)
        _, kvw = _kv_window(kv_ref, i)
        q = q_ref[...]
        outs = []
        for g in range(N_KV_HEADS):
            p, _, _, _, v = _attn_group(i, q, kvw, sk_ref, g)
            outs += _unstack_heads(_dot(p.astype(BF), v))
        attn = jnp.concatenate(outs, axis=1)
        a_ref[...] = attn
        y = _dot(attn.astype(BF), wo_ref[...])
        y_ref[...] = y
        x4_ref[...] = x_ref[...] + _rms_fwd(y, g_ref[1:2, :])

    return _call(body, name="attn_fwd", grid=(s_len // BLK,),
                 in_specs=[_row_spec(BLK), VSPEC, SSPEC, _row_spec(BLK), VSPEC, VSPEC],
                 out_specs=[_row_spec(BLK)] * 3, out_shape=[_sds((s_len, D))] * 3,
                 args=[q, kv, sinks, x3, w_o, post_g], rider=rider)


def _attn_bwd(dx4, y, attn, q, kv, sinks, w_o, post_g, rider=None):
    s_len = q.shape[0]
    n = s_len // BLK

    def body(dx_ref, y_ref, a_ref, q_ref, kv_ref, sk_ref, wo_ref, g_ref,
             dq_ref, dkv_ref, dwo_ref, dg_ref, dsk_ref, wacc):
        i = pl.program_id(0)

        @pl.when(i == 0)
        def _():
            dkv_ref[...] = jnp.zeros_like(dkv_ref)
            wacc[...] = jnp.zeros_like(wacc)
            dg_ref[...] = jnp.zeros_like(dg_ref)
            dsk_ref[...] = jnp.zeros_like(dsk_ref)

        dy, prod = _rms_bwd(y_ref[...], g_ref[1:2, :], dx_ref[...])
        dg_ref[...] += _rowsum(prod)
        dyb = dy.astype(BF)
        attn = a_ref[...]
        wacc[...] += _dot_tn(attn.astype(BF), dyb)
        d_o = _dot_nt(dyb, wo_ref[...])
        dod = d_o * attn
        ks, kvw = _kv_window(kv_ref, i)
        q = q_ref[...]
        lane = lax.broadcasted_iota(jnp.int32, (1, D), 1)
        dqs, dks, dvs = [], [], []
        dsk = jnp.zeros((1, D), F32)
        for g in range(N_KV_HEADS):
            p, ps, qs, k, v = _attn_group(i, q, kvw, sk_ref, g)
            cols = [slice((GQA * g + j) * HEAD_DIM, (GQA * g + j + 1) * HEAD_DIM) for j in range(GQA)]
            do_s = jnp.concatenate([d_o[:, c] for c in cols], axis=0).astype(BF)
            dsum = jnp.concatenate([jnp.sum(dod[:, c], axis=-1, keepdims=True) for c in cols], axis=0)
            dp = _dot_nt(do_s, v)
            dsb = (p * (dp - dsum) * ATT_SCALE).astype(BF)
            sink_part = ps * dsum
            for j in range(GQA):
                dsk = dsk + jnp.where(lane == GQA * g + j, -_rowsum(sink_part[j * BLK:(j + 1) * BLK, :]), 0.0)
            dqs += _unstack_heads(_dot(dsb, k))
            dks.append(_dot_tn(dsb, qs))
            dvs.append(_dot_tn(p.astype(BF), do_s))
        dsk_ref[...] += dsk
        dq_ref[...] = jnp.concatenate(dqs, axis=1).astype(BF)
        dkv_ref[pl.ds(ks, 2 * BLK), :] += jnp.concatenate(dks + dvs, axis=1)

        @pl.when(i == n - 1)
        def _():
            dwo_ref[...] = wacc[...].astype(BF)

    return _call(
        body, name="attn_bwd", grid=(n,),
        in_specs=[_row_spec(BLK), _row_spec(BLK), _row_spec(BLK), _row_spec(BLK), VSPEC, SSPEC, VSPEC, VSPEC],
        out_specs=[_row_spec(BLK), _const_spec((s_len, 2 * KVD)), _const_spec((D, D)),
                   _const_spec((1, D)), _const_spec((1, D))],
        out_shape=[_sds((s_len, D), BF), _sds((s_len, 2 * KVD)), _sds((D, D), BF), _sds((1, D)), _sds((1, D))],
        scratch_shapes=[pltpu.VMEM((D, D), F32)],
        args=[dx4, y, attn, q, kv, sinks, w_o, post_g], rider=rider)


Big = collections.namedtuple("Big", "name src layer L A R C rb")


def _bigs():
    out = {"pool_w": Big("pool_w", "pool_w", None, 4, 4, POOL_G // N_CHIPS, POOL_G, 32)}
    for l in range(2):
        out[f"w_gu{l}"] = Big(f"w_gu{l}", "w_gu", l, 1, 2, D, FF_HALF, 256)
        out[f"w_down{l}"] = Big(f"w_down{l}", "w_down", l, 1, 4, FF // N_CHIPS, D, 352)
        out[f"w_ple_gate{l}"] = Big(f"w_ple_gate{l}", "w_ple_gate", l, 1, 4, D // N_CHIPS, D, 128)
        out[f"w_ple_proj{l}"] = Big(f"w_ple_proj{l}", "w_ple_proj", l, 1, 1, PLE, D // N_CHIPS, 128)
    out["w_q"] = Big("w_q", "w_q", None, 1, 4, D // N_CHIPS, D, 128)
    out["w_o"] = Big("w_o", "w_o", None, 1, 4, D // N_CHIPS, D, 128)
    out["w_kv"] = Big("w_kv", "w_kv", None, 1, 4, D // N_CHIPS, 2 * KVD, 128)
    return out


BIGS = _bigs()
POOL_SCALE = Big("pool_scale", "pool_scale", None, 1, 1, 1, D // N_CHIPS, 1)
BIG_SOURCES = ("w_gu", "w_down", "w_ple_gate", "w_ple_proj", "w_q", "w_o", "w_kv", "pool_w")


def _ncb(t):
    return N_CHIPS // t.A


def _full_shape(t, rows=None):
    return (t.L, t.A, t.R if rows is None else rows, _ncb(t) * t.C)


def _slot_index(t, k):
    return k // _ncb(t), k % _ncb(t)


def _slot(ref, t, k, row0, rows):
    a, cb = _slot_index(t, k)
    return ref.at[:, a, pl.ds(row0, rows), pl.ds(pl.multiple_of(cb * t.C, 128), t.C)]


def _place(t, w, kc, out_dtype):
    rb = min(t.R, 2 * t.rb)

    def body(kc_ref, w_ref, o_ref):
        del kc_ref
        o_ref[...] = w_ref[...].astype(out_dtype)

    def in_map(l, j, kc_ref):
        return (l if t.layer is None else t.layer, j, 0)

    def out_map(l, j, kc_ref):
        a, cb = _slot_index(t, kc_ref[0])
        return (l, a, j, cb)

    return pl.pallas_call(
        body, name=f"place_{t.name}",
        grid_spec=pltpu.PrefetchScalarGridSpec(
            num_scalar_prefetch=1, grid=(t.L, t.R // rb),
            in_specs=[pl.BlockSpec((None, rb, t.C), in_map)],
            out_specs=pl.BlockSpec((None, None, rb, t.C), out_map)),
        out_shape=_sds(_full_shape(t), out_dtype),
        compiler_params=_params(2),
    )(kc, w)


def _place_many(ts, ws, kc, rider):
    n = 8

    def blocks_of(t):
        return next(nb for nb in (8, 4, 2, 1) if t.R % (16 * nb) == 0)

    def body(kc_ref, *refs):
        del kc_ref
        s = pl.program_id(0)
        for ti, t in enumerate(ts):
            @pl.when(s < blocks_of(t))
            def _():
                refs[len(ts) + ti][...] = refs[ti][...].astype(BF)

    in_specs, out_specs = [], []
    for t in ts:
        assert t.L == 1
        nb = blocks_of(t)
        rb = t.R // nb

        def in_map(s, kc_ref, t=t, nb=nb):
            return (0 if t.layer is None else t.layer, jnp.minimum(s, nb - 1), 0)

        def out_map(s, kc_ref, t=t, nb=nb):
            a, cb = _slot_index(t, kc_ref[0])
            return (0, a, jnp.minimum(s, nb - 1), cb)

        in_specs.append(pl.BlockSpec((None, rb, t.C), in_map))
        out_specs.append(pl.BlockSpec((None, None, rb, t.C), out_map))
    return _call(body, name="place_rest", grid=(n,), in_specs=in_specs, out_specs=out_specs,
                 out_shape=[_sds(_full_shape(t), BF) for t in ts], args=list(ws), rider=rider, prefetch=kc)


def _mesh_position():
    x, y, c = lax.axis_index("x"), lax.axis_index("y"), lax.axis_index("c")
    chips = [(1 - x, y), (x, 1 - y), (1 - x, 1 - y)]
    return x, y, c, chips


def _gather_rider(parts, fulls):
    nt = len(parts)
    TO_X, TO_Y, FWD_X, FWD_Y, SIB_X, SIB_Y, SIB_D = range(7)

    def rows_of(ti, core):
        t, r0, r1 = parts[ti]
        h = (r1 - r0) // 2
        return r0 + core * h, h

    def copy(outs, sems, kind, ti, k_src, row0, rows, dev):
        region = _slot(outs[ti], parts[ti][0], k_src, row0, rows)
        return pltpu.make_async_remote_copy(region, region, sems[0].at[ti, kind], sems[1].at[ti, kind],
                                            device_id=dev, device_id_type=MESH)

    def plan(outs, sems):
        x, y, c, _ = _mesh_position()
        me, kx, ky, kd = 2 * x + y, 2 * (1 - x) + y, 2 * x + (1 - y), 2 * (1 - x) + (1 - y)
        dev_x, dev_y, dev_d, sib = (1 - x, y, c), (x, 1 - y, c), (1 - x, 1 - y, c), (x, y, 1 - c)

        def whole(ti):
            return 0, parts[ti][0].R

        def mk(kind, k_send, k_recv, dev, send_rows, recv_rows):
            def build(ti, side):
                k_src = k_send if side == "s" else k_recv
                row0, rows = (send_rows if side == "s" else recv_rows)(ti)
                return copy(outs, sems, kind, ti, k_src, row0, rows, dev)
            return build

        def first_half(core):
            return lambda ti: (rows_of(ti, core)[0], rows_of(ti, core)[1] // 2)

        def second_half(core):
            return lambda ti: (rows_of(ti, core)[0] + rows_of(ti, core)[1] // 2, rows_of(ti, core)[1] // 2)

        mine = lambda ti: rows_of(ti, c)
        theirs = lambda ti: rows_of(ti, 1 - c)
        split = {
            TO_X: mk(TO_X, me, kx, dev_x, mine, mine),
            TO_Y: mk(TO_Y, me, ky, dev_y, mine, mine),
            FWD_X: mk(FWD_X, ky, kd, dev_x, first_half(c), first_half(c)),
            FWD_Y: mk(FWD_Y, kx, kd, dev_y, second_half(c), second_half(c)),
            SIB_X: mk(SIB_X, kx, kx, sib, mine, theirs),
            SIB_Y: mk(SIB_Y, ky, ky, sib, mine, theirs),
            SIB_D: mk(SIB_D, kd, kd, sib, mine, theirs),
        }
        direct = {
            TO_X: mk(TO_X, me, kx, dev_x, whole, whole),
            TO_Y: mk(TO_Y, me, ky, dev_y, whole, whole),
            FWD_X: mk(FWD_X, me, kd, dev_d, whole, whole),
        }
        return split, direct

    is_split = [t.R > 1 for t, _, _ in parts]

    def start(ins, outs, sems):
        split, direct = plan(outs, sems)
        for ti in range(nt):
            kinds = split if is_split[ti] else direct
            kinds[TO_X](ti, "s").start()
            kinds[TO_Y](ti, "s").start()
            if not is_split[ti]:
                kinds[FWD_X](ti, "s").start()

    def mid(ins, outs, sems):
        split, _ = plan(outs, sems)
        for ti in range(nt):
            if is_split[ti]:
                split[TO_Y](ti, "r").wait_recv()
                split[FWD_X](ti, "s").start()
                split[SIB_Y](ti, "s").start()
        for ti in range(nt):
            if is_split[ti]:
                split[TO_X](ti, "r").wait_recv()
                split[FWD_Y](ti, "s").start()
                split[SIB_X](ti, "s").start()

    def finish(ins, outs, sems):
        split, direct = plan(outs, sems)
        for ti in range(nt):
            if is_split[ti]:
                split[FWD_X](ti, "r").wait_recv()
                split[FWD_Y](ti, "r").wait_recv()
                split[SIB_D](ti, "s").start()
            else:
                for kind in (TO_X, TO_Y, FWD_X):
                    direct[kind](ti, "r").wait_recv()
        for ti in range(nt):
            if is_split[ti]:
                for kind in (SIB_X, SIB_Y, SIB_D):
                    split[kind](ti, "r").wait_recv()
        for ti in range(nt):
            kinds = split if is_split[ti] else direct
            for kind in kinds:
                kinds[kind](ti, "s").wait_send()

    sems = pltpu.SemaphoreType.DMA((nt, 7))
    return Rider(list(fulls), [_sds(a.shape, a.dtype) for a in fulls], {i: i for i in range(nt)},
                 [sems, sems], start, mid, finish)


def _pair_exchange(name, specs, grads):
    nt = len(specs)

    def body(*refs):
        gs = refs[:nt]
        lands = refs[nt:2 * nt]
        send, recv = refs[2 * nt:]
        x, y, c, _ = _mesh_position()
        cps = []
        for ti, t in enumerate(specs):
            half = t.R // 2
            cp = pltpu.make_async_remote_copy(gs[ti].at[:, :, pl.ds((1 - c) * half, half), :], lands[ti],
                                              send.at[ti], recv.at[ti],
                                              device_id=(x, y, 1 - c), device_id_type=MESH)
            cp.start()
            cps.append(cp)
        for cp in cps:
            cp.wait()

    return pl.pallas_call(
        body, name=name,
        in_specs=[ANYSPEC] * nt, out_specs=[ANYSPEC] * nt,
        out_shape=[_sds(_full_shape(t, t.R // 2), BF) for t in specs],
        scratch_shapes=[pltpu.SemaphoreType.DMA((nt,)), pltpu.SemaphoreType.DMA((nt,))],
        compiler_params=_params(),
    )(*grads)


def _pair_sum_job(t, g, land):
    assert t.L == 1
    half = t.R // 2
    nj = half // t.rb
    block = (None, t.A, t.rb, _ncb(t) * t.C)

    def fn(j, kc_ref, ins, outs):
        outs[0][...] = (ins[0][...].astype(F32) + ins[1][...].astype(F32)).astype(BF)

    return Job(nj,
               [(g, block, lambda j, kc_ref: (0, 0, kc_ref[1] * nj + j, 0)),
                (land, block, lambda j, kc_ref: (0, 0, j, 0))],
               [(_sds(_full_shape(t, half), BF), block, lambda j, kc_ref: (0, 0, j, 0))], fn)


def _scatter_rider(specs, sums):
    nt = len(specs)

    def copy(ins, outs, sems, ti, j, chip, c):
        t = specs[ti]
        cx, cy = chip
        return pltpu.make_async_remote_copy(_slot(ins[ti], t, 2 * cx + cy, 0, t.R // 2), outs[ti].at[j],
                                            sems[0].at[ti, j], sems[1].at[ti, j],
                                            device_id=(cx, cy, c), device_id_type=MESH)

    def start(ins, outs, sems):
        _, _, c, chips = _mesh_position()
        for j, chip in enumerate(chips):
            for ti in range(nt):
                copy(ins, outs, sems, ti, j, chip, c).start()

    def finish(ins, outs, sems):
        _, _, c, chips = _mesh_position()
        for j, chip in enumerate(chips):
            for ti in range(nt):
                copy(ins, outs, sems, ti, j, chip, c).wait()

    sems = pltpu.SemaphoreType.DMA((nt, N_CHIPS - 1))
    return Rider(list(sums), [_sds((N_CHIPS - 1, t.L, t.R // 2, t.C), BF) for t in specs], {}, [sems, sems],
                 start, None, finish)


def _chip_sum_job(ts, landed):
    t0 = ts[0]
    assert t0.L == 1
    half = t0.R // 2
    nj = half // t0.rb

    def local(j, li):
        return jnp.clip(j - li * nj, 0, nj - 1)

    ins = []
    for li, t in enumerate(ts):
        s, land = landed[t.name]

        def own_map(j, kc_ref, li=li, t=t):
            a, cb = _slot_index(t, kc_ref[0])
            return (0, a, local(j, li), cb)

        ins.append((s, (None, None, t.rb, t.C), own_map))
        ins.append((land, (N_CHIPS - 1, None, t.rb, t.C), lambda j, kc_ref, li=li: (0, 0, local(j, li), 0)))

    def fn(j, kc_ref, in_refs, outs):
        for li in range(len(ts)):
            @pl.when(j // nj == li)
            def _():
                acc = in_refs[2 * li][...].astype(F32)
                for k in range(N_CHIPS - 1):
                    acc = acc + in_refs[2 * li + 1][k].astype(F32)
                outs[0][...] = acc

    return Job(len(ts) * nj, ins,
               [(_sds((len(ts), t0.R, t0.C)), (None, t0.rb, t0.C),
                 lambda j, kc_ref: (j // nj, kc_ref[1] * nj + j % nj, 0))], fn)


def _adamw_job(rb, w, g, m, v):
    n_layers, r, c = w.shape
    nb = r // rb
    block = (None, rb, c)
    index = lambda j, kc_ref: (j // nb, j % nb, 0)

    def fn(j, kc_ref, ins, outs):
        g_v = ins[1][...]
        outs[0][...] = g_v
        outs[1][...], outs[2][...], outs[3][...] = _adamw_math(ins[0][...], g_v, ins[2][...], ins[3][...])

    return Job(n_layers * nb, [(a, block, index) for a in (w, g, m, v)],
               [(_sds(w.shape), block, index)] * 4, fn)


def _chip_sum_fused_job(ts, fused, by_cols):
    t0 = ts[0]
    own0 = fused[t0.name][0]
    if by_cols:
        rows, cols = own0.shape
    else:
        nb, rows, bw = own0.shape
        cols = nb * bw
    nj = rows // t0.rb

    def local(j, li):
        return jnp.clip(j - li * nj, 0, nj - 1)

    ins = []
    for li, t in enumerate(ts):
        own, land = fused[t.name]
        if by_cols:
            ins.append((own, (t.rb, cols), lambda j, kc_ref, li=li: (local(j, li), 0)))
            ins.append((land, (N_CHIPS - 1, t.rb, cols), lambda j, kc_ref, li=li: (0, local(j, li), 0)))
        else:
            ins.append((own, (nb, t.rb, bw), lambda j, kc_ref, li=li: (0, local(j, li), 0)))
            ins.append((land, (N_CHIPS - 1, nb, t.rb, bw), lambda j, kc_ref, li=li: (0, 0, local(j, li), 0)))

    def fn(j, kc_ref, in_refs, outs):
        for li in range(len(ts)):
            @pl.when(j // nj == li)
            def _():
                acc = in_refs[2 * li][...].astype(F32)
                for k in range(N_CHIPS - 1):
                    acc = acc + in_refs[2 * li + 1][k].astype(F32)
                outs[0][...] = acc if by_cols else jnp.concatenate([acc[b] for b in range(nb)], axis=1)

    def out_map(j, kc_ref):
        return (j // nj, j % nj, kc_ref[1]) if by_cols else (j // nj, kc_ref[1] * nj + j % nj, 0)

    return Job(len(ts) * nj, ins, [(_sds((len(ts), t0.R, t0.C)), (None, t0.rb, cols), out_map)], fn)


def _pair_share(halves, by_cols):
    nt = len(halves)

    def part(ref, ti, core):
        axis = 2 if by_cols[ti] else 1
        half = halves[ti].shape[axis] // 2
        piece = pl.ds(pl.multiple_of(core * half, 128 if by_cols[ti] else 8), half)
        return ref.at[:, :, piece] if by_cols[ti] else ref.at[:, piece, :]

    def body(*refs):
        outs = refs[nt:2 * nt]
        send, recv = refs[2 * nt:]
        x, y, c, _ = _mesh_position()
        cps = []
        for ti in range(nt):
            mine = part(outs[ti], ti, c)
            cp = pltpu.make_async_remote_copy(mine, mine, send.at[ti], recv.at[ti],
                                              device_id=(x, y, 1 - c), device_id_type=MESH)
            cp.start()
            cps.append(cp)
        for ti in range(nt):
            theirs = part(outs[ti], ti, 1 - c)
            pltpu.make_async_remote_copy(theirs, theirs, send.at[ti], recv.at[ti],
                                         device_id=(x, y, 1 - c), device_id_type=MESH).wait_recv()
        for cp in cps:
            cp.wait_send()

    return pl.pallas_call(
        body, name="grads_pair_share",
        in_specs=[ANYSPEC] * nt, out_specs=[ANYSPEC] * nt,
        out_shape=[_sds(a.shape, a.dtype) for a in halves],
        scratch_shapes=[pltpu.SemaphoreType.DMA((nt,)), pltpu.SemaphoreType.DMA((nt,))],
        input_output_aliases={i: i for i in range(nt)},
        compiler_params=_params(),
    )(*halves)


def _adamw_math(w, g, m, v):
    m = B1 * m + (1.0 - B1) * g
    v = B2 * v + (1.0 - B2) * (g * g)
    delta = -LR * ((m / BC1) / (jnp.sqrt(v / BC2) + AEPS) + WD * w)
    return delta, m, v


def _adamw(name, rb, w, g, m, v):
    n_layers, r, c = w.shape

    def body(w_ref, g_ref, m_ref, v_ref, go_ref, d_ref, nm_ref, nv_ref):
        g_v = g_ref[...]
        go_ref[...] = g_v
        d_ref[...], nm_ref[...], nv_ref[...] = _adamw_math(w_ref[...], g_v, m_ref[...], v_ref[...])

    spec = pl.BlockSpec((None, rb, c), lambda l, j: (l, j, 0))
    return pl.pallas_call(
        body, name=f"adamw_{name}", grid=(n_layers, r // rb),
        in_specs=[spec] * 4, out_specs=[spec] * 4, out_shape=[_sds(w.shape)] * 4,
        compiler_params=_params(2),
    )(w, g, m, v)


GAIN_ROWS = {"pre_mix_g": 0, "post_mix_g": 2, "pre_ffn_g": 4, "post_ffn_g": 6, "ple_g": 8, "ple_post_g": 10}
ROW_KV_G, ROW_POOL_SCALE, ROW_SINKS, ROW_LOSS, PACK_ROWS = 12, 13, 14, 15, 16
SMALL_NAMES = tuple(GAIN_ROWS) + ("kv_g", "pool_scale", "sinks")


def _small_all_reduce(rows, dpool, rider=None):
    ng, pr = len(WINDOWS), POOL_G // N_CHIPS

    def body(*refs):
        row_refs = refs[:PACK_ROWS]
        dpool_ref, tot_ref, gpool_ref, pack, land, pland, send, recv, psend, precv = refs[PACK_ROWS:]
        x, y, c, _ = _mesh_position()
        me = 4 * x + 2 * y + c
        for r in range(PACK_ROWS):
            pack[r:r + 1, :] = row_refs[r][...]

        def shard_of(k):
            return dpool_ref.at[:, pl.ds(pl.multiple_of(k * pr, pr), pr), :]

        cps = []
        for j in range(1, N_DEV):
            px, py, pc = x ^ (j >> 2), y ^ ((j >> 1) & 1), c ^ (j & 1)
            cps.append(pltpu.make_async_remote_copy(pack, land.at[me], send.at[j], recv.at[j],
                                                    device_id=(px, py, pc), device_id_type=MESH))
            cps.append(pltpu.make_async_remote_copy(shard_of(2 * px + py), pland.at[me], psend.at[j], precv.at[j],
                                                    device_id=(px, py, pc), device_id_type=MESH))
        for cp in cps:
            cp.start()
        land[me] = pack[...]
        pland[me] = dpool_ref[:, pl.ds(pl.multiple_of((2 * x + y) * pr, pr), pr), :]
        for j in range(1, N_DEV):
            pltpu.make_async_remote_copy(pack, land.at[me ^ j], send.at[j], recv.at[j],
                                         device_id=(x, y, c), device_id_type=MESH).wait_recv()
            pltpu.make_async_remote_copy(shard_of(0), pland.at[me ^ j], psend.at[j], precv.at[j],
                                         device_id=(x, y, c), device_id_type=MESH).wait_recv()
        for cp in cps:
            cp.wait_send()
        tot = land[0]
        gp = pland[0].astype(F32)
        for d in range(1, N_DEV):
            tot = tot + land[d]
            gp = gp + pland[d].astype(F32)
        tot_ref[...] = tot
        gpool_ref[...] = gp

    sems = pltpu.SemaphoreType.DMA((N_DEV,))
    return _call(
        body, name="small_all_reduce", grid=(1,),
        in_specs=[VSPEC] * (PACK_ROWS + 1), out_specs=[VSPEC, VSPEC],
        out_shape=[_sds((PACK_ROWS, D)), _sds((ng, pr, POOL_G))],
        scratch_shapes=[pltpu.VMEM((PACK_ROWS, D), F32), pltpu.VMEM((N_DEV, PACK_ROWS, D), F32),
                        pltpu.VMEM((N_DEV, ng, pr, POOL_G), BF), sems, sems, sems, sems],
        args=[*rows, dpool], rider=rider)


def _small_adamw(tot, kc, small_w, small_m, small_v):
    names = SMALL_NAMES
    n = len(names)

    def body(*refs):
        tot_ref, kc_ref = refs[0], refs[1]
        w_refs = dict(zip(names, refs[2:2 + n]))
        m_refs = dict(zip(names, refs[2 + n:2 + 2 * n]))
        v_refs = dict(zip(names, refs[2 + 2 * n:2 + 3 * n]))
        loss_ref = refs[2 + 3 * n]
        out_refs = {nm: refs[3 + 3 * n + 4 * k: 7 + 3 * n + 4 * k] for k, nm in enumerate(names)}
        tot = tot_ref[...]
        loss_ref[...] = 0.5 * jnp.sum(tot[ROW_LOSS:ROW_LOSS + 1, :], axis=-1, keepdims=True) * (1.0 / D)

        def update(nm, g):
            g_ref, d_ref, nm_ref, nv_ref = out_refs[nm]
            g_ref[...] = g
            d_ref[...], nm_ref[...], nv_ref[...] = _adamw_math(w_refs[nm][...], g, m_refs[nm][...], v_refs[nm][...])

        for nm, r in GAIN_ROWS.items():
            update(nm, tot[r:r + 2, :])
        update("kv_g", tot[ROW_KV_G:ROW_KV_G + 1, :])
        k = kc_ref[0]
        width = D // N_CHIPS
        g_scale = jnp.zeros((1, width), F32)
        for kk in range(N_CHIPS):
            g_scale = g_scale + jnp.where(k == kk, tot[ROW_POOL_SCALE:ROW_POOL_SCALE + 1, kk * width:(kk + 1) * width], 0.0)
        update("pool_scale", g_scale)
        update("sinks", tot[ROW_SINKS:ROW_SINKS + 1, 0:N_HEADS])

    ins = [tot, kc] + [small_w[nm] for nm in names] + [small_m[nm] for nm in names] + [small_v[nm] for nm in names]
    out_shape = [_sds((1, 1))]
    for nm in names:
        out_shape += [_sds(small_w[nm].shape)] * 4
    outs = pl.pallas_call(
        body, name="small_adamw",
        in_specs=[VSPEC, SSPEC] + [VSPEC] * (3 * n), out_specs=[VSPEC] * len(out_shape), out_shape=out_shape,
        compiler_params=_params(),
    )(*ins)
    return outs[0], {nm: outs[1 + 4 * k: 5 + 4 * k] for k, nm in enumerate(names)}


def _compute_layout(t, full):
    if t.src == "w_gu":
        return full.reshape(2, D, FF)
    if t.src == "pool_w":
        return full.reshape(len(WINDOWS), POOL_G, POOL_G)
    if t.src == "pool_scale":
        return full.reshape(1, D)
    return full.reshape(t.A * t.R, _ncb(t) * t.C)


def kernel(x, p, pre_mix_g, post_mix_g, pre_ffn_g, post_ffn_g, pool_w, pool_scale, kv_g, w_kv, w_q, sinks, w_o, w_gu, w_down, ple_g, w_ple_gate, w_ple_proj, ple_post_g, loss_target, m_pre_mix_g, m_post_mix_g, m_pre_ffn_g, m_post_ffn_g, m_pool_w, m_pool_scale, m_kv_g, m_w_kv, m_w_q, m_sinks, m_w_o, m_w_gu, m_w_down, m_ple_g, m_w_ple_gate, m_w_ple_proj, m_ple_post_g, v_pre_mix_g, v_post_mix_g, v_pre_ffn_g, v_post_ffn_g, v_pool_w, v_pool_scale, v_kv_g, v_w_kv, v_w_q, v_sinks, v_w_o, v_w_gu, v_w_down, v_ple_g, v_w_ple_gate, v_w_ple_proj, v_ple_post_g):
    weights = dict(pre_mix_g=pre_mix_g, post_mix_g=post_mix_g, pre_ffn_g=pre_ffn_g, post_ffn_g=post_ffn_g,
                   pool_w=pool_w, pool_scale=pool_scale, kv_g=kv_g, w_kv=w_kv, w_q=w_q, sinks=sinks, w_o=w_o,
                   w_gu=w_gu, w_down=w_down, ple_g=ple_g, w_ple_gate=w_ple_gate, w_ple_proj=w_ple_proj,
                   ple_post_g=ple_post_g)
    m_in = dict(pre_mix_g=m_pre_mix_g, post_mix_g=m_post_mix_g, pre_ffn_g=m_pre_ffn_g, post_ffn_g=m_post_ffn_g,
                pool_w=m_pool_w, pool_scale=m_pool_scale, kv_g=m_kv_g, w_kv=m_w_kv, w_q=m_w_q, sinks=m_sinks,
                w_o=m_w_o, w_gu=m_w_gu, w_down=m_w_down, ple_g=m_ple_g, w_ple_gate=m_w_ple_gate,
                w_ple_proj=m_w_ple_proj, ple_post_g=m_ple_post_g)
    v_in = dict(pre_mix_g=v_pre_mix_g, post_mix_g=v_post_mix_g, pre_ffn_g=v_pre_ffn_g, post_ffn_g=v_post_ffn_g,
                pool_w=v_pool_w, pool_scale=v_pool_scale, kv_g=v_kv_g, w_kv=v_w_kv, w_q=v_w_q, sinks=v_sinks,
                w_o=v_w_o, w_gu=v_w_gu, w_down=v_w_down, ple_g=v_ple_g, w_ple_gate=v_w_ple_gate,
                w_ple_proj=v_w_ple_proj, ple_post_g=v_ple_post_g)
    order = ["pre_mix_g", "post_mix_g", "pre_ffn_g", "post_ffn_g", "pool_w", "pool_scale", "kv_g", "w_kv", "w_q",
             "sinks", "w_o", "w_gu", "w_down", "ple_g", "w_ple_gate", "w_ple_proj", "ple_post_g"]

    kc = jnp.stack([2 * lax.axis_index("x") + lax.axis_index("y"), lax.axis_index("c")]).astype(jnp.int32)
    s_len = x.shape[1]
    x2d = x.reshape(s_len, D)
    p3d = p.reshape(2, s_len, PLE)
    target = loss_target.reshape(s_len, D)
    kv_g2d = kv_g.reshape(1, D)
    gains = {nm: weights[nm] for nm in GAIN_ROWS}

    def shard_view(src, a):
        t = next(t for t in BIGS.values() if t.src == src)
        return a.reshape(-1, t.R, t.C)

    first, second = ["pool_w", "pool_scale"], ["w_gu0", "w_down0"]
    rest = [nm for nm in BIGS if nm not in first + second]
    specs = dict(BIGS, pool_scale=POOL_SCALE)
    placed = {nm: _place(BIGS[nm], shard_view(BIGS[nm].src, weights[BIGS[nm].src]), kc, BF)
              for nm in first + second if nm in BIGS}
    placed["pool_scale"] = _place(POOL_SCALE, pool_scale.reshape(1, 1, D // N_CHIPS), kc, F32)

    def gather(names, rows=None):
        rows = rows or {}
        parts = [(specs[nm],) + tuple(rows.get(nm, (0, specs[nm].R))) for nm in names]
        return _gather_rider(parts, [placed[nm] for nm in names])

    def take(names, results):
        for nm, a in zip(names, results):
            placed[nm] = a

    def weight(nm):
        return _compute_layout(specs[nm], placed[nm])

    cast, got = _place_many([BIGS[nm] for nm in rest], [shard_view(BIGS[nm].src, weights[BIGS[nm].src]) for nm in rest],
                            kc, rider=gather(first))
    take(rest, cast)
    take(first, got)


    (y0, x1), got = _mixa_fwd(x2d, gains["pre_mix_g"], weight("pool_w"), weight("pool_scale"), gains["post_mix_g"],
                              rider=gather(second))
    take(second, got)

    ride = ["w_ple_gate0", "w_ple_proj0", "w_q", "w_kv", "w_o", "w_gu1"]
    (f0, x2, g0, u0), got = _ffn_fwd(0, x1, gains["pre_ffn_g"], weight("w_gu0"), weight("w_down0"), gains["post_ffn_g"],
                             rider=gather(ride, {"w_gu1": (0, 320)}))
    take(ride, got)

    ride = ["w_ple_gate1", "w_ple_proj1", "w_gu1"]
    (z0, pe0, x3, q, kv), got = _ple_fwd(
        0, x2, p3d, gains["ple_g"], weight("w_ple_gate0"), weight("w_ple_proj0"), gains["ple_post_g"],
        qkv=(gains["pre_mix_g"], kv_g2d, weight("w_q"), weight("w_kv")),
        rider=gather(ride, {"w_gu1": (320, 704)}))
    take(ride, got)

    ride = ["w_down1", "w_gu1"]
    (attn, y1, x4), got = _attn_fwd(q, kv, sinks, x3, weight("w_o"), gains["post_mix_g"],
                                    rider=gather(ride, {"w_gu1": (704, D)}))
    take(ride, got)

    (f1, x5, g1, u1), _ = _ffn_fwd(1, x4, gains["pre_ffn_g"], weight("w_gu1"), weight("w_down1"), gains["post_ffn_g"])
    (z1, pe1, dx6, loss_row), _ = _ple_fwd(1, x5, p3d, gains["ple_g"], weight("w_ple_gate1"), weight("w_ple_proj1"),
                                           gains["ple_post_g"], target=target)

    local = {}
    landed = {}
    fused = {}

    def pair_stage(tag, names):
        ts = [BIGS[nm] for nm in names]
        gs = [local[nm].reshape(_full_shape(t)) for nm, t in zip(names, ts)]
        lands = _pair_exchange(f"grads_pair_exchange_{tag}", ts, gs)
        jobs = [_pair_sum_job(t, g, l) for t, g, l in zip(ts, gs, lands)]
        return [r[0] for r in _multi_call(f"pair_sum_{tag}", jobs, kc)]

    def scatter(names, sums):
        return _scatter_rider([BIGS[nm] for nm in names], sums)

    def keep(names, sums, got):
        for nm, s, l in zip(names, sums, got):
            landed[nm] = (s, l)

    (dx5, local["w_ple_gate1"], local["w_ple_proj1"], d_ple1, d_plepost1), _ = _ple_bwd(
        1, dx6, x5, z1, pe1, p3d, gains["ple_g"], weight("w_ple_gate1"), gains["ple_post_g"])

    group_a = ["w_ple_gate1", "w_ple_proj1"]
    sums_a = pair_stage("a", group_a)
    (dx4, d_preffn1, d_postffn1, *scattered), _ = _ffn_bwd(
        1, dx5, x4, f1, g1, u1, gains["pre_ffn_g"], weight("w_gu1"), weight("w_down1"), gains["post_ffn_g"], kc)
    fused["w_gu1"], fused["w_down1"] = scattered[0:2], scattered[2:4]

    (dq, dkv, local["w_o"], d_postmix1, d_sinks), got = _attn_bwd(
        dx4, y1, attn, q, kv, sinks, weight("w_o"), gains["post_mix_g"], rider=scatter(group_a, sums_a))
    keep(group_a, sums_a, got)
    dx3, local["w_q"], local["w_kv"], d_premix1, d_kvg = _qkv_bwd(
        dq, dkv, x3, dx4, gains["pre_mix_g"], kv_g2d, weight("w_q"), weight("w_kv"))

    group_b = ["w_o", "w_q", "w_kv"]
    sums_b = pair_stage("b", group_b)
    (dx2, local["w_ple_gate0"], local["w_ple_proj0"], d_ple0, d_plepost0), got = _ple_bwd(
        0, dx3, x2, z0, pe0, p3d, gains["ple_g"], weight("w_ple_gate0"), gains["ple_post_g"],
        rider=scatter(group_b, sums_b))
    keep(group_b, sums_b, got)

    group_c = ["w_ple_gate0", "w_ple_proj0"]
    sums_c = pair_stage("c", group_c)
    (dx1, d_preffn0, d_postffn0, *scattered), _ = _ffn_bwd(
        0, dx2, x1, f0, g0, u0, gains["pre_ffn_g"], weight("w_gu0"), weight("w_down0"), gains["post_ffn_g"], kc)
    fused["w_gu0"], fused["w_down0"] = scattered[0:2], scattered[2:4]

    (dx0, d_pool, d_scale, d_postmix0, d_premix0), _ = _mixa_bwd(
        dx1, x2d, y0, gains["pre_mix_g"], weight("pool_w"), weight("pool_scale"), gains["post_mix_g"])

    rows = [d_premix0, d_premix1, d_postmix0, d_postmix1, d_preffn0, d_preffn1, d_postffn0, d_postffn1,
            d_ple0, d_ple1, d_plepost0, d_plepost1, d_kvg, d_scale, d_sinks, loss_row]
    as2d = lambda a: a.reshape(1, D) if a.ndim == 1 else a
    (tot, g_pool), got = _small_all_reduce(rows, d_pool, rider=scatter(group_c, sums_c))
    keep(group_c, sums_c, got)
    loss, small = _small_adamw(tot, kc, {nm: as2d(weights[nm]) for nm in SMALL_NAMES},
                               {nm: as2d(m_in[nm]) for nm in SMALL_NAMES},
                               {nm: as2d(v_in[nm]) for nm in SMALL_NAMES})

    layers_of = lambda src: [t for t in BIGS.values() if t.src == src]
    own_scatter = ["w_gu", "w_down"]
    others = [src for src in BIG_SOURCES if src not in own_scatter and src != "pool_w"]
    shared = own_scatter + others
    jobs = [_chip_sum_fused_job(layers_of(src), fused, by_cols=src == "w_down") for src in own_scatter]
    jobs += [_chip_sum_job(layers_of(src), landed) for src in others]
    halves = {src: r[0] for src, r in zip(shared, _multi_call("chip_sum", jobs, kc))}
    full_grads = dict(zip(shared, _pair_share([halves[src] for src in shared], [src == "w_down" for src in shared])))
    full_grads["pool_w"] = g_pool

    def adam_args(src):
        return (layers_of(src)[0].rb, shard_view(src, weights[src]), full_grads[src],
                shard_view(src, m_in[src]), shard_view(src, v_in[src]))

    out = {"grad": {}, "delta": {}, "new_m": {}, "new_v": {}}
    results = {src: _adamw(src, *adam_args(src)) for src in own_scatter}
    rest_srcs = others + ["pool_w"]
    results.update(zip(rest_srcs, _multi_call("adamw_rest", [_adamw_job(*adam_args(src)) for src in rest_srcs], kc)))
    for src in BIG_SOURCES:
        shape = weights[src].shape
        for kind, a in zip(("grad", "delta", "new_m", "new_v"), results[src]):
            out[kind][src] = a.reshape(shape)
    for nm in SMALL_NAMES:
        shape = weights[nm].shape
        for kind, a in zip(("grad", "delta", "new_m", "new_v"), small[nm]):
            out[kind][nm] = a.reshape(shape)

    return (loss.reshape(()), dx0.reshape(x.shape),
            *[out["grad"][nm] for nm in order], *[out["delta"][nm] for nm in order],
            *[out["new_m"][nm] for nm in order], *[out["new_v"][nm] for nm in order])
```

```python
import collections

import jax
import jax.numpy as jnp
from jax import lax
from jax.experimental import pallas as pl
from jax.experimental.pallas import tpu as pltpu

D = 1024
FF = 2816
N_HEADS = 16
HEAD_DIM = 64
N_KV_HEADS = 4
GQA = N_HEADS // N_KV_HEADS
KVD = N_KV_HEADS * HEAD_DIM
PLE = 256
BLK = 128
WINDOWS = (2, 4, 8, 16)
POOL_G = 256
HALO = 16
EPS = 1e-6
NEG_INF = -1e30
ATT_SCALE = HEAD_DIM ** -0.5
SLOPES = tuple(2.0 ** (-8.0 * (h + 1) / N_HEADS) for h in range(N_HEADS))
N_CHIPS = 4
N_DEV = 8

LR, B1, B2, AEPS, WD, STEP = 0.001, 0.9, 0.999, 1e-08, 0.01, 10
BC1 = 1.0 - B1 ** STEP
BC2 = 1.0 - B2 ** STEP

BF = jnp.bfloat16
F32 = jnp.float32
MESH = pl.DeviceIdType.MESH
VMEM_LIMIT_V7X = 58 * 1024 * 1024
TM = 256
TM_FFN_BWD = 512
FF_CHUNK = 256
FF_HALF = FF // 2

VSPEC = pl.BlockSpec(memory_space=pltpu.VMEM)
SSPEC = pl.BlockSpec(memory_space=pltpu.SMEM)
ANYSPEC = pl.BlockSpec(memory_space=pl.ANY)


def _params(n_grid=0):
    sem = ("arbitrary",) * n_grid if n_grid else None
    return pltpu.CompilerParams(dimension_semantics=sem, vmem_limit_bytes=VMEM_LIMIT_V7X)


def _sds(shape, dtype=F32):
    return jax.ShapeDtypeStruct(tuple(shape), dtype)


Rider = collections.namedtuple("Rider", "arrays out_shapes aliases scratch start mid finish")
MID_NUM, MID_DEN = 5, 8


def _call(body, *, name, grid, in_specs, out_specs, out_shape, args, scratch_shapes=(), rider=None, prefetch=None):
    ni, no, ns = len(in_specs), len(out_specs), len(scratch_shapes)
    npre = 0 if prefetch is None else 1
    pre = [] if prefetch is None else [prefetch]
    if rider is None:
        rider = Rider([], [], {}, [], None, None, None)
    ri, ro = len(rider.arrays), len(rider.out_shapes)

    def full(*refs):
        pre_refs, refs = refs[:npre], refs[npre:]
        ins, refs = refs[:ni], refs[ni:]
        rins, refs = refs[:ri], refs[ri:]
        outs, refs = refs[:no], refs[no:]
        routs, refs = refs[:ro], refs[ro:]
        scr, rscr = refs[:ns], refs[ns:]
        ids = [pl.program_id(a) for a in range(len(grid))]
        first = ids[0] == 0
        last = ids[0] == grid[0] - 1
        for a in range(1, len(grid)):
            first = first & (ids[a] == 0)
            last = last & (ids[a] == grid[a] - 1)

        if rider.start is not None:
            @pl.when(first)
            def _():
                rider.start(rins, routs, rscr)

        if rider.mid is not None:
            assert len(grid) == 1

            @pl.when(ids[0] == (grid[0] * MID_NUM) // MID_DEN)
            def _():
                rider.mid(rins, routs, rscr)

        body(*pre_refs, *ins, *outs, *scr)

        if rider.finish is not None:
            @pl.when(last)
            def _():
                rider.finish(rins, routs, rscr)

    outs = pl.pallas_call(
        full, name=name,
        grid_spec=pltpu.PrefetchScalarGridSpec(
            num_scalar_prefetch=npre, grid=grid,
            in_specs=list(in_specs) + [ANYSPEC] * ri, out_specs=list(out_specs) + [ANYSPEC] * ro,
            scratch_shapes=list(scratch_shapes) + list(rider.scratch)),
        out_shape=list(out_shape) + list(rider.out_shapes),
        input_output_aliases={npre + ni + a: no + b for a, b in rider.aliases.items()},
        compiler_params=_params(len(grid)))(*pre, *args, *rider.arrays)
    return list(outs[:no]), list(outs[no:])


def _run(name, rider):
    ri = len(rider.arrays)

    def body(*refs):
        rins, routs, rscr = refs[:ri], refs[ri:ri + len(rider.out_shapes)], refs[ri + len(rider.out_shapes):]
        rider.start(rins, routs, rscr)
        if rider.mid is not None:
            rider.mid(rins, routs, rscr)
        rider.finish(rins, routs, rscr)

    return pl.pallas_call(
        body, name=name, in_specs=[ANYSPEC] * ri, out_specs=[ANYSPEC] * len(rider.out_shapes),
        out_shape=list(rider.out_shapes), scratch_shapes=list(rider.scratch),
        input_output_aliases=dict(rider.aliases), compiler_params=_params())(*rider.arrays)


Job = collections.namedtuple("Job", "steps ins outs fn")


def _multi_call(name, jobs, kc):
    n = max(job.steps for job in jobs)

    def clamped(index, steps):
        return lambda s, kc_ref: index(jnp.minimum(s, steps - 1), kc_ref)

    in_specs, out_specs, out_shape, args = [], [], [], []
    for job in jobs:
        for arr, block, index in job.ins:
            in_specs.append(pl.BlockSpec(block, clamped(index, job.steps)))
            args.append(arr)
        for sds, block, index in job.outs:
            out_specs.append(pl.BlockSpec(block, clamped(index, job.steps)))
            out_shape.append(sds)
    n_in = len(args)

    def body(kc_ref, *refs):
        s = pl.program_id(0)
        i0, o0 = 0, n_in
        for job in jobs:
            ins, outs = refs[i0:i0 + len(job.ins)], refs[o0:o0 + len(job.outs)]
            i0, o0 = i0 + len(job.ins), o0 + len(job.outs)

            @pl.when(s < job.steps)
            def _():
                job.fn(s, kc_ref, ins, outs)

    outs, _ = _call(body, name=name, grid=(n,), in_specs=in_specs, out_specs=out_specs, out_shape=out_shape,
                    args=args, prefetch=kc)
    res, o0 = [], 0
    for job in jobs:
        res.append(outs[o0:o0 + len(job.outs)])
        o0 += len(job.outs)
    return res


def _rms_fwd(x, g):
    r = lax.rsqrt(jnp.mean(x * x, axis=-1, keepdims=True) + EPS)
    return x * r * g


def _rms_bwd(x, g, dy):
    r = lax.rsqrt(jnp.mean(x * x, axis=-1, keepdims=True) + EPS)
    xn = x * r
    dxn = dy * g
    dx = r * (dxn - xn * jnp.mean(dxn * xn, axis=-1, keepdims=True))
    return dx, dy * xn


def _rowsum(a):
    return jnp.sum(a, axis=0, keepdims=True)


def _sigmoid(z):
    return 1.0 / (1.0 + jnp.exp(-z))


def _dot(a, b):
    return jnp.dot(a, b, preferred_element_type=F32)


def _dot_nt(a, b):
    return lax.dot_general(a, b, (((1,), (1,)), ((), ())), preferred_element_type=F32)


def _dot_tn(a, b):
    return lax.dot_general(a, b, (((0,), (0,)), ((), ())), preferred_element_type=F32)


def _row_spec(tm, width=D):
    return pl.BlockSpec((tm, width), lambda i: (i, 0))


def _const_spec(shape):
    zeros = (0,) * len(shape)
    return pl.BlockSpec(tuple(shape), lambda *_: zeros)


def _pool_delta(he, pos):
    out = []
    for gi, w in enumerate(WINDOWS):
        hg = he[:, gi * POOL_G:(gi + 1) * POOL_G]
        s = hg
        k = 1
        while k < w:
            s = s + pltpu.roll(s, k, 0)
            k *= 2
        cnt = jnp.maximum(jnp.minimum(pos + 1, w), 1).astype(F32)
        out.append(s / cnt - hg)
    return out


def _load_with_halo_before(x_ref, i, tm):
    r0 = pl.multiple_of(i * tm, tm)
    hs = pl.multiple_of(jnp.maximum(i * tm - HALO, 0), 8)
    xh = jnp.where(i > 0, x_ref[pl.ds(hs, HALO), :], 0.0)
    xt = x_ref[pl.ds(r0, tm), :]
    return xt, jnp.concatenate([xh, xt], axis=0)


def _mixa_fwd(x, pre_g, pool_w, pool_scale, post_g, rider=None):
    s_len = x.shape[0]
    n = s_len // TM

    def body(x_ref, pg_ref, w_ref, sc_ref, qg_ref, y_ref, x1_ref):
        i = pl.program_id(0)
        xt, xe = _load_with_halo_before(x_ref, i, TM)
        he = _rms_fwd(xe, pg_ref[0:1, :])
        pos = i * TM - HALO + lax.broadcasted_iota(jnp.int32, (TM + HALO, 1), 0)
        ds = _pool_delta(he, pos)
        ys = [_dot(ds[gi][HALO:, :].astype(BF), w_ref[gi]) for gi in range(len(WINDOWS))]
        y = jnp.concatenate(ys, axis=1) * sc_ref[...]
        y_ref[...] = y
        x1_ref[...] = xt + _rms_fwd(y, qg_ref[0:1, :])

    return _call(body, name="mixa_fwd", grid=(n,),
                 in_specs=[VSPEC] * 5, out_specs=[_row_spec(TM), _row_spec(TM)],
                 out_shape=[_sds((s_len, D)), _sds((s_len, D))],
                 args=[x, pre_g, pool_w, pool_scale, post_g], rider=rider)


def _mixa_bwd(dx1, x, y, pre_g, pool_w, pool_scale, post_g, rider=None):
    s_len = x.shape[0]
    n = s_len // TM
    ng = len(WINDOWS)

    def body(dx_ref, x_ref, y_ref, pg_ref, w_ref, sc_ref, qg_ref,
             dx0_ref, dw_ref, dsc_ref, dqg_ref, dpg_ref, wacc):
        i = pl.program_id(0)

        @pl.when(i == 0)
        def _():
            wacc[...] = jnp.zeros_like(wacc)
            dsc_ref[...] = jnp.zeros_like(dsc_ref)
            dqg_ref[...] = jnp.zeros_like(dqg_ref)
            dpg_ref[...] = jnp.zeros_like(dpg_ref)

        r0 = pl.multiple_of(i * TM, TM)
        xt, xe = _load_with_halo_before(x_ref, i, TM)
        he = _rms_fwd(xe, pg_ref[0:1, :])
        pos_b = i * TM - HALO + lax.broadcasted_iota(jnp.int32, (TM + HALO, 1), 0)
        ds = _pool_delta(he, pos_b)

        last = i == n - 1
        a0 = pl.multiple_of(jnp.minimum(i * TM + TM, s_len - HALO), 8)
        ye = jnp.concatenate([y_ref[pl.ds(r0, TM), :], y_ref[pl.ds(a0, HALO), :]], axis=0)
        dt = dx_ref[pl.ds(r0, TM), :]
        de = jnp.concatenate([dt, jnp.where(last, 0.0, dx_ref[pl.ds(a0, HALO), :])], axis=0)
        dye, prod = _rms_bwd(ye, qg_ref[0:1, :], de)
        dqg_ref[...] += _rowsum(prod[:TM, :])
        dys = dye * sc_ref[...]
        pos_a = i * TM + lax.broadcasted_iota(jnp.int32, (TM + HALO, 1), 0)

        dhs, dscs = [], []
        for gi, w in enumerate(WINDOWS):
            sl = slice(gi * POOL_G, (gi + 1) * POOL_G)
            wg = w_ref[gi]
            dys_g = dys[:, sl].astype(BF)
            d_g = ds[gi][HALO:, :].astype(BF)
            ypre = _dot(d_g, wg)
            dscs.append(_rowsum(dye[:TM, sl] * ypre))
            wacc[gi] += _dot_tn(d_g, dys_g[:TM, :])
            dd = _dot_nt(dys_g, wg)
            cnt = jnp.minimum(pos_a + 1, w).astype(F32)
            a = dd / cnt
            k = 1
            while k < w:
                a = a + pltpu.roll(a, TM + HALO - k, 0)
                k *= 2
            dhs.append(a[:TM, :] - dd[:TM, :])
        dsc_ref[...] += jnp.concatenate(dscs, axis=1)
        dh = jnp.concatenate(dhs, axis=1)
        dxp, prod2 = _rms_bwd(xt, pg_ref[0:1, :], dh)
        dpg_ref[...] += _rowsum(prod2)
        dx0_ref[...] = dt + dxp

        @pl.when(last)
        def _():
            dw_ref[...] = wacc[...].astype(BF)

    return _call(
        body, name="mixa_bwd", grid=(n,), in_specs=[VSPEC] * 7,
        out_specs=[_row_spec(TM), _const_spec((ng, POOL_G, POOL_G)), _const_spec((1, D)),
                   _const_spec((1, D)), _const_spec((1, D))],
        out_shape=[_sds((s_len, D)), _sds((ng, POOL_G, POOL_G), BF), _sds((1, D)), _sds((1, D)), _sds((1, D))],
        scratch_shapes=[pltpu.VMEM((ng, POOL_G, POOL_G), F32)],
        args=[dx1, x, y, pre_g, pool_w, pool_scale, post_g], rider=rider)


def _ffn_fwd(layer, x1, pre_g, wgu, wd, post_g, rider=None):
    s_len = x1.shape[0]

    def body(x_ref, pg_ref, wgu_ref, wd_ref, qg_ref, f_ref, x2_ref, g_ref, u_ref):
        x = x_ref[...]
        h = _rms_fwd(x, pg_ref[layer:layer + 1, :]).astype(BF)
        f = jnp.zeros((TM, D), F32)
        for c in range(FF // FF_HALF):
            cols = slice(c * FF_HALF, (c + 1) * FF_HALF)
            g = _dot(h, wgu_ref[0, :, cols])
            u = _dot(h, wgu_ref[1, :, cols])
            g_ref[:, cols] = g.astype(BF)
            u_ref[:, cols] = u.astype(BF)
            act = g * _sigmoid(g) * u
            f = f + _dot(act.astype(BF), wd_ref[cols, :])
        f_ref[...] = f
        x2_ref[...] = x + _rms_fwd(f, qg_ref[layer:layer + 1, :])

    return _call(body, name=f"ffn_fwd{layer}", grid=(s_len // TM,),
                 in_specs=[_row_spec(TM), VSPEC, VSPEC, VSPEC, VSPEC],
                 out_specs=[_row_spec(TM), _row_spec(TM), _row_spec(TM, FF), _row_spec(TM, FF)],
                 out_shape=[_sds((s_len, D)), _sds((s_len, D)), _sds((s_len, FF), BF), _sds((s_len, FF), BF)],
                 args=[x1, pre_g, wgu, wd, post_g], rider=rider)


GU_PIECE = 128
DN_PIECE = 64
DN_SLOT = FF // N_CHIPS
HALF_D = D // 2


def _ffn_bwd(layer, dx2, x1, f, g_pre, u_pre, pre_g, wgu, wd, post_g, kc, rider=None):
    s_len = x1.shape[0]
    tm = TM_FFN_BWD
    n = s_len // tm
    nc = FF // FF_CHUNK
    n_gu, n_dn = FF_CHUNK // GU_PIECE, FF_CHUNK // DN_PIECE
    n_pieces = 2 * n_gu + n_dn
    n_blk = FF_HALF // GU_PIECE

    def edge_rows(c, i, kc_ref):
        return (jnp.where((c == 0) | (c == nc - 1), i, n - 1), 0)

    def chunk_at(c, kc_ref):
        return (c + (((kc_ref[0] + 1) % N_CHIPS) * nc) // N_CHIPS) % nc

    def exchange(kc_ref, c, accg, accu, accd, own_gu_ref, land_gu_ref, own_dn_ref, land_dn_ref,
                 pl_gu, pl_dn, sib_gu, sib_dn, mine_gu, mine_dn, sum_gu, sum_dn,
                 psend, precv, ssend, lsem, rrecv):
        x, y, core = lax.axis_index("x"), lax.axis_index("y"), lax.axis_index("c")
        lower = core == 0

        def pair_copy(cc, part):
            p = cc % 2
            src, dst = ((sib_gu, pl_gu), (sib_dn, pl_dn))[part]
            return pltpu.make_async_remote_copy(src.at[p], dst.at[cc], psend.at[p, part], precv.at[cc, part],
                                                device_id=(x, y, 1 - core), device_id_type=MESH)

        def scatter(cc, wait):
            p = cc % 2
            hidden = chunk_at(cc, kc_ref) * FF_CHUNK

            assert n_gu == 2
            k0, k1 = hidden // FF_HALF, (hidden + GU_PIECE) // FF_HALF
            blk = (hidden - k0 * FF_HALF) // GU_PIECE
            for gu in range(2):
                @pl.when(k0 == k1)
                def _():
                    piece(p, wait, 2 * gu, sum_gu.at[p, gu], k0 + 2 * gu, 0, (pl.ds(blk, 2),))

                @pl.when(k0 != k1)
                def _():
                    piece(p, wait, 2 * gu, sum_gu.at[p, gu, 0], k0 + 2 * gu, 0, (blk,))
                    piece(p, wait, 2 * gu + 1, sum_gu.at[p, gu, 1], k1 + 2 * gu, 0, (0,))

            kd = hidden // DN_SLOT
            off = pl.multiple_of(hidden - kd * DN_SLOT, DN_PIECE)
            m = jnp.minimum((DN_SLOT - off) // DN_PIECE, n_dn)
            for mm in range(1, n_dn + 1):
                @pl.when(m == mm)
                def _():
                    rows = mm * DN_PIECE
                    piece(p, wait, 2 * n_gu, sum_dn.at[p, pl.ds(0, rows), :], kd, 1, (pl.ds(off, rows), slice(None)))
                    if mm < n_dn:
                        piece(p, wait, 2 * n_gu + 1, sum_dn.at[p, pl.ds(rows, FF_CHUNK - rows), :], kd + 1, 1,
                              (pl.ds(0, FF_CHUNK - rows), slice(None)))

        def piece(p, wait, pi, src, k, t, where):
            own_ref, land_ref = ((own_gu_ref, land_gu_ref), (own_dn_ref, land_dn_ref))[t]
            kx, ky = k // 2, k % 2
            fx, fy = (kx != x).astype(jnp.int32), (ky != y).astype(jnp.int32)
            local = (fx + fy) == 0
            j = jnp.maximum(fx + 2 * fy - 1, 0)

            @pl.when(local)
            def _():
                cp = pltpu.make_async_copy(src, own_ref.at[where], lsem.at[p, pi])
                if wait:
                    cp.wait()
                else:
                    cp.start()

            @pl.when(jnp.logical_not(local))
            def _():
                cp = pltpu.make_async_remote_copy(src, land_ref.at[(j,) + where], ssend.at[p, pi],
                                                  rrecv.at[t, j], device_id=(kx, ky, core), device_id_type=MESH)
                if wait:
                    cp.wait_send()
                else:
                    cp.start()

        def add_and_scatter(cc):
            p = cc % 2
            pair_copy(cc, 0).wait_recv()
            pair_copy(cc, 1).wait_recv()
            s_gu = (mine_gu[...] + pl_gu[cc].astype(F32)).astype(BF)
            for hc in range(n_gu):
                sum_gu[p, :, hc] = s_gu[:, :, hc * GU_PIECE:(hc + 1) * GU_PIECE]
            sum_dn[p] = (mine_dn[...] + pl_dn[cc].astype(F32)).astype(BF)
            scatter(cc, wait=False)

        @pl.when(c >= 1)
        def _():
            @pl.when(c >= 3)
            def _():
                scatter(c - 3, wait=True)
            add_and_scatter(c - 1)

        @pl.when(c >= 2)
        def _():
            pair_copy(c - 2, 0).wait_send()
            pair_copy(c - 2, 1).wait_send()

        p = c % 2
        my_rows = pl.ds(pl.multiple_of(core * HALF_D, HALF_D), HALF_D)
        sib_rows = pl.ds(pl.multiple_of((1 - core) * HALF_D, HALF_D), HALF_D)
        d_v = accd[...]
        sib_gu[p, 0] = accg[sib_rows, :].astype(BF)
        sib_gu[p, 1] = accu[sib_rows, :].astype(BF)
        sib_dn[p] = jnp.where(lower, d_v[:, HALF_D:], d_v[:, :HALF_D]).astype(BF)
        mine_gu[0] = accg[my_rows, :]
        mine_gu[1] = accu[my_rows, :]
        mine_dn[...] = jnp.where(lower, d_v[:, :HALF_D], d_v[:, HALF_D:])
        pair_copy(c, 0).start()
        pair_copy(c, 1).start()

        @pl.when(c == nc - 1)
        def _():
            scatter(nc - 3, wait=True)
            add_and_scatter(nc - 1)
            for cc in (nc - 2, nc - 1):
                pair_copy(cc, 0).wait_send()
                pair_copy(cc, 1).wait_send()
                scatter(cc, wait=True)
            for t, land_ref in enumerate((land_gu_ref, land_dn_ref)):
                for j in range(N_CHIPS - 1):
                    pltpu.make_async_remote_copy(land_ref.at[j], land_ref.at[j], ssend.at[0, 0], rrecv.at[t, j],
                                                 device_id=(x, y, core), device_id_type=MESH).wait_recv()

    def body(kc_ref, dx_ref, x_ref, f_ref, gp_ref, up_ref, pg_ref, wgu_ref, wd_ref, qg_ref,
             dx1_ref, dpg_ref, dqg_ref, own_gu_ref, land_gu_ref, own_dn_ref, land_dn_ref,
             h_s, df_s, dh_s, accg, accu, accd, *comm):
        c = pl.program_id(0)
        i = pl.program_id(1)
        rows = pl.ds(pl.multiple_of(i * tm, tm), tm)
        pg = pg_ref[layer:layer + 1, :]

        @pl.when((c == 0) & (i == 0))
        def _():
            dpg_ref[...] = jnp.zeros_like(dpg_ref)
            dqg_ref[...] = jnp.zeros_like(dqg_ref)

        @pl.when(c == 0)
        def _():
            h_s[rows, :] = _rms_fwd(x_ref[...], pg).astype(BF)
            df, prod = _rms_bwd(f_ref[...], qg_ref[layer:layer + 1, :], dx_ref[...])
            df_s[rows, :] = df.astype(BF)
            dqg_ref[...] += _rowsum(prod)

        @pl.when(i == 0)
        def _():
            accg[...] = jnp.zeros_like(accg)
            accu[...] = jnp.zeros_like(accu)
            accd[...] = jnp.zeros_like(accd)

        h = h_s[rows, :]
        df = df_s[rows, :]
        wg = wgu_ref[0]
        wu = wgu_ref[1]
        g = gp_ref[...].astype(F32)
        u = up_ref[...].astype(F32)
        sg = _sigmoid(g)
        a = g * sg
        dact = _dot_nt(df, wd_ref[...])
        accd[...] += _dot_tn((a * u).astype(BF), df)
        du = (dact * a).astype(BF)
        dg = (dact * u * (sg * (1.0 + g * (1.0 - sg)))).astype(BF)
        accg[...] += _dot_tn(h, dg)
        accu[...] += _dot_tn(h, du)
        dh = _dot_nt(dg, wg) + _dot_nt(du, wu)

        @pl.when(c == 0)
        def _():
            dh_s[rows, :] = dh

        @pl.when((c > 0) & (c < nc - 1))
        def _():
            dh_s[rows, :] += dh

        @pl.when(c == nc - 1)
        def _():
            dxp, prod = _rms_bwd(x_ref[...], pg, dh_s[rows, :] + dh)
            dpg_ref[...] += _rowsum(prod)
            dx1_ref[...] = dx_ref[...] + dxp

        @pl.when(i == n - 1)
        def _():
            exchange(kc_ref, c, accg, accu, accd, own_gu_ref, land_gu_ref, own_dn_ref, land_dn_ref, *comm)

    dma = pltpu.SemaphoreType.DMA
    return _call(
        body, name=f"ffn_bwd{layer}", grid=(nc, n),
        in_specs=[pl.BlockSpec((tm, D), edge_rows), pl.BlockSpec((tm, D), edge_rows),
                  pl.BlockSpec((tm, D), lambda c, i, kc_ref: (jnp.where(c == 0, i, n - 1), 0),
                               pipeline_mode=pl.Buffered(1)),
                  pl.BlockSpec((tm, FF_CHUNK), lambda c, i, kc_ref: (i, chunk_at(c, kc_ref))),
                  pl.BlockSpec((tm, FF_CHUNK), lambda c, i, kc_ref: (i, chunk_at(c, kc_ref))),
                  VSPEC,
                  pl.BlockSpec((2, D, FF_CHUNK), lambda c, i, kc_ref: (0, 0, chunk_at(c, kc_ref))),
                  pl.BlockSpec((FF_CHUNK, D), lambda c, i, kc_ref: (chunk_at(c, kc_ref), 0)),
                  VSPEC],
        out_specs=[pl.BlockSpec((tm, D), lambda c, i, kc_ref: (jnp.where(c == nc - 1, i, 0), 0)),
                   _const_spec((1, D)), _const_spec((1, D)), ANYSPEC, ANYSPEC, ANYSPEC, ANYSPEC],
        out_shape=[_sds((s_len, D)), _sds((1, D)), _sds((1, D)),
                   _sds((n_blk, HALF_D, GU_PIECE), BF), _sds((N_CHIPS - 1, n_blk, HALF_D, GU_PIECE), BF),
                   _sds((DN_SLOT, HALF_D), BF), _sds((N_CHIPS - 1, DN_SLOT, HALF_D), BF)],
        scratch_shapes=[pltpu.VMEM((s_len, D), BF), pltpu.VMEM((s_len, D), BF), pltpu.VMEM((s_len, D), F32),
                        pltpu.VMEM((D, FF_CHUNK), F32), pltpu.VMEM((D, FF_CHUNK), F32),
                        pltpu.VMEM((FF_CHUNK, D), F32),
                        pltpu.VMEM((nc, 2, HALF_D, FF_CHUNK), BF), pltpu.VMEM((nc, FF_CHUNK, HALF_D), BF),
                        pltpu.VMEM((2, 2, HALF_D, FF_CHUNK), BF), pltpu.VMEM((2, FF_CHUNK, HALF_D), BF),
                        pltpu.VMEM((2, HALF_D, FF_CHUNK), F32), pltpu.VMEM((FF_CHUNK, HALF_D), F32),
                        pltpu.VMEM((2, 2, n_gu, HALF_D, GU_PIECE), BF), pltpu.VMEM((2, FF_CHUNK, HALF_D), BF),
                        dma((2, 2)), dma((nc, 2)), dma((2, n_pieces)), dma((2, n_pieces)), dma((2, N_CHIPS - 1))],
        args=[dx2, x1, f, g_pre, u_pre, pre_g, wgu, wd, post_g], rider=rider, prefetch=kc)


def _ple_fwd(layer, x2, p, ple_g, w_gate, w_proj, post_g, target=None, qkv=None, rider=None):
    s_len = x2.shape[0]
    final = target is not None
    assert not (final and qkv)

    def body(*refs):
        if final:
            x_ref, p_ref, g_ref, wg_ref, wp_ref, qg_ref, t_ref, z_ref, pe_ref, dx_ref, lv_ref = refs
        elif qkv:
            (x_ref, p_ref, g_ref, wg_ref, wp_ref, qg_ref, ng_ref, kg_ref, wq_ref, wkv_ref,
             z_ref, pe_ref, x3_ref, q_ref, kv_ref) = refs
        else:
            x_ref, p_ref, g_ref, wg_ref, wp_ref, qg_ref, z_ref, pe_ref, x3_ref = refs
        x = x_ref[...]
        r = _rms_fwd(x, g_ref[layer:layer + 1, :]).astype(BF)
        z = _dot(r, wg_ref[...])
        pe = _dot(p_ref[...].astype(BF), wp_ref[...])
        z_ref[...] = z
        pe_ref[...] = pe
        x3 = x + _rms_fwd(pe * _sigmoid(z), qg_ref[layer:layer + 1, :])
        if final:
            @pl.when(pl.program_id(0) == 0)
            def _():
                lv_ref[...] = jnp.zeros_like(lv_ref)
            err = x3 - t_ref[...]
            dx_ref[...] = err * (1.0 / D)
            lv_ref[...] += _rowsum(err * err)
        else:
            x3_ref[...] = x3
        if qkv:
            q_ref[...] = _dot(_rms_fwd(x3, ng_ref[layer + 1:layer + 2, :]).astype(BF), wq_ref[...]).astype(BF)
            kv_ref[...] = _dot(_rms_fwd(x3, kg_ref[...]).astype(BF), wkv_ref[...]).astype(BF)

    p_spec = pl.BlockSpec((None, TM, PLE), lambda i: (layer, i, 0))
    in_specs = [_row_spec(TM), p_spec, VSPEC, VSPEC, VSPEC, VSPEC]
    args = [x2, p, ple_g, w_gate, w_proj, post_g]
    out_specs = [_row_spec(TM), _row_spec(TM), _row_spec(TM)]
    out_shape = [_sds((s_len, D))] * 3
    if qkv:
        in_specs += [VSPEC] * 4
        args += list(qkv)
        out_specs += [_row_spec(TM), _row_spec(TM, 2 * KVD)]
        out_shape += [_sds((s_len, D), BF), _sds((s_len, 2 * KVD), BF)]
    if final:
        in_specs.append(_row_spec(TM))
        args.append(target)
        out_specs.append(_const_spec((1, D)))
        out_shape.append(_sds((1, D)))
    return _call(body, name=f"ple_fwd{layer}", grid=(s_len // TM,), in_specs=in_specs, out_specs=out_specs,
                 out_shape=out_shape, args=args, rider=rider)


def _ple_bwd(layer, dx3, x2, z, pe, p, ple_g, w_gate, post_g, rider=None):
    s_len = x2.shape[0]
    n = s_len // TM

    def body(dx_ref, x_ref, z_ref, pe_ref, p_ref, g_ref, wg_ref, qg_ref,
             dx2_ref, dwg_ref, dwp_ref, dg_ref, dqg_ref, gacc, pacc):
        i = pl.program_id(0)

        @pl.when(i == 0)
        def _():
            gacc[...] = jnp.zeros_like(gacc)
            pacc[...] = jnp.zeros_like(pacc)
            dg_ref[...] = jnp.zeros_like(dg_ref)
            dqg_ref[...] = jnp.zeros_like(dqg_ref)

        dx = dx_ref[...]
        x = x_ref[...]
        pe_v = pe_ref[...]
        gate = _sigmoid(z_ref[...])
        de, prod = _rms_bwd(pe_v * gate, qg_ref[layer:layer + 1, :], dx)
        dqg_ref[...] += _rowsum(prod)
        dpe = (de * gate).astype(BF)
        dz = (de * pe_v * gate * (1.0 - gate)).astype(BF)
        pacc[...] += _dot_tn(p_ref[...].astype(BF), dpe)
        g = g_ref[layer:layer + 1, :]
        r = _rms_fwd(x, g).astype(BF)
        gacc[...] += _dot_tn(r, dz)
        dr = _dot_nt(dz, wg_ref[...])
        dxp, prod2 = _rms_bwd(x, g, dr)
        dg_ref[...] += _rowsum(prod2)
        dx2_ref[...] = dx + dxp

        @pl.when(i == n - 1)
        def _():
            dwg_ref[...] = gacc[...].astype(BF)
            dwp_ref[...] = pacc[...].astype(BF)

    p_spec = pl.BlockSpec((None, TM, PLE), lambda i: (layer, i, 0))
    return _call(
        body, name=f"ple_bwd{layer}", grid=(n,),
        in_specs=[_row_spec(TM), _row_spec(TM), _row_spec(TM), _row_spec(TM), p_spec, VSPEC, VSPEC, VSPEC],
        out_specs=[_row_spec(TM), _const_spec((D, D)), _const_spec((PLE, D)), _const_spec((1, D)), _const_spec((1, D))],
        out_shape=[_sds((s_len, D)), _sds((D, D), BF), _sds((PLE, D), BF), _sds((1, D)), _sds((1, D))],
        scratch_shapes=[pltpu.VMEM((D, D), F32), pltpu.VMEM((PLE, D), F32)],
        args=[dx3, x2, z, pe, p, ple_g, w_gate, post_g], rider=rider)


def _qkv_bwd(dq, dkv, x3, dx4, q_g, kv_g, w_q, w_kv):
    s_len = x3.shape[0]
    n = s_len // TM

    def body(dq_ref, dkv_ref, x_ref, dx_ref, qg_ref, kg_ref, wq_ref, wkv_ref,
             dx3_ref, dwq_ref, dwkv_ref, dqg_ref, dkg_ref, qacc, kacc):
        i = pl.program_id(0)

        @pl.when(i == 0)
        def _():
            qacc[...] = jnp.zeros_like(qacc)
            kacc[...] = jnp.zeros_like(kacc)
            dqg_ref[...] = jnp.zeros_like(dqg_ref)
            dkg_ref[...] = jnp.zeros_like(dkg_ref)

        x = x_ref[...]
        qg = qg_ref[1:2, :]
        kg = kg_ref[...]
        dq_v = dq_ref[...]
        dkv_v = dkv_ref[...].astype(BF)
        qacc[...] += _dot_tn(_rms_fwd(x, qg).astype(BF), dq_v)
        kacc[...] += _dot_tn(_rms_fwd(x, kg).astype(BF), dkv_v)
        dxq, prod_q = _rms_bwd(x, qg, _dot_nt(dq_v, wq_ref[...]))
        dxk, prod_k = _rms_bwd(x, kg, _dot_nt(dkv_v, wkv_ref[...]))
        dqg_ref[...] += _rowsum(prod_q)
        dkg_ref[...] += _rowsum(prod_k)
        dx3_ref[...] = dx_ref[...] + dxq + dxk

        @pl.when(i == n - 1)
        def _():
            dwq_ref[...] = qacc[...].astype(BF)
            dwkv_ref[...] = kacc[...].astype(BF)

    outs, _ = _call(
        body, name="qkv_bwd", grid=(n,),
        in_specs=[_row_spec(TM), _row_spec(TM, 2 * KVD), _row_spec(TM), _row_spec(TM), VSPEC, VSPEC, VSPEC, VSPEC],
        out_specs=[_row_spec(TM), _const_spec((D, D)), _const_spec((D, 2 * KVD)),
                   _const_spec((1, D)), _const_spec((1, D))],
        out_shape=[_sds((s_len, D)), _sds((D, D), BF), _sds((D, 2 * KVD), BF), _sds((1, D)), _sds((1, D))],
        scratch_shapes=[pltpu.VMEM((D, D), F32), pltpu.VMEM((D, 2 * KVD), F32)],
        args=[dq, dkv, x3, dx4, q_g, kv_g, w_q, w_kv])
    return outs


def _attn_group(i, q, kvw, sink_ref, g):
    rows = GQA * BLK
    heads = [GQA * g + j for j in range(GQA)]
    off = jnp.where(i > 0, BLK, 0)
    row = lax.broadcasted_iota(jnp.int32, (rows, 2 * BLK), 0)
    rel = (row % BLK) - lax.broadcasted_iota(jnp.int32, (rows, 2 * BLK), 1) + off
    valid = (rel >= 0) & (rel < BLK)
    head_of_row = lax.broadcasted_iota(jnp.int32, (rows, 1), 0) // BLK
    slope = jnp.zeros((rows, 1), F32)
    sink = jnp.zeros((rows, 1), F32)
    for j, h in enumerate(heads):
        slope = jnp.where(head_of_row == j, SLOPES[h], slope)
        sink = jnp.where(head_of_row == j, sink_ref[0, h], sink)
    qs = jnp.concatenate([q[:, h * HEAD_DIM:(h + 1) * HEAD_DIM] for h in heads], axis=0)
    k = kvw[:, g * HEAD_DIM:(g + 1) * HEAD_DIM]
    v = kvw[:, KVD + g * HEAD_DIM:KVD + (g + 1) * HEAD_DIM]
    s = _dot_nt(qs, k) * ATT_SCALE - slope * rel.astype(F32)
    s = jnp.where(valid, s, NEG_INF)
    m = jnp.maximum(jnp.max(s, axis=-1, keepdims=True), sink)
    e = jnp.exp(s - m)
    es = jnp.exp(sink - m)
    inv = 1.0 / (jnp.sum(e, axis=-1, keepdims=True) + es)
    return e * inv, es * inv, qs, k, v


def _unstack_heads(stacked):
    return [stacked[j * BLK:(j + 1) * BLK, :] for j in range(GQA)]


def _kv_window(kv_ref, i):
    ks = pl.multiple_of(jnp.maximum(i * BLK - BLK, 0), BLK)
    return ks, kv_ref[pl.ds(ks, 2 * BLK), :]


def _attn_fwd(q, kv, sinks, x3, w_o, post_g, rider=None):
    s_len = q.shape[0]

    def body(q_ref, kv_ref, sk_ref, x_ref, wo_ref, g_ref, a_ref, y_ref, x4_ref):
        i = pl.program_id(0)
        _, kvw = _kv_window(kv_ref, i)
        q = q_ref[...]
        outs = []
        for g in range(N_KV_HEADS):
            p, _, _, _, v = _attn_group(i, q, kvw, sk_ref, g)
            outs += _unstack_heads(_dot(p.astype(BF), v))
        attn = jnp.concatenate(outs, axis=1)
        a_ref[...] = attn
        y = _dot(attn.astype(BF), wo_ref[...])
        y_ref[...] = y
        x4_ref[...] = x_ref[...] + _rms_fwd(y, g_ref[1:2, :])

    return _call(body, name="attn_fwd", grid=(s_len // BLK,),
                 in_specs=[_row_spec(BLK), VSPEC, SSPEC, _row_spec(BLK), VSPEC, VSPEC],
                 out_specs=[_row_spec(BLK)] * 3, out_shape=[_sds((s_len, D))] * 3,
                 args=[q, kv, sinks, x3, w_o, post_g], rider=rider)


def _attn_bwd(dx4, y, attn, q, kv, sinks, w_o, post_g, rider=None):
    s_len = q.shape[0]
    n = s_len // BLK

    def body(dx_ref, y_ref, a_ref, q_ref, kv_ref, sk_ref, wo_ref, g_ref,
             dq_ref, dkv_ref, dwo_ref, dg_ref, dsk_ref, wacc):
        i = pl.program_id(0)

        @pl.when(i == 0)
        def _():
            dkv_ref[...] = jnp.zeros_like(dkv_ref)
            wacc[...] = jnp.zeros_like(wacc)
            dg_ref[...] = jnp.zeros_like(dg_ref)
            dsk_ref[...] = jnp.zeros_like(dsk_ref)

        dy, prod = _rms_bwd(y_ref[...], g_ref[1:2, :], dx_ref[...])
        dg_ref[...] += _rowsum(prod)
        dyb = dy.astype(BF)
        attn = a_ref[...]
        wacc[...] += _dot_tn(attn.astype(BF), dyb)
        d_o = _dot_nt(dyb, wo_ref[...])
        dod = d_o * attn
        ks, kvw = _kv_window(kv_ref, i)
        q = q_ref[...]
        lane = lax.broadcasted_iota(jnp.int32, (1, D), 1)
        dqs, dks, dvs = [], [], []
        dsk = jnp.zeros((1, D), F32)
        for g in range(N_KV_HEADS):
            p, ps, qs, k, v = _attn_group(i, q, kvw, sk_ref, g)
            cols = [slice((GQA * g + j) * HEAD_DIM, (GQA * g + j + 1) * HEAD_DIM) for j in range(GQA)]
            do_s = jnp.concatenate([d_o[:, c] for c in cols], axis=0).astype(BF)
            dsum = jnp.concatenate([jnp.sum(dod[:, c], axis=-1, keepdims=True) for c in cols], axis=0)
            dp = _dot_nt(do_s, v)
            dsb = (p * (dp - dsum) * ATT_SCALE).astype(BF)
            sink_part = ps * dsum
            for j in range(GQA):
                dsk = dsk + jnp.where(lane == GQA * g + j, -_rowsum(sink_part[j * BLK:(j + 1) * BLK, :]), 0.0)
            dqs += _unstack_heads(_dot(dsb, k))
            dks.append(_dot_tn(dsb, qs))
            dvs.append(_dot_tn(p.astype(BF), do_s))
        dsk_ref[...] += dsk
        dq_ref[...] = jnp.concatenate(dqs, axis=1).astype(BF)
        dkv_ref[pl.ds(ks, 2 * BLK), :] += jnp.concatenate(dks + dvs, axis=1)

        @pl.when(i == n - 1)
        def _():
            dwo_ref[...] = wacc[...].astype(BF)

    return _call(
        body, name="attn_bwd", grid=(n,),
        in_specs=[_row_spec(BLK), _row_spec(BLK), _row_spec(BLK), _row_spec(BLK), VSPEC, SSPEC, VSPEC, VSPEC],
        out_specs=[_row_spec(BLK), _const_spec((s_len, 2 * KVD)), _const_spec((D, D)),
                   _const_spec((1, D)), _const_spec((1, D))],
        out_shape=[_sds((s_len, D), BF), _sds((s_len, 2 * KVD)), _sds((D, D), BF), _sds((1, D)), _sds((1, D))],
        scratch_shapes=[pltpu.VMEM((D, D), F32)],
        args=[dx4, y, attn, q, kv, sinks, w_o, post_g], rider=rider)


Big = collections.namedtuple("Big", "name src layer L A R C rb")


def _bigs():
    out = {"pool_w": Big("pool_w", "pool_w", None, 4, 4, POOL_G // N_CHIPS, POOL_G, 32)}
    for l in range(2):
        out[f"w_gu{l}"] = Big(f"w_gu{l}", "w_gu", l, 1, 2, D, FF_HALF, 256)
        out[f"w_down{l}"] = Big(f"w_down{l}", "w_down", l, 1, 4, FF // N_CHIPS, D, 352)
        out[f"w_ple_gate{l}"] = Big(f"w_ple_gate{l}", "w_ple_gate", l, 1, 4, D // N_CHIPS, D, 128)
        out[f"w_ple_proj{l}"] = Big(f"w_ple_proj{l}", "w_ple_proj", l, 1, 1, PLE, D // N_CHIPS, 128)
    out["w_q"] = Big("w_q", "w_q", None, 1, 4, D // N_CHIPS, D, 128)
    out["w_o"] = Big("w_o", "w_o", None, 1, 4, D // N_CHIPS, D, 128)
    out["w_kv"] = Big("w_kv", "w_kv", None, 1, 4, D // N_CHIPS, 2 * KVD, 128)
    return out


BIGS = _bigs()
POOL_SCALE = Big("pool_scale", "pool_scale", None, 1, 1, 1, D // N_CHIPS, 1)
BIG_SOURCES = ("w_gu", "w_down", "w_ple_gate", "w_ple_proj", "w_q", "w_o", "w_kv", "pool_w")


def _ncb(t):
    return N_CHIPS // t.A


def _full_shape(t, rows=None):
    return (t.L, t.A, t.R if rows is None else rows, _ncb(t) * t.C)


def _slot_index(t, k):
    return k // _ncb(t), k % _ncb(t)


def _slot(ref, t, k, row0, rows):
    a, cb = _slot_index(t, k)
    return ref.at[:, a, pl.ds(row0, rows), pl.ds(pl.multiple_of(cb * t.C, 128), t.C)]


def _place(t, w, kc, out_dtype):
    rb = min(t.R, 2 * t.rb)

    def body(kc_ref, w_ref, o_ref):
        del kc_ref
        o_ref[...] = w_ref[...].astype(out_dtype)

    def in_map(l, j, kc_ref):
        return (l if t.layer is None else t.layer, j, 0)

    def out_map(l, j, kc_ref):
        a, cb = _slot_index(t, kc_ref[0])
        return (l, a, j, cb)

    return pl.pallas_call(
        body, name=f"place_{t.name}",
        grid_spec=pltpu.PrefetchScalarGridSpec(
            num_scalar_prefetch=1, grid=(t.L, t.R // rb),
            in_specs=[pl.BlockSpec((None, rb, t.C), in_map)],
            out_specs=pl.BlockSpec((None, None, rb, t.C), out_map)),
        out_shape=_sds(_full_shape(t), out_dtype),
        compiler_params=_params(2),
    )(kc, w)


def _place_many(ts, ws, kc, rider):
    n = 8

    def blocks_of(t):
        return next(nb for nb in (8, 4, 2, 1) if t.R % (16 * nb) == 0)

    def body(kc_ref, *refs):
        del kc_ref
        s = pl.program_id(0)
        for ti, t in enumerate(ts):
            @pl.when(s < blocks_of(t))
            def _():
                refs[len(ts) + ti][...] = refs[ti][...].astype(BF)

    in_specs, out_specs = [], []
    for t in ts:
        assert t.L == 1
        nb = blocks_of(t)
        rb = t.R // nb

        def in_map(s, kc_ref, t=t, nb=nb):
            return (0 if t.layer is None else t.layer, jnp.minimum(s, nb - 1), 0)

        def out_map(s, kc_ref, t=t, nb=nb):
            a, cb = _slot_index(t, kc_ref[0])
            return (0, a, jnp.minimum(s, nb - 1), cb)

        in_specs.append(pl.BlockSpec((None, rb, t.C), in_map))
        out_specs.append(pl.BlockSpec((None, None, rb, t.C), out_map))
    return _call(body, name="place_rest", grid=(n,), in_specs=in_specs, out_specs=out_specs,
                 out_shape=[_sds(_full_shape(t), BF) for t in ts], args=list(ws), rider=rider, prefetch=kc)


def _mesh_position():
    x, y, c = lax.axis_index("x"), lax.axis_index("y"), lax.axis_index("c")
    chips = [(1 - x, y), (x, 1 - y), (1 - x, 1 - y)]
    return x, y, c, chips


def _gather_rider(parts, fulls):
    nt = len(parts)
    TO_X, TO_Y, FWD_X, FWD_Y, SIB_X, SIB_Y, SIB_D = range(7)

    def rows_of(ti, core):
        t, r0, r1 = parts[ti]
        h = (r1 - r0) // 2
        return r0 + core * h, h

    def copy(outs, sems, kind, ti, k_src, row0, rows, dev):
        region = _slot(outs[ti], parts[ti][0], k_src, row0, rows)
        return pltpu.make_async_remote_copy(region, region, sems[0].at[ti, kind], sems[1].at[ti, kind],
                                            device_id=dev, device_id_type=MESH)

    def plan(outs, sems):
        x, y, c, _ = _mesh_position()
        me, kx, ky, kd = 2 * x + y, 2 * (1 - x) + y, 2 * x + (1 - y), 2 * (1 - x) + (1 - y)
        dev_x, dev_y, dev_d, sib = (1 - x, y, c), (x, 1 - y, c), (1 - x, 1 - y, c), (x, y, 1 - c)

        def whole(ti):
            return 0, parts[ti][0].R

        def mk(kind, k_send, k_recv, dev, send_rows, recv_rows):
            def build(ti, side):
                k_src = k_send if side == "s" else k_recv
                row0, rows = (send_rows if side == "s" else recv_rows)(ti)
                return copy(outs, sems, kind, ti, k_src, row0, rows, dev)
            return build

        def first_half(core):
            return lambda ti: (rows_of(ti, core)[0], rows_of(ti, core)[1] // 2)

        def second_half(core):
            return lambda ti: (rows_of(ti, core)[0] + rows_of(ti, core)[1] // 2, rows_of(ti, core)[1] // 2)

        mine = lambda ti: rows_of(ti, c)
        theirs = lambda ti: rows_of(ti, 1 - c)
        split = {
            TO_X: mk(TO_X, me, kx, dev_x, mine, mine),
            TO_Y: mk(TO_Y, me, ky, dev_y, mine, mine),
            FWD_X: mk(FWD_X, ky, kd, dev_x, first_half(c), first_half(c)),
            FWD_Y: mk(FWD_Y, kx, kd, dev_y, second_half(c), second_half(c)),
            SIB_X: mk(SIB_X, kx, kx, sib, mine, theirs),
            SIB_Y: mk(SIB_Y, ky, ky, sib, mine, theirs),
            SIB_D: mk(SIB_D, kd, kd, sib, mine, theirs),
        }
        direct = {
            TO_X: mk(TO_X, me, kx, dev_x, whole, whole),
            TO_Y: mk(TO_Y, me, ky, dev_y, whole, whole),
            FWD_X: mk(FWD_X, me, kd, dev_d, whole, whole),
        }
        return split, direct

    is_split = [t.R > 1 for t, _, _ in parts]

    def start(ins, outs, sems):
        split, direct = plan(outs, sems)
        for ti in range(nt):
            kinds = split if is_split[ti] else direct
            kinds[TO_X](ti, "s").start()
            kinds[TO_Y](ti, "s").start()
            if not is_split[ti]:
                kinds[FWD_X](ti, "s").start()

    def mid(ins, outs, sems):
        split, _ = plan(outs, sems)
        for ti in range(nt):
            if is_split[ti]:
                split[TO_Y](ti, "r").wait_recv()
                split[FWD_X](ti, "s").start()
                split[SIB_Y](ti, "s").start()
        for ti in range(nt):
            if is_split[ti]:
                split[TO_X](ti, "r").wait_recv()
                split[FWD_Y](ti, "s").start()
                split[SIB_X](ti, "s").start()

    def finish(ins, outs, sems):
        split, direct = plan(outs, sems)
        for ti in range(nt):
            if is_split[ti]:
                split[FWD_X](ti, "r").wait_recv()
                split[FWD_Y](ti, "r").wait_recv()
                split[SIB_D](ti, "s").start()
            else:
                for kind in (TO_X, TO_Y, FWD_X):
                    direct[kind](ti, "r").wait_recv()
        for ti in range(nt):
            if is_split[ti]:
                for kind in (SIB_X, SIB_Y, SIB_D):
                    split[kind](ti, "r").wait_recv()
        for ti in range(nt):
            kinds = split if is_split[ti] else direct
            for kind in kinds:
                kinds[kind](ti, "s").wait_send()

    sems = pltpu.SemaphoreType.DMA((nt, 7))
    return Rider(list(fulls), [_sds(a.shape, a.dtype) for a in fulls], {i: i for i in range(nt)},
                 [sems, sems], start, mid, finish)


def _pair_exchange(name, specs, grads):
    nt = len(specs)

    def body(*refs):
        gs = refs[:nt]
        lands = refs[nt:2 * nt]
        send, recv = refs[2 * nt:]
        x, y, c, _ = _mesh_position()
        cps = []
        for ti, t in enumerate(specs):
            half = t.R // 2
            cp = pltpu.make_async_remote_copy(gs[ti].at[:, :, pl.ds((1 - c) * half, half), :], lands[ti],
                                              send.at[ti], recv.at[ti],
                                              device_id=(x, y, 1 - c), device_id_type=MESH)
            cp.start()
            cps.append(cp)
        for cp in cps:
            cp.wait()

    return pl.pallas_call(
        body, name=name,
        in_specs=[ANYSPEC] * nt, out_specs=[ANYSPEC] * nt,
        out_shape=[_sds(_full_shape(t, t.R // 2), BF) for t in specs],
        scratch_shapes=[pltpu.SemaphoreType.DMA((nt,)), pltpu.SemaphoreType.DMA((nt,))],
        compiler_params=_params(),
    )(*grads)


def _pair_sum_job(t, g, land):
    assert t.L == 1
    half = t.R // 2
    nj = half // t.rb
    block = (None, t.A, t.rb, _ncb(t) * t.C)

    def fn(j, kc_ref, ins, outs):
        outs[0][...] = (ins[0][...].astype(F32) + ins[1][...].astype(F32)).astype(BF)

    return Job(nj,
               [(g, block, lambda j, kc_ref: (0, 0, kc_ref[1] * nj + j, 0)),
                (land, block, lambda j, kc_ref: (0, 0, j, 0))],
               [(_sds(_full_shape(t, half), BF), block, lambda j, kc_ref: (0, 0, j, 0))], fn)


def _scatter_rider(specs, sums):
    nt = len(specs)

    def copy(ins, outs, sems, ti, j, chip, c):
        t = specs[ti]
        cx, cy = chip
        return pltpu.make_async_remote_copy(_slot(ins[ti], t, 2 * cx + cy, 0, t.R // 2), outs[ti].at[j],
                                            sems[0].at[ti, j], sems[1].at[ti, j],
                                            device_id=(cx, cy, c), device_id_type=MESH)

    def start(ins, outs, sems):
        _, _, c, chips = _mesh_position()
        for j, chip in enumerate(chips):
            for ti in range(nt):
                copy(ins, outs, sems, ti, j, chip, c).start()

    def finish(ins, outs, sems):
        _, _, c, chips = _mesh_position()
        for j, chip in enumerate(chips):
            for ti in range(nt):
                copy(ins, outs, sems, ti, j, chip, c).wait()

    sems = pltpu.SemaphoreType.DMA((nt, N_CHIPS - 1))
    return Rider(list(sums), [_sds((N_CHIPS - 1, t.L, t.R // 2, t.C), BF) for t in specs], {}, [sems, sems],
                 start, None, finish)


def _chip_sum_job(ts, landed):
    t0 = ts[0]
    assert t0.L == 1
    half = t0.R // 2
    nj = half // t0.rb

    def local(j, li):
        return jnp.clip(j - li * nj, 0, nj - 1)

    ins = []
    for li, t in enumerate(ts):
        s, land = landed[t.name]

        def own_map(j, kc_ref, li=li, t=t):
            a, cb = _slot_index(t, kc_ref[0])
            return (0, a, local(j, li), cb)

        ins.append((s, (None, None, t.rb, t.C), own_map))
        ins.append((land, (N_CHIPS - 1, None, t.rb, t.C), lambda j, kc_ref, li=li: (0, 0, local(j, li), 0)))

    def fn(j, kc_ref, in_refs, outs):
        for li in range(len(ts)):
            @pl.when(j // nj == li)
            def _():
                acc = in_refs[2 * li][...].astype(F32)
                for k in range(N_CHIPS - 1):
                    acc = acc + in_refs[2 * li + 1][k].astype(F32)
                outs[0][...] = acc

    return Job(len(ts) * nj, ins,
               [(_sds((len(ts), t0.R, t0.C)), (None, t0.rb, t0.C),
                 lambda j, kc_ref: (j // nj, kc_ref[1] * nj + j % nj, 0))], fn)


def _adamw_job(rb, w, g, m, v):
    n_layers, r, c = w.shape
    nb = r // rb
    block = (None, rb, c)
    index = lambda j, kc_ref: (j // nb, j % nb, 0)

    def fn(j, kc_ref, ins, outs):
        g_v = ins[1][...]
        outs[0][...] = g_v
        outs[1][...], outs[2][...], outs[3][...] = _adamw_math(ins[0][...], g_v, ins[2][...], ins[3][...])

    return Job(n_layers * nb, [(a, block, index) for a in (w, g, m, v)],
               [(_sds(w.shape), block, index)] * 4, fn)


def _chip_sum_fused_job(ts, fused, by_cols):
    t0 = ts[0]
    own0 = fused[t0.name][0]
    if by_cols:
        rows, cols = own0.shape
    else:
        nb, rows, bw = own0.shape
        cols = nb * bw
    nj = rows // t0.rb

    def local(j, li):
        return jnp.clip(j - li * nj, 0, nj - 1)

    ins = []
    for li, t in enumerate(ts):
        own, land = fused[t.name]
        if by_cols:
            ins.append((own, (t.rb, cols), lambda j, kc_ref, li=li: (local(j, li), 0)))
            ins.append((land, (N_CHIPS - 1, t.rb, cols), lambda j, kc_ref, li=li: (0, local(j, li), 0)))
        else:
            ins.append((own, (nb, t.rb, bw), lambda j, kc_ref, li=li: (0, local(j, li), 0)))
            ins.append((land, (N_CHIPS - 1, nb, t.rb, bw), lambda j, kc_ref, li=li: (0, 0, local(j, li), 0)))

    def fn(j, kc_ref, in_refs, outs):
        for li in range(len(ts)):
            @pl.when(j // nj == li)
            def _():
                acc = in_refs[2 * li][...].astype(F32)
                for k in range(N_CHIPS - 1):
                    acc = acc + in_refs[2 * li + 1][k].astype(F32)
                outs[0][...] = acc if by_cols else jnp.concatenate([acc[b] for b in range(nb)], axis=1)

    def out_map(j, kc_ref):
        return (j // nj, j % nj, kc_ref[1]) if by_cols else (j // nj, kc_ref[1] * nj + j % nj, 0)

    return Job(len(ts) * nj, ins, [(_sds((len(ts), t0.R, t0.C)), (None, t0.rb, cols), out_map)], fn)


def _share_rider(halves, by_cols):
    nt = len(halves)

    def copy(outs, sems, ti, core, sibling):
        axis = 2 if by_cols[ti] else 1
        half = halves[ti].shape[axis] // 2
        piece = pl.ds(pl.multiple_of(core * half, 128 if by_cols[ti] else 8), half)
        part = outs[ti].at[:, :, piece] if by_cols[ti] else outs[ti].at[:, piece, :]
        return pltpu.make_async_remote_copy(part, part, sems[0].at[ti], sems[1].at[ti],
                                            device_id=sibling, device_id_type=MESH)

    def start(ins, outs, sems):
        x, y, c, _ = _mesh_position()
        for ti in range(nt):
            copy(outs, sems, ti, c, (x, y, 1 - c)).start()

    def finish(ins, outs, sems):
        x, y, c, _ = _mesh_position()
        for ti in range(nt):
            copy(outs, sems, ti, 1 - c, (x, y, 1 - c)).wait_recv()
        for ti in range(nt):
            copy(outs, sems, ti, c, (x, y, 1 - c)).wait_send()

    sems = pltpu.SemaphoreType.DMA((nt,))
    return Rider(list(halves), [_sds(a.shape, a.dtype) for a in halves], {i: i for i in range(nt)}, [sems, sems],
                 start, None, finish)


def _both(r1, r2):
    assert r1.mid is None and r2.mid is None
    ni, no, ns = len(r1.arrays), len(r1.out_shapes), len(r1.scratch)

    def split(fn1, fn2):
        def run(ins, outs, scr):
            fn1(ins[:ni], outs[:no], scr[:ns])
            fn2(ins[ni:], outs[no:], scr[ns:])
        return run

    aliases = dict(r1.aliases)
    aliases.update({ni + a: no + b for a, b in r2.aliases.items()})
    return Rider(r1.arrays + r2.arrays, r1.out_shapes + r2.out_shapes, aliases, r1.scratch + r2.scratch,
                 split(r1.start, r2.start), None, split(r1.finish, r2.finish))


def _adamw_math(w, g, m, v):
    m = B1 * m + (1.0 - B1) * g
    v = B2 * v + (1.0 - B2) * (g * g)
    delta = -LR * ((m / BC1) / (jnp.sqrt(v / BC2) + AEPS) + WD * w)
    return delta, m, v


def _adamw(name, rb, w, g, m, v):
    n_layers, r, c = w.shape

    def body(w_ref, g_ref, m_ref, v_ref, go_ref, d_ref, nm_ref, nv_ref):
        g_v = g_ref[...]
        go_ref[...] = g_v
        d_ref[...], nm_ref[...], nv_ref[...] = _adamw_math(w_ref[...], g_v, m_ref[...], v_ref[...])

    spec = pl.BlockSpec((None, rb, c), lambda l, j: (l, j, 0))
    return pl.pallas_call(
        body, name=f"adamw_{name}", grid=(n_layers, r // rb),
        in_specs=[spec] * 4, out_specs=[spec] * 4, out_shape=[_sds(w.shape)] * 4,
        compiler_params=_params(2),
    )(w, g, m, v)


GAIN_ROWS = {"pre_mix_g": 0, "post_mix_g": 2, "pre_ffn_g": 4, "post_ffn_g": 6, "ple_g": 8, "ple_post_g": 10}
ROW_KV_G, ROW_POOL_SCALE, ROW_SINKS, ROW_LOSS, PACK_ROWS = 12, 13, 14, 15, 16
SMALL_NAMES = tuple(GAIN_ROWS) + ("kv_g", "pool_scale", "sinks")


def _small_all_reduce(rows, dpool, rider=None):
    ng, pr = len(WINDOWS), POOL_G // N_CHIPS

    def body(*refs):
        row_refs = refs[:PACK_ROWS]
        dpool_ref, tot_ref, gpool_ref, pack, land, pland, send, recv, psend, precv = refs[PACK_ROWS:]
        x, y, c, _ = _mesh_position()
        me = 4 * x + 2 * y + c
        for r in range(PACK_ROWS):
            pack[r:r + 1, :] = row_refs[r][...]

        def shard_of(k):
            return dpool_ref.at[:, pl.ds(pl.multiple_of(k * pr, pr), pr), :]

        cps = []
        for j in range(1, N_DEV):
            px, py, pc = x ^ (j >> 2), y ^ ((j >> 1) & 1), c ^ (j & 1)
            cps.append(pltpu.make_async_remote_copy(pack, land.at[me], send.at[j], recv.at[j],
                                                    device_id=(px, py, pc), device_id_type=MESH))
            cps.append(pltpu.make_async_remote_copy(shard_of(2 * px + py), pland.at[me], psend.at[j], precv.at[j],
                                                    device_id=(px, py, pc), device_id_type=MESH))
        for cp in cps:
            cp.start()
        land[me] = pack[...]
        pland[me] = dpool_ref[:, pl.ds(pl.multiple_of((2 * x + y) * pr, pr), pr), :]
        for j in range(1, N_DEV):
            pltpu.make_async_remote_copy(pack, land.at[me ^ j], send.at[j], recv.at[j],
                                         device_id=(x, y, c), device_id_type=MESH).wait_recv()
            pltpu.make_async_remote_copy(shard_of(0), pland.at[me ^ j], psend.at[j], precv.at[j],
                                         device_id=(x, y, c), device_id_type=MESH).wait_recv()
        for cp in cps:
            cp.wait_send()
        tot = land[0]
        gp = pland[0].astype(F32)
        for d in range(1, N_DEV):
            tot = tot + land[d]
            gp = gp + pland[d].astype(F32)
        tot_ref[...] = tot
        gpool_ref[...] = gp

    sems = pltpu.SemaphoreType.DMA((N_DEV,))
    return _call(
        body, name="small_all_reduce", grid=(1,),
        in_specs=[VSPEC] * (PACK_ROWS + 1), out_specs=[VSPEC, VSPEC],
        out_shape=[_sds((PACK_ROWS, D)), _sds((ng, pr, POOL_G))],
        scratch_shapes=[pltpu.VMEM((PACK_ROWS, D), F32), pltpu.VMEM((N_DEV, PACK_ROWS, D), F32),
                        pltpu.VMEM((N_DEV, ng, pr, POOL_G), BF), sems, sems, sems, sems],
        args=[*rows, dpool], rider=rider)


def _small_adamw(tot, kc, small_w, small_m, small_v):
    names = SMALL_NAMES
    n = len(names)

    def body(*refs):
        tot_ref, kc_ref = refs[0], refs[1]
        w_refs = dict(zip(names, refs[2:2 + n]))
        m_refs = dict(zip(names, refs[2 + n:2 + 2 * n]))
        v_refs = dict(zip(names, refs[2 + 2 * n:2 + 3 * n]))
        loss_ref = refs[2 + 3 * n]
        out_refs = {nm: refs[3 + 3 * n + 4 * k: 7 + 3 * n + 4 * k] for k, nm in enumerate(names)}
        tot = tot_ref[...]
        loss_ref[...] = 0.5 * jnp.sum(tot[ROW_LOSS:ROW_LOSS + 1, :], axis=-1, keepdims=True) * (1.0 / D)

        def update(nm, g):
            g_ref, d_ref, nm_ref, nv_ref = out_refs[nm]
            g_ref[...] = g
            d_ref[...], nm_ref[...], nv_ref[...] = _adamw_math(w_refs[nm][...], g, m_refs[nm][...], v_refs[nm][...])

        for nm, r in GAIN_ROWS.items():
            update(nm, tot[r:r + 2, :])
        update("kv_g", tot[ROW_KV_G:ROW_KV_G + 1, :])
        k = kc_ref[0]
        width = D // N_CHIPS
        g_scale = jnp.zeros((1, width), F32)
        for kk in range(N_CHIPS):
            g_scale = g_scale + jnp.where(k == kk, tot[ROW_POOL_SCALE:ROW_POOL_SCALE + 1, kk * width:(kk + 1) * width], 0.0)
        update("pool_scale", g_scale)
        update("sinks", tot[ROW_SINKS:ROW_SINKS + 1, 0:N_HEADS])

    ins = [tot, kc] + [small_w[nm] for nm in names] + [small_m[nm] for nm in names] + [small_v[nm] for nm in names]
    out_shape = [_sds((1, 1))]
    for nm in names:
        out_shape += [_sds(small_w[nm].shape)] * 4
    outs = pl.pallas_call(
        body, name="small_adamw",
        in_specs=[VSPEC, SSPEC] + [VSPEC] * (3 * n), out_specs=[VSPEC] * len(out_shape), out_shape=out_shape,
        compiler_params=_params(),
    )(*ins)
    return outs[0], {nm: outs[1 + 4 * k: 5 + 4 * k] for k, nm in enumerate(names)}


def _compute_layout(t, full):
    if t.src == "w_gu":
        return full.reshape(2, D, FF)
    if t.src == "pool_w":
        return full.reshape(len(WINDOWS), POOL_G, POOL_G)
    if t.src == "pool_scale":
        return full.reshape(1, D)
    return full.reshape(t.A * t.R, _ncb(t) * t.C)


def kernel(x, p, pre_mix_g, post_mix_g, pre_ffn_g, post_ffn_g, pool_w, pool_scale, kv_g, w_kv, w_q, sinks, w_o, w_gu, w_down, ple_g, w_ple_gate, w_ple_proj, ple_post_g, loss_target, m_pre_mix_g, m_post_mix_g, m_pre_ffn_g, m_post_ffn_g, m_pool_w, m_pool_scale, m_kv_g, m_w_kv, m_w_q, m_sinks, m_w_o, m_w_gu, m_w_down, m_ple_g, m_w_ple_gate, m_w_ple_proj, m_ple_post_g, v_pre_mix_g, v_post_mix_g, v_pre_ffn_g, v_post_ffn_g, v_pool_w, v_pool_scale, v_kv_g, v_w_kv, v_w_q, v_sinks, v_w_o, v_w_gu, v_w_down, v_ple_g, v_w_ple_gate, v_w_ple_proj, v_ple_post_g):
    weights = dict(pre_mix_g=pre_mix_g, post_mix_g=post_mix_g, pre_ffn_g=pre_ffn_g, post_ffn_g=post_ffn_g,
                   pool_w=pool_w, pool_scale=pool_scale, kv_g=kv_g, w_kv=w_kv, w_q=w_q, sinks=sinks, w_o=w_o,
                   w_gu=w_gu, w_down=w_down, ple_g=ple_g, w_ple_gate=w_ple_gate, w_ple_proj=w_ple_proj,
                   ple_post_g=ple_post_g)
    m_in = dict(pre_mix_g=m_pre_mix_g, post_mix_g=m_post_mix_g, pre_ffn_g=m_pre_ffn_g, post_ffn_g=m_post_ffn_g,
                pool_w=m_pool_w, pool_scale=m_pool_scale, kv_g=m_kv_g, w_kv=m_w_kv, w_q=m_w_q, sinks=m_sinks,
                w_o=m_w_o, w_gu=m_w_gu, w_down=m_w_down, ple_g=m_ple_g, w_ple_gate=m_w_ple_gate,
                w_ple_proj=m_w_ple_proj, ple_post_g=m_ple_post_g)
    v_in = dict(pre_mix_g=v_pre_mix_g, post_mix_g=v_post_mix_g, pre_ffn_g=v_pre_ffn_g, post_ffn_g=v_post_ffn_g,
                pool_w=v_pool_w, pool_scale=v_pool_scale, kv_g=v_kv_g, w_kv=v_w_kv, w_q=v_w_q, sinks=v_sinks,
                w_o=v_w_o, w_gu=v_w_gu, w_down=v_w_down, ple_g=v_ple_g, w_ple_gate=v_w_ple_gate,
                w_ple_proj=v_w_ple_proj, ple_post_g=v_ple_post_g)
    order = ["pre_mix_g", "post_mix_g", "pre_ffn_g", "post_ffn_g", "pool_w", "pool_scale", "kv_g", "w_kv", "w_q",
             "sinks", "w_o", "w_gu", "w_down", "ple_g", "w_ple_gate", "w_ple_proj", "ple_post_g"]

    kc = jnp.stack([2 * lax.axis_index("x") + lax.axis_index("y"), lax.axis_index("c")]).astype(jnp.int32)
    s_len = x.shape[1]
    x2d = x.reshape(s_len, D)
    p3d = p.reshape(2, s_len, PLE)
    target = loss_target.reshape(s_len, D)
    kv_g2d = kv_g.reshape(1, D)
    gains = {nm: weights[nm] for nm in GAIN_ROWS}

    def shard_view(src, a):
        t = next(t for t in BIGS.values() if t.src == src)
        return a.reshape(-1, t.R, t.C)

    first, second = ["pool_w", "pool_scale"], ["w_gu0", "w_down0"]
    rest = [nm for nm in BIGS if nm not in first + second]
    specs = dict(BIGS, pool_scale=POOL_SCALE)
    placed = {nm: _place(BIGS[nm], shard_view(BIGS[nm].src, weights[BIGS[nm].src]), kc, BF)
              for nm in first + second if nm in BIGS}
    placed["pool_scale"] = _place(POOL_SCALE, pool_scale.reshape(1, 1, D // N_CHIPS), kc, F32)

    def gather(names, rows=None):
        rows = rows or {}
        parts = [(specs[nm],) + tuple(rows.get(nm, (0, specs[nm].R))) for nm in names]
        return _gather_rider(parts, [placed[nm] for nm in names])

    def take(names, results):
        for nm, a in zip(names, results):
            placed[nm] = a

    def weight(nm):
        return _compute_layout(specs[nm], placed[nm])

    cast, got = _place_many([BIGS[nm] for nm in rest], [shard_view(BIGS[nm].src, weights[BIGS[nm].src]) for nm in rest],
                            kc, rider=gather(first))
    take(rest, cast)
    take(first, got)


    (y0, x1), got = _mixa_fwd(x2d, gains["pre_mix_g"], weight("pool_w"), weight("pool_scale"), gains["post_mix_g"],
                              rider=gather(second))
    take(second, got)

    ride = ["w_ple_gate0", "w_ple_proj0", "w_q", "w_kv", "w_o", "w_gu1"]
    (f0, x2, g0, u0), got = _ffn_fwd(0, x1, gains["pre_ffn_g"], weight("w_gu0"), weight("w_down0"), gains["post_ffn_g"],
                             rider=gather(ride, {"w_gu1": (0, 320)}))
    take(ride, got)

    ride = ["w_ple_gate1", "w_ple_proj1", "w_gu1"]
    (z0, pe0, x3, q, kv), got = _ple_fwd(
        0, x2, p3d, gains["ple_g"], weight("w_ple_gate0"), weight("w_ple_proj0"), gains["ple_post_g"],
        qkv=(gains["pre_mix_g"], kv_g2d, weight("w_q"), weight("w_kv")),
        rider=gather(ride, {"w_gu1": (320, 704)}))
    take(ride, got)

    ride = ["w_down1", "w_gu1"]
    (attn, y1, x4), got = _attn_fwd(q, kv, sinks, x3, weight("w_o"), gains["post_mix_g"],
                                    rider=gather(ride, {"w_gu1": (704, D)}))
    take(ride, got)

    (f1, x5, g1, u1), _ = _ffn_fwd(1, x4, gains["pre_ffn_g"], weight("w_gu1"), weight("w_down1"), gains["post_ffn_g"])
    (z1, pe1, dx6, loss_row), _ = _ple_fwd(1, x5, p3d, gains["ple_g"], weight("w_ple_gate1"), weight("w_ple_proj1"),
                                           gains["ple_post_g"], target=target)

    local = {}
    landed = {}
    fused = {}

    def pair_stage(tag, names):
        ts = [BIGS[nm] for nm in names]
        gs = [local[nm].reshape(_full_shape(t)) for nm, t in zip(names, ts)]
        lands = _pair_exchange(f"grads_pair_exchange_{tag}", ts, gs)
        jobs = [_pair_sum_job(t, g, l) for t, g, l in zip(ts, gs, lands)]
        return [r[0] for r in _multi_call(f"pair_sum_{tag}", jobs, kc)]

    def scatter(names, sums):
        return _scatter_rider([BIGS[nm] for nm in names], sums)

    def keep(names, sums, got):
        for nm, s, l in zip(names, sums, got):
            landed[nm] = (s, l)

    (dx5, local["w_ple_gate1"], local["w_ple_proj1"], d_ple1, d_plepost1), _ = _ple_bwd(
        1, dx6, x5, z1, pe1, p3d, gains["ple_g"], weight("w_ple_gate1"), gains["ple_post_g"])

    group_a = ["w_ple_gate1", "w_ple_proj1"]
    sums_a = pair_stage("a", group_a)
    (dx4, d_preffn1, d_postffn1, *scattered), _ = _ffn_bwd(
        1, dx5, x4, f1, g1, u1, gains["pre_ffn_g"], weight("w_gu1"), weight("w_down1"), gains["post_ffn_g"], kc)
    fused["w_gu1"], fused["w_down1"] = scattered[0:2], scattered[2:4]

    (dq, dkv, local["w_o"], d_postmix1, d_sinks), got = _attn_bwd(
        dx4, y1, attn, q, kv, sinks, weight("w_o"), gains["post_mix_g"], rider=scatter(group_a, sums_a))
    keep(group_a, sums_a, got)
    dx3, local["w_q"], local["w_kv"], d_premix1, d_kvg = _qkv_bwd(
        dq, dkv, x3, dx4, gains["pre_mix_g"], kv_g2d, weight("w_q"), weight("w_kv"))

    group_b = ["w_o", "w_q", "w_kv"]
    sums_b = pair_stage("b", group_b)
    (dx2, local["w_ple_gate0"], local["w_ple_proj0"], d_ple0, d_plepost0), got = _ple_bwd(
        0, dx3, x2, z0, pe0, p3d, gains["ple_g"], weight("w_ple_gate0"), gains["ple_post_g"],
        rider=scatter(group_b, sums_b))
    keep(group_b, sums_b, got)

    group_c = ["w_ple_gate0", "w_ple_proj0"]
    sums_c = pair_stage("c", group_c)
    (dx1, d_preffn0, d_postffn0, *scattered), _ = _ffn_bwd(
        0, dx2, x1, f0, g0, u0, gains["pre_ffn_g"], weight("w_gu0"), weight("w_down0"), gains["post_ffn_g"], kc)
    fused["w_gu0"], fused["w_down0"] = scattered[0:2], scattered[2:4]

    (dx0, d_pool, d_scale, d_postmix0, d_premix0), _ = _mixa_bwd(
        dx1, x2d, y0, gains["pre_mix_g"], weight("pool_w"), weight("pool_scale"), gains["post_mix_g"])

    rows = [d_premix0, d_premix1, d_postmix0, d_postmix1, d_preffn0, d_preffn1, d_postffn0, d_postffn1,
            d_ple0, d_ple1, d_plepost0, d_plepost1, d_kvg, d_scale, d_sinks, loss_row]
    as2d = lambda a: a.reshape(1, D) if a.ndim == 1 else a
    layers_of = lambda src: [t for t in BIGS.values() if t.src == src]
    own_scatter = ["w_gu", "w_down"]
    early = own_scatter + ["w_q", "w_o", "w_kv"]
    late = ["w_ple_gate", "w_ple_proj"]
    by_cols = lambda srcs: [src == "w_down" for src in srcs]
    jobs = [_chip_sum_fused_job(layers_of(src), fused, by_cols=src == "w_down") for src in own_scatter]
    jobs += [_chip_sum_job(layers_of(src), landed) for src in early if src not in own_scatter]
    halves = [r[0] for r in _multi_call("chip_sum_early", jobs, kc)]
    (tot, g_pool), got = _small_all_reduce(
        rows, d_pool, rider=_both(scatter(group_c, sums_c), _share_rider(halves, by_cols(early))))
    keep(group_c, sums_c, got[:len(group_c)])
    full_grads = dict(zip(early, got[len(group_c):]))
    loss, small = _small_adamw(tot, kc, {nm: as2d(weights[nm]) for nm in SMALL_NAMES},
                               {nm: as2d(m_in[nm]) for nm in SMALL_NAMES},
                               {nm: as2d(v_in[nm]) for nm in SMALL_NAMES})

    halves = [r[0] for r in _multi_call("chip_sum_late", [_chip_sum_job(layers_of(src), landed) for src in late], kc)]
    full_grads.update(zip(late, _run("grads_pair_share", _share_rider(halves, by_cols(late)))))
    full_grads["pool_w"] = g_pool
    others = [src for src in BIG_SOURCES if src not in own_scatter and src != "pool_w"]

    def adam_args(src):
        return (layers_of(src)[0].rb, shard_view(src, weights[src]), full_grads[src],
                shard_view(src, m_in[src]), shard_view(src, v_in[src]))

    out = {"grad": {}, "delta": {}, "new_m": {}, "new_v": {}}
    results = {src: _adamw(src, *adam_args(src)) for src in own_scatter}
    rest_srcs = others + ["pool_w"]
    results.update(zip(rest_srcs, _multi_call("adamw_rest", [_adamw_job(*adam_args(src)) for src in rest_srcs], kc)))
    for src in BIG_SOURCES:
        shape = weights[src].shape
        for kind, a in zip(("grad", "delta", "new_m", "new_v"), results[src]):
            out[kind][src] = a.reshape(shape)
    for nm in SMALL_NAMES:
        shape = weights[nm].shape
        for kind, a in zip(("grad", "delta", "new_m", "new_v"), small[nm]):
            out[kind][nm] = a.reshape(shape)

    return (loss.reshape(()), dx0.reshape(x.shape),
            *[out["grad"][nm] for nm in order], *[out["delta"][nm] for nm in order],
            *[out["new_m"][nm] for nm in order], *[out["new_v"][nm] for nm in order])
```

```python
import collections

import jax
import jax.numpy as jnp
from jax import lax
from jax.experimental import pallas as pl
from jax.experimental.pallas import tpu as pltpu

D = 1024
FF = 2816
N_HEADS = 16
HEAD_DIM = 64
N_KV_HEADS = 4
GQA = N_HEADS // N_KV_HEADS
KVD = N_KV_HEADS * HEAD_DIM
PLE = 256
BLK = 128
WINDOWS = (2, 4, 8, 16)
POOL_G = 256
HALO = 16
EPS = 1e-6
NEG_INF = -1e30
ATT_SCALE = HEAD_DIM ** -0.5
SLOPES = tuple(2.0 ** (-8.0 * (h + 1) / N_HEADS) for h in range(N_HEADS))
N_CHIPS = 4
N_DEV = 8

LR, B1, B2, AEPS, WD, STEP = 0.001, 0.9, 0.999, 1e-08, 0.01, 10
BC1 = 1.0 - B1 ** STEP
BC2 = 1.0 - B2 ** STEP

BF = jnp.bfloat16
F32 = jnp.float32
MESH = pl.DeviceIdType.MESH
VMEM_LIMIT_V7X = 58 * 1024 * 1024
TM = 256
TM_FFN_BWD = 512
FF_CHUNK = 256
FF_HALF = FF // 2

VSPEC = pl.BlockSpec(memory_space=pltpu.VMEM)
SSPEC = pl.BlockSpec(memory_space=pltpu.SMEM)
ANYSPEC = pl.BlockSpec(memory_space=pl.ANY)


def _params(n_grid=0):
    sem = ("arbitrary",) * n_grid if n_grid else None
    return pltpu.CompilerParams(dimension_semantics=sem, vmem_limit_bytes=VMEM_LIMIT_V7X)


def _sds(shape, dtype=F32):
    return jax.ShapeDtypeStruct(tuple(shape), dtype)


Rider = collections.namedtuple("Rider", "arrays out_shapes aliases scratch start mid finish")
MID_NUM, MID_DEN = 5, 8


def _call(body, *, name, grid, in_specs, out_specs, out_shape, args, scratch_shapes=(), rider=None, prefetch=None):
    ni, no, ns = len(in_specs), len(out_specs), len(scratch_shapes)
    npre = 0 if prefetch is None else 1
    pre = [] if prefetch is None else [prefetch]
    if rider is None:
        rider = Rider([], [], {}, [], None, None, None)
    ri, ro = len(rider.arrays), len(rider.out_shapes)

    def full(*refs):
        pre_refs, refs = refs[:npre], refs[npre:]
        ins, refs = refs[:ni], refs[ni:]
        rins, refs = refs[:ri], refs[ri:]
        outs, refs = refs[:no], refs[no:]
        routs, refs = refs[:ro], refs[ro:]
        scr, rscr = refs[:ns], refs[ns:]
        ids = [pl.program_id(a) for a in range(len(grid))]
        first = ids[0] == 0
        last = ids[0] == grid[0] - 1
        for a in range(1, len(grid)):
            first = first & (ids[a] == 0)
            last = last & (ids[a] == grid[a] - 1)

        if rider.start is not None:
            @pl.when(first)
            def _():
                rider.start(rins, routs, rscr)

        if rider.mid is not None:
            assert len(grid) == 1

            @pl.when(ids[0] == (grid[0] * MID_NUM) // MID_DEN)
            def _():
                rider.mid(rins, routs, rscr)

        body(*pre_refs, *ins, *outs, *scr)

        if rider.finish is not None:
            @pl.when(last)
            def _():
                rider.finish(rins, routs, rscr)

    outs = pl.pallas_call(
        full, name=name,
        grid_spec=pltpu.PrefetchScalarGridSpec(
            num_scalar_prefetch=npre, grid=grid,
            in_specs=list(in_specs) + [ANYSPEC] * ri, out_specs=list(out_specs) + [ANYSPEC] * ro,
            scratch_shapes=list(scratch_shapes) + list(rider.scratch)),
        out_shape=list(out_shape) + list(rider.out_shapes),
        input_output_aliases={npre + ni + a: no + b for a, b in rider.aliases.items()},
        compiler_params=_params(len(grid)))(*pre, *args, *rider.arrays)
    return list(outs[:no]), list(outs[no:])


def _run(name, rider):
    ri = len(rider.arrays)

    def body(*refs):
        rins, routs, rscr = refs[:ri], refs[ri:ri + len(rider.out_shapes)], refs[ri + len(rider.out_shapes):]
        rider.start(rins, routs, rscr)
        if rider.mid is not None:
            rider.mid(rins, routs, rscr)
        rider.finish(rins, routs, rscr)

    return pl.pallas_call(
        body, name=name, in_specs=[ANYSPEC] * ri, out_specs=[ANYSPEC] * len(rider.out_shapes),
        out_shape=list(rider.out_shapes), scratch_shapes=list(rider.scratch),
        input_output_aliases=dict(rider.aliases), compiler_params=_params())(*rider.arrays)


Job = collections.namedtuple("Job", "steps ins outs fn")


def _multi_call(name, jobs, kc, rider=None):
    n = max(job.steps for job in jobs)

    def clamped(index, steps):
        return lambda s, kc_ref: index(jnp.minimum(s, steps - 1), kc_ref)

    in_specs, out_specs, out_shape, args = [], [], [], []
    for job in jobs:
        for arr, block, index, *single in job.ins:
            mode = dict(pipeline_mode=pl.Buffered(1)) if single and single[0] else {}
            in_specs.append(pl.BlockSpec(block, clamped(index, job.steps), **mode))
            args.append(arr)
        for sds, block, index in job.outs:
            out_specs.append(pl.BlockSpec(block, clamped(index, job.steps)))
            out_shape.append(sds)
    n_in = len(args)

    def body(kc_ref, *refs):
        s = pl.program_id(0)
        i0, o0 = 0, n_in
        for job in jobs:
            ins, outs = refs[i0:i0 + len(job.ins)], refs[o0:o0 + len(job.outs)]
            i0, o0 = i0 + len(job.ins), o0 + len(job.outs)

            @pl.when(s < job.steps)
            def _():
                job.fn(s, kc_ref, ins, outs)

    outs, routs = _call(body, name=name, grid=(n,), in_specs=in_specs, out_specs=out_specs, out_shape=out_shape,
                        args=args, prefetch=kc, rider=rider)
    res, o0 = [], 0
    for job in jobs:
        res.append(outs[o0:o0 + len(job.outs)])
        o0 += len(job.outs)
    return res if rider is None else (res, routs)


def _rms_fwd(x, g):
    r = lax.rsqrt(jnp.mean(x * x, axis=-1, keepdims=True) + EPS)
    return x * r * g


def _rms_bwd(x, g, dy):
    r = lax.rsqrt(jnp.mean(x * x, axis=-1, keepdims=True) + EPS)
    xn = x * r
    dxn = dy * g
    dx = r * (dxn - xn * jnp.mean(dxn * xn, axis=-1, keepdims=True))
    return dx, dy * xn


def _rowsum(a):
    return jnp.sum(a, axis=0, keepdims=True)


def _sigmoid(z):
    return 1.0 / (1.0 + jnp.exp(-z))


def _dot(a, b):
    return jnp.dot(a, b, preferred_element_type=F32)


def _dot_nt(a, b):
    return lax.dot_general(a, b, (((1,), (1,)), ((), ())), preferred_element_type=F32)


def _dot_tn(a, b):
    return lax.dot_general(a, b, (((0,), (0,)), ((), ())), preferred_element_type=F32)


def _row_spec(tm, width=D):
    return pl.BlockSpec((tm, width), lambda i: (i, 0))


def _const_spec(shape):
    zeros = (0,) * len(shape)
    return pl.BlockSpec(tuple(shape), lambda *_: zeros)


def _pool_delta(he, pos):
    out = []
    for gi, w in enumerate(WINDOWS):
        hg = he[:, gi * POOL_G:(gi + 1) * POOL_G]
        s = hg
        k = 1
        while k < w:
            s = s + pltpu.roll(s, k, 0)
            k *= 2
        cnt = jnp.maximum(jnp.minimum(pos + 1, w), 1).astype(F32)
        out.append(s / cnt - hg)
    return out


def _load_with_halo_before(x_ref, i, tm):
    r0 = pl.multiple_of(i * tm, tm)
    hs = pl.multiple_of(jnp.maximum(i * tm - HALO, 0), 8)
    xh = jnp.where(i > 0, x_ref[pl.ds(hs, HALO), :], 0.0)
    xt = x_ref[pl.ds(r0, tm), :]
    return xt, jnp.concatenate([xh, xt], axis=0)


def _mixa_fwd_job(x, pre_g, pool_w, pool_scale, post_g):
    s_len = x.shape[0]

    def fn(i, kc_ref, ins, outs):
        x_ref, pg_ref, w_ref, sc_ref, qg_ref = ins
        y_ref, x1_ref = outs
        xt, xe = _load_with_halo_before(x_ref, i, TM)
        he = _rms_fwd(xe, pg_ref[0:1, :])
        pos = i * TM - HALO + lax.broadcasted_iota(jnp.int32, (TM + HALO, 1), 0)
        ds = _pool_delta(he, pos)
        ys = [_dot(ds[gi][HALO:, :].astype(BF), w_ref[gi]) for gi in range(len(WINDOWS))]
        y = jnp.concatenate(ys, axis=1) * sc_ref[...]
        y_ref[...] = y
        x1_ref[...] = xt + _rms_fwd(y, qg_ref[0:1, :])

    def whole(a):
        zeros = (0,) * a.ndim
        return (a, a.shape, lambda j, kc_ref: zeros, True)

    rows = lambda j, kc_ref: (j, 0)
    return Job(s_len // TM, [whole(a) for a in (x, pre_g, pool_w, pool_scale, post_g)],
               [(_sds((s_len, D)), (TM, D), rows), (_sds((s_len, D)), (TM, D), rows)], fn)


def _mixa_bwd(dx1, x, y, pre_g, pool_w, pool_scale, post_g, rider=None):
    s_len = x.shape[0]
    n = s_len // TM
    ng = len(WINDOWS)

    def body(dx_ref, x_ref, y_ref, pg_ref, w_ref, sc_ref, qg_ref,
             dx0_ref, dw_ref, dsc_ref, dqg_ref, dpg_ref, wacc):
        i = pl.program_id(0)

        @pl.when(i == 0)
        def _():
            wacc[...] = jnp.zeros_like(wacc)
            dsc_ref[...] = jnp.zeros_like(dsc_ref)
            dqg_ref[...] = jnp.zeros_like(dqg_ref)
            dpg_ref[...] = jnp.zeros_like(dpg_ref)

        r0 = pl.multiple_of(i * TM, TM)
        xt, xe = _load_with_halo_before(x_ref, i, TM)
        he = _rms_fwd(xe, pg_ref[0:1, :])
        pos_b = i * TM - HALO + lax.broadcasted_iota(jnp.int32, (TM + HALO, 1), 0)
        ds = _pool_delta(he, pos_b)

        last = i == n - 1
        a0 = pl.multiple_of(jnp.minimum(i * TM + TM, s_len - HALO), 8)
        ye = jnp.concatenate([y_ref[pl.ds(r0, TM), :], y_ref[pl.ds(a0, HALO), :]], axis=0)
        dt = dx_ref[pl.ds(r0, TM), :]
        de = jnp.concatenate([dt, jnp.where(last, 0.0, dx_ref[pl.ds(a0, HALO), :])], axis=0)
        dye, prod = _rms_bwd(ye, qg_ref[0:1, :], de)
        dqg_ref[...] += _rowsum(prod[:TM, :])
        dys = dye * sc_ref[...]
        pos_a = i * TM + lax.broadcasted_iota(jnp.int32, (TM + HALO, 1), 0)

        dhs, dscs = [], []
        for gi, w in enumerate(WINDOWS):
            sl = slice(gi * POOL_G, (gi + 1) * POOL_G)
            wg = w_ref[gi]
            dys_g = dys[:, sl].astype(BF)
            d_g = ds[gi][HALO:, :].astype(BF)
            ypre = _dot(d_g, wg)
            dscs.append(_rowsum(dye[:TM, sl] * ypre))
            wacc[gi] += _dot_tn(d_g, dys_g[:TM, :])
            dd = _dot_nt(dys_g, wg)
            cnt = jnp.minimum(pos_a + 1, w).astype(F32)
            a = dd / cnt
            k = 1
            while k < w:
                a = a + pltpu.roll(a, TM + HALO - k, 0)
                k *= 2
            dhs.append(a[:TM, :] - dd[:TM, :])
        dsc_ref[...] += jnp.concatenate(dscs, axis=1)
        dh = jnp.concatenate(dhs, axis=1)
        dxp, prod2 = _rms_bwd(xt, pg_ref[0:1, :], dh)
        dpg_ref[...] += _rowsum(prod2)
        dx0_ref[...] = dt + dxp

        @pl.when(last)
        def _():
            dw_ref[...] = wacc[...].astype(BF)

    return _call(
        body, name="mixa_bwd", grid=(n,), in_specs=[VSPEC] * 7,
        out_specs=[_row_spec(TM), _const_spec((ng, POOL_G, POOL_G)), _const_spec((1, D)),
                   _const_spec((1, D)), _const_spec((1, D))],
        out_shape=[_sds((s_len, D)), _sds((ng, POOL_G, POOL_G), BF), _sds((1, D)), _sds((1, D)), _sds((1, D))],
        scratch_shapes=[pltpu.VMEM((ng, POOL_G, POOL_G), F32)],
        args=[dx1, x, y, pre_g, pool_w, pool_scale, post_g], rider=rider)


def _ffn_fwd(layer, x1, pre_g, wgu, wd, post_g, rider=None):
    s_len = x1.shape[0]

    def body(x_ref, pg_ref, wgu_ref, wd_ref, qg_ref, f_ref, x2_ref, g_ref, u_ref):
        x = x_ref[...]
        h = _rms_fwd(x, pg_ref[layer:layer + 1, :]).astype(BF)
        f = jnp.zeros((TM, D), F32)
        for c in range(FF // FF_HALF):
            cols = slice(c * FF_HALF, (c + 1) * FF_HALF)
            g = _dot(h, wgu_ref[0, :, cols])
            u = _dot(h, wgu_ref[1, :, cols])
            g_ref[:, cols] = g.astype(BF)
            u_ref[:, cols] = u.astype(BF)
            act = g * _sigmoid(g) * u
            f = f + _dot(act.astype(BF), wd_ref[cols, :])
        f_ref[...] = f
        x2_ref[...] = x + _rms_fwd(f, qg_ref[layer:layer + 1, :])

    return _call(body, name=f"ffn_fwd{layer}", grid=(s_len // TM,),
                 in_specs=[_row_spec(TM), VSPEC, VSPEC, VSPEC, VSPEC],
                 out_specs=[_row_spec(TM), _row_spec(TM), _row_spec(TM, FF), _row_spec(TM, FF)],
                 out_shape=[_sds((s_len, D)), _sds((s_len, D)), _sds((s_len, FF), BF), _sds((s_len, FF), BF)],
                 args=[x1, pre_g, wgu, wd, post_g], rider=rider)


GU_PIECE = 128
DN_PIECE = 64
DN_SLOT = FF // N_CHIPS
HALF_D = D // 2


def _ffn_bwd(layer, dx2, x1, f, g_pre, u_pre, pre_g, wgu, wd, post_g, kc, rider=None):
    s_len = x1.shape[0]
    tm = TM_FFN_BWD
    n = s_len // tm
    nc = FF // FF_CHUNK
    n_gu, n_dn = FF_CHUNK // GU_PIECE, FF_CHUNK // DN_PIECE
    n_pieces = 2 * n_gu + n_dn
    n_blk = FF_HALF // GU_PIECE

    def edge_rows(c, i, kc_ref):
        return (jnp.where((c == 0) | (c == nc - 1), i, n - 1), 0)

    def chunk_at(c, kc_ref):
        return (c + (((kc_ref[0] + 1) % N_CHIPS) * nc) // N_CHIPS) % nc

    def exchange(kc_ref, c, accg, accu, accd, own_gu_ref, land_gu_ref, own_dn_ref, land_dn_ref,
                 pl_gu, pl_dn, sib_gu, sib_dn, mine_gu, mine_dn, sum_gu, sum_dn,
                 psend, precv, ssend, lsem, rrecv):
        x, y, core = lax.axis_index("x"), lax.axis_index("y"), lax.axis_index("c")
        lower = core == 0

        def pair_copy(cc, part):
            p = cc % 2
            src, dst = ((sib_gu, pl_gu), (sib_dn, pl_dn))[part]
            return pltpu.make_async_remote_copy(src.at[p], dst.at[cc], psend.at[p, part], precv.at[cc, part],
                                                device_id=(x, y, 1 - core), device_id_type=MESH)

        def scatter(cc, wait):
            p = cc % 2
            hidden = chunk_at(cc, kc_ref) * FF_CHUNK

            assert n_gu == 2
            k0, k1 = hidden // FF_HALF, (hidden + GU_PIECE) // FF_HALF
            blk = (hidden - k0 * FF_HALF) // GU_PIECE
            for gu in range(2):
                @pl.when(k0 == k1)
                def _():
                    piece(p, wait, 2 * gu, sum_gu.at[p, gu], k0 + 2 * gu, 0, (pl.ds(blk, 2),))

                @pl.when(k0 != k1)
                def _():
                    piece(p, wait, 2 * gu, sum_gu.at[p, gu, 0], k0 + 2 * gu, 0, (blk,))
                    piece(p, wait, 2 * gu + 1, sum_gu.at[p, gu, 1], k1 + 2 * gu, 0, (0,))

            kd = hidden // DN_SLOT
            off = pl.multiple_of(hidden - kd * DN_SLOT, DN_PIECE)
            m = jnp.minimum((DN_SLOT - off) // DN_PIECE, n_dn)
            for mm in range(1, n_dn + 1):
                @pl.when(m == mm)
                def _():
                    rows = mm * DN_PIECE
                    piece(p, wait, 2 * n_gu, sum_dn.at[p, pl.ds(0, rows), :], kd, 1, (pl.ds(off, rows), slice(None)))
                    if mm < n_dn:
                        piece(p, wait, 2 * n_gu + 1, sum_dn.at[p, pl.ds(rows, FF_CHUNK - rows), :], kd + 1, 1,
                              (pl.ds(0, FF_CHUNK - rows), slice(None)))

        def piece(p, wait, pi, src, k, t, where):
            own_ref, land_ref = ((own_gu_ref, land_gu_ref), (own_dn_ref, land_dn_ref))[t]
            kx, ky = k // 2, k % 2
            fx, fy = (kx != x).astype(jnp.int32), (ky != y).astype(jnp.int32)
            local = (fx + fy) == 0
            j = jnp.maximum(fx + 2 * fy - 1, 0)

            @pl.when(local)
            def _():
                cp = pltpu.make_async_copy(src, own_ref.at[where], lsem.at[p, pi])
                if wait:
                    cp.wait()
                else:
                    cp.start()

            @pl.when(jnp.logical_not(local))
            def _():
                cp = pltpu.make_async_remote_copy(src, land_ref.at[(j,) + where], ssend.at[p, pi],
                                                  rrecv.at[t, j], device_id=(kx, ky, core), device_id_type=MESH)
                if wait:
                    cp.wait_send()
                else:
                    cp.start()

        def add_and_scatter(cc):
            p = cc % 2
            pair_copy(cc, 0).wait_recv()
            pair_copy(cc, 1).wait_recv()
            s_gu = (mine_gu[...] + pl_gu[cc].astype(F32)).astype(BF)
            for hc in range(n_gu):
                sum_gu[p, :, hc] = s_gu[:, :, hc * GU_PIECE:(hc + 1) * GU_PIECE]
            sum_dn[p] = (mine_dn[...] + pl_dn[cc].astype(F32)).astype(BF)
            scatter(cc, wait=False)

        @pl.when(c >= 1)
        def _():
            @pl.when(c >= 3)
            def _():
                scatter(c - 3, wait=True)
            add_and_scatter(c - 1)

        @pl.when(c >= 2)
        def _():
            pair_copy(c - 2, 0).wait_send()
            pair_copy(c - 2, 1).wait_send()

        p = c % 2
        my_rows = pl.ds(pl.multiple_of(core * HALF_D, HALF_D), HALF_D)
        sib_rows = pl.ds(pl.multiple_of((1 - core) * HALF_D, HALF_D), HALF_D)
        d_v = accd[...]
        sib_gu[p, 0] = accg[sib_rows, :].astype(BF)
        sib_gu[p, 1] = accu[sib_rows, :].astype(BF)
        sib_dn[p] = jnp.where(lower, d_v[:, HALF_D:], d_v[:, :HALF_D]).astype(BF)
        mine_gu[0] = accg[my_rows, :]
        mine_gu[1] = accu[my_rows, :]
        mine_dn[...] = jnp.where(lower, d_v[:, :HALF_D], d_v[:, HALF_D:])
        pair_copy(c, 0).start()
        pair_copy(c, 1).start()

        @pl.when(c == nc - 1)
        def _():
            scatter(nc - 3, wait=True)
            add_and_scatter(nc - 1)
            for cc in (nc - 2, nc - 1):
                pair_copy(cc, 0).wait_send()
                pair_copy(cc, 1).wait_send()
                scatter(cc, wait=True)
            for t, land_ref in enumerate((land_gu_ref, land_dn_ref)):
                for j in range(N_CHIPS - 1):
                    pltpu.make_async_remote_copy(land_ref.at[j], land_ref.at[j], ssend.at[0, 0], rrecv.at[t, j],
                                                 device_id=(x, y, core), device_id_type=MESH).wait_recv()

    def body(kc_ref, dx_ref, x_ref, f_ref, gp_ref, up_ref, pg_ref, wgu_ref, wd_ref, qg_ref,
             dx1_ref, dpg_ref, dqg_ref, own_gu_ref, land_gu_ref, own_dn_ref, land_dn_ref,
             h_s, df_s, dh_s, accg, accu, accd, *comm):
        c = pl.program_id(0)
        i = pl.program_id(1)
        rows = pl.ds(pl.multiple_of(i * tm, tm), tm)
        pg = pg_ref[layer:layer + 1, :]

        @pl.when((c == 0) & (i == 0))
        def _():
            dpg_ref[...] = jnp.zeros_like(dpg_ref)
            dqg_ref[...] = jnp.zeros_like(dqg_ref)

        @pl.when(c == 0)
        def _():
            h_s[rows, :] = _rms_fwd(x_ref[...], pg).astype(BF)
            df, prod = _rms_bwd(f_ref[...], qg_ref[layer:layer + 1, :], dx_ref[...])
            df_s[rows, :] = df.astype(BF)
            dqg_ref[...] += _rowsum(prod)

        @pl.when(i == 0)
        def _():
            accg[...] = jnp.zeros_like(accg)
            accu[...] = jnp.zeros_like(accu)
            accd[...] = jnp.zeros_like(accd)

        h = h_s[rows, :]
        df = df_s[rows, :]
        wg = wgu_ref[0]
        wu = wgu_ref[1]
        g = gp_ref[...].astype(F32)
        u = up_ref[...].astype(F32)
        sg = _sigmoid(g)
        a = g * sg
        dact = _dot_nt(df, wd_ref[...])
        accd[...] += _dot_tn((a * u).astype(BF), df)
        du = (dact * a).astype(BF)
        dg = (dact * u * (sg * (1.0 + g * (1.0 - sg)))).astype(BF)
        accg[...] += _dot_tn(h, dg)
        accu[...] += _dot_tn(h, du)
        dh = _dot_nt(dg, wg) + _dot_nt(du, wu)

        @pl.when(c == 0)
        def _():
            dh_s[rows, :] = dh

        @pl.when((c > 0) & (c < nc - 1))
        def _():
            dh_s[rows, :] += dh

        @pl.when(c == nc - 1)
        def _():
            dxp, prod = _rms_bwd(x_ref[...], pg, dh_s[rows, :] + dh)
            dpg_ref[...] += _rowsum(prod)
            dx1_ref[...] = dx_ref[...] + dxp

        @pl.when(i == n - 1)
        def _():
            exchange(kc_ref, c, accg, accu, accd, own_gu_ref, land_gu_ref, own_dn_ref, land_dn_ref, *comm)

    dma = pltpu.SemaphoreType.DMA
    return _call(
        body, name=f"ffn_bwd{layer}", grid=(nc, n),
        in_specs=[pl.BlockSpec((tm, D), edge_rows), pl.BlockSpec((tm, D), edge_rows),
                  pl.BlockSpec((tm, D), lambda c, i, kc_ref: (jnp.where(c == 0, i, n - 1), 0),
                               pipeline_mode=pl.Buffered(1)),
                  pl.BlockSpec((tm, FF_CHUNK), lambda c, i, kc_ref: (i, chunk_at(c, kc_ref))),
                  pl.BlockSpec((tm, FF_CHUNK), lambda c, i, kc_ref: (i, chunk_at(c, kc_ref))),
                  VSPEC,
                  pl.BlockSpec((2, D, FF_CHUNK), lambda c, i, kc_ref: (0, 0, chunk_at(c, kc_ref))),
                  pl.BlockSpec((FF_CHUNK, D), lambda c, i, kc_ref: (chunk_at(c, kc_ref), 0)),
                  VSPEC],
        out_specs=[pl.BlockSpec((tm, D), lambda c, i, kc_ref: (jnp.where(c == nc - 1, i, 0), 0)),
                   _const_spec((1, D)), _const_spec((1, D)), ANYSPEC, ANYSPEC, ANYSPEC, ANYSPEC],
        out_shape=[_sds((s_len, D)), _sds((1, D)), _sds((1, D)),
                   _sds((n_blk, HALF_D, GU_PIECE), BF), _sds((N_CHIPS - 1, n_blk, HALF_D, GU_PIECE), BF),
                   _sds((DN_SLOT, HALF_D), BF), _sds((N_CHIPS - 1, DN_SLOT, HALF_D), BF)],
        scratch_shapes=[pltpu.VMEM((s_len, D), BF), pltpu.VMEM((s_len, D), BF), pltpu.VMEM((s_len, D), F32),
                        pltpu.VMEM((D, FF_CHUNK), F32), pltpu.VMEM((D, FF_CHUNK), F32),
                        pltpu.VMEM((FF_CHUNK, D), F32),
                        pltpu.VMEM((nc, 2, HALF_D, FF_CHUNK), BF), pltpu.VMEM((nc, FF_CHUNK, HALF_D), BF),
                        pltpu.VMEM((2, 2, HALF_D, FF_CHUNK), BF), pltpu.VMEM((2, FF_CHUNK, HALF_D), BF),
                        pltpu.VMEM((2, HALF_D, FF_CHUNK), F32), pltpu.VMEM((FF_CHUNK, HALF_D), F32),
                        pltpu.VMEM((2, 2, n_gu, HALF_D, GU_PIECE), BF), pltpu.VMEM((2, FF_CHUNK, HALF_D), BF),
                        dma((2, 2)), dma((nc, 2)), dma((2, n_pieces)), dma((2, n_pieces)), dma((2, N_CHIPS - 1))],
        args=[dx2, x1, f, g_pre, u_pre, pre_g, wgu, wd, post_g], rider=rider, prefetch=kc)


def _ple_fwd(layer, x2, p, ple_g, w_gate, w_proj, post_g, target=None, qkv=None, rider=None):
    s_len = x2.shape[0]
    final = target is not None
    assert not (final and qkv)

    def body(*refs):
        if final:
            x_ref, p_ref, g_ref, wg_ref, wp_ref, qg_ref, t_ref, z_ref, pe_ref, dx_ref, lv_ref = refs
        elif qkv:
            (x_ref, p_ref, g_ref, wg_ref, wp_ref, qg_ref, ng_ref, kg_ref, wq_ref, wkv_ref,
             z_ref, pe_ref, x3_ref, q_ref, kv_ref) = refs
        else:
            x_ref, p_ref, g_ref, wg_ref, wp_ref, qg_ref, z_ref, pe_ref, x3_ref = refs
        x = x_ref[...]
        r = _rms_fwd(x, g_ref[layer:layer + 1, :]).astype(BF)
        z = _dot(r, wg_ref[...])
        pe = _dot(p_ref[...].astype(BF), wp_ref[...])
        z_ref[...] = z
        pe_ref[...] = pe
        x3 = x + _rms_fwd(pe * _sigmoid(z), qg_ref[layer:layer + 1, :])
        if final:
            @pl.when(pl.program_id(0) == 0)
            def _():
                lv_ref[...] = jnp.zeros_like(lv_ref)
            err = x3 - t_ref[...]
            dx_ref[...] = err * (1.0 / D)
            lv_ref[...] += _rowsum(err * err)
        else:
            x3_ref[...] = x3
        if qkv:
            q_ref[...] = _dot(_rms_fwd(x3, ng_ref[layer + 1:layer + 2, :]).astype(BF), wq_ref[...]).astype(BF)
            kv_ref[...] = _dot(_rms_fwd(x3, kg_ref[...]).astype(BF), wkv_ref[...]).astype(BF)

    p_spec = pl.BlockSpec((None, TM, PLE), lambda i: (layer, i, 0))
    in_specs = [_row_spec(TM), p_spec, VSPEC, VSPEC, VSPEC, VSPEC]
    args = [x2, p, ple_g, w_gate, w_proj, post_g]
    out_specs = [_row_spec(TM), _row_spec(TM), _row_spec(TM)]
    out_shape = [_sds((s_len, D))] * 3
    if qkv:
        in_specs += [VSPEC] * 4
        args += list(qkv)
        out_specs += [_row_spec(TM), _row_spec(TM, 2 * KVD)]
        out_shape += [_sds((s_len, D), BF), _sds((s_len, 2 * KVD), BF)]
    if final:
        in_specs.append(_row_spec(TM))
        args.append(target)
        out_specs.append(_const_spec((1, D)))
        out_shape.append(_sds((1, D)))
    return _call(body, name=f"ple_fwd{layer}", grid=(s_len // TM,), in_specs=in_specs, out_specs=out_specs,
                 out_shape=out_shape, args=args, rider=rider)


def _ple_bwd(layer, dx3, x2, z, pe, p, ple_g, w_gate, post_g, rider=None):
    s_len = x2.shape[0]
    n = s_len // TM

    def body(dx_ref, x_ref, z_ref, pe_ref, p_ref, g_ref, wg_ref, qg_ref,
             dx2_ref, dwg_ref, dwp_ref, dg_ref, dqg_ref, gacc, pacc):
        i = pl.program_id(0)

        @pl.when(i == 0)
        def _():
            gacc[...] = jnp.zeros_like(gacc)
            pacc[...] = jnp.zeros_like(pacc)
            dg_ref[...] = jnp.zeros_like(dg_ref)
            dqg_ref[...] = jnp.zeros_like(dqg_ref)

        dx = dx_ref[...]
        x = x_ref[...]
        pe_v = pe_ref[...]
        gate = _sigmoid(z_ref[...])
        de, prod = _rms_bwd(pe_v * gate, qg_ref[layer:layer + 1, :], dx)
        dqg_ref[...] += _rowsum(prod)
        dpe = (de * gate).astype(BF)
        dz = (de * pe_v * gate * (1.0 - gate)).astype(BF)
        pacc[...] += _dot_tn(p_ref[...].astype(BF), dpe)
        g = g_ref[layer:layer + 1, :]
        r = _rms_fwd(x, g).astype(BF)
        gacc[...] += _dot_tn(r, dz)
        dr = _dot_nt(dz, wg_ref[...])
        dxp, prod2 = _rms_bwd(x, g, dr)
        dg_ref[...] += _rowsum(prod2)
        dx2_ref[...] = dx + dxp

        @pl.when(i == n - 1)
        def _():
            dwg_ref[...] = gacc[...].astype(BF)
            dwp_ref[...] = pacc[...].astype(BF)

    p_spec = pl.BlockSpec((None, TM, PLE), lambda i: (layer, i, 0))
    return _call(
        body, name=f"ple_bwd{layer}", grid=(n,),
        in_specs=[_row_spec(TM), _row_spec(TM), _row_spec(TM), _row_spec(TM), p_spec, VSPEC, VSPEC, VSPEC],
        out_specs=[_row_spec(TM), _const_spec((D, D)), _const_spec((PLE, D)), _const_spec((1, D)), _const_spec((1, D))],
        out_shape=[_sds((s_len, D)), _sds((D, D), BF), _sds((PLE, D), BF), _sds((1, D)), _sds((1, D))],
        scratch_shapes=[pltpu.VMEM((D, D), F32), pltpu.VMEM((PLE, D), F32)],
        args=[dx3, x2, z, pe, p, ple_g, w_gate, post_g], rider=rider)


def _qkv_bwd(dq, dkv, x3, dx4, q_g, kv_g, w_q, w_kv):
    s_len = x3.shape[0]
    n = s_len // TM

    def body(dq_ref, dkv_ref, x_ref, dx_ref, qg_ref, kg_ref, wq_ref, wkv_ref,
             dx3_ref, dwq_ref, dwkv_ref, dqg_ref, dkg_ref, qacc, kacc):
        i = pl.program_id(0)

        @pl.when(i == 0)
        def _():
            qacc[...] = jnp.zeros_like(qacc)
            kacc[...] = jnp.zeros_like(kacc)
            dqg_ref[...] = jnp.zeros_like(dqg_ref)
            dkg_ref[...] = jnp.zeros_like(dkg_ref)

        x = x_ref[...]
        qg = qg_ref[1:2, :]
        kg = kg_ref[...]
        dq_v = dq_ref[...]
        dkv_v = dkv_ref[...].astype(BF)
        qacc[...] += _dot_tn(_rms_fwd(x, qg).astype(BF), dq_v)
        kacc[...] += _dot_tn(_rms_fwd(x, kg).astype(BF), dkv_v)
        dxq, prod_q = _rms_bwd(x, qg, _dot_nt(dq_v, wq_ref[...]))
        dxk, prod_k = _rms_bwd(x, kg, _dot_nt(dkv_v, wkv_ref[...]))
        dqg_ref[...] += _rowsum(prod_q)
        dkg_ref[...] += _rowsum(prod_k)
        dx3_ref[...] = dx_ref[...] + dxq + dxk

        @pl.when(i == n - 1)
        def _():
            dwq_ref[...] = qacc[...].astype(BF)
            dwkv_ref[...] = kacc[...].astype(BF)

    outs, _ = _call(
        body, name="qkv_bwd", grid=(n,),
        in_specs=[_row_spec(TM), _row_spec(TM, 2 * KVD), _row_spec(TM), _row_spec(TM), VSPEC, VSPEC, VSPEC, VSPEC],
        out_specs=[_row_spec(TM), _const_spec((D, D)), _const_spec((D, 2 * KVD)),
                   _const_spec((1, D)), _const_spec((1, D))],
        out_shape=[_sds((s_len, D)), _sds((D, D), BF), _sds((D, 2 * KVD), BF), _sds((1, D)), _sds((1, D))],
        scratch_shapes=[pltpu.VMEM((D, D), F32), pltpu.VMEM((D, 2 * KVD), F32)],
        args=[dq, dkv, x3, dx4, q_g, kv_g, w_q, w_kv])
    return outs


def _attn_group(i, q, kvw, sink_ref, g):
    rows = GQA * BLK
    heads = [GQA * g + j for j in range(GQA)]
    off = jnp.where(i > 0, BLK, 0)
    row = lax.broadcasted_iota(jnp.int32, (rows, 2 * BLK), 0)
    rel = (row % BLK) - lax.broadcasted_iota(jnp.int32, (rows, 2 * BLK), 1) + off
    valid = (rel >= 0) & (rel < BLK)
    head_of_row = lax.broadcasted_iota(jnp.int32, (rows, 1), 0) // BLK
    slope = jnp.zeros((rows, 1), F32)
    sink = jnp.zeros((rows, 1), F32)
    for j, h in enumerate(heads):
        slope = jnp.where(head_of_row == j, SLOPES[h], slope)
        sink = jnp.where(head_of_row == j, sink_ref[0, h], sink)
    qs = jnp.concatenate([q[:, h * HEAD_DIM:(h + 1) * HEAD_DIM] for h in heads], axis=0)
    k = kvw[:, g * HEAD_DIM:(g + 1) * HEAD_DIM]
    v = kvw[:, KVD + g * HEAD_DIM:KVD + (g + 1) * HEAD_DIM]
    s = _dot_nt(qs, k) * ATT_SCALE - slope * rel.astype(F32)
    s = jnp.where(valid, s, NEG_INF)
    m = jnp.maximum(jnp.max(s, axis=-1, keepdims=True), sink)
    e = jnp.exp(s - m)
    es = jnp.exp(sink - m)
    inv = 1.0 / (jnp.sum(e, axis=-1, keepdims=True) + es)
    return e * inv, es * inv, qs, k, v


def _unstack_heads(stacked):
    return [stacked[j * BLK:(j + 1) * BLK, :] for j in range(GQA)]


def _kv_window(kv_ref, i):
    ks = pl.multiple_of(jnp.maximum(i * BLK - BLK, 0), BLK)
    return ks, kv_ref[pl.ds(ks, 2 * BLK), :]


def _attn_fwd(q, kv, sinks, x3, w_o, post_g, rider=None):
    s_len = q.shape[0]

    def body(q_ref, kv_ref, sk_ref, x_ref, wo_ref, g_ref, a_ref, y_ref, x4_ref):
        i = pl.program_id(0)
        _, kvw = _kv_window(kv_ref, i)
        q = q_ref[...]
        outs = []
        for g in range(N_KV_HEADS):
            p, _, _, _, v = _attn_group(i, q, kvw, sk_ref, g)
            outs += _unstack_heads(_dot(p.astype(BF), v))
        attn = jnp.concatenate(outs, axis=1)
        a_ref[...] = attn
        y = _dot(attn.astype(BF), wo_ref[...])
        y_ref[...] = y
        x4_ref[...] = x_ref[...] + _rms_fwd(y, g_ref[1:2, :])

    return _call(body, name="attn_fwd", grid=(s_len // BLK,),
                 in_specs=[_row_spec(BLK), VSPEC, SSPEC, _row_spec(BLK), VSPEC, VSPEC],
                 out_specs=[_row_spec(BLK)] * 3, out_shape=[_sds((s_len, D))] * 3,
                 args=[q, kv, sinks, x3, w_o, post_g], rider=rider)


def _attn_bwd(dx4, y, attn, q, kv, sinks, w_o, post_g, rider=None):
    s_len = q.shape[0]
    n = s_len // BLK

    def body(dx_ref, y_ref, a_ref, q_ref, kv_ref, sk_ref, wo_ref, g_ref,
             dq_ref, dkv_ref, dwo_ref, dg_ref, dsk_ref, wacc):
        i = pl.program_id(0)

        @pl.when(i == 0)
        def _():
            dkv_ref[...] = jnp.zeros_like(dkv_ref)
            wacc[...] = jnp.zeros_like(wacc)
            dg_ref[...] = jnp.zeros_like(dg_ref)
            dsk_ref[...] = jnp.zeros_like(dsk_ref)

        dy, prod = _rms_bwd(y_ref[...], g_ref[1:2, :], dx_ref[...])
        dg_ref[...] += _rowsum(prod)
        dyb = dy.astype(BF)
        attn = a_ref[...]
        wacc[...] += _dot_tn(attn.astype(BF), dyb)
        d_o = _dot_nt(dyb, wo_ref[...])
        dod = d_o * attn
        ks, kvw = _kv_window(kv_ref, i)
        q = q_ref[...]
        lane = lax.broadcasted_iota(jnp.int32, (1, D), 1)
        dqs, dks, dvs = [], [], []
        dsk = jnp.zeros((1, D), F32)
        for g in range(N_KV_HEADS):
            p, ps, qs, k, v = _attn_group(i, q, kvw, sk_ref, g)
            cols = [slice((GQA * g + j) * HEAD_DIM, (GQA * g + j + 1) * HEAD_DIM) for j in range(GQA)]
            do_s = jnp.concatenate([d_o[:, c] for c in cols], axis=0).astype(BF)
            dsum = jnp.concatenate([jnp.sum(dod[:, c], axis=-1, keepdims=True) for c in cols], axis=0)
            dp = _dot_nt(do_s, v)
            dsb = (p * (dp - dsum) * ATT_SCALE).astype(BF)
            sink_part = ps * dsum
            for j in range(GQA):
                dsk = dsk + jnp.where(lane == GQA * g + j, -_rowsum(sink_part[j * BLK:(j + 1) * BLK, :]), 0.0)
            dqs += _unstack_heads(_dot(dsb, k))
            dks.append(_dot_tn(dsb, qs))
            dvs.append(_dot_tn(p.astype(BF), do_s))
        dsk_ref[...] += dsk
        dq_ref[...] = jnp.concatenate(dqs, axis=1).astype(BF)
        dkv_ref[pl.ds(ks, 2 * BLK), :] += jnp.concatenate(dks + dvs, axis=1)

        @pl.when(i == n - 1)
        def _():
            dwo_ref[...] = wacc[...].astype(BF)

    return _call(
        body, name="attn_bwd", grid=(n,),
        in_specs=[_row_spec(BLK), _row_spec(BLK), _row_spec(BLK), _row_spec(BLK), VSPEC, SSPEC, VSPEC, VSPEC],
        out_specs=[_row_spec(BLK), _const_spec((s_len, 2 * KVD)), _const_spec((D, D)),
                   _const_spec((1, D)), _const_spec((1, D))],
        out_shape=[_sds((s_len, D), BF), _sds((s_len, 2 * KVD)), _sds((D, D), BF), _sds((1, D)), _sds((1, D))],
        scratch_shapes=[pltpu.VMEM((D, D), F32)],
        args=[dx4, y, attn, q, kv, sinks, w_o, post_g], rider=rider)


Big = collections.namedtuple("Big", "name src layer L A R C rb")


def _bigs():
    out = {"pool_w": Big("pool_w", "pool_w", None, 4, 4, POOL_G // N_CHIPS, POOL_G, 32)}
    for l in range(2):
        out[f"w_gu{l}"] = Big(f"w_gu{l}", "w_gu", l, 1, 2, D, FF_HALF, 256)
        out[f"w_down{l}"] = Big(f"w_down{l}", "w_down", l, 1, 4, FF // N_CHIPS, D, 352)
        out[f"w_ple_gate{l}"] = Big(f"w_ple_gate{l}", "w_ple_gate", l, 1, 4, D // N_CHIPS, D, 128)
        out[f"w_ple_proj{l}"] = Big(f"w_ple_proj{l}", "w_ple_proj", l, 1, 1, PLE, D // N_CHIPS, 128)
    out["w_q"] = Big("w_q", "w_q", None, 1, 4, D // N_CHIPS, D, 128)
    out["w_o"] = Big("w_o", "w_o", None, 1, 4, D // N_CHIPS, D, 128)
    out["w_kv"] = Big("w_kv", "w_kv", None, 1, 4, D // N_CHIPS, 2 * KVD, 128)
    return out


BIGS = _bigs()
POOL_SCALE = Big("pool_scale", "pool_scale", None, 1, 1, 1, D // N_CHIPS, 1)
BIG_SOURCES = ("w_gu", "w_down", "w_ple_gate", "w_ple_proj", "w_q", "w_o", "w_kv", "pool_w")


def _ncb(t):
    return N_CHIPS // t.A


def _full_shape(t, rows=None):
    return (t.L, t.A, t.R if rows is None else rows, _ncb(t) * t.C)


def _slot_index(t, k):
    return k // _ncb(t), k % _ncb(t)


def _slot(ref, t, k, row0, rows):
    a, cb = _slot_index(t, k)
    return ref.at[:, a, pl.ds(row0, rows), pl.ds(pl.multiple_of(cb * t.C, 128), t.C)]


def _place_job(t, w, out_dtype=BF):
    nb = next((nb for nb in (8, 4, 2, 1) if t.R % (16 * nb) == 0), 1) if t.L == 1 else 1
    rb = t.R // nb

    def fn(j, kc_ref, ins, outs):
        outs[0][...] = ins[0][...].astype(out_dtype)

    def in_map(j, kc_ref):
        return (j // nb if t.layer is None else t.layer, j % nb, 0)

    def out_map(j, kc_ref):
        a, cb = _slot_index(t, kc_ref[0])
        return (j // nb, a, j % nb, cb)

    return Job(t.L * nb, [(w, (None, rb, t.C), in_map)],
               [(_sds(_full_shape(t), out_dtype), (None, None, rb, t.C), out_map)], fn)


def _mesh_position():
    x, y, c = lax.axis_index("x"), lax.axis_index("y"), lax.axis_index("c")
    chips = [(1 - x, y), (x, 1 - y), (1 - x, 1 - y)]
    return x, y, c, chips


def _gather_rider(parts, fulls):
    nt = len(parts)
    TO_X, TO_Y, FWD_X, FWD_Y, SIB_X, SIB_Y, SIB_D = range(7)

    def rows_of(ti, core):
        t, r0, r1 = parts[ti]
        h = (r1 - r0) // 2
        return r0 + core * h, h

    def copy(outs, sems, kind, ti, k_src, row0, rows, dev):
        region = _slot(outs[ti], parts[ti][0], k_src, row0, rows)
        return pltpu.make_async_remote_copy(region, region, sems[0].at[ti, kind], sems[1].at[ti, kind],
                                            device_id=dev, device_id_type=MESH)

    def plan(outs, sems):
        x, y, c, _ = _mesh_position()
        me, kx, ky, kd = 2 * x + y, 2 * (1 - x) + y, 2 * x + (1 - y), 2 * (1 - x) + (1 - y)
        dev_x, dev_y, dev_d, sib = (1 - x, y, c), (x, 1 - y, c), (1 - x, 1 - y, c), (x, y, 1 - c)

        def whole(ti):
            return 0, parts[ti][0].R

        def mk(kind, k_send, k_recv, dev, send_rows, recv_rows):
            def build(ti, side):
                k_src = k_send if side == "s" else k_recv
                row0, rows = (send_rows if side == "s" else recv_rows)(ti)
                return copy(outs, sems, kind, ti, k_src, row0, rows, dev)
            return build

        def first_half(core):
            return lambda ti: (rows_of(ti, core)[0], rows_of(ti, core)[1] // 2)

        def second_half(core):
            return lambda ti: (rows_of(ti, core)[0] + rows_of(ti, core)[1] // 2, rows_of(ti, core)[1] // 2)

        mine = lambda ti: rows_of(ti, c)
        theirs = lambda ti: rows_of(ti, 1 - c)
        split = {
            TO_X: mk(TO_X, me, kx, dev_x, mine, mine),
            TO_Y: mk(TO_Y, me, ky, dev_y, mine, mine),
            FWD_X: mk(FWD_X, ky, kd, dev_x, first_half(c), first_half(c)),
            FWD_Y: mk(FWD_Y, kx, kd, dev_y, second_half(c), second_half(c)),
            SIB_X: mk(SIB_X, kx, kx, sib, mine, theirs),
            SIB_Y: mk(SIB_Y, ky, ky, sib, mine, theirs),
            SIB_D: mk(SIB_D, kd, kd, sib, mine, theirs),
        }
        direct = {
            TO_X: mk(TO_X, me, kx, dev_x, whole, whole),
            TO_Y: mk(TO_Y, me, ky, dev_y, whole, whole),
            FWD_X: mk(FWD_X, me, kd, dev_d, whole, whole),
        }
        return split, direct

    is_split = [t.R > 1 for t, _, _ in parts]

    def start(ins, outs, sems):
        split, direct = plan(outs, sems)
        for ti in range(nt):
            kinds = split if is_split[ti] else direct
            kinds[TO_X](ti, "s").start()
            kinds[TO_Y](ti, "s").start()
            if not is_split[ti]:
                kinds[FWD_X](ti, "s").start()

    def mid(ins, outs, sems):
        split, _ = plan(outs, sems)
        for ti in range(nt):
            if is_split[ti]:
                split[TO_Y](ti, "r").wait_recv()
                split[FWD_X](ti, "s").start()
                split[SIB_Y](ti, "s").start()
        for ti in range(nt):
            if is_split[ti]:
                split[TO_X](ti, "r").wait_recv()
                split[FWD_Y](ti, "s").start()
                split[SIB_X](ti, "s").start()

    def finish(ins, outs, sems):
        split, direct = plan(outs, sems)
        for ti in range(nt):
            if is_split[ti]:
                split[FWD_X](ti, "r").wait_recv()
                split[FWD_Y](ti, "r").wait_recv()
                split[SIB_D](ti, "s").start()
            else:
                for kind in (TO_X, TO_Y, FWD_X):
                    direct[kind](ti, "r").wait_recv()
        for ti in range(nt):
            if is_split[ti]:
                for kind in (SIB_X, SIB_Y, SIB_D):
                    split[kind](ti, "r").wait_recv()
        for ti in range(nt):
            kinds = split if is_split[ti] else direct
            for kind in kinds:
                kinds[kind](ti, "s").wait_send()

    sems = pltpu.SemaphoreType.DMA((nt, 7))
    return Rider(list(fulls), [_sds(a.shape, a.dtype) for a in fulls], {i: i for i in range(nt)},
                 [sems, sems], start, mid, finish)


def _pair_exchange(name, specs, grads):
    nt = len(specs)

    def body(*refs):
        gs = refs[:nt]
        lands = refs[nt:2 * nt]
        send, recv = refs[2 * nt:]
        x, y, c, _ = _mesh_position()
        cps = []
        for ti, t in enumerate(specs):
            half = t.R // 2
            cp = pltpu.make_async_remote_copy(gs[ti].at[:, :, pl.ds((1 - c) * half, half), :], lands[ti],
                                              send.at[ti], recv.at[ti],
                                              device_id=(x, y, 1 - c), device_id_type=MESH)
            cp.start()
            cps.append(cp)
        for cp in cps:
            cp.wait()

    return pl.pallas_call(
        body, name=name,
        in_specs=[ANYSPEC] * nt, out_specs=[ANYSPEC] * nt,
        out_shape=[_sds(_full_shape(t, t.R // 2), BF) for t in specs],
        scratch_shapes=[pltpu.SemaphoreType.DMA((nt,)), pltpu.SemaphoreType.DMA((nt,))],
        compiler_params=_params(),
    )(*grads)


def _pair_sum_job(t, g, land):
    assert t.L == 1
    half = t.R // 2
    nj = half // t.rb
    block = (None, t.A, t.rb, _ncb(t) * t.C)

    def fn(j, kc_ref, ins, outs):
        outs[0][...] = (ins[0][...].astype(F32) + ins[1][...].astype(F32)).astype(BF)

    return Job(nj,
               [(g, block, lambda j, kc_ref: (0, 0, kc_ref[1] * nj + j, 0)),
                (land, block, lambda j, kc_ref: (0, 0, j, 0))],
               [(_sds(_full_shape(t, half), BF), block, lambda j, kc_ref: (0, 0, j, 0))], fn)


def _scatter_rider(specs, sums):
    nt = len(specs)

    def copy(ins, outs, sems, ti, j, chip, c):
        t = specs[ti]
        cx, cy = chip
        return pltpu.make_async_remote_copy(_slot(ins[ti], t, 2 * cx + cy, 0, t.R // 2), outs[ti].at[j],
                                            sems[0].at[ti, j], sems[1].at[ti, j],
                                            device_id=(cx, cy, c), device_id_type=MESH)

    def start(ins, outs, sems):
        _, _, c, chips = _mesh_position()
        for j, chip in enumerate(chips):
            for ti in range(nt):
                copy(ins, outs, sems, ti, j, chip, c).start()

    def finish(ins, outs, sems):
        _, _, c, chips = _mesh_position()
        for j, chip in enumerate(chips):
            for ti in range(nt):
                copy(ins, outs, sems, ti, j, chip, c).wait()

    sems = pltpu.SemaphoreType.DMA((nt, N_CHIPS - 1))
    return Rider(list(sums), [_sds((N_CHIPS - 1, t.L, t.R // 2, t.C), BF) for t in specs], {}, [sems, sems],
                 start, None, finish)


def _chip_sum_job(ts, landed):
    t0 = ts[0]
    assert t0.L == 1
    half = t0.R // 2
    nj = half // t0.rb

    def local(j, li):
        return jnp.clip(j - li * nj, 0, nj - 1)

    ins = []
    for li, t in enumerate(ts):
        s, land = landed[t.name]

        def own_map(j, kc_ref, li=li, t=t):
            a, cb = _slot_index(t, kc_ref[0])
            return (0, a, local(j, li), cb)

        ins.append((s, (None, None, t.rb, t.C), own_map))
        ins.append((land, (N_CHIPS - 1, None, t.rb, t.C), lambda j, kc_ref, li=li: (0, 0, local(j, li), 0)))

    def fn(j, kc_ref, in_refs, outs):
        for li in range(len(ts)):
            @pl.when(j // nj == li)
            def _():
                acc = in_refs[2 * li][...].astype(F32)
                for k in range(N_CHIPS - 1):
                    acc = acc + in_refs[2 * li + 1][k].astype(F32)
                outs[0][...] = acc

    return Job(len(ts) * nj, ins,
               [(_sds((len(ts), t0.R, t0.C)), (None, t0.rb, t0.C),
                 lambda j, kc_ref: (j // nj, kc_ref[1] * nj + j % nj, 0))], fn)


def _adamw_job(rb, w, g, m, v):
    n_layers, r, c = w.shape
    nb = r // rb
    block = (None, rb, c)
    index = lambda j, kc_ref: (j // nb, j % nb, 0)

    def fn(j, kc_ref, ins, outs):
        g_v = ins[1][...]
        outs[0][...] = g_v
        outs[1][...], outs[2][...], outs[3][...] = _adamw_math(ins[0][...], g_v, ins[2][...], ins[3][...])

    return Job(n_layers * nb, [(a, block, index) for a in (w, g, m, v)],
               [(_sds(w.shape), block, index)] * 4, fn)


def _chip_sum_fused_job(ts, fused, by_cols):
    t0 = ts[0]
    own0 = fused[t0.name][0]
    if by_cols:
        rows, cols = own0.shape
    else:
        nb, rows, bw = own0.shape
        cols = nb * bw
    nj = rows // t0.rb

    def local(j, li):
        return jnp.clip(j - li * nj, 0, nj - 1)

    ins = []
    for li, t in enumerate(ts):
        own, land = fused[t.name]
        if by_cols:
            ins.append((own, (t.rb, cols), lambda j, kc_ref, li=li: (local(j, li), 0)))
            ins.append((land, (N_CHIPS - 1, t.rb, cols), lambda j, kc_ref, li=li: (0, local(j, li), 0)))
        else:
            ins.append((own, (nb, t.rb, bw), lambda j, kc_ref, li=li: (0, local(j, li), 0)))
            ins.append((land, (N_CHIPS - 1, nb, t.rb, bw), lambda j, kc_ref, li=li: (0, 0, local(j, li), 0)))

    def fn(j, kc_ref, in_refs, outs):
        for li in range(len(ts)):
            @pl.when(j // nj == li)
            def _():
                acc = in_refs[2 * li][...].astype(F32)
                for k in range(N_CHIPS - 1):
                    acc = acc + in_refs[2 * li + 1][k].astype(F32)
                outs[0][...] = acc if by_cols else jnp.concatenate([acc[b] for b in range(nb)], axis=1)

    def out_map(j, kc_ref):
        return (j // nj, j % nj, kc_ref[1]) if by_cols else (j // nj, kc_ref[1] * nj + j % nj, 0)

    return Job(len(ts) * nj, ins, [(_sds((len(ts), t0.R, t0.C)), (None, t0.rb, cols), out_map)], fn)


def _share_rider(halves, by_cols):
    nt = len(halves)

    def copy(outs, sems, ti, core, sibling):
        axis = 2 if by_cols[ti] else 1
        half = halves[ti].shape[axis] // 2
        piece = pl.ds(pl.multiple_of(core * half, 128 if by_cols[ti] else 8), half)
        part = outs[ti].at[:, :, piece] if by_cols[ti] else outs[ti].at[:, piece, :]
        return pltpu.make_async_remote_copy(part, part, sems[0].at[ti], sems[1].at[ti],
                                            device_id=sibling, device_id_type=MESH)

    def start(ins, outs, sems):
        x, y, c, _ = _mesh_position()
        for ti in range(nt):
            copy(outs, sems, ti, c, (x, y, 1 - c)).start()

    def finish(ins, outs, sems):
        x, y, c, _ = _mesh_position()
        for ti in range(nt):
            copy(outs, sems, ti, 1 - c, (x, y, 1 - c)).wait_recv()
        for ti in range(nt):
            copy(outs, sems, ti, c, (x, y, 1 - c)).wait_send()

    sems = pltpu.SemaphoreType.DMA((nt,))
    return Rider(list(halves), [_sds(a.shape, a.dtype) for a in halves], {i: i for i in range(nt)}, [sems, sems],
                 start, None, finish)


def _both(r1, r2):
    assert r1.mid is None and r2.mid is None
    ni, no, ns = len(r1.arrays), len(r1.out_shapes), len(r1.scratch)

    def split(fn1, fn2):
        def run(ins, outs, scr):
            fn1(ins[:ni], outs[:no], scr[:ns])
            fn2(ins[ni:], outs[no:], scr[ns:])
        return run

    aliases = dict(r1.aliases)
    aliases.update({ni + a: no + b for a, b in r2.aliases.items()})
    return Rider(r1.arrays + r2.arrays, r1.out_shapes + r2.out_shapes, aliases, r1.scratch + r2.scratch,
                 split(r1.start, r2.start), None, split(r1.finish, r2.finish))


def _adamw_math(w, g, m, v):
    m = B1 * m + (1.0 - B1) * g
    v = B2 * v + (1.0 - B2) * (g * g)
    delta = -LR * ((m / BC1) / (jnp.sqrt(v / BC2) + AEPS) + WD * w)
    return delta, m, v


def _adamw(name, rb, w, g, m, v):
    n_layers, r, c = w.shape

    def body(w_ref, g_ref, m_ref, v_ref, go_ref, d_ref, nm_ref, nv_ref):
        g_v = g_ref[...]
        go_ref[...] = g_v
        d_ref[...], nm_ref[...], nv_ref[...] = _adamw_math(w_ref[...], g_v, m_ref[...], v_ref[...])

    spec = pl.BlockSpec((None, rb, c), lambda l, j: (l, j, 0))
    return pl.pallas_call(
        body, name=f"adamw_{name}", grid=(n_layers, r // rb),
        in_specs=[spec] * 4, out_specs=[spec] * 4, out_shape=[_sds(w.shape)] * 4,
        compiler_params=_params(2),
    )(w, g, m, v)


GAIN_ROWS = {"pre_mix_g": 0, "post_mix_g": 2, "pre_ffn_g": 4, "post_ffn_g": 6, "ple_g": 8, "ple_post_g": 10}
ROW_KV_G, ROW_POOL_SCALE, ROW_SINKS, ROW_LOSS, PACK_ROWS = 12, 13, 14, 15, 16
SMALL_NAMES = tuple(GAIN_ROWS) + ("kv_g", "pool_scale", "sinks")


def _small_all_reduce(rows, dpool, rider=None):
    ng, pr = len(WINDOWS), POOL_G // N_CHIPS

    def body(*refs):
        row_refs = refs[:PACK_ROWS]
        dpool_ref, tot_ref, gpool_ref, pack, land, pland, send, recv, psend, precv = refs[PACK_ROWS:]
        x, y, c, _ = _mesh_position()
        me = 4 * x + 2 * y + c
        for r in range(PACK_ROWS):
            pack[r:r + 1, :] = row_refs[r][...]

        def shard_of(k):
            return dpool_ref.at[:, pl.ds(pl.multiple_of(k * pr, pr), pr), :]

        cps = []
        for j in range(1, N_DEV):
            px, py, pc = x ^ (j >> 2), y ^ ((j >> 1) & 1), c ^ (j & 1)
            cps.append(pltpu.make_async_remote_copy(pack, land.at[me], send.at[j], recv.at[j],
                                                    device_id=(px, py, pc), device_id_type=MESH))
            cps.append(pltpu.make_async_remote_copy(shard_of(2 * px + py), pland.at[me], psend.at[j], precv.at[j],
                                                    device_id=(px, py, pc), device_id_type=MESH))
        for cp in cps:
            cp.start()
        land[me] = pack[...]
        pland[me] = dpool_ref[:, pl.ds(pl.multiple_of((2 * x + y) * pr, pr), pr), :]
        for j in range(1, N_DEV):
            pltpu.make_async_remote_copy(pack, land.at[me ^ j], send.at[j], recv.at[j],
                                         device_id=(x, y, c), device_id_type=MESH).wait_recv()
            pltpu.make_async_remote_copy(shard_of(0), pland.at[me ^ j], psend.at[j], precv.at[j],
                                         device_id=(x, y, c), device_id_type=MESH).wait_recv()
        for cp in cps:
            cp.wait_send()
        tot = land[0]
        gp = pland[0].astype(F32)
        for d in range(1, N_DEV):
            tot = tot + land[d]
            gp = gp + pland[d].astype(F32)
        tot_ref[...] = tot
        gpool_ref[...] = gp

    sems = pltpu.SemaphoreType.DMA((N_DEV,))
    return _call(
        body, name="small_all_reduce", grid=(1,),
        in_specs=[VSPEC] * (PACK_ROWS + 1), out_specs=[VSPEC, VSPEC],
        out_shape=[_sds((PACK_ROWS, D)), _sds((ng, pr, POOL_G))],
        scratch_shapes=[pltpu.VMEM((PACK_ROWS, D), F32), pltpu.VMEM((N_DEV, PACK_ROWS, D), F32),
                        pltpu.VMEM((N_DEV, ng, pr, POOL_G), BF), sems, sems, sems, sems],
        args=[*rows, dpool], rider=rider)


def _small_adamw(tot, kc, small_w, small_m, small_v):
    names = SMALL_NAMES
    n = len(names)

    def body(*refs):
        tot_ref, kc_ref = refs[0], refs[1]
        w_refs = dict(zip(names, refs[2:2 + n]))
        m_refs = dict(zip(names, refs[2 + n:2 + 2 * n]))
        v_refs = dict(zip(names, refs[2 + 2 * n:2 + 3 * n]))
        loss_ref = refs[2 + 3 * n]
        out_refs = {nm: refs[3 + 3 * n + 4 * k: 7 + 3 * n + 4 * k] for k, nm in enumerate(names)}
        tot = tot_ref[...]
        loss_ref[...] = 0.5 * jnp.sum(tot[ROW_LOSS:ROW_LOSS + 1, :], axis=-1, keepdims=True) * (1.0 / D)

        def update(nm, g):
            g_ref, d_ref, nm_ref, nv_ref = out_refs[nm]
            g_ref[...] = g
            d_ref[...], nm_ref[...], nv_ref[...] = _adamw_math(w_refs[nm][...], g, m_refs[nm][...], v_refs[nm][...])

        for nm, r in GAIN_ROWS.items():
            update(nm, tot[r:r + 2, :])
        update("kv_g", tot[ROW_KV_G:ROW_KV_G + 1, :])
        k = kc_ref[0]
        width = D // N_CHIPS
        g_scale = jnp.zeros((1, width), F32)
        for kk in range(N_CHIPS):
            g_scale = g_scale + jnp.where(k == kk, tot[ROW_POOL_SCALE:ROW_POOL_SCALE + 1, kk * width:(kk + 1) * width], 0.0)
        update("pool_scale", g_scale)
        update("sinks", tot[ROW_SINKS:ROW_SINKS + 1, 0:N_HEADS])

    ins = [tot, kc] + [small_w[nm] for nm in names] + [small_m[nm] for nm in names] + [small_v[nm] for nm in names]
    out_shape = [_sds((1, 1))]
    for nm in names:
        out_shape += [_sds(small_w[nm].shape)] * 4
    outs = pl.pallas_call(
        body, name="small_adamw",
        in_specs=[VSPEC, SSPEC] + [VSPEC] * (3 * n), out_specs=[VSPEC] * len(out_shape), out_shape=out_shape,
        compiler_params=_params(),
    )(*ins)
    return outs[0], {nm: outs[1 + 4 * k: 5 + 4 * k] for k, nm in enumerate(names)}


def _compute_layout(t, full):
    if t.src == "w_gu":
        return full.reshape(2, D, FF)
    if t.src == "pool_w":
        return full.reshape(len(WINDOWS), POOL_G, POOL_G)
    if t.src == "pool_scale":
        return full.reshape(1, D)
    return full.reshape(t.A * t.R, _ncb(t) * t.C)


def kernel(x, p, pre_mix_g, post_mix_g, pre_ffn_g, post_ffn_g, pool_w, pool_scale, kv_g, w_kv, w_q, sinks, w_o, w_gu, w_down, ple_g, w_ple_gate, w_ple_proj, ple_post_g, loss_target, m_pre_mix_g, m_post_mix_g, m_pre_ffn_g, m_post_ffn_g, m_pool_w, m_pool_scale, m_kv_g, m_w_kv, m_w_q, m_sinks, m_w_o, m_w_gu, m_w_down, m_ple_g, m_w_ple_gate, m_w_ple_proj, m_ple_post_g, v_pre_mix_g, v_post_mix_g, v_pre_ffn_g, v_post_ffn_g, v_pool_w, v_pool_scale, v_kv_g, v_w_kv, v_w_q, v_sinks, v_w_o, v_w_gu, v_w_down, v_ple_g, v_w_ple_gate, v_w_ple_proj, v_ple_post_g):
    weights = dict(pre_mix_g=pre_mix_g, post_mix_g=post_mix_g, pre_ffn_g=pre_ffn_g, post_ffn_g=post_ffn_g,
                   pool_w=pool_w, pool_scale=pool_scale, kv_g=kv_g, w_kv=w_kv, w_q=w_q, sinks=sinks, w_o=w_o,
                   w_gu=w_gu, w_down=w_down, ple_g=ple_g, w_ple_gate=w_ple_gate, w_ple_proj=w_ple_proj,
                   ple_post_g=ple_post_g)
    m_in = dict(pre_mix_g=m_pre_mix_g, post_mix_g=m_post_mix_g, pre_ffn_g=m_pre_ffn_g, post_ffn_g=m_post_ffn_g,
                pool_w=m_pool_w, pool_scale=m_pool_scale, kv_g=m_kv_g, w_kv=m_w_kv, w_q=m_w_q, sinks=m_sinks,
                w_o=m_w_o, w_gu=m_w_gu, w_down=m_w_down, ple_g=m_ple_g, w_ple_gate=m_w_ple_gate,
                w_ple_proj=m_w_ple_proj, ple_post_g=m_ple_post_g)
    v_in = dict(pre_mix_g=v_pre_mix_g, post_mix_g=v_post_mix_g, pre_ffn_g=v_pre_ffn_g, post_ffn_g=v_post_ffn_g,
                pool_w=v_pool_w, pool_scale=v_pool_scale, kv_g=v_kv_g, w_kv=v_w_kv, w_q=v_w_q, sinks=v_sinks,
                w_o=v_w_o, w_gu=v_w_gu, w_down=v_w_down, ple_g=v_ple_g, w_ple_gate=v_w_ple_gate,
                w_ple_proj=v_w_ple_proj, ple_post_g=v_ple_post_g)
    order = ["pre_mix_g", "post_mix_g", "pre_ffn_g", "post_ffn_g", "pool_w", "pool_scale", "kv_g", "w_kv", "w_q",
             "sinks", "w_o", "w_gu", "w_down", "ple_g", "w_ple_gate", "w_ple_proj", "ple_post_g"]

    kc = jnp.stack([2 * lax.axis_index("x") + lax.axis_index("y"), lax.axis_index("c")]).astype(jnp.int32)
    s_len = x.shape[1]
    x2d = x.reshape(s_len, D)
    p3d = p.reshape(2, s_len, PLE)
    target = loss_target.reshape(s_len, D)
    kv_g2d = kv_g.reshape(1, D)
    gains = {nm: weights[nm] for nm in GAIN_ROWS}

    def shard_view(src, a):
        t = next(t for t in BIGS.values() if t.src == src)
        return a.reshape(-1, t.R, t.C)

    first, second = ["pool_w", "pool_scale"], ["w_gu0", "w_down0"]
    rest = [nm for nm in BIGS if nm not in first + second]
    specs = dict(BIGS, pool_scale=POOL_SCALE)
    placed = {}

    def place_job(nm):
        if nm == "pool_scale":
            return _place_job(POOL_SCALE, pool_scale.reshape(1, 1, D // N_CHIPS), F32)
        return _place_job(BIGS[nm], shard_view(BIGS[nm].src, weights[BIGS[nm].src]))

    def gather(names, rows=None):
        rows = rows or {}
        parts = [(specs[nm],) + tuple(rows.get(nm, (0, specs[nm].R))) for nm in names]
        return _gather_rider(parts, [placed[nm] for nm in names])

    def take(names, results):
        for nm, a in zip(names, results):
            placed[nm] = a

    def weight(nm):
        return _compute_layout(specs[nm], placed[nm])

    take(first, [r[0] for r in _multi_call("place_pool", [place_job(nm) for nm in first], kc)])
    cast, got = _multi_call("place_ffn0", [place_job(nm) for nm in second], kc, rider=gather(first))
    take(second, [r[0] for r in cast])
    take(first, got)
    jobs = [place_job(nm) for nm in rest]
    jobs.append(_mixa_fwd_job(x2d, gains["pre_mix_g"], weight("pool_w"), weight("pool_scale"), gains["post_mix_g"]))
    results, got = _multi_call("cast_and_mixa_fwd", jobs, kc, rider=gather(second))
    take(rest, [r[0] for r in results[:-1]])
    take(second, got)
    y0, x1 = results[-1]

    ride = ["w_ple_gate0", "w_ple_proj0", "w_q", "w_kv", "w_o", "w_gu1"]
    (f0, x2, g0, u0), got = _ffn_fwd(0, x1, gains["pre_ffn_g"], weight("w_gu0"), weight("w_down0"), gains["post_ffn_g"],
                             rider=gather(ride, {"w_gu1": (0, 320)}))
    take(ride, got)

    ride = ["w_ple_gate1", "w_ple_proj1", "w_gu1"]
    (z0, pe0, x3, q, kv), got = _ple_fwd(
        0, x2, p3d, gains["ple_g"], weight("w_ple_gate0"), weight("w_ple_proj0"), gains["ple_post_g"],
        qkv=(gains["pre_mix_g"], kv_g2d, weight("w_q"), weight("w_kv")),
        rider=gather(ride, {"w_gu1": (320, 704)}))
    take(ride, got)

    ride = ["w_down1", "w_gu1"]
    (attn, y1, x4), got = _attn_fwd(q, kv, sinks, x3, weight("w_o"), gains["post_mix_g"],
                                    rider=gather(ride, {"w_gu1": (704, D)}))
    take(ride, got)

    (f1, x5, g1, u1), _ = _ffn_fwd(1, x4, gains["pre_ffn_g"], weight("w_gu1"), weight("w_down1"), gains["post_ffn_g"])
    (z1, pe1, dx6, loss_row), _ = _ple_fwd(1, x5, p3d, gains["ple_g"], weight("w_ple_gate1"), weight("w_ple_proj1"),
                                           gains["ple_post_g"], target=target)

    local = {}
    landed = {}
    fused = {}

    def pair_stage(tag, names):
        ts = [BIGS[nm] for nm in names]
        gs = [local[nm].reshape(_full_shape(t)) for nm, t in zip(names, ts)]
        lands = _pair_exchange(f"grads_pair_exchange_{tag}", ts, gs)
        jobs = [_pair_sum_job(t, g, l) for t, g, l in zip(ts, gs, lands)]
        return [r[0] for r in _multi_call(f"pair_sum_{tag}", jobs, kc)]

    def scatter(names, sums):
        return _scatter_rider([BIGS[nm] for nm in names], sums)

    def keep(names, sums, got):
        for nm, s, l in zip(names, sums, got):
            landed[nm] = (s, l)

    (dx5, local["w_ple_gate1"], local["w_ple_proj1"], d_ple1, d_plepost1), _ = _ple_bwd(
        1, dx6, x5, z1, pe1, p3d, gains["ple_g"], weight("w_ple_gate1"), gains["ple_post_g"])

    group_a = ["w_ple_gate1", "w_ple_proj1"]
    sums_a = pair_stage("a", group_a)
    (dx4, d_preffn1, d_postffn1, *scattered), _ = _ffn_bwd(
        1, dx5, x4, f1, g1, u1, gains["pre_ffn_g"], weight("w_gu1"), weight("w_down1"), gains["post_ffn_g"], kc)
    fused["w_gu1"], fused["w_down1"] = scattered[0:2], scattered[2:4]

    (dq, dkv, local["w_o"], d_postmix1, d_sinks), got = _attn_bwd(
        dx4, y1, attn, q, kv, sinks, weight("w_o"), gains["post_mix_g"], rider=scatter(group_a, sums_a))
    keep(group_a, sums_a, got)
    dx3, local["w_q"], local["w_kv"], d_premix1, d_kvg = _qkv_bwd(
        dq, dkv, x3, dx4, gains["pre_mix_g"], kv_g2d, weight("w_q"), weight("w_kv"))

    group_b = ["w_o", "w_q", "w_kv"]
    sums_b = pair_stage("b", group_b)
    (dx2, local["w_ple_gate0"], local["w_ple_proj0"], d_ple0, d_plepost0), got = _ple_bwd(
        0, dx3, x2, z0, pe0, p3d, gains["ple_g"], weight("w_ple_gate0"), gains["ple_post_g"],
        rider=scatter(group_b, sums_b))
    keep(group_b, sums_b, got)

    group_c = ["w_ple_gate0", "w_ple_proj0"]
    sums_c = pair_stage("c", group_c)
    (dx1, d_preffn0, d_postffn0, *scattered), _ = _ffn_bwd(
        0, dx2, x1, f0, g0, u0, gains["pre_ffn_g"], weight("w_gu0"), weight("w_down0"), gains["post_ffn_g"], kc)
    fused["w_gu0"], fused["w_down0"] = scattered[0:2], scattered[2:4]

    (dx0, d_pool, d_scale, d_postmix0, d_premix0), _ = _mixa_bwd(
        dx1, x2d, y0, gains["pre_mix_g"], weight("pool_w"), weight("pool_scale"), gains["post_mix_g"])

    rows = [d_premix0, d_premix1, d_postmix0, d_postmix1, d_preffn0, d_preffn1, d_postffn0, d_postffn1,
            d_ple0, d_ple1, d_plepost0, d_plepost1, d_kvg, d_scale, d_sinks, loss_row]
    as2d = lambda a: a.reshape(1, D) if a.ndim == 1 else a
    layers_of = lambda src: [t for t in BIGS.values() if t.src == src]
    own_scatter = ["w_gu", "w_down"]
    early = own_scatter + ["w_q", "w_o", "w_kv"]
    late = ["w_ple_gate", "w_ple_proj"]
    by_cols = lambda srcs: [src == "w_down" for src in srcs]
    jobs = [_chip_sum_fused_job(layers_of(src), fused, by_cols=src == "w_down") for src in own_scatter]
    jobs += [_chip_sum_job(layers_of(src), landed) for src in early if src not in own_scatter]
    halves = [r[0] for r in _multi_call("chip_sum_early", jobs, kc)]
    (tot, g_pool), got = _small_all_reduce(
        rows, d_pool, rider=_both(scatter(group_c, sums_c), _share_rider(halves, by_cols(early))))
    keep(group_c, sums_c, got[:len(group_c)])
    full_grads = dict(zip(early, got[len(group_c):]))
    loss, small = _small_adamw(tot, kc, {nm: as2d(weights[nm]) for nm in SMALL_NAMES},
                               {nm: as2d(m_in[nm]) for nm in SMALL_NAMES},
                               {nm: as2d(v_in[nm]) for nm in SMALL_NAMES})

    halves = [r[0] for r in _multi_call("chip_sum_late", [_chip_sum_job(layers_of(src), landed) for src in late], kc)]
    full_grads.update(zip(late, _run("grads_pair_share", _share_rider(halves, by_cols(late)))))
    full_grads["pool_w"] = g_pool
    others = [src for src in BIG_SOURCES if src not in own_scatter and src != "pool_w"]

    def adam_args(src):
        return (layers_of(src)[0].rb, shard_view(src, weights[src]), full_grads[src],
                shard_view(src, m_in[src]), shard_view(src, v_in[src]))

    out = {"grad": {}, "delta": {}, "new_m": {}, "new_v": {}}
    results = {src: _adamw(src, *adam_args(src)) for src in own_scatter}
    rest_srcs = others + ["pool_w"]
    results.update(zip(rest_srcs, _multi_call("adamw_rest", [_adamw_job(*adam_args(src)) for src in rest_srcs], kc)))
    for src in BIG_SOURCES:
        shape = weights[src].shape
        for kind, a in zip(("grad", "delta", "new_m", "new_v"), results[src]):
            out[kind][src] = a.reshape(shape)
    for nm in SMALL_NAMES:
        shape = weights[nm].shape
        for kind, a in zip(("grad", "delta", "new_m", "new_v"), small[nm]):
            out[kind][nm] = a.reshape(shape)

    return (loss.reshape(()), dx0.reshape(x.shape),
            *[out["grad"][nm] for nm in order], *[out["delta"][nm] for nm in order],
            *[out["new_m"][nm] for nm in order], *[out["new_v"][nm] for nm in order])
```

```python
import collections

import jax
import jax.numpy as jnp
from jax import lax
from jax.experimental import pallas as pl
from jax.experimental.pallas import tpu as pltpu

D = 1024
FF = 2816
N_HEADS = 16
HEAD_DIM = 64
N_KV_HEADS = 4
GQA = N_HEADS // N_KV_HEADS
KVD = N_KV_HEADS * HEAD_DIM
PLE = 256
BLK = 128
WINDOWS = (2, 4, 8, 16)
POOL_G = 256
HALO = 16
EPS = 1e-6
NEG_INF = -1e30
ATT_SCALE = HEAD_DIM ** -0.5
SLOPES = tuple(2.0 ** (-8.0 * (h + 1) / N_HEADS) for h in range(N_HEADS))
N_CHIPS = 4
N_DEV = 8

LR, B1, B2, AEPS, WD, STEP = 0.001, 0.9, 0.999, 1e-08, 0.01, 10
BC1 = 1.0 - B1 ** STEP
BC2 = 1.0 - B2 ** STEP

BF = jnp.bfloat16
F32 = jnp.float32
MESH = pl.DeviceIdType.MESH
VMEM_LIMIT_V7X = 58 * 1024 * 1024
TM = 256
TM_FFN_BWD = 512
FF_CHUNK = 256
FF_HALF = FF // 2

VSPEC = pl.BlockSpec(memory_space=pltpu.VMEM)
SSPEC = pl.BlockSpec(memory_space=pltpu.SMEM)
ANYSPEC = pl.BlockSpec(memory_space=pl.ANY)


def _params(n_grid=0):
    sem = ("arbitrary",) * n_grid if n_grid else None
    return pltpu.CompilerParams(dimension_semantics=sem, vmem_limit_bytes=VMEM_LIMIT_V7X)


def _sds(shape, dtype=F32):
    return jax.ShapeDtypeStruct(tuple(shape), dtype)


Rider = collections.namedtuple("Rider", "arrays out_shapes aliases scratch start mid finish")
MID_NUM, MID_DEN = 5, 8


def _call(body, *, name, grid, in_specs, out_specs, out_shape, args, scratch_shapes=(), rider=None, prefetch=None):
    ni, no, ns = len(in_specs), len(out_specs), len(scratch_shapes)
    npre = 0 if prefetch is None else 1
    pre = [] if prefetch is None else [prefetch]
    if rider is None:
        rider = Rider([], [], {}, [], None, None, None)
    ri, ro = len(rider.arrays), len(rider.out_shapes)

    def full(*refs):
        pre_refs, refs = refs[:npre], refs[npre:]
        ins, refs = refs[:ni], refs[ni:]
        rins, refs = refs[:ri], refs[ri:]
        outs, refs = refs[:no], refs[no:]
        routs, refs = refs[:ro], refs[ro:]
        scr, rscr = refs[:ns], refs[ns:]
        ids = [pl.program_id(a) for a in range(len(grid))]
        first = ids[0] == 0
        last = ids[0] == grid[0] - 1
        for a in range(1, len(grid)):
            first = first & (ids[a] == 0)
            last = last & (ids[a] == grid[a] - 1)

        if rider.start is not None:
            @pl.when(first)
            def _():
                rider.start(rins, routs, rscr)

        if rider.mid is not None:
            assert len(grid) == 1

            @pl.when(ids[0] == (grid[0] * MID_NUM) // MID_DEN)
            def _():
                rider.mid(rins, routs, rscr)

        body(*pre_refs, *ins, *outs, *scr)

        if rider.finish is not None:
            @pl.when(last)
            def _():
                rider.finish(rins, routs, rscr)

    outs = pl.pallas_call(
        full, name=name,
        grid_spec=pltpu.PrefetchScalarGridSpec(
            num_scalar_prefetch=npre, grid=grid,
            in_specs=list(in_specs) + [ANYSPEC] * ri, out_specs=list(out_specs) + [ANYSPEC] * ro,
            scratch_shapes=list(scratch_shapes) + list(rider.scratch)),
        out_shape=list(out_shape) + list(rider.out_shapes),
        input_output_aliases={npre + ni + a: no + b for a, b in rider.aliases.items()},
        compiler_params=_params(len(grid)))(*pre, *args, *rider.arrays)
    return list(outs[:no]), list(outs[no:])


def _run(name, rider):
    ri = len(rider.arrays)

    def body(*refs):
        rins, routs, rscr = refs[:ri], refs[ri:ri + len(rider.out_shapes)], refs[ri + len(rider.out_shapes):]
        rider.start(rins, routs, rscr)
        if rider.mid is not None:
            rider.mid(rins, routs, rscr)
        rider.finish(rins, routs, rscr)

    return pl.pallas_call(
        body, name=name, in_specs=[ANYSPEC] * ri, out_specs=[ANYSPEC] * len(rider.out_shapes),
        out_shape=list(rider.out_shapes), scratch_shapes=list(rider.scratch),
        input_output_aliases=dict(rider.aliases), compiler_params=_params())(*rider.arrays)


Job = collections.namedtuple("Job", "steps ins outs fn")


def _multi_call(name, jobs, kc, rider=None):
    n = max(job.steps for job in jobs)

    def clamped(index, steps):
        return lambda s, kc_ref: index(jnp.minimum(s, steps - 1), kc_ref)

    in_specs, out_specs, out_shape, args = [], [], [], []
    for job in jobs:
        for arr, block, index, *single in job.ins:
            mode = dict(pipeline_mode=pl.Buffered(1)) if single and single[0] else {}
            in_specs.append(pl.BlockSpec(block, clamped(index, job.steps), **mode))
            args.append(arr)
        for sds, block, index in job.outs:
            out_specs.append(pl.BlockSpec(block, clamped(index, job.steps)))
            out_shape.append(sds)
    n_in = len(args)

    def body(kc_ref, *refs):
        s = pl.program_id(0)
        i0, o0 = 0, n_in
        for job in jobs:
            ins, outs = refs[i0:i0 + len(job.ins)], refs[o0:o0 + len(job.outs)]
            i0, o0 = i0 + len(job.ins), o0 + len(job.outs)

            @pl.when(s < job.steps)
            def _():
                job.fn(s, kc_ref, ins, outs)

    outs, routs = _call(body, name=name, grid=(n,), in_specs=in_specs, out_specs=out_specs, out_shape=out_shape,
                        args=args, prefetch=kc, rider=rider)
    res, o0 = [], 0
    for job in jobs:
        res.append(outs[o0:o0 + len(job.outs)])
        o0 += len(job.outs)
    return res if rider is None else (res, routs)


def _rms_fwd(x, g):
    r = lax.rsqrt(jnp.mean(x * x, axis=-1, keepdims=True) + EPS)
    return x * r * g


def _rms_bwd(x, g, dy):
    r = lax.rsqrt(jnp.mean(x * x, axis=-1, keepdims=True) + EPS)
    xn = x * r
    dxn = dy * g
    dx = r * (dxn - xn * jnp.mean(dxn * xn, axis=-1, keepdims=True))
    return dx, dy * xn


def _rowsum(a):
    return jnp.sum(a, axis=0, keepdims=True)


def _sigmoid(z):
    return 1.0 / (1.0 + jnp.exp(-z))


def _dot(a, b):
    return jnp.dot(a, b, preferred_element_type=F32)


def _dot_nt(a, b):
    return lax.dot_general(a, b, (((1,), (1,)), ((), ())), preferred_element_type=F32)


def _dot_tn(a, b):
    return lax.dot_general(a, b, (((0,), (0,)), ((), ())), preferred_element_type=F32)


def _row_spec(tm, width=D):
    return pl.BlockSpec((tm, width), lambda i: (i, 0))


def _const_spec(shape):
    zeros = (0,) * len(shape)
    return pl.BlockSpec(tuple(shape), lambda *_: zeros)


def _pool_delta(he, pos):
    out = []
    for gi, w in enumerate(WINDOWS):
        hg = he[:, gi * POOL_G:(gi + 1) * POOL_G]
        s = hg
        k = 1
        while k < w:
            s = s + pltpu.roll(s, k, 0)
            k *= 2
        cnt = jnp.maximum(jnp.minimum(pos + 1, w), 1).astype(F32)
        out.append(s / cnt - hg)
    return out


def _load_with_halo_before(x_ref, i, tm):
    r0 = pl.multiple_of(i * tm, tm)
    hs = pl.multiple_of(jnp.maximum(i * tm - HALO, 0), 8)
    xh = jnp.where(i > 0, x_ref[pl.ds(hs, HALO), :], 0.0)
    xt = x_ref[pl.ds(r0, tm), :]
    return xt, jnp.concatenate([xh, xt], axis=0)


def _mixa_fwd_job(x, pre_g, pool_w, pool_scale, post_g):
    s_len = x.shape[0]

    def fn(i, kc_ref, ins, outs):
        x_ref, pg_ref, w_ref, sc_ref, qg_ref = ins
        y_ref, x1_ref = outs
        xt, xe = _load_with_halo_before(x_ref, i, TM)
        he = _rms_fwd(xe, pg_ref[0:1, :])
        pos = i * TM - HALO + lax.broadcasted_iota(jnp.int32, (TM + HALO, 1), 0)
        ds = _pool_delta(he, pos)
        ys = [_dot(ds[gi][HALO:, :].astype(BF), w_ref[gi]) for gi in range(len(WINDOWS))]
        y = jnp.concatenate(ys, axis=1) * sc_ref[...]
        y_ref[...] = y
        x1_ref[...] = xt + _rms_fwd(y, qg_ref[0:1, :])

    def whole(a):
        zeros = (0,) * a.ndim
        return (a, a.shape, lambda j, kc_ref: zeros, True)

    rows = lambda j, kc_ref: (j, 0)
    return Job(s_len // TM, [whole(a) for a in (x, pre_g, pool_w, pool_scale, post_g)],
               [(_sds((s_len, D)), (TM, D), rows), (_sds((s_len, D)), (TM, D), rows)], fn)


def _mixa_bwd(dx1, x, y, pre_g, pool_w, pool_scale, post_g, rider=None):
    s_len = x.shape[0]
    n = s_len // TM
    ng = len(WINDOWS)

    def body(dx_ref, x_ref, y_ref, pg_ref, w_ref, sc_ref, qg_ref,
             dx0_ref, dw_ref, dsc_ref, dqg_ref, dpg_ref, wacc):
        i = pl.program_id(0)

        @pl.when(i == 0)
        def _():
            wacc[...] = jnp.zeros_like(wacc)
            dsc_ref[...] = jnp.zeros_like(dsc_ref)
            dqg_ref[...] = jnp.zeros_like(dqg_ref)
            dpg_ref[...] = jnp.zeros_like(dpg_ref)

        r0 = pl.multiple_of(i * TM, TM)
        xt, xe = _load_with_halo_before(x_ref, i, TM)
        he = _rms_fwd(xe, pg_ref[0:1, :])
        pos_b = i * TM - HALO + lax.broadcasted_iota(jnp.int32, (TM + HALO, 1), 0)
        ds = _pool_delta(he, pos_b)

        last = i == n - 1
        a0 = pl.multiple_of(jnp.minimum(i * TM + TM, s_len - HALO), 8)
        ye = jnp.concatenate([y_ref[pl.ds(r0, TM), :], y_ref[pl.ds(a0, HALO), :]], axis=0)
        dt = dx_ref[pl.ds(r0, TM), :]
        de = jnp.concatenate([dt, jnp.where(last, 0.0, dx_ref[pl.ds(a0, HALO), :])], axis=0)
        dye, prod = _rms_bwd(ye, qg_ref[0:1, :], de)
        dqg_ref[...] += _rowsum(prod[:TM, :])
        dys = dye * sc_ref[...]
        pos_a = i * TM + lax.broadcasted_iota(jnp.int32, (TM + HALO, 1), 0)

        dhs, dscs = [], []
        for gi, w in enumerate(WINDOWS):
            sl = slice(gi * POOL_G, (gi + 1) * POOL_G)
            wg = w_ref[gi]
            dys_g = dys[:, sl].astype(BF)
            d_g = ds[gi][HALO:, :].astype(BF)
            ypre = _dot(d_g, wg)
            dscs.append(_rowsum(dye[:TM, sl] * ypre))
            wacc[gi] += _dot_tn(d_g, dys_g[:TM, :])
            dd = _dot_nt(dys_g, wg)
            cnt = jnp.minimum(pos_a + 1, w).astype(F32)
            a = dd / cnt
            k = 1
            while k < w:
                a = a + pltpu.roll(a, TM + HALO - k, 0)
                k *= 2
            dhs.append(a[:TM, :] - dd[:TM, :])
        dsc_ref[...] += jnp.concatenate(dscs, axis=1)
        dh = jnp.concatenate(dhs, axis=1)
        dxp, prod2 = _rms_bwd(xt, pg_ref[0:1, :], dh)
        dpg_ref[...] += _rowsum(prod2)
        dx0_ref[...] = dt + dxp

        @pl.when(last)
        def _():
            dw_ref[...] = wacc[...].astype(BF)

    return _call(
        body, name="mixa_bwd", grid=(n,), in_specs=[VSPEC] * 7,
        out_specs=[_row_spec(TM), _const_spec((ng, POOL_G, POOL_G)), _const_spec((1, D)),
                   _const_spec((1, D)), _const_spec((1, D))],
        out_shape=[_sds((s_len, D)), _sds((ng, POOL_G, POOL_G), BF), _sds((1, D)), _sds((1, D)), _sds((1, D))],
        scratch_shapes=[pltpu.VMEM((ng, POOL_G, POOL_G), F32)],
        args=[dx1, x, y, pre_g, pool_w, pool_scale, post_g], rider=rider)


def _ffn_fwd(layer, x1, pre_g, wgu, wd, post_g, rider=None):
    s_len = x1.shape[0]

    def body(x_ref, pg_ref, wgu_ref, wd_ref, qg_ref, f_ref, x2_ref, g_ref, u_ref):
        x = x_ref[...]
        h = _rms_fwd(x, pg_ref[layer:layer + 1, :]).astype(BF)
        f = jnp.zeros((TM, D), F32)
        for c in range(FF // FF_HALF):
            cols = slice(c * FF_HALF, (c + 1) * FF_HALF)
            g = _dot(h, wgu_ref[0, :, cols])
            u = _dot(h, wgu_ref[1, :, cols])
            g_ref[:, cols] = g.astype(BF)
            u_ref[:, cols] = u.astype(BF)
            act = g * _sigmoid(g) * u
            f = f + _dot(act.astype(BF), wd_ref[cols, :])
        f_ref[...] = f
        x2_ref[...] = x + _rms_fwd(f, qg_ref[layer:layer + 1, :])

    return _call(body, name=f"ffn_fwd{layer}", grid=(s_len // TM,),
                 in_specs=[_row_spec(TM), VSPEC, VSPEC, VSPEC, VSPEC],
                 out_specs=[_row_spec(TM), _row_spec(TM), _row_spec(TM, FF), _row_spec(TM, FF)],
                 out_shape=[_sds((s_len, D)), _sds((s_len, D)), _sds((s_len, FF), BF), _sds((s_len, FF), BF)],
                 args=[x1, pre_g, wgu, wd, post_g], rider=rider)


GU_PIECE = 128
DN_PIECE = 64
DN_SLOT = FF // N_CHIPS
HALF_D = D // 2


def _ffn_bwd(layer, dx2, x1, f, g_pre, u_pre, pre_g, wgu, wd, post_g, kc, rider=None):
    s_len = x1.shape[0]
    tm = TM_FFN_BWD
    n = s_len // tm
    nc = FF // FF_CHUNK
    n_gu, n_dn = FF_CHUNK // GU_PIECE, FF_CHUNK // DN_PIECE
    n_pieces = 2 * n_gu + n_dn
    n_blk = FF_HALF // GU_PIECE

    def edge_rows(c, i, kc_ref):
        return (jnp.where((c == 0) | (c == nc - 1), i, n - 1), 0)

    def chunk_at(c, kc_ref):
        return (c + (((kc_ref[0] + 1) % N_CHIPS) * nc) // N_CHIPS) % nc

    def exchange(kc_ref, c, accg, accu, accd, own_gu_ref, land_gu_ref, own_dn_ref, land_dn_ref,
                 pl_gu, pl_dn, sib_gu, sib_dn, mine_gu, mine_dn, sum_gu, sum_dn,
                 psend, precv, ssend, lsem, rrecv):
        x, y, core = lax.axis_index("x"), lax.axis_index("y"), lax.axis_index("c")
        lower = core == 0

        def pair_copy(cc, part):
            p = cc % 2
            src, dst = ((sib_gu, pl_gu), (sib_dn, pl_dn))[part]
            return pltpu.make_async_remote_copy(src.at[p], dst.at[cc], psend.at[p, part], precv.at[cc, part],
                                                device_id=(x, y, 1 - core), device_id_type=MESH)

        def scatter(cc, wait):
            p = cc % 2
            hidden = chunk_at(cc, kc_ref) * FF_CHUNK

            assert n_gu == 2
            k0, k1 = hidden // FF_HALF, (hidden + GU_PIECE) // FF_HALF
            blk = (hidden - k0 * FF_HALF) // GU_PIECE
            for gu in range(2):
                @pl.when(k0 == k1)
                def _():
                    piece(p, wait, 2 * gu, sum_gu.at[p, gu], k0 + 2 * gu, 0, (pl.ds(blk, 2),))

                @pl.when(k0 != k1)
                def _():
                    piece(p, wait, 2 * gu, sum_gu.at[p, gu, 0], k0 + 2 * gu, 0, (blk,))
                    piece(p, wait, 2 * gu + 1, sum_gu.at[p, gu, 1], k1 + 2 * gu, 0, (0,))

            kd = hidden // DN_SLOT
            off = pl.multiple_of(hidden - kd * DN_SLOT, DN_PIECE)
            m = jnp.minimum((DN_SLOT - off) // DN_PIECE, n_dn)
            for mm in range(1, n_dn + 1):
                @pl.when(m == mm)
                def _():
                    rows = mm * DN_PIECE
                    piece(p, wait, 2 * n_gu, sum_dn.at[p, pl.ds(0, rows), :], kd, 1, (pl.ds(off, rows), slice(None)))
                    if mm < n_dn:
                        piece(p, wait, 2 * n_gu + 1, sum_dn.at[p, pl.ds(rows, FF_CHUNK - rows), :], kd + 1, 1,
                              (pl.ds(0, FF_CHUNK - rows), slice(None)))

        def piece(p, wait, pi, src, k, t, where):
            own_ref, land_ref = ((own_gu_ref, land_gu_ref), (own_dn_ref, land_dn_ref))[t]
            kx, ky = k // 2, k % 2
            fx, fy = (kx != x).astype(jnp.int32), (ky != y).astype(jnp.int32)
            local = (fx + fy) == 0
            j = jnp.maximum(fx + 2 * fy - 1, 0)

            @pl.when(local)
            def _():
                cp = pltpu.make_async_copy(src, own_ref.at[where], lsem.at[p, pi])
                if wait:
                    cp.wait()
                else:
                    cp.start()

            @pl.when(jnp.logical_not(local))
            def _():
                cp = pltpu.make_async_remote_copy(src, land_ref.at[(j,) + where], ssend.at[p, pi],
                                                  rrecv.at[t, j], device_id=(kx, ky, core), device_id_type=MESH)
                if wait:
                    cp.wait_send()
                else:
                    cp.start()

        def add_and_scatter(cc):
            p = cc % 2
            pair_copy(cc, 0).wait_recv()
            pair_copy(cc, 1).wait_recv()
            s_gu = (mine_gu[...] + pl_gu[cc].astype(F32)).astype(BF)
            for hc in range(n_gu):
                sum_gu[p, :, hc] = s_gu[:, :, hc * GU_PIECE:(hc + 1) * GU_PIECE]
            sum_dn[p] = (mine_dn[...] + pl_dn[cc].astype(F32)).astype(BF)
            scatter(cc, wait=False)

        @pl.when(c >= 1)
        def _():
            @pl.when(c >= 3)
            def _():
                scatter(c - 3, wait=True)
            add_and_scatter(c - 1)

        @pl.when(c >= 2)
        def _():
            pair_copy(c - 2, 0).wait_send()
            pair_copy(c - 2, 1).wait_send()

        p = c % 2
        my_rows = pl.ds(pl.multiple_of(core * HALF_D, HALF_D), HALF_D)
        sib_rows = pl.ds(pl.multiple_of((1 - core) * HALF_D, HALF_D), HALF_D)
        d_v = accd[...]
        sib_gu[p, 0] = accg[sib_rows, :].astype(BF)
        sib_gu[p, 1] = accu[sib_rows, :].astype(BF)
        sib_dn[p] = jnp.where(lower, d_v[:, HALF_D:], d_v[:, :HALF_D]).astype(BF)
        mine_gu[0] = accg[my_rows, :]
        mine_gu[1] = accu[my_rows, :]
        mine_dn[...] = jnp.where(lower, d_v[:, :HALF_D], d_v[:, HALF_D:])
        pair_copy(c, 0).start()
        pair_copy(c, 1).start()

        @pl.when(c == nc - 1)
        def _():
            scatter(nc - 3, wait=True)
            add_and_scatter(nc - 1)
            for cc in (nc - 2, nc - 1):
                pair_copy(cc, 0).wait_send()
                pair_copy(cc, 1).wait_send()
                scatter(cc, wait=True)
            for t, land_ref in enumerate((land_gu_ref, land_dn_ref)):
                for j in range(N_CHIPS - 1):
                    pltpu.make_async_remote_copy(land_ref.at[j], land_ref.at[j], ssend.at[0, 0], rrecv.at[t, j],
                                                 device_id=(x, y, core), device_id_type=MESH).wait_recv()

    def body(kc_ref, dx_ref, x_ref, f_ref, gp_ref, up_ref, pg_ref, wgu_ref, wd_ref, qg_ref,
             dx1_ref, dpg_ref, dqg_ref, own_gu_ref, land_gu_ref, own_dn_ref, land_dn_ref,
             h_s, df_s, dh_s, accg, accu, accd, *comm):
        c = pl.program_id(0)
        i = pl.program_id(1)
        rows = pl.ds(pl.multiple_of(i * tm, tm), tm)
        pg = pg_ref[layer:layer + 1, :]

        @pl.when((c == 0) & (i == 0))
        def _():
            dpg_ref[...] = jnp.zeros_like(dpg_ref)
            dqg_ref[...] = jnp.zeros_like(dqg_ref)

        @pl.when(c == 0)
        def _():
            h_s[rows, :] = _rms_fwd(x_ref[...], pg).astype(BF)
            df, prod = _rms_bwd(f_ref[...], qg_ref[layer:layer + 1, :], dx_ref[...])
            df_s[rows, :] = df.astype(BF)
            dqg_ref[...] += _rowsum(prod)

        @pl.when(i == 0)
        def _():
            accg[...] = jnp.zeros_like(accg)
            accu[...] = jnp.zeros_like(accu)
            accd[...] = jnp.zeros_like(accd)

        h = h_s[rows, :]
        df = df_s[rows, :]
        wg = wgu_ref[0]
        wu = wgu_ref[1]
        g = gp_ref[...].astype(F32)
        u = up_ref[...].astype(F32)
        sg = _sigmoid(g)
        a = g * sg
        dact = _dot_nt(df, wd_ref[...])
        accd[...] += _dot_tn((a * u).astype(BF), df)
        du = (dact * a).astype(BF)
        dg = (dact * u * (sg * (1.0 + g * (1.0 - sg)))).astype(BF)
        accg[...] += _dot_tn(h, dg)
        accu[...] += _dot_tn(h, du)
        dh = _dot_nt(dg, wg) + _dot_nt(du, wu)

        @pl.when(c == 0)
        def _():
            dh_s[rows, :] = dh

        @pl.when((c > 0) & (c < nc - 1))
        def _():
            dh_s[rows, :] += dh

        @pl.when(c == nc - 1)
        def _():
            dxp, prod = _rms_bwd(x_ref[...], pg, dh_s[rows, :] + dh)
            dpg_ref[...] += _rowsum(prod)
            dx1_ref[...] = dx_ref[...] + dxp

        @pl.when(i == n - 1)
        def _():
            exchange(kc_ref, c, accg, accu, accd, own_gu_ref, land_gu_ref, own_dn_ref, land_dn_ref, *comm)

    dma = pltpu.SemaphoreType.DMA
    return _call(
        body, name=f"ffn_bwd{layer}", grid=(nc, n),
        in_specs=[pl.BlockSpec((tm, D), edge_rows), pl.BlockSpec((tm, D), edge_rows),
                  pl.BlockSpec((tm, D), lambda c, i, kc_ref: (jnp.where(c == 0, i, n - 1), 0),
                               pipeline_mode=pl.Buffered(1)),
                  pl.BlockSpec((tm, FF_CHUNK), lambda c, i, kc_ref: (i, chunk_at(c, kc_ref))),
                  pl.BlockSpec((tm, FF_CHUNK), lambda c, i, kc_ref: (i, chunk_at(c, kc_ref))),
                  VSPEC,
                  pl.BlockSpec((2, D, FF_CHUNK), lambda c, i, kc_ref: (0, 0, chunk_at(c, kc_ref))),
                  pl.BlockSpec((FF_CHUNK, D), lambda c, i, kc_ref: (chunk_at(c, kc_ref), 0)),
                  VSPEC],
        out_specs=[pl.BlockSpec((tm, D), lambda c, i, kc_ref: (jnp.where(c == nc - 1, i, 0), 0)),
                   _const_spec((1, D)), _const_spec((1, D)), ANYSPEC, ANYSPEC, ANYSPEC, ANYSPEC],
        out_shape=[_sds((s_len, D)), _sds((1, D)), _sds((1, D)),
                   _sds((n_blk, HALF_D, GU_PIECE), BF), _sds((N_CHIPS - 1, n_blk, HALF_D, GU_PIECE), BF),
                   _sds((DN_SLOT, HALF_D), BF), _sds((N_CHIPS - 1, DN_SLOT, HALF_D), BF)],
        scratch_shapes=[pltpu.VMEM((s_len, D), BF), pltpu.VMEM((s_len, D), BF), pltpu.VMEM((s_len, D), F32),
                        pltpu.VMEM((D, FF_CHUNK), F32), pltpu.VMEM((D, FF_CHUNK), F32),
                        pltpu.VMEM((FF_CHUNK, D), F32),
                        pltpu.VMEM((nc, 2, HALF_D, FF_CHUNK), BF), pltpu.VMEM((nc, FF_CHUNK, HALF_D), BF),
                        pltpu.VMEM((2, 2, HALF_D, FF_CHUNK), BF), pltpu.VMEM((2, FF_CHUNK, HALF_D), BF),
                        pltpu.VMEM((2, HALF_D, FF_CHUNK), F32), pltpu.VMEM((FF_CHUNK, HALF_D), F32),
                        pltpu.VMEM((2, 2, n_gu, HALF_D, GU_PIECE), BF), pltpu.VMEM((2, FF_CHUNK, HALF_D), BF),
                        dma((2, 2)), dma((nc, 2)), dma((2, n_pieces)), dma((2, n_pieces)), dma((2, N_CHIPS - 1))],
        args=[dx2, x1, f, g_pre, u_pre, pre_g, wgu, wd, post_g], rider=rider, prefetch=kc)


def _ple_fwd(layer, x2, p, ple_g, w_gate, w_proj, post_g, target=None, qkv=None, rider=None):
    s_len = x2.shape[0]
    final = target is not None
    assert not (final and qkv)

    def body(*refs):
        if final:
            x_ref, p_ref, g_ref, wg_ref, wp_ref, qg_ref, t_ref, z_ref, pe_ref, dx_ref, lv_ref = refs
        elif qkv:
            (x_ref, p_ref, g_ref, wg_ref, wp_ref, qg_ref, ng_ref, kg_ref, wq_ref, wkv_ref,
             z_ref, pe_ref, x3_ref, q_ref, kv_ref) = refs
        else:
            x_ref, p_ref, g_ref, wg_ref, wp_ref, qg_ref, z_ref, pe_ref, x3_ref = refs
        x = x_ref[...]
        r = _rms_fwd(x, g_ref[layer:layer + 1, :]).astype(BF)
        z = _dot(r, wg_ref[...])
        pe = _dot(p_ref[...].astype(BF), wp_ref[...])
        z_ref[...] = z
        pe_ref[...] = pe
        x3 = x + _rms_fwd(pe * _sigmoid(z), qg_ref[layer:layer + 1, :])
        if final:
            @pl.when(pl.program_id(0) == 0)
            def _():
                lv_ref[...] = jnp.zeros_like(lv_ref)
            err = x3 - t_ref[...]
            dx_ref[...] = err * (1.0 / D)
            lv_ref[...] += _rowsum(err * err)
        else:
            x3_ref[...] = x3
        if qkv:
            q_ref[...] = _dot(_rms_fwd(x3, ng_ref[layer + 1:layer + 2, :]).astype(BF), wq_ref[...]).astype(BF)
            kv_ref[...] = _dot(_rms_fwd(x3, kg_ref[...]).astype(BF), wkv_ref[...]).astype(BF)

    p_spec = pl.BlockSpec((None, TM, PLE), lambda i: (layer, i, 0))
    in_specs = [_row_spec(TM), p_spec, VSPEC, VSPEC, VSPEC, VSPEC]
    args = [x2, p, ple_g, w_gate, w_proj, post_g]
    out_specs = [_row_spec(TM), _row_spec(TM), _row_spec(TM)]
    out_shape = [_sds((s_len, D))] * 3
    if qkv:
        in_specs += [VSPEC] * 4
        args += list(qkv)
        out_specs += [_row_spec(TM), _row_spec(TM, 2 * KVD)]
        out_shape += [_sds((s_len, D), BF), _sds((s_len, 2 * KVD), BF)]
    if final:
        in_specs.append(_row_spec(TM))
        args.append(target)
        out_specs.append(_const_spec((1, D)))
        out_shape.append(_sds((1, D)))
    return _call(body, name=f"ple_fwd{layer}", grid=(s_len // TM,), in_specs=in_specs, out_specs=out_specs,
                 out_shape=out_shape, args=args, rider=rider)


def _ple_bwd(layer, dx3, x2, z, pe, p, ple_g, w_gate, post_g, rider=None):
    s_len = x2.shape[0]
    n = s_len // TM

    def body(dx_ref, x_ref, z_ref, pe_ref, p_ref, g_ref, wg_ref, qg_ref,
             dx2_ref, dwg_ref, dwp_ref, dg_ref, dqg_ref, gacc, pacc):
        i = pl.program_id(0)

        @pl.when(i == 0)
        def _():
            gacc[...] = jnp.zeros_like(gacc)
            pacc[...] = jnp.zeros_like(pacc)
            dg_ref[...] = jnp.zeros_like(dg_ref)
            dqg_ref[...] = jnp.zeros_like(dqg_ref)

        dx = dx_ref[...]
        x = x_ref[...]
        pe_v = pe_ref[...]
        gate = _sigmoid(z_ref[...])
        de, prod = _rms_bwd(pe_v * gate, qg_ref[layer:layer + 1, :], dx)
        dqg_ref[...] += _rowsum(prod)
        dpe = (de * gate).astype(BF)
        dz = (de * pe_v * gate * (1.0 - gate)).astype(BF)
        pacc[...] += _dot_tn(p_ref[...].astype(BF), dpe)
        g = g_ref[layer:layer + 1, :]
        r = _rms_fwd(x, g).astype(BF)
        gacc[...] += _dot_tn(r, dz)
        dr = _dot_nt(dz, wg_ref[...])
        dxp, prod2 = _rms_bwd(x, g, dr)
        dg_ref[...] += _rowsum(prod2)
        dx2_ref[...] = dx + dxp

        @pl.when(i == n - 1)
        def _():
            dwg_ref[...] = gacc[...].astype(BF)
            dwp_ref[...] = pacc[...].astype(BF)

    p_spec = pl.BlockSpec((None, TM, PLE), lambda i: (layer, i, 0))
    return _call(
        body, name=f"ple_bwd{layer}", grid=(n,),
        in_specs=[_row_spec(TM), _row_spec(TM), _row_spec(TM), _row_spec(TM), p_spec, VSPEC, VSPEC, VSPEC],
        out_specs=[_row_spec(TM), _const_spec((D, D)), _const_spec((PLE, D)), _const_spec((1, D)), _const_spec((1, D))],
        out_shape=[_sds((s_len, D)), _sds((D, D), BF), _sds((PLE, D), BF), _sds((1, D)), _sds((1, D))],
        scratch_shapes=[pltpu.VMEM((D, D), F32), pltpu.VMEM((PLE, D), F32)],
        args=[dx3, x2, z, pe, p, ple_g, w_gate, post_g], rider=rider)


def _qkv_bwd(dq, dkv, x3, dx4, q_g, kv_g, w_q, w_kv):
    s_len = x3.shape[0]
    n = s_len // TM

    def body(dq_ref, dkv_ref, x_ref, dx_ref, qg_ref, kg_ref, wq_ref, wkv_ref,
             dx3_ref, dwq_ref, dwkv_ref, dqg_ref, dkg_ref, qacc, kacc):
        i = pl.program_id(0)

        @pl.when(i == 0)
        def _():
            qacc[...] = jnp.zeros_like(qacc)
            kacc[...] = jnp.zeros_like(kacc)
            dqg_ref[...] = jnp.zeros_like(dqg_ref)
            dkg_ref[...] = jnp.zeros_like(dkg_ref)

        x = x_ref[...]
        qg = qg_ref[1:2, :]
        kg = kg_ref[...]
        dq_v = dq_ref[...]
        dkv_v = dkv_ref[...].astype(BF)
        qacc[...] += _dot_tn(_rms_fwd(x, qg).astype(BF), dq_v)
        kacc[...] += _dot_tn(_rms_fwd(x, kg).astype(BF), dkv_v)
        dxq, prod_q = _rms_bwd(x, qg, _dot_nt(dq_v, wq_ref[...]))
        dxk, prod_k = _rms_bwd(x, kg, _dot_nt(dkv_v, wkv_ref[...]))
        dqg_ref[...] += _rowsum(prod_q)
        dkg_ref[...] += _rowsum(prod_k)
        dx3_ref[...] = dx_ref[...] + dxq + dxk

        @pl.when(i == n - 1)
        def _():
            dwq_ref[...] = qacc[...].astype(BF)
            dwkv_ref[...] = kacc[...].astype(BF)

    outs, _ = _call(
        body, name="qkv_bwd", grid=(n,),
        in_specs=[_row_spec(TM), _row_spec(TM, 2 * KVD), _row_spec(TM), _row_spec(TM), VSPEC, VSPEC, VSPEC, VSPEC],
        out_specs=[_row_spec(TM), _const_spec((D, D)), _const_spec((D, 2 * KVD)),
                   _const_spec((1, D)), _const_spec((1, D))],
        out_shape=[_sds((s_len, D)), _sds((D, D), BF), _sds((D, 2 * KVD), BF), _sds((1, D)), _sds((1, D))],
        scratch_shapes=[pltpu.VMEM((D, D), F32), pltpu.VMEM((D, 2 * KVD), F32)],
        args=[dq, dkv, x3, dx4, q_g, kv_g, w_q, w_kv])
    return outs


def _attn_group(i, q, kvw, sink_ref, g):
    rows = GQA * BLK
    heads = [GQA * g + j for j in range(GQA)]
    off = jnp.where(i > 0, BLK, 0)
    row = lax.broadcasted_iota(jnp.int32, (rows, 2 * BLK), 0)
    rel = (row % BLK) - lax.broadcasted_iota(jnp.int32, (rows, 2 * BLK), 1) + off
    valid = (rel >= 0) & (rel < BLK)
    head_of_row = lax.broadcasted_iota(jnp.int32, (rows, 1), 0) // BLK
    slope = jnp.zeros((rows, 1), F32)
    sink = jnp.zeros((rows, 1), F32)
    for j, h in enumerate(heads):
        slope = jnp.where(head_of_row == j, SLOPES[h], slope)
        sink = jnp.where(head_of_row == j, sink_ref[0, h], sink)
    qs = jnp.concatenate([q[:, h * HEAD_DIM:(h + 1) * HEAD_DIM] for h in heads], axis=0)
    k = kvw[:, g * HEAD_DIM:(g + 1) * HEAD_DIM]
    v = kvw[:, KVD + g * HEAD_DIM:KVD + (g + 1) * HEAD_DIM]
    s = _dot_nt(qs, k) * ATT_SCALE - slope * rel.astype(F32)
    s = jnp.where(valid, s, NEG_INF)
    m = jnp.maximum(jnp.max(s, axis=-1, keepdims=True), sink)
    e = jnp.exp(s - m)
    es = jnp.exp(sink - m)
    inv = 1.0 / (jnp.sum(e, axis=-1, keepdims=True) + es)
    return e * inv, es * inv, qs, k, v


def _unstack_heads(stacked):
    return [stacked[j * BLK:(j + 1) * BLK, :] for j in range(GQA)]


def _kv_window(kv_ref, i):
    ks = pl.multiple_of(jnp.maximum(i * BLK - BLK, 0), BLK)
    return ks, kv_ref[pl.ds(ks, 2 * BLK), :]


def _attn_fwd(q, kv, sinks, x3, w_o, post_g, rider=None):
    s_len = q.shape[0]

    def body(q_ref, kv_ref, sk_ref, x_ref, wo_ref, g_ref, a_ref, y_ref, x4_ref):
        i = pl.program_id(0)
        _, kvw = _kv_window(kv_ref, i)
        q = q_ref[...]
        outs = []
        for g in range(N_KV_HEADS):
            p, _, _, _, v = _attn_group(i, q, kvw, sk_ref, g)
            outs += _unstack_heads(_dot(p.astype(BF), v))
        attn = jnp.concatenate(outs, axis=1)
        a_ref[...] = attn
        y = _dot(attn.astype(BF), wo_ref[...])
        y_ref[...] = y
        x4_ref[...] = x_ref[...] + _rms_fwd(y, g_ref[1:2, :])

    return _call(body, name="attn_fwd", grid=(s_len // BLK,),
                 in_specs=[_row_spec(BLK), VSPEC, SSPEC, _row_spec(BLK), VSPEC, VSPEC],
                 out_specs=[_row_spec(BLK)] * 3, out_shape=[_sds((s_len, D))] * 3,
                 args=[q, kv, sinks, x3, w_o, post_g], rider=rider)


def _attn_bwd(dx4, y, attn, q, kv, sinks, w_o, post_g, rider=None):
    s_len = q.shape[0]
    n = s_len // BLK

    def body(dx_ref, y_ref, a_ref, q_ref, kv_ref, sk_ref, wo_ref, g_ref,
             dq_ref, dkv_ref, dwo_ref, dg_ref, dsk_ref, wacc):
        i = pl.program_id(0)

        @pl.when(i == 0)
        def _():
            dkv_ref[...] = jnp.zeros_like(dkv_ref)
            wacc[...] = jnp.zeros_like(wacc)
            dg_ref[...] = jnp.zeros_like(dg_ref)
            dsk_ref[...] = jnp.zeros_like(dsk_ref)

        dy, prod = _rms_bwd(y_ref[...], g_ref[1:2, :], dx_ref[...])
        dg_ref[...] += _rowsum(prod)
        dyb = dy.astype(BF)
        attn = a_ref[...]
        wacc[...] += _dot_tn(attn.astype(BF), dyb)
        d_o = _dot_nt(dyb, wo_ref[...])
        dod = d_o * attn
        ks, kvw = _kv_window(kv_ref, i)
        q = q_ref[...]
        lane = lax.broadcasted_iota(jnp.int32, (1, D), 1)
        dqs, dks, dvs = [], [], []
        dsk = jnp.zeros((1, D), F32)
        for g in range(N_KV_HEADS):
            p, ps, qs, k, v = _attn_group(i, q, kvw, sk_ref, g)
            cols = [slice((GQA * g + j) * HEAD_DIM, (GQA * g + j + 1) * HEAD_DIM) for j in range(GQA)]
            do_s = jnp.concatenate([d_o[:, c] for c in cols], axis=0).astype(BF)
            dsum = jnp.concatenate([jnp.sum(dod[:, c], axis=-1, keepdims=True) for c in cols], axis=0)
            dp = _dot_nt(do_s, v)
            dsb = (p * (dp - dsum) * ATT_SCALE).astype(BF)
            sink_part = ps * dsum
            for j in range(GQA):
                dsk = dsk + jnp.where(lane == GQA * g + j, -_rowsum(sink_part[j * BLK:(j + 1) * BLK, :]), 0.0)
            dqs += _unstack_heads(_dot(dsb, k))
            dks.append(_dot_tn(dsb, qs))
            dvs.append(_dot_tn(p.astype(BF), do_s))
        dsk_ref[...] += dsk
        dq_ref[...] = jnp.concatenate(dqs, axis=1).astype(BF)
        dkv_ref[pl.ds(ks, 2 * BLK), :] += jnp.concatenate(dks + dvs, axis=1)

        @pl.when(i == n - 1)
        def _():
            dwo_ref[...] = wacc[...].astype(BF)

    return _call(
        body, name="attn_bwd", grid=(n,),
        in_specs=[_row_spec(BLK), _row_spec(BLK), _row_spec(BLK), _row_spec(BLK), VSPEC, SSPEC, VSPEC, VSPEC],
        out_specs=[_row_spec(BLK), _const_spec((s_len, 2 * KVD)), _const_spec((D, D)),
                   _const_spec((1, D)), _const_spec((1, D))],
        out_shape=[_sds((s_len, D), BF), _sds((s_len, 2 * KVD)), _sds((D, D), BF), _sds((1, D)), _sds((1, D))],
        scratch_shapes=[pltpu.VMEM((D, D), F32)],
        args=[dx4, y, attn, q, kv, sinks, w_o, post_g], rider=rider)


Big = collections.namedtuple("Big", "name src layer L A R C rb")


def _bigs():
    out = {"pool_w": Big("pool_w", "pool_w", None, 4, 4, POOL_G // N_CHIPS, POOL_G, 32)}
    for l in range(2):
        out[f"w_gu{l}"] = Big(f"w_gu{l}", "w_gu", l, 1, 2, D, FF_HALF, 256)
        out[f"w_down{l}"] = Big(f"w_down{l}", "w_down", l, 1, 4, FF // N_CHIPS, D, 352)
        out[f"w_ple_gate{l}"] = Big(f"w_ple_gate{l}", "w_ple_gate", l, 1, 4, D // N_CHIPS, D, 128)
        out[f"w_ple_proj{l}"] = Big(f"w_ple_proj{l}", "w_ple_proj", l, 1, 1, PLE, D // N_CHIPS, 128)
    out["w_q"] = Big("w_q", "w_q", None, 1, 4, D // N_CHIPS, D, 128)
    out["w_o"] = Big("w_o", "w_o", None, 1, 4, D // N_CHIPS, D, 128)
    out["w_kv"] = Big("w_kv", "w_kv", None, 1, 4, D // N_CHIPS, 2 * KVD, 128)
    return out


BIGS = _bigs()
POOL_SCALE = Big("pool_scale", "pool_scale", None, 1, 1, 1, D // N_CHIPS, 1)
BIG_SOURCES = ("w_gu", "w_down", "w_ple_gate", "w_ple_proj", "w_q", "w_o", "w_kv", "pool_w")


def _ncb(t):
    return N_CHIPS // t.A


def _full_shape(t, rows=None):
    return (t.L, t.A, t.R if rows is None else rows, _ncb(t) * t.C)


def _slot_index(t, k):
    return k // _ncb(t), k % _ncb(t)


def _slot(ref, t, k, row0, rows):
    a, cb = _slot_index(t, k)
    return ref.at[:, a, pl.ds(row0, rows), pl.ds(pl.multiple_of(cb * t.C, 128), t.C)]


def _place_job(t, w, out_dtype=BF):
    nb = next((nb for nb in (8, 4, 2, 1) if t.R % (16 * nb) == 0), 1) if t.L == 1 else 1
    rb = t.R // nb

    def fn(j, kc_ref, ins, outs):
        outs[0][...] = ins[0][...].astype(out_dtype)

    def in_map(j, kc_ref):
        return (j // nb if t.layer is None else t.layer, j % nb, 0)

    def out_map(j, kc_ref):
        a, cb = _slot_index(t, kc_ref[0])
        return (j // nb, a, j % nb, cb)

    return Job(t.L * nb, [(w, (None, rb, t.C), in_map)],
               [(_sds(_full_shape(t), out_dtype), (None, None, rb, t.C), out_map)], fn)


def _mesh_position():
    x, y, c = lax.axis_index("x"), lax.axis_index("y"), lax.axis_index("c")
    chips = [(1 - x, y), (x, 1 - y), (1 - x, 1 - y)]
    return x, y, c, chips


DIRECT_BELOW = 128 * 1024


def _gather_rider(parts, fulls):
    nt = len(parts)
    TO_X, TO_Y, FWD_X, FWD_Y, SIB_X, SIB_Y, SIB_D = range(7)

    def rows_of(ti, core):
        t, r0, r1 = parts[ti]
        h = (r1 - r0) // 2
        return r0 + core * h, h

    def copy(outs, sems, kind, ti, k_src, row0, rows, dev):
        region = _slot(outs[ti], parts[ti][0], k_src, row0, rows)
        return pltpu.make_async_remote_copy(region, region, sems[0].at[ti, kind], sems[1].at[ti, kind],
                                            device_id=dev, device_id_type=MESH)

    def plan(outs, sems):
        x, y, c, _ = _mesh_position()
        me, kx, ky, kd = 2 * x + y, 2 * (1 - x) + y, 2 * x + (1 - y), 2 * (1 - x) + (1 - y)
        dev_x, dev_y, dev_d, sib = (1 - x, y, c), (x, 1 - y, c), (1 - x, 1 - y, c), (x, y, 1 - c)

        def whole(ti):
            return 0, parts[ti][0].R

        def mk(kind, k_send, k_recv, dev, send_rows, recv_rows):
            def build(ti, side):
                k_src = k_send if side == "s" else k_recv
                row0, rows = (send_rows if side == "s" else recv_rows)(ti)
                return copy(outs, sems, kind, ti, k_src, row0, rows, dev)
            return build

        def first_half(core):
            return lambda ti: (rows_of(ti, core)[0], rows_of(ti, core)[1] // 2)

        def second_half(core):
            return lambda ti: (rows_of(ti, core)[0] + rows_of(ti, core)[1] // 2, rows_of(ti, core)[1] // 2)

        mine = lambda ti: rows_of(ti, c)
        theirs = lambda ti: rows_of(ti, 1 - c)
        split = {
            TO_X: mk(TO_X, me, kx, dev_x, mine, mine),
            TO_Y: mk(TO_Y, me, ky, dev_y, mine, mine),
            FWD_X: mk(FWD_X, ky, kd, dev_x, first_half(c), first_half(c)),
            FWD_Y: mk(FWD_Y, kx, kd, dev_y, second_half(c), second_half(c)),
            SIB_X: mk(SIB_X, kx, kx, sib, mine, theirs),
            SIB_Y: mk(SIB_Y, ky, ky, sib, mine, theirs),
            SIB_D: mk(SIB_D, kd, kd, sib, mine, theirs),
        }
        direct = {
            TO_X: mk(TO_X, me, kx, dev_x, whole, whole),
            TO_Y: mk(TO_Y, me, ky, dev_y, whole, whole),
            FWD_X: mk(FWD_X, me, kd, dev_d, whole, whole),
        }
        return split, direct

    is_split = [t.L * t.R * t.C >= DIRECT_BELOW for t, _, _ in parts]
    assert all(s or (r0, r1) == (0, t.R) for s, (t, r0, r1) in zip(is_split, parts))

    def start(ins, outs, sems):
        split, direct = plan(outs, sems)
        for ti in range(nt):
            kinds = split if is_split[ti] else direct
            kinds[TO_X](ti, "s").start()
            kinds[TO_Y](ti, "s").start()
            if not is_split[ti]:
                kinds[FWD_X](ti, "s").start()

    def mid(ins, outs, sems):
        split, _ = plan(outs, sems)
        for ti in range(nt):
            if is_split[ti]:
                split[TO_Y](ti, "r").wait_recv()
                split[FWD_X](ti, "s").start()
                split[SIB_Y](ti, "s").start()
        for ti in range(nt):
            if is_split[ti]:
                split[TO_X](ti, "r").wait_recv()
                split[FWD_Y](ti, "s").start()
                split[SIB_X](ti, "s").start()

    def finish(ins, outs, sems):
        split, direct = plan(outs, sems)
        for ti in range(nt):
            if is_split[ti]:
                split[FWD_X](ti, "r").wait_recv()
                split[FWD_Y](ti, "r").wait_recv()
                split[SIB_D](ti, "s").start()
            else:
                for kind in (TO_X, TO_Y, FWD_X):
                    direct[kind](ti, "r").wait_recv()
        for ti in range(nt):
            if is_split[ti]:
                for kind in (SIB_X, SIB_Y, SIB_D):
                    split[kind](ti, "r").wait_recv()
        for ti in range(nt):
            kinds = split if is_split[ti] else direct
            for kind in kinds:
                kinds[kind](ti, "s").wait_send()

    sems = pltpu.SemaphoreType.DMA((nt, 7))
    return Rider(list(fulls), [_sds(a.shape, a.dtype) for a in fulls], {i: i for i in range(nt)},
                 [sems, sems], start, mid, finish)


def _pair_exchange(name, specs, grads):
    nt = len(specs)

    def body(*refs):
        gs = refs[:nt]
        lands = refs[nt:2 * nt]
        send, recv = refs[2 * nt:]
        x, y, c, _ = _mesh_position()
        cps = []
        for ti, t in enumerate(specs):
            half = t.R // 2
            cp = pltpu.make_async_remote_copy(gs[ti].at[:, :, pl.ds((1 - c) * half, half), :], lands[ti],
                                              send.at[ti], recv.at[ti],
                                              device_id=(x, y, 1 - c), device_id_type=MESH)
            cp.start()
            cps.append(cp)
        for cp in cps:
            cp.wait()

    return pl.pallas_call(
        body, name=name,
        in_specs=[ANYSPEC] * nt, out_specs=[ANYSPEC] * nt,
        out_shape=[_sds(_full_shape(t, t.R // 2), BF) for t in specs],
        scratch_shapes=[pltpu.SemaphoreType.DMA((nt,)), pltpu.SemaphoreType.DMA((nt,))],
        compiler_params=_params(),
    )(*grads)


def _pair_sum_job(t, g, land):
    assert t.L == 1
    half = t.R // 2
    nj = half // t.rb
    block = (None, t.A, t.rb, _ncb(t) * t.C)

    def fn(j, kc_ref, ins, outs):
        outs[0][...] = (ins[0][...].astype(F32) + ins[1][...].astype(F32)).astype(BF)

    return Job(nj,
               [(g, block, lambda j, kc_ref: (0, 0, kc_ref[1] * nj + j, 0)),
                (land, block, lambda j, kc_ref: (0, 0, j, 0))],
               [(_sds(_full_shape(t, half), BF), block, lambda j, kc_ref: (0, 0, j, 0))], fn)


def _scatter_rider(specs, sums):
    nt = len(specs)

    def copy(ins, outs, sems, ti, j, chip, c):
        t = specs[ti]
        cx, cy = chip
        return pltpu.make_async_remote_copy(_slot(ins[ti], t, 2 * cx + cy, 0, t.R // 2), outs[ti].at[j],
                                            sems[0].at[ti, j], sems[1].at[ti, j],
                                            device_id=(cx, cy, c), device_id_type=MESH)

    def start(ins, outs, sems):
        _, _, c, chips = _mesh_position()
        for j, chip in enumerate(chips):
            for ti in range(nt):
                copy(ins, outs, sems, ti, j, chip, c).start()

    def finish(ins, outs, sems):
        _, _, c, chips = _mesh_position()
        for j, chip in enumerate(chips):
            for ti in range(nt):
                copy(ins, outs, sems, ti, j, chip, c).wait()

    sems = pltpu.SemaphoreType.DMA((nt, N_CHIPS - 1))
    return Rider(list(sums), [_sds((N_CHIPS - 1, t.L, t.R // 2, t.C), BF) for t in specs], {}, [sems, sems],
                 start, None, finish)


def _chip_sum_job(ts, landed):
    t0 = ts[0]
    assert t0.L == 1
    half = t0.R // 2
    nj = half // t0.rb

    def local(j, li):
        return jnp.clip(j - li * nj, 0, nj - 1)

    ins = []
    for li, t in enumerate(ts):
        s, land = landed[t.name]

        def own_map(j, kc_ref, li=li, t=t):
            a, cb = _slot_index(t, kc_ref[0])
            return (0, a, local(j, li), cb)

        ins.append((s, (None, None, t.rb, t.C), own_map))
        ins.append((land, (N_CHIPS - 1, None, t.rb, t.C), lambda j, kc_ref, li=li: (0, 0, local(j, li), 0)))

    def fn(j, kc_ref, in_refs, outs):
        for li in range(len(ts)):
            @pl.when(j // nj == li)
            def _():
                acc = in_refs[2 * li][...].astype(F32)
                for k in range(N_CHIPS - 1):
                    acc = acc + in_refs[2 * li + 1][k].astype(F32)
                outs[0][...] = acc

    return Job(len(ts) * nj, ins,
               [(_sds((len(ts), t0.R, t0.C)), (None, t0.rb, t0.C),
                 lambda j, kc_ref: (j // nj, kc_ref[1] * nj + j % nj, 0))], fn)


def _adamw_job(rb, w, g, m, v):
    n_layers, r, c = w.shape
    nb = r // rb
    block = (None, rb, c)
    index = lambda j, kc_ref: (j // nb, j % nb, 0)

    def fn(j, kc_ref, ins, outs):
        g_v = ins[1][...]
        outs[0][...] = g_v
        outs[1][...], outs[2][...], outs[3][...] = _adamw_math(ins[0][...], g_v, ins[2][...], ins[3][...])

    return Job(n_layers * nb, [(a, block, index) for a in (w, g, m, v)],
               [(_sds(w.shape), block, index)] * 4, fn)


def _chip_sum_fused_job(ts, fused, by_cols):
    t0 = ts[0]
    own0 = fused[t0.name][0]
    if by_cols:
        rows, cols = own0.shape
    else:
        nb, rows, bw = own0.shape
        cols = nb * bw
    nj = rows // t0.rb

    def local(j, li):
        return jnp.clip(j - li * nj, 0, nj - 1)

    ins = []
    for li, t in enumerate(ts):
        own, land = fused[t.name]
        if by_cols:
            ins.append((own, (t.rb, cols), lambda j, kc_ref, li=li: (local(j, li), 0)))
            ins.append((land, (N_CHIPS - 1, t.rb, cols), lambda j, kc_ref, li=li: (0, local(j, li), 0)))
        else:
            ins.append((own, (nb, t.rb, bw), lambda j, kc_ref, li=li: (0, local(j, li), 0)))
            ins.append((land, (N_CHIPS - 1, nb, t.rb, bw), lambda j, kc_ref, li=li: (0, 0, local(j, li), 0)))

    def fn(j, kc_ref, in_refs, outs):
        for li in range(len(ts)):
            @pl.when(j // nj == li)
            def _():
                acc = in_refs[2 * li][...].astype(F32)
                for k in range(N_CHIPS - 1):
                    acc = acc + in_refs[2 * li + 1][k].astype(F32)
                outs[0][...] = acc if by_cols else jnp.concatenate([acc[b] for b in range(nb)], axis=1)

    def out_map(j, kc_ref):
        return (j // nj, j % nj, kc_ref[1]) if by_cols else (j // nj, kc_ref[1] * nj + j % nj, 0)

    return Job(len(ts) * nj, ins, [(_sds((len(ts), t0.R, t0.C)), (None, t0.rb, cols), out_map)], fn)


def _share_rider(halves, by_cols):
    nt = len(halves)

    def copy(outs, sems, ti, core, sibling):
        axis = 2 if by_cols[ti] else 1
        half = halves[ti].shape[axis] // 2
        piece = pl.ds(pl.multiple_of(core * half, 128 if by_cols[ti] else 8), half)
        part = outs[ti].at[:, :, piece] if by_cols[ti] else outs[ti].at[:, piece, :]
        return pltpu.make_async_remote_copy(part, part, sems[0].at[ti], sems[1].at[ti],
                                            device_id=sibling, device_id_type=MESH)

    def start(ins, outs, sems):
        x, y, c, _ = _mesh_position()
        for ti in range(nt):
            copy(outs, sems, ti, c, (x, y, 1 - c)).start()

    def finish(ins, outs, sems):
        x, y, c, _ = _mesh_position()
        for ti in range(nt):
            copy(outs, sems, ti, 1 - c, (x, y, 1 - c)).wait_recv()
        for ti in range(nt):
            copy(outs, sems, ti, c, (x, y, 1 - c)).wait_send()

    sems = pltpu.SemaphoreType.DMA((nt,))
    return Rider(list(halves), [_sds(a.shape, a.dtype) for a in halves], {i: i for i in range(nt)}, [sems, sems],
                 start, None, finish)


def _both(r1, r2):
    assert r1.mid is None and r2.mid is None
    ni, no, ns = len(r1.arrays), len(r1.out_shapes), len(r1.scratch)

    def split(fn1, fn2):
        def run(ins, outs, scr):
            fn1(ins[:ni], outs[:no], scr[:ns])
            fn2(ins[ni:], outs[no:], scr[ns:])
        return run

    aliases = dict(r1.aliases)
    aliases.update({ni + a: no + b for a, b in r2.aliases.items()})
    return Rider(r1.arrays + r2.arrays, r1.out_shapes + r2.out_shapes, aliases, r1.scratch + r2.scratch,
                 split(r1.start, r2.start), None, split(r1.finish, r2.finish))


def _adamw_math(w, g, m, v):
    m = B1 * m + (1.0 - B1) * g
    v = B2 * v + (1.0 - B2) * (g * g)
    delta = -LR * ((m / BC1) / (jnp.sqrt(v / BC2) + AEPS) + WD * w)
    return delta, m, v


def _adamw(name, rb, w, g, m, v):
    n_layers, r, c = w.shape

    def body(w_ref, g_ref, m_ref, v_ref, go_ref, d_ref, nm_ref, nv_ref):
        g_v = g_ref[...]
        go_ref[...] = g_v
        d_ref[...], nm_ref[...], nv_ref[...] = _adamw_math(w_ref[...], g_v, m_ref[...], v_ref[...])

    spec = pl.BlockSpec((None, rb, c), lambda l, j: (l, j, 0))
    return pl.pallas_call(
        body, name=f"adamw_{name}", grid=(n_layers, r // rb),
        in_specs=[spec] * 4, out_specs=[spec] * 4, out_shape=[_sds(w.shape)] * 4,
        compiler_params=_params(2),
    )(w, g, m, v)


GAIN_ROWS = {"pre_mix_g": 0, "post_mix_g": 2, "pre_ffn_g": 4, "post_ffn_g": 6, "ple_g": 8, "ple_post_g": 10}
ROW_KV_G, ROW_POOL_SCALE, ROW_SINKS, ROW_LOSS, PACK_ROWS = 12, 13, 14, 15, 16
SMALL_NAMES = tuple(GAIN_ROWS) + ("kv_g", "pool_scale", "sinks")


def _small_all_reduce(rows, dpool, rider=None):
    ng, pr = len(WINDOWS), POOL_G // N_CHIPS

    def body(*refs):
        row_refs = refs[:PACK_ROWS]
        dpool_ref, tot_ref, gpool_ref, pack, land, pland, send, recv, psend, precv = refs[PACK_ROWS:]
        x, y, c, _ = _mesh_position()
        me = 4 * x + 2 * y + c
        for r in range(PACK_ROWS):
            pack[r:r + 1, :] = row_refs[r][...]

        def shard_of(k):
            return dpool_ref.at[:, pl.ds(pl.multiple_of(k * pr, pr), pr), :]

        cps = []
        for j in range(1, N_DEV):
            px, py, pc = x ^ (j >> 2), y ^ ((j >> 1) & 1), c ^ (j & 1)
            cps.append(pltpu.make_async_remote_copy(pack, land.at[me], send.at[j], recv.at[j],
                                                    device_id=(px, py, pc), device_id_type=MESH))
            cps.append(pltpu.make_async_remote_copy(shard_of(2 * px + py), pland.at[me], psend.at[j], precv.at[j],
                                                    device_id=(px, py, pc), device_id_type=MESH))
        for cp in cps:
            cp.start()
        land[me] = pack[...]
        pland[me] = dpool_ref[:, pl.ds(pl.multiple_of((2 * x + y) * pr, pr), pr), :]
        for j in range(1, N_DEV):
            pltpu.make_async_remote_copy(pack, land.at[me ^ j], send.at[j], recv.at[j],
                                         device_id=(x, y, c), device_id_type=MESH).wait_recv()
            pltpu.make_async_remote_copy(shard_of(0), pland.at[me ^ j], psend.at[j], precv.at[j],
                                         device_id=(x, y, c), device_id_type=MESH).wait_recv()
        for cp in cps:
            cp.wait_send()
        tot = land[0]
        gp = pland[0].astype(F32)
        for d in range(1, N_DEV):
            tot = tot + land[d]
            gp = gp + pland[d].astype(F32)
        tot_ref[...] = tot
        gpool_ref[...] = gp

    sems = pltpu.SemaphoreType.DMA((N_DEV,))
    return _call(
        body, name="small_all_reduce", grid=(1,),
        in_specs=[VSPEC] * (PACK_ROWS + 1), out_specs=[VSPEC, VSPEC],
        out_shape=[_sds((PACK_ROWS, D)), _sds((ng, pr, POOL_G))],
        scratch_shapes=[pltpu.VMEM((PACK_ROWS, D), F32), pltpu.VMEM((N_DEV, PACK_ROWS, D), F32),
                        pltpu.VMEM((N_DEV, ng, pr, POOL_G), BF), sems, sems, sems, sems],
        args=[*rows, dpool], rider=rider)


def _small_adamw(tot, kc, small_w, small_m, small_v):
    names = SMALL_NAMES
    n = len(names)

    def body(*refs):
        tot_ref, kc_ref = refs[0], refs[1]
        w_refs = dict(zip(names, refs[2:2 + n]))
        m_refs = dict(zip(names, refs[2 + n:2 + 2 * n]))
        v_refs = dict(zip(names, refs[2 + 2 * n:2 + 3 * n]))
        loss_ref = refs[2 + 3 * n]
        out_refs = {nm: refs[3 + 3 * n + 4 * k: 7 + 3 * n + 4 * k] for k, nm in enumerate(names)}
        tot = tot_ref[...]
        loss_ref[...] = 0.5 * jnp.sum(tot[ROW_LOSS:ROW_LOSS + 1, :], axis=-1, keepdims=True) * (1.0 / D)

        def update(nm, g):
            g_ref, d_ref, nm_ref, nv_ref = out_refs[nm]
            g_ref[...] = g
            d_ref[...], nm_ref[...], nv_ref[...] = _adamw_math(w_refs[nm][...], g, m_refs[nm][...], v_refs[nm][...])

        for nm, r in GAIN_ROWS.items():
            update(nm, tot[r:r + 2, :])
        update("kv_g", tot[ROW_KV_G:ROW_KV_G + 1, :])
        k = kc_ref[0]
        width = D // N_CHIPS
        g_scale = jnp.zeros((1, width), F32)
        for kk in range(N_CHIPS):
            g_scale = g_scale + jnp.where(k == kk, tot[ROW_POOL_SCALE:ROW_POOL_SCALE + 1, kk * width:(kk + 1) * width], 0.0)
        update("pool_scale", g_scale)
        update("sinks", tot[ROW_SINKS:ROW_SINKS + 1, 0:N_HEADS])

    ins = [tot, kc] + [small_w[nm] for nm in names] + [small_m[nm] for nm in names] + [small_v[nm] for nm in names]
    out_shape = [_sds((1, 1))]
    for nm in names:
        out_shape += [_sds(small_w[nm].shape)] * 4
    outs = pl.pallas_call(
        body, name="small_adamw",
        in_specs=[VSPEC, SSPEC] + [VSPEC] * (3 * n), out_specs=[VSPEC] * len(out_shape), out_shape=out_shape,
        compiler_params=_params(),
    )(*ins)
    return outs[0], {nm: outs[1 + 4 * k: 5 + 4 * k] for k, nm in enumerate(names)}


def _compute_layout(t, full):
    if t.src == "w_gu":
        return full.reshape(2, D, FF)
    if t.src == "pool_w":
        return full.reshape(len(WINDOWS), POOL_G, POOL_G)
    if t.src == "pool_scale":
        return full.reshape(1, D)
    return full.reshape(t.A * t.R, _ncb(t) * t.C)


def kernel(x, p, pre_mix_g, post_mix_g, pre_ffn_g, post_ffn_g, pool_w, pool_scale, kv_g, w_kv, w_q, sinks, w_o, w_gu, w_down, ple_g, w_ple_gate, w_ple_proj, ple_post_g, loss_target, m_pre_mix_g, m_post_mix_g, m_pre_ffn_g, m_post_ffn_g, m_pool_w, m_pool_scale, m_kv_g, m_w_kv, m_w_q, m_sinks, m_w_o, m_w_gu, m_w_down, m_ple_g, m_w_ple_gate, m_w_ple_proj, m_ple_post_g, v_pre_mix_g, v_post_mix_g, v_pre_ffn_g, v_post_ffn_g, v_pool_w, v_pool_scale, v_kv_g, v_w_kv, v_w_q, v_sinks, v_w_o, v_w_gu, v_w_down, v_ple_g, v_w_ple_gate, v_w_ple_proj, v_ple_post_g):
    weights = dict(pre_mix_g=pre_mix_g, post_mix_g=post_mix_g, pre_ffn_g=pre_ffn_g, post_ffn_g=post_ffn_g,
                   pool_w=pool_w, pool_scale=pool_scale, kv_g=kv_g, w_kv=w_kv, w_q=w_q, sinks=sinks, w_o=w_o,
                   w_gu=w_gu, w_down=w_down, ple_g=ple_g, w_ple_gate=w_ple_gate, w_ple_proj=w_ple_proj,
                   ple_post_g=ple_post_g)
    m_in = dict(pre_mix_g=m_pre_mix_g, post_mix_g=m_post_mix_g, pre_ffn_g=m_pre_ffn_g, post_ffn_g=m_post_ffn_g,
                pool_w=m_pool_w, pool_scale=m_pool_scale, kv_g=m_kv_g, w_kv=m_w_kv, w_q=m_w_q, sinks=m_sinks,
                w_o=m_w_o, w_gu=m_w_gu, w_down=m_w_down, ple_g=m_ple_g, w_ple_gate=m_w_ple_gate,
                w_ple_proj=m_w_ple_proj, ple_post_g=m_ple_post_g)
    v_in = dict(pre_mix_g=v_pre_mix_g, post_mix_g=v_post_mix_g, pre_ffn_g=v_pre_ffn_g, post_ffn_g=v_post_ffn_g,
                pool_w=v_pool_w, pool_scale=v_pool_scale, kv_g=v_kv_g, w_kv=v_w_kv, w_q=v_w_q, sinks=v_sinks,
                w_o=v_w_o, w_gu=v_w_gu, w_down=v_w_down, ple_g=v_ple_g, w_ple_gate=v_w_ple_gate,
                w_ple_proj=v_w_ple_proj, ple_post_g=v_ple_post_g)
    order = ["pre_mix_g", "post_mix_g", "pre_ffn_g", "post_ffn_g", "pool_w", "pool_scale", "kv_g", "w_kv", "w_q",
             "sinks", "w_o", "w_gu", "w_down", "ple_g", "w_ple_gate", "w_ple_proj", "ple_post_g"]

    kc = jnp.stack([2 * lax.axis_index("x") + lax.axis_index("y"), lax.axis_index("c")]).astype(jnp.int32)
    s_len = x.shape[1]
    x2d = x.reshape(s_len, D)
    p3d = p.reshape(2, s_len, PLE)
    target = loss_target.reshape(s_len, D)
    kv_g2d = kv_g.reshape(1, D)
    gains = {nm: weights[nm] for nm in GAIN_ROWS}

    def shard_view(src, a):
        t = next(t for t in BIGS.values() if t.src == src)
        return a.reshape(-1, t.R, t.C)

    first, second = ["pool_w", "pool_scale"], ["w_gu0", "w_down0"]
    rest = [nm for nm in BIGS if nm not in first + second]
    specs = dict(BIGS, pool_scale=POOL_SCALE)
    placed = {}

    def place_job(nm):
        if nm == "pool_scale":
            return _place_job(POOL_SCALE, pool_scale.reshape(1, 1, D // N_CHIPS), F32)
        return _place_job(BIGS[nm], shard_view(BIGS[nm].src, weights[BIGS[nm].src]))

    def gather(names, rows=None):
        rows = rows or {}
        parts = [(specs[nm],) + tuple(rows.get(nm, (0, specs[nm].R))) for nm in names]
        return _gather_rider(parts, [placed[nm] for nm in names])

    def take(names, results):
        for nm, a in zip(names, results):
            placed[nm] = a

    def weight(nm):
        return _compute_layout(specs[nm], placed[nm])

    take(first, [r[0] for r in _multi_call("place_pool", [place_job(nm) for nm in first], kc)])
    cast, got = _multi_call("place_ffn0", [place_job(nm) for nm in second], kc, rider=gather(first))
    take(second, [r[0] for r in cast])
    take(first, got)
    jobs = [place_job(nm) for nm in rest]
    jobs.append(_mixa_fwd_job(x2d, gains["pre_mix_g"], weight("pool_w"), weight("pool_scale"), gains["post_mix_g"]))
    results, got = _multi_call("cast_and_mixa_fwd", jobs, kc, rider=gather(second))
    take(rest, [r[0] for r in results[:-1]])
    take(second, got)
    y0, x1 = results[-1]

    ride = ["w_ple_gate0", "w_ple_proj0", "w_q", "w_kv", "w_o", "w_gu1"]
    (f0, x2, g0, u0), got = _ffn_fwd(0, x1, gains["pre_ffn_g"], weight("w_gu0"), weight("w_down0"), gains["post_ffn_g"],
                             rider=gather(ride, {"w_gu1": (0, 320)}))
    take(ride, got)

    ride = ["w_ple_gate1", "w_ple_proj1", "w_gu1"]
    (z0, pe0, x3, q, kv), got = _ple_fwd(
        0, x2, p3d, gains["ple_g"], weight("w_ple_gate0"), weight("w_ple_proj0"), gains["ple_post_g"],
        qkv=(gains["pre_mix_g"], kv_g2d, weight("w_q"), weight("w_kv")),
        rider=gather(ride, {"w_gu1": (320, 704)}))
    take(ride, got)

    ride = ["w_down1", "w_gu1"]
    (attn, y1, x4), got = _attn_fwd(q, kv, sinks, x3, weight("w_o"), gains["post_mix_g"],
                                    rider=gather(ride, {"w_gu1": (704, D)}))
    take(ride, got)

    (f1, x5, g1, u1), _ = _ffn_fwd(1, x4, gains["pre_ffn_g"], weight("w_gu1"), weight("w_down1"), gains["post_ffn_g"])
    (z1, pe1, dx6, loss_row), _ = _ple_fwd(1, x5, p3d, gains["ple_g"], weight("w_ple_gate1"), weight("w_ple_proj1"),
                                           gains["ple_post_g"], target=target)

    local = {}
    landed = {}
    fused = {}

    def pair_stage(tag, names):
        ts = [BIGS[nm] for nm in names]
        gs = [local[nm].reshape(_full_shape(t)) for nm, t in zip(names, ts)]
        lands = _pair_exchange(f"grads_pair_exchange_{tag}", ts, gs)
        jobs = [_pair_sum_job(t, g, l) for t, g, l in zip(ts, gs, lands)]
        return [r[0] for r in _multi_call(f"pair_sum_{tag}", jobs, kc)]

    def scatter(names, sums):
        return _scatter_rider([BIGS[nm] for nm in names], sums)

    def keep(names, sums, got):
        for nm, s, l in zip(names, sums, got):
            landed[nm] = (s, l)

    (dx5, local["w_ple_gate1"], local["w_ple_proj1"], d_ple1, d_plepost1), _ = _ple_bwd(
        1, dx6, x5, z1, pe1, p3d, gains["ple_g"], weight("w_ple_gate1"), gains["ple_post_g"])

    group_a = ["w_ple_gate1", "w_ple_proj1"]
    sums_a = pair_stage("a", group_a)
    (dx4, d_preffn1, d_postffn1, *scattered), _ = _ffn_bwd(
        1, dx5, x4, f1, g1, u1, gains["pre_ffn_g"], weight("w_gu1"), weight("w_down1"), gains["post_ffn_g"], kc)
    fused["w_gu1"], fused["w_down1"] = scattered[0:2], scattered[2:4]

    (dq, dkv, local["w_o"], d_postmix1, d_sinks), got = _attn_bwd(
        dx4, y1, attn, q, kv, sinks, weight("w_o"), gains["post_mix_g"], rider=scatter(group_a, sums_a))
    keep(group_a, sums_a, got)
    dx3, local["w_q"], local["w_kv"], d_premix1, d_kvg = _qkv_bwd(
        dq, dkv, x3, dx4, gains["pre_mix_g"], kv_g2d, weight("w_q"), weight("w_kv"))

    group_b = ["w_o", "w_q", "w_kv"]
    sums_b = pair_stage("b", group_b)
    (dx2, local["w_ple_gate0"], local["w_ple_proj0"], d_ple0, d_plepost0), got = _ple_bwd(
        0, dx3, x2, z0, pe0, p3d, gains["ple_g"], weight("w_ple_gate0"), gains["ple_post_g"],
        rider=scatter(group_b, sums_b))
    keep(group_b, sums_b, got)

    group_c = ["w_ple_gate0", "w_ple_proj0"]
    sums_c = pair_stage("c", group_c)
    (dx1, d_preffn0, d_postffn0, *scattered), _ = _ffn_bwd(
        0, dx2, x1, f0, g0, u0, gains["pre_ffn_g"], weight("w_gu0"), weight("w_down0"), gains["post_ffn_g"], kc)
    fused["w_gu0"], fused["w_down0"] = scattered[0:2], scattered[2:4]

    (dx0, d_pool, d_scale, d_postmix0, d_premix0), _ = _mixa_bwd(
        dx1, x2d, y0, gains["pre_mix_g"], weight("pool_w"), weight("pool_scale"), gains["post_mix_g"])

    rows = [d_premix0, d_premix1, d_postmix0, d_postmix1, d_preffn0, d_preffn1, d_postffn0, d_postffn1,
            d_ple0, d_ple1, d_plepost0, d_plepost1, d_kvg, d_scale, d_sinks, loss_row]
    as2d = lambda a: a.reshape(1, D) if a.ndim == 1 else a
    layers_of = lambda src: [t for t in BIGS.values() if t.src == src]
    own_scatter = ["w_gu", "w_down"]
    early = own_scatter + ["w_q", "w_o", "w_kv"]
    late = ["w_ple_gate", "w_ple_proj"]
    by_cols = lambda srcs: [src == "w_down" for src in srcs]
    jobs = [_chip_sum_fused_job(layers_of(src), fused, by_cols=src == "w_down") for src in own_scatter]
    jobs += [_chip_sum_job(layers_of(src), landed) for src in early if src not in own_scatter]
    halves = [r[0] for r in _multi_call("chip_sum_early", jobs, kc)]
    (tot, g_pool), got = _small_all_reduce(
        rows, d_pool, rider=_both(scatter(group_c, sums_c), _share_rider(halves, by_cols(early))))
    keep(group_c, sums_c, got[:len(group_c)])
    full_grads = dict(zip(early, got[len(group_c):]))
    loss, small = _small_adamw(tot, kc, {nm: as2d(weights[nm]) for nm in SMALL_NAMES},
                               {nm: as2d(m_in[nm]) for nm in SMALL_NAMES},
                               {nm: as2d(v_in[nm]) for nm in SMALL_NAMES})

    halves = [r[0] for r in _multi_call("chip_sum_late", [_chip_sum_job(layers_of(src), landed) for src in late], kc)]
    full_grads.update(zip(late, _run("grads_pair_share", _share_rider(halves, by_cols(late)))))
    full_grads["pool_w"] = g_pool
    others = [src for src in BIG_SOURCES if src not in own_scatter and src != "pool_w"]

    def adam_args(src):
        return (layers_of(src)[0].rb, shard_view(src, weights[src]), full_grads[src],
                shard_view(src, m_in[src]), shard_view(src, v_in[src]))

    out = {"grad": {}, "delta": {}, "new_m": {}, "new_v": {}}
    results = {src: _adamw(src, *adam_args(src)) for src in own_scatter}
    rest_srcs = others + ["pool_w"]
    results.update(zip(rest_srcs, _multi_call("adamw_rest", [_adamw_job(*adam_args(src)) for src in rest_srcs], kc)))
    for src in BIG_SOURCES:
        shape = weights[src].shape
        for kind, a in zip(("grad", "delta", "new_m", "new_v"), results[src]):
            out[kind][src] = a.reshape(shape)
    for nm in SMALL_NAMES:
        shape = weights[nm].shape
        for kind, a in zip(("grad", "delta", "new_m", "new_v"), small[nm]):
            out[kind][nm] = a.reshape(shape)

    return (loss.reshape(()), dx0.reshape(x.shape),
            *[out["grad"][nm] for nm in order], *[out["delta"][nm] for nm in order],
            *[out["new_m"][nm] for nm in order], *[out["new_v"][nm] for nm in order])
```

```python
import collections

import jax
import jax.numpy as jnp
from jax import lax
from jax.experimental import pallas as pl
from jax.experimental.pallas import tpu as pltpu

D = 1024
FF = 2816
N_HEADS = 16
HEAD_DIM = 64
N_KV_HEADS = 4
GQA = N_HEADS // N_KV_HEADS
KVD = N_KV_HEADS * HEAD_DIM
PLE = 256
BLK = 128
WINDOWS = (2, 4, 8, 16)
POOL_G = 256
HALO = 16
EPS = 1e-6
NEG_INF = -1e30
ATT_SCALE = HEAD_DIM ** -0.5
SLOPES = tuple(2.0 ** (-8.0 * (h + 1) / N_HEADS) for h in range(N_HEADS))
N_CHIPS = 4
N_DEV = 8

LR, B1, B2, AEPS, WD, STEP = 0.001, 0.9, 0.999, 1e-08, 0.01, 10
BC1 = 1.0 - B1 ** STEP
BC2 = 1.0 - B2 ** STEP

BF = jnp.bfloat16
F32 = jnp.float32
MESH = pl.DeviceIdType.MESH
VMEM_LIMIT_V7X = 58 * 1024 * 1024
TM = 256
TM_FFN_BWD = 512
FF_CHUNK = 256
FF_HALF = FF // 2

VSPEC = pl.BlockSpec(memory_space=pltpu.VMEM)
SSPEC = pl.BlockSpec(memory_space=pltpu.SMEM)
ANYSPEC = pl.BlockSpec(memory_space=pl.ANY)


def _params(n_grid=0):
    sem = ("arbitrary",) * n_grid if n_grid else None
    return pltpu.CompilerParams(dimension_semantics=sem, vmem_limit_bytes=VMEM_LIMIT_V7X)


def _sds(shape, dtype=F32):
    return jax.ShapeDtypeStruct(tuple(shape), dtype)


Rider = collections.namedtuple("Rider", "arrays out_shapes aliases scratch start mid finish")
MID_NUM, MID_DEN = 5, 8


def _call(body, *, name, grid, in_specs, out_specs, out_shape, args, scratch_shapes=(), rider=None, prefetch=None):
    ni, no, ns = len(in_specs), len(out_specs), len(scratch_shapes)
    npre = 0 if prefetch is None else 1
    pre = [] if prefetch is None else [prefetch]
    if rider is None:
        rider = Rider([], [], {}, [], None, None, None)
    ri, ro = len(rider.arrays), len(rider.out_shapes)

    def full(*refs):
        pre_refs, refs = refs[:npre], refs[npre:]
        ins, refs = refs[:ni], refs[ni:]
        rins, refs = refs[:ri], refs[ri:]
        outs, refs = refs[:no], refs[no:]
        routs, refs = refs[:ro], refs[ro:]
        scr, rscr = refs[:ns], refs[ns:]
        ids = [pl.program_id(a) for a in range(len(grid))]
        first = ids[0] == 0
        last = ids[0] == grid[0] - 1
        for a in range(1, len(grid)):
            first = first & (ids[a] == 0)
            last = last & (ids[a] == grid[a] - 1)

        if rider.start is not None:
            @pl.when(first)
            def _():
                rider.start(rins, routs, rscr)

        if rider.mid is not None:
            assert len(grid) == 1

            @pl.when(ids[0] == (grid[0] * MID_NUM) // MID_DEN)
            def _():
                rider.mid(rins, routs, rscr)

        body(*pre_refs, *ins, *outs, *scr)

        if rider.finish is not None:
            @pl.when(last)
            def _():
                rider.finish(rins, routs, rscr)

    outs = pl.pallas_call(
        full, name=name,
        grid_spec=pltpu.PrefetchScalarGridSpec(
            num_scalar_prefetch=npre, grid=grid,
            in_specs=list(in_specs) + [ANYSPEC] * ri, out_specs=list(out_specs) + [ANYSPEC] * ro,
            scratch_shapes=list(scratch_shapes) + list(rider.scratch)),
        out_shape=list(out_shape) + list(rider.out_shapes),
        input_output_aliases={npre + ni + a: no + b for a, b in rider.aliases.items()},
        compiler_params=_params(len(grid)))(*pre, *args, *rider.arrays)
    return list(outs[:no]), list(outs[no:])


def _run(name, rider):
    ri = len(rider.arrays)

    def body(*refs):
        rins, routs, rscr = refs[:ri], refs[ri:ri + len(rider.out_shapes)], refs[ri + len(rider.out_shapes):]
        rider.start(rins, routs, rscr)
        if rider.mid is not None:
            rider.mid(rins, routs, rscr)
        rider.finish(rins, routs, rscr)

    return pl.pallas_call(
        body, name=name, in_specs=[ANYSPEC] * ri, out_specs=[ANYSPEC] * len(rider.out_shapes),
        out_shape=list(rider.out_shapes), scratch_shapes=list(rider.scratch),
        input_output_aliases=dict(rider.aliases), compiler_params=_params())(*rider.arrays)


Job = collections.namedtuple("Job", "steps ins outs fn")


def _multi_call(name, jobs, kc, rider=None):
    n = max(job.steps for job in jobs)

    def clamped(index, steps):
        return lambda s, kc_ref: index(jnp.minimum(s, steps - 1), kc_ref)

    in_specs, out_specs, out_shape, args = [], [], [], []
    for job in jobs:
        for arr, block, index, *single in job.ins:
            mode = dict(pipeline_mode=pl.Buffered(1)) if single and single[0] else {}
            in_specs.append(pl.BlockSpec(block, clamped(index, job.steps), **mode))
            args.append(arr)
        for sds, block, index in job.outs:
            out_specs.append(pl.BlockSpec(block, clamped(index, job.steps)))
            out_shape.append(sds)
    n_in = len(args)

    def body(kc_ref, *refs):
        s = pl.program_id(0)
        i0, o0 = 0, n_in
        for job in jobs:
            ins, outs = refs[i0:i0 + len(job.ins)], refs[o0:o0 + len(job.outs)]
            i0, o0 = i0 + len(job.ins), o0 + len(job.outs)

            @pl.when(s < job.steps)
            def _():
                job.fn(s, kc_ref, ins, outs)

    outs, routs = _call(body, name=name, grid=(n,), in_specs=in_specs, out_specs=out_specs, out_shape=out_shape,
                        args=args, prefetch=kc, rider=rider)
    res, o0 = [], 0
    for job in jobs:
        res.append(outs[o0:o0 + len(job.outs)])
        o0 += len(job.outs)
    return res if rider is None else (res, routs)


def _rms_fwd(x, g):
    r = lax.rsqrt(jnp.mean(x * x, axis=-1, keepdims=True) + EPS)
    return x * r * g


def _rms_bwd(x, g, dy):
    r = lax.rsqrt(jnp.mean(x * x, axis=-1, keepdims=True) + EPS)
    xn = x * r
    dxn = dy * g
    dx = r * (dxn - xn * jnp.mean(dxn * xn, axis=-1, keepdims=True))
    return dx, dy * xn


def _rowsum(a):
    return jnp.sum(a, axis=0, keepdims=True)


def _sigmoid(z):
    return 1.0 / (1.0 + jnp.exp(-z))


def _dot(a, b):
    return jnp.dot(a, b, preferred_element_type=F32)


def _dot_nt(a, b):
    return lax.dot_general(a, b, (((1,), (1,)), ((), ())), preferred_element_type=F32)


def _dot_tn(a, b):
    return lax.dot_general(a, b, (((0,), (0,)), ((), ())), preferred_element_type=F32)


def _row_spec(tm, width=D):
    return pl.BlockSpec((tm, width), lambda i: (i, 0))


def _const_spec(shape):
    zeros = (0,) * len(shape)
    return pl.BlockSpec(tuple(shape), lambda *_: zeros)


def _pool_delta(he, pos):
    out = []
    for gi, w in enumerate(WINDOWS):
        hg = he[:, gi * POOL_G:(gi + 1) * POOL_G]
        s = hg
        k = 1
        while k < w:
            s = s + pltpu.roll(s, k, 0)
            k *= 2
        cnt = jnp.maximum(jnp.minimum(pos + 1, w), 1).astype(F32)
        out.append(s / cnt - hg)
    return out


def _load_with_halo_before(x_ref, i, tm):
    r0 = pl.multiple_of(i * tm, tm)
    hs = pl.multiple_of(jnp.maximum(i * tm - HALO, 0), 8)
    xh = jnp.where(i > 0, x_ref[pl.ds(hs, HALO), :], 0.0)
    xt = x_ref[pl.ds(r0, tm), :]
    return xt, jnp.concatenate([xh, xt], axis=0)


def _mixa_fwd_job(x, pre_g, pool_w, pool_scale, post_g):
    s_len = x.shape[0]

    def fn(i, kc_ref, ins, outs):
        x_ref, pg_ref, w_ref, sc_ref, qg_ref = ins
        y_ref, x1_ref = outs
        xt, xe = _load_with_halo_before(x_ref, i, TM)
        he = _rms_fwd(xe, pg_ref[0:1, :])
        pos = i * TM - HALO + lax.broadcasted_iota(jnp.int32, (TM + HALO, 1), 0)
        ds = _pool_delta(he, pos)
        ys = [_dot(ds[gi][HALO:, :].astype(BF), w_ref[gi]) for gi in range(len(WINDOWS))]
        y = jnp.concatenate(ys, axis=1) * sc_ref[...]
        y_ref[...] = y
        x1_ref[...] = xt + _rms_fwd(y, qg_ref[0:1, :])

    def whole(a):
        zeros = (0,) * a.ndim
        return (a, a.shape, lambda j, kc_ref: zeros, True)

    rows = lambda j, kc_ref: (j, 0)
    return Job(s_len // TM, [whole(a) for a in (x, pre_g, pool_w, pool_scale, post_g)],
               [(_sds((s_len, D)), (TM, D), rows), (_sds((s_len, D)), (TM, D), rows)], fn)


def _mixa_bwd(dx1, x, y, pre_g, pool_w, pool_scale, post_g, rider=None):
    s_len = x.shape[0]
    n = s_len // TM
    ng = len(WINDOWS)

    def body(dx_ref, x_ref, y_ref, pg_ref, w_ref, sc_ref, qg_ref,
             dx0_ref, dw_ref, dsc_ref, dqg_ref, dpg_ref, wacc):
        i = pl.program_id(0)

        @pl.when(i == 0)
        def _():
            wacc[...] = jnp.zeros_like(wacc)
            dsc_ref[...] = jnp.zeros_like(dsc_ref)
            dqg_ref[...] = jnp.zeros_like(dqg_ref)
            dpg_ref[...] = jnp.zeros_like(dpg_ref)

        r0 = pl.multiple_of(i * TM, TM)
        xt, xe = _load_with_halo_before(x_ref, i, TM)
        he = _rms_fwd(xe, pg_ref[0:1, :])
        pos_b = i * TM - HALO + lax.broadcasted_iota(jnp.int32, (TM + HALO, 1), 0)
        ds = _pool_delta(he, pos_b)

        last = i == n - 1
        a0 = pl.multiple_of(jnp.minimum(i * TM + TM, s_len - HALO), 8)
        ye = jnp.concatenate([y_ref[pl.ds(r0, TM), :], y_ref[pl.ds(a0, HALO), :]], axis=0)
        dt = dx_ref[pl.ds(r0, TM), :]
        de = jnp.concatenate([dt, jnp.where(last, 0.0, dx_ref[pl.ds(a0, HALO), :])], axis=0)
        dye, prod = _rms_bwd(ye, qg_ref[0:1, :], de)
        dqg_ref[...] += _rowsum(prod[:TM, :])
        dys = dye * sc_ref[...]
        pos_a = i * TM + lax.broadcasted_iota(jnp.int32, (TM + HALO, 1), 0)

        dhs, dscs = [], []
        for gi, w in enumerate(WINDOWS):
            sl = slice(gi * POOL_G, (gi + 1) * POOL_G)
            wg = w_ref[gi]
            dys_g = dys[:, sl].astype(BF)
            d_g = ds[gi][HALO:, :].astype(BF)
            ypre = _dot(d_g, wg)
            dscs.append(_rowsum(dye[:TM, sl] * ypre))
            wacc[gi] += _dot_tn(d_g, dys_g[:TM, :])
            dd = _dot_nt(dys_g, wg)
            cnt = jnp.minimum(pos_a + 1, w).astype(F32)
            a = dd / cnt
            k = 1
            while k < w:
                a = a + pltpu.roll(a, TM + HALO - k, 0)
                k *= 2
            dhs.append(a[:TM, :] - dd[:TM, :])
        dsc_ref[...] += jnp.concatenate(dscs, axis=1)
        dh = jnp.concatenate(dhs, axis=1)
        dxp, prod2 = _rms_bwd(xt, pg_ref[0:1, :], dh)
        dpg_ref[...] += _rowsum(prod2)
        dx0_ref[...] = dt + dxp

        @pl.when(last)
        def _():
            dw_ref[...] = wacc[...].astype(BF)

    return _call(
        body, name="mixa_bwd", grid=(n,), in_specs=[VSPEC] * 7,
        out_specs=[_row_spec(TM), _const_spec((ng, POOL_G, POOL_G)), _const_spec((1, D)),
                   _const_spec((1, D)), _const_spec((1, D))],
        out_shape=[_sds((s_len, D)), _sds((ng, POOL_G, POOL_G), BF), _sds((1, D)), _sds((1, D)), _sds((1, D))],
        scratch_shapes=[pltpu.VMEM((ng, POOL_G, POOL_G), F32)],
        args=[dx1, x, y, pre_g, pool_w, pool_scale, post_g], rider=rider)


def _ffn_fwd(layer, x1, pre_g, wgu, wd, post_g, rider=None):
    s_len = x1.shape[0]

    def body(x_ref, pg_ref, wgu_ref, wd_ref, qg_ref, f_ref, x2_ref, g_ref, u_ref):
        x = x_ref[...]
        h = _rms_fwd(x, pg_ref[layer:layer + 1, :]).astype(BF)
        f = jnp.zeros((TM, D), F32)
        for c in range(FF // FF_HALF):
            cols = slice(c * FF_HALF, (c + 1) * FF_HALF)
            g = _dot(h, wgu_ref[0, :, cols])
            u = _dot(h, wgu_ref[1, :, cols])
            g_ref[:, cols] = g.astype(BF)
            u_ref[:, cols] = u.astype(BF)
            act = g * _sigmoid(g) * u
            f = f + _dot(act.astype(BF), wd_ref[cols, :])
        f_ref[...] = f
        x2_ref[...] = x + _rms_fwd(f, qg_ref[layer:layer + 1, :])

    return _call(body, name=f"ffn_fwd{layer}", grid=(s_len // TM,),
                 in_specs=[_row_spec(TM), VSPEC, VSPEC, VSPEC, VSPEC],
                 out_specs=[_row_spec(TM), _row_spec(TM), _row_spec(TM, FF), _row_spec(TM, FF)],
                 out_shape=[_sds((s_len, D)), _sds((s_len, D)), _sds((s_len, FF), BF), _sds((s_len, FF), BF)],
                 args=[x1, pre_g, wgu, wd, post_g], rider=rider)


GU_PIECE = 128
DN_PIECE = 64
DN_SLOT = FF // N_CHIPS
HALF_D = D // 2


def _ffn_bwd(layer, dx2, x1, f, g_pre, u_pre, pre_g, wgu, wd, post_g, kc, rider=None):
    s_len = x1.shape[0]
    tm = TM_FFN_BWD
    n = s_len // tm
    nc = FF // FF_CHUNK
    n_gu, n_dn = FF_CHUNK // GU_PIECE, FF_CHUNK // DN_PIECE
    n_pieces = 2 * n_gu + n_dn
    n_blk = FF_HALF // GU_PIECE

    def edge_rows(c, i, kc_ref):
        return (jnp.where((c == 0) | (c == nc - 1), i, n - 1), 0)

    def chunk_at(c, kc_ref):
        return (c + (((kc_ref[0] + 1) % N_CHIPS) * nc) // N_CHIPS) % nc

    def exchange(kc_ref, c, accg, accu, accd, own_gu_ref, land_gu_ref, own_dn_ref, land_dn_ref,
                 pl_gu, pl_dn, sib_gu, sib_dn, mine_gu, mine_dn, sum_gu, sum_dn,
                 psend, precv, ssend, lsem, rrecv):
        x, y, core = lax.axis_index("x"), lax.axis_index("y"), lax.axis_index("c")
        lower = core == 0

        def pair_copy(cc, part):
            p = cc % 2
            src, dst = ((sib_gu, pl_gu), (sib_dn, pl_dn))[part]
            return pltpu.make_async_remote_copy(src.at[p], dst.at[cc], psend.at[p, part], precv.at[cc, part],
                                                device_id=(x, y, 1 - core), device_id_type=MESH)

        def scatter(cc, wait):
            p = cc % 2
            hidden = chunk_at(cc, kc_ref) * FF_CHUNK

            assert n_gu == 2
            k0, k1 = hidden // FF_HALF, (hidden + GU_PIECE) // FF_HALF
            blk = (hidden - k0 * FF_HALF) // GU_PIECE
            for gu in range(2):
                @pl.when(k0 == k1)
                def _():
                    piece(p, wait, 2 * gu, sum_gu.at[p, gu], k0 + 2 * gu, 0, (pl.ds(blk, 2),))

                @pl.when(k0 != k1)
                def _():
                    piece(p, wait, 2 * gu, sum_gu.at[p, gu, 0], k0 + 2 * gu, 0, (blk,))
                    piece(p, wait, 2 * gu + 1, sum_gu.at[p, gu, 1], k1 + 2 * gu, 0, (0,))

            kd = hidden // DN_SLOT
            off = pl.multiple_of(hidden - kd * DN_SLOT, DN_PIECE)
            m = jnp.minimum((DN_SLOT - off) // DN_PIECE, n_dn)
            for mm in range(1, n_dn + 1):
                @pl.when(m == mm)
                def _():
                    rows = mm * DN_PIECE
                    piece(p, wait, 2 * n_gu, sum_dn.at[p, pl.ds(0, rows), :], kd, 1, (pl.ds(off, rows), slice(None)))
                    if mm < n_dn:
                        piece(p, wait, 2 * n_gu + 1, sum_dn.at[p, pl.ds(rows, FF_CHUNK - rows), :], kd + 1, 1,
                              (pl.ds(0, FF_CHUNK - rows), slice(None)))

        def piece(p, wait, pi, src, k, t, where):
            own_ref, land_ref = ((own_gu_ref, land_gu_ref), (own_dn_ref, land_dn_ref))[t]
            kx, ky = k // 2, k % 2
            fx, fy = (kx != x).astype(jnp.int32), (ky != y).astype(jnp.int32)
            local = (fx + fy) == 0
            j = jnp.maximum(fx + 2 * fy - 1, 0)

            @pl.when(local)
            def _():
                cp = pltpu.make_async_copy(src, own_ref.at[where], lsem.at[p, pi])
                if wait:
                    cp.wait()
                else:
                    cp.start()

            @pl.when(jnp.logical_not(local))
            def _():
                cp = pltpu.make_async_remote_copy(src, land_ref.at[(j,) + where], ssend.at[p, pi],
                                                  rrecv.at[t, j], device_id=(kx, ky, core), device_id_type=MESH)
                if wait:
                    cp.wait_send()
                else:
                    cp.start()

        def add_and_scatter(cc):
            p = cc % 2
            pair_copy(cc, 0).wait_recv()
            pair_copy(cc, 1).wait_recv()
            s_gu = (mine_gu[...] + pl_gu[cc].astype(F32)).astype(BF)
            for hc in range(n_gu):
                sum_gu[p, :, hc] = s_gu[:, :, hc * GU_PIECE:(hc + 1) * GU_PIECE]
            sum_dn[p] = (mine_dn[...] + pl_dn[cc].astype(F32)).astype(BF)
            scatter(cc, wait=False)

        @pl.when(c >= 1)
        def _():
            @pl.when(c >= 3)
            def _():
                scatter(c - 3, wait=True)
            add_and_scatter(c - 1)

        @pl.when(c >= 2)
        def _():
            pair_copy(c - 2, 0).wait_send()
            pair_copy(c - 2, 1).wait_send()

        p = c % 2
        my_rows = pl.ds(pl.multiple_of(core * HALF_D, HALF_D), HALF_D)
        sib_rows = pl.ds(pl.multiple_of((1 - core) * HALF_D, HALF_D), HALF_D)
        d_v = accd[...]
        sib_gu[p, 0] = accg[sib_rows, :].astype(BF)
        sib_gu[p, 1] = accu[sib_rows, :].astype(BF)
        sib_dn[p] = jnp.where(lower, d_v[:, HALF_D:], d_v[:, :HALF_D]).astype(BF)
        mine_gu[0] = accg[my_rows, :]
        mine_gu[1] = accu[my_rows, :]
        mine_dn[...] = jnp.where(lower, d_v[:, :HALF_D], d_v[:, HALF_D:])
        pair_copy(c, 0).start()
        pair_copy(c, 1).start()

        @pl.when(c == nc - 1)
        def _():
            scatter(nc - 3, wait=True)
            add_and_scatter(nc - 1)
            for cc in (nc - 2, nc - 1):
                pair_copy(cc, 0).wait_send()
                pair_copy(cc, 1).wait_send()
                scatter(cc, wait=True)
            for t, land_ref in enumerate((land_gu_ref, land_dn_ref)):
                for j in range(N_CHIPS - 1):
                    pltpu.make_async_remote_copy(land_ref.at[j], land_ref.at[j], ssend.at[0, 0], rrecv.at[t, j],
                                                 device_id=(x, y, core), device_id_type=MESH).wait_recv()

    def body(kc_ref, dx_ref, x_ref, f_ref, gp_ref, up_ref, pg_ref, wgu_ref, wd_ref, qg_ref,
             dx1_ref, dpg_ref, dqg_ref, own_gu_ref, land_gu_ref, own_dn_ref, land_dn_ref,
             h_s, df_s, dh_s, accg, accu, accd, *comm):
        c = pl.program_id(0)
        i = pl.program_id(1)
        rows = pl.ds(pl.multiple_of(i * tm, tm), tm)
        pg = pg_ref[layer:layer + 1, :]

        @pl.when((c == 0) & (i == 0))
        def _():
            dpg_ref[...] = jnp.zeros_like(dpg_ref)
            dqg_ref[...] = jnp.zeros_like(dqg_ref)

        @pl.when(c == 0)
        def _():
            h_s[rows, :] = _rms_fwd(x_ref[...], pg).astype(BF)
            df, prod = _rms_bwd(f_ref[...], qg_ref[layer:layer + 1, :], dx_ref[...])
            df_s[rows, :] = df.astype(BF)
            dqg_ref[...] += _rowsum(prod)

        @pl.when(i == 0)
        def _():
            accg[...] = jnp.zeros_like(accg)
            accu[...] = jnp.zeros_like(accu)
            accd[...] = jnp.zeros_like(accd)

        h = h_s[rows, :]
        df = df_s[rows, :]
        wg = wgu_ref[0]
        wu = wgu_ref[1]
        g = gp_ref[...].astype(F32)
        u = up_ref[...].astype(F32)
        sg = _sigmoid(g)
        a = g * sg
        dact = _dot_nt(df, wd_ref[...])
        accd[...] += _dot_tn((a * u).astype(BF), df)
        du = (dact * a).astype(BF)
        dg = (dact * u * (sg * (1.0 + g * (1.0 - sg)))).astype(BF)
        accg[...] += _dot_tn(h, dg)
        accu[...] += _dot_tn(h, du)
        dh = _dot_nt(dg, wg) + _dot_nt(du, wu)

        @pl.when(c == 0)
        def _():
            dh_s[rows, :] = dh

        @pl.when((c > 0) & (c < nc - 1))
        def _():
            dh_s[rows, :] += dh

        @pl.when(c == nc - 1)
        def _():
            dxp, prod = _rms_bwd(x_ref[...], pg, dh_s[rows, :] + dh)
            dpg_ref[...] += _rowsum(prod)
            dx1_ref[...] = dx_ref[...] + dxp

        @pl.when(i == n - 1)
        def _():
            exchange(kc_ref, c, accg, accu, accd, own_gu_ref, land_gu_ref, own_dn_ref, land_dn_ref, *comm)

    dma = pltpu.SemaphoreType.DMA
    return _call(
        body, name=f"ffn_bwd{layer}", grid=(nc, n),
        in_specs=[pl.BlockSpec((tm, D), edge_rows), pl.BlockSpec((tm, D), edge_rows),
                  pl.BlockSpec((tm, D), lambda c, i, kc_ref: (jnp.where(c == 0, i, n - 1), 0),
                               pipeline_mode=pl.Buffered(1)),
                  pl.BlockSpec((tm, FF_CHUNK), lambda c, i, kc_ref: (i, chunk_at(c, kc_ref))),
                  pl.BlockSpec((tm, FF_CHUNK), lambda c, i, kc_ref: (i, chunk_at(c, kc_ref))),
                  VSPEC,
                  pl.BlockSpec((2, D, FF_CHUNK), lambda c, i, kc_ref: (0, 0, chunk_at(c, kc_ref))),
                  pl.BlockSpec((FF_CHUNK, D), lambda c, i, kc_ref: (chunk_at(c, kc_ref), 0)),
                  VSPEC],
        out_specs=[pl.BlockSpec((tm, D), lambda c, i, kc_ref: (jnp.where(c == nc - 1, i, 0), 0)),
                   _const_spec((1, D)), _const_spec((1, D)), ANYSPEC, ANYSPEC, ANYSPEC, ANYSPEC],
        out_shape=[_sds((s_len, D)), _sds((1, D)), _sds((1, D)),
                   _sds((n_blk, HALF_D, GU_PIECE), BF), _sds((N_CHIPS - 1, n_blk, HALF_D, GU_PIECE), BF),
                   _sds((DN_SLOT, HALF_D), BF), _sds((N_CHIPS - 1, DN_SLOT, HALF_D), BF)],
        scratch_shapes=[pltpu.VMEM((s_len, D), BF), pltpu.VMEM((s_len, D), BF), pltpu.VMEM((s_len, D), F32),
                        pltpu.VMEM((D, FF_CHUNK), F32), pltpu.VMEM((D, FF_CHUNK), F32),
                        pltpu.VMEM((FF_CHUNK, D), F32),
                        pltpu.VMEM((nc, 2, HALF_D, FF_CHUNK), BF), pltpu.VMEM((nc, FF_CHUNK, HALF_D), BF),
                        pltpu.VMEM((2, 2, HALF_D, FF_CHUNK), BF), pltpu.VMEM((2, FF_CHUNK, HALF_D), BF),
                        pltpu.VMEM((2, HALF_D, FF_CHUNK), F32), pltpu.VMEM((FF_CHUNK, HALF_D), F32),
                        pltpu.VMEM((2, 2, n_gu, HALF_D, GU_PIECE), BF), pltpu.VMEM((2, FF_CHUNK, HALF_D), BF),
                        dma((2, 2)), dma((nc, 2)), dma((2, n_pieces)), dma((2, n_pieces)), dma((2, N_CHIPS - 1))],
        args=[dx2, x1, f, g_pre, u_pre, pre_g, wgu, wd, post_g], rider=rider, prefetch=kc)


def _ple_fwd(layer, x2, p, ple_g, w_gate, w_proj, post_g, target=None, qkv=None, rider=None):
    s_len = x2.shape[0]
    final = target is not None
    assert not (final and qkv)

    def body(*refs):
        if final:
            x_ref, p_ref, g_ref, wg_ref, wp_ref, qg_ref, t_ref, z_ref, pe_ref, dx_ref, lv_ref = refs
        elif qkv:
            (x_ref, p_ref, g_ref, wg_ref, wp_ref, qg_ref, ng_ref, kg_ref, wq_ref, wkv_ref,
             z_ref, pe_ref, x3_ref, q_ref, kv_ref) = refs
        else:
            x_ref, p_ref, g_ref, wg_ref, wp_ref, qg_ref, z_ref, pe_ref, x3_ref = refs
        x = x_ref[...]
        r = _rms_fwd(x, g_ref[layer:layer + 1, :]).astype(BF)
        z = _dot(r, wg_ref[...])
        pe = _dot(p_ref[...].astype(BF), wp_ref[...])
        z_ref[...] = z
        pe_ref[...] = pe
        x3 = x + _rms_fwd(pe * _sigmoid(z), qg_ref[layer:layer + 1, :])
        if final:
            @pl.when(pl.program_id(0) == 0)
            def _():
                lv_ref[...] = jnp.zeros_like(lv_ref)
            err = x3 - t_ref[...]
            dx_ref[...] = err * (1.0 / D)
            lv_ref[...] += _rowsum(err * err)
        else:
            x3_ref[...] = x3
        if qkv:
            q_ref[...] = _dot(_rms_fwd(x3, ng_ref[layer + 1:layer + 2, :]).astype(BF), wq_ref[...]).astype(BF)
            kv_ref[...] = _dot(_rms_fwd(x3, kg_ref[...]).astype(BF), wkv_ref[...]).astype(BF)

    p_spec = pl.BlockSpec((None, TM, PLE), lambda i: (layer, i, 0))
    in_specs = [_row_spec(TM), p_spec, VSPEC, VSPEC, VSPEC, VSPEC]
    args = [x2, p, ple_g, w_gate, w_proj, post_g]
    out_specs = [_row_spec(TM), _row_spec(TM), _row_spec(TM)]
    out_shape = [_sds((s_len, D))] * 3
    if qkv:
        in_specs += [VSPEC] * 4
        args += list(qkv)
        out_specs += [_row_spec(TM), _row_spec(TM, 2 * KVD)]
        out_shape += [_sds((s_len, D), BF), _sds((s_len, 2 * KVD), BF)]
    if final:
        in_specs.append(_row_spec(TM))
        args.append(target)
        out_specs.append(_const_spec((1, D)))
        out_shape.append(_sds((1, D)))
    return _call(body, name=f"ple_fwd{layer}", grid=(s_len // TM,), in_specs=in_specs, out_specs=out_specs,
                 out_shape=out_shape, args=args, rider=rider)


def _ple_bwd(layer, dx3, x2, z, pe, p, ple_g, w_gate, post_g, rider=None):
    s_len = x2.shape[0]
    n = s_len // TM

    def body(dx_ref, x_ref, z_ref, pe_ref, p_ref, g_ref, wg_ref, qg_ref,
             dx2_ref, dwg_ref, dwp_ref, dg_ref, dqg_ref, gacc, pacc):
        i = pl.program_id(0)

        @pl.when(i == 0)
        def _():
            gacc[...] = jnp.zeros_like(gacc)
            pacc[...] = jnp.zeros_like(pacc)
            dg_ref[...] = jnp.zeros_like(dg_ref)
            dqg_ref[...] = jnp.zeros_like(dqg_ref)

        dx = dx_ref[...]
        x = x_ref[...]
        pe_v = pe_ref[...]
        gate = _sigmoid(z_ref[...])
        de, prod = _rms_bwd(pe_v * gate, qg_ref[layer:layer + 1, :], dx)
        dqg_ref[...] += _rowsum(prod)
        dpe = (de * gate).astype(BF)
        dz = (de * pe_v * gate * (1.0 - gate)).astype(BF)
        pacc[...] += _dot_tn(p_ref[...].astype(BF), dpe)
        g = g_ref[layer:layer + 1, :]
        r = _rms_fwd(x, g).astype(BF)
        gacc[...] += _dot_tn(r, dz)
        dr = _dot_nt(dz, wg_ref[...])
        dxp, prod2 = _rms_bwd(x, g, dr)
        dg_ref[...] += _rowsum(prod2)
        dx2_ref[...] = dx + dxp

        @pl.when(i == n - 1)
        def _():
            dwg_ref[...] = gacc[...].astype(BF)
            dwp_ref[...] = pacc[...].astype(BF)

    p_spec = pl.BlockSpec((None, TM, PLE), lambda i: (layer, i, 0))
    return _call(
        body, name=f"ple_bwd{layer}", grid=(n,),
        in_specs=[_row_spec(TM), _row_spec(TM), _row_spec(TM), _row_spec(TM), p_spec, VSPEC, VSPEC, VSPEC],
        out_specs=[_row_spec(TM), _const_spec((D, D)), _const_spec((PLE, D)), _const_spec((1, D)), _const_spec((1, D))],
        out_shape=[_sds((s_len, D)), _sds((D, D), BF), _sds((PLE, D), BF), _sds((1, D)), _sds((1, D))],
        scratch_shapes=[pltpu.VMEM((D, D), F32), pltpu.VMEM((PLE, D), F32)],
        args=[dx3, x2, z, pe, p, ple_g, w_gate, post_g], rider=rider)


def _qkv_bwd(dq, dkv, x3, dx4, q_g, kv_g, w_q, w_kv):
    s_len = x3.shape[0]
    n = s_len // TM

    def body(dq_ref, dkv_ref, x_ref, dx_ref, qg_ref, kg_ref, wq_ref, wkv_ref,
             dx3_ref, dwq_ref, dwkv_ref, dqg_ref, dkg_ref, qacc, kacc):
        i = pl.program_id(0)

        @pl.when(i == 0)
        def _():
            qacc[...] = jnp.zeros_like(qacc)
            kacc[...] = jnp.zeros_like(kacc)
            dqg_ref[...] = jnp.zeros_like(dqg_ref)
            dkg_ref[...] = jnp.zeros_like(dkg_ref)

        x = x_ref[...]
        qg = qg_ref[1:2, :]
        kg = kg_ref[...]
        dq_v = dq_ref[...]
        dkv_v = dkv_ref[...].astype(BF)
        qacc[...] += _dot_tn(_rms_fwd(x, qg).astype(BF), dq_v)
        kacc[...] += _dot_tn(_rms_fwd(x, kg).astype(BF), dkv_v)
        dxq, prod_q = _rms_bwd(x, qg, _dot_nt(dq_v, wq_ref[...]))
        dxk, prod_k = _rms_bwd(x, kg, _dot_nt(dkv_v, wkv_ref[...]))
        dqg_ref[...] += _rowsum(prod_q)
        dkg_ref[...] += _rowsum(prod_k)
        dx3_ref[...] = dx_ref[...] + dxq + dxk

        @pl.when(i == n - 1)
        def _():
            dwq_ref[...] = qacc[...].astype(BF)
            dwkv_ref[...] = kacc[...].astype(BF)

    outs, _ = _call(
        body, name="qkv_bwd", grid=(n,),
        in_specs=[_row_spec(TM), _row_spec(TM, 2 * KVD), _row_spec(TM), _row_spec(TM), VSPEC, VSPEC, VSPEC, VSPEC],
        out_specs=[_row_spec(TM), _const_spec((D, D)), _const_spec((D, 2 * KVD)),
                   _const_spec((1, D)), _const_spec((1, D))],
        out_shape=[_sds((s_len, D)), _sds((D, D), BF), _sds((D, 2 * KVD), BF), _sds((1, D)), _sds((1, D))],
        scratch_shapes=[pltpu.VMEM((D, D), F32), pltpu.VMEM((D, 2 * KVD), F32)],
        args=[dq, dkv, x3, dx4, q_g, kv_g, w_q, w_kv])
    return outs


def _attn_group(i, q, kvw, sink_ref, g):
    rows = GQA * BLK
    heads = [GQA * g + j for j in range(GQA)]
    off = jnp.where(i > 0, BLK, 0)
    row = lax.broadcasted_iota(jnp.int32, (rows, 2 * BLK), 0)
    rel = (row % BLK) - lax.broadcasted_iota(jnp.int32, (rows, 2 * BLK), 1) + off
    valid = (rel >= 0) & (rel < BLK)
    head_of_row = lax.broadcasted_iota(jnp.int32, (rows, 1), 0) // BLK
    slope = jnp.zeros((rows, 1), F32)
    sink = jnp.zeros((rows, 1), F32)
    for j, h in enumerate(heads):
        slope = jnp.where(head_of_row == j, SLOPES[h], slope)
        sink = jnp.where(head_of_row == j, sink_ref[0, h], sink)
    qs = jnp.concatenate([q[:, h * HEAD_DIM:(h + 1) * HEAD_DIM] for h in heads], axis=0)
    k = kvw[:, g * HEAD_DIM:(g + 1) * HEAD_DIM]
    v = kvw[:, KVD + g * HEAD_DIM:KVD + (g + 1) * HEAD_DIM]
    s = _dot_nt(qs, k) * ATT_SCALE - slope * rel.astype(F32)
    s = jnp.where(valid, s, NEG_INF)
    m = jnp.maximum(jnp.max(s, axis=-1, keepdims=True), sink)
    e = jnp.exp(s - m)
    es = jnp.exp(sink - m)
    inv = 1.0 / (jnp.sum(e, axis=-1, keepdims=True) + es)
    return e * inv, es * inv, qs, k, v


def _unstack_heads(stacked):
    return [stacked[j * BLK:(j + 1) * BLK, :] for j in range(GQA)]


def _kv_window(kv_ref, i):
    ks = pl.multiple_of(jnp.maximum(i * BLK - BLK, 0), BLK)
    return ks, kv_ref[pl.ds(ks, 2 * BLK), :]


def _attn_fwd(q, kv, sinks, x3, w_o, post_g, rider=None):
    s_len = q.shape[0]

    def body(q_ref, kv_ref, sk_ref, x_ref, wo_ref, g_ref, a_ref, y_ref, x4_ref):
        i = pl.program_id(0)
        _, kvw = _kv_window(kv_ref, i)
        q = q_ref[...]
        outs = []
        for g in range(N_KV_HEADS):
            p, _, _, _, v = _attn_group(i, q, kvw, sk_ref, g)
            outs += _unstack_heads(_dot(p.astype(BF), v))
        attn = jnp.concatenate(outs, axis=1)
        a_ref[...] = attn
        y = _dot(attn.astype(BF), wo_ref[...])
        y_ref[...] = y
        x4_ref[...] = x_ref[...] + _rms_fwd(y, g_ref[1:2, :])

    return _call(body, name="attn_fwd", grid=(s_len // BLK,),
                 in_specs=[_row_spec(BLK), VSPEC, SSPEC, _row_spec(BLK), VSPEC, VSPEC],
                 out_specs=[_row_spec(BLK)] * 3, out_shape=[_sds((s_len, D))] * 3,
                 args=[q, kv, sinks, x3, w_o, post_g], rider=rider)


def _attn_bwd(dx4, y, attn, q, kv, sinks, w_o, post_g, rider=None):
    s_len = q.shape[0]
    n = s_len // BLK

    def body(dx_ref, y_ref, a_ref, q_ref, kv_ref, sk_ref, wo_ref, g_ref,
             dq_ref, dkv_ref, dwo_ref, dg_ref, dsk_ref, wacc):
        i = pl.program_id(0)

        @pl.when(i == 0)
        def _():
            dkv_ref[...] = jnp.zeros_like(dkv_ref)
            wacc[...] = jnp.zeros_like(wacc)
            dg_ref[...] = jnp.zeros_like(dg_ref)
            dsk_ref[...] = jnp.zeros_like(dsk_ref)

        dy, prod = _rms_bwd(y_ref[...], g_ref[1:2, :], dx_ref[...])
        dg_ref[...] += _rowsum(prod)
        dyb = dy.astype(BF)
        attn = a_ref[...]
        wacc[...] += _dot_tn(attn.astype(BF), dyb)
        d_o = _dot_nt(dyb, wo_ref[...])
        dod = d_o * attn
        ks, kvw = _kv_window(kv_ref, i)
        q = q_ref[...]
        lane = lax.broadcasted_iota(jnp.int32, (1, D), 1)
        dqs, dks, dvs = [], [], []
        dsk = jnp.zeros((1, D), F32)
        for g in range(N_KV_HEADS):
            p, ps, qs, k, v = _attn_group(i, q, kvw, sk_ref, g)
            cols = [slice((GQA * g + j) * HEAD_DIM, (GQA * g + j + 1) * HEAD_DIM) for j in range(GQA)]
            do_s = jnp.concatenate([d_o[:, c] for c in cols], axis=0).astype(BF)
            dsum = jnp.concatenate([jnp.sum(dod[:, c], axis=-1, keepdims=True) for c in cols], axis=0)
            dp = _dot_nt(do_s, v)
            dsb = (p * (dp - dsum) * ATT_SCALE).astype(BF)
            sink_part = ps * dsum
            for j in range(GQA):
                dsk = dsk + jnp.where(lane == GQA * g + j, -_rowsum(sink_part[j * BLK:(j + 1) * BLK, :]), 0.0)
            dqs += _unstack_heads(_dot(dsb, k))
            dks.append(_dot_tn(dsb, qs))
            dvs.append(_dot_tn(p.astype(BF), do_s))
        dsk_ref[...] += dsk
        dq_ref[...] = jnp.concatenate(dqs, axis=1).astype(BF)
        dkv_ref[pl.ds(ks, 2 * BLK), :] += jnp.concatenate(dks + dvs, axis=1)

        @pl.when(i == n - 1)
        def _():
            dwo_ref[...] = wacc[...].astype(BF)

    return _call(
        body, name="attn_bwd", grid=(n,),
        in_specs=[_row_spec(BLK), _row_spec(BLK), _row_spec(BLK), _row_spec(BLK), VSPEC, SSPEC, VSPEC, VSPEC],
        out_specs=[_row_spec(BLK), _const_spec((s_len, 2 * KVD)), _const_spec((D, D)),
                   _const_spec((1, D)), _const_spec((1, D))],
        out_shape=[_sds((s_len, D), BF), _sds((s_len, 2 * KVD)), _sds((D, D), BF), _sds((1, D)), _sds((1, D))],
        scratch_shapes=[pltpu.VMEM((D, D), F32)],
        args=[dx4, y, attn, q, kv, sinks, w_o, post_g], rider=rider)


Big = collections.namedtuple("Big", "name src layer L A R C rb")


def _bigs():
    out = {"pool_w": Big("pool_w", "pool_w", None, 4, 4, POOL_G // N_CHIPS, POOL_G, 32)}
    for l in range(2):
        out[f"w_gu{l}"] = Big(f"w_gu{l}", "w_gu", l, 1, 2, D, FF_HALF, 256)
        out[f"w_down{l}"] = Big(f"w_down{l}", "w_down", l, 1, 4, FF // N_CHIPS, D, 352)
        out[f"w_ple_gate{l}"] = Big(f"w_ple_gate{l}", "w_ple_gate", l, 1, 4, D // N_CHIPS, D, 128)
        out[f"w_ple_proj{l}"] = Big(f"w_ple_proj{l}", "w_ple_proj", l, 1, 1, PLE, D // N_CHIPS, 128)
    out["w_q"] = Big("w_q", "w_q", None, 1, 4, D // N_CHIPS, D, 128)
    out["w_o"] = Big("w_o", "w_o", None, 1, 4, D // N_CHIPS, D, 128)
    out["w_kv"] = Big("w_kv", "w_kv", None, 1, 4, D // N_CHIPS, 2 * KVD, 128)
    return out


BIGS = _bigs()
POOL_SCALE = Big("pool_scale", "pool_scale", None, 1, 1, 1, D // N_CHIPS, 1)
BIG_SOURCES = ("w_gu", "w_down", "w_ple_gate", "w_ple_proj", "w_q", "w_o", "w_kv", "pool_w")


def _ncb(t):
    return N_CHIPS // t.A


def _full_shape(t, rows=None):
    return (t.L, t.A, t.R if rows is None else rows, _ncb(t) * t.C)


def _slot_index(t, k):
    return k // _ncb(t), k % _ncb(t)


def _slot(ref, t, k, row0, rows):
    a, cb = _slot_index(t, k)
    return ref.at[:, a, pl.ds(row0, rows), pl.ds(pl.multiple_of(cb * t.C, 128), t.C)]


def _place_job(t, w, out_dtype=BF):
    nb = next((nb for nb in (8, 4, 2, 1) if t.R % (16 * nb) == 0), 1) if t.L == 1 else 1
    rb = t.R // nb

    def fn(j, kc_ref, ins, outs):
        outs[0][...] = ins[0][...].astype(out_dtype)

    def in_map(j, kc_ref):
        return (j // nb if t.layer is None else t.layer, j % nb, 0)

    def out_map(j, kc_ref):
        a, cb = _slot_index(t, kc_ref[0])
        return (j // nb, a, j % nb, cb)

    return Job(t.L * nb, [(w, (None, rb, t.C), in_map)],
               [(_sds(_full_shape(t), out_dtype), (None, None, rb, t.C), out_map)], fn)


def _mesh_position():
    x, y, c = lax.axis_index("x"), lax.axis_index("y"), lax.axis_index("c")
    chips = [(1 - x, y), (x, 1 - y), (1 - x, 1 - y)]
    return x, y, c, chips


DIRECT_BELOW = 1024


def _gather_rider(parts, fulls):
    nt = len(parts)
    TO_X, TO_Y, FWD_X, FWD_Y, SIB_X, SIB_Y, SIB_D = range(7)

    def rows_of(ti, core):
        t, r0, r1 = parts[ti]
        h = (r1 - r0) // 2
        return r0 + core * h, h

    def copy(outs, sems, kind, ti, k_src, row0, rows, dev):
        region = _slot(outs[ti], parts[ti][0], k_src, row0, rows)
        return pltpu.make_async_remote_copy(region, region, sems[0].at[ti, kind], sems[1].at[ti, kind],
                                            device_id=dev, device_id_type=MESH)

    def plan(outs, sems):
        x, y, c, _ = _mesh_position()
        me, kx, ky, kd = 2 * x + y, 2 * (1 - x) + y, 2 * x + (1 - y), 2 * (1 - x) + (1 - y)
        dev_x, dev_y, dev_d, sib = (1 - x, y, c), (x, 1 - y, c), (1 - x, 1 - y, c), (x, y, 1 - c)

        def whole(ti):
            return 0, parts[ti][0].R

        def mk(kind, k_send, k_recv, dev, send_rows, recv_rows):
            def build(ti, side):
                k_src = k_send if side == "s" else k_recv
                row0, rows = (send_rows if side == "s" else recv_rows)(ti)
                return copy(outs, sems, kind, ti, k_src, row0, rows, dev)
            return build

        def first_half(core):
            return lambda ti: (rows_of(ti, core)[0], rows_of(ti, core)[1] // 2)

        def second_half(core):
            return lambda ti: (rows_of(ti, core)[0] + rows_of(ti, core)[1] // 2, rows_of(ti, core)[1] // 2)

        mine = lambda ti: rows_of(ti, c)
        theirs = lambda ti: rows_of(ti, 1 - c)
        split = {
            TO_X: mk(TO_X, me, kx, dev_x, mine, mine),
            TO_Y: mk(TO_Y, me, ky, dev_y, mine, mine),
            FWD_X: mk(FWD_X, ky, kd, dev_x, first_half(c), first_half(c)),
            FWD_Y: mk(FWD_Y, kx, kd, dev_y, second_half(c), second_half(c)),
            SIB_X: mk(SIB_X, kx, kx, sib, mine, theirs),
            SIB_Y: mk(SIB_Y, ky, ky, sib, mine, theirs),
            SIB_D: mk(SIB_D, kd, kd, sib, mine, theirs),
        }
        direct = {
            TO_X: mk(TO_X, me, kx, dev_x, whole, whole),
            TO_Y: mk(TO_Y, me, ky, dev_y, whole, whole),
            FWD_X: mk(FWD_X, me, kd, dev_d, whole, whole),
        }
        return split, direct

    is_split = [t.L * t.R * t.C >= DIRECT_BELOW for t, _, _ in parts]
    assert all(s or (r0, r1) == (0, t.R) for s, (t, r0, r1) in zip(is_split, parts))

    def start(ins, outs, sems):
        split, direct = plan(outs, sems)
        for ti in range(nt):
            kinds = split if is_split[ti] else direct
            kinds[TO_X](ti, "s").start()
            kinds[TO_Y](ti, "s").start()
            if not is_split[ti]:
                kinds[FWD_X](ti, "s").start()

    def mid(ins, outs, sems):
        split, _ = plan(outs, sems)
        for ti in range(nt):
            if is_split[ti]:
                split[TO_Y](ti, "r").wait_recv()
                split[FWD_X](ti, "s").start()
                split[SIB_Y](ti, "s").start()
        for ti in range(nt):
            if is_split[ti]:
                split[TO_X](ti, "r").wait_recv()
                split[FWD_Y](ti, "s").start()
                split[SIB_X](ti, "s").start()

    def finish(ins, outs, sems):
        split, direct = plan(outs, sems)
        for ti in range(nt):
            if is_split[ti]:
                split[FWD_X](ti, "r").wait_recv()
                split[FWD_Y](ti, "r").wait_recv()
                split[SIB_D](ti, "s").start()
            else:
                for kind in (TO_X, TO_Y, FWD_X):
                    direct[kind](ti, "r").wait_recv()
        for ti in range(nt):
            if is_split[ti]:
                for kind in (SIB_X, SIB_Y, SIB_D):
                    split[kind](ti, "r").wait_recv()
        for ti in range(nt):
            kinds = split if is_split[ti] else direct
            for kind in kinds:
                kinds[kind](ti, "s").wait_send()

    sems = pltpu.SemaphoreType.DMA((nt, 7))
    return Rider(list(fulls), [_sds(a.shape, a.dtype) for a in fulls], {i: i for i in range(nt)},
                 [sems, sems], start, mid, finish)


def _pair_exchange_rider(specs, grads):
    nt = len(specs)

    def copy(ins, outs, sems, ti, c, sibling):
        half = specs[ti].R // 2
        return pltpu.make_async_remote_copy(ins[ti].at[:, :, pl.ds((1 - c) * half, half), :], outs[ti],
                                            sems[0].at[ti], sems[1].at[ti], device_id=sibling, device_id_type=MESH)

    def start(ins, outs, sems):
        x, y, c, _ = _mesh_position()
        for ti in range(nt):
            copy(ins, outs, sems, ti, c, (x, y, 1 - c)).start()

    def finish(ins, outs, sems):
        x, y, c, _ = _mesh_position()
        for ti in range(nt):
            copy(ins, outs, sems, ti, c, (x, y, 1 - c)).wait()

    sems = pltpu.SemaphoreType.DMA((nt,))
    return Rider(list(grads), [_sds(_full_shape(t, t.R // 2), BF) for t in specs], {}, [sems, sems], start, None, finish)


def _pair_sum_job(t, g, land):
    assert t.L == 1
    half = t.R // 2
    nj = half // t.rb
    block = (None, t.A, t.rb, _ncb(t) * t.C)

    def fn(j, kc_ref, ins, outs):
        outs[0][...] = (ins[0][...].astype(F32) + ins[1][...].astype(F32)).astype(BF)

    return Job(nj,
               [(g, block, lambda j, kc_ref: (0, 0, kc_ref[1] * nj + j, 0)),
                (land, block, lambda j, kc_ref: (0, 0, j, 0))],
               [(_sds(_full_shape(t, half), BF), block, lambda j, kc_ref: (0, 0, j, 0))], fn)


def _scatter_rider(specs, sums):
    nt = len(specs)

    def copy(ins, outs, sems, ti, j, chip, c):
        t = specs[ti]
        cx, cy = chip
        return pltpu.make_async_remote_copy(_slot(ins[ti], t, 2 * cx + cy, 0, t.R // 2), outs[ti].at[j],
                                            sems[0].at[ti, j], sems[1].at[ti, j],
                                            device_id=(cx, cy, c), device_id_type=MESH)

    def start(ins, outs, sems):
        _, _, c, chips = _mesh_position()
        for j, chip in enumerate(chips):
            for ti in range(nt):
                copy(ins, outs, sems, ti, j, chip, c).start()

    def finish(ins, outs, sems):
        _, _, c, chips = _mesh_position()
        for j, chip in enumerate(chips):
            for ti in range(nt):
                copy(ins, outs, sems, ti, j, chip, c).wait()

    sems = pltpu.SemaphoreType.DMA((nt, N_CHIPS - 1))
    return Rider(list(sums), [_sds((N_CHIPS - 1, t.L, t.R // 2, t.C), BF) for t in specs], {}, [sems, sems],
                 start, None, finish)


def _chip_sum_job(ts, landed):
    t0 = ts[0]
    assert t0.L == 1
    half = t0.R // 2
    nj = half // t0.rb

    def local(j, li):
        return jnp.clip(j - li * nj, 0, nj - 1)

    ins = []
    for li, t in enumerate(ts):
        s, land = landed[t.name]

        def own_map(j, kc_ref, li=li, t=t):
            a, cb = _slot_index(t, kc_ref[0])
            return (0, a, local(j, li), cb)

        ins.append((s, (None, None, t.rb, t.C), own_map))
        ins.append((land, (N_CHIPS - 1, None, t.rb, t.C), lambda j, kc_ref, li=li: (0, 0, local(j, li), 0)))

    def fn(j, kc_ref, in_refs, outs):
        for li in range(len(ts)):
            @pl.when(j // nj == li)
            def _():
                acc = in_refs[2 * li][...].astype(F32)
                for k in range(N_CHIPS - 1):
                    acc = acc + in_refs[2 * li + 1][k].astype(F32)
                outs[0][...] = acc

    return Job(len(ts) * nj, ins,
               [(_sds((len(ts), t0.R, t0.C)), (None, t0.rb, t0.C),
                 lambda j, kc_ref: (j // nj, kc_ref[1] * nj + j % nj, 0))], fn)


def _adamw_job(rb, w, g, m, v):
    n_layers, r, c = w.shape
    nb = r // rb
    block = (None, rb, c)
    index = lambda j, kc_ref: (j // nb, j % nb, 0)

    def fn(j, kc_ref, ins, outs):
        g_v = ins[1][...]
        outs[0][...] = g_v
        outs[1][...], outs[2][...], outs[3][...] = _adamw_math(ins[0][...], g_v, ins[2][...], ins[3][...])

    return Job(n_layers * nb, [(a, block, index) for a in (w, g, m, v)],
               [(_sds(w.shape), block, index)] * 4, fn)


def _chip_sum_fused_job(ts, fused, by_cols):
    t0 = ts[0]
    own0 = fused[t0.name][0]
    if by_cols:
        rows, cols = own0.shape
    else:
        nb, rows, bw = own0.shape
        cols = nb * bw
    nj = rows // t0.rb

    def local(j, li):
        return jnp.clip(j - li * nj, 0, nj - 1)

    ins = []
    for li, t in enumerate(ts):
        own, land = fused[t.name]
        if by_cols:
            ins.append((own, (t.rb, cols), lambda j, kc_ref, li=li: (local(j, li), 0)))
            ins.append((land, (N_CHIPS - 1, t.rb, cols), lambda j, kc_ref, li=li: (0, local(j, li), 0)))
        else:
            ins.append((own, (nb, t.rb, bw), lambda j, kc_ref, li=li: (0, local(j, li), 0)))
            ins.append((land, (N_CHIPS - 1, nb, t.rb, bw), lambda j, kc_ref, li=li: (0, 0, local(j, li), 0)))

    def fn(j, kc_ref, in_refs, outs):
        for li in range(len(ts)):
            @pl.when(j // nj == li)
            def _():
                acc = in_refs[2 * li][...].astype(F32)
                for k in range(N_CHIPS - 1):
                    acc = acc + in_refs[2 * li + 1][k].astype(F32)
                outs[0][...] = acc if by_cols else jnp.concatenate([acc[b] for b in range(nb)], axis=1)

    def out_map(j, kc_ref):
        return (j // nj, j % nj, kc_ref[1]) if by_cols else (j // nj, kc_ref[1] * nj + j % nj, 0)

    return Job(len(ts) * nj, ins, [(_sds((len(ts), t0.R, t0.C)), (None, t0.rb, cols), out_map)], fn)


def _share_rider(halves, by_cols):
    nt = len(halves)

    def copy(outs, sems, ti, core, sibling):
        axis = 2 if by_cols[ti] else 1
        half = halves[ti].shape[axis] // 2
        piece = pl.ds(pl.multiple_of(core * half, 128 if by_cols[ti] else 8), half)
        part = outs[ti].at[:, :, piece] if by_cols[ti] else outs[ti].at[:, piece, :]
        return pltpu.make_async_remote_copy(part, part, sems[0].at[ti], sems[1].at[ti],
                                            device_id=sibling, device_id_type=MESH)

    def start(ins, outs, sems):
        x, y, c, _ = _mesh_position()
        for ti in range(nt):
            copy(outs, sems, ti, c, (x, y, 1 - c)).start()

    def finish(ins, outs, sems):
        x, y, c, _ = _mesh_position()
        for ti in range(nt):
            copy(outs, sems, ti, 1 - c, (x, y, 1 - c)).wait_recv()
        for ti in range(nt):
            copy(outs, sems, ti, c, (x, y, 1 - c)).wait_send()

    sems = pltpu.SemaphoreType.DMA((nt,))
    return Rider(list(halves), [_sds(a.shape, a.dtype) for a in halves], {i: i for i in range(nt)}, [sems, sems],
                 start, None, finish)


def _both(r1, r2):
    assert r1.mid is None and r2.mid is None
    ni, no, ns = len(r1.arrays), len(r1.out_shapes), len(r1.scratch)

    def split(fn1, fn2):
        def run(ins, outs, scr):
            fn1(ins[:ni], outs[:no], scr[:ns])
            fn2(ins[ni:], outs[no:], scr[ns:])
        return run

    aliases = dict(r1.aliases)
    aliases.update({ni + a: no + b for a, b in r2.aliases.items()})
    return Rider(r1.arrays + r2.arrays, r1.out_shapes + r2.out_shapes, aliases, r1.scratch + r2.scratch,
                 split(r1.start, r2.start), None, split(r1.finish, r2.finish))


def _adamw_math(w, g, m, v):
    m = B1 * m + (1.0 - B1) * g
    v = B2 * v + (1.0 - B2) * (g * g)
    delta = -LR * ((m / BC1) / (jnp.sqrt(v / BC2) + AEPS) + WD * w)
    return delta, m, v


def _adamw(name, rb, w, g, m, v):
    n_layers, r, c = w.shape

    def body(w_ref, g_ref, m_ref, v_ref, go_ref, d_ref, nm_ref, nv_ref):
        g_v = g_ref[...]
        go_ref[...] = g_v
        d_ref[...], nm_ref[...], nv_ref[...] = _adamw_math(w_ref[...], g_v, m_ref[...], v_ref[...])

    spec = pl.BlockSpec((None, rb, c), lambda l, j: (l, j, 0))
    return pl.pallas_call(
        body, name=f"adamw_{name}", grid=(n_layers, r // rb),
        in_specs=[spec] * 4, out_specs=[spec] * 4, out_shape=[_sds(w.shape)] * 4,
        compiler_params=_params(2),
    )(w, g, m, v)


GAIN_ROWS = {"pre_mix_g": 0, "post_mix_g": 2, "pre_ffn_g": 4, "post_ffn_g": 6, "ple_g": 8, "ple_post_g": 10}
ROW_KV_G, ROW_POOL_SCALE, ROW_SINKS, ROW_LOSS, PACK_ROWS = 12, 13, 14, 15, 16
SMALL_NAMES = tuple(GAIN_ROWS) + ("kv_g", "pool_scale", "sinks")


def _small_all_reduce(rows, dpool, rider=None):
    ng, pr = len(WINDOWS), POOL_G // N_CHIPS

    def body(*refs):
        row_refs = refs[:PACK_ROWS]
        dpool_ref, tot_ref, gpool_ref, pack, land, pland, send, recv, psend, precv = refs[PACK_ROWS:]
        x, y, c, _ = _mesh_position()
        me = 4 * x + 2 * y + c
        for r in range(PACK_ROWS):
            pack[r:r + 1, :] = row_refs[r][...]

        def shard_of(k):
            return dpool_ref.at[:, pl.ds(pl.multiple_of(k * pr, pr), pr), :]

        cps = []
        for j in range(1, N_DEV):
            px, py, pc = x ^ (j >> 2), y ^ ((j >> 1) & 1), c ^ (j & 1)
            cps.append(pltpu.make_async_remote_copy(pack, land.at[me], send.at[j], recv.at[j],
                                                    device_id=(px, py, pc), device_id_type=MESH))
            cps.append(pltpu.make_async_remote_copy(shard_of(2 * px + py), pland.at[me], psend.at[j], precv.at[j],
                                                    device_id=(px, py, pc), device_id_type=MESH))
        for cp in cps:
            cp.start()
        land[me] = pack[...]
        pland[me] = dpool_ref[:, pl.ds(pl.multiple_of((2 * x + y) * pr, pr), pr), :]
        for j in range(1, N_DEV):
            pltpu.make_async_remote_copy(pack, land.at[me ^ j], send.at[j], recv.at[j],
                                         device_id=(x, y, c), device_id_type=MESH).wait_recv()
            pltpu.make_async_remote_copy(shard_of(0), pland.at[me ^ j], psend.at[j], precv.at[j],
                                         device_id=(x, y, c), device_id_type=MESH).wait_recv()
        for cp in cps:
            cp.wait_send()
        tot = land[0]
        gp = pland[0].astype(F32)
        for d in range(1, N_DEV):
            tot = tot + land[d]
            gp = gp + pland[d].astype(F32)
        tot_ref[...] = tot
        gpool_ref[...] = gp

    sems = pltpu.SemaphoreType.DMA((N_DEV,))
    return _call(
        body, name="small_all_reduce", grid=(1,),
        in_specs=[VSPEC] * (PACK_ROWS + 1), out_specs=[VSPEC, VSPEC],
        out_shape=[_sds((PACK_ROWS, D)), _sds((ng, pr, POOL_G))],
        scratch_shapes=[pltpu.VMEM((PACK_ROWS, D), F32), pltpu.VMEM((N_DEV, PACK_ROWS, D), F32),
                        pltpu.VMEM((N_DEV, ng, pr, POOL_G), BF), sems, sems, sems, sems],
        args=[*rows, dpool], rider=rider)


def _small_adamw(tot, kc, small_w, small_m, small_v):
    names = SMALL_NAMES
    n = len(names)

    def body(*refs):
        tot_ref, kc_ref = refs[0], refs[1]
        w_refs = dict(zip(names, refs[2:2 + n]))
        m_refs = dict(zip(names, refs[2 + n:2 + 2 * n]))
        v_refs = dict(zip(names, refs[2 + 2 * n:2 + 3 * n]))
        loss_ref = refs[2 + 3 * n]
        out_refs = {nm: refs[3 + 3 * n + 4 * k: 7 + 3 * n + 4 * k] for k, nm in enumerate(names)}
        tot = tot_ref[...]
        loss_ref[...] = 0.5 * jnp.sum(tot[ROW_LOSS:ROW_LOSS + 1, :], axis=-1, keepdims=True) * (1.0 / D)

        def update(nm, g):
            g_ref, d_ref, nm_ref, nv_ref = out_refs[nm]
            g_ref[...] = g
            d_ref[...], nm_ref[...], nv_ref[...] = _adamw_math(w_refs[nm][...], g, m_refs[nm][...], v_refs[nm][...])

        for nm, r in GAIN_ROWS.items():
            update(nm, tot[r:r + 2, :])
        update("kv_g", tot[ROW_KV_G:ROW_KV_G + 1, :])
        k = kc_ref[0]
        width = D // N_CHIPS
        g_scale = jnp.zeros((1, width), F32)
        for kk in range(N_CHIPS):
            g_scale = g_scale + jnp.where(k == kk, tot[ROW_POOL_SCALE:ROW_POOL_SCALE + 1, kk * width:(kk + 1) * width], 0.0)
        update("pool_scale", g_scale)
        update("sinks", tot[ROW_SINKS:ROW_SINKS + 1, 0:N_HEADS])

    ins = [tot, kc] + [small_w[nm] for nm in names] + [small_m[nm] for nm in names] + [small_v[nm] for nm in names]
    out_shape = [_sds((1, 1))]
    for nm in names:
        out_shape += [_sds(small_w[nm].shape)] * 4
    outs = pl.pallas_call(
        body, name="small_adamw",
        in_specs=[VSPEC, SSPEC] + [VSPEC] * (3 * n), out_specs=[VSPEC] * len(out_shape), out_shape=out_shape,
        compiler_params=_params(),
    )(*ins)
    return outs[0], {nm: outs[1 + 4 * k: 5 + 4 * k] for k, nm in enumerate(names)}


def _compute_layout(t, full):
    if t.src == "w_gu":
        return full.reshape(2, D, FF)
    if t.src == "pool_w":
        return full.reshape(len(WINDOWS), POOL_G, POOL_G)
    if t.src == "pool_scale":
        return full.reshape(1, D)
    return full.reshape(t.A * t.R, _ncb(t) * t.C)


def kernel(x, p, pre_mix_g, post_mix_g, pre_ffn_g, post_ffn_g, pool_w, pool_scale, kv_g, w_kv, w_q, sinks, w_o, w_gu, w_down, ple_g, w_ple_gate, w_ple_proj, ple_post_g, loss_target, m_pre_mix_g, m_post_mix_g, m_pre_ffn_g, m_post_ffn_g, m_pool_w, m_pool_scale, m_kv_g, m_w_kv, m_w_q, m_sinks, m_w_o, m_w_gu, m_w_down, m_ple_g, m_w_ple_gate, m_w_ple_proj, m_ple_post_g, v_pre_mix_g, v_post_mix_g, v_pre_ffn_g, v_post_ffn_g, v_pool_w, v_pool_scale, v_kv_g, v_w_kv, v_w_q, v_sinks, v_w_o, v_w_gu, v_w_down, v_ple_g, v_w_ple_gate, v_w_ple_proj, v_ple_post_g):
    weights = dict(pre_mix_g=pre_mix_g, post_mix_g=post_mix_g, pre_ffn_g=pre_ffn_g, post_ffn_g=post_ffn_g,
                   pool_w=pool_w, pool_scale=pool_scale, kv_g=kv_g, w_kv=w_kv, w_q=w_q, sinks=sinks, w_o=w_o,
                   w_gu=w_gu, w_down=w_down, ple_g=ple_g, w_ple_gate=w_ple_gate, w_ple_proj=w_ple_proj,
                   ple_post_g=ple_post_g)
    m_in = dict(pre_mix_g=m_pre_mix_g, post_mix_g=m_post_mix_g, pre_ffn_g=m_pre_ffn_g, post_ffn_g=m_post_ffn_g,
                pool_w=m_pool_w, pool_scale=m_pool_scale, kv_g=m_kv_g, w_kv=m_w_kv, w_q=m_w_q, sinks=m_sinks,
                w_o=m_w_o, w_gu=m_w_gu, w_down=m_w_down, ple_g=m_ple_g, w_ple_gate=m_w_ple_gate,
                w_ple_proj=m_w_ple_proj, ple_post_g=m_ple_post_g)
    v_in = dict(pre_mix_g=v_pre_mix_g, post_mix_g=v_post_mix_g, pre_ffn_g=v_pre_ffn_g, post_ffn_g=v_post_ffn_g,
                pool_w=v_pool_w, pool_scale=v_pool_scale, kv_g=v_kv_g, w_kv=v_w_kv, w_q=v_w_q, sinks=v_sinks,
                w_o=v_w_o, w_gu=v_w_gu, w_down=v_w_down, ple_g=v_ple_g, w_ple_gate=v_w_ple_gate,
                w_ple_proj=v_w_ple_proj, ple_post_g=v_ple_post_g)
    order = ["pre_mix_g", "post_mix_g", "pre_ffn_g", "post_ffn_g", "pool_w", "pool_scale", "kv_g", "w_kv", "w_q",
             "sinks", "w_o", "w_gu", "w_down", "ple_g", "w_ple_gate", "w_ple_proj", "ple_post_g"]

    kc = jnp.stack([2 * lax.axis_index("x") + lax.axis_index("y"), lax.axis_index("c")]).astype(jnp.int32)
    s_len = x.shape[1]
    x2d = x.reshape(s_len, D)
    p3d = p.reshape(2, s_len, PLE)
    target = loss_target.reshape(s_len, D)
    kv_g2d = kv_g.reshape(1, D)
    gains = {nm: weights[nm] for nm in GAIN_ROWS}

    def shard_view(src, a):
        t = next(t for t in BIGS.values() if t.src == src)
        return a.reshape(-1, t.R, t.C)

    first, second = ["pool_w", "pool_scale"], ["w_gu0", "w_down0"]
    rest = [nm for nm in BIGS if nm not in first + second]
    specs = dict(BIGS, pool_scale=POOL_SCALE)
    placed = {}

    def place_job(nm):
        if nm == "pool_scale":
            return _place_job(POOL_SCALE, pool_scale.reshape(1, 1, D // N_CHIPS), F32)
        return _place_job(BIGS[nm], shard_view(BIGS[nm].src, weights[BIGS[nm].src]))

    def gather(names, rows=None):
        rows = rows or {}
        parts = [(specs[nm],) + tuple(rows.get(nm, (0, specs[nm].R))) for nm in names]
        return _gather_rider(parts, [placed[nm] for nm in names])

    def take(names, results):
        for nm, a in zip(names, results):
            placed[nm] = a

    def weight(nm):
        return _compute_layout(specs[nm], placed[nm])

    take(first, [r[0] for r in _multi_call("place_pool", [place_job(nm) for nm in first], kc)])
    cast, got = _multi_call("place_ffn0", [place_job(nm) for nm in second], kc, rider=gather(first))
    take(second, [r[0] for r in cast])
    take(first, got)
    jobs = [place_job(nm) for nm in rest]
    jobs.append(_mixa_fwd_job(x2d, gains["pre_mix_g"], weight("pool_w"), weight("pool_scale"), gains["post_mix_g"]))
    results, got = _multi_call("cast_and_mixa_fwd", jobs, kc, rider=gather(second))
    take(rest, [r[0] for r in results[:-1]])
    take(second, got)
    y0, x1 = results[-1]

    ride = ["w_ple_gate0", "w_ple_proj0", "w_q", "w_kv", "w_o", "w_gu1"]
    (f0, x2, g0, u0), got = _ffn_fwd(0, x1, gains["pre_ffn_g"], weight("w_gu0"), weight("w_down0"), gains["post_ffn_g"],
                             rider=gather(ride, {"w_gu1": (0, 320)}))
    take(ride, got)

    ride = ["w_ple_gate1", "w_ple_proj1", "w_gu1"]
    (z0, pe0, x3, q, kv), got = _ple_fwd(
        0, x2, p3d, gains["ple_g"], weight("w_ple_gate0"), weight("w_ple_proj0"), gains["ple_post_g"],
        qkv=(gains["pre_mix_g"], kv_g2d, weight("w_q"), weight("w_kv")),
        rider=gather(ride, {"w_gu1": (320, 704)}))
    take(ride, got)

    ride = ["w_down1", "w_gu1"]
    (attn, y1, x4), got = _attn_fwd(q, kv, sinks, x3, weight("w_o"), gains["post_mix_g"],
                                    rider=gather(ride, {"w_gu1": (704, D)}))
    take(ride, got)

    (f1, x5, g1, u1), _ = _ffn_fwd(1, x4, gains["pre_ffn_g"], weight("w_gu1"), weight("w_down1"), gains["post_ffn_g"])
    (z1, pe1, dx6, loss_row), _ = _ple_fwd(1, x5, p3d, gains["ple_g"], weight("w_ple_gate1"), weight("w_ple_proj1"),
                                           gains["ple_post_g"], target=target)

    local = {}
    landed = {}
    fused = {}

    def local_grads(names):
        return [local[nm].reshape(_full_shape(BIGS[nm])) for nm in names]

    def pair_exchange(names):
        return _pair_exchange_rider([BIGS[nm] for nm in names], local_grads(names))

    def pair_sum(tag, names, lands):
        jobs = [_pair_sum_job(BIGS[nm], g, l) for nm, g, l in zip(names, local_grads(names), lands)]
        return [r[0] for r in _multi_call(f"pair_sum_{tag}", jobs, kc)]

    def scatter(names, sums):
        return _scatter_rider([BIGS[nm] for nm in names], sums)

    def keep(names, sums, got):
        for nm, s, l in zip(names, sums, got):
            landed[nm] = (s, l)

    (dx5, local["w_ple_gate1"], local["w_ple_proj1"], d_ple1, d_plepost1), _ = _ple_bwd(
        1, dx6, x5, z1, pe1, p3d, gains["ple_g"], weight("w_ple_gate1"), gains["ple_post_g"])

    group_a = ["w_ple_gate1", "w_ple_proj1"]
    (dx4, d_preffn1, d_postffn1, *scattered), lands_a = _ffn_bwd(
        1, dx5, x4, f1, g1, u1, gains["pre_ffn_g"], weight("w_gu1"), weight("w_down1"), gains["post_ffn_g"], kc,
        rider=pair_exchange(group_a))
    fused["w_gu1"], fused["w_down1"] = scattered[0:2], scattered[2:4]
    sums_a = pair_sum("a", group_a, lands_a)

    (dq, dkv, local["w_o"], d_postmix1, d_sinks), got = _attn_bwd(
        dx4, y1, attn, q, kv, sinks, weight("w_o"), gains["post_mix_g"], rider=scatter(group_a, sums_a))
    keep(group_a, sums_a, got)
    dx3, local["w_q"], local["w_kv"], d_premix1, d_kvg = _qkv_bwd(
        dq, dkv, x3, dx4, gains["pre_mix_g"], kv_g2d, weight("w_q"), weight("w_kv"))

    group_b = ["w_o", "w_q", "w_kv"]
    sums_b = pair_sum("b", group_b, _run("grads_pair_exchange_b", pair_exchange(group_b)))
    (dx2, local["w_ple_gate0"], local["w_ple_proj0"], d_ple0, d_plepost0), got = _ple_bwd(
        0, dx3, x2, z0, pe0, p3d, gains["ple_g"], weight("w_ple_gate0"), gains["ple_post_g"],
        rider=scatter(group_b, sums_b))
    keep(group_b, sums_b, got)

    group_c = ["w_ple_gate0", "w_ple_proj0"]
    (dx1, d_preffn0, d_postffn0, *scattered), lands_c = _ffn_bwd(
        0, dx2, x1, f0, g0, u0, gains["pre_ffn_g"], weight("w_gu0"), weight("w_down0"), gains["post_ffn_g"], kc,
        rider=pair_exchange(group_c))
    fused["w_gu0"], fused["w_down0"] = scattered[0:2], scattered[2:4]
    sums_c = pair_sum("c", group_c, lands_c)

    (dx0, d_pool, d_scale, d_postmix0, d_premix0), _ = _mixa_bwd(
        dx1, x2d, y0, gains["pre_mix_g"], weight("pool_w"), weight("pool_scale"), gains["post_mix_g"])

    rows = [d_premix0, d_premix1, d_postmix0, d_postmix1, d_preffn0, d_preffn1, d_postffn0, d_postffn1,
            d_ple0, d_ple1, d_plepost0, d_plepost1, d_kvg, d_scale, d_sinks, loss_row]
    as2d = lambda a: a.reshape(1, D) if a.ndim == 1 else a
    layers_of = lambda src: [t for t in BIGS.values() if t.src == src]
    own_scatter = ["w_gu", "w_down"]
    early = own_scatter + ["w_q", "w_o", "w_kv"]
    late = ["w_ple_gate", "w_ple_proj"]
    by_cols = lambda srcs: [src == "w_down" for src in srcs]
    jobs = [_chip_sum_fused_job(layers_of(src), fused, by_cols=src == "w_down") for src in own_scatter]
    jobs += [_chip_sum_job(layers_of(src), landed) for src in early if src not in own_scatter]
    halves = [r[0] for r in _multi_call("chip_sum_early", jobs, kc)]
    (tot, g_pool), got = _small_all_reduce(
        rows, d_pool, rider=_both(scatter(group_c, sums_c), _share_rider(halves, by_cols(early))))
    keep(group_c, sums_c, got[:len(group_c)])
    full_grads = dict(zip(early, got[len(group_c):]))
    loss, small = _small_adamw(tot, kc, {nm: as2d(weights[nm]) for nm in SMALL_NAMES},
                               {nm: as2d(m_in[nm]) for nm in SMALL_NAMES},
                               {nm: as2d(v_in[nm]) for nm in SMALL_NAMES})

    halves = [r[0] for r in _multi_call("chip_sum_late", [_chip_sum_job(layers_of(src), landed) for src in late], kc)]
    full_grads.update(zip(late, _run("grads_pair_share", _share_rider(halves, by_cols(late)))))
    full_grads["pool_w"] = g_pool
    others = [src for src in BIG_SOURCES if src not in own_scatter and src != "pool_w"]

    def adam_args(src):
        return (layers_of(src)[0].rb, shard_view(src, weights[src]), full_grads[src],
                shard_view(src, m_in[src]), shard_view(src, v_in[src]))

    out = {"grad": {}, "delta": {}, "new_m": {}, "new_v": {}}
    results = {src: _adamw(src, *adam_args(src)) for src in own_scatter}
    rest_srcs = others + ["pool_w"]
    results.update(zip(rest_srcs, _multi_call("adamw_rest", [_adamw_job(*adam_args(src)) for src in rest_srcs], kc)))
    for src in BIG_SOURCES:
        shape = weights[src].shape
        for kind, a in zip(("grad", "delta", "new_m", "new_v"), results[src]):
            out[kind][src] = a.reshape(shape)
    for nm in SMALL_NAMES:
        shape = weights[nm].shape
        for kind, a in zip(("grad", "delta", "new_m", "new_v"), small[nm]):
            out[kind][nm] = a.reshape(shape)

    return (loss.reshape(()), dx0.reshape(x.shape),
            *[out["grad"][nm] for nm in order], *[out["delta"][nm] for nm in order],
            *[out["new_m"][nm] for nm in order], *[out["new_v"][nm] for nm in order])
```

```python
import collections

import jax
import jax.numpy as jnp
from jax import lax
from jax.experimental import pallas as pl
from jax.experimental.pallas import tpu as pltpu

D = 1024
FF = 2816
N_HEADS = 16
HEAD_DIM = 64
N_KV_HEADS = 4
GQA = N_HEADS // N_KV_HEADS
KVD = N_KV_HEADS * HEAD_DIM
PLE = 256
BLK = 128
WINDOWS = (2, 4, 8, 16)
POOL_G = 256
HALO = 16
EPS = 1e-6
NEG_INF = -1e30
ATT_SCALE = HEAD_DIM ** -0.5
SLOPES = tuple(2.0 ** (-8.0 * (h + 1) / N_HEADS) for h in range(N_HEADS))
N_CHIPS = 4
N_DEV = 8

LR, B1, B2, AEPS, WD, STEP = 0.001, 0.9, 0.999, 1e-08, 0.01, 10
BC1 = 1.0 - B1 ** STEP
BC2 = 1.0 - B2 ** STEP

BF = jnp.bfloat16
F32 = jnp.float32
MESH = pl.DeviceIdType.MESH
VMEM_LIMIT_V7X = 58 * 1024 * 1024
TM = 256
TM_FFN_BWD = 512
FF_CHUNK = 256
FF_HALF = FF // 2

VSPEC = pl.BlockSpec(memory_space=pltpu.VMEM)
SSPEC = pl.BlockSpec(memory_space=pltpu.SMEM)
ANYSPEC = pl.BlockSpec(memory_space=pl.ANY)


def _params(n_grid=0):
    sem = ("arbitrary",) * n_grid if n_grid else None
    return pltpu.CompilerParams(dimension_semantics=sem, vmem_limit_bytes=VMEM_LIMIT_V7X)


def _sds(shape, dtype=F32):
    return jax.ShapeDtypeStruct(tuple(shape), dtype)


Rider = collections.namedtuple("Rider", "arrays out_shapes aliases scratch start mid finish")
MID_NUM, MID_DEN = 5, 8


def _call(body, *, name, grid, in_specs, out_specs, out_shape, args, scratch_shapes=(), rider=None, prefetch=None):
    ni, no, ns = len(in_specs), len(out_specs), len(scratch_shapes)
    npre = 0 if prefetch is None else 1
    pre = [] if prefetch is None else [prefetch]
    if rider is None:
        rider = Rider([], [], {}, [], None, None, None)
    ri, ro = len(rider.arrays), len(rider.out_shapes)

    def full(*refs):
        pre_refs, refs = refs[:npre], refs[npre:]
        ins, refs = refs[:ni], refs[ni:]
        rins, refs = refs[:ri], refs[ri:]
        outs, refs = refs[:no], refs[no:]
        routs, refs = refs[:ro], refs[ro:]
        scr, rscr = refs[:ns], refs[ns:]
        ids = [pl.program_id(a) for a in range(len(grid))]
        first = ids[0] == 0
        last = ids[0] == grid[0] - 1
        for a in range(1, len(grid)):
            first = first & (ids[a] == 0)
            last = last & (ids[a] == grid[a] - 1)

        if rider.start is not None:
            @pl.when(first)
            def _():
                rider.start(rins, routs, rscr)

        if rider.mid is not None:
            assert len(grid) == 1

            @pl.when(ids[0] == (grid[0] * MID_NUM) // MID_DEN)
            def _():
                rider.mid(rins, routs, rscr)

        body(*pre_refs, *ins, *outs, *scr)

        if rider.finish is not None:
            @pl.when(last)
            def _():
                rider.finish(rins, routs, rscr)

    outs = pl.pallas_call(
        full, name=name,
        grid_spec=pltpu.PrefetchScalarGridSpec(
            num_scalar_prefetch=npre, grid=grid,
            in_specs=list(in_specs) + [ANYSPEC] * ri, out_specs=list(out_specs) + [ANYSPEC] * ro,
            scratch_shapes=list(scratch_shapes) + list(rider.scratch)),
        out_shape=list(out_shape) + list(rider.out_shapes),
        input_output_aliases={npre + ni + a: no + b for a, b in rider.aliases.items()},
        compiler_params=_params(len(grid)))(*pre, *args, *rider.arrays)
    return list(outs[:no]), list(outs[no:])


def _run(name, rider):
    ri = len(rider.arrays)

    def body(*refs):
        rins, routs, rscr = refs[:ri], refs[ri:ri + len(rider.out_shapes)], refs[ri + len(rider.out_shapes):]
        rider.start(rins, routs, rscr)
        if rider.mid is not None:
            rider.mid(rins, routs, rscr)
        rider.finish(rins, routs, rscr)

    return pl.pallas_call(
        body, name=name, in_specs=[ANYSPEC] * ri, out_specs=[ANYSPEC] * len(rider.out_shapes),
        out_shape=list(rider.out_shapes), scratch_shapes=list(rider.scratch),
        input_output_aliases=dict(rider.aliases), compiler_params=_params())(*rider.arrays)


Job = collections.namedtuple("Job", "steps ins outs fn")


def _multi_call(name, jobs, kc, rider=None):
    n = max(job.steps for job in jobs)

    def clamped(index, steps):
        return lambda s, kc_ref: index(jnp.minimum(s, steps - 1), kc_ref)

    in_specs, out_specs, out_shape, args = [], [], [], []
    for job in jobs:
        for arr, block, index, *single in job.ins:
            mode = dict(pipeline_mode=pl.Buffered(1)) if single and single[0] else {}
            in_specs.append(pl.BlockSpec(block, clamped(index, job.steps), **mode))
            args.append(arr)
        for sds, block, index in job.outs:
            out_specs.append(pl.BlockSpec(block, clamped(index, job.steps)))
            out_shape.append(sds)
    n_in = len(args)

    def body(kc_ref, *refs):
        s = pl.program_id(0)
        i0, o0 = 0, n_in
        for job in jobs:
            ins, outs = refs[i0:i0 + len(job.ins)], refs[o0:o0 + len(job.outs)]
            i0, o0 = i0 + len(job.ins), o0 + len(job.outs)

            @pl.when(s < job.steps)
            def _():
                job.fn(s, kc_ref, ins, outs)

    outs, routs = _call(body, name=name, grid=(n,), in_specs=in_specs, out_specs=out_specs, out_shape=out_shape,
                        args=args, prefetch=kc, rider=rider)
    res, o0 = [], 0
    for job in jobs:
        res.append(outs[o0:o0 + len(job.outs)])
        o0 += len(job.outs)
    return res if rider is None else (res, routs)


def _rms_fwd(x, g):
    r = lax.rsqrt(jnp.mean(x * x, axis=-1, keepdims=True) + EPS)
    return x * r * g


def _rms_bwd(x, g, dy):
    r = lax.rsqrt(jnp.mean(x * x, axis=-1, keepdims=True) + EPS)
    xn = x * r
    dxn = dy * g
    dx = r * (dxn - xn * jnp.mean(dxn * xn, axis=-1, keepdims=True))
    return dx, dy * xn


def _rowsum(a):
    return jnp.sum(a, axis=0, keepdims=True)


def _sigmoid(z):
    return 1.0 / (1.0 + jnp.exp(-z))


def _dot(a, b):
    return jnp.dot(a, b, preferred_element_type=F32)


def _dot_nt(a, b):
    return lax.dot_general(a, b, (((1,), (1,)), ((), ())), preferred_element_type=F32)


def _dot_tn(a, b):
    return lax.dot_general(a, b, (((0,), (0,)), ((), ())), preferred_element_type=F32)


def _row_spec(tm, width=D):
    return pl.BlockSpec((tm, width), lambda i: (i, 0))


def _const_spec(shape):
    zeros = (0,) * len(shape)
    return pl.BlockSpec(tuple(shape), lambda *_: zeros)


def _pool_delta(he, pos):
    out = []
    for gi, w in enumerate(WINDOWS):
        hg = he[:, gi * POOL_G:(gi + 1) * POOL_G]
        s = hg
        k = 1
        while k < w:
            s = s + pltpu.roll(s, k, 0)
            k *= 2
        cnt = jnp.maximum(jnp.minimum(pos + 1, w), 1).astype(F32)
        out.append(s / cnt - hg)
    return out


def _load_with_halo_before(x_ref, i, tm):
    r0 = pl.multiple_of(i * tm, tm)
    hs = pl.multiple_of(jnp.maximum(i * tm - HALO, 0), 8)
    xh = jnp.where(i > 0, x_ref[pl.ds(hs, HALO), :], 0.0)
    xt = x_ref[pl.ds(r0, tm), :]
    return xt, jnp.concatenate([xh, xt], axis=0)


def _mixa_fwd_job(x, pre_g, pool_w, pool_scale, post_g):
    s_len = x.shape[0]

    def fn(i, kc_ref, ins, outs):
        x_ref, pg_ref, w_ref, sc_ref, qg_ref = ins
        y_ref, x1_ref = outs
        xt, xe = _load_with_halo_before(x_ref, i, TM)
        he = _rms_fwd(xe, pg_ref[0:1, :])
        pos = i * TM - HALO + lax.broadcasted_iota(jnp.int32, (TM + HALO, 1), 0)
        ds = _pool_delta(he, pos)
        ys = [_dot(ds[gi][HALO:, :].astype(BF), w_ref[gi]) for gi in range(len(WINDOWS))]
        y = jnp.concatenate(ys, axis=1) * sc_ref[...]
        y_ref[...] = y
        x1_ref[...] = xt + _rms_fwd(y, qg_ref[0:1, :])

    def whole(a):
        zeros = (0,) * a.ndim
        return (a, a.shape, lambda j, kc_ref: zeros, True)

    rows = lambda j, kc_ref: (j, 0)
    return Job(s_len // TM, [whole(a) for a in (x, pre_g, pool_w, pool_scale, post_g)],
               [(_sds((s_len, D)), (TM, D), rows), (_sds((s_len, D)), (TM, D), rows)], fn)


def _mixa_bwd(dx1, x, y, pre_g, pool_w, pool_scale, post_g, rider=None):
    s_len = x.shape[0]
    n = s_len // TM
    ng = len(WINDOWS)

    def body(dx_ref, x_ref, y_ref, pg_ref, w_ref, sc_ref, qg_ref,
             dx0_ref, dw_ref, dsc_ref, dqg_ref, dpg_ref, wacc):
        i = pl.program_id(0)

        @pl.when(i == 0)
        def _():
            wacc[...] = jnp.zeros_like(wacc)
            dsc_ref[...] = jnp.zeros_like(dsc_ref)
            dqg_ref[...] = jnp.zeros_like(dqg_ref)
            dpg_ref[...] = jnp.zeros_like(dpg_ref)

        r0 = pl.multiple_of(i * TM, TM)
        xt, xe = _load_with_halo_before(x_ref, i, TM)
        he = _rms_fwd(xe, pg_ref[0:1, :])
        pos_b = i * TM - HALO + lax.broadcasted_iota(jnp.int32, (TM + HALO, 1), 0)
        ds = _pool_delta(he, pos_b)

        last = i == n - 1
        a0 = pl.multiple_of(jnp.minimum(i * TM + TM, s_len - HALO), 8)
        ye = jnp.concatenate([y_ref[pl.ds(r0, TM), :], y_ref[pl.ds(a0, HALO), :]], axis=0)
        dt = dx_ref[pl.ds(r0, TM), :]
        de = jnp.concatenate([dt, jnp.where(last, 0.0, dx_ref[pl.ds(a0, HALO), :])], axis=0)
        dye, prod = _rms_bwd(ye, qg_ref[0:1, :], de)
        dqg_ref[...] += _rowsum(prod[:TM, :])
        dys = dye * sc_ref[...]
        pos_a = i * TM + lax.broadcasted_iota(jnp.int32, (TM + HALO, 1), 0)

        dhs, dscs = [], []
        for gi, w in enumerate(WINDOWS):
            sl = slice(gi * POOL_G, (gi + 1) * POOL_G)
            wg = w_ref[gi]
            dys_g = dys[:, sl].astype(BF)
            d_g = ds[gi][HALO:, :].astype(BF)
            ypre = _dot(d_g, wg)
            dscs.append(_rowsum(dye[:TM, sl] * ypre))
            wacc[gi] += _dot_tn(d_g, dys_g[:TM, :])
            dd = _dot_nt(dys_g, wg)
            cnt = jnp.minimum(pos_a + 1, w).astype(F32)
            a = dd / cnt
            k = 1
            while k < w:
                a = a + pltpu.roll(a, TM + HALO - k, 0)
                k *= 2
            dhs.append(a[:TM, :] - dd[:TM, :])
        dsc_ref[...] += jnp.concatenate(dscs, axis=1)
        dh = jnp.concatenate(dhs, axis=1)
        dxp, prod2 = _rms_bwd(xt, pg_ref[0:1, :], dh)
        dpg_ref[...] += _rowsum(prod2)
        dx0_ref[...] = dt + dxp

        @pl.when(last)
        def _():
            dw_ref[...] = wacc[...].astype(BF)

    return _call(
        body, name="mixa_bwd", grid=(n,), in_specs=[VSPEC] * 7,
        out_specs=[_row_spec(TM), _const_spec((ng, POOL_G, POOL_G)), _const_spec((1, D)),
                   _const_spec((1, D)), _const_spec((1, D))],
        out_shape=[_sds((s_len, D)), _sds((ng, POOL_G, POOL_G), BF), _sds((1, D)), _sds((1, D)), _sds((1, D))],
        scratch_shapes=[pltpu.VMEM((ng, POOL_G, POOL_G), F32)],
        args=[dx1, x, y, pre_g, pool_w, pool_scale, post_g], rider=rider)


def _ffn_fwd(layer, x1, pre_g, wgu, wd, post_g, rider=None):
    s_len = x1.shape[0]

    def body(x_ref, pg_ref, wgu_ref, wd_ref, qg_ref, f_ref, x2_ref, g_ref, u_ref):
        x = x_ref[...]
        h = _rms_fwd(x, pg_ref[layer:layer + 1, :]).astype(BF)
        f = jnp.zeros((TM, D), F32)
        for c in range(FF // FF_HALF):
            cols = slice(c * FF_HALF, (c + 1) * FF_HALF)
            g = _dot(h, wgu_ref[0, :, cols])
            u = _dot(h, wgu_ref[1, :, cols])
            g_ref[:, cols] = g.astype(BF)
            u_ref[:, cols] = u.astype(BF)
            act = g * _sigmoid(g) * u
            f = f + _dot(act.astype(BF), wd_ref[cols, :])
        f_ref[...] = f
        x2_ref[...] = x + _rms_fwd(f, qg_ref[layer:layer + 1, :])

    return _call(body, name=f"ffn_fwd{layer}", grid=(s_len // TM,),
                 in_specs=[_row_spec(TM), VSPEC, VSPEC, VSPEC, VSPEC],
                 out_specs=[_row_spec(TM), _row_spec(TM), _row_spec(TM, FF), _row_spec(TM, FF)],
                 out_shape=[_sds((s_len, D)), _sds((s_len, D)), _sds((s_len, FF), BF), _sds((s_len, FF), BF)],
                 args=[x1, pre_g, wgu, wd, post_g], rider=rider)


GU_PIECE = 128
DN_PIECE = 64
DN_SLOT = FF // N_CHIPS
HALF_D = D // 2


def _ffn_bwd(layer, dx2, x1, f, g_pre, u_pre, pre_g, wgu, wd, post_g, kc, rider=None):
    s_len = x1.shape[0]
    tm = TM_FFN_BWD
    n = s_len // tm
    nc = FF // FF_CHUNK
    n_gu, n_dn = FF_CHUNK // GU_PIECE, FF_CHUNK // DN_PIECE
    n_pieces = 2 * n_gu + n_dn
    n_blk = FF_HALF // GU_PIECE

    def edge_rows(c, i, kc_ref):
        return (jnp.where((c == 0) | (c == nc - 1), i, n - 1), 0)

    def chunk_at(c, kc_ref):
        return (c + (((kc_ref[0] + 1) % N_CHIPS) * nc) // N_CHIPS) % nc

    def exchange(kc_ref, c, accg, accu, accd, own_gu_ref, land_gu_ref, own_dn_ref, land_dn_ref,
                 pl_gu, pl_dn, sib_gu, sib_dn, mine_gu, mine_dn, sum_gu, sum_dn,
                 psend, precv, ssend, lsem, rrecv):
        x, y, core = lax.axis_index("x"), lax.axis_index("y"), lax.axis_index("c")
        lower = core == 0

        def pair_copy(cc, part):
            p = cc % 2
            src, dst = ((sib_gu, pl_gu), (sib_dn, pl_dn))[part]
            return pltpu.make_async_remote_copy(src.at[p], dst.at[cc], psend.at[p, part], precv.at[cc, part],
                                                device_id=(x, y, 1 - core), device_id_type=MESH)

        def scatter(cc, wait):
            p = cc % 2
            hidden = chunk_at(cc, kc_ref) * FF_CHUNK

            assert n_gu == 2
            k0, k1 = hidden // FF_HALF, (hidden + GU_PIECE) // FF_HALF
            blk = (hidden - k0 * FF_HALF) // GU_PIECE
            for gu in range(2):
                @pl.when(k0 == k1)
                def _():
                    piece(p, wait, 2 * gu, sum_gu.at[p, gu], k0 + 2 * gu, 0, (pl.ds(blk, 2),))

                @pl.when(k0 != k1)
                def _():
                    piece(p, wait, 2 * gu, sum_gu.at[p, gu, 0], k0 + 2 * gu, 0, (blk,))
                    piece(p, wait, 2 * gu + 1, sum_gu.at[p, gu, 1], k1 + 2 * gu, 0, (0,))

            kd = hidden // DN_SLOT
            off = pl.multiple_of(hidden - kd * DN_SLOT, DN_PIECE)
            m = jnp.minimum((DN_SLOT - off) // DN_PIECE, n_dn)
            for mm in range(1, n_dn + 1):
                @pl.when(m == mm)
                def _():
                    rows = mm * DN_PIECE
                    piece(p, wait, 2 * n_gu, sum_dn.at[p, pl.ds(0, rows), :], kd, 1, (pl.ds(off, rows), slice(None)))
                    if mm < n_dn:
                        piece(p, wait, 2 * n_gu + 1, sum_dn.at[p, pl.ds(rows, FF_CHUNK - rows), :], kd + 1, 1,
                              (pl.ds(0, FF_CHUNK - rows), slice(None)))

        def piece(p, wait, pi, src, k, t, where):
            own_ref, land_ref = ((own_gu_ref, land_gu_ref), (own_dn_ref, land_dn_ref))[t]
            kx, ky = k // 2, k % 2
            fx, fy = (kx != x).astype(jnp.int32), (ky != y).astype(jnp.int32)
            local = (fx + fy) == 0
            j = jnp.maximum(fx + 2 * fy - 1, 0)

            @pl.when(local)
            def _():
                cp = pltpu.make_async_copy(src, own_ref.at[where], lsem.at[p, pi])
                if wait:
                    cp.wait()
                else:
                    cp.start()

            @pl.when(jnp.logical_not(local))
            def _():
                cp = pltpu.make_async_remote_copy(src, land_ref.at[(j,) + where], ssend.at[p, pi],
                                                  rrecv.at[t, j], device_id=(kx, ky, core), device_id_type=MESH)
                if wait:
                    cp.wait_send()
                else:
                    cp.start()

        def add_and_scatter(cc):
            p = cc % 2
            pair_copy(cc, 0).wait_recv()
            pair_copy(cc, 1).wait_recv()
            s_gu = (mine_gu[...] + pl_gu[cc].astype(F32)).astype(BF)
            for hc in range(n_gu):
                sum_gu[p, :, hc] = s_gu[:, :, hc * GU_PIECE:(hc + 1) * GU_PIECE]
            sum_dn[p] = (mine_dn[...] + pl_dn[cc].astype(F32)).astype(BF)
            scatter(cc, wait=False)

        @pl.when(c >= 1)
        def _():
            @pl.when(c >= 3)
            def _():
                scatter(c - 3, wait=True)
            add_and_scatter(c - 1)

        @pl.when(c >= 2)
        def _():
            pair_copy(c - 2, 0).wait_send()
            pair_copy(c - 2, 1).wait_send()

        p = c % 2
        my_rows = pl.ds(pl.multiple_of(core * HALF_D, HALF_D), HALF_D)
        sib_rows = pl.ds(pl.multiple_of((1 - core) * HALF_D, HALF_D), HALF_D)
        d_v = accd[...]
        sib_gu[p, 0] = accg[sib_rows, :].astype(BF)
        sib_gu[p, 1] = accu[sib_rows, :].astype(BF)
        sib_dn[p] = jnp.where(lower, d_v[:, HALF_D:], d_v[:, :HALF_D]).astype(BF)
        mine_gu[0] = accg[my_rows, :]
        mine_gu[1] = accu[my_rows, :]
        mine_dn[...] = jnp.where(lower, d_v[:, :HALF_D], d_v[:, HALF_D:])
        pair_copy(c, 0).start()
        pair_copy(c, 1).start()

        @pl.when(c == nc - 1)
        def _():
            scatter(nc - 3, wait=True)
            add_and_scatter(nc - 1)
            for cc in (nc - 2, nc - 1):
                pair_copy(cc, 0).wait_send()
                pair_copy(cc, 1).wait_send()
                scatter(cc, wait=True)
            for t, land_ref in enumerate((land_gu_ref, land_dn_ref)):
                for j in range(N_CHIPS - 1):
                    pltpu.make_async_remote_copy(land_ref.at[j], land_ref.at[j], ssend.at[0, 0], rrecv.at[t, j],
                                                 device_id=(x, y, core), device_id_type=MESH).wait_recv()

    def body(kc_ref, dx_ref, x_ref, f_ref, gp_ref, up_ref, pg_ref, wgu_ref, wd_ref, qg_ref,
             dx1_ref, dpg_ref, dqg_ref, own_gu_ref, land_gu_ref, own_dn_ref, land_dn_ref,
             h_s, df_s, dh_s, accg, accu, accd, *comm):
        c = pl.program_id(0)
        i = pl.program_id(1)
        rows = pl.ds(pl.multiple_of(i * tm, tm), tm)
        pg = pg_ref[layer:layer + 1, :]

        @pl.when((c == 0) & (i == 0))
        def _():
            dpg_ref[...] = jnp.zeros_like(dpg_ref)
            dqg_ref[...] = jnp.zeros_like(dqg_ref)

        @pl.when(c == 0)
        def _():
            h_s[rows, :] = _rms_fwd(x_ref[...], pg).astype(BF)
            df, prod = _rms_bwd(f_ref[...], qg_ref[layer:layer + 1, :], dx_ref[...])
            df_s[rows, :] = df.astype(BF)
            dqg_ref[...] += _rowsum(prod)

        @pl.when(i == 0)
        def _():
            accg[...] = jnp.zeros_like(accg)
            accu[...] = jnp.zeros_like(accu)
            accd[...] = jnp.zeros_like(accd)

        h = h_s[rows, :]
        df = df_s[rows, :]
        wg = wgu_ref[0]
        wu = wgu_ref[1]
        g = gp_ref[...].astype(F32)
        u = up_ref[...].astype(F32)
        sg = _sigmoid(g)
        a = g * sg
        dact = _dot_nt(df, wd_ref[...])
        accd[...] += _dot_tn((a * u).astype(BF), df)
        du = (dact * a).astype(BF)
        dg = (dact * u * (sg * (1.0 + g * (1.0 - sg)))).astype(BF)
        accg[...] += _dot_tn(h, dg)
        accu[...] += _dot_tn(h, du)
        dh = _dot_nt(dg, wg) + _dot_nt(du, wu)

        @pl.when(c == 0)
        def _():
            dh_s[rows, :] = dh

        @pl.when((c > 0) & (c < nc - 1))
        def _():
            dh_s[rows, :] += dh

        @pl.when(c == nc - 1)
        def _():
            dxp, prod = _rms_bwd(x_ref[...], pg, dh_s[rows, :] + dh)
            dpg_ref[...] += _rowsum(prod)
            dx1_ref[...] = dx_ref[...] + dxp

        @pl.when(i == n - 1)
        def _():
            exchange(kc_ref, c, accg, accu, accd, own_gu_ref, land_gu_ref, own_dn_ref, land_dn_ref, *comm)

    dma = pltpu.SemaphoreType.DMA
    return _call(
        body, name=f"ffn_bwd{layer}", grid=(nc, n),
        in_specs=[pl.BlockSpec((tm, D), edge_rows), pl.BlockSpec((tm, D), edge_rows),
                  pl.BlockSpec((tm, D), lambda c, i, kc_ref: (jnp.where(c == 0, i, n - 1), 0),
                               pipeline_mode=pl.Buffered(1)),
                  pl.BlockSpec((tm, FF_CHUNK), lambda c, i, kc_ref: (i, chunk_at(c, kc_ref))),
                  pl.BlockSpec((tm, FF_CHUNK), lambda c, i, kc_ref: (i, chunk_at(c, kc_ref))),
                  VSPEC,
                  pl.BlockSpec((2, D, FF_CHUNK), lambda c, i, kc_ref: (0, 0, chunk_at(c, kc_ref))),
                  pl.BlockSpec((FF_CHUNK, D), lambda c, i, kc_ref: (chunk_at(c, kc_ref), 0)),
                  VSPEC],
        out_specs=[pl.BlockSpec((tm, D), lambda c, i, kc_ref: (jnp.where(c == nc - 1, i, 0), 0)),
                   _const_spec((1, D)), _const_spec((1, D)), ANYSPEC, ANYSPEC, ANYSPEC, ANYSPEC],
        out_shape=[_sds((s_len, D)), _sds((1, D)), _sds((1, D)),
                   _sds((n_blk, HALF_D, GU_PIECE), BF), _sds((N_CHIPS - 1, n_blk, HALF_D, GU_PIECE), BF),
                   _sds((DN_SLOT, HALF_D), BF), _sds((N_CHIPS - 1, DN_SLOT, HALF_D), BF)],
        scratch_shapes=[pltpu.VMEM((s_len, D), BF), pltpu.VMEM((s_len, D), BF), pltpu.VMEM((s_len, D), F32),
                        pltpu.VMEM((D, FF_CHUNK), F32), pltpu.VMEM((D, FF_CHUNK), F32),
                        pltpu.VMEM((FF_CHUNK, D), F32),
                        pltpu.VMEM((nc, 2, HALF_D, FF_CHUNK), BF), pltpu.VMEM((nc, FF_CHUNK, HALF_D), BF),
                        pltpu.VMEM((2, 2, HALF_D, FF_CHUNK), BF), pltpu.VMEM((2, FF_CHUNK, HALF_D), BF),
                        pltpu.VMEM((2, HALF_D, FF_CHUNK), F32), pltpu.VMEM((FF_CHUNK, HALF_D), F32),
                        pltpu.VMEM((2, 2, n_gu, HALF_D, GU_PIECE), BF), pltpu.VMEM((2, FF_CHUNK, HALF_D), BF),
                        dma((2, 2)), dma((nc, 2)), dma((2, n_pieces)), dma((2, n_pieces)), dma((2, N_CHIPS - 1))],
        args=[dx2, x1, f, g_pre, u_pre, pre_g, wgu, wd, post_g], rider=rider, prefetch=kc)


def _ple_fwd(layer, x2, p, ple_g, w_gate, w_proj, post_g, target=None, qkv=None, rider=None):
    s_len = x2.shape[0]
    final = target is not None
    assert not (final and qkv)

    def body(*refs):
        if final:
            x_ref, p_ref, g_ref, wg_ref, wp_ref, qg_ref, t_ref, z_ref, pe_ref, dx_ref, lv_ref = refs
        elif qkv:
            (x_ref, p_ref, g_ref, wg_ref, wp_ref, qg_ref, ng_ref, kg_ref, wq_ref, wkv_ref,
             z_ref, pe_ref, x3_ref, q_ref, kv_ref) = refs
        else:
            x_ref, p_ref, g_ref, wg_ref, wp_ref, qg_ref, z_ref, pe_ref, x3_ref = refs
        x = x_ref[...]
        r = _rms_fwd(x, g_ref[layer:layer + 1, :]).astype(BF)
        z = _dot(r, wg_ref[...])
        pe = _dot(p_ref[...].astype(BF), wp_ref[...])
        z_ref[...] = z
        pe_ref[...] = pe
        x3 = x + _rms_fwd(pe * _sigmoid(z), qg_ref[layer:layer + 1, :])
        if final:
            @pl.when(pl.program_id(0) == 0)
            def _():
                lv_ref[...] = jnp.zeros_like(lv_ref)
            err = x3 - t_ref[...]
            dx_ref[...] = err * (1.0 / D)
            lv_ref[...] += _rowsum(err * err)
        else:
            x3_ref[...] = x3
        if qkv:
            q_ref[...] = _dot(_rms_fwd(x3, ng_ref[layer + 1:layer + 2, :]).astype(BF), wq_ref[...]).astype(BF)
            kv_ref[...] = _dot(_rms_fwd(x3, kg_ref[...]).astype(BF), wkv_ref[...]).astype(BF)

    p_spec = pl.BlockSpec((None, TM, PLE), lambda i: (layer, i, 0))
    in_specs = [_row_spec(TM), p_spec, VSPEC, VSPEC, VSPEC, VSPEC]
    args = [x2, p, ple_g, w_gate, w_proj, post_g]
    out_specs = [_row_spec(TM), _row_spec(TM), _row_spec(TM)]
    out_shape = [_sds((s_len, D))] * 3
    if qkv:
        in_specs += [VSPEC] * 4
        args += list(qkv)
        out_specs += [_row_spec(TM), _row_spec(TM, 2 * KVD)]
        out_shape += [_sds((s_len, D), BF), _sds((s_len, 2 * KVD), BF)]
    if final:
        in_specs.append(_row_spec(TM))
        args.append(target)
        out_specs.append(_const_spec((1, D)))
        out_shape.append(_sds((1, D)))
    return _call(body, name=f"ple_fwd{layer}", grid=(s_len // TM,), in_specs=in_specs, out_specs=out_specs,
                 out_shape=out_shape, args=args, rider=rider)


def _ple_bwd(layer, dx3, x2, z, pe, p, ple_g, w_gate, post_g, rider=None):
    s_len = x2.shape[0]
    n = s_len // TM

    def body(dx_ref, x_ref, z_ref, pe_ref, p_ref, g_ref, wg_ref, qg_ref,
             dx2_ref, dwg_ref, dwp_ref, dg_ref, dqg_ref, gacc, pacc):
        i = pl.program_id(0)

        @pl.when(i == 0)
        def _():
            gacc[...] = jnp.zeros_like(gacc)
            pacc[...] = jnp.zeros_like(pacc)
            dg_ref[...] = jnp.zeros_like(dg_ref)
            dqg_ref[...] = jnp.zeros_like(dqg_ref)

        dx = dx_ref[...]
        x = x_ref[...]
        pe_v = pe_ref[...]
        gate = _sigmoid(z_ref[...])
        de, prod = _rms_bwd(pe_v * gate, qg_ref[layer:layer + 1, :], dx)
        dqg_ref[...] += _rowsum(prod)
        dpe = (de * gate).astype(BF)
        dz = (de * pe_v * gate * (1.0 - gate)).astype(BF)
        pacc[...] += _dot_tn(p_ref[...].astype(BF), dpe)
        g = g_ref[layer:layer + 1, :]
        r = _rms_fwd(x, g).astype(BF)
        gacc[...] += _dot_tn(r, dz)
        dr = _dot_nt(dz, wg_ref[...])
        dxp, prod2 = _rms_bwd(x, g, dr)
        dg_ref[...] += _rowsum(prod2)
        dx2_ref[...] = dx + dxp

        @pl.when(i == n - 1)
        def _():
            dwg_ref[...] = gacc[...].astype(BF)
            dwp_ref[...] = pacc[...].astype(BF)

    p_spec = pl.BlockSpec((None, TM, PLE), lambda i: (layer, i, 0))
    return _call(
        body, name=f"ple_bwd{layer}", grid=(n,),
        in_specs=[_row_spec(TM), _row_spec(TM), _row_spec(TM), _row_spec(TM), p_spec, VSPEC, VSPEC, VSPEC],
        out_specs=[_row_spec(TM), _const_spec((D, D)), _const_spec((PLE, D)), _const_spec((1, D)), _const_spec((1, D))],
        out_shape=[_sds((s_len, D)), _sds((D, D), BF), _sds((PLE, D), BF), _sds((1, D)), _sds((1, D))],
        scratch_shapes=[pltpu.VMEM((D, D), F32), pltpu.VMEM((PLE, D), F32)],
        args=[dx3, x2, z, pe, p, ple_g, w_gate, post_g], rider=rider)


def _qkv_bwd(dq, dkv, x3, dx4, q_g, kv_g, w_q, w_kv):
    s_len = x3.shape[0]
    n = s_len // TM

    def body(dq_ref, dkv_ref, x_ref, dx_ref, qg_ref, kg_ref, wq_ref, wkv_ref,
             dx3_ref, dwq_ref, dwkv_ref, dqg_ref, dkg_ref, qacc, kacc):
        i = pl.program_id(0)

        @pl.when(i == 0)
        def _():
            qacc[...] = jnp.zeros_like(qacc)
            kacc[...] = jnp.zeros_like(kacc)
            dqg_ref[...] = jnp.zeros_like(dqg_ref)
            dkg_ref[...] = jnp.zeros_like(dkg_ref)

        x = x_ref[...]
        qg = qg_ref[1:2, :]
        kg = kg_ref[...]
        dq_v = dq_ref[...]
        dkv_v = dkv_ref[...].astype(BF)
        qacc[...] += _dot_tn(_rms_fwd(x, qg).astype(BF), dq_v)
        kacc[...] += _dot_tn(_rms_fwd(x, kg).astype(BF), dkv_v)
        dxq, prod_q = _rms_bwd(x, qg, _dot_nt(dq_v, wq_ref[...]))
        dxk, prod_k = _rms_bwd(x, kg, _dot_nt(dkv_v, wkv_ref[...]))
        dqg_ref[...] += _rowsum(prod_q)
        dkg_ref[...] += _rowsum(prod_k)
        dx3_ref[...] = dx_ref[...] + dxq + dxk

        @pl.when(i == n - 1)
        def _():
            dwq_ref[...] = qacc[...].astype(BF)
            dwkv_ref[...] = kacc[...].astype(BF)

    outs, _ = _call(
        body, name="qkv_bwd", grid=(n,),
        in_specs=[_row_spec(TM), _row_spec(TM, 2 * KVD), _row_spec(TM), _row_spec(TM), VSPEC, VSPEC, VSPEC, VSPEC],
        out_specs=[_row_spec(TM), _const_spec((D, D)), _const_spec((D, 2 * KVD)),
                   _const_spec((1, D)), _const_spec((1, D))],
        out_shape=[_sds((s_len, D)), _sds((D, D), BF), _sds((D, 2 * KVD), BF), _sds((1, D)), _sds((1, D))],
        scratch_shapes=[pltpu.VMEM((D, D), F32), pltpu.VMEM((D, 2 * KVD), F32)],
        args=[dq, dkv, x3, dx4, q_g, kv_g, w_q, w_kv])
    return outs


def _attn_group(i, q, kvw, sink_ref, g):
    rows = GQA * BLK
    heads = [GQA * g + j for j in range(GQA)]
    off = jnp.where(i > 0, BLK, 0)
    row = lax.broadcasted_iota(jnp.int32, (rows, 2 * BLK), 0)
    rel = (row % BLK) - lax.broadcasted_iota(jnp.int32, (rows, 2 * BLK), 1) + off
    valid = (rel >= 0) & (rel < BLK)
    head_of_row = lax.broadcasted_iota(jnp.int32, (rows, 1), 0) // BLK
    slope = jnp.zeros((rows, 1), F32)
    sink = jnp.zeros((rows, 1), F32)
    for j, h in enumerate(heads):
        slope = jnp.where(head_of_row == j, SLOPES[h], slope)
        sink = jnp.where(head_of_row == j, sink_ref[0, h], sink)
    qs = jnp.concatenate([q[:, h * HEAD_DIM:(h + 1) * HEAD_DIM] for h in heads], axis=0)
    k = kvw[:, g * HEAD_DIM:(g + 1) * HEAD_DIM]
    v = kvw[:, KVD + g * HEAD_DIM:KVD + (g + 1) * HEAD_DIM]
    s = _dot_nt(qs, k) * ATT_SCALE - slope * rel.astype(F32)
    s = jnp.where(valid, s, NEG_INF)
    m = jnp.maximum(jnp.max(s, axis=-1, keepdims=True), sink)
    e = jnp.exp(s - m)
    es = jnp.exp(sink - m)
    inv = 1.0 / (jnp.sum(e, axis=-1, keepdims=True) + es)
    return e * inv, es * inv, qs, k, v


def _unstack_heads(stacked):
    return [stacked[j * BLK:(j + 1) * BLK, :] for j in range(GQA)]


def _kv_window(kv_ref, i):
    ks = pl.multiple_of(jnp.maximum(i * BLK - BLK, 0), BLK)
    return ks, kv_ref[pl.ds(ks, 2 * BLK), :]


def _attn_fwd(q, kv, sinks, x3, w_o, post_g, rider=None):
    s_len = q.shape[0]

    def body(q_ref, kv_ref, sk_ref, x_ref, wo_ref, g_ref, a_ref, y_ref, x4_ref):
        i = pl.program_id(0)
        _, kvw = _kv_window(kv_ref, i)
        q = q_ref[...]
        outs = []
        for g in range(N_KV_HEADS):
            p, _, _, _, v = _attn_group(i, q, kvw, sk_ref, g)
            outs += _unstack_heads(_dot(p.astype(BF), v))
        attn = jnp.concatenate(outs, axis=1)
        a_ref[...] = attn
        y = _dot(attn.astype(BF), wo_ref[...])
        y_ref[...] = y
        x4_ref[...] = x_ref[...] + _rms_fwd(y, g_ref[1:2, :])

    return _call(body, name="attn_fwd", grid=(s_len // BLK,),
                 in_specs=[_row_spec(BLK), VSPEC, SSPEC, _row_spec(BLK), VSPEC, VSPEC],
                 out_specs=[_row_spec(BLK)] * 3, out_shape=[_sds((s_len, D))] * 3,
                 args=[q, kv, sinks, x3, w_o, post_g], rider=rider)


ATT_STEP_BLOCKS = 2


def _attn_bwd(dx4, y, attn, q, kv, sinks, w_o, post_g, rider=None):
    s_len = q.shape[0]
    rows = ATT_STEP_BLOCKS * BLK
    n = s_len // rows

    def body(dx_ref, y_ref, a_ref, q_ref, kv_ref, sk_ref, wo_ref, g_ref,
             dq_ref, dkv_ref, dwo_ref, dg_ref, dsk_ref, wacc):
        i = pl.program_id(0)

        @pl.when(i == 0)
        def _():
            dkv_ref[...] = jnp.zeros_like(dkv_ref)
            wacc[...] = jnp.zeros_like(wacc)
            dg_ref[...] = jnp.zeros_like(dg_ref)
            dsk_ref[...] = jnp.zeros_like(dsk_ref)

        dy, prod = _rms_bwd(y_ref[...], g_ref[1:2, :], dx_ref[...])
        dg_ref[...] += _rowsum(prod)
        dyb = dy.astype(BF)
        attn_all = a_ref[...]
        wacc[...] += _dot_tn(attn_all.astype(BF), dyb)
        d_o_all = _dot_nt(dyb, wo_ref[...])
        q_all = q_ref[...]
        lane = lax.broadcasted_iota(jnp.int32, (1, D), 1)
        dsk = jnp.zeros((1, D), F32)
        for sub in range(ATT_STEP_BLOCKS):
            blk = i * ATT_STEP_BLOCKS + sub
            sl = slice(sub * BLK, (sub + 1) * BLK)
            d_o, q = d_o_all[sl, :], q_all[sl, :]
            dod = d_o * attn_all[sl, :]
            ks, kvw = _kv_window(kv_ref, blk)
            dqs, dks, dvs = [], [], []
            for g in range(N_KV_HEADS):
                p, ps, qs, k, v = _attn_group(blk, q, kvw, sk_ref, g)
                cols = [slice((GQA * g + j) * HEAD_DIM, (GQA * g + j + 1) * HEAD_DIM) for j in range(GQA)]
                do_s = jnp.concatenate([d_o[:, c] for c in cols], axis=0).astype(BF)
                dsum = jnp.concatenate([jnp.sum(dod[:, c], axis=-1, keepdims=True) for c in cols], axis=0)
                dp = _dot_nt(do_s, v)
                dsb = (p * (dp - dsum) * ATT_SCALE).astype(BF)
                sink_part = ps * dsum
                for j in range(GQA):
                    dsk = dsk + jnp.where(lane == GQA * g + j, -_rowsum(sink_part[j * BLK:(j + 1) * BLK, :]), 0.0)
                dqs += _unstack_heads(_dot(dsb, k))
                dks.append(_dot_tn(dsb, qs))
                dvs.append(_dot_tn(p.astype(BF), do_s))
            dq_ref[sl, :] = jnp.concatenate(dqs, axis=1).astype(BF)
            dkv_ref[pl.ds(ks, 2 * BLK), :] += jnp.concatenate(dks + dvs, axis=1)
        dsk_ref[...] += dsk

        @pl.when(i == n - 1)
        def _():
            dwo_ref[...] = wacc[...].astype(BF)

    return _call(
        body, name="attn_bwd", grid=(n,),
        in_specs=[_row_spec(rows), _row_spec(rows), _row_spec(rows), _row_spec(rows), VSPEC, SSPEC, VSPEC, VSPEC],
        out_specs=[_row_spec(rows), _const_spec((s_len, 2 * KVD)), _const_spec((D, D)),
                   _const_spec((1, D)), _const_spec((1, D))],
        out_shape=[_sds((s_len, D), BF), _sds((s_len, 2 * KVD)), _sds((D, D), BF), _sds((1, D)), _sds((1, D))],
        scratch_shapes=[pltpu.VMEM((D, D), F32)],
        args=[dx4, y, attn, q, kv, sinks, w_o, post_g], rider=rider)


Big = collections.namedtuple("Big", "name src layer L A R C rb")


def _bigs():
    out = {"pool_w": Big("pool_w", "pool_w", None, 4, 4, POOL_G // N_CHIPS, POOL_G, 32)}
    for l in range(2):
        out[f"w_gu{l}"] = Big(f"w_gu{l}", "w_gu", l, 1, 2, D, FF_HALF, 256)
        out[f"w_down{l}"] = Big(f"w_down{l}", "w_down", l, 1, 4, FF // N_CHIPS, D, 352)
        out[f"w_ple_gate{l}"] = Big(f"w_ple_gate{l}", "w_ple_gate", l, 1, 4, D // N_CHIPS, D, 128)
        out[f"w_ple_proj{l}"] = Big(f"w_ple_proj{l}", "w_ple_proj", l, 1, 1, PLE, D // N_CHIPS, 128)
    out["w_q"] = Big("w_q", "w_q", None, 1, 4, D // N_CHIPS, D, 128)
    out["w_o"] = Big("w_o", "w_o", None, 1, 4, D // N_CHIPS, D, 128)
    out["w_kv"] = Big("w_kv", "w_kv", None, 1, 4, D // N_CHIPS, 2 * KVD, 128)
    return out


BIGS = _bigs()
POOL_SCALE = Big("pool_scale", "pool_scale", None, 1, 1, 1, D // N_CHIPS, 1)
BIG_SOURCES = ("w_gu", "w_down", "w_ple_gate", "w_ple_proj", "w_q", "w_o", "w_kv", "pool_w")


def _ncb(t):
    return N_CHIPS // t.A


def _full_shape(t, rows=None):
    return (t.L, t.A, t.R if rows is None else rows, _ncb(t) * t.C)


def _slot_index(t, k):
    return k // _ncb(t), k % _ncb(t)


def _slot(ref, t, k, row0, rows):
    a, cb = _slot_index(t, k)
    return ref.at[:, a, pl.ds(row0, rows), pl.ds(pl.multiple_of(cb * t.C, 128), t.C)]


def _place_job(t, w, out_dtype=BF):
    nb = next((nb for nb in (8, 4, 2, 1) if t.R % (16 * nb) == 0), 1) if t.L == 1 else 1
    rb = t.R // nb

    def fn(j, kc_ref, ins, outs):
        outs[0][...] = ins[0][...].astype(out_dtype)

    def in_map(j, kc_ref):
        return (j // nb if t.layer is None else t.layer, j % nb, 0)

    def out_map(j, kc_ref):
        a, cb = _slot_index(t, kc_ref[0])
        return (j // nb, a, j % nb, cb)

    return Job(t.L * nb, [(w, (None, rb, t.C), in_map)],
               [(_sds(_full_shape(t), out_dtype), (None, None, rb, t.C), out_map)], fn)


def _mesh_position():
    x, y, c = lax.axis_index("x"), lax.axis_index("y"), lax.axis_index("c")
    chips = [(1 - x, y), (x, 1 - y), (1 - x, 1 - y)]
    return x, y, c, chips


DIRECT_BELOW = 1024


def _gather_rider(parts, fulls):
    nt = len(parts)
    TO_X, TO_Y, FWD_X, FWD_Y, SIB_X, SIB_Y, SIB_D = range(7)

    def rows_of(ti, core):
        t, r0, r1 = parts[ti]
        h = (r1 - r0) // 2
        return r0 + core * h, h

    def copy(outs, sems, kind, ti, k_src, row0, rows, dev):
        region = _slot(outs[ti], parts[ti][0], k_src, row0, rows)
        return pltpu.make_async_remote_copy(region, region, sems[0].at[ti, kind], sems[1].at[ti, kind],
                                            device_id=dev, device_id_type=MESH)

    def plan(outs, sems):
        x, y, c, _ = _mesh_position()
        me, kx, ky, kd = 2 * x + y, 2 * (1 - x) + y, 2 * x + (1 - y), 2 * (1 - x) + (1 - y)
        dev_x, dev_y, dev_d, sib = (1 - x, y, c), (x, 1 - y, c), (1 - x, 1 - y, c), (x, y, 1 - c)

        def whole(ti):
            return 0, parts[ti][0].R

        def mk(kind, k_send, k_recv, dev, send_rows, recv_rows):
            def build(ti, side):
                k_src = k_send if side == "s" else k_recv
                row0, rows = (send_rows if side == "s" else recv_rows)(ti)
                return copy(outs, sems, kind, ti, k_src, row0, rows, dev)
            return build

        def first_half(core):
            return lambda ti: (rows_of(ti, core)[0], rows_of(ti, core)[1] // 2)

        def second_half(core):
            return lambda ti: (rows_of(ti, core)[0] + rows_of(ti, core)[1] // 2, rows_of(ti, core)[1] // 2)

        mine = lambda ti: rows_of(ti, c)
        theirs = lambda ti: rows_of(ti, 1 - c)
        split = {
            TO_X: mk(TO_X, me, kx, dev_x, mine, mine),
            TO_Y: mk(TO_Y, me, ky, dev_y, mine, mine),
            FWD_X: mk(FWD_X, ky, kd, dev_x, first_half(c), first_half(c)),
            FWD_Y: mk(FWD_Y, kx, kd, dev_y, second_half(c), second_half(c)),
            SIB_X: mk(SIB_X, kx, kx, sib, mine, theirs),
            SIB_Y: mk(SIB_Y, ky, ky, sib, mine, theirs),
            SIB_D: mk(SIB_D, kd, kd, sib, mine, theirs),
        }
        direct = {
            TO_X: mk(TO_X, me, kx, dev_x, whole, whole),
            TO_Y: mk(TO_Y, me, ky, dev_y, whole, whole),
            FWD_X: mk(FWD_X, me, kd, dev_d, whole, whole),
        }
        return split, direct

    is_split = [t.L * t.R * t.C >= DIRECT_BELOW for t, _, _ in parts]
    assert all(s or (r0, r1) == (0, t.R) for s, (t, r0, r1) in zip(is_split, parts))

    def start(ins, outs, sems):
        split, direct = plan(outs, sems)
        for ti in range(nt):
            kinds = split if is_split[ti] else direct
            kinds[TO_X](ti, "s").start()
            kinds[TO_Y](ti, "s").start()
            if not is_split[ti]:
                kinds[FWD_X](ti, "s").start()

    def mid(ins, outs, sems):
        split, _ = plan(outs, sems)
        for ti in range(nt):
            if is_split[ti]:
                split[TO_Y](ti, "r").wait_recv()
                split[FWD_X](ti, "s").start()
                split[SIB_Y](ti, "s").start()
        for ti in range(nt):
            if is_split[ti]:
                split[TO_X](ti, "r").wait_recv()
                split[FWD_Y](ti, "s").start()
                split[SIB_X](ti, "s").start()

    def finish(ins, outs, sems):
        split, direct = plan(outs, sems)
        for ti in range(nt):
            if is_split[ti]:
                split[FWD_X](ti, "r").wait_recv()
                split[FWD_Y](ti, "r").wait_recv()
                split[SIB_D](ti, "s").start()
            else:
                for kind in (TO_X, TO_Y, FWD_X):
                    direct[kind](ti, "r").wait_recv()
        for ti in range(nt):
            if is_split[ti]:
                for kind in (SIB_X, SIB_Y, SIB_D):
                    split[kind](ti, "r").wait_recv()
        for ti in range(nt):
            kinds = split if is_split[ti] else direct
            for kind in kinds:
                kinds[kind](ti, "s").wait_send()

    sems = pltpu.SemaphoreType.DMA((nt, 7))
    return Rider(list(fulls), [_sds(a.shape, a.dtype) for a in fulls], {i: i for i in range(nt)},
                 [sems, sems], start, mid, finish)


def _pair_exchange_rider(specs, grads):
    nt = len(specs)

    def copy(ins, outs, sems, ti, c, sibling):
        half = specs[ti].R // 2
        return pltpu.make_async_remote_copy(ins[ti].at[:, :, pl.ds((1 - c) * half, half), :], outs[ti],
                                            sems[0].at[ti], sems[1].at[ti], device_id=sibling, device_id_type=MESH)

    def start(ins, outs, sems):
        x, y, c, _ = _mesh_position()
        for ti in range(nt):
            copy(ins, outs, sems, ti, c, (x, y, 1 - c)).start()

    def finish(ins, outs, sems):
        x, y, c, _ = _mesh_position()
        for ti in range(nt):
            copy(ins, outs, sems, ti, c, (x, y, 1 - c)).wait()

    sems = pltpu.SemaphoreType.DMA((nt,))
    return Rider(list(grads), [_sds(_full_shape(t, t.R // 2), BF) for t in specs], {}, [sems, sems], start, None, finish)


def _pair_sum_job(t, g, land):
    assert t.L == 1
    half = t.R // 2
    nj = half // t.rb
    block = (None, t.A, t.rb, _ncb(t) * t.C)

    def fn(j, kc_ref, ins, outs):
        outs[0][...] = (ins[0][...].astype(F32) + ins[1][...].astype(F32)).astype(BF)

    return Job(nj,
               [(g, block, lambda j, kc_ref: (0, 0, kc_ref[1] * nj + j, 0)),
                (land, block, lambda j, kc_ref: (0, 0, j, 0))],
               [(_sds(_full_shape(t, half), BF), block, lambda j, kc_ref: (0, 0, j, 0))], fn)


def _scatter_rider(specs, sums):
    nt = len(specs)

    def copy(ins, outs, sems, ti, j, chip, c):
        t = specs[ti]
        cx, cy = chip
        return pltpu.make_async_remote_copy(_slot(ins[ti], t, 2 * cx + cy, 0, t.R // 2), outs[ti].at[j],
                                            sems[0].at[ti, j], sems[1].at[ti, j],
                                            device_id=(cx, cy, c), device_id_type=MESH)

    def start(ins, outs, sems):
        _, _, c, chips = _mesh_position()
        for j, chip in enumerate(chips):
            for ti in range(nt):
                copy(ins, outs, sems, ti, j, chip, c).start()

    def finish(ins, outs, sems):
        _, _, c, chips = _mesh_position()
        for j, chip in enumerate(chips):
            for ti in range(nt):
                copy(ins, outs, sems, ti, j, chip, c).wait()

    sems = pltpu.SemaphoreType.DMA((nt, N_CHIPS - 1))
    return Rider(list(sums), [_sds((N_CHIPS - 1, t.L, t.R // 2, t.C), BF) for t in specs], {}, [sems, sems],
                 start, None, finish)


def _chip_sum_job(ts, landed):
    t0 = ts[0]
    assert t0.L == 1
    half = t0.R // 2
    nj = half // t0.rb

    def local(j, li):
        return jnp.clip(j - li * nj, 0, nj - 1)

    ins = []
    for li, t in enumerate(ts):
        s, land = landed[t.name]

        def own_map(j, kc_ref, li=li, t=t):
            a, cb = _slot_index(t, kc_ref[0])
            return (0, a, local(j, li), cb)

        ins.append((s, (None, None, t.rb, t.C), own_map))
        ins.append((land, (N_CHIPS - 1, None, t.rb, t.C), lambda j, kc_ref, li=li: (0, 0, local(j, li), 0)))

    def fn(j, kc_ref, in_refs, outs):
        for li in range(len(ts)):
            @pl.when(j // nj == li)
            def _():
                acc = in_refs[2 * li][...].astype(F32)
                for k in range(N_CHIPS - 1):
                    acc = acc + in_refs[2 * li + 1][k].astype(F32)
                outs[0][...] = acc

    return Job(len(ts) * nj, ins,
               [(_sds((len(ts), t0.R, t0.C)), (None, t0.rb, t0.C),
                 lambda j, kc_ref: (j // nj, kc_ref[1] * nj + j % nj, 0))], fn)


def _adamw_job(rb, w, g, m, v):
    n_layers, r, c = w.shape
    nb = r // rb
    block = (None, rb, c)
    index = lambda j, kc_ref: (j // nb, j % nb, 0)

    def fn(j, kc_ref, ins, outs):
        g_v = ins[1][...]
        outs[0][...] = g_v
        outs[1][...], outs[2][...], outs[3][...] = _adamw_math(ins[0][...], g_v, ins[2][...], ins[3][...])

    return Job(n_layers * nb, [(a, block, index) for a in (w, g, m, v)],
               [(_sds(w.shape), block, index)] * 4, fn)


def _chip_sum_fused_job(ts, fused, by_cols):
    t0 = ts[0]
    own0 = fused[t0.name][0]
    if by_cols:
        rows, cols = own0.shape
    else:
        nb, rows, bw = own0.shape
        cols = nb * bw
    nj = rows // t0.rb

    def local(j, li):
        return jnp.clip(j - li * nj, 0, nj - 1)

    ins = []
    for li, t in enumerate(ts):
        own, land = fused[t.name]
        if by_cols:
            ins.append((own, (t.rb, cols), lambda j, kc_ref, li=li: (local(j, li), 0)))
            ins.append((land, (N_CHIPS - 1, t.rb, cols), lambda j, kc_ref, li=li: (0, local(j, li), 0)))
        else:
            ins.append((own, (nb, t.rb, bw), lambda j, kc_ref, li=li: (0, local(j, li), 0)))
            ins.append((land, (N_CHIPS - 1, nb, t.rb, bw), lambda j, kc_ref, li=li: (0, 0, local(j, li), 0)))

    def fn(j, kc_ref, in_refs, outs):
        for li in range(len(ts)):
            @pl.when(j // nj == li)
            def _():
                acc = in_refs[2 * li][...].astype(F32)
                for k in range(N_CHIPS - 1):
                    acc = acc + in_refs[2 * li + 1][k].astype(F32)
                outs[0][...] = acc if by_cols else jnp.concatenate([acc[b] for b in range(nb)], axis=1)

    def out_map(j, kc_ref):
        return (j // nj, j % nj, kc_ref[1]) if by_cols else (j // nj, kc_ref[1] * nj + j % nj, 0)

    return Job(len(ts) * nj, ins, [(_sds((len(ts), t0.R, t0.C)), (None, t0.rb, cols), out_map)], fn)


def _share_rider(halves, by_cols):
    nt = len(halves)

    def copy(outs, sems, ti, core, sibling):
        axis = 2 if by_cols[ti] else 1
        half = halves[ti].shape[axis] // 2
        piece = pl.ds(pl.multiple_of(core * half, 128 if by_cols[ti] else 8), half)
        part = outs[ti].at[:, :, piece] if by_cols[ti] else outs[ti].at[:, piece, :]
        return pltpu.make_async_remote_copy(part, part, sems[0].at[ti], sems[1].at[ti],
                                            device_id=sibling, device_id_type=MESH)

    def start(ins, outs, sems):
        x, y, c, _ = _mesh_position()
        for ti in range(nt):
            copy(outs, sems, ti, c, (x, y, 1 - c)).start()

    def finish(ins, outs, sems):
        x, y, c, _ = _mesh_position()
        for ti in range(nt):
            copy(outs, sems, ti, 1 - c, (x, y, 1 - c)).wait_recv()
        for ti in range(nt):
            copy(outs, sems, ti, c, (x, y, 1 - c)).wait_send()

    sems = pltpu.SemaphoreType.DMA((nt,))
    return Rider(list(halves), [_sds(a.shape, a.dtype) for a in halves], {i: i for i in range(nt)}, [sems, sems],
                 start, None, finish)


def _both(r1, r2):
    assert r1.mid is None and r2.mid is None
    ni, no, ns = len(r1.arrays), len(r1.out_shapes), len(r1.scratch)

    def split(fn1, fn2):
        def run(ins, outs, scr):
            fn1(ins[:ni], outs[:no], scr[:ns])
            fn2(ins[ni:], outs[no:], scr[ns:])
        return run

    aliases = dict(r1.aliases)
    aliases.update({ni + a: no + b for a, b in r2.aliases.items()})
    return Rider(r1.arrays + r2.arrays, r1.out_shapes + r2.out_shapes, aliases, r1.scratch + r2.scratch,
                 split(r1.start, r2.start), None, split(r1.finish, r2.finish))


def _adamw_math(w, g, m, v):
    m = B1 * m + (1.0 - B1) * g
    v = B2 * v + (1.0 - B2) * (g * g)
    delta = -LR * ((m / BC1) / (jnp.sqrt(v / BC2) + AEPS) + WD * w)
    return delta, m, v


def _adamw(name, rb, w, g, m, v):
    n_layers, r, c = w.shape

    def body(w_ref, g_ref, m_ref, v_ref, go_ref, d_ref, nm_ref, nv_ref):
        g_v = g_ref[...]
        go_ref[...] = g_v
        d_ref[...], nm_ref[...], nv_ref[...] = _adamw_math(w_ref[...], g_v, m_ref[...], v_ref[...])

    spec = pl.BlockSpec((None, rb, c), lambda l, j: (l, j, 0))
    return pl.pallas_call(
        body, name=f"adamw_{name}", grid=(n_layers, r // rb),
        in_specs=[spec] * 4, out_specs=[spec] * 4, out_shape=[_sds(w.shape)] * 4,
        compiler_params=_params(2),
    )(w, g, m, v)


GAIN_ROWS = {"pre_mix_g": 0, "post_mix_g": 2, "pre_ffn_g": 4, "post_ffn_g": 6, "ple_g": 8, "ple_post_g": 10}
ROW_KV_G, ROW_POOL_SCALE, ROW_SINKS, ROW_LOSS, PACK_ROWS = 12, 13, 14, 15, 16
SMALL_NAMES = tuple(GAIN_ROWS) + ("kv_g", "pool_scale", "sinks")


def _small_all_reduce(rows, dpool, rider=None):
    ng, pr = len(WINDOWS), POOL_G // N_CHIPS

    def body(*refs):
        row_refs = refs[:PACK_ROWS]
        dpool_ref, tot_ref, gpool_ref, pack, land, pland, send, recv, psend, precv = refs[PACK_ROWS:]
        x, y, c, _ = _mesh_position()
        me = 4 * x + 2 * y + c
        for r in range(PACK_ROWS):
            pack[r:r + 1, :] = row_refs[r][...]

        def shard_of(k):
            return dpool_ref.at[:, pl.ds(pl.multiple_of(k * pr, pr), pr), :]

        cps = []
        for j in range(1, N_DEV):
            px, py, pc = x ^ (j >> 2), y ^ ((j >> 1) & 1), c ^ (j & 1)
            cps.append(pltpu.make_async_remote_copy(pack, land.at[me], send.at[j], recv.at[j],
                                                    device_id=(px, py, pc), device_id_type=MESH))
            cps.append(pltpu.make_async_remote_copy(shard_of(2 * px + py), pland.at[me], psend.at[j], precv.at[j],
                                                    device_id=(px, py, pc), device_id_type=MESH))
        for cp in cps:
            cp.start()
        land[me] = pack[...]
        pland[me] = dpool_ref[:, pl.ds(pl.multiple_of((2 * x + y) * pr, pr), pr), :]
        for j in range(1, N_DEV):
            pltpu.make_async_remote_copy(pack, land.at[me ^ j], send.at[j], recv.at[j],
                                         device_id=(x, y, c), device_id_type=MESH).wait_recv()
            pltpu.make_async_remote_copy(shard_of(0), pland.at[me ^ j], psend.at[j], precv.at[j],
                                         device_id=(x, y, c), device_id_type=MESH).wait_recv()
        for cp in cps:
            cp.wait_send()
        tot = land[0]
        gp = pland[0].astype(F32)
        for d in range(1, N_DEV):
            tot = tot + land[d]
            gp = gp + pland[d].astype(F32)
        tot_ref[...] = tot
        gpool_ref[...] = gp

    sems = pltpu.SemaphoreType.DMA((N_DEV,))
    return _call(
        body, name="small_all_reduce", grid=(1,),
        in_specs=[VSPEC] * (PACK_ROWS + 1), out_specs=[VSPEC, VSPEC],
        out_shape=[_sds((PACK_ROWS, D)), _sds((ng, pr, POOL_G))],
        scratch_shapes=[pltpu.VMEM((PACK_ROWS, D), F32), pltpu.VMEM((N_DEV, PACK_ROWS, D), F32),
                        pltpu.VMEM((N_DEV, ng, pr, POOL_G), BF), sems, sems, sems, sems],
        args=[*rows, dpool], rider=rider)


def _small_adamw(tot, kc, small_w, small_m, small_v):
    names = SMALL_NAMES
    n = len(names)

    def body(*refs):
        tot_ref, kc_ref = refs[0], refs[1]
        w_refs = dict(zip(names, refs[2:2 + n]))
        m_refs = dict(zip(names, refs[2 + n:2 + 2 * n]))
        v_refs = dict(zip(names, refs[2 + 2 * n:2 + 3 * n]))
        loss_ref = refs[2 + 3 * n]
        out_refs = {nm: refs[3 + 3 * n + 4 * k: 7 + 3 * n + 4 * k] for k, nm in enumerate(names)}
        tot = tot_ref[...]
        loss_ref[...] = 0.5 * jnp.sum(tot[ROW_LOSS:ROW_LOSS + 1, :], axis=-1, keepdims=True) * (1.0 / D)

        def update(nm, g):
            g_ref, d_ref, nm_ref, nv_ref = out_refs[nm]
            g_ref[...] = g
            d_ref[...], nm_ref[...], nv_ref[...] = _adamw_math(w_refs[nm][...], g, m_refs[nm][...], v_refs[nm][...])

        for nm, r in GAIN_ROWS.items():
            update(nm, tot[r:r + 2, :])
        update("kv_g", tot[ROW_KV_G:ROW_KV_G + 1, :])
        k = kc_ref[0]
        width = D // N_CHIPS
        g_scale = jnp.zeros((1, width), F32)
        for kk in range(N_CHIPS):
            g_scale = g_scale + jnp.where(k == kk, tot[ROW_POOL_SCALE:ROW_POOL_SCALE + 1, kk * width:(kk + 1) * width], 0.0)
        update("pool_scale", g_scale)
        update("sinks", tot[ROW_SINKS:ROW_SINKS + 1, 0:N_HEADS])

    ins = [tot, kc] + [small_w[nm] for nm in names] + [small_m[nm] for nm in names] + [small_v[nm] for nm in names]
    out_shape = [_sds((1, 1))]
    for nm in names:
        out_shape += [_sds(small_w[nm].shape)] * 4
    outs = pl.pallas_call(
        body, name="small_adamw",
        in_specs=[VSPEC, SSPEC] + [VSPEC] * (3 * n), out_specs=[VSPEC] * len(out_shape), out_shape=out_shape,
        compiler_params=_params(),
    )(*ins)
    return outs[0], {nm: outs[1 + 4 * k: 5 + 4 * k] for k, nm in enumerate(names)}


def _compute_layout(t, full):
    if t.src == "w_gu":
        return full.reshape(2, D, FF)
    if t.src == "pool_w":
        return full.reshape(len(WINDOWS), POOL_G, POOL_G)
    if t.src == "pool_scale":
        return full.reshape(1, D)
    return full.reshape(t.A * t.R, _ncb(t) * t.C)


def kernel(x, p, pre_mix_g, post_mix_g, pre_ffn_g, post_ffn_g, pool_w, pool_scale, kv_g, w_kv, w_q, sinks, w_o, w_gu, w_down, ple_g, w_ple_gate, w_ple_proj, ple_post_g, loss_target, m_pre_mix_g, m_post_mix_g, m_pre_ffn_g, m_post_ffn_g, m_pool_w, m_pool_scale, m_kv_g, m_w_kv, m_w_q, m_sinks, m_w_o, m_w_gu, m_w_down, m_ple_g, m_w_ple_gate, m_w_ple_proj, m_ple_post_g, v_pre_mix_g, v_post_mix_g, v_pre_ffn_g, v_post_ffn_g, v_pool_w, v_pool_scale, v_kv_g, v_w_kv, v_w_q, v_sinks, v_w_o, v_w_gu, v_w_down, v_ple_g, v_w_ple_gate, v_w_ple_proj, v_ple_post_g):
    weights = dict(pre_mix_g=pre_mix_g, post_mix_g=post_mix_g, pre_ffn_g=pre_ffn_g, post_ffn_g=post_ffn_g,
                   pool_w=pool_w, pool_scale=pool_scale, kv_g=kv_g, w_kv=w_kv, w_q=w_q, sinks=sinks, w_o=w_o,
                   w_gu=w_gu, w_down=w_down, ple_g=ple_g, w_ple_gate=w_ple_gate, w_ple_proj=w_ple_proj,
                   ple_post_g=ple_post_g)
    m_in = dict(pre_mix_g=m_pre_mix_g, post_mix_g=m_post_mix_g, pre_ffn_g=m_pre_ffn_g, post_ffn_g=m_post_ffn_g,
                pool_w=m_pool_w, pool_scale=m_pool_scale, kv_g=m_kv_g, w_kv=m_w_kv, w_q=m_w_q, sinks=m_sinks,
                w_o=m_w_o, w_gu=m_w_gu, w_down=m_w_down, ple_g=m_ple_g, w_ple_gate=m_w_ple_gate,
                w_ple_proj=m_w_ple_proj, ple_post_g=m_ple_post_g)
    v_in = dict(pre_mix_g=v_pre_mix_g, post_mix_g=v_post_mix_g, pre_ffn_g=v_pre_ffn_g, post_ffn_g=v_post_ffn_g,
                pool_w=v_pool_w, pool_scale=v_pool_scale, kv_g=v_kv_g, w_kv=v_w_kv, w_q=v_w_q, sinks=v_sinks,
                w_o=v_w_o, w_gu=v_w_gu, w_down=v_w_down, ple_g=v_ple_g, w_ple_gate=v_w_ple_gate,
                w_ple_proj=v_w_ple_proj, ple_post_g=v_ple_post_g)
    order = ["pre_mix_g", "post_mix_g", "pre_ffn_g", "post_ffn_g", "pool_w", "pool_scale", "kv_g", "w_kv", "w_q",
             "sinks", "w_o", "w_gu", "w_down", "ple_g", "w_ple_gate", "w_ple_proj", "ple_post_g"]

    kc = jnp.stack([2 * lax.axis_index("x") + lax.axis_index("y"), lax.axis_index("c")]).astype(jnp.int32)
    s_len = x.shape[1]
    x2d = x.reshape(s_len, D)
    p3d = p.reshape(2, s_len, PLE)
    target = loss_target.reshape(s_len, D)
    kv_g2d = kv_g.reshape(1, D)
    gains = {nm: weights[nm] for nm in GAIN_ROWS}

    def shard_view(src, a):
        t = next(t for t in BIGS.values() if t.src == src)
        return a.reshape(-1, t.R, t.C)

    first, second = ["pool_w", "pool_scale"], ["w_gu0", "w_down0"]
    rest = [nm for nm in BIGS if nm not in first + second]
    specs = dict(BIGS, pool_scale=POOL_SCALE)
    placed = {}

    def place_job(nm):
        if nm == "pool_scale":
            return _place_job(POOL_SCALE, pool_scale.reshape(1, 1, D // N_CHIPS), F32)
        return _place_job(BIGS[nm], shard_view(BIGS[nm].src, weights[BIGS[nm].src]))

    def gather(names, rows=None):
        rows = rows or {}
        parts = [(specs[nm],) + tuple(rows.get(nm, (0, specs[nm].R))) for nm in names]
        return _gather_rider(parts, [placed[nm] for nm in names])

    def take(names, results):
        for nm, a in zip(names, results):
            placed[nm] = a

    def weight(nm):
        return _compute_layout(specs[nm], placed[nm])

    take(first, [r[0] for r in _multi_call("place_pool", [place_job(nm) for nm in first], kc)])
    cast, got = _multi_call("place_ffn0", [place_job(nm) for nm in second], kc, rider=gather(first))
    take(second, [r[0] for r in cast])
    take(first, got)
    jobs = [place_job(nm) for nm in rest]
    jobs.append(_mixa_fwd_job(x2d, gains["pre_mix_g"], weight("pool_w"), weight("pool_scale"), gains["post_mix_g"]))
    results, got = _multi_call("cast_and_mixa_fwd", jobs, kc, rider=gather(second))
    take(rest, [r[0] for r in results[:-1]])
    take(second, got)
    y0, x1 = results[-1]

    ride = ["w_ple_gate0", "w_ple_proj0", "w_q", "w_kv", "w_o", "w_gu1"]
    (f0, x2, g0, u0), got = _ffn_fwd(0, x1, gains["pre_ffn_g"], weight("w_gu0"), weight("w_down0"), gains["post_ffn_g"],
                             rider=gather(ride, {"w_gu1": (0, 320)}))
    take(ride, got)

    ride = ["w_ple_gate1", "w_ple_proj1", "w_gu1"]
    (z0, pe0, x3, q, kv), got = _ple_fwd(
        0, x2, p3d, gains["ple_g"], weight("w_ple_gate0"), weight("w_ple_proj0"), gains["ple_post_g"],
        qkv=(gains["pre_mix_g"], kv_g2d, weight("w_q"), weight("w_kv")),
        rider=gather(ride, {"w_gu1": (320, 704)}))
    take(ride, got)

    ride = ["w_down1", "w_gu1"]
    (attn, y1, x4), got = _attn_fwd(q, kv, sinks, x3, weight("w_o"), gains["post_mix_g"],
                                    rider=gather(ride, {"w_gu1": (704, D)}))
    take(ride, got)

    (f1, x5, g1, u1), _ = _ffn_fwd(1, x4, gains["pre_ffn_g"], weight("w_gu1"), weight("w_down1"), gains["post_ffn_g"])
    (z1, pe1, dx6, loss_row), _ = _ple_fwd(1, x5, p3d, gains["ple_g"], weight("w_ple_gate1"), weight("w_ple_proj1"),
                                           gains["ple_post_g"], target=target)

    local = {}
    landed = {}
    fused = {}

    def local_grads(names):
        return [local[nm].reshape(_full_shape(BIGS[nm])) for nm in names]

    def pair_exchange(names):
        return _pair_exchange_rider([BIGS[nm] for nm in names], local_grads(names))

    def pair_sum(tag, names, lands):
        jobs = [_pair_sum_job(BIGS[nm], g, l) for nm, g, l in zip(names, local_grads(names), lands)]
        return [r[0] for r in _multi_call(f"pair_sum_{tag}", jobs, kc)]

    def scatter(names, sums):
        return _scatter_rider([BIGS[nm] for nm in names], sums)

    def keep(names, sums, got):
        for nm, s, l in zip(names, sums, got):
            landed[nm] = (s, l)

    (dx5, local["w_ple_gate1"], local["w_ple_proj1"], d_ple1, d_plepost1), _ = _ple_bwd(
        1, dx6, x5, z1, pe1, p3d, gains["ple_g"], weight("w_ple_gate1"), gains["ple_post_g"])

    group_a = ["w_ple_gate1", "w_ple_proj1"]
    (dx4, d_preffn1, d_postffn1, *scattered), lands_a = _ffn_bwd(
        1, dx5, x4, f1, g1, u1, gains["pre_ffn_g"], weight("w_gu1"), weight("w_down1"), gains["post_ffn_g"], kc,
        rider=pair_exchange(group_a))
    fused["w_gu1"], fused["w_down1"] = scattered[0:2], scattered[2:4]
    sums_a = pair_sum("a", group_a, lands_a)

    (dq, dkv, local["w_o"], d_postmix1, d_sinks), got = _attn_bwd(
        dx4, y1, attn, q, kv, sinks, weight("w_o"), gains["post_mix_g"], rider=scatter(group_a, sums_a))
    keep(group_a, sums_a, got)
    dx3, local["w_q"], local["w_kv"], d_premix1, d_kvg = _qkv_bwd(
        dq, dkv, x3, dx4, gains["pre_mix_g"], kv_g2d, weight("w_q"), weight("w_kv"))

    group_b = ["w_o", "w_q", "w_kv"]
    sums_b = pair_sum("b", group_b, _run("grads_pair_exchange_b", pair_exchange(group_b)))
    (dx2, local["w_ple_gate0"], local["w_ple_proj0"], d_ple0, d_plepost0), got = _ple_bwd(
        0, dx3, x2, z0, pe0, p3d, gains["ple_g"], weight("w_ple_gate0"), gains["ple_post_g"],
        rider=scatter(group_b, sums_b))
    keep(group_b, sums_b, got)

    group_c = ["w_ple_gate0", "w_ple_proj0"]
    (dx1, d_preffn0, d_postffn0, *scattered), lands_c = _ffn_bwd(
        0, dx2, x1, f0, g0, u0, gains["pre_ffn_g"], weight("w_gu0"), weight("w_down0"), gains["post_ffn_g"], kc,
        rider=pair_exchange(group_c))
    fused["w_gu0"], fused["w_down0"] = scattered[0:2], scattered[2:4]
    sums_c = pair_sum("c", group_c, lands_c)

    (dx0, d_pool, d_scale, d_postmix0, d_premix0), _ = _mixa_bwd(
        dx1, x2d, y0, gains["pre_mix_g"], weight("pool_w"), weight("pool_scale"), gains["post_mix_g"])

    rows = [d_premix0, d_premix1, d_postmix0, d_postmix1, d_preffn0, d_preffn1, d_postffn0, d_postffn1,
            d_ple0, d_ple1, d_plepost0, d_plepost1, d_kvg, d_scale, d_sinks, loss_row]
    as2d = lambda a: a.reshape(1, D) if a.ndim == 1 else a
    layers_of = lambda src: [t for t in BIGS.values() if t.src == src]
    own_scatter = ["w_gu", "w_down"]
    early = own_scatter + ["w_q", "w_o", "w_kv"]
    late = ["w_ple_gate", "w_ple_proj"]
    by_cols = lambda srcs: [src == "w_down" for src in srcs]
    jobs = [_chip_sum_fused_job(layers_of(src), fused, by_cols=src == "w_down") for src in own_scatter]
    jobs += [_chip_sum_job(layers_of(src), landed) for src in early if src not in own_scatter]
    halves = [r[0] for r in _multi_call("chip_sum_early", jobs, kc)]
    (tot, g_pool), got = _small_all_reduce(
        rows, d_pool, rider=_both(scatter(group_c, sums_c), _share_rider(halves, by_cols(early))))
    keep(group_c, sums_c, got[:len(group_c)])
    full_grads = dict(zip(early, got[len(group_c):]))
    loss, small = _small_adamw(tot, kc, {nm: as2d(weights[nm]) for nm in SMALL_NAMES},
                               {nm: as2d(m_in[nm]) for nm in SMALL_NAMES},
                               {nm: as2d(v_in[nm]) for nm in SMALL_NAMES})

    halves = [r[0] for r in _multi_call("chip_sum_late", [_chip_sum_job(layers_of(src), landed) for src in late], kc)]
    full_grads.update(zip(late, _run("grads_pair_share", _share_rider(halves, by_cols(late)))))
    full_grads["pool_w"] = g_pool
    others = [src for src in BIG_SOURCES if src not in own_scatter and src != "pool_w"]

    def adam_args(src):
        return (layers_of(src)[0].rb, shard_view(src, weights[src]), full_grads[src],
                shard_view(src, m_in[src]), shard_view(src, v_in[src]))

    out = {"grad": {}, "delta": {}, "new_m": {}, "new_v": {}}
    results = {src: _adamw(src, *adam_args(src)) for src in own_scatter}
    rest_srcs = others + ["pool_w"]
    results.update(zip(rest_srcs, _multi_call("adamw_rest", [_adamw_job(*adam_args(src)) for src in rest_srcs], kc)))
    for src in BIG_SOURCES:
        shape = weights[src].shape
        for kind, a in zip(("grad", "delta", "new_m", "new_v"), results[src]):
            out[kind][src] = a.reshape(shape)
    for nm in SMALL_NAMES:
        shape = weights[nm].shape
        for kind, a in zip(("grad", "delta", "new_m", "new_v"), small[nm]):
            out[kind][nm] = a.reshape(shape)

    return (loss.reshape(()), dx0.reshape(x.shape),
            *[out["grad"][nm] for nm in order], *[out["delta"][nm] for nm in order],
            *[out["new_m"][nm] for nm in order], *[out["new_v"][nm] for nm in order])
```

```python
import collections

import jax
import jax.numpy as jnp
from jax import lax
from jax.experimental import pallas as pl
from jax.experimental.pallas import tpu as pltpu

D = 1024
FF = 2816
N_HEADS = 16
HEAD_DIM = 64
N_KV_HEADS = 4
GQA = N_HEADS // N_KV_HEADS
KVD = N_KV_HEADS * HEAD_DIM
PLE = 256
BLK = 128
WINDOWS = (2, 4, 8, 16)
POOL_G = 256
HALO = 16
EPS = 1e-6
NEG_INF = -1e30
ATT_SCALE = HEAD_DIM ** -0.5
SLOPES = tuple(2.0 ** (-8.0 * (h + 1) / N_HEADS) for h in range(N_HEADS))
N_CHIPS = 4
N_DEV = 8

LR, B1, B2, AEPS, WD, STEP = 0.001, 0.9, 0.999, 1e-08, 0.01, 10
BC1 = 1.0 - B1 ** STEP
BC2 = 1.0 - B2 ** STEP

BF = jnp.bfloat16
F32 = jnp.float32
MESH = pl.DeviceIdType.MESH
VMEM_LIMIT_V7X = 58 * 1024 * 1024
TM = 256
TM_FFN_BWD = 512
FF_CHUNK = 256
FF_HALF = FF // 2

VSPEC = pl.BlockSpec(memory_space=pltpu.VMEM)
SSPEC = pl.BlockSpec(memory_space=pltpu.SMEM)
ANYSPEC = pl.BlockSpec(memory_space=pl.ANY)


def _params(n_grid=0):
    sem = ("arbitrary",) * n_grid if n_grid else None
    return pltpu.CompilerParams(dimension_semantics=sem, vmem_limit_bytes=VMEM_LIMIT_V7X)


def _sds(shape, dtype=F32):
    return jax.ShapeDtypeStruct(tuple(shape), dtype)


Rider = collections.namedtuple("Rider", "arrays out_shapes aliases scratch start mid finish")
MID_NUM, MID_DEN = 5, 8


def _call(body, *, name, grid, in_specs, out_specs, out_shape, args, scratch_shapes=(), rider=None, prefetch=None):
    ni, no, ns = len(in_specs), len(out_specs), len(scratch_shapes)
    npre = 0 if prefetch is None else 1
    pre = [] if prefetch is None else [prefetch]
    if rider is None:
        rider = Rider([], [], {}, [], None, None, None)
    ri, ro = len(rider.arrays), len(rider.out_shapes)

    def full(*refs):
        pre_refs, refs = refs[:npre], refs[npre:]
        ins, refs = refs[:ni], refs[ni:]
        rins, refs = refs[:ri], refs[ri:]
        outs, refs = refs[:no], refs[no:]
        routs, refs = refs[:ro], refs[ro:]
        scr, rscr = refs[:ns], refs[ns:]
        ids = [pl.program_id(a) for a in range(len(grid))]
        first = ids[0] == 0
        last = ids[0] == grid[0] - 1
        for a in range(1, len(grid)):
            first = first & (ids[a] == 0)
            last = last & (ids[a] == grid[a] - 1)

        if rider.start is not None:
            @pl.when(first)
            def _():
                rider.start(rins, routs, rscr)

        if rider.mid is not None:
            assert len(grid) == 1

            @pl.when(ids[0] == (grid[0] * MID_NUM) // MID_DEN)
            def _():
                rider.mid(rins, routs, rscr)

        body(*pre_refs, *ins, *outs, *scr)

        if rider.finish is not None:
            @pl.when(last)
            def _():
                rider.finish(rins, routs, rscr)

    outs = pl.pallas_call(
        full, name=name,
        grid_spec=pltpu.PrefetchScalarGridSpec(
            num_scalar_prefetch=npre, grid=grid,
            in_specs=list(in_specs) + [ANYSPEC] * ri, out_specs=list(out_specs) + [ANYSPEC] * ro,
            scratch_shapes=list(scratch_shapes) + list(rider.scratch)),
        out_shape=list(out_shape) + list(rider.out_shapes),
        input_output_aliases={npre + ni + a: no + b for a, b in rider.aliases.items()},
        compiler_params=_params(len(grid)))(*pre, *args, *rider.arrays)
    return list(outs[:no]), list(outs[no:])


def _run(name, rider):
    ri = len(rider.arrays)

    def body(*refs):
        rins, routs, rscr = refs[:ri], refs[ri:ri + len(rider.out_shapes)], refs[ri + len(rider.out_shapes):]
        rider.start(rins, routs, rscr)
        if rider.mid is not None:
            rider.mid(rins, routs, rscr)
        rider.finish(rins, routs, rscr)

    return pl.pallas_call(
        body, name=name, in_specs=[ANYSPEC] * ri, out_specs=[ANYSPEC] * len(rider.out_shapes),
        out_shape=list(rider.out_shapes), scratch_shapes=list(rider.scratch),
        input_output_aliases=dict(rider.aliases), compiler_params=_params())(*rider.arrays)


Job = collections.namedtuple("Job", "steps ins outs fn")


def _multi_call(name, jobs, kc, rider=None):
    n = max(job.steps for job in jobs)

    def clamped(index, steps):
        return lambda s, kc_ref: index(jnp.minimum(s, steps - 1), kc_ref)

    in_specs, out_specs, out_shape, args = [], [], [], []
    for job in jobs:
        for arr, block, index, *single in job.ins:
            mode = dict(pipeline_mode=pl.Buffered(1)) if single and single[0] else {}
            in_specs.append(pl.BlockSpec(block, clamped(index, job.steps), **mode))
            args.append(arr)
        for sds, block, index in job.outs:
            out_specs.append(pl.BlockSpec(block, clamped(index, job.steps)))
            out_shape.append(sds)
    n_in = len(args)

    def body(kc_ref, *refs):
        s = pl.program_id(0)
        i0, o0 = 0, n_in
        for job in jobs:
            ins, outs = refs[i0:i0 + len(job.ins)], refs[o0:o0 + len(job.outs)]
            i0, o0 = i0 + len(job.ins), o0 + len(job.outs)

            @pl.when(s < job.steps)
            def _():
                job.fn(s, kc_ref, ins, outs)

    outs, routs = _call(body, name=name, grid=(n,), in_specs=in_specs, out_specs=out_specs, out_shape=out_shape,
                        args=args, prefetch=kc, rider=rider)
    res, o0 = [], 0
    for job in jobs:
        res.append(outs[o0:o0 + len(job.outs)])
        o0 += len(job.outs)
    return res if rider is None else (res, routs)


def _rms_fwd(x, g):
    r = lax.rsqrt(jnp.mean(x * x, axis=-1, keepdims=True) + EPS)
    return x * r * g


def _rms_bwd(x, g, dy):
    r = lax.rsqrt(jnp.mean(x * x, axis=-1, keepdims=True) + EPS)
    xn = x * r
    dxn = dy * g
    dx = r * (dxn - xn * jnp.mean(dxn * xn, axis=-1, keepdims=True))
    return dx, dy * xn


def _rowsum(a):
    return jnp.sum(a, axis=0, keepdims=True)


def _sigmoid(z):
    return 1.0 / (1.0 + jnp.exp(-z))


def _dot(a, b):
    return jnp.dot(a, b, preferred_element_type=F32)


def _dot_nt(a, b):
    return lax.dot_general(a, b, (((1,), (1,)), ((), ())), preferred_element_type=F32)


def _dot_tn(a, b):
    return lax.dot_general(a, b, (((0,), (0,)), ((), ())), preferred_element_type=F32)


def _row_spec(tm, width=D):
    return pl.BlockSpec((tm, width), lambda i: (i, 0))


def _const_spec(shape):
    zeros = (0,) * len(shape)
    return pl.BlockSpec(tuple(shape), lambda *_: zeros)


def _pool_delta(he, pos):
    out = []
    for gi, w in enumerate(WINDOWS):
        hg = he[:, gi * POOL_G:(gi + 1) * POOL_G]
        s = hg
        k = 1
        while k < w:
            s = s + pltpu.roll(s, k, 0)
            k *= 2
        cnt = jnp.maximum(jnp.minimum(pos + 1, w), 1).astype(F32)
        out.append(s / cnt - hg)
    return out


def _load_with_halo_before(x_ref, i, tm):
    r0 = pl.multiple_of(i * tm, tm)
    hs = pl.multiple_of(jnp.maximum(i * tm - HALO, 0), 8)
    xh = jnp.where(i > 0, x_ref[pl.ds(hs, HALO), :], 0.0)
    xt = x_ref[pl.ds(r0, tm), :]
    return xt, jnp.concatenate([xh, xt], axis=0)


def _mixa_fwd_job(x, pre_g, pool_w, pool_scale, post_g):
    s_len = x.shape[0]

    def fn(i, kc_ref, ins, outs):
        x_ref, pg_ref, w_ref, sc_ref, qg_ref = ins
        y_ref, x1_ref = outs
        xt, xe = _load_with_halo_before(x_ref, i, TM)
        he = _rms_fwd(xe, pg_ref[0:1, :])
        pos = i * TM - HALO + lax.broadcasted_iota(jnp.int32, (TM + HALO, 1), 0)
        ds = _pool_delta(he, pos)
        ys = [_dot(ds[gi][HALO:, :].astype(BF), w_ref[gi]) for gi in range(len(WINDOWS))]
        y = jnp.concatenate(ys, axis=1) * sc_ref[...]
        y_ref[...] = y
        x1_ref[...] = xt + _rms_fwd(y, qg_ref[0:1, :])

    def whole(a):
        zeros = (0,) * a.ndim
        return (a, a.shape, lambda j, kc_ref: zeros, True)

    rows = lambda j, kc_ref: (j, 0)
    return Job(s_len // TM, [whole(a) for a in (x, pre_g, pool_w, pool_scale, post_g)],
               [(_sds((s_len, D)), (TM, D), rows), (_sds((s_len, D)), (TM, D), rows)], fn)


def _mixa_bwd(dx1, x, y, pre_g, pool_w, pool_scale, post_g, rider=None):
    s_len = x.shape[0]
    n = s_len // TM
    ng = len(WINDOWS)

    def body(dx_ref, x_ref, y_ref, pg_ref, w_ref, sc_ref, qg_ref,
             dx0_ref, dw_ref, dsc_ref, dqg_ref, dpg_ref, wacc):
        i = pl.program_id(0)

        @pl.when(i == 0)
        def _():
            wacc[...] = jnp.zeros_like(wacc)
            dsc_ref[...] = jnp.zeros_like(dsc_ref)
            dqg_ref[...] = jnp.zeros_like(dqg_ref)
            dpg_ref[...] = jnp.zeros_like(dpg_ref)

        r0 = pl.multiple_of(i * TM, TM)
        xt, xe = _load_with_halo_before(x_ref, i, TM)
        he = _rms_fwd(xe, pg_ref[0:1, :])
        pos_b = i * TM - HALO + lax.broadcasted_iota(jnp.int32, (TM + HALO, 1), 0)
        ds = _pool_delta(he, pos_b)

        last = i == n - 1
        a0 = pl.multiple_of(jnp.minimum(i * TM + TM, s_len - HALO), 8)
        ye = jnp.concatenate([y_ref[pl.ds(r0, TM), :], y_ref[pl.ds(a0, HALO), :]], axis=0)
        dt = dx_ref[pl.ds(r0, TM), :]
        de = jnp.concatenate([dt, jnp.where(last, 0.0, dx_ref[pl.ds(a0, HALO), :])], axis=0)
        dye, prod = _rms_bwd(ye, qg_ref[0:1, :], de)
        dqg_ref[...] += _rowsum(prod[:TM, :])
        dys = dye * sc_ref[...]
        pos_a = i * TM + lax.broadcasted_iota(jnp.int32, (TM + HALO, 1), 0)

        dhs, dscs = [], []
        for gi, w in enumerate(WINDOWS):
            sl = slice(gi * POOL_G, (gi + 1) * POOL_G)
            wg = w_ref[gi]
            dys_g = dys[:, sl].astype(BF)
            d_g = ds[gi][HALO:, :].astype(BF)
            ypre = _dot(d_g, wg)
            dscs.append(_rowsum(dye[:TM, sl] * ypre))
            wacc[gi] += _dot_tn(d_g, dys_g[:TM, :])
            dd = _dot_nt(dys_g, wg)
            cnt = jnp.minimum(pos_a + 1, w).astype(F32)
            a = dd / cnt
            k = 1
            while k < w:
                a = a + pltpu.roll(a, TM + HALO - k, 0)
                k *= 2
            dhs.append(a[:TM, :] - dd[:TM, :])
        dsc_ref[...] += jnp.concatenate(dscs, axis=1)
        dh = jnp.concatenate(dhs, axis=1)
        dxp, prod2 = _rms_bwd(xt, pg_ref[0:1, :], dh)
        dpg_ref[...] += _rowsum(prod2)
        dx0_ref[...] = dt + dxp

        @pl.when(last)
        def _():
            dw_ref[...] = wacc[...].astype(BF)

    return _call(
        body, name="mixa_bwd", grid=(n,), in_specs=[VSPEC] * 7,
        out_specs=[_row_spec(TM), _const_spec((ng, POOL_G, POOL_G)), _const_spec((1, D)),
                   _const_spec((1, D)), _const_spec((1, D))],
        out_shape=[_sds((s_len, D)), _sds((ng, POOL_G, POOL_G), BF), _sds((1, D)), _sds((1, D)), _sds((1, D))],
        scratch_shapes=[pltpu.VMEM((ng, POOL_G, POOL_G), F32)],
        args=[dx1, x, y, pre_g, pool_w, pool_scale, post_g], rider=rider)


def _ffn_fwd(layer, x1, pre_g, wgu, wd, post_g, rider=None):
    s_len = x1.shape[0]

    def body(x_ref, pg_ref, wgu_ref, wd_ref, qg_ref, f_ref, x2_ref, g_ref, u_ref):
        x = x_ref[...]
        h = _rms_fwd(x, pg_ref[layer:layer + 1, :]).astype(BF)
        f = jnp.zeros((TM, D), F32)
        for c in range(FF // FF_HALF):
            cols = slice(c * FF_HALF, (c + 1) * FF_HALF)
            g = _dot(h, wgu_ref[0, :, cols])
            u = _dot(h, wgu_ref[1, :, cols])
            g_ref[:, cols] = g.astype(BF)
            u_ref[:, cols] = u.astype(BF)
            act = g * _sigmoid(g) * u
            f = f + _dot(act.astype(BF), wd_ref[cols, :])
        f_ref[...] = f
        x2_ref[...] = x + _rms_fwd(f, qg_ref[layer:layer + 1, :])

    return _call(body, name=f"ffn_fwd{layer}", grid=(s_len // TM,),
                 in_specs=[_row_spec(TM), VSPEC, VSPEC, VSPEC, VSPEC],
                 out_specs=[_row_spec(TM), _row_spec(TM), _row_spec(TM, FF), _row_spec(TM, FF)],
                 out_shape=[_sds((s_len, D)), _sds((s_len, D)), _sds((s_len, FF), BF), _sds((s_len, FF), BF)],
                 args=[x1, pre_g, wgu, wd, post_g], rider=rider)


GU_PIECE = 128
DN_PIECE = 64
DN_SLOT = FF // N_CHIPS
HALF_D = D // 2
CHUNK_STRIDE = 6
CHUNK_START = (1, 7, 4, 10)


def _ffn_bwd(layer, dx2, x1, f, g_pre, u_pre, pre_g, wgu, wd, post_g, kc, rider=None):
    s_len = x1.shape[0]
    tm = TM_FFN_BWD
    n = s_len // tm
    nc = FF // FF_CHUNK
    n_gu, n_dn = FF_CHUNK // GU_PIECE, FF_CHUNK // DN_PIECE
    n_pieces = 2 * n_gu + n_dn
    n_blk = FF_HALF // GU_PIECE

    def edge_rows(c, i, kc_ref):
        return (jnp.where((c == 0) | (c == nc - 1), i, n - 1), 0)

    def chunk_at(c, kc_ref):
        k = kc_ref[0]
        start = jnp.where(k == 0, CHUNK_START[0], jnp.where(k == 1, CHUNK_START[1],
                                                            jnp.where(k == 2, CHUNK_START[2], CHUNK_START[3])))
        return ((c + start) * CHUNK_STRIDE) % nc

    def exchange(kc_ref, c, accg, accu, accd, own_gu_ref, land_gu_ref, own_dn_ref, land_dn_ref,
                 pl_gu, pl_dn, sib_gu, sib_dn, mine_gu, mine_dn, sum_gu, sum_dn,
                 psend, precv, ssend, lsem, rrecv):
        x, y, core = lax.axis_index("x"), lax.axis_index("y"), lax.axis_index("c")
        lower = core == 0

        def pair_copy(cc, part):
            p = cc % 2
            src, dst = ((sib_gu, pl_gu), (sib_dn, pl_dn))[part]
            return pltpu.make_async_remote_copy(src.at[p], dst.at[cc], psend.at[p, part], precv.at[cc, part],
                                                device_id=(x, y, 1 - core), device_id_type=MESH)

        def scatter(cc, wait):
            p = cc % 2
            hidden = chunk_at(cc, kc_ref) * FF_CHUNK

            assert n_gu == 2
            k0, k1 = hidden // FF_HALF, (hidden + GU_PIECE) // FF_HALF
            blk = (hidden - k0 * FF_HALF) // GU_PIECE
            for gu in range(2):
                @pl.when(k0 == k1)
                def _():
                    piece(p, wait, 2 * gu, sum_gu.at[p, gu], k0 + 2 * gu, 0, (pl.ds(blk, 2),))

                @pl.when(k0 != k1)
                def _():
                    piece(p, wait, 2 * gu, sum_gu.at[p, gu, 0], k0 + 2 * gu, 0, (blk,))
                    piece(p, wait, 2 * gu + 1, sum_gu.at[p, gu, 1], k1 + 2 * gu, 0, (0,))

            kd = hidden // DN_SLOT
            off = pl.multiple_of(hidden - kd * DN_SLOT, DN_PIECE)
            m = jnp.minimum((DN_SLOT - off) // DN_PIECE, n_dn)
            for mm in range(1, n_dn + 1):
                @pl.when(m == mm)
                def _():
                    rows = mm * DN_PIECE
                    piece(p, wait, 2 * n_gu, sum_dn.at[p, pl.ds(0, rows), :], kd, 1, (pl.ds(off, rows), slice(None)))
                    if mm < n_dn:
                        piece(p, wait, 2 * n_gu + 1, sum_dn.at[p, pl.ds(rows, FF_CHUNK - rows), :], kd + 1, 1,
                              (pl.ds(0, FF_CHUNK - rows), slice(None)))

        def piece(p, wait, pi, src, k, t, where):
            own_ref, land_ref = ((own_gu_ref, land_gu_ref), (own_dn_ref, land_dn_ref))[t]
            kx, ky = k // 2, k % 2
            fx, fy = (kx != x).astype(jnp.int32), (ky != y).astype(jnp.int32)
            local = (fx + fy) == 0
            j = jnp.maximum(fx + 2 * fy - 1, 0)

            @pl.when(local)
            def _():
                cp = pltpu.make_async_copy(src, own_ref.at[where], lsem.at[p, pi])
                if wait:
                    cp.wait()
                else:
                    cp.start()

            @pl.when(jnp.logical_not(local))
            def _():
                cp = pltpu.make_async_remote_copy(src, land_ref.at[(j,) + where], ssend.at[p, pi],
                                                  rrecv.at[t, j], device_id=(kx, ky, core), device_id_type=MESH)
                if wait:
                    cp.wait_send()
                else:
                    cp.start()

        def add_and_scatter(cc):
            p = cc % 2
            pair_copy(cc, 0).wait_recv()
            pair_copy(cc, 1).wait_recv()
            s_gu = (mine_gu[...] + pl_gu[cc].astype(F32)).astype(BF)
            for hc in range(n_gu):
                sum_gu[p, :, hc] = s_gu[:, :, hc * GU_PIECE:(hc + 1) * GU_PIECE]
            sum_dn[p] = (mine_dn[...] + pl_dn[cc].astype(F32)).astype(BF)
            scatter(cc, wait=False)

        @pl.when(c >= 1)
        def _():
            @pl.when(c >= 3)
            def _():
                scatter(c - 3, wait=True)
            add_and_scatter(c - 1)

        @pl.when(c >= 2)
        def _():
            pair_copy(c - 2, 0).wait_send()
            pair_copy(c - 2, 1).wait_send()

        p = c % 2
        my_rows = pl.ds(pl.multiple_of(core * HALF_D, HALF_D), HALF_D)
        sib_rows = pl.ds(pl.multiple_of((1 - core) * HALF_D, HALF_D), HALF_D)
        d_v = accd[...]
        sib_gu[p, 0] = accg[sib_rows, :].astype(BF)
        sib_gu[p, 1] = accu[sib_rows, :].astype(BF)
        sib_dn[p] = jnp.where(lower, d_v[:, HALF_D:], d_v[:, :HALF_D]).astype(BF)
        mine_gu[0] = accg[my_rows, :]
        mine_gu[1] = accu[my_rows, :]
        mine_dn[...] = jnp.where(lower, d_v[:, :HALF_D], d_v[:, HALF_D:])
        pair_copy(c, 0).start()
        pair_copy(c, 1).start()

        @pl.when(c == nc - 1)
        def _():
            scatter(nc - 3, wait=True)
            add_and_scatter(nc - 1)
            for cc in (nc - 2, nc - 1):
                pair_copy(cc, 0).wait_send()
                pair_copy(cc, 1).wait_send()
                scatter(cc, wait=True)
            for t, land_ref in enumerate((land_gu_ref, land_dn_ref)):
                for j in range(N_CHIPS - 1):
                    pltpu.make_async_remote_copy(land_ref.at[j], land_ref.at[j], ssend.at[0, 0], rrecv.at[t, j],
                                                 device_id=(x, y, core), device_id_type=MESH).wait_recv()

    def body(kc_ref, dx_ref, x_ref, f_ref, gp_ref, up_ref, pg_ref, wgu_ref, wd_ref, qg_ref,
             dx1_ref, dpg_ref, dqg_ref, own_gu_ref, land_gu_ref, own_dn_ref, land_dn_ref,
             h_s, df_s, dh_s, accg, accu, accd, *comm):
        c = pl.program_id(0)
        i = pl.program_id(1)
        rows = pl.ds(pl.multiple_of(i * tm, tm), tm)
        pg = pg_ref[layer:layer + 1, :]

        @pl.when((c == 0) & (i == 0))
        def _():
            dpg_ref[...] = jnp.zeros_like(dpg_ref)
            dqg_ref[...] = jnp.zeros_like(dqg_ref)

        @pl.when(c == 0)
        def _():
            h_s[rows, :] = _rms_fwd(x_ref[...], pg).astype(BF)
            df, prod = _rms_bwd(f_ref[...], qg_ref[layer:layer + 1, :], dx_ref[...])
            df_s[rows, :] = df.astype(BF)
            dqg_ref[...] += _rowsum(prod)

        @pl.when(i == 0)
        def _():
            accg[...] = jnp.zeros_like(accg)
            accu[...] = jnp.zeros_like(accu)
            accd[...] = jnp.zeros_like(accd)

        h = h_s[rows, :]
        df = df_s[rows, :]
        wg = wgu_ref[0]
        wu = wgu_ref[1]
        g = gp_ref[...].astype(F32)
        u = up_ref[...].astype(F32)
        sg = _sigmoid(g)
        a = g * sg
        dact = _dot_nt(df, wd_ref[...])
        accd[...] += _dot_tn((a * u).astype(BF), df)
        du = (dact * a).astype(BF)
        dg = (dact * u * (sg * (1.0 + g * (1.0 - sg)))).astype(BF)
        accg[...] += _dot_tn(h, dg)
        accu[...] += _dot_tn(h, du)
        dh = _dot_nt(dg, wg) + _dot_nt(du, wu)

        @pl.when(c == 0)
        def _():
            dh_s[rows, :] = dh

        @pl.when((c > 0) & (c < nc - 1))
        def _():
            dh_s[rows, :] += dh

        @pl.when(c == nc - 1)
        def _():
            dxp, prod = _rms_bwd(x_ref[...], pg, dh_s[rows, :] + dh)
            dpg_ref[...] += _rowsum(prod)
            dx1_ref[...] = dx_ref[...] + dxp

        @pl.when(i == n - 1)
        def _():
            exchange(kc_ref, c, accg, accu, accd, own_gu_ref, land_gu_ref, own_dn_ref, land_dn_ref, *comm)

    dma = pltpu.SemaphoreType.DMA
    return _call(
        body, name=f"ffn_bwd{layer}", grid=(nc, n),
        in_specs=[pl.BlockSpec((tm, D), edge_rows), pl.BlockSpec((tm, D), edge_rows),
                  pl.BlockSpec((tm, D), lambda c, i, kc_ref: (jnp.where(c == 0, i, n - 1), 0),
                               pipeline_mode=pl.Buffered(1)),
                  pl.BlockSpec((tm, FF_CHUNK), lambda c, i, kc_ref: (i, chunk_at(c, kc_ref))),
                  pl.BlockSpec((tm, FF_CHUNK), lambda c, i, kc_ref: (i, chunk_at(c, kc_ref))),
                  VSPEC,
                  pl.BlockSpec((2, D, FF_CHUNK), lambda c, i, kc_ref: (0, 0, chunk_at(c, kc_ref))),
                  pl.BlockSpec((FF_CHUNK, D), lambda c, i, kc_ref: (chunk_at(c, kc_ref), 0)),
                  VSPEC],
        out_specs=[pl.BlockSpec((tm, D), lambda c, i, kc_ref: (jnp.where(c == nc - 1, i, 0), 0)),
                   _const_spec((1, D)), _const_spec((1, D)), ANYSPEC, ANYSPEC, ANYSPEC, ANYSPEC],
        out_shape=[_sds((s_len, D)), _sds((1, D)), _sds((1, D)),
                   _sds((n_blk, HALF_D, GU_PIECE), BF), _sds((N_CHIPS - 1, n_blk, HALF_D, GU_PIECE), BF),
                   _sds((DN_SLOT, HALF_D), BF), _sds((N_CHIPS - 1, DN_SLOT, HALF_D), BF)],
        scratch_shapes=[pltpu.VMEM((s_len, D), BF), pltpu.VMEM((s_len, D), BF), pltpu.VMEM((s_len, D), F32),
                        pltpu.VMEM((D, FF_CHUNK), F32), pltpu.VMEM((D, FF_CHUNK), F32),
                        pltpu.VMEM((FF_CHUNK, D), F32),
                        pltpu.VMEM((nc, 2, HALF_D, FF_CHUNK), BF), pltpu.VMEM((nc, FF_CHUNK, HALF_D), BF),
                        pltpu.VMEM((2, 2, HALF_D, FF_CHUNK), BF), pltpu.VMEM((2, FF_CHUNK, HALF_D), BF),
                        pltpu.VMEM((2, HALF_D, FF_CHUNK), F32), pltpu.VMEM((FF_CHUNK, HALF_D), F32),
                        pltpu.VMEM((2, 2, n_gu, HALF_D, GU_PIECE), BF), pltpu.VMEM((2, FF_CHUNK, HALF_D), BF),
                        dma((2, 2)), dma((nc, 2)), dma((2, n_pieces)), dma((2, n_pieces)), dma((2, N_CHIPS - 1))],
        args=[dx2, x1, f, g_pre, u_pre, pre_g, wgu, wd, post_g], rider=rider, prefetch=kc)


def _ple_fwd(layer, x2, p, ple_g, w_gate, w_proj, post_g, target=None, qkv=None, rider=None):
    s_len = x2.shape[0]
    final = target is not None
    assert not (final and qkv)

    def body(*refs):
        if final:
            x_ref, p_ref, g_ref, wg_ref, wp_ref, qg_ref, t_ref, z_ref, pe_ref, dx_ref, lv_ref = refs
        elif qkv:
            (x_ref, p_ref, g_ref, wg_ref, wp_ref, qg_ref, ng_ref, kg_ref, wq_ref, wkv_ref,
             z_ref, pe_ref, x3_ref, q_ref, kv_ref) = refs
        else:
            x_ref, p_ref, g_ref, wg_ref, wp_ref, qg_ref, z_ref, pe_ref, x3_ref = refs
        x = x_ref[...]
        r = _rms_fwd(x, g_ref[layer:layer + 1, :]).astype(BF)
        z = _dot(r, wg_ref[...])
        pe = _dot(p_ref[...].astype(BF), wp_ref[...])
        z_ref[...] = z
        pe_ref[...] = pe
        x3 = x + _rms_fwd(pe * _sigmoid(z), qg_ref[layer:layer + 1, :])
        if final:
            @pl.when(pl.program_id(0) == 0)
            def _():
                lv_ref[...] = jnp.zeros_like(lv_ref)
            err = x3 - t_ref[...]
            dx_ref[...] = err * (1.0 / D)
            lv_ref[...] += _rowsum(err * err)
        else:
            x3_ref[...] = x3
        if qkv:
            q_ref[...] = _dot(_rms_fwd(x3, ng_ref[layer + 1:layer + 2, :]).astype(BF), wq_ref[...]).astype(BF)
            kv_ref[...] = _dot(_rms_fwd(x3, kg_ref[...]).astype(BF), wkv_ref[...]).astype(BF)

    p_spec = pl.BlockSpec((None, TM, PLE), lambda i: (layer, i, 0))
    in_specs = [_row_spec(TM), p_spec, VSPEC, VSPEC, VSPEC, VSPEC]
    args = [x2, p, ple_g, w_gate, w_proj, post_g]
    out_specs = [_row_spec(TM), _row_spec(TM), _row_spec(TM)]
    out_shape = [_sds((s_len, D))] * 3
    if qkv:
        in_specs += [VSPEC] * 4
        args += list(qkv)
        out_specs += [_row_spec(TM), _row_spec(TM, 2 * KVD)]
        out_shape += [_sds((s_len, D), BF), _sds((s_len, 2 * KVD), BF)]
    if final:
        in_specs.append(_row_spec(TM))
        args.append(target)
        out_specs.append(_const_spec((1, D)))
        out_shape.append(_sds((1, D)))
    return _call(body, name=f"ple_fwd{layer}", grid=(s_len // TM,), in_specs=in_specs, out_specs=out_specs,
                 out_shape=out_shape, args=args, rider=rider)


def _ple_bwd(layer, dx3, x2, z, pe, p, ple_g, w_gate, post_g, rider=None):
    s_len = x2.shape[0]
    n = s_len // TM

    def body(dx_ref, x_ref, z_ref, pe_ref, p_ref, g_ref, wg_ref, qg_ref,
             dx2_ref, dwg_ref, dwp_ref, dg_ref, dqg_ref, gacc, pacc):
        i = pl.program_id(0)

        @pl.when(i == 0)
        def _():
            gacc[...] = jnp.zeros_like(gacc)
            pacc[...] = jnp.zeros_like(pacc)
            dg_ref[...] = jnp.zeros_like(dg_ref)
            dqg_ref[...] = jnp.zeros_like(dqg_ref)

        dx = dx_ref[...]
        x = x_ref[...]
        pe_v = pe_ref[...]
        gate = _sigmoid(z_ref[...])
        de, prod = _rms_bwd(pe_v * gate, qg_ref[layer:layer + 1, :], dx)
        dqg_ref[...] += _rowsum(prod)
        dpe = (de * gate).astype(BF)
        dz = (de * pe_v * gate * (1.0 - gate)).astype(BF)
        pacc[...] += _dot_tn(p_ref[...].astype(BF), dpe)
        g = g_ref[layer:layer + 1, :]
        r = _rms_fwd(x, g).astype(BF)
        gacc[...] += _dot_tn(r, dz)
        dr = _dot_nt(dz, wg_ref[...])
        dxp, prod2 = _rms_bwd(x, g, dr)
        dg_ref[...] += _rowsum(prod2)
        dx2_ref[...] = dx + dxp

        @pl.when(i == n - 1)
        def _():
            dwg_ref[...] = gacc[...].astype(BF)
            dwp_ref[...] = pacc[...].astype(BF)

    p_spec = pl.BlockSpec((None, TM, PLE), lambda i: (layer, i, 0))
    return _call(
        body, name=f"ple_bwd{layer}", grid=(n,),
        in_specs=[_row_spec(TM), _row_spec(TM), _row_spec(TM), _row_spec(TM), p_spec, VSPEC, VSPEC, VSPEC],
        out_specs=[_row_spec(TM), _const_spec((D, D)), _const_spec((PLE, D)), _const_spec((1, D)), _const_spec((1, D))],
        out_shape=[_sds((s_len, D)), _sds((D, D), BF), _sds((PLE, D), BF), _sds((1, D)), _sds((1, D))],
        scratch_shapes=[pltpu.VMEM((D, D), F32), pltpu.VMEM((PLE, D), F32)],
        args=[dx3, x2, z, pe, p, ple_g, w_gate, post_g], rider=rider)


def _qkv_bwd(dq, dkv, x3, dx4, q_g, kv_g, w_q, w_kv):
    s_len = x3.shape[0]
    n = s_len // TM

    def body(dq_ref, dkv_ref, x_ref, dx_ref, qg_ref, kg_ref, wq_ref, wkv_ref,
             dx3_ref, dwq_ref, dwkv_ref, dqg_ref, dkg_ref, qacc, kacc):
        i = pl.program_id(0)

        @pl.when(i == 0)
        def _():
            qacc[...] = jnp.zeros_like(qacc)
            kacc[...] = jnp.zeros_like(kacc)
            dqg_ref[...] = jnp.zeros_like(dqg_ref)
            dkg_ref[...] = jnp.zeros_like(dkg_ref)

        x = x_ref[...]
        qg = qg_ref[1:2, :]
        kg = kg_ref[...]
        dq_v = dq_ref[...]
        dkv_v = dkv_ref[...].astype(BF)
        qacc[...] += _dot_tn(_rms_fwd(x, qg).astype(BF), dq_v)
        kacc[...] += _dot_tn(_rms_fwd(x, kg).astype(BF), dkv_v)
        dxq, prod_q = _rms_bwd(x, qg, _dot_nt(dq_v, wq_ref[...]))
        dxk, prod_k = _rms_bwd(x, kg, _dot_nt(dkv_v, wkv_ref[...]))
        dqg_ref[...] += _rowsum(prod_q)
        dkg_ref[...] += _rowsum(prod_k)
        dx3_ref[...] = dx_ref[...] + dxq + dxk

        @pl.when(i == n - 1)
        def _():
            dwq_ref[...] = qacc[...].astype(BF)
            dwkv_ref[...] = kacc[...].astype(BF)

    outs, _ = _call(
        body, name="qkv_bwd", grid=(n,),
        in_specs=[_row_spec(TM), _row_spec(TM, 2 * KVD), _row_spec(TM), _row_spec(TM), VSPEC, VSPEC, VSPEC, VSPEC],
        out_specs=[_row_spec(TM), _const_spec((D, D)), _const_spec((D, 2 * KVD)),
                   _const_spec((1, D)), _const_spec((1, D))],
        out_shape=[_sds((s_len, D)), _sds((D, D), BF), _sds((D, 2 * KVD), BF), _sds((1, D)), _sds((1, D))],
        scratch_shapes=[pltpu.VMEM((D, D), F32), pltpu.VMEM((D, 2 * KVD), F32)],
        args=[dq, dkv, x3, dx4, q_g, kv_g, w_q, w_kv])
    return outs


def _attn_group(i, q, kvw, sink_ref, g):
    rows = GQA * BLK
    heads = [GQA * g + j for j in range(GQA)]
    off = jnp.where(i > 0, BLK, 0)
    row = lax.broadcasted_iota(jnp.int32, (rows, 2 * BLK), 0)
    rel = (row % BLK) - lax.broadcasted_iota(jnp.int32, (rows, 2 * BLK), 1) + off
    valid = (rel >= 0) & (rel < BLK)
    head_of_row = lax.broadcasted_iota(jnp.int32, (rows, 1), 0) // BLK
    slope = jnp.zeros((rows, 1), F32)
    sink = jnp.zeros((rows, 1), F32)
    for j, h in enumerate(heads):
        slope = jnp.where(head_of_row == j, SLOPES[h], slope)
        sink = jnp.where(head_of_row == j, sink_ref[0, h], sink)
    qs = jnp.concatenate([q[:, h * HEAD_DIM:(h + 1) * HEAD_DIM] for h in heads], axis=0)
    k = kvw[:, g * HEAD_DIM:(g + 1) * HEAD_DIM]
    v = kvw[:, KVD + g * HEAD_DIM:KVD + (g + 1) * HEAD_DIM]
    s = _dot_nt(qs, k) * ATT_SCALE - slope * rel.astype(F32)
    s = jnp.where(valid, s, NEG_INF)
    m = jnp.maximum(jnp.max(s, axis=-1, keepdims=True), sink)
    e = jnp.exp(s - m)
    es = jnp.exp(sink - m)
    inv = 1.0 / (jnp.sum(e, axis=-1, keepdims=True) + es)
    return e * inv, es * inv, qs, k, v


def _unstack_heads(stacked):
    return [stacked[j * BLK:(j + 1) * BLK, :] for j in range(GQA)]


def _kv_window(kv_ref, i):
    ks = pl.multiple_of(jnp.maximum(i * BLK - BLK, 0), BLK)
    return ks, kv_ref[pl.ds(ks, 2 * BLK), :]


def _attn_fwd(q, kv, sinks, x3, w_o, post_g, rider=None):
    s_len = q.shape[0]

    def body(q_ref, kv_ref, sk_ref, x_ref, wo_ref, g_ref, a_ref, y_ref, x4_ref):
        i = pl.program_id(0)
        _, kvw = _kv_window(kv_ref, i)
        q = q_ref[...]
        outs = []
        for g in range(N_KV_HEADS):
            p, _, _, _, v = _attn_group(i, q, kvw, sk_ref, g)
            outs += _unstack_heads(_dot(p.astype(BF), v))
        attn = jnp.concatenate(outs, axis=1)
        a_ref[...] = attn
        y = _dot(attn.astype(BF), wo_ref[...])
        y_ref[...] = y
        x4_ref[...] = x_ref[...] + _rms_fwd(y, g_ref[1:2, :])

    return _call(body, name="attn_fwd", grid=(s_len // BLK,),
                 in_specs=[_row_spec(BLK), VSPEC, SSPEC, _row_spec(BLK), VSPEC, VSPEC],
                 out_specs=[_row_spec(BLK)] * 3, out_shape=[_sds((s_len, D))] * 3,
                 args=[q, kv, sinks, x3, w_o, post_g], rider=rider)


ATT_STEP_BLOCKS = 2


def _attn_bwd(dx4, y, attn, q, kv, sinks, w_o, post_g, rider=None):
    s_len = q.shape[0]
    rows = ATT_STEP_BLOCKS * BLK
    n = s_len // rows

    def body(dx_ref, y_ref, a_ref, q_ref, kv_ref, sk_ref, wo_ref, g_ref,
             dq_ref, dkv_ref, dwo_ref, dg_ref, dsk_ref, wacc):
        i = pl.program_id(0)

        @pl.when(i == 0)
        def _():
            dkv_ref[...] = jnp.zeros_like(dkv_ref)
            wacc[...] = jnp.zeros_like(wacc)
            dg_ref[...] = jnp.zeros_like(dg_ref)
            dsk_ref[...] = jnp.zeros_like(dsk_ref)

        dy, prod = _rms_bwd(y_ref[...], g_ref[1:2, :], dx_ref[...])
        dg_ref[...] += _rowsum(prod)
        dyb = dy.astype(BF)
        attn_all = a_ref[...]
        wacc[...] += _dot_tn(attn_all.astype(BF), dyb)
        d_o_all = _dot_nt(dyb, wo_ref[...])
        q_all = q_ref[...]
        lane = lax.broadcasted_iota(jnp.int32, (1, D), 1)
        dsk = jnp.zeros((1, D), F32)
        for sub in range(ATT_STEP_BLOCKS):
            blk = i * ATT_STEP_BLOCKS + sub
            sl = slice(sub * BLK, (sub + 1) * BLK)
            d_o, q = d_o_all[sl, :], q_all[sl, :]
            dod = d_o * attn_all[sl, :]
            ks, kvw = _kv_window(kv_ref, blk)
            dqs, dks, dvs = [], [], []
            for g in range(N_KV_HEADS):
                p, ps, qs, k, v = _attn_group(blk, q, kvw, sk_ref, g)
                cols = [slice((GQA * g + j) * HEAD_DIM, (GQA * g + j + 1) * HEAD_DIM) for j in range(GQA)]
                do_s = jnp.concatenate([d_o[:, c] for c in cols], axis=0).astype(BF)
                dsum = jnp.concatenate([jnp.sum(dod[:, c], axis=-1, keepdims=True) for c in cols], axis=0)
                dp = _dot_nt(do_s, v)
                dsb = (p * (dp - dsum) * ATT_SCALE).astype(BF)
                sink_part = ps * dsum
                for j in range(GQA):
                    dsk = dsk + jnp.where(lane == GQA * g + j, -_rowsum(sink_part[j * BLK:(j + 1) * BLK, :]), 0.0)
                dqs += _unstack_heads(_dot(dsb, k))
                dks.append(_dot_tn(dsb, qs))
                dvs.append(_dot_tn(p.astype(BF), do_s))
            dq_ref[sl, :] = jnp.concatenate(dqs, axis=1).astype(BF)
            dkv_ref[pl.ds(ks, 2 * BLK), :] += jnp.concatenate(dks + dvs, axis=1)
        dsk_ref[...] += dsk

        @pl.when(i == n - 1)
        def _():
            dwo_ref[...] = wacc[...].astype(BF)

    return _call(
        body, name="attn_bwd", grid=(n,),
        in_specs=[_row_spec(rows), _row_spec(rows), _row_spec(rows), _row_spec(rows), VSPEC, SSPEC, VSPEC, VSPEC],
        out_specs=[_row_spec(rows), _const_spec((s_len, 2 * KVD)), _const_spec((D, D)),
                   _const_spec((1, D)), _const_spec((1, D))],
        out_shape=[_sds((s_len, D), BF), _sds((s_len, 2 * KVD)), _sds((D, D), BF), _sds((1, D)), _sds((1, D))],
        scratch_shapes=[pltpu.VMEM((D, D), F32)],
        args=[dx4, y, attn, q, kv, sinks, w_o, post_g], rider=rider)


Big = collections.namedtuple("Big", "name src layer L A R C rb")


def _bigs():
    out = {"pool_w": Big("pool_w", "pool_w", None, 4, 4, POOL_G // N_CHIPS, POOL_G, 32)}
    for l in range(2):
        out[f"w_gu{l}"] = Big(f"w_gu{l}", "w_gu", l, 1, 2, D, FF_HALF, 256)
        out[f"w_down{l}"] = Big(f"w_down{l}", "w_down", l, 1, 4, FF // N_CHIPS, D, 352)
        out[f"w_ple_gate{l}"] = Big(f"w_ple_gate{l}", "w_ple_gate", l, 1, 4, D // N_CHIPS, D, 128)
        out[f"w_ple_proj{l}"] = Big(f"w_ple_proj{l}", "w_ple_proj", l, 1, 1, PLE, D // N_CHIPS, 128)
    out["w_q"] = Big("w_q", "w_q", None, 1, 4, D // N_CHIPS, D, 128)
    out["w_o"] = Big("w_o", "w_o", None, 1, 4, D // N_CHIPS, D, 128)
    out["w_kv"] = Big("w_kv", "w_kv", None, 1, 4, D // N_CHIPS, 2 * KVD, 128)
    return out


BIGS = _bigs()
POOL_SCALE = Big("pool_scale", "pool_scale", None, 1, 1, 1, D // N_CHIPS, 1)
BIG_SOURCES = ("w_gu", "w_down", "w_ple_gate", "w_ple_proj", "w_q", "w_o", "w_kv", "pool_w")


def _ncb(t):
    return N_CHIPS // t.A


def _full_shape(t, rows=None):
    return (t.L, t.A, t.R if rows is None else rows, _ncb(t) * t.C)


def _slot_index(t, k):
    return k // _ncb(t), k % _ncb(t)


def _slot(ref, t, k, row0, rows):
    a, cb = _slot_index(t, k)
    return ref.at[:, a, pl.ds(row0, rows), pl.ds(pl.multiple_of(cb * t.C, 128), t.C)]


def _place_job(t, w, out_dtype=BF):
    nb = next((nb for nb in (8, 4, 2, 1) if t.R % (16 * nb) == 0), 1) if t.L == 1 else 1
    rb = t.R // nb

    def fn(j, kc_ref, ins, outs):
        outs[0][...] = ins[0][...].astype(out_dtype)

    def in_map(j, kc_ref):
        return (j // nb if t.layer is None else t.layer, j % nb, 0)

    def out_map(j, kc_ref):
        a, cb = _slot_index(t, kc_ref[0])
        return (j // nb, a, j % nb, cb)

    return Job(t.L * nb, [(w, (None, rb, t.C), in_map)],
               [(_sds(_full_shape(t), out_dtype), (None, None, rb, t.C), out_map)], fn)


def _mesh_position():
    x, y, c = lax.axis_index("x"), lax.axis_index("y"), lax.axis_index("c")
    chips = [(1 - x, y), (x, 1 - y), (1 - x, 1 - y)]
    return x, y, c, chips


DIRECT_BELOW = 1024


def _gather_rider(parts, fulls):
    nt = len(parts)
    TO_X, TO_Y, FWD_X, FWD_Y, SIB_X, SIB_Y, SIB_D = range(7)

    def rows_of(ti, core):
        t, r0, r1 = parts[ti]
        h = (r1 - r0) // 2
        return r0 + core * h, h

    def copy(outs, sems, kind, ti, k_src, row0, rows, dev):
        region = _slot(outs[ti], parts[ti][0], k_src, row0, rows)
        return pltpu.make_async_remote_copy(region, region, sems[0].at[ti, kind], sems[1].at[ti, kind],
                                            device_id=dev, device_id_type=MESH)

    def plan(outs, sems):
        x, y, c, _ = _mesh_position()
        me, kx, ky, kd = 2 * x + y, 2 * (1 - x) + y, 2 * x + (1 - y), 2 * (1 - x) + (1 - y)
        dev_x, dev_y, dev_d, sib = (1 - x, y, c), (x, 1 - y, c), (1 - x, 1 - y, c), (x, y, 1 - c)

        def whole(ti):
            return 0, parts[ti][0].R

        def mk(kind, k_send, k_recv, dev, send_rows, recv_rows):
            def build(ti, side):
                k_src = k_send if side == "s" else k_recv
                row0, rows = (send_rows if side == "s" else recv_rows)(ti)
                return copy(outs, sems, kind, ti, k_src, row0, rows, dev)
            return build

        def first_half(core):
            return lambda ti: (rows_of(ti, core)[0], rows_of(ti, core)[1] // 2)

        def second_half(core):
            return lambda ti: (rows_of(ti, core)[0] + rows_of(ti, core)[1] // 2, rows_of(ti, core)[1] // 2)

        mine = lambda ti: rows_of(ti, c)
        theirs = lambda ti: rows_of(ti, 1 - c)
        split = {
            TO_X: mk(TO_X, me, kx, dev_x, mine, mine),
            TO_Y: mk(TO_Y, me, ky, dev_y, mine, mine),
            FWD_X: mk(FWD_X, ky, kd, dev_x, first_half(c), first_half(c)),
            FWD_Y: mk(FWD_Y, kx, kd, dev_y, second_half(c), second_half(c)),
            SIB_X: mk(SIB_X, kx, kx, sib, mine, theirs),
            SIB_Y: mk(SIB_Y, ky, ky, sib, mine, theirs),
            SIB_D: mk(SIB_D, kd, kd, sib, mine, theirs),
        }
        direct = {
            TO_X: mk(TO_X, me, kx, dev_x, whole, whole),
            TO_Y: mk(TO_Y, me, ky, dev_y, whole, whole),
            FWD_X: mk(FWD_X, me, kd, dev_d, whole, whole),
        }
        return split, direct

    is_split = [t.L * t.R * t.C >= DIRECT_BELOW for t, _, _ in parts]
    assert all(s or (r0, r1) == (0, t.R) for s, (t, r0, r1) in zip(is_split, parts))

    def start(ins, outs, sems):
        split, direct = plan(outs, sems)
        for ti in range(nt):
            kinds = split if is_split[ti] else direct
            kinds[TO_X](ti, "s").start()
            kinds[TO_Y](ti, "s").start()
            if not is_split[ti]:
                kinds[FWD_X](ti, "s").start()

    def mid(ins, outs, sems):
        split, _ = plan(outs, sems)
        for ti in range(nt):
            if is_split[ti]:
                split[TO_Y](ti, "r").wait_recv()
                split[FWD_X](ti, "s").start()
                split[SIB_Y](ti, "s").start()
        for ti in range(nt):
            if is_split[ti]:
                split[TO_X](ti, "r").wait_recv()
                split[FWD_Y](ti, "s").start()
                split[SIB_X](ti, "s").start()

    def finish(ins, outs, sems):
        split, direct = plan(outs, sems)
        for ti in range(nt):
            if is_split[ti]:
                split[FWD_X](ti, "r").wait_recv()
                split[FWD_Y](ti, "r").wait_recv()
                split[SIB_D](ti, "s").start()
            else:
                for kind in (TO_X, TO_Y, FWD_X):
                    direct[kind](ti, "r").wait_recv()
        for ti in range(nt):
            if is_split[ti]:
                for kind in (SIB_X, SIB_Y, SIB_D):
                    split[kind](ti, "r").wait_recv()
        for ti in range(nt):
            kinds = split if is_split[ti] else direct
            for kind in kinds:
                kinds[kind](ti, "s").wait_send()

    sems = pltpu.SemaphoreType.DMA((nt, 7))
    return Rider(list(fulls), [_sds(a.shape, a.dtype) for a in fulls], {i: i for i in range(nt)},
                 [sems, sems], start, mid, finish)


def _pair_exchange_rider(specs, grads):
    nt = len(specs)

    def copy(ins, outs, sems, ti, c, sibling):
        half = specs[ti].R // 2
        return pltpu.make_async_remote_copy(ins[ti].at[:, :, pl.ds((1 - c) * half, half), :], outs[ti],
                                            sems[0].at[ti], sems[1].at[ti], device_id=sibling, device_id_type=MESH)

    def start(ins, outs, sems):
        x, y, c, _ = _mesh_position()
        for ti in range(nt):
            copy(ins, outs, sems, ti, c, (x, y, 1 - c)).start()

    def finish(ins, outs, sems):
        x, y, c, _ = _mesh_position()
        for ti in range(nt):
            copy(ins, outs, sems, ti, c, (x, y, 1 - c)).wait()

    sems = pltpu.SemaphoreType.DMA((nt,))
    return Rider(list(grads), [_sds(_full_shape(t, t.R // 2), BF) for t in specs], {}, [sems, sems], start, None, finish)


def _pair_sum_job(t, g, land):
    assert t.L == 1
    half = t.R // 2
    nj = half // t.rb
    block = (None, t.A, t.rb, _ncb(t) * t.C)

    def fn(j, kc_ref, ins, outs):
        outs[0][...] = (ins[0][...].astype(F32) + ins[1][...].astype(F32)).astype(BF)

    return Job(nj,
               [(g, block, lambda j, kc_ref: (0, 0, kc_ref[1] * nj + j, 0)),
                (land, block, lambda j, kc_ref: (0, 0, j, 0))],
               [(_sds(_full_shape(t, half), BF), block, lambda j, kc_ref: (0, 0, j, 0))], fn)


def _scatter_rider(specs, sums):
    nt = len(specs)

    def copy(ins, outs, sems, ti, j, chip, c):
        t = specs[ti]
        cx, cy = chip
        return pltpu.make_async_remote_copy(_slot(ins[ti], t, 2 * cx + cy, 0, t.R // 2), outs[ti].at[j],
                                            sems[0].at[ti, j], sems[1].at[ti, j],
                                            device_id=(cx, cy, c), device_id_type=MESH)

    def start(ins, outs, sems):
        _, _, c, chips = _mesh_position()
        for j, chip in enumerate(chips):
            for ti in range(nt):
                copy(ins, outs, sems, ti, j, chip, c).start()

    def finish(ins, outs, sems):
        _, _, c, chips = _mesh_position()
        for j, chip in enumerate(chips):
            for ti in range(nt):
                copy(ins, outs, sems, ti, j, chip, c).wait()

    sems = pltpu.SemaphoreType.DMA((nt, N_CHIPS - 1))
    return Rider(list(sums), [_sds((N_CHIPS - 1, t.L, t.R // 2, t.C), BF) for t in specs], {}, [sems, sems],
                 start, None, finish)


def _chip_sum_job(ts, landed):
    t0 = ts[0]
    assert t0.L == 1
    half = t0.R // 2
    nj = half // t0.rb

    def local(j, li):
        return jnp.clip(j - li * nj, 0, nj - 1)

    ins = []
    for li, t in enumerate(ts):
        s, land = landed[t.name]

        def own_map(j, kc_ref, li=li, t=t):
            a, cb = _slot_index(t, kc_ref[0])
            return (0, a, local(j, li), cb)

        ins.append((s, (None, None, t.rb, t.C), own_map))
        ins.append((land, (N_CHIPS - 1, None, t.rb, t.C), lambda j, kc_ref, li=li: (0, 0, local(j, li), 0)))

    def fn(j, kc_ref, in_refs, outs):
        for li in range(len(ts)):
            @pl.when(j // nj == li)
            def _():
                acc = in_refs[2 * li][...].astype(F32)
                for k in range(N_CHIPS - 1):
                    acc = acc + in_refs[2 * li + 1][k].astype(F32)
                outs[0][...] = acc

    return Job(len(ts) * nj, ins,
               [(_sds((len(ts), t0.R, t0.C)), (None, t0.rb, t0.C),
                 lambda j, kc_ref: (j // nj, kc_ref[1] * nj + j % nj, 0))], fn)


def _adamw_job(rb, w, g, m, v):
    n_layers, r, c = w.shape
    nb = r // rb
    block = (None, rb, c)
    index = lambda j, kc_ref: (j // nb, j % nb, 0)

    def fn(j, kc_ref, ins, outs):
        g_v = ins[1][...]
        outs[0][...] = g_v
        outs[1][...], outs[2][...], outs[3][...] = _adamw_math(ins[0][...], g_v, ins[2][...], ins[3][...])

    return Job(n_layers * nb, [(a, block, index) for a in (w, g, m, v)],
               [(_sds(w.shape), block, index)] * 4, fn)


def _chip_sum_fused_job(ts, fused, by_cols):
    t0 = ts[0]
    own0 = fused[t0.name][0]
    if by_cols:
        rows, cols = own0.shape
    else:
        nb, rows, bw = own0.shape
        cols = nb * bw
    nj = rows // t0.rb

    def local(j, li):
        return jnp.clip(j - li * nj, 0, nj - 1)

    ins = []
    for li, t in enumerate(ts):
        own, land = fused[t.name]
        if by_cols:
            ins.append((own, (t.rb, cols), lambda j, kc_ref, li=li: (local(j, li), 0)))
            ins.append((land, (N_CHIPS - 1, t.rb, cols), lambda j, kc_ref, li=li: (0, local(j, li), 0)))
        else:
            ins.append((own, (nb, t.rb, bw), lambda j, kc_ref, li=li: (0, local(j, li), 0)))
            ins.append((land, (N_CHIPS - 1, nb, t.rb, bw), lambda j, kc_ref, li=li: (0, 0, local(j, li), 0)))

    def fn(j, kc_ref, in_refs, outs):
        for li in range(len(ts)):
            @pl.when(j // nj == li)
            def _():
                acc = in_refs[2 * li][...].astype(F32)
                for k in range(N_CHIPS - 1):
                    acc = acc + in_refs[2 * li + 1][k].astype(F32)
                outs[0][...] = acc if by_cols else jnp.concatenate([acc[b] for b in range(nb)], axis=1)

    def out_map(j, kc_ref):
        return (j // nj, j % nj, kc_ref[1]) if by_cols else (j // nj, kc_ref[1] * nj + j % nj, 0)

    return Job(len(ts) * nj, ins, [(_sds((len(ts), t0.R, t0.C)), (None, t0.rb, cols), out_map)], fn)


def _share_rider(halves, by_cols):
    nt = len(halves)

    def copy(outs, sems, ti, core, sibling):
        axis = 2 if by_cols[ti] else 1
        half = halves[ti].shape[axis] // 2
        piece = pl.ds(pl.multiple_of(core * half, 128 if by_cols[ti] else 8), half)
        part = outs[ti].at[:, :, piece] if by_cols[ti] else outs[ti].at[:, piece, :]
        return pltpu.make_async_remote_copy(part, part, sems[0].at[ti], sems[1].at[ti],
                                            device_id=sibling, device_id_type=MESH)

    def start(ins, outs, sems):
        x, y, c, _ = _mesh_position()
        for ti in range(nt):
            copy(outs, sems, ti, c, (x, y, 1 - c)).start()

    def finish(ins, outs, sems):
        x, y, c, _ = _mesh_position()
        for ti in range(nt):
            copy(outs, sems, ti, 1 - c, (x, y, 1 - c)).wait_recv()
        for ti in range(nt):
            copy(outs, sems, ti, c, (x, y, 1 - c)).wait_send()

    sems = pltpu.SemaphoreType.DMA((nt,))
    return Rider(list(halves), [_sds(a.shape, a.dtype) for a in halves], {i: i for i in range(nt)}, [sems, sems],
                 start, None, finish)


def _both(r1, r2):
    assert r1.mid is None and r2.mid is None
    ni, no, ns = len(r1.arrays), len(r1.out_shapes), len(r1.scratch)

    def split(fn1, fn2):
        def run(ins, outs, scr):
            fn1(ins[:ni], outs[:no], scr[:ns])
            fn2(ins[ni:], outs[no:], scr[ns:])
        return run

    aliases = dict(r1.aliases)
    aliases.update({ni + a: no + b for a, b in r2.aliases.items()})
    return Rider(r1.arrays + r2.arrays, r1.out_shapes + r2.out_shapes, aliases, r1.scratch + r2.scratch,
                 split(r1.start, r2.start), None, split(r1.finish, r2.finish))


def _adamw_math(w, g, m, v):
    m = B1 * m + (1.0 - B1) * g
    v = B2 * v + (1.0 - B2) * (g * g)
    delta = -LR * ((m / BC1) / (jnp.sqrt(v / BC2) + AEPS) + WD * w)
    return delta, m, v


def _adamw(name, rb, w, g, m, v):
    n_layers, r, c = w.shape

    def body(w_ref, g_ref, m_ref, v_ref, go_ref, d_ref, nm_ref, nv_ref):
        g_v = g_ref[...]
        go_ref[...] = g_v
        d_ref[...], nm_ref[...], nv_ref[...] = _adamw_math(w_ref[...], g_v, m_ref[...], v_ref[...])

    spec = pl.BlockSpec((None, rb, c), lambda l, j: (l, j, 0))
    return pl.pallas_call(
        body, name=f"adamw_{name}", grid=(n_layers, r // rb),
        in_specs=[spec] * 4, out_specs=[spec] * 4, out_shape=[_sds(w.shape)] * 4,
        compiler_params=_params(2),
    )(w, g, m, v)


GAIN_ROWS = {"pre_mix_g": 0, "post_mix_g": 2, "pre_ffn_g": 4, "post_ffn_g": 6, "ple_g": 8, "ple_post_g": 10}
ROW_KV_G, ROW_POOL_SCALE, ROW_SINKS, ROW_LOSS, PACK_ROWS = 12, 13, 14, 15, 16
SMALL_NAMES = tuple(GAIN_ROWS) + ("kv_g", "pool_scale", "sinks")


def _small_all_reduce(rows, dpool, rider=None):
    ng, pr = len(WINDOWS), POOL_G // N_CHIPS

    def body(*refs):
        row_refs = refs[:PACK_ROWS]
        dpool_ref, tot_ref, gpool_ref, pack, land, pland, send, recv, psend, precv = refs[PACK_ROWS:]
        x, y, c, _ = _mesh_position()
        me = 4 * x + 2 * y + c
        for r in range(PACK_ROWS):
            pack[r:r + 1, :] = row_refs[r][...]

        def shard_of(k):
            return dpool_ref.at[:, pl.ds(pl.multiple_of(k * pr, pr), pr), :]

        cps = []
        for j in range(1, N_DEV):
            px, py, pc = x ^ (j >> 2), y ^ ((j >> 1) & 1), c ^ (j & 1)
            cps.append(pltpu.make_async_remote_copy(pack, land.at[me], send.at[j], recv.at[j],
                                                    device_id=(px, py, pc), device_id_type=MESH))
            cps.append(pltpu.make_async_remote_copy(shard_of(2 * px + py), pland.at[me], psend.at[j], precv.at[j],
                                                    device_id=(px, py, pc), device_id_type=MESH))
        for cp in cps:
            cp.start()
        land[me] = pack[...]
        pland[me] = dpool_ref[:, pl.ds(pl.multiple_of((2 * x + y) * pr, pr), pr), :]
        for j in range(1, N_DEV):
            pltpu.make_async_remote_copy(pack, land.at[me ^ j], send.at[j], recv.at[j],
                                         device_id=(x, y, c), device_id_type=MESH).wait_recv()
            pltpu.make_async_remote_copy(shard_of(0), pland.at[me ^ j], psend.at[j], precv.at[j],
                                         device_id=(x, y, c), device_id_type=MESH).wait_recv()
        for cp in cps:
            cp.wait_send()
        tot = land[0]
        gp = pland[0].astype(F32)
        for d in range(1, N_DEV):
            tot = tot + land[d]
            gp = gp + pland[d].astype(F32)
        tot_ref[...] = tot
        gpool_ref[...] = gp

    sems = pltpu.SemaphoreType.DMA((N_DEV,))
    return _call(
        body, name="small_all_reduce", grid=(1,),
        in_specs=[VSPEC] * (PACK_ROWS + 1), out_specs=[VSPEC, VSPEC],
        out_shape=[_sds((PACK_ROWS, D)), _sds((ng, pr, POOL_G))],
        scratch_shapes=[pltpu.VMEM((PACK_ROWS, D), F32), pltpu.VMEM((N_DEV, PACK_ROWS, D), F32),
                        pltpu.VMEM((N_DEV, ng, pr, POOL_G), BF), sems, sems, sems, sems],
        args=[*rows, dpool], rider=rider)


def _small_adamw(tot, kc, small_w, small_m, small_v):
    names = SMALL_NAMES
    n = len(names)

    def body(*refs):
        tot_ref, kc_ref = refs[0], refs[1]
        w_refs = dict(zip(names, refs[2:2 + n]))
        m_refs = dict(zip(names, refs[2 + n:2 + 2 * n]))
        v_refs = dict(zip(names, refs[2 + 2 * n:2 + 3 * n]))
        loss_ref = refs[2 + 3 * n]
        out_refs = {nm: refs[3 + 3 * n + 4 * k: 7 + 3 * n + 4 * k] for k, nm in enumerate(names)}
        tot = tot_ref[...]
        loss_ref[...] = 0.5 * jnp.sum(tot[ROW_LOSS:ROW_LOSS + 1, :], axis=-1, keepdims=True) * (1.0 / D)

        def update(nm, g):
            g_ref, d_ref, nm_ref, nv_ref = out_refs[nm]
            g_ref[...] = g
            d_ref[...], nm_ref[...], nv_ref[...] = _adamw_math(w_refs[nm][...], g, m_refs[nm][...], v_refs[nm][...])

        for nm, r in GAIN_ROWS.items():
            update(nm, tot[r:r + 2, :])
        update("kv_g", tot[ROW_KV_G:ROW_KV_G + 1, :])
        k = kc_ref[0]
        width = D // N_CHIPS
        g_scale = jnp.zeros((1, width), F32)
        for kk in range(N_CHIPS):
            g_scale = g_scale + jnp.where(k == kk, tot[ROW_POOL_SCALE:ROW_POOL_SCALE + 1, kk * width:(kk + 1) * width], 0.0)
        update("pool_scale", g_scale)
        update("sinks", tot[ROW_SINKS:ROW_SINKS + 1, 0:N_HEADS])

    ins = [tot, kc] + [small_w[nm] for nm in names] + [small_m[nm] for nm in names] + [small_v[nm] for nm in names]
    out_shape = [_sds((1, 1))]
    for nm in names:
        out_shape += [_sds(small_w[nm].shape)] * 4
    outs = pl.pallas_call(
        body, name="small_adamw",
        in_specs=[VSPEC, SSPEC] + [VSPEC] * (3 * n), out_specs=[VSPEC] * len(out_shape), out_shape=out_shape,
        compiler_params=_params(),
    )(*ins)
    return outs[0], {nm: outs[1 + 4 * k: 5 + 4 * k] for k, nm in enumerate(names)}


def _compute_layout(t, full):
    if t.src == "w_gu":
        return full.reshape(2, D, FF)
    if t.src == "pool_w":
        return full.reshape(len(WINDOWS), POOL_G, POOL_G)
    if t.src == "pool_scale":
        return full.reshape(1, D)
    return full.reshape(t.A * t.R, _ncb(t) * t.C)


def kernel(x, p, pre_mix_g, post_mix_g, pre_ffn_g, post_ffn_g, pool_w, pool_scale, kv_g, w_kv, w_q, sinks, w_o, w_gu, w_down, ple_g, w_ple_gate, w_ple_proj, ple_post_g, loss_target, m_pre_mix_g, m_post_mix_g, m_pre_ffn_g, m_post_ffn_g, m_pool_w, m_pool_scale, m_kv_g, m_w_kv, m_w_q, m_sinks, m_w_o, m_w_gu, m_w_down, m_ple_g, m_w_ple_gate, m_w_ple_proj, m_ple_post_g, v_pre_mix_g, v_post_mix_g, v_pre_ffn_g, v_post_ffn_g, v_pool_w, v_pool_scale, v_kv_g, v_w_kv, v_w_q, v_sinks, v_w_o, v_w_gu, v_w_down, v_ple_g, v_w_ple_gate, v_w_ple_proj, v_ple_post_g):
    weights = dict(pre_mix_g=pre_mix_g, post_mix_g=post_mix_g, pre_ffn_g=pre_ffn_g, post_ffn_g=post_ffn_g,
                   pool_w=pool_w, pool_scale=pool_scale, kv_g=kv_g, w_kv=w_kv, w_q=w_q, sinks=sinks, w_o=w_o,
                   w_gu=w_gu, w_down=w_down, ple_g=ple_g, w_ple_gate=w_ple_gate, w_ple_proj=w_ple_proj,
                   ple_post_g=ple_post_g)
    m_in = dict(pre_mix_g=m_pre_mix_g, post_mix_g=m_post_mix_g, pre_ffn_g=m_pre_ffn_g, post_ffn_g=m_post_ffn_g,
                pool_w=m_pool_w, pool_scale=m_pool_scale, kv_g=m_kv_g, w_kv=m_w_kv, w_q=m_w_q, sinks=m_sinks,
                w_o=m_w_o, w_gu=m_w_gu, w_down=m_w_down, ple_g=m_ple_g, w_ple_gate=m_w_ple_gate,
                w_ple_proj=m_w_ple_proj, ple_post_g=m_ple_post_g)
    v_in = dict(pre_mix_g=v_pre_mix_g, post_mix_g=v_post_mix_g, pre_ffn_g=v_pre_ffn_g, post_ffn_g=v_post_ffn_g,
                pool_w=v_pool_w, pool_scale=v_pool_scale, kv_g=v_kv_g, w_kv=v_w_kv, w_q=v_w_q, sinks=v_sinks,
                w_o=v_w_o, w_gu=v_w_gu, w_down=v_w_down, ple_g=v_ple_g, w_ple_gate=v_w_ple_gate,
                w_ple_proj=v_w_ple_proj, ple_post_g=v_ple_post_g)
    order = ["pre_mix_g", "post_mix_g", "pre_ffn_g", "post_ffn_g", "pool_w", "pool_scale", "kv_g", "w_kv", "w_q",
             "sinks", "w_o", "w_gu", "w_down", "ple_g", "w_ple_gate", "w_ple_proj", "ple_post_g"]

    kc = jnp.stack([2 * lax.axis_index("x") + lax.axis_index("y"), lax.axis_index("c")]).astype(jnp.int32)
    s_len = x.shape[1]
    x2d = x.reshape(s_len, D)
    p3d = p.reshape(2, s_len, PLE)
    target = loss_target.reshape(s_len, D)
    kv_g2d = kv_g.reshape(1, D)
    gains = {nm: weights[nm] for nm in GAIN_ROWS}

    def shard_view(src, a):
        t = next(t for t in BIGS.values() if t.src == src)
        return a.reshape(-1, t.R, t.C)

    first, second = ["pool_w", "pool_scale"], ["w_gu0", "w_down0"]
    rest = [nm for nm in BIGS if nm not in first + second]
    specs = dict(BIGS, pool_scale=POOL_SCALE)
    placed = {}

    def place_job(nm):
        if nm == "pool_scale":
            return _place_job(POOL_SCALE, pool_scale.reshape(1, 1, D // N_CHIPS), F32)
        return _place_job(BIGS[nm], shard_view(BIGS[nm].src, weights[BIGS[nm].src]))

    def gather(names, rows=None):
        rows = rows or {}
        parts = [(specs[nm],) + tuple(rows.get(nm, (0, specs[nm].R))) for nm in names]
        return _gather_rider(parts, [placed[nm] for nm in names])

    def take(names, results):
        for nm, a in zip(names, results):
            placed[nm] = a

    def weight(nm):
        return _compute_layout(specs[nm], placed[nm])

    take(first, [r[0] for r in _multi_call("place_pool", [place_job(nm) for nm in first], kc)])
    cast, got = _multi_call("place_ffn0", [place_job(nm) for nm in second], kc, rider=gather(first))
    take(second, [r[0] for r in cast])
    take(first, got)
    jobs = [place_job(nm) for nm in rest]
    jobs.append(_mixa_fwd_job(x2d, gains["pre_mix_g"], weight("pool_w"), weight("pool_scale"), gains["post_mix_g"]))
    results, got = _multi_call("cast_and_mixa_fwd", jobs, kc, rider=gather(second))
    take(rest, [r[0] for r in results[:-1]])
    take(second, got)
    y0, x1 = results[-1]

    ride = ["w_ple_gate0", "w_ple_proj0", "w_q", "w_kv", "w_o", "w_gu1"]
    (f0, x2, g0, u0), got = _ffn_fwd(0, x1, gains["pre_ffn_g"], weight("w_gu0"), weight("w_down0"), gains["post_ffn_g"],
                             rider=gather(ride, {"w_gu1": (0, 320)}))
    take(ride, got)

    ride = ["w_ple_gate1", "w_ple_proj1", "w_gu1"]
    (z0, pe0, x3, q, kv), got = _ple_fwd(
        0, x2, p3d, gains["ple_g"], weight("w_ple_gate0"), weight("w_ple_proj0"), gains["ple_post_g"],
        qkv=(gains["pre_mix_g"], kv_g2d, weight("w_q"), weight("w_kv")),
        rider=gather(ride, {"w_gu1": (320, 704)}))
    take(ride, got)

    ride = ["w_down1", "w_gu1"]
    (attn, y1, x4), got = _attn_fwd(q, kv, sinks, x3, weight("w_o"), gains["post_mix_g"],
                                    rider=gather(ride, {"w_gu1": (704, D)}))
    take(ride, got)

    (f1, x5, g1, u1), _ = _ffn_fwd(1, x4, gains["pre_ffn_g"], weight("w_gu1"), weight("w_down1"), gains["post_ffn_g"])
    (z1, pe1, dx6, loss_row), _ = _ple_fwd(1, x5, p3d, gains["ple_g"], weight("w_ple_gate1"), weight("w_ple_proj1"),
                                           gains["ple_post_g"], target=target)

    local = {}
    landed = {}
    fused = {}

    def local_grads(names):
        return [local[nm].reshape(_full_shape(BIGS[nm])) for nm in names]

    def pair_exchange(names):
        return _pair_exchange_rider([BIGS[nm] for nm in names], local_grads(names))

    def pair_sum(tag, names, lands):
        jobs = [_pair_sum_job(BIGS[nm], g, l) for nm, g, l in zip(names, local_grads(names), lands)]
        return [r[0] for r in _multi_call(f"pair_sum_{tag}", jobs, kc)]

    def scatter(names, sums):
        return _scatter_rider([BIGS[nm] for nm in names], sums)

    def keep(names, sums, got):
        for nm, s, l in zip(names, sums, got):
            landed[nm] = (s, l)

    (dx5, local["w_ple_gate1"], local["w_ple_proj1"], d_ple1, d_plepost1), _ = _ple_bwd(
        1, dx6, x5, z1, pe1, p3d, gains["ple_g"], weight("w_ple_gate1"), gains["ple_post_g"])

    group_a = ["w_ple_gate1", "w_ple_proj1"]
    (dx4, d_preffn1, d_postffn1, *scattered), lands_a = _ffn_bwd(
        1, dx5, x4, f1, g1, u1, gains["pre_ffn_g"], weight("w_gu1"), weight("w_down1"), gains["post_ffn_g"], kc,
        rider=pair_exchange(group_a))
    fused["w_gu1"], fused["w_down1"] = scattered[0:2], scattered[2:4]
    sums_a = pair_sum("a", group_a, lands_a)

    (dq, dkv, local["w_o"], d_postmix1, d_sinks), got = _attn_bwd(
        dx4, y1, attn, q, kv, sinks, weight("w_o"), gains["post_mix_g"], rider=scatter(group_a, sums_a))
    keep(group_a, sums_a, got)
    dx3, local["w_q"], local["w_kv"], d_premix1, d_kvg = _qkv_bwd(
        dq, dkv, x3, dx4, gains["pre_mix_g"], kv_g2d, weight("w_q"), weight("w_kv"))

    group_b = ["w_o", "w_q", "w_kv"]
    sums_b = pair_sum("b", group_b, _run("grads_pair_exchange_b", pair_exchange(group_b)))
    (dx2, local["w_ple_gate0"], local["w_ple_proj0"], d_ple0, d_plepost0), got = _ple_bwd(
        0, dx3, x2, z0, pe0, p3d, gains["ple_g"], weight("w_ple_gate0"), gains["ple_post_g"],
        rider=scatter(group_b, sums_b))
    keep(group_b, sums_b, got)

    group_c = ["w_ple_gate0", "w_ple_proj0"]
    (dx1, d_preffn0, d_postffn0, *scattered), lands_c = _ffn_bwd(
        0, dx2, x1, f0, g0, u0, gains["pre_ffn_g"], weight("w_gu0"), weight("w_down0"), gains["post_ffn_g"], kc,
        rider=pair_exchange(group_c))
    fused["w_gu0"], fused["w_down0"] = scattered[0:2], scattered[2:4]
    sums_c = pair_sum("c", group_c, lands_c)

    (dx0, d_pool, d_scale, d_postmix0, d_premix0), _ = _mixa_bwd(
        dx1, x2d, y0, gains["pre_mix_g"], weight("pool_w"), weight("pool_scale"), gains["post_mix_g"])

    rows = [d_premix0, d_premix1, d_postmix0, d_postmix1, d_preffn0, d_preffn1, d_postffn0, d_postffn1,
            d_ple0, d_ple1, d_plepost0, d_plepost1, d_kvg, d_scale, d_sinks, loss_row]
    as2d = lambda a: a.reshape(1, D) if a.ndim == 1 else a
    layers_of = lambda src: [t for t in BIGS.values() if t.src == src]
    own_scatter = ["w_gu", "w_down"]
    early = own_scatter + ["w_q", "w_o", "w_kv"]
    late = ["w_ple_gate", "w_ple_proj"]
    by_cols = lambda srcs: [src == "w_down" for src in srcs]
    jobs = [_chip_sum_fused_job(layers_of(src), fused, by_cols=src == "w_down") for src in own_scatter]
    jobs += [_chip_sum_job(layers_of(src), landed) for src in early if src not in own_scatter]
    halves = [r[0] for r in _multi_call("chip_sum_early", jobs, kc)]
    (tot, g_pool), got = _small_all_reduce(
        rows, d_pool, rider=_both(scatter(group_c, sums_c), _share_rider(halves, by_cols(early))))
    keep(group_c, sums_c, got[:len(group_c)])
    full_grads = dict(zip(early, got[len(group_c):]))
    loss, small = _small_adamw(tot, kc, {nm: as2d(weights[nm]) for nm in SMALL_NAMES},
                               {nm: as2d(m_in[nm]) for nm in SMALL_NAMES},
                               {nm: as2d(v_in[nm]) for nm in SMALL_NAMES})

    halves = [r[0] for r in _multi_call("chip_sum_late", [_chip_sum_job(layers_of(src), landed) for src in late], kc)]
    full_grads.update(zip(late, _run("grads_pair_share", _share_rider(halves, by_cols(late)))))
    full_grads["pool_w"] = g_pool
    others = [src for src in BIG_SOURCES if src not in own_scatter and src != "pool_w"]

    def adam_args(src):
        return (layers_of(src)[0].rb, shard_view(src, weights[src]), full_grads[src],
                shard_view(src, m_in[src]), shard_view(src, v_in[src]))

    out = {"grad": {}, "delta": {}, "new_m": {}, "new_v": {}}
    results = {src: _adamw(src, *adam_args(src)) for src in own_scatter}
    rest_srcs = others + ["pool_w"]
    results.update(zip(rest_srcs, _multi_call("adamw_rest", [_adamw_job(*adam_args(src)) for src in rest_srcs], kc)))
    for src in BIG_SOURCES:
        shape = weights[src].shape
        for kind, a in zip(("grad", "delta", "new_m", "new_v"), results[src]):
            out[kind][src] = a.reshape(shape)
    for nm in SMALL_NAMES:
        shape = weights[nm].shape
        for kind, a in zip(("grad", "delta", "new_m", "new_v"), small[nm]):
            out[kind][nm] = a.reshape(shape)

    return (loss.reshape(()), dx0.reshape(x.shape),
            *[out["grad"][nm] for nm in order], *[out["delta"][nm] for nm in order],
            *[out["new_m"][nm] for nm in order], *[out["new_v"][nm] for nm in order])
```

```python
import collections

import jax
import jax.numpy as jnp
from jax import lax
from jax.experimental import pallas as pl
from jax.experimental.pallas import tpu as pltpu

D = 1024
FF = 2816
N_HEADS = 16
HEAD_DIM = 64
N_KV_HEADS = 4
GQA = N_HEADS // N_KV_HEADS
KVD = N_KV_HEADS * HEAD_DIM
PLE = 256
BLK = 128
WINDOWS = (2, 4, 8, 16)
POOL_G = 256
HALO = 16
EPS = 1e-6
NEG_INF = -1e30
ATT_SCALE = HEAD_DIM ** -0.5
SLOPES = tuple(2.0 ** (-8.0 * (h + 1) / N_HEADS) for h in range(N_HEADS))
N_CHIPS = 4
N_DEV = 8

LR, B1, B2, AEPS, WD, STEP = 0.001, 0.9, 0.999, 1e-08, 0.01, 10
BC1 = 1.0 - B1 ** STEP
BC2 = 1.0 - B2 ** STEP

BF = jnp.bfloat16
F32 = jnp.float32
MESH = pl.DeviceIdType.MESH
VMEM_LIMIT_V7X = 58 * 1024 * 1024
TM = 256
TM_FFN_BWD = 512
FF_CHUNK = 256
FF_HALF = FF // 2

VSPEC = pl.BlockSpec(memory_space=pltpu.VMEM)
SSPEC = pl.BlockSpec(memory_space=pltpu.SMEM)
ANYSPEC = pl.BlockSpec(memory_space=pl.ANY)


def _params(n_grid=0):
    sem = ("arbitrary",) * n_grid if n_grid else None
    return pltpu.CompilerParams(dimension_semantics=sem, vmem_limit_bytes=VMEM_LIMIT_V7X)


def _sds(shape, dtype=F32):
    return jax.ShapeDtypeStruct(tuple(shape), dtype)


Rider = collections.namedtuple("Rider", "arrays out_shapes aliases scratch start mid finish")
MID_NUM, MID_DEN = 5, 8


def _call(body, *, name, grid, in_specs, out_specs, out_shape, args, scratch_shapes=(), rider=None, prefetch=None):
    ni, no, ns = len(in_specs), len(out_specs), len(scratch_shapes)
    npre = 0 if prefetch is None else 1
    pre = [] if prefetch is None else [prefetch]
    if rider is None:
        rider = Rider([], [], {}, [], None, None, None)
    ri, ro = len(rider.arrays), len(rider.out_shapes)

    def full(*refs):
        pre_refs, refs = refs[:npre], refs[npre:]
        ins, refs = refs[:ni], refs[ni:]
        rins, refs = refs[:ri], refs[ri:]
        outs, refs = refs[:no], refs[no:]
        routs, refs = refs[:ro], refs[ro:]
        scr, rscr = refs[:ns], refs[ns:]
        ids = [pl.program_id(a) for a in range(len(grid))]
        first = ids[0] == 0
        last = ids[0] == grid[0] - 1
        for a in range(1, len(grid)):
            first = first & (ids[a] == 0)
            last = last & (ids[a] == grid[a] - 1)

        if rider.start is not None:
            @pl.when(first)
            def _():
                rider.start(rins, routs, rscr)

        if rider.mid is not None:
            assert len(grid) == 1

            @pl.when(ids[0] == (grid[0] * MID_NUM) // MID_DEN)
            def _():
                rider.mid(rins, routs, rscr)

        body(*pre_refs, *ins, *outs, *scr)

        if rider.finish is not None:
            @pl.when(last)
            def _():
                rider.finish(rins, routs, rscr)

    outs = pl.pallas_call(
        full, name=name,
        grid_spec=pltpu.PrefetchScalarGridSpec(
            num_scalar_prefetch=npre, grid=grid,
            in_specs=list(in_specs) + [ANYSPEC] * ri, out_specs=list(out_specs) + [ANYSPEC] * ro,
            scratch_shapes=list(scratch_shapes) + list(rider.scratch)),
        out_shape=list(out_shape) + list(rider.out_shapes),
        input_output_aliases={npre + ni + a: no + b for a, b in rider.aliases.items()},
        compiler_params=_params(len(grid)))(*pre, *args, *rider.arrays)
    return list(outs[:no]), list(outs[no:])


def _run(name, rider):
    ri = len(rider.arrays)

    def body(*refs):
        rins, routs, rscr = refs[:ri], refs[ri:ri + len(rider.out_shapes)], refs[ri + len(rider.out_shapes):]
        rider.start(rins, routs, rscr)
        if rider.mid is not None:
            rider.mid(rins, routs, rscr)
        rider.finish(rins, routs, rscr)

    return pl.pallas_call(
        body, name=name, in_specs=[ANYSPEC] * ri, out_specs=[ANYSPEC] * len(rider.out_shapes),
        out_shape=list(rider.out_shapes), scratch_shapes=list(rider.scratch),
        input_output_aliases=dict(rider.aliases), compiler_params=_params())(*rider.arrays)


Job = collections.namedtuple("Job", "steps ins outs fn")


def _multi_call(name, jobs, kc, rider=None):
    n = max(job.steps for job in jobs)

    def clamped(index, steps):
        return lambda s, kc_ref: index(jnp.minimum(s, steps - 1), kc_ref)

    in_specs, out_specs, out_shape, args = [], [], [], []
    for job in jobs:
        for arr, block, index, *single in job.ins:
            mode = dict(pipeline_mode=pl.Buffered(1)) if single and single[0] else {}
            in_specs.append(pl.BlockSpec(block, clamped(index, job.steps), **mode))
            args.append(arr)
        for sds, block, index in job.outs:
            out_specs.append(pl.BlockSpec(block, clamped(index, job.steps)))
            out_shape.append(sds)
    n_in = len(args)

    def body(kc_ref, *refs):
        s = pl.program_id(0)
        i0, o0 = 0, n_in
        for job in jobs:
            ins, outs = refs[i0:i0 + len(job.ins)], refs[o0:o0 + len(job.outs)]
            i0, o0 = i0 + len(job.ins), o0 + len(job.outs)

            @pl.when(s < job.steps)
            def _():
                job.fn(s, kc_ref, ins, outs)

    outs, routs = _call(body, name=name, grid=(n,), in_specs=in_specs, out_specs=out_specs, out_shape=out_shape,
                        args=args, prefetch=kc, rider=rider)
    res, o0 = [], 0
    for job in jobs:
        res.append(outs[o0:o0 + len(job.outs)])
        o0 += len(job.outs)
    return res if rider is None else (res, routs)


def _rms_fwd(x, g):
    r = lax.rsqrt(jnp.mean(x * x, axis=-1, keepdims=True) + EPS)
    return x * r * g


def _rms_bwd(x, g, dy):
    r = lax.rsqrt(jnp.mean(x * x, axis=-1, keepdims=True) + EPS)
    xn = x * r
    dxn = dy * g
    dx = r * (dxn - xn * jnp.mean(dxn * xn, axis=-1, keepdims=True))
    return dx, dy * xn


def _rowsum(a):
    return jnp.sum(a, axis=0, keepdims=True)


def _sigmoid(z):
    return 1.0 / (1.0 + jnp.exp(-z))


def _dot(a, b):
    return jnp.dot(a, b, preferred_element_type=F32)


def _dot_nt(a, b):
    return lax.dot_general(a, b, (((1,), (1,)), ((), ())), preferred_element_type=F32)


def _dot_tn(a, b):
    return lax.dot_general(a, b, (((0,), (0,)), ((), ())), preferred_element_type=F32)


def _row_spec(tm, width=D):
    return pl.BlockSpec((tm, width), lambda i: (i, 0))


def _const_spec(shape):
    zeros = (0,) * len(shape)
    return pl.BlockSpec(tuple(shape), lambda *_: zeros)


def _pool_delta(he, pos):
    out = []
    for gi, w in enumerate(WINDOWS):
        hg = he[:, gi * POOL_G:(gi + 1) * POOL_G]
        s = hg
        k = 1
        while k < w:
            s = s + pltpu.roll(s, k, 0)
            k *= 2
        cnt = jnp.maximum(jnp.minimum(pos + 1, w), 1).astype(F32)
        out.append(s / cnt - hg)
    return out


def _load_with_halo_before(x_ref, i, tm):
    r0 = pl.multiple_of(i * tm, tm)
    hs = pl.multiple_of(jnp.maximum(i * tm - HALO, 0), 8)
    xh = jnp.where(i > 0, x_ref[pl.ds(hs, HALO), :], 0.0)
    xt = x_ref[pl.ds(r0, tm), :]
    return xt, jnp.concatenate([xh, xt], axis=0)


def _mixa_fwd_job(x, pre_g, pool_w, pool_scale, post_g):
    s_len = x.shape[0]

    def fn(i, kc_ref, ins, outs):
        x_ref, pg_ref, w_ref, sc_ref, qg_ref = ins
        y_ref, x1_ref = outs
        xt, xe = _load_with_halo_before(x_ref, i, TM)
        he = _rms_fwd(xe, pg_ref[0:1, :])
        pos = i * TM - HALO + lax.broadcasted_iota(jnp.int32, (TM + HALO, 1), 0)
        ds = _pool_delta(he, pos)
        ys = [_dot(ds[gi][HALO:, :].astype(BF), w_ref[gi]) for gi in range(len(WINDOWS))]
        y = jnp.concatenate(ys, axis=1) * sc_ref[...]
        y_ref[...] = y
        x1_ref[...] = xt + _rms_fwd(y, qg_ref[0:1, :])

    def whole(a):
        zeros = (0,) * a.ndim
        return (a, a.shape, lambda j, kc_ref: zeros, True)

    rows = lambda j, kc_ref: (j, 0)
    return Job(s_len // TM, [whole(a) for a in (x, pre_g, pool_w, pool_scale, post_g)],
               [(_sds((s_len, D)), (TM, D), rows), (_sds((s_len, D)), (TM, D), rows)], fn)


def _mixa_bwd(dx1, x, y, pre_g, pool_w, pool_scale, post_g, rider=None):
    s_len = x.shape[0]
    n = s_len // TM
    ng = len(WINDOWS)

    def body(dx_ref, x_ref, y_ref, pg_ref, w_ref, sc_ref, qg_ref,
             dx0_ref, dw_ref, dsc_ref, dqg_ref, dpg_ref, wacc):
        i = pl.program_id(0)

        @pl.when(i == 0)
        def _():
            wacc[...] = jnp.zeros_like(wacc)
            dsc_ref[...] = jnp.zeros_like(dsc_ref)
            dqg_ref[...] = jnp.zeros_like(dqg_ref)
            dpg_ref[...] = jnp.zeros_like(dpg_ref)

        r0 = pl.multiple_of(i * TM, TM)
        xt, xe = _load_with_halo_before(x_ref, i, TM)
        he = _rms_fwd(xe, pg_ref[0:1, :])
        pos_b = i * TM - HALO + lax.broadcasted_iota(jnp.int32, (TM + HALO, 1), 0)
        ds = _pool_delta(he, pos_b)

        last = i == n - 1
        a0 = pl.multiple_of(jnp.minimum(i * TM + TM, s_len - HALO), 8)
        ye = jnp.concatenate([y_ref[pl.ds(r0, TM), :], y_ref[pl.ds(a0, HALO), :]], axis=0)
        dt = dx_ref[pl.ds(r0, TM), :]
        de = jnp.concatenate([dt, jnp.where(last, 0.0, dx_ref[pl.ds(a0, HALO), :])], axis=0)
        dye, prod = _rms_bwd(ye, qg_ref[0:1, :], de)
        dqg_ref[...] += _rowsum(prod[:TM, :])
        dys = dye * sc_ref[...]
        pos_a = i * TM + lax.broadcasted_iota(jnp.int32, (TM + HALO, 1), 0)

        dhs, dscs = [], []
        for gi, w in enumerate(WINDOWS):
            sl = slice(gi * POOL_G, (gi + 1) * POOL_G)
            wg = w_ref[gi]
            dys_g = dys[:, sl].astype(BF)
            d_g = ds[gi][HALO:, :].astype(BF)
            ypre = _dot(d_g, wg)
            dscs.append(_rowsum(dye[:TM, sl] * ypre))
            wacc[gi] += _dot_tn(d_g, dys_g[:TM, :])
            dd = _dot_nt(dys_g, wg)
            cnt = jnp.minimum(pos_a + 1, w).astype(F32)
            a = dd / cnt
            k = 1
            while k < w:
                a = a + pltpu.roll(a, TM + HALO - k, 0)
                k *= 2
            dhs.append(a[:TM, :] - dd[:TM, :])
        dsc_ref[...] += jnp.concatenate(dscs, axis=1)
        dh = jnp.concatenate(dhs, axis=1)
        dxp, prod2 = _rms_bwd(xt, pg_ref[0:1, :], dh)
        dpg_ref[...] += _rowsum(prod2)
        dx0_ref[...] = dt + dxp

        @pl.when(last)
        def _():
            dw_ref[...] = wacc[...].astype(BF)

    return _call(
        body, name="mixa_bwd", grid=(n,), in_specs=[VSPEC] * 7,
        out_specs=[_row_spec(TM), _const_spec((ng, POOL_G, POOL_G)), _const_spec((1, D)),
                   _const_spec((1, D)), _const_spec((1, D))],
        out_shape=[_sds((s_len, D)), _sds((ng, POOL_G, POOL_G), BF), _sds((1, D)), _sds((1, D)), _sds((1, D))],
        scratch_shapes=[pltpu.VMEM((ng, POOL_G, POOL_G), F32)],
        args=[dx1, x, y, pre_g, pool_w, pool_scale, post_g], rider=rider)


def _ffn_fwd(layer, x1, pre_g, wgu, wd, post_g, rider=None):
    s_len = x1.shape[0]

    def body(x_ref, pg_ref, wgu_ref, wd_ref, qg_ref, f_ref, x2_ref, g_ref, u_ref):
        x = x_ref[...]
        h = _rms_fwd(x, pg_ref[layer:layer + 1, :]).astype(BF)
        f = jnp.zeros((TM, D), F32)
        for c in range(FF // FF_HALF):
            cols = slice(c * FF_HALF, (c + 1) * FF_HALF)
            g = _dot(h, wgu_ref[0, :, cols])
            u = _dot(h, wgu_ref[1, :, cols])
            g_ref[:, cols] = g.astype(BF)
            u_ref[:, cols] = u.astype(BF)
            act = g * _sigmoid(g) * u
            f = f + _dot(act.astype(BF), wd_ref[cols, :])
        f_ref[...] = f
        x2_ref[...] = x + _rms_fwd(f, qg_ref[layer:layer + 1, :])

    return _call(body, name=f"ffn_fwd{layer}", grid=(s_len // TM,),
                 in_specs=[_row_spec(TM), VSPEC, VSPEC, VSPEC, VSPEC],
                 out_specs=[_row_spec(TM), _row_spec(TM), _row_spec(TM, FF), _row_spec(TM, FF)],
                 out_shape=[_sds((s_len, D)), _sds((s_len, D)), _sds((s_len, FF), BF), _sds((s_len, FF), BF)],
                 args=[x1, pre_g, wgu, wd, post_g], rider=rider)


GU_PIECE = 128
DN_PIECE = 64
DN_SLOT = FF // N_CHIPS
HALF_D = D // 2
CHUNK_STRIDE = 6
CHUNK_START = (1, 7, 4, 10)


def _ffn_bwd(layer, dx2, x1, f, g_pre, u_pre, pre_g, wgu, wd, post_g, kc, rider=None):
    s_len = x1.shape[0]
    tm = TM_FFN_BWD
    n = s_len // tm
    nc = FF // FF_CHUNK
    n_gu, n_dn = FF_CHUNK // GU_PIECE, FF_CHUNK // DN_PIECE
    n_pieces = 2 * n_gu + n_dn
    n_blk = FF_HALF // GU_PIECE

    def edge_rows(c, i, kc_ref):
        return (jnp.where((c == 0) | (c == nc - 1), i, n - 1), 0)

    def chunk_at(c, kc_ref):
        k = kc_ref[0]
        start = jnp.where(k == 0, CHUNK_START[0], jnp.where(k == 1, CHUNK_START[1],
                                                            jnp.where(k == 2, CHUNK_START[2], CHUNK_START[3])))
        return ((c + start) * CHUNK_STRIDE) % nc

    def exchange(kc_ref, c, accg, accu, accd, own_gu_ref, land_gu_ref, own_dn_ref, land_dn_ref,
                 pl_gu, pl_dn, sib_gu, sib_dn, mine_gu, mine_dn, sum_gu, sum_dn,
                 psend, precv, ssend, lsem, rrecv):
        x, y, core = lax.axis_index("x"), lax.axis_index("y"), lax.axis_index("c")
        lower = core == 0

        def pair_copy(cc, part):
            p = cc % 2
            src, dst = ((sib_gu, pl_gu), (sib_dn, pl_dn))[part]
            return pltpu.make_async_remote_copy(src.at[p], dst.at[cc], psend.at[p, part], precv.at[cc, part],
                                                device_id=(x, y, 1 - core), device_id_type=MESH)

        def scatter(cc, wait):
            p = cc % 2
            hidden = chunk_at(cc, kc_ref) * FF_CHUNK

            assert n_gu == 2
            k0, k1 = hidden // FF_HALF, (hidden + GU_PIECE) // FF_HALF
            blk = (hidden - k0 * FF_HALF) // GU_PIECE
            for gu in range(2):
                @pl.when(k0 == k1)
                def _():
                    piece(p, wait, 2 * gu, sum_gu.at[p, gu], k0 + 2 * gu, 0, (pl.ds(blk, 2),))

                @pl.when(k0 != k1)
                def _():
                    piece(p, wait, 2 * gu, sum_gu.at[p, gu, 0], k0 + 2 * gu, 0, (blk,))
                    piece(p, wait, 2 * gu + 1, sum_gu.at[p, gu, 1], k1 + 2 * gu, 0, (0,))

            kd = hidden // DN_SLOT
            off = pl.multiple_of(hidden - kd * DN_SLOT, DN_PIECE)
            m = jnp.minimum((DN_SLOT - off) // DN_PIECE, n_dn)
            for mm in range(1, n_dn + 1):
                @pl.when(m == mm)
                def _():
                    rows = mm * DN_PIECE
                    piece(p, wait, 2 * n_gu, sum_dn.at[p, pl.ds(0, rows), :], kd, 1, (pl.ds(off, rows), slice(None)))
                    if mm < n_dn:
                        piece(p, wait, 2 * n_gu + 1, sum_dn.at[p, pl.ds(rows, FF_CHUNK - rows), :], kd + 1, 1,
                              (pl.ds(0, FF_CHUNK - rows), slice(None)))

        def piece(p, wait, pi, src, k, t, where):
            own_ref, land_ref = ((own_gu_ref, land_gu_ref), (own_dn_ref, land_dn_ref))[t]
            kx, ky = k // 2, k % 2
            fx, fy = (kx != x).astype(jnp.int32), (ky != y).astype(jnp.int32)
            local = (fx + fy) == 0
            j = jnp.maximum(fx + 2 * fy - 1, 0)

            @pl.when(local)
            def _():
                cp = pltpu.make_async_copy(src, own_ref.at[where], lsem.at[p, pi])
                if wait:
                    cp.wait()
                else:
                    cp.start()

            @pl.when(jnp.logical_not(local))
            def _():
                cp = pltpu.make_async_remote_copy(src, land_ref.at[(j,) + where], ssend.at[p, pi],
                                                  rrecv.at[t, j], device_id=(kx, ky, core), device_id_type=MESH)
                if wait:
                    cp.wait_send()
                else:
                    cp.start()

        def add_and_scatter(cc):
            p = cc % 2
            pair_copy(cc, 0).wait_recv()
            pair_copy(cc, 1).wait_recv()
            s_gu = (mine_gu[...] + pl_gu[cc].astype(F32)).astype(BF)
            for hc in range(n_gu):
                sum_gu[p, :, hc] = s_gu[:, :, hc * GU_PIECE:(hc + 1) * GU_PIECE]
            sum_dn[p] = (mine_dn[...] + pl_dn[cc].astype(F32)).astype(BF)
            scatter(cc, wait=False)

        @pl.when(c >= 1)
        def _():
            @pl.when(c >= 3)
            def _():
                scatter(c - 3, wait=True)
            add_and_scatter(c - 1)

        @pl.when(c >= 2)
        def _():
            pair_copy(c - 2, 0).wait_send()
            pair_copy(c - 2, 1).wait_send()

        p = c % 2
        my_rows = pl.ds(pl.multiple_of(core * HALF_D, HALF_D), HALF_D)
        sib_rows = pl.ds(pl.multiple_of((1 - core) * HALF_D, HALF_D), HALF_D)
        d_v = accd[...]
        sib_gu[p, 0] = accg[sib_rows, :].astype(BF)
        sib_gu[p, 1] = accu[sib_rows, :].astype(BF)
        sib_dn[p] = jnp.where(lower, d_v[:, HALF_D:], d_v[:, :HALF_D]).astype(BF)
        mine_gu[0] = accg[my_rows, :]
        mine_gu[1] = accu[my_rows, :]
        mine_dn[...] = jnp.where(lower, d_v[:, :HALF_D], d_v[:, HALF_D:])
        pair_copy(c, 0).start()
        pair_copy(c, 1).start()

        @pl.when(c == nc - 1)
        def _():
            scatter(nc - 3, wait=True)
            add_and_scatter(nc - 1)
            for cc in (nc - 2, nc - 1):
                pair_copy(cc, 0).wait_send()
                pair_copy(cc, 1).wait_send()
                scatter(cc, wait=True)
            for t, land_ref in enumerate((land_gu_ref, land_dn_ref)):
                for j in range(N_CHIPS - 1):
                    pltpu.make_async_remote_copy(land_ref.at[j], land_ref.at[j], ssend.at[0, 0], rrecv.at[t, j],
                                                 device_id=(x, y, core), device_id_type=MESH).wait_recv()

    def body(kc_ref, dx_ref, x_ref, f_ref, gp_ref, up_ref, pg_ref, wgu_ref, wd_ref, qg_ref,
             dx1_ref, dpg_ref, dqg_ref, own_gu_ref, land_gu_ref, own_dn_ref, land_dn_ref,
             h_s, df_s, dh_s, accg, accu, accd, *comm):
        c = pl.program_id(0)
        i = pl.program_id(1)
        rows = pl.ds(pl.multiple_of(i * tm, tm), tm)
        pg = pg_ref[layer:layer + 1, :]

        @pl.when((c == 0) & (i == 0))
        def _():
            dpg_ref[...] = jnp.zeros_like(dpg_ref)
            dqg_ref[...] = jnp.zeros_like(dqg_ref)

        @pl.when(c == 0)
        def _():
            h_s[rows, :] = _rms_fwd(x_ref[...], pg).astype(BF)
            df, prod = _rms_bwd(f_ref[...], qg_ref[layer:layer + 1, :], dx_ref[...])
            df_s[rows, :] = df.astype(BF)
            dqg_ref[...] += _rowsum(prod)

        @pl.when(i == 0)
        def _():
            accg[...] = jnp.zeros_like(accg)
            accu[...] = jnp.zeros_like(accu)
            accd[...] = jnp.zeros_like(accd)

        h = h_s[rows, :]
        df = df_s[rows, :]
        wg = wgu_ref[0]
        wu = wgu_ref[1]
        g = gp_ref[...].astype(F32)
        u = up_ref[...].astype(F32)
        sg = _sigmoid(g)
        a = g * sg
        dact = _dot_nt(df, wd_ref[...])
        accd[...] += _dot_tn((a * u).astype(BF), df)
        du = (dact * a).astype(BF)
        dg = (dact * u * (sg * (1.0 + g * (1.0 - sg)))).astype(BF)
        accg[...] += _dot_tn(h, dg)
        accu[...] += _dot_tn(h, du)
        dh = _dot_nt(dg, wg) + _dot_nt(du, wu)

        @pl.when(c == 0)
        def _():
            dh_s[rows, :] = dh

        @pl.when((c > 0) & (c < nc - 1))
        def _():
            dh_s[rows, :] += dh

        @pl.when(c == nc - 1)
        def _():
            dxp, prod = _rms_bwd(x_ref[...], pg, dh_s[rows, :] + dh)
            dpg_ref[...] += _rowsum(prod)
            dx1_ref[...] = dx_ref[...] + dxp

        @pl.when(i == n - 1)
        def _():
            exchange(kc_ref, c, accg, accu, accd, own_gu_ref, land_gu_ref, own_dn_ref, land_dn_ref, *comm)

    dma = pltpu.SemaphoreType.DMA
    return _call(
        body, name=f"ffn_bwd{layer}", grid=(nc, n),
        in_specs=[pl.BlockSpec((tm, D), edge_rows), pl.BlockSpec((tm, D), edge_rows),
                  pl.BlockSpec((tm, D), lambda c, i, kc_ref: (jnp.where(c == 0, i, n - 1), 0),
                               pipeline_mode=pl.Buffered(1)),
                  pl.BlockSpec((tm, FF_CHUNK), lambda c, i, kc_ref: (i, chunk_at(c, kc_ref))),
                  pl.BlockSpec((tm, FF_CHUNK), lambda c, i, kc_ref: (i, chunk_at(c, kc_ref))),
                  VSPEC,
                  pl.BlockSpec((2, D, FF_CHUNK), lambda c, i, kc_ref: (0, 0, chunk_at(c, kc_ref))),
                  pl.BlockSpec((FF_CHUNK, D), lambda c, i, kc_ref: (chunk_at(c, kc_ref), 0)),
                  VSPEC],
        out_specs=[pl.BlockSpec((tm, D), lambda c, i, kc_ref: (jnp.where(c == nc - 1, i, 0), 0)),
                   _const_spec((1, D)), _const_spec((1, D)), ANYSPEC, ANYSPEC, ANYSPEC, ANYSPEC],
        out_shape=[_sds((s_len, D)), _sds((1, D)), _sds((1, D)),
                   _sds((n_blk, HALF_D, GU_PIECE), BF), _sds((N_CHIPS - 1, n_blk, HALF_D, GU_PIECE), BF),
                   _sds((DN_SLOT, HALF_D), BF), _sds((N_CHIPS - 1, DN_SLOT, HALF_D), BF)],
        scratch_shapes=[pltpu.VMEM((s_len, D), BF), pltpu.VMEM((s_len, D), BF), pltpu.VMEM((s_len, D), F32),
                        pltpu.VMEM((D, FF_CHUNK), F32), pltpu.VMEM((D, FF_CHUNK), F32),
                        pltpu.VMEM((FF_CHUNK, D), F32),
                        pltpu.VMEM((nc, 2, HALF_D, FF_CHUNK), BF), pltpu.VMEM((nc, FF_CHUNK, HALF_D), BF),
                        pltpu.VMEM((2, 2, HALF_D, FF_CHUNK), BF), pltpu.VMEM((2, FF_CHUNK, HALF_D), BF),
                        pltpu.VMEM((2, HALF_D, FF_CHUNK), F32), pltpu.VMEM((FF_CHUNK, HALF_D), F32),
                        pltpu.VMEM((2, 2, n_gu, HALF_D, GU_PIECE), BF), pltpu.VMEM((2, FF_CHUNK, HALF_D), BF),
                        dma((2, 2)), dma((nc, 2)), dma((2, n_pieces)), dma((2, n_pieces)), dma((2, N_CHIPS - 1))],
        args=[dx2, x1, f, g_pre, u_pre, pre_g, wgu, wd, post_g], rider=rider, prefetch=kc)


def _ple_fwd(layer, x2, p, ple_g, w_gate, w_proj, post_g, target=None, qkv=None, rider=None):
    s_len = x2.shape[0]
    final = target is not None
    assert not (final and qkv)

    def body(*refs):
        if final:
            x_ref, p_ref, g_ref, wg_ref, wp_ref, qg_ref, t_ref, z_ref, pe_ref, dx_ref, lv_ref = refs
        elif qkv:
            (x_ref, p_ref, g_ref, wg_ref, wp_ref, qg_ref, ng_ref, kg_ref, wq_ref, wkv_ref,
             z_ref, pe_ref, x3_ref, q_ref, kv_ref) = refs
        else:
            x_ref, p_ref, g_ref, wg_ref, wp_ref, qg_ref, z_ref, pe_ref, x3_ref = refs
        x = x_ref[...]
        r = _rms_fwd(x, g_ref[layer:layer + 1, :]).astype(BF)
        z = _dot(r, wg_ref[...])
        pe = _dot(p_ref[...].astype(BF), wp_ref[...])
        z_ref[...] = z
        pe_ref[...] = pe
        x3 = x + _rms_fwd(pe * _sigmoid(z), qg_ref[layer:layer + 1, :])
        if final:
            @pl.when(pl.program_id(0) == 0)
            def _():
                lv_ref[...] = jnp.zeros_like(lv_ref)
            err = x3 - t_ref[...]
            dx_ref[...] = err * (1.0 / D)
            lv_ref[...] += _rowsum(err * err)
        else:
            x3_ref[...] = x3
        if qkv:
            q_ref[...] = _dot(_rms_fwd(x3, ng_ref[layer + 1:layer + 2, :]).astype(BF), wq_ref[...]).astype(BF)
            kv_ref[...] = _dot(_rms_fwd(x3, kg_ref[...]).astype(BF), wkv_ref[...]).astype(BF)

    p_spec = pl.BlockSpec((None, TM, PLE), lambda i: (layer, i, 0))
    in_specs = [_row_spec(TM), p_spec, VSPEC, VSPEC, VSPEC, VSPEC]
    args = [x2, p, ple_g, w_gate, w_proj, post_g]
    out_specs = [_row_spec(TM), _row_spec(TM), _row_spec(TM)]
    out_shape = [_sds((s_len, D))] * 3
    if qkv:
        in_specs += [VSPEC] * 4
        args += list(qkv)
        out_specs += [_row_spec(TM), _row_spec(TM, 2 * KVD)]
        out_shape += [_sds((s_len, D), BF), _sds((s_len, 2 * KVD), BF)]
    if final:
        in_specs.append(_row_spec(TM))
        args.append(target)
        out_specs.append(_const_spec((1, D)))
        out_shape.append(_sds((1, D)))
    return _call(body, name=f"ple_fwd{layer}", grid=(s_len // TM,), in_specs=in_specs, out_specs=out_specs,
                 out_shape=out_shape, args=args, rider=rider)


def _ple_bwd(layer, dx3, x2, z, pe, p, ple_g, w_gate, post_g, rider=None):
    s_len = x2.shape[0]
    n = s_len // TM

    def body(dx_ref, x_ref, z_ref, pe_ref, p_ref, g_ref, wg_ref, qg_ref,
             dx2_ref, dwg_ref, dwp_ref, dg_ref, dqg_ref, gacc, pacc):
        i = pl.program_id(0)

        @pl.when(i == 0)
        def _():
            gacc[...] = jnp.zeros_like(gacc)
            pacc[...] = jnp.zeros_like(pacc)
            dg_ref[...] = jnp.zeros_like(dg_ref)
            dqg_ref[...] = jnp.zeros_like(dqg_ref)

        dx = dx_ref[...]
        x = x_ref[...]
        pe_v = pe_ref[...]
        gate = _sigmoid(z_ref[...])
        de, prod = _rms_bwd(pe_v * gate, qg_ref[layer:layer + 1, :], dx)
        dqg_ref[...] += _rowsum(prod)
        dpe = (de * gate).astype(BF)
        dz = (de * pe_v * gate * (1.0 - gate)).astype(BF)
        pacc[...] += _dot_tn(p_ref[...].astype(BF), dpe)
        g = g_ref[layer:layer + 1, :]
        r = _rms_fwd(x, g).astype(BF)
        gacc[...] += _dot_tn(r, dz)
        dr = _dot_nt(dz, wg_ref[...])
        dxp, prod2 = _rms_bwd(x, g, dr)
        dg_ref[...] += _rowsum(prod2)
        dx2_ref[...] = dx + dxp

        @pl.when(i == n - 1)
        def _():
            dwg_ref[...] = gacc[...].astype(BF)
            dwp_ref[...] = pacc[...].astype(BF)

    p_spec = pl.BlockSpec((None, TM, PLE), lambda i: (layer, i, 0))
    return _call(
        body, name=f"ple_bwd{layer}", grid=(n,),
        in_specs=[_row_spec(TM), _row_spec(TM), _row_spec(TM), _row_spec(TM), p_spec, VSPEC, VSPEC, VSPEC],
        out_specs=[_row_spec(TM), _const_spec((D, D)), _const_spec((PLE, D)), _const_spec((1, D)), _const_spec((1, D))],
        out_shape=[_sds((s_len, D)), _sds((D, D), BF), _sds((PLE, D), BF), _sds((1, D)), _sds((1, D))],
        scratch_shapes=[pltpu.VMEM((D, D), F32), pltpu.VMEM((PLE, D), F32)],
        args=[dx3, x2, z, pe, p, ple_g, w_gate, post_g], rider=rider)


def _qkv_bwd(dq, dkv, x3, dx4, q_g, kv_g, w_q, w_kv):
    s_len = x3.shape[0]
    n = s_len // TM

    def body(dq_ref, dkv_ref, x_ref, dx_ref, qg_ref, kg_ref, wq_ref, wkv_ref,
             dx3_ref, dwq_ref, dwkv_ref, dqg_ref, dkg_ref, qacc, kacc):
        i = pl.program_id(0)

        @pl.when(i == 0)
        def _():
            qacc[...] = jnp.zeros_like(qacc)
            kacc[...] = jnp.zeros_like(kacc)
            dqg_ref[...] = jnp.zeros_like(dqg_ref)
            dkg_ref[...] = jnp.zeros_like(dkg_ref)

        x = x_ref[...]
        qg = qg_ref[1:2, :]
        kg = kg_ref[...]
        dq_v = dq_ref[...]
        dkv_v = dkv_ref[...].astype(BF)
        qacc[...] += _dot_tn(_rms_fwd(x, qg).astype(BF), dq_v)
        kacc[...] += _dot_tn(_rms_fwd(x, kg).astype(BF), dkv_v)
        dxq, prod_q = _rms_bwd(x, qg, _dot_nt(dq_v, wq_ref[...]))
        dxk, prod_k = _rms_bwd(x, kg, _dot_nt(dkv_v, wkv_ref[...]))
        dqg_ref[...] += _rowsum(prod_q)
        dkg_ref[...] += _rowsum(prod_k)
        dx3_ref[...] = dx_ref[...] + dxq + dxk

        @pl.when(i == n - 1)
        def _():
            dwq_ref[...] = qacc[...].astype(BF)
            dwkv_ref[...] = kacc[...].astype(BF)

    outs, _ = _call(
        body, name="qkv_bwd", grid=(n,),
        in_specs=[_row_spec(TM), _row_spec(TM, 2 * KVD), _row_spec(TM), _row_spec(TM), VSPEC, VSPEC, VSPEC, VSPEC],
        out_specs=[_row_spec(TM), _const_spec((D, D)), _const_spec((D, 2 * KVD)),
                   _const_spec((1, D)), _const_spec((1, D))],
        out_shape=[_sds((s_len, D)), _sds((D, D), BF), _sds((D, 2 * KVD), BF), _sds((1, D)), _sds((1, D))],
        scratch_shapes=[pltpu.VMEM((D, D), F32), pltpu.VMEM((D, 2 * KVD), F32)],
        args=[dq, dkv, x3, dx4, q_g, kv_g, w_q, w_kv])
    return outs


def _attn_group(i, q, kvw, sink_ref, g):
    rows = GQA * BLK
    heads = [GQA * g + j for j in range(GQA)]
    off = jnp.where(i > 0, BLK, 0)
    row = lax.broadcasted_iota(jnp.int32, (rows, 2 * BLK), 0)
    rel = (row % BLK) - lax.broadcasted_iota(jnp.int32, (rows, 2 * BLK), 1) + off
    valid = (rel >= 0) & (rel < BLK)
    head_of_row = lax.broadcasted_iota(jnp.int32, (rows, 1), 0) // BLK
    slope = jnp.zeros((rows, 1), F32)
    sink = jnp.zeros((rows, 1), F32)
    for j, h in enumerate(heads):
        slope = jnp.where(head_of_row == j, SLOPES[h], slope)
        sink = jnp.where(head_of_row == j, sink_ref[0, h], sink)
    qs = jnp.concatenate([q[:, h * HEAD_DIM:(h + 1) * HEAD_DIM] for h in heads], axis=0)
    k = kvw[:, g * HEAD_DIM:(g + 1) * HEAD_DIM]
    v = kvw[:, KVD + g * HEAD_DIM:KVD + (g + 1) * HEAD_DIM]
    s = _dot_nt(qs, k) * ATT_SCALE - slope * rel.astype(F32)
    s = jnp.where(valid, s, NEG_INF)
    m = jnp.maximum(jnp.max(s, axis=-1, keepdims=True), sink)
    e = jnp.exp(s - m)
    es = jnp.exp(sink - m)
    inv = 1.0 / (jnp.sum(e, axis=-1, keepdims=True) + es)
    return e * inv, es * inv, qs, k, v


def _unstack_heads(stacked):
    return [stacked[j * BLK:(j + 1) * BLK, :] for j in range(GQA)]


def _kv_window(kv_ref, i):
    ks = pl.multiple_of(jnp.maximum(i * BLK - BLK, 0), BLK)
    return ks, kv_ref[pl.ds(ks, 2 * BLK), :]


def _attn_fwd(q, kv, sinks, x3, w_o, post_g, rider=None):
    s_len = q.shape[0]

    def body(q_ref, kv_ref, sk_ref, x_ref, wo_ref, g_ref, a_ref, y_ref, x4_ref):
        i = pl.program_id(0)
        _, kvw = _kv_window(kv_ref, i)
        q = q_ref[...]
        outs = []
        for g in range(N_KV_HEADS):
            p, _, _, _, v = _attn_group(i, q, kvw, sk_ref, g)
            outs += _unstack_heads(_dot(p.astype(BF), v))
        attn = jnp.concatenate(outs, axis=1)
        a_ref[...] = attn
        y = _dot(attn.astype(BF), wo_ref[...])
        y_ref[...] = y
        x4_ref[...] = x_ref[...] + _rms_fwd(y, g_ref[1:2, :])

    return _call(body, name="attn_fwd", grid=(s_len // BLK,),
                 in_specs=[_row_spec(BLK), VSPEC, SSPEC, _row_spec(BLK), VSPEC, VSPEC],
                 out_specs=[_row_spec(BLK)] * 3, out_shape=[_sds((s_len, D))] * 3,
                 args=[q, kv, sinks, x3, w_o, post_g], rider=rider)


ATT_STEP_BLOCKS = 2


def _attn_bwd(dx4, y, attn, q, kv, sinks, w_o, post_g, rider=None):
    s_len = q.shape[0]
    rows = ATT_STEP_BLOCKS * BLK
    n = s_len // rows

    def body(dx_ref, y_ref, a_ref, q_ref, kv_ref, sk_ref, wo_ref, g_ref,
             dq_ref, dkv_ref, dwo_ref, dg_ref, dsk_ref, wacc):
        i = pl.program_id(0)

        @pl.when(i == 0)
        def _():
            dkv_ref[...] = jnp.zeros_like(dkv_ref)
            wacc[...] = jnp.zeros_like(wacc)
            dg_ref[...] = jnp.zeros_like(dg_ref)
            dsk_ref[...] = jnp.zeros_like(dsk_ref)

        dy, prod = _rms_bwd(y_ref[...], g_ref[1:2, :], dx_ref[...])
        dg_ref[...] += _rowsum(prod)
        dyb = dy.astype(BF)
        attn_all = a_ref[...]
        wacc[...] += _dot_tn(attn_all.astype(BF), dyb)
        d_o_all = _dot_nt(dyb, wo_ref[...])
        q_all = q_ref[...]
        lane = lax.broadcasted_iota(jnp.int32, (1, D), 1)
        dsk = jnp.zeros((1, D), F32)
        for sub in range(ATT_STEP_BLOCKS):
            blk = i * ATT_STEP_BLOCKS + sub
            sl = slice(sub * BLK, (sub + 1) * BLK)
            d_o, q = d_o_all[sl, :], q_all[sl, :]
            dod = d_o * attn_all[sl, :]
            ks, kvw = _kv_window(kv_ref, blk)
            dqs, dks, dvs = [], [], []
            for g in range(N_KV_HEADS):
                p, ps, qs, k, v = _attn_group(blk, q, kvw, sk_ref, g)
                cols = [slice((GQA * g + j) * HEAD_DIM, (GQA * g + j + 1) * HEAD_DIM) for j in range(GQA)]
                do_s = jnp.concatenate([d_o[:, c] for c in cols], axis=0).astype(BF)
                dsum = jnp.concatenate([jnp.sum(dod[:, c], axis=-1, keepdims=True) for c in cols], axis=0)
                dp = _dot_nt(do_s, v)
                dsb = (p * (dp - dsum) * ATT_SCALE).astype(BF)
                sink_part = ps * dsum
                for j in range(GQA):
                    dsk = dsk + jnp.where(lane == GQA * g + j, -_rowsum(sink_part[j * BLK:(j + 1) * BLK, :]), 0.0)
                dqs += _unstack_heads(_dot(dsb, k))
                dks.append(_dot_tn(dsb, qs))
                dvs.append(_dot_tn(p.astype(BF), do_s))
            dq_ref[sl, :] = jnp.concatenate(dqs, axis=1).astype(BF)
            dkv_ref[pl.ds(ks, 2 * BLK), :] += jnp.concatenate(dks + dvs, axis=1)
        dsk_ref[...] += dsk

        @pl.when(i == n - 1)
        def _():
            dwo_ref[...] = wacc[...].astype(BF)

    return _call(
        body, name="attn_bwd", grid=(n,),
        in_specs=[_row_spec(rows), _row_spec(rows), _row_spec(rows), _row_spec(rows), VSPEC, SSPEC, VSPEC, VSPEC],
        out_specs=[_row_spec(rows), _const_spec((s_len, 2 * KVD)), _const_spec((D, D)),
                   _const_spec((1, D)), _const_spec((1, D))],
        out_shape=[_sds((s_len, D), BF), _sds((s_len, 2 * KVD)), _sds((D, D), BF), _sds((1, D)), _sds((1, D))],
        scratch_shapes=[pltpu.VMEM((D, D), F32)],
        args=[dx4, y, attn, q, kv, sinks, w_o, post_g], rider=rider)


Big = collections.namedtuple("Big", "name src layer L A R C rb")


def _bigs():
    out = {"pool_w": Big("pool_w", "pool_w", None, 4, 4, POOL_G // N_CHIPS, POOL_G, 32)}
    for l in range(2):
        out[f"w_gu{l}"] = Big(f"w_gu{l}", "w_gu", l, 1, 2, D, FF_HALF, 256)
        out[f"w_down{l}"] = Big(f"w_down{l}", "w_down", l, 1, 4, FF // N_CHIPS, D, 352)
        out[f"w_ple_gate{l}"] = Big(f"w_ple_gate{l}", "w_ple_gate", l, 1, 4, D // N_CHIPS, D, 128)
        out[f"w_ple_proj{l}"] = Big(f"w_ple_proj{l}", "w_ple_proj", l, 1, 1, PLE, D // N_CHIPS, 128)
    out["w_q"] = Big("w_q", "w_q", None, 1, 4, D // N_CHIPS, D, 128)
    out["w_o"] = Big("w_o", "w_o", None, 1, 4, D // N_CHIPS, D, 128)
    out["w_kv"] = Big("w_kv", "w_kv", None, 1, 4, D // N_CHIPS, 2 * KVD, 128)
    return out


BIGS = _bigs()
POOL_SCALE = Big("pool_scale", "pool_scale", None, 1, 1, 1, D // N_CHIPS, 1)
BIG_SOURCES = ("w_gu", "w_down", "w_ple_gate", "w_ple_proj", "w_q", "w_o", "w_kv", "pool_w")


def _ncb(t):
    return N_CHIPS // t.A


def _full_shape(t, rows=None):
    return (t.L, t.A, t.R if rows is None else rows, _ncb(t) * t.C)


def _slot_index(t, k):
    return k // _ncb(t), k % _ncb(t)


def _slot(ref, t, k, row0, rows):
    a, cb = _slot_index(t, k)
    return ref.at[:, a, pl.ds(row0, rows), pl.ds(pl.multiple_of(cb * t.C, 128), t.C)]


def _place_job(t, w, out_dtype=BF):
    nb = next((nb for nb in (8, 4, 2, 1) if t.R % (16 * nb) == 0), 1) if t.L == 1 else 1
    rb = t.R // nb

    def fn(j, kc_ref, ins, outs):
        outs[0][...] = ins[0][...].astype(out_dtype)

    def in_map(j, kc_ref):
        return (j // nb if t.layer is None else t.layer, j % nb, 0)

    def out_map(j, kc_ref):
        a, cb = _slot_index(t, kc_ref[0])
        return (j // nb, a, j % nb, cb)

    return Job(t.L * nb, [(w, (None, rb, t.C), in_map)],
               [(_sds(_full_shape(t), out_dtype), (None, None, rb, t.C), out_map)], fn)


def _mesh_position():
    x, y, c = lax.axis_index("x"), lax.axis_index("y"), lax.axis_index("c")
    chips = [(1 - x, y), (x, 1 - y), (1 - x, 1 - y)]
    return x, y, c, chips


DIRECT_BELOW = 1024


def _gather_rider(parts, fulls):
    nt = len(parts)
    TO_X, TO_Y, FWD_X, FWD_Y, SIB_X, SIB_Y, SIB_D = range(7)

    def rows_of(ti, core):
        t, r0, r1 = parts[ti]
        h = (r1 - r0) // 2
        return r0 + core * h, h

    def copy(outs, sems, kind, ti, k_src, row0, rows, dev):
        region = _slot(outs[ti], parts[ti][0], k_src, row0, rows)
        return pltpu.make_async_remote_copy(region, region, sems[0].at[ti, kind], sems[1].at[ti, kind],
                                            device_id=dev, device_id_type=MESH)

    def plan(outs, sems):
        x, y, c, _ = _mesh_position()
        me, kx, ky, kd = 2 * x + y, 2 * (1 - x) + y, 2 * x + (1 - y), 2 * (1 - x) + (1 - y)
        dev_x, dev_y, dev_d, sib = (1 - x, y, c), (x, 1 - y, c), (1 - x, 1 - y, c), (x, y, 1 - c)

        def whole(ti):
            return 0, parts[ti][0].R

        def mk(kind, k_send, k_recv, dev, send_rows, recv_rows):
            def build(ti, side):
                k_src = k_send if side == "s" else k_recv
                row0, rows = (send_rows if side == "s" else recv_rows)(ti)
                return copy(outs, sems, kind, ti, k_src, row0, rows, dev)
            return build

        def first_half(core):
            return lambda ti: (rows_of(ti, core)[0], rows_of(ti, core)[1] // 2)

        def second_half(core):
            return lambda ti: (rows_of(ti, core)[0] + rows_of(ti, core)[1] // 2, rows_of(ti, core)[1] // 2)

        mine = lambda ti: rows_of(ti, c)
        theirs = lambda ti: rows_of(ti, 1 - c)
        split = {
            TO_X: mk(TO_X, me, kx, dev_x, mine, mine),
            TO_Y: mk(TO_Y, me, ky, dev_y, mine, mine),
            FWD_X: mk(FWD_X, ky, kd, dev_x, first_half(c), first_half(c)),
            FWD_Y: mk(FWD_Y, kx, kd, dev_y, second_half(c), second_half(c)),
            SIB_X: mk(SIB_X, kx, kx, sib, mine, theirs),
            SIB_Y: mk(SIB_Y, ky, ky, sib, mine, theirs),
            SIB_D: mk(SIB_D, kd, kd, sib, mine, theirs),
        }
        direct = {
            TO_X: mk(TO_X, me, kx, dev_x, whole, whole),
            TO_Y: mk(TO_Y, me, ky, dev_y, whole, whole),
            FWD_X: mk(FWD_X, me, kd, dev_d, whole, whole),
        }
        return split, direct

    is_split = [t.L * t.R * t.C >= DIRECT_BELOW for t, _, _ in parts]
    assert all(s or (r0, r1) == (0, t.R) for s, (t, r0, r1) in zip(is_split, parts))

    def start(ins, outs, sems):
        split, direct = plan(outs, sems)
        for ti in range(nt):
            kinds = split if is_split[ti] else direct
            kinds[TO_X](ti, "s").start()
            kinds[TO_Y](ti, "s").start()
            if not is_split[ti]:
                kinds[FWD_X](ti, "s").start()

    def mid(ins, outs, sems):
        split, _ = plan(outs, sems)
        for ti in range(nt):
            if is_split[ti]:
                split[TO_Y](ti, "r").wait_recv()
                split[FWD_X](ti, "s").start()
                split[SIB_Y](ti, "s").start()
        for ti in range(nt):
            if is_split[ti]:
                split[TO_X](ti, "r").wait_recv()
                split[FWD_Y](ti, "s").start()
                split[SIB_X](ti, "s").start()

    def finish(ins, outs, sems):
        split, direct = plan(outs, sems)
        for ti in range(nt):
            if is_split[ti]:
                split[FWD_X](ti, "r").wait_recv()
                split[FWD_Y](ti, "r").wait_recv()
                split[SIB_D](ti, "s").start()
            else:
                for kind in (TO_X, TO_Y, FWD_X):
                    direct[kind](ti, "r").wait_recv()
        for ti in range(nt):
            if is_split[ti]:
                for kind in (SIB_X, SIB_Y, SIB_D):
                    split[kind](ti, "r").wait_recv()
        for ti in range(nt):
            kinds = split if is_split[ti] else direct
            for kind in kinds:
                kinds[kind](ti, "s").wait_send()

    sems = pltpu.SemaphoreType.DMA((nt, 7))
    return Rider(list(fulls), [_sds(a.shape, a.dtype) for a in fulls], {i: i for i in range(nt)},
                 [sems, sems], start, mid, finish)


def _pair_exchange_rider(specs, grads):
    nt = len(specs)

    def copy(ins, outs, sems, ti, c, sibling):
        half = specs[ti].R // 2
        return pltpu.make_async_remote_copy(ins[ti].at[:, :, pl.ds((1 - c) * half, half), :], outs[ti],
                                            sems[0].at[ti], sems[1].at[ti], device_id=sibling, device_id_type=MESH)

    def start(ins, outs, sems):
        x, y, c, _ = _mesh_position()
        for ti in range(nt):
            copy(ins, outs, sems, ti, c, (x, y, 1 - c)).start()

    def finish(ins, outs, sems):
        x, y, c, _ = _mesh_position()
        for ti in range(nt):
            copy(ins, outs, sems, ti, c, (x, y, 1 - c)).wait()

    sems = pltpu.SemaphoreType.DMA((nt,))
    return Rider(list(grads), [_sds(_full_shape(t, t.R // 2), BF) for t in specs], {}, [sems, sems], start, None, finish)


def _pair_sum_job(t, g, land):
    assert t.L == 1
    half = t.R // 2
    nj = half // t.rb
    block = (None, t.A, t.rb, _ncb(t) * t.C)

    def fn(j, kc_ref, ins, outs):
        outs[0][...] = (ins[0][...].astype(F32) + ins[1][...].astype(F32)).astype(BF)

    return Job(nj,
               [(g, block, lambda j, kc_ref: (0, 0, kc_ref[1] * nj + j, 0)),
                (land, block, lambda j, kc_ref: (0, 0, j, 0))],
               [(_sds(_full_shape(t, half), BF), block, lambda j, kc_ref: (0, 0, j, 0))], fn)


def _scatter_rider(specs, sums):
    nt = len(specs)

    def copy(ins, outs, sems, ti, j, chip, c):
        t = specs[ti]
        cx, cy = chip
        return pltpu.make_async_remote_copy(_slot(ins[ti], t, 2 * cx + cy, 0, t.R // 2), outs[ti].at[j],
                                            sems[0].at[ti, j], sems[1].at[ti, j],
                                            device_id=(cx, cy, c), device_id_type=MESH)

    def start(ins, outs, sems):
        _, _, c, chips = _mesh_position()
        for j, chip in enumerate(chips):
            for ti in range(nt):
                copy(ins, outs, sems, ti, j, chip, c).start()

    def finish(ins, outs, sems):
        _, _, c, chips = _mesh_position()
        for j, chip in enumerate(chips):
            for ti in range(nt):
                copy(ins, outs, sems, ti, j, chip, c).wait()

    sems = pltpu.SemaphoreType.DMA((nt, N_CHIPS - 1))
    return Rider(list(sums), [_sds((N_CHIPS - 1, t.L, t.R // 2, t.C), BF) for t in specs], {}, [sems, sems],
                 start, None, finish)


def _chip_sum_job(ts, landed):
    t0 = ts[0]
    assert t0.L == 1
    half = t0.R // 2
    nj = half // t0.rb

    def local(j, li):
        return jnp.clip(j - li * nj, 0, nj - 1)

    ins = []
    for li, t in enumerate(ts):
        s, land = landed[t.name]

        def own_map(j, kc_ref, li=li, t=t):
            a, cb = _slot_index(t, kc_ref[0])
            return (0, a, local(j, li), cb)

        ins.append((s, (None, None, t.rb, t.C), own_map))
        ins.append((land, (N_CHIPS - 1, None, t.rb, t.C), lambda j, kc_ref, li=li: (0, 0, local(j, li), 0)))

    def fn(j, kc_ref, in_refs, outs):
        for li in range(len(ts)):
            @pl.when(j // nj == li)
            def _():
                acc = in_refs[2 * li][...].astype(F32)
                for k in range(N_CHIPS - 1):
                    acc = acc + in_refs[2 * li + 1][k].astype(F32)
                outs[0][...] = acc

    return Job(len(ts) * nj, ins,
               [(_sds((len(ts), t0.R, t0.C)), (None, t0.rb, t0.C),
                 lambda j, kc_ref: (j // nj, kc_ref[1] * nj + j % nj, 0))], fn)


def _adamw_job(rb, w, g, m, v):
    n_layers, r, c = w.shape
    nb = r // rb
    block = (None, rb, c)
    index = lambda j, kc_ref: (j // nb, j % nb, 0)

    def fn(j, kc_ref, ins, outs):
        g_v = ins[1][...]
        outs[0][...] = g_v
        outs[1][...], outs[2][...], outs[3][...] = _adamw_math(ins[0][...], g_v, ins[2][...], ins[3][...])

    return Job(n_layers * nb, [(a, block, index) for a in (w, g, m, v)],
               [(_sds(w.shape), block, index)] * 4, fn)


def _chip_sum_fused_job(ts, fused, by_cols):
    t0 = ts[0]
    own0 = fused[t0.name][0]
    if by_cols:
        rows, cols = own0.shape
    else:
        nb, rows, bw = own0.shape
        cols = nb * bw
    nj = rows // t0.rb

    def local(j, li):
        return jnp.clip(j - li * nj, 0, nj - 1)

    ins = []
    for li, t in enumerate(ts):
        own, land = fused[t.name]
        if by_cols:
            ins.append((own, (t.rb, cols), lambda j, kc_ref, li=li: (local(j, li), 0)))
            ins.append((land, (N_CHIPS - 1, t.rb, cols), lambda j, kc_ref, li=li: (0, local(j, li), 0)))
        else:
            ins.append((own, (nb, t.rb, bw), lambda j, kc_ref, li=li: (0, local(j, li), 0)))
            ins.append((land, (N_CHIPS - 1, nb, t.rb, bw), lambda j, kc_ref, li=li: (0, 0, local(j, li), 0)))

    def fn(j, kc_ref, in_refs, outs):
        for li in range(len(ts)):
            @pl.when(j // nj == li)
            def _():
                acc = in_refs[2 * li][...].astype(F32)
                for k in range(N_CHIPS - 1):
                    acc = acc + in_refs[2 * li + 1][k].astype(F32)
                outs[0][...] = acc if by_cols else jnp.concatenate([acc[b] for b in range(nb)], axis=1)

    def out_map(j, kc_ref):
        return (j // nj, j % nj, kc_ref[1]) if by_cols else (j // nj, kc_ref[1] * nj + j % nj, 0)

    return Job(len(ts) * nj, ins, [(_sds((len(ts), t0.R, t0.C)), (None, t0.rb, cols), out_map)], fn)


def _share_rider(halves, by_cols):
    nt = len(halves)

    def copy(outs, sems, ti, core, sibling):
        axis = 2 if by_cols[ti] else 1
        half = halves[ti].shape[axis] // 2
        piece = pl.ds(pl.multiple_of(core * half, 128 if by_cols[ti] else 8), half)
        part = outs[ti].at[:, :, piece] if by_cols[ti] else outs[ti].at[:, piece, :]
        return pltpu.make_async_remote_copy(part, part, sems[0].at[ti], sems[1].at[ti],
                                            device_id=sibling, device_id_type=MESH)

    def start(ins, outs, sems):
        x, y, c, _ = _mesh_position()
        for ti in range(nt):
            copy(outs, sems, ti, c, (x, y, 1 - c)).start()

    def finish(ins, outs, sems):
        x, y, c, _ = _mesh_position()
        for ti in range(nt):
            copy(outs, sems, ti, 1 - c, (x, y, 1 - c)).wait_recv()
        for ti in range(nt):
            copy(outs, sems, ti, c, (x, y, 1 - c)).wait_send()

    sems = pltpu.SemaphoreType.DMA((nt,))
    return Rider(list(halves), [_sds(a.shape, a.dtype) for a in halves], {i: i for i in range(nt)}, [sems, sems],
                 start, None, finish)


def _both(r1, r2):
    assert r1.mid is None and r2.mid is None
    ni, no, ns = len(r1.arrays), len(r1.out_shapes), len(r1.scratch)

    def split(fn1, fn2):
        def run(ins, outs, scr):
            fn1(ins[:ni], outs[:no], scr[:ns])
            fn2(ins[ni:], outs[no:], scr[ns:])
        return run

    aliases = dict(r1.aliases)
    aliases.update({ni + a: no + b for a, b in r2.aliases.items()})
    return Rider(r1.arrays + r2.arrays, r1.out_shapes + r2.out_shapes, aliases, r1.scratch + r2.scratch,
                 split(r1.start, r2.start), None, split(r1.finish, r2.finish))


def _adamw_math(w, g, m, v):
    m = B1 * m + (1.0 - B1) * g
    v = B2 * v + (1.0 - B2) * (g * g)
    delta = -LR * ((m / BC1) / (jnp.sqrt(v / BC2) + AEPS) + WD * w)
    return delta, m, v


def _adamw(name, rb, w, g, m, v):
    n_layers, r, c = w.shape

    def body(w_ref, g_ref, m_ref, v_ref, go_ref, d_ref, nm_ref, nv_ref):
        g_v = g_ref[...]
        go_ref[...] = g_v
        d_ref[...], nm_ref[...], nv_ref[...] = _adamw_math(w_ref[...], g_v, m_ref[...], v_ref[...])

    spec = pl.BlockSpec((None, rb, c), lambda l, j: (l, j, 0))
    return pl.pallas_call(
        body, name=f"adamw_{name}", grid=(n_layers, r // rb),
        in_specs=[spec] * 4, out_specs=[spec] * 4, out_shape=[_sds(w.shape)] * 4,
        compiler_params=_params(2),
    )(w, g, m, v)


GAIN_ROWS = {"pre_mix_g": 0, "post_mix_g": 2, "pre_ffn_g": 4, "post_ffn_g": 6, "ple_g": 8, "ple_post_g": 10}
ROW_KV_G, ROW_POOL_SCALE, ROW_SINKS, ROW_LOSS, PACK_ROWS = 12, 13, 14, 15, 16
SMALL_NAMES = tuple(GAIN_ROWS) + ("kv_g", "pool_scale", "sinks")


def _small_all_reduce(rows, dpool, rider=None):
    ng, pr = len(WINDOWS), POOL_G // N_CHIPS

    def body(*refs):
        row_refs = refs[:PACK_ROWS]
        dpool_ref, tot_ref, gpool_ref, pack, land, pland, send, recv, psend, precv = refs[PACK_ROWS:]
        x, y, c, _ = _mesh_position()
        me = 4 * x + 2 * y + c
        for r in range(PACK_ROWS):
            pack[r:r + 1, :] = row_refs[r][...]

        def shard_of(k):
            return dpool_ref.at[:, pl.ds(pl.multiple_of(k * pr, pr), pr), :]

        cps = []
        for j in range(1, N_DEV):
            px, py, pc = x ^ (j >> 2), y ^ ((j >> 1) & 1), c ^ (j & 1)
            cps.append(pltpu.make_async_remote_copy(pack, land.at[me], send.at[j], recv.at[j],
                                                    device_id=(px, py, pc), device_id_type=MESH))
            cps.append(pltpu.make_async_remote_copy(shard_of(2 * px + py), pland.at[me], psend.at[j], precv.at[j],
                                                    device_id=(px, py, pc), device_id_type=MESH))
        for cp in cps:
            cp.start()
        land[me] = pack[...]
        pland[me] = dpool_ref[:, pl.ds(pl.multiple_of((2 * x + y) * pr, pr), pr), :]
        for j in range(1, N_DEV):
            pltpu.make_async_remote_copy(pack, land.at[me ^ j], send.at[j], recv.at[j],
                                         device_id=(x, y, c), device_id_type=MESH).wait_recv()
            pltpu.make_async_remote_copy(shard_of(0), pland.at[me ^ j], psend.at[j], precv.at[j],
                                         device_id=(x, y, c), device_id_type=MESH).wait_recv()
        for cp in cps:
            cp.wait_send()
        tot = land[0]
        gp = pland[0].astype(F32)
        for d in range(1, N_DEV):
            tot = tot + land[d]
            gp = gp + pland[d].astype(F32)
        tot_ref[...] = tot
        gpool_ref[...] = gp

    sems = pltpu.SemaphoreType.DMA((N_DEV,))
    return _call(
        body, name="small_all_reduce", grid=(1,),
        in_specs=[VSPEC] * (PACK_ROWS + 1), out_specs=[VSPEC, VSPEC],
        out_shape=[_sds((PACK_ROWS, D)), _sds((ng, pr, POOL_G))],
        scratch_shapes=[pltpu.VMEM((PACK_ROWS, D), F32), pltpu.VMEM((N_DEV, PACK_ROWS, D), F32),
                        pltpu.VMEM((N_DEV, ng, pr, POOL_G), BF), sems, sems, sems, sems],
        args=[*rows, dpool], rider=rider)


def _small_adamw(tot, kc, small_w, small_m, small_v):
    names = SMALL_NAMES
    n = len(names)

    def body(*refs):
        tot_ref, kc_ref = refs[0], refs[1]
        w_refs = dict(zip(names, refs[2:2 + n]))
        m_refs = dict(zip(names, refs[2 + n:2 + 2 * n]))
        v_refs = dict(zip(names, refs[2 + 2 * n:2 + 3 * n]))
        loss_ref = refs[2 + 3 * n]
        out_refs = {nm: refs[3 + 3 * n + 4 * k: 7 + 3 * n + 4 * k] for k, nm in enumerate(names)}
        tot = tot_ref[...]
        loss_ref[...] = 0.5 * jnp.sum(tot[ROW_LOSS:ROW_LOSS + 1, :], axis=-1, keepdims=True) * (1.0 / D)

        def update(nm, g):
            g_ref, d_ref, nm_ref, nv_ref = out_refs[nm]
            g_ref[...] = g
            d_ref[...], nm_ref[...], nv_ref[...] = _adamw_math(w_refs[nm][...], g, m_refs[nm][...], v_refs[nm][...])

        for nm, r in GAIN_ROWS.items():
            update(nm, tot[r:r + 2, :])
        update("kv_g", tot[ROW_KV_G:ROW_KV_G + 1, :])
        k = kc_ref[0]
        width = D // N_CHIPS
        g_scale = jnp.zeros((1, width), F32)
        for kk in range(N_CHIPS):
            g_scale = g_scale + jnp.where(k == kk, tot[ROW_POOL_SCALE:ROW_POOL_SCALE + 1, kk * width:(kk + 1) * width], 0.0)
        update("pool_scale", g_scale)
        update("sinks", tot[ROW_SINKS:ROW_SINKS + 1, 0:N_HEADS])

    ins = [tot, kc] + [small_w[nm] for nm in names] + [small_m[nm] for nm in names] + [small_v[nm] for nm in names]
    out_shape = [_sds((1, 1))]
    for nm in names:
        out_shape += [_sds(small_w[nm].shape)] * 4
    outs = pl.pallas_call(
        body, name="small_adamw",
        in_specs=[VSPEC, SSPEC] + [VSPEC] * (3 * n), out_specs=[VSPEC] * len(out_shape), out_shape=out_shape,
        compiler_params=_params(),
    )(*ins)
    return outs[0], {nm: outs[1 + 4 * k: 5 + 4 * k] for k, nm in enumerate(names)}


def _compute_layout(t, full):
    if t.src == "w_gu":
        return full.reshape(2, D, FF)
    if t.src == "pool_w":
        return full.reshape(len(WINDOWS), POOL_G, POOL_G)
    if t.src == "pool_scale":
        return full.reshape(1, D)
    return full.reshape(t.A * t.R, _ncb(t) * t.C)


def kernel(x, p, pre_mix_g, post_mix_g, pre_ffn_g, post_ffn_g, pool_w, pool_scale, kv_g, w_kv, w_q, sinks, w_o, w_gu, w_down, ple_g, w_ple_gate, w_ple_proj, ple_post_g, loss_target, m_pre_mix_g, m_post_mix_g, m_pre_ffn_g, m_post_ffn_g, m_pool_w, m_pool_scale, m_kv_g, m_w_kv, m_w_q, m_sinks, m_w_o, m_w_gu, m_w_down, m_ple_g, m_w_ple_gate, m_w_ple_proj, m_ple_post_g, v_pre_mix_g, v_post_mix_g, v_pre_ffn_g, v_post_ffn_g, v_pool_w, v_pool_scale, v_kv_g, v_w_kv, v_w_q, v_sinks, v_w_o, v_w_gu, v_w_down, v_ple_g, v_w_ple_gate, v_w_ple_proj, v_ple_post_g):
    weights = dict(pre_mix_g=pre_mix_g, post_mix_g=post_mix_g, pre_ffn_g=pre_ffn_g, post_ffn_g=post_ffn_g,
                   pool_w=pool_w, pool_scale=pool_scale, kv_g=kv_g, w_kv=w_kv, w_q=w_q, sinks=sinks, w_o=w_o,
                   w_gu=w_gu, w_down=w_down, ple_g=ple_g, w_ple_gate=w_ple_gate, w_ple_proj=w_ple_proj,
                   ple_post_g=ple_post_g)
    m_in = dict(pre_mix_g=m_pre_mix_g, post_mix_g=m_post_mix_g, pre_ffn_g=m_pre_ffn_g, post_ffn_g=m_post_ffn_g,
                pool_w=m_pool_w, pool_scale=m_pool_scale, kv_g=m_kv_g, w_kv=m_w_kv, w_q=m_w_q, sinks=m_sinks,
                w_o=m_w_o, w_gu=m_w_gu, w_down=m_w_down, ple_g=m_ple_g, w_ple_gate=m_w_ple_gate,
                w_ple_proj=m_w_ple_proj, ple_post_g=m_ple_post_g)
    v_in = dict(pre_mix_g=v_pre_mix_g, post_mix_g=v_post_mix_g, pre_ffn_g=v_pre_ffn_g, post_ffn_g=v_post_ffn_g,
                pool_w=v_pool_w, pool_scale=v_pool_scale, kv_g=v_kv_g, w_kv=v_w_kv, w_q=v_w_q, sinks=v_sinks,
                w_o=v_w_o, w_gu=v_w_gu, w_down=v_w_down, ple_g=v_ple_g, w_ple_gate=v_w_ple_gate,
                w_ple_proj=v_w_ple_proj, ple_post_g=v_ple_post_g)
    order = ["pre_mix_g", "post_mix_g", "pre_ffn_g", "post_ffn_g", "pool_w", "pool_scale", "kv_g", "w_kv", "w_q",
             "sinks", "w_o", "w_gu", "w_down", "ple_g", "w_ple_gate", "w_ple_proj", "ple_post_g"]

    kc = jnp.stack([2 * lax.axis_index("x") + lax.axis_index("y"), lax.axis_index("c")]).astype(jnp.int32)
    s_len = x.shape[1]
    x2d = x.reshape(s_len, D)
    p3d = p.reshape(2, s_len, PLE)
    target = loss_target.reshape(s_len, D)
    kv_g2d = kv_g.reshape(1, D)
    gains = {nm: weights[nm] for nm in GAIN_ROWS}

    def shard_view(src, a):
        t = next(t for t in BIGS.values() if t.src == src)
        return a.reshape(-1, t.R, t.C)

    first, second = ["pool_w", "pool_scale"], ["w_gu0", "w_down0"]
    rest = [nm for nm in BIGS if nm not in first + second]
    specs = dict(BIGS, pool_scale=POOL_SCALE)
    placed = {}

    def place_job(nm):
        if nm == "pool_scale":
            return _place_job(POOL_SCALE, pool_scale.reshape(1, 1, D // N_CHIPS), F32)
        return _place_job(BIGS[nm], shard_view(BIGS[nm].src, weights[BIGS[nm].src]))

    def gather(names, rows=None):
        rows = rows or {}
        parts = [(specs[nm],) + tuple(rows.get(nm, (0, specs[nm].R))) for nm in names]
        return _gather_rider(parts, [placed[nm] for nm in names])

    def take(names, results):
        for nm, a in zip(names, results):
            placed[nm] = a

    def weight(nm):
        return _compute_layout(specs[nm], placed[nm])

    take(first, [r[0] for r in _multi_call("place_pool", [place_job(nm) for nm in first], kc)])
    cast, got = _multi_call("place_ffn0", [place_job(nm) for nm in second], kc, rider=gather(first))
    take(second, [r[0] for r in cast])
    take(first, got)
    jobs = [place_job(nm) for nm in rest]
    jobs.append(_mixa_fwd_job(x2d, gains["pre_mix_g"], weight("pool_w"), weight("pool_scale"), gains["post_mix_g"]))
    results, got = _multi_call("cast_and_mixa_fwd", jobs, kc, rider=gather(second))
    take(rest, [r[0] for r in results[:-1]])
    take(second, got)
    y0, x1 = results[-1]

    ride = ["w_ple_gate0", "w_ple_proj0", "w_q", "w_kv", "w_o", "w_gu1"]
    (f0, x2, g0, u0), got = _ffn_fwd(0, x1, gains["pre_ffn_g"], weight("w_gu0"), weight("w_down0"), gains["post_ffn_g"],
                             rider=gather(ride, {"w_gu1": (0, 320)}))
    take(ride, got)

    ride = ["w_ple_gate1", "w_ple_proj1", "w_gu1"]
    (z0, pe0, x3, q, kv), got = _ple_fwd(
        0, x2, p3d, gains["ple_g"], weight("w_ple_gate0"), weight("w_ple_proj0"), gains["ple_post_g"],
        qkv=(gains["pre_mix_g"], kv_g2d, weight("w_q"), weight("w_kv")),
        rider=gather(ride, {"w_gu1": (320, 704)}))
    take(ride, got)

    ride = ["w_down1", "w_gu1"]
    (attn, y1, x4), got = _attn_fwd(q, kv, sinks, x3, weight("w_o"), gains["post_mix_g"],
                                    rider=gather(ride, {"w_gu1": (704, D)}))
    take(ride, got)

    (f1, x5, g1, u1), _ = _ffn_fwd(1, x4, gains["pre_ffn_g"], weight("w_gu1"), weight("w_down1"), gains["post_ffn_g"])
    (z1, pe1, dx6, loss_row), _ = _ple_fwd(1, x5, p3d, gains["ple_g"], weight("w_ple_gate1"), weight("w_ple_proj1"),
                                           gains["ple_post_g"], target=target)

    local = {}
    landed = {}
    fused = {}

    def local_grads(names):
        return [local[nm].reshape(_full_shape(BIGS[nm])) for nm in names]

    def pair_exchange(names):
        return _pair_exchange_rider([BIGS[nm] for nm in names], local_grads(names))

    def pair_sum(tag, names, lands):
        jobs = [_pair_sum_job(BIGS[nm], g, l) for nm, g, l in zip(names, local_grads(names), lands)]
        return [r[0] for r in _multi_call(f"pair_sum_{tag}", jobs, kc)]

    def scatter(names, sums):
        return _scatter_rider([BIGS[nm] for nm in names], sums)

    def keep(names, sums, got):
        for nm, s, l in zip(names, sums, got):
            landed[nm] = (s, l)

    (dx5, local["w_ple_gate1"], local["w_ple_proj1"], d_ple1, d_plepost1), _ = _ple_bwd(
        1, dx6, x5, z1, pe1, p3d, gains["ple_g"], weight("w_ple_gate1"), gains["ple_post_g"])

    group_a = ["w_ple_gate1", "w_ple_proj1"]
    (dx4, d_preffn1, d_postffn1, *scattered), lands_a = _ffn_bwd(
        1, dx5, x4, f1, g1, u1, gains["pre_ffn_g"], weight("w_gu1"), weight("w_down1"), gains["post_ffn_g"], kc,
        rider=pair_exchange(group_a))
    fused["w_gu1"], fused["w_down1"] = scattered[0:2], scattered[2:4]
    sums_a = pair_sum("a", group_a, lands_a)

    (dq, dkv, local["w_o"], d_postmix1, d_sinks), got = _attn_bwd(
        dx4, y1, attn, q, kv, sinks, weight("w_o"), gains["post_mix_g"], rider=scatter(group_a, sums_a))
    keep(group_a, sums_a, got)
    dx3, local["w_q"], local["w_kv"], d_premix1, d_kvg = _qkv_bwd(
        dq, dkv, x3, dx4, gains["pre_mix_g"], kv_g2d, weight("w_q"), weight("w_kv"))

    group_b = ["w_o", "w_q", "w_kv"]
    sums_b = pair_sum("b", group_b, _run("grads_pair_exchange_b", pair_exchange(group_b)))
    (dx2, local["w_ple_gate0"], local["w_ple_proj0"], d_ple0, d_plepost0), got = _ple_bwd(
        0, dx3, x2, z0, pe0, p3d, gains["ple_g"], weight("w_ple_gate0"), gains["ple_post_g"],
        rider=scatter(group_b, sums_b))
    keep(group_b, sums_b, got)

    group_c = ["w_ple_gate0", "w_ple_proj0"]
    (dx1, d_preffn0, d_postffn0, *scattered), lands_c = _ffn_bwd(
        0, dx2, x1, f0, g0, u0, gains["pre_ffn_g"], weight("w_gu0"), weight("w_down0"), gains["post_ffn_g"], kc,
        rider=pair_exchange(group_c))
    fused["w_gu0"], fused["w_down0"] = scattered[0:2], scattered[2:4]
    sums_c = pair_sum("c", group_c, lands_c)

    layers_of = lambda src: [t for t in BIGS.values() if t.src == src]
    own_scatter = ["w_gu", "w_down"]
    early = own_scatter + ["w_q", "w_o", "w_kv"]
    late = ["w_ple_gate", "w_ple_proj"]
    by_cols = lambda srcs: [src == "w_down" for src in srcs]
    jobs = [_chip_sum_fused_job(layers_of(src), fused, by_cols=src == "w_down") for src in own_scatter]
    jobs += [_chip_sum_job(layers_of(src), landed) for src in early if src not in own_scatter]
    halves = [r[0] for r in _multi_call("chip_sum_early", jobs, kc)]
    (dx0, d_pool, d_scale, d_postmix0, d_premix0), got = _mixa_bwd(
        dx1, x2d, y0, gains["pre_mix_g"], weight("pool_w"), weight("pool_scale"), gains["post_mix_g"],
        rider=_both(scatter(group_c, sums_c), _share_rider(halves, by_cols(early))))
    keep(group_c, sums_c, got[:len(group_c)])
    full_grads = dict(zip(early, got[len(group_c):]))

    rows = [d_premix0, d_premix1, d_postmix0, d_postmix1, d_preffn0, d_preffn1, d_postffn0, d_postffn1,
            d_ple0, d_ple1, d_plepost0, d_plepost1, d_kvg, d_scale, d_sinks, loss_row]
    as2d = lambda a: a.reshape(1, D) if a.ndim == 1 else a
    (tot, g_pool), _ = _small_all_reduce(rows, d_pool)
    loss, small = _small_adamw(tot, kc, {nm: as2d(weights[nm]) for nm in SMALL_NAMES},
                               {nm: as2d(m_in[nm]) for nm in SMALL_NAMES},
                               {nm: as2d(v_in[nm]) for nm in SMALL_NAMES})

    halves = [r[0] for r in _multi_call("chip_sum_late", [_chip_sum_job(layers_of(src), landed) for src in late], kc)]
    full_grads.update(zip(late, _run("grads_pair_share", _share_rider(halves, by_cols(late)))))
    full_grads["pool_w"] = g_pool
    others = [src for src in BIG_SOURCES if src not in own_scatter and src != "pool_w"]

    def adam_args(src):
        return (layers_of(src)[0].rb, shard_view(src, weights[src]), full_grads[src],
                shard_view(src, m_in[src]), shard_view(src, v_in[src]))

    out = {"grad": {}, "delta": {}, "new_m": {}, "new_v": {}}
    results = {src: _adamw(src, *adam_args(src)) for src in own_scatter}
    rest_srcs = others + ["pool_w"]
    results.update(zip(rest_srcs, _multi_call("adamw_rest", [_adamw_job(*adam_args(src)) for src in rest_srcs], kc)))
    for src in BIG_SOURCES:
        shape = weights[src].shape
        for kind, a in zip(("grad", "delta", "new_m", "new_v"), results[src]):
            out[kind][src] = a.reshape(shape)
    for nm in SMALL_NAMES:
        shape = weights[nm].shape
        for kind, a in zip(("grad", "delta", "new_m", "new_v"), small[nm]):
            out[kind][nm] = a.reshape(shape)

    return (loss.reshape(()), dx0.reshape(x.shape),
            *[out["grad"][nm] for nm in order], *[out["delta"][nm] for nm in order],
            *[out["new_m"][nm] for nm in order], *[out["new_v"][nm] for nm in order])
```

```python
import collections

import jax
import jax.numpy as jnp
from jax import lax
from jax.experimental import pallas as pl
from jax.experimental.pallas import tpu as pltpu

D = 1024
FF = 2816
N_HEADS = 16
HEAD_DIM = 64
N_KV_HEADS = 4
GQA = N_HEADS // N_KV_HEADS
KVD = N_KV_HEADS * HEAD_DIM
PLE = 256
BLK = 128
WINDOWS = (2, 4, 8, 16)
POOL_G = 256
HALO = 16
EPS = 1e-6
NEG_INF = -1e30
ATT_SCALE = HEAD_DIM ** -0.5
SLOPES = tuple(2.0 ** (-8.0 * (h + 1) / N_HEADS) for h in range(N_HEADS))
N_CHIPS = 4
N_DEV = 8

LR, B1, B2, AEPS, WD, STEP = 0.001, 0.9, 0.999, 1e-08, 0.01, 10
BC1 = 1.0 - B1 ** STEP
BC2 = 1.0 - B2 ** STEP

BF = jnp.bfloat16
F32 = jnp.float32
MESH = pl.DeviceIdType.MESH
VMEM_LIMIT_V7X = 58 * 1024 * 1024
TM = 256
TM_FFN_BWD = 512
FF_CHUNK = 256
FF_HALF = FF // 2

VSPEC = pl.BlockSpec(memory_space=pltpu.VMEM)
SSPEC = pl.BlockSpec(memory_space=pltpu.SMEM)
ANYSPEC = pl.BlockSpec(memory_space=pl.ANY)


def _params(n_grid=0):
    sem = ("arbitrary",) * n_grid if n_grid else None
    return pltpu.CompilerParams(dimension_semantics=sem, vmem_limit_bytes=VMEM_LIMIT_V7X)


def _sds(shape, dtype=F32):
    return jax.ShapeDtypeStruct(tuple(shape), dtype)


Rider = collections.namedtuple("Rider", "arrays out_shapes aliases scratch start mid finish")
MID_NUM, MID_DEN = 5, 8


def _call(body, *, name, grid, in_specs, out_specs, out_shape, args, scratch_shapes=(), rider=None, prefetch=None):
    ni, no, ns = len(in_specs), len(out_specs), len(scratch_shapes)
    npre = 0 if prefetch is None else 1
    pre = [] if prefetch is None else [prefetch]
    if rider is None:
        rider = Rider([], [], {}, [], None, None, None)
    ri, ro = len(rider.arrays), len(rider.out_shapes)

    def full(*refs):
        pre_refs, refs = refs[:npre], refs[npre:]
        ins, refs = refs[:ni], refs[ni:]
        rins, refs = refs[:ri], refs[ri:]
        outs, refs = refs[:no], refs[no:]
        routs, refs = refs[:ro], refs[ro:]
        scr, rscr = refs[:ns], refs[ns:]
        ids = [pl.program_id(a) for a in range(len(grid))]
        first = ids[0] == 0
        last = ids[0] == grid[0] - 1
        for a in range(1, len(grid)):
            first = first & (ids[a] == 0)
            last = last & (ids[a] == grid[a] - 1)

        if rider.start is not None:
            @pl.when(first)
            def _():
                rider.start(rins, routs, rscr)

        if rider.mid is not None:
            assert len(grid) == 1

            @pl.when(ids[0] == (grid[0] * MID_NUM) // MID_DEN)
            def _():
                rider.mid(rins, routs, rscr)

        body(*pre_refs, *ins, *outs, *scr)

        if rider.finish is not None:
            @pl.when(last)
            def _():
                rider.finish(rins, routs, rscr)

    outs = pl.pallas_call(
        full, name=name,
        grid_spec=pltpu.PrefetchScalarGridSpec(
            num_scalar_prefetch=npre, grid=grid,
            in_specs=list(in_specs) + [ANYSPEC] * ri, out_specs=list(out_specs) + [ANYSPEC] * ro,
            scratch_shapes=list(scratch_shapes) + list(rider.scratch)),
        out_shape=list(out_shape) + list(rider.out_shapes),
        input_output_aliases={npre + ni + a: no + b for a, b in rider.aliases.items()},
        compiler_params=_params(len(grid)))(*pre, *args, *rider.arrays)
    return list(outs[:no]), list(outs[no:])


def _run(name, rider):
    ri = len(rider.arrays)

    def body(*refs):
        rins, routs, rscr = refs[:ri], refs[ri:ri + len(rider.out_shapes)], refs[ri + len(rider.out_shapes):]
        rider.start(rins, routs, rscr)
        if rider.mid is not None:
            rider.mid(rins, routs, rscr)
        rider.finish(rins, routs, rscr)

    return pl.pallas_call(
        body, name=name, in_specs=[ANYSPEC] * ri, out_specs=[ANYSPEC] * len(rider.out_shapes),
        out_shape=list(rider.out_shapes), scratch_shapes=list(rider.scratch),
        input_output_aliases=dict(rider.aliases), compiler_params=_params())(*rider.arrays)


Job = collections.namedtuple("Job", "steps ins outs fn")


def _multi_call(name, jobs, kc, rider=None):
    n = max(job.steps for job in jobs)

    def clamped(index, steps):
        return lambda s, kc_ref: index(jnp.minimum(s, steps - 1), kc_ref)

    in_specs, out_specs, out_shape, args = [], [], [], []
    for job in jobs:
        for arr, block, index, *single in job.ins:
            mode = dict(pipeline_mode=pl.Buffered(1)) if single and single[0] else {}
            in_specs.append(pl.BlockSpec(block, clamped(index, job.steps), **mode))
            args.append(arr)
        for sds, block, index in job.outs:
            out_specs.append(pl.BlockSpec(block, clamped(index, job.steps)))
            out_shape.append(sds)
    n_in = len(args)

    def body(kc_ref, *refs):
        s = pl.program_id(0)
        i0, o0 = 0, n_in
        for job in jobs:
            ins, outs = refs[i0:i0 + len(job.ins)], refs[o0:o0 + len(job.outs)]
            i0, o0 = i0 + len(job.ins), o0 + len(job.outs)

            @pl.when(s < job.steps)
            def _():
                job.fn(s, kc_ref, ins, outs)

    outs, routs = _call(body, name=name, grid=(n,), in_specs=in_specs, out_specs=out_specs, out_shape=out_shape,
                        args=args, prefetch=kc, rider=rider)
    res, o0 = [], 0
    for job in jobs:
        res.append(outs[o0:o0 + len(job.outs)])
        o0 += len(job.outs)
    return res if rider is None else (res, routs)


def _rms_fwd(x, g):
    r = lax.rsqrt(jnp.mean(x * x, axis=-1, keepdims=True) + EPS)
    return x * r * g


def _rms_bwd(x, g, dy):
    r = lax.rsqrt(jnp.mean(x * x, axis=-1, keepdims=True) + EPS)
    xn = x * r
    dxn = dy * g
    dx = r * (dxn - xn * jnp.mean(dxn * xn, axis=-1, keepdims=True))
    return dx, dy * xn


def _rowsum(a):
    return jnp.sum(a, axis=0, keepdims=True)


def _sigmoid(z):
    return 1.0 / (1.0 + jnp.exp(-z))


def _dot(a, b):
    return jnp.dot(a, b, preferred_element_type=F32)


def _dot_nt(a, b):
    return lax.dot_general(a, b, (((1,), (1,)), ((), ())), preferred_element_type=F32)


def _dot_tn(a, b):
    return lax.dot_general(a, b, (((0,), (0,)), ((), ())), preferred_element_type=F32)


def _row_spec(tm, width=D):
    return pl.BlockSpec((tm, width), lambda i: (i, 0))


def _const_spec(shape):
    zeros = (0,) * len(shape)
    return pl.BlockSpec(tuple(shape), lambda *_: zeros)


def _pool_delta(he, pos):
    out = []
    for gi, w in enumerate(WINDOWS):
        hg = he[:, gi * POOL_G:(gi + 1) * POOL_G]
        s = hg
        k = 1
        while k < w:
            s = s + pltpu.roll(s, k, 0)
            k *= 2
        cnt = jnp.maximum(jnp.minimum(pos + 1, w), 1).astype(F32)
        out.append(s / cnt - hg)
    return out


def _load_with_halo_before(x_ref, i, tm):
    r0 = pl.multiple_of(i * tm, tm)
    hs = pl.multiple_of(jnp.maximum(i * tm - HALO, 0), 8)
    xh = jnp.where(i > 0, x_ref[pl.ds(hs, HALO), :], 0.0)
    xt = x_ref[pl.ds(r0, tm), :]
    return xt, jnp.concatenate([xh, xt], axis=0)


def _mixa_fwd_job(x, pre_g, pool_w, pool_scale, post_g):
    s_len = x.shape[0]

    def fn(i, kc_ref, ins, outs):
        x_ref, pg_ref, w_ref, sc_ref, qg_ref = ins
        y_ref, x1_ref = outs
        xt, xe = _load_with_halo_before(x_ref, i, TM)
        he = _rms_fwd(xe, pg_ref[0:1, :])
        pos = i * TM - HALO + lax.broadcasted_iota(jnp.int32, (TM + HALO, 1), 0)
        ds = _pool_delta(he, pos)
        ys = [_dot(ds[gi][HALO:, :].astype(BF), w_ref[gi]) for gi in range(len(WINDOWS))]
        y = jnp.concatenate(ys, axis=1) * sc_ref[...]
        y_ref[...] = y
        x1_ref[...] = xt + _rms_fwd(y, qg_ref[0:1, :])

    def whole(a):
        zeros = (0,) * a.ndim
        return (a, a.shape, lambda j, kc_ref: zeros, True)

    rows = lambda j, kc_ref: (j, 0)
    return Job(s_len // TM, [whole(a) for a in (x, pre_g, pool_w, pool_scale, post_g)],
               [(_sds((s_len, D)), (TM, D), rows), (_sds((s_len, D)), (TM, D), rows)], fn)


def _mixa_bwd(dx1, x, y, pre_g, pool_w, pool_scale, post_g, rider=None):
    s_len = x.shape[0]
    n = s_len // TM
    ng = len(WINDOWS)

    def body(dx_ref, x_ref, y_ref, pg_ref, w_ref, sc_ref, qg_ref,
             dx0_ref, dw_ref, dsc_ref, dqg_ref, dpg_ref, wacc):
        i = pl.program_id(0)

        @pl.when(i == 0)
        def _():
            wacc[...] = jnp.zeros_like(wacc)
            dsc_ref[...] = jnp.zeros_like(dsc_ref)
            dqg_ref[...] = jnp.zeros_like(dqg_ref)
            dpg_ref[...] = jnp.zeros_like(dpg_ref)

        r0 = pl.multiple_of(i * TM, TM)
        xt, xe = _load_with_halo_before(x_ref, i, TM)
        he = _rms_fwd(xe, pg_ref[0:1, :])
        pos_b = i * TM - HALO + lax.broadcasted_iota(jnp.int32, (TM + HALO, 1), 0)
        ds = _pool_delta(he, pos_b)

        last = i == n - 1
        a0 = pl.multiple_of(jnp.minimum(i * TM + TM, s_len - HALO), 8)
        ye = jnp.concatenate([y_ref[pl.ds(r0, TM), :], y_ref[pl.ds(a0, HALO), :]], axis=0)
        dt = dx_ref[pl.ds(r0, TM), :]
        de = jnp.concatenate([dt, jnp.where(last, 0.0, dx_ref[pl.ds(a0, HALO), :])], axis=0)
        dye, prod = _rms_bwd(ye, qg_ref[0:1, :], de)
        dqg_ref[...] += _rowsum(prod[:TM, :])
        dys = dye * sc_ref[...]
        pos_a = i * TM + lax.broadcasted_iota(jnp.int32, (TM + HALO, 1), 0)

        dhs, dscs = [], []
        for gi, w in enumerate(WINDOWS):
            sl = slice(gi * POOL_G, (gi + 1) * POOL_G)
            wg = w_ref[gi]
            dys_g = dys[:, sl].astype(BF)
            d_g = ds[gi][HALO:, :].astype(BF)
            ypre = _dot(d_g, wg)
            dscs.append(_rowsum(dye[:TM, sl] * ypre))
            wacc[gi] += _dot_tn(d_g, dys_g[:TM, :])
            dd = _dot_nt(dys_g, wg)
            cnt = jnp.minimum(pos_a + 1, w).astype(F32)
            a = dd / cnt
            k = 1
            while k < w:
                a = a + pltpu.roll(a, TM + HALO - k, 0)
                k *= 2
            dhs.append(a[:TM, :] - dd[:TM, :])
        dsc_ref[...] += jnp.concatenate(dscs, axis=1)
        dh = jnp.concatenate(dhs, axis=1)
        dxp, prod2 = _rms_bwd(xt, pg_ref[0:1, :], dh)
        dpg_ref[...] += _rowsum(prod2)
        dx0_ref[...] = dt + dxp

        @pl.when(last)
        def _():
            dw_ref[...] = wacc[...].astype(BF)

    return _call(
        body, name="mixa_bwd", grid=(n,), in_specs=[VSPEC] * 7,
        out_specs=[_row_spec(TM), _const_spec((ng, POOL_G, POOL_G)), _const_spec((1, D)),
                   _const_spec((1, D)), _const_spec((1, D))],
        out_shape=[_sds((s_len, D)), _sds((ng, POOL_G, POOL_G), BF), _sds((1, D)), _sds((1, D)), _sds((1, D))],
        scratch_shapes=[pltpu.VMEM((ng, POOL_G, POOL_G), F32)],
        args=[dx1, x, y, pre_g, pool_w, pool_scale, post_g], rider=rider)


def _ffn_fwd(layer, x1, pre_g, wgu, wd, post_g, rider=None):
    s_len = x1.shape[0]

    def body(x_ref, pg_ref, wgu_ref, wd_ref, qg_ref, f_ref, x2_ref, g_ref, u_ref):
        x = x_ref[...]
        h = _rms_fwd(x, pg_ref[layer:layer + 1, :]).astype(BF)
        f = jnp.zeros((TM, D), F32)
        for c in range(FF // FF_HALF):
            cols = slice(c * FF_HALF, (c + 1) * FF_HALF)
            g = _dot(h, wgu_ref[0, :, cols])
            u = _dot(h, wgu_ref[1, :, cols])
            g_ref[:, cols] = g.astype(BF)
            u_ref[:, cols] = u.astype(BF)
            act = g * _sigmoid(g) * u
            f = f + _dot(act.astype(BF), wd_ref[cols, :])
        f_ref[...] = f
        x2_ref[...] = x + _rms_fwd(f, qg_ref[layer:layer + 1, :])

    return _call(body, name=f"ffn_fwd{layer}", grid=(s_len // TM,),
                 in_specs=[_row_spec(TM), VSPEC, VSPEC, VSPEC, VSPEC],
                 out_specs=[_row_spec(TM), _row_spec(TM), _row_spec(TM, FF), _row_spec(TM, FF)],
                 out_shape=[_sds((s_len, D)), _sds((s_len, D)), _sds((s_len, FF), BF), _sds((s_len, FF), BF)],
                 args=[x1, pre_g, wgu, wd, post_g], rider=rider)


GU_PIECE = 128
DN_PIECE = 64
DN_SLOT = FF // N_CHIPS
HALF_D = D // 2
CHUNK_STRIDE = 6
CHUNK_START = (1, 7, 4, 10)


def _ffn_bwd(layer, dx2, x1, f, g_pre, u_pre, pre_g, wgu, wd, post_g, kc, rider=None):
    s_len = x1.shape[0]
    tm = TM_FFN_BWD
    n = s_len // tm
    nc = FF // FF_CHUNK
    n_gu, n_dn = FF_CHUNK // GU_PIECE, FF_CHUNK // DN_PIECE
    n_pieces = 2 * n_gu + n_dn
    n_blk = FF_HALF // GU_PIECE

    def edge_rows(c, i, kc_ref):
        return (jnp.where((c == 0) | (c == nc - 1), i, n - 1), 0)

    def chunk_at(c, kc_ref):
        k = kc_ref[0]
        start = jnp.where(k == 0, CHUNK_START[0], jnp.where(k == 1, CHUNK_START[1],
                                                            jnp.where(k == 2, CHUNK_START[2], CHUNK_START[3])))
        return ((c + start) * CHUNK_STRIDE) % nc

    def exchange(kc_ref, c, accg, accu, accd, own_gu_ref, land_gu_ref, own_dn_ref, land_dn_ref,
                 pl_gu, pl_dn, sib_gu, sib_dn, mine_gu, mine_dn, sum_gu, sum_dn,
                 psend, precv, ssend, lsem, rrecv):
        x, y, core = lax.axis_index("x"), lax.axis_index("y"), lax.axis_index("c")
        lower = core == 0

        def pair_copy(cc, part):
            p = cc % 2
            src, dst = ((sib_gu, pl_gu), (sib_dn, pl_dn))[part]
            return pltpu.make_async_remote_copy(src.at[p], dst.at[cc], psend.at[p, part], precv.at[cc, part],
                                                device_id=(x, y, 1 - core), device_id_type=MESH)

        def scatter(cc, wait):
            p = cc % 2
            hidden = chunk_at(cc, kc_ref) * FF_CHUNK

            assert n_gu == 2
            k0, k1 = hidden // FF_HALF, (hidden + GU_PIECE) // FF_HALF
            blk = (hidden - k0 * FF_HALF) // GU_PIECE
            for gu in range(2):
                @pl.when(k0 == k1)
                def _():
                    piece(p, wait, 2 * gu, sum_gu.at[p, gu], k0 + 2 * gu, 0, (pl.ds(blk, 2),))

                @pl.when(k0 != k1)
                def _():
                    piece(p, wait, 2 * gu, sum_gu.at[p, gu, 0], k0 + 2 * gu, 0, (blk,))
                    piece(p, wait, 2 * gu + 1, sum_gu.at[p, gu, 1], k1 + 2 * gu, 0, (0,))

            kd = hidden // DN_SLOT
            off = pl.multiple_of(hidden - kd * DN_SLOT, DN_PIECE)
            m = jnp.minimum((DN_SLOT - off) // DN_PIECE, n_dn)
            for mm in range(1, n_dn + 1):
                @pl.when(m == mm)
                def _():
                    rows = mm * DN_PIECE
                    piece(p, wait, 2 * n_gu, sum_dn.at[p, pl.ds(0, rows), :], kd, 1, (pl.ds(off, rows), slice(None)))
                    if mm < n_dn:
                        piece(p, wait, 2 * n_gu + 1, sum_dn.at[p, pl.ds(rows, FF_CHUNK - rows), :], kd + 1, 1,
                              (pl.ds(0, FF_CHUNK - rows), slice(None)))

        def piece(p, wait, pi, src, k, t, where):
            own_ref, land_ref = ((own_gu_ref, land_gu_ref), (own_dn_ref, land_dn_ref))[t]
            kx, ky = k // 2, k % 2
            fx, fy = (kx != x).astype(jnp.int32), (ky != y).astype(jnp.int32)
            local = (fx + fy) == 0
            j = jnp.maximum(fx + 2 * fy - 1, 0)

            @pl.when(local)
            def _():
                cp = pltpu.make_async_copy(src, own_ref.at[where], lsem.at[p, pi])
                if wait:
                    cp.wait()
                else:
                    cp.start()

            @pl.when(jnp.logical_not(local))
            def _():
                cp = pltpu.make_async_remote_copy(src, land_ref.at[(j,) + where], ssend.at[p, pi],
                                                  rrecv.at[t, j], device_id=(kx, ky, core), device_id_type=MESH)
                if wait:
                    cp.wait_send()
                else:
                    cp.start()

        def add_and_scatter(cc):
            p = cc % 2
            pair_copy(cc, 0).wait_recv()
            pair_copy(cc, 1).wait_recv()
            s_gu = (mine_gu[...] + pl_gu[cc].astype(F32)).astype(BF)
            for hc in range(n_gu):
                sum_gu[p, :, hc] = s_gu[:, :, hc * GU_PIECE:(hc + 1) * GU_PIECE]
            sum_dn[p] = (mine_dn[...] + pl_dn[cc].astype(F32)).astype(BF)
            scatter(cc, wait=False)

        @pl.when(c >= 1)
        def _():
            @pl.when(c >= 3)
            def _():
                scatter(c - 3, wait=True)
            add_and_scatter(c - 1)

        @pl.when(c >= 2)
        def _():
            pair_copy(c - 2, 0).wait_send()
            pair_copy(c - 2, 1).wait_send()

        p = c % 2
        my_rows = pl.ds(pl.multiple_of(core * HALF_D, HALF_D), HALF_D)
        sib_rows = pl.ds(pl.multiple_of((1 - core) * HALF_D, HALF_D), HALF_D)
        d_v = accd[...]
        sib_gu[p, 0] = accg[sib_rows, :].astype(BF)
        sib_gu[p, 1] = accu[sib_rows, :].astype(BF)
        sib_dn[p] = jnp.where(lower, d_v[:, HALF_D:], d_v[:, :HALF_D]).astype(BF)
        mine_gu[0] = accg[my_rows, :]
        mine_gu[1] = accu[my_rows, :]
        mine_dn[...] = jnp.where(lower, d_v[:, :HALF_D], d_v[:, HALF_D:])
        pair_copy(c, 0).start()
        pair_copy(c, 1).start()

        @pl.when(c == nc - 1)
        def _():
            scatter(nc - 3, wait=True)
            add_and_scatter(nc - 1)
            for cc in (nc - 2, nc - 1):
                pair_copy(cc, 0).wait_send()
                pair_copy(cc, 1).wait_send()
                scatter(cc, wait=True)
            for t, land_ref in enumerate((land_gu_ref, land_dn_ref)):
                for j in range(N_CHIPS - 1):
                    pltpu.make_async_remote_copy(land_ref.at[j], land_ref.at[j], ssend.at[0, 0], rrecv.at[t, j],
                                                 device_id=(x, y, core), device_id_type=MESH).wait_recv()

    def body(kc_ref, dx_ref, x_ref, f_ref, gp_ref, up_ref, pg_ref, wgu_ref, wd_ref, qg_ref,
             dx1_ref, dpg_ref, dqg_ref, own_gu_ref, land_gu_ref, own_dn_ref, land_dn_ref,
             h_s, df_s, dh_s, accg, accu, accd, *comm):
        c = pl.program_id(0)
        i = pl.program_id(1)
        rows = pl.ds(pl.multiple_of(i * tm, tm), tm)
        pg = pg_ref[layer:layer + 1, :]

        @pl.when((c == 0) & (i == 0))
        def _():
            dpg_ref[...] = jnp.zeros_like(dpg_ref)
            dqg_ref[...] = jnp.zeros_like(dqg_ref)

        @pl.when(c == 0)
        def _():
            h_s[rows, :] = _rms_fwd(x_ref[...], pg).astype(BF)
            df, prod = _rms_bwd(f_ref[...], qg_ref[layer:layer + 1, :], dx_ref[...])
            df_s[rows, :] = df.astype(BF)
            dqg_ref[...] += _rowsum(prod)

        @pl.when(i == 0)
        def _():
            accg[...] = jnp.zeros_like(accg)
            accu[...] = jnp.zeros_like(accu)
            accd[...] = jnp.zeros_like(accd)

        h = h_s[rows, :]
        df = df_s[rows, :]
        wg = wgu_ref[0]
        wu = wgu_ref[1]
        g = gp_ref[...].astype(F32)
        u = up_ref[...].astype(F32)
        sg = _sigmoid(g)
        a = g * sg
        dact = _dot_nt(df, wd_ref[...])
        accd[...] += _dot_tn((a * u).astype(BF), df)
        du = (dact * a).astype(BF)
        dg = (dact * u * (sg * (1.0 + g * (1.0 - sg)))).astype(BF)
        accg[...] += _dot_tn(h, dg)
        accu[...] += _dot_tn(h, du)
        dh = _dot_nt(dg, wg) + _dot_nt(du, wu)

        @pl.when(c == 0)
        def _():
            dh_s[rows, :] = dh

        @pl.when((c > 0) & (c < nc - 1))
        def _():
            dh_s[rows, :] += dh

        @pl.when(c == nc - 1)
        def _():
            dxp, prod = _rms_bwd(x_ref[...], pg, dh_s[rows, :] + dh)
            dpg_ref[...] += _rowsum(prod)
            dx1_ref[...] = dx_ref[...] + dxp

        @pl.when(i == n - 1)
        def _():
            exchange(kc_ref, c, accg, accu, accd, own_gu_ref, land_gu_ref, own_dn_ref, land_dn_ref, *comm)

    dma = pltpu.SemaphoreType.DMA
    return _call(
        body, name=f"ffn_bwd{layer}", grid=(nc, n),
        in_specs=[pl.BlockSpec((tm, D), edge_rows), pl.BlockSpec((tm, D), edge_rows),
                  pl.BlockSpec((tm, D), lambda c, i, kc_ref: (jnp.where(c == 0, i, n - 1), 0),
                               pipeline_mode=pl.Buffered(1)),
                  pl.BlockSpec((tm, FF_CHUNK), lambda c, i, kc_ref: (i, chunk_at(c, kc_ref))),
                  pl.BlockSpec((tm, FF_CHUNK), lambda c, i, kc_ref: (i, chunk_at(c, kc_ref))),
                  VSPEC,
                  pl.BlockSpec((2, D, FF_CHUNK), lambda c, i, kc_ref: (0, 0, chunk_at(c, kc_ref))),
                  pl.BlockSpec((FF_CHUNK, D), lambda c, i, kc_ref: (chunk_at(c, kc_ref), 0)),
                  VSPEC],
        out_specs=[pl.BlockSpec((tm, D), lambda c, i, kc_ref: (jnp.where(c == nc - 1, i, 0), 0)),
                   _const_spec((1, D)), _const_spec((1, D)), ANYSPEC, ANYSPEC, ANYSPEC, ANYSPEC],
        out_shape=[_sds((s_len, D)), _sds((1, D)), _sds((1, D)),
                   _sds((n_blk, HALF_D, GU_PIECE), BF), _sds((N_CHIPS - 1, n_blk, HALF_D, GU_PIECE), BF),
                   _sds((DN_SLOT, HALF_D), BF), _sds((N_CHIPS - 1, DN_SLOT, HALF_D), BF)],
        scratch_shapes=[pltpu.VMEM((s_len, D), BF), pltpu.VMEM((s_len, D), BF), pltpu.VMEM((s_len, D), F32),
                        pltpu.VMEM((D, FF_CHUNK), F32), pltpu.VMEM((D, FF_CHUNK), F32),
                        pltpu.VMEM((FF_CHUNK, D), F32),
                        pltpu.VMEM((nc, 2, HALF_D, FF_CHUNK), BF), pltpu.VMEM((nc, FF_CHUNK, HALF_D), BF),
                        pltpu.VMEM((2, 2, HALF_D, FF_CHUNK), BF), pltpu.VMEM((2, FF_CHUNK, HALF_D), BF),
                        pltpu.VMEM((2, HALF_D, FF_CHUNK), F32), pltpu.VMEM((FF_CHUNK, HALF_D), F32),
                        pltpu.VMEM((2, 2, n_gu, HALF_D, GU_PIECE), BF), pltpu.VMEM((2, FF_CHUNK, HALF_D), BF),
                        dma((2, 2)), dma((nc, 2)), dma((2, n_pieces)), dma((2, n_pieces)), dma((2, N_CHIPS - 1))],
        args=[dx2, x1, f, g_pre, u_pre, pre_g, wgu, wd, post_g], rider=rider, prefetch=kc)


def _ple_fwd(layer, x2, p, ple_g, w_gate, w_proj, post_g, target=None, qkv=None, rider=None):
    s_len = x2.shape[0]
    final = target is not None
    assert not (final and qkv)

    def body(*refs):
        if final:
            x_ref, p_ref, g_ref, wg_ref, wp_ref, qg_ref, t_ref, z_ref, pe_ref, dx_ref, lv_ref = refs
        elif qkv:
            (x_ref, p_ref, g_ref, wg_ref, wp_ref, qg_ref, ng_ref, kg_ref, wq_ref, wkv_ref,
             z_ref, pe_ref, x3_ref, q_ref, kv_ref) = refs
        else:
            x_ref, p_ref, g_ref, wg_ref, wp_ref, qg_ref, z_ref, pe_ref, x3_ref = refs
        x = x_ref[...]
        r = _rms_fwd(x, g_ref[layer:layer + 1, :]).astype(BF)
        z = _dot(r, wg_ref[...])
        pe = _dot(p_ref[...].astype(BF), wp_ref[...])
        z_ref[...] = z
        pe_ref[...] = pe
        x3 = x + _rms_fwd(pe * _sigmoid(z), qg_ref[layer:layer + 1, :])
        if final:
            @pl.when(pl.program_id(0) == 0)
            def _():
                lv_ref[...] = jnp.zeros_like(lv_ref)
            err = x3 - t_ref[...]
            dx_ref[...] = err * (1.0 / D)
            lv_ref[...] += _rowsum(err * err)
        else:
            x3_ref[...] = x3
        if qkv:
            q_ref[...] = _dot(_rms_fwd(x3, ng_ref[layer + 1:layer + 2, :]).astype(BF), wq_ref[...]).astype(BF)
            kv_ref[...] = _dot(_rms_fwd(x3, kg_ref[...]).astype(BF), wkv_ref[...]).astype(BF)

    p_spec = pl.BlockSpec((None, TM, PLE), lambda i: (layer, i, 0))
    in_specs = [_row_spec(TM), p_spec, VSPEC, VSPEC, VSPEC, VSPEC]
    args = [x2, p, ple_g, w_gate, w_proj, post_g]
    out_specs = [_row_spec(TM), _row_spec(TM), _row_spec(TM)]
    out_shape = [_sds((s_len, D))] * 3
    if qkv:
        in_specs += [VSPEC] * 4
        args += list(qkv)
        out_specs += [_row_spec(TM), _row_spec(TM, 2 * KVD)]
        out_shape += [_sds((s_len, D), BF), _sds((s_len, 2 * KVD), BF)]
    if final:
        in_specs.append(_row_spec(TM))
        args.append(target)
        out_specs.append(_const_spec((1, D)))
        out_shape.append(_sds((1, D)))
    return _call(body, name=f"ple_fwd{layer}", grid=(s_len // TM,), in_specs=in_specs, out_specs=out_specs,
                 out_shape=out_shape, args=args, rider=rider)


def _ple_bwd(layer, dx3, x2, z, pe, p, ple_g, w_gate, post_g, rider=None):
    s_len = x2.shape[0]
    n = s_len // TM

    def body(dx_ref, x_ref, z_ref, pe_ref, p_ref, g_ref, wg_ref, qg_ref,
             dx2_ref, dwg_ref, dwp_ref, dg_ref, dqg_ref, gacc, pacc):
        i = pl.program_id(0)

        @pl.when(i == 0)
        def _():
            gacc[...] = jnp.zeros_like(gacc)
            pacc[...] = jnp.zeros_like(pacc)
            dg_ref[...] = jnp.zeros_like(dg_ref)
            dqg_ref[...] = jnp.zeros_like(dqg_ref)

        dx = dx_ref[...]
        x = x_ref[...]
        pe_v = pe_ref[...]
        gate = _sigmoid(z_ref[...])
        de, prod = _rms_bwd(pe_v * gate, qg_ref[layer:layer + 1, :], dx)
        dqg_ref[...] += _rowsum(prod)
        dpe = (de * gate).astype(BF)
        dz = (de * pe_v * gate * (1.0 - gate)).astype(BF)
        pacc[...] += _dot_tn(p_ref[...].astype(BF), dpe)
        g = g_ref[layer:layer + 1, :]
        r = _rms_fwd(x, g).astype(BF)
        gacc[...] += _dot_tn(r, dz)
        dr = _dot_nt(dz, wg_ref[...])
        dxp, prod2 = _rms_bwd(x, g, dr)
        dg_ref[...] += _rowsum(prod2)
        dx2_ref[...] = dx + dxp

        @pl.when(i == n - 1)
        def _():
            dwg_ref[...] = gacc[...].astype(BF)
            dwp_ref[...] = pacc[...].astype(BF)

    p_spec = pl.BlockSpec((None, TM, PLE), lambda i: (layer, i, 0))
    return _call(
        body, name=f"ple_bwd{layer}", grid=(n,),
        in_specs=[_row_spec(TM), _row_spec(TM), _row_spec(TM), _row_spec(TM), p_spec, VSPEC, VSPEC, VSPEC],
        out_specs=[_row_spec(TM), _const_spec((D, D)), _const_spec((PLE, D)), _const_spec((1, D)), _const_spec((1, D))],
        out_shape=[_sds((s_len, D)), _sds((D, D), BF), _sds((PLE, D), BF), _sds((1, D)), _sds((1, D))],
        scratch_shapes=[pltpu.VMEM((D, D), F32), pltpu.VMEM((PLE, D), F32)],
        args=[dx3, x2, z, pe, p, ple_g, w_gate, post_g], rider=rider)


def _qkv_bwd(dq, dkv, x3, dx4, q_g, kv_g, w_q, w_kv):
    s_len = x3.shape[0]
    n = s_len // TM

    def body(dq_ref, dkv_ref, x_ref, dx_ref, qg_ref, kg_ref, wq_ref, wkv_ref,
             dx3_ref, dwq_ref, dwkv_ref, dqg_ref, dkg_ref, qacc, kacc):
        i = pl.program_id(0)

        @pl.when(i == 0)
        def _():
            qacc[...] = jnp.zeros_like(qacc)
            kacc[...] = jnp.zeros_like(kacc)
            dqg_ref[...] = jnp.zeros_like(dqg_ref)
            dkg_ref[...] = jnp.zeros_like(dkg_ref)

        x = x_ref[...]
        qg = qg_ref[1:2, :]
        kg = kg_ref[...]
        dq_v = dq_ref[...]
        dkv_v = dkv_ref[...].astype(BF)
        qacc[...] += _dot_tn(_rms_fwd(x, qg).astype(BF), dq_v)
        kacc[...] += _dot_tn(_rms_fwd(x, kg).astype(BF), dkv_v)
        dxq, prod_q = _rms_bwd(x, qg, _dot_nt(dq_v, wq_ref[...]))
        dxk, prod_k = _rms_bwd(x, kg, _dot_nt(dkv_v, wkv_ref[...]))
        dqg_ref[...] += _rowsum(prod_q)
        dkg_ref[...] += _rowsum(prod_k)
        dx3_ref[...] = dx_ref[...] + dxq + dxk

        @pl.when(i == n - 1)
        def _():
            dwq_ref[...] = qacc[...].astype(BF)
            dwkv_ref[...] = kacc[...].astype(BF)

    outs, _ = _call(
        body, name="qkv_bwd", grid=(n,),
        in_specs=[_row_spec(TM), _row_spec(TM, 2 * KVD), _row_spec(TM), _row_spec(TM), VSPEC, VSPEC, VSPEC, VSPEC],
        out_specs=[_row_spec(TM), _const_spec((D, D)), _const_spec((D, 2 * KVD)),
                   _const_spec((1, D)), _const_spec((1, D))],
        out_shape=[_sds((s_len, D)), _sds((D, D), BF), _sds((D, 2 * KVD), BF), _sds((1, D)), _sds((1, D))],
        scratch_shapes=[pltpu.VMEM((D, D), F32), pltpu.VMEM((D, 2 * KVD), F32)],
        args=[dq, dkv, x3, dx4, q_g, kv_g, w_q, w_kv])
    return outs


def _attn_group(i, q, kvw, sink_ref, g):
    rows = GQA * BLK
    heads = [GQA * g + j for j in range(GQA)]
    off = jnp.where(i > 0, BLK, 0)
    row = lax.broadcasted_iota(jnp.int32, (rows, 2 * BLK), 0)
    rel = (row % BLK) - lax.broadcasted_iota(jnp.int32, (rows, 2 * BLK), 1) + off
    valid = (rel >= 0) & (rel < BLK)
    head_of_row = lax.broadcasted_iota(jnp.int32, (rows, 1), 0) // BLK
    slope = jnp.zeros((rows, 1), F32)
    sink = jnp.zeros((rows, 1), F32)
    for j, h in enumerate(heads):
        slope = jnp.where(head_of_row == j, SLOPES[h], slope)
        sink = jnp.where(head_of_row == j, sink_ref[0, h], sink)
    qs = jnp.concatenate([q[:, h * HEAD_DIM:(h + 1) * HEAD_DIM] for h in heads], axis=0)
    k = kvw[:, g * HEAD_DIM:(g + 1) * HEAD_DIM]
    v = kvw[:, KVD + g * HEAD_DIM:KVD + (g + 1) * HEAD_DIM]
    s = _dot_nt(qs, k) * ATT_SCALE - slope * rel.astype(F32)
    s = jnp.where(valid, s, NEG_INF)
    m = jnp.maximum(jnp.max(s, axis=-1, keepdims=True), sink)
    e = jnp.exp(s - m)
    es = jnp.exp(sink - m)
    inv = 1.0 / (jnp.sum(e, axis=-1, keepdims=True) + es)
    return e * inv, es * inv, qs, k, v


def _unstack_heads(stacked):
    return [stacked[j * BLK:(j + 1) * BLK, :] for j in range(GQA)]


def _kv_window(kv_ref, i):
    ks = pl.multiple_of(jnp.maximum(i * BLK - BLK, 0), BLK)
    return ks, kv_ref[pl.ds(ks, 2 * BLK), :]


def _attn_fwd(q, kv, sinks, x3, w_o, post_g, rider=None):
    s_len = q.shape[0]

    def body(q_ref, kv_ref, sk_ref, x_ref, wo_ref, g_ref, a_ref, y_ref, x4_ref):
        i = pl.program_id(0)
        _, kvw = _kv_window(kv_ref, i)
        q = q_ref[...]
        outs = []
        for g in range(N_KV_HEADS):
            p, _, _, _, v = _attn_group(i, q, kvw, sk_ref, g)
            outs += _unstack_heads(_dot(p.astype(BF), v))
        attn = jnp.concatenate(outs, axis=1)
        a_ref[...] = attn
        y = _dot(attn.astype(BF), wo_ref[...])
        y_ref[...] = y
        x4_ref[...] = x_ref[...] + _rms_fwd(y, g_ref[1:2, :])

    return _call(body, name="attn_fwd", grid=(s_len // BLK,),
                 in_specs=[_row_spec(BLK), VSPEC, SSPEC, _row_spec(BLK), VSPEC, VSPEC],
                 out_specs=[_row_spec(BLK)] * 3, out_shape=[_sds((s_len, D))] * 3,
                 args=[q, kv, sinks, x3, w_o, post_g], rider=rider)


ATT_STEP_BLOCKS = 2


def _attn_bwd(dx4, y, attn, q, kv, sinks, w_o, post_g, rider=None):
    s_len = q.shape[0]
    rows = ATT_STEP_BLOCKS * BLK
    n = s_len // rows

    def body(dx_ref, y_ref, a_ref, q_ref, kv_ref, sk_ref, wo_ref, g_ref,
             dq_ref, dkv_ref, dwo_ref, dg_ref, dsk_ref, wacc):
        i = pl.program_id(0)

        @pl.when(i == 0)
        def _():
            dkv_ref[...] = jnp.zeros_like(dkv_ref)
            wacc[...] = jnp.zeros_like(wacc)
            dg_ref[...] = jnp.zeros_like(dg_ref)
            dsk_ref[...] = jnp.zeros_like(dsk_ref)

        dy, prod = _rms_bwd(y_ref[...], g_ref[1:2, :], dx_ref[...])
        dg_ref[...] += _rowsum(prod)
        dyb = dy.astype(BF)
        attn_all = a_ref[...]
        wacc[...] += _dot_tn(attn_all.astype(BF), dyb)
        d_o_all = _dot_nt(dyb, wo_ref[...])
        q_all = q_ref[...]
        lane = lax.broadcasted_iota(jnp.int32, (1, D), 1)
        dsk = jnp.zeros((1, D), F32)
        for sub in range(ATT_STEP_BLOCKS):
            blk = i * ATT_STEP_BLOCKS + sub
            sl = slice(sub * BLK, (sub + 1) * BLK)
            d_o, q = d_o_all[sl, :], q_all[sl, :]
            dod = d_o * attn_all[sl, :]
            ks, kvw = _kv_window(kv_ref, blk)
            dqs, dks, dvs = [], [], []
            for g in range(N_KV_HEADS):
                p, ps, qs, k, v = _attn_group(blk, q, kvw, sk_ref, g)
                cols = [slice((GQA * g + j) * HEAD_DIM, (GQA * g + j + 1) * HEAD_DIM) for j in range(GQA)]
                do_s = jnp.concatenate([d_o[:, c] for c in cols], axis=0).astype(BF)
                dsum = jnp.concatenate([jnp.sum(dod[:, c], axis=-1, keepdims=True) for c in cols], axis=0)
                dp = _dot_nt(do_s, v)
                dsb = (p * (dp - dsum) * ATT_SCALE).astype(BF)
                sink_part = ps * dsum
                for j in range(GQA):
                    dsk = dsk + jnp.where(lane == GQA * g + j, -_rowsum(sink_part[j * BLK:(j + 1) * BLK, :]), 0.0)
                dqs += _unstack_heads(_dot(dsb, k))
                dks.append(_dot_tn(dsb, qs))
                dvs.append(_dot_tn(p.astype(BF), do_s))
            dq_ref[sl, :] = jnp.concatenate(dqs, axis=1).astype(BF)
            dkv_ref[pl.ds(ks, 2 * BLK), :] += jnp.concatenate(dks + dvs, axis=1)
        dsk_ref[...] += dsk

        @pl.when(i == n - 1)
        def _():
            dwo_ref[...] = wacc[...].astype(BF)

    return _call(
        body, name="attn_bwd", grid=(n,),
        in_specs=[_row_spec(rows), _row_spec(rows), _row_spec(rows), _row_spec(rows), VSPEC, SSPEC, VSPEC, VSPEC],
        out_specs=[_row_spec(rows), _const_spec((s_len, 2 * KVD)), _const_spec((D, D)),
                   _const_spec((1, D)), _const_spec((1, D))],
        out_shape=[_sds((s_len, D), BF), _sds((s_len, 2 * KVD)), _sds((D, D), BF), _sds((1, D)), _sds((1, D))],
        scratch_shapes=[pltpu.VMEM((D, D), F32)],
        args=[dx4, y, attn, q, kv, sinks, w_o, post_g], rider=rider)


Big = collections.namedtuple("Big", "name src layer L A R C rb")


def _bigs():
    out = {"pool_w": Big("pool_w", "pool_w", None, 4, 4, POOL_G // N_CHIPS, POOL_G, 32)}
    for l in range(2):
        out[f"w_gu{l}"] = Big(f"w_gu{l}", "w_gu", l, 1, 2, D, FF_HALF, 256)
        out[f"w_down{l}"] = Big(f"w_down{l}", "w_down", l, 1, 4, FF // N_CHIPS, D, 352)
        out[f"w_ple_gate{l}"] = Big(f"w_ple_gate{l}", "w_ple_gate", l, 1, 4, D // N_CHIPS, D, 128)
        out[f"w_ple_proj{l}"] = Big(f"w_ple_proj{l}", "w_ple_proj", l, 1, 1, PLE, D // N_CHIPS, 128)
    out["w_q"] = Big("w_q", "w_q", None, 1, 4, D // N_CHIPS, D, 128)
    out["w_o"] = Big("w_o", "w_o", None, 1, 4, D // N_CHIPS, D, 128)
    out["w_kv"] = Big("w_kv", "w_kv", None, 1, 4, D // N_CHIPS, 2 * KVD, 128)
    return out


BIGS = _bigs()
POOL_SCALE = Big("pool_scale", "pool_scale", None, 1, 1, 1, D // N_CHIPS, 1)
BIG_SOURCES = ("w_gu", "w_down", "w_ple_gate", "w_ple_proj", "w_q", "w_o", "w_kv", "pool_w")


def _ncb(t):
    return N_CHIPS // t.A


def _full_shape(t, rows=None):
    return (t.L, t.A, t.R if rows is None else rows, _ncb(t) * t.C)


def _slot_index(t, k):
    return k // _ncb(t), k % _ncb(t)


def _slot(ref, t, k, row0, rows):
    a, cb = _slot_index(t, k)
    return ref.at[:, a, pl.ds(row0, rows), pl.ds(pl.multiple_of(cb * t.C, 128), t.C)]


def _place_job(t, w, out_dtype=BF):
    nb = next((nb for nb in (8, 4, 2, 1) if t.R % (16 * nb) == 0), 1) if t.L == 1 else 1
    rb = t.R // nb

    def fn(j, kc_ref, ins, outs):
        outs[0][...] = ins[0][...].astype(out_dtype)

    def in_map(j, kc_ref):
        return (j // nb if t.layer is None else t.layer, j % nb, 0)

    def out_map(j, kc_ref):
        a, cb = _slot_index(t, kc_ref[0])
        return (j // nb, a, j % nb, cb)

    return Job(t.L * nb, [(w, (None, rb, t.C), in_map)],
               [(_sds(_full_shape(t), out_dtype), (None, None, rb, t.C), out_map)], fn)


def _mesh_position():
    x, y, c = lax.axis_index("x"), lax.axis_index("y"), lax.axis_index("c")
    chips = [(1 - x, y), (x, 1 - y), (1 - x, 1 - y)]
    return x, y, c, chips


DIRECT_BELOW = 1024


def _gather_rider(parts, fulls):
    nt = len(parts)
    TO_X, TO_Y, FWD_X, FWD_Y, SIB_X, SIB_Y, SIB_D = range(7)

    def rows_of(ti, core):
        t, r0, r1 = parts[ti]
        h = (r1 - r0) // 2
        return r0 + core * h, h

    def copy(outs, sems, kind, ti, k_src, row0, rows, dev):
        region = _slot(outs[ti], parts[ti][0], k_src, row0, rows)
        return pltpu.make_async_remote_copy(region, region, sems[0].at[ti, kind], sems[1].at[ti, kind],
                                            device_id=dev, device_id_type=MESH)

    def plan(outs, sems):
        x, y, c, _ = _mesh_position()
        me, kx, ky, kd = 2 * x + y, 2 * (1 - x) + y, 2 * x + (1 - y), 2 * (1 - x) + (1 - y)
        dev_x, dev_y, dev_d, sib = (1 - x, y, c), (x, 1 - y, c), (1 - x, 1 - y, c), (x, y, 1 - c)

        def whole(ti):
            return 0, parts[ti][0].R

        def mk(kind, k_send, k_recv, dev, send_rows, recv_rows):
            def build(ti, side):
                k_src = k_send if side == "s" else k_recv
                row0, rows = (send_rows if side == "s" else recv_rows)(ti)
                return copy(outs, sems, kind, ti, k_src, row0, rows, dev)
            return build

        def first_half(core):
            return lambda ti: (rows_of(ti, core)[0], rows_of(ti, core)[1] // 2)

        def second_half(core):
            return lambda ti: (rows_of(ti, core)[0] + rows_of(ti, core)[1] // 2, rows_of(ti, core)[1] // 2)

        mine = lambda ti: rows_of(ti, c)
        theirs = lambda ti: rows_of(ti, 1 - c)
        split = {
            TO_X: mk(TO_X, me, kx, dev_x, mine, mine),
            TO_Y: mk(TO_Y, me, ky, dev_y, mine, mine),
            FWD_X: mk(FWD_X, ky, kd, dev_x, first_half(c), first_half(c)),
            FWD_Y: mk(FWD_Y, kx, kd, dev_y, second_half(c), second_half(c)),
            SIB_X: mk(SIB_X, kx, kx, sib, mine, theirs),
            SIB_Y: mk(SIB_Y, ky, ky, sib, mine, theirs),
            SIB_D: mk(SIB_D, kd, kd, sib, mine, theirs),
        }
        direct = {
            TO_X: mk(TO_X, me, kx, dev_x, whole, whole),
            TO_Y: mk(TO_Y, me, ky, dev_y, whole, whole),
            FWD_X: mk(FWD_X, me, kd, dev_d, whole, whole),
        }
        return split, direct

    is_split = [t.L * t.R * t.C >= DIRECT_BELOW for t, _, _ in parts]
    assert all(s or (r0, r1) == (0, t.R) for s, (t, r0, r1) in zip(is_split, parts))

    def start(ins, outs, sems):
        split, direct = plan(outs, sems)
        for ti in range(nt):
            kinds = split if is_split[ti] else direct
            kinds[TO_X](ti, "s").start()
            kinds[TO_Y](ti, "s").start()
            if not is_split[ti]:
                kinds[FWD_X](ti, "s").start()

    def mid(ins, outs, sems):
        split, _ = plan(outs, sems)
        for ti in range(nt):
            if is_split[ti]:
                split[TO_Y](ti, "r").wait_recv()
                split[FWD_X](ti, "s").start()
                split[SIB_Y](ti, "s").start()
        for ti in range(nt):
            if is_split[ti]:
                split[TO_X](ti, "r").wait_recv()
                split[FWD_Y](ti, "s").start()
                split[SIB_X](ti, "s").start()

    def finish(ins, outs, sems):
        split, direct = plan(outs, sems)
        for ti in range(nt):
            if is_split[ti]:
                split[FWD_X](ti, "r").wait_recv()
                split[FWD_Y](ti, "r").wait_recv()
                split[SIB_D](ti, "s").start()
            else:
                for kind in (TO_X, TO_Y, FWD_X):
                    direct[kind](ti, "r").wait_recv()
        for ti in range(nt):
            if is_split[ti]:
                for kind in (SIB_X, SIB_Y, SIB_D):
                    split[kind](ti, "r").wait_recv()
        for ti in range(nt):
            kinds = split if is_split[ti] else direct
            for kind in kinds:
                kinds[kind](ti, "s").wait_send()

    sems = pltpu.SemaphoreType.DMA((nt, 7))
    return Rider(list(fulls), [_sds(a.shape, a.dtype) for a in fulls], {i: i for i in range(nt)},
                 [sems, sems], start, mid, finish)


def _pair_exchange_rider(specs, grads):
    nt = len(specs)

    def copy(ins, outs, sems, ti, c, sibling):
        half = specs[ti].R // 2
        return pltpu.make_async_remote_copy(ins[ti].at[:, :, pl.ds((1 - c) * half, half), :], outs[ti],
                                            sems[0].at[ti], sems[1].at[ti], device_id=sibling, device_id_type=MESH)

    def start(ins, outs, sems):
        x, y, c, _ = _mesh_position()
        for ti in range(nt):
            copy(ins, outs, sems, ti, c, (x, y, 1 - c)).start()

    def finish(ins, outs, sems):
        x, y, c, _ = _mesh_position()
        for ti in range(nt):
            copy(ins, outs, sems, ti, c, (x, y, 1 - c)).wait()

    sems = pltpu.SemaphoreType.DMA((nt,))
    return Rider(list(grads), [_sds(_full_shape(t, t.R // 2), BF) for t in specs], {}, [sems, sems], start, None, finish)


def _pair_sum_job(t, g, land):
    assert t.L == 1
    half = t.R // 2
    nj = half // t.rb
    block = (None, t.A, t.rb, _ncb(t) * t.C)

    def fn(j, kc_ref, ins, outs):
        outs[0][...] = (ins[0][...].astype(F32) + ins[1][...].astype(F32)).astype(BF)

    return Job(nj,
               [(g, block, lambda j, kc_ref: (0, 0, kc_ref[1] * nj + j, 0)),
                (land, block, lambda j, kc_ref: (0, 0, j, 0))],
               [(_sds(_full_shape(t, half), BF), block, lambda j, kc_ref: (0, 0, j, 0))], fn)


def _scatter_rider(specs, sums):
    nt = len(specs)

    def copy(ins, outs, sems, ti, j, chip, c):
        t = specs[ti]
        cx, cy = chip
        return pltpu.make_async_remote_copy(_slot(ins[ti], t, 2 * cx + cy, 0, t.R // 2), outs[ti].at[j],
                                            sems[0].at[ti, j], sems[1].at[ti, j],
                                            device_id=(cx, cy, c), device_id_type=MESH)

    def start(ins, outs, sems):
        _, _, c, chips = _mesh_position()
        for j, chip in enumerate(chips):
            for ti in range(nt):
                copy(ins, outs, sems, ti, j, chip, c).start()

    def finish(ins, outs, sems):
        _, _, c, chips = _mesh_position()
        for j, chip in enumerate(chips):
            for ti in range(nt):
                copy(ins, outs, sems, ti, j, chip, c).wait()

    sems = pltpu.SemaphoreType.DMA((nt, N_CHIPS - 1))
    return Rider(list(sums), [_sds((N_CHIPS - 1, t.L, t.R // 2, t.C), BF) for t in specs], {}, [sems, sems],
                 start, None, finish)


def _chip_sum_job(ts, landed):
    t0 = ts[0]
    assert t0.L == 1
    half = t0.R // 2
    nj = half // t0.rb

    def local(j, li):
        return jnp.clip(j - li * nj, 0, nj - 1)

    ins = []
    for li, t in enumerate(ts):
        s, land = landed[t.name]

        def own_map(j, kc_ref, li=li, t=t):
            a, cb = _slot_index(t, kc_ref[0])
            return (0, a, local(j, li), cb)

        ins.append((s, (None, None, t.rb, t.C), own_map))
        ins.append((land, (N_CHIPS - 1, None, t.rb, t.C), lambda j, kc_ref, li=li: (0, 0, local(j, li), 0)))

    def fn(j, kc_ref, in_refs, outs):
        for li in range(len(ts)):
            @pl.when(j // nj == li)
            def _():
                acc = in_refs[2 * li][...].astype(F32)
                for k in range(N_CHIPS - 1):
                    acc = acc + in_refs[2 * li + 1][k].astype(F32)
                outs[0][...] = acc

    return Job(len(ts) * nj, ins,
               [(_sds((len(ts), t0.R, t0.C)), (None, t0.rb, t0.C),
                 lambda j, kc_ref: (j // nj, kc_ref[1] * nj + j % nj, 0))], fn)


def _adamw_job(rb, w, g, m, v):
    n_layers, r, c = w.shape
    nb = r // rb
    block = (None, rb, c)
    index = lambda j, kc_ref: (j // nb, j % nb, 0)

    def fn(j, kc_ref, ins, outs):
        g_v = ins[1][...]
        outs[0][...] = g_v
        outs[1][...], outs[2][...], outs[3][...] = _adamw_math(ins[0][...], g_v, ins[2][...], ins[3][...])

    return Job(n_layers * nb, [(a, block, index) for a in (w, g, m, v)],
               [(_sds(w.shape), block, index)] * 4, fn)


def _chip_sum_fused_job(ts, fused, by_cols):
    t0 = ts[0]
    own0 = fused[t0.name][0]
    if by_cols:
        rows, cols = own0.shape
    else:
        nb, rows, bw = own0.shape
        cols = nb * bw
    nj = rows // t0.rb

    def local(j, li):
        return jnp.clip(j - li * nj, 0, nj - 1)

    ins = []
    for li, t in enumerate(ts):
        own, land = fused[t.name]
        if by_cols:
            ins.append((own, (t.rb, cols), lambda j, kc_ref, li=li: (local(j, li), 0)))
            ins.append((land, (N_CHIPS - 1, t.rb, cols), lambda j, kc_ref, li=li: (0, local(j, li), 0)))
        else:
            ins.append((own, (nb, t.rb, bw), lambda j, kc_ref, li=li: (0, local(j, li), 0)))
            ins.append((land, (N_CHIPS - 1, nb, t.rb, bw), lambda j, kc_ref, li=li: (0, 0, local(j, li), 0)))

    def fn(j, kc_ref, in_refs, outs):
        for li in range(len(ts)):
            @pl.when(j // nj == li)
            def _():
                acc = in_refs[2 * li][...].astype(F32)
                for k in range(N_CHIPS - 1):
                    acc = acc + in_refs[2 * li + 1][k].astype(F32)
                outs[0][...] = acc if by_cols else jnp.concatenate([acc[b] for b in range(nb)], axis=1)

    def out_map(j, kc_ref):
        return (j // nj, j % nj, kc_ref[1]) if by_cols else (j // nj, kc_ref[1] * nj + j % nj, 0)

    return Job(len(ts) * nj, ins, [(_sds((len(ts), t0.R, t0.C)), (None, t0.rb, cols), out_map)], fn)


def _share_rider(halves, by_cols):
    nt = len(halves)

    def copy(outs, sems, ti, core, sibling):
        axis = 2 if by_cols[ti] else 1
        half = halves[ti].shape[axis] // 2
        piece = pl.ds(pl.multiple_of(core * half, 128 if by_cols[ti] else 8), half)
        part = outs[ti].at[:, :, piece] if by_cols[ti] else outs[ti].at[:, piece, :]
        return pltpu.make_async_remote_copy(part, part, sems[0].at[ti], sems[1].at[ti],
                                            device_id=sibling, device_id_type=MESH)

    def start(ins, outs, sems):
        x, y, c, _ = _mesh_position()
        for ti in range(nt):
            copy(outs, sems, ti, c, (x, y, 1 - c)).start()

    def finish(ins, outs, sems):
        x, y, c, _ = _mesh_position()
        for ti in range(nt):
            copy(outs, sems, ti, 1 - c, (x, y, 1 - c)).wait_recv()
        for ti in range(nt):
            copy(outs, sems, ti, c, (x, y, 1 - c)).wait_send()

    sems = pltpu.SemaphoreType.DMA((nt,))
    return Rider(list(halves), [_sds(a.shape, a.dtype) for a in halves], {i: i for i in range(nt)}, [sems, sems],
                 start, None, finish)


def _both(r1, r2):
    assert r1.mid is None and r2.mid is None
    ni, no, ns = len(r1.arrays), len(r1.out_shapes), len(r1.scratch)

    def split(fn1, fn2):
        def run(ins, outs, scr):
            fn1(ins[:ni], outs[:no], scr[:ns])
            fn2(ins[ni:], outs[no:], scr[ns:])
        return run

    aliases = dict(r1.aliases)
    aliases.update({ni + a: no + b for a, b in r2.aliases.items()})
    return Rider(r1.arrays + r2.arrays, r1.out_shapes + r2.out_shapes, aliases, r1.scratch + r2.scratch,
                 split(r1.start, r2.start), None, split(r1.finish, r2.finish))


def _adamw_math(w, g, m, v):
    m = B1 * m + (1.0 - B1) * g
    v = B2 * v + (1.0 - B2) * (g * g)
    delta = -LR * ((m / BC1) / (jnp.sqrt(v / BC2) + AEPS) + WD * w)
    return delta, m, v


def _adamw(name, rb, w, g, m, v):
    n_layers, r, c = w.shape

    def body(w_ref, g_ref, m_ref, v_ref, go_ref, d_ref, nm_ref, nv_ref):
        g_v = g_ref[...]
        go_ref[...] = g_v
        d_ref[...], nm_ref[...], nv_ref[...] = _adamw_math(w_ref[...], g_v, m_ref[...], v_ref[...])

    spec = pl.BlockSpec((None, rb, c), lambda l, j: (l, j, 0))
    return pl.pallas_call(
        body, name=f"adamw_{name}", grid=(n_layers, r // rb),
        in_specs=[spec] * 4, out_specs=[spec] * 4, out_shape=[_sds(w.shape)] * 4,
        compiler_params=_params(2),
    )(w, g, m, v)


GAIN_ROWS = {"pre_mix_g": 0, "post_mix_g": 2, "pre_ffn_g": 4, "post_ffn_g": 6, "ple_g": 8, "ple_post_g": 10}
ROW_KV_G, ROW_POOL_SCALE, ROW_SINKS, ROW_LOSS, PACK_ROWS = 12, 13, 14, 15, 16
SMALL_NAMES = tuple(GAIN_ROWS) + ("kv_g", "pool_scale", "sinks")


def _small_all_reduce(rows, dpool, rider=None):
    ng, pr = len(WINDOWS), POOL_G // N_CHIPS

    def body(*refs):
        row_refs = refs[:PACK_ROWS]
        dpool_ref, tot_ref, gpool_ref, pack, land, pland, send, recv, psend, precv = refs[PACK_ROWS:]
        x, y, c, _ = _mesh_position()
        me = 4 * x + 2 * y + c
        for r in range(PACK_ROWS):
            pack[r:r + 1, :] = row_refs[r][...]

        def shard_of(k):
            return dpool_ref.at[:, pl.ds(pl.multiple_of(k * pr, pr), pr), :]

        cps = []
        for j in range(1, N_DEV):
            px, py, pc = x ^ (j >> 2), y ^ ((j >> 1) & 1), c ^ (j & 1)
            cps.append(pltpu.make_async_remote_copy(pack, land.at[me], send.at[j], recv.at[j],
                                                    device_id=(px, py, pc), device_id_type=MESH))
            cps.append(pltpu.make_async_remote_copy(shard_of(2 * px + py), pland.at[me], psend.at[j], precv.at[j],
                                                    device_id=(px, py, pc), device_id_type=MESH))
        for cp in cps:
            cp.start()
        land[me] = pack[...]
        pland[me] = dpool_ref[:, pl.ds(pl.multiple_of((2 * x + y) * pr, pr), pr), :]
        for j in range(1, N_DEV):
            pltpu.make_async_remote_copy(pack, land.at[me ^ j], send.at[j], recv.at[j],
                                         device_id=(x, y, c), device_id_type=MESH).wait_recv()
            pltpu.make_async_remote_copy(shard_of(0), pland.at[me ^ j], psend.at[j], precv.at[j],
                                         device_id=(x, y, c), device_id_type=MESH).wait_recv()
        for cp in cps:
            cp.wait_send()
        tot = land[0]
        gp = pland[0].astype(F32)
        for d in range(1, N_DEV):
            tot = tot + land[d]
            gp = gp + pland[d].astype(F32)
        tot_ref[...] = tot
        gpool_ref[...] = gp

    sems = pltpu.SemaphoreType.DMA((N_DEV,))
    return _call(
        body, name="small_all_reduce", grid=(1,),
        in_specs=[VSPEC] * (PACK_ROWS + 1), out_specs=[VSPEC, VSPEC],
        out_shape=[_sds((PACK_ROWS, D)), _sds((ng, pr, POOL_G))],
        scratch_shapes=[pltpu.VMEM((PACK_ROWS, D), F32), pltpu.VMEM((N_DEV, PACK_ROWS, D), F32),
                        pltpu.VMEM((N_DEV, ng, pr, POOL_G), BF), sems, sems, sems, sems],
        args=[*rows, dpool], rider=rider)


def _small_adamw(tot, kc, small_w, small_m, small_v):
    names = SMALL_NAMES
    n = len(names)

    def body(*refs):
        tot_ref, kc_ref = refs[0], refs[1]
        w_refs = dict(zip(names, refs[2:2 + n]))
        m_refs = dict(zip(names, refs[2 + n:2 + 2 * n]))
        v_refs = dict(zip(names, refs[2 + 2 * n:2 + 3 * n]))
        loss_ref = refs[2 + 3 * n]
        out_refs = {nm: refs[3 + 3 * n + 4 * k: 7 + 3 * n + 4 * k] for k, nm in enumerate(names)}
        tot = tot_ref[...]
        loss_ref[...] = 0.5 * jnp.sum(tot[ROW_LOSS:ROW_LOSS + 1, :], axis=-1, keepdims=True) * (1.0 / D)

        def update(nm, g):
            g_ref, d_ref, nm_ref, nv_ref = out_refs[nm]
            g_ref[...] = g
            d_ref[...], nm_ref[...], nv_ref[...] = _adamw_math(w_refs[nm][...], g, m_refs[nm][...], v_refs[nm][...])

        for nm, r in GAIN_ROWS.items():
            update(nm, tot[r:r + 2, :])
        update("kv_g", tot[ROW_KV_G:ROW_KV_G + 1, :])
        k = kc_ref[0]
        width = D // N_CHIPS
        g_scale = jnp.zeros((1, width), F32)
        for kk in range(N_CHIPS):
            g_scale = g_scale + jnp.where(k == kk, tot[ROW_POOL_SCALE:ROW_POOL_SCALE + 1, kk * width:(kk + 1) * width], 0.0)
        update("pool_scale", g_scale)
        update("sinks", tot[ROW_SINKS:ROW_SINKS + 1, 0:N_HEADS])

    ins = [tot, kc] + [small_w[nm] for nm in names] + [small_m[nm] for nm in names] + [small_v[nm] for nm in names]
    out_shape = [_sds((1, 1))]
    for nm in names:
        out_shape += [_sds(small_w[nm].shape)] * 4
    outs = pl.pallas_call(
        body, name="small_adamw",
        in_specs=[VSPEC, SSPEC] + [VSPEC] * (3 * n), out_specs=[VSPEC] * len(out_shape), out_shape=out_shape,
        compiler_params=_params(),
    )(*ins)
    return outs[0], {nm: outs[1 + 4 * k: 5 + 4 * k] for k, nm in enumerate(names)}


def _compute_layout(t, full):
    if t.src == "w_gu":
        return full.reshape(2, D, FF)
    if t.src == "pool_w":
        return full.reshape(len(WINDOWS), POOL_G, POOL_G)
    if t.src == "pool_scale":
        return full.reshape(1, D)
    return full.reshape(t.A * t.R, _ncb(t) * t.C)


def kernel(x, p, pre_mix_g, post_mix_g, pre_ffn_g, post_ffn_g, pool_w, pool_scale, kv_g, w_kv, w_q, sinks, w_o, w_gu, w_down, ple_g, w_ple_gate, w_ple_proj, ple_post_g, loss_target, m_pre_mix_g, m_post_mix_g, m_pre_ffn_g, m_post_ffn_g, m_pool_w, m_pool_scale, m_kv_g, m_w_kv, m_w_q, m_sinks, m_w_o, m_w_gu, m_w_down, m_ple_g, m_w_ple_gate, m_w_ple_proj, m_ple_post_g, v_pre_mix_g, v_post_mix_g, v_pre_ffn_g, v_post_ffn_g, v_pool_w, v_pool_scale, v_kv_g, v_w_kv, v_w_q, v_sinks, v_w_o, v_w_gu, v_w_down, v_ple_g, v_w_ple_gate, v_w_ple_proj, v_ple_post_g):
    weights = dict(pre_mix_g=pre_mix_g, post_mix_g=post_mix_g, pre_ffn_g=pre_ffn_g, post_ffn_g=post_ffn_g,
                   pool_w=pool_w, pool_scale=pool_scale, kv_g=kv_g, w_kv=w_kv, w_q=w_q, sinks=sinks, w_o=w_o,
                   w_gu=w_gu, w_down=w_down, ple_g=ple_g, w_ple_gate=w_ple_gate, w_ple_proj=w_ple_proj,
                   ple_post_g=ple_post_g)
    m_in = dict(pre_mix_g=m_pre_mix_g, post_mix_g=m_post_mix_g, pre_ffn_g=m_pre_ffn_g, post_ffn_g=m_post_ffn_g,
                pool_w=m_pool_w, pool_scale=m_pool_scale, kv_g=m_kv_g, w_kv=m_w_kv, w_q=m_w_q, sinks=m_sinks,
                w_o=m_w_o, w_gu=m_w_gu, w_down=m_w_down, ple_g=m_ple_g, w_ple_gate=m_w_ple_gate,
                w_ple_proj=m_w_ple_proj, ple_post_g=m_ple_post_g)
    v_in = dict(pre_mix_g=v_pre_mix_g, post_mix_g=v_post_mix_g, pre_ffn_g=v_pre_ffn_g, post_ffn_g=v_post_ffn_g,
                pool_w=v_pool_w, pool_scale=v_pool_scale, kv_g=v_kv_g, w_kv=v_w_kv, w_q=v_w_q, sinks=v_sinks,
                w_o=v_w_o, w_gu=v_w_gu, w_down=v_w_down, ple_g=v_ple_g, w_ple_gate=v_w_ple_gate,
                w_ple_proj=v_w_ple_proj, ple_post_g=v_ple_post_g)
    order = ["pre_mix_g", "post_mix_g", "pre_ffn_g", "post_ffn_g", "pool_w", "pool_scale", "kv_g", "w_kv", "w_q",
             "sinks", "w_o", "w_gu", "w_down", "ple_g", "w_ple_gate", "w_ple_proj", "ple_post_g"]

    kc = jnp.stack([2 * lax.axis_index("x") + lax.axis_index("y"), lax.axis_index("c")]).astype(jnp.int32)
    s_len = x.shape[1]
    x2d = x.reshape(s_len, D)
    p3d = p.reshape(2, s_len, PLE)
    target = loss_target.reshape(s_len, D)
    kv_g2d = kv_g.reshape(1, D)
    gains = {nm: weights[nm] for nm in GAIN_ROWS}

    def shard_view(src, a):
        t = next(t for t in BIGS.values() if t.src == src)
        return a.reshape(-1, t.R, t.C)

    first, second = ["pool_w", "pool_scale"], ["w_gu0", "w_down0"]
    rest = [nm for nm in BIGS if nm not in first + second]
    specs = dict(BIGS, pool_scale=POOL_SCALE)
    placed = {}

    def place_job(nm):
        if nm == "pool_scale":
            return _place_job(POOL_SCALE, pool_scale.reshape(1, 1, D // N_CHIPS), F32)
        return _place_job(BIGS[nm], shard_view(BIGS[nm].src, weights[BIGS[nm].src]))

    def gather(names, rows=None):
        rows = rows or {}
        parts = [(specs[nm],) + tuple(rows.get(nm, (0, specs[nm].R))) for nm in names]
        return _gather_rider(parts, [placed[nm] for nm in names])

    def take(names, results):
        for nm, a in zip(names, results):
            placed[nm] = a

    def weight(nm):
        return _compute_layout(specs[nm], placed[nm])

    take(first, [r[0] for r in _multi_call("place_pool", [place_job(nm) for nm in first], kc)])
    cast, got = _multi_call("place_ffn0", [place_job(nm) for nm in second], kc, rider=gather(first))
    take(second, [r[0] for r in cast])
    take(first, got)
    jobs = [place_job(nm) for nm in rest]
    jobs.append(_mixa_fwd_job(x2d, gains["pre_mix_g"], weight("pool_w"), weight("pool_scale"), gains["post_mix_g"]))
    results, got = _multi_call("cast_and_mixa_fwd", jobs, kc, rider=gather(second))
    take(rest, [r[0] for r in results[:-1]])
    take(second, got)
    y0, x1 = results[-1]

    ride = ["w_ple_gate0", "w_ple_proj0", "w_q", "w_kv", "w_o", "w_gu1"]
    (f0, x2, g0, u0), got = _ffn_fwd(0, x1, gains["pre_ffn_g"], weight("w_gu0"), weight("w_down0"), gains["post_ffn_g"],
                             rider=gather(ride, {"w_gu1": (0, 320)}))
    take(ride, got)

    ride = ["w_ple_gate1", "w_ple_proj1", "w_gu1"]
    (z0, pe0, x3, q, kv), got = _ple_fwd(
        0, x2, p3d, gains["ple_g"], weight("w_ple_gate0"), weight("w_ple_proj0"), gains["ple_post_g"],
        qkv=(gains["pre_mix_g"], kv_g2d, weight("w_q"), weight("w_kv")),
        rider=gather(ride, {"w_gu1": (320, 704)}))
    take(ride, got)

    ride = ["w_down1", "w_gu1"]
    (attn, y1, x4), got = _attn_fwd(q, kv, sinks, x3, weight("w_o"), gains["post_mix_g"],
                                    rider=gather(ride, {"w_gu1": (704, D)}))
    take(ride, got)

    (f1, x5, g1, u1), _ = _ffn_fwd(1, x4, gains["pre_ffn_g"], weight("w_gu1"), weight("w_down1"), gains["post_ffn_g"])
    (z1, pe1, dx6, loss_row), _ = _ple_fwd(1, x5, p3d, gains["ple_g"], weight("w_ple_gate1"), weight("w_ple_proj1"),
                                           gains["ple_post_g"], target=target)

    local = {}
    landed = {}
    fused = {}

    def local_grads(names):
        return [local[nm].reshape(_full_shape(BIGS[nm])) for nm in names]

    def pair_exchange(names):
        return _pair_exchange_rider([BIGS[nm] for nm in names], local_grads(names))

    def pair_sum(tag, names, lands):
        jobs = [_pair_sum_job(BIGS[nm], g, l) for nm, g, l in zip(names, local_grads(names), lands)]
        return [r[0] for r in _multi_call(f"pair_sum_{tag}", jobs, kc)]

    def scatter(names, sums):
        return _scatter_rider([BIGS[nm] for nm in names], sums)

    def keep(names, sums, got):
        for nm, s, l in zip(names, sums, got):
            landed[nm] = (s, l)

    (dx5, local["w_ple_gate1"], local["w_ple_proj1"], d_ple1, d_plepost1), _ = _ple_bwd(
        1, dx6, x5, z1, pe1, p3d, gains["ple_g"], weight("w_ple_gate1"), gains["ple_post_g"])

    group_a = ["w_ple_gate1", "w_ple_proj1"]
    (dx4, d_preffn1, d_postffn1, *scattered), lands_a = _ffn_bwd(
        1, dx5, x4, f1, g1, u1, gains["pre_ffn_g"], weight("w_gu1"), weight("w_down1"), gains["post_ffn_g"], kc,
        rider=pair_exchange(group_a))
    fused["w_gu1"], fused["w_down1"] = scattered[0:2], scattered[2:4]
    sums_a = pair_sum("a", group_a, lands_a)

    (dq, dkv, local["w_o"], d_postmix1, d_sinks), got = _attn_bwd(
        dx4, y1, attn, q, kv, sinks, weight("w_o"), gains["post_mix_g"], rider=scatter(group_a, sums_a))
    keep(group_a, sums_a, got)
    dx3, local["w_q"], local["w_kv"], d_premix1, d_kvg = _qkv_bwd(
        dq, dkv, x3, dx4, gains["pre_mix_g"], kv_g2d, weight("w_q"), weight("w_kv"))

    group_b = ["w_o", "w_q", "w_kv"]
    (dx2, local["w_ple_gate0"], local["w_ple_proj0"], d_ple0, d_plepost0), lands_b = _ple_bwd(
        0, dx3, x2, z0, pe0, p3d, gains["ple_g"], weight("w_ple_gate0"), gains["ple_post_g"],
        rider=pair_exchange(group_b))
    sums_b = pair_sum("b", group_b, lands_b)

    group_c = ["w_ple_gate0", "w_ple_proj0"]
    (dx1, d_preffn0, d_postffn0, *scattered), got = _ffn_bwd(
        0, dx2, x1, f0, g0, u0, gains["pre_ffn_g"], weight("w_gu0"), weight("w_down0"), gains["post_ffn_g"], kc,
        rider=_both(pair_exchange(group_c), scatter(group_b, sums_b)))
    fused["w_gu0"], fused["w_down0"] = scattered[0:2], scattered[2:4]
    sums_c = pair_sum("c", group_c, got[:len(group_c)])
    keep(group_b, sums_b, got[len(group_c):])

    layers_of = lambda src: [t for t in BIGS.values() if t.src == src]
    own_scatter = ["w_gu", "w_down"]
    early = own_scatter + ["w_q", "w_o", "w_kv"]
    late = ["w_ple_gate", "w_ple_proj"]
    by_cols = lambda srcs: [src == "w_down" for src in srcs]
    jobs = [_chip_sum_fused_job(layers_of(src), fused, by_cols=src == "w_down") for src in own_scatter]
    jobs += [_chip_sum_job(layers_of(src), landed) for src in early if src not in own_scatter]
    halves = [r[0] for r in _multi_call("chip_sum_early", jobs, kc)]
    (dx0, d_pool, d_scale, d_postmix0, d_premix0), got = _mixa_bwd(
        dx1, x2d, y0, gains["pre_mix_g"], weight("pool_w"), weight("pool_scale"), gains["post_mix_g"],
        rider=_both(scatter(group_c, sums_c), _share_rider(halves, by_cols(early))))
    keep(group_c, sums_c, got[:len(group_c)])
    full_grads = dict(zip(early, got[len(group_c):]))

    rows = [d_premix0, d_premix1, d_postmix0, d_postmix1, d_preffn0, d_preffn1, d_postffn0, d_postffn1,
            d_ple0, d_ple1, d_plepost0, d_plepost1, d_kvg, d_scale, d_sinks, loss_row]
    as2d = lambda a: a.reshape(1, D) if a.ndim == 1 else a
    (tot, g_pool), _ = _small_all_reduce(rows, d_pool)
    loss, small = _small_adamw(tot, kc, {nm: as2d(weights[nm]) for nm in SMALL_NAMES},
                               {nm: as2d(m_in[nm]) for nm in SMALL_NAMES},
                               {nm: as2d(v_in[nm]) for nm in SMALL_NAMES})

    halves = [r[0] for r in _multi_call("chip_sum_late", [_chip_sum_job(layers_of(src), landed) for src in late], kc)]
    full_grads.update(zip(late, _run("grads_pair_share", _share_rider(halves, by_cols(late)))))
    full_grads["pool_w"] = g_pool
    others = [src for src in BIG_SOURCES if src not in own_scatter and src != "pool_w"]

    def adam_args(src):
        return (layers_of(src)[0].rb, shard_view(src, weights[src]), full_grads[src],
                shard_view(src, m_in[src]), shard_view(src, v_in[src]))

    out = {"grad": {}, "delta": {}, "new_m": {}, "new_v": {}}
    results = {src: _adamw(src, *adam_args(src)) for src in own_scatter}
    rest_srcs = others + ["pool_w"]
    results.update(zip(rest_srcs, _multi_call("adamw_rest", [_adamw_job(*adam_args(src)) for src in rest_srcs], kc)))
    for src in BIG_SOURCES:
        shape = weights[src].shape
        for kind, a in zip(("grad", "delta", "new_m", "new_v"), results[src]):
            out[kind][src] = a.reshape(shape)
    for nm in SMALL_NAMES:
        shape = weights[nm].shape
        for kind, a in zip(("grad", "delta", "new_m", "new_v"), small[nm]):
            out[kind][nm] = a.reshape(shape)

    return (loss.reshape(()), dx0.reshape(x.shape),
            *[out["grad"][nm] for nm in order], *[out["delta"][nm] for nm in order],
            *[out["new_m"][nm] for nm in order], *[out["new_v"][nm] for nm in order])
```

```python
import collections

import jax
import jax.numpy as jnp
from jax import lax
from jax.experimental import pallas as pl
from jax.experimental.pallas import tpu as pltpu

D = 1024
FF = 2816
N_HEADS = 16
HEAD_DIM = 64
N_KV_HEADS = 4
GQA = N_HEADS // N_KV_HEADS
KVD = N_KV_HEADS * HEAD_DIM
PLE = 256
BLK = 128
WINDOWS = (2, 4, 8, 16)
POOL_G = 256
HALO = 16
EPS = 1e-6
NEG_INF = -1e30
ATT_SCALE = HEAD_DIM ** -0.5
SLOPES = tuple(2.0 ** (-8.0 * (h + 1) / N_HEADS) for h in range(N_HEADS))
N_CHIPS = 4
N_DEV = 8

LR, B1, B2, AEPS, WD, STEP = 0.001, 0.9, 0.999, 1e-08, 0.01, 10
BC1 = 1.0 - B1 ** STEP
BC2 = 1.0 - B2 ** STEP

BF = jnp.bfloat16
F32 = jnp.float32
MESH = pl.DeviceIdType.MESH
VMEM_LIMIT_V7X = 58 * 1024 * 1024
TM = 256
TM_FFN_BWD = 512
FF_CHUNK = 256
FF_HALF = FF // 2

VSPEC = pl.BlockSpec(memory_space=pltpu.VMEM)
SSPEC = pl.BlockSpec(memory_space=pltpu.SMEM)
ANYSPEC = pl.BlockSpec(memory_space=pl.ANY)


def _params(n_grid=0):
    sem = ("arbitrary",) * n_grid if n_grid else None
    return pltpu.CompilerParams(dimension_semantics=sem, vmem_limit_bytes=VMEM_LIMIT_V7X)


def _sds(shape, dtype=F32):
    return jax.ShapeDtypeStruct(tuple(shape), dtype)


Rider = collections.namedtuple("Rider", "arrays out_shapes aliases scratch start mid finish")
MID_NUM, MID_DEN = 5, 8


def _call(body, *, name, grid, in_specs, out_specs, out_shape, args, scratch_shapes=(), rider=None, prefetch=None):
    ni, no, ns = len(in_specs), len(out_specs), len(scratch_shapes)
    npre = 0 if prefetch is None else 1
    pre = [] if prefetch is None else [prefetch]
    if rider is None:
        rider = Rider([], [], {}, [], None, None, None)
    ri, ro = len(rider.arrays), len(rider.out_shapes)

    def full(*refs):
        pre_refs, refs = refs[:npre], refs[npre:]
        ins, refs = refs[:ni], refs[ni:]
        rins, refs = refs[:ri], refs[ri:]
        outs, refs = refs[:no], refs[no:]
        routs, refs = refs[:ro], refs[ro:]
        scr, rscr = refs[:ns], refs[ns:]
        ids = [pl.program_id(a) for a in range(len(grid))]
        first = ids[0] == 0
        last = ids[0] == grid[0] - 1
        for a in range(1, len(grid)):
            first = first & (ids[a] == 0)
            last = last & (ids[a] == grid[a] - 1)

        if rider.start is not None:
            @pl.when(first)
            def _():
                rider.start(rins, routs, rscr)

        if rider.mid is not None:
            assert len(grid) == 1

            @pl.when(ids[0] == (grid[0] * MID_NUM) // MID_DEN)
            def _():
                rider.mid(rins, routs, rscr)

        body(*pre_refs, *ins, *outs, *scr)

        if rider.finish is not None:
            @pl.when(last)
            def _():
                rider.finish(rins, routs, rscr)

    outs = pl.pallas_call(
        full, name=name,
        grid_spec=pltpu.PrefetchScalarGridSpec(
            num_scalar_prefetch=npre, grid=grid,
            in_specs=list(in_specs) + [ANYSPEC] * ri, out_specs=list(out_specs) + [ANYSPEC] * ro,
            scratch_shapes=list(scratch_shapes) + list(rider.scratch)),
        out_shape=list(out_shape) + list(rider.out_shapes),
        input_output_aliases={npre + ni + a: no + b for a, b in rider.aliases.items()},
        compiler_params=_params(len(grid)))(*pre, *args, *rider.arrays)
    return list(outs[:no]), list(outs[no:])


def _run(name, rider):
    ri = len(rider.arrays)

    def body(*refs):
        rins, routs, rscr = refs[:ri], refs[ri:ri + len(rider.out_shapes)], refs[ri + len(rider.out_shapes):]
        rider.start(rins, routs, rscr)
        if rider.mid is not None:
            rider.mid(rins, routs, rscr)
        rider.finish(rins, routs, rscr)

    return pl.pallas_call(
        body, name=name, in_specs=[ANYSPEC] * ri, out_specs=[ANYSPEC] * len(rider.out_shapes),
        out_shape=list(rider.out_shapes), scratch_shapes=list(rider.scratch),
        input_output_aliases=dict(rider.aliases), compiler_params=_params())(*rider.arrays)


Job = collections.namedtuple("Job", "steps ins outs fn")


def _multi_call(name, jobs, kc, rider=None):
    n = max(job.steps for job in jobs)

    def clamped(index, steps):
        return lambda s, kc_ref: index(jnp.minimum(s, steps - 1), kc_ref)

    in_specs, out_specs, out_shape, args = [], [], [], []
    for job in jobs:
        for arr, block, index, *single in job.ins:
            mode = dict(pipeline_mode=pl.Buffered(1)) if single and single[0] else {}
            in_specs.append(pl.BlockSpec(block, clamped(index, job.steps), **mode))
            args.append(arr)
        for sds, block, index in job.outs:
            out_specs.append(pl.BlockSpec(block, clamped(index, job.steps)))
            out_shape.append(sds)
    n_in = len(args)

    def body(kc_ref, *refs):
        s = pl.program_id(0)
        i0, o0 = 0, n_in
        for job in jobs:
            ins, outs = refs[i0:i0 + len(job.ins)], refs[o0:o0 + len(job.outs)]
            i0, o0 = i0 + len(job.ins), o0 + len(job.outs)

            @pl.when(s < job.steps)
            def _():
                job.fn(s, kc_ref, ins, outs)

    outs, routs = _call(body, name=name, grid=(n,), in_specs=in_specs, out_specs=out_specs, out_shape=out_shape,
                        args=args, prefetch=kc, rider=rider)
    res, o0 = [], 0
    for job in jobs:
        res.append(outs[o0:o0 + len(job.outs)])
        o0 += len(job.outs)
    return res if rider is None else (res, routs)


def _rms_fwd(x, g):
    r = lax.rsqrt(jnp.mean(x * x, axis=-1, keepdims=True) + EPS)
    return x * r * g


def _rms_bwd(x, g, dy):
    r = lax.rsqrt(jnp.mean(x * x, axis=-1, keepdims=True) + EPS)
    xn = x * r
    dxn = dy * g
    dx = r * (dxn - xn * jnp.mean(dxn * xn, axis=-1, keepdims=True))
    return dx, dy * xn


def _rowsum(a):
    return jnp.sum(a, axis=0, keepdims=True)


def _sigmoid(z):
    return 1.0 / (1.0 + jnp.exp(-z))


def _dot(a, b):
    return jnp.dot(a, b, preferred_element_type=F32)


def _dot_nt(a, b):
    return lax.dot_general(a, b, (((1,), (1,)), ((), ())), preferred_element_type=F32)


def _dot_tn(a, b):
    return lax.dot_general(a, b, (((0,), (0,)), ((), ())), preferred_element_type=F32)


def _row_spec(tm, width=D):
    return pl.BlockSpec((tm, width), lambda i: (i, 0))


def _const_spec(shape):
    zeros = (0,) * len(shape)
    return pl.BlockSpec(tuple(shape), lambda *_: zeros)


def _pool_delta(he, pos):
    out = []
    for gi, w in enumerate(WINDOWS):
        hg = he[:, gi * POOL_G:(gi + 1) * POOL_G]
        s = hg
        k = 1
        while k < w:
            s = s + pltpu.roll(s, k, 0)
            k *= 2
        cnt = jnp.maximum(jnp.minimum(pos + 1, w), 1).astype(F32)
        out.append(s / cnt - hg)
    return out


def _load_with_halo_before(x_ref, i, tm):
    r0 = pl.multiple_of(i * tm, tm)
    hs = pl.multiple_of(jnp.maximum(i * tm - HALO, 0), 8)
    xh = jnp.where(i > 0, x_ref[pl.ds(hs, HALO), :], 0.0)
    xt = x_ref[pl.ds(r0, tm), :]
    return xt, jnp.concatenate([xh, xt], axis=0)


def _mixa_fwd_job(x, pre_g, pool_w, pool_scale, post_g):
    s_len = x.shape[0]

    def fn(i, kc_ref, ins, outs):
        x_ref, pg_ref, w_ref, sc_ref, qg_ref = ins
        y_ref, x1_ref = outs
        xt, xe = _load_with_halo_before(x_ref, i, TM)
        he = _rms_fwd(xe, pg_ref[0:1, :])
        pos = i * TM - HALO + lax.broadcasted_iota(jnp.int32, (TM + HALO, 1), 0)
        ds = _pool_delta(he, pos)
        ys = [_dot(ds[gi][HALO:, :].astype(BF), w_ref[gi]) for gi in range(len(WINDOWS))]
        y = jnp.concatenate(ys, axis=1) * sc_ref[...]
        y_ref[...] = y
        x1_ref[...] = xt + _rms_fwd(y, qg_ref[0:1, :])

    def whole(a):
        zeros = (0,) * a.ndim
        return (a, a.shape, lambda j, kc_ref: zeros, True)

    rows = lambda j, kc_ref: (j, 0)
    return Job(s_len // TM, [whole(a) for a in (x, pre_g, pool_w, pool_scale, post_g)],
               [(_sds((s_len, D)), (TM, D), rows), (_sds((s_len, D)), (TM, D), rows)], fn)


def _mixa_bwd(dx1, x, y, pre_g, pool_w, pool_scale, post_g, rider=None):
    s_len = x.shape[0]
    n = s_len // TM
    ng = len(WINDOWS)

    def body(dx_ref, x_ref, y_ref, pg_ref, w_ref, sc_ref, qg_ref,
             dx0_ref, dw_ref, dsc_ref, dqg_ref, dpg_ref, wacc):
        i = pl.program_id(0)

        @pl.when(i == 0)
        def _():
            wacc[...] = jnp.zeros_like(wacc)
            dsc_ref[...] = jnp.zeros_like(dsc_ref)
            dqg_ref[...] = jnp.zeros_like(dqg_ref)
            dpg_ref[...] = jnp.zeros_like(dpg_ref)

        r0 = pl.multiple_of(i * TM, TM)
        xt, xe = _load_with_halo_before(x_ref, i, TM)
        he = _rms_fwd(xe, pg_ref[0:1, :])
        pos_b = i * TM - HALO + lax.broadcasted_iota(jnp.int32, (TM + HALO, 1), 0)
        ds = _pool_delta(he, pos_b)

        last = i == n - 1
        a0 = pl.multiple_of(jnp.minimum(i * TM + TM, s_len - HALO), 8)
        ye = jnp.concatenate([y_ref[pl.ds(r0, TM), :], y_ref[pl.ds(a0, HALO), :]], axis=0)
        dt = dx_ref[pl.ds(r0, TM), :]
        de = jnp.concatenate([dt, jnp.where(last, 0.0, dx_ref[pl.ds(a0, HALO), :])], axis=0)
        dye, prod = _rms_bwd(ye, qg_ref[0:1, :], de)
        dqg_ref[...] += _rowsum(prod[:TM, :])
        dys = dye * sc_ref[...]
        pos_a = i * TM + lax.broadcasted_iota(jnp.int32, (TM + HALO, 1), 0)

        dhs, dscs = [], []
        for gi, w in enumerate(WINDOWS):
            sl = slice(gi * POOL_G, (gi + 1) * POOL_G)
            wg = w_ref[gi]
            dys_g = dys[:, sl].astype(BF)
            d_g = ds[gi][HALO:, :].astype(BF)
            ypre = _dot(d_g, wg)
            dscs.append(_rowsum(dye[:TM, sl] * ypre))
            wacc[gi] += _dot_tn(d_g, dys_g[:TM, :])
            dd = _dot_nt(dys_g, wg)
            cnt = jnp.minimum(pos_a + 1, w).astype(F32)
            a = dd / cnt
            k = 1
            while k < w:
                a = a + pltpu.roll(a, TM + HALO - k, 0)
                k *= 2
            dhs.append(a[:TM, :] - dd[:TM, :])
        dsc_ref[...] += jnp.concatenate(dscs, axis=1)
        dh = jnp.concatenate(dhs, axis=1)
        dxp, prod2 = _rms_bwd(xt, pg_ref[0:1, :], dh)
        dpg_ref[...] += _rowsum(prod2)
        dx0_ref[...] = dt + dxp

        @pl.when(last)
        def _():
            dw_ref[...] = wacc[...].astype(BF)

    return _call(
        body, name="mixa_bwd", grid=(n,), in_specs=[VSPEC] * 7,
        out_specs=[_row_spec(TM), _const_spec((ng, POOL_G, POOL_G)), _const_spec((1, D)),
                   _const_spec((1, D)), _const_spec((1, D))],
        out_shape=[_sds((s_len, D)), _sds((ng, POOL_G, POOL_G), BF), _sds((1, D)), _sds((1, D)), _sds((1, D))],
        scratch_shapes=[pltpu.VMEM((ng, POOL_G, POOL_G), F32)],
        args=[dx1, x, y, pre_g, pool_w, pool_scale, post_g], rider=rider)


def _ffn_fwd(layer, x1, pre_g, wgu, wd, post_g, rider=None):
    s_len = x1.shape[0]

    def body(x_ref, pg_ref, wgu_ref, wd_ref, qg_ref, f_ref, x2_ref, g_ref, u_ref):
        x = x_ref[...]
        h = _rms_fwd(x, pg_ref[layer:layer + 1, :]).astype(BF)
        f = jnp.zeros((TM, D), F32)
        for c in range(FF // FF_HALF):
            cols = slice(c * FF_HALF, (c + 1) * FF_HALF)
            g = _dot(h, wgu_ref[0, :, cols])
            u = _dot(h, wgu_ref[1, :, cols])
            g_ref[:, cols] = g.astype(BF)
            u_ref[:, cols] = u.astype(BF)
            act = g * _sigmoid(g) * u
            f = f + _dot(act.astype(BF), wd_ref[cols, :])
        f_ref[...] = f
        x2_ref[...] = x + _rms_fwd(f, qg_ref[layer:layer + 1, :])

    return _call(body, name=f"ffn_fwd{layer}", grid=(s_len // TM,),
                 in_specs=[_row_spec(TM), VSPEC, VSPEC, VSPEC, VSPEC],
                 out_specs=[_row_spec(TM), _row_spec(TM), _row_spec(TM, FF), _row_spec(TM, FF)],
                 out_shape=[_sds((s_len, D)), _sds((s_len, D)), _sds((s_len, FF), BF), _sds((s_len, FF), BF)],
                 args=[x1, pre_g, wgu, wd, post_g], rider=rider)


GU_PIECE = 128
DN_PIECE = 64
DN_SLOT = FF // N_CHIPS
HALF_D = D // 2
CHUNK_STRIDE = 6
CHUNK_START = (1, 7, 4, 10)


def _ffn_bwd(layer, dx2, x1, f, g_pre, u_pre, pre_g, wgu, wd, post_g, kc, rider=None):
    s_len = x1.shape[0]
    tm = TM_FFN_BWD
    n = s_len // tm
    nc = FF // FF_CHUNK
    n_gu, n_dn = FF_CHUNK // GU_PIECE, FF_CHUNK // DN_PIECE
    n_pieces = 2 * n_gu + n_dn
    n_blk = FF_HALF // GU_PIECE

    def edge_rows(c, i, kc_ref):
        return (jnp.where((c == 0) | (c == nc - 1), i, n - 1), 0)

    def chunk_at(c, kc_ref):
        k = kc_ref[0]
        start = jnp.where(k == 0, CHUNK_START[0], jnp.where(k == 1, CHUNK_START[1],
                                                            jnp.where(k == 2, CHUNK_START[2], CHUNK_START[3])))
        return ((c + start) * CHUNK_STRIDE) % nc

    def exchange(kc_ref, c, accg, accu, accd, own_gu_ref, land_gu_ref, own_dn_ref, land_dn_ref,
                 pl_gu, pl_dn, sib_gu, sib_dn, mine_gu, mine_dn, sum_gu, sum_dn,
                 psend, precv, ssend, lsem, rrecv):
        x, y, core = lax.axis_index("x"), lax.axis_index("y"), lax.axis_index("c")
        lower = core == 0

        def pair_copy(cc, part):
            p = cc % 2
            src, dst = ((sib_gu, pl_gu), (sib_dn, pl_dn))[part]
            return pltpu.make_async_remote_copy(src.at[p], dst.at[cc], psend.at[p, part], precv.at[cc, part],
                                                device_id=(x, y, 1 - core), device_id_type=MESH)

        def scatter(cc, wait):
            p = cc % 2
            hidden = chunk_at(cc, kc_ref) * FF_CHUNK

            assert n_gu == 2
            k0, k1 = hidden // FF_HALF, (hidden + GU_PIECE) // FF_HALF
            blk = (hidden - k0 * FF_HALF) // GU_PIECE
            for gu in range(2):
                @pl.when(k0 == k1)
                def _():
                    piece(p, wait, 2 * gu, sum_gu.at[p, gu], k0 + 2 * gu, 0, (pl.ds(blk, 2),))

                @pl.when(k0 != k1)
                def _():
                    piece(p, wait, 2 * gu, sum_gu.at[p, gu, 0], k0 + 2 * gu, 0, (blk,))
                    piece(p, wait, 2 * gu + 1, sum_gu.at[p, gu, 1], k1 + 2 * gu, 0, (0,))

            kd = hidden // DN_SLOT
            off = pl.multiple_of(hidden - kd * DN_SLOT, DN_PIECE)
            m = jnp.minimum((DN_SLOT - off) // DN_PIECE, n_dn)
            for mm in range(1, n_dn + 1):
                @pl.when(m == mm)
                def _():
                    rows = mm * DN_PIECE
                    piece(p, wait, 2 * n_gu, sum_dn.at[p, pl.ds(0, rows), :], kd, 1, (pl.ds(off, rows), slice(None)))
                    if mm < n_dn:
                        piece(p, wait, 2 * n_gu + 1, sum_dn.at[p, pl.ds(rows, FF_CHUNK - rows), :], kd + 1, 1,
                              (pl.ds(0, FF_CHUNK - rows), slice(None)))

        def piece(p, wait, pi, src, k, t, where):
            own_ref, land_ref = ((own_gu_ref, land_gu_ref), (own_dn_ref, land_dn_ref))[t]
            kx, ky = k // 2, k % 2
            fx, fy = (kx != x).astype(jnp.int32), (ky != y).astype(jnp.int32)
            local = (fx + fy) == 0
            j = jnp.maximum(fx + 2 * fy - 1, 0)

            @pl.when(local)
            def _():
                cp = pltpu.make_async_copy(src, own_ref.at[where], lsem.at[p, pi])
                if wait:
                    cp.wait()
                else:
                    cp.start()

            @pl.when(jnp.logical_not(local))
            def _():
                cp = pltpu.make_async_remote_copy(src, land_ref.at[(j,) + where], ssend.at[p, pi],
                                                  rrecv.at[t, j], device_id=(kx, ky, core), device_id_type=MESH)
                if wait:
                    cp.wait_send()
                else:
                    cp.start()

        def add_and_scatter(cc):
            p = cc % 2
            pair_copy(cc, 0).wait_recv()
            pair_copy(cc, 1).wait_recv()
            s_gu = (mine_gu[...] + pl_gu[cc].astype(F32)).astype(BF)
            for hc in range(n_gu):
                sum_gu[p, :, hc] = s_gu[:, :, hc * GU_PIECE:(hc + 1) * GU_PIECE]
            sum_dn[p] = (mine_dn[...] + pl_dn[cc].astype(F32)).astype(BF)
            scatter(cc, wait=False)

        @pl.when(c >= 1)
        def _():
            @pl.when(c >= 3)
            def _():
                scatter(c - 3, wait=True)
            add_and_scatter(c - 1)

        @pl.when(c >= 2)
        def _():
            pair_copy(c - 2, 0).wait_send()
            pair_copy(c - 2, 1).wait_send()

        p = c % 2
        my_rows = pl.ds(pl.multiple_of(core * HALF_D, HALF_D), HALF_D)
        sib_rows = pl.ds(pl.multiple_of((1 - core) * HALF_D, HALF_D), HALF_D)
        d_v = accd[...]
        sib_gu[p, 0] = accg[sib_rows, :].astype(BF)
        sib_gu[p, 1] = accu[sib_rows, :].astype(BF)
        sib_dn[p] = jnp.where(lower, d_v[:, HALF_D:], d_v[:, :HALF_D]).astype(BF)
        mine_gu[0] = accg[my_rows, :]
        mine_gu[1] = accu[my_rows, :]
        mine_dn[...] = jnp.where(lower, d_v[:, :HALF_D], d_v[:, HALF_D:])
        pair_copy(c, 0).start()
        pair_copy(c, 1).start()

        @pl.when(c == nc - 1)
        def _():
            scatter(nc - 3, wait=True)
            add_and_scatter(nc - 1)
            for cc in (nc - 2, nc - 1):
                pair_copy(cc, 0).wait_send()
                pair_copy(cc, 1).wait_send()
                scatter(cc, wait=True)
            for t, land_ref in enumerate((land_gu_ref, land_dn_ref)):
                for j in range(N_CHIPS - 1):
                    pltpu.make_async_remote_copy(land_ref.at[j], land_ref.at[j], ssend.at[0, 0], rrecv.at[t, j],
                                                 device_id=(x, y, core), device_id_type=MESH).wait_recv()

    def body(kc_ref, dx_ref, x_ref, f_ref, gp_ref, up_ref, pg_ref, wgu_ref, wd_ref, qg_ref,
             dx1_ref, dpg_ref, dqg_ref, own_gu_ref, land_gu_ref, own_dn_ref, land_dn_ref,
             h_s, df_s, dh_s, accg, accu, accd, *comm):
        c = pl.program_id(0)
        i = pl.program_id(1)
        rows = pl.ds(pl.multiple_of(i * tm, tm), tm)
        pg = pg_ref[layer:layer + 1, :]

        @pl.when((c == 0) & (i == 0))
        def _():
            dpg_ref[...] = jnp.zeros_like(dpg_ref)
            dqg_ref[...] = jnp.zeros_like(dqg_ref)

        @pl.when(c == 0)
        def _():
            h_s[rows, :] = _rms_fwd(x_ref[...], pg).astype(BF)
            df, prod = _rms_bwd(f_ref[...], qg_ref[layer:layer + 1, :], dx_ref[...])
            df_s[rows, :] = df.astype(BF)
            dqg_ref[...] += _rowsum(prod)

        @pl.when(i == 0)
        def _():
            accg[...] = jnp.zeros_like(accg)
            accu[...] = jnp.zeros_like(accu)
            accd[...] = jnp.zeros_like(accd)

        h = h_s[rows, :]
        df = df_s[rows, :]
        wg = wgu_ref[0]
        wu = wgu_ref[1]
        g = gp_ref[...].astype(F32)
        u = up_ref[...].astype(F32)
        sg = _sigmoid(g)
        a = g * sg
        dact = _dot_nt(df, wd_ref[...])
        accd[...] += _dot_tn((a * u).astype(BF), df)
        du = (dact * a).astype(BF)
        dg = (dact * u * (sg * (1.0 + g * (1.0 - sg)))).astype(BF)
        accg[...] += _dot_tn(h, dg)
        accu[...] += _dot_tn(h, du)
        dh = _dot_nt(dg, wg) + _dot_nt(du, wu)

        @pl.when(c == 0)
        def _():
            dh_s[rows, :] = dh

        @pl.when((c > 0) & (c < nc - 1))
        def _():
            dh_s[rows, :] += dh

        @pl.when(c == nc - 1)
        def _():
            dxp, prod = _rms_bwd(x_ref[...], pg, dh_s[rows, :] + dh)
            dpg_ref[...] += _rowsum(prod)
            dx1_ref[...] = dx_ref[...] + dxp

        @pl.when(i == n - 1)
        def _():
            exchange(kc_ref, c, accg, accu, accd, own_gu_ref, land_gu_ref, own_dn_ref, land_dn_ref, *comm)

    dma = pltpu.SemaphoreType.DMA
    return _call(
        body, name=f"ffn_bwd{layer}", grid=(nc, n),
        in_specs=[pl.BlockSpec((tm, D), edge_rows), pl.BlockSpec((tm, D), edge_rows),
                  pl.BlockSpec((tm, D), lambda c, i, kc_ref: (jnp.where(c == 0, i, n - 1), 0),
                               pipeline_mode=pl.Buffered(1)),
                  pl.BlockSpec((tm, FF_CHUNK), lambda c, i, kc_ref: (i, chunk_at(c, kc_ref))),
                  pl.BlockSpec((tm, FF_CHUNK), lambda c, i, kc_ref: (i, chunk_at(c, kc_ref))),
                  VSPEC,
                  pl.BlockSpec((2, D, FF_CHUNK), lambda c, i, kc_ref: (0, 0, chunk_at(c, kc_ref))),
                  pl.BlockSpec((FF_CHUNK, D), lambda c, i, kc_ref: (chunk_at(c, kc_ref), 0)),
                  VSPEC],
        out_specs=[pl.BlockSpec((tm, D), lambda c, i, kc_ref: (jnp.where(c == nc - 1, i, 0), 0)),
                   _const_spec((1, D)), _const_spec((1, D)), ANYSPEC, ANYSPEC, ANYSPEC, ANYSPEC],
        out_shape=[_sds((s_len, D)), _sds((1, D)), _sds((1, D)),
                   _sds((n_blk, HALF_D, GU_PIECE), BF), _sds((N_CHIPS - 1, n_blk, HALF_D, GU_PIECE), BF),
                   _sds((DN_SLOT, HALF_D), BF), _sds((N_CHIPS - 1, DN_SLOT, HALF_D), BF)],
        scratch_shapes=[pltpu.VMEM((s_len, D), BF), pltpu.VMEM((s_len, D), BF), pltpu.VMEM((s_len, D), F32),
                        pltpu.VMEM((D, FF_CHUNK), F32), pltpu.VMEM((D, FF_CHUNK), F32),
                        pltpu.VMEM((FF_CHUNK, D), F32),
                        pltpu.VMEM((nc, 2, HALF_D, FF_CHUNK), BF), pltpu.VMEM((nc, FF_CHUNK, HALF_D), BF),
                        pltpu.VMEM((2, 2, HALF_D, FF_CHUNK), BF), pltpu.VMEM((2, FF_CHUNK, HALF_D), BF),
                        pltpu.VMEM((2, HALF_D, FF_CHUNK), F32), pltpu.VMEM((FF_CHUNK, HALF_D), F32),
                        pltpu.VMEM((2, 2, n_gu, HALF_D, GU_PIECE), BF), pltpu.VMEM((2, FF_CHUNK, HALF_D), BF),
                        dma((2, 2)), dma((nc, 2)), dma((2, n_pieces)), dma((2, n_pieces)), dma((2, N_CHIPS - 1))],
        args=[dx2, x1, f, g_pre, u_pre, pre_g, wgu, wd, post_g], rider=rider, prefetch=kc)


def _ple_fwd(layer, x2, p, ple_g, w_gate, w_proj, post_g, target=None, qkv=None, rider=None):
    s_len = x2.shape[0]
    final = target is not None
    assert not (final and qkv)

    def body(*refs):
        if final:
            x_ref, p_ref, g_ref, wg_ref, wp_ref, qg_ref, t_ref, z_ref, pe_ref, dx_ref, lv_ref = refs
        elif qkv:
            (x_ref, p_ref, g_ref, wg_ref, wp_ref, qg_ref, ng_ref, kg_ref, wq_ref, wkv_ref,
             z_ref, pe_ref, x3_ref, q_ref, kv_ref) = refs
        else:
            x_ref, p_ref, g_ref, wg_ref, wp_ref, qg_ref, z_ref, pe_ref, x3_ref = refs
        x = x_ref[...]
        r = _rms_fwd(x, g_ref[layer:layer + 1, :]).astype(BF)
        z = _dot(r, wg_ref[...])
        pe = _dot(p_ref[...].astype(BF), wp_ref[...])
        z_ref[...] = z
        pe_ref[...] = pe
        x3 = x + _rms_fwd(pe * _sigmoid(z), qg_ref[layer:layer + 1, :])
        if final:
            @pl.when(pl.program_id(0) == 0)
            def _():
                lv_ref[...] = jnp.zeros_like(lv_ref)
            err = x3 - t_ref[...]
            dx_ref[...] = err * (1.0 / D)
            lv_ref[...] += _rowsum(err * err)
        else:
            x3_ref[...] = x3
        if qkv:
            q_ref[...] = _dot(_rms_fwd(x3, ng_ref[layer + 1:layer + 2, :]).astype(BF), wq_ref[...]).astype(BF)
            kv_ref[...] = _dot(_rms_fwd(x3, kg_ref[...]).astype(BF), wkv_ref[...]).astype(BF)

    p_spec = pl.BlockSpec((None, TM, PLE), lambda i: (layer, i, 0))
    in_specs = [_row_spec(TM), p_spec, VSPEC, VSPEC, VSPEC, VSPEC]
    args = [x2, p, ple_g, w_gate, w_proj, post_g]
    out_specs = [_row_spec(TM), _row_spec(TM), _row_spec(TM)]
    out_shape = [_sds((s_len, D))] * 3
    if qkv:
        in_specs += [VSPEC] * 4
        args += list(qkv)
        out_specs += [_row_spec(TM), _row_spec(TM, 2 * KVD)]
        out_shape += [_sds((s_len, D), BF), _sds((s_len, 2 * KVD), BF)]
    if final:
        in_specs.append(_row_spec(TM))
        args.append(target)
        out_specs.append(_const_spec((1, D)))
        out_shape.append(_sds((1, D)))
    return _call(body, name=f"ple_fwd{layer}", grid=(s_len // TM,), in_specs=in_specs, out_specs=out_specs,
                 out_shape=out_shape, args=args, rider=rider)


def _ple_bwd(layer, dx3, x2, z, pe, p, ple_g, w_gate, post_g, rider=None):
    s_len = x2.shape[0]
    n = s_len // TM

    def body(dx_ref, x_ref, z_ref, pe_ref, p_ref, g_ref, wg_ref, qg_ref,
             dx2_ref, dwg_ref, dwp_ref, dg_ref, dqg_ref, gacc, pacc):
        i = pl.program_id(0)

        @pl.when(i == 0)
        def _():
            gacc[...] = jnp.zeros_like(gacc)
            pacc[...] = jnp.zeros_like(pacc)
            dg_ref[...] = jnp.zeros_like(dg_ref)
            dqg_ref[...] = jnp.zeros_like(dqg_ref)

        dx = dx_ref[...]
        x = x_ref[...]
        pe_v = pe_ref[...]
        gate = _sigmoid(z_ref[...])
        de, prod = _rms_bwd(pe_v * gate, qg_ref[layer:layer + 1, :], dx)
        dqg_ref[...] += _rowsum(prod)
        dpe = (de * gate).astype(BF)
        dz = (de * pe_v * gate * (1.0 - gate)).astype(BF)
        pacc[...] += _dot_tn(p_ref[...].astype(BF), dpe)
        g = g_ref[layer:layer + 1, :]
        r = _rms_fwd(x, g).astype(BF)
        gacc[...] += _dot_tn(r, dz)
        dr = _dot_nt(dz, wg_ref[...])
        dxp, prod2 = _rms_bwd(x, g, dr)
        dg_ref[...] += _rowsum(prod2)
        dx2_ref[...] = dx + dxp

        @pl.when(i == n - 1)
        def _():
            dwg_ref[...] = gacc[...].astype(BF)
            dwp_ref[...] = pacc[...].astype(BF)

    p_spec = pl.BlockSpec((None, TM, PLE), lambda i: (layer, i, 0))
    return _call(
        body, name=f"ple_bwd{layer}", grid=(n,),
        in_specs=[_row_spec(TM), _row_spec(TM), _row_spec(TM), _row_spec(TM), p_spec, VSPEC, VSPEC, VSPEC],
        out_specs=[_row_spec(TM), _const_spec((D, D)), _const_spec((PLE, D)), _const_spec((1, D)), _const_spec((1, D))],
        out_shape=[_sds((s_len, D)), _sds((D, D), BF), _sds((PLE, D), BF), _sds((1, D)), _sds((1, D))],
        scratch_shapes=[pltpu.VMEM((D, D), F32), pltpu.VMEM((PLE, D), F32)],
        args=[dx3, x2, z, pe, p, ple_g, w_gate, post_g], rider=rider)


def _qkv_bwd(dq, dkv, x3, dx4, q_g, kv_g, w_q, w_kv):
    s_len = x3.shape[0]
    n = s_len // TM

    def body(dq_ref, dkv_ref, x_ref, dx_ref, qg_ref, kg_ref, wq_ref, wkv_ref,
             dx3_ref, dwq_ref, dwkv_ref, dqg_ref, dkg_ref, qacc, kacc):
        i = pl.program_id(0)

        @pl.when(i == 0)
        def _():
            qacc[...] = jnp.zeros_like(qacc)
            kacc[...] = jnp.zeros_like(kacc)
            dqg_ref[...] = jnp.zeros_like(dqg_ref)
            dkg_ref[...] = jnp.zeros_like(dkg_ref)

        x = x_ref[...]
        qg = qg_ref[1:2, :]
        kg = kg_ref[...]
        dq_v = dq_ref[...]
        dkv_v = dkv_ref[...].astype(BF)
        qacc[...] += _dot_tn(_rms_fwd(x, qg).astype(BF), dq_v)
        kacc[...] += _dot_tn(_rms_fwd(x, kg).astype(BF), dkv_v)
        dxq, prod_q = _rms_bwd(x, qg, _dot_nt(dq_v, wq_ref[...]))
        dxk, prod_k = _rms_bwd(x, kg, _dot_nt(dkv_v, wkv_ref[...]))
        dqg_ref[...] += _rowsum(prod_q)
        dkg_ref[...] += _rowsum(prod_k)
        dx3_ref[...] = dx_ref[...] + dxq + dxk

        @pl.when(i == n - 1)
        def _():
            dwq_ref[...] = qacc[...].astype(BF)
            dwkv_ref[...] = kacc[...].astype(BF)

    outs, _ = _call(
        body, name="qkv_bwd", grid=(n,),
        in_specs=[_row_spec(TM), _row_spec(TM, 2 * KVD), _row_spec(TM), _row_spec(TM), VSPEC, VSPEC, VSPEC, VSPEC],
        out_specs=[_row_spec(TM), _const_spec((D, D)), _const_spec((D, 2 * KVD)),
                   _const_spec((1, D)), _const_spec((1, D))],
        out_shape=[_sds((s_len, D)), _sds((D, D), BF), _sds((D, 2 * KVD), BF), _sds((1, D)), _sds((1, D))],
        scratch_shapes=[pltpu.VMEM((D, D), F32), pltpu.VMEM((D, 2 * KVD), F32)],
        args=[dq, dkv, x3, dx4, q_g, kv_g, w_q, w_kv])
    return outs


def _attn_group(i, q, kvw, sink_ref, g):
    rows = GQA * BLK
    heads = [GQA * g + j for j in range(GQA)]
    off = jnp.where(i > 0, BLK, 0)
    row = lax.broadcasted_iota(jnp.int32, (rows, 2 * BLK), 0)
    rel = (row % BLK) - lax.broadcasted_iota(jnp.int32, (rows, 2 * BLK), 1) + off
    valid = (rel >= 0) & (rel < BLK)
    head_of_row = lax.broadcasted_iota(jnp.int32, (rows, 1), 0) // BLK
    slope = jnp.zeros((rows, 1), F32)
    sink = jnp.zeros((rows, 1), F32)
    for j, h in enumerate(heads):
        slope = jnp.where(head_of_row == j, SLOPES[h], slope)
        sink = jnp.where(head_of_row == j, sink_ref[0, h], sink)
    qs = jnp.concatenate([q[:, h * HEAD_DIM:(h + 1) * HEAD_DIM] for h in heads], axis=0)
    k = kvw[:, g * HEAD_DIM:(g + 1) * HEAD_DIM]
    v = kvw[:, KVD + g * HEAD_DIM:KVD + (g + 1) * HEAD_DIM]
    s = _dot_nt(qs, k) * ATT_SCALE - slope * rel.astype(F32)
    s = jnp.where(valid, s, NEG_INF)
    m = jnp.maximum(jnp.max(s, axis=-1, keepdims=True), sink)
    e = jnp.exp(s - m)
    es = jnp.exp(sink - m)
    inv = 1.0 / (jnp.sum(e, axis=-1, keepdims=True) + es)
    return e * inv, es * inv, qs, k, v


def _unstack_heads(stacked):
    return [stacked[j * BLK:(j + 1) * BLK, :] for j in range(GQA)]


def _kv_window(kv_ref, i):
    ks = pl.multiple_of(jnp.maximum(i * BLK - BLK, 0), BLK)
    return ks, kv_ref[pl.ds(ks, 2 * BLK), :]


def _attn_fwd(q, kv, sinks, x3, w_o, post_g, rider=None):
    s_len = q.shape[0]

    def body(q_ref, kv_ref, sk_ref, x_ref, wo_ref, g_ref, a_ref, y_ref, x4_ref):
        i = pl.program_id(0)
        _, kvw = _kv_window(kv_ref, i)
        q = q_ref[...]
        outs = []
        for g in range(N_KV_HEADS):
            p, _, _, _, v = _attn_group(i, q, kvw, sk_ref, g)
            outs += _unstack_heads(_dot(p.astype(BF), v))
        attn = jnp.concatenate(outs, axis=1)
        a_ref[...] = attn
        y = _dot(attn.astype(BF), wo_ref[...])
        y_ref[...] = y
        x4_ref[...] = x_ref[...] + _rms_fwd(y, g_ref[1:2, :])

    return _call(body, name="attn_fwd", grid=(s_len // BLK,),
                 in_specs=[_row_spec(BLK), VSPEC, SSPEC, _row_spec(BLK), VSPEC, VSPEC],
                 out_specs=[_row_spec(BLK)] * 3, out_shape=[_sds((s_len, D))] * 3,
                 args=[q, kv, sinks, x3, w_o, post_g], rider=rider)


ATT_STEP_BLOCKS = 2


def _attn_bwd(dx4, y, attn, q, kv, sinks, w_o, post_g, rider=None):
    s_len = q.shape[0]
    rows = ATT_STEP_BLOCKS * BLK
    n = s_len // rows

    def body(dx_ref, y_ref, a_ref, q_ref, kv_ref, sk_ref, wo_ref, g_ref,
             dq_ref, dkv_ref, dwo_ref, dg_ref, dsk_ref, wacc):
        i = pl.program_id(0)

        @pl.when(i == 0)
        def _():
            dkv_ref[...] = jnp.zeros_like(dkv_ref)
            wacc[...] = jnp.zeros_like(wacc)
            dg_ref[...] = jnp.zeros_like(dg_ref)
            dsk_ref[...] = jnp.zeros_like(dsk_ref)

        dy, prod = _rms_bwd(y_ref[...], g_ref[1:2, :], dx_ref[...])
        dg_ref[...] += _rowsum(prod)
        dyb = dy.astype(BF)
        attn_all = a_ref[...]
        wacc[...] += _dot_tn(attn_all.astype(BF), dyb)
        d_o_all = _dot_nt(dyb, wo_ref[...])
        q_all = q_ref[...]
        lane = lax.broadcasted_iota(jnp.int32, (1, D), 1)
        dsk = jnp.zeros((1, D), F32)
        for sub in range(ATT_STEP_BLOCKS):
            blk = i * ATT_STEP_BLOCKS + sub
            sl = slice(sub * BLK, (sub + 1) * BLK)
            d_o, q = d_o_all[sl, :], q_all[sl, :]
            dod = d_o * attn_all[sl, :]
            ks, kvw = _kv_window(kv_ref, blk)
            dqs, dks, dvs = [], [], []
            for g in range(N_KV_HEADS):
                p, ps, qs, k, v = _attn_group(blk, q, kvw, sk_ref, g)
                cols = [slice((GQA * g + j) * HEAD_DIM, (GQA * g + j + 1) * HEAD_DIM) for j in range(GQA)]
                do_s = jnp.concatenate([d_o[:, c] for c in cols], axis=0).astype(BF)
                dsum = jnp.concatenate([jnp.sum(dod[:, c], axis=-1, keepdims=True) for c in cols], axis=0)
                dp = _dot_nt(do_s, v)
                dsb = (p * (dp - dsum) * ATT_SCALE).astype(BF)
                sink_part = ps * dsum
                for j in range(GQA):
                    dsk = dsk + jnp.where(lane == GQA * g + j, -_rowsum(sink_part[j * BLK:(j + 1) * BLK, :]), 0.0)
                dqs += _unstack_heads(_dot(dsb, k))
                dks.append(_dot_tn(dsb, qs))
                dvs.append(_dot_tn(p.astype(BF), do_s))
            dq_ref[sl, :] = jnp.concatenate(dqs, axis=1).astype(BF)
            dkv_ref[pl.ds(ks, 2 * BLK), :] += jnp.concatenate(dks + dvs, axis=1)
        dsk_ref[...] += dsk

        @pl.when(i == n - 1)
        def _():
            dwo_ref[...] = wacc[...].astype(BF)

    return _call(
        body, name="attn_bwd", grid=(n,),
        in_specs=[_row_spec(rows), _row_spec(rows), _row_spec(rows), _row_spec(rows), VSPEC, SSPEC, VSPEC, VSPEC],
        out_specs=[_row_spec(rows), _const_spec((s_len, 2 * KVD)), _const_spec((D, D)),
                   _const_spec((1, D)), _const_spec((1, D))],
        out_shape=[_sds((s_len, D), BF), _sds((s_len, 2 * KVD)), _sds((D, D), BF), _sds((1, D)), _sds((1, D))],
        scratch_shapes=[pltpu.VMEM((D, D), F32)],
        args=[dx4, y, attn, q, kv, sinks, w_o, post_g], rider=rider)


Big = collections.namedtuple("Big", "name src layer L A R C rb")


def _bigs():
    out = {"pool_w": Big("pool_w", "pool_w", None, 4, 4, POOL_G // N_CHIPS, POOL_G, 32)}
    for l in range(2):
        out[f"w_gu{l}"] = Big(f"w_gu{l}", "w_gu", l, 1, 2, D, FF_HALF, 256)
        out[f"w_down{l}"] = Big(f"w_down{l}", "w_down", l, 1, 4, FF // N_CHIPS, D, 352)
        out[f"w_ple_gate{l}"] = Big(f"w_ple_gate{l}", "w_ple_gate", l, 1, 4, D // N_CHIPS, D, 128)
        out[f"w_ple_proj{l}"] = Big(f"w_ple_proj{l}", "w_ple_proj", l, 1, 1, PLE, D // N_CHIPS, 128)
    out["w_q"] = Big("w_q", "w_q", None, 1, 4, D // N_CHIPS, D, 128)
    out["w_o"] = Big("w_o", "w_o", None, 1, 4, D // N_CHIPS, D, 128)
    out["w_kv"] = Big("w_kv", "w_kv", None, 1, 4, D // N_CHIPS, 2 * KVD, 128)
    return out


BIGS = _bigs()
POOL_SCALE = Big("pool_scale", "pool_scale", None, 1, 1, 1, D // N_CHIPS, 1)
BIG_SOURCES = ("w_gu", "w_down", "w_ple_gate", "w_ple_proj", "w_q", "w_o", "w_kv", "pool_w")


def _ncb(t):
    return N_CHIPS // t.A


def _full_shape(t, rows=None):
    return (t.L, t.A, t.R if rows is None else rows, _ncb(t) * t.C)


def _slot_index(t, k):
    return k // _ncb(t), k % _ncb(t)


def _slot(ref, t, k, row0, rows):
    a, cb = _slot_index(t, k)
    return ref.at[:, a, pl.ds(row0, rows), pl.ds(pl.multiple_of(cb * t.C, 128), t.C)]


def _place_job(t, w, out_dtype=BF):
    nb = next((nb for nb in (8, 4, 2, 1) if t.R % (16 * nb) == 0), 1) if t.L == 1 else 1
    rb = t.R // nb

    def fn(j, kc_ref, ins, outs):
        outs[0][...] = ins[0][...].astype(out_dtype)

    def in_map(j, kc_ref):
        return (j // nb if t.layer is None else t.layer, j % nb, 0)

    def out_map(j, kc_ref):
        a, cb = _slot_index(t, kc_ref[0])
        return (j // nb, a, j % nb, cb)

    return Job(t.L * nb, [(w, (None, rb, t.C), in_map)],
               [(_sds(_full_shape(t), out_dtype), (None, None, rb, t.C), out_map)], fn)


def _mesh_position():
    x, y, c = lax.axis_index("x"), lax.axis_index("y"), lax.axis_index("c")
    chips = [(1 - x, y), (x, 1 - y), (1 - x, 1 - y)]
    return x, y, c, chips


DIRECT_BELOW = 1024


def _gather_rider(parts, fulls):
    nt = len(parts)
    TO_X, TO_Y, FWD_X, FWD_Y, SIB_X, SIB_Y, SIB_D = range(7)

    def rows_of(ti, core):
        t, r0, r1 = parts[ti]
        h = (r1 - r0) // 2
        return r0 + core * h, h

    def copy(outs, sems, kind, ti, k_src, row0, rows, dev):
        region = _slot(outs[ti], parts[ti][0], k_src, row0, rows)
        return pltpu.make_async_remote_copy(region, region, sems[0].at[ti, kind], sems[1].at[ti, kind],
                                            device_id=dev, device_id_type=MESH)

    def plan(outs, sems):
        x, y, c, _ = _mesh_position()
        me, kx, ky, kd = 2 * x + y, 2 * (1 - x) + y, 2 * x + (1 - y), 2 * (1 - x) + (1 - y)
        dev_x, dev_y, dev_d, sib = (1 - x, y, c), (x, 1 - y, c), (1 - x, 1 - y, c), (x, y, 1 - c)

        def whole(ti):
            return 0, parts[ti][0].R

        def mk(kind, k_send, k_recv, dev, send_rows, recv_rows):
            def build(ti, side):
                k_src = k_send if side == "s" else k_recv
                row0, rows = (send_rows if side == "s" else recv_rows)(ti)
                return copy(outs, sems, kind, ti, k_src, row0, rows, dev)
            return build

        def first_half(core):
            return lambda ti: (rows_of(ti, core)[0], rows_of(ti, core)[1] // 2)

        def second_half(core):
            return lambda ti: (rows_of(ti, core)[0] + rows_of(ti, core)[1] // 2, rows_of(ti, core)[1] // 2)

        mine = lambda ti: rows_of(ti, c)
        theirs = lambda ti: rows_of(ti, 1 - c)
        split = {
            TO_X: mk(TO_X, me, kx, dev_x, mine, mine),
            TO_Y: mk(TO_Y, me, ky, dev_y, mine, mine),
            FWD_X: mk(FWD_X, ky, kd, dev_x, first_half(c), first_half(c)),
            FWD_Y: mk(FWD_Y, kx, kd, dev_y, second_half(c), second_half(c)),
            SIB_X: mk(SIB_X, kx, kx, sib, mine, theirs),
            SIB_Y: mk(SIB_Y, ky, ky, sib, mine, theirs),
            SIB_D: mk(SIB_D, kd, kd, sib, mine, theirs),
        }
        direct = {
            TO_X: mk(TO_X, me, kx, dev_x, whole, whole),
            TO_Y: mk(TO_Y, me, ky, dev_y, whole, whole),
            FWD_X: mk(FWD_X, me, kd, dev_d, whole, whole),
        }
        return split, direct

    is_split = [t.L * t.R * t.C >= DIRECT_BELOW for t, _, _ in parts]
    assert all(s or (r0, r1) == (0, t.R) for s, (t, r0, r1) in zip(is_split, parts))

    def start(ins, outs, sems):
        split, direct = plan(outs, sems)
        for ti in range(nt):
            kinds = split if is_split[ti] else direct
            kinds[TO_X](ti, "s").start()
            kinds[TO_Y](ti, "s").start()
            if not is_split[ti]:
                kinds[FWD_X](ti, "s").start()

    def mid(ins, outs, sems):
        split, _ = plan(outs, sems)
        for ti in range(nt):
            if is_split[ti]:
                split[TO_Y](ti, "r").wait_recv()
                split[FWD_X](ti, "s").start()
                split[SIB_Y](ti, "s").start()
        for ti in range(nt):
            if is_split[ti]:
                split[TO_X](ti, "r").wait_recv()
                split[FWD_Y](ti, "s").start()
                split[SIB_X](ti, "s").start()

    def finish(ins, outs, sems):
        split, direct = plan(outs, sems)
        for ti in range(nt):
            if is_split[ti]:
                split[FWD_X](ti, "r").wait_recv()
                split[FWD_Y](ti, "r").wait_recv()
                split[SIB_D](ti, "s").start()
            else:
                for kind in (TO_X, TO_Y, FWD_X):
                    direct[kind](ti, "r").wait_recv()
        for ti in range(nt):
            if is_split[ti]:
                for kind in (SIB_X, SIB_Y, SIB_D):
                    split[kind](ti, "r").wait_recv()
        for ti in range(nt):
            kinds = split if is_split[ti] else direct
            for kind in kinds:
                kinds[kind](ti, "s").wait_send()

    sems = pltpu.SemaphoreType.DMA((nt, 7))
    return Rider(list(fulls), [_sds(a.shape, a.dtype) for a in fulls], {i: i for i in range(nt)},
                 [sems, sems], start, mid, finish)


def _pair_exchange_rider(specs, grads):
    nt = len(specs)

    def copy(ins, outs, sems, ti, c, sibling):
        half = specs[ti].R // 2
        return pltpu.make_async_remote_copy(ins[ti].at[:, :, pl.ds((1 - c) * half, half), :], outs[ti],
                                            sems[0].at[ti], sems[1].at[ti], device_id=sibling, device_id_type=MESH)

    def start(ins, outs, sems):
        x, y, c, _ = _mesh_position()
        for ti in range(nt):
            copy(ins, outs, sems, ti, c, (x, y, 1 - c)).start()

    def finish(ins, outs, sems):
        x, y, c, _ = _mesh_position()
        for ti in range(nt):
            copy(ins, outs, sems, ti, c, (x, y, 1 - c)).wait()

    sems = pltpu.SemaphoreType.DMA((nt,))
    return Rider(list(grads), [_sds(_full_shape(t, t.R // 2), BF) for t in specs], {}, [sems, sems], start, None, finish)


def _pair_sum_job(t, g, land):
    assert t.L == 1
    half = t.R // 2
    nj = half // t.rb
    block = (None, t.A, t.rb, _ncb(t) * t.C)

    def fn(j, kc_ref, ins, outs):
        outs[0][...] = (ins[0][...].astype(F32) + ins[1][...].astype(F32)).astype(BF)

    return Job(nj,
               [(g, block, lambda j, kc_ref: (0, 0, kc_ref[1] * nj + j, 0)),
                (land, block, lambda j, kc_ref: (0, 0, j, 0))],
               [(_sds(_full_shape(t, half), BF), block, lambda j, kc_ref: (0, 0, j, 0))], fn)


def _scatter_rider(specs, sums):
    nt = len(specs)

    def copy(ins, outs, sems, ti, j, chip, c):
        t = specs[ti]
        cx, cy = chip
        return pltpu.make_async_remote_copy(_slot(ins[ti], t, 2 * cx + cy, 0, t.R // 2), outs[ti].at[j],
                                            sems[0].at[ti, j], sems[1].at[ti, j],
                                            device_id=(cx, cy, c), device_id_type=MESH)

    def start(ins, outs, sems):
        _, _, c, chips = _mesh_position()
        for j, chip in enumerate(chips):
            for ti in range(nt):
                copy(ins, outs, sems, ti, j, chip, c).start()

    def finish(ins, outs, sems):
        _, _, c, chips = _mesh_position()
        for j, chip in enumerate(chips):
            for ti in range(nt):
                copy(ins, outs, sems, ti, j, chip, c).wait()

    sems = pltpu.SemaphoreType.DMA((nt, N_CHIPS - 1))
    return Rider(list(sums), [_sds((N_CHIPS - 1, t.L, t.R // 2, t.C), BF) for t in specs], {}, [sems, sems],
                 start, None, finish)


def _chip_sum_job(ts, landed):
    t0 = ts[0]
    assert t0.L == 1
    half = t0.R // 2
    nj = half // t0.rb

    def local(j, li):
        return jnp.clip(j - li * nj, 0, nj - 1)

    ins = []
    for li, t in enumerate(ts):
        s, land = landed[t.name]

        def own_map(j, kc_ref, li=li, t=t):
            a, cb = _slot_index(t, kc_ref[0])
            return (0, a, local(j, li), cb)

        ins.append((s, (None, None, t.rb, t.C), own_map))
        ins.append((land, (N_CHIPS - 1, None, t.rb, t.C), lambda j, kc_ref, li=li: (0, 0, local(j, li), 0)))

    def fn(j, kc_ref, in_refs, outs):
        for li in range(len(ts)):
            @pl.when(j // nj == li)
            def _():
                acc = in_refs[2 * li][...].astype(F32)
                for k in range(N_CHIPS - 1):
                    acc = acc + in_refs[2 * li + 1][k].astype(F32)
                outs[0][...] = acc

    return Job(len(ts) * nj, ins,
               [(_sds((len(ts), t0.R, t0.C)), (None, t0.rb, t0.C),
                 lambda j, kc_ref: (j // nj, kc_ref[1] * nj + j % nj, 0))], fn)


def _adamw_job(rb, w, g, m, v):
    n_layers, r, c = w.shape
    nb = r // rb
    block = (None, rb, c)
    index = lambda j, kc_ref: (j // nb, j % nb, 0)

    def fn(j, kc_ref, ins, outs):
        g_v = ins[1][...]
        outs[0][...] = g_v
        outs[1][...], outs[2][...], outs[3][...] = _adamw_math(ins[0][...], g_v, ins[2][...], ins[3][...])

    return Job(n_layers * nb, [(a, block, index) for a in (w, g, m, v)],
               [(_sds(w.shape), block, index)] * 4, fn)


def _chip_sum_fused_job(ts, fused, by_cols):
    t0 = ts[0]
    own0 = fused[t0.name][0]
    if by_cols:
        rows, cols = own0.shape
    else:
        nb, rows, bw = own0.shape
        cols = nb * bw
    nj = rows // t0.rb

    def local(j, li):
        return jnp.clip(j - li * nj, 0, nj - 1)

    ins = []
    for li, t in enumerate(ts):
        own, land = fused[t.name]
        if by_cols:
            ins.append((own, (t.rb, cols), lambda j, kc_ref, li=li: (local(j, li), 0)))
            ins.append((land, (N_CHIPS - 1, t.rb, cols), lambda j, kc_ref, li=li: (0, local(j, li), 0)))
        else:
            ins.append((own, (nb, t.rb, bw), lambda j, kc_ref, li=li: (0, local(j, li), 0)))
            ins.append((land, (N_CHIPS - 1, nb, t.rb, bw), lambda j, kc_ref, li=li: (0, 0, local(j, li), 0)))

    def fn(j, kc_ref, in_refs, outs):
        for li in range(len(ts)):
            @pl.when(j // nj == li)
            def _():
                acc = in_refs[2 * li][...].astype(F32)
                for k in range(N_CHIPS - 1):
                    acc = acc + in_refs[2 * li + 1][k].astype(F32)
                outs[0][...] = acc if by_cols else jnp.concatenate([acc[b] for b in range(nb)], axis=1)

    def out_map(j, kc_ref):
        return (j // nj, j % nj, kc_ref[1]) if by_cols else (j // nj, kc_ref[1] * nj + j % nj, 0)

    return Job(len(ts) * nj, ins, [(_sds((len(ts), t0.R, t0.C)), (None, t0.rb, cols), out_map)], fn)


def _share_rider(halves, by_cols):
    nt = len(halves)

    def copy(outs, sems, ti, core, sibling):
        axis = 2 if by_cols[ti] else 1
        half = halves[ti].shape[axis] // 2
        piece = pl.ds(pl.multiple_of(core * half, 128 if by_cols[ti] else 8), half)
        part = outs[ti].at[:, :, piece] if by_cols[ti] else outs[ti].at[:, piece, :]
        return pltpu.make_async_remote_copy(part, part, sems[0].at[ti], sems[1].at[ti],
                                            device_id=sibling, device_id_type=MESH)

    def start(ins, outs, sems):
        x, y, c, _ = _mesh_position()
        for ti in range(nt):
            copy(outs, sems, ti, c, (x, y, 1 - c)).start()

    def finish(ins, outs, sems):
        x, y, c, _ = _mesh_position()
        for ti in range(nt):
            copy(outs, sems, ti, 1 - c, (x, y, 1 - c)).wait_recv()
        for ti in range(nt):
            copy(outs, sems, ti, c, (x, y, 1 - c)).wait_send()

    sems = pltpu.SemaphoreType.DMA((nt,))
    return Rider(list(halves), [_sds(a.shape, a.dtype) for a in halves], {i: i for i in range(nt)}, [sems, sems],
                 start, None, finish)


def _both(r1, r2):
    assert r1.mid is None and r2.mid is None
    ni, no, ns = len(r1.arrays), len(r1.out_shapes), len(r1.scratch)

    def split(fn1, fn2):
        def run(ins, outs, scr):
            fn1(ins[:ni], outs[:no], scr[:ns])
            fn2(ins[ni:], outs[no:], scr[ns:])
        return run

    aliases = dict(r1.aliases)
    aliases.update({ni + a: no + b for a, b in r2.aliases.items()})
    return Rider(r1.arrays + r2.arrays, r1.out_shapes + r2.out_shapes, aliases, r1.scratch + r2.scratch,
                 split(r1.start, r2.start), None, split(r1.finish, r2.finish))


def _adamw_math(w, g, m, v):
    m = B1 * m + (1.0 - B1) * g
    v = B2 * v + (1.0 - B2) * (g * g)
    delta = -LR * ((m / BC1) / (jnp.sqrt(v / BC2) + AEPS) + WD * w)
    return delta, m, v


def _adamw(name, rb, w, g, m, v):
    n_layers, r, c = w.shape

    def body(w_ref, g_ref, m_ref, v_ref, go_ref, d_ref, nm_ref, nv_ref):
        g_v = g_ref[...]
        go_ref[...] = g_v
        d_ref[...], nm_ref[...], nv_ref[...] = _adamw_math(w_ref[...], g_v, m_ref[...], v_ref[...])

    spec = pl.BlockSpec((None, rb, c), lambda l, j: (l, j, 0))
    return pl.pallas_call(
        body, name=f"adamw_{name}", grid=(n_layers, r // rb),
        in_specs=[spec] * 4, out_specs=[spec] * 4, out_shape=[_sds(w.shape)] * 4,
        compiler_params=_params(2),
    )(w, g, m, v)


GAIN_ROWS = {"pre_mix_g": 0, "post_mix_g": 2, "pre_ffn_g": 4, "post_ffn_g": 6, "ple_g": 8, "ple_post_g": 10}
ROW_KV_G, ROW_POOL_SCALE, ROW_SINKS, ROW_LOSS, PACK_ROWS = 12, 13, 14, 15, 16
SMALL_NAMES = tuple(GAIN_ROWS) + ("kv_g", "pool_scale", "sinks")


def _small_all_reduce(rows, dpool, rider=None):
    ng, pr = len(WINDOWS), POOL_G // N_CHIPS

    def body(*refs):
        row_refs = refs[:PACK_ROWS]
        dpool_ref, tot_ref, gpool_ref, pack, land, pland, send, recv, psend, precv = refs[PACK_ROWS:]
        x, y, c, _ = _mesh_position()
        me = 4 * x + 2 * y + c
        for r in range(PACK_ROWS):
            pack[r:r + 1, :] = row_refs[r][...]

        def shard_of(k):
            return dpool_ref.at[:, pl.ds(pl.multiple_of(k * pr, pr), pr), :]

        cps = []
        for j in range(1, N_DEV):
            px, py, pc = x ^ (j >> 2), y ^ ((j >> 1) & 1), c ^ (j & 1)
            cps.append(pltpu.make_async_remote_copy(pack, land.at[me], send.at[j], recv.at[j],
                                                    device_id=(px, py, pc), device_id_type=MESH))
            cps.append(pltpu.make_async_remote_copy(shard_of(2 * px + py), pland.at[me], psend.at[j], precv.at[j],
                                                    device_id=(px, py, pc), device_id_type=MESH))
        for cp in cps:
            cp.start()
        land[me] = pack[...]
        pland[me] = dpool_ref[:, pl.ds(pl.multiple_of((2 * x + y) * pr, pr), pr), :]
        for j in range(1, N_DEV):
            pltpu.make_async_remote_copy(pack, land.at[me ^ j], send.at[j], recv.at[j],
                                         device_id=(x, y, c), device_id_type=MESH).wait_recv()
            pltpu.make_async_remote_copy(shard_of(0), pland.at[me ^ j], psend.at[j], precv.at[j],
                                         device_id=(x, y, c), device_id_type=MESH).wait_recv()
        for cp in cps:
            cp.wait_send()
        tot = land[0]
        gp = pland[0].astype(F32)
        for d in range(1, N_DEV):
            tot = tot + land[d]
            gp = gp + pland[d].astype(F32)
        tot_ref[...] = tot
        gpool_ref[...] = gp

    sems = pltpu.SemaphoreType.DMA((N_DEV,))
    return _call(
        body, name="small_all_reduce", grid=(1,),
        in_specs=[VSPEC] * (PACK_ROWS + 1), out_specs=[VSPEC, VSPEC],
        out_shape=[_sds((PACK_ROWS, D)), _sds((ng, pr, POOL_G))],
        scratch_shapes=[pltpu.VMEM((PACK_ROWS, D), F32), pltpu.VMEM((N_DEV, PACK_ROWS, D), F32),
                        pltpu.VMEM((N_DEV, ng, pr, POOL_G), BF), sems, sems, sems, sems],
        args=[*rows, dpool], rider=rider)


def _small_adamw(tot, kc, small_w, small_m, small_v):
    names = SMALL_NAMES
    n = len(names)

    def body(*refs):
        tot_ref, kc_ref = refs[0], refs[1]
        w_refs = dict(zip(names, refs[2:2 + n]))
        m_refs = dict(zip(names, refs[2 + n:2 + 2 * n]))
        v_refs = dict(zip(names, refs[2 + 2 * n:2 + 3 * n]))
        loss_ref = refs[2 + 3 * n]
        out_refs = {nm: refs[3 + 3 * n + 4 * k: 7 + 3 * n + 4 * k] for k, nm in enumerate(names)}
        tot = tot_ref[...]
        loss_ref[...] = 0.5 * jnp.sum(tot[ROW_LOSS:ROW_LOSS + 1, :], axis=-1, keepdims=True) * (1.0 / D)

        def update(nm, g):
            g_ref, d_ref, nm_ref, nv_ref = out_refs[nm]
            g_ref[...] = g
            d_ref[...], nm_ref[...], nv_ref[...] = _adamw_math(w_refs[nm][...], g, m_refs[nm][...], v_refs[nm][...])

        for nm, r in GAIN_ROWS.items():
            update(nm, tot[r:r + 2, :])
        update("kv_g", tot[ROW_KV_G:ROW_KV_G + 1, :])
        k = kc_ref[0]
        width = D // N_CHIPS
        g_scale = jnp.zeros((1, width), F32)
        for kk in range(N_CHIPS):
            g_scale = g_scale + jnp.where(k == kk, tot[ROW_POOL_SCALE:ROW_POOL_SCALE + 1, kk * width:(kk + 1) * width], 0.0)
        update("pool_scale", g_scale)
        update("sinks", tot[ROW_SINKS:ROW_SINKS + 1, 0:N_HEADS])

    ins = [tot, kc] + [small_w[nm] for nm in names] + [small_m[nm] for nm in names] + [small_v[nm] for nm in names]
    out_shape = [_sds((1, 1))]
    for nm in names:
        out_shape += [_sds(small_w[nm].shape)] * 4
    outs = pl.pallas_call(
        body, name="small_adamw",
        in_specs=[VSPEC, SSPEC] + [VSPEC] * (3 * n), out_specs=[VSPEC] * len(out_shape), out_shape=out_shape,
        compiler_params=_params(),
    )(*ins)
    return outs[0], {nm: outs[1 + 4 * k: 5 + 4 * k] for k, nm in enumerate(names)}


def _compute_layout(t, full):
    if t.src == "w_gu":
        return full.reshape(2, D, FF)
    if t.src == "pool_w":
        return full.reshape(len(WINDOWS), POOL_G, POOL_G)
    if t.src == "pool_scale":
        return full.reshape(1, D)
    return full.reshape(t.A * t.R, _ncb(t) * t.C)


def kernel(x, p, pre_mix_g, post_mix_g, pre_ffn_g, post_ffn_g, pool_w, pool_scale, kv_g, w_kv, w_q, sinks, w_o, w_gu, w_down, ple_g, w_ple_gate, w_ple_proj, ple_post_g, loss_target, m_pre_mix_g, m_post_mix_g, m_pre_ffn_g, m_post_ffn_g, m_pool_w, m_pool_scale, m_kv_g, m_w_kv, m_w_q, m_sinks, m_w_o, m_w_gu, m_w_down, m_ple_g, m_w_ple_gate, m_w_ple_proj, m_ple_post_g, v_pre_mix_g, v_post_mix_g, v_pre_ffn_g, v_post_ffn_g, v_pool_w, v_pool_scale, v_kv_g, v_w_kv, v_w_q, v_sinks, v_w_o, v_w_gu, v_w_down, v_ple_g, v_w_ple_gate, v_w_ple_proj, v_ple_post_g):
    weights = dict(pre_mix_g=pre_mix_g, post_mix_g=post_mix_g, pre_ffn_g=pre_ffn_g, post_ffn_g=post_ffn_g,
                   pool_w=pool_w, pool_scale=pool_scale, kv_g=kv_g, w_kv=w_kv, w_q=w_q, sinks=sinks, w_o=w_o,
                   w_gu=w_gu, w_down=w_down, ple_g=ple_g, w_ple_gate=w_ple_gate, w_ple_proj=w_ple_proj,
                   ple_post_g=ple_post_g)
    m_in = dict(pre_mix_g=m_pre_mix_g, post_mix_g=m_post_mix_g, pre_ffn_g=m_pre_ffn_g, post_ffn_g=m_post_ffn_g,
                pool_w=m_pool_w, pool_scale=m_pool_scale, kv_g=m_kv_g, w_kv=m_w_kv, w_q=m_w_q, sinks=m_sinks,
                w_o=m_w_o, w_gu=m_w_gu, w_down=m_w_down, ple_g=m_ple_g, w_ple_gate=m_w_ple_gate,
                w_ple_proj=m_w_ple_proj, ple_post_g=m_ple_post_g)
    v_in = dict(pre_mix_g=v_pre_mix_g, post_mix_g=v_post_mix_g, pre_ffn_g=v_pre_ffn_g, post_ffn_g=v_post_ffn_g,
                pool_w=v_pool_w, pool_scale=v_pool_scale, kv_g=v_kv_g, w_kv=v_w_kv, w_q=v_w_q, sinks=v_sinks,
                w_o=v_w_o, w_gu=v_w_gu, w_down=v_w_down, ple_g=v_ple_g, w_ple_gate=v_w_ple_gate,
                w_ple_proj=v_w_ple_proj, ple_post_g=v_ple_post_g)
    order = ["pre_mix_g", "post_mix_g", "pre_ffn_g", "post_ffn_g", "pool_w", "pool_scale", "kv_g", "w_kv", "w_q",
             "sinks", "w_o", "w_gu", "w_down", "ple_g", "w_ple_gate", "w_ple_proj", "ple_post_g"]

    kc = jnp.stack([2 * lax.axis_index("x") + lax.axis_index("y"), lax.axis_index("c")]).astype(jnp.int32)
    s_len = x.shape[1]
    x2d = x.reshape(s_len, D)
    p3d = p.reshape(2, s_len, PLE)
    target = loss_target.reshape(s_len, D)
    kv_g2d = kv_g.reshape(1, D)
    gains = {nm: weights[nm] for nm in GAIN_ROWS}

    def shard_view(src, a):
        t = next(t for t in BIGS.values() if t.src == src)
        return a.reshape(-1, t.R, t.C)

    first, second = ["pool_w", "pool_scale"], ["w_gu0", "w_down0"]
    rest = [nm for nm in BIGS if nm not in first + second]
    specs = dict(BIGS, pool_scale=POOL_SCALE)
    placed = {}

    def place_job(nm):
        if nm == "pool_scale":
            return _place_job(POOL_SCALE, pool_scale.reshape(1, 1, D // N_CHIPS), F32)
        return _place_job(BIGS[nm], shard_view(BIGS[nm].src, weights[BIGS[nm].src]))

    def gather(names, rows=None):
        rows = rows or {}
        parts = [(specs[nm],) + tuple(rows.get(nm, (0, specs[nm].R))) for nm in names]
        return _gather_rider(parts, [placed[nm] for nm in names])

    def take(names, results):
        for nm, a in zip(names, results):
            placed[nm] = a

    def weight(nm):
        return _compute_layout(specs[nm], placed[nm])

    take(first, [r[0] for r in _multi_call("place_pool", [place_job(nm) for nm in first], kc)])
    cast, got = _multi_call("place_ffn0", [place_job(nm) for nm in second], kc, rider=gather(first))
    take(second, [r[0] for r in cast])
    take(first, got)
    jobs = [place_job(nm) for nm in rest]
    jobs.append(_mixa_fwd_job(x2d, gains["pre_mix_g"], weight("pool_w"), weight("pool_scale"), gains["post_mix_g"]))
    results, got = _multi_call("cast_and_mixa_fwd", jobs, kc, rider=gather(second))
    take(rest, [r[0] for r in results[:-1]])
    take(second, got)
    y0, x1 = results[-1]

    ride = ["w_ple_gate0", "w_ple_proj0", "w_q", "w_kv", "w_o", "w_gu1"]
    (f0, x2, g0, u0), got = _ffn_fwd(0, x1, gains["pre_ffn_g"], weight("w_gu0"), weight("w_down0"), gains["post_ffn_g"],
                             rider=gather(ride, {"w_gu1": (0, 320)}))
    take(ride, got)

    ride = ["w_ple_gate1", "w_ple_proj1", "w_gu1"]
    (z0, pe0, x3, q, kv), got = _ple_fwd(
        0, x2, p3d, gains["ple_g"], weight("w_ple_gate0"), weight("w_ple_proj0"), gains["ple_post_g"],
        qkv=(gains["pre_mix_g"], kv_g2d, weight("w_q"), weight("w_kv")),
        rider=gather(ride, {"w_gu1": (320, 704)}))
    take(ride, got)

    ride = ["w_down1", "w_gu1"]
    (attn, y1, x4), got = _attn_fwd(q, kv, sinks, x3, weight("w_o"), gains["post_mix_g"],
                                    rider=gather(ride, {"w_gu1": (704, D)}))
    take(ride, got)

    (f1, x5, g1, u1), _ = _ffn_fwd(1, x4, gains["pre_ffn_g"], weight("w_gu1"), weight("w_down1"), gains["post_ffn_g"])
    (z1, pe1, dx6, loss_row), _ = _ple_fwd(1, x5, p3d, gains["ple_g"], weight("w_ple_gate1"), weight("w_ple_proj1"),
                                           gains["ple_post_g"], target=target)

    local = {}
    landed = {}
    fused = {}

    def local_grads(names):
        return [local[nm].reshape(_full_shape(BIGS[nm])) for nm in names]

    def pair_exchange(names):
        return _pair_exchange_rider([BIGS[nm] for nm in names], local_grads(names))

    def pair_sum(tag, names, lands):
        jobs = [_pair_sum_job(BIGS[nm], g, l) for nm, g, l in zip(names, local_grads(names), lands)]
        return [r[0] for r in _multi_call(f"pair_sum_{tag}", jobs, kc)]

    def scatter(names, sums):
        return _scatter_rider([BIGS[nm] for nm in names], sums)

    def keep(names, sums, got):
        for nm, s, l in zip(names, sums, got):
            landed[nm] = (s, l)

    (dx5, local["w_ple_gate1"], local["w_ple_proj1"], d_ple1, d_plepost1), _ = _ple_bwd(
        1, dx6, x5, z1, pe1, p3d, gains["ple_g"], weight("w_ple_gate1"), gains["ple_post_g"])

    group_a = ["w_ple_gate1", "w_ple_proj1"]
    (dx4, d_preffn1, d_postffn1, *scattered), lands_a = _ffn_bwd(
        1, dx5, x4, f1, g1, u1, gains["pre_ffn_g"], weight("w_gu1"), weight("w_down1"), gains["post_ffn_g"], kc,
        rider=pair_exchange(group_a))
    fused["w_gu1"], fused["w_down1"] = scattered[0:2], scattered[2:4]
    sums_a = pair_sum("a", group_a, lands_a)

    (dq, dkv, local["w_o"], d_postmix1, d_sinks), _ = _attn_bwd(
        dx4, y1, attn, q, kv, sinks, weight("w_o"), gains["post_mix_g"])
    dx3, local["w_q"], local["w_kv"], d_premix1, d_kvg = _qkv_bwd(
        dq, dkv, x3, dx4, gains["pre_mix_g"], kv_g2d, weight("w_q"), weight("w_kv"))

    group_b = ["w_o", "w_q", "w_kv"]
    (dx2, local["w_ple_gate0"], local["w_ple_proj0"], d_ple0, d_plepost0), lands_b = _ple_bwd(
        0, dx3, x2, z0, pe0, p3d, gains["ple_g"], weight("w_ple_gate0"), gains["ple_post_g"],
        rider=pair_exchange(group_b))
    sums_b = pair_sum("b", group_b, lands_b)

    group_c = ["w_ple_gate0", "w_ple_proj0"]
    (dx1, d_preffn0, d_postffn0, *scattered), got = _ffn_bwd(
        0, dx2, x1, f0, g0, u0, gains["pre_ffn_g"], weight("w_gu0"), weight("w_down0"), gains["post_ffn_g"], kc,
        rider=_both(pair_exchange(group_c), scatter(group_a + group_b, sums_a + sums_b)))
    fused["w_gu0"], fused["w_down0"] = scattered[0:2], scattered[2:4]
    sums_c = pair_sum("c", group_c, got[:len(group_c)])
    keep(group_a + group_b, sums_a + sums_b, got[len(group_c):])

    layers_of = lambda src: [t for t in BIGS.values() if t.src == src]
    own_scatter = ["w_gu", "w_down"]
    early = own_scatter + ["w_q", "w_o", "w_kv"]
    late = ["w_ple_gate", "w_ple_proj"]
    by_cols = lambda srcs: [src == "w_down" for src in srcs]
    jobs = [_chip_sum_fused_job(layers_of(src), fused, by_cols=src == "w_down") for src in own_scatter]
    jobs += [_chip_sum_job(layers_of(src), landed) for src in early if src not in own_scatter]
    halves = [r[0] for r in _multi_call("chip_sum_early", jobs, kc)]
    (dx0, d_pool, d_scale, d_postmix0, d_premix0), got = _mixa_bwd(
        dx1, x2d, y0, gains["pre_mix_g"], weight("pool_w"), weight("pool_scale"), gains["post_mix_g"],
        rider=_both(scatter(group_c, sums_c), _share_rider(halves, by_cols(early))))
    keep(group_c, sums_c, got[:len(group_c)])
    full_grads = dict(zip(early, got[len(group_c):]))

    rows = [d_premix0, d_premix1, d_postmix0, d_postmix1, d_preffn0, d_preffn1, d_postffn0, d_postffn1,
            d_ple0, d_ple1, d_plepost0, d_plepost1, d_kvg, d_scale, d_sinks, loss_row]
    as2d = lambda a: a.reshape(1, D) if a.ndim == 1 else a
    (tot, g_pool), _ = _small_all_reduce(rows, d_pool)
    loss, small = _small_adamw(tot, kc, {nm: as2d(weights[nm]) for nm in SMALL_NAMES},
                               {nm: as2d(m_in[nm]) for nm in SMALL_NAMES},
                               {nm: as2d(v_in[nm]) for nm in SMALL_NAMES})

    halves = [r[0] for r in _multi_call("chip_sum_late", [_chip_sum_job(layers_of(src), landed) for src in late], kc)]
    full_grads.update(zip(late, _run("grads_pair_share", _share_rider(halves, by_cols(late)))))
    full_grads["pool_w"] = g_pool
    others = [src for src in BIG_SOURCES if src not in own_scatter and src != "pool_w"]

    def adam_args(src):
        return (layers_of(src)[0].rb, shard_view(src, weights[src]), full_grads[src],
                shard_view(src, m_in[src]), shard_view(src, v_in[src]))

    out = {"grad": {}, "delta": {}, "new_m": {}, "new_v": {}}
    results = {src: _adamw(src, *adam_args(src)) for src in own_scatter}
    rest_srcs = others + ["pool_w"]
    results.update(zip(rest_srcs, _multi_call("adamw_rest", [_adamw_job(*adam_args(src)) for src in rest_srcs], kc)))
    for src in BIG_SOURCES:
        shape = weights[src].shape
        for kind, a in zip(("grad", "delta", "new_m", "new_v"), results[src]):
            out[kind][src] = a.reshape(shape)
    for nm in SMALL_NAMES:
        shape = weights[nm].shape
        for kind, a in zip(("grad", "delta", "new_m", "new_v"), small[nm]):
            out[kind][nm] = a.reshape(shape)

    return (loss.reshape(()), dx0.reshape(x.shape),
            *[out["grad"][nm] for nm in order], *[out["delta"][nm] for nm in order],
            *[out["new_m"][nm] for nm in order], *[out["new_v"][nm] for nm in order])
```

```python
import collections

import jax
import jax.numpy as jnp
from jax import lax
from jax.experimental import pallas as pl
from jax.experimental.pallas import tpu as pltpu

D = 1024
FF = 2816
N_HEADS = 16
HEAD_DIM = 64
N_KV_HEADS = 4
GQA = N_HEADS // N_KV_HEADS
KVD = N_KV_HEADS * HEAD_DIM
PLE = 256
BLK = 128
WINDOWS = (2, 4, 8, 16)
POOL_G = 256
HALO = 16
EPS = 1e-6
NEG_INF = -1e30
ATT_SCALE = HEAD_DIM ** -0.5
SLOPES = tuple(2.0 ** (-8.0 * (h + 1) / N_HEADS) for h in range(N_HEADS))
N_CHIPS = 4
N_DEV = 8

LR, B1, B2, AEPS, WD, STEP = 0.001, 0.9, 0.999, 1e-08, 0.01, 10
BC1 = 1.0 - B1 ** STEP
BC2 = 1.0 - B2 ** STEP

BF = jnp.bfloat16
F32 = jnp.float32
MESH = pl.DeviceIdType.MESH
VMEM_LIMIT_V7X = 58 * 1024 * 1024
TM = 256
TM_FFN_BWD = 512
FF_CHUNK = 256
FF_HALF = FF // 2

VSPEC = pl.BlockSpec(memory_space=pltpu.VMEM)
SSPEC = pl.BlockSpec(memory_space=pltpu.SMEM)
ANYSPEC = pl.BlockSpec(memory_space=pl.ANY)


def _params(n_grid=0):
    sem = ("arbitrary",) * n_grid if n_grid else None
    return pltpu.CompilerParams(dimension_semantics=sem, vmem_limit_bytes=VMEM_LIMIT_V7X)


def _sds(shape, dtype=F32):
    return jax.ShapeDtypeStruct(tuple(shape), dtype)


Rider = collections.namedtuple("Rider", "arrays out_shapes aliases scratch start mid finish")
MID_NUM, MID_DEN = 5, 8


def _call(body, *, name, grid, in_specs, out_specs, out_shape, args, scratch_shapes=(), rider=None, prefetch=None):
    ni, no, ns = len(in_specs), len(out_specs), len(scratch_shapes)
    npre = 0 if prefetch is None else 1
    pre = [] if prefetch is None else [prefetch]
    if rider is None:
        rider = Rider([], [], {}, [], None, None, None)
    ri, ro = len(rider.arrays), len(rider.out_shapes)

    def full(*refs):
        pre_refs, refs = refs[:npre], refs[npre:]
        ins, refs = refs[:ni], refs[ni:]
        rins, refs = refs[:ri], refs[ri:]
        outs, refs = refs[:no], refs[no:]
        routs, refs = refs[:ro], refs[ro:]
        scr, rscr = refs[:ns], refs[ns:]
        ids = [pl.program_id(a) for a in range(len(grid))]
        first = ids[0] == 0
        last = ids[0] == grid[0] - 1
        for a in range(1, len(grid)):
            first = first & (ids[a] == 0)
            last = last & (ids[a] == grid[a] - 1)

        if rider.start is not None:
            @pl.when(first)
            def _():
                rider.start(rins, routs, rscr)

        if rider.mid is not None:
            assert len(grid) == 1

            @pl.when(ids[0] == (grid[0] * MID_NUM) // MID_DEN)
            def _():
                rider.mid(rins, routs, rscr)

        body(*pre_refs, *ins, *outs, *scr)

        if rider.finish is not None:
            @pl.when(last)
            def _():
                rider.finish(rins, routs, rscr)

    outs = pl.pallas_call(
        full, name=name,
        grid_spec=pltpu.PrefetchScalarGridSpec(
            num_scalar_prefetch=npre, grid=grid,
            in_specs=list(in_specs) + [ANYSPEC] * ri, out_specs=list(out_specs) + [ANYSPEC] * ro,
            scratch_shapes=list(scratch_shapes) + list(rider.scratch)),
        out_shape=list(out_shape) + list(rider.out_shapes),
        input_output_aliases={npre + ni + a: no + b for a, b in rider.aliases.items()},
        compiler_params=_params(len(grid)))(*pre, *args, *rider.arrays)
    return list(outs[:no]), list(outs[no:])


def _run(name, rider):
    ri = len(rider.arrays)

    def body(*refs):
        rins, routs, rscr = refs[:ri], refs[ri:ri + len(rider.out_shapes)], refs[ri + len(rider.out_shapes):]
        rider.start(rins, routs, rscr)
        if rider.mid is not None:
            rider.mid(rins, routs, rscr)
        rider.finish(rins, routs, rscr)

    return pl.pallas_call(
        body, name=name, in_specs=[ANYSPEC] * ri, out_specs=[ANYSPEC] * len(rider.out_shapes),
        out_shape=list(rider.out_shapes), scratch_shapes=list(rider.scratch),
        input_output_aliases=dict(rider.aliases), compiler_params=_params())(*rider.arrays)


Job = collections.namedtuple("Job", "steps ins outs fn")


def _multi_call(name, jobs, kc, rider=None):
    n = max(job.steps for job in jobs)

    def clamped(index, steps):
        return lambda s, kc_ref: index(jnp.minimum(s, steps - 1), kc_ref)

    in_specs, out_specs, out_shape, args = [], [], [], []
    for job in jobs:
        for arr, block, index, *single in job.ins:
            mode = dict(pipeline_mode=pl.Buffered(1)) if single and single[0] else {}
            in_specs.append(pl.BlockSpec(block, clamped(index, job.steps), **mode))
            args.append(arr)
        for sds, block, index in job.outs:
            out_specs.append(pl.BlockSpec(block, clamped(index, job.steps)))
            out_shape.append(sds)
    n_in = len(args)

    def body(kc_ref, *refs):
        s = pl.program_id(0)
        i0, o0 = 0, n_in
        for job in jobs:
            ins, outs = refs[i0:i0 + len(job.ins)], refs[o0:o0 + len(job.outs)]
            i0, o0 = i0 + len(job.ins), o0 + len(job.outs)

            @pl.when(s < job.steps)
            def _():
                job.fn(s, kc_ref, ins, outs)

    outs, routs = _call(body, name=name, grid=(n,), in_specs=in_specs, out_specs=out_specs, out_shape=out_shape,
                        args=args, prefetch=kc, rider=rider)
    res, o0 = [], 0
    for job in jobs:
        res.append(outs[o0:o0 + len(job.outs)])
        o0 += len(job.outs)
    return res if rider is None else (res, routs)


def _rms_fwd(x, g):
    r = lax.rsqrt(jnp.mean(x * x, axis=-1, keepdims=True) + EPS)
    return x * r * g


def _rms_bwd(x, g, dy):
    r = lax.rsqrt(jnp.mean(x * x, axis=-1, keepdims=True) + EPS)
    xn = x * r
    dxn = dy * g
    dx = r * (dxn - xn * jnp.mean(dxn * xn, axis=-1, keepdims=True))
    return dx, dy * xn


def _rowsum(a):
    return jnp.sum(a, axis=0, keepdims=True)


def _sigmoid(z):
    return 1.0 / (1.0 + jnp.exp(-z))


def _dot(a, b):
    return jnp.dot(a, b, preferred_element_type=F32)


def _dot_nt(a, b):
    return lax.dot_general(a, b, (((1,), (1,)), ((), ())), preferred_element_type=F32)


def _dot_tn(a, b):
    return lax.dot_general(a, b, (((0,), (0,)), ((), ())), preferred_element_type=F32)


def _row_spec(tm, width=D):
    return pl.BlockSpec((tm, width), lambda i: (i, 0))


def _const_spec(shape):
    zeros = (0,) * len(shape)
    return pl.BlockSpec(tuple(shape), lambda *_: zeros)


def _pool_delta(he, pos):
    out = []
    for gi, w in enumerate(WINDOWS):
        hg = he[:, gi * POOL_G:(gi + 1) * POOL_G]
        s = hg
        k = 1
        while k < w:
            s = s + pltpu.roll(s, k, 0)
            k *= 2
        cnt = jnp.maximum(jnp.minimum(pos + 1, w), 1).astype(F32)
        out.append(s / cnt - hg)
    return out


def _load_with_halo_before(x_ref, i, tm):
    r0 = pl.multiple_of(i * tm, tm)
    hs = pl.multiple_of(jnp.maximum(i * tm - HALO, 0), 8)
    xh = jnp.where(i > 0, x_ref[pl.ds(hs, HALO), :], 0.0)
    xt = x_ref[pl.ds(r0, tm), :]
    return xt, jnp.concatenate([xh, xt], axis=0)


def _mixa_fwd_job(x, pre_g, pool_w, pool_scale, post_g):
    s_len = x.shape[0]

    def fn(i, kc_ref, ins, outs):
        x_ref, pg_ref, w_ref, sc_ref, qg_ref = ins
        y_ref, x1_ref = outs
        xt, xe = _load_with_halo_before(x_ref, i, TM)
        he = _rms_fwd(xe, pg_ref[0:1, :])
        pos = i * TM - HALO + lax.broadcasted_iota(jnp.int32, (TM + HALO, 1), 0)
        ds = _pool_delta(he, pos)
        ys = [_dot(ds[gi][HALO:, :].astype(BF), w_ref[gi]) for gi in range(len(WINDOWS))]
        y = jnp.concatenate(ys, axis=1) * sc_ref[...]
        y_ref[...] = y
        x1_ref[...] = xt + _rms_fwd(y, qg_ref[0:1, :])

    def whole(a):
        zeros = (0,) * a.ndim
        return (a, a.shape, lambda j, kc_ref: zeros, True)

    rows = lambda j, kc_ref: (j, 0)
    return Job(s_len // TM, [whole(a) for a in (x, pre_g, pool_w, pool_scale, post_g)],
               [(_sds((s_len, D)), (TM, D), rows), (_sds((s_len, D)), (TM, D), rows)], fn)


def _mixa_bwd(dx1, x, y, pre_g, pool_w, pool_scale, post_g, rider=None):
    s_len = x.shape[0]
    n = s_len // TM
    ng = len(WINDOWS)

    def body(dx_ref, x_ref, y_ref, pg_ref, w_ref, sc_ref, qg_ref,
             dx0_ref, dw_ref, dsc_ref, dqg_ref, dpg_ref, wacc):
        i = pl.program_id(0)

        @pl.when(i == 0)
        def _():
            wacc[...] = jnp.zeros_like(wacc)
            dsc_ref[...] = jnp.zeros_like(dsc_ref)
            dqg_ref[...] = jnp.zeros_like(dqg_ref)
            dpg_ref[...] = jnp.zeros_like(dpg_ref)

        r0 = pl.multiple_of(i * TM, TM)
        xt, xe = _load_with_halo_before(x_ref, i, TM)
        he = _rms_fwd(xe, pg_ref[0:1, :])
        pos_b = i * TM - HALO + lax.broadcasted_iota(jnp.int32, (TM + HALO, 1), 0)
        ds = _pool_delta(he, pos_b)

        last = i == n - 1
        a0 = pl.multiple_of(jnp.minimum(i * TM + TM, s_len - HALO), 8)
        ye = jnp.concatenate([y_ref[pl.ds(r0, TM), :], y_ref[pl.ds(a0, HALO), :]], axis=0)
        dt = dx_ref[pl.ds(r0, TM), :]
        de = jnp.concatenate([dt, jnp.where(last, 0.0, dx_ref[pl.ds(a0, HALO), :])], axis=0)
        dye, prod = _rms_bwd(ye, qg_ref[0:1, :], de)
        dqg_ref[...] += _rowsum(prod[:TM, :])
        dys = dye * sc_ref[...]
        pos_a = i * TM + lax.broadcasted_iota(jnp.int32, (TM + HALO, 1), 0)

        dhs, dscs = [], []
        for gi, w in enumerate(WINDOWS):
            sl = slice(gi * POOL_G, (gi + 1) * POOL_G)
            wg = w_ref[gi]
            dys_g = dys[:, sl].astype(BF)
            d_g = ds[gi][HALO:, :].astype(BF)
            ypre = _dot(d_g, wg)
            dscs.append(_rowsum(dye[:TM, sl] * ypre))
            wacc[gi] += _dot_tn(d_g, dys_g[:TM, :])
            dd = _dot_nt(dys_g, wg)
            cnt = jnp.minimum(pos_a + 1, w).astype(F32)
            a = dd / cnt
            k = 1
            while k < w:
                a = a + pltpu.roll(a, TM + HALO - k, 0)
                k *= 2
            dhs.append(a[:TM, :] - dd[:TM, :])
        dsc_ref[...] += jnp.concatenate(dscs, axis=1)
        dh = jnp.concatenate(dhs, axis=1)
        dxp, prod2 = _rms_bwd(xt, pg_ref[0:1, :], dh)
        dpg_ref[...] += _rowsum(prod2)
        dx0_ref[...] = dt + dxp

        @pl.when(last)
        def _():
            dw_ref[...] = wacc[...].astype(BF)

    return _call(
        body, name="mixa_bwd", grid=(n,), in_specs=[VSPEC] * 7,
        out_specs=[_row_spec(TM), _const_spec((ng, POOL_G, POOL_G)), _const_spec((1, D)),
                   _const_spec((1, D)), _const_spec((1, D))],
        out_shape=[_sds((s_len, D)), _sds((ng, POOL_G, POOL_G), BF), _sds((1, D)), _sds((1, D)), _sds((1, D))],
        scratch_shapes=[pltpu.VMEM((ng, POOL_G, POOL_G), F32)],
        args=[dx1, x, y, pre_g, pool_w, pool_scale, post_g], rider=rider)


def _ffn_fwd(layer, x1, pre_g, wgu, wd, post_g, rider=None):
    s_len = x1.shape[0]

    def body(x_ref, pg_ref, wgu_ref, wd_ref, qg_ref, f_ref, x2_ref, g_ref, u_ref):
        x = x_ref[...]
        h = _rms_fwd(x, pg_ref[layer:layer + 1, :]).astype(BF)
        f = jnp.zeros((TM, D), F32)
        for c in range(FF // FF_HALF):
            cols = slice(c * FF_HALF, (c + 1) * FF_HALF)
            g = _dot(h, wgu_ref[0, :, cols])
            u = _dot(h, wgu_ref[1, :, cols])
            g_ref[:, cols] = g.astype(BF)
            u_ref[:, cols] = u.astype(BF)
            act = g * _sigmoid(g) * u
            f = f + _dot(act.astype(BF), wd_ref[cols, :])
        f_ref[...] = f
        x2_ref[...] = x + _rms_fwd(f, qg_ref[layer:layer + 1, :])

    return _call(body, name=f"ffn_fwd{layer}", grid=(s_len // TM,),
                 in_specs=[_row_spec(TM), VSPEC, VSPEC, VSPEC, VSPEC],
                 out_specs=[_row_spec(TM), _row_spec(TM), _row_spec(TM, FF), _row_spec(TM, FF)],
                 out_shape=[_sds((s_len, D)), _sds((s_len, D)), _sds((s_len, FF), BF), _sds((s_len, FF), BF)],
                 args=[x1, pre_g, wgu, wd, post_g], rider=rider)


GU_PIECE = 128
DN_PIECE = 64
DN_SLOT = FF // N_CHIPS
HALF_D = D // 2
CHUNK_STRIDE = 6
CHUNK_START = (1, 7, 4, 10)


def _ffn_bwd(layer, dx2, x1, f, g_pre, u_pre, pre_g, wgu, wd, post_g, kc, rider=None):
    s_len = x1.shape[0]
    tm = TM_FFN_BWD
    n = s_len // tm
    nc = FF // FF_CHUNK
    n_gu, n_dn = FF_CHUNK // GU_PIECE, FF_CHUNK // DN_PIECE
    n_pieces = 2 * n_gu + n_dn
    n_blk = FF_HALF // GU_PIECE

    def edge_rows(c, i, kc_ref):
        return (jnp.where((c == 0) | (c == nc - 1), i, n - 1), 0)

    def chunk_at(c, kc_ref):
        k = kc_ref[0]
        start = jnp.where(k == 0, CHUNK_START[0], jnp.where(k == 1, CHUNK_START[1],
                                                            jnp.where(k == 2, CHUNK_START[2], CHUNK_START[3])))
        return ((c + start) * CHUNK_STRIDE) % nc

    def exchange(kc_ref, c, accg, accu, accd, own_gu_ref, land_gu_ref, own_dn_ref, land_dn_ref,
                 pl_gu, pl_dn, sib_gu, sib_dn, mine_gu, mine_dn, sum_gu, sum_dn,
                 psend, precv, ssend, lsem, rrecv):
        x, y, core = lax.axis_index("x"), lax.axis_index("y"), lax.axis_index("c")
        lower = core == 0

        def pair_copy(cc, part):
            p = cc % 2
            src, dst = ((sib_gu, pl_gu), (sib_dn, pl_dn))[part]
            return pltpu.make_async_remote_copy(src.at[p], dst.at[cc], psend.at[p, part], precv.at[cc, part],
                                                device_id=(x, y, 1 - core), device_id_type=MESH)

        def scatter(cc, wait):
            p = cc % 2
            hidden = chunk_at(cc, kc_ref) * FF_CHUNK

            assert n_gu == 2
            k0, k1 = hidden // FF_HALF, (hidden + GU_PIECE) // FF_HALF
            blk = (hidden - k0 * FF_HALF) // GU_PIECE
            for gu in range(2):
                @pl.when(k0 == k1)
                def _():
                    piece(p, wait, 2 * gu, sum_gu.at[p, gu], k0 + 2 * gu, 0, (pl.ds(blk, 2),))

                @pl.when(k0 != k1)
                def _():
                    piece(p, wait, 2 * gu, sum_gu.at[p, gu, 0], k0 + 2 * gu, 0, (blk,))
                    piece(p, wait, 2 * gu + 1, sum_gu.at[p, gu, 1], k1 + 2 * gu, 0, (0,))

            kd = hidden // DN_SLOT
            off = pl.multiple_of(hidden - kd * DN_SLOT, DN_PIECE)
            m = jnp.minimum((DN_SLOT - off) // DN_PIECE, n_dn)
            for mm in range(1, n_dn + 1):
                @pl.when(m == mm)
                def _():
                    rows = mm * DN_PIECE
                    piece(p, wait, 2 * n_gu, sum_dn.at[p, pl.ds(0, rows), :], kd, 1, (pl.ds(off, rows), slice(None)))
                    if mm < n_dn:
                        piece(p, wait, 2 * n_gu + 1, sum_dn.at[p, pl.ds(rows, FF_CHUNK - rows), :], kd + 1, 1,
                              (pl.ds(0, FF_CHUNK - rows), slice(None)))

        def piece(p, wait, pi, src, k, t, where):
            own_ref, land_ref = ((own_gu_ref, land_gu_ref), (own_dn_ref, land_dn_ref))[t]
            kx, ky = k // 2, k % 2
            fx, fy = (kx != x).astype(jnp.int32), (ky != y).astype(jnp.int32)
            local = (fx + fy) == 0
            j = jnp.maximum(fx + 2 * fy - 1, 0)

            @pl.when(local)
            def _():
                cp = pltpu.make_async_copy(src, own_ref.at[where], lsem.at[p, pi])
                if wait:
                    cp.wait()
                else:
                    cp.start()

            @pl.when(jnp.logical_not(local))
            def _():
                cp = pltpu.make_async_remote_copy(src, land_ref.at[(j,) + where], ssend.at[p, pi],
                                                  rrecv.at[t, j], device_id=(kx, ky, core), device_id_type=MESH)
                if wait:
                    cp.wait_send()
                else:
                    cp.start()

        def add_and_scatter(cc):
            p = cc % 2
            pair_copy(cc, 0).wait_recv()
            pair_copy(cc, 1).wait_recv()
            s_gu = (mine_gu[...] + pl_gu[cc].astype(F32)).astype(BF)
            for hc in range(n_gu):
                sum_gu[p, :, hc] = s_gu[:, :, hc * GU_PIECE:(hc + 1) * GU_PIECE]
            sum_dn[p] = (mine_dn[...] + pl_dn[cc].astype(F32)).astype(BF)
            scatter(cc, wait=False)

        @pl.when(c >= 1)
        def _():
            @pl.when(c >= 3)
            def _():
                scatter(c - 3, wait=True)
            add_and_scatter(c - 1)

        @pl.when(c >= 2)
        def _():
            pair_copy(c - 2, 0).wait_send()
            pair_copy(c - 2, 1).wait_send()

        p = c % 2
        my_rows = pl.ds(pl.multiple_of(core * HALF_D, HALF_D), HALF_D)
        sib_rows = pl.ds(pl.multiple_of((1 - core) * HALF_D, HALF_D), HALF_D)
        d_v = accd[...]
        sib_gu[p, 0] = accg[sib_rows, :].astype(BF)
        sib_gu[p, 1] = accu[sib_rows, :].astype(BF)
        sib_dn[p] = jnp.where(lower, d_v[:, HALF_D:], d_v[:, :HALF_D]).astype(BF)
        mine_gu[0] = accg[my_rows, :]
        mine_gu[1] = accu[my_rows, :]
        mine_dn[...] = jnp.where(lower, d_v[:, :HALF_D], d_v[:, HALF_D:])
        pair_copy(c, 0).start()
        pair_copy(c, 1).start()

        @pl.when(c == nc - 1)
        def _():
            scatter(nc - 3, wait=True)
            add_and_scatter(nc - 1)
            for cc in (nc - 2, nc - 1):
                pair_copy(cc, 0).wait_send()
                pair_copy(cc, 1).wait_send()
                scatter(cc, wait=True)
            for t, land_ref in enumerate((land_gu_ref, land_dn_ref)):
                for j in range(N_CHIPS - 1):
                    pltpu.make_async_remote_copy(land_ref.at[j], land_ref.at[j], ssend.at[0, 0], rrecv.at[t, j],
                                                 device_id=(x, y, core), device_id_type=MESH).wait_recv()

    def body(kc_ref, dx_ref, x_ref, f_ref, gp_ref, up_ref, pg_ref, wgu_ref, wd_ref, qg_ref,
             dx1_ref, dpg_ref, dqg_ref, own_gu_ref, land_gu_ref, own_dn_ref, land_dn_ref,
             h_s, df_s, dh_s, accg, accu, accd, *comm):
        c = pl.program_id(0)
        i = pl.program_id(1)
        rows = pl.ds(pl.multiple_of(i * tm, tm), tm)
        pg = pg_ref[layer:layer + 1, :]

        @pl.when((c == 0) & (i == 0))
        def _():
            dpg_ref[...] = jnp.zeros_like(dpg_ref)
            dqg_ref[...] = jnp.zeros_like(dqg_ref)

        @pl.when(c == 0)
        def _():
            h_s[rows, :] = _rms_fwd(x_ref[...], pg).astype(BF)
            df, prod = _rms_bwd(f_ref[...], qg_ref[layer:layer + 1, :], dx_ref[...])
            df_s[rows, :] = df.astype(BF)
            dqg_ref[...] += _rowsum(prod)

        @pl.when(i == 0)
        def _():
            accg[...] = jnp.zeros_like(accg)
            accu[...] = jnp.zeros_like(accu)
            accd[...] = jnp.zeros_like(accd)

        h = h_s[rows, :]
        df = df_s[rows, :]
        wg = wgu_ref[0]
        wu = wgu_ref[1]
        g = gp_ref[...].astype(F32)
        u = up_ref[...].astype(F32)
        sg = _sigmoid(g)
        a = g * sg
        dact = _dot_nt(df, wd_ref[...])
        accd[...] += _dot_tn((a * u).astype(BF), df)
        du = (dact * a).astype(BF)
        dg = (dact * u * (sg * (1.0 + g * (1.0 - sg)))).astype(BF)
        accg[...] += _dot_tn(h, dg)
        accu[...] += _dot_tn(h, du)
        dh = _dot_nt(dg, wg) + _dot_nt(du, wu)

        @pl.when(c == 0)
        def _():
            dh_s[rows, :] = dh

        @pl.when((c > 0) & (c < nc - 1))
        def _():
            dh_s[rows, :] += dh

        @pl.when(c == nc - 1)
        def _():
            dxp, prod = _rms_bwd(x_ref[...], pg, dh_s[rows, :] + dh)
            dpg_ref[...] += _rowsum(prod)
            dx1_ref[...] = dx_ref[...] + dxp

        @pl.when(i == n - 1)
        def _():
            exchange(kc_ref, c, accg, accu, accd, own_gu_ref, land_gu_ref, own_dn_ref, land_dn_ref, *comm)

    dma = pltpu.SemaphoreType.DMA
    return _call(
        body, name=f"ffn_bwd{layer}", grid=(nc, n),
        in_specs=[pl.BlockSpec((tm, D), edge_rows), pl.BlockSpec((tm, D), edge_rows),
                  pl.BlockSpec((tm, D), lambda c, i, kc_ref: (jnp.where(c == 0, i, n - 1), 0),
                               pipeline_mode=pl.Buffered(1)),
                  pl.BlockSpec((tm, FF_CHUNK), lambda c, i, kc_ref: (i, chunk_at(c, kc_ref))),
                  pl.BlockSpec((tm, FF_CHUNK), lambda c, i, kc_ref: (i, chunk_at(c, kc_ref))),
                  VSPEC,
                  pl.BlockSpec((2, D, FF_CHUNK), lambda c, i, kc_ref: (0, 0, chunk_at(c, kc_ref))),
                  pl.BlockSpec((FF_CHUNK, D), lambda c, i, kc_ref: (chunk_at(c, kc_ref), 0)),
                  VSPEC],
        out_specs=[pl.BlockSpec((tm, D), lambda c, i, kc_ref: (jnp.where(c == nc - 1, i, 0), 0)),
                   _const_spec((1, D)), _const_spec((1, D)), ANYSPEC, ANYSPEC, ANYSPEC, ANYSPEC],
        out_shape=[_sds((s_len, D)), _sds((1, D)), _sds((1, D)),
                   _sds((n_blk, HALF_D, GU_PIECE), BF), _sds((N_CHIPS - 1, n_blk, HALF_D, GU_PIECE), BF),
                   _sds((DN_SLOT, HALF_D), BF), _sds((N_CHIPS - 1, DN_SLOT, HALF_D), BF)],
        scratch_shapes=[pltpu.VMEM((s_len, D), BF), pltpu.VMEM((s_len, D), BF), pltpu.VMEM((s_len, D), F32),
                        pltpu.VMEM((D, FF_CHUNK), F32), pltpu.VMEM((D, FF_CHUNK), F32),
                        pltpu.VMEM((FF_CHUNK, D), F32),
                        pltpu.VMEM((nc, 2, HALF_D, FF_CHUNK), BF), pltpu.VMEM((nc, FF_CHUNK, HALF_D), BF),
                        pltpu.VMEM((2, 2, HALF_D, FF_CHUNK), BF), pltpu.VMEM((2, FF_CHUNK, HALF_D), BF),
                        pltpu.VMEM((2, HALF_D, FF_CHUNK), F32), pltpu.VMEM((FF_CHUNK, HALF_D), F32),
                        pltpu.VMEM((2, 2, n_gu, HALF_D, GU_PIECE), BF), pltpu.VMEM((2, FF_CHUNK, HALF_D), BF),
                        dma((2, 2)), dma((nc, 2)), dma((2, n_pieces)), dma((2, n_pieces)), dma((2, N_CHIPS - 1))],
        args=[dx2, x1, f, g_pre, u_pre, pre_g, wgu, wd, post_g], rider=rider, prefetch=kc)


def _ple_fwd(layer, x2, p, ple_g, w_gate, w_proj, post_g, target=None, qkv=None, rider=None):
    s_len = x2.shape[0]
    final = target is not None
    assert not (final and qkv)

    def body(*refs):
        if final:
            x_ref, p_ref, g_ref, wg_ref, wp_ref, qg_ref, t_ref, z_ref, pe_ref, dx_ref, lv_ref = refs
        elif qkv:
            (x_ref, p_ref, g_ref, wg_ref, wp_ref, qg_ref, ng_ref, kg_ref, wq_ref, wkv_ref,
             z_ref, pe_ref, x3_ref, q_ref, kv_ref) = refs
        else:
            x_ref, p_ref, g_ref, wg_ref, wp_ref, qg_ref, z_ref, pe_ref, x3_ref = refs
        x = x_ref[...]
        r = _rms_fwd(x, g_ref[layer:layer + 1, :]).astype(BF)
        z = _dot(r, wg_ref[...])
        pe = _dot(p_ref[...].astype(BF), wp_ref[...])
        z_ref[...] = z
        pe_ref[...] = pe
        x3 = x + _rms_fwd(pe * _sigmoid(z), qg_ref[layer:layer + 1, :])
        if final:
            @pl.when(pl.program_id(0) == 0)
            def _():
                lv_ref[...] = jnp.zeros_like(lv_ref)
            err = x3 - t_ref[...]
            dx_ref[...] = err * (1.0 / D)
            lv_ref[...] += _rowsum(err * err)
        else:
            x3_ref[...] = x3
        if qkv:
            q_ref[...] = _dot(_rms_fwd(x3, ng_ref[layer + 1:layer + 2, :]).astype(BF), wq_ref[...]).astype(BF)
            kv_ref[...] = _dot(_rms_fwd(x3, kg_ref[...]).astype(BF), wkv_ref[...]).astype(BF)

    p_spec = pl.BlockSpec((None, TM, PLE), lambda i: (layer, i, 0))
    in_specs = [_row_spec(TM), p_spec, VSPEC, VSPEC, VSPEC, VSPEC]
    args = [x2, p, ple_g, w_gate, w_proj, post_g]
    out_specs = [_row_spec(TM), _row_spec(TM), _row_spec(TM)]
    out_shape = [_sds((s_len, D))] * 3
    if qkv:
        in_specs += [VSPEC] * 4
        args += list(qkv)
        out_specs += [_row_spec(TM), _row_spec(TM, 2 * KVD)]
        out_shape += [_sds((s_len, D), BF), _sds((s_len, 2 * KVD), BF)]
    if final:
        in_specs.append(_row_spec(TM))
        args.append(target)
        out_specs.append(_const_spec((1, D)))
        out_shape.append(_sds((1, D)))
    return _call(body, name=f"ple_fwd{layer}", grid=(s_len // TM,), in_specs=in_specs, out_specs=out_specs,
                 out_shape=out_shape, args=args, rider=rider)


def _ple_bwd(layer, dx3, x2, z, pe, p, ple_g, w_gate, post_g, rider=None):
    s_len = x2.shape[0]
    n = s_len // TM

    def body(dx_ref, x_ref, z_ref, pe_ref, p_ref, g_ref, wg_ref, qg_ref,
             dx2_ref, dwg_ref, dwp_ref, dg_ref, dqg_ref, gacc, pacc):
        i = pl.program_id(0)

        @pl.when(i == 0)
        def _():
            gacc[...] = jnp.zeros_like(gacc)
            pacc[...] = jnp.zeros_like(pacc)
            dg_ref[...] = jnp.zeros_like(dg_ref)
            dqg_ref[...] = jnp.zeros_like(dqg_ref)

        dx = dx_ref[...]
        x = x_ref[...]
        pe_v = pe_ref[...]
        gate = _sigmoid(z_ref[...])
        de, prod = _rms_bwd(pe_v * gate, qg_ref[layer:layer + 1, :], dx)
        dqg_ref[...] += _rowsum(prod)
        dpe = (de * gate).astype(BF)
        dz = (de * pe_v * gate * (1.0 - gate)).astype(BF)
        pacc[...] += _dot_tn(p_ref[...].astype(BF), dpe)
        g = g_ref[layer:layer + 1, :]
        r = _rms_fwd(x, g).astype(BF)
        gacc[...] += _dot_tn(r, dz)
        dr = _dot_nt(dz, wg_ref[...])
        dxp, prod2 = _rms_bwd(x, g, dr)
        dg_ref[...] += _rowsum(prod2)
        dx2_ref[...] = dx + dxp

        @pl.when(i == n - 1)
        def _():
            dwg_ref[...] = gacc[...].astype(BF)
            dwp_ref[...] = pacc[...].astype(BF)

    p_spec = pl.BlockSpec((None, TM, PLE), lambda i: (layer, i, 0))
    return _call(
        body, name=f"ple_bwd{layer}", grid=(n,),
        in_specs=[_row_spec(TM), _row_spec(TM), _row_spec(TM), _row_spec(TM), p_spec, VSPEC, VSPEC, VSPEC],
        out_specs=[_row_spec(TM), _const_spec((D, D)), _const_spec((PLE, D)), _const_spec((1, D)), _const_spec((1, D))],
        out_shape=[_sds((s_len, D)), _sds((D, D), BF), _sds((PLE, D), BF), _sds((1, D)), _sds((1, D))],
        scratch_shapes=[pltpu.VMEM((D, D), F32), pltpu.VMEM((PLE, D), F32)],
        args=[dx3, x2, z, pe, p, ple_g, w_gate, post_g], rider=rider)


def _qkv_bwd(dq, dkv, x3, dx4, q_g, kv_g, w_q, w_kv):
    s_len = x3.shape[0]
    n = s_len // TM

    def body(dq_ref, dkv_ref, x_ref, dx_ref, qg_ref, kg_ref, wq_ref, wkv_ref,
             dx3_ref, dwq_ref, dwkv_ref, dqg_ref, dkg_ref, qacc, kacc):
        i = pl.program_id(0)

        @pl.when(i == 0)
        def _():
            qacc[...] = jnp.zeros_like(qacc)
            kacc[...] = jnp.zeros_like(kacc)
            dqg_ref[...] = jnp.zeros_like(dqg_ref)
            dkg_ref[...] = jnp.zeros_like(dkg_ref)

        x = x_ref[...]
        qg = qg_ref[1:2, :]
        kg = kg_ref[...]
        dq_v = dq_ref[...]
        dkv_v = dkv_ref[...].astype(BF)
        qacc[...] += _dot_tn(_rms_fwd(x, qg).astype(BF), dq_v)
        kacc[...] += _dot_tn(_rms_fwd(x, kg).astype(BF), dkv_v)
        dxq, prod_q = _rms_bwd(x, qg, _dot_nt(dq_v, wq_ref[...]))
        dxk, prod_k = _rms_bwd(x, kg, _dot_nt(dkv_v, wkv_ref[...]))
        dqg_ref[...] += _rowsum(prod_q)
        dkg_ref[...] += _rowsum(prod_k)
        dx3_ref[...] = dx_ref[...] + dxq + dxk

        @pl.when(i == n - 1)
        def _():
            dwq_ref[...] = qacc[...].astype(BF)
            dwkv_ref[...] = kacc[...].astype(BF)

    outs, _ = _call(
        body, name="qkv_bwd", grid=(n,),
        in_specs=[_row_spec(TM), _row_spec(TM, 2 * KVD), _row_spec(TM), _row_spec(TM), VSPEC, VSPEC, VSPEC, VSPEC],
        out_specs=[_row_spec(TM), _const_spec((D, D)), _const_spec((D, 2 * KVD)),
                   _const_spec((1, D)), _const_spec((1, D))],
        out_shape=[_sds((s_len, D)), _sds((D, D), BF), _sds((D, 2 * KVD), BF), _sds((1, D)), _sds((1, D))],
        scratch_shapes=[pltpu.VMEM((D, D), F32), pltpu.VMEM((D, 2 * KVD), F32)],
        args=[dq, dkv, x3, dx4, q_g, kv_g, w_q, w_kv])
    return outs


def _attn_group(i, q, kvw, sink_ref, g):
    rows = GQA * BLK
    heads = [GQA * g + j for j in range(GQA)]
    off = jnp.where(i > 0, BLK, 0)
    row = lax.broadcasted_iota(jnp.int32, (rows, 2 * BLK), 0)
    rel = (row % BLK) - lax.broadcasted_iota(jnp.int32, (rows, 2 * BLK), 1) + off
    valid = (rel >= 0) & (rel < BLK)
    head_of_row = lax.broadcasted_iota(jnp.int32, (rows, 1), 0) // BLK
    slope = jnp.zeros((rows, 1), F32)
    sink = jnp.zeros((rows, 1), F32)
    for j, h in enumerate(heads):
        slope = jnp.where(head_of_row == j, SLOPES[h], slope)
        sink = jnp.where(head_of_row == j, sink_ref[0, h], sink)
    qs = jnp.concatenate([q[:, h * HEAD_DIM:(h + 1) * HEAD_DIM] for h in heads], axis=0)
    k = kvw[:, g * HEAD_DIM:(g + 1) * HEAD_DIM]
    v = kvw[:, KVD + g * HEAD_DIM:KVD + (g + 1) * HEAD_DIM]
    s = _dot_nt(qs, k) * ATT_SCALE - slope * rel.astype(F32)
    s = jnp.where(valid, s, NEG_INF)
    m = jnp.maximum(jnp.max(s, axis=-1, keepdims=True), sink)
    e = jnp.exp(s - m)
    es = jnp.exp(sink - m)
    inv = 1.0 / (jnp.sum(e, axis=-1, keepdims=True) + es)
    return e * inv, es * inv, qs, k, v


def _unstack_heads(stacked):
    return [stacked[j * BLK:(j + 1) * BLK, :] for j in range(GQA)]


def _kv_window(kv_ref, i):
    ks = pl.multiple_of(jnp.maximum(i * BLK - BLK, 0), BLK)
    return ks, kv_ref[pl.ds(ks, 2 * BLK), :]


def _attn_fwd(q, kv, sinks, x3, w_o, post_g, rider=None):
    s_len = q.shape[0]

    def body(q_ref, kv_ref, sk_ref, x_ref, wo_ref, g_ref, a_ref, y_ref, x4_ref):
        i = pl.program_id(0)
        _, kvw = _kv_window(kv_ref, i)
        q = q_ref[...]
        outs = []
        for g in range(N_KV_HEADS):
            p, _, _, _, v = _attn_group(i, q, kvw, sk_ref, g)
            outs += _unstack_heads(_dot(p.astype(BF), v))
        attn = jnp.concatenate(outs, axis=1)
        a_ref[...] = attn
        y = _dot(attn.astype(BF), wo_ref[...])
        y_ref[...] = y
        x4_ref[...] = x_ref[...] + _rms_fwd(y, g_ref[1:2, :])

    return _call(body, name="attn_fwd", grid=(s_len // BLK,),
                 in_specs=[_row_spec(BLK), VSPEC, SSPEC, _row_spec(BLK), VSPEC, VSPEC],
                 out_specs=[_row_spec(BLK)] * 3, out_shape=[_sds((s_len, D))] * 3,
                 args=[q, kv, sinks, x3, w_o, post_g], rider=rider)


ATT_STEP_BLOCKS = 2


def _attn_bwd(dx4, y, attn, q, kv, sinks, w_o, post_g, rider=None):
    s_len = q.shape[0]
    rows = ATT_STEP_BLOCKS * BLK
    n = s_len // rows

    def body(dx_ref, y_ref, a_ref, q_ref, kv_ref, sk_ref, wo_ref, g_ref,
             dq_ref, dkv_ref, dwo_ref, dg_ref, dsk_ref, wacc):
        i = pl.program_id(0)

        @pl.when(i == 0)
        def _():
            dkv_ref[...] = jnp.zeros_like(dkv_ref)
            wacc[...] = jnp.zeros_like(wacc)
            dg_ref[...] = jnp.zeros_like(dg_ref)
            dsk_ref[...] = jnp.zeros_like(dsk_ref)

        dy, prod = _rms_bwd(y_ref[...], g_ref[1:2, :], dx_ref[...])
        dg_ref[...] += _rowsum(prod)
        dyb = dy.astype(BF)
        attn_all = a_ref[...]
        wacc[...] += _dot_tn(attn_all.astype(BF), dyb)
        d_o_all = _dot_nt(dyb, wo_ref[...])
        q_all = q_ref[...]
        lane = lax.broadcasted_iota(jnp.int32, (1, D), 1)
        dsk = jnp.zeros((1, D), F32)
        for sub in range(ATT_STEP_BLOCKS):
            blk = i * ATT_STEP_BLOCKS + sub
            sl = slice(sub * BLK, (sub + 1) * BLK)
            d_o, q = d_o_all[sl, :], q_all[sl, :]
            dod = d_o * attn_all[sl, :]
            ks, kvw = _kv_window(kv_ref, blk)
            dqs, dks, dvs = [], [], []
            for g in range(N_KV_HEADS):
                p, ps, qs, k, v = _attn_group(blk, q, kvw, sk_ref, g)
                cols = [slice((GQA * g + j) * HEAD_DIM, (GQA * g + j + 1) * HEAD_DIM) for j in range(GQA)]
                do_s = jnp.concatenate([d_o[:, c] for c in cols], axis=0).astype(BF)
                dsum = jnp.concatenate([jnp.sum(dod[:, c], axis=-1, keepdims=True) for c in cols], axis=0)
                dp = _dot_nt(do_s, v)
                dsb = (p * (dp - dsum) * ATT_SCALE).astype(BF)
                sink_part = ps * dsum
                for j in range(GQA):
                    dsk = dsk + jnp.where(lane == GQA * g + j, -_rowsum(sink_part[j * BLK:(j + 1) * BLK, :]), 0.0)
                dqs += _unstack_heads(_dot(dsb, k))
                dks.append(_dot_tn(dsb, qs))
                dvs.append(_dot_tn(p.astype(BF), do_s))
            dq_ref[sl, :] = jnp.concatenate(dqs, axis=1).astype(BF)
            dkv_ref[pl.ds(ks, 2 * BLK), :] += jnp.concatenate(dks + dvs, axis=1)
        dsk_ref[...] += dsk

        @pl.when(i == n - 1)
        def _():
            dwo_ref[...] = wacc[...].astype(BF)

    return _call(
        body, name="attn_bwd", grid=(n,),
        in_specs=[_row_spec(rows), _row_spec(rows), _row_spec(rows), _row_spec(rows), VSPEC, SSPEC, VSPEC, VSPEC],
        out_specs=[_row_spec(rows), _const_spec((s_len, 2 * KVD)), _const_spec((D, D)),
                   _const_spec((1, D)), _const_spec((1, D))],
        out_shape=[_sds((s_len, D), BF), _sds((s_len, 2 * KVD)), _sds((D, D), BF), _sds((1, D)), _sds((1, D))],
        scratch_shapes=[pltpu.VMEM((D, D), F32)],
        args=[dx4, y, attn, q, kv, sinks, w_o, post_g], rider=rider)


Big = collections.namedtuple("Big", "name src layer L A R C rb")


def _bigs():
    out = {"pool_w": Big("pool_w", "pool_w", None, 4, 4, POOL_G // N_CHIPS, POOL_G, 32)}
    for l in range(2):
        out[f"w_gu{l}"] = Big(f"w_gu{l}", "w_gu", l, 1, 2, D, FF_HALF, 256)
        out[f"w_down{l}"] = Big(f"w_down{l}", "w_down", l, 1, 4, FF // N_CHIPS, D, 352)
        out[f"w_ple_gate{l}"] = Big(f"w_ple_gate{l}", "w_ple_gate", l, 1, 4, D // N_CHIPS, D, 128)
        out[f"w_ple_proj{l}"] = Big(f"w_ple_proj{l}", "w_ple_proj", l, 1, 1, PLE, D // N_CHIPS, 128)
    out["w_q"] = Big("w_q", "w_q", None, 1, 4, D // N_CHIPS, D, 128)
    out["w_o"] = Big("w_o", "w_o", None, 1, 4, D // N_CHIPS, D, 128)
    out["w_kv"] = Big("w_kv", "w_kv", None, 1, 4, D // N_CHIPS, 2 * KVD, 128)
    return out


BIGS = _bigs()
POOL_SCALE = Big("pool_scale", "pool_scale", None, 1, 1, 1, D // N_CHIPS, 1)
BIG_SOURCES = ("w_gu", "w_down", "w_ple_gate", "w_ple_proj", "w_q", "w_o", "w_kv", "pool_w")


def _ncb(t):
    return N_CHIPS // t.A


def _full_shape(t, rows=None):
    return (t.L, t.A, t.R if rows is None else rows, _ncb(t) * t.C)


def _slot_index(t, k):
    return k // _ncb(t), k % _ncb(t)


def _slot(ref, t, k, row0, rows):
    a, cb = _slot_index(t, k)
    return ref.at[:, a, pl.ds(row0, rows), pl.ds(pl.multiple_of(cb * t.C, 128), t.C)]


def _place_job(t, w, out_dtype=BF):
    nb = next((nb for nb in (8, 4, 2, 1) if t.R % (16 * nb) == 0), 1) if t.L == 1 else 1
    rb = t.R // nb

    def fn(j, kc_ref, ins, outs):
        outs[0][...] = ins[0][...].astype(out_dtype)

    def in_map(j, kc_ref):
        return (j // nb if t.layer is None else t.layer, j % nb, 0)

    def out_map(j, kc_ref):
        a, cb = _slot_index(t, kc_ref[0])
        return (j // nb, a, j % nb, cb)

    return Job(t.L * nb, [(w, (None, rb, t.C), in_map)],
               [(_sds(_full_shape(t), out_dtype), (None, None, rb, t.C), out_map)], fn)


def _mesh_position():
    x, y, c = lax.axis_index("x"), lax.axis_index("y"), lax.axis_index("c")
    chips = [(1 - x, y), (x, 1 - y), (1 - x, 1 - y)]
    return x, y, c, chips


DIRECT_BELOW = 1024


def _gather_rider(parts, fulls):
    nt = len(parts)
    TO_X, TO_Y, FWD_X, FWD_Y, SIB_X, SIB_Y, SIB_D = range(7)

    def rows_of(ti, core):
        t, r0, r1 = parts[ti]
        h = (r1 - r0) // 2
        return r0 + core * h, h

    def copy(outs, sems, kind, ti, k_src, row0, rows, dev):
        region = _slot(outs[ti], parts[ti][0], k_src, row0, rows)
        return pltpu.make_async_remote_copy(region, region, sems[0].at[ti, kind], sems[1].at[ti, kind],
                                            device_id=dev, device_id_type=MESH)

    def plan(outs, sems):
        x, y, c, _ = _mesh_position()
        me, kx, ky, kd = 2 * x + y, 2 * (1 - x) + y, 2 * x + (1 - y), 2 * (1 - x) + (1 - y)
        dev_x, dev_y, dev_d, sib = (1 - x, y, c), (x, 1 - y, c), (1 - x, 1 - y, c), (x, y, 1 - c)

        def whole(ti):
            return 0, parts[ti][0].R

        def mk(kind, k_send, k_recv, dev, send_rows, recv_rows):
            def build(ti, side):
                k_src = k_send if side == "s" else k_recv
                row0, rows = (send_rows if side == "s" else recv_rows)(ti)
                return copy(outs, sems, kind, ti, k_src, row0, rows, dev)
            return build

        def first_half(core):
            return lambda ti: (rows_of(ti, core)[0], rows_of(ti, core)[1] // 2)

        def second_half(core):
            return lambda ti: (rows_of(ti, core)[0] + rows_of(ti, core)[1] // 2, rows_of(ti, core)[1] // 2)

        mine = lambda ti: rows_of(ti, c)
        theirs = lambda ti: rows_of(ti, 1 - c)
        split = {
            TO_X: mk(TO_X, me, kx, dev_x, mine, mine),
            TO_Y: mk(TO_Y, me, ky, dev_y, mine, mine),
            FWD_X: mk(FWD_X, ky, kd, dev_x, first_half(c), first_half(c)),
            FWD_Y: mk(FWD_Y, kx, kd, dev_y, second_half(c), second_half(c)),
            SIB_X: mk(SIB_X, kx, kx, sib, mine, theirs),
            SIB_Y: mk(SIB_Y, ky, ky, sib, mine, theirs),
            SIB_D: mk(SIB_D, kd, kd, sib, mine, theirs),
        }
        direct = {
            TO_X: mk(TO_X, me, kx, dev_x, whole, whole),
            TO_Y: mk(TO_Y, me, ky, dev_y, whole, whole),
            FWD_X: mk(FWD_X, me, kd, dev_d, whole, whole),
        }
        return split, direct

    is_split = [t.L * t.R * t.C >= DIRECT_BELOW for t, _, _ in parts]
    assert all(s or (r0, r1) == (0, t.R) for s, (t, r0, r1) in zip(is_split, parts))

    def start(ins, outs, sems):
        split, direct = plan(outs, sems)
        for ti in range(nt):
            kinds = split if is_split[ti] else direct
            kinds[TO_X](ti, "s").start()
            kinds[TO_Y](ti, "s").start()
            if not is_split[ti]:
                kinds[FWD_X](ti, "s").start()

    def mid(ins, outs, sems):
        split, _ = plan(outs, sems)
        for ti in range(nt):
            if is_split[ti]:
                split[TO_Y](ti, "r").wait_recv()
                split[FWD_X](ti, "s").start()
                split[SIB_Y](ti, "s").start()
        for ti in range(nt):
            if is_split[ti]:
                split[TO_X](ti, "r").wait_recv()
                split[FWD_Y](ti, "s").start()
                split[SIB_X](ti, "s").start()

    def finish(ins, outs, sems):
        split, direct = plan(outs, sems)
        for ti in range(nt):
            if is_split[ti]:
                split[FWD_X](ti, "r").wait_recv()
                split[FWD_Y](ti, "r").wait_recv()
                split[SIB_D](ti, "s").start()
            else:
                for kind in (TO_X, TO_Y, FWD_X):
                    direct[kind](ti, "r").wait_recv()
        for ti in range(nt):
            if is_split[ti]:
                for kind in (SIB_X, SIB_Y, SIB_D):
                    split[kind](ti, "r").wait_recv()
        for ti in range(nt):
            kinds = split if is_split[ti] else direct
            for kind in kinds:
                kinds[kind](ti, "s").wait_send()

    sems = pltpu.SemaphoreType.DMA((nt, 7))
    return Rider(list(fulls), [_sds(a.shape, a.dtype) for a in fulls], {i: i for i in range(nt)},
                 [sems, sems], start, mid, finish)


def _pair_exchange_rider(specs, grads):
    nt = len(specs)

    def copy(ins, outs, sems, ti, c, sibling):
        half = specs[ti].R // 2
        return pltpu.make_async_remote_copy(ins[ti].at[:, :, pl.ds((1 - c) * half, half), :], outs[ti],
                                            sems[0].at[ti], sems[1].at[ti], device_id=sibling, device_id_type=MESH)

    def start(ins, outs, sems):
        x, y, c, _ = _mesh_position()
        for ti in range(nt):
            copy(ins, outs, sems, ti, c, (x, y, 1 - c)).start()

    def finish(ins, outs, sems):
        x, y, c, _ = _mesh_position()
        for ti in range(nt):
            copy(ins, outs, sems, ti, c, (x, y, 1 - c)).wait()

    sems = pltpu.SemaphoreType.DMA((nt,))
    return Rider(list(grads), [_sds(_full_shape(t, t.R // 2), BF) for t in specs], {}, [sems, sems], start, None, finish)


def _pair_sum_job(t, g, land):
    assert t.L == 1
    half = t.R // 2
    nj = half // t.rb
    block = (None, t.A, t.rb, _ncb(t) * t.C)

    def fn(j, kc_ref, ins, outs):
        outs[0][...] = (ins[0][...].astype(F32) + ins[1][...].astype(F32)).astype(BF)

    return Job(nj,
               [(g, block, lambda j, kc_ref: (0, 0, kc_ref[1] * nj + j, 0)),
                (land, block, lambda j, kc_ref: (0, 0, j, 0))],
               [(_sds(_full_shape(t, half), BF), block, lambda j, kc_ref: (0, 0, j, 0))], fn)


def _scatter_rider(specs, sums):
    nt = len(specs)

    def copy(ins, outs, sems, ti, j, chip, c):
        t = specs[ti]
        cx, cy = chip
        return pltpu.make_async_remote_copy(_slot(ins[ti], t, 2 * cx + cy, 0, t.R // 2), outs[ti].at[j],
                                            sems[0].at[ti, j], sems[1].at[ti, j],
                                            device_id=(cx, cy, c), device_id_type=MESH)

    def start(ins, outs, sems):
        _, _, c, chips = _mesh_position()
        for j, chip in enumerate(chips):
            for ti in range(nt):
                copy(ins, outs, sems, ti, j, chip, c).start()

    def finish(ins, outs, sems):
        _, _, c, chips = _mesh_position()
        for j, chip in enumerate(chips):
            for ti in range(nt):
                copy(ins, outs, sems, ti, j, chip, c).wait()

    sems = pltpu.SemaphoreType.DMA((nt, N_CHIPS - 1))
    return Rider(list(sums), [_sds((N_CHIPS - 1, t.L, t.R // 2, t.C), BF) for t in specs], {}, [sems, sems],
                 start, None, finish)


def _chip_sum_job(ts, landed):
    t0 = ts[0]
    assert t0.L == 1
    half = t0.R // 2
    nj = half // t0.rb

    def local(j, li):
        return jnp.clip(j - li * nj, 0, nj - 1)

    ins = []
    for li, t in enumerate(ts):
        s, land = landed[t.name]

        def own_map(j, kc_ref, li=li, t=t):
            a, cb = _slot_index(t, kc_ref[0])
            return (0, a, local(j, li), cb)

        ins.append((s, (None, None, t.rb, t.C), own_map))
        ins.append((land, (N_CHIPS - 1, None, t.rb, t.C), lambda j, kc_ref, li=li: (0, 0, local(j, li), 0)))

    def fn(j, kc_ref, in_refs, outs):
        for li in range(len(ts)):
            @pl.when(j // nj == li)
            def _():
                acc = in_refs[2 * li][...].astype(F32)
                for k in range(N_CHIPS - 1):
                    acc = acc + in_refs[2 * li + 1][k].astype(F32)
                outs[0][...] = acc

    return Job(len(ts) * nj, ins,
               [(_sds((len(ts), t0.R, t0.C)), (None, t0.rb, t0.C),
                 lambda j, kc_ref: (j // nj, kc_ref[1] * nj + j % nj, 0))], fn)


def _adamw_job(rb, w, g, m, v):
    n_layers, r, c = w.shape
    nb = r // rb
    block = (None, rb, c)
    index = lambda j, kc_ref: (j // nb, j % nb, 0)

    def fn(j, kc_ref, ins, outs):
        g_v = ins[1][...]
        outs[0][...] = g_v
        outs[1][...], outs[2][...], outs[3][...] = _adamw_math(ins[0][...], g_v, ins[2][...], ins[3][...])

    return Job(n_layers * nb, [(a, block, index) for a in (w, g, m, v)],
               [(_sds(w.shape), block, index)] * 4, fn)


def _chip_sum_fused_job(ts, fused, by_cols):
    t0 = ts[0]
    own0 = fused[t0.name][0]
    if by_cols:
        rows, cols = own0.shape
    else:
        nb, rows, bw = own0.shape
        cols = nb * bw
    nj = rows // t0.rb

    def local(j, li):
        return jnp.clip(j - li * nj, 0, nj - 1)

    ins = []
    for li, t in enumerate(ts):
        own, land = fused[t.name]
        if by_cols:
            ins.append((own, (t.rb, cols), lambda j, kc_ref, li=li: (local(j, li), 0)))
            ins.append((land, (N_CHIPS - 1, t.rb, cols), lambda j, kc_ref, li=li: (0, local(j, li), 0)))
        else:
            ins.append((own, (nb, t.rb, bw), lambda j, kc_ref, li=li: (0, local(j, li), 0)))
            ins.append((land, (N_CHIPS - 1, nb, t.rb, bw), lambda j, kc_ref, li=li: (0, 0, local(j, li), 0)))

    def fn(j, kc_ref, in_refs, outs):
        for li in range(len(ts)):
            @pl.when(j // nj == li)
            def _():
                acc = in_refs[2 * li][...].astype(F32)
                for k in range(N_CHIPS - 1):
                    acc = acc + in_refs[2 * li + 1][k].astype(F32)
                outs[0][...] = acc if by_cols else jnp.concatenate([acc[b] for b in range(nb)], axis=1)

    def out_map(j, kc_ref):
        return (j // nj, j % nj, kc_ref[1]) if by_cols else (j // nj, kc_ref[1] * nj + j % nj, 0)

    return Job(len(ts) * nj, ins, [(_sds((len(ts), t0.R, t0.C)), (None, t0.rb, cols), out_map)], fn)


def _share_rider(halves, by_cols):
    nt = len(halves)

    def copy(outs, sems, ti, core, sibling):
        axis = 2 if by_cols[ti] else 1
        half = halves[ti].shape[axis] // 2
        piece = pl.ds(pl.multiple_of(core * half, 128 if by_cols[ti] else 8), half)
        part = outs[ti].at[:, :, piece] if by_cols[ti] else outs[ti].at[:, piece, :]
        return pltpu.make_async_remote_copy(part, part, sems[0].at[ti], sems[1].at[ti],
                                            device_id=sibling, device_id_type=MESH)

    def start(ins, outs, sems):
        x, y, c, _ = _mesh_position()
        for ti in range(nt):
            copy(outs, sems, ti, c, (x, y, 1 - c)).start()

    def finish(ins, outs, sems):
        x, y, c, _ = _mesh_position()
        for ti in range(nt):
            copy(outs, sems, ti, 1 - c, (x, y, 1 - c)).wait_recv()
        for ti in range(nt):
            copy(outs, sems, ti, c, (x, y, 1 - c)).wait_send()

    sems = pltpu.SemaphoreType.DMA((nt,))
    return Rider(list(halves), [_sds(a.shape, a.dtype) for a in halves], {i: i for i in range(nt)}, [sems, sems],
                 start, None, finish)


def _both(r1, r2):
    assert r1.mid is None and r2.mid is None
    ni, no, ns = len(r1.arrays), len(r1.out_shapes), len(r1.scratch)

    def split(fn1, fn2):
        def run(ins, outs, scr):
            fn1(ins[:ni], outs[:no], scr[:ns])
            fn2(ins[ni:], outs[no:], scr[ns:])
        return run

    aliases = dict(r1.aliases)
    aliases.update({ni + a: no + b for a, b in r2.aliases.items()})
    return Rider(r1.arrays + r2.arrays, r1.out_shapes + r2.out_shapes, aliases, r1.scratch + r2.scratch,
                 split(r1.start, r2.start), None, split(r1.finish, r2.finish))


def _adamw_math(w, g, m, v):
    m = B1 * m + (1.0 - B1) * g
    v = B2 * v + (1.0 - B2) * (g * g)
    delta = -LR * ((m / BC1) / (jnp.sqrt(v / BC2) + AEPS) + WD * w)
    return delta, m, v


GAIN_ROWS = {"pre_mix_g": 0, "post_mix_g": 2, "pre_ffn_g": 4, "post_ffn_g": 6, "ple_g": 8, "ple_post_g": 10}
ROW_KV_G, ROW_POOL_SCALE, ROW_SINKS, ROW_LOSS, PACK_ROWS = 12, 13, 14, 15, 16
SMALL_NAMES = tuple(GAIN_ROWS) + ("kv_g", "pool_scale", "sinks")


def _small_all_reduce(rows, dpool, rider=None):
    ng, pr = len(WINDOWS), POOL_G // N_CHIPS

    def body(*refs):
        row_refs = refs[:PACK_ROWS]
        dpool_ref, tot_ref, gpool_ref, pack, land, pland, send, recv, psend, precv = refs[PACK_ROWS:]
        x, y, c, _ = _mesh_position()
        me = 4 * x + 2 * y + c
        for r in range(PACK_ROWS):
            pack[r:r + 1, :] = row_refs[r][...]

        def shard_of(k):
            return dpool_ref.at[:, pl.ds(pl.multiple_of(k * pr, pr), pr), :]

        cps = []
        for j in range(1, N_DEV):
            px, py, pc = x ^ (j >> 2), y ^ ((j >> 1) & 1), c ^ (j & 1)
            cps.append(pltpu.make_async_remote_copy(pack, land.at[me], send.at[j], recv.at[j],
                                                    device_id=(px, py, pc), device_id_type=MESH))
            cps.append(pltpu.make_async_remote_copy(shard_of(2 * px + py), pland.at[me], psend.at[j], precv.at[j],
                                                    device_id=(px, py, pc), device_id_type=MESH))
        for cp in cps:
            cp.start()
        land[me] = pack[...]
        pland[me] = dpool_ref[:, pl.ds(pl.multiple_of((2 * x + y) * pr, pr), pr), :]
        for j in range(1, N_DEV):
            pltpu.make_async_remote_copy(pack, land.at[me ^ j], send.at[j], recv.at[j],
                                         device_id=(x, y, c), device_id_type=MESH).wait_recv()
            pltpu.make_async_remote_copy(shard_of(0), pland.at[me ^ j], psend.at[j], precv.at[j],
                                         device_id=(x, y, c), device_id_type=MESH).wait_recv()
        for cp in cps:
            cp.wait_send()
        tot = land[0]
        gp = pland[0].astype(F32)
        for d in range(1, N_DEV):
            tot = tot + land[d]
            gp = gp + pland[d].astype(F32)
        tot_ref[...] = tot
        gpool_ref[...] = gp

    sems = pltpu.SemaphoreType.DMA((N_DEV,))
    return _call(
        body, name="small_all_reduce", grid=(1,),
        in_specs=[VSPEC] * (PACK_ROWS + 1), out_specs=[VSPEC, VSPEC],
        out_shape=[_sds((PACK_ROWS, D)), _sds((ng, pr, POOL_G))],
        scratch_shapes=[pltpu.VMEM((PACK_ROWS, D), F32), pltpu.VMEM((N_DEV, PACK_ROWS, D), F32),
                        pltpu.VMEM((N_DEV, ng, pr, POOL_G), BF), sems, sems, sems, sems],
        args=[*rows, dpool], rider=rider)


def _small_adamw(tot, kc, small_w, small_m, small_v):
    names = SMALL_NAMES
    n = len(names)

    def body(*refs):
        tot_ref, kc_ref = refs[0], refs[1]
        w_refs = dict(zip(names, refs[2:2 + n]))
        m_refs = dict(zip(names, refs[2 + n:2 + 2 * n]))
        v_refs = dict(zip(names, refs[2 + 2 * n:2 + 3 * n]))
        loss_ref = refs[2 + 3 * n]
        out_refs = {nm: refs[3 + 3 * n + 4 * k: 7 + 3 * n + 4 * k] for k, nm in enumerate(names)}
        tot = tot_ref[...]
        loss_ref[...] = 0.5 * jnp.sum(tot[ROW_LOSS:ROW_LOSS + 1, :], axis=-1, keepdims=True) * (1.0 / D)

        def update(nm, g):
            g_ref, d_ref, nm_ref, nv_ref = out_refs[nm]
            g_ref[...] = g
            d_ref[...], nm_ref[...], nv_ref[...] = _adamw_math(w_refs[nm][...], g, m_refs[nm][...], v_refs[nm][...])

        for nm, r in GAIN_ROWS.items():
            update(nm, tot[r:r + 2, :])
        update("kv_g", tot[ROW_KV_G:ROW_KV_G + 1, :])
        k = kc_ref[0]
        width = D // N_CHIPS
        g_scale = jnp.zeros((1, width), F32)
        for kk in range(N_CHIPS):
            g_scale = g_scale + jnp.where(k == kk, tot[ROW_POOL_SCALE:ROW_POOL_SCALE + 1, kk * width:(kk + 1) * width], 0.0)
        update("pool_scale", g_scale)
        update("sinks", tot[ROW_SINKS:ROW_SINKS + 1, 0:N_HEADS])

    ins = [tot, kc] + [small_w[nm] for nm in names] + [small_m[nm] for nm in names] + [small_v[nm] for nm in names]
    out_shape = [_sds((1, 1))]
    for nm in names:
        out_shape += [_sds(small_w[nm].shape)] * 4
    outs = pl.pallas_call(
        body, name="small_adamw",
        in_specs=[VSPEC, SSPEC] + [VSPEC] * (3 * n), out_specs=[VSPEC] * len(out_shape), out_shape=out_shape,
        compiler_params=_params(),
    )(*ins)
    return outs[0], {nm: outs[1 + 4 * k: 5 + 4 * k] for k, nm in enumerate(names)}


def _compute_layout(t, full):
    if t.src == "w_gu":
        return full.reshape(2, D, FF)
    if t.src == "pool_w":
        return full.reshape(len(WINDOWS), POOL_G, POOL_G)
    if t.src == "pool_scale":
        return full.reshape(1, D)
    return full.reshape(t.A * t.R, _ncb(t) * t.C)


def kernel(x, p, pre_mix_g, post_mix_g, pre_ffn_g, post_ffn_g, pool_w, pool_scale, kv_g, w_kv, w_q, sinks, w_o, w_gu, w_down, ple_g, w_ple_gate, w_ple_proj, ple_post_g, loss_target, m_pre_mix_g, m_post_mix_g, m_pre_ffn_g, m_post_ffn_g, m_pool_w, m_pool_scale, m_kv_g, m_w_kv, m_w_q, m_sinks, m_w_o, m_w_gu, m_w_down, m_ple_g, m_w_ple_gate, m_w_ple_proj, m_ple_post_g, v_pre_mix_g, v_post_mix_g, v_pre_ffn_g, v_post_ffn_g, v_pool_w, v_pool_scale, v_kv_g, v_w_kv, v_w_q, v_sinks, v_w_o, v_w_gu, v_w_down, v_ple_g, v_w_ple_gate, v_w_ple_proj, v_ple_post_g):
    weights = dict(pre_mix_g=pre_mix_g, post_mix_g=post_mix_g, pre_ffn_g=pre_ffn_g, post_ffn_g=post_ffn_g,
                   pool_w=pool_w, pool_scale=pool_scale, kv_g=kv_g, w_kv=w_kv, w_q=w_q, sinks=sinks, w_o=w_o,
                   w_gu=w_gu, w_down=w_down, ple_g=ple_g, w_ple_gate=w_ple_gate, w_ple_proj=w_ple_proj,
                   ple_post_g=ple_post_g)
    m_in = dict(pre_mix_g=m_pre_mix_g, post_mix_g=m_post_mix_g, pre_ffn_g=m_pre_ffn_g, post_ffn_g=m_post_ffn_g,
                pool_w=m_pool_w, pool_scale=m_pool_scale, kv_g=m_kv_g, w_kv=m_w_kv, w_q=m_w_q, sinks=m_sinks,
                w_o=m_w_o, w_gu=m_w_gu, w_down=m_w_down, ple_g=m_ple_g, w_ple_gate=m_w_ple_gate,
                w_ple_proj=m_w_ple_proj, ple_post_g=m_ple_post_g)
    v_in = dict(pre_mix_g=v_pre_mix_g, post_mix_g=v_post_mix_g, pre_ffn_g=v_pre_ffn_g, post_ffn_g=v_post_ffn_g,
                pool_w=v_pool_w, pool_scale=v_pool_scale, kv_g=v_kv_g, w_kv=v_w_kv, w_q=v_w_q, sinks=v_sinks,
                w_o=v_w_o, w_gu=v_w_gu, w_down=v_w_down, ple_g=v_ple_g, w_ple_gate=v_w_ple_gate,
                w_ple_proj=v_w_ple_proj, ple_post_g=v_ple_post_g)
    order = ["pre_mix_g", "post_mix_g", "pre_ffn_g", "post_ffn_g", "pool_w", "pool_scale", "kv_g", "w_kv", "w_q",
             "sinks", "w_o", "w_gu", "w_down", "ple_g", "w_ple_gate", "w_ple_proj", "ple_post_g"]

    kc = jnp.stack([2 * lax.axis_index("x") + lax.axis_index("y"), lax.axis_index("c")]).astype(jnp.int32)
    s_len = x.shape[1]
    x2d = x.reshape(s_len, D)
    p3d = p.reshape(2, s_len, PLE)
    target = loss_target.reshape(s_len, D)
    kv_g2d = kv_g.reshape(1, D)
    gains = {nm: weights[nm] for nm in GAIN_ROWS}

    def shard_view(src, a):
        t = next(t for t in BIGS.values() if t.src == src)
        return a.reshape(-1, t.R, t.C)

    first, second = ["pool_w", "pool_scale"], ["w_gu0", "w_down0"]
    rest = [nm for nm in BIGS if nm not in first + second]
    specs = dict(BIGS, pool_scale=POOL_SCALE)
    placed = {}

    def place_job(nm):
        if nm == "pool_scale":
            return _place_job(POOL_SCALE, pool_scale.reshape(1, 1, D // N_CHIPS), F32)
        return _place_job(BIGS[nm], shard_view(BIGS[nm].src, weights[BIGS[nm].src]))

    def gather(names, rows=None):
        rows = rows or {}
        parts = [(specs[nm],) + tuple(rows.get(nm, (0, specs[nm].R))) for nm in names]
        return _gather_rider(parts, [placed[nm] for nm in names])

    def take(names, results):
        for nm, a in zip(names, results):
            placed[nm] = a

    def weight(nm):
        return _compute_layout(specs[nm], placed[nm])

    take(first, [r[0] for r in _multi_call("place_pool", [place_job(nm) for nm in first], kc)])
    cast, got = _multi_call("place_ffn0", [place_job(nm) for nm in second], kc, rider=gather(first))
    take(second, [r[0] for r in cast])
    take(first, got)
    jobs = [place_job(nm) for nm in rest]
    jobs.append(_mixa_fwd_job(x2d, gains["pre_mix_g"], weight("pool_w"), weight("pool_scale"), gains["post_mix_g"]))
    results, got = _multi_call("cast_and_mixa_fwd", jobs, kc, rider=gather(second))
    take(rest, [r[0] for r in results[:-1]])
    take(second, got)
    y0, x1 = results[-1]

    ride = ["w_ple_gate0", "w_ple_proj0", "w_q", "w_kv", "w_o", "w_gu1"]
    (f0, x2, g0, u0), got = _ffn_fwd(0, x1, gains["pre_ffn_g"], weight("w_gu0"), weight("w_down0"), gains["post_ffn_g"],
                             rider=gather(ride, {"w_gu1": (0, 320)}))
    take(ride, got)

    ride = ["w_ple_gate1", "w_ple_proj1", "w_gu1"]
    (z0, pe0, x3, q, kv), got = _ple_fwd(
        0, x2, p3d, gains["ple_g"], weight("w_ple_gate0"), weight("w_ple_proj0"), gains["ple_post_g"],
        qkv=(gains["pre_mix_g"], kv_g2d, weight("w_q"), weight("w_kv")),
        rider=gather(ride, {"w_gu1": (320, 704)}))
    take(ride, got)

    ride = ["w_down1", "w_gu1"]
    (attn, y1, x4), got = _attn_fwd(q, kv, sinks, x3, weight("w_o"), gains["post_mix_g"],
                                    rider=gather(ride, {"w_gu1": (704, D)}))
    take(ride, got)

    (f1, x5, g1, u1), _ = _ffn_fwd(1, x4, gains["pre_ffn_g"], weight("w_gu1"), weight("w_down1"), gains["post_ffn_g"])
    (z1, pe1, dx6, loss_row), _ = _ple_fwd(1, x5, p3d, gains["ple_g"], weight("w_ple_gate1"), weight("w_ple_proj1"),
                                           gains["ple_post_g"], target=target)

    local = {}
    landed = {}
    fused = {}

    def local_grads(names):
        return [local[nm].reshape(_full_shape(BIGS[nm])) for nm in names]

    def pair_exchange(names):
        return _pair_exchange_rider([BIGS[nm] for nm in names], local_grads(names))

    def pair_sum(tag, names, lands):
        jobs = [_pair_sum_job(BIGS[nm], g, l) for nm, g, l in zip(names, local_grads(names), lands)]
        return [r[0] for r in _multi_call(f"pair_sum_{tag}", jobs, kc)]

    def scatter(names, sums):
        return _scatter_rider([BIGS[nm] for nm in names], sums)

    def keep(names, sums, got):
        for nm, s, l in zip(names, sums, got):
            landed[nm] = (s, l)

    (dx5, local["w_ple_gate1"], local["w_ple_proj1"], d_ple1, d_plepost1), _ = _ple_bwd(
        1, dx6, x5, z1, pe1, p3d, gains["ple_g"], weight("w_ple_gate1"), gains["ple_post_g"])

    group_a = ["w_ple_gate1", "w_ple_proj1"]
    (dx4, d_preffn1, d_postffn1, *scattered), lands_a = _ffn_bwd(
        1, dx5, x4, f1, g1, u1, gains["pre_ffn_g"], weight("w_gu1"), weight("w_down1"), gains["post_ffn_g"], kc,
        rider=pair_exchange(group_a))
    fused["w_gu1"], fused["w_down1"] = scattered[0:2], scattered[2:4]
    sums_a = pair_sum("a", group_a, lands_a)

    (dq, dkv, local["w_o"], d_postmix1, d_sinks), _ = _attn_bwd(
        dx4, y1, attn, q, kv, sinks, weight("w_o"), gains["post_mix_g"])
    dx3, local["w_q"], local["w_kv"], d_premix1, d_kvg = _qkv_bwd(
        dq, dkv, x3, dx4, gains["pre_mix_g"], kv_g2d, weight("w_q"), weight("w_kv"))

    group_b = ["w_o", "w_q", "w_kv"]
    (dx2, local["w_ple_gate0"], local["w_ple_proj0"], d_ple0, d_plepost0), lands_b = _ple_bwd(
        0, dx3, x2, z0, pe0, p3d, gains["ple_g"], weight("w_ple_gate0"), gains["ple_post_g"],
        rider=pair_exchange(group_b))
    sums_b = pair_sum("b", group_b, lands_b)

    group_c = ["w_ple_gate0", "w_ple_proj0"]
    (dx1, d_preffn0, d_postffn0, *scattered), got = _ffn_bwd(
        0, dx2, x1, f0, g0, u0, gains["pre_ffn_g"], weight("w_gu0"), weight("w_down0"), gains["post_ffn_g"], kc,
        rider=_both(pair_exchange(group_c), scatter(group_a + group_b, sums_a + sums_b)))
    fused["w_gu0"], fused["w_down0"] = scattered[0:2], scattered[2:4]
    sums_c = pair_sum("c", group_c, got[:len(group_c)])
    keep(group_a + group_b, sums_a + sums_b, got[len(group_c):])

    layers_of = lambda src: [t for t in BIGS.values() if t.src == src]
    own_scatter = ["w_gu", "w_down"]
    early = own_scatter + ["w_q", "w_o", "w_kv"]
    late = ["w_ple_gate", "w_ple_proj"]
    by_cols = lambda srcs: [src == "w_down" for src in srcs]
    jobs = [_chip_sum_fused_job(layers_of(src), fused, by_cols=src == "w_down") for src in own_scatter]
    jobs += [_chip_sum_job(layers_of(src), landed) for src in early if src not in own_scatter]
    halves = [r[0] for r in _multi_call("chip_sum_early", jobs, kc)]
    (dx0, d_pool, d_scale, d_postmix0, d_premix0), got = _mixa_bwd(
        dx1, x2d, y0, gains["pre_mix_g"], weight("pool_w"), weight("pool_scale"), gains["post_mix_g"],
        rider=_both(scatter(group_c, sums_c), _share_rider(halves, by_cols(early))))
    keep(group_c, sums_c, got[:len(group_c)])
    full_grads = dict(zip(early, got[len(group_c):]))

    rows = [d_premix0, d_premix1, d_postmix0, d_postmix1, d_preffn0, d_preffn1, d_postffn0, d_postffn1,
            d_ple0, d_ple1, d_plepost0, d_plepost1, d_kvg, d_scale, d_sinks, loss_row]
    as2d = lambda a: a.reshape(1, D) if a.ndim == 1 else a
    (tot, g_pool), _ = _small_all_reduce(rows, d_pool)
    loss, small = _small_adamw(tot, kc, {nm: as2d(weights[nm]) for nm in SMALL_NAMES},
                               {nm: as2d(m_in[nm]) for nm in SMALL_NAMES},
                               {nm: as2d(v_in[nm]) for nm in SMALL_NAMES})

    halves = [r[0] for r in _multi_call("chip_sum_late", [_chip_sum_job(layers_of(src), landed) for src in late], kc)]
    full_grads.update(zip(late, _run("grads_pair_share", _share_rider(halves, by_cols(late)))))
    full_grads["pool_w"] = g_pool

    def adam_job(src):
        rb = layers_of(src)[0].rb // (1 if src == "pool_w" else 2)
        return _adamw_job(rb, shard_view(src, weights[src]), full_grads[src],
                          shard_view(src, m_in[src]), shard_view(src, v_in[src]))

    out = {"grad": {}, "delta": {}, "new_m": {}, "new_v": {}}
    results = dict(zip(BIG_SOURCES, _multi_call("adamw", [adam_job(src) for src in BIG_SOURCES], kc)))
    for src in BIG_SOURCES:
        shape = weights[src].shape
        for kind, a in zip(("grad", "delta", "new_m", "new_v"), results[src]):
            out[kind][src] = a.reshape(shape)
    for nm in SMALL_NAMES:
        shape = weights[nm].shape
        for kind, a in zip(("grad", "delta", "new_m", "new_v"), small[nm]):
            out[kind][nm] = a.reshape(shape)

    return (loss.reshape(()), dx0.reshape(x.shape),
            *[out["grad"][nm] for nm in order], *[out["delta"][nm] for nm in order],
            *[out["new_m"][nm] for nm in order], *[out["new_v"][nm] for nm in order])
```

```python
import collections

import jax
import jax.numpy as jnp
from jax import lax
from jax.experimental import pallas as pl
from jax.experimental.pallas import tpu as pltpu

D = 1024
FF = 2816
N_HEADS = 16
HEAD_DIM = 64
N_KV_HEADS = 4
GQA = N_HEADS // N_KV_HEADS
KVD = N_KV_HEADS * HEAD_DIM
PLE = 256
BLK = 128
WINDOWS = (2, 4, 8, 16)
POOL_G = 256
HALO = 16
EPS = 1e-6
NEG_INF = -1e30
ATT_SCALE = HEAD_DIM ** -0.5
SLOPES = tuple(2.0 ** (-8.0 * (h + 1) / N_HEADS) for h in range(N_HEADS))
N_CHIPS = 4
N_DEV = 8

LR, B1, B2, AEPS, WD, STEP = 0.001, 0.9, 0.999, 1e-08, 0.01, 10
BC1 = 1.0 - B1 ** STEP
BC2 = 1.0 - B2 ** STEP

BF = jnp.bfloat16
F32 = jnp.float32
MESH = pl.DeviceIdType.MESH
VMEM_LIMIT_V7X = 58 * 1024 * 1024
TM = 256
TM_FFN_BWD = 512
FF_CHUNK = 256
FF_HALF = FF // 2

VSPEC = pl.BlockSpec(memory_space=pltpu.VMEM)
SSPEC = pl.BlockSpec(memory_space=pltpu.SMEM)
ANYSPEC = pl.BlockSpec(memory_space=pl.ANY)


def _params(n_grid=0):
    sem = ("arbitrary",) * n_grid if n_grid else None
    return pltpu.CompilerParams(dimension_semantics=sem, vmem_limit_bytes=VMEM_LIMIT_V7X)


def _sds(shape, dtype=F32):
    return jax.ShapeDtypeStruct(tuple(shape), dtype)


Rider = collections.namedtuple("Rider", "arrays out_shapes aliases scratch start mid finish")
MID_NUM, MID_DEN = 5, 8


def _call(body, *, name, grid, in_specs, out_specs, out_shape, args, scratch_shapes=(), rider=None, prefetch=None):
    ni, no, ns = len(in_specs), len(out_specs), len(scratch_shapes)
    npre = 0 if prefetch is None else 1
    pre = [] if prefetch is None else [prefetch]
    if rider is None:
        rider = Rider([], [], {}, [], None, None, None)
    ri, ro = len(rider.arrays), len(rider.out_shapes)

    def full(*refs):
        pre_refs, refs = refs[:npre], refs[npre:]
        ins, refs = refs[:ni], refs[ni:]
        rins, refs = refs[:ri], refs[ri:]
        outs, refs = refs[:no], refs[no:]
        routs, refs = refs[:ro], refs[ro:]
        scr, rscr = refs[:ns], refs[ns:]
        ids = [pl.program_id(a) for a in range(len(grid))]
        first = ids[0] == 0
        last = ids[0] == grid[0] - 1
        for a in range(1, len(grid)):
            first = first & (ids[a] == 0)
            last = last & (ids[a] == grid[a] - 1)

        if rider.start is not None:
            @pl.when(first)
            def _():
                rider.start(rins, routs, rscr)

        if rider.mid is not None:
            assert len(grid) == 1

            @pl.when(ids[0] == (grid[0] * MID_NUM) // MID_DEN)
            def _():
                rider.mid(rins, routs, rscr)

        body(*pre_refs, *ins, *outs, *scr)

        if rider.finish is not None:
            @pl.when(last)
            def _():
                rider.finish(rins, routs, rscr)

    outs = pl.pallas_call(
        full, name=name,
        grid_spec=pltpu.PrefetchScalarGridSpec(
            num_scalar_prefetch=npre, grid=grid,
            in_specs=list(in_specs) + [ANYSPEC] * ri, out_specs=list(out_specs) + [ANYSPEC] * ro,
            scratch_shapes=list(scratch_shapes) + list(rider.scratch)),
        out_shape=list(out_shape) + list(rider.out_shapes),
        input_output_aliases={npre + ni + a: no + b for a, b in rider.aliases.items()},
        compiler_params=_params(len(grid)))(*pre, *args, *rider.arrays)
    return list(outs[:no]), list(outs[no:])


def _run(name, rider):
    ri = len(rider.arrays)

    def body(*refs):
        rins, routs, rscr = refs[:ri], refs[ri:ri + len(rider.out_shapes)], refs[ri + len(rider.out_shapes):]
        rider.start(rins, routs, rscr)
        if rider.mid is not None:
            rider.mid(rins, routs, rscr)
        rider.finish(rins, routs, rscr)

    return pl.pallas_call(
        body, name=name, in_specs=[ANYSPEC] * ri, out_specs=[ANYSPEC] * len(rider.out_shapes),
        out_shape=list(rider.out_shapes), scratch_shapes=list(rider.scratch),
        input_output_aliases=dict(rider.aliases), compiler_params=_params())(*rider.arrays)


Job = collections.namedtuple("Job", "steps ins outs fn")


def _multi_call(name, jobs, kc, rider=None):
    n = max(job.steps for job in jobs)

    def clamped(index, steps):
        return lambda s, kc_ref: index(jnp.minimum(s, steps - 1), kc_ref)

    in_specs, out_specs, out_shape, args = [], [], [], []
    for job in jobs:
        for arr, block, index, *single in job.ins:
            mode = dict(pipeline_mode=pl.Buffered(1)) if single and single[0] else {}
            in_specs.append(pl.BlockSpec(block, clamped(index, job.steps), **mode))
            args.append(arr)
        for sds, block, index in job.outs:
            out_specs.append(pl.BlockSpec(block, clamped(index, job.steps)))
            out_shape.append(sds)
    n_in = len(args)

    def body(kc_ref, *refs):
        s = pl.program_id(0)
        i0, o0 = 0, n_in
        for job in jobs:
            ins, outs = refs[i0:i0 + len(job.ins)], refs[o0:o0 + len(job.outs)]
            i0, o0 = i0 + len(job.ins), o0 + len(job.outs)

            @pl.when(s < job.steps)
            def _():
                job.fn(s, kc_ref, ins, outs)

    outs, routs = _call(body, name=name, grid=(n,), in_specs=in_specs, out_specs=out_specs, out_shape=out_shape,
                        args=args, prefetch=kc, rider=rider)
    res, o0 = [], 0
    for job in jobs:
        res.append(outs[o0:o0 + len(job.outs)])
        o0 += len(job.outs)
    return res if rider is None else (res, routs)


def _rms_fwd(x, g):
    r = lax.rsqrt(jnp.mean(x * x, axis=-1, keepdims=True) + EPS)
    return x * r * g


def _rms_bwd(x, g, dy):
    r = lax.rsqrt(jnp.mean(x * x, axis=-1, keepdims=True) + EPS)
    xn = x * r
    dxn = dy * g
    dx = r * (dxn - xn * jnp.mean(dxn * xn, axis=-1, keepdims=True))
    return dx, dy * xn


def _rowsum(a):
    return jnp.sum(a, axis=0, keepdims=True)


def _sigmoid(z):
    return 1.0 / (1.0 + jnp.exp(-z))


def _dot(a, b):
    return jnp.dot(a, b, preferred_element_type=F32)


def _dot_nt(a, b):
    return lax.dot_general(a, b, (((1,), (1,)), ((), ())), preferred_element_type=F32)


def _dot_tn(a, b):
    return lax.dot_general(a, b, (((0,), (0,)), ((), ())), preferred_element_type=F32)


def _row_spec(tm, width=D):
    return pl.BlockSpec((tm, width), lambda i: (i, 0))


def _const_spec(shape):
    zeros = (0,) * len(shape)
    return pl.BlockSpec(tuple(shape), lambda *_: zeros)


def _pool_delta(he, pos):
    out = []
    for gi, w in enumerate(WINDOWS):
        hg = he[:, gi * POOL_G:(gi + 1) * POOL_G]
        s = hg
        k = 1
        while k < w:
            s = s + pltpu.roll(s, k, 0)
            k *= 2
        cnt = jnp.maximum(jnp.minimum(pos + 1, w), 1).astype(F32)
        out.append(s / cnt - hg)
    return out


def _load_with_halo_before(x_ref, i, tm):
    r0 = pl.multiple_of(i * tm, tm)
    hs = pl.multiple_of(jnp.maximum(i * tm - HALO, 0), 8)
    xh = jnp.where(i > 0, x_ref[pl.ds(hs, HALO), :], 0.0)
    xt = x_ref[pl.ds(r0, tm), :]
    return xt, jnp.concatenate([xh, xt], axis=0)


def _mixa_fwd_job(x, pre_g, pool_w, pool_scale, post_g):
    s_len = x.shape[0]

    def fn(i, kc_ref, ins, outs):
        x_ref, pg_ref, w_ref, sc_ref, qg_ref = ins
        y_ref, x1_ref = outs
        xt, xe = _load_with_halo_before(x_ref, i, TM)
        he = _rms_fwd(xe, pg_ref[0:1, :])
        pos = i * TM - HALO + lax.broadcasted_iota(jnp.int32, (TM + HALO, 1), 0)
        ds = _pool_delta(he, pos)
        ys = [_dot(ds[gi][HALO:, :].astype(BF), w_ref[gi]) for gi in range(len(WINDOWS))]
        y = jnp.concatenate(ys, axis=1) * sc_ref[...]
        y_ref[...] = y
        x1_ref[...] = xt + _rms_fwd(y, qg_ref[0:1, :])

    def whole(a):
        zeros = (0,) * a.ndim
        return (a, a.shape, lambda j, kc_ref: zeros, True)

    rows = lambda j, kc_ref: (j, 0)
    return Job(s_len // TM, [whole(a) for a in (x, pre_g, pool_w, pool_scale, post_g)],
               [(_sds((s_len, D)), (TM, D), rows), (_sds((s_len, D)), (TM, D), rows)], fn)


def _mixa_bwd(dx1, x, y, pre_g, pool_w, pool_scale, post_g, rider=None):
    s_len = x.shape[0]
    n = s_len // TM
    ng = len(WINDOWS)

    def body(dx_ref, x_ref, y_ref, pg_ref, w_ref, sc_ref, qg_ref,
             dx0_ref, dw_ref, dsc_ref, dqg_ref, dpg_ref, wacc):
        i = pl.program_id(0)

        @pl.when(i == 0)
        def _():
            wacc[...] = jnp.zeros_like(wacc)
            dsc_ref[...] = jnp.zeros_like(dsc_ref)
            dqg_ref[...] = jnp.zeros_like(dqg_ref)
            dpg_ref[...] = jnp.zeros_like(dpg_ref)

        r0 = pl.multiple_of(i * TM, TM)
        xt, xe = _load_with_halo_before(x_ref, i, TM)
        he = _rms_fwd(xe, pg_ref[0:1, :])
        pos_b = i * TM - HALO + lax.broadcasted_iota(jnp.int32, (TM + HALO, 1), 0)
        ds = _pool_delta(he, pos_b)

        last = i == n - 1
        a0 = pl.multiple_of(jnp.minimum(i * TM + TM, s_len - HALO), 8)
        ye = jnp.concatenate([y_ref[pl.ds(r0, TM), :], y_ref[pl.ds(a0, HALO), :]], axis=0)
        dt = dx_ref[pl.ds(r0, TM), :]
        de = jnp.concatenate([dt, jnp.where(last, 0.0, dx_ref[pl.ds(a0, HALO), :])], axis=0)
        dye, prod = _rms_bwd(ye, qg_ref[0:1, :], de)
        dqg_ref[...] += _rowsum(prod[:TM, :])
        dys = dye * sc_ref[...]
        pos_a = i * TM + lax.broadcasted_iota(jnp.int32, (TM + HALO, 1), 0)

        dhs, dscs = [], []
        for gi, w in enumerate(WINDOWS):
            sl = slice(gi * POOL_G, (gi + 1) * POOL_G)
            wg = w_ref[gi]
            dys_g = dys[:, sl].astype(BF)
            d_g = ds[gi][HALO:, :].astype(BF)
            ypre = _dot(d_g, wg)
            dscs.append(_rowsum(dye[:TM, sl] * ypre))
            wacc[gi] += _dot_tn(d_g, dys_g[:TM, :])
            dd = _dot_nt(dys_g, wg)
            cnt = jnp.minimum(pos_a + 1, w).astype(F32)
            a = dd / cnt
            k = 1
            while k < w:
                a = a + pltpu.roll(a, TM + HALO - k, 0)
                k *= 2
            dhs.append(a[:TM, :] - dd[:TM, :])
        dsc_ref[...] += jnp.concatenate(dscs, axis=1)
        dh = jnp.concatenate(dhs, axis=1)
        dxp, prod2 = _rms_bwd(xt, pg_ref[0:1, :], dh)
        dpg_ref[...] += _rowsum(prod2)
        dx0_ref[...] = dt + dxp

        @pl.when(last)
        def _():
            dw_ref[...] = wacc[...].astype(BF)

    return _call(
        body, name="mixa_bwd", grid=(n,), in_specs=[VSPEC] * 7,
        out_specs=[_row_spec(TM), _const_spec((ng, POOL_G, POOL_G)), _const_spec((1, D)),
                   _const_spec((1, D)), _const_spec((1, D))],
        out_shape=[_sds((s_len, D)), _sds((ng, POOL_G, POOL_G), BF), _sds((1, D)), _sds((1, D)), _sds((1, D))],
        scratch_shapes=[pltpu.VMEM((ng, POOL_G, POOL_G), F32)],
        args=[dx1, x, y, pre_g, pool_w, pool_scale, post_g], rider=rider)


def _ffn_fwd(layer, x1, pre_g, wgu, wd, post_g, rider=None):
    s_len = x1.shape[0]

    def body(x_ref, pg_ref, wgu_ref, wd_ref, qg_ref, f_ref, x2_ref, g_ref, u_ref):
        x = x_ref[...]
        h = _rms_fwd(x, pg_ref[layer:layer + 1, :]).astype(BF)
        f = jnp.zeros((TM, D), F32)
        for c in range(FF // FF_HALF):
            cols = slice(c * FF_HALF, (c + 1) * FF_HALF)
            g = _dot(h, wgu_ref[0, :, cols])
            u = _dot(h, wgu_ref[1, :, cols])
            g_ref[:, cols] = g.astype(BF)
            u_ref[:, cols] = u.astype(BF)
            act = g * _sigmoid(g) * u
            f = f + _dot(act.astype(BF), wd_ref[cols, :])
        f_ref[...] = f
        x2_ref[...] = x + _rms_fwd(f, qg_ref[layer:layer + 1, :])

    return _call(body, name=f"ffn_fwd{layer}", grid=(s_len // TM,),
                 in_specs=[_row_spec(TM), VSPEC, VSPEC, VSPEC, VSPEC],
                 out_specs=[_row_spec(TM), _row_spec(TM), _row_spec(TM, FF), _row_spec(TM, FF)],
                 out_shape=[_sds((s_len, D)), _sds((s_len, D)), _sds((s_len, FF), BF), _sds((s_len, FF), BF)],
                 args=[x1, pre_g, wgu, wd, post_g], rider=rider)


GU_PIECE = 128
DN_PIECE = 64
DN_SLOT = FF // N_CHIPS
HALF_D = D // 2
CHUNK_STRIDE = 6
CHUNK_START = (1, 7, 4, 10)


def _ffn_bwd(layer, dx2, x1, f, g_pre, u_pre, pre_g, wgu, wd, post_g, kc, rider=None):
    s_len = x1.shape[0]
    tm = TM_FFN_BWD
    n = s_len // tm
    nc = FF // FF_CHUNK
    n_gu, n_dn = FF_CHUNK // GU_PIECE, FF_CHUNK // DN_PIECE
    n_pieces = 2 * n_gu + n_dn
    n_blk = FF_HALF // GU_PIECE

    def edge_rows(c, i, kc_ref):
        return (jnp.where((c == 0) | (c == nc - 1), i, n - 1), 0)

    def chunk_at(c, kc_ref):
        k = kc_ref[0]
        start = jnp.where(k == 0, CHUNK_START[0], jnp.where(k == 1, CHUNK_START[1],
                                                            jnp.where(k == 2, CHUNK_START[2], CHUNK_START[3])))
        return ((c + start) * CHUNK_STRIDE) % nc

    def exchange(kc_ref, c, accg, accu, accd, own_gu_ref, land_gu_ref, own_dn_ref, land_dn_ref,
                 pl_gu, pl_dn, sib_gu, sib_dn, mine_gu, mine_dn, sum_gu, sum_dn,
                 psend, precv, ssend, lsem, rrecv):
        x, y, core = lax.axis_index("x"), lax.axis_index("y"), lax.axis_index("c")
        lower = core == 0

        def pair_copy(cc, part):
            p = cc % 2
            src, dst = ((sib_gu, pl_gu), (sib_dn, pl_dn))[part]
            return pltpu.make_async_remote_copy(src.at[p], dst.at[cc], psend.at[p, part], precv.at[cc, part],
                                                device_id=(x, y, 1 - core), device_id_type=MESH)

        def scatter(cc, wait):
            p = cc % 2
            hidden = chunk_at(cc, kc_ref) * FF_CHUNK

            assert n_gu == 2
            k0, k1 = hidden // FF_HALF, (hidden + GU_PIECE) // FF_HALF
            blk = (hidden - k0 * FF_HALF) // GU_PIECE
            for gu in range(2):
                @pl.when(k0 == k1)
                def _():
                    piece(p, wait, 2 * gu, sum_gu.at[p, gu], k0 + 2 * gu, 0, (pl.ds(blk, 2),))

                @pl.when(k0 != k1)
                def _():
                    piece(p, wait, 2 * gu, sum_gu.at[p, gu, 0], k0 + 2 * gu, 0, (blk,))
                    piece(p, wait, 2 * gu + 1, sum_gu.at[p, gu, 1], k1 + 2 * gu, 0, (0,))

            kd = hidden // DN_SLOT
            off = pl.multiple_of(hidden - kd * DN_SLOT, DN_PIECE)
            m = jnp.minimum((DN_SLOT - off) // DN_PIECE, n_dn)
            for mm in range(1, n_dn + 1):
                @pl.when(m == mm)
                def _():
                    rows = mm * DN_PIECE
                    piece(p, wait, 2 * n_gu, sum_dn.at[p, pl.ds(0, rows), :], kd, 1, (pl.ds(off, rows), slice(None)))
                    if mm < n_dn:
                        piece(p, wait, 2 * n_gu + 1, sum_dn.at[p, pl.ds(rows, FF_CHUNK - rows), :], kd + 1, 1,
                              (pl.ds(0, FF_CHUNK - rows), slice(None)))

        def piece(p, wait, pi, src, k, t, where):
            own_ref, land_ref = ((own_gu_ref, land_gu_ref), (own_dn_ref, land_dn_ref))[t]
            kx, ky = k // 2, k % 2
            fx, fy = (kx != x).astype(jnp.int32), (ky != y).astype(jnp.int32)
            local = (fx + fy) == 0
            j = jnp.maximum(fx + 2 * fy - 1, 0)

            @pl.when(local)
            def _():
                cp = pltpu.make_async_copy(src, own_ref.at[where], lsem.at[p, pi])
                if wait:
                    cp.wait()
                else:
                    cp.start()

            @pl.when(jnp.logical_not(local))
            def _():
                cp = pltpu.make_async_remote_copy(src, land_ref.at[(j,) + where], ssend.at[p, pi],
                                                  rrecv.at[t, j], device_id=(kx, ky, core), device_id_type=MESH)
                if wait:
                    cp.wait_send()
                else:
                    cp.start()

        def add_and_scatter(cc):
            p = cc % 2
            pair_copy(cc, 0).wait_recv()
            pair_copy(cc, 1).wait_recv()
            s_gu = (mine_gu[...] + pl_gu[cc].astype(F32)).astype(BF)
            for hc in range(n_gu):
                sum_gu[p, :, hc] = s_gu[:, :, hc * GU_PIECE:(hc + 1) * GU_PIECE]
            sum_dn[p] = (mine_dn[...] + pl_dn[cc].astype(F32)).astype(BF)
            scatter(cc, wait=False)

        @pl.when(c >= 1)
        def _():
            @pl.when(c >= 3)
            def _():
                scatter(c - 3, wait=True)
            add_and_scatter(c - 1)

        @pl.when(c >= 2)
        def _():
            pair_copy(c - 2, 0).wait_send()
            pair_copy(c - 2, 1).wait_send()

        p = c % 2
        my_rows = pl.ds(pl.multiple_of(core * HALF_D, HALF_D), HALF_D)
        sib_rows = pl.ds(pl.multiple_of((1 - core) * HALF_D, HALF_D), HALF_D)
        d_v = accd[...]
        sib_gu[p, 0] = accg[sib_rows, :].astype(BF)
        sib_gu[p, 1] = accu[sib_rows, :].astype(BF)
        sib_dn[p] = jnp.where(lower, d_v[:, HALF_D:], d_v[:, :HALF_D]).astype(BF)
        mine_gu[0] = accg[my_rows, :]
        mine_gu[1] = accu[my_rows, :]
        mine_dn[...] = jnp.where(lower, d_v[:, :HALF_D], d_v[:, HALF_D:])
        pair_copy(c, 0).start()
        pair_copy(c, 1).start()

        @pl.when(c == nc - 1)
        def _():
            scatter(nc - 3, wait=True)
            add_and_scatter(nc - 1)
            for cc in (nc - 2, nc - 1):
                pair_copy(cc, 0).wait_send()
                pair_copy(cc, 1).wait_send()
                scatter(cc, wait=True)
            for t, land_ref in enumerate((land_gu_ref, land_dn_ref)):
                for j in range(N_CHIPS - 1):
                    pltpu.make_async_remote_copy(land_ref.at[j], land_ref.at[j], ssend.at[0, 0], rrecv.at[t, j],
                                                 device_id=(x, y, core), device_id_type=MESH).wait_recv()

    def body(kc_ref, dx_ref, x_ref, f_ref, gp_ref, up_ref, pg_ref, wgu_ref, wd_ref, qg_ref,
             dx1_ref, dpg_ref, dqg_ref, own_gu_ref, land_gu_ref, own_dn_ref, land_dn_ref,
             h_s, df_s, dh_s, accg, accu, accd, *comm):
        c = pl.program_id(0)
        i = pl.program_id(1)
        rows = pl.ds(pl.multiple_of(i * tm, tm), tm)
        pg = pg_ref[layer:layer + 1, :]

        @pl.when((c == 0) & (i == 0))
        def _():
            dpg_ref[...] = jnp.zeros_like(dpg_ref)
            dqg_ref[...] = jnp.zeros_like(dqg_ref)

        @pl.when(c == 0)
        def _():
            h_s[rows, :] = _rms_fwd(x_ref[...], pg).astype(BF)
            df, prod = _rms_bwd(f_ref[...], qg_ref[layer:layer + 1, :], dx_ref[...])
            df_s[rows, :] = df.astype(BF)
            dqg_ref[...] += _rowsum(prod)

        @pl.when(i == 0)
        def _():
            accg[...] = jnp.zeros_like(accg)
            accu[...] = jnp.zeros_like(accu)
            accd[...] = jnp.zeros_like(accd)

        h = h_s[rows, :]
        df = df_s[rows, :]
        wg = wgu_ref[0]
        wu = wgu_ref[1]
        g = gp_ref[...].astype(F32)
        u = up_ref[...].astype(F32)
        sg = _sigmoid(g)
        a = g * sg
        dact = _dot_nt(df, wd_ref[...])
        accd[...] += _dot_tn((a * u).astype(BF), df)
        du = (dact * a).astype(BF)
        dg = (dact * u * (sg * (1.0 + g * (1.0 - sg)))).astype(BF)
        accg[...] += _dot_tn(h, dg)
        accu[...] += _dot_tn(h, du)
        dh = _dot_nt(dg, wg) + _dot_nt(du, wu)

        @pl.when(c == 0)
        def _():
            dh_s[rows, :] = dh

        @pl.when((c > 0) & (c < nc - 1))
        def _():
            dh_s[rows, :] += dh

        @pl.when(c == nc - 1)
        def _():
            dxp, prod = _rms_bwd(x_ref[...], pg, dh_s[rows, :] + dh)
            dpg_ref[...] += _rowsum(prod)
            dx1_ref[...] = dx_ref[...] + dxp

        @pl.when(i == n - 1)
        def _():
            exchange(kc_ref, c, accg, accu, accd, own_gu_ref, land_gu_ref, own_dn_ref, land_dn_ref, *comm)

    dma = pltpu.SemaphoreType.DMA
    return _call(
        body, name=f"ffn_bwd{layer}", grid=(nc, n),
        in_specs=[pl.BlockSpec((tm, D), edge_rows), pl.BlockSpec((tm, D), edge_rows),
                  pl.BlockSpec((tm, D), lambda c, i, kc_ref: (jnp.where(c == 0, i, n - 1), 0),
                               pipeline_mode=pl.Buffered(1)),
                  pl.BlockSpec((tm, FF_CHUNK), lambda c, i, kc_ref: (i, chunk_at(c, kc_ref))),
                  pl.BlockSpec((tm, FF_CHUNK), lambda c, i, kc_ref: (i, chunk_at(c, kc_ref))),
                  VSPEC,
                  pl.BlockSpec((2, D, FF_CHUNK), lambda c, i, kc_ref: (0, 0, chunk_at(c, kc_ref))),
                  pl.BlockSpec((FF_CHUNK, D), lambda c, i, kc_ref: (chunk_at(c, kc_ref), 0)),
                  VSPEC],
        out_specs=[pl.BlockSpec((tm, D), lambda c, i, kc_ref: (jnp.where(c == nc - 1, i, 0), 0)),
                   _const_spec((1, D)), _const_spec((1, D)), ANYSPEC, ANYSPEC, ANYSPEC, ANYSPEC],
        out_shape=[_sds((s_len, D)), _sds((1, D)), _sds((1, D)),
                   _sds((n_blk, HALF_D, GU_PIECE), BF), _sds((N_CHIPS - 1, n_blk, HALF_D, GU_PIECE), BF),
                   _sds((DN_SLOT, HALF_D), BF), _sds((N_CHIPS - 1, DN_SLOT, HALF_D), BF)],
        scratch_shapes=[pltpu.VMEM((s_len, D), BF), pltpu.VMEM((s_len, D), BF), pltpu.VMEM((s_len, D), F32),
                        pltpu.VMEM((D, FF_CHUNK), F32), pltpu.VMEM((D, FF_CHUNK), F32),
                        pltpu.VMEM((FF_CHUNK, D), F32),
                        pltpu.VMEM((nc, 2, HALF_D, FF_CHUNK), BF), pltpu.VMEM((nc, FF_CHUNK, HALF_D), BF),
                        pltpu.VMEM((2, 2, HALF_D, FF_CHUNK), BF), pltpu.VMEM((2, FF_CHUNK, HALF_D), BF),
                        pltpu.VMEM((2, HALF_D, FF_CHUNK), F32), pltpu.VMEM((FF_CHUNK, HALF_D), F32),
                        pltpu.VMEM((2, 2, n_gu, HALF_D, GU_PIECE), BF), pltpu.VMEM((2, FF_CHUNK, HALF_D), BF),
                        dma((2, 2)), dma((nc, 2)), dma((2, n_pieces)), dma((2, n_pieces)), dma((2, N_CHIPS - 1))],
        args=[dx2, x1, f, g_pre, u_pre, pre_g, wgu, wd, post_g], rider=rider, prefetch=kc)


def _ple_fwd(layer, x2, p, ple_g, w_gate, w_proj, post_g, target=None, qkv=None, rider=None):
    s_len = x2.shape[0]
    final = target is not None
    assert not (final and qkv)

    def body(*refs):
        if final:
            x_ref, p_ref, g_ref, wg_ref, wp_ref, qg_ref, t_ref, z_ref, pe_ref, dx_ref, lv_ref = refs
        elif qkv:
            (x_ref, p_ref, g_ref, wg_ref, wp_ref, qg_ref, ng_ref, kg_ref, wq_ref, wkv_ref,
             z_ref, pe_ref, x3_ref, q_ref, kv_ref) = refs
        else:
            x_ref, p_ref, g_ref, wg_ref, wp_ref, qg_ref, z_ref, pe_ref, x3_ref = refs
        x = x_ref[...]
        r = _rms_fwd(x, g_ref[layer:layer + 1, :]).astype(BF)
        z = _dot(r, wg_ref[...])
        pe = _dot(p_ref[...].astype(BF), wp_ref[...])
        z_ref[...] = z
        pe_ref[...] = pe
        x3 = x + _rms_fwd(pe * _sigmoid(z), qg_ref[layer:layer + 1, :])
        if final:
            @pl.when(pl.program_id(0) == 0)
            def _():
                lv_ref[...] = jnp.zeros_like(lv_ref)
            err = x3 - t_ref[...]
            dx_ref[...] = err * (1.0 / D)
            lv_ref[...] += _rowsum(err * err)
        else:
            x3_ref[...] = x3
        if qkv:
            q_ref[...] = _dot(_rms_fwd(x3, ng_ref[layer + 1:layer + 2, :]).astype(BF), wq_ref[...]).astype(BF)
            kv_ref[...] = _dot(_rms_fwd(x3, kg_ref[...]).astype(BF), wkv_ref[...]).astype(BF)

    p_spec = pl.BlockSpec((None, TM, PLE), lambda i: (layer, i, 0))
    in_specs = [_row_spec(TM), p_spec, VSPEC, VSPEC, VSPEC, VSPEC]
    args = [x2, p, ple_g, w_gate, w_proj, post_g]
    out_specs = [_row_spec(TM), _row_spec(TM), _row_spec(TM)]
    out_shape = [_sds((s_len, D))] * 3
    if qkv:
        in_specs += [VSPEC] * 4
        args += list(qkv)
        out_specs += [_row_spec(TM), _row_spec(TM, 2 * KVD)]
        out_shape += [_sds((s_len, D), BF), _sds((s_len, 2 * KVD), BF)]
    if final:
        in_specs.append(_row_spec(TM))
        args.append(target)
        out_specs.append(_const_spec((1, D)))
        out_shape.append(_sds((1, D)))
    return _call(body, name=f"ple_fwd{layer}", grid=(s_len // TM,), in_specs=in_specs, out_specs=out_specs,
                 out_shape=out_shape, args=args, rider=rider)


def _ple_bwd(layer, dx3, x2, z, pe, p, ple_g, w_gate, post_g, rider=None):
    s_len = x2.shape[0]
    n = s_len // TM

    def body(dx_ref, x_ref, z_ref, pe_ref, p_ref, g_ref, wg_ref, qg_ref,
             dx2_ref, dwg_ref, dwp_ref, dg_ref, dqg_ref, gacc, pacc):
        i = pl.program_id(0)

        @pl.when(i == 0)
        def _():
            gacc[...] = jnp.zeros_like(gacc)
            pacc[...] = jnp.zeros_like(pacc)
            dg_ref[...] = jnp.zeros_like(dg_ref)
            dqg_ref[...] = jnp.zeros_like(dqg_ref)

        dx = dx_ref[...]
        x = x_ref[...]
        pe_v = pe_ref[...]
        gate = _sigmoid(z_ref[...])
        de, prod = _rms_bwd(pe_v * gate, qg_ref[layer:layer + 1, :], dx)
        dqg_ref[...] += _rowsum(prod)
        dpe = (de * gate).astype(BF)
        dz = (de * pe_v * gate * (1.0 - gate)).astype(BF)
        pacc[...] += _dot_tn(p_ref[...].astype(BF), dpe)
        g = g_ref[layer:layer + 1, :]
        r = _rms_fwd(x, g).astype(BF)
        gacc[...] += _dot_tn(r, dz)
        dr = _dot_nt(dz, wg_ref[...])
        dxp, prod2 = _rms_bwd(x, g, dr)
        dg_ref[...] += _rowsum(prod2)
        dx2_ref[...] = dx + dxp

        @pl.when(i == n - 1)
        def _():
            dwg_ref[...] = gacc[...].astype(BF)
            dwp_ref[...] = pacc[...].astype(BF)

    p_spec = pl.BlockSpec((None, TM, PLE), lambda i: (layer, i, 0))
    return _call(
        body, name=f"ple_bwd{layer}", grid=(n,),
        in_specs=[_row_spec(TM), _row_spec(TM), _row_spec(TM), _row_spec(TM), p_spec, VSPEC, VSPEC, VSPEC],
        out_specs=[_row_spec(TM), _const_spec((D, D)), _const_spec((PLE, D)), _const_spec((1, D)), _const_spec((1, D))],
        out_shape=[_sds((s_len, D)), _sds((D, D), BF), _sds((PLE, D), BF), _sds((1, D)), _sds((1, D))],
        scratch_shapes=[pltpu.VMEM((D, D), F32), pltpu.VMEM((PLE, D), F32)],
        args=[dx3, x2, z, pe, p, ple_g, w_gate, post_g], rider=rider)


def _qkv_bwd(dq, dkv, x3, dx4, q_g, kv_g, w_q, w_kv):
    s_len = x3.shape[0]
    n = s_len // TM

    def body(dq_ref, dkv_ref, x_ref, dx_ref, qg_ref, kg_ref, wq_ref, wkv_ref,
             dx3_ref, dwq_ref, dwkv_ref, dqg_ref, dkg_ref, qacc, kacc):
        i = pl.program_id(0)

        @pl.when(i == 0)
        def _():
            qacc[...] = jnp.zeros_like(qacc)
            kacc[...] = jnp.zeros_like(kacc)
            dqg_ref[...] = jnp.zeros_like(dqg_ref)
            dkg_ref[...] = jnp.zeros_like(dkg_ref)

        x = x_ref[...]
        qg = qg_ref[1:2, :]
        kg = kg_ref[...]
        dq_v = dq_ref[...]
        dkv_v = dkv_ref[...].astype(BF)
        qacc[...] += _dot_tn(_rms_fwd(x, qg).astype(BF), dq_v)
        kacc[...] += _dot_tn(_rms_fwd(x, kg).astype(BF), dkv_v)
        dxq, prod_q = _rms_bwd(x, qg, _dot_nt(dq_v, wq_ref[...]))
        dxk, prod_k = _rms_bwd(x, kg, _dot_nt(dkv_v, wkv_ref[...]))
        dqg_ref[...] += _rowsum(prod_q)
        dkg_ref[...] += _rowsum(prod_k)
        dx3_ref[...] = dx_ref[...] + dxq + dxk

        @pl.when(i == n - 1)
        def _():
            dwq_ref[...] = qacc[...].astype(BF)
            dwkv_ref[...] = kacc[...].astype(BF)

    outs, _ = _call(
        body, name="qkv_bwd", grid=(n,),
        in_specs=[_row_spec(TM), _row_spec(TM, 2 * KVD), _row_spec(TM), _row_spec(TM), VSPEC, VSPEC, VSPEC, VSPEC],
        out_specs=[_row_spec(TM), _const_spec((D, D)), _const_spec((D, 2 * KVD)),
                   _const_spec((1, D)), _const_spec((1, D))],
        out_shape=[_sds((s_len, D)), _sds((D, D), BF), _sds((D, 2 * KVD), BF), _sds((1, D)), _sds((1, D))],
        scratch_shapes=[pltpu.VMEM((D, D), F32), pltpu.VMEM((D, 2 * KVD), F32)],
        args=[dq, dkv, x3, dx4, q_g, kv_g, w_q, w_kv])
    return outs


def _attn_group(i, q, kvw, sink_ref, g):
    rows = GQA * BLK
    heads = [GQA * g + j for j in range(GQA)]
    off = jnp.where(i > 0, BLK, 0)
    row = lax.broadcasted_iota(jnp.int32, (rows, 2 * BLK), 0)
    rel = (row % BLK) - lax.broadcasted_iota(jnp.int32, (rows, 2 * BLK), 1) + off
    valid = (rel >= 0) & (rel < BLK)
    head_of_row = lax.broadcasted_iota(jnp.int32, (rows, 1), 0) // BLK
    slope = jnp.zeros((rows, 1), F32)
    sink = jnp.zeros((rows, 1), F32)
    for j, h in enumerate(heads):
        slope = jnp.where(head_of_row == j, SLOPES[h], slope)
        sink = jnp.where(head_of_row == j, sink_ref[0, h], sink)
    qs = jnp.concatenate([q[:, h * HEAD_DIM:(h + 1) * HEAD_DIM] for h in heads], axis=0)
    k = kvw[:, g * HEAD_DIM:(g + 1) * HEAD_DIM]
    v = kvw[:, KVD + g * HEAD_DIM:KVD + (g + 1) * HEAD_DIM]
    s = _dot_nt(qs, k) * ATT_SCALE - slope * rel.astype(F32)
    s = jnp.where(valid, s, NEG_INF)
    m = jnp.maximum(jnp.max(s, axis=-1, keepdims=True), sink)
    e = jnp.exp(s - m)
    es = jnp.exp(sink - m)
    inv = 1.0 / (jnp.sum(e, axis=-1, keepdims=True) + es)
    return e * inv, es * inv, qs, k, v


def _unstack_heads(stacked):
    return [stacked[j * BLK:(j + 1) * BLK, :] for j in range(GQA)]


def _kv_window(kv_ref, i):
    ks = pl.multiple_of(jnp.maximum(i * BLK - BLK, 0), BLK)
    return ks, kv_ref[pl.ds(ks, 2 * BLK), :]


def _attn_fwd(q, kv, sinks, x3, w_o, post_g, rider=None):
    s_len = q.shape[0]

    def body(q_ref, kv_ref, sk_ref, x_ref, wo_ref, g_ref, a_ref, y_ref, x4_ref):
        i = pl.program_id(0)
        _, kvw = _kv_window(kv_ref, i)
        q = q_ref[...]
        outs = []
        for g in range(N_KV_HEADS):
            p, _, _, _, v = _attn_group(i, q, kvw, sk_ref, g)
            outs += _unstack_heads(_dot(p.astype(BF), v))
        attn = jnp.concatenate(outs, axis=1)
        a_ref[...] = attn
        y = _dot(attn.astype(BF), wo_ref[...])
        y_ref[...] = y
        x4_ref[...] = x_ref[...] + _rms_fwd(y, g_ref[1:2, :])

    return _call(body, name="attn_fwd", grid=(s_len // BLK,),
                 in_specs=[_row_spec(BLK), VSPEC, SSPEC, _row_spec(BLK), VSPEC, VSPEC],
                 out_specs=[_row_spec(BLK)] * 3, out_shape=[_sds((s_len, D))] * 3,
                 args=[q, kv, sinks, x3, w_o, post_g], rider=rider)


ATT_STEP_BLOCKS = 2


def _attn_bwd(dx4, y, attn, q, kv, sinks, w_o, post_g, rider=None):
    s_len = q.shape[0]
    rows = ATT_STEP_BLOCKS * BLK
    n = s_len // rows

    def body(dx_ref, y_ref, a_ref, q_ref, kv_ref, sk_ref, wo_ref, g_ref,
             dq_ref, dkv_ref, dwo_ref, dg_ref, dsk_ref, wacc):
        i = pl.program_id(0)

        @pl.when(i == 0)
        def _():
            dkv_ref[...] = jnp.zeros_like(dkv_ref)
            wacc[...] = jnp.zeros_like(wacc)
            dg_ref[...] = jnp.zeros_like(dg_ref)
            dsk_ref[...] = jnp.zeros_like(dsk_ref)

        dy, prod = _rms_bwd(y_ref[...], g_ref[1:2, :], dx_ref[...])
        dg_ref[...] += _rowsum(prod)
        dyb = dy.astype(BF)
        attn_all = a_ref[...]
        wacc[...] += _dot_tn(attn_all.astype(BF), dyb)
        d_o_all = _dot_nt(dyb, wo_ref[...])
        q_all = q_ref[...]
        lane = lax.broadcasted_iota(jnp.int32, (1, D), 1)
        dsk = jnp.zeros((1, D), F32)
        for sub in range(ATT_STEP_BLOCKS):
            blk = i * ATT_STEP_BLOCKS + sub
            sl = slice(sub * BLK, (sub + 1) * BLK)
            d_o, q = d_o_all[sl, :], q_all[sl, :]
            dod = d_o * attn_all[sl, :]
            ks, kvw = _kv_window(kv_ref, blk)
            dqs, dks, dvs = [], [], []
            for g in range(N_KV_HEADS):
                p, ps, qs, k, v = _attn_group(blk, q, kvw, sk_ref, g)
                cols = [slice((GQA * g + j) * HEAD_DIM, (GQA * g + j + 1) * HEAD_DIM) for j in range(GQA)]
                do_s = jnp.concatenate([d_o[:, c] for c in cols], axis=0).astype(BF)
                dsum = jnp.concatenate([jnp.sum(dod[:, c], axis=-1, keepdims=True) for c in cols], axis=0)
                dp = _dot_nt(do_s, v)
                dsb = (p * (dp - dsum) * ATT_SCALE).astype(BF)
                sink_part = ps * dsum
                for j in range(GQA):
                    dsk = dsk + jnp.where(lane == GQA * g + j, -_rowsum(sink_part[j * BLK:(j + 1) * BLK, :]), 0.0)
                dqs += _unstack_heads(_dot(dsb, k))
                dks.append(_dot_tn(dsb, qs))
                dvs.append(_dot_tn(p.astype(BF), do_s))
            dq_ref[sl, :] = jnp.concatenate(dqs, axis=1).astype(BF)
            dkv_ref[pl.ds(ks, 2 * BLK), :] += jnp.concatenate(dks + dvs, axis=1)
        dsk_ref[...] += dsk

        @pl.when(i == n - 1)
        def _():
            dwo_ref[...] = wacc[...].astype(BF)

    return _call(
        body, name="attn_bwd", grid=(n,),
        in_specs=[_row_spec(rows), _row_spec(rows), _row_spec(rows), _row_spec(rows), VSPEC, SSPEC, VSPEC, VSPEC],
        out_specs=[_row_spec(rows), _const_spec((s_len, 2 * KVD)), _const_spec((D, D)),
                   _const_spec((1, D)), _const_spec((1, D))],
        out_shape=[_sds((s_len, D), BF), _sds((s_len, 2 * KVD)), _sds((D, D), BF), _sds((1, D)), _sds((1, D))],
        scratch_shapes=[pltpu.VMEM((D, D), F32)],
        args=[dx4, y, attn, q, kv, sinks, w_o, post_g], rider=rider)


Big = collections.namedtuple("Big", "name src layer L A R C rb")


def _bigs():
    out = {"pool_w": Big("pool_w", "pool_w", None, 4, 4, POOL_G // N_CHIPS, POOL_G, 32)}
    for l in range(2):
        out[f"w_gu{l}"] = Big(f"w_gu{l}", "w_gu", l, 1, 2, D, FF_HALF, 256)
        out[f"w_down{l}"] = Big(f"w_down{l}", "w_down", l, 1, 4, FF // N_CHIPS, D, 352)
        out[f"w_ple_gate{l}"] = Big(f"w_ple_gate{l}", "w_ple_gate", l, 1, 4, D // N_CHIPS, D, 128)
        out[f"w_ple_proj{l}"] = Big(f"w_ple_proj{l}", "w_ple_proj", l, 1, 1, PLE, D // N_CHIPS, 128)
    out["w_q"] = Big("w_q", "w_q", None, 1, 4, D // N_CHIPS, D, 128)
    out["w_o"] = Big("w_o", "w_o", None, 1, 4, D // N_CHIPS, D, 128)
    out["w_kv"] = Big("w_kv", "w_kv", None, 1, 4, D // N_CHIPS, 2 * KVD, 128)
    return out


BIGS = _bigs()
POOL_SCALE = Big("pool_scale", "pool_scale", None, 1, 1, 1, D // N_CHIPS, 1)
BIG_SOURCES = ("w_gu", "w_down", "w_ple_gate", "w_ple_proj", "w_q", "w_o", "w_kv", "pool_w")


def _ncb(t):
    return N_CHIPS // t.A


def _full_shape(t, rows=None):
    return (t.L, t.A, t.R if rows is None else rows, _ncb(t) * t.C)


def _slot_index(t, k):
    return k // _ncb(t), k % _ncb(t)


def _slot(ref, t, k, row0, rows):
    a, cb = _slot_index(t, k)
    return ref.at[:, a, pl.ds(row0, rows), pl.ds(pl.multiple_of(cb * t.C, 128), t.C)]


def _place_job(t, w, out_dtype=BF):
    nb = next((nb for nb in (8, 4, 2, 1) if t.R % (16 * nb) == 0), 1) if t.L == 1 else 1
    rb = t.R // nb

    def fn(j, kc_ref, ins, outs):
        outs[0][...] = ins[0][...].astype(out_dtype)

    def in_map(j, kc_ref):
        return (j // nb if t.layer is None else t.layer, j % nb, 0)

    def out_map(j, kc_ref):
        a, cb = _slot_index(t, kc_ref[0])
        return (j // nb, a, j % nb, cb)

    return Job(t.L * nb, [(w, (None, rb, t.C), in_map)],
               [(_sds(_full_shape(t), out_dtype), (None, None, rb, t.C), out_map)], fn)


def _mesh_position():
    x, y, c = lax.axis_index("x"), lax.axis_index("y"), lax.axis_index("c")
    chips = [(1 - x, y), (x, 1 - y), (1 - x, 1 - y)]
    return x, y, c, chips


DIRECT_BELOW = 1024


def _gather_rider(parts, fulls):
    nt = len(parts)
    TO_X, TO_Y, FWD_X, FWD_Y, SIB_X, SIB_Y, SIB_D = range(7)

    def rows_of(ti, core):
        t, r0, r1 = parts[ti]
        h = (r1 - r0) // 2
        return r0 + core * h, h

    def copy(outs, sems, kind, ti, k_src, row0, rows, dev):
        region = _slot(outs[ti], parts[ti][0], k_src, row0, rows)
        return pltpu.make_async_remote_copy(region, region, sems[0].at[ti, kind], sems[1].at[ti, kind],
                                            device_id=dev, device_id_type=MESH)

    def plan(outs, sems):
        x, y, c, _ = _mesh_position()
        me, kx, ky, kd = 2 * x + y, 2 * (1 - x) + y, 2 * x + (1 - y), 2 * (1 - x) + (1 - y)
        dev_x, dev_y, dev_d, sib = (1 - x, y, c), (x, 1 - y, c), (1 - x, 1 - y, c), (x, y, 1 - c)

        def whole(ti):
            return 0, parts[ti][0].R

        def mk(kind, k_send, k_recv, dev, send_rows, recv_rows):
            def build(ti, side):
                k_src = k_send if side == "s" else k_recv
                row0, rows = (send_rows if side == "s" else recv_rows)(ti)
                return copy(outs, sems, kind, ti, k_src, row0, rows, dev)
            return build

        def first_half(core):
            return lambda ti: (rows_of(ti, core)[0], rows_of(ti, core)[1] // 2)

        def second_half(core):
            return lambda ti: (rows_of(ti, core)[0] + rows_of(ti, core)[1] // 2, rows_of(ti, core)[1] // 2)

        mine = lambda ti: rows_of(ti, c)
        theirs = lambda ti: rows_of(ti, 1 - c)
        split = {
            TO_X: mk(TO_X, me, kx, dev_x, mine, mine),
            TO_Y: mk(TO_Y, me, ky, dev_y, mine, mine),
            FWD_X: mk(FWD_X, ky, kd, dev_x, first_half(c), first_half(c)),
            FWD_Y: mk(FWD_Y, kx, kd, dev_y, second_half(c), second_half(c)),
            SIB_X: mk(SIB_X, kx, kx, sib, mine, theirs),
            SIB_Y: mk(SIB_Y, ky, ky, sib, mine, theirs),
            SIB_D: mk(SIB_D, kd, kd, sib, mine, theirs),
        }
        direct = {
            TO_X: mk(TO_X, me, kx, dev_x, whole, whole),
            TO_Y: mk(TO_Y, me, ky, dev_y, whole, whole),
            FWD_X: mk(FWD_X, me, kd, dev_d, whole, whole),
        }
        return split, direct

    is_split = [t.L * t.R * t.C >= DIRECT_BELOW for t, _, _ in parts]
    assert all(s or (r0, r1) == (0, t.R) for s, (t, r0, r1) in zip(is_split, parts))

    def start(ins, outs, sems):
        split, direct = plan(outs, sems)
        for ti in range(nt):
            kinds = split if is_split[ti] else direct
            kinds[TO_X](ti, "s").start()
            kinds[TO_Y](ti, "s").start()
            if not is_split[ti]:
                kinds[FWD_X](ti, "s").start()

    def mid(ins, outs, sems):
        split, _ = plan(outs, sems)
        for ti in range(nt):
            if is_split[ti]:
                split[TO_Y](ti, "r").wait_recv()
                split[FWD_X](ti, "s").start()
                split[SIB_Y](ti, "s").start()
        for ti in range(nt):
            if is_split[ti]:
                split[TO_X](ti, "r").wait_recv()
                split[FWD_Y](ti, "s").start()
                split[SIB_X](ti, "s").start()

    def finish(ins, outs, sems):
        split, direct = plan(outs, sems)
        for ti in range(nt):
            if is_split[ti]:
                split[FWD_X](ti, "r").wait_recv()
                split[FWD_Y](ti, "r").wait_recv()
                split[SIB_D](ti, "s").start()
            else:
                for kind in (TO_X, TO_Y, FWD_X):
                    direct[kind](ti, "r").wait_recv()
        for ti in range(nt):
            if is_split[ti]:
                for kind in (SIB_X, SIB_Y, SIB_D):
                    split[kind](ti, "r").wait_recv()
        for ti in range(nt):
            kinds = split if is_split[ti] else direct
            for kind in kinds:
                kinds[kind](ti, "s").wait_send()

    sems = pltpu.SemaphoreType.DMA((nt, 7))
    return Rider(list(fulls), [_sds(a.shape, a.dtype) for a in fulls], {i: i for i in range(nt)},
                 [sems, sems], start, mid, finish)


def _pair_exchange_rider(specs, grads):
    nt = len(specs)

    def copy(ins, outs, sems, ti, c, sibling):
        half = specs[ti].R // 2
        return pltpu.make_async_remote_copy(ins[ti].at[:, :, pl.ds((1 - c) * half, half), :], outs[ti],
                                            sems[0].at[ti], sems[1].at[ti], device_id=sibling, device_id_type=MESH)

    def start(ins, outs, sems):
        x, y, c, _ = _mesh_position()
        for ti in range(nt):
            copy(ins, outs, sems, ti, c, (x, y, 1 - c)).start()

    def finish(ins, outs, sems):
        x, y, c, _ = _mesh_position()
        for ti in range(nt):
            copy(ins, outs, sems, ti, c, (x, y, 1 - c)).wait()

    sems = pltpu.SemaphoreType.DMA((nt,))
    return Rider(list(grads), [_sds(_full_shape(t, t.R // 2), BF) for t in specs], {}, [sems, sems], start, None, finish)


def _pair_sum_job(t, g, land):
    assert t.L == 1
    half = t.R // 2
    nj = half // t.rb
    block = (None, t.A, t.rb, _ncb(t) * t.C)

    def fn(j, kc_ref, ins, outs):
        outs[0][...] = (ins[0][...].astype(F32) + ins[1][...].astype(F32)).astype(BF)

    return Job(nj,
               [(g, block, lambda j, kc_ref: (0, 0, kc_ref[1] * nj + j, 0)),
                (land, block, lambda j, kc_ref: (0, 0, j, 0))],
               [(_sds(_full_shape(t, half), BF), block, lambda j, kc_ref: (0, 0, j, 0))], fn)


def _scatter_rider(specs, sums):
    nt = len(specs)

    def copy(ins, outs, sems, ti, j, chip, c):
        t = specs[ti]
        cx, cy = chip
        return pltpu.make_async_remote_copy(_slot(ins[ti], t, 2 * cx + cy, 0, t.R // 2), outs[ti].at[j],
                                            sems[0].at[ti, j], sems[1].at[ti, j],
                                            device_id=(cx, cy, c), device_id_type=MESH)

    def start(ins, outs, sems):
        _, _, c, chips = _mesh_position()
        for j, chip in enumerate(chips):
            for ti in range(nt):
                copy(ins, outs, sems, ti, j, chip, c).start()

    def finish(ins, outs, sems):
        _, _, c, chips = _mesh_position()
        for j, chip in enumerate(chips):
            for ti in range(nt):
                copy(ins, outs, sems, ti, j, chip, c).wait()

    sems = pltpu.SemaphoreType.DMA((nt, N_CHIPS - 1))
    return Rider(list(sums), [_sds((N_CHIPS - 1, t.L, t.R // 2, t.C), BF) for t in specs], {}, [sems, sems],
                 start, None, finish)


def _chip_sum_job(ts, landed):
    t0 = ts[0]
    assert t0.L == 1
    half = t0.R // 2
    nj = half // t0.rb

    def local(j, li):
        return jnp.clip(j - li * nj, 0, nj - 1)

    ins = []
    for li, t in enumerate(ts):
        s, land = landed[t.name]

        def own_map(j, kc_ref, li=li, t=t):
            a, cb = _slot_index(t, kc_ref[0])
            return (0, a, local(j, li), cb)

        ins.append((s, (None, None, t.rb, t.C), own_map))
        ins.append((land, (N_CHIPS - 1, None, t.rb, t.C), lambda j, kc_ref, li=li: (0, 0, local(j, li), 0)))

    def fn(j, kc_ref, in_refs, outs):
        for li in range(len(ts)):
            @pl.when(j // nj == li)
            def _():
                acc = in_refs[2 * li][...].astype(F32)
                for k in range(N_CHIPS - 1):
                    acc = acc + in_refs[2 * li + 1][k].astype(F32)
                outs[0][...] = acc

    return Job(len(ts) * nj, ins,
               [(_sds((len(ts), t0.R, t0.C)), (None, t0.rb, t0.C),
                 lambda j, kc_ref: (j // nj, kc_ref[1] * nj + j % nj, 0))], fn)


def _adamw_job(rb, w, g, m, v):
    n_layers, r, c = w.shape
    nb = r // rb
    block = (None, rb, c)
    index = lambda j, kc_ref: (j // nb, j % nb, 0)

    def fn(j, kc_ref, ins, outs):
        g_v = ins[1][...]
        outs[0][...] = g_v
        outs[1][...], outs[2][...], outs[3][...] = _adamw_math(ins[0][...], g_v, ins[2][...], ins[3][...])

    return Job(n_layers * nb, [(a, block, index) for a in (w, g, m, v)],
               [(_sds(w.shape), block, index)] * 4, fn)


def _chip_sum_fused_job(ts, fused, by_cols):
    t0 = ts[0]
    own0 = fused[t0.name][0]
    if by_cols:
        rows, cols = own0.shape
    else:
        nb, rows, bw = own0.shape
        cols = nb * bw
    nj = rows // t0.rb

    def local(j, li):
        return jnp.clip(j - li * nj, 0, nj - 1)

    ins = []
    for li, t in enumerate(ts):
        own, land = fused[t.name]
        if by_cols:
            ins.append((own, (t.rb, cols), lambda j, kc_ref, li=li: (local(j, li), 0)))
            ins.append((land, (N_CHIPS - 1, t.rb, cols), lambda j, kc_ref, li=li: (0, local(j, li), 0)))
        else:
            ins.append((own, (nb, t.rb, bw), lambda j, kc_ref, li=li: (0, local(j, li), 0)))
            ins.append((land, (N_CHIPS - 1, nb, t.rb, bw), lambda j, kc_ref, li=li: (0, 0, local(j, li), 0)))

    def fn(j, kc_ref, in_refs, outs):
        for li in range(len(ts)):
            @pl.when(j // nj == li)
            def _():
                acc = in_refs[2 * li][...].astype(F32)
                for k in range(N_CHIPS - 1):
                    acc = acc + in_refs[2 * li + 1][k].astype(F32)
                outs[0][...] = acc if by_cols else jnp.concatenate([acc[b] for b in range(nb)], axis=1)

    def out_map(j, kc_ref):
        return (j // nj, j % nj, kc_ref[1]) if by_cols else (j // nj, kc_ref[1] * nj + j % nj, 0)

    return Job(len(ts) * nj, ins, [(_sds((len(ts), t0.R, t0.C)), (None, t0.rb, cols), out_map)], fn)


def _share_rider(halves, by_cols):
    nt = len(halves)

    def copy(outs, sems, ti, core, sibling):
        axis = 2 if by_cols[ti] else 1
        half = halves[ti].shape[axis] // 2
        piece = pl.ds(pl.multiple_of(core * half, 128 if by_cols[ti] else 8), half)
        part = outs[ti].at[:, :, piece] if by_cols[ti] else outs[ti].at[:, piece, :]
        return pltpu.make_async_remote_copy(part, part, sems[0].at[ti], sems[1].at[ti],
                                            device_id=sibling, device_id_type=MESH)

    def start(ins, outs, sems):
        x, y, c, _ = _mesh_position()
        for ti in range(nt):
            copy(outs, sems, ti, c, (x, y, 1 - c)).start()

    def finish(ins, outs, sems):
        x, y, c, _ = _mesh_position()
        for ti in range(nt):
            copy(outs, sems, ti, 1 - c, (x, y, 1 - c)).wait_recv()
        for ti in range(nt):
            copy(outs, sems, ti, c, (x, y, 1 - c)).wait_send()

    sems = pltpu.SemaphoreType.DMA((nt,))
    return Rider(list(halves), [_sds(a.shape, a.dtype) for a in halves], {i: i for i in range(nt)}, [sems, sems],
                 start, None, finish)


def _both(r1, r2):
    assert r1.mid is None and r2.mid is None
    ni, no, ns = len(r1.arrays), len(r1.out_shapes), len(r1.scratch)

    def split(fn1, fn2):
        def run(ins, outs, scr):
            fn1(ins[:ni], outs[:no], scr[:ns])
            fn2(ins[ni:], outs[no:], scr[ns:])
        return run

    aliases = dict(r1.aliases)
    aliases.update({ni + a: no + b for a, b in r2.aliases.items()})
    return Rider(r1.arrays + r2.arrays, r1.out_shapes + r2.out_shapes, aliases, r1.scratch + r2.scratch,
                 split(r1.start, r2.start), None, split(r1.finish, r2.finish))


def _adamw_math(w, g, m, v):
    m = B1 * m + (1.0 - B1) * g
    v = B2 * v + (1.0 - B2) * (g * g)
    delta = -LR * ((m / BC1) / (jnp.sqrt(v / BC2) + AEPS) + WD * w)
    return delta, m, v


GAIN_ROWS = {"pre_mix_g": 0, "post_mix_g": 2, "pre_ffn_g": 4, "post_ffn_g": 6, "ple_g": 8, "ple_post_g": 10}
ROW_KV_G, ROW_POOL_SCALE, ROW_SINKS, ROW_LOSS, PACK_ROWS = 12, 13, 14, 15, 16
SMALL_NAMES = tuple(GAIN_ROWS) + ("kv_g", "pool_scale", "sinks")


def _small_all_reduce(rows, dpool, rider=None):
    ng, pr = len(WINDOWS), POOL_G // N_CHIPS

    def body(*refs):
        row_refs = refs[:PACK_ROWS]
        dpool_ref, tot_ref, gpool_ref, pack, land, pland, send, recv, psend, precv = refs[PACK_ROWS:]
        x, y, c, _ = _mesh_position()
        me = 4 * x + 2 * y + c
        for r in range(PACK_ROWS):
            pack[r:r + 1, :] = row_refs[r][...]

        def shard_of(k):
            return dpool_ref.at[:, pl.ds(pl.multiple_of(k * pr, pr), pr), :]

        cps = []
        for j in range(1, N_DEV):
            px, py, pc = x ^ (j >> 2), y ^ ((j >> 1) & 1), c ^ (j & 1)
            cps.append(pltpu.make_async_remote_copy(pack, land.at[me], send.at[j], recv.at[j],
                                                    device_id=(px, py, pc), device_id_type=MESH))
            cps.append(pltpu.make_async_remote_copy(shard_of(2 * px + py), pland.at[me], psend.at[j], precv.at[j],
                                                    device_id=(px, py, pc), device_id_type=MESH))
        for cp in cps:
            cp.start()
        land[me] = pack[...]
        pland[me] = dpool_ref[:, pl.ds(pl.multiple_of((2 * x + y) * pr, pr), pr), :]
        for j in range(1, N_DEV):
            pltpu.make_async_remote_copy(pack, land.at[me ^ j], send.at[j], recv.at[j],
                                         device_id=(x, y, c), device_id_type=MESH).wait_recv()
            pltpu.make_async_remote_copy(shard_of(0), pland.at[me ^ j], psend.at[j], precv.at[j],
                                         device_id=(x, y, c), device_id_type=MESH).wait_recv()
        for cp in cps:
            cp.wait_send()
        tot = land[0]
        gp = pland[0].astype(F32)
        for d in range(1, N_DEV):
            tot = tot + land[d]
            gp = gp + pland[d].astype(F32)
        tot_ref[...] = tot
        gpool_ref[...] = gp

    sems = pltpu.SemaphoreType.DMA((N_DEV,))
    return _call(
        body, name="small_all_reduce", grid=(1,),
        in_specs=[VSPEC] * (PACK_ROWS + 1), out_specs=[VSPEC, VSPEC],
        out_shape=[_sds((PACK_ROWS, D)), _sds((ng, pr, POOL_G))],
        scratch_shapes=[pltpu.VMEM((PACK_ROWS, D), F32), pltpu.VMEM((N_DEV, PACK_ROWS, D), F32),
                        pltpu.VMEM((N_DEV, ng, pr, POOL_G), BF), sems, sems, sems, sems],
        args=[*rows, dpool], rider=rider)


def _small_adamw(tot, kc, small_w, small_m, small_v):
    names = SMALL_NAMES
    n = len(names)

    def body(*refs):
        tot_ref, kc_ref = refs[0], refs[1]
        w_refs = dict(zip(names, refs[2:2 + n]))
        m_refs = dict(zip(names, refs[2 + n:2 + 2 * n]))
        v_refs = dict(zip(names, refs[2 + 2 * n:2 + 3 * n]))
        loss_ref = refs[2 + 3 * n]
        out_refs = {nm: refs[3 + 3 * n + 4 * k: 7 + 3 * n + 4 * k] for k, nm in enumerate(names)}
        tot = tot_ref[...]
        loss_ref[...] = 0.5 * jnp.sum(tot[ROW_LOSS:ROW_LOSS + 1, :], axis=-1, keepdims=True) * (1.0 / D)

        def update(nm, g):
            g_ref, d_ref, nm_ref, nv_ref = out_refs[nm]
            g_ref[...] = g
            d_ref[...], nm_ref[...], nv_ref[...] = _adamw_math(w_refs[nm][...], g, m_refs[nm][...], v_refs[nm][...])

        for nm, r in GAIN_ROWS.items():
            update(nm, tot[r:r + 2, :])
        update("kv_g", tot[ROW_KV_G:ROW_KV_G + 1, :])
        k = kc_ref[0]
        width = D // N_CHIPS
        g_scale = jnp.zeros((1, width), F32)
        for kk in range(N_CHIPS):
            g_scale = g_scale + jnp.where(k == kk, tot[ROW_POOL_SCALE:ROW_POOL_SCALE + 1, kk * width:(kk + 1) * width], 0.0)
        update("pool_scale", g_scale)
        update("sinks", tot[ROW_SINKS:ROW_SINKS + 1, 0:N_HEADS])

    ins = [tot, kc] + [small_w[nm] for nm in names] + [small_m[nm] for nm in names] + [small_v[nm] for nm in names]
    out_shape = [_sds((1, 1))]
    for nm in names:
        out_shape += [_sds(small_w[nm].shape)] * 4
    outs = pl.pallas_call(
        body, name="small_adamw",
        in_specs=[VSPEC, SSPEC] + [VSPEC] * (3 * n), out_specs=[VSPEC] * len(out_shape), out_shape=out_shape,
        compiler_params=_params(),
    )(*ins)
    return outs[0], {nm: outs[1 + 4 * k: 5 + 4 * k] for k, nm in enumerate(names)}


def _compute_layout(t, full):
    if t.src == "w_gu":
        return full.reshape(2, D, FF)
    if t.src == "pool_w":
        return full.reshape(len(WINDOWS), POOL_G, POOL_G)
    if t.src == "pool_scale":
        return full.reshape(1, D)
    return full.reshape(t.A * t.R, _ncb(t) * t.C)


def kernel(x, p, pre_mix_g, post_mix_g, pre_ffn_g, post_ffn_g, pool_w, pool_scale, kv_g, w_kv, w_q, sinks, w_o, w_gu, w_down, ple_g, w_ple_gate, w_ple_proj, ple_post_g, loss_target, m_pre_mix_g, m_post_mix_g, m_pre_ffn_g, m_post_ffn_g, m_pool_w, m_pool_scale, m_kv_g, m_w_kv, m_w_q, m_sinks, m_w_o, m_w_gu, m_w_down, m_ple_g, m_w_ple_gate, m_w_ple_proj, m_ple_post_g, v_pre_mix_g, v_post_mix_g, v_pre_ffn_g, v_post_ffn_g, v_pool_w, v_pool_scale, v_kv_g, v_w_kv, v_w_q, v_sinks, v_w_o, v_w_gu, v_w_down, v_ple_g, v_w_ple_gate, v_w_ple_proj, v_ple_post_g):
    weights = dict(pre_mix_g=pre_mix_g, post_mix_g=post_mix_g, pre_ffn_g=pre_ffn_g, post_ffn_g=post_ffn_g,
                   pool_w=pool_w, pool_scale=pool_scale, kv_g=kv_g, w_kv=w_kv, w_q=w_q, sinks=sinks, w_o=w_o,
                   w_gu=w_gu, w_down=w_down, ple_g=ple_g, w_ple_gate=w_ple_gate, w_ple_proj=w_ple_proj,
                   ple_post_g=ple_post_g)
    m_in = dict(pre_mix_g=m_pre_mix_g, post_mix_g=m_post_mix_g, pre_ffn_g=m_pre_ffn_g, post_ffn_g=m_post_ffn_g,
                pool_w=m_pool_w, pool_scale=m_pool_scale, kv_g=m_kv_g, w_kv=m_w_kv, w_q=m_w_q, sinks=m_sinks,
                w_o=m_w_o, w_gu=m_w_gu, w_down=m_w_down, ple_g=m_ple_g, w_ple_gate=m_w_ple_gate,
                w_ple_proj=m_w_ple_proj, ple_post_g=m_ple_post_g)
    v_in = dict(pre_mix_g=v_pre_mix_g, post_mix_g=v_post_mix_g, pre_ffn_g=v_pre_ffn_g, post_ffn_g=v_post_ffn_g,
                pool_w=v_pool_w, pool_scale=v_pool_scale, kv_g=v_kv_g, w_kv=v_w_kv, w_q=v_w_q, sinks=v_sinks,
                w_o=v_w_o, w_gu=v_w_gu, w_down=v_w_down, ple_g=v_ple_g, w_ple_gate=v_w_ple_gate,
                w_ple_proj=v_w_ple_proj, ple_post_g=v_ple_post_g)
    order = ["pre_mix_g", "post_mix_g", "pre_ffn_g", "post_ffn_g", "pool_w", "pool_scale", "kv_g", "w_kv", "w_q",
             "sinks", "w_o", "w_gu", "w_down", "ple_g", "w_ple_gate", "w_ple_proj", "ple_post_g"]

    kc = jnp.stack([2 * lax.axis_index("x") + lax.axis_index("y"), lax.axis_index("c")]).astype(jnp.int32)
    s_len = x.shape[1]
    x2d = x.reshape(s_len, D)
    p3d = p.reshape(2, s_len, PLE)
    target = loss_target.reshape(s_len, D)
    kv_g2d = kv_g.reshape(1, D)
    gains = {nm: weights[nm] for nm in GAIN_ROWS}

    def shard_view(src, a):
        t = next(t for t in BIGS.values() if t.src == src)
        return a.reshape(-1, t.R, t.C)

    first, second = ["pool_w", "pool_scale"], ["w_gu0", "w_down0"]
    rest = [nm for nm in BIGS if nm not in first + second]
    specs = dict(BIGS, pool_scale=POOL_SCALE)
    placed = {}

    def place_job(nm):
        if nm == "pool_scale":
            return _place_job(POOL_SCALE, pool_scale.reshape(1, 1, D // N_CHIPS), F32)
        return _place_job(BIGS[nm], shard_view(BIGS[nm].src, weights[BIGS[nm].src]))

    def gather(names, rows=None):
        rows = rows or {}
        parts = [(specs[nm],) + tuple(rows.get(nm, (0, specs[nm].R))) for nm in names]
        return _gather_rider(parts, [placed[nm] for nm in names])

    def take(names, results):
        for nm, a in zip(names, results):
            placed[nm] = a

    def weight(nm):
        return _compute_layout(specs[nm], placed[nm])

    take(first, [r[0] for r in _multi_call("place_pool", [place_job(nm) for nm in first], kc)])
    cast, got = _multi_call("place_ffn0", [place_job(nm) for nm in second], kc, rider=gather(first))
    take(second, [r[0] for r in cast])
    take(first, got)
    jobs = [place_job(nm) for nm in rest]
    jobs.append(_mixa_fwd_job(x2d, gains["pre_mix_g"], weight("pool_w"), weight("pool_scale"), gains["post_mix_g"]))
    results, got = _multi_call("cast_and_mixa_fwd", jobs, kc, rider=gather(second))
    take(rest, [r[0] for r in results[:-1]])
    take(second, got)
    y0, x1 = results[-1]

    ride = ["w_ple_gate0", "w_ple_proj0", "w_q", "w_kv", "w_o", "w_gu1"]
    (f0, x2, g0, u0), got = _ffn_fwd(0, x1, gains["pre_ffn_g"], weight("w_gu0"), weight("w_down0"), gains["post_ffn_g"],
                             rider=gather(ride, {"w_gu1": (0, 320)}))
    take(ride, got)

    ride = ["w_ple_gate1", "w_ple_proj1", "w_gu1"]
    (z0, pe0, x3, q, kv), got = _ple_fwd(
        0, x2, p3d, gains["ple_g"], weight("w_ple_gate0"), weight("w_ple_proj0"), gains["ple_post_g"],
        qkv=(gains["pre_mix_g"], kv_g2d, weight("w_q"), weight("w_kv")),
        rider=gather(ride, {"w_gu1": (320, 704)}))
    take(ride, got)

    ride = ["w_down1", "w_gu1"]
    (attn, y1, x4), got = _attn_fwd(q, kv, sinks, x3, weight("w_o"), gains["post_mix_g"],
                                    rider=gather(ride, {"w_gu1": (704, D)}))
    take(ride, got)

    (f1, x5, g1, u1), _ = _ffn_fwd(1, x4, gains["pre_ffn_g"], weight("w_gu1"), weight("w_down1"), gains["post_ffn_g"])
    (z1, pe1, dx6, loss_row), _ = _ple_fwd(1, x5, p3d, gains["ple_g"], weight("w_ple_gate1"), weight("w_ple_proj1"),
                                           gains["ple_post_g"], target=target)

    local = {}
    landed = {}
    fused = {}

    def local_grads(names):
        return [local[nm].reshape(_full_shape(BIGS[nm])) for nm in names]

    def pair_exchange(names):
        return _pair_exchange_rider([BIGS[nm] for nm in names], local_grads(names))

    def pair_sum(tag, names, lands):
        jobs = [_pair_sum_job(BIGS[nm], g, l) for nm, g, l in zip(names, local_grads(names), lands)]
        return [r[0] for r in _multi_call(f"pair_sum_{tag}", jobs, kc)]

    def scatter(names, sums):
        return _scatter_rider([BIGS[nm] for nm in names], sums)

    def keep(names, sums, got):
        for nm, s, l in zip(names, sums, got):
            landed[nm] = (s, l)

    (dx5, local["w_ple_gate1"], local["w_ple_proj1"], d_ple1, d_plepost1), _ = _ple_bwd(
        1, dx6, x5, z1, pe1, p3d, gains["ple_g"], weight("w_ple_gate1"), gains["ple_post_g"])

    group_a = ["w_ple_gate1", "w_ple_proj1"]
    (dx4, d_preffn1, d_postffn1, *scattered), lands_a = _ffn_bwd(
        1, dx5, x4, f1, g1, u1, gains["pre_ffn_g"], weight("w_gu1"), weight("w_down1"), gains["post_ffn_g"], kc,
        rider=pair_exchange(group_a))
    fused["w_gu1"], fused["w_down1"] = scattered[0:2], scattered[2:4]

    (dq, dkv, local["w_o"], d_postmix1, d_sinks), _ = _attn_bwd(
        dx4, y1, attn, q, kv, sinks, weight("w_o"), gains["post_mix_g"])
    dx3, local["w_q"], local["w_kv"], d_premix1, d_kvg = _qkv_bwd(
        dq, dkv, x3, dx4, gains["pre_mix_g"], kv_g2d, weight("w_q"), weight("w_kv"))

    group_b = ["w_o", "w_q", "w_kv"]
    (dx2, local["w_ple_gate0"], local["w_ple_proj0"], d_ple0, d_plepost0), lands_b = _ple_bwd(
        0, dx3, x2, z0, pe0, p3d, gains["ple_g"], weight("w_ple_gate0"), gains["ple_post_g"],
        rider=pair_exchange(group_b))
    group_ab = group_a + group_b
    sums_ab = pair_sum("ab", group_ab, lands_a + lands_b)

    group_c = ["w_ple_gate0", "w_ple_proj0"]
    (dx1, d_preffn0, d_postffn0, *scattered), got = _ffn_bwd(
        0, dx2, x1, f0, g0, u0, gains["pre_ffn_g"], weight("w_gu0"), weight("w_down0"), gains["post_ffn_g"], kc,
        rider=_both(pair_exchange(group_c), scatter(group_ab, sums_ab)))
    fused["w_gu0"], fused["w_down0"] = scattered[0:2], scattered[2:4]
    sums_c = pair_sum("c", group_c, got[:len(group_c)])
    keep(group_ab, sums_ab, got[len(group_c):])

    layers_of = lambda src: [t for t in BIGS.values() if t.src == src]
    own_scatter = ["w_gu", "w_down"]
    early = own_scatter + ["w_q", "w_o", "w_kv"]
    late = ["w_ple_gate", "w_ple_proj"]
    by_cols = lambda srcs: [src == "w_down" for src in srcs]
    jobs = [_chip_sum_fused_job(layers_of(src), fused, by_cols=src == "w_down") for src in own_scatter]
    jobs += [_chip_sum_job(layers_of(src), landed) for src in early if src not in own_scatter]
    halves = [r[0] for r in _multi_call("chip_sum_early", jobs, kc)]
    (dx0, d_pool, d_scale, d_postmix0, d_premix0), got = _mixa_bwd(
        dx1, x2d, y0, gains["pre_mix_g"], weight("pool_w"), weight("pool_scale"), gains["post_mix_g"],
        rider=_both(scatter(group_c, sums_c), _share_rider(halves, by_cols(early))))
    keep(group_c, sums_c, got[:len(group_c)])
    full_grads = dict(zip(early, got[len(group_c):]))

    rows = [d_premix0, d_premix1, d_postmix0, d_postmix1, d_preffn0, d_preffn1, d_postffn0, d_postffn1,
            d_ple0, d_ple1, d_plepost0, d_plepost1, d_kvg, d_scale, d_sinks, loss_row]
    as2d = lambda a: a.reshape(1, D) if a.ndim == 1 else a
    (tot, g_pool), _ = _small_all_reduce(rows, d_pool)
    loss, small = _small_adamw(tot, kc, {nm: as2d(weights[nm]) for nm in SMALL_NAMES},
                               {nm: as2d(m_in[nm]) for nm in SMALL_NAMES},
                               {nm: as2d(v_in[nm]) for nm in SMALL_NAMES})

    halves = [r[0] for r in _multi_call("chip_sum_late", [_chip_sum_job(layers_of(src), landed) for src in late], kc)]
    full_grads.update(zip(late, _run("grads_pair_share", _share_rider(halves, by_cols(late)))))
    full_grads["pool_w"] = g_pool

    def adam_job(src):
        rb = layers_of(src)[0].rb // (1 if src == "pool_w" else 2)
        return _adamw_job(rb, shard_view(src, weights[src]), full_grads[src],
                          shard_view(src, m_in[src]), shard_view(src, v_in[src]))

    out = {"grad": {}, "delta": {}, "new_m": {}, "new_v": {}}
    results = dict(zip(BIG_SOURCES, _multi_call("adamw", [adam_job(src) for src in BIG_SOURCES], kc)))
    for src in BIG_SOURCES:
        shape = weights[src].shape
        for kind, a in zip(("grad", "delta", "new_m", "new_v"), results[src]):
            out[kind][src] = a.reshape(shape)
    for nm in SMALL_NAMES:
        shape = weights[nm].shape
        for kind, a in zip(("grad", "delta", "new_m", "new_v"), small[nm]):
            out[kind][nm] = a.reshape(shape)

    return (loss.reshape(()), dx0.reshape(x.shape),
            *[out["grad"][nm] for nm in order], *[out["delta"][nm] for nm in order],
            *[out["new_m"][nm] for nm in order], *[out["new_v"][nm] for nm in order])
```

```python
import collections

import jax
import jax.numpy as jnp
from jax import lax
from jax.experimental import pallas as pl
from jax.experimental.pallas import tpu as pltpu

D = 1024
FF = 2816
N_HEADS = 16
HEAD_DIM = 64
N_KV_HEADS = 4
GQA = N_HEADS // N_KV_HEADS
KVD = N_KV_HEADS * HEAD_DIM
PLE = 256
BLK = 128
WINDOWS = (2, 4, 8, 16)
POOL_G = 256
HALO = 16
EPS = 1e-6
NEG_INF = -1e30
ATT_SCALE = HEAD_DIM ** -0.5
SLOPES = tuple(2.0 ** (-8.0 * (h + 1) / N_HEADS) for h in range(N_HEADS))
N_CHIPS = 4
N_DEV = 8

LR, B1, B2, AEPS, WD, STEP = 0.001, 0.9, 0.999, 1e-08, 0.01, 10
BC1 = 1.0 - B1 ** STEP
BC2 = 1.0 - B2 ** STEP

BF = jnp.bfloat16
F32 = jnp.float32
MESH = pl.DeviceIdType.MESH
VMEM_LIMIT_V7X = 58 * 1024 * 1024
TM = 256
TM_FFN_BWD = 512
FF_CHUNK = 256
FF_HALF = FF // 2

VSPEC = pl.BlockSpec(memory_space=pltpu.VMEM)
SSPEC = pl.BlockSpec(memory_space=pltpu.SMEM)
ANYSPEC = pl.BlockSpec(memory_space=pl.ANY)


def _params(n_grid=0):
    sem = ("arbitrary",) * n_grid if n_grid else None
    return pltpu.CompilerParams(dimension_semantics=sem, vmem_limit_bytes=VMEM_LIMIT_V7X)


def _sds(shape, dtype=F32):
    return jax.ShapeDtypeStruct(tuple(shape), dtype)


Rider = collections.namedtuple("Rider", "arrays out_shapes aliases scratch start mid finish")
MID_NUM, MID_DEN = 5, 8


def _call(body, *, name, grid, in_specs, out_specs, out_shape, args, scratch_shapes=(), rider=None, prefetch=None):
    ni, no, ns = len(in_specs), len(out_specs), len(scratch_shapes)
    npre = 0 if prefetch is None else 1
    pre = [] if prefetch is None else [prefetch]
    if rider is None:
        rider = Rider([], [], {}, [], None, None, None)
    ri, ro = len(rider.arrays), len(rider.out_shapes)

    def full(*refs):
        pre_refs, refs = refs[:npre], refs[npre:]
        ins, refs = refs[:ni], refs[ni:]
        rins, refs = refs[:ri], refs[ri:]
        outs, refs = refs[:no], refs[no:]
        routs, refs = refs[:ro], refs[ro:]
        scr, rscr = refs[:ns], refs[ns:]
        ids = [pl.program_id(a) for a in range(len(grid))]
        first = ids[0] == 0
        last = ids[0] == grid[0] - 1
        for a in range(1, len(grid)):
            first = first & (ids[a] == 0)
            last = last & (ids[a] == grid[a] - 1)

        if rider.start is not None:
            @pl.when(first)
            def _():
                rider.start(rins, routs, rscr)

        if rider.mid is not None:
            assert len(grid) == 1

            @pl.when(ids[0] == (grid[0] * MID_NUM) // MID_DEN)
            def _():
                rider.mid(rins, routs, rscr)

        body(*pre_refs, *ins, *outs, *scr)

        if rider.finish is not None:
            @pl.when(last)
            def _():
                rider.finish(rins, routs, rscr)

    outs = pl.pallas_call(
        full, name=name,
        grid_spec=pltpu.PrefetchScalarGridSpec(
            num_scalar_prefetch=npre, grid=grid,
            in_specs=list(in_specs) + [ANYSPEC] * ri, out_specs=list(out_specs) + [ANYSPEC] * ro,
            scratch_shapes=list(scratch_shapes) + list(rider.scratch)),
        out_shape=list(out_shape) + list(rider.out_shapes),
        input_output_aliases={npre + ni + a: no + b for a, b in rider.aliases.items()},
        compiler_params=_params(len(grid)))(*pre, *args, *rider.arrays)
    return list(outs[:no]), list(outs[no:])


def _run(name, rider):
    ri = len(rider.arrays)

    def body(*refs):
        rins, routs, rscr = refs[:ri], refs[ri:ri + len(rider.out_shapes)], refs[ri + len(rider.out_shapes):]
        rider.start(rins, routs, rscr)
        if rider.mid is not None:
            rider.mid(rins, routs, rscr)
        rider.finish(rins, routs, rscr)

    return pl.pallas_call(
        body, name=name, in_specs=[ANYSPEC] * ri, out_specs=[ANYSPEC] * len(rider.out_shapes),
        out_shape=list(rider.out_shapes), scratch_shapes=list(rider.scratch),
        input_output_aliases=dict(rider.aliases), compiler_params=_params())(*rider.arrays)


Job = collections.namedtuple("Job", "steps ins outs fn")


def _multi_call(name, jobs, kc, rider=None):
    n = max(job.steps for job in jobs)

    def clamped(index, steps):
        return lambda s, kc_ref: index(jnp.minimum(s, steps - 1), kc_ref)

    in_specs, out_specs, out_shape, args = [], [], [], []
    for job in jobs:
        for arr, block, index, *single in job.ins:
            mode = dict(pipeline_mode=pl.Buffered(1)) if single and single[0] else {}
            in_specs.append(pl.BlockSpec(block, clamped(index, job.steps), **mode))
            args.append(arr)
        for sds, block, index in job.outs:
            out_specs.append(pl.BlockSpec(block, clamped(index, job.steps)))
            out_shape.append(sds)
    n_in = len(args)

    def body(kc_ref, *refs):
        s = pl.program_id(0)
        i0, o0 = 0, n_in
        for job in jobs:
            ins, outs = refs[i0:i0 + len(job.ins)], refs[o0:o0 + len(job.outs)]
            i0, o0 = i0 + len(job.ins), o0 + len(job.outs)

            @pl.when(s < job.steps)
            def _():
                job.fn(s, kc_ref, ins, outs)

    outs, routs = _call(body, name=name, grid=(n,), in_specs=in_specs, out_specs=out_specs, out_shape=out_shape,
                        args=args, prefetch=kc, rider=rider)
    res, o0 = [], 0
    for job in jobs:
        res.append(outs[o0:o0 + len(job.outs)])
        o0 += len(job.outs)
    return res if rider is None else (res, routs)


def _rms_fwd(x, g):
    r = lax.rsqrt(jnp.mean(x * x, axis=-1, keepdims=True) + EPS)
    return x * r * g


def _rms_bwd(x, g, dy):
    r = lax.rsqrt(jnp.mean(x * x, axis=-1, keepdims=True) + EPS)
    xn = x * r
    dxn = dy * g
    dx = r * (dxn - xn * jnp.mean(dxn * xn, axis=-1, keepdims=True))
    return dx, dy * xn


def _rowsum(a):
    return jnp.sum(a, axis=0, keepdims=True)


def _sigmoid(z):
    return 1.0 / (1.0 + jnp.exp(-z))


def _dot(a, b):
    return jnp.dot(a, b, preferred_element_type=F32)


def _dot_nt(a, b):
    return lax.dot_general(a, b, (((1,), (1,)), ((), ())), preferred_element_type=F32)


def _dot_tn(a, b):
    return lax.dot_general(a, b, (((0,), (0,)), ((), ())), preferred_element_type=F32)


def _row_spec(tm, width=D):
    return pl.BlockSpec((tm, width), lambda i: (i, 0))


def _const_spec(shape):
    zeros = (0,) * len(shape)
    return pl.BlockSpec(tuple(shape), lambda *_: zeros)


def _pool_delta(he, pos):
    out = []
    for gi, w in enumerate(WINDOWS):
        hg = he[:, gi * POOL_G:(gi + 1) * POOL_G]
        s = hg
        k = 1
        while k < w:
            s = s + pltpu.roll(s, k, 0)
            k *= 2
        cnt = jnp.maximum(jnp.minimum(pos + 1, w), 1).astype(F32)
        out.append(s / cnt - hg)
    return out


def _load_with_halo_before(x_ref, i, tm):
    r0 = pl.multiple_of(i * tm, tm)
    hs = pl.multiple_of(jnp.maximum(i * tm - HALO, 0), 8)
    xh = jnp.where(i > 0, x_ref[pl.ds(hs, HALO), :], 0.0)
    xt = x_ref[pl.ds(r0, tm), :]
    return xt, jnp.concatenate([xh, xt], axis=0)


def _mixa_fwd_job(x, pre_g, pool_w, pool_scale, post_g):
    s_len = x.shape[0]

    def fn(i, kc_ref, ins, outs):
        x_ref, pg_ref, w_ref, sc_ref, qg_ref = ins
        y_ref, x1_ref = outs
        xt, xe = _load_with_halo_before(x_ref, i, TM)
        he = _rms_fwd(xe, pg_ref[0:1, :])
        pos = i * TM - HALO + lax.broadcasted_iota(jnp.int32, (TM + HALO, 1), 0)
        ds = _pool_delta(he, pos)
        ys = [_dot(ds[gi][HALO:, :].astype(BF), w_ref[gi]) for gi in range(len(WINDOWS))]
        y = jnp.concatenate(ys, axis=1) * sc_ref[...]
        y_ref[...] = y
        x1_ref[...] = xt + _rms_fwd(y, qg_ref[0:1, :])

    def whole(a):
        zeros = (0,) * a.ndim
        return (a, a.shape, lambda j, kc_ref: zeros, True)

    rows = lambda j, kc_ref: (j, 0)
    return Job(s_len // TM, [whole(a) for a in (x, pre_g, pool_w, pool_scale, post_g)],
               [(_sds((s_len, D)), (TM, D), rows), (_sds((s_len, D)), (TM, D), rows)], fn)


def _mixa_bwd(dx1, x, y, pre_g, pool_w, pool_scale, post_g, rider=None):
    s_len = x.shape[0]
    n = s_len // TM
    ng = len(WINDOWS)

    def body(dx_ref, x_ref, y_ref, pg_ref, w_ref, sc_ref, qg_ref,
             dx0_ref, dw_ref, dsc_ref, dqg_ref, dpg_ref, wacc):
        i = pl.program_id(0)

        @pl.when(i == 0)
        def _():
            wacc[...] = jnp.zeros_like(wacc)
            dsc_ref[...] = jnp.zeros_like(dsc_ref)
            dqg_ref[...] = jnp.zeros_like(dqg_ref)
            dpg_ref[...] = jnp.zeros_like(dpg_ref)

        r0 = pl.multiple_of(i * TM, TM)
        xt, xe = _load_with_halo_before(x_ref, i, TM)
        he = _rms_fwd(xe, pg_ref[0:1, :])
        pos_b = i * TM - HALO + lax.broadcasted_iota(jnp.int32, (TM + HALO, 1), 0)
        ds = _pool_delta(he, pos_b)

        last = i == n - 1
        a0 = pl.multiple_of(jnp.minimum(i * TM + TM, s_len - HALO), 8)
        ye = jnp.concatenate([y_ref[pl.ds(r0, TM), :], y_ref[pl.ds(a0, HALO), :]], axis=0)
        dt = dx_ref[pl.ds(r0, TM), :]
        de = jnp.concatenate([dt, jnp.where(last, 0.0, dx_ref[pl.ds(a0, HALO), :])], axis=0)
        dye, prod = _rms_bwd(ye, qg_ref[0:1, :], de)
        dqg_ref[...] += _rowsum(prod[:TM, :])
        dys = dye * sc_ref[...]
        pos_a = i * TM + lax.broadcasted_iota(jnp.int32, (TM + HALO, 1), 0)

        dhs, dscs = [], []
        for gi, w in enumerate(WINDOWS):
            sl = slice(gi * POOL_G, (gi + 1) * POOL_G)
            wg = w_ref[gi]
            dys_g = dys[:, sl].astype(BF)
            d_g = ds[gi][HALO:, :].astype(BF)
            ypre = _dot(d_g, wg)
            dscs.append(_rowsum(dye[:TM, sl] * ypre))
            wacc[gi] += _dot_tn(d_g, dys_g[:TM, :])
            dd = _dot_nt(dys_g, wg)
            cnt = jnp.minimum(pos_a + 1, w).astype(F32)
            a = dd / cnt
            k = 1
            while k < w:
                a = a + pltpu.roll(a, TM + HALO - k, 0)
                k *= 2
            dhs.append(a[:TM, :] - dd[:TM, :])
        dsc_ref[...] += jnp.concatenate(dscs, axis=1)
        dh = jnp.concatenate(dhs, axis=1)
        dxp, prod2 = _rms_bwd(xt, pg_ref[0:1, :], dh)
        dpg_ref[...] += _rowsum(prod2)
        dx0_ref[...] = dt + dxp

        @pl.when(last)
        def _():
            dw_ref[...] = wacc[...].astype(BF)

    return _call(
        body, name="mixa_bwd", grid=(n,), in_specs=[VSPEC] * 7,
        out_specs=[_row_spec(TM), _const_spec((ng, POOL_G, POOL_G)), _const_spec((1, D)),
                   _const_spec((1, D)), _const_spec((1, D))],
        out_shape=[_sds((s_len, D)), _sds((ng, POOL_G, POOL_G), BF), _sds((1, D)), _sds((1, D)), _sds((1, D))],
        scratch_shapes=[pltpu.VMEM((ng, POOL_G, POOL_G), F32)],
        args=[dx1, x, y, pre_g, pool_w, pool_scale, post_g], rider=rider)


def _ple_math(layer, x, p_blk, g_ref, wg_ref, wp_ref, qg_ref):
    r = _rms_fwd(x, g_ref[layer:layer + 1, :]).astype(BF)
    z = _dot(r, wg_ref[...])
    pe = _dot(p_blk.astype(BF), wp_ref[...])
    return z, pe, x + _rms_fwd(pe * _sigmoid(z), qg_ref[layer:layer + 1, :])


def _ffn_fwd(layer, x1, pre_g, wgu, wd, post_g, rider=None, head=None):
    s_len = x1.shape[0]

    def body(*refs):
        if head:
            (x_ref, pg_ref, wgu_ref, wd_ref, qg_ref, p_ref, eg_ref, wg_ref, wp_ref, eq_ref, t_ref,
             f_ref, x2_ref, g_ref, u_ref, z_ref, pe_ref, dx_ref, lv_ref) = refs
        else:
            x_ref, pg_ref, wgu_ref, wd_ref, qg_ref, f_ref, x2_ref, g_ref, u_ref = refs
        x = x_ref[...]
        h = _rms_fwd(x, pg_ref[layer:layer + 1, :]).astype(BF)
        f = jnp.zeros((TM, D), F32)
        for c in range(FF // FF_HALF):
            cols = slice(c * FF_HALF, (c + 1) * FF_HALF)
            g = _dot(h, wgu_ref[0, :, cols])
            u = _dot(h, wgu_ref[1, :, cols])
            g_ref[:, cols] = g.astype(BF)
            u_ref[:, cols] = u.astype(BF)
            act = g * _sigmoid(g) * u
            f = f + _dot(act.astype(BF), wd_ref[cols, :])
        f_ref[...] = f
        x2 = x + _rms_fwd(f, qg_ref[layer:layer + 1, :])
        x2_ref[...] = x2
        if head:
            @pl.when(pl.program_id(0) == 0)
            def _():
                lv_ref[...] = jnp.zeros_like(lv_ref)
            z, pe, x3 = _ple_math(layer, x2, p_ref[...], eg_ref, wg_ref, wp_ref, eq_ref)
            z_ref[...] = z
            pe_ref[...] = pe
            err = x3 - t_ref[...]
            dx_ref[...] = err * (1.0 / D)
            lv_ref[...] += _rowsum(err * err)

    in_specs = [_row_spec(TM), VSPEC, VSPEC, VSPEC, VSPEC]
    args = [x1, pre_g, wgu, wd, post_g]
    out_specs = [_row_spec(TM), _row_spec(TM), _row_spec(TM, FF), _row_spec(TM, FF)]
    out_shape = [_sds((s_len, D)), _sds((s_len, D)), _sds((s_len, FF), BF), _sds((s_len, FF), BF)]
    if head:
        p, ple_g, w_gate, w_proj, ple_post_g, target = head
        in_specs += [pl.BlockSpec((None, TM, PLE), lambda i: (layer, i, 0)), VSPEC, VSPEC, VSPEC, VSPEC, _row_spec(TM)]
        args += [p, ple_g, w_gate, w_proj, ple_post_g, target]
        out_specs += [_row_spec(TM), _row_spec(TM), _row_spec(TM), _const_spec((1, D))]
        out_shape += [_sds((s_len, D))] * 3 + [_sds((1, D))]
    return _call(body, name=f"ffn_fwd{layer}", grid=(s_len // TM,), in_specs=in_specs, out_specs=out_specs,
                 out_shape=out_shape, args=args, rider=rider)


GU_PIECE = 128
DN_PIECE = 64
DN_SLOT = FF // N_CHIPS
HALF_D = D // 2
CHUNK_STRIDE = 6
CHUNK_START = (1, 7, 4, 10)


def _ffn_bwd(layer, dx2, x1, f, g_pre, u_pre, pre_g, wgu, wd, post_g, kc, rider=None):
    s_len = x1.shape[0]
    tm = TM_FFN_BWD
    n = s_len // tm
    nc = FF // FF_CHUNK
    n_gu, n_dn = FF_CHUNK // GU_PIECE, FF_CHUNK // DN_PIECE
    n_pieces = 2 * n_gu + n_dn
    n_blk = FF_HALF // GU_PIECE

    def edge_rows(c, i, kc_ref):
        return (jnp.where((c == 0) | (c == nc - 1), i, n - 1), 0)

    def chunk_at(c, kc_ref):
        k = kc_ref[0]
        start = jnp.where(k == 0, CHUNK_START[0], jnp.where(k == 1, CHUNK_START[1],
                                                            jnp.where(k == 2, CHUNK_START[2], CHUNK_START[3])))
        return ((c + start) * CHUNK_STRIDE) % nc

    def exchange(kc_ref, c, accg, accu, accd, own_gu_ref, land_gu_ref, own_dn_ref, land_dn_ref,
                 pl_gu, pl_dn, sib_gu, sib_dn, mine_gu, mine_dn, sum_gu, sum_dn,
                 psend, precv, ssend, lsem, rrecv):
        x, y, core = lax.axis_index("x"), lax.axis_index("y"), lax.axis_index("c")
        lower = core == 0

        def pair_copy(cc, part):
            p = cc % 2
            src, dst = ((sib_gu, pl_gu), (sib_dn, pl_dn))[part]
            return pltpu.make_async_remote_copy(src.at[p], dst.at[cc], psend.at[p, part], precv.at[cc, part],
                                                device_id=(x, y, 1 - core), device_id_type=MESH)

        def scatter(cc, wait):
            p = cc % 2
            hidden = chunk_at(cc, kc_ref) * FF_CHUNK

            assert n_gu == 2
            k0, k1 = hidden // FF_HALF, (hidden + GU_PIECE) // FF_HALF
            blk = (hidden - k0 * FF_HALF) // GU_PIECE
            for gu in range(2):
                @pl.when(k0 == k1)
                def _():
                    piece(p, wait, 2 * gu, sum_gu.at[p, gu], k0 + 2 * gu, 0, (pl.ds(blk, 2),))

                @pl.when(k0 != k1)
                def _():
                    piece(p, wait, 2 * gu, sum_gu.at[p, gu, 0], k0 + 2 * gu, 0, (blk,))
                    piece(p, wait, 2 * gu + 1, sum_gu.at[p, gu, 1], k1 + 2 * gu, 0, (0,))

            kd = hidden // DN_SLOT
            off = pl.multiple_of(hidden - kd * DN_SLOT, DN_PIECE)
            m = jnp.minimum((DN_SLOT - off) // DN_PIECE, n_dn)
            for mm in range(1, n_dn + 1):
                @pl.when(m == mm)
                def _():
                    rows = mm * DN_PIECE
                    piece(p, wait, 2 * n_gu, sum_dn.at[p, pl.ds(0, rows), :], kd, 1, (pl.ds(off, rows), slice(None)))
                    if mm < n_dn:
                        piece(p, wait, 2 * n_gu + 1, sum_dn.at[p, pl.ds(rows, FF_CHUNK - rows), :], kd + 1, 1,
                              (pl.ds(0, FF_CHUNK - rows), slice(None)))

        def piece(p, wait, pi, src, k, t, where):
            own_ref, land_ref = ((own_gu_ref, land_gu_ref), (own_dn_ref, land_dn_ref))[t]
            kx, ky = k // 2, k % 2
            fx, fy = (kx != x).astype(jnp.int32), (ky != y).astype(jnp.int32)
            local = (fx + fy) == 0
            j = jnp.maximum(fx + 2 * fy - 1, 0)

            @pl.when(local)
            def _():
                cp = pltpu.make_async_copy(src, own_ref.at[where], lsem.at[p, pi])
                if wait:
                    cp.wait()
                else:
                    cp.start()

            @pl.when(jnp.logical_not(local))
            def _():
                cp = pltpu.make_async_remote_copy(src, land_ref.at[(j,) + where], ssend.at[p, pi],
                                                  rrecv.at[t, j], device_id=(kx, ky, core), device_id_type=MESH)
                if wait:
                    cp.wait_send()
                else:
                    cp.start()

        def add_and_scatter(cc):
            p = cc % 2
            pair_copy(cc, 0).wait_recv()
            pair_copy(cc, 1).wait_recv()
            s_gu = (mine_gu[...] + pl_gu[cc].astype(F32)).astype(BF)
            for hc in range(n_gu):
                sum_gu[p, :, hc] = s_gu[:, :, hc * GU_PIECE:(hc + 1) * GU_PIECE]
            sum_dn[p] = (mine_dn[...] + pl_dn[cc].astype(F32)).astype(BF)
            scatter(cc, wait=False)

        @pl.when(c >= 1)
        def _():
            @pl.when(c >= 3)
            def _():
                scatter(c - 3, wait=True)
            add_and_scatter(c - 1)

        @pl.when(c >= 2)
        def _():
            pair_copy(c - 2, 0).wait_send()
            pair_copy(c - 2, 1).wait_send()

        p = c % 2
        my_rows = pl.ds(pl.multiple_of(core * HALF_D, HALF_D), HALF_D)
        sib_rows = pl.ds(pl.multiple_of((1 - core) * HALF_D, HALF_D), HALF_D)
        d_v = accd[...]
        sib_gu[p, 0] = accg[sib_rows, :].astype(BF)
        sib_gu[p, 1] = accu[sib_rows, :].astype(BF)
        sib_dn[p] = jnp.where(lower, d_v[:, HALF_D:], d_v[:, :HALF_D]).astype(BF)
        mine_gu[0] = accg[my_rows, :]
        mine_gu[1] = accu[my_rows, :]
        mine_dn[...] = jnp.where(lower, d_v[:, :HALF_D], d_v[:, HALF_D:])
        pair_copy(c, 0).start()
        pair_copy(c, 1).start()

        @pl.when(c == nc - 1)
        def _():
            scatter(nc - 3, wait=True)
            add_and_scatter(nc - 1)
            for cc in (nc - 2, nc - 1):
                pair_copy(cc, 0).wait_send()
                pair_copy(cc, 1).wait_send()
                scatter(cc, wait=True)
            for t, land_ref in enumerate((land_gu_ref, land_dn_ref)):
                for j in range(N_CHIPS - 1):
                    pltpu.make_async_remote_copy(land_ref.at[j], land_ref.at[j], ssend.at[0, 0], rrecv.at[t, j],
                                                 device_id=(x, y, core), device_id_type=MESH).wait_recv()

    def body(kc_ref, dx_ref, x_ref, f_ref, gp_ref, up_ref, pg_ref, wgu_ref, wd_ref, qg_ref,
             dx1_ref, dpg_ref, dqg_ref, own_gu_ref, land_gu_ref, own_dn_ref, land_dn_ref,
             h_s, df_s, dh_s, accg, accu, accd, *comm):
        c = pl.program_id(0)
        i = pl.program_id(1)
        rows = pl.ds(pl.multiple_of(i * tm, tm), tm)
        pg = pg_ref[layer:layer + 1, :]

        @pl.when((c == 0) & (i == 0))
        def _():
            dpg_ref[...] = jnp.zeros_like(dpg_ref)
            dqg_ref[...] = jnp.zeros_like(dqg_ref)

        @pl.when(c == 0)
        def _():
            h_s[rows, :] = _rms_fwd(x_ref[...], pg).astype(BF)
            df, prod = _rms_bwd(f_ref[...], qg_ref[layer:layer + 1, :], dx_ref[...])
            df_s[rows, :] = df.astype(BF)
            dqg_ref[...] += _rowsum(prod)

        @pl.when(i == 0)
        def _():
            accg[...] = jnp.zeros_like(accg)
            accu[...] = jnp.zeros_like(accu)
            accd[...] = jnp.zeros_like(accd)

        h = h_s[rows, :]
        df = df_s[rows, :]
        wg = wgu_ref[0]
        wu = wgu_ref[1]
        g = gp_ref[...].astype(F32)
        u = up_ref[...].astype(F32)
        sg = _sigmoid(g)
        a = g * sg
        dact = _dot_nt(df, wd_ref[...])
        accd[...] += _dot_tn((a * u).astype(BF), df)
        du = (dact * a).astype(BF)
        dg = (dact * u * (sg * (1.0 + g * (1.0 - sg)))).astype(BF)
        accg[...] += _dot_tn(h, dg)
        accu[...] += _dot_tn(h, du)
        dh = _dot_nt(dg, wg) + _dot_nt(du, wu)

        @pl.when(c == 0)
        def _():
            dh_s[rows, :] = dh

        @pl.when((c > 0) & (c < nc - 1))
        def _():
            dh_s[rows, :] += dh

        @pl.when(c == nc - 1)
        def _():
            dxp, prod = _rms_bwd(x_ref[...], pg, dh_s[rows, :] + dh)
            dpg_ref[...] += _rowsum(prod)
            dx1_ref[...] = dx_ref[...] + dxp

        @pl.when(i == n - 1)
        def _():
            exchange(kc_ref, c, accg, accu, accd, own_gu_ref, land_gu_ref, own_dn_ref, land_dn_ref, *comm)

    dma = pltpu.SemaphoreType.DMA
    return _call(
        body, name=f"ffn_bwd{layer}", grid=(nc, n),
        in_specs=[pl.BlockSpec((tm, D), edge_rows), pl.BlockSpec((tm, D), edge_rows),
                  pl.BlockSpec((tm, D), lambda c, i, kc_ref: (jnp.where(c == 0, i, n - 1), 0),
                               pipeline_mode=pl.Buffered(1)),
                  pl.BlockSpec((tm, FF_CHUNK), lambda c, i, kc_ref: (i, chunk_at(c, kc_ref))),
                  pl.BlockSpec((tm, FF_CHUNK), lambda c, i, kc_ref: (i, chunk_at(c, kc_ref))),
                  VSPEC,
                  pl.BlockSpec((2, D, FF_CHUNK), lambda c, i, kc_ref: (0, 0, chunk_at(c, kc_ref))),
                  pl.BlockSpec((FF_CHUNK, D), lambda c, i, kc_ref: (chunk_at(c, kc_ref), 0)),
                  VSPEC],
        out_specs=[pl.BlockSpec((tm, D), lambda c, i, kc_ref: (jnp.where(c == nc - 1, i, 0), 0)),
                   _const_spec((1, D)), _const_spec((1, D)), ANYSPEC, ANYSPEC, ANYSPEC, ANYSPEC],
        out_shape=[_sds((s_len, D)), _sds((1, D)), _sds((1, D)),
                   _sds((n_blk, HALF_D, GU_PIECE), BF), _sds((N_CHIPS - 1, n_blk, HALF_D, GU_PIECE), BF),
                   _sds((DN_SLOT, HALF_D), BF), _sds((N_CHIPS - 1, DN_SLOT, HALF_D), BF)],
        scratch_shapes=[pltpu.VMEM((s_len, D), BF), pltpu.VMEM((s_len, D), BF), pltpu.VMEM((s_len, D), F32),
                        pltpu.VMEM((D, FF_CHUNK), F32), pltpu.VMEM((D, FF_CHUNK), F32),
                        pltpu.VMEM((FF_CHUNK, D), F32),
                        pltpu.VMEM((nc, 2, HALF_D, FF_CHUNK), BF), pltpu.VMEM((nc, FF_CHUNK, HALF_D), BF),
                        pltpu.VMEM((2, 2, HALF_D, FF_CHUNK), BF), pltpu.VMEM((2, FF_CHUNK, HALF_D), BF),
                        pltpu.VMEM((2, HALF_D, FF_CHUNK), F32), pltpu.VMEM((FF_CHUNK, HALF_D), F32),
                        pltpu.VMEM((2, 2, n_gu, HALF_D, GU_PIECE), BF), pltpu.VMEM((2, FF_CHUNK, HALF_D), BF),
                        dma((2, 2)), dma((nc, 2)), dma((2, n_pieces)), dma((2, n_pieces)), dma((2, N_CHIPS - 1))],
        args=[dx2, x1, f, g_pre, u_pre, pre_g, wgu, wd, post_g], rider=rider, prefetch=kc)


def _ple_fwd(layer, x2, p, ple_g, w_gate, w_proj, post_g, qkv=None, rider=None):
    s_len = x2.shape[0]

    def body(*refs):
        if qkv:
            (x_ref, p_ref, g_ref, wg_ref, wp_ref, qg_ref, ng_ref, kg_ref, wq_ref, wkv_ref,
             z_ref, pe_ref, x3_ref, q_ref, kv_ref) = refs
        else:
            x_ref, p_ref, g_ref, wg_ref, wp_ref, qg_ref, z_ref, pe_ref, x3_ref = refs
        z, pe, x3 = _ple_math(layer, x_ref[...], p_ref[...], g_ref, wg_ref, wp_ref, qg_ref)
        z_ref[...] = z
        pe_ref[...] = pe
        x3_ref[...] = x3
        if qkv:
            q_ref[...] = _dot(_rms_fwd(x3, ng_ref[layer + 1:layer + 2, :]).astype(BF), wq_ref[...]).astype(BF)
            kv_ref[...] = _dot(_rms_fwd(x3, kg_ref[...]).astype(BF), wkv_ref[...]).astype(BF)

    p_spec = pl.BlockSpec((None, TM, PLE), lambda i: (layer, i, 0))
    in_specs = [_row_spec(TM), p_spec, VSPEC, VSPEC, VSPEC, VSPEC]
    args = [x2, p, ple_g, w_gate, w_proj, post_g]
    out_specs = [_row_spec(TM), _row_spec(TM), _row_spec(TM)]
    out_shape = [_sds((s_len, D))] * 3
    if qkv:
        in_specs += [VSPEC] * 4
        args += list(qkv)
        out_specs += [_row_spec(TM), _row_spec(TM, 2 * KVD)]
        out_shape += [_sds((s_len, D), BF), _sds((s_len, 2 * KVD), BF)]
    return _call(body, name=f"ple_fwd{layer}", grid=(s_len // TM,), in_specs=in_specs, out_specs=out_specs,
                 out_shape=out_shape, args=args, rider=rider)


def _ple_bwd(layer, dx3, x2, z, pe, p, ple_g, w_gate, post_g, rider=None):
    s_len = x2.shape[0]
    n = s_len // TM

    def body(dx_ref, x_ref, z_ref, pe_ref, p_ref, g_ref, wg_ref, qg_ref,
             dx2_ref, dwg_ref, dwp_ref, dg_ref, dqg_ref, gacc, pacc):
        i = pl.program_id(0)

        @pl.when(i == 0)
        def _():
            gacc[...] = jnp.zeros_like(gacc)
            pacc[...] = jnp.zeros_like(pacc)
            dg_ref[...] = jnp.zeros_like(dg_ref)
            dqg_ref[...] = jnp.zeros_like(dqg_ref)

        dx = dx_ref[...]
        x = x_ref[...]
        pe_v = pe_ref[...]
        gate = _sigmoid(z_ref[...])
        de, prod = _rms_bwd(pe_v * gate, qg_ref[layer:layer + 1, :], dx)
        dqg_ref[...] += _rowsum(prod)
        dpe = (de * gate).astype(BF)
        dz = (de * pe_v * gate * (1.0 - gate)).astype(BF)
        pacc[...] += _dot_tn(p_ref[...].astype(BF), dpe)
        g = g_ref[layer:layer + 1, :]
        r = _rms_fwd(x, g).astype(BF)
        gacc[...] += _dot_tn(r, dz)
        dr = _dot_nt(dz, wg_ref[...])
        dxp, prod2 = _rms_bwd(x, g, dr)
        dg_ref[...] += _rowsum(prod2)
        dx2_ref[...] = dx + dxp

        @pl.when(i == n - 1)
        def _():
            dwg_ref[...] = gacc[...].astype(BF)
            dwp_ref[...] = pacc[...].astype(BF)

    p_spec = pl.BlockSpec((None, TM, PLE), lambda i: (layer, i, 0))
    return _call(
        body, name=f"ple_bwd{layer}", grid=(n,),
        in_specs=[_row_spec(TM), _row_spec(TM), _row_spec(TM), _row_spec(TM), p_spec, VSPEC, VSPEC, VSPEC],
        out_specs=[_row_spec(TM), _const_spec((D, D)), _const_spec((PLE, D)), _const_spec((1, D)), _const_spec((1, D))],
        out_shape=[_sds((s_len, D)), _sds((D, D), BF), _sds((PLE, D), BF), _sds((1, D)), _sds((1, D))],
        scratch_shapes=[pltpu.VMEM((D, D), F32), pltpu.VMEM((PLE, D), F32)],
        args=[dx3, x2, z, pe, p, ple_g, w_gate, post_g], rider=rider)


def _qkv_bwd(dq, dkv, x3, dx4, q_g, kv_g, w_q, w_kv):
    s_len = x3.shape[0]
    n = s_len // TM

    def body(dq_ref, dkv_ref, x_ref, dx_ref, qg_ref, kg_ref, wq_ref, wkv_ref,
             dx3_ref, dwq_ref, dwkv_ref, dqg_ref, dkg_ref, qacc, kacc):
        i = pl.program_id(0)

        @pl.when(i == 0)
        def _():
            qacc[...] = jnp.zeros_like(qacc)
            kacc[...] = jnp.zeros_like(kacc)
            dqg_ref[...] = jnp.zeros_like(dqg_ref)
            dkg_ref[...] = jnp.zeros_like(dkg_ref)

        x = x_ref[...]
        qg = qg_ref[1:2, :]
        kg = kg_ref[...]
        dq_v = dq_ref[...]
        dkv_v = dkv_ref[...].astype(BF)
        qacc[...] += _dot_tn(_rms_fwd(x, qg).astype(BF), dq_v)
        kacc[...] += _dot_tn(_rms_fwd(x, kg).astype(BF), dkv_v)
        dxq, prod_q = _rms_bwd(x, qg, _dot_nt(dq_v, wq_ref[...]))
        dxk, prod_k = _rms_bwd(x, kg, _dot_nt(dkv_v, wkv_ref[...]))
        dqg_ref[...] += _rowsum(prod_q)
        dkg_ref[...] += _rowsum(prod_k)
        dx3_ref[...] = dx_ref[...] + dxq + dxk

        @pl.when(i == n - 1)
        def _():
            dwq_ref[...] = qacc[...].astype(BF)
            dwkv_ref[...] = kacc[...].astype(BF)

    outs, _ = _call(
        body, name="qkv_bwd", grid=(n,),
        in_specs=[_row_spec(TM), _row_spec(TM, 2 * KVD), _row_spec(TM), _row_spec(TM), VSPEC, VSPEC, VSPEC, VSPEC],
        out_specs=[_row_spec(TM), _const_spec((D, D)), _const_spec((D, 2 * KVD)),
                   _const_spec((1, D)), _const_spec((1, D))],
        out_shape=[_sds((s_len, D)), _sds((D, D), BF), _sds((D, 2 * KVD), BF), _sds((1, D)), _sds((1, D))],
        scratch_shapes=[pltpu.VMEM((D, D), F32), pltpu.VMEM((D, 2 * KVD), F32)],
        args=[dq, dkv, x3, dx4, q_g, kv_g, w_q, w_kv])
    return outs


def _attn_group(i, q, kvw, sink_ref, g):
    rows = GQA * BLK
    heads = [GQA * g + j for j in range(GQA)]
    off = jnp.where(i > 0, BLK, 0)
    row = lax.broadcasted_iota(jnp.int32, (rows, 2 * BLK), 0)
    rel = (row % BLK) - lax.broadcasted_iota(jnp.int32, (rows, 2 * BLK), 1) + off
    valid = (rel >= 0) & (rel < BLK)
    head_of_row = lax.broadcasted_iota(jnp.int32, (rows, 1), 0) // BLK
    slope = jnp.zeros((rows, 1), F32)
    sink = jnp.zeros((rows, 1), F32)
    for j, h in enumerate(heads):
        slope = jnp.where(head_of_row == j, SLOPES[h], slope)
        sink = jnp.where(head_of_row == j, sink_ref[0, h], sink)
    qs = jnp.concatenate([q[:, h * HEAD_DIM:(h + 1) * HEAD_DIM] for h in heads], axis=0)
    k = kvw[:, g * HEAD_DIM:(g + 1) * HEAD_DIM]
    v = kvw[:, KVD + g * HEAD_DIM:KVD + (g + 1) * HEAD_DIM]
    s = _dot_nt(qs, k) * ATT_SCALE - slope * rel.astype(F32)
    s = jnp.where(valid, s, NEG_INF)
    m = jnp.maximum(jnp.max(s, axis=-1, keepdims=True), sink)
    e = jnp.exp(s - m)
    es = jnp.exp(sink - m)
    inv = 1.0 / (jnp.sum(e, axis=-1, keepdims=True) + es)
    return e * inv, es * inv, qs, k, v


def _unstack_heads(stacked):
    return [stacked[j * BLK:(j + 1) * BLK, :] for j in range(GQA)]


def _kv_window(kv_ref, i):
    ks = pl.multiple_of(jnp.maximum(i * BLK - BLK, 0), BLK)
    return ks, kv_ref[pl.ds(ks, 2 * BLK), :]


def _attn_fwd(q, kv, sinks, x3, w_o, post_g, rider=None):
    s_len = q.shape[0]

    def body(q_ref, kv_ref, sk_ref, x_ref, wo_ref, g_ref, a_ref, y_ref, x4_ref):
        i = pl.program_id(0)
        _, kvw = _kv_window(kv_ref, i)
        q = q_ref[...]
        outs = []
        for g in range(N_KV_HEADS):
            p, _, _, _, v = _attn_group(i, q, kvw, sk_ref, g)
            outs += _unstack_heads(_dot(p.astype(BF), v))
        attn = jnp.concatenate(outs, axis=1)
        a_ref[...] = attn
        y = _dot(attn.astype(BF), wo_ref[...])
        y_ref[...] = y
        x4_ref[...] = x_ref[...] + _rms_fwd(y, g_ref[1:2, :])

    return _call(body, name="attn_fwd", grid=(s_len // BLK,),
                 in_specs=[_row_spec(BLK), VSPEC, SSPEC, _row_spec(BLK), VSPEC, VSPEC],
                 out_specs=[_row_spec(BLK)] * 3, out_shape=[_sds((s_len, D))] * 3,
                 args=[q, kv, sinks, x3, w_o, post_g], rider=rider)


ATT_STEP_BLOCKS = 2


def _attn_bwd(dx4, y, attn, q, kv, sinks, w_o, post_g, rider=None):
    s_len = q.shape[0]
    rows = ATT_STEP_BLOCKS * BLK
    n = s_len // rows

    def body(dx_ref, y_ref, a_ref, q_ref, kv_ref, sk_ref, wo_ref, g_ref,
             dq_ref, dkv_ref, dwo_ref, dg_ref, dsk_ref, wacc):
        i = pl.program_id(0)

        @pl.when(i == 0)
        def _():
            dkv_ref[...] = jnp.zeros_like(dkv_ref)
            wacc[...] = jnp.zeros_like(wacc)
            dg_ref[...] = jnp.zeros_like(dg_ref)
            dsk_ref[...] = jnp.zeros_like(dsk_ref)

        dy, prod = _rms_bwd(y_ref[...], g_ref[1:2, :], dx_ref[...])
        dg_ref[...] += _rowsum(prod)
        dyb = dy.astype(BF)
        attn_all = a_ref[...]
        wacc[...] += _dot_tn(attn_all.astype(BF), dyb)
        d_o_all = _dot_nt(dyb, wo_ref[...])
        q_all = q_ref[...]
        lane = lax.broadcasted_iota(jnp.int32, (1, D), 1)
        dsk = jnp.zeros((1, D), F32)
        for sub in range(ATT_STEP_BLOCKS):
            blk = i * ATT_STEP_BLOCKS + sub
            sl = slice(sub * BLK, (sub + 1) * BLK)
            d_o, q = d_o_all[sl, :], q_all[sl, :]
            dod = d_o * attn_all[sl, :]
            ks, kvw = _kv_window(kv_ref, blk)
            dqs, dks, dvs = [], [], []
            for g in range(N_KV_HEADS):
                p, ps, qs, k, v = _attn_group(blk, q, kvw, sk_ref, g)
                cols = [slice((GQA * g + j) * HEAD_DIM, (GQA * g + j + 1) * HEAD_DIM) for j in range(GQA)]
                do_s = jnp.concatenate([d_o[:, c] for c in cols], axis=0).astype(BF)
                dsum = jnp.concatenate([jnp.sum(dod[:, c], axis=-1, keepdims=True) for c in cols], axis=0)
                dp = _dot_nt(do_s, v)
                dsb = (p * (dp - dsum) * ATT_SCALE).astype(BF)
                sink_part = ps * dsum
                for j in range(GQA):
                    dsk = dsk + jnp.where(lane == GQA * g + j, -_rowsum(sink_part[j * BLK:(j + 1) * BLK, :]), 0.0)
                dqs += _unstack_heads(_dot(dsb, k))
                dks.append(_dot_tn(dsb, qs))
                dvs.append(_dot_tn(p.astype(BF), do_s))
            dq_ref[sl, :] = jnp.concatenate(dqs, axis=1).astype(BF)
            dkv_ref[pl.ds(ks, 2 * BLK), :] += jnp.concatenate(dks + dvs, axis=1)
        dsk_ref[...] += dsk

        @pl.when(i == n - 1)
        def _():
            dwo_ref[...] = wacc[...].astype(BF)

    return _call(
        body, name="attn_bwd", grid=(n,),
        in_specs=[_row_spec(rows), _row_spec(rows), _row_spec(rows), _row_spec(rows), VSPEC, SSPEC, VSPEC, VSPEC],
        out_specs=[_row_spec(rows), _const_spec((s_len, 2 * KVD)), _const_spec((D, D)),
                   _const_spec((1, D)), _const_spec((1, D))],
        out_shape=[_sds((s_len, D), BF), _sds((s_len, 2 * KVD)), _sds((D, D), BF), _sds((1, D)), _sds((1, D))],
        scratch_shapes=[pltpu.VMEM((D, D), F32)],
        args=[dx4, y, attn, q, kv, sinks, w_o, post_g], rider=rider)


Big = collections.namedtuple("Big", "name src layer L A R C rb")


def _bigs():
    out = {"pool_w": Big("pool_w", "pool_w", None, 4, 4, POOL_G // N_CHIPS, POOL_G, 32)}
    for l in range(2):
        out[f"w_gu{l}"] = Big(f"w_gu{l}", "w_gu", l, 1, 2, D, FF_HALF, 256)
        out[f"w_down{l}"] = Big(f"w_down{l}", "w_down", l, 1, 4, FF // N_CHIPS, D, 352)
        out[f"w_ple_gate{l}"] = Big(f"w_ple_gate{l}", "w_ple_gate", l, 1, 4, D // N_CHIPS, D, 128)
        out[f"w_ple_proj{l}"] = Big(f"w_ple_proj{l}", "w_ple_proj", l, 1, 1, PLE, D // N_CHIPS, 128)
    out["w_q"] = Big("w_q", "w_q", None, 1, 4, D // N_CHIPS, D, 128)
    out["w_o"] = Big("w_o", "w_o", None, 1, 4, D // N_CHIPS, D, 128)
    out["w_kv"] = Big("w_kv", "w_kv", None, 1, 4, D // N_CHIPS, 2 * KVD, 128)
    return out


BIGS = _bigs()
POOL_SCALE = Big("pool_scale", "pool_scale", None, 1, 1, 1, D // N_CHIPS, 1)
BIG_SOURCES = ("w_gu", "w_down", "w_ple_gate", "w_ple_proj", "w_q", "w_o", "w_kv", "pool_w")


def _ncb(t):
    return N_CHIPS // t.A


def _full_shape(t, rows=None):
    return (t.L, t.A, t.R if rows is None else rows, _ncb(t) * t.C)


def _slot_index(t, k):
    return k // _ncb(t), k % _ncb(t)


def _slot(ref, t, k, row0, rows):
    a, cb = _slot_index(t, k)
    return ref.at[:, a, pl.ds(row0, rows), pl.ds(pl.multiple_of(cb * t.C, 128), t.C)]


def _place_job(t, w, out_dtype=BF):
    nb = next((nb for nb in (8, 4, 2, 1) if t.R % (16 * nb) == 0), 1) if t.L == 1 else 1
    rb = t.R // nb

    def fn(j, kc_ref, ins, outs):
        outs[0][...] = ins[0][...].astype(out_dtype)

    def in_map(j, kc_ref):
        return (j // nb if t.layer is None else t.layer, j % nb, 0)

    def out_map(j, kc_ref):
        a, cb = _slot_index(t, kc_ref[0])
        return (j // nb, a, j % nb, cb)

    return Job(t.L * nb, [(w, (None, rb, t.C), in_map)],
               [(_sds(_full_shape(t), out_dtype), (None, None, rb, t.C), out_map)], fn)


def _mesh_position():
    x, y, c = lax.axis_index("x"), lax.axis_index("y"), lax.axis_index("c")
    chips = [(1 - x, y), (x, 1 - y), (1 - x, 1 - y)]
    return x, y, c, chips


DIRECT_BELOW = 1024


def _gather_rider(parts, fulls):
    nt = len(parts)
    TO_X, TO_Y, FWD_X, FWD_Y, SIB_X, SIB_Y, SIB_D = range(7)

    def rows_of(ti, core):
        t, r0, r1 = parts[ti]
        h = (r1 - r0) // 2
        return r0 + core * h, h

    def copy(outs, sems, kind, ti, k_src, row0, rows, dev):
        region = _slot(outs[ti], parts[ti][0], k_src, row0, rows)
        return pltpu.make_async_remote_copy(region, region, sems[0].at[ti, kind], sems[1].at[ti, kind],
                                            device_id=dev, device_id_type=MESH)

    def plan(outs, sems):
        x, y, c, _ = _mesh_position()
        me, kx, ky, kd = 2 * x + y, 2 * (1 - x) + y, 2 * x + (1 - y), 2 * (1 - x) + (1 - y)
        dev_x, dev_y, dev_d, sib = (1 - x, y, c), (x, 1 - y, c), (1 - x, 1 - y, c), (x, y, 1 - c)

        def whole(ti):
            return 0, parts[ti][0].R

        def mk(kind, k_send, k_recv, dev, send_rows, recv_rows):
            def build(ti, side):
                k_src = k_send if side == "s" else k_recv
                row0, rows = (send_rows if side == "s" else recv_rows)(ti)
                return copy(outs, sems, kind, ti, k_src, row0, rows, dev)
            return build

        def first_half(core):
            return lambda ti: (rows_of(ti, core)[0], rows_of(ti, core)[1] // 2)

        def second_half(core):
            return lambda ti: (rows_of(ti, core)[0] + rows_of(ti, core)[1] // 2, rows_of(ti, core)[1] // 2)

        mine = lambda ti: rows_of(ti, c)
        theirs = lambda ti: rows_of(ti, 1 - c)
        split = {
            TO_X: mk(TO_X, me, kx, dev_x, mine, mine),
            TO_Y: mk(TO_Y, me, ky, dev_y, mine, mine),
            FWD_X: mk(FWD_X, ky, kd, dev_x, first_half(c), first_half(c)),
            FWD_Y: mk(FWD_Y, kx, kd, dev_y, second_half(c), second_half(c)),
            SIB_X: mk(SIB_X, kx, kx, sib, mine, theirs),
            SIB_Y: mk(SIB_Y, ky, ky, sib, mine, theirs),
            SIB_D: mk(SIB_D, kd, kd, sib, mine, theirs),
        }
        direct = {
            TO_X: mk(TO_X, me, kx, dev_x, whole, whole),
            TO_Y: mk(TO_Y, me, ky, dev_y, whole, whole),
            FWD_X: mk(FWD_X, me, kd, dev_d, whole, whole),
        }
        return split, direct

    is_split = [t.L * t.R * t.C >= DIRECT_BELOW for t, _, _ in parts]
    assert all(s or (r0, r1) == (0, t.R) for s, (t, r0, r1) in zip(is_split, parts))

    def start(ins, outs, sems):
        split, direct = plan(outs, sems)
        for ti in range(nt):
            kinds = split if is_split[ti] else direct
            kinds[TO_X](ti, "s").start()
            kinds[TO_Y](ti, "s").start()
            if not is_split[ti]:
                kinds[FWD_X](ti, "s").start()

    def mid(ins, outs, sems):
        split, _ = plan(outs, sems)
        for ti in range(nt):
            if is_split[ti]:
                split[TO_Y](ti, "r").wait_recv()
                split[FWD_X](ti, "s").start()
                split[SIB_Y](ti, "s").start()
        for ti in range(nt):
            if is_split[ti]:
                split[TO_X](ti, "r").wait_recv()
                split[FWD_Y](ti, "s").start()
                split[SIB_X](ti, "s").start()

    def finish(ins, outs, sems):
        split, direct = plan(outs, sems)
        for ti in range(nt):
            if is_split[ti]:
                split[FWD_X](ti, "r").wait_recv()
                split[FWD_Y](ti, "r").wait_recv()
                split[SIB_D](ti, "s").start()
            else:
                for kind in (TO_X, TO_Y, FWD_X):
                    direct[kind](ti, "r").wait_recv()
        for ti in range(nt):
            if is_split[ti]:
                for kind in (SIB_X, SIB_Y, SIB_D):
                    split[kind](ti, "r").wait_recv()
        for ti in range(nt):
            kinds = split if is_split[ti] else direct
            for kind in kinds:
                kinds[kind](ti, "s").wait_send()

    sems = pltpu.SemaphoreType.DMA((nt, 7))
    return Rider(list(fulls), [_sds(a.shape, a.dtype) for a in fulls], {i: i for i in range(nt)},
                 [sems, sems], start, mid, finish)


def _pair_exchange_rider(specs, grads):
    nt = len(specs)

    def copy(ins, outs, sems, ti, c, sibling):
        half = specs[ti].R // 2
        return pltpu.make_async_remote_copy(ins[ti].at[:, :, pl.ds((1 - c) * half, half), :], outs[ti],
                                            sems[0].at[ti], sems[1].at[ti], device_id=sibling, device_id_type=MESH)

    def start(ins, outs, sems):
        x, y, c, _ = _mesh_position()
        for ti in range(nt):
            copy(ins, outs, sems, ti, c, (x, y, 1 - c)).start()

    def finish(ins, outs, sems):
        x, y, c, _ = _mesh_position()
        for ti in range(nt):
            copy(ins, outs, sems, ti, c, (x, y, 1 - c)).wait()

    sems = pltpu.SemaphoreType.DMA((nt,))
    return Rider(list(grads), [_sds(_full_shape(t, t.R // 2), BF) for t in specs], {}, [sems, sems], start, None, finish)


def _pair_sum_job(t, g, land):
    assert t.L == 1
    half = t.R // 2
    nj = half // t.rb
    block = (None, t.A, t.rb, _ncb(t) * t.C)

    def fn(j, kc_ref, ins, outs):
        outs[0][...] = (ins[0][...].astype(F32) + ins[1][...].astype(F32)).astype(BF)

    return Job(nj,
               [(g, block, lambda j, kc_ref: (0, 0, kc_ref[1] * nj + j, 0)),
                (land, block, lambda j, kc_ref: (0, 0, j, 0))],
               [(_sds(_full_shape(t, half), BF), block, lambda j, kc_ref: (0, 0, j, 0))], fn)


def _scatter_rider(specs, sums):
    nt = len(specs)

    def copy(ins, outs, sems, ti, j, chip, c):
        t = specs[ti]
        cx, cy = chip
        return pltpu.make_async_remote_copy(_slot(ins[ti], t, 2 * cx + cy, 0, t.R // 2), outs[ti].at[j],
                                            sems[0].at[ti, j], sems[1].at[ti, j],
                                            device_id=(cx, cy, c), device_id_type=MESH)

    def start(ins, outs, sems):
        _, _, c, chips = _mesh_position()
        for j, chip in enumerate(chips):
            for ti in range(nt):
                copy(ins, outs, sems, ti, j, chip, c).start()

    def finish(ins, outs, sems):
        _, _, c, chips = _mesh_position()
        for j, chip in enumerate(chips):
            for ti in range(nt):
                copy(ins, outs, sems, ti, j, chip, c).wait()

    sems = pltpu.SemaphoreType.DMA((nt, N_CHIPS - 1))
    return Rider(list(sums), [_sds((N_CHIPS - 1, t.L, t.R // 2, t.C), BF) for t in specs], {}, [sems, sems],
                 start, None, finish)


def _chip_sum_job(ts, landed):
    t0 = ts[0]
    assert t0.L == 1
    half = t0.R // 2
    nj = half // t0.rb

    def local(j, li):
        return jnp.clip(j - li * nj, 0, nj - 1)

    ins = []
    for li, t in enumerate(ts):
        s, land = landed[t.name]

        def own_map(j, kc_ref, li=li, t=t):
            a, cb = _slot_index(t, kc_ref[0])
            return (0, a, local(j, li), cb)

        ins.append((s, (None, None, t.rb, t.C), own_map))
        ins.append((land, (N_CHIPS - 1, None, t.rb, t.C), lambda j, kc_ref, li=li: (0, 0, local(j, li), 0)))

    def fn(j, kc_ref, in_refs, outs):
        for li in range(len(ts)):
            @pl.when(j // nj == li)
            def _():
                acc = in_refs[2 * li][...].astype(F32)
                for k in range(N_CHIPS - 1):
                    acc = acc + in_refs[2 * li + 1][k].astype(F32)
                outs[0][...] = acc

    return Job(len(ts) * nj, ins,
               [(_sds((len(ts), t0.R, t0.C)), (None, t0.rb, t0.C),
                 lambda j, kc_ref: (j // nj, kc_ref[1] * nj + j % nj, 0))], fn)


def _adamw_job(rb, w, g, m, v):
    n_layers, r, c = w.shape
    nb = r // rb
    block = (None, rb, c)
    index = lambda j, kc_ref: (j // nb, j % nb, 0)

    def fn(j, kc_ref, ins, outs):
        g_v = ins[1][...]
        outs[0][...] = g_v
        outs[1][...], outs[2][...], outs[3][...] = _adamw_math(ins[0][...], g_v, ins[2][...], ins[3][...])

    return Job(n_layers * nb, [(a, block, index) for a in (w, g, m, v)],
               [(_sds(w.shape), block, index)] * 4, fn)


def _chip_sum_fused_job(ts, fused, by_cols):
    t0 = ts[0]
    own0 = fused[t0.name][0]
    if by_cols:
        rows, cols = own0.shape
    else:
        nb, rows, bw = own0.shape
        cols = nb * bw
    nj = rows // t0.rb

    def local(j, li):
        return jnp.clip(j - li * nj, 0, nj - 1)

    ins = []
    for li, t in enumerate(ts):
        own, land = fused[t.name]
        if by_cols:
            ins.append((own, (t.rb, cols), lambda j, kc_ref, li=li: (local(j, li), 0)))
            ins.append((land, (N_CHIPS - 1, t.rb, cols), lambda j, kc_ref, li=li: (0, local(j, li), 0)))
        else:
            ins.append((own, (nb, t.rb, bw), lambda j, kc_ref, li=li: (0, local(j, li), 0)))
            ins.append((land, (N_CHIPS - 1, nb, t.rb, bw), lambda j, kc_ref, li=li: (0, 0, local(j, li), 0)))

    def fn(j, kc_ref, in_refs, outs):
        for li in range(len(ts)):
            @pl.when(j // nj == li)
            def _():
                acc = in_refs[2 * li][...].astype(F32)
                for k in range(N_CHIPS - 1):
                    acc = acc + in_refs[2 * li + 1][k].astype(F32)
                outs[0][...] = acc if by_cols else jnp.concatenate([acc[b] for b in range(nb)], axis=1)

    def out_map(j, kc_ref):
        return (j // nj, j % nj, kc_ref[1]) if by_cols else (j // nj, kc_ref[1] * nj + j % nj, 0)

    return Job(len(ts) * nj, ins, [(_sds((len(ts), t0.R, t0.C)), (None, t0.rb, cols), out_map)], fn)


def _share_rider(halves, by_cols):
    nt = len(halves)

    def copy(outs, sems, ti, core, sibling):
        axis = 2 if by_cols[ti] else 1
        half = halves[ti].shape[axis] // 2
        piece = pl.ds(pl.multiple_of(core * half, 128 if by_cols[ti] else 8), half)
        part = outs[ti].at[:, :, piece] if by_cols[ti] else outs[ti].at[:, piece, :]
        return pltpu.make_async_remote_copy(part, part, sems[0].at[ti], sems[1].at[ti],
                                            device_id=sibling, device_id_type=MESH)

    def start(ins, outs, sems):
        x, y, c, _ = _mesh_position()
        for ti in range(nt):
            copy(outs, sems, ti, c, (x, y, 1 - c)).start()

    def finish(ins, outs, sems):
        x, y, c, _ = _mesh_position()
        for ti in range(nt):
            copy(outs, sems, ti, 1 - c, (x, y, 1 - c)).wait_recv()
        for ti in range(nt):
            copy(outs, sems, ti, c, (x, y, 1 - c)).wait_send()

    sems = pltpu.SemaphoreType.DMA((nt,))
    return Rider(list(halves), [_sds(a.shape, a.dtype) for a in halves], {i: i for i in range(nt)}, [sems, sems],
                 start, None, finish)


def _both(r1, r2):
    assert r1.mid is None and r2.mid is None
    ni, no, ns = len(r1.arrays), len(r1.out_shapes), len(r1.scratch)

    def split(fn1, fn2):
        def run(ins, outs, scr):
            fn1(ins[:ni], outs[:no], scr[:ns])
            fn2(ins[ni:], outs[no:], scr[ns:])
        return run

    aliases = dict(r1.aliases)
    aliases.update({ni + a: no + b for a, b in r2.aliases.items()})
    return Rider(r1.arrays + r2.arrays, r1.out_shapes + r2.out_shapes, aliases, r1.scratch + r2.scratch,
                 split(r1.start, r2.start), None, split(r1.finish, r2.finish))


def _adamw_math(w, g, m, v):
    m = B1 * m + (1.0 - B1) * g
    v = B2 * v + (1.0 - B2) * (g * g)
    delta = -LR * ((m / BC1) / (jnp.sqrt(v / BC2) + AEPS) + WD * w)
    return delta, m, v


GAIN_ROWS = {"pre_mix_g": 0, "post_mix_g": 2, "pre_ffn_g": 4, "post_ffn_g": 6, "ple_g": 8, "ple_post_g": 10}
ROW_KV_G, ROW_POOL_SCALE, ROW_SINKS, ROW_LOSS, PACK_ROWS = 12, 13, 14, 15, 16
SMALL_NAMES = tuple(GAIN_ROWS) + ("kv_g", "pool_scale", "sinks")


def _small_all_reduce(rows, dpool, rider=None):
    ng, pr = len(WINDOWS), POOL_G // N_CHIPS

    def body(*refs):
        row_refs = refs[:PACK_ROWS]
        dpool_ref, tot_ref, gpool_ref, pack, land, pland, send, recv, psend, precv = refs[PACK_ROWS:]
        x, y, c, _ = _mesh_position()
        me = 4 * x + 2 * y + c
        for r in range(PACK_ROWS):
            pack[r:r + 1, :] = row_refs[r][...]

        def shard_of(k):
            return dpool_ref.at[:, pl.ds(pl.multiple_of(k * pr, pr), pr), :]

        cps = []
        for j in range(1, N_DEV):
            px, py, pc = x ^ (j >> 2), y ^ ((j >> 1) & 1), c ^ (j & 1)
            cps.append(pltpu.make_async_remote_copy(pack, land.at[me], send.at[j], recv.at[j],
                                                    device_id=(px, py, pc), device_id_type=MESH))
            cps.append(pltpu.make_async_remote_copy(shard_of(2 * px + py), pland.at[me], psend.at[j], precv.at[j],
                                                    device_id=(px, py, pc), device_id_type=MESH))
        for cp in cps:
            cp.start()
        land[me] = pack[...]
        pland[me] = dpool_ref[:, pl.ds(pl.multiple_of((2 * x + y) * pr, pr), pr), :]
        for j in range(1, N_DEV):
            pltpu.make_async_remote_copy(pack, land.at[me ^ j], send.at[j], recv.at[j],
                                         device_id=(x, y, c), device_id_type=MESH).wait_recv()
            pltpu.make_async_remote_copy(shard_of(0), pland.at[me ^ j], psend.at[j], precv.at[j],
                                         device_id=(x, y, c), device_id_type=MESH).wait_recv()
        for cp in cps:
            cp.wait_send()
        tot = land[0]
        gp = pland[0].astype(F32)
        for d in range(1, N_DEV):
            tot = tot + land[d]
            gp = gp + pland[d].astype(F32)
        tot_ref[...] = tot
        gpool_ref[...] = gp

    sems = pltpu.SemaphoreType.DMA((N_DEV,))
    return _call(
        body, name="small_all_reduce", grid=(1,),
        in_specs=[VSPEC] * (PACK_ROWS + 1), out_specs=[VSPEC, VSPEC],
        out_shape=[_sds((PACK_ROWS, D)), _sds((ng, pr, POOL_G))],
        scratch_shapes=[pltpu.VMEM((PACK_ROWS, D), F32), pltpu.VMEM((N_DEV, PACK_ROWS, D), F32),
                        pltpu.VMEM((N_DEV, ng, pr, POOL_G), BF), sems, sems, sems, sems],
        args=[*rows, dpool], rider=rider)


def _small_adamw(tot, kc, small_w, small_m, small_v):
    names = SMALL_NAMES
    n = len(names)

    def body(*refs):
        tot_ref, kc_ref = refs[0], refs[1]
        w_refs = dict(zip(names, refs[2:2 + n]))
        m_refs = dict(zip(names, refs[2 + n:2 + 2 * n]))
        v_refs = dict(zip(names, refs[2 + 2 * n:2 + 3 * n]))
        loss_ref = refs[2 + 3 * n]
        out_refs = {nm: refs[3 + 3 * n + 4 * k: 7 + 3 * n + 4 * k] for k, nm in enumerate(names)}
        tot = tot_ref[...]
        loss_ref[...] = 0.5 * jnp.sum(tot[ROW_LOSS:ROW_LOSS + 1, :], axis=-1, keepdims=True) * (1.0 / D)

        def update(nm, g):
            g_ref, d_ref, nm_ref, nv_ref = out_refs[nm]
            g_ref[...] = g
            d_ref[...], nm_ref[...], nv_ref[...] = _adamw_math(w_refs[nm][...], g, m_refs[nm][...], v_refs[nm][...])

        for nm, r in GAIN_ROWS.items():
            update(nm, tot[r:r + 2, :])
        update("kv_g", tot[ROW_KV_G:ROW_KV_G + 1, :])
        k = kc_ref[0]
        width = D // N_CHIPS
        g_scale = jnp.zeros((1, width), F32)
        for kk in range(N_CHIPS):
            g_scale = g_scale + jnp.where(k == kk, tot[ROW_POOL_SCALE:ROW_POOL_SCALE + 1, kk * width:(kk + 1) * width], 0.0)
        update("pool_scale", g_scale)
        update("sinks", tot[ROW_SINKS:ROW_SINKS + 1, 0:N_HEADS])

    ins = [tot, kc] + [small_w[nm] for nm in names] + [small_m[nm] for nm in names] + [small_v[nm] for nm in names]
    out_shape = [_sds((1, 1))]
    for nm in names:
        out_shape += [_sds(small_w[nm].shape)] * 4
    outs = pl.pallas_call(
        body, name="small_adamw",
        in_specs=[VSPEC, SSPEC] + [VSPEC] * (3 * n), out_specs=[VSPEC] * len(out_shape), out_shape=out_shape,
        compiler_params=_params(),
    )(*ins)
    return outs[0], {nm: outs[1 + 4 * k: 5 + 4 * k] for k, nm in enumerate(names)}


def _compute_layout(t, full):
    if t.src == "w_gu":
        return full.reshape(2, D, FF)
    if t.src == "pool_w":
        return full.reshape(len(WINDOWS), POOL_G, POOL_G)
    if t.src == "pool_scale":
        return full.reshape(1, D)
    return full.reshape(t.A * t.R, _ncb(t) * t.C)


def kernel(x, p, pre_mix_g, post_mix_g, pre_ffn_g, post_ffn_g, pool_w, pool_scale, kv_g, w_kv, w_q, sinks, w_o, w_gu, w_down, ple_g, w_ple_gate, w_ple_proj, ple_post_g, loss_target, m_pre_mix_g, m_post_mix_g, m_pre_ffn_g, m_post_ffn_g, m_pool_w, m_pool_scale, m_kv_g, m_w_kv, m_w_q, m_sinks, m_w_o, m_w_gu, m_w_down, m_ple_g, m_w_ple_gate, m_w_ple_proj, m_ple_post_g, v_pre_mix_g, v_post_mix_g, v_pre_ffn_g, v_post_ffn_g, v_pool_w, v_pool_scale, v_kv_g, v_w_kv, v_w_q, v_sinks, v_w_o, v_w_gu, v_w_down, v_ple_g, v_w_ple_gate, v_w_ple_proj, v_ple_post_g):
    weights = dict(pre_mix_g=pre_mix_g, post_mix_g=post_mix_g, pre_ffn_g=pre_ffn_g, post_ffn_g=post_ffn_g,
                   pool_w=pool_w, pool_scale=pool_scale, kv_g=kv_g, w_kv=w_kv, w_q=w_q, sinks=sinks, w_o=w_o,
                   w_gu=w_gu, w_down=w_down, ple_g=ple_g, w_ple_gate=w_ple_gate, w_ple_proj=w_ple_proj,
                   ple_post_g=ple_post_g)
    m_in = dict(pre_mix_g=m_pre_mix_g, post_mix_g=m_post_mix_g, pre_ffn_g=m_pre_ffn_g, post_ffn_g=m_post_ffn_g,
                pool_w=m_pool_w, pool_scale=m_pool_scale, kv_g=m_kv_g, w_kv=m_w_kv, w_q=m_w_q, sinks=m_sinks,
                w_o=m_w_o, w_gu=m_w_gu, w_down=m_w_down, ple_g=m_ple_g, w_ple_gate=m_w_ple_gate,
                w_ple_proj=m_w_ple_proj, ple_post_g=m_ple_post_g)
    v_in = dict(pre_mix_g=v_pre_mix_g, post_mix_g=v_post_mix_g, pre_ffn_g=v_pre_ffn_g, post_ffn_g=v_post_ffn_g,
                pool_w=v_pool_w, pool_scale=v_pool_scale, kv_g=v_kv_g, w_kv=v_w_kv, w_q=v_w_q, sinks=v_sinks,
                w_o=v_w_o, w_gu=v_w_gu, w_down=v_w_down, ple_g=v_ple_g, w_ple_gate=v_w_ple_gate,
                w_ple_proj=v_w_ple_proj, ple_post_g=v_ple_post_g)
    order = ["pre_mix_g", "post_mix_g", "pre_ffn_g", "post_ffn_g", "pool_w", "pool_scale", "kv_g", "w_kv", "w_q",
             "sinks", "w_o", "w_gu", "w_down", "ple_g", "w_ple_gate", "w_ple_proj", "ple_post_g"]

    kc = jnp.stack([2 * lax.axis_index("x") + lax.axis_index("y"), lax.axis_index("c")]).astype(jnp.int32)
    s_len = x.shape[1]
    x2d = x.reshape(s_len, D)
    p3d = p.reshape(2, s_len, PLE)
    target = loss_target.reshape(s_len, D)
    kv_g2d = kv_g.reshape(1, D)
    gains = {nm: weights[nm] for nm in GAIN_ROWS}

    def shard_view(src, a):
        t = next(t for t in BIGS.values() if t.src == src)
        return a.reshape(-1, t.R, t.C)

    first, second = ["pool_w", "pool_scale"], ["w_gu0", "w_down0"]
    rest = [nm for nm in BIGS if nm not in first + second]
    specs = dict(BIGS, pool_scale=POOL_SCALE)
    placed = {}

    def place_job(nm):
        if nm == "pool_scale":
            return _place_job(POOL_SCALE, pool_scale.reshape(1, 1, D // N_CHIPS), F32)
        return _place_job(BIGS[nm], shard_view(BIGS[nm].src, weights[BIGS[nm].src]))

    def gather(names, rows=None):
        rows = rows or {}
        parts = [(specs[nm],) + tuple(rows.get(nm, (0, specs[nm].R))) for nm in names]
        return _gather_rider(parts, [placed[nm] for nm in names])

    def take(names, results):
        for nm, a in zip(names, results):
            placed[nm] = a

    def weight(nm):
        return _compute_layout(specs[nm], placed[nm])

    take(first, [r[0] for r in _multi_call("place_pool", [place_job(nm) for nm in first], kc)])
    cast, got = _multi_call("place_ffn0", [place_job(nm) for nm in second], kc, rider=gather(first))
    take(second, [r[0] for r in cast])
    take(first, got)
    jobs = [place_job(nm) for nm in rest]
    jobs.append(_mixa_fwd_job(x2d, gains["pre_mix_g"], weight("pool_w"), weight("pool_scale"), gains["post_mix_g"]))
    results, got = _multi_call("cast_and_mixa_fwd", jobs, kc, rider=gather(second))
    take(rest, [r[0] for r in results[:-1]])
    take(second, got)
    y0, x1 = results[-1]

    ride = ["w_ple_gate0", "w_ple_proj0", "w_q", "w_kv", "w_o", "w_gu1"]
    (f0, x2, g0, u0), got = _ffn_fwd(0, x1, gains["pre_ffn_g"], weight("w_gu0"), weight("w_down0"), gains["post_ffn_g"],
                             rider=gather(ride, {"w_gu1": (0, 320)}))
    take(ride, got)

    ride = ["w_ple_gate1", "w_ple_proj1", "w_gu1"]
    (z0, pe0, x3, q, kv), got = _ple_fwd(
        0, x2, p3d, gains["ple_g"], weight("w_ple_gate0"), weight("w_ple_proj0"), gains["ple_post_g"],
        qkv=(gains["pre_mix_g"], kv_g2d, weight("w_q"), weight("w_kv")),
        rider=gather(ride, {"w_gu1": (320, 704)}))
    take(ride, got)

    ride = ["w_down1", "w_gu1"]
    (attn, y1, x4), got = _attn_fwd(q, kv, sinks, x3, weight("w_o"), gains["post_mix_g"],
                                    rider=gather(ride, {"w_gu1": (704, D)}))
    take(ride, got)

    (f1, x5, g1, u1, z1, pe1, dx6, loss_row), _ = _ffn_fwd(
        1, x4, gains["pre_ffn_g"], weight("w_gu1"), weight("w_down1"), gains["post_ffn_g"],
        head=(p3d, gains["ple_g"], weight("w_ple_gate1"), weight("w_ple_proj1"), gains["ple_post_g"], target))

    local = {}
    landed = {}
    fused = {}

    def local_grads(names):
        return [local[nm].reshape(_full_shape(BIGS[nm])) for nm in names]

    def pair_exchange(names):
        return _pair_exchange_rider([BIGS[nm] for nm in names], local_grads(names))

    def pair_sum(tag, names, lands):
        jobs = [_pair_sum_job(BIGS[nm], g, l) for nm, g, l in zip(names, local_grads(names), lands)]
        return [r[0] for r in _multi_call(f"pair_sum_{tag}", jobs, kc)]

    def scatter(names, sums):
        return _scatter_rider([BIGS[nm] for nm in names], sums)

    def keep(names, sums, got):
        for nm, s, l in zip(names, sums, got):
            landed[nm] = (s, l)

    (dx5, local["w_ple_gate1"], local["w_ple_proj1"], d_ple1, d_plepost1), _ = _ple_bwd(
        1, dx6, x5, z1, pe1, p3d, gains["ple_g"], weight("w_ple_gate1"), gains["ple_post_g"])

    group_a = ["w_ple_gate1", "w_ple_proj1"]
    (dx4, d_preffn1, d_postffn1, *scattered), lands_a = _ffn_bwd(
        1, dx5, x4, f1, g1, u1, gains["pre_ffn_g"], weight("w_gu1"), weight("w_down1"), gains["post_ffn_g"], kc,
        rider=pair_exchange(group_a))
    fused["w_gu1"], fused["w_down1"] = scattered[0:2], scattered[2:4]

    (dq, dkv, local["w_o"], d_postmix1, d_sinks), _ = _attn_bwd(
        dx4, y1, attn, q, kv, sinks, weight("w_o"), gains["post_mix_g"])
    dx3, local["w_q"], local["w_kv"], d_premix1, d_kvg = _qkv_bwd(
        dq, dkv, x3, dx4, gains["pre_mix_g"], kv_g2d, weight("w_q"), weight("w_kv"))

    group_b = ["w_o", "w_q", "w_kv"]
    (dx2, local["w_ple_gate0"], local["w_ple_proj0"], d_ple0, d_plepost0), lands_b = _ple_bwd(
        0, dx3, x2, z0, pe0, p3d, gains["ple_g"], weight("w_ple_gate0"), gains["ple_post_g"],
        rider=pair_exchange(group_b))
    group_ab = group_a + group_b
    sums_ab = pair_sum("ab", group_ab, lands_a + lands_b)

    group_c = ["w_ple_gate0", "w_ple_proj0"]
    (dx1, d_preffn0, d_postffn0, *scattered), got = _ffn_bwd(
        0, dx2, x1, f0, g0, u0, gains["pre_ffn_g"], weight("w_gu0"), weight("w_down0"), gains["post_ffn_g"], kc,
        rider=_both(pair_exchange(group_c), scatter(group_ab, sums_ab)))
    fused["w_gu0"], fused["w_down0"] = scattered[0:2], scattered[2:4]
    sums_c = pair_sum("c", group_c, got[:len(group_c)])
    keep(group_ab, sums_ab, got[len(group_c):])

    layers_of = lambda src: [t for t in BIGS.values() if t.src == src]
    own_scatter = ["w_gu", "w_down"]
    early = own_scatter + ["w_q", "w_o", "w_kv"]
    late = ["w_ple_gate", "w_ple_proj"]
    by_cols = lambda srcs: [src == "w_down" for src in srcs]
    jobs = [_chip_sum_fused_job(layers_of(src), fused, by_cols=src == "w_down") for src in own_scatter]
    jobs += [_chip_sum_job(layers_of(src), landed) for src in early if src not in own_scatter]
    halves = [r[0] for r in _multi_call("chip_sum_early", jobs, kc)]
    (dx0, d_pool, d_scale, d_postmix0, d_premix0), got = _mixa_bwd(
        dx1, x2d, y0, gains["pre_mix_g"], weight("pool_w"), weight("pool_scale"), gains["post_mix_g"],
        rider=_both(scatter(group_c, sums_c), _share_rider(halves, by_cols(early))))
    keep(group_c, sums_c, got[:len(group_c)])
    full_grads = dict(zip(early, got[len(group_c):]))

    rows = [d_premix0, d_premix1, d_postmix0, d_postmix1, d_preffn0, d_preffn1, d_postffn0, d_postffn1,
            d_ple0, d_ple1, d_plepost0, d_plepost1, d_kvg, d_scale, d_sinks, loss_row]
    as2d = lambda a: a.reshape(1, D) if a.ndim == 1 else a
    (tot, g_pool), _ = _small_all_reduce(rows, d_pool)
    loss, small = _small_adamw(tot, kc, {nm: as2d(weights[nm]) for nm in SMALL_NAMES},
                               {nm: as2d(m_in[nm]) for nm in SMALL_NAMES},
                               {nm: as2d(v_in[nm]) for nm in SMALL_NAMES})

    halves = [r[0] for r in _multi_call("chip_sum_late", [_chip_sum_job(layers_of(src), landed) for src in late], kc)]
    full_grads.update(zip(late, _run("grads_pair_share", _share_rider(halves, by_cols(late)))))
    full_grads["pool_w"] = g_pool

    def adam_job(src):
        rb = layers_of(src)[0].rb // (1 if src == "pool_w" else 2)
        return _adamw_job(rb, shard_view(src, weights[src]), full_grads[src],
                          shard_view(src, m_in[src]), shard_view(src, v_in[src]))

    out = {"grad": {}, "delta": {}, "new_m": {}, "new_v": {}}
    results = dict(zip(BIG_SOURCES, _multi_call("adamw", [adam_job(src) for src in BIG_SOURCES], kc)))
    for src in BIG_SOURCES:
        shape = weights[src].shape
        for kind, a in zip(("grad", "delta", "new_m", "new_v"), results[src]):
            out[kind][src] = a.reshape(shape)
    for nm in SMALL_NAMES:
        shape = weights[nm].shape
        for kind, a in zip(("grad", "delta", "new_m", "new_v"), small[nm]):
            out[kind][nm] = a.reshape(shape)

    return (loss.reshape(()), dx0.reshape(x.shape),
            *[out["grad"][nm] for nm in order], *[out["delta"][nm] for nm in order],
            *[out["new_m"][nm] for nm in order], *[out["new_v"][nm] for nm in order])
```

```python
import collections

import jax
import jax.numpy as jnp
from jax import lax
from jax.experimental import pallas as pl
from jax.experimental.pallas import tpu as pltpu

D = 1024
FF = 2816
N_HEADS = 16
HEAD_DIM = 64
N_KV_HEADS = 4
GQA = N_HEADS // N_KV_HEADS
KVD = N_KV_HEADS * HEAD_DIM
PLE = 256
BLK = 128
WINDOWS = (2, 4, 8, 16)
POOL_G = 256
HALO = 16
EPS = 1e-6
NEG_INF = -1e30
ATT_SCALE = HEAD_DIM ** -0.5
SLOPES = tuple(2.0 ** (-8.0 * (h + 1) / N_HEADS) for h in range(N_HEADS))
N_CHIPS = 4
N_DEV = 8

LR, B1, B2, AEPS, WD, STEP = 0.001, 0.9, 0.999, 1e-08, 0.01, 10
BC1 = 1.0 - B1 ** STEP
BC2 = 1.0 - B2 ** STEP

BF = jnp.bfloat16
F32 = jnp.float32
MESH = pl.DeviceIdType.MESH
VMEM_LIMIT_V7X = 58 * 1024 * 1024
TM = 256
TM_FFN_BWD = 512
FF_CHUNK = 256
FF_HALF = FF // 2

VSPEC = pl.BlockSpec(memory_space=pltpu.VMEM)
SSPEC = pl.BlockSpec(memory_space=pltpu.SMEM)
ANYSPEC = pl.BlockSpec(memory_space=pl.ANY)


def _params(n_grid=0):
    sem = ("arbitrary",) * n_grid if n_grid else None
    return pltpu.CompilerParams(dimension_semantics=sem, vmem_limit_bytes=VMEM_LIMIT_V7X)


def _sds(shape, dtype=F32):
    return jax.ShapeDtypeStruct(tuple(shape), dtype)


Rider = collections.namedtuple("Rider", "arrays out_shapes aliases scratch start mid finish")
MID_NUM, MID_DEN = 5, 8


def _call(body, *, name, grid, in_specs, out_specs, out_shape, args, scratch_shapes=(), rider=None, prefetch=None):
    ni, no, ns = len(in_specs), len(out_specs), len(scratch_shapes)
    npre = 0 if prefetch is None else 1
    pre = [] if prefetch is None else [prefetch]
    if rider is None:
        rider = Rider([], [], {}, [], None, None, None)
    ri, ro = len(rider.arrays), len(rider.out_shapes)

    def full(*refs):
        pre_refs, refs = refs[:npre], refs[npre:]
        ins, refs = refs[:ni], refs[ni:]
        rins, refs = refs[:ri], refs[ri:]
        outs, refs = refs[:no], refs[no:]
        routs, refs = refs[:ro], refs[ro:]
        scr, rscr = refs[:ns], refs[ns:]
        ids = [pl.program_id(a) for a in range(len(grid))]
        first = ids[0] == 0
        last = ids[0] == grid[0] - 1
        for a in range(1, len(grid)):
            first = first & (ids[a] == 0)
            last = last & (ids[a] == grid[a] - 1)

        if rider.start is not None:
            @pl.when(first)
            def _():
                rider.start(rins, routs, rscr)

        if rider.mid is not None:
            assert len(grid) == 1

            @pl.when(ids[0] == (grid[0] * MID_NUM) // MID_DEN)
            def _():
                rider.mid(rins, routs, rscr)

        body(*pre_refs, *ins, *outs, *scr)

        if rider.finish is not None:
            @pl.when(last)
            def _():
                rider.finish(rins, routs, rscr)

    outs = pl.pallas_call(
        full, name=name,
        grid_spec=pltpu.PrefetchScalarGridSpec(
            num_scalar_prefetch=npre, grid=grid,
            in_specs=list(in_specs) + [ANYSPEC] * ri, out_specs=list(out_specs) + [ANYSPEC] * ro,
            scratch_shapes=list(scratch_shapes) + list(rider.scratch)),
        out_shape=list(out_shape) + list(rider.out_shapes),
        input_output_aliases={npre + ni + a: no + b for a, b in rider.aliases.items()},
        compiler_params=_params(len(grid)))(*pre, *args, *rider.arrays)
    return list(outs[:no]), list(outs[no:])


Job = collections.namedtuple("Job", "steps ins outs fn")


def _multi_call(name, jobs, kc, rider=None):
    n = max(job.steps for job in jobs)

    def clamped(index, steps):
        return lambda s, kc_ref: index(jnp.minimum(s, steps - 1), kc_ref)

    in_specs, out_specs, out_shape, args = [], [], [], []
    for job in jobs:
        for arr, block, index, *single in job.ins:
            mode = dict(pipeline_mode=pl.Buffered(1)) if single and single[0] else {}
            in_specs.append(pl.BlockSpec(block, clamped(index, job.steps), **mode))
            args.append(arr)
        for sds, block, index in job.outs:
            out_specs.append(pl.BlockSpec(block, clamped(index, job.steps)))
            out_shape.append(sds)
    n_in = len(args)

    def body(kc_ref, *refs):
        s = pl.program_id(0)
        i0, o0 = 0, n_in
        for job in jobs:
            ins, outs = refs[i0:i0 + len(job.ins)], refs[o0:o0 + len(job.outs)]
            i0, o0 = i0 + len(job.ins), o0 + len(job.outs)

            @pl.when(s < job.steps)
            def _():
                job.fn(s, kc_ref, ins, outs)

    outs, routs = _call(body, name=name, grid=(n,), in_specs=in_specs, out_specs=out_specs, out_shape=out_shape,
                        args=args, prefetch=kc, rider=rider)
    res, o0 = [], 0
    for job in jobs:
        res.append(outs[o0:o0 + len(job.outs)])
        o0 += len(job.outs)
    return res if rider is None else (res, routs)


def _rms_fwd(x, g):
    r = lax.rsqrt(jnp.mean(x * x, axis=-1, keepdims=True) + EPS)
    return x * r * g


def _rms_bwd(x, g, dy):
    r = lax.rsqrt(jnp.mean(x * x, axis=-1, keepdims=True) + EPS)
    xn = x * r
    dxn = dy * g
    dx = r * (dxn - xn * jnp.mean(dxn * xn, axis=-1, keepdims=True))
    return dx, dy * xn


def _rowsum(a):
    return jnp.sum(a, axis=0, keepdims=True)


def _sigmoid(z):
    return 1.0 / (1.0 + jnp.exp(-z))


def _dot(a, b):
    return jnp.dot(a, b, preferred_element_type=F32)


def _dot_nt(a, b):
    return lax.dot_general(a, b, (((1,), (1,)), ((), ())), preferred_element_type=F32)


def _dot_tn(a, b):
    return lax.dot_general(a, b, (((0,), (0,)), ((), ())), preferred_element_type=F32)


def _row_spec(tm, width=D):
    return pl.BlockSpec((tm, width), lambda i: (i, 0))


def _const_spec(shape):
    zeros = (0,) * len(shape)
    return pl.BlockSpec(tuple(shape), lambda *_: zeros)


def _pool_delta(he, pos):
    out = []
    for gi, w in enumerate(WINDOWS):
        hg = he[:, gi * POOL_G:(gi + 1) * POOL_G]
        s = hg
        k = 1
        while k < w:
            s = s + pltpu.roll(s, k, 0)
            k *= 2
        cnt = jnp.maximum(jnp.minimum(pos + 1, w), 1).astype(F32)
        out.append(s / cnt - hg)
    return out


def _load_with_halo_before(x_ref, i, tm):
    r0 = pl.multiple_of(i * tm, tm)
    hs = pl.multiple_of(jnp.maximum(i * tm - HALO, 0), 8)
    xh = jnp.where(i > 0, x_ref[pl.ds(hs, HALO), :], 0.0)
    xt = x_ref[pl.ds(r0, tm), :]
    return xt, jnp.concatenate([xh, xt], axis=0)


def _mixa_fwd_job(x, pre_g, pool_w, pool_scale, post_g):
    s_len = x.shape[0]

    def fn(i, kc_ref, ins, outs):
        x_ref, pg_ref, w_ref, sc_ref, qg_ref = ins
        y_ref, x1_ref = outs
        xt, xe = _load_with_halo_before(x_ref, i, TM)
        he = _rms_fwd(xe, pg_ref[0:1, :])
        pos = i * TM - HALO + lax.broadcasted_iota(jnp.int32, (TM + HALO, 1), 0)
        ds = _pool_delta(he, pos)
        ys = [_dot(ds[gi][HALO:, :].astype(BF), w_ref[gi]) for gi in range(len(WINDOWS))]
        y = jnp.concatenate(ys, axis=1) * sc_ref[...]
        y_ref[...] = y
        x1_ref[...] = xt + _rms_fwd(y, qg_ref[0:1, :])

    def whole(a):
        zeros = (0,) * a.ndim
        return (a, a.shape, lambda j, kc_ref: zeros, True)

    rows = lambda j, kc_ref: (j, 0)
    return Job(s_len // TM, [whole(a) for a in (x, pre_g, pool_w, pool_scale, post_g)],
               [(_sds((s_len, D)), (TM, D), rows), (_sds((s_len, D)), (TM, D), rows)], fn)


def _mixa_bwd(dx1, x, y, pre_g, pool_w, pool_scale, post_g, rider=None):
    s_len = x.shape[0]
    n = s_len // TM
    ng = len(WINDOWS)

    def body(dx_ref, x_ref, y_ref, pg_ref, w_ref, sc_ref, qg_ref,
             dx0_ref, dw_ref, dsc_ref, dqg_ref, dpg_ref, wacc):
        i = pl.program_id(0)

        @pl.when(i == 0)
        def _():
            wacc[...] = jnp.zeros_like(wacc)
            dsc_ref[...] = jnp.zeros_like(dsc_ref)
            dqg_ref[...] = jnp.zeros_like(dqg_ref)
            dpg_ref[...] = jnp.zeros_like(dpg_ref)

        r0 = pl.multiple_of(i * TM, TM)
        xt, xe = _load_with_halo_before(x_ref, i, TM)
        he = _rms_fwd(xe, pg_ref[0:1, :])
        pos_b = i * TM - HALO + lax.broadcasted_iota(jnp.int32, (TM + HALO, 1), 0)
        ds = _pool_delta(he, pos_b)

        last = i == n - 1
        a0 = pl.multiple_of(jnp.minimum(i * TM + TM, s_len - HALO), 8)
        ye = jnp.concatenate([y_ref[pl.ds(r0, TM), :], y_ref[pl.ds(a0, HALO), :]], axis=0)
        dt = dx_ref[pl.ds(r0, TM), :]
        de = jnp.concatenate([dt, jnp.where(last, 0.0, dx_ref[pl.ds(a0, HALO), :])], axis=0)
        dye, prod = _rms_bwd(ye, qg_ref[0:1, :], de)
        dqg_ref[...] += _rowsum(prod[:TM, :])
        dys = dye * sc_ref[...]
        pos_a = i * TM + lax.broadcasted_iota(jnp.int32, (TM + HALO, 1), 0)

        dhs, dscs = [], []
        for gi, w in enumerate(WINDOWS):
            sl = slice(gi * POOL_G, (gi + 1) * POOL_G)
            wg = w_ref[gi]
            dys_g = dys[:, sl].astype(BF)
            d_g = ds[gi][HALO:, :].astype(BF)
            ypre = _dot(d_g, wg)
            dscs.append(_rowsum(dye[:TM, sl] * ypre))
            wacc[gi] += _dot_tn(d_g, dys_g[:TM, :])
            dd = _dot_nt(dys_g, wg)
            cnt = jnp.minimum(pos_a + 1, w).astype(F32)
            a = dd / cnt
            k = 1
            while k < w:
                a = a + pltpu.roll(a, TM + HALO - k, 0)
                k *= 2
            dhs.append(a[:TM, :] - dd[:TM, :])
        dsc_ref[...] += jnp.concatenate(dscs, axis=1)
        dh = jnp.concatenate(dhs, axis=1)
        dxp, prod2 = _rms_bwd(xt, pg_ref[0:1, :], dh)
        dpg_ref[...] += _rowsum(prod2)
        dx0_ref[...] = dt + dxp

        @pl.when(last)
        def _():
            dw_ref[...] = wacc[...].astype(BF)

    return _call(
        body, name="mixa_bwd", grid=(n,), in_specs=[VSPEC] * 7,
        out_specs=[_row_spec(TM), _const_spec((ng, POOL_G, POOL_G)), _const_spec((1, D)),
                   _const_spec((1, D)), _const_spec((1, D))],
        out_shape=[_sds((s_len, D)), _sds((ng, POOL_G, POOL_G), BF), _sds((1, D)), _sds((1, D)), _sds((1, D))],
        scratch_shapes=[pltpu.VMEM((ng, POOL_G, POOL_G), F32)],
        args=[dx1, x, y, pre_g, pool_w, pool_scale, post_g], rider=rider)


def _ple_math(layer, x, p_blk, g_ref, wg_ref, wp_ref, qg_ref):
    r = _rms_fwd(x, g_ref[layer:layer + 1, :]).astype(BF)
    z = _dot(r, wg_ref[...])
    pe = _dot(p_blk.astype(BF), wp_ref[...])
    return z, pe, x + _rms_fwd(pe * _sigmoid(z), qg_ref[layer:layer + 1, :])


def _ffn_fwd(layer, x1, pre_g, wgu, wd, post_g, rider=None, head=None):
    s_len = x1.shape[0]

    def body(*refs):
        if head:
            (x_ref, pg_ref, wgu_ref, wd_ref, qg_ref, p_ref, eg_ref, wg_ref, wp_ref, eq_ref, t_ref,
             f_ref, x2_ref, g_ref, u_ref, z_ref, pe_ref, dx_ref, lv_ref) = refs
        else:
            x_ref, pg_ref, wgu_ref, wd_ref, qg_ref, f_ref, x2_ref, g_ref, u_ref = refs
        x = x_ref[...]
        h = _rms_fwd(x, pg_ref[layer:layer + 1, :]).astype(BF)
        f = jnp.zeros((TM, D), F32)
        for c in range(FF // FF_HALF):
            cols = slice(c * FF_HALF, (c + 1) * FF_HALF)
            g = _dot(h, wgu_ref[0, :, cols])
            u = _dot(h, wgu_ref[1, :, cols])
            g_ref[:, cols] = g.astype(BF)
            u_ref[:, cols] = u.astype(BF)
            act = g * _sigmoid(g) * u
            f = f + _dot(act.astype(BF), wd_ref[cols, :])
        f_ref[...] = f
        x2 = x + _rms_fwd(f, qg_ref[layer:layer + 1, :])
        x2_ref[...] = x2
        if head:
            @pl.when(pl.program_id(0) == 0)
            def _():
                lv_ref[...] = jnp.zeros_like(lv_ref)
            z, pe, x3 = _ple_math(layer, x2, p_ref[...], eg_ref, wg_ref, wp_ref, eq_ref)
            z_ref[...] = z
            pe_ref[...] = pe
            err = x3 - t_ref[...]
            dx_ref[...] = err * (1.0 / D)
            lv_ref[...] += _rowsum(err * err)

    in_specs = [_row_spec(TM), VSPEC, VSPEC, VSPEC, VSPEC]
    args = [x1, pre_g, wgu, wd, post_g]
    out_specs = [_row_spec(TM), _row_spec(TM), _row_spec(TM, FF), _row_spec(TM, FF)]
    out_shape = [_sds((s_len, D)), _sds((s_len, D)), _sds((s_len, FF), BF), _sds((s_len, FF), BF)]
    if head:
        p, ple_g, w_gate, w_proj, ple_post_g, target = head
        in_specs += [pl.BlockSpec((None, TM, PLE), lambda i: (layer, i, 0)), VSPEC, VSPEC, VSPEC, VSPEC, _row_spec(TM)]
        args += [p, ple_g, w_gate, w_proj, ple_post_g, target]
        out_specs += [_row_spec(TM), _row_spec(TM), _row_spec(TM), _const_spec((1, D))]
        out_shape += [_sds((s_len, D))] * 3 + [_sds((1, D))]
    return _call(body, name=f"ffn_fwd{layer}", grid=(s_len // TM,), in_specs=in_specs, out_specs=out_specs,
                 out_shape=out_shape, args=args, rider=rider)


GU_PIECE = 128
DN_PIECE = 64
DN_SLOT = FF // N_CHIPS
HALF_D = D // 2
CHUNK_STRIDE = 6
CHUNK_START = (1, 7, 4, 10)


def _ffn_bwd(layer, dx2, x1, f, g_pre, u_pre, pre_g, wgu, wd, post_g, kc, rider=None):
    s_len = x1.shape[0]
    tm = TM_FFN_BWD
    n = s_len // tm
    nc = FF // FF_CHUNK
    n_gu, n_dn = FF_CHUNK // GU_PIECE, FF_CHUNK // DN_PIECE
    n_pieces = 2 * n_gu + n_dn
    n_blk = FF_HALF // GU_PIECE

    def edge_rows(c, i, kc_ref):
        return (jnp.where((c == 0) | (c == nc - 1), i, n - 1), 0)

    def chunk_at(c, kc_ref):
        k = kc_ref[0]
        start = jnp.where(k == 0, CHUNK_START[0], jnp.where(k == 1, CHUNK_START[1],
                                                            jnp.where(k == 2, CHUNK_START[2], CHUNK_START[3])))
        return ((c + start) * CHUNK_STRIDE) % nc

    def exchange(kc_ref, c, accg, accu, accd, own_gu_ref, land_gu_ref, own_dn_ref, land_dn_ref,
                 pl_gu, pl_dn, sib_gu, sib_dn, mine_gu, mine_dn, sum_gu, sum_dn,
                 psend, precv, ssend, lsem, rrecv):
        x, y, core = lax.axis_index("x"), lax.axis_index("y"), lax.axis_index("c")
        lower = core == 0

        def pair_copy(cc, part):
            p = cc % 2
            src, dst = ((sib_gu, pl_gu), (sib_dn, pl_dn))[part]
            return pltpu.make_async_remote_copy(src.at[p], dst.at[cc], psend.at[p, part], precv.at[cc, part],
                                                device_id=(x, y, 1 - core), device_id_type=MESH)

        def scatter(cc, wait):
            p = cc % 2
            hidden = chunk_at(cc, kc_ref) * FF_CHUNK

            assert n_gu == 2
            k0, k1 = hidden // FF_HALF, (hidden + GU_PIECE) // FF_HALF
            blk = (hidden - k0 * FF_HALF) // GU_PIECE
            for gu in range(2):
                @pl.when(k0 == k1)
                def _():
                    piece(p, wait, 2 * gu, sum_gu.at[p, gu], k0 + 2 * gu, 0, (pl.ds(blk, 2),))

                @pl.when(k0 != k1)
                def _():
                    piece(p, wait, 2 * gu, sum_gu.at[p, gu, 0], k0 + 2 * gu, 0, (blk,))
                    piece(p, wait, 2 * gu + 1, sum_gu.at[p, gu, 1], k1 + 2 * gu, 0, (0,))

            kd = hidden // DN_SLOT
            off = pl.multiple_of(hidden - kd * DN_SLOT, DN_PIECE)
            m = jnp.minimum((DN_SLOT - off) // DN_PIECE, n_dn)
            for mm in range(1, n_dn + 1):
                @pl.when(m == mm)
                def _():
                    rows = mm * DN_PIECE
                    piece(p, wait, 2 * n_gu, sum_dn.at[p, pl.ds(0, rows), :], kd, 1, (pl.ds(off, rows), slice(None)))
                    if mm < n_dn:
                        piece(p, wait, 2 * n_gu + 1, sum_dn.at[p, pl.ds(rows, FF_CHUNK - rows), :], kd + 1, 1,
                              (pl.ds(0, FF_CHUNK - rows), slice(None)))

        def piece(p, wait, pi, src, k, t, where):
            own_ref, land_ref = ((own_gu_ref, land_gu_ref), (own_dn_ref, land_dn_ref))[t]
            kx, ky = k // 2, k % 2
            fx, fy = (kx != x).astype(jnp.int32), (ky != y).astype(jnp.int32)
            local = (fx + fy) == 0
            j = jnp.maximum(fx + 2 * fy - 1, 0)

            @pl.when(local)
            def _():
                cp = pltpu.make_async_copy(src, own_ref.at[where], lsem.at[p, pi])
                if wait:
                    cp.wait()
                else:
                    cp.start()

            @pl.when(jnp.logical_not(local))
            def _():
                cp = pltpu.make_async_remote_copy(src, land_ref.at[(j,) + where], ssend.at[p, pi],
                                                  rrecv.at[t, j], device_id=(kx, ky, core), device_id_type=MESH)
                if wait:
                    cp.wait_send()
                else:
                    cp.start()

        def add_and_scatter(cc):
            p = cc % 2
            pair_copy(cc, 0).wait_recv()
            pair_copy(cc, 1).wait_recv()
            s_gu = (mine_gu[...] + pl_gu[cc].astype(F32)).astype(BF)
            for hc in range(n_gu):
                sum_gu[p, :, hc] = s_gu[:, :, hc * GU_PIECE:(hc + 1) * GU_PIECE]
            sum_dn[p] = (mine_dn[...] + pl_dn[cc].astype(F32)).astype(BF)
            scatter(cc, wait=False)

        @pl.when(c >= 1)
        def _():
            @pl.when(c >= 3)
            def _():
                scatter(c - 3, wait=True)
            add_and_scatter(c - 1)

        @pl.when(c >= 2)
        def _():
            pair_copy(c - 2, 0).wait_send()
            pair_copy(c - 2, 1).wait_send()

        p = c % 2
        my_rows = pl.ds(pl.multiple_of(core * HALF_D, HALF_D), HALF_D)
        sib_rows = pl.ds(pl.multiple_of((1 - core) * HALF_D, HALF_D), HALF_D)
        d_v = accd[...]
        sib_gu[p, 0] = accg[sib_rows, :].astype(BF)
        sib_gu[p, 1] = accu[sib_rows, :].astype(BF)
        sib_dn[p] = jnp.where(lower, d_v[:, HALF_D:], d_v[:, :HALF_D]).astype(BF)
        mine_gu[0] = accg[my_rows, :]
        mine_gu[1] = accu[my_rows, :]
        mine_dn[...] = jnp.where(lower, d_v[:, :HALF_D], d_v[:, HALF_D:])
        pair_copy(c, 0).start()
        pair_copy(c, 1).start()

        @pl.when(c == nc - 1)
        def _():
            scatter(nc - 3, wait=True)
            add_and_scatter(nc - 1)
            for cc in (nc - 2, nc - 1):
                pair_copy(cc, 0).wait_send()
                pair_copy(cc, 1).wait_send()
                scatter(cc, wait=True)
            for t, land_ref in enumerate((land_gu_ref, land_dn_ref)):
                for j in range(N_CHIPS - 1):
                    pltpu.make_async_remote_copy(land_ref.at[j], land_ref.at[j], ssend.at[0, 0], rrecv.at[t, j],
                                                 device_id=(x, y, core), device_id_type=MESH).wait_recv()

    def body(kc_ref, dx_ref, x_ref, f_ref, gp_ref, up_ref, pg_ref, wgu_ref, wd_ref, qg_ref,
             dx1_ref, dpg_ref, dqg_ref, own_gu_ref, land_gu_ref, own_dn_ref, land_dn_ref,
             h_s, df_s, dh_s, accg, accu, accd, *comm):
        c = pl.program_id(0)
        i = pl.program_id(1)
        rows = pl.ds(pl.multiple_of(i * tm, tm), tm)
        pg = pg_ref[layer:layer + 1, :]

        @pl.when((c == 0) & (i == 0))
        def _():
            dpg_ref[...] = jnp.zeros_like(dpg_ref)
            dqg_ref[...] = jnp.zeros_like(dqg_ref)

        @pl.when(c == 0)
        def _():
            h_s[rows, :] = _rms_fwd(x_ref[...], pg).astype(BF)
            df, prod = _rms_bwd(f_ref[...], qg_ref[layer:layer + 1, :], dx_ref[...])
            df_s[rows, :] = df.astype(BF)
            dqg_ref[...] += _rowsum(prod)

        @pl.when(i == 0)
        def _():
            accg[...] = jnp.zeros_like(accg)
            accu[...] = jnp.zeros_like(accu)
            accd[...] = jnp.zeros_like(accd)

        h = h_s[rows, :]
        df = df_s[rows, :]
        wg = wgu_ref[0]
        wu = wgu_ref[1]
        g = gp_ref[...].astype(F32)
        u = up_ref[...].astype(F32)
        sg = _sigmoid(g)
        a = g * sg
        dact = _dot_nt(df, wd_ref[...])
        accd[...] += _dot_tn((a * u).astype(BF), df)
        du = (dact * a).astype(BF)
        dg = (dact * u * (sg * (1.0 + g * (1.0 - sg)))).astype(BF)
        accg[...] += _dot_tn(h, dg)
        accu[...] += _dot_tn(h, du)
        dh = _dot_nt(dg, wg) + _dot_nt(du, wu)

        @pl.when(c == 0)
        def _():
            dh_s[rows, :] = dh

        @pl.when((c > 0) & (c < nc - 1))
        def _():
            dh_s[rows, :] += dh

        @pl.when(c == nc - 1)
        def _():
            dxp, prod = _rms_bwd(x_ref[...], pg, dh_s[rows, :] + dh)
            dpg_ref[...] += _rowsum(prod)
            dx1_ref[...] = dx_ref[...] + dxp

        @pl.when(i == n - 1)
        def _():
            exchange(kc_ref, c, accg, accu, accd, own_gu_ref, land_gu_ref, own_dn_ref, land_dn_ref, *comm)

    dma = pltpu.SemaphoreType.DMA
    return _call(
        body, name=f"ffn_bwd{layer}", grid=(nc, n),
        in_specs=[pl.BlockSpec((tm, D), edge_rows), pl.BlockSpec((tm, D), edge_rows),
                  pl.BlockSpec((tm, D), lambda c, i, kc_ref: (jnp.where(c == 0, i, n - 1), 0),
                               pipeline_mode=pl.Buffered(1)),
                  pl.BlockSpec((tm, FF_CHUNK), lambda c, i, kc_ref: (i, chunk_at(c, kc_ref))),
                  pl.BlockSpec((tm, FF_CHUNK), lambda c, i, kc_ref: (i, chunk_at(c, kc_ref))),
                  VSPEC,
                  pl.BlockSpec((2, D, FF_CHUNK), lambda c, i, kc_ref: (0, 0, chunk_at(c, kc_ref))),
                  pl.BlockSpec((FF_CHUNK, D), lambda c, i, kc_ref: (chunk_at(c, kc_ref), 0)),
                  VSPEC],
        out_specs=[pl.BlockSpec((tm, D), lambda c, i, kc_ref: (jnp.where(c == nc - 1, i, 0), 0)),
                   _const_spec((1, D)), _const_spec((1, D)), ANYSPEC, ANYSPEC, ANYSPEC, ANYSPEC],
        out_shape=[_sds((s_len, D)), _sds((1, D)), _sds((1, D)),
                   _sds((n_blk, HALF_D, GU_PIECE), BF), _sds((N_CHIPS - 1, n_blk, HALF_D, GU_PIECE), BF),
                   _sds((DN_SLOT, HALF_D), BF), _sds((N_CHIPS - 1, DN_SLOT, HALF_D), BF)],
        scratch_shapes=[pltpu.VMEM((s_len, D), BF), pltpu.VMEM((s_len, D), BF), pltpu.VMEM((s_len, D), F32),
                        pltpu.VMEM((D, FF_CHUNK), F32), pltpu.VMEM((D, FF_CHUNK), F32),
                        pltpu.VMEM((FF_CHUNK, D), F32),
                        pltpu.VMEM((nc, 2, HALF_D, FF_CHUNK), BF), pltpu.VMEM((nc, FF_CHUNK, HALF_D), BF),
                        pltpu.VMEM((2, 2, HALF_D, FF_CHUNK), BF), pltpu.VMEM((2, FF_CHUNK, HALF_D), BF),
                        pltpu.VMEM((2, HALF_D, FF_CHUNK), F32), pltpu.VMEM((FF_CHUNK, HALF_D), F32),
                        pltpu.VMEM((2, 2, n_gu, HALF_D, GU_PIECE), BF), pltpu.VMEM((2, FF_CHUNK, HALF_D), BF),
                        dma((2, 2)), dma((nc, 2)), dma((2, n_pieces)), dma((2, n_pieces)), dma((2, N_CHIPS - 1))],
        args=[dx2, x1, f, g_pre, u_pre, pre_g, wgu, wd, post_g], rider=rider, prefetch=kc)


def _ple_fwd(layer, x2, p, ple_g, w_gate, w_proj, post_g, qkv=None, rider=None):
    s_len = x2.shape[0]

    def body(*refs):
        if qkv:
            (x_ref, p_ref, g_ref, wg_ref, wp_ref, qg_ref, ng_ref, kg_ref, wq_ref, wkv_ref,
             z_ref, pe_ref, x3_ref, q_ref, kv_ref) = refs
        else:
            x_ref, p_ref, g_ref, wg_ref, wp_ref, qg_ref, z_ref, pe_ref, x3_ref = refs
        z, pe, x3 = _ple_math(layer, x_ref[...], p_ref[...], g_ref, wg_ref, wp_ref, qg_ref)
        z_ref[...] = z
        pe_ref[...] = pe
        x3_ref[...] = x3
        if qkv:
            q_ref[...] = _dot(_rms_fwd(x3, ng_ref[layer + 1:layer + 2, :]).astype(BF), wq_ref[...]).astype(BF)
            kv_ref[...] = _dot(_rms_fwd(x3, kg_ref[...]).astype(BF), wkv_ref[...]).astype(BF)

    p_spec = pl.BlockSpec((None, TM, PLE), lambda i: (layer, i, 0))
    in_specs = [_row_spec(TM), p_spec, VSPEC, VSPEC, VSPEC, VSPEC]
    args = [x2, p, ple_g, w_gate, w_proj, post_g]
    out_specs = [_row_spec(TM), _row_spec(TM), _row_spec(TM)]
    out_shape = [_sds((s_len, D))] * 3
    if qkv:
        in_specs += [VSPEC] * 4
        args += list(qkv)
        out_specs += [_row_spec(TM), _row_spec(TM, 2 * KVD)]
        out_shape += [_sds((s_len, D), BF), _sds((s_len, 2 * KVD), BF)]
    return _call(body, name=f"ple_fwd{layer}", grid=(s_len // TM,), in_specs=in_specs, out_specs=out_specs,
                 out_shape=out_shape, args=args, rider=rider)


def _ple_bwd(layer, dx3, x2, z, pe, p, ple_g, w_gate, post_g, rider=None):
    s_len = x2.shape[0]
    n = s_len // TM

    def body(dx_ref, x_ref, z_ref, pe_ref, p_ref, g_ref, wg_ref, qg_ref,
             dx2_ref, dwg_ref, dwp_ref, dg_ref, dqg_ref, gacc, pacc):
        i = pl.program_id(0)

        @pl.when(i == 0)
        def _():
            gacc[...] = jnp.zeros_like(gacc)
            pacc[...] = jnp.zeros_like(pacc)
            dg_ref[...] = jnp.zeros_like(dg_ref)
            dqg_ref[...] = jnp.zeros_like(dqg_ref)

        dx = dx_ref[...]
        x = x_ref[...]
        pe_v = pe_ref[...]
        gate = _sigmoid(z_ref[...])
        de, prod = _rms_bwd(pe_v * gate, qg_ref[layer:layer + 1, :], dx)
        dqg_ref[...] += _rowsum(prod)
        dpe = (de * gate).astype(BF)
        dz = (de * pe_v * gate * (1.0 - gate)).astype(BF)
        pacc[...] += _dot_tn(p_ref[...].astype(BF), dpe)
        g = g_ref[layer:layer + 1, :]
        r = _rms_fwd(x, g).astype(BF)
        gacc[...] += _dot_tn(r, dz)
        dr = _dot_nt(dz, wg_ref[...])
        dxp, prod2 = _rms_bwd(x, g, dr)
        dg_ref[...] += _rowsum(prod2)
        dx2_ref[...] = dx + dxp

        @pl.when(i == n - 1)
        def _():
            dwg_ref[...] = gacc[...].astype(BF)
            dwp_ref[...] = pacc[...].astype(BF)

    p_spec = pl.BlockSpec((None, TM, PLE), lambda i: (layer, i, 0))
    return _call(
        body, name=f"ple_bwd{layer}", grid=(n,),
        in_specs=[_row_spec(TM), _row_spec(TM), _row_spec(TM), _row_spec(TM), p_spec, VSPEC, VSPEC, VSPEC],
        out_specs=[_row_spec(TM), _const_spec((D, D)), _const_spec((PLE, D)), _const_spec((1, D)), _const_spec((1, D))],
        out_shape=[_sds((s_len, D)), _sds((D, D), BF), _sds((PLE, D), BF), _sds((1, D)), _sds((1, D))],
        scratch_shapes=[pltpu.VMEM((D, D), F32), pltpu.VMEM((PLE, D), F32)],
        args=[dx3, x2, z, pe, p, ple_g, w_gate, post_g], rider=rider)


def _qkv_bwd(dq, dkv, x3, dx4, q_g, kv_g, w_q, w_kv):
    s_len = x3.shape[0]
    n = s_len // TM

    def body(dq_ref, dkv_ref, x_ref, dx_ref, qg_ref, kg_ref, wq_ref, wkv_ref,
             dx3_ref, dwq_ref, dwkv_ref, dqg_ref, dkg_ref, qacc, kacc):
        i = pl.program_id(0)

        @pl.when(i == 0)
        def _():
            qacc[...] = jnp.zeros_like(qacc)
            kacc[...] = jnp.zeros_like(kacc)
            dqg_ref[...] = jnp.zeros_like(dqg_ref)
            dkg_ref[...] = jnp.zeros_like(dkg_ref)

        x = x_ref[...]
        qg = qg_ref[1:2, :]
        kg = kg_ref[...]
        dq_v = dq_ref[...]
        dkv_v = dkv_ref[...].astype(BF)
        qacc[...] += _dot_tn(_rms_fwd(x, qg).astype(BF), dq_v)
        kacc[...] += _dot_tn(_rms_fwd(x, kg).astype(BF), dkv_v)
        dxq, prod_q = _rms_bwd(x, qg, _dot_nt(dq_v, wq_ref[...]))
        dxk, prod_k = _rms_bwd(x, kg, _dot_nt(dkv_v, wkv_ref[...]))
        dqg_ref[...] += _rowsum(prod_q)
        dkg_ref[...] += _rowsum(prod_k)
        dx3_ref[...] = dx_ref[...] + dxq + dxk

        @pl.when(i == n - 1)
        def _():
            dwq_ref[...] = qacc[...].astype(BF)
            dwkv_ref[...] = kacc[...].astype(BF)

    outs, _ = _call(
        body, name="qkv_bwd", grid=(n,),
        in_specs=[_row_spec(TM), _row_spec(TM, 2 * KVD), _row_spec(TM), _row_spec(TM), VSPEC, VSPEC, VSPEC, VSPEC],
        out_specs=[_row_spec(TM), _const_spec((D, D)), _const_spec((D, 2 * KVD)),
                   _const_spec((1, D)), _const_spec((1, D))],
        out_shape=[_sds((s_len, D)), _sds((D, D), BF), _sds((D, 2 * KVD), BF), _sds((1, D)), _sds((1, D))],
        scratch_shapes=[pltpu.VMEM((D, D), F32), pltpu.VMEM((D, 2 * KVD), F32)],
        args=[dq, dkv, x3, dx4, q_g, kv_g, w_q, w_kv])
    return outs


def _attn_group(i, q, kvw, sink_ref, g):
    rows = GQA * BLK
    heads = [GQA * g + j for j in range(GQA)]
    off = jnp.where(i > 0, BLK, 0)
    row = lax.broadcasted_iota(jnp.int32, (rows, 2 * BLK), 0)
    rel = (row % BLK) - lax.broadcasted_iota(jnp.int32, (rows, 2 * BLK), 1) + off
    valid = (rel >= 0) & (rel < BLK)
    head_of_row = lax.broadcasted_iota(jnp.int32, (rows, 1), 0) // BLK
    slope = jnp.zeros((rows, 1), F32)
    sink = jnp.zeros((rows, 1), F32)
    for j, h in enumerate(heads):
        slope = jnp.where(head_of_row == j, SLOPES[h], slope)
        sink = jnp.where(head_of_row == j, sink_ref[0, h], sink)
    qs = jnp.concatenate([q[:, h * HEAD_DIM:(h + 1) * HEAD_DIM] for h in heads], axis=0)
    k = kvw[:, g * HEAD_DIM:(g + 1) * HEAD_DIM]
    v = kvw[:, KVD + g * HEAD_DIM:KVD + (g + 1) * HEAD_DIM]
    s = _dot_nt(qs, k) * ATT_SCALE - slope * rel.astype(F32)
    s = jnp.where(valid, s, NEG_INF)
    m = jnp.maximum(jnp.max(s, axis=-1, keepdims=True), sink)
    e = jnp.exp(s - m)
    es = jnp.exp(sink - m)
    inv = 1.0 / (jnp.sum(e, axis=-1, keepdims=True) + es)
    return e * inv, es * inv, qs, k, v


def _unstack_heads(stacked):
    return [stacked[j * BLK:(j + 1) * BLK, :] for j in range(GQA)]


def _kv_window(kv_ref, i):
    ks = pl.multiple_of(jnp.maximum(i * BLK - BLK, 0), BLK)
    return ks, kv_ref[pl.ds(ks, 2 * BLK), :]


def _attn_fwd(q, kv, sinks, x3, w_o, post_g, rider=None):
    s_len = q.shape[0]

    def body(q_ref, kv_ref, sk_ref, x_ref, wo_ref, g_ref, a_ref, y_ref, x4_ref):
        i = pl.program_id(0)
        _, kvw = _kv_window(kv_ref, i)
        q = q_ref[...]
        outs = []
        for g in range(N_KV_HEADS):
            p, _, _, _, v = _attn_group(i, q, kvw, sk_ref, g)
            outs += _unstack_heads(_dot(p.astype(BF), v))
        attn = jnp.concatenate(outs, axis=1)
        a_ref[...] = attn
        y = _dot(attn.astype(BF), wo_ref[...])
        y_ref[...] = y
        x4_ref[...] = x_ref[...] + _rms_fwd(y, g_ref[1:2, :])

    return _call(body, name="attn_fwd", grid=(s_len // BLK,),
                 in_specs=[_row_spec(BLK), VSPEC, SSPEC, _row_spec(BLK), VSPEC, VSPEC],
                 out_specs=[_row_spec(BLK)] * 3, out_shape=[_sds((s_len, D))] * 3,
                 args=[q, kv, sinks, x3, w_o, post_g], rider=rider)


ATT_STEP_BLOCKS = 2


def _attn_bwd(dx4, y, attn, q, kv, sinks, w_o, post_g, rider=None):
    s_len = q.shape[0]
    rows = ATT_STEP_BLOCKS * BLK
    n = s_len // rows

    def body(dx_ref, y_ref, a_ref, q_ref, kv_ref, sk_ref, wo_ref, g_ref,
             dq_ref, dkv_ref, dwo_ref, dg_ref, dsk_ref, wacc):
        i = pl.program_id(0)

        @pl.when(i == 0)
        def _():
            dkv_ref[...] = jnp.zeros_like(dkv_ref)
            wacc[...] = jnp.zeros_like(wacc)
            dg_ref[...] = jnp.zeros_like(dg_ref)
            dsk_ref[...] = jnp.zeros_like(dsk_ref)

        dy, prod = _rms_bwd(y_ref[...], g_ref[1:2, :], dx_ref[...])
        dg_ref[...] += _rowsum(prod)
        dyb = dy.astype(BF)
        attn_all = a_ref[...]
        wacc[...] += _dot_tn(attn_all.astype(BF), dyb)
        d_o_all = _dot_nt(dyb, wo_ref[...])
        q_all = q_ref[...]
        lane = lax.broadcasted_iota(jnp.int32, (1, D), 1)
        dsk = jnp.zeros((1, D), F32)
        for sub in range(ATT_STEP_BLOCKS):
            blk = i * ATT_STEP_BLOCKS + sub
            sl = slice(sub * BLK, (sub + 1) * BLK)
            d_o, q = d_o_all[sl, :], q_all[sl, :]
            dod = d_o * attn_all[sl, :]
            ks, kvw = _kv_window(kv_ref, blk)
            dqs, dks, dvs = [], [], []
            for g in range(N_KV_HEADS):
                p, ps, qs, k, v = _attn_group(blk, q, kvw, sk_ref, g)
                cols = [slice((GQA * g + j) * HEAD_DIM, (GQA * g + j + 1) * HEAD_DIM) for j in range(GQA)]
                do_s = jnp.concatenate([d_o[:, c] for c in cols], axis=0).astype(BF)
                dsum = jnp.concatenate([jnp.sum(dod[:, c], axis=-1, keepdims=True) for c in cols], axis=0)
                dp = _dot_nt(do_s, v)
                dsb = (p * (dp - dsum) * ATT_SCALE).astype(BF)
                sink_part = ps * dsum
                for j in range(GQA):
                    dsk = dsk + jnp.where(lane == GQA * g + j, -_rowsum(sink_part[j * BLK:(j + 1) * BLK, :]), 0.0)
                dqs += _unstack_heads(_dot(dsb, k))
                dks.append(_dot_tn(dsb, qs))
                dvs.append(_dot_tn(p.astype(BF), do_s))
            dq_ref[sl, :] = jnp.concatenate(dqs, axis=1).astype(BF)
            dkv_ref[pl.ds(ks, 2 * BLK), :] += jnp.concatenate(dks + dvs, axis=1)
        dsk_ref[...] += dsk

        @pl.when(i == n - 1)
        def _():
            dwo_ref[...] = wacc[...].astype(BF)

    return _call(
        body, name="attn_bwd", grid=(n,),
        in_specs=[_row_spec(rows), _row_spec(rows), _row_spec(rows), _row_spec(rows), VSPEC, SSPEC, VSPEC, VSPEC],
        out_specs=[_row_spec(rows), _const_spec((s_len, 2 * KVD)), _const_spec((D, D)),
                   _const_spec((1, D)), _const_spec((1, D))],
        out_shape=[_sds((s_len, D), BF), _sds((s_len, 2 * KVD)), _sds((D, D), BF), _sds((1, D)), _sds((1, D))],
        scratch_shapes=[pltpu.VMEM((D, D), F32)],
        args=[dx4, y, attn, q, kv, sinks, w_o, post_g], rider=rider)


Big = collections.namedtuple("Big", "name src layer L A R C rb")


def _bigs():
    out = {"pool_w": Big("pool_w", "pool_w", None, 4, 4, POOL_G // N_CHIPS, POOL_G, 32)}
    for l in range(2):
        out[f"w_gu{l}"] = Big(f"w_gu{l}", "w_gu", l, 1, 2, D, FF_HALF, 256)
        out[f"w_down{l}"] = Big(f"w_down{l}", "w_down", l, 1, 4, FF // N_CHIPS, D, 352)
        out[f"w_ple_gate{l}"] = Big(f"w_ple_gate{l}", "w_ple_gate", l, 1, 4, D // N_CHIPS, D, 128)
        out[f"w_ple_proj{l}"] = Big(f"w_ple_proj{l}", "w_ple_proj", l, 1, 1, PLE, D // N_CHIPS, 128)
    out["w_q"] = Big("w_q", "w_q", None, 1, 4, D // N_CHIPS, D, 128)
    out["w_o"] = Big("w_o", "w_o", None, 1, 4, D // N_CHIPS, D, 128)
    out["w_kv"] = Big("w_kv", "w_kv", None, 1, 4, D // N_CHIPS, 2 * KVD, 128)
    return out


BIGS = _bigs()
POOL_SCALE = Big("pool_scale", "pool_scale", None, 1, 1, 1, D // N_CHIPS, 1)
BIG_SOURCES = ("w_gu", "w_down", "w_ple_gate", "w_ple_proj", "w_q", "w_o", "w_kv", "pool_w")


def _ncb(t):
    return N_CHIPS // t.A


def _full_shape(t, rows=None):
    return (t.L, t.A, t.R if rows is None else rows, _ncb(t) * t.C)


def _slot_index(t, k):
    return k // _ncb(t), k % _ncb(t)


def _slot(ref, t, k, row0, rows):
    a, cb = _slot_index(t, k)
    return ref.at[:, a, pl.ds(row0, rows), pl.ds(pl.multiple_of(cb * t.C, 128), t.C)]


def _place_job(t, w, out_dtype=BF):
    nb = next((nb for nb in (8, 4, 2, 1) if t.R % (16 * nb) == 0), 1) if t.L == 1 else 1
    rb = t.R // nb

    def fn(j, kc_ref, ins, outs):
        outs[0][...] = ins[0][...].astype(out_dtype)

    def in_map(j, kc_ref):
        return (j // nb if t.layer is None else t.layer, j % nb, 0)

    def out_map(j, kc_ref):
        a, cb = _slot_index(t, kc_ref[0])
        return (j // nb, a, j % nb, cb)

    return Job(t.L * nb, [(w, (None, rb, t.C), in_map)],
               [(_sds(_full_shape(t), out_dtype), (None, None, rb, t.C), out_map)], fn)


def _mesh_position():
    x, y, c = lax.axis_index("x"), lax.axis_index("y"), lax.axis_index("c")
    chips = [(1 - x, y), (x, 1 - y), (1 - x, 1 - y)]
    return x, y, c, chips


DIRECT_BELOW = 1024


def _gather_rider(parts, fulls):
    nt = len(parts)
    TO_X, TO_Y, FWD_X, FWD_Y, SIB_X, SIB_Y, SIB_D = range(7)

    def rows_of(ti, core):
        t, r0, r1 = parts[ti]
        h = (r1 - r0) // 2
        return r0 + core * h, h

    def copy(outs, sems, kind, ti, k_src, row0, rows, dev):
        region = _slot(outs[ti], parts[ti][0], k_src, row0, rows)
        return pltpu.make_async_remote_copy(region, region, sems[0].at[ti, kind], sems[1].at[ti, kind],
                                            device_id=dev, device_id_type=MESH)

    def plan(outs, sems):
        x, y, c, _ = _mesh_position()
        me, kx, ky, kd = 2 * x + y, 2 * (1 - x) + y, 2 * x + (1 - y), 2 * (1 - x) + (1 - y)
        dev_x, dev_y, dev_d, sib = (1 - x, y, c), (x, 1 - y, c), (1 - x, 1 - y, c), (x, y, 1 - c)

        def whole(ti):
            return 0, parts[ti][0].R

        def mk(kind, k_send, k_recv, dev, send_rows, recv_rows):
            def build(ti, side):
                k_src = k_send if side == "s" else k_recv
                row0, rows = (send_rows if side == "s" else recv_rows)(ti)
                return copy(outs, sems, kind, ti, k_src, row0, rows, dev)
            return build

        def first_half(core):
            return lambda ti: (rows_of(ti, core)[0], rows_of(ti, core)[1] // 2)

        def second_half(core):
            return lambda ti: (rows_of(ti, core)[0] + rows_of(ti, core)[1] // 2, rows_of(ti, core)[1] // 2)

        mine = lambda ti: rows_of(ti, c)
        theirs = lambda ti: rows_of(ti, 1 - c)
        split = {
            TO_X: mk(TO_X, me, kx, dev_x, mine, mine),
            TO_Y: mk(TO_Y, me, ky, dev_y, mine, mine),
            FWD_X: mk(FWD_X, ky, kd, dev_x, first_half(c), first_half(c)),
            FWD_Y: mk(FWD_Y, kx, kd, dev_y, second_half(c), second_half(c)),
            SIB_X: mk(SIB_X, kx, kx, sib, mine, theirs),
            SIB_Y: mk(SIB_Y, ky, ky, sib, mine, theirs),
            SIB_D: mk(SIB_D, kd, kd, sib, mine, theirs),
        }
        direct = {
            TO_X: mk(TO_X, me, kx, dev_x, whole, whole),
            TO_Y: mk(TO_Y, me, ky, dev_y, whole, whole),
            FWD_X: mk(FWD_X, me, kd, dev_d, whole, whole),
        }
        return split, direct

    is_split = [t.L * t.R * t.C >= DIRECT_BELOW for t, _, _ in parts]
    assert all(s or (r0, r1) == (0, t.R) for s, (t, r0, r1) in zip(is_split, parts))

    def start(ins, outs, sems):
        split, direct = plan(outs, sems)
        for ti in range(nt):
            kinds = split if is_split[ti] else direct
            kinds[TO_X](ti, "s").start()
            kinds[TO_Y](ti, "s").start()
            if not is_split[ti]:
                kinds[FWD_X](ti, "s").start()

    def mid(ins, outs, sems):
        split, _ = plan(outs, sems)
        for ti in range(nt):
            if is_split[ti]:
                split[TO_Y](ti, "r").wait_recv()
                split[FWD_X](ti, "s").start()
                split[SIB_Y](ti, "s").start()
        for ti in range(nt):
            if is_split[ti]:
                split[TO_X](ti, "r").wait_recv()
                split[FWD_Y](ti, "s").start()
                split[SIB_X](ti, "s").start()

    def finish(ins, outs, sems):
        split, direct = plan(outs, sems)
        for ti in range(nt):
            if is_split[ti]:
                split[FWD_X](ti, "r").wait_recv()
                split[FWD_Y](ti, "r").wait_recv()
                split[SIB_D](ti, "s").start()
            else:
                for kind in (TO_X, TO_Y, FWD_X):
                    direct[kind](ti, "r").wait_recv()
        for ti in range(nt):
            if is_split[ti]:
                for kind in (SIB_X, SIB_Y, SIB_D):
                    split[kind](ti, "r").wait_recv()
        for ti in range(nt):
            kinds = split if is_split[ti] else direct
            for kind in kinds:
                kinds[kind](ti, "s").wait_send()

    sems = pltpu.SemaphoreType.DMA((nt, 7))
    return Rider(list(fulls), [_sds(a.shape, a.dtype) for a in fulls], {i: i for i in range(nt)},
                 [sems, sems], start, mid, finish)


def _pair_exchange_rider(specs, grads):
    nt = len(specs)

    def copy(ins, outs, sems, ti, c, sibling):
        half = specs[ti].R // 2
        return pltpu.make_async_remote_copy(ins[ti].at[:, :, pl.ds((1 - c) * half, half), :], outs[ti],
                                            sems[0].at[ti], sems[1].at[ti], device_id=sibling, device_id_type=MESH)

    def start(ins, outs, sems):
        x, y, c, _ = _mesh_position()
        for ti in range(nt):
            copy(ins, outs, sems, ti, c, (x, y, 1 - c)).start()

    def finish(ins, outs, sems):
        x, y, c, _ = _mesh_position()
        for ti in range(nt):
            copy(ins, outs, sems, ti, c, (x, y, 1 - c)).wait()

    sems = pltpu.SemaphoreType.DMA((nt,))
    return Rider(list(grads), [_sds(_full_shape(t, t.R // 2), BF) for t in specs], {}, [sems, sems], start, None, finish)


def _pair_sum_job(t, g, land):
    assert t.L == 1
    half = t.R // 2
    nj = half // t.rb
    block = (None, t.A, t.rb, _ncb(t) * t.C)

    def fn(j, kc_ref, ins, outs):
        outs[0][...] = (ins[0][...].astype(F32) + ins[1][...].astype(F32)).astype(BF)

    return Job(nj,
               [(g, block, lambda j, kc_ref: (0, 0, kc_ref[1] * nj + j, 0)),
                (land, block, lambda j, kc_ref: (0, 0, j, 0))],
               [(_sds(_full_shape(t, half), BF), block, lambda j, kc_ref: (0, 0, j, 0))], fn)


def _scatter_rider(specs, sums):
    nt = len(specs)

    def copy(ins, outs, sems, ti, j, chip, c):
        t = specs[ti]
        cx, cy = chip
        return pltpu.make_async_remote_copy(_slot(ins[ti], t, 2 * cx + cy, 0, t.R // 2), outs[ti].at[j],
                                            sems[0].at[ti, j], sems[1].at[ti, j],
                                            device_id=(cx, cy, c), device_id_type=MESH)

    def start(ins, outs, sems):
        _, _, c, chips = _mesh_position()
        for j, chip in enumerate(chips):
            for ti in range(nt):
                copy(ins, outs, sems, ti, j, chip, c).start()

    def finish(ins, outs, sems):
        _, _, c, chips = _mesh_position()
        for j, chip in enumerate(chips):
            for ti in range(nt):
                copy(ins, outs, sems, ti, j, chip, c).wait()

    sems = pltpu.SemaphoreType.DMA((nt, N_CHIPS - 1))
    return Rider(list(sums), [_sds((N_CHIPS - 1, t.L, t.R // 2, t.C), BF) for t in specs], {}, [sems, sems],
                 start, None, finish)


def _chip_sum_job(ts, landed):
    t0 = ts[0]
    assert t0.L == 1
    half = t0.R // 2
    nj = half // t0.rb

    def local(j, li):
        return jnp.clip(j - li * nj, 0, nj - 1)

    ins = []
    for li, t in enumerate(ts):
        s, land = landed[t.name]

        def own_map(j, kc_ref, li=li, t=t):
            a, cb = _slot_index(t, kc_ref[0])
            return (0, a, local(j, li), cb)

        ins.append((s, (None, None, t.rb, t.C), own_map))
        ins.append((land, (N_CHIPS - 1, None, t.rb, t.C), lambda j, kc_ref, li=li: (0, 0, local(j, li), 0)))

    def fn(j, kc_ref, in_refs, outs):
        for li in range(len(ts)):
            @pl.when(j // nj == li)
            def _():
                acc = in_refs[2 * li][...].astype(F32)
                for k in range(N_CHIPS - 1):
                    acc = acc + in_refs[2 * li + 1][k].astype(F32)
                outs[0][...] = acc

    return Job(len(ts) * nj, ins,
               [(_sds((len(ts), t0.R, t0.C)), (None, t0.rb, t0.C),
                 lambda j, kc_ref: (j // nj, kc_ref[1] * nj + j % nj, 0))], fn)


def _adamw_job(rb, w, g, m, v):
    n_layers, r, c = w.shape
    nb = r // rb
    block = (None, rb, c)
    index = lambda j, kc_ref: (j // nb, j % nb, 0)

    def fn(j, kc_ref, ins, outs):
        g_v = ins[1][...]
        outs[0][...] = g_v
        outs[1][...], outs[2][...], outs[3][...] = _adamw_math(ins[0][...], g_v, ins[2][...], ins[3][...])

    return Job(n_layers * nb, [(a, block, index) for a in (w, g, m, v)],
               [(_sds(w.shape), block, index)] * 4, fn)


def _chip_sum_fused_job(ts, fused, by_cols):
    t0 = ts[0]
    own0 = fused[t0.name][0]
    if by_cols:
        rows, cols = own0.shape
    else:
        nb, rows, bw = own0.shape
        cols = nb * bw
    nj = rows // t0.rb

    def local(j, li):
        return jnp.clip(j - li * nj, 0, nj - 1)

    ins = []
    for li, t in enumerate(ts):
        own, land = fused[t.name]
        if by_cols:
            ins.append((own, (t.rb, cols), lambda j, kc_ref, li=li: (local(j, li), 0)))
            ins.append((land, (N_CHIPS - 1, t.rb, cols), lambda j, kc_ref, li=li: (0, local(j, li), 0)))
        else:
            ins.append((own, (nb, t.rb, bw), lambda j, kc_ref, li=li: (0, local(j, li), 0)))
            ins.append((land, (N_CHIPS - 1, nb, t.rb, bw), lambda j, kc_ref, li=li: (0, 0, local(j, li), 0)))

    def fn(j, kc_ref, in_refs, outs):
        for li in range(len(ts)):
            @pl.when(j // nj == li)
            def _():
                acc = in_refs[2 * li][...].astype(F32)
                for k in range(N_CHIPS - 1):
                    acc = acc + in_refs[2 * li + 1][k].astype(F32)
                outs[0][...] = acc if by_cols else jnp.concatenate([acc[b] for b in range(nb)], axis=1)

    def out_map(j, kc_ref):
        return (j // nj, j % nj, kc_ref[1]) if by_cols else (j // nj, kc_ref[1] * nj + j % nj, 0)

    return Job(len(ts) * nj, ins, [(_sds((len(ts), t0.R, t0.C)), (None, t0.rb, cols), out_map)], fn)


def _share_rider(halves, by_cols):
    nt = len(halves)

    def copy(outs, sems, ti, core, sibling):
        axis = 2 if by_cols[ti] else 1
        half = halves[ti].shape[axis] // 2
        piece = pl.ds(pl.multiple_of(core * half, 128 if by_cols[ti] else 8), half)
        part = outs[ti].at[:, :, piece] if by_cols[ti] else outs[ti].at[:, piece, :]
        return pltpu.make_async_remote_copy(part, part, sems[0].at[ti], sems[1].at[ti],
                                            device_id=sibling, device_id_type=MESH)

    def start(ins, outs, sems):
        x, y, c, _ = _mesh_position()
        for ti in range(nt):
            copy(outs, sems, ti, c, (x, y, 1 - c)).start()

    def finish(ins, outs, sems):
        x, y, c, _ = _mesh_position()
        for ti in range(nt):
            copy(outs, sems, ti, 1 - c, (x, y, 1 - c)).wait_recv()
        for ti in range(nt):
            copy(outs, sems, ti, c, (x, y, 1 - c)).wait_send()

    sems = pltpu.SemaphoreType.DMA((nt,))
    return Rider(list(halves), [_sds(a.shape, a.dtype) for a in halves], {i: i for i in range(nt)}, [sems, sems],
                 start, None, finish)


def _both(r1, r2):
    assert r1.mid is None and r2.mid is None
    ni, no, ns = len(r1.arrays), len(r1.out_shapes), len(r1.scratch)

    def split(fn1, fn2):
        def run(ins, outs, scr):
            fn1(ins[:ni], outs[:no], scr[:ns])
            fn2(ins[ni:], outs[no:], scr[ns:])
        return run

    aliases = dict(r1.aliases)
    aliases.update({ni + a: no + b for a, b in r2.aliases.items()})
    return Rider(r1.arrays + r2.arrays, r1.out_shapes + r2.out_shapes, aliases, r1.scratch + r2.scratch,
                 split(r1.start, r2.start), None, split(r1.finish, r2.finish))


def _adamw_math(w, g, m, v):
    m = B1 * m + (1.0 - B1) * g
    v = B2 * v + (1.0 - B2) * (g * g)
    delta = -LR * ((m / BC1) / (jnp.sqrt(v / BC2) + AEPS) + WD * w)
    return delta, m, v


GAIN_ROWS = {"pre_mix_g": 0, "post_mix_g": 2, "pre_ffn_g": 4, "post_ffn_g": 6, "ple_g": 8, "ple_post_g": 10}
ROW_KV_G, ROW_POOL_SCALE, ROW_SINKS, ROW_LOSS, PACK_ROWS = 12, 13, 14, 15, 16
SMALL_NAMES = tuple(GAIN_ROWS) + ("kv_g", "pool_scale", "sinks")


def _small_all_reduce(rows, dpool, rider=None):
    ng, pr = len(WINDOWS), POOL_G // N_CHIPS

    def body(*refs):
        row_refs = refs[:PACK_ROWS]
        dpool_ref, tot_ref, gpool_ref, pack, land, pland, send, recv, psend, precv = refs[PACK_ROWS:]
        x, y, c, _ = _mesh_position()
        me = 4 * x + 2 * y + c
        for r in range(PACK_ROWS):
            pack[r:r + 1, :] = row_refs[r][...]

        def shard_of(k):
            return dpool_ref.at[:, pl.ds(pl.multiple_of(k * pr, pr), pr), :]

        cps = []
        for j in range(1, N_DEV):
            px, py, pc = x ^ (j >> 2), y ^ ((j >> 1) & 1), c ^ (j & 1)
            cps.append(pltpu.make_async_remote_copy(pack, land.at[me], send.at[j], recv.at[j],
                                                    device_id=(px, py, pc), device_id_type=MESH))
            cps.append(pltpu.make_async_remote_copy(shard_of(2 * px + py), pland.at[me], psend.at[j], precv.at[j],
                                                    device_id=(px, py, pc), device_id_type=MESH))
        for cp in cps:
            cp.start()
        land[me] = pack[...]
        pland[me] = dpool_ref[:, pl.ds(pl.multiple_of((2 * x + y) * pr, pr), pr), :]
        for j in range(1, N_DEV):
            pltpu.make_async_remote_copy(pack, land.at[me ^ j], send.at[j], recv.at[j],
                                         device_id=(x, y, c), device_id_type=MESH).wait_recv()
            pltpu.make_async_remote_copy(shard_of(0), pland.at[me ^ j], psend.at[j], precv.at[j],
                                         device_id=(x, y, c), device_id_type=MESH).wait_recv()
        for cp in cps:
            cp.wait_send()
        tot = land[0]
        gp = pland[0].astype(F32)
        for d in range(1, N_DEV):
            tot = tot + land[d]
            gp = gp + pland[d].astype(F32)
        tot_ref[...] = tot
        gpool_ref[...] = gp

    sems = pltpu.SemaphoreType.DMA((N_DEV,))
    return _call(
        body, name="small_all_reduce", grid=(1,),
        in_specs=[VSPEC] * (PACK_ROWS + 1), out_specs=[VSPEC, VSPEC],
        out_shape=[_sds((PACK_ROWS, D)), _sds((ng, pr, POOL_G))],
        scratch_shapes=[pltpu.VMEM((PACK_ROWS, D), F32), pltpu.VMEM((N_DEV, PACK_ROWS, D), F32),
                        pltpu.VMEM((N_DEV, ng, pr, POOL_G), BF), sems, sems, sems, sems],
        args=[*rows, dpool], rider=rider)


def _small_adamw(tot, kc, small_w, small_m, small_v):
    names = SMALL_NAMES
    n = len(names)

    def body(*refs):
        tot_ref, kc_ref = refs[0], refs[1]
        w_refs = dict(zip(names, refs[2:2 + n]))
        m_refs = dict(zip(names, refs[2 + n:2 + 2 * n]))
        v_refs = dict(zip(names, refs[2 + 2 * n:2 + 3 * n]))
        loss_ref = refs[2 + 3 * n]
        out_refs = {nm: refs[3 + 3 * n + 4 * k: 7 + 3 * n + 4 * k] for k, nm in enumerate(names)}
        tot = tot_ref[...]
        loss_ref[...] = 0.5 * jnp.sum(tot[ROW_LOSS:ROW_LOSS + 1, :], axis=-1, keepdims=True) * (1.0 / D)

        def update(nm, g):
            g_ref, d_ref, nm_ref, nv_ref = out_refs[nm]
            g_ref[...] = g
            d_ref[...], nm_ref[...], nv_ref[...] = _adamw_math(w_refs[nm][...], g, m_refs[nm][...], v_refs[nm][...])

        for nm, r in GAIN_ROWS.items():
            update(nm, tot[r:r + 2, :])
        update("kv_g", tot[ROW_KV_G:ROW_KV_G + 1, :])
        k = kc_ref[0]
        width = D // N_CHIPS
        g_scale = jnp.zeros((1, width), F32)
        for kk in range(N_CHIPS):
            g_scale = g_scale + jnp.where(k == kk, tot[ROW_POOL_SCALE:ROW_POOL_SCALE + 1, kk * width:(kk + 1) * width], 0.0)
        update("pool_scale", g_scale)
        update("sinks", tot[ROW_SINKS:ROW_SINKS + 1, 0:N_HEADS])

    ins = [tot, kc] + [small_w[nm] for nm in names] + [small_m[nm] for nm in names] + [small_v[nm] for nm in names]
    out_shape = [_sds((1, 1))]
    for nm in names:
        out_shape += [_sds(small_w[nm].shape)] * 4
    outs = pl.pallas_call(
        body, name="small_adamw",
        in_specs=[VSPEC, SSPEC] + [VSPEC] * (3 * n), out_specs=[VSPEC] * len(out_shape), out_shape=out_shape,
        compiler_params=_params(),
    )(*ins)
    return outs[0], {nm: outs[1 + 4 * k: 5 + 4 * k] for k, nm in enumerate(names)}


def _compute_layout(t, full):
    if t.src == "w_gu":
        return full.reshape(2, D, FF)
    if t.src == "pool_w":
        return full.reshape(len(WINDOWS), POOL_G, POOL_G)
    if t.src == "pool_scale":
        return full.reshape(1, D)
    return full.reshape(t.A * t.R, _ncb(t) * t.C)


def kernel(x, p, pre_mix_g, post_mix_g, pre_ffn_g, post_ffn_g, pool_w, pool_scale, kv_g, w_kv, w_q, sinks, w_o, w_gu, w_down, ple_g, w_ple_gate, w_ple_proj, ple_post_g, loss_target, m_pre_mix_g, m_post_mix_g, m_pre_ffn_g, m_post_ffn_g, m_pool_w, m_pool_scale, m_kv_g, m_w_kv, m_w_q, m_sinks, m_w_o, m_w_gu, m_w_down, m_ple_g, m_w_ple_gate, m_w_ple_proj, m_ple_post_g, v_pre_mix_g, v_post_mix_g, v_pre_ffn_g, v_post_ffn_g, v_pool_w, v_pool_scale, v_kv_g, v_w_kv, v_w_q, v_sinks, v_w_o, v_w_gu, v_w_down, v_ple_g, v_w_ple_gate, v_w_ple_proj, v_ple_post_g):
    weights = dict(pre_mix_g=pre_mix_g, post_mix_g=post_mix_g, pre_ffn_g=pre_ffn_g, post_ffn_g=post_ffn_g,
                   pool_w=pool_w, pool_scale=pool_scale, kv_g=kv_g, w_kv=w_kv, w_q=w_q, sinks=sinks, w_o=w_o,
                   w_gu=w_gu, w_down=w_down, ple_g=ple_g, w_ple_gate=w_ple_gate, w_ple_proj=w_ple_proj,
                   ple_post_g=ple_post_g)
    m_in = dict(pre_mix_g=m_pre_mix_g, post_mix_g=m_post_mix_g, pre_ffn_g=m_pre_ffn_g, post_ffn_g=m_post_ffn_g,
                pool_w=m_pool_w, pool_scale=m_pool_scale, kv_g=m_kv_g, w_kv=m_w_kv, w_q=m_w_q, sinks=m_sinks,
                w_o=m_w_o, w_gu=m_w_gu, w_down=m_w_down, ple_g=m_ple_g, w_ple_gate=m_w_ple_gate,
                w_ple_proj=m_w_ple_proj, ple_post_g=m_ple_post_g)
    v_in = dict(pre_mix_g=v_pre_mix_g, post_mix_g=v_post_mix_g, pre_ffn_g=v_pre_ffn_g, post_ffn_g=v_post_ffn_g,
                pool_w=v_pool_w, pool_scale=v_pool_scale, kv_g=v_kv_g, w_kv=v_w_kv, w_q=v_w_q, sinks=v_sinks,
                w_o=v_w_o, w_gu=v_w_gu, w_down=v_w_down, ple_g=v_ple_g, w_ple_gate=v_w_ple_gate,
                w_ple_proj=v_w_ple_proj, ple_post_g=v_ple_post_g)
    order = ["pre_mix_g", "post_mix_g", "pre_ffn_g", "post_ffn_g", "pool_w", "pool_scale", "kv_g", "w_kv", "w_q",
             "sinks", "w_o", "w_gu", "w_down", "ple_g", "w_ple_gate", "w_ple_proj", "ple_post_g"]

    kc = jnp.stack([2 * lax.axis_index("x") + lax.axis_index("y"), lax.axis_index("c")]).astype(jnp.int32)
    s_len = x.shape[1]
    x2d = x.reshape(s_len, D)
    p3d = p.reshape(2, s_len, PLE)
    target = loss_target.reshape(s_len, D)
    kv_g2d = kv_g.reshape(1, D)
    gains = {nm: weights[nm] for nm in GAIN_ROWS}

    def shard_view(src, a):
        t = next(t for t in BIGS.values() if t.src == src)
        return a.reshape(-1, t.R, t.C)

    first, second = ["pool_w", "pool_scale"], ["w_gu0", "w_down0"]
    rest = [nm for nm in BIGS if nm not in first + second]
    specs = dict(BIGS, pool_scale=POOL_SCALE)
    placed = {}

    def place_job(nm):
        if nm == "pool_scale":
            return _place_job(POOL_SCALE, pool_scale.reshape(1, 1, D // N_CHIPS), F32)
        return _place_job(BIGS[nm], shard_view(BIGS[nm].src, weights[BIGS[nm].src]))

    def gather(names, rows=None):
        rows = rows or {}
        parts = [(specs[nm],) + tuple(rows.get(nm, (0, specs[nm].R))) for nm in names]
        return _gather_rider(parts, [placed[nm] for nm in names])

    def take(names, results):
        for nm, a in zip(names, results):
            placed[nm] = a

    def weight(nm):
        return _compute_layout(specs[nm], placed[nm])

    take(first, [r[0] for r in _multi_call("place_pool", [place_job(nm) for nm in first], kc)])
    cast, got = _multi_call("place_ffn0", [place_job(nm) for nm in second], kc, rider=gather(first))
    take(second, [r[0] for r in cast])
    take(first, got)
    jobs = [place_job(nm) for nm in rest]
    jobs.append(_mixa_fwd_job(x2d, gains["pre_mix_g"], weight("pool_w"), weight("pool_scale"), gains["post_mix_g"]))
    results, got = _multi_call("cast_and_mixa_fwd", jobs, kc, rider=gather(second))
    take(rest, [r[0] for r in results[:-1]])
    take(second, got)
    y0, x1 = results[-1]

    ride = ["w_ple_gate0", "w_ple_proj0", "w_q", "w_kv", "w_o", "w_gu1"]
    (f0, x2, g0, u0), got = _ffn_fwd(0, x1, gains["pre_ffn_g"], weight("w_gu0"), weight("w_down0"), gains["post_ffn_g"],
                             rider=gather(ride, {"w_gu1": (0, 320)}))
    take(ride, got)

    ride = ["w_ple_gate1", "w_ple_proj1", "w_gu1"]
    (z0, pe0, x3, q, kv), got = _ple_fwd(
        0, x2, p3d, gains["ple_g"], weight("w_ple_gate0"), weight("w_ple_proj0"), gains["ple_post_g"],
        qkv=(gains["pre_mix_g"], kv_g2d, weight("w_q"), weight("w_kv")),
        rider=gather(ride, {"w_gu1": (320, 704)}))
    take(ride, got)

    ride = ["w_down1", "w_gu1"]
    (attn, y1, x4), got = _attn_fwd(q, kv, sinks, x3, weight("w_o"), gains["post_mix_g"],
                                    rider=gather(ride, {"w_gu1": (704, D)}))
    take(ride, got)

    (f1, x5, g1, u1, z1, pe1, dx6, loss_row), _ = _ffn_fwd(
        1, x4, gains["pre_ffn_g"], weight("w_gu1"), weight("w_down1"), gains["post_ffn_g"],
        head=(p3d, gains["ple_g"], weight("w_ple_gate1"), weight("w_ple_proj1"), gains["ple_post_g"], target))

    local = {}
    landed = {}
    fused = {}

    def local_grads(names):
        return [local[nm].reshape(_full_shape(BIGS[nm])) for nm in names]

    def pair_exchange(names):
        return _pair_exchange_rider([BIGS[nm] for nm in names], local_grads(names))

    def pair_sum(tag, names, lands):
        jobs = [_pair_sum_job(BIGS[nm], g, l) for nm, g, l in zip(names, local_grads(names), lands)]
        return [r[0] for r in _multi_call(f"pair_sum_{tag}", jobs, kc)]

    def scatter(names, sums):
        return _scatter_rider([BIGS[nm] for nm in names], sums)

    def keep(names, sums, got):
        for nm, s, l in zip(names, sums, got):
            landed[nm] = (s, l)

    (dx5, local["w_ple_gate1"], local["w_ple_proj1"], d_ple1, d_plepost1), _ = _ple_bwd(
        1, dx6, x5, z1, pe1, p3d, gains["ple_g"], weight("w_ple_gate1"), gains["ple_post_g"])

    group_a = ["w_ple_gate1", "w_ple_proj1"]
    (dx4, d_preffn1, d_postffn1, *scattered), lands_a = _ffn_bwd(
        1, dx5, x4, f1, g1, u1, gains["pre_ffn_g"], weight("w_gu1"), weight("w_down1"), gains["post_ffn_g"], kc,
        rider=pair_exchange(group_a))
    fused["w_gu1"], fused["w_down1"] = scattered[0:2], scattered[2:4]

    (dq, dkv, local["w_o"], d_postmix1, d_sinks), _ = _attn_bwd(
        dx4, y1, attn, q, kv, sinks, weight("w_o"), gains["post_mix_g"])
    dx3, local["w_q"], local["w_kv"], d_premix1, d_kvg = _qkv_bwd(
        dq, dkv, x3, dx4, gains["pre_mix_g"], kv_g2d, weight("w_q"), weight("w_kv"))

    group_b = ["w_o", "w_q", "w_kv"]
    (dx2, local["w_ple_gate0"], local["w_ple_proj0"], d_ple0, d_plepost0), lands_b = _ple_bwd(
        0, dx3, x2, z0, pe0, p3d, gains["ple_g"], weight("w_ple_gate0"), gains["ple_post_g"],
        rider=pair_exchange(group_b))
    group_ab = group_a + group_b
    sums_ab = pair_sum("ab", group_ab, lands_a + lands_b)

    group_c = ["w_ple_gate0", "w_ple_proj0"]
    (dx1, d_preffn0, d_postffn0, *scattered), got = _ffn_bwd(
        0, dx2, x1, f0, g0, u0, gains["pre_ffn_g"], weight("w_gu0"), weight("w_down0"), gains["post_ffn_g"], kc,
        rider=_both(pair_exchange(group_c), scatter(group_ab, sums_ab)))
    fused["w_gu0"], fused["w_down0"] = scattered[0:2], scattered[2:4]
    sums_c = pair_sum("c", group_c, got[:len(group_c)])
    keep(group_ab, sums_ab, got[len(group_c):])

    layers_of = lambda src: [t for t in BIGS.values() if t.src == src]
    own_scatter = ["w_gu", "w_down"]
    early = own_scatter + ["w_q", "w_o", "w_kv"]
    late = ["w_ple_gate", "w_ple_proj"]
    by_cols = lambda srcs: [src == "w_down" for src in srcs]
    jobs = [_chip_sum_fused_job(layers_of(src), fused, by_cols=src == "w_down") for src in own_scatter]
    jobs += [_chip_sum_job(layers_of(src), landed) for src in early if src not in own_scatter]
    halves = [r[0] for r in _multi_call("chip_sum_early", jobs, kc)]
    (dx0, d_pool, d_scale, d_postmix0, d_premix0), got = _mixa_bwd(
        dx1, x2d, y0, gains["pre_mix_g"], weight("pool_w"), weight("pool_scale"), gains["post_mix_g"],
        rider=_both(scatter(group_c, sums_c), _share_rider(halves, by_cols(early))))
    keep(group_c, sums_c, got[:len(group_c)])
    full_grads = dict(zip(early, got[len(group_c):]))

    rows = [d_premix0, d_premix1, d_postmix0, d_postmix1, d_preffn0, d_preffn1, d_postffn0, d_postffn1,
            d_ple0, d_ple1, d_plepost0, d_plepost1, d_kvg, d_scale, d_sinks, loss_row]
    as2d = lambda a: a.reshape(1, D) if a.ndim == 1 else a
    halves = [r[0] for r in _multi_call("chip_sum_late", [_chip_sum_job(layers_of(src), landed) for src in late], kc)]
    (tot, g_pool), got = _small_all_reduce(rows, d_pool, rider=_share_rider(halves, by_cols(late)))
    full_grads.update(zip(late, got))
    full_grads["pool_w"] = g_pool
    loss, small = _small_adamw(tot, kc, {nm: as2d(weights[nm]) for nm in SMALL_NAMES},
                               {nm: as2d(m_in[nm]) for nm in SMALL_NAMES},
                               {nm: as2d(v_in[nm]) for nm in SMALL_NAMES})


    def adam_job(src):
        rb = layers_of(src)[0].rb // (1 if src == "pool_w" else 2)
        return _adamw_job(rb, shard_view(src, weights[src]), full_grads[src],
                          shard_view(src, m_in[src]), shard_view(src, v_in[src]))

    out = {"grad": {}, "delta": {}, "new_m": {}, "new_v": {}}
    results = dict(zip(BIG_SOURCES, _multi_call("adamw", [adam_job(src) for src in BIG_SOURCES], kc)))
    for src in BIG_SOURCES:
        shape = weights[src].shape
        for kind, a in zip(("grad", "delta", "new_m", "new_v"), results[src]):
            out[kind][src] = a.reshape(shape)
    for nm in SMALL_NAMES:
        shape = weights[nm].shape
        for kind, a in zip(("grad", "delta", "new_m", "new_v"), small[nm]):
            out[kind][nm] = a.reshape(shape)

    return (loss.reshape(()), dx0.reshape(x.shape),
            *[out["grad"][nm] for nm in order], *[out["delta"][nm] for nm in order],
            *[out["new_m"][nm] for nm in order], *[out["new_v"][nm] for nm in order])
```

```python
import collections

import jax
import jax.numpy as jnp
from jax import lax
from jax.experimental import pallas as pl
from jax.experimental.pallas import tpu as pltpu

D = 1024
FF = 2816
N_HEADS = 16
HEAD_DIM = 64
N_KV_HEADS = 4
GQA = N_HEADS // N_KV_HEADS
KVD = N_KV_HEADS * HEAD_DIM
PLE = 256
BLK = 128
WINDOWS = (2, 4, 8, 16)
POOL_G = 256
HALO = 16
EPS = 1e-6
NEG_INF = -1e30
ATT_SCALE = HEAD_DIM ** -0.5
SLOPES = tuple(2.0 ** (-8.0 * (h + 1) / N_HEADS) for h in range(N_HEADS))
N_CHIPS = 4
N_DEV = 8

LR, B1, B2, AEPS, WD, STEP = 0.001, 0.9, 0.999, 1e-08, 0.01, 10
BC1 = 1.0 - B1 ** STEP
BC2 = 1.0 - B2 ** STEP

BF = jnp.bfloat16
F32 = jnp.float32
MESH = pl.DeviceIdType.MESH
VMEM_LIMIT_V7X = 58 * 1024 * 1024
TM = 256
TM_FFN_BWD = 512
FF_CHUNK = 256
FF_HALF = FF // 2

VSPEC = pl.BlockSpec(memory_space=pltpu.VMEM)
SSPEC = pl.BlockSpec(memory_space=pltpu.SMEM)
ANYSPEC = pl.BlockSpec(memory_space=pl.ANY)


def _params(n_grid=0):
    sem = ("arbitrary",) * n_grid if n_grid else None
    return pltpu.CompilerParams(dimension_semantics=sem, vmem_limit_bytes=VMEM_LIMIT_V7X)


def _sds(shape, dtype=F32):
    return jax.ShapeDtypeStruct(tuple(shape), dtype)


Rider = collections.namedtuple("Rider", "arrays out_shapes aliases scratch start mid finish")
MID_NUM, MID_DEN = 5, 8


def _call(body, *, name, grid, in_specs, out_specs, out_shape, args, scratch_shapes=(), rider=None, prefetch=None):
    ni, no, ns = len(in_specs), len(out_specs), len(scratch_shapes)
    npre = 0 if prefetch is None else 1
    pre = [] if prefetch is None else [prefetch]
    if rider is None:
        rider = Rider([], [], {}, [], None, None, None)
    ri, ro = len(rider.arrays), len(rider.out_shapes)

    def full(*refs):
        pre_refs, refs = refs[:npre], refs[npre:]
        ins, refs = refs[:ni], refs[ni:]
        rins, refs = refs[:ri], refs[ri:]
        outs, refs = refs[:no], refs[no:]
        routs, refs = refs[:ro], refs[ro:]
        scr, rscr = refs[:ns], refs[ns:]
        ids = [pl.program_id(a) for a in range(len(grid))]
        first = ids[0] == 0
        last = ids[0] == grid[0] - 1
        for a in range(1, len(grid)):
            first = first & (ids[a] == 0)
            last = last & (ids[a] == grid[a] - 1)

        if rider.start is not None:
            @pl.when(first)
            def _():
                rider.start(rins, routs, rscr)

        if rider.mid is not None:
            assert len(grid) == 1

            @pl.when(ids[0] == (grid[0] * MID_NUM) // MID_DEN)
            def _():
                rider.mid(rins, routs, rscr)

        body(*pre_refs, *ins, *outs, *scr)

        if rider.finish is not None:
            @pl.when(last)
            def _():
                rider.finish(rins, routs, rscr)

    outs = pl.pallas_call(
        full, name=name,
        grid_spec=pltpu.PrefetchScalarGridSpec(
            num_scalar_prefetch=npre, grid=grid,
            in_specs=list(in_specs) + [ANYSPEC] * ri, out_specs=list(out_specs) + [ANYSPEC] * ro,
            scratch_shapes=list(scratch_shapes) + list(rider.scratch)),
        out_shape=list(out_shape) + list(rider.out_shapes),
        input_output_aliases={npre + ni + a: no + b for a, b in rider.aliases.items()},
        compiler_params=_params(len(grid)))(*pre, *args, *rider.arrays)
    return list(outs[:no]), list(outs[no:])


Job = collections.namedtuple("Job", "steps ins outs fn")


def _multi_call(name, jobs, kc, rider=None):
    n = max(job.steps for job in jobs)

    def clamped(index, steps):
        return lambda s, kc_ref: index(jnp.minimum(s, steps - 1), kc_ref)

    in_specs, out_specs, out_shape, args = [], [], [], []
    for job in jobs:
        for arr, block, index, *single in job.ins:
            mode = dict(pipeline_mode=pl.Buffered(1)) if single and single[0] else {}
            in_specs.append(pl.BlockSpec(block, clamped(index, job.steps), **mode))
            args.append(arr)
        for sds, block, index in job.outs:
            out_specs.append(pl.BlockSpec(block, clamped(index, job.steps)))
            out_shape.append(sds)
    n_in = len(args)

    def body(kc_ref, *refs):
        s = pl.program_id(0)
        i0, o0 = 0, n_in
        for job in jobs:
            ins, outs = refs[i0:i0 + len(job.ins)], refs[o0:o0 + len(job.outs)]
            i0, o0 = i0 + len(job.ins), o0 + len(job.outs)

            @pl.when(s < job.steps)
            def _():
                job.fn(s, kc_ref, ins, outs)

    outs, routs = _call(body, name=name, grid=(n,), in_specs=in_specs, out_specs=out_specs, out_shape=out_shape,
                        args=args, prefetch=kc, rider=rider)
    res, o0 = [], 0
    for job in jobs:
        res.append(outs[o0:o0 + len(job.outs)])
        o0 += len(job.outs)
    return res if rider is None else (res, routs)


def _rms_fwd(x, g):
    r = lax.rsqrt(jnp.mean(x * x, axis=-1, keepdims=True) + EPS)
    return x * r * g


def _rms_bwd(x, g, dy):
    r = lax.rsqrt(jnp.mean(x * x, axis=-1, keepdims=True) + EPS)
    xn = x * r
    dxn = dy * g
    dx = r * (dxn - xn * jnp.mean(dxn * xn, axis=-1, keepdims=True))
    return dx, dy * xn


def _rowsum(a):
    return jnp.sum(a, axis=0, keepdims=True)


def _sigmoid(z):
    return 1.0 / (1.0 + jnp.exp(-z))


def _dot(a, b):
    return jnp.dot(a, b, preferred_element_type=F32)


def _dot_nt(a, b):
    return lax.dot_general(a, b, (((1,), (1,)), ((), ())), preferred_element_type=F32)


def _dot_tn(a, b):
    return lax.dot_general(a, b, (((0,), (0,)), ((), ())), preferred_element_type=F32)


def _row_spec(tm, width=D):
    return pl.BlockSpec((tm, width), lambda i: (i, 0))


def _const_spec(shape):
    zeros = (0,) * len(shape)
    return pl.BlockSpec(tuple(shape), lambda *_: zeros)


def _pool_delta(he, pos):
    out = []
    for gi, w in enumerate(WINDOWS):
        hg = he[:, gi * POOL_G:(gi + 1) * POOL_G]
        s = hg
        k = 1
        while k < w:
            s = s + pltpu.roll(s, k, 0)
            k *= 2
        cnt = jnp.maximum(jnp.minimum(pos + 1, w), 1).astype(F32)
        out.append(s / cnt - hg)
    return out


def _load_with_halo_before(x_ref, i, tm):
    r0 = pl.multiple_of(i * tm, tm)
    hs = pl.multiple_of(jnp.maximum(i * tm - HALO, 0), 8)
    xh = jnp.where(i > 0, x_ref[pl.ds(hs, HALO), :], 0.0)
    xt = x_ref[pl.ds(r0, tm), :]
    return xt, jnp.concatenate([xh, xt], axis=0)


def _mixa_fwd_job(x, pre_g, pool_w, pool_scale, post_g):
    s_len = x.shape[0]

    def fn(i, kc_ref, ins, outs):
        x_ref, pg_ref, w_ref, sc_ref, qg_ref = ins
        y_ref, x1_ref = outs
        xt, xe = _load_with_halo_before(x_ref, i, TM)
        he = _rms_fwd(xe, pg_ref[0:1, :])
        pos = i * TM - HALO + lax.broadcasted_iota(jnp.int32, (TM + HALO, 1), 0)
        ds = _pool_delta(he, pos)
        ys = [_dot(ds[gi][HALO:, :].astype(BF), w_ref[gi]) for gi in range(len(WINDOWS))]
        y = jnp.concatenate(ys, axis=1) * sc_ref[...]
        y_ref[...] = y
        x1_ref[...] = xt + _rms_fwd(y, qg_ref[0:1, :])

    def whole(a):
        zeros = (0,) * a.ndim
        return (a, a.shape, lambda j, kc_ref: zeros, True)

    rows = lambda j, kc_ref: (j, 0)
    return Job(s_len // TM, [whole(a) for a in (x, pre_g, pool_w, pool_scale, post_g)],
               [(_sds((s_len, D)), (TM, D), rows), (_sds((s_len, D)), (TM, D), rows)], fn)


def _mixa_bwd(dx1, x, y, pre_g, pool_w, pool_scale, post_g, rider=None):
    s_len = x.shape[0]
    n = s_len // TM
    ng = len(WINDOWS)

    def body(dx_ref, x_ref, y_ref, pg_ref, w_ref, sc_ref, qg_ref,
             dx0_ref, dw_ref, dsc_ref, dqg_ref, dpg_ref, wacc):
        i = pl.program_id(0)

        @pl.when(i == 0)
        def _():
            wacc[...] = jnp.zeros_like(wacc)
            dsc_ref[...] = jnp.zeros_like(dsc_ref)
            dqg_ref[...] = jnp.zeros_like(dqg_ref)
            dpg_ref[...] = jnp.zeros_like(dpg_ref)

        r0 = pl.multiple_of(i * TM, TM)
        xt, xe = _load_with_halo_before(x_ref, i, TM)
        he = _rms_fwd(xe, pg_ref[0:1, :])
        pos_b = i * TM - HALO + lax.broadcasted_iota(jnp.int32, (TM + HALO, 1), 0)
        ds = _pool_delta(he, pos_b)

        last = i == n - 1
        a0 = pl.multiple_of(jnp.minimum(i * TM + TM, s_len - HALO), 8)
        ye = jnp.concatenate([y_ref[pl.ds(r0, TM), :], y_ref[pl.ds(a0, HALO), :]], axis=0)
        dt = dx_ref[pl.ds(r0, TM), :]
        de = jnp.concatenate([dt, jnp.where(last, 0.0, dx_ref[pl.ds(a0, HALO), :])], axis=0)
        dye, prod = _rms_bwd(ye, qg_ref[0:1, :], de)
        dqg_ref[...] += _rowsum(prod[:TM, :])
        dys = dye * sc_ref[...]
        pos_a = i * TM + lax.broadcasted_iota(jnp.int32, (TM + HALO, 1), 0)

        dhs, dscs = [], []
        for gi, w in enumerate(WINDOWS):
            sl = slice(gi * POOL_G, (gi + 1) * POOL_G)
            wg = w_ref[gi]
            dys_g = dys[:, sl].astype(BF)
            d_g = ds[gi][HALO:, :].astype(BF)
            ypre = _dot(d_g, wg)
            dscs.append(_rowsum(dye[:TM, sl] * ypre))
            wacc[gi] += _dot_tn(d_g, dys_g[:TM, :])
            dd = _dot_nt(dys_g, wg)
            cnt = jnp.minimum(pos_a + 1, w).astype(F32)
            a = dd / cnt
            k = 1
            while k < w:
                a = a + pltpu.roll(a, TM + HALO - k, 0)
                k *= 2
            dhs.append(a[:TM, :] - dd[:TM, :])
        dsc_ref[...] += jnp.concatenate(dscs, axis=1)
        dh = jnp.concatenate(dhs, axis=1)
        dxp, prod2 = _rms_bwd(xt, pg_ref[0:1, :], dh)
        dpg_ref[...] += _rowsum(prod2)
        dx0_ref[...] = dt + dxp

        @pl.when(last)
        def _():
            dw_ref[...] = wacc[...].astype(BF)

    return _call(
        body, name="mixa_bwd", grid=(n,), in_specs=[VSPEC] * 7,
        out_specs=[_row_spec(TM), _const_spec((ng, POOL_G, POOL_G)), _const_spec((1, D)),
                   _const_spec((1, D)), _const_spec((1, D))],
        out_shape=[_sds((s_len, D)), _sds((ng, POOL_G, POOL_G), BF), _sds((1, D)), _sds((1, D)), _sds((1, D))],
        scratch_shapes=[pltpu.VMEM((ng, POOL_G, POOL_G), F32)],
        args=[dx1, x, y, pre_g, pool_w, pool_scale, post_g], rider=rider)


def _ple_math(layer, x, p_blk, g_ref, wg_ref, wp_ref, qg_ref):
    r = _rms_fwd(x, g_ref[layer:layer + 1, :]).astype(BF)
    z = _dot(r, wg_ref[...])
    pe = _dot(p_blk.astype(BF), wp_ref[...])
    return z, pe, x + _rms_fwd(pe * _sigmoid(z), qg_ref[layer:layer + 1, :])


def _ffn_fwd(layer, x1, pre_g, wgu, wd, post_g, rider=None, head=None):
    s_len = x1.shape[0]

    def body(*refs):
        if head:
            (x_ref, pg_ref, wgu_ref, wd_ref, qg_ref, p_ref, eg_ref, wg_ref, wp_ref, eq_ref, t_ref,
             f_ref, x2_ref, g_ref, u_ref, z_ref, pe_ref, dx_ref, lv_ref) = refs
        else:
            x_ref, pg_ref, wgu_ref, wd_ref, qg_ref, f_ref, x2_ref, g_ref, u_ref = refs
        x = x_ref[...]
        h = _rms_fwd(x, pg_ref[layer:layer + 1, :]).astype(BF)
        f = jnp.zeros((TM, D), F32)
        for c in range(FF // FF_HALF):
            cols = slice(c * FF_HALF, (c + 1) * FF_HALF)
            g = _dot(h, wgu_ref[0, :, cols])
            u = _dot(h, wgu_ref[1, :, cols])
            g_ref[:, cols] = g.astype(BF)
            u_ref[:, cols] = u.astype(BF)
            act = g * _sigmoid(g) * u
            f = f + _dot(act.astype(BF), wd_ref[cols, :])
        f_ref[...] = f
        x2 = x + _rms_fwd(f, qg_ref[layer:layer + 1, :])
        x2_ref[...] = x2
        if head:
            @pl.when(pl.program_id(0) == 0)
            def _():
                lv_ref[...] = jnp.zeros_like(lv_ref)
            z, pe, x3 = _ple_math(layer, x2, p_ref[...], eg_ref, wg_ref, wp_ref, eq_ref)
            z_ref[...] = z
            pe_ref[...] = pe
            err = x3 - t_ref[...]
            dx_ref[...] = err * (1.0 / D)
            lv_ref[...] += _rowsum(err * err)

    in_specs = [_row_spec(TM), VSPEC, VSPEC, VSPEC, VSPEC]
    args = [x1, pre_g, wgu, wd, post_g]
    out_specs = [_row_spec(TM), _row_spec(TM), _row_spec(TM, FF), _row_spec(TM, FF)]
    out_shape = [_sds((s_len, D)), _sds((s_len, D)), _sds((s_len, FF), BF), _sds((s_len, FF), BF)]
    if head:
        p, ple_g, w_gate, w_proj, ple_post_g, target = head
        in_specs += [pl.BlockSpec((None, TM, PLE), lambda i: (layer, i, 0)), VSPEC, VSPEC, VSPEC, VSPEC, _row_spec(TM)]
        args += [p, ple_g, w_gate, w_proj, ple_post_g, target]
        out_specs += [_row_spec(TM), _row_spec(TM), _row_spec(TM), _const_spec((1, D))]
        out_shape += [_sds((s_len, D))] * 3 + [_sds((1, D))]
    return _call(body, name=f"ffn_fwd{layer}", grid=(s_len // TM,), in_specs=in_specs, out_specs=out_specs,
                 out_shape=out_shape, args=args, rider=rider)


GU_PIECE = 128
DN_PIECE = 64
DN_SLOT = FF // N_CHIPS
HALF_D = D // 2
CHUNK_STRIDE = 6
CHUNK_START = (1, 7, 4, 10)


def _ffn_bwd(layer, dx2, x1, f, g_pre, u_pre, pre_g, wgu, wd, post_g, kc, rider=None):
    s_len = x1.shape[0]
    tm = TM_FFN_BWD
    n = s_len // tm
    nc = FF // FF_CHUNK
    n_gu, n_dn = FF_CHUNK // GU_PIECE, FF_CHUNK // DN_PIECE
    n_pieces = 2 * n_gu + n_dn
    n_blk = FF_HALF // GU_PIECE

    def edge_rows(c, i, kc_ref):
        return (jnp.where((c == 0) | (c == nc - 1), i, n - 1), 0)

    def chunk_at(c, kc_ref):
        k = kc_ref[0]
        start = jnp.where(k == 0, CHUNK_START[0], jnp.where(k == 1, CHUNK_START[1],
                                                            jnp.where(k == 2, CHUNK_START[2], CHUNK_START[3])))
        return ((c + start) * CHUNK_STRIDE) % nc

    def exchange(kc_ref, c, accg, accu, accd, own_gu_ref, land_gu_ref, own_dn_ref, land_dn_ref,
                 pl_gu, pl_dn, sib_gu, sib_dn, mine_gu, mine_dn, sum_gu, sum_dn,
                 psend, precv, ssend, lsem, rrecv):
        x, y, core = lax.axis_index("x"), lax.axis_index("y"), lax.axis_index("c")
        lower = core == 0

        def pair_copy(cc, part):
            p = cc % 2
            src, dst = ((sib_gu, pl_gu), (sib_dn, pl_dn))[part]
            return pltpu.make_async_remote_copy(src.at[p], dst.at[cc], psend.at[p, part], precv.at[cc, part],
                                                device_id=(x, y, 1 - core), device_id_type=MESH)

        def scatter(cc, wait):
            p = cc % 2
            hidden = chunk_at(cc, kc_ref) * FF_CHUNK

            assert n_gu == 2
            k0, k1 = hidden // FF_HALF, (hidden + GU_PIECE) // FF_HALF
            blk = (hidden - k0 * FF_HALF) // GU_PIECE
            for gu in range(2):
                @pl.when(k0 == k1)
                def _():
                    piece(p, wait, 2 * gu, sum_gu.at[p, gu], k0 + 2 * gu, 0, (pl.ds(blk, 2),))

                @pl.when(k0 != k1)
                def _():
                    piece(p, wait, 2 * gu, sum_gu.at[p, gu, 0], k0 + 2 * gu, 0, (blk,))
                    piece(p, wait, 2 * gu + 1, sum_gu.at[p, gu, 1], k1 + 2 * gu, 0, (0,))

            kd = hidden // DN_SLOT
            off = pl.multiple_of(hidden - kd * DN_SLOT, DN_PIECE)
            m = jnp.minimum((DN_SLOT - off) // DN_PIECE, n_dn)
            for mm in range(1, n_dn + 1):
                @pl.when(m == mm)
                def _():
                    rows = mm * DN_PIECE
                    piece(p, wait, 2 * n_gu, sum_dn.at[p, pl.ds(0, rows), :], kd, 1, (pl.ds(off, rows), slice(None)))
                    if mm < n_dn:
                        piece(p, wait, 2 * n_gu + 1, sum_dn.at[p, pl.ds(rows, FF_CHUNK - rows), :], kd + 1, 1,
                              (pl.ds(0, FF_CHUNK - rows), slice(None)))

        def piece(p, wait, pi, src, k, t, where):
            own_ref, land_ref = ((own_gu_ref, land_gu_ref), (own_dn_ref, land_dn_ref))[t]
            kx, ky = k // 2, k % 2
            fx, fy = (kx != x).astype(jnp.int32), (ky != y).astype(jnp.int32)
            local = (fx + fy) == 0
            j = jnp.maximum(fx + 2 * fy - 1, 0)

            @pl.when(local)
            def _():
                cp = pltpu.make_async_copy(src, own_ref.at[where], lsem.at[p, pi])
                if wait:
                    cp.wait()
                else:
                    cp.start()

            @pl.when(jnp.logical_not(local))
            def _():
                cp = pltpu.make_async_remote_copy(src, land_ref.at[(j,) + where], ssend.at[p, pi],
                                                  rrecv.at[t, j], device_id=(kx, ky, core), device_id_type=MESH)
                if wait:
                    cp.wait_send()
                else:
                    cp.start()

        def add_and_scatter(cc):
            p = cc % 2
            pair_copy(cc, 0).wait_recv()
            pair_copy(cc, 1).wait_recv()
            s_gu = (mine_gu[...] + pl_gu[cc].astype(F32)).astype(BF)
            for hc in range(n_gu):
                sum_gu[p, :, hc] = s_gu[:, :, hc * GU_PIECE:(hc + 1) * GU_PIECE]
            sum_dn[p] = (mine_dn[...] + pl_dn[cc].astype(F32)).astype(BF)
            scatter(cc, wait=False)

        @pl.when(c >= 1)
        def _():
            @pl.when(c >= 3)
            def _():
                scatter(c - 3, wait=True)
            add_and_scatter(c - 1)

        @pl.when(c >= 2)
        def _():
            pair_copy(c - 2, 0).wait_send()
            pair_copy(c - 2, 1).wait_send()

        p = c % 2
        my_rows = pl.ds(pl.multiple_of(core * HALF_D, HALF_D), HALF_D)
        sib_rows = pl.ds(pl.multiple_of((1 - core) * HALF_D, HALF_D), HALF_D)
        d_v = accd[...]
        sib_gu[p, 0] = accg[sib_rows, :].astype(BF)
        sib_gu[p, 1] = accu[sib_rows, :].astype(BF)
        sib_dn[p] = jnp.where(lower, d_v[:, HALF_D:], d_v[:, :HALF_D]).astype(BF)
        mine_gu[0] = accg[my_rows, :]
        mine_gu[1] = accu[my_rows, :]
        mine_dn[...] = jnp.where(lower, d_v[:, :HALF_D], d_v[:, HALF_D:])
        pair_copy(c, 0).start()
        pair_copy(c, 1).start()

        @pl.when(c == nc - 1)
        def _():
            scatter(nc - 3, wait=True)
            add_and_scatter(nc - 1)
            for cc in (nc - 2, nc - 1):
                pair_copy(cc, 0).wait_send()
                pair_copy(cc, 1).wait_send()
                scatter(cc, wait=True)
            for t, land_ref in enumerate((land_gu_ref, land_dn_ref)):
                for j in range(N_CHIPS - 1):
                    pltpu.make_async_remote_copy(land_ref.at[j], land_ref.at[j], ssend.at[0, 0], rrecv.at[t, j],
                                                 device_id=(x, y, core), device_id_type=MESH).wait_recv()

    def body(kc_ref, dx_ref, x_ref, f_ref, gp_ref, up_ref, pg_ref, wgu_ref, wd_ref, qg_ref,
             dx1_ref, dpg_ref, dqg_ref, own_gu_ref, land_gu_ref, own_dn_ref, land_dn_ref,
             h_s, df_s, dh_s, accg, accu, accd, *comm):
        c = pl.program_id(0)
        i = pl.program_id(1)
        rows = pl.ds(pl.multiple_of(i * tm, tm), tm)
        pg = pg_ref[layer:layer + 1, :]

        @pl.when((c == 0) & (i == 0))
        def _():
            dpg_ref[...] = jnp.zeros_like(dpg_ref)
            dqg_ref[...] = jnp.zeros_like(dqg_ref)

        @pl.when(c == 0)
        def _():
            h_s[rows, :] = _rms_fwd(x_ref[...], pg).astype(BF)
            df, prod = _rms_bwd(f_ref[...], qg_ref[layer:layer + 1, :], dx_ref[...])
            df_s[rows, :] = df.astype(BF)
            dqg_ref[...] += _rowsum(prod)

        @pl.when(i == 0)
        def _():
            accg[...] = jnp.zeros_like(accg)
            accu[...] = jnp.zeros_like(accu)
            accd[...] = jnp.zeros_like(accd)

        h = h_s[rows, :]
        df = df_s[rows, :]
        wg = wgu_ref[0]
        wu = wgu_ref[1]
        g = gp_ref[...].astype(F32)
        u = up_ref[...].astype(F32)
        sg = _sigmoid(g)
        a = g * sg
        dact = _dot_nt(df, wd_ref[...])
        accd[...] += _dot_tn((a * u).astype(BF), df)
        du = (dact * a).astype(BF)
        dg = (dact * u * (sg * (1.0 + g * (1.0 - sg)))).astype(BF)
        accg[...] += _dot_tn(h, dg)
        accu[...] += _dot_tn(h, du)
        dh = _dot_nt(dg, wg) + _dot_nt(du, wu)

        @pl.when(c == 0)
        def _():
            dh_s[rows, :] = dh

        @pl.when((c > 0) & (c < nc - 1))
        def _():
            dh_s[rows, :] += dh

        @pl.when(c == nc - 1)
        def _():
            dxp, prod = _rms_bwd(x_ref[...], pg, dh_s[rows, :] + dh)
            dpg_ref[...] += _rowsum(prod)
            dx1_ref[...] = dx_ref[...] + dxp

        @pl.when(i == n - 1)
        def _():
            exchange(kc_ref, c, accg, accu, accd, own_gu_ref, land_gu_ref, own_dn_ref, land_dn_ref, *comm)

    dma = pltpu.SemaphoreType.DMA
    return _call(
        body, name=f"ffn_bwd{layer}", grid=(nc, n),
        in_specs=[pl.BlockSpec((tm, D), edge_rows), pl.BlockSpec((tm, D), edge_rows),
                  pl.BlockSpec((tm, D), lambda c, i, kc_ref: (jnp.where(c == 0, i, n - 1), 0),
                               pipeline_mode=pl.Buffered(1)),
                  pl.BlockSpec((tm, FF_CHUNK), lambda c, i, kc_ref: (i, chunk_at(c, kc_ref))),
                  pl.BlockSpec((tm, FF_CHUNK), lambda c, i, kc_ref: (i, chunk_at(c, kc_ref))),
                  VSPEC,
                  pl.BlockSpec((2, D, FF_CHUNK), lambda c, i, kc_ref: (0, 0, chunk_at(c, kc_ref))),
                  pl.BlockSpec((FF_CHUNK, D), lambda c, i, kc_ref: (chunk_at(c, kc_ref), 0)),
                  VSPEC],
        out_specs=[pl.BlockSpec((tm, D), lambda c, i, kc_ref: (jnp.where(c == nc - 1, i, 0), 0)),
                   _const_spec((1, D)), _const_spec((1, D)), ANYSPEC, ANYSPEC, ANYSPEC, ANYSPEC],
        out_shape=[_sds((s_len, D)), _sds((1, D)), _sds((1, D)),
                   _sds((n_blk, HALF_D, GU_PIECE), BF), _sds((N_CHIPS - 1, n_blk, HALF_D, GU_PIECE), BF),
                   _sds((DN_SLOT, HALF_D), BF), _sds((N_CHIPS - 1, DN_SLOT, HALF_D), BF)],
        scratch_shapes=[pltpu.VMEM((s_len, D), BF), pltpu.VMEM((s_len, D), BF), pltpu.VMEM((s_len, D), F32),
                        pltpu.VMEM((D, FF_CHUNK), F32), pltpu.VMEM((D, FF_CHUNK), F32),
                        pltpu.VMEM((FF_CHUNK, D), F32),
                        pltpu.VMEM((nc, 2, HALF_D, FF_CHUNK), BF), pltpu.VMEM((nc, FF_CHUNK, HALF_D), BF),
                        pltpu.VMEM((2, 2, HALF_D, FF_CHUNK), BF), pltpu.VMEM((2, FF_CHUNK, HALF_D), BF),
                        pltpu.VMEM((2, HALF_D, FF_CHUNK), F32), pltpu.VMEM((FF_CHUNK, HALF_D), F32),
                        pltpu.VMEM((2, 2, n_gu, HALF_D, GU_PIECE), BF), pltpu.VMEM((2, FF_CHUNK, HALF_D), BF),
                        dma((2, 2)), dma((nc, 2)), dma((2, n_pieces)), dma((2, n_pieces)), dma((2, N_CHIPS - 1))],
        args=[dx2, x1, f, g_pre, u_pre, pre_g, wgu, wd, post_g], rider=rider, prefetch=kc)


def _ple_fwd(layer, x2, p, ple_g, w_gate, w_proj, post_g, qkv=None, rider=None):
    s_len = x2.shape[0]

    def body(*refs):
        if qkv:
            (x_ref, p_ref, g_ref, wg_ref, wp_ref, qg_ref, ng_ref, kg_ref, wq_ref, wkv_ref,
             z_ref, pe_ref, x3_ref, q_ref, kv_ref) = refs
        else:
            x_ref, p_ref, g_ref, wg_ref, wp_ref, qg_ref, z_ref, pe_ref, x3_ref = refs
        z, pe, x3 = _ple_math(layer, x_ref[...], p_ref[...], g_ref, wg_ref, wp_ref, qg_ref)
        z_ref[...] = z
        pe_ref[...] = pe
        x3_ref[...] = x3
        if qkv:
            q_ref[...] = _dot(_rms_fwd(x3, ng_ref[layer + 1:layer + 2, :]).astype(BF), wq_ref[...]).astype(BF)
            kv_ref[...] = _dot(_rms_fwd(x3, kg_ref[...]).astype(BF), wkv_ref[...]).astype(BF)

    p_spec = pl.BlockSpec((None, TM, PLE), lambda i: (layer, i, 0))
    in_specs = [_row_spec(TM), p_spec, VSPEC, VSPEC, VSPEC, VSPEC]
    args = [x2, p, ple_g, w_gate, w_proj, post_g]
    out_specs = [_row_spec(TM), _row_spec(TM), _row_spec(TM)]
    out_shape = [_sds((s_len, D))] * 3
    if qkv:
        in_specs += [VSPEC] * 4
        args += list(qkv)
        out_specs += [_row_spec(TM), _row_spec(TM, 2 * KVD)]
        out_shape += [_sds((s_len, D), BF), _sds((s_len, 2 * KVD), BF)]
    return _call(body, name=f"ple_fwd{layer}", grid=(s_len // TM,), in_specs=in_specs, out_specs=out_specs,
                 out_shape=out_shape, args=args, rider=rider)


def _ple_bwd(layer, dx3, x2, z, pe, p, ple_g, w_gate, post_g, rider=None):
    s_len = x2.shape[0]
    n = s_len // TM

    def body(dx_ref, x_ref, z_ref, pe_ref, p_ref, g_ref, wg_ref, qg_ref,
             dx2_ref, dwg_ref, dwp_ref, dg_ref, dqg_ref, gacc, pacc):
        i = pl.program_id(0)

        @pl.when(i == 0)
        def _():
            gacc[...] = jnp.zeros_like(gacc)
            pacc[...] = jnp.zeros_like(pacc)
            dg_ref[...] = jnp.zeros_like(dg_ref)
            dqg_ref[...] = jnp.zeros_like(dqg_ref)

        dx = dx_ref[...]
        x = x_ref[...]
        pe_v = pe_ref[...]
        gate = _sigmoid(z_ref[...])
        de, prod = _rms_bwd(pe_v * gate, qg_ref[layer:layer + 1, :], dx)
        dqg_ref[...] += _rowsum(prod)
        dpe = (de * gate).astype(BF)
        dz = (de * pe_v * gate * (1.0 - gate)).astype(BF)
        pacc[...] += _dot_tn(p_ref[...].astype(BF), dpe)
        g = g_ref[layer:layer + 1, :]
        r = _rms_fwd(x, g).astype(BF)
        gacc[...] += _dot_tn(r, dz)
        dr = _dot_nt(dz, wg_ref[...])
        dxp, prod2 = _rms_bwd(x, g, dr)
        dg_ref[...] += _rowsum(prod2)
        dx2_ref[...] = dx + dxp

        @pl.when(i == n - 1)
        def _():
            dwg_ref[...] = gacc[...].astype(BF)
            dwp_ref[...] = pacc[...].astype(BF)

    p_spec = pl.BlockSpec((None, TM, PLE), lambda i: (layer, i, 0))
    return _call(
        body, name=f"ple_bwd{layer}", grid=(n,),
        in_specs=[_row_spec(TM), _row_spec(TM), _row_spec(TM), _row_spec(TM), p_spec, VSPEC, VSPEC, VSPEC],
        out_specs=[_row_spec(TM), _const_spec((D, D)), _const_spec((PLE, D)), _const_spec((1, D)), _const_spec((1, D))],
        out_shape=[_sds((s_len, D)), _sds((D, D), BF), _sds((PLE, D), BF), _sds((1, D)), _sds((1, D))],
        scratch_shapes=[pltpu.VMEM((D, D), F32), pltpu.VMEM((PLE, D), F32)],
        args=[dx3, x2, z, pe, p, ple_g, w_gate, post_g], rider=rider)


def _qkv_bwd(dq, dkv, x3, dx4, q_g, kv_g, w_q, w_kv):
    s_len = x3.shape[0]
    n = s_len // TM

    def body(dq_ref, dkv_ref, x_ref, dx_ref, qg_ref, kg_ref, wq_ref, wkv_ref,
             dx3_ref, dwq_ref, dwkv_ref, dqg_ref, dkg_ref, qacc, kacc):
        i = pl.program_id(0)

        @pl.when(i == 0)
        def _():
            qacc[...] = jnp.zeros_like(qacc)
            kacc[...] = jnp.zeros_like(kacc)
            dqg_ref[...] = jnp.zeros_like(dqg_ref)
            dkg_ref[...] = jnp.zeros_like(dkg_ref)

        x = x_ref[...]
        qg = qg_ref[1:2, :]
        kg = kg_ref[...]
        dq_v = dq_ref[...]
        dkv_v = dkv_ref[...].astype(BF)
        qacc[...] += _dot_tn(_rms_fwd(x, qg).astype(BF), dq_v)
        kacc[...] += _dot_tn(_rms_fwd(x, kg).astype(BF), dkv_v)
        dxq, prod_q = _rms_bwd(x, qg, _dot_nt(dq_v, wq_ref[...]))
        dxk, prod_k = _rms_bwd(x, kg, _dot_nt(dkv_v, wkv_ref[...]))
        dqg_ref[...] += _rowsum(prod_q)
        dkg_ref[...] += _rowsum(prod_k)
        dx3_ref[...] = dx_ref[...] + dxq + dxk

        @pl.when(i == n - 1)
        def _():
            dwq_ref[...] = qacc[...].astype(BF)
            dwkv_ref[...] = kacc[...].astype(BF)

    outs, _ = _call(
        body, name="qkv_bwd", grid=(n,),
        in_specs=[_row_spec(TM), _row_spec(TM, 2 * KVD), _row_spec(TM), _row_spec(TM), VSPEC, VSPEC, VSPEC, VSPEC],
        out_specs=[_row_spec(TM), _const_spec((D, D)), _const_spec((D, 2 * KVD)),
                   _const_spec((1, D)), _const_spec((1, D))],
        out_shape=[_sds((s_len, D)), _sds((D, D), BF), _sds((D, 2 * KVD), BF), _sds((1, D)), _sds((1, D))],
        scratch_shapes=[pltpu.VMEM((D, D), F32), pltpu.VMEM((D, 2 * KVD), F32)],
        args=[dq, dkv, x3, dx4, q_g, kv_g, w_q, w_kv])
    return outs


def _attn_group(i, q, kvw, sink_ref, g):
    rows = GQA * BLK
    heads = [GQA * g + j for j in range(GQA)]
    off = jnp.where(i > 0, BLK, 0)
    row = lax.broadcasted_iota(jnp.int32, (rows, 2 * BLK), 0)
    rel = (row % BLK) - lax.broadcasted_iota(jnp.int32, (rows, 2 * BLK), 1) + off
    valid = (rel >= 0) & (rel < BLK)
    head_of_row = lax.broadcasted_iota(jnp.int32, (rows, 1), 0) // BLK
    slope = jnp.zeros((rows, 1), F32)
    sink = jnp.zeros((rows, 1), F32)
    for j, h in enumerate(heads):
        slope = jnp.where(head_of_row == j, SLOPES[h], slope)
        sink = jnp.where(head_of_row == j, sink_ref[0, h], sink)
    qs = jnp.concatenate([q[:, h * HEAD_DIM:(h + 1) * HEAD_DIM] for h in heads], axis=0)
    k = kvw[:, g * HEAD_DIM:(g + 1) * HEAD_DIM]
    v = kvw[:, KVD + g * HEAD_DIM:KVD + (g + 1) * HEAD_DIM]
    s = _dot_nt(qs, k) * ATT_SCALE - slope * rel.astype(F32)
    s = jnp.where(valid, s, NEG_INF)
    m = jnp.maximum(jnp.max(s, axis=-1, keepdims=True), sink)
    e = jnp.exp(s - m)
    es = jnp.exp(sink - m)
    inv = 1.0 / (jnp.sum(e, axis=-1, keepdims=True) + es)
    return e * inv, es * inv, qs, k, v


def _unstack_heads(stacked):
    return [stacked[j * BLK:(j + 1) * BLK, :] for j in range(GQA)]


def _kv_window(kv_ref, i):
    ks = pl.multiple_of(jnp.maximum(i * BLK - BLK, 0), BLK)
    return ks, kv_ref[pl.ds(ks, 2 * BLK), :]


def _attn_fwd(q, kv, sinks, x3, w_o, post_g, rider=None):
    s_len = q.shape[0]

    def body(q_ref, kv_ref, sk_ref, x_ref, wo_ref, g_ref, a_ref, y_ref, x4_ref):
        i = pl.program_id(0)
        _, kvw = _kv_window(kv_ref, i)
        q = q_ref[...]
        outs = []
        for g in range(N_KV_HEADS):
            p, _, _, _, v = _attn_group(i, q, kvw, sk_ref, g)
            outs += _unstack_heads(_dot(p.astype(BF), v))
        attn = jnp.concatenate(outs, axis=1)
        a_ref[...] = attn
        y = _dot(attn.astype(BF), wo_ref[...])
        y_ref[...] = y
        x4_ref[...] = x_ref[...] + _rms_fwd(y, g_ref[1:2, :])

    return _call(body, name="attn_fwd", grid=(s_len // BLK,),
                 in_specs=[_row_spec(BLK), VSPEC, SSPEC, _row_spec(BLK), VSPEC, VSPEC],
                 out_specs=[_row_spec(BLK)] * 3, out_shape=[_sds((s_len, D))] * 3,
                 args=[q, kv, sinks, x3, w_o, post_g], rider=rider)


ATT_STEP_BLOCKS = 4


def _attn_bwd(dx4, y, attn, q, kv, sinks, w_o, post_g, rider=None):
    s_len = q.shape[0]
    rows = ATT_STEP_BLOCKS * BLK
    n = s_len // rows

    def body(dx_ref, y_ref, a_ref, q_ref, kv_ref, sk_ref, wo_ref, g_ref,
             dq_ref, dkv_ref, dwo_ref, dg_ref, dsk_ref, wacc):
        i = pl.program_id(0)

        @pl.when(i == 0)
        def _():
            dkv_ref[...] = jnp.zeros_like(dkv_ref)
            wacc[...] = jnp.zeros_like(wacc)
            dg_ref[...] = jnp.zeros_like(dg_ref)
            dsk_ref[...] = jnp.zeros_like(dsk_ref)

        dy, prod = _rms_bwd(y_ref[...], g_ref[1:2, :], dx_ref[...])
        dg_ref[...] += _rowsum(prod)
        dyb = dy.astype(BF)
        attn_all = a_ref[...]
        wacc[...] += _dot_tn(attn_all.astype(BF), dyb)
        d_o_all = _dot_nt(dyb, wo_ref[...])
        q_all = q_ref[...]
        lane = lax.broadcasted_iota(jnp.int32, (1, D), 1)
        dsk = jnp.zeros((1, D), F32)
        for sub in range(ATT_STEP_BLOCKS):
            blk = i * ATT_STEP_BLOCKS + sub
            sl = slice(sub * BLK, (sub + 1) * BLK)
            d_o, q = d_o_all[sl, :], q_all[sl, :]
            dod = d_o * attn_all[sl, :]
            ks, kvw = _kv_window(kv_ref, blk)
            dqs, dks, dvs = [], [], []
            for g in range(N_KV_HEADS):
                p, ps, qs, k, v = _attn_group(blk, q, kvw, sk_ref, g)
                cols = [slice((GQA * g + j) * HEAD_DIM, (GQA * g + j + 1) * HEAD_DIM) for j in range(GQA)]
                do_s = jnp.concatenate([d_o[:, c] for c in cols], axis=0).astype(BF)
                dsum = jnp.concatenate([jnp.sum(dod[:, c], axis=-1, keepdims=True) for c in cols], axis=0)
                dp = _dot_nt(do_s, v)
                dsb = (p * (dp - dsum) * ATT_SCALE).astype(BF)
                sink_part = ps * dsum
                for j in range(GQA):
                    dsk = dsk + jnp.where(lane == GQA * g + j, -_rowsum(sink_part[j * BLK:(j + 1) * BLK, :]), 0.0)
                dqs += _unstack_heads(_dot(dsb, k))
                dks.append(_dot_tn(dsb, qs))
                dvs.append(_dot_tn(p.astype(BF), do_s))
            dq_ref[sl, :] = jnp.concatenate(dqs, axis=1).astype(BF)
            dkv_ref[pl.ds(ks, 2 * BLK), :] += jnp.concatenate(dks + dvs, axis=1)
        dsk_ref[...] += dsk

        @pl.when(i == n - 1)
        def _():
            dwo_ref[...] = wacc[...].astype(BF)

    return _call(
        body, name="attn_bwd", grid=(n,),
        in_specs=[_row_spec(rows), _row_spec(rows), _row_spec(rows), _row_spec(rows), VSPEC, SSPEC, VSPEC, VSPEC],
        out_specs=[_row_spec(rows), _const_spec((s_len, 2 * KVD)), _const_spec((D, D)),
                   _const_spec((1, D)), _const_spec((1, D))],
        out_shape=[_sds((s_len, D), BF), _sds((s_len, 2 * KVD)), _sds((D, D), BF), _sds((1, D)), _sds((1, D))],
        scratch_shapes=[pltpu.VMEM((D, D), F32)],
        args=[dx4, y, attn, q, kv, sinks, w_o, post_g], rider=rider)


Big = collections.namedtuple("Big", "name src layer L A R C rb")


def _bigs():
    out = {"pool_w": Big("pool_w", "pool_w", None, 4, 4, POOL_G // N_CHIPS, POOL_G, 32)}
    for l in range(2):
        out[f"w_gu{l}"] = Big(f"w_gu{l}", "w_gu", l, 1, 2, D, FF_HALF, 256)
        out[f"w_down{l}"] = Big(f"w_down{l}", "w_down", l, 1, 4, FF // N_CHIPS, D, 352)
        out[f"w_ple_gate{l}"] = Big(f"w_ple_gate{l}", "w_ple_gate", l, 1, 4, D // N_CHIPS, D, 128)
        out[f"w_ple_proj{l}"] = Big(f"w_ple_proj{l}", "w_ple_proj", l, 1, 1, PLE, D // N_CHIPS, 128)
    out["w_q"] = Big("w_q", "w_q", None, 1, 4, D // N_CHIPS, D, 128)
    out["w_o"] = Big("w_o", "w_o", None, 1, 4, D // N_CHIPS, D, 128)
    out["w_kv"] = Big("w_kv", "w_kv", None, 1, 4, D // N_CHIPS, 2 * KVD, 128)
    return out


BIGS = _bigs()
POOL_SCALE = Big("pool_scale", "pool_scale", None, 1, 1, 1, D // N_CHIPS, 1)
BIG_SOURCES = ("w_gu", "w_down", "w_ple_gate", "w_ple_proj", "w_q", "w_o", "w_kv", "pool_w")


def _ncb(t):
    return N_CHIPS // t.A


def _full_shape(t, rows=None):
    return (t.L, t.A, t.R if rows is None else rows, _ncb(t) * t.C)


def _slot_index(t, k):
    return k // _ncb(t), k % _ncb(t)


def _slot(ref, t, k, row0, rows):
    a, cb = _slot_index(t, k)
    return ref.at[:, a, pl.ds(row0, rows), pl.ds(pl.multiple_of(cb * t.C, 128), t.C)]


def _place_job(t, w, out_dtype=BF):
    nb = next((nb for nb in (8, 4, 2, 1) if t.R % (16 * nb) == 0), 1) if t.L == 1 else 1
    rb = t.R // nb

    def fn(j, kc_ref, ins, outs):
        outs[0][...] = ins[0][...].astype(out_dtype)

    def in_map(j, kc_ref):
        return (j // nb if t.layer is None else t.layer, j % nb, 0)

    def out_map(j, kc_ref):
        a, cb = _slot_index(t, kc_ref[0])
        return (j // nb, a, j % nb, cb)

    return Job(t.L * nb, [(w, (None, rb, t.C), in_map)],
               [(_sds(_full_shape(t), out_dtype), (None, None, rb, t.C), out_map)], fn)


def _mesh_position():
    x, y, c = lax.axis_index("x"), lax.axis_index("y"), lax.axis_index("c")
    chips = [(1 - x, y), (x, 1 - y), (1 - x, 1 - y)]
    return x, y, c, chips


DIRECT_BELOW = 1024


def _gather_rider(parts, fulls):
    nt = len(parts)
    TO_X, TO_Y, FWD_X, FWD_Y, SIB_X, SIB_Y, SIB_D = range(7)

    def rows_of(ti, core):
        t, r0, r1 = parts[ti]
        h = (r1 - r0) // 2
        return r0 + core * h, h

    def copy(outs, sems, kind, ti, k_src, row0, rows, dev):
        region = _slot(outs[ti], parts[ti][0], k_src, row0, rows)
        return pltpu.make_async_remote_copy(region, region, sems[0].at[ti, kind], sems[1].at[ti, kind],
                                            device_id=dev, device_id_type=MESH)

    def plan(outs, sems):
        x, y, c, _ = _mesh_position()
        me, kx, ky, kd = 2 * x + y, 2 * (1 - x) + y, 2 * x + (1 - y), 2 * (1 - x) + (1 - y)
        dev_x, dev_y, dev_d, sib = (1 - x, y, c), (x, 1 - y, c), (1 - x, 1 - y, c), (x, y, 1 - c)

        def whole(ti):
            return 0, parts[ti][0].R

        def mk(kind, k_send, k_recv, dev, send_rows, recv_rows):
            def build(ti, side):
                k_src = k_send if side == "s" else k_recv
                row0, rows = (send_rows if side == "s" else recv_rows)(ti)
                return copy(outs, sems, kind, ti, k_src, row0, rows, dev)
            return build

        def first_half(core):
            return lambda ti: (rows_of(ti, core)[0], rows_of(ti, core)[1] // 2)

        def second_half(core):
            return lambda ti: (rows_of(ti, core)[0] + rows_of(ti, core)[1] // 2, rows_of(ti, core)[1] // 2)

        mine = lambda ti: rows_of(ti, c)
        theirs = lambda ti: rows_of(ti, 1 - c)
        split = {
            TO_X: mk(TO_X, me, kx, dev_x, mine, mine),
            TO_Y: mk(TO_Y, me, ky, dev_y, mine, mine),
            FWD_X: mk(FWD_X, ky, kd, dev_x, first_half(c), first_half(c)),
            FWD_Y: mk(FWD_Y, kx, kd, dev_y, second_half(c), second_half(c)),
            SIB_X: mk(SIB_X, kx, kx, sib, mine, theirs),
            SIB_Y: mk(SIB_Y, ky, ky, sib, mine, theirs),
            SIB_D: mk(SIB_D, kd, kd, sib, mine, theirs),
        }
        direct = {
            TO_X: mk(TO_X, me, kx, dev_x, whole, whole),
            TO_Y: mk(TO_Y, me, ky, dev_y, whole, whole),
            FWD_X: mk(FWD_X, me, kd, dev_d, whole, whole),
        }
        return split, direct

    is_split = [t.L * t.R * t.C >= DIRECT_BELOW for t, _, _ in parts]
    assert all(s or (r0, r1) == (0, t.R) for s, (t, r0, r1) in zip(is_split, parts))

    def start(ins, outs, sems):
        split, direct = plan(outs, sems)
        for ti in range(nt):
            kinds = split if is_split[ti] else direct
            kinds[TO_X](ti, "s").start()
            kinds[TO_Y](ti, "s").start()
            if not is_split[ti]:
                kinds[FWD_X](ti, "s").start()

    def mid(ins, outs, sems):
        split, _ = plan(outs, sems)
        for ti in range(nt):
            if is_split[ti]:
                split[TO_Y](ti, "r").wait_recv()
                split[FWD_X](ti, "s").start()
                split[SIB_Y](ti, "s").start()
        for ti in range(nt):
            if is_split[ti]:
                split[TO_X](ti, "r").wait_recv()
                split[FWD_Y](ti, "s").start()
                split[SIB_X](ti, "s").start()

    def finish(ins, outs, sems):
        split, direct = plan(outs, sems)
        for ti in range(nt):
            if is_split[ti]:
                split[FWD_X](ti, "r").wait_recv()
                split[FWD_Y](ti, "r").wait_recv()
                split[SIB_D](ti, "s").start()
            else:
                for kind in (TO_X, TO_Y, FWD_X):
                    direct[kind](ti, "r").wait_recv()
        for ti in range(nt):
            if is_split[ti]:
                for kind in (SIB_X, SIB_Y, SIB_D):
                    split[kind](ti, "r").wait_recv()
        for ti in range(nt):
            kinds = split if is_split[ti] else direct
            for kind in kinds:
                kinds[kind](ti, "s").wait_send()

    sems = pltpu.SemaphoreType.DMA((nt, 7))
    return Rider(list(fulls), [_sds(a.shape, a.dtype) for a in fulls], {i: i for i in range(nt)},
                 [sems, sems], start, mid, finish)


def _pair_exchange_rider(specs, grads):
    nt = len(specs)

    def copy(ins, outs, sems, ti, c, sibling):
        half = specs[ti].R // 2
        return pltpu.make_async_remote_copy(ins[ti].at[:, :, pl.ds((1 - c) * half, half), :], outs[ti],
                                            sems[0].at[ti], sems[1].at[ti], device_id=sibling, device_id_type=MESH)

    def start(ins, outs, sems):
        x, y, c, _ = _mesh_position()
        for ti in range(nt):
            copy(ins, outs, sems, ti, c, (x, y, 1 - c)).start()

    def finish(ins, outs, sems):
        x, y, c, _ = _mesh_position()
        for ti in range(nt):
            copy(ins, outs, sems, ti, c, (x, y, 1 - c)).wait()

    sems = pltpu.SemaphoreType.DMA((nt,))
    return Rider(list(grads), [_sds(_full_shape(t, t.R // 2), BF) for t in specs], {}, [sems, sems], start, None, finish)


def _pair_sum_job(t, g, land):
    assert t.L == 1
    half = t.R // 2
    nj = half // t.rb
    block = (None, t.A, t.rb, _ncb(t) * t.C)

    def fn(j, kc_ref, ins, outs):
        outs[0][...] = (ins[0][...].astype(F32) + ins[1][...].astype(F32)).astype(BF)

    return Job(nj,
               [(g, block, lambda j, kc_ref: (0, 0, kc_ref[1] * nj + j, 0)),
                (land, block, lambda j, kc_ref: (0, 0, j, 0))],
               [(_sds(_full_shape(t, half), BF), block, lambda j, kc_ref: (0, 0, j, 0))], fn)


def _scatter_rider(specs, sums):
    nt = len(specs)

    def copy(ins, outs, sems, ti, j, chip, c):
        t = specs[ti]
        cx, cy = chip
        return pltpu.make_async_remote_copy(_slot(ins[ti], t, 2 * cx + cy, 0, t.R // 2), outs[ti].at[j],
                                            sems[0].at[ti, j], sems[1].at[ti, j],
                                            device_id=(cx, cy, c), device_id_type=MESH)

    def start(ins, outs, sems):
        _, _, c, chips = _mesh_position()
        for j, chip in enumerate(chips):
            for ti in range(nt):
                copy(ins, outs, sems, ti, j, chip, c).start()

    def finish(ins, outs, sems):
        _, _, c, chips = _mesh_position()
        for j, chip in enumerate(chips):
            for ti in range(nt):
                copy(ins, outs, sems, ti, j, chip, c).wait()

    sems = pltpu.SemaphoreType.DMA((nt, N_CHIPS - 1))
    return Rider(list(sums), [_sds((N_CHIPS - 1, t.L, t.R // 2, t.C), BF) for t in specs], {}, [sems, sems],
                 start, None, finish)


def _chip_sum_job(ts, landed):
    t0 = ts[0]
    assert t0.L == 1
    half = t0.R // 2
    nj = half // t0.rb

    def local(j, li):
        return jnp.clip(j - li * nj, 0, nj - 1)

    ins = []
    for li, t in enumerate(ts):
        s, land = landed[t.name]

        def own_map(j, kc_ref, li=li, t=t):
            a, cb = _slot_index(t, kc_ref[0])
            return (0, a, local(j, li), cb)

        ins.append((s, (None, None, t.rb, t.C), own_map))
        ins.append((land, (N_CHIPS - 1, None, t.rb, t.C), lambda j, kc_ref, li=li: (0, 0, local(j, li), 0)))

    def fn(j, kc_ref, in_refs, outs):
        for li in range(len(ts)):
            @pl.when(j // nj == li)
            def _():
                acc = in_refs[2 * li][...].astype(F32)
                for k in range(N_CHIPS - 1):
                    acc = acc + in_refs[2 * li + 1][k].astype(F32)
                outs[0][...] = acc

    return Job(len(ts) * nj, ins,
               [(_sds((len(ts), t0.R, t0.C)), (None, t0.rb, t0.C),
                 lambda j, kc_ref: (j // nj, kc_ref[1] * nj + j % nj, 0))], fn)


def _adamw_job(rb, w, g, m, v):
    n_layers, r, c = w.shape
    nb = r // rb
    block = (None, rb, c)
    index = lambda j, kc_ref: (j // nb, j % nb, 0)

    def fn(j, kc_ref, ins, outs):
        g_v = ins[1][...]
        outs[0][...] = g_v
        outs[1][...], outs[2][...], outs[3][...] = _adamw_math(ins[0][...], g_v, ins[2][...], ins[3][...])

    return Job(n_layers * nb, [(a, block, index) for a in (w, g, m, v)],
               [(_sds(w.shape), block, index)] * 4, fn)


def _chip_sum_fused_job(ts, fused, by_cols):
    t0 = ts[0]
    own0 = fused[t0.name][0]
    if by_cols:
        rows, cols = own0.shape
    else:
        nb, rows, bw = own0.shape
        cols = nb * bw
    nj = rows // t0.rb

    def local(j, li):
        return jnp.clip(j - li * nj, 0, nj - 1)

    ins = []
    for li, t in enumerate(ts):
        own, land = fused[t.name]
        if by_cols:
            ins.append((own, (t.rb, cols), lambda j, kc_ref, li=li: (local(j, li), 0)))
            ins.append((land, (N_CHIPS - 1, t.rb, cols), lambda j, kc_ref, li=li: (0, local(j, li), 0)))
        else:
            ins.append((own, (nb, t.rb, bw), lambda j, kc_ref, li=li: (0, local(j, li), 0)))
            ins.append((land, (N_CHIPS - 1, nb, t.rb, bw), lambda j, kc_ref, li=li: (0, 0, local(j, li), 0)))

    def fn(j, kc_ref, in_refs, outs):
        for li in range(len(ts)):
            @pl.when(j // nj == li)
            def _():
                acc = in_refs[2 * li][...].astype(F32)
                for k in range(N_CHIPS - 1):
                    acc = acc + in_refs[2 * li + 1][k].astype(F32)
                outs[0][...] = acc if by_cols else jnp.concatenate([acc[b] for b in range(nb)], axis=1)

    def out_map(j, kc_ref):
        return (j // nj, j % nj, kc_ref[1]) if by_cols else (j // nj, kc_ref[1] * nj + j % nj, 0)

    return Job(len(ts) * nj, ins, [(_sds((len(ts), t0.R, t0.C)), (None, t0.rb, cols), out_map)], fn)


def _share_rider(halves, by_cols):
    nt = len(halves)

    def copy(outs, sems, ti, core, sibling):
        axis = 2 if by_cols[ti] else 1
        half = halves[ti].shape[axis] // 2
        piece = pl.ds(pl.multiple_of(core * half, 128 if by_cols[ti] else 8), half)
        part = outs[ti].at[:, :, piece] if by_cols[ti] else outs[ti].at[:, piece, :]
        return pltpu.make_async_remote_copy(part, part, sems[0].at[ti], sems[1].at[ti],
                                            device_id=sibling, device_id_type=MESH)

    def start(ins, outs, sems):
        x, y, c, _ = _mesh_position()
        for ti in range(nt):
            copy(outs, sems, ti, c, (x, y, 1 - c)).start()

    def finish(ins, outs, sems):
        x, y, c, _ = _mesh_position()
        for ti in range(nt):
            copy(outs, sems, ti, 1 - c, (x, y, 1 - c)).wait_recv()
        for ti in range(nt):
            copy(outs, sems, ti, c, (x, y, 1 - c)).wait_send()

    sems = pltpu.SemaphoreType.DMA((nt,))
    return Rider(list(halves), [_sds(a.shape, a.dtype) for a in halves], {i: i for i in range(nt)}, [sems, sems],
                 start, None, finish)


def _both(r1, r2):
    assert r1.mid is None and r2.mid is None
    ni, no, ns = len(r1.arrays), len(r1.out_shapes), len(r1.scratch)

    def split(fn1, fn2):
        def run(ins, outs, scr):
            fn1(ins[:ni], outs[:no], scr[:ns])
            fn2(ins[ni:], outs[no:], scr[ns:])
        return run

    aliases = dict(r1.aliases)
    aliases.update({ni + a: no + b for a, b in r2.aliases.items()})
    return Rider(r1.arrays + r2.arrays, r1.out_shapes + r2.out_shapes, aliases, r1.scratch + r2.scratch,
                 split(r1.start, r2.start), None, split(r1.finish, r2.finish))


def _adamw_math(w, g, m, v):
    m = B1 * m + (1.0 - B1) * g
    v = B2 * v + (1.0 - B2) * (g * g)
    delta = -LR * ((m / BC1) / (jnp.sqrt(v / BC2) + AEPS) + WD * w)
    return delta, m, v


GAIN_ROWS = {"pre_mix_g": 0, "post_mix_g": 2, "pre_ffn_g": 4, "post_ffn_g": 6, "ple_g": 8, "ple_post_g": 10}
ROW_KV_G, ROW_POOL_SCALE, ROW_SINKS, ROW_LOSS, PACK_ROWS = 12, 13, 14, 15, 16
SMALL_NAMES = tuple(GAIN_ROWS) + ("kv_g", "pool_scale", "sinks")


def _small_all_reduce(rows, dpool, rider=None):
    ng, pr = len(WINDOWS), POOL_G // N_CHIPS

    def body(*refs):
        row_refs = refs[:PACK_ROWS]
        dpool_ref, tot_ref, gpool_ref, pack, land, pland, send, recv, psend, precv = refs[PACK_ROWS:]
        x, y, c, _ = _mesh_position()
        me = 4 * x + 2 * y + c
        for r in range(PACK_ROWS):
            pack[r:r + 1, :] = row_refs[r][...]

        def shard_of(k):
            return dpool_ref.at[:, pl.ds(pl.multiple_of(k * pr, pr), pr), :]

        cps = []
        for j in range(1, N_DEV):
            px, py, pc = x ^ (j >> 2), y ^ ((j >> 1) & 1), c ^ (j & 1)
            cps.append(pltpu.make_async_remote_copy(pack, land.at[me], send.at[j], recv.at[j],
                                                    device_id=(px, py, pc), device_id_type=MESH))
            cps.append(pltpu.make_async_remote_copy(shard_of(2 * px + py), pland.at[me], psend.at[j], precv.at[j],
                                                    device_id=(px, py, pc), device_id_type=MESH))
        for cp in cps:
            cp.start()
        land[me] = pack[...]
        pland[me] = dpool_ref[:, pl.ds(pl.multiple_of((2 * x + y) * pr, pr), pr), :]
        for j in range(1, N_DEV):
            pltpu.make_async_remote_copy(pack, land.at[me ^ j], send.at[j], recv.at[j],
                                         device_id=(x, y, c), device_id_type=MESH).wait_recv()
            pltpu.make_async_remote_copy(shard_of(0), pland.at[me ^ j], psend.at[j], precv.at[j],
                                         device_id=(x, y, c), device_id_type=MESH).wait_recv()
        for cp in cps:
            cp.wait_send()
        tot = land[0]
        gp = pland[0].astype(F32)
        for d in range(1, N_DEV):
            tot = tot + land[d]
            gp = gp + pland[d].astype(F32)
        tot_ref[...] = tot
        gpool_ref[...] = gp

    sems = pltpu.SemaphoreType.DMA((N_DEV,))
    return _call(
        body, name="small_all_reduce", grid=(1,),
        in_specs=[VSPEC] * (PACK_ROWS + 1), out_specs=[VSPEC, VSPEC],
        out_shape=[_sds((PACK_ROWS, D)), _sds((ng, pr, POOL_G))],
        scratch_shapes=[pltpu.VMEM((PACK_ROWS, D), F32), pltpu.VMEM((N_DEV, PACK_ROWS, D), F32),
                        pltpu.VMEM((N_DEV, ng, pr, POOL_G), BF), sems, sems, sems, sems],
        args=[*rows, dpool], rider=rider)


def _small_adamw(tot, kc, small_w, small_m, small_v):
    names = SMALL_NAMES
    n = len(names)

    def body(*refs):
        tot_ref, kc_ref = refs[0], refs[1]
        w_refs = dict(zip(names, refs[2:2 + n]))
        m_refs = dict(zip(names, refs[2 + n:2 + 2 * n]))
        v_refs = dict(zip(names, refs[2 + 2 * n:2 + 3 * n]))
        loss_ref = refs[2 + 3 * n]
        out_refs = {nm: refs[3 + 3 * n + 4 * k: 7 + 3 * n + 4 * k] for k, nm in enumerate(names)}
        tot = tot_ref[...]
        loss_ref[...] = 0.5 * jnp.sum(tot[ROW_LOSS:ROW_LOSS + 1, :], axis=-1, keepdims=True) * (1.0 / D)

        def update(nm, g):
            g_ref, d_ref, nm_ref, nv_ref = out_refs[nm]
            g_ref[...] = g
            d_ref[...], nm_ref[...], nv_ref[...] = _adamw_math(w_refs[nm][...], g, m_refs[nm][...], v_refs[nm][...])

        for nm, r in GAIN_ROWS.items():
            update(nm, tot[r:r + 2, :])
        update("kv_g", tot[ROW_KV_G:ROW_KV_G + 1, :])
        k = kc_ref[0]
        width = D // N_CHIPS
        g_scale = jnp.zeros((1, width), F32)
        for kk in range(N_CHIPS):
            g_scale = g_scale + jnp.where(k == kk, tot[ROW_POOL_SCALE:ROW_POOL_SCALE + 1, kk * width:(kk + 1) * width], 0.0)
        update("pool_scale", g_scale)
        update("sinks", tot[ROW_SINKS:ROW_SINKS + 1, 0:N_HEADS])

    ins = [tot, kc] + [small_w[nm] for nm in names] + [small_m[nm] for nm in names] + [small_v[nm] for nm in names]
    out_shape = [_sds((1, 1))]
    for nm in names:
        out_shape += [_sds(small_w[nm].shape)] * 4
    outs = pl.pallas_call(
        body, name="small_adamw",
        in_specs=[VSPEC, SSPEC] + [VSPEC] * (3 * n), out_specs=[VSPEC] * len(out_shape), out_shape=out_shape,
        compiler_params=_params(),
    )(*ins)
    return outs[0], {nm: outs[1 + 4 * k: 5 + 4 * k] for k, nm in enumerate(names)}


def _compute_layout(t, full):
    if t.src == "w_gu":
        return full.reshape(2, D, FF)
    if t.src == "pool_w":
        return full.reshape(len(WINDOWS), POOL_G, POOL_G)
    if t.src == "pool_scale":
        return full.reshape(1, D)
    return full.reshape(t.A * t.R, _ncb(t) * t.C)


def kernel(x, p, pre_mix_g, post_mix_g, pre_ffn_g, post_ffn_g, pool_w, pool_scale, kv_g, w_kv, w_q, sinks, w_o, w_gu, w_down, ple_g, w_ple_gate, w_ple_proj, ple_post_g, loss_target, m_pre_mix_g, m_post_mix_g, m_pre_ffn_g, m_post_ffn_g, m_pool_w, m_pool_scale, m_kv_g, m_w_kv, m_w_q, m_sinks, m_w_o, m_w_gu, m_w_down, m_ple_g, m_w_ple_gate, m_w_ple_proj, m_ple_post_g, v_pre_mix_g, v_post_mix_g, v_pre_ffn_g, v_post_ffn_g, v_pool_w, v_pool_scale, v_kv_g, v_w_kv, v_w_q, v_sinks, v_w_o, v_w_gu, v_w_down, v_ple_g, v_w_ple_gate, v_w_ple_proj, v_ple_post_g):
    weights = dict(pre_mix_g=pre_mix_g, post_mix_g=post_mix_g, pre_ffn_g=pre_ffn_g, post_ffn_g=post_ffn_g,
                   pool_w=pool_w, pool_scale=pool_scale, kv_g=kv_g, w_kv=w_kv, w_q=w_q, sinks=sinks, w_o=w_o,
                   w_gu=w_gu, w_down=w_down, ple_g=ple_g, w_ple_gate=w_ple_gate, w_ple_proj=w_ple_proj,
                   ple_post_g=ple_post_g)
    m_in = dict(pre_mix_g=m_pre_mix_g, post_mix_g=m_post_mix_g, pre_ffn_g=m_pre_ffn_g, post_ffn_g=m_post_ffn_g,
                pool_w=m_pool_w, pool_scale=m_pool_scale, kv_g=m_kv_g, w_kv=m_w_kv, w_q=m_w_q, sinks=m_sinks,
                w_o=m_w_o, w_gu=m_w_gu, w_down=m_w_down, ple_g=m_ple_g, w_ple_gate=m_w_ple_gate,
                w_ple_proj=m_w_ple_proj, ple_post_g=m_ple_post_g)
    v_in = dict(pre_mix_g=v_pre_mix_g, post_mix_g=v_post_mix_g, pre_ffn_g=v_pre_ffn_g, post_ffn_g=v_post_ffn_g,
                pool_w=v_pool_w, pool_scale=v_pool_scale, kv_g=v_kv_g, w_kv=v_w_kv, w_q=v_w_q, sinks=v_sinks,
                w_o=v_w_o, w_gu=v_w_gu, w_down=v_w_down, ple_g=v_ple_g, w_ple_gate=v_w_ple_gate,
                w_ple_proj=v_w_ple_proj, ple_post_g=v_ple_post_g)
    order = ["pre_mix_g", "post_mix_g", "pre_ffn_g", "post_ffn_g", "pool_w", "pool_scale", "kv_g", "w_kv", "w_q",
             "sinks", "w_o", "w_gu", "w_down", "ple_g", "w_ple_gate", "w_ple_proj", "ple_post_g"]

    kc = jnp.stack([2 * lax.axis_index("x") + lax.axis_index("y"), lax.axis_index("c")]).astype(jnp.int32)
    s_len = x.shape[1]
    x2d = x.reshape(s_len, D)
    p3d = p.reshape(2, s_len, PLE)
    target = loss_target.reshape(s_len, D)
    kv_g2d = kv_g.reshape(1, D)
    gains = {nm: weights[nm] for nm in GAIN_ROWS}

    def shard_view(src, a):
        t = next(t for t in BIGS.values() if t.src == src)
        return a.reshape(-1, t.R, t.C)

    first, second = ["pool_w", "pool_scale"], ["w_gu0", "w_down0"]
    rest = [nm for nm in BIGS if nm not in first + second]
    specs = dict(BIGS, pool_scale=POOL_SCALE)
    placed = {}

    def place_job(nm):
        if nm == "pool_scale":
            return _place_job(POOL_SCALE, pool_scale.reshape(1, 1, D // N_CHIPS), F32)
        return _place_job(BIGS[nm], shard_view(BIGS[nm].src, weights[BIGS[nm].src]))

    def gather(names, rows=None):
        rows = rows or {}
        parts = [(specs[nm],) + tuple(rows.get(nm, (0, specs[nm].R))) for nm in names]
        return _gather_rider(parts, [placed[nm] for nm in names])

    def take(names, results):
        for nm, a in zip(names, results):
            placed[nm] = a

    def weight(nm):
        return _compute_layout(specs[nm], placed[nm])

    take(first, [r[0] for r in _multi_call("place_pool", [place_job(nm) for nm in first], kc)])
    cast, got = _multi_call("place_ffn0", [place_job(nm) for nm in second], kc, rider=gather(first))
    take(second, [r[0] for r in cast])
    take(first, got)
    jobs = [place_job(nm) for nm in rest]
    jobs.append(_mixa_fwd_job(x2d, gains["pre_mix_g"], weight("pool_w"), weight("pool_scale"), gains["post_mix_g"]))
    results, got = _multi_call("cast_and_mixa_fwd", jobs, kc, rider=gather(second))
    take(rest, [r[0] for r in results[:-1]])
    take(second, got)
    y0, x1 = results[-1]

    ride = ["w_ple_gate0", "w_ple_proj0", "w_q", "w_kv", "w_o", "w_gu1"]
    (f0, x2, g0, u0), got = _ffn_fwd(0, x1, gains["pre_ffn_g"], weight("w_gu0"), weight("w_down0"), gains["post_ffn_g"],
                             rider=gather(ride, {"w_gu1": (0, 320)}))
    take(ride, got)

    ride = ["w_ple_gate1", "w_ple_proj1", "w_gu1"]
    (z0, pe0, x3, q, kv), got = _ple_fwd(
        0, x2, p3d, gains["ple_g"], weight("w_ple_gate0"), weight("w_ple_proj0"), gains["ple_post_g"],
        qkv=(gains["pre_mix_g"], kv_g2d, weight("w_q"), weight("w_kv")),
        rider=gather(ride, {"w_gu1": (320, 704)}))
    take(ride, got)

    ride = ["w_down1", "w_gu1"]
    (attn, y1, x4), got = _attn_fwd(q, kv, sinks, x3, weight("w_o"), gains["post_mix_g"],
                                    rider=gather(ride, {"w_gu1": (704, D)}))
    take(ride, got)

    (f1, x5, g1, u1, z1, pe1, dx6, loss_row), _ = _ffn_fwd(
        1, x4, gains["pre_ffn_g"], weight("w_gu1"), weight("w_down1"), gains["post_ffn_g"],
        head=(p3d, gains["ple_g"], weight("w_ple_gate1"), weight("w_ple_proj1"), gains["ple_post_g"], target))

    local = {}
    landed = {}
    fused = {}

    def local_grads(names):
        return [local[nm].reshape(_full_shape(BIGS[nm])) for nm in names]

    def pair_exchange(names):
        return _pair_exchange_rider([BIGS[nm] for nm in names], local_grads(names))

    def pair_sum(tag, names, lands):
        jobs = [_pair_sum_job(BIGS[nm], g, l) for nm, g, l in zip(names, local_grads(names), lands)]
        return [r[0] for r in _multi_call(f"pair_sum_{tag}", jobs, kc)]

    def scatter(names, sums):
        return _scatter_rider([BIGS[nm] for nm in names], sums)

    def keep(names, sums, got):
        for nm, s, l in zip(names, sums, got):
            landed[nm] = (s, l)

    (dx5, local["w_ple_gate1"], local["w_ple_proj1"], d_ple1, d_plepost1), _ = _ple_bwd(
        1, dx6, x5, z1, pe1, p3d, gains["ple_g"], weight("w_ple_gate1"), gains["ple_post_g"])

    group_a = ["w_ple_gate1", "w_ple_proj1"]
    (dx4, d_preffn1, d_postffn1, *scattered), lands_a = _ffn_bwd(
        1, dx5, x4, f1, g1, u1, gains["pre_ffn_g"], weight("w_gu1"), weight("w_down1"), gains["post_ffn_g"], kc,
        rider=pair_exchange(group_a))
    fused["w_gu1"], fused["w_down1"] = scattered[0:2], scattered[2:4]

    (dq, dkv, local["w_o"], d_postmix1, d_sinks), _ = _attn_bwd(
        dx4, y1, attn, q, kv, sinks, weight("w_o"), gains["post_mix_g"])
    dx3, local["w_q"], local["w_kv"], d_premix1, d_kvg = _qkv_bwd(
        dq, dkv, x3, dx4, gains["pre_mix_g"], kv_g2d, weight("w_q"), weight("w_kv"))

    group_b = ["w_o", "w_q", "w_kv"]
    (dx2, local["w_ple_gate0"], local["w_ple_proj0"], d_ple0, d_plepost0), lands_b = _ple_bwd(
        0, dx3, x2, z0, pe0, p3d, gains["ple_g"], weight("w_ple_gate0"), gains["ple_post_g"],
        rider=pair_exchange(group_b))
    group_ab = group_a + group_b
    sums_ab = pair_sum("ab", group_ab, lands_a + lands_b)

    group_c = ["w_ple_gate0", "w_ple_proj0"]
    (dx1, d_preffn0, d_postffn0, *scattered), got = _ffn_bwd(
        0, dx2, x1, f0, g0, u0, gains["pre_ffn_g"], weight("w_gu0"), weight("w_down0"), gains["post_ffn_g"], kc,
        rider=_both(pair_exchange(group_c), scatter(group_ab, sums_ab)))
    fused["w_gu0"], fused["w_down0"] = scattered[0:2], scattered[2:4]
    lands_c = got[:len(group_c)]
    keep(group_ab, sums_ab, got[len(group_c):])

    layers_of = lambda src: [t for t in BIGS.values() if t.src == src]
    own_scatter = ["w_gu", "w_down"]
    early = own_scatter + ["w_q", "w_o", "w_kv"]
    late = ["w_ple_gate", "w_ple_proj"]
    by_cols = lambda srcs: [src == "w_down" for src in srcs]
    jobs = [_chip_sum_fused_job(layers_of(src), fused, by_cols=src == "w_down") for src in own_scatter]
    jobs += [_chip_sum_job(layers_of(src), landed) for src in early if src not in own_scatter]
    jobs_c = [_pair_sum_job(BIGS[nm], g, l) for nm, g, l in zip(group_c, local_grads(group_c), lands_c)]
    sums = [r[0] for r in _multi_call("chip_sum_early", jobs + jobs_c, kc)]
    halves, sums_c = sums[:len(jobs)], sums[len(jobs):]
    (dx0, d_pool, d_scale, d_postmix0, d_premix0), got = _mixa_bwd(
        dx1, x2d, y0, gains["pre_mix_g"], weight("pool_w"), weight("pool_scale"), gains["post_mix_g"],
        rider=_both(scatter(group_c, sums_c), _share_rider(halves, by_cols(early))))
    keep(group_c, sums_c, got[:len(group_c)])
    full_grads = dict(zip(early, got[len(group_c):]))

    rows = [d_premix0, d_premix1, d_postmix0, d_postmix1, d_preffn0, d_preffn1, d_postffn0, d_postffn1,
            d_ple0, d_ple1, d_plepost0, d_plepost1, d_kvg, d_scale, d_sinks, loss_row]
    as2d = lambda a: a.reshape(1, D) if a.ndim == 1 else a
    halves = [r[0] for r in _multi_call("chip_sum_late", [_chip_sum_job(layers_of(src), landed) for src in late], kc)]
    (tot, g_pool), got = _small_all_reduce(rows, d_pool, rider=_share_rider(halves, by_cols(late)))
    full_grads.update(zip(late, got))
    full_grads["pool_w"] = g_pool
    loss, small = _small_adamw(tot, kc, {nm: as2d(weights[nm]) for nm in SMALL_NAMES},
                               {nm: as2d(m_in[nm]) for nm in SMALL_NAMES},
                               {nm: as2d(v_in[nm]) for nm in SMALL_NAMES})


    def adam_job(src):
        rb = layers_of(src)[0].rb // (1 if src == "pool_w" else 2)
        return _adamw_job(rb, shard_view(src, weights[src]), full_grads[src],
                          shard_view(src, m_in[src]), shard_view(src, v_in[src]))

    out = {"grad": {}, "delta": {}, "new_m": {}, "new_v": {}}
    results = dict(zip(BIG_SOURCES, _multi_call("adamw", [adam_job(src) for src in BIG_SOURCES], kc)))
    for src in BIG_SOURCES:
        shape = weights[src].shape
        for kind, a in zip(("grad", "delta", "new_m", "new_v"), results[src]):
            out[kind][src] = a.reshape(shape)
    for nm in SMALL_NAMES:
        shape = weights[nm].shape
        for kind, a in zip(("grad", "delta", "new_m", "new_v"), small[nm]):
            out[kind][nm] = a.reshape(shape)

    return (loss.reshape(()), dx0.reshape(x.shape),
            *[out["grad"][nm] for nm in order], *[out["delta"][nm] for nm in order],
            *[out["new_m"][nm] for nm in order], *[out["new_v"][nm] for nm in order])
```

```python
import collections

import jax
import jax.numpy as jnp
from jax import lax
from jax.experimental import pallas as pl
from jax.experimental.pallas import tpu as pltpu

D = 1024
FF = 2816
N_HEADS = 16
HEAD_DIM = 64
N_KV_HEADS = 4
GQA = N_HEADS // N_KV_HEADS
KVD = N_KV_HEADS * HEAD_DIM
PLE = 256
BLK = 128
WINDOWS = (2, 4, 8, 16)
POOL_G = 256
HALO = 16
EPS = 1e-6
NEG_INF = -1e30
ATT_SCALE = HEAD_DIM ** -0.5
SLOPES = tuple(2.0 ** (-8.0 * (h + 1) / N_HEADS) for h in range(N_HEADS))
N_CHIPS = 4
N_DEV = 8

LR, B1, B2, AEPS, WD, STEP = 0.001, 0.9, 0.999, 1e-08, 0.01, 10
BC1 = 1.0 - B1 ** STEP
BC2 = 1.0 - B2 ** STEP

BF = jnp.bfloat16
F32 = jnp.float32
MESH = pl.DeviceIdType.MESH
VMEM_LIMIT_V7X = 58 * 1024 * 1024
TM = 256
TM_FFN_BWD = 512
FF_CHUNK = 256
FF_HALF = FF // 2

VSPEC = pl.BlockSpec(memory_space=pltpu.VMEM)
SSPEC = pl.BlockSpec(memory_space=pltpu.SMEM)
ANYSPEC = pl.BlockSpec(memory_space=pl.ANY)


def _params(n_grid=0):
    sem = ("arbitrary",) * n_grid if n_grid else None
    return pltpu.CompilerParams(dimension_semantics=sem, vmem_limit_bytes=VMEM_LIMIT_V7X)


def _sds(shape, dtype=F32):
    return jax.ShapeDtypeStruct(tuple(shape), dtype)


Rider = collections.namedtuple("Rider", "arrays out_shapes aliases scratch start mid finish")
MID_NUM, MID_DEN = 5, 8


def _call(body, *, name, grid, in_specs, out_specs, out_shape, args, scratch_shapes=(), rider=None, prefetch=None):
    ni, no, ns = len(in_specs), len(out_specs), len(scratch_shapes)
    npre = 0 if prefetch is None else 1
    pre = [] if prefetch is None else [prefetch]
    if rider is None:
        rider = Rider([], [], {}, [], None, None, None)
    ri, ro = len(rider.arrays), len(rider.out_shapes)

    def full(*refs):
        pre_refs, refs = refs[:npre], refs[npre:]
        ins, refs = refs[:ni], refs[ni:]
        rins, refs = refs[:ri], refs[ri:]
        outs, refs = refs[:no], refs[no:]
        routs, refs = refs[:ro], refs[ro:]
        scr, rscr = refs[:ns], refs[ns:]
        ids = [pl.program_id(a) for a in range(len(grid))]
        first = ids[0] == 0
        last = ids[0] == grid[0] - 1
        for a in range(1, len(grid)):
            first = first & (ids[a] == 0)
            last = last & (ids[a] == grid[a] - 1)

        if rider.start is not None:
            @pl.when(first)
            def _():
                rider.start(rins, routs, rscr)

        if rider.mid is not None:
            assert len(grid) == 1

            @pl.when(ids[0] == (grid[0] * MID_NUM) // MID_DEN)
            def _():
                rider.mid(rins, routs, rscr)

        body(*pre_refs, *ins, *outs, *scr)

        if rider.finish is not None:
            @pl.when(last)
            def _():
                rider.finish(rins, routs, rscr)

    outs = pl.pallas_call(
        full, name=name,
        grid_spec=pltpu.PrefetchScalarGridSpec(
            num_scalar_prefetch=npre, grid=grid,
            in_specs=list(in_specs) + [ANYSPEC] * ri, out_specs=list(out_specs) + [ANYSPEC] * ro,
            scratch_shapes=list(scratch_shapes) + list(rider.scratch)),
        out_shape=list(out_shape) + list(rider.out_shapes),
        input_output_aliases={npre + ni + a: no + b for a, b in rider.aliases.items()},
        compiler_params=_params(len(grid)))(*pre, *args, *rider.arrays)
    return list(outs[:no]), list(outs[no:])


Job = collections.namedtuple("Job", "steps ins outs fn")


def _multi_call(name, jobs, kc, rider=None):
    n = max(job.steps for job in jobs)

    def clamped(index, steps):
        return lambda s, kc_ref: index(jnp.minimum(s, steps - 1), kc_ref)

    in_specs, out_specs, out_shape, args = [], [], [], []
    for job in jobs:
        for arr, block, index, *single in job.ins:
            mode = dict(pipeline_mode=pl.Buffered(1)) if single and single[0] else {}
            in_specs.append(pl.BlockSpec(block, clamped(index, job.steps), **mode))
            args.append(arr)
        for sds, block, index in job.outs:
            out_specs.append(pl.BlockSpec(block, clamped(index, job.steps)))
            out_shape.append(sds)
    n_in = len(args)

    def body(kc_ref, *refs):
        s = pl.program_id(0)
        i0, o0 = 0, n_in
        for job in jobs:
            ins, outs = refs[i0:i0 + len(job.ins)], refs[o0:o0 + len(job.outs)]
            i0, o0 = i0 + len(job.ins), o0 + len(job.outs)

            @pl.when(s < job.steps)
            def _():
                job.fn(s, kc_ref, ins, outs)

    outs, routs = _call(body, name=name, grid=(n,), in_specs=in_specs, out_specs=out_specs, out_shape=out_shape,
                        args=args, prefetch=kc, rider=rider)
    res, o0 = [], 0
    for job in jobs:
        res.append(outs[o0:o0 + len(job.outs)])
        o0 += len(job.outs)
    return res if rider is None else (res, routs)


def _rms_fwd(x, g):
    r = lax.rsqrt(jnp.mean(x * x, axis=-1, keepdims=True) + EPS)
    return x * r * g


def _rms_bwd(x, g, dy):
    r = lax.rsqrt(jnp.mean(x * x, axis=-1, keepdims=True) + EPS)
    xn = x * r
    dxn = dy * g
    dx = r * (dxn - xn * jnp.mean(dxn * xn, axis=-1, keepdims=True))
    return dx, dy * xn


def _rowsum(a):
    return jnp.sum(a, axis=0, keepdims=True)


def _sigmoid(z):
    return 1.0 / (1.0 + jnp.exp(-z))


def _dot(a, b):
    return jnp.dot(a, b, preferred_element_type=F32)


def _dot_nt(a, b):
    return lax.dot_general(a, b, (((1,), (1,)), ((), ())), preferred_element_type=F32)


def _dot_tn(a, b):
    return lax.dot_general(a, b, (((0,), (0,)), ((), ())), preferred_element_type=F32)


def _row_spec(tm, width=D):
    return pl.BlockSpec((tm, width), lambda i: (i, 0))


def _const_spec(shape):
    zeros = (0,) * len(shape)
    return pl.BlockSpec(tuple(shape), lambda *_: zeros)


def _pool_delta(he, pos):
    out = []
    for gi, w in enumerate(WINDOWS):
        hg = he[:, gi * POOL_G:(gi + 1) * POOL_G]
        s = hg
        k = 1
        while k < w:
            s = s + pltpu.roll(s, k, 0)
            k *= 2
        cnt = jnp.maximum(jnp.minimum(pos + 1, w), 1).astype(F32)
        out.append(s / cnt - hg)
    return out


def _load_with_halo_before(x_ref, i, tm):
    r0 = pl.multiple_of(i * tm, tm)
    hs = pl.multiple_of(jnp.maximum(i * tm - HALO, 0), 8)
    xh = jnp.where(i > 0, x_ref[pl.ds(hs, HALO), :], 0.0)
    xt = x_ref[pl.ds(r0, tm), :]
    return xt, jnp.concatenate([xh, xt], axis=0)


def _mixa_fwd_job(x, pre_g, pool_w, pool_scale, post_g):
    s_len = x.shape[0]

    def fn(i, kc_ref, ins, outs):
        x_ref, pg_ref, w_ref, sc_ref, qg_ref = ins
        y_ref, x1_ref = outs
        xt, xe = _load_with_halo_before(x_ref, i, TM)
        he = _rms_fwd(xe, pg_ref[0:1, :])
        pos = i * TM - HALO + lax.broadcasted_iota(jnp.int32, (TM + HALO, 1), 0)
        ds = _pool_delta(he, pos)
        ys = [_dot(ds[gi][HALO:, :].astype(BF), w_ref[gi]) for gi in range(len(WINDOWS))]
        y = jnp.concatenate(ys, axis=1) * sc_ref[...]
        y_ref[...] = y
        x1_ref[...] = xt + _rms_fwd(y, qg_ref[0:1, :])

    def whole(a):
        zeros = (0,) * a.ndim
        return (a, a.shape, lambda j, kc_ref: zeros, True)

    rows = lambda j, kc_ref: (j, 0)
    return Job(s_len // TM, [whole(a) for a in (x, pre_g, pool_w, pool_scale, post_g)],
               [(_sds((s_len, D)), (TM, D), rows), (_sds((s_len, D)), (TM, D), rows)], fn)


def _mixa_bwd(dx1, x, y, pre_g, pool_w, pool_scale, post_g, rider=None):
    s_len = x.shape[0]
    n = s_len // TM
    ng = len(WINDOWS)

    def body(dx_ref, x_ref, y_ref, pg_ref, w_ref, sc_ref, qg_ref,
             dx0_ref, dw_ref, dsc_ref, dqg_ref, dpg_ref, wacc):
        i = pl.program_id(0)

        @pl.when(i == 0)
        def _():
            wacc[...] = jnp.zeros_like(wacc)
            dsc_ref[...] = jnp.zeros_like(dsc_ref)
            dqg_ref[...] = jnp.zeros_like(dqg_ref)
            dpg_ref[...] = jnp.zeros_like(dpg_ref)

        r0 = pl.multiple_of(i * TM, TM)
        xt, xe = _load_with_halo_before(x_ref, i, TM)
        he = _rms_fwd(xe, pg_ref[0:1, :])
        pos_b = i * TM - HALO + lax.broadcasted_iota(jnp.int32, (TM + HALO, 1), 0)
        ds = _pool_delta(he, pos_b)

        last = i == n - 1
        a0 = pl.multiple_of(jnp.minimum(i * TM + TM, s_len - HALO), 8)
        ye = jnp.concatenate([y_ref[pl.ds(r0, TM), :], y_ref[pl.ds(a0, HALO), :]], axis=0)
        dt = dx_ref[pl.ds(r0, TM), :]
        de = jnp.concatenate([dt, jnp.where(last, 0.0, dx_ref[pl.ds(a0, HALO), :])], axis=0)
        dye, prod = _rms_bwd(ye, qg_ref[0:1, :], de)
        dqg_ref[...] += _rowsum(prod[:TM, :])
        dys = dye * sc_ref[...]
        pos_a = i * TM + lax.broadcasted_iota(jnp.int32, (TM + HALO, 1), 0)

        dhs, dscs = [], []
        for gi, w in enumerate(WINDOWS):
            sl = slice(gi * POOL_G, (gi + 1) * POOL_G)
            wg = w_ref[gi]
            dys_g = dys[:, sl].astype(BF)
            d_g = ds[gi][HALO:, :].astype(BF)
            ypre = _dot(d_g, wg)
            dscs.append(_rowsum(dye[:TM, sl] * ypre))
            wacc[gi] += _dot_tn(d_g, dys_g[:TM, :])
            dd = _dot_nt(dys_g, wg)
            cnt = jnp.minimum(pos_a + 1, w).astype(F32)
            a = dd / cnt
            k = 1
            while k < w:
                a = a + pltpu.roll(a, TM + HALO - k, 0)
                k *= 2
            dhs.append(a[:TM, :] - dd[:TM, :])
        dsc_ref[...] += jnp.concatenate(dscs, axis=1)
        dh = jnp.concatenate(dhs, axis=1)
        dxp, prod2 = _rms_bwd(xt, pg_ref[0:1, :], dh)
        dpg_ref[...] += _rowsum(prod2)
        dx0_ref[...] = dt + dxp

        @pl.when(last)
        def _():
            dw_ref[...] = wacc[...].astype(BF)

    return _call(
        body, name="mixa_bwd", grid=(n,), in_specs=[VSPEC] * 7,
        out_specs=[_row_spec(TM), _const_spec((ng, POOL_G, POOL_G)), _const_spec((1, D)),
                   _const_spec((1, D)), _const_spec((1, D))],
        out_shape=[_sds((s_len, D)), _sds((ng, POOL_G, POOL_G), BF), _sds((1, D)), _sds((1, D)), _sds((1, D))],
        scratch_shapes=[pltpu.VMEM((ng, POOL_G, POOL_G), F32)],
        args=[dx1, x, y, pre_g, pool_w, pool_scale, post_g], rider=rider)


def _ple_math(layer, x, p_blk, g_ref, wg_ref, wp_ref, qg_ref):
    r = _rms_fwd(x, g_ref[layer:layer + 1, :]).astype(BF)
    z = _dot(r, wg_ref[...])
    pe = _dot(p_blk.astype(BF), wp_ref[...])
    return z, pe, x + _rms_fwd(pe * _sigmoid(z), qg_ref[layer:layer + 1, :])


def _ffn_fwd(layer, x1, pre_g, wgu, wd, post_g, rider=None, head=None):
    s_len = x1.shape[0]

    def body(*refs):
        if head:
            (x_ref, pg_ref, wgu_ref, wd_ref, qg_ref, p_ref, eg_ref, wg_ref, wp_ref, eq_ref, t_ref,
             f_ref, x2_ref, g_ref, u_ref, z_ref, pe_ref, dx_ref, lv_ref) = refs
        else:
            x_ref, pg_ref, wgu_ref, wd_ref, qg_ref, f_ref, x2_ref, g_ref, u_ref = refs
        x = x_ref[...]
        h = _rms_fwd(x, pg_ref[layer:layer + 1, :]).astype(BF)
        f = jnp.zeros((TM, D), F32)
        for c in range(FF // FF_HALF):
            cols = slice(c * FF_HALF, (c + 1) * FF_HALF)
            g = _dot(h, wgu_ref[0, :, cols])
            u = _dot(h, wgu_ref[1, :, cols])
            g_ref[:, cols] = g.astype(BF)
            u_ref[:, cols] = u.astype(BF)
            act = g * _sigmoid(g) * u
            f = f + _dot(act.astype(BF), wd_ref[cols, :])
        f_ref[...] = f
        x2 = x + _rms_fwd(f, qg_ref[layer:layer + 1, :])
        x2_ref[...] = x2
        if head:
            @pl.when(pl.program_id(0) == 0)
            def _():
                lv_ref[...] = jnp.zeros_like(lv_ref)
            z, pe, x3 = _ple_math(layer, x2, p_ref[...], eg_ref, wg_ref, wp_ref, eq_ref)
            z_ref[...] = z
            pe_ref[...] = pe
            err = x3 - t_ref[...]
            dx_ref[...] = err * (1.0 / D)
            lv_ref[...] += _rowsum(err * err)

    in_specs = [_row_spec(TM), VSPEC, VSPEC, VSPEC, VSPEC]
    args = [x1, pre_g, wgu, wd, post_g]
    out_specs = [_row_spec(TM), _row_spec(TM), _row_spec(TM, FF), _row_spec(TM, FF)]
    out_shape = [_sds((s_len, D)), _sds((s_len, D)), _sds((s_len, FF), BF), _sds((s_len, FF), BF)]
    if head:
        p, ple_g, w_gate, w_proj, ple_post_g, target = head
        in_specs += [pl.BlockSpec((None, TM, PLE), lambda i: (layer, i, 0)), VSPEC, VSPEC, VSPEC, VSPEC, _row_spec(TM)]
        args += [p, ple_g, w_gate, w_proj, ple_post_g, target]
        out_specs += [_row_spec(TM), _row_spec(TM), _row_spec(TM), _const_spec((1, D))]
        out_shape += [_sds((s_len, D))] * 3 + [_sds((1, D))]
    return _call(body, name=f"ffn_fwd{layer}", grid=(s_len // TM,), in_specs=in_specs, out_specs=out_specs,
                 out_shape=out_shape, args=args, rider=rider)


GU_PIECE = 128
DN_PIECE = 64
DN_SLOT = FF // N_CHIPS
HALF_D = D // 2
CHUNK_STRIDE = 6
CHUNK_START = (1, 7, 4, 10)


def _ffn_bwd(layer, dx2, x1, f, g_pre, u_pre, pre_g, wgu, wd, post_g, kc, rider=None):
    s_len = x1.shape[0]
    tm = TM_FFN_BWD
    n = s_len // tm
    nc = FF // FF_CHUNK
    n_gu, n_dn = FF_CHUNK // GU_PIECE, FF_CHUNK // DN_PIECE
    n_pieces = 2 * n_gu + n_dn
    n_blk = FF_HALF // GU_PIECE

    def edge_rows(c, i, kc_ref):
        return (jnp.where((c == 0) | (c == nc - 1), i, n - 1), 0)

    def chunk_at(c, kc_ref):
        k = kc_ref[0]
        start = jnp.where(k == 0, CHUNK_START[0], jnp.where(k == 1, CHUNK_START[1],
                                                            jnp.where(k == 2, CHUNK_START[2], CHUNK_START[3])))
        return ((c + start) * CHUNK_STRIDE) % nc

    def exchange(kc_ref, c, accg, accu, accd, own_gu_ref, land_gu_ref, own_dn_ref, land_dn_ref,
                 pl_gu, pl_dn, sib_gu, sib_dn, mine_gu, mine_dn, sum_gu, sum_dn,
                 psend, precv, ssend, lsem, rrecv):
        x, y, core = lax.axis_index("x"), lax.axis_index("y"), lax.axis_index("c")
        lower = core == 0

        def pair_copy(cc, part):
            p = cc % 2
            src, dst = ((sib_gu, pl_gu), (sib_dn, pl_dn))[part]
            return pltpu.make_async_remote_copy(src.at[p], dst.at[cc], psend.at[p, part], precv.at[cc, part],
                                                device_id=(x, y, 1 - core), device_id_type=MESH)

        def scatter(cc, wait):
            p = cc % 2
            hidden = chunk_at(cc, kc_ref) * FF_CHUNK

            assert n_gu == 2
            k0, k1 = hidden // FF_HALF, (hidden + GU_PIECE) // FF_HALF
            blk = (hidden - k0 * FF_HALF) // GU_PIECE
            for gu in range(2):
                @pl.when(k0 == k1)
                def _():
                    piece(p, wait, 2 * gu, sum_gu.at[p, gu], k0 + 2 * gu, 0, (pl.ds(blk, 2),))

                @pl.when(k0 != k1)
                def _():
                    piece(p, wait, 2 * gu, sum_gu.at[p, gu, 0], k0 + 2 * gu, 0, (blk,))
                    piece(p, wait, 2 * gu + 1, sum_gu.at[p, gu, 1], k1 + 2 * gu, 0, (0,))

            kd = hidden // DN_SLOT
            off = pl.multiple_of(hidden - kd * DN_SLOT, DN_PIECE)
            m = jnp.minimum((DN_SLOT - off) // DN_PIECE, n_dn)
            for mm in range(1, n_dn + 1):
                @pl.when(m == mm)
                def _():
                    rows = mm * DN_PIECE
                    piece(p, wait, 2 * n_gu, sum_dn.at[p, pl.ds(0, rows), :], kd, 1, (pl.ds(off, rows), slice(None)))
                    if mm < n_dn:
                        piece(p, wait, 2 * n_gu + 1, sum_dn.at[p, pl.ds(rows, FF_CHUNK - rows), :], kd + 1, 1,
                              (pl.ds(0, FF_CHUNK - rows), slice(None)))

        def piece(p, wait, pi, src, k, t, where):
            own_ref, land_ref = ((own_gu_ref, land_gu_ref), (own_dn_ref, land_dn_ref))[t]
            kx, ky = k // 2, k % 2
            fx, fy = (kx != x).astype(jnp.int32), (ky != y).astype(jnp.int32)
            local = (fx + fy) == 0
            j = jnp.maximum(fx + 2 * fy - 1, 0)

            @pl.when(local)
            def _():
                cp = pltpu.make_async_copy(src, own_ref.at[where], lsem.at[p, pi])
                if wait:
                    cp.wait()
                else:
                    cp.start()

            @pl.when(jnp.logical_not(local))
            def _():
                cp = pltpu.make_async_remote_copy(src, land_ref.at[(j,) + where], ssend.at[p, pi],
                                                  rrecv.at[t, j], device_id=(kx, ky, core), device_id_type=MESH)
                if wait:
                    cp.wait_send()
                else:
                    cp.start()

        def add_and_scatter(cc):
            p = cc % 2
            pair_copy(cc, 0).wait_recv()
            pair_copy(cc, 1).wait_recv()
            s_gu = (mine_gu[...] + pl_gu[cc].astype(F32)).astype(BF)
            for hc in range(n_gu):
                sum_gu[p, :, hc] = s_gu[:, :, hc * GU_PIECE:(hc + 1) * GU_PIECE]
            sum_dn[p] = (mine_dn[...] + pl_dn[cc].astype(F32)).astype(BF)
            scatter(cc, wait=False)

        @pl.when(c >= 1)
        def _():
            @pl.when(c >= 3)
            def _():
                scatter(c - 3, wait=True)
            add_and_scatter(c - 1)

        @pl.when(c >= 2)
        def _():
            pair_copy(c - 2, 0).wait_send()
            pair_copy(c - 2, 1).wait_send()

        p = c % 2
        my_rows = pl.ds(pl.multiple_of(core * HALF_D, HALF_D), HALF_D)
        sib_rows = pl.ds(pl.multiple_of((1 - core) * HALF_D, HALF_D), HALF_D)
        d_v = accd[...]
        sib_gu[p, 0] = accg[sib_rows, :].astype(BF)
        sib_gu[p, 1] = accu[sib_rows, :].astype(BF)
        sib_dn[p] = jnp.where(lower, d_v[:, HALF_D:], d_v[:, :HALF_D]).astype(BF)
        mine_gu[0] = accg[my_rows, :]
        mine_gu[1] = accu[my_rows, :]
        mine_dn[...] = jnp.where(lower, d_v[:, :HALF_D], d_v[:, HALF_D:])
        pair_copy(c, 0).start()
        pair_copy(c, 1).start()

        @pl.when(c == nc - 1)
        def _():
            scatter(nc - 3, wait=True)
            add_and_scatter(nc - 1)
            for cc in (nc - 2, nc - 1):
                pair_copy(cc, 0).wait_send()
                pair_copy(cc, 1).wait_send()
                scatter(cc, wait=True)
            for t, land_ref in enumerate((land_gu_ref, land_dn_ref)):
                for j in range(N_CHIPS - 1):
                    pltpu.make_async_remote_copy(land_ref.at[j], land_ref.at[j], ssend.at[0, 0], rrecv.at[t, j],
                                                 device_id=(x, y, core), device_id_type=MESH).wait_recv()

    def body(kc_ref, dx_ref, x_ref, f_ref, gp_ref, up_ref, pg_ref, wgu_ref, wd_ref, qg_ref,
             dx1_ref, dpg_ref, dqg_ref, own_gu_ref, land_gu_ref, own_dn_ref, land_dn_ref,
             h_s, df_s, dh_s, accg, accu, accd, *comm):
        c = pl.program_id(0)
        i = pl.program_id(1)
        rows = pl.ds(pl.multiple_of(i * tm, tm), tm)
        pg = pg_ref[layer:layer + 1, :]

        @pl.when((c == 0) & (i == 0))
        def _():
            dpg_ref[...] = jnp.zeros_like(dpg_ref)
            dqg_ref[...] = jnp.zeros_like(dqg_ref)

        @pl.when(c == 0)
        def _():
            h_s[rows, :] = _rms_fwd(x_ref[...], pg).astype(BF)
            df, prod = _rms_bwd(f_ref[...], qg_ref[layer:layer + 1, :], dx_ref[...])
            df_s[rows, :] = df.astype(BF)
            dqg_ref[...] += _rowsum(prod)

        @pl.when(i == 0)
        def _():
            accg[...] = jnp.zeros_like(accg)
            accu[...] = jnp.zeros_like(accu)
            accd[...] = jnp.zeros_like(accd)

        h = h_s[rows, :]
        df = df_s[rows, :]
        wg = wgu_ref[0]
        wu = wgu_ref[1]
        g = gp_ref[...].astype(F32)
        u = up_ref[...].astype(F32)
        sg = _sigmoid(g)
        a = g * sg
        dact = _dot_nt(df, wd_ref[...])
        accd[...] += _dot_tn((a * u).astype(BF), df)
        du = (dact * a).astype(BF)
        dg = (dact * u * (sg * (1.0 + g * (1.0 - sg)))).astype(BF)
        accg[...] += _dot_tn(h, dg)
        accu[...] += _dot_tn(h, du)
        dh = _dot_nt(dg, wg) + _dot_nt(du, wu)

        @pl.when(c == 0)
        def _():
            dh_s[rows, :] = dh

        @pl.when((c > 0) & (c < nc - 1))
        def _():
            dh_s[rows, :] += dh

        @pl.when(c == nc - 1)
        def _():
            dxp, prod = _rms_bwd(x_ref[...], pg, dh_s[rows, :] + dh)
            dpg_ref[...] += _rowsum(prod)
            dx1_ref[...] = dx_ref[...] + dxp

        @pl.when(i == n - 1)
        def _():
            exchange(kc_ref, c, accg, accu, accd, own_gu_ref, land_gu_ref, own_dn_ref, land_dn_ref, *comm)

    dma = pltpu.SemaphoreType.DMA
    return _call(
        body, name=f"ffn_bwd{layer}", grid=(nc, n),
        in_specs=[pl.BlockSpec((tm, D), edge_rows), pl.BlockSpec((tm, D), edge_rows),
                  pl.BlockSpec((tm, D), lambda c, i, kc_ref: (jnp.where(c == 0, i, n - 1), 0),
                               pipeline_mode=pl.Buffered(1)),
                  pl.BlockSpec((tm, FF_CHUNK), lambda c, i, kc_ref: (i, chunk_at(c, kc_ref))),
                  pl.BlockSpec((tm, FF_CHUNK), lambda c, i, kc_ref: (i, chunk_at(c, kc_ref))),
                  VSPEC,
                  pl.BlockSpec((2, D, FF_CHUNK), lambda c, i, kc_ref: (0, 0, chunk_at(c, kc_ref))),
                  pl.BlockSpec((FF_CHUNK, D), lambda c, i, kc_ref: (chunk_at(c, kc_ref), 0)),
                  VSPEC],
        out_specs=[pl.BlockSpec((tm, D), lambda c, i, kc_ref: (jnp.where(c == nc - 1, i, 0), 0)),
                   _const_spec((1, D)), _const_spec((1, D)), ANYSPEC, ANYSPEC, ANYSPEC, ANYSPEC],
        out_shape=[_sds((s_len, D)), _sds((1, D)), _sds((1, D)),
                   _sds((n_blk, HALF_D, GU_PIECE), BF), _sds((N_CHIPS - 1, n_blk, HALF_D, GU_PIECE), BF),
                   _sds((DN_SLOT, HALF_D), BF), _sds((N_CHIPS - 1, DN_SLOT, HALF_D), BF)],
        scratch_shapes=[pltpu.VMEM((s_len, D), BF), pltpu.VMEM((s_len, D), BF), pltpu.VMEM((s_len, D), F32),
                        pltpu.VMEM((D, FF_CHUNK), F32), pltpu.VMEM((D, FF_CHUNK), F32),
                        pltpu.VMEM((FF_CHUNK, D), F32),
                        pltpu.VMEM((nc, 2, HALF_D, FF_CHUNK), BF), pltpu.VMEM((nc, FF_CHUNK, HALF_D), BF),
                        pltpu.VMEM((2, 2, HALF_D, FF_CHUNK), BF), pltpu.VMEM((2, FF_CHUNK, HALF_D), BF),
                        pltpu.VMEM((2, HALF_D, FF_CHUNK), F32), pltpu.VMEM((FF_CHUNK, HALF_D), F32),
                        pltpu.VMEM((2, 2, n_gu, HALF_D, GU_PIECE), BF), pltpu.VMEM((2, FF_CHUNK, HALF_D), BF),
                        dma((2, 2)), dma((nc, 2)), dma((2, n_pieces)), dma((2, n_pieces)), dma((2, N_CHIPS - 1))],
        args=[dx2, x1, f, g_pre, u_pre, pre_g, wgu, wd, post_g], rider=rider, prefetch=kc)


def _ple_fwd(layer, x2, p, ple_g, w_gate, w_proj, post_g, qkv=None, rider=None):
    s_len = x2.shape[0]

    def body(*refs):
        if qkv:
            (x_ref, p_ref, g_ref, wg_ref, wp_ref, qg_ref, ng_ref, kg_ref, wq_ref, wkv_ref,
             z_ref, pe_ref, x3_ref, q_ref, kv_ref) = refs
        else:
            x_ref, p_ref, g_ref, wg_ref, wp_ref, qg_ref, z_ref, pe_ref, x3_ref = refs
        z, pe, x3 = _ple_math(layer, x_ref[...], p_ref[...], g_ref, wg_ref, wp_ref, qg_ref)
        z_ref[...] = z
        pe_ref[...] = pe
        x3_ref[...] = x3
        if qkv:
            q_ref[...] = _dot(_rms_fwd(x3, ng_ref[layer + 1:layer + 2, :]).astype(BF), wq_ref[...]).astype(BF)
            kv_ref[...] = _dot(_rms_fwd(x3, kg_ref[...]).astype(BF), wkv_ref[...]).astype(BF)

    p_spec = pl.BlockSpec((None, TM, PLE), lambda i: (layer, i, 0))
    in_specs = [_row_spec(TM), p_spec, VSPEC, VSPEC, VSPEC, VSPEC]
    args = [x2, p, ple_g, w_gate, w_proj, post_g]
    out_specs = [_row_spec(TM), _row_spec(TM), _row_spec(TM)]
    out_shape = [_sds((s_len, D))] * 3
    if qkv:
        in_specs += [VSPEC] * 4
        args += list(qkv)
        out_specs += [_row_spec(TM), _row_spec(TM, 2 * KVD)]
        out_shape += [_sds((s_len, D), BF), _sds((s_len, 2 * KVD), BF)]
    return _call(body, name=f"ple_fwd{layer}", grid=(s_len // TM,), in_specs=in_specs, out_specs=out_specs,
                 out_shape=out_shape, args=args, rider=rider)


def _ple_bwd(layer, dx3, x2, z, pe, p, ple_g, w_gate, post_g, rider=None):
    s_len = x2.shape[0]
    n = s_len // TM

    def body(dx_ref, x_ref, z_ref, pe_ref, p_ref, g_ref, wg_ref, qg_ref,
             dx2_ref, dwg_ref, dwp_ref, dg_ref, dqg_ref, gacc, pacc):
        i = pl.program_id(0)

        @pl.when(i == 0)
        def _():
            gacc[...] = jnp.zeros_like(gacc)
            pacc[...] = jnp.zeros_like(pacc)
            dg_ref[...] = jnp.zeros_like(dg_ref)
            dqg_ref[...] = jnp.zeros_like(dqg_ref)

        dx = dx_ref[...]
        x = x_ref[...]
        pe_v = pe_ref[...]
        gate = _sigmoid(z_ref[...])
        de, prod = _rms_bwd(pe_v * gate, qg_ref[layer:layer + 1, :], dx)
        dqg_ref[...] += _rowsum(prod)
        dpe = (de * gate).astype(BF)
        dz = (de * pe_v * gate * (1.0 - gate)).astype(BF)
        pacc[...] += _dot_tn(p_ref[...].astype(BF), dpe)
        g = g_ref[layer:layer + 1, :]
        r = _rms_fwd(x, g).astype(BF)
        gacc[...] += _dot_tn(r, dz)
        dr = _dot_nt(dz, wg_ref[...])
        dxp, prod2 = _rms_bwd(x, g, dr)
        dg_ref[...] += _rowsum(prod2)
        dx2_ref[...] = dx + dxp

        @pl.when(i == n - 1)
        def _():
            dwg_ref[...] = gacc[...].astype(BF)
            dwp_ref[...] = pacc[...].astype(BF)

    p_spec = pl.BlockSpec((None, TM, PLE), lambda i: (layer, i, 0))
    return _call(
        body, name=f"ple_bwd{layer}", grid=(n,),
        in_specs=[_row_spec(TM), _row_spec(TM), _row_spec(TM), _row_spec(TM), p_spec, VSPEC, VSPEC, VSPEC],
        out_specs=[_row_spec(TM), _const_spec((D, D)), _const_spec((PLE, D)), _const_spec((1, D)), _const_spec((1, D))],
        out_shape=[_sds((s_len, D)), _sds((D, D), BF), _sds((PLE, D), BF), _sds((1, D)), _sds((1, D))],
        scratch_shapes=[pltpu.VMEM((D, D), F32), pltpu.VMEM((PLE, D), F32)],
        args=[dx3, x2, z, pe, p, ple_g, w_gate, post_g], rider=rider)


def _qkv_bwd(dq, dkv, x3, dx4, q_g, kv_g, w_q, w_kv):
    s_len = x3.shape[0]
    n = s_len // TM

    def body(dq_ref, dkv_ref, x_ref, dx_ref, qg_ref, kg_ref, wq_ref, wkv_ref,
             dx3_ref, dwq_ref, dwkv_ref, dqg_ref, dkg_ref, qacc, kacc):
        i = pl.program_id(0)

        @pl.when(i == 0)
        def _():
            qacc[...] = jnp.zeros_like(qacc)
            kacc[...] = jnp.zeros_like(kacc)
            dqg_ref[...] = jnp.zeros_like(dqg_ref)
            dkg_ref[...] = jnp.zeros_like(dkg_ref)

        x = x_ref[...]
        qg = qg_ref[1:2, :]
        kg = kg_ref[...]
        dq_v = dq_ref[...]
        dkv_v = dkv_ref[...].astype(BF)
        qacc[...] += _dot_tn(_rms_fwd(x, qg).astype(BF), dq_v)
        kacc[...] += _dot_tn(_rms_fwd(x, kg).astype(BF), dkv_v)
        dxq, prod_q = _rms_bwd(x, qg, _dot_nt(dq_v, wq_ref[...]))
        dxk, prod_k = _rms_bwd(x, kg, _dot_nt(dkv_v, wkv_ref[...]))
        dqg_ref[...] += _rowsum(prod_q)
        dkg_ref[...] += _rowsum(prod_k)
        dx3_ref[...] = dx_ref[...] + dxq + dxk

        @pl.when(i == n - 1)
        def _():
            dwq_ref[...] = qacc[...].astype(BF)
            dwkv_ref[...] = kacc[...].astype(BF)

    outs, _ = _call(
        body, name="qkv_bwd", grid=(n,),
        in_specs=[_row_spec(TM), _row_spec(TM, 2 * KVD), _row_spec(TM), _row_spec(TM), VSPEC, VSPEC, VSPEC, VSPEC],
        out_specs=[_row_spec(TM), _const_spec((D, D)), _const_spec((D, 2 * KVD)),
                   _const_spec((1, D)), _const_spec((1, D))],
        out_shape=[_sds((s_len, D)), _sds((D, D), BF), _sds((D, 2 * KVD), BF), _sds((1, D)), _sds((1, D))],
        scratch_shapes=[pltpu.VMEM((D, D), F32), pltpu.VMEM((D, 2 * KVD), F32)],
        args=[dq, dkv, x3, dx4, q_g, kv_g, w_q, w_kv])
    return outs


def _attn_group(i, q, kvw, sink_ref, g):
    rows = GQA * BLK
    heads = [GQA * g + j for j in range(GQA)]
    off = jnp.where(i > 0, BLK, 0)
    row = lax.broadcasted_iota(jnp.int32, (rows, 2 * BLK), 0)
    rel = (row % BLK) - lax.broadcasted_iota(jnp.int32, (rows, 2 * BLK), 1) + off
    valid = (rel >= 0) & (rel < BLK)
    head_of_row = lax.broadcasted_iota(jnp.int32, (rows, 1), 0) // BLK
    slope = jnp.zeros((rows, 1), F32)
    sink = jnp.zeros((rows, 1), F32)
    for j, h in enumerate(heads):
        slope = jnp.where(head_of_row == j, SLOPES[h], slope)
        sink = jnp.where(head_of_row == j, sink_ref[0, h], sink)
    qs = jnp.concatenate([q[:, h * HEAD_DIM:(h + 1) * HEAD_DIM] for h in heads], axis=0)
    k = kvw[:, g * HEAD_DIM:(g + 1) * HEAD_DIM]
    v = kvw[:, KVD + g * HEAD_DIM:KVD + (g + 1) * HEAD_DIM]
    s = _dot_nt(qs, k) * ATT_SCALE - slope * rel.astype(F32)
    s = jnp.where(valid, s, NEG_INF)
    m = jnp.maximum(jnp.max(s, axis=-1, keepdims=True), sink)
    e = jnp.exp(s - m)
    es = jnp.exp(sink - m)
    inv = 1.0 / (jnp.sum(e, axis=-1, keepdims=True) + es)
    return e * inv, es * inv, qs, k, v


def _unstack_heads(stacked):
    return [stacked[j * BLK:(j + 1) * BLK, :] for j in range(GQA)]


def _kv_window(kv_ref, i):
    ks = pl.multiple_of(jnp.maximum(i * BLK - BLK, 0), BLK)
    return ks, kv_ref[pl.ds(ks, 2 * BLK), :]


def _attn_fwd(q, kv, sinks, x3, w_o, post_g, rider=None):
    s_len = q.shape[0]

    def body(q_ref, kv_ref, sk_ref, x_ref, wo_ref, g_ref, a_ref, y_ref, x4_ref):
        i = pl.program_id(0)
        _, kvw = _kv_window(kv_ref, i)
        q = q_ref[...]
        outs = []
        for g in range(N_KV_HEADS):
            p, _, _, _, v = _attn_group(i, q, kvw, sk_ref, g)
            outs += _unstack_heads(_dot(p.astype(BF), v))
        attn = jnp.concatenate(outs, axis=1)
        a_ref[...] = attn
        y = _dot(attn.astype(BF), wo_ref[...])
        y_ref[...] = y
        x4_ref[...] = x_ref[...] + _rms_fwd(y, g_ref[1:2, :])

    return _call(body, name="attn_fwd", grid=(s_len // BLK,),
                 in_specs=[_row_spec(BLK), VSPEC, SSPEC, _row_spec(BLK), VSPEC, VSPEC],
                 out_specs=[_row_spec(BLK)] * 3, out_shape=[_sds((s_len, D))] * 3,
                 args=[q, kv, sinks, x3, w_o, post_g], rider=rider)


ATT_STEP_BLOCKS = 2


def _attn_bwd(dx4, y, attn, q, kv, sinks, w_o, post_g, rider=None):
    s_len = q.shape[0]
    rows = ATT_STEP_BLOCKS * BLK
    n = s_len // rows

    def body(dx_ref, y_ref, a_ref, q_ref, kv_ref, sk_ref, wo_ref, g_ref,
             dq_ref, dkv_ref, dwo_ref, dg_ref, dsk_ref, wacc):
        i = pl.program_id(0)

        @pl.when(i == 0)
        def _():
            dkv_ref[...] = jnp.zeros_like(dkv_ref)
            wacc[...] = jnp.zeros_like(wacc)
            dg_ref[...] = jnp.zeros_like(dg_ref)
            dsk_ref[...] = jnp.zeros_like(dsk_ref)

        dy, prod = _rms_bwd(y_ref[...], g_ref[1:2, :], dx_ref[...])
        dg_ref[...] += _rowsum(prod)
        dyb = dy.astype(BF)
        attn_all = a_ref[...]
        wacc[...] += _dot_tn(attn_all.astype(BF), dyb)
        d_o_all = _dot_nt(dyb, wo_ref[...])
        q_all = q_ref[...]
        lane = lax.broadcasted_iota(jnp.int32, (1, D), 1)
        dsk = jnp.zeros((1, D), F32)
        for sub in range(ATT_STEP_BLOCKS):
            blk = i * ATT_STEP_BLOCKS + sub
            sl = slice(sub * BLK, (sub + 1) * BLK)
            d_o, q = d_o_all[sl, :], q_all[sl, :]
            dod = d_o * attn_all[sl, :]
            ks, kvw = _kv_window(kv_ref, blk)
            dqs, dks, dvs = [], [], []
            for g in range(N_KV_HEADS):
                p, ps, qs, k, v = _attn_group(blk, q, kvw, sk_ref, g)
                cols = [slice((GQA * g + j) * HEAD_DIM, (GQA * g + j + 1) * HEAD_DIM) for j in range(GQA)]
                do_s = jnp.concatenate([d_o[:, c] for c in cols], axis=0).astype(BF)
                dsum = jnp.concatenate([jnp.sum(dod[:, c], axis=-1, keepdims=True) for c in cols], axis=0)
                dp = _dot_nt(do_s, v)
                dsb = (p * (dp - dsum) * ATT_SCALE).astype(BF)
                sink_part = ps * dsum
                for j in range(GQA):
                    dsk = dsk + jnp.where(lane == GQA * g + j, -_rowsum(sink_part[j * BLK:(j + 1) * BLK, :]), 0.0)
                dqs += _unstack_heads(_dot(dsb, k))
                dks.append(_dot_tn(dsb, qs))
                dvs.append(_dot_tn(p.astype(BF), do_s))
            dq_ref[sl, :] = jnp.concatenate(dqs, axis=1).astype(BF)
            dkv_ref[pl.ds(ks, 2 * BLK), :] += jnp.concatenate(dks + dvs, axis=1)
        dsk_ref[...] += dsk

        @pl.when(i == n - 1)
        def _():
            dwo_ref[...] = wacc[...].astype(BF)

    return _call(
        body, name="attn_bwd", grid=(n,),
        in_specs=[_row_spec(rows), _row_spec(rows), _row_spec(rows), _row_spec(rows), VSPEC, SSPEC, VSPEC, VSPEC],
        out_specs=[_row_spec(rows), _const_spec((s_len, 2 * KVD)), _const_spec((D, D)),
                   _const_spec((1, D)), _const_spec((1, D))],
        out_shape=[_sds((s_len, D), BF), _sds((s_len, 2 * KVD)), _sds((D, D), BF), _sds((1, D)), _sds((1, D))],
        scratch_shapes=[pltpu.VMEM((D, D), F32)],
        args=[dx4, y, attn, q, kv, sinks, w_o, post_g], rider=rider)


Big = collections.namedtuple("Big", "name src layer L A R C rb")


def _bigs():
    out = {"pool_w": Big("pool_w", "pool_w", None, 4, 4, POOL_G // N_CHIPS, POOL_G, 32)}
    for l in range(2):
        out[f"w_gu{l}"] = Big(f"w_gu{l}", "w_gu", l, 1, 2, D, FF_HALF, 256)
        out[f"w_down{l}"] = Big(f"w_down{l}", "w_down", l, 1, 4, FF // N_CHIPS, D, 352)
        out[f"w_ple_gate{l}"] = Big(f"w_ple_gate{l}", "w_ple_gate", l, 1, 4, D // N_CHIPS, D, 128)
        out[f"w_ple_proj{l}"] = Big(f"w_ple_proj{l}", "w_ple_proj", l, 1, 1, PLE, D // N_CHIPS, 128)
    out["w_q"] = Big("w_q", "w_q", None, 1, 4, D // N_CHIPS, D, 128)
    out["w_o"] = Big("w_o", "w_o", None, 1, 4, D // N_CHIPS, D, 128)
    out["w_kv"] = Big("w_kv", "w_kv", None, 1, 4, D // N_CHIPS, 2 * KVD, 128)
    return out


BIGS = _bigs()
POOL_SCALE = Big("pool_scale", "pool_scale", None, 1, 1, 1, D // N_CHIPS, 1)
BIG_SOURCES = ("w_gu", "w_down", "w_ple_gate", "w_ple_proj", "w_q", "w_o", "w_kv", "pool_w")


def _ncb(t):
    return N_CHIPS // t.A


def _full_shape(t, rows=None):
    return (t.L, t.A, t.R if rows is None else rows, _ncb(t) * t.C)


def _slot_index(t, k):
    return k // _ncb(t), k % _ncb(t)


def _slot(ref, t, k, row0, rows):
    a, cb = _slot_index(t, k)
    return ref.at[:, a, pl.ds(row0, rows), pl.ds(pl.multiple_of(cb * t.C, 128), t.C)]


def _place_job(t, w, out_dtype=BF):
    nb = next((nb for nb in (8, 4, 2, 1) if t.R % (16 * nb) == 0), 1) if t.L == 1 else 1
    rb = t.R // nb

    def fn(j, kc_ref, ins, outs):
        outs[0][...] = ins[0][...].astype(out_dtype)

    def in_map(j, kc_ref):
        return (j // nb if t.layer is None else t.layer, j % nb, 0)

    def out_map(j, kc_ref):
        a, cb = _slot_index(t, kc_ref[0])
        return (j // nb, a, j % nb, cb)

    return Job(t.L * nb, [(w, (None, rb, t.C), in_map)],
               [(_sds(_full_shape(t), out_dtype), (None, None, rb, t.C), out_map)], fn)


def _mesh_position():
    x, y, c = lax.axis_index("x"), lax.axis_index("y"), lax.axis_index("c")
    chips = [(1 - x, y), (x, 1 - y), (1 - x, 1 - y)]
    return x, y, c, chips


DIRECT_BELOW = 1024


def _gather_rider(parts, fulls):
    nt = len(parts)
    TO_X, TO_Y, FWD_X, FWD_Y, SIB_X, SIB_Y, SIB_D = range(7)

    def rows_of(ti, core):
        t, r0, r1 = parts[ti]
        h = (r1 - r0) // 2
        return r0 + core * h, h

    def copy(outs, sems, kind, ti, k_src, row0, rows, dev):
        region = _slot(outs[ti], parts[ti][0], k_src, row0, rows)
        return pltpu.make_async_remote_copy(region, region, sems[0].at[ti, kind], sems[1].at[ti, kind],
                                            device_id=dev, device_id_type=MESH)

    def plan(outs, sems):
        x, y, c, _ = _mesh_position()
        me, kx, ky, kd = 2 * x + y, 2 * (1 - x) + y, 2 * x + (1 - y), 2 * (1 - x) + (1 - y)
        dev_x, dev_y, dev_d, sib = (1 - x, y, c), (x, 1 - y, c), (1 - x, 1 - y, c), (x, y, 1 - c)

        def whole(ti):
            return 0, parts[ti][0].R

        def mk(kind, k_send, k_recv, dev, send_rows, recv_rows):
            def build(ti, side):
                k_src = k_send if side == "s" else k_recv
                row0, rows = (send_rows if side == "s" else recv_rows)(ti)
                return copy(outs, sems, kind, ti, k_src, row0, rows, dev)
            return build

        def first_half(core):
            return lambda ti: (rows_of(ti, core)[0], rows_of(ti, core)[1] // 2)

        def second_half(core):
            return lambda ti: (rows_of(ti, core)[0] + rows_of(ti, core)[1] // 2, rows_of(ti, core)[1] // 2)

        mine = lambda ti: rows_of(ti, c)
        theirs = lambda ti: rows_of(ti, 1 - c)
        split = {
            TO_X: mk(TO_X, me, kx, dev_x, mine, mine),
            TO_Y: mk(TO_Y, me, ky, dev_y, mine, mine),
            FWD_X: mk(FWD_X, ky, kd, dev_x, first_half(c), first_half(c)),
            FWD_Y: mk(FWD_Y, kx, kd, dev_y, second_half(c), second_half(c)),
            SIB_X: mk(SIB_X, kx, kx, sib, mine, theirs),
            SIB_Y: mk(SIB_Y, ky, ky, sib, mine, theirs),
            SIB_D: mk(SIB_D, kd, kd, sib, mine, theirs),
        }
        direct = {
            TO_X: mk(TO_X, me, kx, dev_x, whole, whole),
            TO_Y: mk(TO_Y, me, ky, dev_y, whole, whole),
            FWD_X: mk(FWD_X, me, kd, dev_d, whole, whole),
        }
        return split, direct

    is_split = [t.L * t.R * t.C >= DIRECT_BELOW for t, _, _ in parts]
    assert all(s or (r0, r1) == (0, t.R) for s, (t, r0, r1) in zip(is_split, parts))

    def start(ins, outs, sems):
        split, direct = plan(outs, sems)
        for ti in range(nt):
            kinds = split if is_split[ti] else direct
            kinds[TO_X](ti, "s").start()
            kinds[TO_Y](ti, "s").start()
            if not is_split[ti]:
                kinds[FWD_X](ti, "s").start()

    def mid(ins, outs, sems):
        split, _ = plan(outs, sems)
        for ti in range(nt):
            if is_split[ti]:
                split[TO_Y](ti, "r").wait_recv()
                split[FWD_X](ti, "s").start()
                split[SIB_Y](ti, "s").start()
        for ti in range(nt):
            if is_split[ti]:
                split[TO_X](ti, "r").wait_recv()
                split[FWD_Y](ti, "s").start()
                split[SIB_X](ti, "s").start()

    def finish(ins, outs, sems):
        split, direct = plan(outs, sems)
        for ti in range(nt):
            if is_split[ti]:
                split[FWD_X](ti, "r").wait_recv()
                split[FWD_Y](ti, "r").wait_recv()
                split[SIB_D](ti, "s").start()
            else:
                for kind in (TO_X, TO_Y, FWD_X):
                    direct[kind](ti, "r").wait_recv()
        for ti in range(nt):
            if is_split[ti]:
                for kind in (SIB_X, SIB_Y, SIB_D):
                    split[kind](ti, "r").wait_recv()
        for ti in range(nt):
            kinds = split if is_split[ti] else direct
            for kind in kinds:
                kinds[kind](ti, "s").wait_send()

    sems = pltpu.SemaphoreType.DMA((nt, 7))
    return Rider(list(fulls), [_sds(a.shape, a.dtype) for a in fulls], {i: i for i in range(nt)},
                 [sems, sems], start, mid, finish)


def _pair_exchange_rider(specs, grads):
    nt = len(specs)

    def copy(ins, outs, sems, ti, c, sibling):
        half = specs[ti].R // 2
        return pltpu.make_async_remote_copy(ins[ti].at[:, :, pl.ds((1 - c) * half, half), :], outs[ti],
                                            sems[0].at[ti], sems[1].at[ti], device_id=sibling, device_id_type=MESH)

    def start(ins, outs, sems):
        x, y, c, _ = _mesh_position()
        for ti in range(nt):
            copy(ins, outs, sems, ti, c, (x, y, 1 - c)).start()

    def finish(ins, outs, sems):
        x, y, c, _ = _mesh_position()
        for ti in range(nt):
            copy(ins, outs, sems, ti, c, (x, y, 1 - c)).wait()

    sems = pltpu.SemaphoreType.DMA((nt,))
    return Rider(list(grads), [_sds(_full_shape(t, t.R // 2), BF) for t in specs], {}, [sems, sems], start, None, finish)


def _pair_sum_job(t, g, land):
    assert t.L == 1
    half = t.R // 2
    nj = half // t.rb
    block = (None, t.A, t.rb, _ncb(t) * t.C)

    def fn(j, kc_ref, ins, outs):
        outs[0][...] = (ins[0][...].astype(F32) + ins[1][...].astype(F32)).astype(BF)

    return Job(nj,
               [(g, block, lambda j, kc_ref: (0, 0, kc_ref[1] * nj + j, 0)),
                (land, block, lambda j, kc_ref: (0, 0, j, 0))],
               [(_sds(_full_shape(t, half), BF), block, lambda j, kc_ref: (0, 0, j, 0))], fn)


def _scatter_rider(specs, sums):
    nt = len(specs)

    def copy(ins, outs, sems, ti, j, chip, c):
        t = specs[ti]
        cx, cy = chip
        return pltpu.make_async_remote_copy(_slot(ins[ti], t, 2 * cx + cy, 0, t.R // 2), outs[ti].at[j],
                                            sems[0].at[ti, j], sems[1].at[ti, j],
                                            device_id=(cx, cy, c), device_id_type=MESH)

    def start(ins, outs, sems):
        _, _, c, chips = _mesh_position()
        for j, chip in enumerate(chips):
            for ti in range(nt):
                copy(ins, outs, sems, ti, j, chip, c).start()

    def finish(ins, outs, sems):
        _, _, c, chips = _mesh_position()
        for j, chip in enumerate(chips):
            for ti in range(nt):
                copy(ins, outs, sems, ti, j, chip, c).wait()

    sems = pltpu.SemaphoreType.DMA((nt, N_CHIPS - 1))
    return Rider(list(sums), [_sds((N_CHIPS - 1, t.L, t.R // 2, t.C), BF) for t in specs], {}, [sems, sems],
                 start, None, finish)


def _chip_sum_job(ts, landed):
    t0 = ts[0]
    assert t0.L == 1
    half = t0.R // 2
    nj = half // t0.rb

    def local(j, li):
        return jnp.clip(j - li * nj, 0, nj - 1)

    ins = []
    for li, t in enumerate(ts):
        s, land = landed[t.name]

        def own_map(j, kc_ref, li=li, t=t):
            a, cb = _slot_index(t, kc_ref[0])
            return (0, a, local(j, li), cb)

        ins.append((s, (None, None, t.rb, t.C), own_map))
        ins.append((land, (N_CHIPS - 1, None, t.rb, t.C), lambda j, kc_ref, li=li: (0, 0, local(j, li), 0)))

    def fn(j, kc_ref, in_refs, outs):
        for li in range(len(ts)):
            @pl.when(j // nj == li)
            def _():
                acc = in_refs[2 * li][...].astype(F32)
                for k in range(N_CHIPS - 1):
                    acc = acc + in_refs[2 * li + 1][k].astype(F32)
                outs[0][...] = acc

    return Job(len(ts) * nj, ins,
               [(_sds((len(ts), t0.R, t0.C)), (None, t0.rb, t0.C),
                 lambda j, kc_ref: (j // nj, kc_ref[1] * nj + j % nj, 0))], fn)


def _adamw_job(rb, w, g, m, v):
    n_layers, r, c = w.shape
    nb = r // rb
    block = (None, rb, c)
    index = lambda j, kc_ref: (j // nb, j % nb, 0)

    def fn(j, kc_ref, ins, outs):
        g_v = ins[1][...]
        outs[0][...] = g_v
        outs[1][...], outs[2][...], outs[3][...] = _adamw_math(ins[0][...], g_v, ins[2][...], ins[3][...])

    return Job(n_layers * nb, [(a, block, index) for a in (w, g, m, v)],
               [(_sds(w.shape), block, index)] * 4, fn)


def _chip_sum_fused_job(ts, fused, by_cols):
    t0 = ts[0]
    own0 = fused[t0.name][0]
    if by_cols:
        rows, cols = own0.shape
    else:
        nb, rows, bw = own0.shape
        cols = nb * bw
    nj = rows // t0.rb

    def local(j, li):
        return jnp.clip(j - li * nj, 0, nj - 1)

    ins = []
    for li, t in enumerate(ts):
        own, land = fused[t.name]
        if by_cols:
            ins.append((own, (t.rb, cols), lambda j, kc_ref, li=li: (local(j, li), 0)))
            ins.append((land, (N_CHIPS - 1, t.rb, cols), lambda j, kc_ref, li=li: (0, local(j, li), 0)))
        else:
            ins.append((own, (nb, t.rb, bw), lambda j, kc_ref, li=li: (0, local(j, li), 0)))
            ins.append((land, (N_CHIPS - 1, nb, t.rb, bw), lambda j, kc_ref, li=li: (0, 0, local(j, li), 0)))

    def fn(j, kc_ref, in_refs, outs):
        for li in range(len(ts)):
            @pl.when(j // nj == li)
            def _():
                acc = in_refs[2 * li][...].astype(F32)
                for k in range(N_CHIPS - 1):
                    acc = acc + in_refs[2 * li + 1][k].astype(F32)
                outs[0][...] = acc if by_cols else jnp.concatenate([acc[b] for b in range(nb)], axis=1)

    def out_map(j, kc_ref):
        return (j // nj, j % nj, kc_ref[1]) if by_cols else (j // nj, kc_ref[1] * nj + j % nj, 0)

    return Job(len(ts) * nj, ins, [(_sds((len(ts), t0.R, t0.C)), (None, t0.rb, cols), out_map)], fn)


def _share_rider(halves, by_cols):
    nt = len(halves)

    def copy(outs, sems, ti, core, sibling):
        axis = 2 if by_cols[ti] else 1
        half = halves[ti].shape[axis] // 2
        piece = pl.ds(pl.multiple_of(core * half, 128 if by_cols[ti] else 8), half)
        part = outs[ti].at[:, :, piece] if by_cols[ti] else outs[ti].at[:, piece, :]
        return pltpu.make_async_remote_copy(part, part, sems[0].at[ti], sems[1].at[ti],
                                            device_id=sibling, device_id_type=MESH)

    def start(ins, outs, sems):
        x, y, c, _ = _mesh_position()
        for ti in range(nt):
            copy(outs, sems, ti, c, (x, y, 1 - c)).start()

    def finish(ins, outs, sems):
        x, y, c, _ = _mesh_position()
        for ti in range(nt):
            copy(outs, sems, ti, 1 - c, (x, y, 1 - c)).wait_recv()
        for ti in range(nt):
            copy(outs, sems, ti, c, (x, y, 1 - c)).wait_send()

    sems = pltpu.SemaphoreType.DMA((nt,))
    return Rider(list(halves), [_sds(a.shape, a.dtype) for a in halves], {i: i for i in range(nt)}, [sems, sems],
                 start, None, finish)


def _both(r1, r2):
    assert r1.mid is None and r2.mid is None
    ni, no, ns = len(r1.arrays), len(r1.out_shapes), len(r1.scratch)

    def split(fn1, fn2):
        def run(ins, outs, scr):
            fn1(ins[:ni], outs[:no], scr[:ns])
            fn2(ins[ni:], outs[no:], scr[ns:])
        return run

    aliases = dict(r1.aliases)
    aliases.update({ni + a: no + b for a, b in r2.aliases.items()})
    return Rider(r1.arrays + r2.arrays, r1.out_shapes + r2.out_shapes, aliases, r1.scratch + r2.scratch,
                 split(r1.start, r2.start), None, split(r1.finish, r2.finish))


def _adamw_math(w, g, m, v):
    m = B1 * m + (1.0 - B1) * g
    v = B2 * v + (1.0 - B2) * (g * g)
    delta = -LR * ((m / BC1) / (jnp.sqrt(v / BC2) + AEPS) + WD * w)
    return delta, m, v


GAIN_ROWS = {"pre_mix_g": 0, "post_mix_g": 2, "pre_ffn_g": 4, "post_ffn_g": 6, "ple_g": 8, "ple_post_g": 10}
ROW_KV_G, ROW_POOL_SCALE, ROW_SINKS, ROW_LOSS, PACK_ROWS = 12, 13, 14, 15, 16
SMALL_NAMES = tuple(GAIN_ROWS) + ("kv_g", "pool_scale", "sinks")


def _small_all_reduce(rows, dpool, rider=None):
    ng, pr = len(WINDOWS), POOL_G // N_CHIPS

    def body(*refs):
        row_refs = refs[:PACK_ROWS]
        dpool_ref, tot_ref, gpool_ref, pack, land, pland, send, recv, psend, precv = refs[PACK_ROWS:]
        x, y, c, _ = _mesh_position()
        me = 4 * x + 2 * y + c
        for r in range(PACK_ROWS):
            pack[r:r + 1, :] = row_refs[r][...]

        def shard_of(k):
            return dpool_ref.at[:, pl.ds(pl.multiple_of(k * pr, pr), pr), :]

        cps = []
        for j in range(1, N_DEV):
            px, py, pc = x ^ (j >> 2), y ^ ((j >> 1) & 1), c ^ (j & 1)
            cps.append(pltpu.make_async_remote_copy(pack, land.at[me], send.at[j], recv.at[j],
                                                    device_id=(px, py, pc), device_id_type=MESH))
            cps.append(pltpu.make_async_remote_copy(shard_of(2 * px + py), pland.at[me], psend.at[j], precv.at[j],
                                                    device_id=(px, py, pc), device_id_type=MESH))
        for cp in cps:
            cp.start()
        land[me] = pack[...]
        pland[me] = dpool_ref[:, pl.ds(pl.multiple_of((2 * x + y) * pr, pr), pr), :]
        for j in range(1, N_DEV):
            pltpu.make_async_remote_copy(pack, land.at[me ^ j], send.at[j], recv.at[j],
                                         device_id=(x, y, c), device_id_type=MESH).wait_recv()
            pltpu.make_async_remote_copy(shard_of(0), pland.at[me ^ j], psend.at[j], precv.at[j],
                                         device_id=(x, y, c), device_id_type=MESH).wait_recv()
        for cp in cps:
            cp.wait_send()
        tot = land[0]
        gp = pland[0].astype(F32)
        for d in range(1, N_DEV):
            tot = tot + land[d]
            gp = gp + pland[d].astype(F32)
        tot_ref[...] = tot
        gpool_ref[...] = gp

    sems = pltpu.SemaphoreType.DMA((N_DEV,))
    return _call(
        body, name="small_all_reduce", grid=(1,),
        in_specs=[VSPEC] * (PACK_ROWS + 1), out_specs=[VSPEC, VSPEC],
        out_shape=[_sds((PACK_ROWS, D)), _sds((ng, pr, POOL_G))],
        scratch_shapes=[pltpu.VMEM((PACK_ROWS, D), F32), pltpu.VMEM((N_DEV, PACK_ROWS, D), F32),
                        pltpu.VMEM((N_DEV, ng, pr, POOL_G), BF), sems, sems, sems, sems],
        args=[*rows, dpool], rider=rider)


def _small_adamw(tot, kc, small_w, small_m, small_v):
    names = SMALL_NAMES
    n = len(names)

    def body(*refs):
        tot_ref, kc_ref = refs[0], refs[1]
        w_refs = dict(zip(names, refs[2:2 + n]))
        m_refs = dict(zip(names, refs[2 + n:2 + 2 * n]))
        v_refs = dict(zip(names, refs[2 + 2 * n:2 + 3 * n]))
        loss_ref = refs[2 + 3 * n]
        out_refs = {nm: refs[3 + 3 * n + 4 * k: 7 + 3 * n + 4 * k] for k, nm in enumerate(names)}
        tot = tot_ref[...]
        loss_ref[...] = 0.5 * jnp.sum(tot[ROW_LOSS:ROW_LOSS + 1, :], axis=-1, keepdims=True) * (1.0 / D)

        def update(nm, g):
            g_ref, d_ref, nm_ref, nv_ref = out_refs[nm]
            g_ref[...] = g
            d_ref[...], nm_ref[...], nv_ref[...] = _adamw_math(w_refs[nm][...], g, m_refs[nm][...], v_refs[nm][...])

        for nm, r in GAIN_ROWS.items():
            update(nm, tot[r:r + 2, :])
        update("kv_g", tot[ROW_KV_G:ROW_KV_G + 1, :])
        k = kc_ref[0]
        width = D // N_CHIPS
        g_scale = jnp.zeros((1, width), F32)
        for kk in range(N_CHIPS):
            g_scale = g_scale + jnp.where(k == kk, tot[ROW_POOL_SCALE:ROW_POOL_SCALE + 1, kk * width:(kk + 1) * width], 0.0)
        update("pool_scale", g_scale)
        update("sinks", tot[ROW_SINKS:ROW_SINKS + 1, 0:N_HEADS])

    ins = [tot, kc] + [small_w[nm] for nm in names] + [small_m[nm] for nm in names] + [small_v[nm] for nm in names]
    out_shape = [_sds((1, 1))]
    for nm in names:
        out_shape += [_sds(small_w[nm].shape)] * 4
    outs = pl.pallas_call(
        body, name="small_adamw",
        in_specs=[VSPEC, SSPEC] + [VSPEC] * (3 * n), out_specs=[VSPEC] * len(out_shape), out_shape=out_shape,
        compiler_params=_params(),
    )(*ins)
    return outs[0], {nm: outs[1 + 4 * k: 5 + 4 * k] for k, nm in enumerate(names)}


def _compute_layout(t, full):
    if t.src == "w_gu":
        return full.reshape(2, D, FF)
    if t.src == "pool_w":
        return full.reshape(len(WINDOWS), POOL_G, POOL_G)
    if t.src == "pool_scale":
        return full.reshape(1, D)
    return full.reshape(t.A * t.R, _ncb(t) * t.C)


def kernel(x, p, pre_mix_g, post_mix_g, pre_ffn_g, post_ffn_g, pool_w, pool_scale, kv_g, w_kv, w_q, sinks, w_o, w_gu, w_down, ple_g, w_ple_gate, w_ple_proj, ple_post_g, loss_target, m_pre_mix_g, m_post_mix_g, m_pre_ffn_g, m_post_ffn_g, m_pool_w, m_pool_scale, m_kv_g, m_w_kv, m_w_q, m_sinks, m_w_o, m_w_gu, m_w_down, m_ple_g, m_w_ple_gate, m_w_ple_proj, m_ple_post_g, v_pre_mix_g, v_post_mix_g, v_pre_ffn_g, v_post_ffn_g, v_pool_w, v_pool_scale, v_kv_g, v_w_kv, v_w_q, v_sinks, v_w_o, v_w_gu, v_w_down, v_ple_g, v_w_ple_gate, v_w_ple_proj, v_ple_post_g):
    weights = dict(pre_mix_g=pre_mix_g, post_mix_g=post_mix_g, pre_ffn_g=pre_ffn_g, post_ffn_g=post_ffn_g,
                   pool_w=pool_w, pool_scale=pool_scale, kv_g=kv_g, w_kv=w_kv, w_q=w_q, sinks=sinks, w_o=w_o,
                   w_gu=w_gu, w_down=w_down, ple_g=ple_g, w_ple_gate=w_ple_gate, w_ple_proj=w_ple_proj,
                   ple_post_g=ple_post_g)
    m_in = dict(pre_mix_g=m_pre_mix_g, post_mix_g=m_post_mix_g, pre_ffn_g=m_pre_ffn_g, post_ffn_g=m_post_ffn_g,
                pool_w=m_pool_w, pool_scale=m_pool_scale, kv_g=m_kv_g, w_kv=m_w_kv, w_q=m_w_q, sinks=m_sinks,
                w_o=m_w_o, w_gu=m_w_gu, w_down=m_w_down, ple_g=m_ple_g, w_ple_gate=m_w_ple_gate,
                w_ple_proj=m_w_ple_proj, ple_post_g=m_ple_post_g)
    v_in = dict(pre_mix_g=v_pre_mix_g, post_mix_g=v_post_mix_g, pre_ffn_g=v_pre_ffn_g, post_ffn_g=v_post_ffn_g,
                pool_w=v_pool_w, pool_scale=v_pool_scale, kv_g=v_kv_g, w_kv=v_w_kv, w_q=v_w_q, sinks=v_sinks,
                w_o=v_w_o, w_gu=v_w_gu, w_down=v_w_down, ple_g=v_ple_g, w_ple_gate=v_w_ple_gate,
                w_ple_proj=v_w_ple_proj, ple_post_g=v_ple_post_g)
    order = ["pre_mix_g", "post_mix_g", "pre_ffn_g", "post_ffn_g", "pool_w", "pool_scale", "kv_g", "w_kv", "w_q",
             "sinks", "w_o", "w_gu", "w_down", "ple_g", "w_ple_gate", "w_ple_proj", "ple_post_g"]

    kc = jnp.stack([2 * lax.axis_index("x") + lax.axis_index("y"), lax.axis_index("c")]).astype(jnp.int32)
    s_len = x.shape[1]
    x2d = x.reshape(s_len, D)
    p3d = p.reshape(2, s_len, PLE)
    target = loss_target.reshape(s_len, D)
    kv_g2d = kv_g.reshape(1, D)
    gains = {nm: weights[nm] for nm in GAIN_ROWS}

    def shard_view(src, a):
        t = next(t for t in BIGS.values() if t.src == src)
        return a.reshape(-1, t.R, t.C)

    first, second = ["pool_w", "pool_scale"], ["w_gu0", "w_down0"]
    rest = [nm for nm in BIGS if nm not in first + second]
    specs = dict(BIGS, pool_scale=POOL_SCALE)
    placed = {}

    def place_job(nm):
        if nm == "pool_scale":
            return _place_job(POOL_SCALE, pool_scale.reshape(1, 1, D // N_CHIPS), F32)
        return _place_job(BIGS[nm], shard_view(BIGS[nm].src, weights[BIGS[nm].src]))

    def gather(names, rows=None):
        rows = rows or {}
        parts = [(specs[nm],) + tuple(rows.get(nm, (0, specs[nm].R))) for nm in names]
        return _gather_rider(parts, [placed[nm] for nm in names])

    def take(names, results):
        for nm, a in zip(names, results):
            placed[nm] = a

    def weight(nm):
        return _compute_layout(specs[nm], placed[nm])

    take(first, [r[0] for r in _multi_call("place_pool", [place_job(nm) for nm in first], kc)])
    cast, got = _multi_call("place_ffn0", [place_job(nm) for nm in second], kc, rider=gather(first))
    take(second, [r[0] for r in cast])
    take(first, got)
    jobs = [place_job(nm) for nm in rest]
    jobs.append(_mixa_fwd_job(x2d, gains["pre_mix_g"], weight("pool_w"), weight("pool_scale"), gains["post_mix_g"]))
    results, got = _multi_call("cast_and_mixa_fwd", jobs, kc, rider=gather(second))
    take(rest, [r[0] for r in results[:-1]])
    take(second, got)
    y0, x1 = results[-1]

    ride = ["w_ple_gate0", "w_ple_proj0", "w_q", "w_kv", "w_o", "w_gu1"]
    (f0, x2, g0, u0), got = _ffn_fwd(0, x1, gains["pre_ffn_g"], weight("w_gu0"), weight("w_down0"), gains["post_ffn_g"],
                             rider=gather(ride, {"w_gu1": (0, 320)}))
    take(ride, got)

    ride = ["w_ple_gate1", "w_ple_proj1", "w_gu1"]
    (z0, pe0, x3, q, kv), got = _ple_fwd(
        0, x2, p3d, gains["ple_g"], weight("w_ple_gate0"), weight("w_ple_proj0"), gains["ple_post_g"],
        qkv=(gains["pre_mix_g"], kv_g2d, weight("w_q"), weight("w_kv")),
        rider=gather(ride, {"w_gu1": (320, 704)}))
    take(ride, got)

    ride = ["w_down1", "w_gu1"]
    (attn, y1, x4), got = _attn_fwd(q, kv, sinks, x3, weight("w_o"), gains["post_mix_g"],
                                    rider=gather(ride, {"w_gu1": (704, D)}))
    take(ride, got)

    (f1, x5, g1, u1, z1, pe1, dx6, loss_row), _ = _ffn_fwd(
        1, x4, gains["pre_ffn_g"], weight("w_gu1"), weight("w_down1"), gains["post_ffn_g"],
        head=(p3d, gains["ple_g"], weight("w_ple_gate1"), weight("w_ple_proj1"), gains["ple_post_g"], target))

    local = {}
    landed = {}
    fused = {}

    def local_grads(names):
        return [local[nm].reshape(_full_shape(BIGS[nm])) for nm in names]

    def pair_exchange(names):
        return _pair_exchange_rider([BIGS[nm] for nm in names], local_grads(names))

    def pair_sum(tag, names, lands):
        jobs = [_pair_sum_job(BIGS[nm], g, l) for nm, g, l in zip(names, local_grads(names), lands)]
        return [r[0] for r in _multi_call(f"pair_sum_{tag}", jobs, kc)]

    def scatter(names, sums):
        return _scatter_rider([BIGS[nm] for nm in names], sums)

    def keep(names, sums, got):
        for nm, s, l in zip(names, sums, got):
            landed[nm] = (s, l)

    (dx5, local["w_ple_gate1"], local["w_ple_proj1"], d_ple1, d_plepost1), _ = _ple_bwd(
        1, dx6, x5, z1, pe1, p3d, gains["ple_g"], weight("w_ple_gate1"), gains["ple_post_g"])

    group_a = ["w_ple_gate1", "w_ple_proj1"]
    (dx4, d_preffn1, d_postffn1, *scattered), lands_a = _ffn_bwd(
        1, dx5, x4, f1, g1, u1, gains["pre_ffn_g"], weight("w_gu1"), weight("w_down1"), gains["post_ffn_g"], kc,
        rider=pair_exchange(group_a))
    fused["w_gu1"], fused["w_down1"] = scattered[0:2], scattered[2:4]

    (dq, dkv, local["w_o"], d_postmix1, d_sinks), _ = _attn_bwd(
        dx4, y1, attn, q, kv, sinks, weight("w_o"), gains["post_mix_g"])
    dx3, local["w_q"], local["w_kv"], d_premix1, d_kvg = _qkv_bwd(
        dq, dkv, x3, dx4, gains["pre_mix_g"], kv_g2d, weight("w_q"), weight("w_kv"))

    group_b = ["w_o", "w_q", "w_kv"]
    (dx2, local["w_ple_gate0"], local["w_ple_proj0"], d_ple0, d_plepost0), lands_b = _ple_bwd(
        0, dx3, x2, z0, pe0, p3d, gains["ple_g"], weight("w_ple_gate0"), gains["ple_post_g"],
        rider=pair_exchange(group_b))
    group_ab = group_a + group_b
    sums_ab = pair_sum("ab", group_ab, lands_a + lands_b)

    group_c = ["w_ple_gate0", "w_ple_proj0"]
    (dx1, d_preffn0, d_postffn0, *scattered), got = _ffn_bwd(
        0, dx2, x1, f0, g0, u0, gains["pre_ffn_g"], weight("w_gu0"), weight("w_down0"), gains["post_ffn_g"], kc,
        rider=_both(pair_exchange(group_c), scatter(group_ab, sums_ab)))
    fused["w_gu0"], fused["w_down0"] = scattered[0:2], scattered[2:4]
    lands_c = got[:len(group_c)]
    keep(group_ab, sums_ab, got[len(group_c):])

    layers_of = lambda src: [t for t in BIGS.values() if t.src == src]
    own_scatter = ["w_gu", "w_down"]
    early = own_scatter + ["w_q", "w_o", "w_kv"]
    late = ["w_ple_gate", "w_ple_proj"]
    by_cols = lambda srcs: [src == "w_down" for src in srcs]
    jobs = [_chip_sum_fused_job(layers_of(src), fused, by_cols=src == "w_down") for src in own_scatter]
    jobs += [_chip_sum_job(layers_of(src), landed) for src in early if src not in own_scatter]
    jobs_c = [_pair_sum_job(BIGS[nm], g, l) for nm, g, l in zip(group_c, local_grads(group_c), lands_c)]
    sums = [r[0] for r in _multi_call("chip_sum_early", jobs + jobs_c, kc)]
    halves, sums_c = sums[:len(jobs)], sums[len(jobs):]
    (dx0, d_pool, d_scale, d_postmix0, d_premix0), got = _mixa_bwd(
        dx1, x2d, y0, gains["pre_mix_g"], weight("pool_w"), weight("pool_scale"), gains["post_mix_g"],
        rider=_both(scatter(group_c, sums_c), _share_rider(halves, by_cols(early))))
    keep(group_c, sums_c, got[:len(group_c)])
    full_grads = dict(zip(early, got[len(group_c):]))

    rows = [d_premix0, d_premix1, d_postmix0, d_postmix1, d_preffn0, d_preffn1, d_postffn0, d_postffn1,
            d_ple0, d_ple1, d_plepost0, d_plepost1, d_kvg, d_scale, d_sinks, loss_row]
    as2d = lambda a: a.reshape(1, D) if a.ndim == 1 else a
    halves = [r[0] for r in _multi_call("chip_sum_late", [_chip_sum_job(layers_of(src), landed) for src in late], kc)]
    (tot, g_pool), got = _small_all_reduce(rows, d_pool, rider=_share_rider(halves, by_cols(late)))
    full_grads.update(zip(late, got))
    full_grads["pool_w"] = g_pool
    loss, small = _small_adamw(tot, kc, {nm: as2d(weights[nm]) for nm in SMALL_NAMES},
                               {nm: as2d(m_in[nm]) for nm in SMALL_NAMES},
                               {nm: as2d(v_in[nm]) for nm in SMALL_NAMES})


    def adam_job(src):
        rb = layers_of(src)[0].rb // (1 if src == "pool_w" else 2)
        return _adamw_job(rb, shard_view(src, weights[src]), full_grads[src],
                          shard_view(src, m_in[src]), shard_view(src, v_in[src]))

    out = {"grad": {}, "delta": {}, "new_m": {}, "new_v": {}}
    results = dict(zip(BIG_SOURCES, _multi_call("adamw", [adam_job(src) for src in BIG_SOURCES], kc)))
    for src in BIG_SOURCES:
        shape = weights[src].shape
        for kind, a in zip(("grad", "delta", "new_m", "new_v"), results[src]):
            out[kind][src] = a.reshape(shape)
    for nm in SMALL_NAMES:
        shape = weights[nm].shape
        for kind, a in zip(("grad", "delta", "new_m", "new_v"), small[nm]):
            out[kind][nm] = a.reshape(shape)

    return (loss.reshape(()), dx0.reshape(x.shape),
            *[out["grad"][nm] for nm in order], *[out["delta"][nm] for nm in order],
            *[out["new_m"][nm] for nm in order], *[out["new_v"][nm] for nm in order])
```

```python
import collections

import jax
import jax.numpy as jnp
from jax import lax
from jax.experimental import pallas as pl
from jax.experimental.pallas import tpu as pltpu

D = 1024
FF = 2816
N_HEADS = 16
HEAD_DIM = 64
N_KV_HEADS = 4
GQA = N_HEADS // N_KV_HEADS
KVD = N_KV_HEADS * HEAD_DIM
PLE = 256
BLK = 128
WINDOWS = (2, 4, 8, 16)
POOL_G = 256
HALO = 16
EPS = 1e-6
NEG_INF = -1e30
ATT_SCALE = HEAD_DIM ** -0.5
SLOPES = tuple(2.0 ** (-8.0 * (h + 1) / N_HEADS) for h in range(N_HEADS))
N_CHIPS = 4
N_DEV = 8

LR, B1, B2, AEPS, WD, STEP = 0.001, 0.9, 0.999, 1e-08, 0.01, 10
BC1 = 1.0 - B1 ** STEP
BC2 = 1.0 - B2 ** STEP

BF = jnp.bfloat16
F32 = jnp.float32
MESH = pl.DeviceIdType.MESH
VMEM_LIMIT_V7X = 58 * 1024 * 1024
TM = 256
TM_FFN_BWD = 512
FF_CHUNK = 256
FF_HALF = FF // 2

VSPEC = pl.BlockSpec(memory_space=pltpu.VMEM)
SSPEC = pl.BlockSpec(memory_space=pltpu.SMEM)
ANYSPEC = pl.BlockSpec(memory_space=pl.ANY)


def _params(n_grid=0):
    sem = ("arbitrary",) * n_grid if n_grid else None
    return pltpu.CompilerParams(dimension_semantics=sem, vmem_limit_bytes=VMEM_LIMIT_V7X)


def _sds(shape, dtype=F32):
    return jax.ShapeDtypeStruct(tuple(shape), dtype)


Rider = collections.namedtuple("Rider", "arrays out_shapes aliases scratch start mid finish")
MID_NUM, MID_DEN = 5, 8


def _call(body, *, name, grid, in_specs, out_specs, out_shape, args, scratch_shapes=(), rider=None, prefetch=None):
    ni, no, ns = len(in_specs), len(out_specs), len(scratch_shapes)
    npre = 0 if prefetch is None else 1
    pre = [] if prefetch is None else [prefetch]
    if rider is None:
        rider = Rider([], [], {}, [], None, None, None)
    ri, ro = len(rider.arrays), len(rider.out_shapes)

    def full(*refs):
        pre_refs, refs = refs[:npre], refs[npre:]
        ins, refs = refs[:ni], refs[ni:]
        rins, refs = refs[:ri], refs[ri:]
        outs, refs = refs[:no], refs[no:]
        routs, refs = refs[:ro], refs[ro:]
        scr, rscr = refs[:ns], refs[ns:]
        ids = [pl.program_id(a) for a in range(len(grid))]
        first = ids[0] == 0
        last = ids[0] == grid[0] - 1
        for a in range(1, len(grid)):
            first = first & (ids[a] == 0)
            last = last & (ids[a] == grid[a] - 1)

        if rider.start is not None:
            @pl.when(first)
            def _():
                rider.start(rins, routs, rscr)

        if rider.mid is not None:
            assert len(grid) == 1

            @pl.when(ids[0] == (grid[0] * MID_NUM) // MID_DEN)
            def _():
                rider.mid(rins, routs, rscr)

        body(*pre_refs, *ins, *outs, *scr)

        if rider.finish is not None:
            @pl.when(last)
            def _():
                rider.finish(rins, routs, rscr)

    outs = pl.pallas_call(
        full, name=name,
        grid_spec=pltpu.PrefetchScalarGridSpec(
            num_scalar_prefetch=npre, grid=grid,
            in_specs=list(in_specs) + [ANYSPEC] * ri, out_specs=list(out_specs) + [ANYSPEC] * ro,
            scratch_shapes=list(scratch_shapes) + list(rider.scratch)),
        out_shape=list(out_shape) + list(rider.out_shapes),
        input_output_aliases={npre + ni + a: no + b for a, b in rider.aliases.items()},
        compiler_params=_params(len(grid)))(*pre, *args, *rider.arrays)
    return list(outs[:no]), list(outs[no:])


Job = collections.namedtuple("Job", "steps ins outs fn")


def _multi_call(name, jobs, kc, rider=None):
    n = max(job.steps for job in jobs)

    def clamped(index, steps):
        return lambda s, kc_ref: index(jnp.minimum(s, steps - 1), kc_ref)

    in_specs, out_specs, out_shape, args = [], [], [], []
    for job in jobs:
        for arr, block, index, *single in job.ins:
            mode = dict(pipeline_mode=pl.Buffered(1)) if single and single[0] else {}
            in_specs.append(pl.BlockSpec(block, clamped(index, job.steps), **mode))
            args.append(arr)
        for sds, block, index in job.outs:
            out_specs.append(pl.BlockSpec(block, clamped(index, job.steps)))
            out_shape.append(sds)
    n_in = len(args)

    def body(kc_ref, *refs):
        s = pl.program_id(0)
        i0, o0 = 0, n_in
        for job in jobs:
            ins, outs = refs[i0:i0 + len(job.ins)], refs[o0:o0 + len(job.outs)]
            i0, o0 = i0 + len(job.ins), o0 + len(job.outs)

            @pl.when(s < job.steps)
            def _():
                job.fn(s, kc_ref, ins, outs)

    outs, routs = _call(body, name=name, grid=(n,), in_specs=in_specs, out_specs=out_specs, out_shape=out_shape,
                        args=args, prefetch=kc, rider=rider)
    res, o0 = [], 0
    for job in jobs:
        res.append(outs[o0:o0 + len(job.outs)])
        o0 += len(job.outs)
    return res if rider is None else (res, routs)


def _rms_fwd(x, g):
    r = lax.rsqrt(jnp.mean(x * x, axis=-1, keepdims=True) + EPS)
    return x * r * g


def _rms_bwd(x, g, dy):
    r = lax.rsqrt(jnp.mean(x * x, axis=-1, keepdims=True) + EPS)
    xn = x * r
    dxn = dy * g
    dx = r * (dxn - xn * jnp.mean(dxn * xn, axis=-1, keepdims=True))
    return dx, dy * xn


def _rowsum(a):
    return jnp.sum(a, axis=0, keepdims=True)


def _sigmoid(z):
    return 1.0 / (1.0 + jnp.exp(-z))


def _dot(a, b):
    return jnp.dot(a, b, preferred_element_type=F32)


def _dot_nt(a, b):
    return lax.dot_general(a, b, (((1,), (1,)), ((), ())), preferred_element_type=F32)


def _dot_tn(a, b):
    return lax.dot_general(a, b, (((0,), (0,)), ((), ())), preferred_element_type=F32)


def _row_spec(tm, width=D):
    return pl.BlockSpec((tm, width), lambda i: (i, 0))


def _const_spec(shape):
    zeros = (0,) * len(shape)
    return pl.BlockSpec(tuple(shape), lambda *_: zeros)


def _pool_delta(he, pos):
    out = []
    for gi, w in enumerate(WINDOWS):
        hg = he[:, gi * POOL_G:(gi + 1) * POOL_G]
        s = hg
        k = 1
        while k < w:
            s = s + pltpu.roll(s, k, 0)
            k *= 2
        cnt = jnp.maximum(jnp.minimum(pos + 1, w), 1).astype(F32)
        out.append(s / cnt - hg)
    return out


def _load_with_halo_before(x_ref, i, tm):
    r0 = pl.multiple_of(i * tm, tm)
    hs = pl.multiple_of(jnp.maximum(i * tm - HALO, 0), 8)
    xh = jnp.where(i > 0, x_ref[pl.ds(hs, HALO), :], 0.0)
    xt = x_ref[pl.ds(r0, tm), :]
    return xt, jnp.concatenate([xh, xt], axis=0)


def _mixa_fwd_job(x, pre_g, pool_w, pool_scale, post_g):
    s_len = x.shape[0]

    def fn(i, kc_ref, ins, outs):
        x_ref, pg_ref, w_ref, sc_ref, qg_ref = ins
        y_ref, x1_ref = outs
        xt, xe = _load_with_halo_before(x_ref, i, TM)
        he = _rms_fwd(xe, pg_ref[0:1, :])
        pos = i * TM - HALO + lax.broadcasted_iota(jnp.int32, (TM + HALO, 1), 0)
        ds = _pool_delta(he, pos)
        ys = [_dot(ds[gi][HALO:, :].astype(BF), w_ref[gi]) for gi in range(len(WINDOWS))]
        y = jnp.concatenate(ys, axis=1) * sc_ref[...]
        y_ref[...] = y
        x1_ref[...] = xt + _rms_fwd(y, qg_ref[0:1, :])

    def whole(a):
        zeros = (0,) * a.ndim
        return (a, a.shape, lambda j, kc_ref: zeros, True)

    rows = lambda j, kc_ref: (j, 0)
    return Job(s_len // TM, [whole(a) for a in (x, pre_g, pool_w, pool_scale, post_g)],
               [(_sds((s_len, D)), (TM, D), rows), (_sds((s_len, D)), (TM, D), rows)], fn)


def _mixa_bwd(dx1, x, y, pre_g, pool_w, pool_scale, post_g, rider=None):
    s_len = x.shape[0]
    n = s_len // TM
    ng = len(WINDOWS)

    def body(dx_ref, x_ref, y_ref, pg_ref, w_ref, sc_ref, qg_ref,
             dx0_ref, dw_ref, dsc_ref, dqg_ref, dpg_ref, wacc):
        i = pl.program_id(0)

        @pl.when(i == 0)
        def _():
            wacc[...] = jnp.zeros_like(wacc)
            dsc_ref[...] = jnp.zeros_like(dsc_ref)
            dqg_ref[...] = jnp.zeros_like(dqg_ref)
            dpg_ref[...] = jnp.zeros_like(dpg_ref)

        r0 = pl.multiple_of(i * TM, TM)
        xt, xe = _load_with_halo_before(x_ref, i, TM)
        he = _rms_fwd(xe, pg_ref[0:1, :])
        pos_b = i * TM - HALO + lax.broadcasted_iota(jnp.int32, (TM + HALO, 1), 0)
        ds = _pool_delta(he, pos_b)

        last = i == n - 1
        a0 = pl.multiple_of(jnp.minimum(i * TM + TM, s_len - HALO), 8)
        ye = jnp.concatenate([y_ref[pl.ds(r0, TM), :], y_ref[pl.ds(a0, HALO), :]], axis=0)
        dt = dx_ref[pl.ds(r0, TM), :]
        de = jnp.concatenate([dt, jnp.where(last, 0.0, dx_ref[pl.ds(a0, HALO), :])], axis=0)
        dye, prod = _rms_bwd(ye, qg_ref[0:1, :], de)
        dqg_ref[...] += _rowsum(prod[:TM, :])
        dys = dye * sc_ref[...]
        pos_a = i * TM + lax.broadcasted_iota(jnp.int32, (TM + HALO, 1), 0)

        dhs, dscs = [], []
        for gi, w in enumerate(WINDOWS):
            sl = slice(gi * POOL_G, (gi + 1) * POOL_G)
            wg = w_ref[gi]
            dys_g = dys[:, sl].astype(BF)
            d_g = ds[gi][HALO:, :].astype(BF)
            ypre = _dot(d_g, wg)
            dscs.append(_rowsum(dye[:TM, sl] * ypre))
            wacc[gi] += _dot_tn(d_g, dys_g[:TM, :])
            dd = _dot_nt(dys_g, wg)
            cnt = jnp.minimum(pos_a + 1, w).astype(F32)
            a = dd / cnt
            k = 1
            while k < w:
                a = a + pltpu.roll(a, TM + HALO - k, 0)
                k *= 2
            dhs.append(a[:TM, :] - dd[:TM, :])
        dsc_ref[...] += jnp.concatenate(dscs, axis=1)
        dh = jnp.concatenate(dhs, axis=1)
        dxp, prod2 = _rms_bwd(xt, pg_ref[0:1, :], dh)
        dpg_ref[...] += _rowsum(prod2)
        dx0_ref[...] = dt + dxp

        @pl.when(last)
        def _():
            dw_ref[...] = wacc[...].astype(BF)

    return _call(
        body, name="mixa_bwd", grid=(n,), in_specs=[VSPEC] * 7,
        out_specs=[_row_spec(TM), _const_spec((ng, POOL_G, POOL_G)), _const_spec((1, D)),
                   _const_spec((1, D)), _const_spec((1, D))],
        out_shape=[_sds((s_len, D)), _sds((ng, POOL_G, POOL_G), BF), _sds((1, D)), _sds((1, D)), _sds((1, D))],
        scratch_shapes=[pltpu.VMEM((ng, POOL_G, POOL_G), F32)],
        args=[dx1, x, y, pre_g, pool_w, pool_scale, post_g], rider=rider)


def _ple_math(layer, x, p_blk, g_ref, wg_ref, wp_ref, qg_ref):
    r = _rms_fwd(x, g_ref[layer:layer + 1, :]).astype(BF)
    z = _dot(r, wg_ref[...])
    pe = _dot(p_blk.astype(BF), wp_ref[...])
    return z, pe, x + _rms_fwd(pe * _sigmoid(z), qg_ref[layer:layer + 1, :])


def _ple_bwd_math(layer, dx, x, z, pe, p_blk, g_ref, wg_ref, qg_ref):
    gate = _sigmoid(z)
    de, prod = _rms_bwd(pe * gate, qg_ref[layer:layer + 1, :], dx)
    dpe = (de * gate).astype(BF)
    dz = (de * pe * gate * (1.0 - gate)).astype(BF)
    dwp = _dot_tn(p_blk.astype(BF), dpe)
    g = g_ref[layer:layer + 1, :]
    dwg = _dot_tn(_rms_fwd(x, g).astype(BF), dz)
    dxp, prod2 = _rms_bwd(x, g, _dot_nt(dz, wg_ref[...]))
    return dx + dxp, dwp, dwg, _rowsum(prod2), _rowsum(prod)


def _ffn_fwd(layer, x1, pre_g, wgu, wd, post_g, rider=None, head=None):
    s_len = x1.shape[0]

    def body(*refs):
        if head:
            (x_ref, pg_ref, wgu_ref, wd_ref, qg_ref, p_ref, eg_ref, wg_ref, wp_ref, eq_ref, t_ref,
             f_ref, x2_ref, g_ref, u_ref, dx_ref, dwg_ref, dwp_ref, deg_ref, deq_ref, lv_ref, gacc, pacc) = refs
        else:
            x_ref, pg_ref, wgu_ref, wd_ref, qg_ref, f_ref, x2_ref, g_ref, u_ref = refs
        x = x_ref[...]
        h = _rms_fwd(x, pg_ref[layer:layer + 1, :]).astype(BF)
        f = jnp.zeros((TM, D), F32)
        for c in range(FF // FF_HALF):
            cols = slice(c * FF_HALF, (c + 1) * FF_HALF)
            g = _dot(h, wgu_ref[0, :, cols])
            u = _dot(h, wgu_ref[1, :, cols])
            g_ref[:, cols] = g.astype(BF)
            u_ref[:, cols] = u.astype(BF)
            act = g * _sigmoid(g) * u
            f = f + _dot(act.astype(BF), wd_ref[cols, :])
        f_ref[...] = f
        x2 = x + _rms_fwd(f, qg_ref[layer:layer + 1, :])
        x2_ref[...] = x2
        if head:
            step = pl.program_id(0)

            @pl.when(step == 0)
            def _():
                for ref in (lv_ref, deg_ref, deq_ref, gacc, pacc):
                    ref[...] = jnp.zeros_like(ref)
            p_blk = p_ref[...]
            z, pe, x3 = _ple_math(layer, x2, p_blk, eg_ref, wg_ref, wp_ref, eq_ref)
            err = x3 - t_ref[...]
            lv_ref[...] += _rowsum(err * err)
            dx, dwp, dwg, deg, deq = _ple_bwd_math(layer, err * (1.0 / D), x2, z, pe, p_blk, eg_ref, wg_ref, eq_ref)
            dx_ref[...] = dx
            pacc[...] += dwp
            gacc[...] += dwg
            deg_ref[...] += deg
            deq_ref[...] += deq

            @pl.when(step == s_len // TM - 1)
            def _():
                dwg_ref[...] = gacc[...].astype(BF)
                dwp_ref[...] = pacc[...].astype(BF)

    in_specs = [_row_spec(TM), VSPEC, VSPEC, VSPEC, VSPEC]
    args = [x1, pre_g, wgu, wd, post_g]
    out_specs = [_row_spec(TM), _row_spec(TM), _row_spec(TM, FF), _row_spec(TM, FF)]
    out_shape = [_sds((s_len, D)), _sds((s_len, D)), _sds((s_len, FF), BF), _sds((s_len, FF), BF)]
    scratch = []
    if head:
        p, ple_g, w_gate, w_proj, ple_post_g, target = head
        in_specs += [pl.BlockSpec((None, TM, PLE), lambda i: (layer, i, 0)), VSPEC, VSPEC, VSPEC, VSPEC, _row_spec(TM)]
        args += [p, ple_g, w_gate, w_proj, ple_post_g, target]
        out_specs += [_row_spec(TM), _const_spec((D, D)), _const_spec((PLE, D))] + [_const_spec((1, D))] * 3
        out_shape += [_sds((s_len, D)), _sds((D, D), BF), _sds((PLE, D), BF)] + [_sds((1, D))] * 3
        scratch = [pltpu.VMEM((D, D), F32), pltpu.VMEM((PLE, D), F32)]
    return _call(body, name=f"ffn_fwd{layer}", grid=(s_len // TM,), in_specs=in_specs, out_specs=out_specs,
                 out_shape=out_shape, args=args, scratch_shapes=scratch, rider=rider)


GU_PIECE = 128
DN_PIECE = 64
DN_SLOT = FF // N_CHIPS
HALF_D = D // 2
CHUNK_STRIDE = 6
CHUNK_START = (1, 7, 4, 10)


def _ffn_bwd(layer, dx2, x1, f, g_pre, u_pre, pre_g, wgu, wd, post_g, kc, rider=None):
    s_len = x1.shape[0]
    tm = TM_FFN_BWD
    n = s_len // tm
    nc = FF // FF_CHUNK
    n_gu, n_dn = FF_CHUNK // GU_PIECE, FF_CHUNK // DN_PIECE
    n_pieces = 2 * n_gu + n_dn
    n_blk = FF_HALF // GU_PIECE

    def edge_rows(c, i, kc_ref):
        return (jnp.where((c == 0) | (c == nc - 1), i, n - 1), 0)

    def chunk_at(c, kc_ref):
        k = kc_ref[0]
        start = jnp.where(k == 0, CHUNK_START[0], jnp.where(k == 1, CHUNK_START[1],
                                                            jnp.where(k == 2, CHUNK_START[2], CHUNK_START[3])))
        return ((c + start) * CHUNK_STRIDE) % nc

    def exchange(kc_ref, c, accg, accu, accd, own_gu_ref, land_gu_ref, own_dn_ref, land_dn_ref,
                 pl_gu, pl_dn, sib_gu, sib_dn, mine_gu, mine_dn, sum_gu, sum_dn,
                 psend, precv, ssend, lsem, rrecv):
        x, y, core = lax.axis_index("x"), lax.axis_index("y"), lax.axis_index("c")
        lower = core == 0

        def pair_copy(cc, part):
            p = cc % 2
            src, dst = ((sib_gu, pl_gu), (sib_dn, pl_dn))[part]
            return pltpu.make_async_remote_copy(src.at[p], dst.at[cc], psend.at[p, part], precv.at[cc, part],
                                                device_id=(x, y, 1 - core), device_id_type=MESH)

        def scatter(cc, wait):
            p = cc % 2
            hidden = chunk_at(cc, kc_ref) * FF_CHUNK

            assert n_gu == 2
            k0, k1 = hidden // FF_HALF, (hidden + GU_PIECE) // FF_HALF
            blk = (hidden - k0 * FF_HALF) // GU_PIECE
            for gu in range(2):
                @pl.when(k0 == k1)
                def _():
                    piece(p, wait, 2 * gu, sum_gu.at[p, gu], k0 + 2 * gu, 0, (pl.ds(blk, 2),))

                @pl.when(k0 != k1)
                def _():
                    piece(p, wait, 2 * gu, sum_gu.at[p, gu, 0], k0 + 2 * gu, 0, (blk,))
                    piece(p, wait, 2 * gu + 1, sum_gu.at[p, gu, 1], k1 + 2 * gu, 0, (0,))

            kd = hidden // DN_SLOT
            off = pl.multiple_of(hidden - kd * DN_SLOT, DN_PIECE)
            m = jnp.minimum((DN_SLOT - off) // DN_PIECE, n_dn)
            for mm in range(1, n_dn + 1):
                @pl.when(m == mm)
                def _():
                    rows = mm * DN_PIECE
                    piece(p, wait, 2 * n_gu, sum_dn.at[p, pl.ds(0, rows), :], kd, 1, (pl.ds(off, rows), slice(None)))
                    if mm < n_dn:
                        piece(p, wait, 2 * n_gu + 1, sum_dn.at[p, pl.ds(rows, FF_CHUNK - rows), :], kd + 1, 1,
                              (pl.ds(0, FF_CHUNK - rows), slice(None)))

        def piece(p, wait, pi, src, k, t, where):
            own_ref, land_ref = ((own_gu_ref, land_gu_ref), (own_dn_ref, land_dn_ref))[t]
            kx, ky = k // 2, k % 2
            fx, fy = (kx != x).astype(jnp.int32), (ky != y).astype(jnp.int32)
            local = (fx + fy) == 0
            j = jnp.maximum(fx + 2 * fy - 1, 0)

            @pl.when(local)
            def _():
                cp = pltpu.make_async_copy(src, own_ref.at[where], lsem.at[p, pi])
                if wait:
                    cp.wait()
                else:
                    cp.start()

            @pl.when(jnp.logical_not(local))
            def _():
                cp = pltpu.make_async_remote_copy(src, land_ref.at[(j,) + where], ssend.at[p, pi],
                                                  rrecv.at[t, j], device_id=(kx, ky, core), device_id_type=MESH)
                if wait:
                    cp.wait_send()
                else:
                    cp.start()

        def add_and_scatter(cc):
            p = cc % 2
            pair_copy(cc, 0).wait_recv()
            pair_copy(cc, 1).wait_recv()
            s_gu = (mine_gu[...] + pl_gu[cc].astype(F32)).astype(BF)
            for hc in range(n_gu):
                sum_gu[p, :, hc] = s_gu[:, :, hc * GU_PIECE:(hc + 1) * GU_PIECE]
            sum_dn[p] = (mine_dn[...] + pl_dn[cc].astype(F32)).astype(BF)
            scatter(cc, wait=False)

        @pl.when(c >= 1)
        def _():
            @pl.when(c >= 3)
            def _():
                scatter(c - 3, wait=True)
            add_and_scatter(c - 1)

        @pl.when(c >= 2)
        def _():
            pair_copy(c - 2, 0).wait_send()
            pair_copy(c - 2, 1).wait_send()

        p = c % 2
        my_rows = pl.ds(pl.multiple_of(core * HALF_D, HALF_D), HALF_D)
        sib_rows = pl.ds(pl.multiple_of((1 - core) * HALF_D, HALF_D), HALF_D)
        d_v = accd[...]
        sib_gu[p, 0] = accg[sib_rows, :].astype(BF)
        sib_gu[p, 1] = accu[sib_rows, :].astype(BF)
        sib_dn[p] = jnp.where(lower, d_v[:, HALF_D:], d_v[:, :HALF_D]).astype(BF)
        mine_gu[0] = accg[my_rows, :]
        mine_gu[1] = accu[my_rows, :]
        mine_dn[...] = jnp.where(lower, d_v[:, :HALF_D], d_v[:, HALF_D:])
        pair_copy(c, 0).start()
        pair_copy(c, 1).start()

        @pl.when(c == nc - 1)
        def _():
            scatter(nc - 3, wait=True)
            add_and_scatter(nc - 1)
            for cc in (nc - 2, nc - 1):
                pair_copy(cc, 0).wait_send()
                pair_copy(cc, 1).wait_send()
                scatter(cc, wait=True)
            for t, land_ref in enumerate((land_gu_ref, land_dn_ref)):
                for j in range(N_CHIPS - 1):
                    pltpu.make_async_remote_copy(land_ref.at[j], land_ref.at[j], ssend.at[0, 0], rrecv.at[t, j],
                                                 device_id=(x, y, core), device_id_type=MESH).wait_recv()

    def body(kc_ref, dx_ref, x_ref, f_ref, gp_ref, up_ref, pg_ref, wgu_ref, wd_ref, qg_ref,
             dx1_ref, dpg_ref, dqg_ref, own_gu_ref, land_gu_ref, own_dn_ref, land_dn_ref,
             h_s, df_s, dh_s, accg, accu, accd, *comm):
        c = pl.program_id(0)
        i = pl.program_id(1)
        rows = pl.ds(pl.multiple_of(i * tm, tm), tm)
        pg = pg_ref[layer:layer + 1, :]

        @pl.when((c == 0) & (i == 0))
        def _():
            dpg_ref[...] = jnp.zeros_like(dpg_ref)
            dqg_ref[...] = jnp.zeros_like(dqg_ref)

        @pl.when(c == 0)
        def _():
            h_s[rows, :] = _rms_fwd(x_ref[...], pg).astype(BF)
            df, prod = _rms_bwd(f_ref[...], qg_ref[layer:layer + 1, :], dx_ref[...])
            df_s[rows, :] = df.astype(BF)
            dqg_ref[...] += _rowsum(prod)

        @pl.when(i == 0)
        def _():
            accg[...] = jnp.zeros_like(accg)
            accu[...] = jnp.zeros_like(accu)
            accd[...] = jnp.zeros_like(accd)

        h = h_s[rows, :]
        df = df_s[rows, :]
        wg = wgu_ref[0]
        wu = wgu_ref[1]
        g = gp_ref[...].astype(F32)
        u = up_ref[...].astype(F32)
        sg = _sigmoid(g)
        a = g * sg
        dact = _dot_nt(df, wd_ref[...])
        accd[...] += _dot_tn((a * u).astype(BF), df)
        du = (dact * a).astype(BF)
        dg = (dact * u * (sg * (1.0 + g * (1.0 - sg)))).astype(BF)
        accg[...] += _dot_tn(h, dg)
        accu[...] += _dot_tn(h, du)
        dh = _dot_nt(dg, wg) + _dot_nt(du, wu)

        @pl.when(c == 0)
        def _():
            dh_s[rows, :] = dh

        @pl.when((c > 0) & (c < nc - 1))
        def _():
            dh_s[rows, :] += dh

        @pl.when(c == nc - 1)
        def _():
            dxp, prod = _rms_bwd(x_ref[...], pg, dh_s[rows, :] + dh)
            dpg_ref[...] += _rowsum(prod)
            dx1_ref[...] = dx_ref[...] + dxp

        @pl.when(i == n - 1)
        def _():
            exchange(kc_ref, c, accg, accu, accd, own_gu_ref, land_gu_ref, own_dn_ref, land_dn_ref, *comm)

    dma = pltpu.SemaphoreType.DMA
    return _call(
        body, name=f"ffn_bwd{layer}", grid=(nc, n),
        in_specs=[pl.BlockSpec((tm, D), edge_rows), pl.BlockSpec((tm, D), edge_rows),
                  pl.BlockSpec((tm, D), lambda c, i, kc_ref: (jnp.where(c == 0, i, n - 1), 0),
                               pipeline_mode=pl.Buffered(1)),
                  pl.BlockSpec((tm, FF_CHUNK), lambda c, i, kc_ref: (i, chunk_at(c, kc_ref))),
                  pl.BlockSpec((tm, FF_CHUNK), lambda c, i, kc_ref: (i, chunk_at(c, kc_ref))),
                  VSPEC,
                  pl.BlockSpec((2, D, FF_CHUNK), lambda c, i, kc_ref: (0, 0, chunk_at(c, kc_ref))),
                  pl.BlockSpec((FF_CHUNK, D), lambda c, i, kc_ref: (chunk_at(c, kc_ref), 0)),
                  VSPEC],
        out_specs=[pl.BlockSpec((tm, D), lambda c, i, kc_ref: (jnp.where(c == nc - 1, i, 0), 0)),
                   _const_spec((1, D)), _const_spec((1, D)), ANYSPEC, ANYSPEC, ANYSPEC, ANYSPEC],
        out_shape=[_sds((s_len, D)), _sds((1, D)), _sds((1, D)),
                   _sds((n_blk, HALF_D, GU_PIECE), BF), _sds((N_CHIPS - 1, n_blk, HALF_D, GU_PIECE), BF),
                   _sds((DN_SLOT, HALF_D), BF), _sds((N_CHIPS - 1, DN_SLOT, HALF_D), BF)],
        scratch_shapes=[pltpu.VMEM((s_len, D), BF), pltpu.VMEM((s_len, D), BF), pltpu.VMEM((s_len, D), F32),
                        pltpu.VMEM((D, FF_CHUNK), F32), pltpu.VMEM((D, FF_CHUNK), F32),
                        pltpu.VMEM((FF_CHUNK, D), F32),
                        pltpu.VMEM((nc, 2, HALF_D, FF_CHUNK), BF), pltpu.VMEM((nc, FF_CHUNK, HALF_D), BF),
                        pltpu.VMEM((2, 2, HALF_D, FF_CHUNK), BF), pltpu.VMEM((2, FF_CHUNK, HALF_D), BF),
                        pltpu.VMEM((2, HALF_D, FF_CHUNK), F32), pltpu.VMEM((FF_CHUNK, HALF_D), F32),
                        pltpu.VMEM((2, 2, n_gu, HALF_D, GU_PIECE), BF), pltpu.VMEM((2, FF_CHUNK, HALF_D), BF),
                        dma((2, 2)), dma((nc, 2)), dma((2, n_pieces)), dma((2, n_pieces)), dma((2, N_CHIPS - 1))],
        args=[dx2, x1, f, g_pre, u_pre, pre_g, wgu, wd, post_g], rider=rider, prefetch=kc)


def _ple_fwd(layer, x2, p, ple_g, w_gate, w_proj, post_g, qkv=None, rider=None):
    s_len = x2.shape[0]

    def body(*refs):
        if qkv:
            (x_ref, p_ref, g_ref, wg_ref, wp_ref, qg_ref, ng_ref, kg_ref, wq_ref, wkv_ref,
             z_ref, pe_ref, x3_ref, q_ref, kv_ref) = refs
        else:
            x_ref, p_ref, g_ref, wg_ref, wp_ref, qg_ref, z_ref, pe_ref, x3_ref = refs
        z, pe, x3 = _ple_math(layer, x_ref[...], p_ref[...], g_ref, wg_ref, wp_ref, qg_ref)
        z_ref[...] = z
        pe_ref[...] = pe
        x3_ref[...] = x3
        if qkv:
            q_ref[...] = _dot(_rms_fwd(x3, ng_ref[layer + 1:layer + 2, :]).astype(BF), wq_ref[...]).astype(BF)
            kv_ref[...] = _dot(_rms_fwd(x3, kg_ref[...]).astype(BF), wkv_ref[...]).astype(BF)

    p_spec = pl.BlockSpec((None, TM, PLE), lambda i: (layer, i, 0))
    in_specs = [_row_spec(TM), p_spec, VSPEC, VSPEC, VSPEC, VSPEC]
    args = [x2, p, ple_g, w_gate, w_proj, post_g]
    out_specs = [_row_spec(TM), _row_spec(TM), _row_spec(TM)]
    out_shape = [_sds((s_len, D))] * 3
    if qkv:
        in_specs += [VSPEC] * 4
        args += list(qkv)
        out_specs += [_row_spec(TM), _row_spec(TM, 2 * KVD)]
        out_shape += [_sds((s_len, D), BF), _sds((s_len, 2 * KVD), BF)]
    return _call(body, name=f"ple_fwd{layer}", grid=(s_len // TM,), in_specs=in_specs, out_specs=out_specs,
                 out_shape=out_shape, args=args, rider=rider)


def _ple_bwd(layer, dx3, x2, z, pe, p, ple_g, w_gate, post_g, rider=None):
    s_len = x2.shape[0]
    n = s_len // TM

    def body(dx_ref, x_ref, z_ref, pe_ref, p_ref, g_ref, wg_ref, qg_ref,
             dx2_ref, dwg_ref, dwp_ref, dg_ref, dqg_ref, gacc, pacc):
        i = pl.program_id(0)

        @pl.when(i == 0)
        def _():
            gacc[...] = jnp.zeros_like(gacc)
            pacc[...] = jnp.zeros_like(pacc)
            dg_ref[...] = jnp.zeros_like(dg_ref)
            dqg_ref[...] = jnp.zeros_like(dqg_ref)

        dx2, dwp, dwg, dg, dqg = _ple_bwd_math(layer, dx_ref[...], x_ref[...], z_ref[...], pe_ref[...], p_ref[...],
                                                g_ref, wg_ref, qg_ref)
        dx2_ref[...] = dx2
        pacc[...] += dwp
        gacc[...] += dwg
        dg_ref[...] += dg
        dqg_ref[...] += dqg

        @pl.when(i == n - 1)
        def _():
            dwg_ref[...] = gacc[...].astype(BF)
            dwp_ref[...] = pacc[...].astype(BF)

    p_spec = pl.BlockSpec((None, TM, PLE), lambda i: (layer, i, 0))
    return _call(
        body, name=f"ple_bwd{layer}", grid=(n,),
        in_specs=[_row_spec(TM), _row_spec(TM), _row_spec(TM), _row_spec(TM), p_spec, VSPEC, VSPEC, VSPEC],
        out_specs=[_row_spec(TM), _const_spec((D, D)), _const_spec((PLE, D)), _const_spec((1, D)), _const_spec((1, D))],
        out_shape=[_sds((s_len, D)), _sds((D, D), BF), _sds((PLE, D), BF), _sds((1, D)), _sds((1, D))],
        scratch_shapes=[pltpu.VMEM((D, D), F32), pltpu.VMEM((PLE, D), F32)],
        args=[dx3, x2, z, pe, p, ple_g, w_gate, post_g], rider=rider)


def _qkv_bwd(dq, dkv, x3, dx4, q_g, kv_g, w_q, w_kv):
    s_len = x3.shape[0]
    n = s_len // TM

    def body(dq_ref, dkv_ref, x_ref, dx_ref, qg_ref, kg_ref, wq_ref, wkv_ref,
             dx3_ref, dwq_ref, dwkv_ref, dqg_ref, dkg_ref, qacc, kacc):
        i = pl.program_id(0)

        @pl.when(i == 0)
        def _():
            qacc[...] = jnp.zeros_like(qacc)
            kacc[...] = jnp.zeros_like(kacc)
            dqg_ref[...] = jnp.zeros_like(dqg_ref)
            dkg_ref[...] = jnp.zeros_like(dkg_ref)

        x = x_ref[...]
        qg = qg_ref[1:2, :]
        kg = kg_ref[...]
        dq_v = dq_ref[...]
        dkv_v = dkv_ref[...].astype(BF)
        qacc[...] += _dot_tn(_rms_fwd(x, qg).astype(BF), dq_v)
        kacc[...] += _dot_tn(_rms_fwd(x, kg).astype(BF), dkv_v)
        dxq, prod_q = _rms_bwd(x, qg, _dot_nt(dq_v, wq_ref[...]))
        dxk, prod_k = _rms_bwd(x, kg, _dot_nt(dkv_v, wkv_ref[...]))
        dqg_ref[...] += _rowsum(prod_q)
        dkg_ref[...] += _rowsum(prod_k)
        dx3_ref[...] = dx_ref[...] + dxq + dxk

        @pl.when(i == n - 1)
        def _():
            dwq_ref[...] = qacc[...].astype(BF)
            dwkv_ref[...] = kacc[...].astype(BF)

    outs, _ = _call(
        body, name="qkv_bwd", grid=(n,),
        in_specs=[_row_spec(TM), _row_spec(TM, 2 * KVD), _row_spec(TM), _row_spec(TM), VSPEC, VSPEC, VSPEC, VSPEC],
        out_specs=[_row_spec(TM), _const_spec((D, D)), _const_spec((D, 2 * KVD)),
                   _const_spec((1, D)), _const_spec((1, D))],
        out_shape=[_sds((s_len, D)), _sds((D, D), BF), _sds((D, 2 * KVD), BF), _sds((1, D)), _sds((1, D))],
        scratch_shapes=[pltpu.VMEM((D, D), F32), pltpu.VMEM((D, 2 * KVD), F32)],
        args=[dq, dkv, x3, dx4, q_g, kv_g, w_q, w_kv])
    return outs


def _attn_group(i, q, kvw, sink_ref, g):
    rows = GQA * BLK
    heads = [GQA * g + j for j in range(GQA)]
    off = jnp.where(i > 0, BLK, 0)
    row = lax.broadcasted_iota(jnp.int32, (rows, 2 * BLK), 0)
    rel = (row % BLK) - lax.broadcasted_iota(jnp.int32, (rows, 2 * BLK), 1) + off
    valid = (rel >= 0) & (rel < BLK)
    head_of_row = lax.broadcasted_iota(jnp.int32, (rows, 1), 0) // BLK
    slope = jnp.zeros((rows, 1), F32)
    sink = jnp.zeros((rows, 1), F32)
    for j, h in enumerate(heads):
        slope = jnp.where(head_of_row == j, SLOPES[h], slope)
        sink = jnp.where(head_of_row == j, sink_ref[0, h], sink)
    qs = jnp.concatenate([q[:, h * HEAD_DIM:(h + 1) * HEAD_DIM] for h in heads], axis=0)
    k = kvw[:, g * HEAD_DIM:(g + 1) * HEAD_DIM]
    v = kvw[:, KVD + g * HEAD_DIM:KVD + (g + 1) * HEAD_DIM]
    s = _dot_nt(qs, k) * ATT_SCALE - slope * rel.astype(F32)
    s = jnp.where(valid, s, NEG_INF)
    m = jnp.maximum(jnp.max(s, axis=-1, keepdims=True), sink)
    e = jnp.exp(s - m)
    es = jnp.exp(sink - m)
    inv = 1.0 / (jnp.sum(e, axis=-1, keepdims=True) + es)
    return e * inv, es * inv, qs, k, v


def _unstack_heads(stacked):
    return [stacked[j * BLK:(j + 1) * BLK, :] for j in range(GQA)]


def _kv_window(kv_ref, i):
    ks = pl.multiple_of(jnp.maximum(i * BLK - BLK, 0), BLK)
    return ks, kv_ref[pl.ds(ks, 2 * BLK), :]


def _attn_fwd(q, kv, sinks, x3, w_o, post_g, rider=None):
    s_len = q.shape[0]

    def body(q_ref, kv_ref, sk_ref, x_ref, wo_ref, g_ref, a_ref, y_ref, x4_ref):
        i = pl.program_id(0)
        _, kvw = _kv_window(kv_ref, i)
        q = q_ref[...]
        outs = []
        for g in range(N_KV_HEADS):
            p, _, _, _, v = _attn_group(i, q, kvw, sk_ref, g)
            outs += _unstack_heads(_dot(p.astype(BF), v))
        attn = jnp.concatenate(outs, axis=1)
        a_ref[...] = attn
        y = _dot(attn.astype(BF), wo_ref[...])
        y_ref[...] = y
        x4_ref[...] = x_ref[...] + _rms_fwd(y, g_ref[1:2, :])

    return _call(body, name="attn_fwd", grid=(s_len // BLK,),
                 in_specs=[_row_spec(BLK), VSPEC, SSPEC, _row_spec(BLK), VSPEC, VSPEC],
                 out_specs=[_row_spec(BLK)] * 3, out_shape=[_sds((s_len, D))] * 3,
                 args=[q, kv, sinks, x3, w_o, post_g], rider=rider)


ATT_STEP_BLOCKS = 2


def _attn_bwd(dx4, y, attn, q, kv, sinks, w_o, post_g, rider=None):
    s_len = q.shape[0]
    rows = ATT_STEP_BLOCKS * BLK
    n = s_len // rows

    def body(dx_ref, y_ref, a_ref, q_ref, kv_ref, sk_ref, wo_ref, g_ref,
             dq_ref, dkv_ref, dwo_ref, dg_ref, dsk_ref, wacc):
        i = pl.program_id(0)

        @pl.when(i == 0)
        def _():
            dkv_ref[...] = jnp.zeros_like(dkv_ref)
            wacc[...] = jnp.zeros_like(wacc)
            dg_ref[...] = jnp.zeros_like(dg_ref)
            dsk_ref[...] = jnp.zeros_like(dsk_ref)

        dy, prod = _rms_bwd(y_ref[...], g_ref[1:2, :], dx_ref[...])
        dg_ref[...] += _rowsum(prod)
        dyb = dy.astype(BF)
        attn_all = a_ref[...]
        wacc[...] += _dot_tn(attn_all.astype(BF), dyb)
        d_o_all = _dot_nt(dyb, wo_ref[...])
        q_all = q_ref[...]
        lane = lax.broadcasted_iota(jnp.int32, (1, D), 1)
        dsk = jnp.zeros((1, D), F32)
        for sub in range(ATT_STEP_BLOCKS):
            blk = i * ATT_STEP_BLOCKS + sub
            sl = slice(sub * BLK, (sub + 1) * BLK)
            d_o, q = d_o_all[sl, :], q_all[sl, :]
            dod = d_o * attn_all[sl, :]
            ks, kvw = _kv_window(kv_ref, blk)
            dqs, dks, dvs = [], [], []
            for g in range(N_KV_HEADS):
                p, ps, qs, k, v = _attn_group(blk, q, kvw, sk_ref, g)
                cols = [slice((GQA * g + j) * HEAD_DIM, (GQA * g + j + 1) * HEAD_DIM) for j in range(GQA)]
                do_s = jnp.concatenate([d_o[:, c] for c in cols], axis=0).astype(BF)
                dsum = jnp.concatenate([jnp.sum(dod[:, c], axis=-1, keepdims=True) for c in cols], axis=0)
                dp = _dot_nt(do_s, v)
                dsb = (p * (dp - dsum) * ATT_SCALE).astype(BF)
                sink_part = ps * dsum
                for j in range(GQA):
                    dsk = dsk + jnp.where(lane == GQA * g + j, -_rowsum(sink_part[j * BLK:(j + 1) * BLK, :]), 0.0)
                dqs += _unstack_heads(_dot(dsb, k))
                dks.append(_dot_tn(dsb, qs))
                dvs.append(_dot_tn(p.astype(BF), do_s))
            dq_ref[sl, :] = jnp.concatenate(dqs, axis=1).astype(BF)
            dkv_ref[pl.ds(ks, 2 * BLK), :] += jnp.concatenate(dks + dvs, axis=1)
        dsk_ref[...] += dsk

        @pl.when(i == n - 1)
        def _():
            dwo_ref[...] = wacc[...].astype(BF)

    return _call(
        body, name="attn_bwd", grid=(n,),
        in_specs=[_row_spec(rows), _row_spec(rows), _row_spec(rows), _row_spec(rows), VSPEC, SSPEC, VSPEC, VSPEC],
        out_specs=[_row_spec(rows), _const_spec((s_len, 2 * KVD)), _const_spec((D, D)),
                   _const_spec((1, D)), _const_spec((1, D))],
        out_shape=[_sds((s_len, D), BF), _sds((s_len, 2 * KVD)), _sds((D, D), BF), _sds((1, D)), _sds((1, D))],
        scratch_shapes=[pltpu.VMEM((D, D), F32)],
        args=[dx4, y, attn, q, kv, sinks, w_o, post_g], rider=rider)


Big = collections.namedtuple("Big", "name src layer L A R C rb")


def _bigs():
    out = {"pool_w": Big("pool_w", "pool_w", None, 4, 4, POOL_G // N_CHIPS, POOL_G, 32)}
    for l in range(2):
        out[f"w_gu{l}"] = Big(f"w_gu{l}", "w_gu", l, 1, 2, D, FF_HALF, 256)
        out[f"w_down{l}"] = Big(f"w_down{l}", "w_down", l, 1, 4, FF // N_CHIPS, D, 352)
        out[f"w_ple_gate{l}"] = Big(f"w_ple_gate{l}", "w_ple_gate", l, 1, 4, D // N_CHIPS, D, 128)
        out[f"w_ple_proj{l}"] = Big(f"w_ple_proj{l}", "w_ple_proj", l, 1, 1, PLE, D // N_CHIPS, 128)
    out["w_q"] = Big("w_q", "w_q", None, 1, 4, D // N_CHIPS, D, 128)
    out["w_o"] = Big("w_o", "w_o", None, 1, 4, D // N_CHIPS, D, 128)
    out["w_kv"] = Big("w_kv", "w_kv", None, 1, 4, D // N_CHIPS, 2 * KVD, 128)
    return out


BIGS = _bigs()
POOL_SCALE = Big("pool_scale", "pool_scale", None, 1, 1, 1, D // N_CHIPS, 1)
BIG_SOURCES = ("w_gu", "w_down", "w_ple_gate", "w_ple_proj", "w_q", "w_o", "w_kv", "pool_w")


def _ncb(t):
    return N_CHIPS // t.A


def _full_shape(t, rows=None):
    return (t.L, t.A, t.R if rows is None else rows, _ncb(t) * t.C)


def _slot_index(t, k):
    return k // _ncb(t), k % _ncb(t)


def _slot(ref, t, k, row0, rows):
    a, cb = _slot_index(t, k)
    return ref.at[:, a, pl.ds(row0, rows), pl.ds(pl.multiple_of(cb * t.C, 128), t.C)]


def _place_job(t, w, out_dtype=BF):
    nb = next((nb for nb in (8, 4, 2, 1) if t.R % (16 * nb) == 0), 1) if t.L == 1 else 1
    rb = t.R // nb

    def fn(j, kc_ref, ins, outs):
        outs[0][...] = ins[0][...].astype(out_dtype)

    def in_map(j, kc_ref):
        return (j // nb if t.layer is None else t.layer, j % nb, 0)

    def out_map(j, kc_ref):
        a, cb = _slot_index(t, kc_ref[0])
        return (j // nb, a, j % nb, cb)

    return Job(t.L * nb, [(w, (None, rb, t.C), in_map)],
               [(_sds(_full_shape(t), out_dtype), (None, None, rb, t.C), out_map)], fn)


def _mesh_position():
    x, y, c = lax.axis_index("x"), lax.axis_index("y"), lax.axis_index("c")
    chips = [(1 - x, y), (x, 1 - y), (1 - x, 1 - y)]
    return x, y, c, chips


DIRECT_BELOW = 1024


def _gather_rider(parts, fulls):
    nt = len(parts)
    TO_X, TO_Y, FWD_X, FWD_Y, SIB_X, SIB_Y, SIB_D = range(7)

    def rows_of(ti, core):
        t, r0, r1 = parts[ti]
        h = (r1 - r0) // 2
        return r0 + core * h, h

    def copy(outs, sems, kind, ti, k_src, row0, rows, dev):
        region = _slot(outs[ti], parts[ti][0], k_src, row0, rows)
        return pltpu.make_async_remote_copy(region, region, sems[0].at[ti, kind], sems[1].at[ti, kind],
                                            device_id=dev, device_id_type=MESH)

    def plan(outs, sems):
        x, y, c, _ = _mesh_position()
        me, kx, ky, kd = 2 * x + y, 2 * (1 - x) + y, 2 * x + (1 - y), 2 * (1 - x) + (1 - y)
        dev_x, dev_y, dev_d, sib = (1 - x, y, c), (x, 1 - y, c), (1 - x, 1 - y, c), (x, y, 1 - c)

        def whole(ti):
            return 0, parts[ti][0].R

        def mk(kind, k_send, k_recv, dev, send_rows, recv_rows):
            def build(ti, side):
                k_src = k_send if side == "s" else k_recv
                row0, rows = (send_rows if side == "s" else recv_rows)(ti)
                return copy(outs, sems, kind, ti, k_src, row0, rows, dev)
            return build

        def first_half(core):
            return lambda ti: (rows_of(ti, core)[0], rows_of(ti, core)[1] // 2)

        def second_half(core):
            return lambda ti: (rows_of(ti, core)[0] + rows_of(ti, core)[1] // 2, rows_of(ti, core)[1] // 2)

        mine = lambda ti: rows_of(ti, c)
        theirs = lambda ti: rows_of(ti, 1 - c)
        split = {
            TO_X: mk(TO_X, me, kx, dev_x, mine, mine),
            TO_Y: mk(TO_Y, me, ky, dev_y, mine, mine),
            FWD_X: mk(FWD_X, ky, kd, dev_x, first_half(c), first_half(c)),
            FWD_Y: mk(FWD_Y, kx, kd, dev_y, second_half(c), second_half(c)),
            SIB_X: mk(SIB_X, kx, kx, sib, mine, theirs),
            SIB_Y: mk(SIB_Y, ky, ky, sib, mine, theirs),
            SIB_D: mk(SIB_D, kd, kd, sib, mine, theirs),
        }
        direct = {
            TO_X: mk(TO_X, me, kx, dev_x, whole, whole),
            TO_Y: mk(TO_Y, me, ky, dev_y, whole, whole),
            FWD_X: mk(FWD_X, me, kd, dev_d, whole, whole),
        }
        return split, direct

    is_split = [t.L * t.R * t.C >= DIRECT_BELOW for t, _, _ in parts]
    assert all(s or (r0, r1) == (0, t.R) for s, (t, r0, r1) in zip(is_split, parts))

    def start(ins, outs, sems):
        split, direct = plan(outs, sems)
        for ti in range(nt):
            kinds = split if is_split[ti] else direct
            kinds[TO_X](ti, "s").start()
            kinds[TO_Y](ti, "s").start()
            if not is_split[ti]:
                kinds[FWD_X](ti, "s").start()

    def mid(ins, outs, sems):
        split, _ = plan(outs, sems)
        for ti in range(nt):
            if is_split[ti]:
                split[TO_Y](ti, "r").wait_recv()
                split[FWD_X](ti, "s").start()
                split[SIB_Y](ti, "s").start()
        for ti in range(nt):
            if is_split[ti]:
                split[TO_X](ti, "r").wait_recv()
                split[FWD_Y](ti, "s").start()
                split[SIB_X](ti, "s").start()

    def finish(ins, outs, sems):
        split, direct = plan(outs, sems)
        for ti in range(nt):
            if is_split[ti]:
                split[FWD_X](ti, "r").wait_recv()
                split[FWD_Y](ti, "r").wait_recv()
                split[SIB_D](ti, "s").start()
            else:
                for kind in (TO_X, TO_Y, FWD_X):
                    direct[kind](ti, "r").wait_recv()
        for ti in range(nt):
            if is_split[ti]:
                for kind in (SIB_X, SIB_Y, SIB_D):
                    split[kind](ti, "r").wait_recv()
        for ti in range(nt):
            kinds = split if is_split[ti] else direct
            for kind in kinds:
                kinds[kind](ti, "s").wait_send()

    sems = pltpu.SemaphoreType.DMA((nt, 7))
    return Rider(list(fulls), [_sds(a.shape, a.dtype) for a in fulls], {i: i for i in range(nt)},
                 [sems, sems], start, mid, finish)


def _pair_exchange_rider(specs, grads):
    nt = len(specs)

    def copy(ins, outs, sems, ti, c, sibling):
        half = specs[ti].R // 2
        return pltpu.make_async_remote_copy(ins[ti].at[:, :, pl.ds((1 - c) * half, half), :], outs[ti],
                                            sems[0].at[ti], sems[1].at[ti], device_id=sibling, device_id_type=MESH)

    def start(ins, outs, sems):
        x, y, c, _ = _mesh_position()
        for ti in range(nt):
            copy(ins, outs, sems, ti, c, (x, y, 1 - c)).start()

    def finish(ins, outs, sems):
        x, y, c, _ = _mesh_position()
        for ti in range(nt):
            copy(ins, outs, sems, ti, c, (x, y, 1 - c)).wait()

    sems = pltpu.SemaphoreType.DMA((nt,))
    return Rider(list(grads), [_sds(_full_shape(t, t.R // 2), BF) for t in specs], {}, [sems, sems], start, None, finish)


def _pair_sum_job(t, g, land):
    assert t.L == 1
    half = t.R // 2
    nj = half // t.rb
    block = (None, t.A, t.rb, _ncb(t) * t.C)

    def fn(j, kc_ref, ins, outs):
        outs[0][...] = (ins[0][...].astype(F32) + ins[1][...].astype(F32)).astype(BF)

    return Job(nj,
               [(g, block, lambda j, kc_ref: (0, 0, kc_ref[1] * nj + j, 0)),
                (land, block, lambda j, kc_ref: (0, 0, j, 0))],
               [(_sds(_full_shape(t, half), BF), block, lambda j, kc_ref: (0, 0, j, 0))], fn)


def _scatter_rider(specs, sums):
    nt = len(specs)

    def copy(ins, outs, sems, ti, j, chip, c):
        t = specs[ti]
        cx, cy = chip
        return pltpu.make_async_remote_copy(_slot(ins[ti], t, 2 * cx + cy, 0, t.R // 2), outs[ti].at[j],
                                            sems[0].at[ti, j], sems[1].at[ti, j],
                                            device_id=(cx, cy, c), device_id_type=MESH)

    def start(ins, outs, sems):
        _, _, c, chips = _mesh_position()
        for j, chip in enumerate(chips):
            for ti in range(nt):
                copy(ins, outs, sems, ti, j, chip, c).start()

    def finish(ins, outs, sems):
        _, _, c, chips = _mesh_position()
        for j, chip in enumerate(chips):
            for ti in range(nt):
                copy(ins, outs, sems, ti, j, chip, c).wait()

    sems = pltpu.SemaphoreType.DMA((nt, N_CHIPS - 1))
    return Rider(list(sums), [_sds((N_CHIPS - 1, t.L, t.R // 2, t.C), BF) for t in specs], {}, [sems, sems],
                 start, None, finish)


def _chip_sum_job(ts, landed):
    t0 = ts[0]
    assert t0.L == 1
    half = t0.R // 2
    nj = half // t0.rb

    def local(j, li):
        return jnp.clip(j - li * nj, 0, nj - 1)

    ins = []
    for li, t in enumerate(ts):
        s, land = landed[t.name]

        def own_map(j, kc_ref, li=li, t=t):
            a, cb = _slot_index(t, kc_ref[0])
            return (0, a, local(j, li), cb)

        ins.append((s, (None, None, t.rb, t.C), own_map))
        ins.append((land, (N_CHIPS - 1, None, t.rb, t.C), lambda j, kc_ref, li=li: (0, 0, local(j, li), 0)))

    def fn(j, kc_ref, in_refs, outs):
        for li in range(len(ts)):
            @pl.when(j // nj == li)
            def _():
                acc = in_refs[2 * li][...].astype(F32)
                for k in range(N_CHIPS - 1):
                    acc = acc + in_refs[2 * li + 1][k].astype(F32)
                outs[0][...] = acc

    return Job(len(ts) * nj, ins,
               [(_sds((len(ts), t0.R, t0.C)), (None, t0.rb, t0.C),
                 lambda j, kc_ref: (j // nj, kc_ref[1] * nj + j % nj, 0))], fn)


def _adamw_job(rb, w, g, m, v):
    n_layers, r, c = w.shape
    nb = r // rb
    block = (None, rb, c)
    index = lambda j, kc_ref: (j // nb, j % nb, 0)

    def fn(j, kc_ref, ins, outs):
        g_v = ins[1][...]
        outs[0][...] = g_v
        outs[1][...], outs[2][...], outs[3][...] = _adamw_math(ins[0][...], g_v, ins[2][...], ins[3][...])

    return Job(n_layers * nb, [(a, block, index) for a in (w, g, m, v)],
               [(_sds(w.shape), block, index)] * 4, fn)


def _chip_sum_fused_job(ts, fused, by_cols):
    t0 = ts[0]
    own0 = fused[t0.name][0]
    if by_cols:
        rows, cols = own0.shape
    else:
        nb, rows, bw = own0.shape
        cols = nb * bw
    nj = rows // t0.rb

    def local(j, li):
        return jnp.clip(j - li * nj, 0, nj - 1)

    ins = []
    for li, t in enumerate(ts):
        own, land = fused[t.name]
        if by_cols:
            ins.append((own, (t.rb, cols), lambda j, kc_ref, li=li: (local(j, li), 0)))
            ins.append((land, (N_CHIPS - 1, t.rb, cols), lambda j, kc_ref, li=li: (0, local(j, li), 0)))
        else:
            ins.append((own, (nb, t.rb, bw), lambda j, kc_ref, li=li: (0, local(j, li), 0)))
            ins.append((land, (N_CHIPS - 1, nb, t.rb, bw), lambda j, kc_ref, li=li: (0, 0, local(j, li), 0)))

    def fn(j, kc_ref, in_refs, outs):
        for li in range(len(ts)):
            @pl.when(j // nj == li)
            def _():
                acc = in_refs[2 * li][...].astype(F32)
                for k in range(N_CHIPS - 1):
                    acc = acc + in_refs[2 * li + 1][k].astype(F32)
                outs[0][...] = acc if by_cols else jnp.concatenate([acc[b] for b in range(nb)], axis=1)

    def out_map(j, kc_ref):
        return (j // nj, j % nj, kc_ref[1]) if by_cols else (j // nj, kc_ref[1] * nj + j % nj, 0)

    return Job(len(ts) * nj, ins, [(_sds((len(ts), t0.R, t0.C)), (None, t0.rb, cols), out_map)], fn)


def _share_rider(halves, by_cols):
    nt = len(halves)

    def copy(outs, sems, ti, core, sibling):
        axis = 2 if by_cols[ti] else 1
        half = halves[ti].shape[axis] // 2
        piece = pl.ds(pl.multiple_of(core * half, 128 if by_cols[ti] else 8), half)
        part = outs[ti].at[:, :, piece] if by_cols[ti] else outs[ti].at[:, piece, :]
        return pltpu.make_async_remote_copy(part, part, sems[0].at[ti], sems[1].at[ti],
                                            device_id=sibling, device_id_type=MESH)

    def start(ins, outs, sems):
        x, y, c, _ = _mesh_position()
        for ti in range(nt):
            copy(outs, sems, ti, c, (x, y, 1 - c)).start()

    def finish(ins, outs, sems):
        x, y, c, _ = _mesh_position()
        for ti in range(nt):
            copy(outs, sems, ti, 1 - c, (x, y, 1 - c)).wait_recv()
        for ti in range(nt):
            copy(outs, sems, ti, c, (x, y, 1 - c)).wait_send()

    sems = pltpu.SemaphoreType.DMA((nt,))
    return Rider(list(halves), [_sds(a.shape, a.dtype) for a in halves], {i: i for i in range(nt)}, [sems, sems],
                 start, None, finish)


def _both(r1, r2):
    assert r1.mid is None and r2.mid is None
    ni, no, ns = len(r1.arrays), len(r1.out_shapes), len(r1.scratch)

    def split(fn1, fn2):
        def run(ins, outs, scr):
            fn1(ins[:ni], outs[:no], scr[:ns])
            fn2(ins[ni:], outs[no:], scr[ns:])
        return run

    aliases = dict(r1.aliases)
    aliases.update({ni + a: no + b for a, b in r2.aliases.items()})
    return Rider(r1.arrays + r2.arrays, r1.out_shapes + r2.out_shapes, aliases, r1.scratch + r2.scratch,
                 split(r1.start, r2.start), None, split(r1.finish, r2.finish))


def _adamw_math(w, g, m, v):
    m = B1 * m + (1.0 - B1) * g
    v = B2 * v + (1.0 - B2) * (g * g)
    delta = -LR * ((m / BC1) / (jnp.sqrt(v / BC2) + AEPS) + WD * w)
    return delta, m, v


GAIN_ROWS = {"pre_mix_g": 0, "post_mix_g": 2, "pre_ffn_g": 4, "post_ffn_g": 6, "ple_g": 8, "ple_post_g": 10}
ROW_KV_G, ROW_POOL_SCALE, ROW_SINKS, ROW_LOSS, PACK_ROWS = 12, 13, 14, 15, 16
SMALL_NAMES = tuple(GAIN_ROWS) + ("kv_g", "pool_scale", "sinks")


def _small_all_reduce(rows, dpool, rider=None):
    ng, pr = len(WINDOWS), POOL_G // N_CHIPS

    def body(*refs):
        row_refs = refs[:PACK_ROWS]
        dpool_ref, tot_ref, gpool_ref, pack, land, pland, send, recv, psend, precv = refs[PACK_ROWS:]
        x, y, c, _ = _mesh_position()
        me = 4 * x + 2 * y + c
        for r in range(PACK_ROWS):
            pack[r:r + 1, :] = row_refs[r][...]

        def shard_of(k):
            return dpool_ref.at[:, pl.ds(pl.multiple_of(k * pr, pr), pr), :]

        cps = []
        for j in range(1, N_DEV):
            px, py, pc = x ^ (j >> 2), y ^ ((j >> 1) & 1), c ^ (j & 1)
            cps.append(pltpu.make_async_remote_copy(pack, land.at[me], send.at[j], recv.at[j],
                                                    device_id=(px, py, pc), device_id_type=MESH))
            cps.append(pltpu.make_async_remote_copy(shard_of(2 * px + py), pland.at[me], psend.at[j], precv.at[j],
                                                    device_id=(px, py, pc), device_id_type=MESH))
        for cp in cps:
            cp.start()
        land[me] = pack[...]
        pland[me] = dpool_ref[:, pl.ds(pl.multiple_of((2 * x + y) * pr, pr), pr), :]
        for j in range(1, N_DEV):
            pltpu.make_async_remote_copy(pack, land.at[me ^ j], send.at[j], recv.at[j],
                                         device_id=(x, y, c), device_id_type=MESH).wait_recv()
            pltpu.make_async_remote_copy(shard_of(0), pland.at[me ^ j], psend.at[j], precv.at[j],
                                         device_id=(x, y, c), device_id_type=MESH).wait_recv()
        for cp in cps:
            cp.wait_send()
        tot = land[0]
        gp = pland[0].astype(F32)
        for d in range(1, N_DEV):
            tot = tot + land[d]
            gp = gp + pland[d].astype(F32)
        tot_ref[...] = tot
        gpool_ref[...] = gp

    sems = pltpu.SemaphoreType.DMA((N_DEV,))
    return _call(
        body, name="small_all_reduce", grid=(1,),
        in_specs=[VSPEC] * (PACK_ROWS + 1), out_specs=[VSPEC, VSPEC],
        out_shape=[_sds((PACK_ROWS, D)), _sds((ng, pr, POOL_G))],
        scratch_shapes=[pltpu.VMEM((PACK_ROWS, D), F32), pltpu.VMEM((N_DEV, PACK_ROWS, D), F32),
                        pltpu.VMEM((N_DEV, ng, pr, POOL_G), BF), sems, sems, sems, sems],
        args=[*rows, dpool], rider=rider)


def _small_adamw(tot, kc, small_w, small_m, small_v):
    names = SMALL_NAMES
    n = len(names)

    def body(*refs):
        tot_ref, kc_ref = refs[0], refs[1]
        w_refs = dict(zip(names, refs[2:2 + n]))
        m_refs = dict(zip(names, refs[2 + n:2 + 2 * n]))
        v_refs = dict(zip(names, refs[2 + 2 * n:2 + 3 * n]))
        loss_ref = refs[2 + 3 * n]
        out_refs = {nm: refs[3 + 3 * n + 4 * k: 7 + 3 * n + 4 * k] for k, nm in enumerate(names)}
        tot = tot_ref[...]
        loss_ref[...] = 0.5 * jnp.sum(tot[ROW_LOSS:ROW_LOSS + 1, :], axis=-1, keepdims=True) * (1.0 / D)

        def update(nm, g):
            g_ref, d_ref, nm_ref, nv_ref = out_refs[nm]
            g_ref[...] = g
            d_ref[...], nm_ref[...], nv_ref[...] = _adamw_math(w_refs[nm][...], g, m_refs[nm][...], v_refs[nm][...])

        for nm, r in GAIN_ROWS.items():
            update(nm, tot[r:r + 2, :])
        update("kv_g", tot[ROW_KV_G:ROW_KV_G + 1, :])
        k = kc_ref[0]
        width = D // N_CHIPS
        g_scale = jnp.zeros((1, width), F32)
        for kk in range(N_CHIPS):
            g_scale = g_scale + jnp.where(k == kk, tot[ROW_POOL_SCALE:ROW_POOL_SCALE + 1, kk * width:(kk + 1) * width], 0.0)
        update("pool_scale", g_scale)
        update("sinks", tot[ROW_SINKS:ROW_SINKS + 1, 0:N_HEADS])

    ins = [tot, kc] + [small_w[nm] for nm in names] + [small_m[nm] for nm in names] + [small_v[nm] for nm in names]
    out_shape = [_sds((1, 1))]
    for nm in names:
        out_shape += [_sds(small_w[nm].shape)] * 4
    outs = pl.pallas_call(
        body, name="small_adamw",
        in_specs=[VSPEC, SSPEC] + [VSPEC] * (3 * n), out_specs=[VSPEC] * len(out_shape), out_shape=out_shape,
        compiler_params=_params(),
    )(*ins)
    return outs[0], {nm: outs[1 + 4 * k: 5 + 4 * k] for k, nm in enumerate(names)}


def _compute_layout(t, full):
    if t.src == "w_gu":
        return full.reshape(2, D, FF)
    if t.src == "pool_w":
        return full.reshape(len(WINDOWS), POOL_G, POOL_G)
    if t.src == "pool_scale":
        return full.reshape(1, D)
    return full.reshape(t.A * t.R, _ncb(t) * t.C)


def kernel(x, p, pre_mix_g, post_mix_g, pre_ffn_g, post_ffn_g, pool_w, pool_scale, kv_g, w_kv, w_q, sinks, w_o, w_gu, w_down, ple_g, w_ple_gate, w_ple_proj, ple_post_g, loss_target, m_pre_mix_g, m_post_mix_g, m_pre_ffn_g, m_post_ffn_g, m_pool_w, m_pool_scale, m_kv_g, m_w_kv, m_w_q, m_sinks, m_w_o, m_w_gu, m_w_down, m_ple_g, m_w_ple_gate, m_w_ple_proj, m_ple_post_g, v_pre_mix_g, v_post_mix_g, v_pre_ffn_g, v_post_ffn_g, v_pool_w, v_pool_scale, v_kv_g, v_w_kv, v_w_q, v_sinks, v_w_o, v_w_gu, v_w_down, v_ple_g, v_w_ple_gate, v_w_ple_proj, v_ple_post_g):
    weights = dict(pre_mix_g=pre_mix_g, post_mix_g=post_mix_g, pre_ffn_g=pre_ffn_g, post_ffn_g=post_ffn_g,
                   pool_w=pool_w, pool_scale=pool_scale, kv_g=kv_g, w_kv=w_kv, w_q=w_q, sinks=sinks, w_o=w_o,
                   w_gu=w_gu, w_down=w_down, ple_g=ple_g, w_ple_gate=w_ple_gate, w_ple_proj=w_ple_proj,
                   ple_post_g=ple_post_g)
    m_in = dict(pre_mix_g=m_pre_mix_g, post_mix_g=m_post_mix_g, pre_ffn_g=m_pre_ffn_g, post_ffn_g=m_post_ffn_g,
                pool_w=m_pool_w, pool_scale=m_pool_scale, kv_g=m_kv_g, w_kv=m_w_kv, w_q=m_w_q, sinks=m_sinks,
                w_o=m_w_o, w_gu=m_w_gu, w_down=m_w_down, ple_g=m_ple_g, w_ple_gate=m_w_ple_gate,
                w_ple_proj=m_w_ple_proj, ple_post_g=m_ple_post_g)
    v_in = dict(pre_mix_g=v_pre_mix_g, post_mix_g=v_post_mix_g, pre_ffn_g=v_pre_ffn_g, post_ffn_g=v_post_ffn_g,
                pool_w=v_pool_w, pool_scale=v_pool_scale, kv_g=v_kv_g, w_kv=v_w_kv, w_q=v_w_q, sinks=v_sinks,
                w_o=v_w_o, w_gu=v_w_gu, w_down=v_w_down, ple_g=v_ple_g, w_ple_gate=v_w_ple_gate,
                w_ple_proj=v_w_ple_proj, ple_post_g=v_ple_post_g)
    order = ["pre_mix_g", "post_mix_g", "pre_ffn_g", "post_ffn_g", "pool_w", "pool_scale", "kv_g", "w_kv", "w_q",
             "sinks", "w_o", "w_gu", "w_down", "ple_g", "w_ple_gate", "w_ple_proj", "ple_post_g"]

    kc = jnp.stack([2 * lax.axis_index("x") + lax.axis_index("y"), lax.axis_index("c")]).astype(jnp.int32)
    s_len = x.shape[1]
    x2d = x.reshape(s_len, D)
    p3d = p.reshape(2, s_len, PLE)
    target = loss_target.reshape(s_len, D)
    kv_g2d = kv_g.reshape(1, D)
    gains = {nm: weights[nm] for nm in GAIN_ROWS}

    def shard_view(src, a):
        t = next(t for t in BIGS.values() if t.src == src)
        return a.reshape(-1, t.R, t.C)

    first, second = ["pool_w", "pool_scale"], ["w_gu0", "w_down0"]
    rest = [nm for nm in BIGS if nm not in first + second]
    specs = dict(BIGS, pool_scale=POOL_SCALE)
    placed = {}

    def place_job(nm):
        if nm == "pool_scale":
            return _place_job(POOL_SCALE, pool_scale.reshape(1, 1, D // N_CHIPS), F32)
        return _place_job(BIGS[nm], shard_view(BIGS[nm].src, weights[BIGS[nm].src]))

    def gather(names, rows=None):
        rows = rows or {}
        parts = [(specs[nm],) + tuple(rows.get(nm, (0, specs[nm].R))) for nm in names]
        return _gather_rider(parts, [placed[nm] for nm in names])

    def take(names, results):
        for nm, a in zip(names, results):
            placed[nm] = a

    def weight(nm):
        return _compute_layout(specs[nm], placed[nm])

    take(first, [r[0] for r in _multi_call("place_pool", [place_job(nm) for nm in first], kc)])
    cast, got = _multi_call("place_ffn0", [place_job(nm) for nm in second], kc, rider=gather(first))
    take(second, [r[0] for r in cast])
    take(first, got)
    jobs = [place_job(nm) for nm in rest]
    jobs.append(_mixa_fwd_job(x2d, gains["pre_mix_g"], weight("pool_w"), weight("pool_scale"), gains["post_mix_g"]))
    results, got = _multi_call("cast_and_mixa_fwd", jobs, kc, rider=gather(second))
    take(rest, [r[0] for r in results[:-1]])
    take(second, got)
    y0, x1 = results[-1]

    ride = ["w_ple_gate0", "w_ple_proj0", "w_q", "w_kv", "w_o", "w_gu1"]
    (f0, x2, g0, u0), got = _ffn_fwd(0, x1, gains["pre_ffn_g"], weight("w_gu0"), weight("w_down0"), gains["post_ffn_g"],
                             rider=gather(ride, {"w_gu1": (0, 320)}))
    take(ride, got)

    ride = ["w_ple_gate1", "w_ple_proj1", "w_gu1"]
    (z0, pe0, x3, q, kv), got = _ple_fwd(
        0, x2, p3d, gains["ple_g"], weight("w_ple_gate0"), weight("w_ple_proj0"), gains["ple_post_g"],
        qkv=(gains["pre_mix_g"], kv_g2d, weight("w_q"), weight("w_kv")),
        rider=gather(ride, {"w_gu1": (320, 704)}))
    take(ride, got)

    ride = ["w_down1", "w_gu1"]
    (attn, y1, x4), got = _attn_fwd(q, kv, sinks, x3, weight("w_o"), gains["post_mix_g"],
                                    rider=gather(ride, {"w_gu1": (704, D)}))
    take(ride, got)

    local = {}
    (f1, x5, g1, u1, dx5, local["w_ple_gate1"], local["w_ple_proj1"], d_ple1, d_plepost1, loss_row), _ = _ffn_fwd(
        1, x4, gains["pre_ffn_g"], weight("w_gu1"), weight("w_down1"), gains["post_ffn_g"],
        head=(p3d, gains["ple_g"], weight("w_ple_gate1"), weight("w_ple_proj1"), gains["ple_post_g"], target))

    landed = {}
    fused = {}

    def local_grads(names):
        return [local[nm].reshape(_full_shape(BIGS[nm])) for nm in names]

    def pair_exchange(names):
        return _pair_exchange_rider([BIGS[nm] for nm in names], local_grads(names))

    def pair_sum(tag, names, lands):
        jobs = [_pair_sum_job(BIGS[nm], g, l) for nm, g, l in zip(names, local_grads(names), lands)]
        return [r[0] for r in _multi_call(f"pair_sum_{tag}", jobs, kc)]

    def scatter(names, sums):
        return _scatter_rider([BIGS[nm] for nm in names], sums)

    def keep(names, sums, got):
        for nm, s, l in zip(names, sums, got):
            landed[nm] = (s, l)

    group_a = ["w_ple_gate1", "w_ple_proj1"]
    (dx4, d_preffn1, d_postffn1, *scattered), lands_a = _ffn_bwd(
        1, dx5, x4, f1, g1, u1, gains["pre_ffn_g"], weight("w_gu1"), weight("w_down1"), gains["post_ffn_g"], kc,
        rider=pair_exchange(group_a))
    fused["w_gu1"], fused["w_down1"] = scattered[0:2], scattered[2:4]

    (dq, dkv, local["w_o"], d_postmix1, d_sinks), _ = _attn_bwd(
        dx4, y1, attn, q, kv, sinks, weight("w_o"), gains["post_mix_g"])
    dx3, local["w_q"], local["w_kv"], d_premix1, d_kvg = _qkv_bwd(
        dq, dkv, x3, dx4, gains["pre_mix_g"], kv_g2d, weight("w_q"), weight("w_kv"))

    group_b = ["w_o", "w_q", "w_kv"]
    (dx2, local["w_ple_gate0"], local["w_ple_proj0"], d_ple0, d_plepost0), lands_b = _ple_bwd(
        0, dx3, x2, z0, pe0, p3d, gains["ple_g"], weight("w_ple_gate0"), gains["ple_post_g"],
        rider=pair_exchange(group_b))
    group_ab = group_a + group_b
    sums_ab = pair_sum("ab", group_ab, lands_a + lands_b)

    group_c = ["w_ple_gate0", "w_ple_proj0"]
    (dx1, d_preffn0, d_postffn0, *scattered), got = _ffn_bwd(
        0, dx2, x1, f0, g0, u0, gains["pre_ffn_g"], weight("w_gu0"), weight("w_down0"), gains["post_ffn_g"], kc,
        rider=_both(pair_exchange(group_c), scatter(group_ab, sums_ab)))
    fused["w_gu0"], fused["w_down0"] = scattered[0:2], scattered[2:4]
    lands_c = got[:len(group_c)]
    keep(group_ab, sums_ab, got[len(group_c):])

    layers_of = lambda src: [t for t in BIGS.values() if t.src == src]
    own_scatter = ["w_gu", "w_down"]
    early = own_scatter + ["w_q", "w_o", "w_kv"]
    late = ["w_ple_gate", "w_ple_proj"]
    by_cols = lambda srcs: [src == "w_down" for src in srcs]
    jobs = [_chip_sum_fused_job(layers_of(src), fused, by_cols=src == "w_down") for src in own_scatter]
    jobs += [_chip_sum_job(layers_of(src), landed) for src in early if src not in own_scatter]
    jobs_c = [_pair_sum_job(BIGS[nm], g, l) for nm, g, l in zip(group_c, local_grads(group_c), lands_c)]
    sums = [r[0] for r in _multi_call("chip_sum_early", jobs + jobs_c, kc)]
    halves, sums_c = sums[:len(jobs)], sums[len(jobs):]
    (dx0, d_pool, d_scale, d_postmix0, d_premix0), got = _mixa_bwd(
        dx1, x2d, y0, gains["pre_mix_g"], weight("pool_w"), weight("pool_scale"), gains["post_mix_g"],
        rider=_both(scatter(group_c, sums_c), _share_rider(halves, by_cols(early))))
    keep(group_c, sums_c, got[:len(group_c)])
    full_grads = dict(zip(early, got[len(group_c):]))

    rows = [d_premix0, d_premix1, d_postmix0, d_postmix1, d_preffn0, d_preffn1, d_postffn0, d_postffn1,
            d_ple0, d_ple1, d_plepost0, d_plepost1, d_kvg, d_scale, d_sinks, loss_row]
    as2d = lambda a: a.reshape(1, D) if a.ndim == 1 else a
    halves = [r[0] for r in _multi_call("chip_sum_late", [_chip_sum_job(layers_of(src), landed) for src in late], kc)]
    (tot, g_pool), got = _small_all_reduce(rows, d_pool, rider=_share_rider(halves, by_cols(late)))
    full_grads.update(zip(late, got))
    full_grads["pool_w"] = g_pool
    loss, small = _small_adamw(tot, kc, {nm: as2d(weights[nm]) for nm in SMALL_NAMES},
                               {nm: as2d(m_in[nm]) for nm in SMALL_NAMES},
                               {nm: as2d(v_in[nm]) for nm in SMALL_NAMES})


    def adam_job(src):
        rb = layers_of(src)[0].rb // (1 if src == "pool_w" else 2)
        return _adamw_job(rb, shard_view(src, weights[src]), full_grads[src],
                          shard_view(src, m_in[src]), shard_view(src, v_in[src]))

    out = {"grad": {}, "delta": {}, "new_m": {}, "new_v": {}}
    results = dict(zip(BIG_SOURCES, _multi_call("adamw", [adam_job(src) for src in BIG_SOURCES], kc)))
    for src in BIG_SOURCES:
        shape = weights[src].shape
        for kind, a in zip(("grad", "delta", "new_m", "new_v"), results[src]):
            out[kind][src] = a.reshape(shape)
    for nm in SMALL_NAMES:
        shape = weights[nm].shape
        for kind, a in zip(("grad", "delta", "new_m", "new_v"), small[nm]):
            out[kind][nm] = a.reshape(shape)

    return (loss.reshape(()), dx0.reshape(x.shape),
            *[out["grad"][nm] for nm in order], *[out["delta"][nm] for nm in order],
            *[out["new_m"][nm] for nm in order], *[out["new_v"][nm] for nm in order])
```

```python
import collections

import jax
import jax.numpy as jnp
from jax import lax
from jax.experimental import pallas as pl
from jax.experimental.pallas import tpu as pltpu

D = 1024
FF = 2816
N_HEADS = 16
HEAD_DIM = 64
N_KV_HEADS = 4
GQA = N_HEADS // N_KV_HEADS
KVD = N_KV_HEADS * HEAD_DIM
PLE = 256
BLK = 128
WINDOWS = (2, 4, 8, 16)
POOL_G = 256
HALO = 16
EPS = 1e-6
NEG_INF = -1e30
ATT_SCALE = HEAD_DIM ** -0.5
SLOPES = tuple(2.0 ** (-8.0 * (h + 1) / N_HEADS) for h in range(N_HEADS))
N_CHIPS = 4
N_DEV = 8

LR, B1, B2, AEPS, WD, STEP = 0.001, 0.9, 0.999, 1e-08, 0.01, 10
BC1 = 1.0 - B1 ** STEP
BC2 = 1.0 - B2 ** STEP

BF = jnp.bfloat16
F32 = jnp.float32
MESH = pl.DeviceIdType.MESH
VMEM_LIMIT_V7X = 58 * 1024 * 1024
TM = 256
TM_FFN_BWD = 512
FF_CHUNK = 256
FF_HALF = FF // 2

VSPEC = pl.BlockSpec(memory_space=pltpu.VMEM)
SSPEC = pl.BlockSpec(memory_space=pltpu.SMEM)
ANYSPEC = pl.BlockSpec(memory_space=pl.ANY)


def _params(n_grid=0):
    sem = ("arbitrary",) * n_grid if n_grid else None
    return pltpu.CompilerParams(dimension_semantics=sem, vmem_limit_bytes=VMEM_LIMIT_V7X)


def _sds(shape, dtype=F32):
    return jax.ShapeDtypeStruct(tuple(shape), dtype)


Rider = collections.namedtuple("Rider", "arrays out_shapes aliases scratch start mid finish")
MID_NUM, MID_DEN = 5, 8


def _call(body, *, name, grid, in_specs, out_specs, out_shape, args, scratch_shapes=(), rider=None, prefetch=None):
    ni, no, ns = len(in_specs), len(out_specs), len(scratch_shapes)
    npre = 0 if prefetch is None else 1
    pre = [] if prefetch is None else [prefetch]
    if rider is None:
        rider = Rider([], [], {}, [], None, None, None)
    ri, ro = len(rider.arrays), len(rider.out_shapes)

    def full(*refs):
        pre_refs, refs = refs[:npre], refs[npre:]
        ins, refs = refs[:ni], refs[ni:]
        rins, refs = refs[:ri], refs[ri:]
        outs, refs = refs[:no], refs[no:]
        routs, refs = refs[:ro], refs[ro:]
        scr, rscr = refs[:ns], refs[ns:]
        ids = [pl.program_id(a) for a in range(len(grid))]
        first = ids[0] == 0
        last = ids[0] == grid[0] - 1
        for a in range(1, len(grid)):
            first = first & (ids[a] == 0)
            last = last & (ids[a] == grid[a] - 1)

        if rider.start is not None:
            @pl.when(first)
            def _():
                rider.start(rins, routs, rscr)

        if rider.mid is not None:
            assert len(grid) == 1

            @pl.when(ids[0] == (grid[0] * MID_NUM) // MID_DEN)
            def _():
                rider.mid(rins, routs, rscr)

        body(*pre_refs, *ins, *outs, *scr)

        if rider.finish is not None:
            @pl.when(last)
            def _():
                rider.finish(rins, routs, rscr)

    outs = pl.pallas_call(
        full, name=name,
        grid_spec=pltpu.PrefetchScalarGridSpec(
            num_scalar_prefetch=npre, grid=grid,
            in_specs=list(in_specs) + [ANYSPEC] * ri, out_specs=list(out_specs) + [ANYSPEC] * ro,
            scratch_shapes=list(scratch_shapes) + list(rider.scratch)),
        out_shape=list(out_shape) + list(rider.out_shapes),
        input_output_aliases={npre + ni + a: no + b for a, b in rider.aliases.items()},
        compiler_params=_params(len(grid)))(*pre, *args, *rider.arrays)
    return list(outs[:no]), list(outs[no:])


Job = collections.namedtuple("Job", "steps ins outs fn")


def _multi_call(name, jobs, kc, rider=None):
    n = max(job.steps for job in jobs)

    def clamped(index, steps):
        return lambda s, kc_ref: index(jnp.minimum(s, steps - 1), kc_ref)

    in_specs, out_specs, out_shape, args = [], [], [], []
    for job in jobs:
        for arr, block, index, *single in job.ins:
            mode = dict(pipeline_mode=pl.Buffered(1)) if single and single[0] else {}
            in_specs.append(pl.BlockSpec(block, clamped(index, job.steps), **mode))
            args.append(arr)
        for sds, block, index in job.outs:
            out_specs.append(pl.BlockSpec(block, clamped(index, job.steps)))
            out_shape.append(sds)
    n_in = len(args)

    def body(kc_ref, *refs):
        s = pl.program_id(0)
        i0, o0 = 0, n_in
        for job in jobs:
            ins, outs = refs[i0:i0 + len(job.ins)], refs[o0:o0 + len(job.outs)]
            i0, o0 = i0 + len(job.ins), o0 + len(job.outs)

            @pl.when(s < job.steps)
            def _():
                job.fn(s, kc_ref, ins, outs)

    outs, routs = _call(body, name=name, grid=(n,), in_specs=in_specs, out_specs=out_specs, out_shape=out_shape,
                        args=args, prefetch=kc, rider=rider)
    res, o0 = [], 0
    for job in jobs:
        res.append(outs[o0:o0 + len(job.outs)])
        o0 += len(job.outs)
    return res if rider is None else (res, routs)


def _rms_fwd(x, g):
    r = lax.rsqrt(jnp.mean(x * x, axis=-1, keepdims=True) + EPS)
    return x * r * g


def _rms_bwd(x, g, dy):
    r = lax.rsqrt(jnp.mean(x * x, axis=-1, keepdims=True) + EPS)
    xn = x * r
    dxn = dy * g
    dx = r * (dxn - xn * jnp.mean(dxn * xn, axis=-1, keepdims=True))
    return dx, dy * xn


def _rowsum(a):
    return jnp.sum(a, axis=0, keepdims=True)


def _sigmoid(z):
    return 1.0 / (1.0 + jnp.exp(-z))


def _dot(a, b):
    return jnp.dot(a, b, preferred_element_type=F32)


def _dot_nt(a, b):
    return lax.dot_general(a, b, (((1,), (1,)), ((), ())), preferred_element_type=F32)


def _dot_tn(a, b):
    return lax.dot_general(a, b, (((0,), (0,)), ((), ())), preferred_element_type=F32)


def _row_spec(tm, width=D):
    return pl.BlockSpec((tm, width), lambda i: (i, 0))


def _const_spec(shape):
    zeros = (0,) * len(shape)
    return pl.BlockSpec(tuple(shape), lambda *_: zeros)


def _pool_delta(he, pos):
    out = []
    for gi, w in enumerate(WINDOWS):
        hg = he[:, gi * POOL_G:(gi + 1) * POOL_G]
        s = hg
        k = 1
        while k < w:
            s = s + pltpu.roll(s, k, 0)
            k *= 2
        cnt = jnp.maximum(jnp.minimum(pos + 1, w), 1).astype(F32)
        out.append(s / cnt - hg)
    return out


def _load_with_halo_before(x_ref, i, tm):
    r0 = pl.multiple_of(i * tm, tm)
    hs = pl.multiple_of(jnp.maximum(i * tm - HALO, 0), 8)
    xh = jnp.where(i > 0, x_ref[pl.ds(hs, HALO), :], 0.0)
    xt = x_ref[pl.ds(r0, tm), :]
    return xt, jnp.concatenate([xh, xt], axis=0)


def _mixa_fwd_job(x, pre_g, pool_w, pool_scale, post_g):
    s_len = x.shape[0]

    def fn(i, kc_ref, ins, outs):
        x_ref, pg_ref, w_ref, sc_ref, qg_ref = ins
        y_ref, x1_ref = outs
        xt, xe = _load_with_halo_before(x_ref, i, TM)
        he = _rms_fwd(xe, pg_ref[0:1, :])
        pos = i * TM - HALO + lax.broadcasted_iota(jnp.int32, (TM + HALO, 1), 0)
        ds = _pool_delta(he, pos)
        ys = [_dot(ds[gi][HALO:, :].astype(BF), w_ref[gi]) for gi in range(len(WINDOWS))]
        y = jnp.concatenate(ys, axis=1) * sc_ref[...]
        y_ref[...] = y
        x1_ref[...] = xt + _rms_fwd(y, qg_ref[0:1, :])

    def whole(a):
        zeros = (0,) * a.ndim
        return (a, a.shape, lambda j, kc_ref: zeros, True)

    rows = lambda j, kc_ref: (j, 0)
    return Job(s_len // TM, [whole(a) for a in (x, pre_g, pool_w, pool_scale, post_g)],
               [(_sds((s_len, D)), (TM, D), rows), (_sds((s_len, D)), (TM, D), rows)], fn)


def _mixa_bwd(dx1, x, y, pre_g, pool_w, pool_scale, post_g, rider=None):
    s_len = x.shape[0]
    n = s_len // TM
    ng = len(WINDOWS)

    def body(dx_ref, x_ref, y_ref, pg_ref, w_ref, sc_ref, qg_ref,
             dx0_ref, dw_ref, dsc_ref, dqg_ref, dpg_ref, wacc):
        i = pl.program_id(0)

        @pl.when(i == 0)
        def _():
            wacc[...] = jnp.zeros_like(wacc)
            dsc_ref[...] = jnp.zeros_like(dsc_ref)
            dqg_ref[...] = jnp.zeros_like(dqg_ref)
            dpg_ref[...] = jnp.zeros_like(dpg_ref)

        r0 = pl.multiple_of(i * TM, TM)
        xt, xe = _load_with_halo_before(x_ref, i, TM)
        he = _rms_fwd(xe, pg_ref[0:1, :])
        pos_b = i * TM - HALO + lax.broadcasted_iota(jnp.int32, (TM + HALO, 1), 0)
        ds = _pool_delta(he, pos_b)

        last = i == n - 1
        a0 = pl.multiple_of(jnp.minimum(i * TM + TM, s_len - HALO), 8)
        ye = jnp.concatenate([y_ref[pl.ds(r0, TM), :], y_ref[pl.ds(a0, HALO), :]], axis=0)
        dt = dx_ref[pl.ds(r0, TM), :]
        de = jnp.concatenate([dt, jnp.where(last, 0.0, dx_ref[pl.ds(a0, HALO), :])], axis=0)
        dye, prod = _rms_bwd(ye, qg_ref[0:1, :], de)
        dqg_ref[...] += _rowsum(prod[:TM, :])
        dys = dye * sc_ref[...]
        pos_a = i * TM + lax.broadcasted_iota(jnp.int32, (TM + HALO, 1), 0)

        dhs, dscs = [], []
        for gi, w in enumerate(WINDOWS):
            sl = slice(gi * POOL_G, (gi + 1) * POOL_G)
            wg = w_ref[gi]
            dys_g = dys[:, sl].astype(BF)
            d_g = ds[gi][HALO:, :].astype(BF)
            ypre = _dot(d_g, wg)
            dscs.append(_rowsum(dye[:TM, sl] * ypre))
            wacc[gi] += _dot_tn(d_g, dys_g[:TM, :])
            dd = _dot_nt(dys_g, wg)
            cnt = jnp.minimum(pos_a + 1, w).astype(F32)
            a = dd / cnt
            k = 1
            while k < w:
                a = a + pltpu.roll(a, TM + HALO - k, 0)
                k *= 2
            dhs.append(a[:TM, :] - dd[:TM, :])
        dsc_ref[...] += jnp.concatenate(dscs, axis=1)
        dh = jnp.concatenate(dhs, axis=1)
        dxp, prod2 = _rms_bwd(xt, pg_ref[0:1, :], dh)
        dpg_ref[...] += _rowsum(prod2)
        dx0_ref[...] = dt + dxp

        @pl.when(last)
        def _():
            dw_ref[...] = wacc[...].astype(BF)

    return _call(
        body, name="mixa_bwd", grid=(n,), in_specs=[VSPEC] * 7,
        out_specs=[_row_spec(TM), _const_spec((ng, POOL_G, POOL_G)), _const_spec((1, D)),
                   _const_spec((1, D)), _const_spec((1, D))],
        out_shape=[_sds((s_len, D)), _sds((ng, POOL_G, POOL_G), BF), _sds((1, D)), _sds((1, D)), _sds((1, D))],
        scratch_shapes=[pltpu.VMEM((ng, POOL_G, POOL_G), F32)],
        args=[dx1, x, y, pre_g, pool_w, pool_scale, post_g], rider=rider)


def _ple_math(layer, x, p_blk, g_ref, wg_ref, wp_ref, qg_ref):
    r = _rms_fwd(x, g_ref[layer:layer + 1, :]).astype(BF)
    z = _dot(r, wg_ref[...])
    pe = _dot(p_blk.astype(BF), wp_ref[...])
    return z, pe, x + _rms_fwd(pe * _sigmoid(z), qg_ref[layer:layer + 1, :])


def _ple_bwd_math(layer, dx, x, z, pe, p_blk, g_ref, wg_ref, qg_ref):
    gate = _sigmoid(z)
    de, prod = _rms_bwd(pe * gate, qg_ref[layer:layer + 1, :], dx)
    dpe = (de * gate).astype(BF)
    dz = (de * pe * gate * (1.0 - gate)).astype(BF)
    dwp = _dot_tn(p_blk.astype(BF), dpe)
    g = g_ref[layer:layer + 1, :]
    dwg = _dot_tn(_rms_fwd(x, g).astype(BF), dz)
    dxp, prod2 = _rms_bwd(x, g, _dot_nt(dz, wg_ref[...]))
    return dx + dxp, dwp, dwg, _rowsum(prod2), _rowsum(prod)


def _ffn_fwd(layer, x1, pre_g, wgu, wd, post_g, rider=None, head=None):
    s_len = x1.shape[0]

    def body(*refs):
        if head:
            (x_ref, pg_ref, wgu_ref, wd_ref, qg_ref, p_ref, eg_ref, wg_ref, wp_ref, eq_ref, t_ref,
             f_ref, g_ref, u_ref, dx_ref, dwg_ref, dwp_ref, deg_ref, deq_ref, lv_ref, gacc, pacc) = refs
        else:
            x_ref, pg_ref, wgu_ref, wd_ref, qg_ref, f_ref, x2_ref, g_ref, u_ref = refs
        x = x_ref[...]
        h = _rms_fwd(x, pg_ref[layer:layer + 1, :]).astype(BF)
        f = jnp.zeros((TM, D), F32)
        for c in range(FF // FF_HALF):
            cols = slice(c * FF_HALF, (c + 1) * FF_HALF)
            g = _dot(h, wgu_ref[0, :, cols])
            u = _dot(h, wgu_ref[1, :, cols])
            g_ref[:, cols] = g.astype(BF)
            u_ref[:, cols] = u.astype(BF)
            act = g * _sigmoid(g) * u
            f = f + _dot(act.astype(BF), wd_ref[cols, :])
        f_ref[...] = f
        x2 = x + _rms_fwd(f, qg_ref[layer:layer + 1, :])
        if not head:
            x2_ref[...] = x2
        else:
            step = pl.program_id(0)

            @pl.when(step == 0)
            def _():
                for ref in (lv_ref, deg_ref, deq_ref, gacc, pacc):
                    ref[...] = jnp.zeros_like(ref)
            p_blk = p_ref[...]
            z, pe, x3 = _ple_math(layer, x2, p_blk, eg_ref, wg_ref, wp_ref, eq_ref)
            err = x3 - t_ref[...]
            lv_ref[...] += _rowsum(err * err)
            dx, dwp, dwg, deg, deq = _ple_bwd_math(layer, err * (1.0 / D), x2, z, pe, p_blk, eg_ref, wg_ref, eq_ref)
            dx_ref[...] = dx
            pacc[...] += dwp
            gacc[...] += dwg
            deg_ref[...] += deg
            deq_ref[...] += deq

            @pl.when(step == s_len // TM - 1)
            def _():
                dwg_ref[...] = gacc[...].astype(BF)
                dwp_ref[...] = pacc[...].astype(BF)

    in_specs = [_row_spec(TM), VSPEC, VSPEC, VSPEC, VSPEC]
    args = [x1, pre_g, wgu, wd, post_g]
    out_specs = [_row_spec(TM), _row_spec(TM), _row_spec(TM, FF), _row_spec(TM, FF)]
    out_shape = [_sds((s_len, D)), _sds((s_len, D)), _sds((s_len, FF), BF), _sds((s_len, FF), BF)]
    scratch = []
    if head:
        del out_specs[1], out_shape[1]
        p, ple_g, w_gate, w_proj, ple_post_g, target = head
        in_specs += [pl.BlockSpec((None, TM, PLE), lambda i: (layer, i, 0)), VSPEC, VSPEC, VSPEC, VSPEC, _row_spec(TM)]
        args += [p, ple_g, w_gate, w_proj, ple_post_g, target]
        out_specs += [_row_spec(TM), _const_spec((D, D)), _const_spec((PLE, D))] + [_const_spec((1, D))] * 3
        out_shape += [_sds((s_len, D)), _sds((D, D), BF), _sds((PLE, D), BF)] + [_sds((1, D))] * 3
        scratch = [pltpu.VMEM((D, D), F32), pltpu.VMEM((PLE, D), F32)]
    return _call(body, name=f"ffn_fwd{layer}", grid=(s_len // TM,), in_specs=in_specs, out_specs=out_specs,
                 out_shape=out_shape, args=args, scratch_shapes=scratch, rider=rider)


GU_PIECE = 128
DN_PIECE = 64
DN_SLOT = FF // N_CHIPS
HALF_D = D // 2
CHUNK_STRIDE = 6
CHUNK_START = (1, 7, 4, 10)


def _ffn_bwd(layer, dx2, x1, f, g_pre, u_pre, pre_g, wgu, wd, post_g, kc, rider=None):
    s_len = x1.shape[0]
    tm = TM_FFN_BWD
    n = s_len // tm
    nc = FF // FF_CHUNK
    n_gu, n_dn = FF_CHUNK // GU_PIECE, FF_CHUNK // DN_PIECE
    n_pieces = 2 * n_gu + n_dn
    n_blk = FF_HALF // GU_PIECE

    def edge_rows(c, i, kc_ref):
        return (jnp.where((c == 0) | (c == nc - 1), i, n - 1), 0)

    def chunk_at(c, kc_ref):
        k = kc_ref[0]
        start = jnp.where(k == 0, CHUNK_START[0], jnp.where(k == 1, CHUNK_START[1],
                                                            jnp.where(k == 2, CHUNK_START[2], CHUNK_START[3])))
        return ((c + start) * CHUNK_STRIDE) % nc

    def exchange(kc_ref, c, accg, accu, accd, own_gu_ref, land_gu_ref, own_dn_ref, land_dn_ref,
                 pl_gu, pl_dn, sib_gu, sib_dn, mine_gu, mine_dn, sum_gu, sum_dn,
                 psend, precv, ssend, lsem, rrecv):
        x, y, core = lax.axis_index("x"), lax.axis_index("y"), lax.axis_index("c")
        lower = core == 0

        def pair_copy(cc, part):
            p = cc % 2
            src, dst = ((sib_gu, pl_gu), (sib_dn, pl_dn))[part]
            return pltpu.make_async_remote_copy(src.at[p], dst.at[cc], psend.at[p, part], precv.at[cc, part],
                                                device_id=(x, y, 1 - core), device_id_type=MESH)

        def scatter(cc, wait):
            p = cc % 2
            hidden = chunk_at(cc, kc_ref) * FF_CHUNK

            assert n_gu == 2
            k0, k1 = hidden // FF_HALF, (hidden + GU_PIECE) // FF_HALF
            blk = (hidden - k0 * FF_HALF) // GU_PIECE
            for gu in range(2):
                @pl.when(k0 == k1)
                def _():
                    piece(p, wait, 2 * gu, sum_gu.at[p, gu], k0 + 2 * gu, 0, (pl.ds(blk, 2),))

                @pl.when(k0 != k1)
                def _():
                    piece(p, wait, 2 * gu, sum_gu.at[p, gu, 0], k0 + 2 * gu, 0, (blk,))
                    piece(p, wait, 2 * gu + 1, sum_gu.at[p, gu, 1], k1 + 2 * gu, 0, (0,))

            kd = hidden // DN_SLOT
            off = pl.multiple_of(hidden - kd * DN_SLOT, DN_PIECE)
            m = jnp.minimum((DN_SLOT - off) // DN_PIECE, n_dn)
            for mm in range(1, n_dn + 1):
                @pl.when(m == mm)
                def _():
                    rows = mm * DN_PIECE
                    piece(p, wait, 2 * n_gu, sum_dn.at[p, pl.ds(0, rows), :], kd, 1, (pl.ds(off, rows), slice(None)))
                    if mm < n_dn:
                        piece(p, wait, 2 * n_gu + 1, sum_dn.at[p, pl.ds(rows, FF_CHUNK - rows), :], kd + 1, 1,
                              (pl.ds(0, FF_CHUNK - rows), slice(None)))

        def piece(p, wait, pi, src, k, t, where):
            own_ref, land_ref = ((own_gu_ref, land_gu_ref), (own_dn_ref, land_dn_ref))[t]
            kx, ky = k // 2, k % 2
            fx, fy = (kx != x).astype(jnp.int32), (ky != y).astype(jnp.int32)
            local = (fx + fy) == 0
            j = jnp.maximum(fx + 2 * fy - 1, 0)

            @pl.when(local)
            def _():
                cp = pltpu.make_async_copy(src, own_ref.at[where], lsem.at[p, pi])
                if wait:
                    cp.wait()
                else:
                    cp.start()

            @pl.when(jnp.logical_not(local))
            def _():
                cp = pltpu.make_async_remote_copy(src, land_ref.at[(j,) + where], ssend.at[p, pi],
                                                  rrecv.at[t, j], device_id=(kx, ky, core), device_id_type=MESH)
                if wait:
                    cp.wait_send()
                else:
                    cp.start()

        def add_and_scatter(cc):
            p = cc % 2
            pair_copy(cc, 0).wait_recv()
            pair_copy(cc, 1).wait_recv()
            s_gu = (mine_gu[...] + pl_gu[cc].astype(F32)).astype(BF)
            for hc in range(n_gu):
                sum_gu[p, :, hc] = s_gu[:, :, hc * GU_PIECE:(hc + 1) * GU_PIECE]
            sum_dn[p] = (mine_dn[...] + pl_dn[cc].astype(F32)).astype(BF)
            scatter(cc, wait=False)

        @pl.when(c >= 1)
        def _():
            @pl.when(c >= 3)
            def _():
                scatter(c - 3, wait=True)
            add_and_scatter(c - 1)

        @pl.when(c >= 2)
        def _():
            pair_copy(c - 2, 0).wait_send()
            pair_copy(c - 2, 1).wait_send()

        p = c % 2
        my_rows = pl.ds(pl.multiple_of(core * HALF_D, HALF_D), HALF_D)
        sib_rows = pl.ds(pl.multiple_of((1 - core) * HALF_D, HALF_D), HALF_D)
        d_v = accd[...]
        sib_gu[p, 0] = accg[sib_rows, :].astype(BF)
        sib_gu[p, 1] = accu[sib_rows, :].astype(BF)
        sib_dn[p] = jnp.where(lower, d_v[:, HALF_D:], d_v[:, :HALF_D]).astype(BF)
        mine_gu[0] = accg[my_rows, :]
        mine_gu[1] = accu[my_rows, :]
        mine_dn[...] = jnp.where(lower, d_v[:, :HALF_D], d_v[:, HALF_D:])
        pair_copy(c, 0).start()
        pair_copy(c, 1).start()

        @pl.when(c == nc - 1)
        def _():
            scatter(nc - 3, wait=True)
            add_and_scatter(nc - 1)
            for cc in (nc - 2, nc - 1):
                pair_copy(cc, 0).wait_send()
                pair_copy(cc, 1).wait_send()
                scatter(cc, wait=True)
            for t, land_ref in enumerate((land_gu_ref, land_dn_ref)):
                for j in range(N_CHIPS - 1):
                    pltpu.make_async_remote_copy(land_ref.at[j], land_ref.at[j], ssend.at[0, 0], rrecv.at[t, j],
                                                 device_id=(x, y, core), device_id_type=MESH).wait_recv()

    def body(kc_ref, dx_ref, x_ref, f_ref, gp_ref, up_ref, pg_ref, wgu_ref, wd_ref, qg_ref,
             dx1_ref, dpg_ref, dqg_ref, own_gu_ref, land_gu_ref, own_dn_ref, land_dn_ref,
             h_s, df_s, dh_s, accg, accu, accd, *comm):
        c = pl.program_id(0)
        i = pl.program_id(1)
        rows = pl.ds(pl.multiple_of(i * tm, tm), tm)
        pg = pg_ref[layer:layer + 1, :]

        @pl.when((c == 0) & (i == 0))
        def _():
            dpg_ref[...] = jnp.zeros_like(dpg_ref)
            dqg_ref[...] = jnp.zeros_like(dqg_ref)

        @pl.when(c == 0)
        def _():
            h_s[rows, :] = _rms_fwd(x_ref[...], pg).astype(BF)
            df, prod = _rms_bwd(f_ref[...], qg_ref[layer:layer + 1, :], dx_ref[...])
            df_s[rows, :] = df.astype(BF)
            dqg_ref[...] += _rowsum(prod)

        @pl.when(i == 0)
        def _():
            accg[...] = jnp.zeros_like(accg)
            accu[...] = jnp.zeros_like(accu)
            accd[...] = jnp.zeros_like(accd)

        h = h_s[rows, :]
        df = df_s[rows, :]
        wg = wgu_ref[0]
        wu = wgu_ref[1]
        g = gp_ref[...].astype(F32)
        u = up_ref[...].astype(F32)
        sg = _sigmoid(g)
        a = g * sg
        dact = _dot_nt(df, wd_ref[...])
        accd[...] += _dot_tn((a * u).astype(BF), df)
        du = (dact * a).astype(BF)
        dg = (dact * u * (sg * (1.0 + g * (1.0 - sg)))).astype(BF)
        accg[...] += _dot_tn(h, dg)
        accu[...] += _dot_tn(h, du)
        dh = _dot_nt(dg, wg) + _dot_nt(du, wu)

        @pl.when(c == 0)
        def _():
            dh_s[rows, :] = dh

        @pl.when((c > 0) & (c < nc - 1))
        def _():
            dh_s[rows, :] += dh

        @pl.when(c == nc - 1)
        def _():
            dxp, prod = _rms_bwd(x_ref[...], pg, dh_s[rows, :] + dh)
            dpg_ref[...] += _rowsum(prod)
            dx1_ref[...] = dx_ref[...] + dxp

        @pl.when(i == n - 1)
        def _():
            exchange(kc_ref, c, accg, accu, accd, own_gu_ref, land_gu_ref, own_dn_ref, land_dn_ref, *comm)

    dma = pltpu.SemaphoreType.DMA
    return _call(
        body, name=f"ffn_bwd{layer}", grid=(nc, n),
        in_specs=[pl.BlockSpec((tm, D), edge_rows), pl.BlockSpec((tm, D), edge_rows),
                  pl.BlockSpec((tm, D), lambda c, i, kc_ref: (jnp.where(c == 0, i, n - 1), 0),
                               pipeline_mode=pl.Buffered(1)),
                  pl.BlockSpec((tm, FF_CHUNK), lambda c, i, kc_ref: (i, chunk_at(c, kc_ref))),
                  pl.BlockSpec((tm, FF_CHUNK), lambda c, i, kc_ref: (i, chunk_at(c, kc_ref))),
                  VSPEC,
                  pl.BlockSpec((2, D, FF_CHUNK), lambda c, i, kc_ref: (0, 0, chunk_at(c, kc_ref))),
                  pl.BlockSpec((FF_CHUNK, D), lambda c, i, kc_ref: (chunk_at(c, kc_ref), 0)),
                  VSPEC],
        out_specs=[pl.BlockSpec((tm, D), lambda c, i, kc_ref: (jnp.where(c == nc - 1, i, 0), 0)),
                   _const_spec((1, D)), _const_spec((1, D)), ANYSPEC, ANYSPEC, ANYSPEC, ANYSPEC],
        out_shape=[_sds((s_len, D)), _sds((1, D)), _sds((1, D)),
                   _sds((n_blk, HALF_D, GU_PIECE), BF), _sds((N_CHIPS - 1, n_blk, HALF_D, GU_PIECE), BF),
                   _sds((DN_SLOT, HALF_D), BF), _sds((N_CHIPS - 1, DN_SLOT, HALF_D), BF)],
        scratch_shapes=[pltpu.VMEM((s_len, D), BF), pltpu.VMEM((s_len, D), BF), pltpu.VMEM((s_len, D), F32),
                        pltpu.VMEM((D, FF_CHUNK), F32), pltpu.VMEM((D, FF_CHUNK), F32),
                        pltpu.VMEM((FF_CHUNK, D), F32),
                        pltpu.VMEM((nc, 2, HALF_D, FF_CHUNK), BF), pltpu.VMEM((nc, FF_CHUNK, HALF_D), BF),
                        pltpu.VMEM((2, 2, HALF_D, FF_CHUNK), BF), pltpu.VMEM((2, FF_CHUNK, HALF_D), BF),
                        pltpu.VMEM((2, HALF_D, FF_CHUNK), F32), pltpu.VMEM((FF_CHUNK, HALF_D), F32),
                        pltpu.VMEM((2, 2, n_gu, HALF_D, GU_PIECE), BF), pltpu.VMEM((2, FF_CHUNK, HALF_D), BF),
                        dma((2, 2)), dma((nc, 2)), dma((2, n_pieces)), dma((2, n_pieces)), dma((2, N_CHIPS - 1))],
        args=[dx2, x1, f, g_pre, u_pre, pre_g, wgu, wd, post_g], rider=rider, prefetch=kc)


def _ple_fwd(layer, x2, p, ple_g, w_gate, w_proj, post_g, qkv=None, rider=None):
    s_len = x2.shape[0]

    def body(*refs):
        if qkv:
            (x_ref, p_ref, g_ref, wg_ref, wp_ref, qg_ref, ng_ref, kg_ref, wq_ref, wkv_ref,
             z_ref, pe_ref, x3_ref, q_ref, kv_ref) = refs
        else:
            x_ref, p_ref, g_ref, wg_ref, wp_ref, qg_ref, z_ref, pe_ref, x3_ref = refs
        z, pe, x3 = _ple_math(layer, x_ref[...], p_ref[...], g_ref, wg_ref, wp_ref, qg_ref)
        z_ref[...] = z
        pe_ref[...] = pe
        x3_ref[...] = x3
        if qkv:
            q_ref[...] = _dot(_rms_fwd(x3, ng_ref[layer + 1:layer + 2, :]).astype(BF), wq_ref[...]).astype(BF)
            kv_ref[...] = _dot(_rms_fwd(x3, kg_ref[...]).astype(BF), wkv_ref[...]).astype(BF)

    p_spec = pl.BlockSpec((None, TM, PLE), lambda i: (layer, i, 0))
    in_specs = [_row_spec(TM), p_spec, VSPEC, VSPEC, VSPEC, VSPEC]
    args = [x2, p, ple_g, w_gate, w_proj, post_g]
    out_specs = [_row_spec(TM), _row_spec(TM), _row_spec(TM)]
    out_shape = [_sds((s_len, D))] * 3
    if qkv:
        in_specs += [VSPEC] * 4
        args += list(qkv)
        out_specs += [_row_spec(TM), _row_spec(TM, 2 * KVD)]
        out_shape += [_sds((s_len, D), BF), _sds((s_len, 2 * KVD), BF)]
    return _call(body, name=f"ple_fwd{layer}", grid=(s_len // TM,), in_specs=in_specs, out_specs=out_specs,
                 out_shape=out_shape, args=args, rider=rider)


def _ple_bwd(layer, dx3, x2, z, pe, p, ple_g, w_gate, post_g, rider=None):
    s_len = x2.shape[0]
    n = s_len // TM

    def body(dx_ref, x_ref, z_ref, pe_ref, p_ref, g_ref, wg_ref, qg_ref,
             dx2_ref, dwg_ref, dwp_ref, dg_ref, dqg_ref, gacc, pacc):
        i = pl.program_id(0)

        @pl.when(i == 0)
        def _():
            gacc[...] = jnp.zeros_like(gacc)
            pacc[...] = jnp.zeros_like(pacc)
            dg_ref[...] = jnp.zeros_like(dg_ref)
            dqg_ref[...] = jnp.zeros_like(dqg_ref)

        dx2, dwp, dwg, dg, dqg = _ple_bwd_math(layer, dx_ref[...], x_ref[...], z_ref[...], pe_ref[...], p_ref[...],
                                                g_ref, wg_ref, qg_ref)
        dx2_ref[...] = dx2
        pacc[...] += dwp
        gacc[...] += dwg
        dg_ref[...] += dg
        dqg_ref[...] += dqg

        @pl.when(i == n - 1)
        def _():
            dwg_ref[...] = gacc[...].astype(BF)
            dwp_ref[...] = pacc[...].astype(BF)

    p_spec = pl.BlockSpec((None, TM, PLE), lambda i: (layer, i, 0))
    return _call(
        body, name=f"ple_bwd{layer}", grid=(n,),
        in_specs=[_row_spec(TM), _row_spec(TM), _row_spec(TM), _row_spec(TM), p_spec, VSPEC, VSPEC, VSPEC],
        out_specs=[_row_spec(TM), _const_spec((D, D)), _const_spec((PLE, D)), _const_spec((1, D)), _const_spec((1, D))],
        out_shape=[_sds((s_len, D)), _sds((D, D), BF), _sds((PLE, D), BF), _sds((1, D)), _sds((1, D))],
        scratch_shapes=[pltpu.VMEM((D, D), F32), pltpu.VMEM((PLE, D), F32)],
        args=[dx3, x2, z, pe, p, ple_g, w_gate, post_g], rider=rider)


def _qkv_bwd(dq, dkv, x3, dx4, q_g, kv_g, w_q, w_kv):
    s_len = x3.shape[0]
    n = s_len // TM

    def body(dq_ref, dkv_ref, x_ref, dx_ref, qg_ref, kg_ref, wq_ref, wkv_ref,
             dx3_ref, dwq_ref, dwkv_ref, dqg_ref, dkg_ref, qacc, kacc):
        i = pl.program_id(0)

        @pl.when(i == 0)
        def _():
            qacc[...] = jnp.zeros_like(qacc)
            kacc[...] = jnp.zeros_like(kacc)
            dqg_ref[...] = jnp.zeros_like(dqg_ref)
            dkg_ref[...] = jnp.zeros_like(dkg_ref)

        x = x_ref[...]
        qg = qg_ref[1:2, :]
        kg = kg_ref[...]
        dq_v = dq_ref[...]
        dkv_v = dkv_ref[...].astype(BF)
        qacc[...] += _dot_tn(_rms_fwd(x, qg).astype(BF), dq_v)
        kacc[...] += _dot_tn(_rms_fwd(x, kg).astype(BF), dkv_v)
        dxq, prod_q = _rms_bwd(x, qg, _dot_nt(dq_v, wq_ref[...]))
        dxk, prod_k = _rms_bwd(x, kg, _dot_nt(dkv_v, wkv_ref[...]))
        dqg_ref[...] += _rowsum(prod_q)
        dkg_ref[...] += _rowsum(prod_k)
        dx3_ref[...] = dx_ref[...] + dxq + dxk

        @pl.when(i == n - 1)
        def _():
            dwq_ref[...] = qacc[...].astype(BF)
            dwkv_ref[...] = kacc[...].astype(BF)

    outs, _ = _call(
        body, name="qkv_bwd", grid=(n,),
        in_specs=[_row_spec(TM), _row_spec(TM, 2 * KVD), _row_spec(TM), _row_spec(TM), VSPEC, VSPEC, VSPEC, VSPEC],
        out_specs=[_row_spec(TM), _const_spec((D, D)), _const_spec((D, 2 * KVD)),
                   _const_spec((1, D)), _const_spec((1, D))],
        out_shape=[_sds((s_len, D)), _sds((D, D), BF), _sds((D, 2 * KVD), BF), _sds((1, D)), _sds((1, D))],
        scratch_shapes=[pltpu.VMEM((D, D), F32), pltpu.VMEM((D, 2 * KVD), F32)],
        args=[dq, dkv, x3, dx4, q_g, kv_g, w_q, w_kv])
    return outs


def _attn_group(i, q, kvw, sink_ref, g):
    rows = GQA * BLK
    heads = [GQA * g + j for j in range(GQA)]
    off = jnp.where(i > 0, BLK, 0)
    row = lax.broadcasted_iota(jnp.int32, (rows, 2 * BLK), 0)
    rel = (row % BLK) - lax.broadcasted_iota(jnp.int32, (rows, 2 * BLK), 1) + off
    valid = (rel >= 0) & (rel < BLK)
    head_of_row = lax.broadcasted_iota(jnp.int32, (rows, 1), 0) // BLK
    slope = jnp.zeros((rows, 1), F32)
    sink = jnp.zeros((rows, 1), F32)
    for j, h in enumerate(heads):
        slope = jnp.where(head_of_row == j, SLOPES[h], slope)
        sink = jnp.where(head_of_row == j, sink_ref[0, h], sink)
    qs = jnp.concatenate([q[:, h * HEAD_DIM:(h + 1) * HEAD_DIM] for h in heads], axis=0)
    k = kvw[:, g * HEAD_DIM:(g + 1) * HEAD_DIM]
    v = kvw[:, KVD + g * HEAD_DIM:KVD + (g + 1) * HEAD_DIM]
    s = _dot_nt(qs, k) * ATT_SCALE - slope * rel.astype(F32)
    s = jnp.where(valid, s, NEG_INF)
    m = jnp.maximum(jnp.max(s, axis=-1, keepdims=True), sink)
    e = jnp.exp(s - m)
    es = jnp.exp(sink - m)
    inv = 1.0 / (jnp.sum(e, axis=-1, keepdims=True) + es)
    return e * inv, es * inv, qs, k, v


def _unstack_heads(stacked):
    return [stacked[j * BLK:(j + 1) * BLK, :] for j in range(GQA)]


def _kv_window(kv_ref, i):
    ks = pl.multiple_of(jnp.maximum(i * BLK - BLK, 0), BLK)
    return ks, kv_ref[pl.ds(ks, 2 * BLK), :]


def _attn_fwd(q, kv, sinks, x3, w_o, post_g, rider=None):
    s_len = q.shape[0]

    def body(q_ref, kv_ref, sk_ref, x_ref, wo_ref, g_ref, a_ref, y_ref, x4_ref):
        i = pl.program_id(0)
        _, kvw = _kv_window(kv_ref, i)
        q = q_ref[...]
        outs = []
        for g in range(N_KV_HEADS):
            p, _, _, _, v = _attn_group(i, q, kvw, sk_ref, g)
            outs += _unstack_heads(_dot(p.astype(BF), v))
        attn = jnp.concatenate(outs, axis=1)
        a_ref[...] = attn
        y = _dot(attn.astype(BF), wo_ref[...])
        y_ref[...] = y
        x4_ref[...] = x_ref[...] + _rms_fwd(y, g_ref[1:2, :])

    return _call(body, name="attn_fwd", grid=(s_len // BLK,),
                 in_specs=[_row_spec(BLK), VSPEC, SSPEC, _row_spec(BLK), VSPEC, VSPEC],
                 out_specs=[_row_spec(BLK)] * 3, out_shape=[_sds((s_len, D))] * 3,
                 args=[q, kv, sinks, x3, w_o, post_g], rider=rider)


ATT_STEP_BLOCKS = 2


def _attn_bwd(dx4, y, attn, q, kv, sinks, w_o, post_g, rider=None):
    s_len = q.shape[0]
    rows = ATT_STEP_BLOCKS * BLK
    n = s_len // rows

    def body(dx_ref, y_ref, a_ref, q_ref, kv_ref, sk_ref, wo_ref, g_ref,
             dq_ref, dkv_ref, dwo_ref, dg_ref, dsk_ref, wacc):
        i = pl.program_id(0)

        @pl.when(i == 0)
        def _():
            dkv_ref[...] = jnp.zeros_like(dkv_ref)
            wacc[...] = jnp.zeros_like(wacc)
            dg_ref[...] = jnp.zeros_like(dg_ref)
            dsk_ref[...] = jnp.zeros_like(dsk_ref)

        dy, prod = _rms_bwd(y_ref[...], g_ref[1:2, :], dx_ref[...])
        dg_ref[...] += _rowsum(prod)
        dyb = dy.astype(BF)
        attn_all = a_ref[...]
        wacc[...] += _dot_tn(attn_all.astype(BF), dyb)
        d_o_all = _dot_nt(dyb, wo_ref[...])
        q_all = q_ref[...]
        lane = lax.broadcasted_iota(jnp.int32, (1, D), 1)
        dsk = jnp.zeros((1, D), F32)
        for sub in range(ATT_STEP_BLOCKS):
            blk = i * ATT_STEP_BLOCKS + sub
            sl = slice(sub * BLK, (sub + 1) * BLK)
            d_o, q = d_o_all[sl, :], q_all[sl, :]
            dod = d_o * attn_all[sl, :]
            ks, kvw = _kv_window(kv_ref, blk)
            dqs, dks, dvs = [], [], []
            for g in range(N_KV_HEADS):
                p, ps, qs, k, v = _attn_group(blk, q, kvw, sk_ref, g)
                cols = [slice((GQA * g + j) * HEAD_DIM, (GQA * g + j + 1) * HEAD_DIM) for j in range(GQA)]
                do_s = jnp.concatenate([d_o[:, c] for c in cols], axis=0).astype(BF)
                dsum = jnp.concatenate([jnp.sum(dod[:, c], axis=-1, keepdims=True) for c in cols], axis=0)
                dp = _dot_nt(do_s, v)
                dsb = (p * (dp - dsum) * ATT_SCALE).astype(BF)
                sink_part = ps * dsum
                for j in range(GQA):
                    dsk = dsk + jnp.where(lane == GQA * g + j, -_rowsum(sink_part[j * BLK:(j + 1) * BLK, :]), 0.0)
                dqs += _unstack_heads(_dot(dsb, k))
                dks.append(_dot_tn(dsb, qs))
                dvs.append(_dot_tn(p.astype(BF), do_s))
            dq_ref[sl, :] = jnp.concatenate(dqs, axis=1).astype(BF)
            dkv_ref[pl.ds(ks, 2 * BLK), :] += jnp.concatenate(dks + dvs, axis=1)
        dsk_ref[...] += dsk

        @pl.when(i == n - 1)
        def _():
            dwo_ref[...] = wacc[...].astype(BF)

    return _call(
        body, name="attn_bwd", grid=(n,),
        in_specs=[_row_spec(rows), _row_spec(rows), _row_spec(rows), _row_spec(rows), VSPEC, SSPEC, VSPEC, VSPEC],
        out_specs=[_row_spec(rows), _const_spec((s_len, 2 * KVD)), _const_spec((D, D)),
                   _const_spec((1, D)), _const_spec((1, D))],
        out_shape=[_sds((s_len, D), BF), _sds((s_len, 2 * KVD)), _sds((D, D), BF), _sds((1, D)), _sds((1, D))],
        scratch_shapes=[pltpu.VMEM((D, D), F32)],
        args=[dx4, y, attn, q, kv, sinks, w_o, post_g], rider=rider)


Big = collections.namedtuple("Big", "name src layer L A R C rb")


def _bigs():
    out = {"pool_w": Big("pool_w", "pool_w", None, 4, 4, POOL_G // N_CHIPS, POOL_G, 32)}
    for l in range(2):
        out[f"w_gu{l}"] = Big(f"w_gu{l}", "w_gu", l, 1, 2, D, FF_HALF, 256)
        out[f"w_down{l}"] = Big(f"w_down{l}", "w_down", l, 1, 4, FF // N_CHIPS, D, 352)
        out[f"w_ple_gate{l}"] = Big(f"w_ple_gate{l}", "w_ple_gate", l, 1, 4, D // N_CHIPS, D, 128)
        out[f"w_ple_proj{l}"] = Big(f"w_ple_proj{l}", "w_ple_proj", l, 1, 1, PLE, D // N_CHIPS, 128)
    out["w_q"] = Big("w_q", "w_q", None, 1, 4, D // N_CHIPS, D, 128)
    out["w_o"] = Big("w_o", "w_o", None, 1, 4, D // N_CHIPS, D, 128)
    out["w_kv"] = Big("w_kv", "w_kv", None, 1, 4, D // N_CHIPS, 2 * KVD, 128)
    return out


BIGS = _bigs()
POOL_SCALE = Big("pool_scale", "pool_scale", None, 1, 1, 1, D // N_CHIPS, 1)
BIG_SOURCES = ("w_gu", "w_down", "w_ple_gate", "w_ple_proj", "w_q", "w_o", "w_kv", "pool_w")


def _ncb(t):
    return N_CHIPS // t.A


def _full_shape(t, rows=None):
    return (t.L, t.A, t.R if rows is None else rows, _ncb(t) * t.C)


def _slot_index(t, k):
    return k // _ncb(t), k % _ncb(t)


def _slot(ref, t, k, row0, rows):
    a, cb = _slot_index(t, k)
    return ref.at[:, a, pl.ds(row0, rows), pl.ds(pl.multiple_of(cb * t.C, 128), t.C)]


def _place_job(t, w, out_dtype=BF):
    nb = next((nb for nb in (8, 4, 2, 1) if t.R % (16 * nb) == 0), 1) if t.L == 1 else 1
    rb = t.R // nb

    def fn(j, kc_ref, ins, outs):
        outs[0][...] = ins[0][...].astype(out_dtype)

    def in_map(j, kc_ref):
        return (j // nb if t.layer is None else t.layer, j % nb, 0)

    def out_map(j, kc_ref):
        a, cb = _slot_index(t, kc_ref[0])
        return (j // nb, a, j % nb, cb)

    return Job(t.L * nb, [(w, (None, rb, t.C), in_map)],
               [(_sds(_full_shape(t), out_dtype), (None, None, rb, t.C), out_map)], fn)


def _mesh_position():
    x, y, c = lax.axis_index("x"), lax.axis_index("y"), lax.axis_index("c")
    chips = [(1 - x, y), (x, 1 - y), (1 - x, 1 - y)]
    return x, y, c, chips


DIRECT_BELOW = 1024


def _gather_rider(parts, fulls):
    nt = len(parts)
    TO_X, TO_Y, FWD_X, FWD_Y, SIB_X, SIB_Y, SIB_D = range(7)

    def rows_of(ti, core):
        t, r0, r1 = parts[ti]
        h = (r1 - r0) // 2
        return r0 + core * h, h

    def copy(outs, sems, kind, ti, k_src, row0, rows, dev):
        region = _slot(outs[ti], parts[ti][0], k_src, row0, rows)
        return pltpu.make_async_remote_copy(region, region, sems[0].at[ti, kind], sems[1].at[ti, kind],
                                            device_id=dev, device_id_type=MESH)

    def plan(outs, sems):
        x, y, c, _ = _mesh_position()
        me, kx, ky, kd = 2 * x + y, 2 * (1 - x) + y, 2 * x + (1 - y), 2 * (1 - x) + (1 - y)
        dev_x, dev_y, dev_d, sib = (1 - x, y, c), (x, 1 - y, c), (1 - x, 1 - y, c), (x, y, 1 - c)

        def whole(ti):
            return 0, parts[ti][0].R

        def mk(kind, k_send, k_recv, dev, send_rows, recv_rows):
            def build(ti, side):
                k_src = k_send if side == "s" else k_recv
                row0, rows = (send_rows if side == "s" else recv_rows)(ti)
                return copy(outs, sems, kind, ti, k_src, row0, rows, dev)
            return build

        def first_half(core):
            return lambda ti: (rows_of(ti, core)[0], rows_of(ti, core)[1] // 2)

        def second_half(core):
            return lambda ti: (rows_of(ti, core)[0] + rows_of(ti, core)[1] // 2, rows_of(ti, core)[1] // 2)

        mine = lambda ti: rows_of(ti, c)
        theirs = lambda ti: rows_of(ti, 1 - c)
        split = {
            TO_X: mk(TO_X, me, kx, dev_x, mine, mine),
            TO_Y: mk(TO_Y, me, ky, dev_y, mine, mine),
            FWD_X: mk(FWD_X, ky, kd, dev_x, first_half(c), first_half(c)),
            FWD_Y: mk(FWD_Y, kx, kd, dev_y, second_half(c), second_half(c)),
            SIB_X: mk(SIB_X, kx, kx, sib, mine, theirs),
            SIB_Y: mk(SIB_Y, ky, ky, sib, mine, theirs),
            SIB_D: mk(SIB_D, kd, kd, sib, mine, theirs),
        }
        direct = {
            TO_X: mk(TO_X, me, kx, dev_x, whole, whole),
            TO_Y: mk(TO_Y, me, ky, dev_y, whole, whole),
            FWD_X: mk(FWD_X, me, kd, dev_d, whole, whole),
        }
        return split, direct

    is_split = [t.L * t.R * t.C >= DIRECT_BELOW for t, _, _ in parts]
    assert all(s or (r0, r1) == (0, t.R) for s, (t, r0, r1) in zip(is_split, parts))

    def start(ins, outs, sems):
        split, direct = plan(outs, sems)
        for ti in range(nt):
            kinds = split if is_split[ti] else direct
            kinds[TO_X](ti, "s").start()
            kinds[TO_Y](ti, "s").start()
            if not is_split[ti]:
                kinds[FWD_X](ti, "s").start()

    def mid(ins, outs, sems):
        split, _ = plan(outs, sems)
        for ti in range(nt):
            if is_split[ti]:
                split[TO_Y](ti, "r").wait_recv()
                split[FWD_X](ti, "s").start()
                split[SIB_Y](ti, "s").start()
        for ti in range(nt):
            if is_split[ti]:
                split[TO_X](ti, "r").wait_recv()
                split[FWD_Y](ti, "s").start()
                split[SIB_X](ti, "s").start()

    def finish(ins, outs, sems):
        split, direct = plan(outs, sems)
        for ti in range(nt):
            if is_split[ti]:
                split[FWD_X](ti, "r").wait_recv()
                split[FWD_Y](ti, "r").wait_recv()
                split[SIB_D](ti, "s").start()
            else:
                for kind in (TO_X, TO_Y, FWD_X):
                    direct[kind](ti, "r").wait_recv()
        for ti in range(nt):
            if is_split[ti]:
                for kind in (SIB_X, SIB_Y, SIB_D):
                    split[kind](ti, "r").wait_recv()
        for ti in range(nt):
            kinds = split if is_split[ti] else direct
            for kind in kinds:
                kinds[kind](ti, "s").wait_send()

    sems = pltpu.SemaphoreType.DMA((nt, 7))
    return Rider(list(fulls), [_sds(a.shape, a.dtype) for a in fulls], {i: i for i in range(nt)},
                 [sems, sems], start, mid, finish)


def _pair_exchange_rider(specs, grads):
    nt = len(specs)

    def copy(ins, outs, sems, ti, c, sibling):
        half = specs[ti].R // 2
        return pltpu.make_async_remote_copy(ins[ti].at[:, :, pl.ds((1 - c) * half, half), :], outs[ti],
                                            sems[0].at[ti], sems[1].at[ti], device_id=sibling, device_id_type=MESH)

    def start(ins, outs, sems):
        x, y, c, _ = _mesh_position()
        for ti in range(nt):
            copy(ins, outs, sems, ti, c, (x, y, 1 - c)).start()

    def finish(ins, outs, sems):
        x, y, c, _ = _mesh_position()
        for ti in range(nt):
            copy(ins, outs, sems, ti, c, (x, y, 1 - c)).wait()

    sems = pltpu.SemaphoreType.DMA((nt,))
    return Rider(list(grads), [_sds(_full_shape(t, t.R // 2), BF) for t in specs], {}, [sems, sems], start, None, finish)


def _pair_sum_job(t, g, land):
    assert t.L == 1
    half = t.R // 2
    nj = half // t.rb
    block = (None, t.A, t.rb, _ncb(t) * t.C)

    def fn(j, kc_ref, ins, outs):
        outs[0][...] = (ins[0][...].astype(F32) + ins[1][...].astype(F32)).astype(BF)

    return Job(nj,
               [(g, block, lambda j, kc_ref: (0, 0, kc_ref[1] * nj + j, 0)),
                (land, block, lambda j, kc_ref: (0, 0, j, 0))],
               [(_sds(_full_shape(t, half), BF), block, lambda j, kc_ref: (0, 0, j, 0))], fn)


def _scatter_rider(specs, sums):
    nt = len(specs)

    def copy(ins, outs, sems, ti, j, chip, c):
        t = specs[ti]
        cx, cy = chip
        return pltpu.make_async_remote_copy(_slot(ins[ti], t, 2 * cx + cy, 0, t.R // 2), outs[ti].at[j],
                                            sems[0].at[ti, j], sems[1].at[ti, j],
                                            device_id=(cx, cy, c), device_id_type=MESH)

    def start(ins, outs, sems):
        _, _, c, chips = _mesh_position()
        for j, chip in enumerate(chips):
            for ti in range(nt):
                copy(ins, outs, sems, ti, j, chip, c).start()

    def finish(ins, outs, sems):
        _, _, c, chips = _mesh_position()
        for j, chip in enumerate(chips):
            for ti in range(nt):
                copy(ins, outs, sems, ti, j, chip, c).wait()

    sems = pltpu.SemaphoreType.DMA((nt, N_CHIPS - 1))
    return Rider(list(sums), [_sds((N_CHIPS - 1, t.L, t.R // 2, t.C), BF) for t in specs], {}, [sems, sems],
                 start, None, finish)


def _chip_sum_job(ts, landed):
    t0 = ts[0]
    assert t0.L == 1
    half = t0.R // 2
    nj = half // t0.rb

    def local(j, li):
        return jnp.clip(j - li * nj, 0, nj - 1)

    ins = []
    for li, t in enumerate(ts):
        s, land = landed[t.name]

        def own_map(j, kc_ref, li=li, t=t):
            a, cb = _slot_index(t, kc_ref[0])
            return (0, a, local(j, li), cb)

        ins.append((s, (None, None, t.rb, t.C), own_map))
        ins.append((land, (N_CHIPS - 1, None, t.rb, t.C), lambda j, kc_ref, li=li: (0, 0, local(j, li), 0)))

    def fn(j, kc_ref, in_refs, outs):
        for li in range(len(ts)):
            @pl.when(j // nj == li)
            def _():
                acc = in_refs[2 * li][...].astype(F32)
                for k in range(N_CHIPS - 1):
                    acc = acc + in_refs[2 * li + 1][k].astype(F32)
                outs[0][...] = acc

    return Job(len(ts) * nj, ins,
               [(_sds((len(ts), t0.R, t0.C)), (None, t0.rb, t0.C),
                 lambda j, kc_ref: (j // nj, kc_ref[1] * nj + j % nj, 0))], fn)


def _adamw_job(rb, w, g, m, v):
    n_layers, r, c = w.shape
    nb = r // rb
    block = (None, rb, c)
    index = lambda j, kc_ref: (j // nb, j % nb, 0)

    def fn(j, kc_ref, ins, outs):
        g_v = ins[1][...]
        outs[0][...] = g_v
        outs[1][...], outs[2][...], outs[3][...] = _adamw_math(ins[0][...], g_v, ins[2][...], ins[3][...])

    return Job(n_layers * nb, [(a, block, index) for a in (w, g, m, v)],
               [(_sds(w.shape), block, index)] * 4, fn)


def _chip_sum_fused_job(ts, fused, by_cols):
    t0 = ts[0]
    own0 = fused[t0.name][0]
    if by_cols:
        rows, cols = own0.shape
    else:
        nb, rows, bw = own0.shape
        cols = nb * bw
    nj = rows // t0.rb

    def local(j, li):
        return jnp.clip(j - li * nj, 0, nj - 1)

    ins = []
    for li, t in enumerate(ts):
        own, land = fused[t.name]
        if by_cols:
            ins.append((own, (t.rb, cols), lambda j, kc_ref, li=li: (local(j, li), 0)))
            ins.append((land, (N_CHIPS - 1, t.rb, cols), lambda j, kc_ref, li=li: (0, local(j, li), 0)))
        else:
            ins.append((own, (nb, t.rb, bw), lambda j, kc_ref, li=li: (0, local(j, li), 0)))
            ins.append((land, (N_CHIPS - 1, nb, t.rb, bw), lambda j, kc_ref, li=li: (0, 0, local(j, li), 0)))

    def fn(j, kc_ref, in_refs, outs):
        for li in range(len(ts)):
            @pl.when(j // nj == li)
            def _():
                acc = in_refs[2 * li][...].astype(F32)
                for k in range(N_CHIPS - 1):
                    acc = acc + in_refs[2 * li + 1][k].astype(F32)
                outs[0][...] = acc if by_cols else jnp.concatenate([acc[b] for b in range(nb)], axis=1)

    def out_map(j, kc_ref):
        return (j // nj, j % nj, kc_ref[1]) if by_cols else (j // nj, kc_ref[1] * nj + j % nj, 0)

    return Job(len(ts) * nj, ins, [(_sds((len(ts), t0.R, t0.C)), (None, t0.rb, cols), out_map)], fn)


def _share_rider(halves, by_cols):
    nt = len(halves)

    def copy(outs, sems, ti, core, sibling):
        axis = 2 if by_cols[ti] else 1
        half = halves[ti].shape[axis] // 2
        piece = pl.ds(pl.multiple_of(core * half, 128 if by_cols[ti] else 8), half)
        part = outs[ti].at[:, :, piece] if by_cols[ti] else outs[ti].at[:, piece, :]
        return pltpu.make_async_remote_copy(part, part, sems[0].at[ti], sems[1].at[ti],
                                            device_id=sibling, device_id_type=MESH)

    def start(ins, outs, sems):
        x, y, c, _ = _mesh_position()
        for ti in range(nt):
            copy(outs, sems, ti, c, (x, y, 1 - c)).start()

    def finish(ins, outs, sems):
        x, y, c, _ = _mesh_position()
        for ti in range(nt):
            copy(outs, sems, ti, 1 - c, (x, y, 1 - c)).wait_recv()
        for ti in range(nt):
            copy(outs, sems, ti, c, (x, y, 1 - c)).wait_send()

    sems = pltpu.SemaphoreType.DMA((nt,))
    return Rider(list(halves), [_sds(a.shape, a.dtype) for a in halves], {i: i for i in range(nt)}, [sems, sems],
                 start, None, finish)


def _both(r1, r2):
    assert r1.mid is None and r2.mid is None
    ni, no, ns = len(r1.arrays), len(r1.out_shapes), len(r1.scratch)

    def split(fn1, fn2):
        def run(ins, outs, scr):
            fn1(ins[:ni], outs[:no], scr[:ns])
            fn2(ins[ni:], outs[no:], scr[ns:])
        return run

    aliases = dict(r1.aliases)
    aliases.update({ni + a: no + b for a, b in r2.aliases.items()})
    return Rider(r1.arrays + r2.arrays, r1.out_shapes + r2.out_shapes, aliases, r1.scratch + r2.scratch,
                 split(r1.start, r2.start), None, split(r1.finish, r2.finish))


def _adamw_math(w, g, m, v):
    m = B1 * m + (1.0 - B1) * g
    v = B2 * v + (1.0 - B2) * (g * g)
    delta = -LR * ((m / BC1) / (jnp.sqrt(v / BC2) + AEPS) + WD * w)
    return delta, m, v


GAIN_ROWS = {"pre_mix_g": 0, "post_mix_g": 2, "pre_ffn_g": 4, "post_ffn_g": 6, "ple_g": 8, "ple_post_g": 10}
ROW_KV_G, ROW_POOL_SCALE, ROW_SINKS, ROW_LOSS, PACK_ROWS = 12, 13, 14, 15, 16
SMALL_NAMES = tuple(GAIN_ROWS) + ("kv_g", "pool_scale", "sinks")


def _small_all_reduce(rows, dpool, rider=None):
    ng, pr = len(WINDOWS), POOL_G // N_CHIPS

    def body(*refs):
        row_refs = refs[:PACK_ROWS]
        dpool_ref, tot_ref, gpool_ref, pack, land, pland, send, recv, psend, precv = refs[PACK_ROWS:]
        x, y, c, _ = _mesh_position()
        me = 4 * x + 2 * y + c
        for r in range(PACK_ROWS):
            pack[r:r + 1, :] = row_refs[r][...]

        def shard_of(k):
            return dpool_ref.at[:, pl.ds(pl.multiple_of(k * pr, pr), pr), :]

        cps = []
        for j in range(1, N_DEV):
            px, py, pc = x ^ (j >> 2), y ^ ((j >> 1) & 1), c ^ (j & 1)
            cps.append(pltpu.make_async_remote_copy(pack, land.at[me], send.at[j], recv.at[j],
                                                    device_id=(px, py, pc), device_id_type=MESH))
            cps.append(pltpu.make_async_remote_copy(shard_of(2 * px + py), pland.at[me], psend.at[j], precv.at[j],
                                                    device_id=(px, py, pc), device_id_type=MESH))
        for cp in cps:
            cp.start()
        land[me] = pack[...]
        pland[me] = dpool_ref[:, pl.ds(pl.multiple_of((2 * x + y) * pr, pr), pr), :]
        for j in range(1, N_DEV):
            pltpu.make_async_remote_copy(pack, land.at[me ^ j], send.at[j], recv.at[j],
                                         device_id=(x, y, c), device_id_type=MESH).wait_recv()
            pltpu.make_async_remote_copy(shard_of(0), pland.at[me ^ j], psend.at[j], precv.at[j],
                                         device_id=(x, y, c), device_id_type=MESH).wait_recv()
        for cp in cps:
            cp.wait_send()
        tot = land[0]
        gp = pland[0].astype(F32)
        for d in range(1, N_DEV):
            tot = tot + land[d]
            gp = gp + pland[d].astype(F32)
        tot_ref[...] = tot
        gpool_ref[...] = gp

    sems = pltpu.SemaphoreType.DMA((N_DEV,))
    return _call(
        body, name="small_all_reduce", grid=(1,),
        in_specs=[VSPEC] * (PACK_ROWS + 1), out_specs=[VSPEC, VSPEC],
        out_shape=[_sds((PACK_ROWS, D)), _sds((ng, pr, POOL_G))],
        scratch_shapes=[pltpu.VMEM((PACK_ROWS, D), F32), pltpu.VMEM((N_DEV, PACK_ROWS, D), F32),
                        pltpu.VMEM((N_DEV, ng, pr, POOL_G), BF), sems, sems, sems, sems],
        args=[*rows, dpool], rider=rider)


def _small_adamw(tot, kc, small_w, small_m, small_v):
    names = SMALL_NAMES
    n = len(names)

    def body(*refs):
        tot_ref, kc_ref = refs[0], refs[1]
        w_refs = dict(zip(names, refs[2:2 + n]))
        m_refs = dict(zip(names, refs[2 + n:2 + 2 * n]))
        v_refs = dict(zip(names, refs[2 + 2 * n:2 + 3 * n]))
        loss_ref = refs[2 + 3 * n]
        out_refs = {nm: refs[3 + 3 * n + 4 * k: 7 + 3 * n + 4 * k] for k, nm in enumerate(names)}
        tot = tot_ref[...]
        loss_ref[...] = 0.5 * jnp.sum(tot[ROW_LOSS:ROW_LOSS + 1, :], axis=-1, keepdims=True) * (1.0 / D)

        def update(nm, g):
            g_ref, d_ref, nm_ref, nv_ref = out_refs[nm]
            g_ref[...] = g
            d_ref[...], nm_ref[...], nv_ref[...] = _adamw_math(w_refs[nm][...], g, m_refs[nm][...], v_refs[nm][...])

        for nm, r in GAIN_ROWS.items():
            update(nm, tot[r:r + 2, :])
        update("kv_g", tot[ROW_KV_G:ROW_KV_G + 1, :])
        k = kc_ref[0]
        width = D // N_CHIPS
        g_scale = jnp.zeros((1, width), F32)
        for kk in range(N_CHIPS):
            g_scale = g_scale + jnp.where(k == kk, tot[ROW_POOL_SCALE:ROW_POOL_SCALE + 1, kk * width:(kk + 1) * width], 0.0)
        update("pool_scale", g_scale)
        update("sinks", tot[ROW_SINKS:ROW_SINKS + 1, 0:N_HEADS])

    ins = [tot, kc] + [small_w[nm] for nm in names] + [small_m[nm] for nm in names] + [small_v[nm] for nm in names]
    out_shape = [_sds((1, 1))]
    for nm in names:
        out_shape += [_sds(small_w[nm].shape)] * 4
    outs = pl.pallas_call(
        body, name="small_adamw",
        in_specs=[VSPEC, SSPEC] + [VSPEC] * (3 * n), out_specs=[VSPEC] * len(out_shape), out_shape=out_shape,
        compiler_params=_params(),
    )(*ins)
    return outs[0], {nm: outs[1 + 4 * k: 5 + 4 * k] for k, nm in enumerate(names)}


def _compute_layout(t, full):
    if t.src == "w_gu":
        return full.reshape(2, D, FF)
    if t.src == "pool_w":
        return full.reshape(len(WINDOWS), POOL_G, POOL_G)
    if t.src == "pool_scale":
        return full.reshape(1, D)
    return full.reshape(t.A * t.R, _ncb(t) * t.C)


def kernel(x, p, pre_mix_g, post_mix_g, pre_ffn_g, post_ffn_g, pool_w, pool_scale, kv_g, w_kv, w_q, sinks, w_o, w_gu, w_down, ple_g, w_ple_gate, w_ple_proj, ple_post_g, loss_target, m_pre_mix_g, m_post_mix_g, m_pre_ffn_g, m_post_ffn_g, m_pool_w, m_pool_scale, m_kv_g, m_w_kv, m_w_q, m_sinks, m_w_o, m_w_gu, m_w_down, m_ple_g, m_w_ple_gate, m_w_ple_proj, m_ple_post_g, v_pre_mix_g, v_post_mix_g, v_pre_ffn_g, v_post_ffn_g, v_pool_w, v_pool_scale, v_kv_g, v_w_kv, v_w_q, v_sinks, v_w_o, v_w_gu, v_w_down, v_ple_g, v_w_ple_gate, v_w_ple_proj, v_ple_post_g):
    weights = dict(pre_mix_g=pre_mix_g, post_mix_g=post_mix_g, pre_ffn_g=pre_ffn_g, post_ffn_g=post_ffn_g,
                   pool_w=pool_w, pool_scale=pool_scale, kv_g=kv_g, w_kv=w_kv, w_q=w_q, sinks=sinks, w_o=w_o,
                   w_gu=w_gu, w_down=w_down, ple_g=ple_g, w_ple_gate=w_ple_gate, w_ple_proj=w_ple_proj,
                   ple_post_g=ple_post_g)
    m_in = dict(pre_mix_g=m_pre_mix_g, post_mix_g=m_post_mix_g, pre_ffn_g=m_pre_ffn_g, post_ffn_g=m_post_ffn_g,
                pool_w=m_pool_w, pool_scale=m_pool_scale, kv_g=m_kv_g, w_kv=m_w_kv, w_q=m_w_q, sinks=m_sinks,
                w_o=m_w_o, w_gu=m_w_gu, w_down=m_w_down, ple_g=m_ple_g, w_ple_gate=m_w_ple_gate,
                w_ple_proj=m_w_ple_proj, ple_post_g=m_ple_post_g)
    v_in = dict(pre_mix_g=v_pre_mix_g, post_mix_g=v_post_mix_g, pre_ffn_g=v_pre_ffn_g, post_ffn_g=v_post_ffn_g,
                pool_w=v_pool_w, pool_scale=v_pool_scale, kv_g=v_kv_g, w_kv=v_w_kv, w_q=v_w_q, sinks=v_sinks,
                w_o=v_w_o, w_gu=v_w_gu, w_down=v_w_down, ple_g=v_ple_g, w_ple_gate=v_w_ple_gate,
                w_ple_proj=v_w_ple_proj, ple_post_g=v_ple_post_g)
    order = ["pre_mix_g", "post_mix_g", "pre_ffn_g", "post_ffn_g", "pool_w", "pool_scale", "kv_g", "w_kv", "w_q",
             "sinks", "w_o", "w_gu", "w_down", "ple_g", "w_ple_gate", "w_ple_proj", "ple_post_g"]

    kc = jnp.stack([2 * lax.axis_index("x") + lax.axis_index("y"), lax.axis_index("c")]).astype(jnp.int32)
    s_len = x.shape[1]
    x2d = x.reshape(s_len, D)
    p3d = p.reshape(2, s_len, PLE)
    target = loss_target.reshape(s_len, D)
    kv_g2d = kv_g.reshape(1, D)
    gains = {nm: weights[nm] for nm in GAIN_ROWS}

    def shard_view(src, a):
        t = next(t for t in BIGS.values() if t.src == src)
        return a.reshape(-1, t.R, t.C)

    first, second = ["pool_w", "pool_scale"], ["w_gu0", "w_down0"]
    rest = [nm for nm in BIGS if nm not in first + second]
    specs = dict(BIGS, pool_scale=POOL_SCALE)
    placed = {}

    def place_job(nm):
        if nm == "pool_scale":
            return _place_job(POOL_SCALE, pool_scale.reshape(1, 1, D // N_CHIPS), F32)
        return _place_job(BIGS[nm], shard_view(BIGS[nm].src, weights[BIGS[nm].src]))

    def gather(names, rows=None):
        rows = rows or {}
        parts = [(specs[nm],) + tuple(rows.get(nm, (0, specs[nm].R))) for nm in names]
        return _gather_rider(parts, [placed[nm] for nm in names])

    def take(names, results):
        for nm, a in zip(names, results):
            placed[nm] = a

    def weight(nm):
        return _compute_layout(specs[nm], placed[nm])

    take(first, [r[0] for r in _multi_call("place_pool", [place_job(nm) for nm in first], kc)])
    cast, got = _multi_call("place_ffn0", [place_job(nm) for nm in second], kc, rider=gather(first))
    take(second, [r[0] for r in cast])
    take(first, got)
    jobs = [place_job(nm) for nm in rest]
    jobs.append(_mixa_fwd_job(x2d, gains["pre_mix_g"], weight("pool_w"), weight("pool_scale"), gains["post_mix_g"]))
    results, got = _multi_call("cast_and_mixa_fwd", jobs, kc, rider=gather(second))
    take(rest, [r[0] for r in results[:-1]])
    take(second, got)
    y0, x1 = results[-1]

    ride = ["w_ple_gate0", "w_ple_proj0", "w_q", "w_kv", "w_o", "w_gu1"]
    (f0, x2, g0, u0), got = _ffn_fwd(0, x1, gains["pre_ffn_g"], weight("w_gu0"), weight("w_down0"), gains["post_ffn_g"],
                             rider=gather(ride, {"w_gu1": (0, 320)}))
    take(ride, got)

    ride = ["w_ple_gate1", "w_ple_proj1", "w_gu1"]
    (z0, pe0, x3, q, kv), got = _ple_fwd(
        0, x2, p3d, gains["ple_g"], weight("w_ple_gate0"), weight("w_ple_proj0"), gains["ple_post_g"],
        qkv=(gains["pre_mix_g"], kv_g2d, weight("w_q"), weight("w_kv")),
        rider=gather(ride, {"w_gu1": (320, 704)}))
    take(ride, got)

    ride = ["w_down1", "w_gu1"]
    (attn, y1, x4), got = _attn_fwd(q, kv, sinks, x3, weight("w_o"), gains["post_mix_g"],
                                    rider=gather(ride, {"w_gu1": (704, D)}))
    take(ride, got)

    local = {}
    (f1, g1, u1, dx5, local["w_ple_gate1"], local["w_ple_proj1"], d_ple1, d_plepost1, loss_row), _ = _ffn_fwd(
        1, x4, gains["pre_ffn_g"], weight("w_gu1"), weight("w_down1"), gains["post_ffn_g"],
        head=(p3d, gains["ple_g"], weight("w_ple_gate1"), weight("w_ple_proj1"), gains["ple_post_g"], target))

    landed = {}
    fused = {}

    def local_grads(names):
        return [local[nm].reshape(_full_shape(BIGS[nm])) for nm in names]

    def pair_exchange(names):
        return _pair_exchange_rider([BIGS[nm] for nm in names], local_grads(names))

    def pair_sum(tag, names, lands):
        jobs = [_pair_sum_job(BIGS[nm], g, l) for nm, g, l in zip(names, local_grads(names), lands)]
        return [r[0] for r in _multi_call(f"pair_sum_{tag}", jobs, kc)]

    def scatter(names, sums):
        return _scatter_rider([BIGS[nm] for nm in names], sums)

    def keep(names, sums, got):
        for nm, s, l in zip(names, sums, got):
            landed[nm] = (s, l)

    group_a = ["w_ple_gate1", "w_ple_proj1"]
    (dx4, d_preffn1, d_postffn1, *scattered), lands_a = _ffn_bwd(
        1, dx5, x4, f1, g1, u1, gains["pre_ffn_g"], weight("w_gu1"), weight("w_down1"), gains["post_ffn_g"], kc,
        rider=pair_exchange(group_a))
    fused["w_gu1"], fused["w_down1"] = scattered[0:2], scattered[2:4]

    (dq, dkv, local["w_o"], d_postmix1, d_sinks), _ = _attn_bwd(
        dx4, y1, attn, q, kv, sinks, weight("w_o"), gains["post_mix_g"])
    dx3, local["w_q"], local["w_kv"], d_premix1, d_kvg = _qkv_bwd(
        dq, dkv, x3, dx4, gains["pre_mix_g"], kv_g2d, weight("w_q"), weight("w_kv"))

    group_b = ["w_o", "w_q", "w_kv"]
    (dx2, local["w_ple_gate0"], local["w_ple_proj0"], d_ple0, d_plepost0), lands_b = _ple_bwd(
        0, dx3, x2, z0, pe0, p3d, gains["ple_g"], weight("w_ple_gate0"), gains["ple_post_g"],
        rider=pair_exchange(group_b))
    group_ab = group_a + group_b
    sums_ab = pair_sum("ab", group_ab, lands_a + lands_b)

    group_c = ["w_ple_gate0", "w_ple_proj0"]
    (dx1, d_preffn0, d_postffn0, *scattered), got = _ffn_bwd(
        0, dx2, x1, f0, g0, u0, gains["pre_ffn_g"], weight("w_gu0"), weight("w_down0"), gains["post_ffn_g"], kc,
        rider=_both(pair_exchange(group_c), scatter(group_ab, sums_ab)))
    fused["w_gu0"], fused["w_down0"] = scattered[0:2], scattered[2:4]
    lands_c = got[:len(group_c)]
    keep(group_ab, sums_ab, got[len(group_c):])

    layers_of = lambda src: [t for t in BIGS.values() if t.src == src]
    own_scatter = ["w_gu", "w_down"]
    early = own_scatter + ["w_q", "w_o", "w_kv"]
    late = ["w_ple_gate", "w_ple_proj"]
    by_cols = lambda srcs: [src == "w_down" for src in srcs]
    jobs = [_chip_sum_fused_job(layers_of(src), fused, by_cols=src == "w_down") for src in own_scatter]
    jobs += [_chip_sum_job(layers_of(src), landed) for src in early if src not in own_scatter]
    jobs_c = [_pair_sum_job(BIGS[nm], g, l) for nm, g, l in zip(group_c, local_grads(group_c), lands_c)]
    sums = [r[0] for r in _multi_call("chip_sum_early", jobs + jobs_c, kc)]
    halves, sums_c = sums[:len(jobs)], sums[len(jobs):]
    (dx0, d_pool, d_scale, d_postmix0, d_premix0), got = _mixa_bwd(
        dx1, x2d, y0, gains["pre_mix_g"], weight("pool_w"), weight("pool_scale"), gains["post_mix_g"],
        rider=_both(scatter(group_c, sums_c), _share_rider(halves, by_cols(early))))
    keep(group_c, sums_c, got[:len(group_c)])
    full_grads = dict(zip(early, got[len(group_c):]))

    rows = [d_premix0, d_premix1, d_postmix0, d_postmix1, d_preffn0, d_preffn1, d_postffn0, d_postffn1,
            d_ple0, d_ple1, d_plepost0, d_plepost1, d_kvg, d_scale, d_sinks, loss_row]
    as2d = lambda a: a.reshape(1, D) if a.ndim == 1 else a
    halves = [r[0] for r in _multi_call("chip_sum_late", [_chip_sum_job(layers_of(src), landed) for src in late], kc)]
    (tot, g_pool), got = _small_all_reduce(rows, d_pool, rider=_share_rider(halves, by_cols(late)))
    full_grads.update(zip(late, got))
    full_grads["pool_w"] = g_pool
    loss, small = _small_adamw(tot, kc, {nm: as2d(weights[nm]) for nm in SMALL_NAMES},
                               {nm: as2d(m_in[nm]) for nm in SMALL_NAMES},
                               {nm: as2d(v_in[nm]) for nm in SMALL_NAMES})


    def adam_job(src):
        rb = layers_of(src)[0].rb // (1 if src == "pool_w" else 2)
        return _adamw_job(rb, shard_view(src, weights[src]), full_grads[src],
                          shard_view(src, m_in[src]), shard_view(src, v_in[src]))

    out = {"grad": {}, "delta": {}, "new_m": {}, "new_v": {}}
    results = dict(zip(BIG_SOURCES, _multi_call("adamw", [adam_job(src) for src in BIG_SOURCES], kc)))
    for src in BIG_SOURCES:
        shape = weights[src].shape
        for kind, a in zip(("grad", "delta", "new_m", "new_v"), results[src]):
            out[kind][src] = a.reshape(shape)
    for nm in SMALL_NAMES:
        shape = weights[nm].shape
        for kind, a in zip(("grad", "delta", "new_m", "new_v"), small[nm]):
            out[kind][nm] = a.reshape(shape)

    return (loss.reshape(()), dx0.reshape(x.shape),
            *[out["grad"][nm] for nm in order], *[out["delta"][nm] for nm in order],
            *[out["new_m"][nm] for nm in order], *[out["new_v"][nm] for nm in order])
```

```python
import collections

import jax
import jax.numpy as jnp
from jax import lax
from jax.experimental import pallas as pl
from jax.experimental.pallas import tpu as pltpu

D = 1024
FF = 2816
N_HEADS = 16
HEAD_DIM = 64
N_KV_HEADS = 4
GQA = N_HEADS // N_KV_HEADS
KVD = N_KV_HEADS * HEAD_DIM
PLE = 256
BLK = 128
WINDOWS = (2, 4, 8, 16)
POOL_G = 256
HALO = 16
EPS = 1e-6
NEG_INF = -1e30
ATT_SCALE = HEAD_DIM ** -0.5
SLOPES = tuple(2.0 ** (-8.0 * (h + 1) / N_HEADS) for h in range(N_HEADS))
N_CHIPS = 4
N_DEV = 8

LR, B1, B2, AEPS, WD, STEP = 0.001, 0.9, 0.999, 1e-08, 0.01, 10
BC1 = 1.0 - B1 ** STEP
BC2 = 1.0 - B2 ** STEP

BF = jnp.bfloat16
F32 = jnp.float32
MESH = pl.DeviceIdType.MESH
VMEM_LIMIT_V7X = 58 * 1024 * 1024
TM = 256
TM_FFN_BWD = 512
FF_CHUNK = 256
FF_HALF = FF // 2

VSPEC = pl.BlockSpec(memory_space=pltpu.VMEM)
SSPEC = pl.BlockSpec(memory_space=pltpu.SMEM)
ANYSPEC = pl.BlockSpec(memory_space=pl.ANY)


def _params(n_grid=0):
    sem = ("arbitrary",) * n_grid if n_grid else None
    return pltpu.CompilerParams(dimension_semantics=sem, vmem_limit_bytes=VMEM_LIMIT_V7X)


def _sds(shape, dtype=F32):
    return jax.ShapeDtypeStruct(tuple(shape), dtype)


Rider = collections.namedtuple("Rider", "arrays out_shapes aliases scratch start mid finish")
MID_NUM, MID_DEN = 5, 8


def _call(body, *, name, grid, in_specs, out_specs, out_shape, args, scratch_shapes=(), rider=None, prefetch=None):
    ni, no, ns = len(in_specs), len(out_specs), len(scratch_shapes)
    npre = 0 if prefetch is None else 1
    pre = [] if prefetch is None else [prefetch]
    if rider is None:
        rider = Rider([], [], {}, [], None, None, None)
    ri, ro = len(rider.arrays), len(rider.out_shapes)

    def full(*refs):
        pre_refs, refs = refs[:npre], refs[npre:]
        ins, refs = refs[:ni], refs[ni:]
        rins, refs = refs[:ri], refs[ri:]
        outs, refs = refs[:no], refs[no:]
        routs, refs = refs[:ro], refs[ro:]
        scr, rscr = refs[:ns], refs[ns:]
        ids = [pl.program_id(a) for a in range(len(grid))]
        first = ids[0] == 0
        last = ids[0] == grid[0] - 1
        for a in range(1, len(grid)):
            first = first & (ids[a] == 0)
            last = last & (ids[a] == grid[a] - 1)

        if rider.start is not None:
            @pl.when(first)
            def _():
                rider.start(rins, routs, rscr)

        if rider.mid is not None:
            assert len(grid) == 1

            @pl.when(ids[0] == (grid[0] * MID_NUM) // MID_DEN)
            def _():
                rider.mid(rins, routs, rscr)

        body(*pre_refs, *ins, *outs, *scr)

        if rider.finish is not None:
            @pl.when(last)
            def _():
                rider.finish(rins, routs, rscr)

    outs = pl.pallas_call(
        full, name=name,
        grid_spec=pltpu.PrefetchScalarGridSpec(
            num_scalar_prefetch=npre, grid=grid,
            in_specs=list(in_specs) + [ANYSPEC] * ri, out_specs=list(out_specs) + [ANYSPEC] * ro,
            scratch_shapes=list(scratch_shapes) + list(rider.scratch)),
        out_shape=list(out_shape) + list(rider.out_shapes),
        input_output_aliases={npre + ni + a: no + b for a, b in rider.aliases.items()},
        compiler_params=_params(len(grid)))(*pre, *args, *rider.arrays)
    return list(outs[:no]), list(outs[no:])


Job = collections.namedtuple("Job", "steps ins outs fn")


def _multi_call(name, jobs, kc, rider=None):
    n = max(job.steps for job in jobs)

    def clamped(index, steps):
        return lambda s, kc_ref: index(jnp.minimum(s, steps - 1), kc_ref)

    in_specs, out_specs, out_shape, args = [], [], [], []
    for job in jobs:
        for arr, block, index, *single in job.ins:
            mode = dict(pipeline_mode=pl.Buffered(1)) if single and single[0] else {}
            in_specs.append(pl.BlockSpec(block, clamped(index, job.steps), **mode))
            args.append(arr)
        for sds, block, index in job.outs:
            out_specs.append(pl.BlockSpec(block, clamped(index, job.steps)))
            out_shape.append(sds)
    n_in = len(args)

    def body(kc_ref, *refs):
        s = pl.program_id(0)
        i0, o0 = 0, n_in
        for job in jobs:
            ins, outs = refs[i0:i0 + len(job.ins)], refs[o0:o0 + len(job.outs)]
            i0, o0 = i0 + len(job.ins), o0 + len(job.outs)

            @pl.when(s < job.steps)
            def _():
                job.fn(s, kc_ref, ins, outs)

    outs, routs = _call(body, name=name, grid=(n,), in_specs=in_specs, out_specs=out_specs, out_shape=out_shape,
                        args=args, prefetch=kc, rider=rider)
    res, o0 = [], 0
    for job in jobs:
        res.append(outs[o0:o0 + len(job.outs)])
        o0 += len(job.outs)
    return res if rider is None else (res, routs)


def _rms_fwd(x, g):
    r = lax.rsqrt(jnp.mean(x * x, axis=-1, keepdims=True) + EPS)
    return x * r * g


def _rms_bwd(x, g, dy):
    r = lax.rsqrt(jnp.mean(x * x, axis=-1, keepdims=True) + EPS)
    xn = x * r
    dxn = dy * g
    dx = r * (dxn - xn * jnp.mean(dxn * xn, axis=-1, keepdims=True))
    return dx, dy * xn


def _rowsum(a):
    return jnp.sum(a, axis=0, keepdims=True)


def _sigmoid(z):
    return 1.0 / (1.0 + jnp.exp(-z))


def _dot(a, b):
    return jnp.dot(a, b, preferred_element_type=F32)


def _dot_nt(a, b):
    return lax.dot_general(a, b, (((1,), (1,)), ((), ())), preferred_element_type=F32)


def _dot_tn(a, b):
    return lax.dot_general(a, b, (((0,), (0,)), ((), ())), preferred_element_type=F32)


def _row_spec(tm, width=D):
    return pl.BlockSpec((tm, width), lambda i: (i, 0))


def _const_spec(shape):
    zeros = (0,) * len(shape)
    return pl.BlockSpec(tuple(shape), lambda *_: zeros)


def _pool_delta(he, pos):
    out = []
    for gi, w in enumerate(WINDOWS):
        hg = he[:, gi * POOL_G:(gi + 1) * POOL_G]
        s = hg
        k = 1
        while k < w:
            s = s + pltpu.roll(s, k, 0)
            k *= 2
        cnt = jnp.maximum(jnp.minimum(pos + 1, w), 1).astype(F32)
        out.append(s / cnt - hg)
    return out


def _load_with_halo_before(x_ref, i, tm):
    r0 = pl.multiple_of(i * tm, tm)
    hs = pl.multiple_of(jnp.maximum(i * tm - HALO, 0), 8)
    xh = jnp.where(i > 0, x_ref[pl.ds(hs, HALO), :], 0.0)
    xt = x_ref[pl.ds(r0, tm), :]
    return xt, jnp.concatenate([xh, xt], axis=0)


def _mixa_fwd_job(x, pre_g, pool_w, pool_scale, post_g):
    s_len = x.shape[0]

    def fn(i, kc_ref, ins, outs):
        x_ref, pg_ref, w_ref, sc_ref, qg_ref = ins
        y_ref, x1_ref = outs
        xt, xe = _load_with_halo_before(x_ref, i, TM)
        he = _rms_fwd(xe, pg_ref[0:1, :])
        pos = i * TM - HALO + lax.broadcasted_iota(jnp.int32, (TM + HALO, 1), 0)
        ds = _pool_delta(he, pos)
        ys = [_dot(ds[gi][HALO:, :].astype(BF), w_ref[gi]) for gi in range(len(WINDOWS))]
        y = jnp.concatenate(ys, axis=1) * sc_ref[...]
        y_ref[...] = y
        x1_ref[...] = xt + _rms_fwd(y, qg_ref[0:1, :])

    def whole(a):
        zeros = (0,) * a.ndim
        return (a, a.shape, lambda j, kc_ref: zeros, True)

    rows = lambda j, kc_ref: (j, 0)
    return Job(s_len // TM, [whole(a) for a in (x, pre_g, pool_w, pool_scale, post_g)],
               [(_sds((s_len, D)), (TM, D), rows), (_sds((s_len, D)), (TM, D), rows)], fn)


def _mixa_bwd(dx1, x, y, pre_g, pool_w, pool_scale, post_g, rider=None):
    s_len = x.shape[0]
    n = s_len // TM
    ng = len(WINDOWS)

    def body(dx_ref, x_ref, y_ref, pg_ref, w_ref, sc_ref, qg_ref,
             dx0_ref, dw_ref, dsc_ref, dqg_ref, dpg_ref, wacc):
        i = pl.program_id(0)

        @pl.when(i == 0)
        def _():
            wacc[...] = jnp.zeros_like(wacc)
            dsc_ref[...] = jnp.zeros_like(dsc_ref)
            dqg_ref[...] = jnp.zeros_like(dqg_ref)
            dpg_ref[...] = jnp.zeros_like(dpg_ref)

        r0 = pl.multiple_of(i * TM, TM)
        xt, xe = _load_with_halo_before(x_ref, i, TM)
        he = _rms_fwd(xe, pg_ref[0:1, :])
        pos_b = i * TM - HALO + lax.broadcasted_iota(jnp.int32, (TM + HALO, 1), 0)
        ds = _pool_delta(he, pos_b)

        last = i == n - 1
        a0 = pl.multiple_of(jnp.minimum(i * TM + TM, s_len - HALO), 8)
        ye = jnp.concatenate([y_ref[pl.ds(r0, TM), :], y_ref[pl.ds(a0, HALO), :]], axis=0)
        dt = dx_ref[pl.ds(r0, TM), :]
        de = jnp.concatenate([dt, jnp.where(last, 0.0, dx_ref[pl.ds(a0, HALO), :])], axis=0)
        dye, prod = _rms_bwd(ye, qg_ref[0:1, :], de)
        dqg_ref[...] += _rowsum(prod[:TM, :])
        dys = dye * sc_ref[...]
        pos_a = i * TM + lax.broadcasted_iota(jnp.int32, (TM + HALO, 1), 0)

        dhs, dscs = [], []
        for gi, w in enumerate(WINDOWS):
            sl = slice(gi * POOL_G, (gi + 1) * POOL_G)
            wg = w_ref[gi]
            dys_g = dys[:, sl].astype(BF)
            d_g = ds[gi][HALO:, :].astype(BF)
            ypre = _dot(d_g, wg)
            dscs.append(_rowsum(dye[:TM, sl] * ypre))
            wacc[gi] += _dot_tn(d_g, dys_g[:TM, :])
            dd = _dot_nt(dys_g, wg)
            cnt = jnp.minimum(pos_a + 1, w).astype(F32)
            a = dd / cnt
            k = 1
            while k < w:
                a = a + pltpu.roll(a, TM + HALO - k, 0)
                k *= 2
            dhs.append(a[:TM, :] - dd[:TM, :])
        dsc_ref[...] += jnp.concatenate(dscs, axis=1)
        dh = jnp.concatenate(dhs, axis=1)
        dxp, prod2 = _rms_bwd(xt, pg_ref[0:1, :], dh)
        dpg_ref[...] += _rowsum(prod2)
        dx0_ref[...] = dt + dxp

        @pl.when(last)
        def _():
            dw_ref[...] = wacc[...].astype(BF)

    return _call(
        body, name="mixa_bwd", grid=(n,), in_specs=[VSPEC] * 7,
        out_specs=[_row_spec(TM), _const_spec((ng, POOL_G, POOL_G)), _const_spec((1, D)),
                   _const_spec((1, D)), _const_spec((1, D))],
        out_shape=[_sds((s_len, D)), _sds((ng, POOL_G, POOL_G), BF), _sds((1, D)), _sds((1, D)), _sds((1, D))],
        scratch_shapes=[pltpu.VMEM((ng, POOL_G, POOL_G), F32)],
        args=[dx1, x, y, pre_g, pool_w, pool_scale, post_g], rider=rider)


def _ple_math(layer, x, p_blk, g_ref, wg_ref, wp_ref, qg_ref):
    r = _rms_fwd(x, g_ref[layer:layer + 1, :]).astype(BF)
    z = _dot(r, wg_ref[...])
    pe = _dot(p_blk.astype(BF), wp_ref[...])
    return z, pe, x + _rms_fwd(pe * _sigmoid(z), qg_ref[layer:layer + 1, :])


def _ple_bwd_math(layer, dx, x, z, pe, p_blk, g_ref, wg_ref, qg_ref):
    gate = _sigmoid(z)
    de, prod = _rms_bwd(pe * gate, qg_ref[layer:layer + 1, :], dx)
    dpe = (de * gate).astype(BF)
    dz = (de * pe * gate * (1.0 - gate)).astype(BF)
    dwp = _dot_tn(p_blk.astype(BF), dpe)
    g = g_ref[layer:layer + 1, :]
    dwg = _dot_tn(_rms_fwd(x, g).astype(BF), dz)
    dxp, prod2 = _rms_bwd(x, g, _dot_nt(dz, wg_ref[...]))
    return dx + dxp, dwp, dwg, _rowsum(prod2), _rowsum(prod)


def _ffn_fwd(layer, x1, pre_g, wgu, wd, post_g, rider=None, head=None):
    s_len = x1.shape[0]

    def body(*refs):
        if head:
            (x_ref, pg_ref, wgu_ref, wd_ref, qg_ref, p_ref, eg_ref, wg_ref, wp_ref, eq_ref, t_ref,
             f_ref, g_ref, u_ref, dx_ref, dwg_ref, dwp_ref, deg_ref, deq_ref, lv_ref, gacc, pacc) = refs
        else:
            x_ref, pg_ref, wgu_ref, wd_ref, qg_ref, f_ref, x2_ref, g_ref, u_ref = refs
        x = x_ref[...]
        h = _rms_fwd(x, pg_ref[layer:layer + 1, :]).astype(BF)
        f = jnp.zeros((TM, D), F32)
        for c in range(FF // FF_HALF):
            cols = slice(c * FF_HALF, (c + 1) * FF_HALF)
            g = _dot(h, wgu_ref[0, :, cols])
            u = _dot(h, wgu_ref[1, :, cols])
            g_ref[:, cols] = g.astype(BF)
            u_ref[:, cols] = u.astype(BF)
            act = g * _sigmoid(g) * u
            f = f + _dot(act.astype(BF), wd_ref[cols, :])
        f_ref[...] = f
        x2 = x + _rms_fwd(f, qg_ref[layer:layer + 1, :])
        if not head:
            x2_ref[...] = x2
        else:
            step = pl.program_id(0)

            @pl.when(step == 0)
            def _():
                for ref in (lv_ref, deg_ref, deq_ref, gacc, pacc):
                    ref[...] = jnp.zeros_like(ref)
            p_blk = p_ref[...]
            z, pe, x3 = _ple_math(layer, x2, p_blk, eg_ref, wg_ref, wp_ref, eq_ref)
            err = x3 - t_ref[...]
            lv_ref[...] += _rowsum(err * err)
            dx, dwp, dwg, deg, deq = _ple_bwd_math(layer, err * (1.0 / D), x2, z, pe, p_blk, eg_ref, wg_ref, eq_ref)
            dx_ref[...] = dx
            pacc[...] += dwp
            gacc[...] += dwg
            deg_ref[...] += deg
            deq_ref[...] += deq

            @pl.when(step == s_len // TM - 1)
            def _():
                dwg_ref[...] = gacc[...].astype(BF)
                dwp_ref[...] = pacc[...].astype(BF)

    in_specs = [_row_spec(TM), VSPEC, VSPEC, VSPEC, VSPEC]
    args = [x1, pre_g, wgu, wd, post_g]
    out_specs = [_row_spec(TM), _row_spec(TM), _row_spec(TM, FF), _row_spec(TM, FF)]
    out_shape = [_sds((s_len, D)), _sds((s_len, D)), _sds((s_len, FF), BF), _sds((s_len, FF), BF)]
    scratch = []
    if head:
        del out_specs[1], out_shape[1]
        p, ple_g, w_gate, w_proj, ple_post_g, target = head
        in_specs += [pl.BlockSpec((None, TM, PLE), lambda i: (layer, i, 0)), VSPEC, VSPEC, VSPEC, VSPEC, _row_spec(TM)]
        args += [p, ple_g, w_gate, w_proj, ple_post_g, target]
        out_specs += [_row_spec(TM), _const_spec((D, D)), _const_spec((PLE, D))] + [_const_spec((1, D))] * 3
        out_shape += [_sds((s_len, D)), _sds((D, D), BF), _sds((PLE, D), BF)] + [_sds((1, D))] * 3
        scratch = [pltpu.VMEM((D, D), F32), pltpu.VMEM((PLE, D), F32)]
    return _call(body, name=f"ffn_fwd{layer}", grid=(s_len // TM,), in_specs=in_specs, out_specs=out_specs,
                 out_shape=out_shape, args=args, scratch_shapes=scratch, rider=rider)


GU_PIECE = 128
DN_PIECE = 64
DN_SLOT = FF // N_CHIPS
HALF_D = D // 2
CHUNK_STRIDE = 6
CHUNK_START = (1, 7, 4, 10)


def _ffn_bwd(layer, dx2, x1, f, g_pre, u_pre, pre_g, wgu, wd, post_g, kc, rider=None):
    s_len = x1.shape[0]
    tm = TM_FFN_BWD
    n = s_len // tm
    nc = FF // FF_CHUNK
    n_gu, n_dn = FF_CHUNK // GU_PIECE, FF_CHUNK // DN_PIECE
    n_pieces = 2 * n_gu + n_dn
    n_blk = FF_HALF // GU_PIECE

    def edge_rows(c, i, kc_ref):
        return (jnp.where((c == 0) | (c == nc - 1), i, n - 1), 0)

    def chunk_at(c, kc_ref):
        k = kc_ref[0]
        start = jnp.where(k == 0, CHUNK_START[0], jnp.where(k == 1, CHUNK_START[1],
                                                            jnp.where(k == 2, CHUNK_START[2], CHUNK_START[3])))
        return ((c + start) * CHUNK_STRIDE) % nc

    def exchange(kc_ref, c, accg, accu, accd, own_gu_ref, land_gu_ref, own_dn_ref, land_dn_ref,
                 pl_gu, pl_dn, sib_gu, sib_dn, mine_gu, mine_dn, sum_gu, sum_dn,
                 psend, precv, ssend, lsem, rrecv):
        x, y, core = lax.axis_index("x"), lax.axis_index("y"), lax.axis_index("c")
        lower = core == 0

        def pair_copy(cc, part):
            p = cc % 2
            src, dst = ((sib_gu, pl_gu), (sib_dn, pl_dn))[part]
            return pltpu.make_async_remote_copy(src.at[p], dst.at[cc], psend.at[p, part], precv.at[cc, part],
                                                device_id=(x, y, 1 - core), device_id_type=MESH)

        def scatter(cc, wait):
            p = cc % 2
            hidden = chunk_at(cc, kc_ref) * FF_CHUNK

            assert n_gu == 2
            k0, k1 = hidden // FF_HALF, (hidden + GU_PIECE) // FF_HALF
            blk = (hidden - k0 * FF_HALF) // GU_PIECE
            for gu in range(2):
                @pl.when(k0 == k1)
                def _():
                    piece(p, wait, 2 * gu, sum_gu.at[p, gu], k0 + 2 * gu, 0, (pl.ds(blk, 2),))

                @pl.when(k0 != k1)
                def _():
                    piece(p, wait, 2 * gu, sum_gu.at[p, gu, 0], k0 + 2 * gu, 0, (blk,))
                    piece(p, wait, 2 * gu + 1, sum_gu.at[p, gu, 1], k1 + 2 * gu, 0, (0,))

            kd = hidden // DN_SLOT
            off = pl.multiple_of(hidden - kd * DN_SLOT, DN_PIECE)
            m = jnp.minimum((DN_SLOT - off) // DN_PIECE, n_dn)
            for mm in range(1, n_dn + 1):
                @pl.when(m == mm)
                def _():
                    rows = mm * DN_PIECE
                    piece(p, wait, 2 * n_gu, sum_dn.at[p, pl.ds(0, rows), :], kd, 1, (pl.ds(off, rows), slice(None)))
                    if mm < n_dn:
                        piece(p, wait, 2 * n_gu + 1, sum_dn.at[p, pl.ds(rows, FF_CHUNK - rows), :], kd + 1, 1,
                              (pl.ds(0, FF_CHUNK - rows), slice(None)))

        def piece(p, wait, pi, src, k, t, where):
            own_ref, land_ref = ((own_gu_ref, land_gu_ref), (own_dn_ref, land_dn_ref))[t]
            kx, ky = k // 2, k % 2
            fx, fy = (kx != x).astype(jnp.int32), (ky != y).astype(jnp.int32)
            local = (fx + fy) == 0
            j = jnp.maximum(fx + 2 * fy - 1, 0)

            @pl.when(local)
            def _():
                cp = pltpu.make_async_copy(src, own_ref.at[where], lsem.at[p, pi])
                if wait:
                    cp.wait()
                else:
                    cp.start()

            @pl.when(jnp.logical_not(local))
            def _():
                cp = pltpu.make_async_remote_copy(src, land_ref.at[(j,) + where], ssend.at[p, pi],
                                                  rrecv.at[t, j], device_id=(kx, ky, core), device_id_type=MESH)
                if wait:
                    cp.wait_send()
                else:
                    cp.start()

        def add_and_scatter(cc):
            p = cc % 2
            pair_copy(cc, 0).wait_recv()
            pair_copy(cc, 1).wait_recv()
            s_gu = (mine_gu[...] + pl_gu[cc].astype(F32)).astype(BF)
            for hc in range(n_gu):
                sum_gu[p, :, hc] = s_gu[:, :, hc * GU_PIECE:(hc + 1) * GU_PIECE]
            sum_dn[p] = (mine_dn[...] + pl_dn[cc].astype(F32)).astype(BF)
            scatter(cc, wait=False)

        @pl.when(c >= 1)
        def _():
            @pl.when(c >= 3)
            def _():
                scatter(c - 3, wait=True)
            add_and_scatter(c - 1)

        @pl.when(c >= 2)
        def _():
            pair_copy(c - 2, 0).wait_send()
            pair_copy(c - 2, 1).wait_send()

        p = c % 2
        my_rows = pl.ds(pl.multiple_of(core * HALF_D, HALF_D), HALF_D)
        sib_rows = pl.ds(pl.multiple_of((1 - core) * HALF_D, HALF_D), HALF_D)
        d_v = accd[...]
        sib_gu[p, 0] = accg[sib_rows, :].astype(BF)
        sib_gu[p, 1] = accu[sib_rows, :].astype(BF)
        sib_dn[p] = jnp.where(lower, d_v[:, HALF_D:], d_v[:, :HALF_D]).astype(BF)
        mine_gu[0] = accg[my_rows, :]
        mine_gu[1] = accu[my_rows, :]
        mine_dn[...] = jnp.where(lower, d_v[:, :HALF_D], d_v[:, HALF_D:])
        pair_copy(c, 0).start()
        pair_copy(c, 1).start()

        @pl.when(c == nc - 1)
        def _():
            scatter(nc - 3, wait=True)
            add_and_scatter(nc - 1)
            for cc in (nc - 2, nc - 1):
                pair_copy(cc, 0).wait_send()
                pair_copy(cc, 1).wait_send()
                scatter(cc, wait=True)
            for t, land_ref in enumerate((land_gu_ref, land_dn_ref)):
                for j in range(N_CHIPS - 1):
                    pltpu.make_async_remote_copy(land_ref.at[j], land_ref.at[j], ssend.at[0, 0], rrecv.at[t, j],
                                                 device_id=(x, y, core), device_id_type=MESH).wait_recv()

    def body(kc_ref, dx_ref, x_ref, f_ref, gp_ref, up_ref, pg_ref, wgu_ref, wd_ref, qg_ref,
             dx1_ref, dpg_ref, dqg_ref, own_gu_ref, land_gu_ref, own_dn_ref, land_dn_ref,
             h_s, df_s, dh_s, accg, accu, accd, *comm):
        c = pl.program_id(0)
        i = pl.program_id(1)
        rows = pl.ds(pl.multiple_of(i * tm, tm), tm)
        pg = pg_ref[layer:layer + 1, :]

        @pl.when((c == 0) & (i == 0))
        def _():
            dpg_ref[...] = jnp.zeros_like(dpg_ref)
            dqg_ref[...] = jnp.zeros_like(dqg_ref)

        @pl.when(c == 0)
        def _():
            h_s[rows, :] = _rms_fwd(x_ref[...], pg).astype(BF)
            df, prod = _rms_bwd(f_ref[...], qg_ref[layer:layer + 1, :], dx_ref[...])
            df_s[rows, :] = df.astype(BF)
            dqg_ref[...] += _rowsum(prod)

        @pl.when(i == 0)
        def _():
            accg[...] = jnp.zeros_like(accg)
            accu[...] = jnp.zeros_like(accu)
            accd[...] = jnp.zeros_like(accd)

        h = h_s[rows, :]
        df = df_s[rows, :]
        wg = wgu_ref[0]
        wu = wgu_ref[1]
        g = gp_ref[...].astype(F32)
        u = up_ref[...].astype(F32)
        sg = _sigmoid(g)
        a = g * sg
        dact = _dot_nt(df, wd_ref[...])
        accd[...] += _dot_tn((a * u).astype(BF), df)
        du = (dact * a).astype(BF)
        dg = (dact * u * (sg * (1.0 + g * (1.0 - sg)))).astype(BF)
        accg[...] += _dot_tn(h, dg)
        accu[...] += _dot_tn(h, du)
        dh = _dot_nt(dg, wg) + _dot_nt(du, wu)

        @pl.when(c == 0)
        def _():
            dh_s[rows, :] = dh

        @pl.when((c > 0) & (c < nc - 1))
        def _():
            dh_s[rows, :] += dh

        @pl.when(c == nc - 1)
        def _():
            dxp, prod = _rms_bwd(x_ref[...], pg, dh_s[rows, :] + dh)
            dpg_ref[...] += _rowsum(prod)
            dx1_ref[...] = dx_ref[...] + dxp

        @pl.when(i == n - 1)
        def _():
            exchange(kc_ref, c, accg, accu, accd, own_gu_ref, land_gu_ref, own_dn_ref, land_dn_ref, *comm)

    dma = pltpu.SemaphoreType.DMA
    return _call(
        body, name=f"ffn_bwd{layer}", grid=(nc, n),
        in_specs=[pl.BlockSpec((tm, D), edge_rows), pl.BlockSpec((tm, D), edge_rows),
                  pl.BlockSpec((tm, D), lambda c, i, kc_ref: (jnp.where(c == 0, i, n - 1), 0),
                               pipeline_mode=pl.Buffered(1)),
                  pl.BlockSpec((tm, FF_CHUNK), lambda c, i, kc_ref: (i, chunk_at(c, kc_ref))),
                  pl.BlockSpec((tm, FF_CHUNK), lambda c, i, kc_ref: (i, chunk_at(c, kc_ref))),
                  VSPEC,
                  pl.BlockSpec((2, D, FF_CHUNK), lambda c, i, kc_ref: (0, 0, chunk_at(c, kc_ref))),
                  pl.BlockSpec((FF_CHUNK, D), lambda c, i, kc_ref: (chunk_at(c, kc_ref), 0)),
                  VSPEC],
        out_specs=[pl.BlockSpec((tm, D), lambda c, i, kc_ref: (jnp.where(c == nc - 1, i, 0), 0)),
                   _const_spec((1, D)), _const_spec((1, D)), ANYSPEC, ANYSPEC, ANYSPEC, ANYSPEC],
        out_shape=[_sds((s_len, D)), _sds((1, D)), _sds((1, D)),
                   _sds((n_blk, HALF_D, GU_PIECE), BF), _sds((N_CHIPS - 1, n_blk, HALF_D, GU_PIECE), BF),
                   _sds((DN_SLOT, HALF_D), BF), _sds((N_CHIPS - 1, DN_SLOT, HALF_D), BF)],
        scratch_shapes=[pltpu.VMEM((s_len, D), BF), pltpu.VMEM((s_len, D), BF), pltpu.VMEM((s_len, D), F32),
                        pltpu.VMEM((D, FF_CHUNK), F32), pltpu.VMEM((D, FF_CHUNK), F32),
                        pltpu.VMEM((FF_CHUNK, D), F32),
                        pltpu.VMEM((nc, 2, HALF_D, FF_CHUNK), BF), pltpu.VMEM((nc, FF_CHUNK, HALF_D), BF),
                        pltpu.VMEM((2, 2, HALF_D, FF_CHUNK), BF), pltpu.VMEM((2, FF_CHUNK, HALF_D), BF),
                        pltpu.VMEM((2, HALF_D, FF_CHUNK), F32), pltpu.VMEM((FF_CHUNK, HALF_D), F32),
                        pltpu.VMEM((2, 2, n_gu, HALF_D, GU_PIECE), BF), pltpu.VMEM((2, FF_CHUNK, HALF_D), BF),
                        dma((2, 2)), dma((nc, 2)), dma((2, n_pieces)), dma((2, n_pieces)), dma((2, N_CHIPS - 1))],
        args=[dx2, x1, f, g_pre, u_pre, pre_g, wgu, wd, post_g], rider=rider, prefetch=kc)


def _ple_fwd(layer, x2, p, ple_g, w_gate, w_proj, post_g, qkv=None, rider=None):
    s_len = x2.shape[0]

    def body(*refs):
        if qkv:
            (x_ref, p_ref, g_ref, wg_ref, wp_ref, qg_ref, ng_ref, kg_ref, wq_ref, wkv_ref,
             z_ref, pe_ref, x3_ref, q_ref, kv_ref) = refs
        else:
            x_ref, p_ref, g_ref, wg_ref, wp_ref, qg_ref, z_ref, pe_ref, x3_ref = refs
        z, pe, x3 = _ple_math(layer, x_ref[...], p_ref[...], g_ref, wg_ref, wp_ref, qg_ref)
        z_ref[...] = z
        pe_ref[...] = pe
        x3_ref[...] = x3
        if qkv:
            q_ref[...] = _dot(_rms_fwd(x3, ng_ref[layer + 1:layer + 2, :]).astype(BF), wq_ref[...]).astype(BF)
            kv_ref[...] = _dot(_rms_fwd(x3, kg_ref[...]).astype(BF), wkv_ref[...]).astype(BF)

    p_spec = pl.BlockSpec((None, TM, PLE), lambda i: (layer, i, 0))
    in_specs = [_row_spec(TM), p_spec, VSPEC, VSPEC, VSPEC, VSPEC]
    args = [x2, p, ple_g, w_gate, w_proj, post_g]
    out_specs = [_row_spec(TM), _row_spec(TM), _row_spec(TM)]
    out_shape = [_sds((s_len, D))] * 3
    if qkv:
        in_specs += [VSPEC] * 4
        args += list(qkv)
        out_specs += [_row_spec(TM), _row_spec(TM, 2 * KVD)]
        out_shape += [_sds((s_len, D), BF), _sds((s_len, 2 * KVD), BF)]
    return _call(body, name=f"ple_fwd{layer}", grid=(s_len // TM,), in_specs=in_specs, out_specs=out_specs,
                 out_shape=out_shape, args=args, rider=rider)


def _ple_bwd(layer, dx3, x2, z, pe, p, ple_g, w_gate, post_g, rider=None):
    s_len = x2.shape[0]
    n = s_len // TM

    def body(dx_ref, x_ref, z_ref, pe_ref, p_ref, g_ref, wg_ref, qg_ref,
             dx2_ref, dwg_ref, dwp_ref, dg_ref, dqg_ref, gacc, pacc):
        i = pl.program_id(0)

        @pl.when(i == 0)
        def _():
            gacc[...] = jnp.zeros_like(gacc)
            pacc[...] = jnp.zeros_like(pacc)
            dg_ref[...] = jnp.zeros_like(dg_ref)
            dqg_ref[...] = jnp.zeros_like(dqg_ref)

        dx2, dwp, dwg, dg, dqg = _ple_bwd_math(layer, dx_ref[...], x_ref[...], z_ref[...], pe_ref[...], p_ref[...],
                                                g_ref, wg_ref, qg_ref)
        dx2_ref[...] = dx2
        pacc[...] += dwp
        gacc[...] += dwg
        dg_ref[...] += dg
        dqg_ref[...] += dqg

        @pl.when(i == n - 1)
        def _():
            dwg_ref[...] = gacc[...].astype(BF)
            dwp_ref[...] = pacc[...].astype(BF)

    p_spec = pl.BlockSpec((None, TM, PLE), lambda i: (layer, i, 0))
    return _call(
        body, name=f"ple_bwd{layer}", grid=(n,),
        in_specs=[_row_spec(TM), _row_spec(TM), _row_spec(TM), _row_spec(TM), p_spec, VSPEC, VSPEC, VSPEC],
        out_specs=[_row_spec(TM), _const_spec((D, D)), _const_spec((PLE, D)), _const_spec((1, D)), _const_spec((1, D))],
        out_shape=[_sds((s_len, D)), _sds((D, D), BF), _sds((PLE, D), BF), _sds((1, D)), _sds((1, D))],
        scratch_shapes=[pltpu.VMEM((D, D), F32), pltpu.VMEM((PLE, D), F32)],
        args=[dx3, x2, z, pe, p, ple_g, w_gate, post_g], rider=rider)


def _qkv_bwd(dq, dkv, x3, dx4, q_g, kv_g, w_q, w_kv):
    s_len = x3.shape[0]
    n = s_len // TM

    def body(dq_ref, dkv_ref, x_ref, dx_ref, qg_ref, kg_ref, wq_ref, wkv_ref,
             dx3_ref, dwq_ref, dwkv_ref, dqg_ref, dkg_ref, qacc, kacc):
        i = pl.program_id(0)

        @pl.when(i == 0)
        def _():
            qacc[...] = jnp.zeros_like(qacc)
            kacc[...] = jnp.zeros_like(kacc)
            dqg_ref[...] = jnp.zeros_like(dqg_ref)
            dkg_ref[...] = jnp.zeros_like(dkg_ref)

        x = x_ref[...]
        qg = qg_ref[1:2, :]
        kg = kg_ref[...]
        dq_v = dq_ref[...]
        dkv_v = dkv_ref[...].astype(BF)
        qacc[...] += _dot_tn(_rms_fwd(x, qg).astype(BF), dq_v)
        kacc[...] += _dot_tn(_rms_fwd(x, kg).astype(BF), dkv_v)
        dxq, prod_q = _rms_bwd(x, qg, _dot_nt(dq_v, wq_ref[...]))
        dxk, prod_k = _rms_bwd(x, kg, _dot_nt(dkv_v, wkv_ref[...]))
        dqg_ref[...] += _rowsum(prod_q)
        dkg_ref[...] += _rowsum(prod_k)
        dx3_ref[...] = dx_ref[...] + dxq + dxk

        @pl.when(i == n - 1)
        def _():
            dwq_ref[...] = qacc[...].astype(BF)
            dwkv_ref[...] = kacc[...].astype(BF)

    outs, _ = _call(
        body, name="qkv_bwd", grid=(n,),
        in_specs=[_row_spec(TM), _row_spec(TM, 2 * KVD), _row_spec(TM), _row_spec(TM), VSPEC, VSPEC, VSPEC, VSPEC],
        out_specs=[_row_spec(TM), _const_spec((D, D)), _const_spec((D, 2 * KVD)),
                   _const_spec((1, D)), _const_spec((1, D))],
        out_shape=[_sds((s_len, D)), _sds((D, D), BF), _sds((D, 2 * KVD), BF), _sds((1, D)), _sds((1, D))],
        scratch_shapes=[pltpu.VMEM((D, D), F32), pltpu.VMEM((D, 2 * KVD), F32)],
        args=[dq, dkv, x3, dx4, q_g, kv_g, w_q, w_kv])
    return outs


def _attn_group(i, q, kvw, sink_ref, g):
    rows = GQA * BLK
    heads = [GQA * g + j for j in range(GQA)]
    off = jnp.where(i > 0, BLK, 0)
    row = lax.broadcasted_iota(jnp.int32, (rows, 2 * BLK), 0)
    rel = (row % BLK) - lax.broadcasted_iota(jnp.int32, (rows, 2 * BLK), 1) + off
    valid = (rel >= 0) & (rel < BLK)
    head_of_row = lax.broadcasted_iota(jnp.int32, (rows, 1), 0) // BLK
    slope = jnp.zeros((rows, 1), F32)
    sink = jnp.zeros((rows, 1), F32)
    for j, h in enumerate(heads):
        slope = jnp.where(head_of_row == j, SLOPES[h], slope)
        sink = jnp.where(head_of_row == j, sink_ref[0, h], sink)
    qs = jnp.concatenate([q[:, h * HEAD_DIM:(h + 1) * HEAD_DIM] for h in heads], axis=0)
    k = kvw[:, g * HEAD_DIM:(g + 1) * HEAD_DIM]
    v = kvw[:, KVD + g * HEAD_DIM:KVD + (g + 1) * HEAD_DIM]
    s = _dot_nt(qs, k) * ATT_SCALE - slope * rel.astype(F32)
    s = jnp.where(valid, s, NEG_INF)
    m = jnp.maximum(jnp.max(s, axis=-1, keepdims=True), sink)
    e = jnp.exp(s - m)
    es = jnp.exp(sink - m)
    inv = 1.0 / (jnp.sum(e, axis=-1, keepdims=True) + es)
    return e * inv, es * inv, qs, k, v


def _unstack_heads(stacked):
    return [stacked[j * BLK:(j + 1) * BLK, :] for j in range(GQA)]


def _kv_window(kv_ref, i):
    ks = pl.multiple_of(jnp.maximum(i * BLK - BLK, 0), BLK)
    return ks, kv_ref[pl.ds(ks, 2 * BLK), :]


def _attn_fwd(q, kv, sinks, x3, w_o, post_g, rider=None):
    s_len = q.shape[0]

    def body(q_ref, kv_ref, sk_ref, x_ref, wo_ref, g_ref, a_ref, y_ref, x4_ref):
        i = pl.program_id(0)
        _, kvw = _kv_window(kv_ref, i)
        q = q_ref[...]
        outs = []
        for g in range(N_KV_HEADS):
            p, _, _, _, v = _attn_group(i, q, kvw, sk_ref, g)
            outs += _unstack_heads(_dot(p.astype(BF), v))
        attn = jnp.concatenate(outs, axis=1)
        a_ref[...] = attn
        y = _dot(attn.astype(BF), wo_ref[...])
        y_ref[...] = y
        x4_ref[...] = x_ref[...] + _rms_fwd(y, g_ref[1:2, :])

    return _call(body, name="attn_fwd", grid=(s_len // BLK,),
                 in_specs=[_row_spec(BLK), VSPEC, SSPEC, _row_spec(BLK), VSPEC, VSPEC],
                 out_specs=[_row_spec(BLK)] * 3, out_shape=[_sds((s_len, D))] * 3,
                 args=[q, kv, sinks, x3, w_o, post_g], rider=rider)


ATT_STEP_BLOCKS = 2


def _attn_bwd(dx4, y, attn, q, kv, sinks, w_o, post_g, rider=None):
    s_len = q.shape[0]
    rows = ATT_STEP_BLOCKS * BLK
    n = s_len // rows

    def body(dx_ref, y_ref, a_ref, q_ref, kv_ref, sk_ref, wo_ref, g_ref,
             dq_ref, dkv_ref, dwo_ref, dg_ref, dsk_ref, wacc):
        i = pl.program_id(0)

        @pl.when(i == 0)
        def _():
            dkv_ref[...] = jnp.zeros_like(dkv_ref)
            wacc[...] = jnp.zeros_like(wacc)
            dg_ref[...] = jnp.zeros_like(dg_ref)
            dsk_ref[...] = jnp.zeros_like(dsk_ref)

        dy, prod = _rms_bwd(y_ref[...], g_ref[1:2, :], dx_ref[...])
        dg_ref[...] += _rowsum(prod)
        dyb = dy.astype(BF)
        attn_all = a_ref[...]
        wacc[...] += _dot_tn(attn_all.astype(BF), dyb)
        d_o_all = _dot_nt(dyb, wo_ref[...])
        q_all = q_ref[...]
        lane = lax.broadcasted_iota(jnp.int32, (1, D), 1)
        dsk = jnp.zeros((1, D), F32)
        for sub in range(ATT_STEP_BLOCKS):
            blk = i * ATT_STEP_BLOCKS + sub
            sl = slice(sub * BLK, (sub + 1) * BLK)
            d_o, q = d_o_all[sl, :], q_all[sl, :]
            dod = d_o * attn_all[sl, :]
            ks, kvw = _kv_window(kv_ref, blk)
            dqs, dks, dvs = [], [], []
            for g in range(N_KV_HEADS):
                p, ps, qs, k, v = _attn_group(blk, q, kvw, sk_ref, g)
                cols = [slice((GQA * g + j) * HEAD_DIM, (GQA * g + j + 1) * HEAD_DIM) for j in range(GQA)]
                do_s = jnp.concatenate([d_o[:, c] for c in cols], axis=0).astype(BF)
                dsum = jnp.concatenate([jnp.sum(dod[:, c], axis=-1, keepdims=True) for c in cols], axis=0)
                dp = _dot_nt(do_s, v)
                dsb = (p * (dp - dsum) * ATT_SCALE).astype(BF)
                sink_part = ps * dsum
                for j in range(GQA):
                    dsk = dsk + jnp.where(lane == GQA * g + j, -_rowsum(sink_part[j * BLK:(j + 1) * BLK, :]), 0.0)
                dqs += _unstack_heads(_dot(dsb, k))
                dks.append(_dot_tn(dsb, qs))
                dvs.append(_dot_tn(p.astype(BF), do_s))
            dq_ref[sl, :] = jnp.concatenate(dqs, axis=1).astype(BF)
            dkv_ref[pl.ds(ks, 2 * BLK), :] += jnp.concatenate(dks + dvs, axis=1)
        dsk_ref[...] += dsk

        @pl.when(i == n - 1)
        def _():
            dwo_ref[...] = wacc[...].astype(BF)

    return _call(
        body, name="attn_bwd", grid=(n,),
        in_specs=[_row_spec(rows), _row_spec(rows), _row_spec(rows), _row_spec(rows), VSPEC, SSPEC, VSPEC, VSPEC],
        out_specs=[_row_spec(rows), _const_spec((s_len, 2 * KVD)), _const_spec((D, D)),
                   _const_spec((1, D)), _const_spec((1, D))],
        out_shape=[_sds((s_len, D), BF), _sds((s_len, 2 * KVD)), _sds((D, D), BF), _sds((1, D)), _sds((1, D))],
        scratch_shapes=[pltpu.VMEM((D, D), F32)],
        args=[dx4, y, attn, q, kv, sinks, w_o, post_g], rider=rider)


Big = collections.namedtuple("Big", "name src layer L A R C rb")


def _bigs():
    out = {"pool_w": Big("pool_w", "pool_w", None, 4, 4, POOL_G // N_CHIPS, POOL_G, 32)}
    for l in range(2):
        out[f"w_gu{l}"] = Big(f"w_gu{l}", "w_gu", l, 1, 2, D, FF_HALF, 256)
        out[f"w_down{l}"] = Big(f"w_down{l}", "w_down", l, 1, 4, FF // N_CHIPS, D, 352)
        out[f"w_ple_gate{l}"] = Big(f"w_ple_gate{l}", "w_ple_gate", l, 1, 4, D // N_CHIPS, D, 128)
        out[f"w_ple_proj{l}"] = Big(f"w_ple_proj{l}", "w_ple_proj", l, 1, 1, PLE, D // N_CHIPS, 128)
    out["w_q"] = Big("w_q", "w_q", None, 1, 4, D // N_CHIPS, D, 128)
    out["w_o"] = Big("w_o", "w_o", None, 1, 4, D // N_CHIPS, D, 128)
    out["w_kv"] = Big("w_kv", "w_kv", None, 1, 4, D // N_CHIPS, 2 * KVD, 128)
    return out


BIGS = _bigs()
POOL_SCALE = Big("pool_scale", "pool_scale", None, 1, 1, 1, D // N_CHIPS, 1)
BIG_SOURCES = ("w_gu", "w_down", "w_ple_gate", "w_ple_proj", "w_q", "w_o", "w_kv", "pool_w")


def _ncb(t):
    return N_CHIPS // t.A


def _full_shape(t, rows=None):
    return (t.L, t.A, t.R if rows is None else rows, _ncb(t) * t.C)


def _slot_index(t, k):
    return k // _ncb(t), k % _ncb(t)


def _slot(ref, t, k, row0, rows):
    a, cb = _slot_index(t, k)
    return ref.at[:, a, pl.ds(row0, rows), pl.ds(pl.multiple_of(cb * t.C, 128), t.C)]


def _place_job(t, w, out_dtype=BF):
    nb = next((nb for nb in (8, 4, 2, 1) if t.R % (16 * nb) == 0), 1) if t.L == 1 else 1
    rb = t.R // nb

    def fn(j, kc_ref, ins, outs):
        outs[0][...] = ins[0][...].astype(out_dtype)

    def in_map(j, kc_ref):
        return (j // nb if t.layer is None else t.layer, j % nb, 0)

    def out_map(j, kc_ref):
        a, cb = _slot_index(t, kc_ref[0])
        return (j // nb, a, j % nb, cb)

    return Job(t.L * nb, [(w, (None, rb, t.C), in_map)],
               [(_sds(_full_shape(t), out_dtype), (None, None, rb, t.C), out_map)], fn)


def _mesh_position():
    x, y, c = lax.axis_index("x"), lax.axis_index("y"), lax.axis_index("c")
    chips = [(1 - x, y), (x, 1 - y), (1 - x, 1 - y)]
    return x, y, c, chips


DIRECT_BELOW = 1024


def _gather_rider(parts, fulls):
    nt = len(parts)
    TO_X, TO_Y, FWD_X, FWD_Y, SIB_X, SIB_Y, SIB_D = range(7)

    def rows_of(ti, core):
        t, r0, r1 = parts[ti]
        h = (r1 - r0) // 2
        return r0 + core * h, h

    def copy(outs, sems, kind, ti, k_src, row0, rows, dev):
        region = _slot(outs[ti], parts[ti][0], k_src, row0, rows)
        return pltpu.make_async_remote_copy(region, region, sems[0].at[ti, kind], sems[1].at[ti, kind],
                                            device_id=dev, device_id_type=MESH)

    def plan(outs, sems):
        x, y, c, _ = _mesh_position()
        me, kx, ky, kd = 2 * x + y, 2 * (1 - x) + y, 2 * x + (1 - y), 2 * (1 - x) + (1 - y)
        dev_x, dev_y, dev_d, sib = (1 - x, y, c), (x, 1 - y, c), (1 - x, 1 - y, c), (x, y, 1 - c)

        def whole(ti):
            return 0, parts[ti][0].R

        def mk(kind, k_send, k_recv, dev, send_rows, recv_rows):
            def build(ti, side):
                k_src = k_send if side == "s" else k_recv
                row0, rows = (send_rows if side == "s" else recv_rows)(ti)
                return copy(outs, sems, kind, ti, k_src, row0, rows, dev)
            return build

        def first_half(core):
            return lambda ti: (rows_of(ti, core)[0], rows_of(ti, core)[1] // 2)

        def second_half(core):
            return lambda ti: (rows_of(ti, core)[0] + rows_of(ti, core)[1] // 2, rows_of(ti, core)[1] // 2)

        mine = lambda ti: rows_of(ti, c)
        theirs = lambda ti: rows_of(ti, 1 - c)
        split = {
            TO_X: mk(TO_X, me, kx, dev_x, mine, mine),
            TO_Y: mk(TO_Y, me, ky, dev_y, mine, mine),
            FWD_X: mk(FWD_X, ky, kd, dev_x, first_half(c), first_half(c)),
            FWD_Y: mk(FWD_Y, kx, kd, dev_y, second_half(c), second_half(c)),
            SIB_X: mk(SIB_X, kx, kx, sib, mine, theirs),
            SIB_Y: mk(SIB_Y, ky, ky, sib, mine, theirs),
            SIB_D: mk(SIB_D, kd, kd, sib, mine, theirs),
        }
        direct = {
            TO_X: mk(TO_X, me, kx, dev_x, whole, whole),
            TO_Y: mk(TO_Y, me, ky, dev_y, whole, whole),
            FWD_X: mk(FWD_X, me, kd, dev_d, whole, whole),
        }
        return split, direct

    is_split = [t.L * t.R * t.C >= DIRECT_BELOW for t, _, _ in parts]
    assert all(s or (r0, r1) == (0, t.R) for s, (t, r0, r1) in zip(is_split, parts))

    def start(ins, outs, sems):
        split, direct = plan(outs, sems)
        for ti in range(nt):
            kinds = split if is_split[ti] else direct
            kinds[TO_X](ti, "s").start()
            kinds[TO_Y](ti, "s").start()
            if not is_split[ti]:
                kinds[FWD_X](ti, "s").start()

    def mid(ins, outs, sems):
        split, _ = plan(outs, sems)
        for ti in range(nt):
            if is_split[ti]:
                split[TO_Y](ti, "r").wait_recv()
                split[FWD_X](ti, "s").start()
                split[SIB_Y](ti, "s").start()
        for ti in range(nt):
            if is_split[ti]:
                split[TO_X](ti, "r").wait_recv()
                split[FWD_Y](ti, "s").start()
                split[SIB_X](ti, "s").start()

    def finish(ins, outs, sems):
        split, direct = plan(outs, sems)
        for ti in range(nt):
            if is_split[ti]:
                split[FWD_X](ti, "r").wait_recv()
                split[FWD_Y](ti, "r").wait_recv()
                split[SIB_D](ti, "s").start()
            else:
                for kind in (TO_X, TO_Y, FWD_X):
                    direct[kind](ti, "r").wait_recv()
        for ti in range(nt):
            if is_split[ti]:
                for kind in (SIB_X, SIB_Y, SIB_D):
                    split[kind](ti, "r").wait_recv()
        for ti in range(nt):
            kinds = split if is_split[ti] else direct
            for kind in kinds:
                kinds[kind](ti, "s").wait_send()

    sems = pltpu.SemaphoreType.DMA((nt, 7))
    return Rider(list(fulls), [_sds(a.shape, a.dtype) for a in fulls], {i: i for i in range(nt)},
                 [sems, sems], start, mid, finish)


def _pair_exchange_rider(specs, grads):
    nt = len(specs)

    def copy(ins, outs, sems, ti, c, sibling):
        half = specs[ti].R // 2
        return pltpu.make_async_remote_copy(ins[ti].at[:, :, pl.ds((1 - c) * half, half), :], outs[ti],
                                            sems[0].at[ti], sems[1].at[ti], device_id=sibling, device_id_type=MESH)

    def start(ins, outs, sems):
        x, y, c, _ = _mesh_position()
        for ti in range(nt):
            copy(ins, outs, sems, ti, c, (x, y, 1 - c)).start()

    def finish(ins, outs, sems):
        x, y, c, _ = _mesh_position()
        for ti in range(nt):
            copy(ins, outs, sems, ti, c, (x, y, 1 - c)).wait()

    sems = pltpu.SemaphoreType.DMA((nt,))
    return Rider(list(grads), [_sds(_full_shape(t, t.R // 2), BF) for t in specs], {}, [sems, sems], start, None, finish)


def _pair_sum_job(t, g, land):
    assert t.L == 1
    half = t.R // 2
    nj = half // t.rb
    block = (None, t.A, t.rb, _ncb(t) * t.C)

    def fn(j, kc_ref, ins, outs):
        outs[0][...] = (ins[0][...].astype(F32) + ins[1][...].astype(F32)).astype(BF)

    return Job(nj,
               [(g, block, lambda j, kc_ref: (0, 0, kc_ref[1] * nj + j, 0)),
                (land, block, lambda j, kc_ref: (0, 0, j, 0))],
               [(_sds(_full_shape(t, half), BF), block, lambda j, kc_ref: (0, 0, j, 0))], fn)


def _scatter_rider(specs, sums):
    nt = len(specs)

    def copy(ins, outs, sems, ti, j, chip, c):
        t = specs[ti]
        cx, cy = chip
        return pltpu.make_async_remote_copy(_slot(ins[ti], t, 2 * cx + cy, 0, t.R // 2), outs[ti].at[j],
                                            sems[0].at[ti, j], sems[1].at[ti, j],
                                            device_id=(cx, cy, c), device_id_type=MESH)

    def start(ins, outs, sems):
        _, _, c, chips = _mesh_position()
        for j, chip in enumerate(chips):
            for ti in range(nt):
                copy(ins, outs, sems, ti, j, chip, c).start()

    def finish(ins, outs, sems):
        _, _, c, chips = _mesh_position()
        for j, chip in enumerate(chips):
            for ti in range(nt):
                copy(ins, outs, sems, ti, j, chip, c).wait()

    sems = pltpu.SemaphoreType.DMA((nt, N_CHIPS - 1))
    return Rider(list(sums), [_sds((N_CHIPS - 1, t.L, t.R // 2, t.C), BF) for t in specs], {}, [sems, sems],
                 start, None, finish)


def _chip_sum_job(ts, landed):
    t0 = ts[0]
    assert t0.L == 1
    half = t0.R // 2
    nj = half // t0.rb

    def local(j, li):
        return jnp.clip(j - li * nj, 0, nj - 1)

    ins = []
    for li, t in enumerate(ts):
        s, land = landed[t.name]

        def own_map(j, kc_ref, li=li, t=t):
            a, cb = _slot_index(t, kc_ref[0])
            return (0, a, local(j, li), cb)

        ins.append((s, (None, None, t.rb, t.C), own_map))
        ins.append((land, (N_CHIPS - 1, None, t.rb, t.C), lambda j, kc_ref, li=li: (0, 0, local(j, li), 0)))

    def fn(j, kc_ref, in_refs, outs):
        for li in range(len(ts)):
            @pl.when(j // nj == li)
            def _():
                acc = in_refs[2 * li][...].astype(F32)
                for k in range(N_CHIPS - 1):
                    acc = acc + in_refs[2 * li + 1][k].astype(F32)
                outs[0][...] = acc

    return Job(len(ts) * nj, ins,
               [(_sds((len(ts), t0.R, t0.C)), (None, t0.rb, t0.C),
                 lambda j, kc_ref: (j // nj, kc_ref[1] * nj + j % nj, 0))], fn)


def _adamw_job(rb, w, g, m, v):
    n_layers, r, c = w.shape
    nb = r // rb
    block = (None, rb, c)
    index = lambda j, kc_ref: (j // nb, j % nb, 0)

    def fn(j, kc_ref, ins, outs):
        g_v = ins[1][...]
        outs[0][...] = g_v
        outs[1][...], outs[2][...], outs[3][...] = _adamw_math(ins[0][...], g_v, ins[2][...], ins[3][...])

    return Job(n_layers * nb, [(a, block, index) for a in (w, g, m, v)],
               [(_sds(w.shape), block, index)] * 4, fn)


def _chip_sum_fused_job(ts, fused, by_cols):
    t0 = ts[0]
    own0 = fused[t0.name][0]
    if by_cols:
        rows, cols = own0.shape
    else:
        nb, rows, bw = own0.shape
        cols = nb * bw
    nj = rows // t0.rb

    def local(j, li):
        return jnp.clip(j - li * nj, 0, nj - 1)

    ins = []
    for li, t in enumerate(ts):
        own, land = fused[t.name]
        if by_cols:
            ins.append((own, (t.rb, cols), lambda j, kc_ref, li=li: (local(j, li), 0)))
            ins.append((land, (N_CHIPS - 1, t.rb, cols), lambda j, kc_ref, li=li: (0, local(j, li), 0)))
        else:
            ins.append((own, (nb, t.rb, bw), lambda j, kc_ref, li=li: (0, local(j, li), 0)))
            ins.append((land, (N_CHIPS - 1, nb, t.rb, bw), lambda j, kc_ref, li=li: (0, 0, local(j, li), 0)))

    def fn(j, kc_ref, in_refs, outs):
        for li in range(len(ts)):
            @pl.when(j // nj == li)
            def _():
                acc = in_refs[2 * li][...].astype(F32)
                for k in range(N_CHIPS - 1):
                    acc = acc + in_refs[2 * li + 1][k].astype(F32)
                outs[0][...] = acc if by_cols else jnp.concatenate([acc[b] for b in range(nb)], axis=1)

    def out_map(j, kc_ref):
        return (j // nj, j % nj, kc_ref[1]) if by_cols else (j // nj, kc_ref[1] * nj + j % nj, 0)

    return Job(len(ts) * nj, ins, [(_sds((len(ts), t0.R, t0.C)), (None, t0.rb, cols), out_map)], fn)


def _share_rider(halves, by_cols):
    nt = len(halves)

    def copy(outs, sems, ti, core, sibling):
        axis = 2 if by_cols[ti] else 1
        half = halves[ti].shape[axis] // 2
        piece = pl.ds(pl.multiple_of(core * half, 128 if by_cols[ti] else 8), half)
        part = outs[ti].at[:, :, piece] if by_cols[ti] else outs[ti].at[:, piece, :]
        return pltpu.make_async_remote_copy(part, part, sems[0].at[ti], sems[1].at[ti],
                                            device_id=sibling, device_id_type=MESH)

    def start(ins, outs, sems):
        x, y, c, _ = _mesh_position()
        for ti in range(nt):
            copy(outs, sems, ti, c, (x, y, 1 - c)).start()

    def finish(ins, outs, sems):
        x, y, c, _ = _mesh_position()
        for ti in range(nt):
            copy(outs, sems, ti, 1 - c, (x, y, 1 - c)).wait_recv()
        for ti in range(nt):
            copy(outs, sems, ti, c, (x, y, 1 - c)).wait_send()

    sems = pltpu.SemaphoreType.DMA((nt,))
    return Rider(list(halves), [_sds(a.shape, a.dtype) for a in halves], {i: i for i in range(nt)}, [sems, sems],
                 start, None, finish)


def _both(r1, r2):
    assert r1.mid is None and r2.mid is None
    ni, no, ns = len(r1.arrays), len(r1.out_shapes), len(r1.scratch)

    def split(fn1, fn2):
        def run(ins, outs, scr):
            fn1(ins[:ni], outs[:no], scr[:ns])
            fn2(ins[ni:], outs[no:], scr[ns:])
        return run

    aliases = dict(r1.aliases)
    aliases.update({ni + a: no + b for a, b in r2.aliases.items()})
    return Rider(r1.arrays + r2.arrays, r1.out_shapes + r2.out_shapes, aliases, r1.scratch + r2.scratch,
                 split(r1.start, r2.start), None, split(r1.finish, r2.finish))


def _adamw_math(w, g, m, v):
    m = B1 * m + (1.0 - B1) * g
    v = B2 * v + (1.0 - B2) * (g * g)
    delta = -LR * ((m / BC1) / (jnp.sqrt(v / BC2) + AEPS) + WD * w)
    return delta, m, v


GAIN_ROWS = {"pre_mix_g": 0, "post_mix_g": 2, "pre_ffn_g": 4, "post_ffn_g": 6, "ple_g": 8, "ple_post_g": 10}
ROW_KV_G, ROW_POOL_SCALE, ROW_SINKS, ROW_LOSS, PACK_ROWS = 12, 13, 14, 15, 16
SMALL_NAMES = tuple(GAIN_ROWS) + ("kv_g", "pool_scale", "sinks")


def _small_all_reduce(rows, dpool, rider=None):
    ng, pr = len(WINDOWS), POOL_G // N_CHIPS

    def body(*refs):
        row_refs = refs[:PACK_ROWS]
        dpool_ref, tot_ref, gpool_ref, pack, land, pland, send, recv, psend, precv = refs[PACK_ROWS:]
        x, y, c, _ = _mesh_position()
        me = 4 * x + 2 * y + c
        for r in range(PACK_ROWS):
            pack[r:r + 1, :] = row_refs[r][...]

        def shard_of(k):
            return dpool_ref.at[:, pl.ds(pl.multiple_of(k * pr, pr), pr), :]

        cps = []
        for j in range(1, N_DEV):
            px, py, pc = x ^ (j >> 2), y ^ ((j >> 1) & 1), c ^ (j & 1)
            cps.append(pltpu.make_async_remote_copy(pack, land.at[me], send.at[j], recv.at[j],
                                                    device_id=(px, py, pc), device_id_type=MESH))
            cps.append(pltpu.make_async_remote_copy(shard_of(2 * px + py), pland.at[me], psend.at[j], precv.at[j],
                                                    device_id=(px, py, pc), device_id_type=MESH))
        for cp in cps:
            cp.start()
        land[me] = pack[...]
        pland[me] = dpool_ref[:, pl.ds(pl.multiple_of((2 * x + y) * pr, pr), pr), :]
        for j in range(1, N_DEV):
            pltpu.make_async_remote_copy(pack, land.at[me ^ j], send.at[j], recv.at[j],
                                         device_id=(x, y, c), device_id_type=MESH).wait_recv()
            pltpu.make_async_remote_copy(shard_of(0), pland.at[me ^ j], psend.at[j], precv.at[j],
                                         device_id=(x, y, c), device_id_type=MESH).wait_recv()
        for cp in cps:
            cp.wait_send()
        tot = land[0]
        gp = pland[0].astype(F32)
        for d in range(1, N_DEV):
            tot = tot + land[d]
            gp = gp + pland[d].astype(F32)
        tot_ref[...] = tot
        gpool_ref[...] = gp

    sems = pltpu.SemaphoreType.DMA((N_DEV,))
    return _call(
        body, name="small_all_reduce", grid=(1,),
        in_specs=[VSPEC] * (PACK_ROWS + 1), out_specs=[VSPEC, VSPEC],
        out_shape=[_sds((PACK_ROWS, D)), _sds((ng, pr, POOL_G))],
        scratch_shapes=[pltpu.VMEM((PACK_ROWS, D), F32), pltpu.VMEM((N_DEV, PACK_ROWS, D), F32),
                        pltpu.VMEM((N_DEV, ng, pr, POOL_G), BF), sems, sems, sems, sems],
        args=[*rows, dpool], rider=rider)


def _small_adamw(tot, kc, small_w, small_m, small_v):
    names = SMALL_NAMES
    n = len(names)

    def body(*refs):
        tot_ref, kc_ref = refs[0], refs[1]
        w_refs = dict(zip(names, refs[2:2 + n]))
        m_refs = dict(zip(names, refs[2 + n:2 + 2 * n]))
        v_refs = dict(zip(names, refs[2 + 2 * n:2 + 3 * n]))
        loss_ref = refs[2 + 3 * n]
        out_refs = {nm: refs[3 + 3 * n + 4 * k: 7 + 3 * n + 4 * k] for k, nm in enumerate(names)}
        tot = tot_ref[...]
        loss_ref[...] = 0.5 * jnp.sum(tot[ROW_LOSS:ROW_LOSS + 1, :], axis=-1, keepdims=True) * (1.0 / D)

        def update(nm, g):
            g_ref, d_ref, nm_ref, nv_ref = out_refs[nm]
            g_ref[...] = g
            d_ref[...], nm_ref[...], nv_ref[...] = _adamw_math(w_refs[nm][...], g, m_refs[nm][...], v_refs[nm][...])

        for nm, r in GAIN_ROWS.items():
            update(nm, tot[r:r + 2, :])
        update("kv_g", tot[ROW_KV_G:ROW_KV_G + 1, :])
        k = kc_ref[0]
        width = D // N_CHIPS
        g_scale = jnp.zeros((1, width), F32)
        for kk in range(N_CHIPS):
            g_scale = g_scale + jnp.where(k == kk, tot[ROW_POOL_SCALE:ROW_POOL_SCALE + 1, kk * width:(kk + 1) * width], 0.0)
        update("pool_scale", g_scale)
        update("sinks", tot[ROW_SINKS:ROW_SINKS + 1, 0:N_HEADS])

    ins = [tot, kc] + [small_w[nm] for nm in names] + [small_m[nm] for nm in names] + [small_v[nm] for nm in names]
    out_shape = [_sds((1, 1))]
    for nm in names:
        out_shape += [_sds(small_w[nm].shape)] * 4
    outs = pl.pallas_call(
        body, name="small_adamw",
        in_specs=[VSPEC, SSPEC] + [VSPEC] * (3 * n), out_specs=[VSPEC] * len(out_shape), out_shape=out_shape,
        compiler_params=_params(),
    )(*ins)
    return outs[0], {nm: outs[1 + 4 * k: 5 + 4 * k] for k, nm in enumerate(names)}


def _compute_layout(t, full):
    if t.src == "w_gu":
        return full.reshape(2, D, FF)
    if t.src == "pool_w":
        return full.reshape(len(WINDOWS), POOL_G, POOL_G)
    if t.src == "pool_scale":
        return full.reshape(1, D)
    return full.reshape(t.A * t.R, _ncb(t) * t.C)


def kernel(x, p, pre_mix_g, post_mix_g, pre_ffn_g, post_ffn_g, pool_w, pool_scale, kv_g, w_kv, w_q, sinks, w_o, w_gu, w_down, ple_g, w_ple_gate, w_ple_proj, ple_post_g, loss_target, m_pre_mix_g, m_post_mix_g, m_pre_ffn_g, m_post_ffn_g, m_pool_w, m_pool_scale, m_kv_g, m_w_kv, m_w_q, m_sinks, m_w_o, m_w_gu, m_w_down, m_ple_g, m_w_ple_gate, m_w_ple_proj, m_ple_post_g, v_pre_mix_g, v_post_mix_g, v_pre_ffn_g, v_post_ffn_g, v_pool_w, v_pool_scale, v_kv_g, v_w_kv, v_w_q, v_sinks, v_w_o, v_w_gu, v_w_down, v_ple_g, v_w_ple_gate, v_w_ple_proj, v_ple_post_g):
    weights = dict(pre_mix_g=pre_mix_g, post_mix_g=post_mix_g, pre_ffn_g=pre_ffn_g, post_ffn_g=post_ffn_g,
                   pool_w=pool_w, pool_scale=pool_scale, kv_g=kv_g, w_kv=w_kv, w_q=w_q, sinks=sinks, w_o=w_o,
                   w_gu=w_gu, w_down=w_down, ple_g=ple_g, w_ple_gate=w_ple_gate, w_ple_proj=w_ple_proj,
                   ple_post_g=ple_post_g)
    m_in = dict(pre_mix_g=m_pre_mix_g, post_mix_g=m_post_mix_g, pre_ffn_g=m_pre_ffn_g, post_ffn_g=m_post_ffn_g,
                pool_w=m_pool_w, pool_scale=m_pool_scale, kv_g=m_kv_g, w_kv=m_w_kv, w_q=m_w_q, sinks=m_sinks,
                w_o=m_w_o, w_gu=m_w_gu, w_down=m_w_down, ple_g=m_ple_g, w_ple_gate=m_w_ple_gate,
                w_ple_proj=m_w_ple_proj, ple_post_g=m_ple_post_g)
    v_in = dict(pre_mix_g=v_pre_mix_g, post_mix_g=v_post_mix_g, pre_ffn_g=v_pre_ffn_g, post_ffn_g=v_post_ffn_g,
                pool_w=v_pool_w, pool_scale=v_pool_scale, kv_g=v_kv_g, w_kv=v_w_kv, w_q=v_w_q, sinks=v_sinks,
                w_o=v_w_o, w_gu=v_w_gu, w_down=v_w_down, ple_g=v_ple_g, w_ple_gate=v_w_ple_gate,
                w_ple_proj=v_w_ple_proj, ple_post_g=v_ple_post_g)
    order = ["pre_mix_g", "post_mix_g", "pre_ffn_g", "post_ffn_g", "pool_w", "pool_scale", "kv_g", "w_kv", "w_q",
             "sinks", "w_o", "w_gu", "w_down", "ple_g", "w_ple_gate", "w_ple_proj", "ple_post_g"]

    kc = jnp.stack([2 * lax.axis_index("x") + lax.axis_index("y"), lax.axis_index("c")]).astype(jnp.int32)
    s_len = x.shape[1]
    x2d = x.reshape(s_len, D)
    p3d = p.reshape(2, s_len, PLE)
    target = loss_target.reshape(s_len, D)
    kv_g2d = kv_g.reshape(1, D)
    gains = {nm: weights[nm] for nm in GAIN_ROWS}

    def shard_view(src, a):
        t = next(t for t in BIGS.values() if t.src == src)
        return a.reshape(-1, t.R, t.C)

    first, second = ["pool_w", "pool_scale"], ["w_gu0", "w_down0"]
    rest = [nm for nm in BIGS if nm not in first + second]
    specs = dict(BIGS, pool_scale=POOL_SCALE)
    placed = {}

    def place_job(nm):
        if nm == "pool_scale":
            return _place_job(POOL_SCALE, pool_scale.reshape(1, 1, D // N_CHIPS), F32)
        return _place_job(BIGS[nm], shard_view(BIGS[nm].src, weights[BIGS[nm].src]))

    def gather(names, rows=None):
        rows = rows or {}
        parts = [(specs[nm],) + tuple(rows.get(nm, (0, specs[nm].R))) for nm in names]
        return _gather_rider(parts, [placed[nm] for nm in names])

    def take(names, results):
        for nm, a in zip(names, results):
            placed[nm] = a

    def weight(nm):
        return _compute_layout(specs[nm], placed[nm])

    take(first, [r[0] for r in _multi_call("place_pool", [place_job(nm) for nm in first], kc)])
    cast, got = _multi_call("place_ffn0", [place_job(nm) for nm in second], kc, rider=gather(first))
    take(second, [r[0] for r in cast])
    take(first, got)
    jobs = [place_job(nm) for nm in rest]
    jobs.append(_mixa_fwd_job(x2d, gains["pre_mix_g"], weight("pool_w"), weight("pool_scale"), gains["post_mix_g"]))
    results, got = _multi_call("cast_and_mixa_fwd", jobs, kc, rider=gather(second))
    take(rest, [r[0] for r in results[:-1]])
    take(second, got)
    y0, x1 = results[-1]

    ride = ["w_ple_gate0", "w_ple_proj0", "w_q", "w_kv", "w_o", "w_gu1"]
    (f0, x2, g0, u0), got = _ffn_fwd(0, x1, gains["pre_ffn_g"], weight("w_gu0"), weight("w_down0"), gains["post_ffn_g"],
                             rider=gather(ride, {"w_gu1": (0, 320)}))
    take(ride, got)

    ride = ["w_ple_gate1", "w_ple_proj1", "w_gu1"]
    (z0, pe0, x3, q, kv), got = _ple_fwd(
        0, x2, p3d, gains["ple_g"], weight("w_ple_gate0"), weight("w_ple_proj0"), gains["ple_post_g"],
        qkv=(gains["pre_mix_g"], kv_g2d, weight("w_q"), weight("w_kv")),
        rider=gather(ride, {"w_gu1": (320, 704)}))
    take(ride, got)

    ride = ["w_down1", "w_gu1"]
    (attn, y1, x4), got = _attn_fwd(q, kv, sinks, x3, weight("w_o"), gains["post_mix_g"],
                                    rider=gather(ride, {"w_gu1": (704, D)}))
    take(ride, got)

    local = {}
    (f1, g1, u1, dx5, local["w_ple_gate1"], local["w_ple_proj1"], d_ple1, d_plepost1, loss_row), _ = _ffn_fwd(
        1, x4, gains["pre_ffn_g"], weight("w_gu1"), weight("w_down1"), gains["post_ffn_g"],
        head=(p3d, gains["ple_g"], weight("w_ple_gate1"), weight("w_ple_proj1"), gains["ple_post_g"], target))

    landed = {}
    fused = {}

    def local_grads(names):
        return [local[nm].reshape(_full_shape(BIGS[nm])) for nm in names]

    def pair_exchange(names):
        return _pair_exchange_rider([BIGS[nm] for nm in names], local_grads(names))

    def pair_sum(tag, names, lands):
        jobs = [_pair_sum_job(BIGS[nm], g, l) for nm, g, l in zip(names, local_grads(names), lands)]
        return [r[0] for r in _multi_call(f"pair_sum_{tag}", jobs, kc)]

    def scatter(names, sums):
        return _scatter_rider([BIGS[nm] for nm in names], sums)

    def keep(names, sums, got):
        for nm, s, l in zip(names, sums, got):
            landed[nm] = (s, l)

    group_a = ["w_ple_gate1", "w_ple_proj1"]
    (dx4, d_preffn1, d_postffn1, *scattered), lands_a = _ffn_bwd(
        1, dx5, x4, f1, g1, u1, gains["pre_ffn_g"], weight("w_gu1"), weight("w_down1"), gains["post_ffn_g"], kc,
        rider=pair_exchange(group_a))
    fused["w_gu1"], fused["w_down1"] = scattered[0:2], scattered[2:4]

    (dq, dkv, local["w_o"], d_postmix1, d_sinks), _ = _attn_bwd(
        dx4, y1, attn, q, kv, sinks, weight("w_o"), gains["post_mix_g"])
    dx3, local["w_q"], local["w_kv"], d_premix1, d_kvg = _qkv_bwd(
        dq, dkv, x3, dx4, gains["pre_mix_g"], kv_g2d, weight("w_q"), weight("w_kv"))

    group_b = ["w_o", "w_q", "w_kv"]
    (dx2, local["w_ple_gate0"], local["w_ple_proj0"], d_ple0, d_plepost0), lands_b = _ple_bwd(
        0, dx3, x2, z0, pe0, p3d, gains["ple_g"], weight("w_ple_gate0"), gains["ple_post_g"],
        rider=pair_exchange(group_b))
    group_ab = group_a + group_b
    sums_ab = pair_sum("ab", group_ab, lands_a + lands_b)

    group_c = ["w_ple_gate0", "w_ple_proj0"]
    (dx1, d_preffn0, d_postffn0, *scattered), got = _ffn_bwd(
        0, dx2, x1, f0, g0, u0, gains["pre_ffn_g"], weight("w_gu0"), weight("w_down0"), gains["post_ffn_g"], kc,
        rider=_both(pair_exchange(group_c), scatter(group_ab, sums_ab)))
    fused["w_gu0"], fused["w_down0"] = scattered[0:2], scattered[2:4]
    lands_c = got[:len(group_c)]
    keep(group_ab, sums_ab, got[len(group_c):])

    layers_of = lambda src: [t for t in BIGS.values() if t.src == src]
    own_scatter = ["w_gu", "w_down"]
    early = own_scatter + ["w_q", "w_o", "w_kv"]
    late = ["w_ple_gate", "w_ple_proj"]
    by_cols = lambda srcs: [src == "w_down" for src in srcs]
    jobs = [_chip_sum_fused_job(layers_of(src), fused, by_cols=src == "w_down") for src in own_scatter]
    jobs += [_chip_sum_job(layers_of(src), landed) for src in early if src not in own_scatter]
    jobs_c = [_pair_sum_job(BIGS[nm], g, l) for nm, g, l in zip(group_c, local_grads(group_c), lands_c)]
    sums = [r[0] for r in _multi_call("chip_sum_early", jobs + jobs_c, kc)]
    halves, sums_c = sums[:len(jobs)], sums[len(jobs):]
    (dx0, d_pool, d_scale, d_postmix0, d_premix0), got = _mixa_bwd(
        dx1, x2d, y0, gains["pre_mix_g"], weight("pool_w"), weight("pool_scale"), gains["post_mix_g"],
        rider=_both(scatter(group_c, sums_c), _share_rider(halves, by_cols(early))))
    keep(group_c, sums_c, got[:len(group_c)])
    full_grads = dict(zip(early, got[len(group_c):]))

    rows = [d_premix0, d_premix1, d_postmix0, d_postmix1, d_preffn0, d_preffn1, d_postffn0, d_postffn1,
            d_ple0, d_ple1, d_plepost0, d_plepost1, d_kvg, d_scale, d_sinks, loss_row]
    as2d = lambda a: a.reshape(1, D) if a.ndim == 1 else a
    halves = [r[0] for r in _multi_call("chip_sum_late", [_chip_sum_job(layers_of(src), landed) for src in late], kc)]
    (tot, g_pool), got = _small_all_reduce(rows, d_pool, rider=_share_rider(halves, by_cols(late)))
    full_grads.update(zip(late, got))
    full_grads["pool_w"] = g_pool
    loss, small = _small_adamw(tot, kc, {nm: as2d(weights[nm]) for nm in SMALL_NAMES},
                               {nm: as2d(m_in[nm]) for nm in SMALL_NAMES},
                               {nm: as2d(v_in[nm]) for nm in SMALL_NAMES})


    def adam_job(src):
        rb = layers_of(src)[0].rb // (1 if src == "pool_w" else 2)
        return _adamw_job(rb, shard_view(src, weights[src]), full_grads[src],
                          shard_view(src, m_in[src]), shard_view(src, v_in[src]))

    out = {"grad": {}, "delta": {}, "new_m": {}, "new_v": {}}
    untouched = Rider([dx0], [_sds(dx0.shape)], {0: 0}, [], None, None, None)
    res, (dx0,) = _multi_call("adamw", [adam_job(src) for src in BIG_SOURCES], kc, rider=untouched)
    results = dict(zip(BIG_SOURCES, res))
    for src in BIG_SOURCES:
        shape = weights[src].shape
        for kind, a in zip(("grad", "delta", "new_m", "new_v"), results[src]):
            out[kind][src] = a.reshape(shape)
    for nm in SMALL_NAMES:
        shape = weights[nm].shape
        for kind, a in zip(("grad", "delta", "new_m", "new_v"), small[nm]):
            out[kind][nm] = a.reshape(shape)

    return (loss.reshape(()), dx0.reshape(x.shape),
            *[out["grad"][nm] for nm in order], *[out["delta"][nm] for nm in order],
            *[out["new_m"][nm] for nm in order], *[out["new_v"][nm] for nm in order])
```

```python
import collections

import jax
import jax.numpy as jnp
from jax import lax
from jax.experimental import pallas as pl
from jax.experimental.pallas import tpu as pltpu

D = 1024
FF = 2816
N_HEADS = 16
HEAD_DIM = 64
N_KV_HEADS = 4
GQA = N_HEADS // N_KV_HEADS
KVD = N_KV_HEADS * HEAD_DIM
PLE = 256
BLK = 128
WINDOWS = (2, 4, 8, 16)
POOL_G = 256
HALO = 16
EPS = 1e-6
NEG_INF = -1e30
ATT_SCALE = HEAD_DIM ** -0.5
SLOPES = tuple(2.0 ** (-8.0 * (h + 1) / N_HEADS) for h in range(N_HEADS))
N_CHIPS = 4
N_DEV = 8

LR, B1, B2, AEPS, WD, STEP = 0.001, 0.9, 0.999, 1e-08, 0.01, 10
BC1 = 1.0 - B1 ** STEP
BC2 = 1.0 - B2 ** STEP

BF = jnp.bfloat16
F32 = jnp.float32
MESH = pl.DeviceIdType.MESH
VMEM_LIMIT_V7X = 58 * 1024 * 1024
TM = 256
TM_FFN_BWD = 512
FF_CHUNK = 256
FF_HALF = FF // 2

VSPEC = pl.BlockSpec(memory_space=pltpu.VMEM)
SSPEC = pl.BlockSpec(memory_space=pltpu.SMEM)
ANYSPEC = pl.BlockSpec(memory_space=pl.ANY)


def _params(n_grid=0):
    sem = ("arbitrary",) * n_grid if n_grid else None
    return pltpu.CompilerParams(dimension_semantics=sem, vmem_limit_bytes=VMEM_LIMIT_V7X)


def _sds(shape, dtype=F32):
    return jax.ShapeDtypeStruct(tuple(shape), dtype)


Rider = collections.namedtuple("Rider", "arrays out_shapes aliases scratch start mid finish")
MID_NUM, MID_DEN = 5, 8


def _call(body, *, name, grid, in_specs, out_specs, out_shape, args, scratch_shapes=(), rider=None, prefetch=None):
    ni, no, ns = len(in_specs), len(out_specs), len(scratch_shapes)
    npre = 0 if prefetch is None else 1
    pre = [] if prefetch is None else [prefetch]
    if rider is None:
        rider = Rider([], [], {}, [], None, None, None)
    ri, ro = len(rider.arrays), len(rider.out_shapes)

    def full(*refs):
        pre_refs, refs = refs[:npre], refs[npre:]
        ins, refs = refs[:ni], refs[ni:]
        rins, refs = refs[:ri], refs[ri:]
        outs, refs = refs[:no], refs[no:]
        routs, refs = refs[:ro], refs[ro:]
        scr, rscr = refs[:ns], refs[ns:]
        ids = [pl.program_id(a) for a in range(len(grid))]
        first = ids[0] == 0
        last = ids[0] == grid[0] - 1
        for a in range(1, len(grid)):
            first = first & (ids[a] == 0)
            last = last & (ids[a] == grid[a] - 1)

        if rider.start is not None:
            @pl.when(first)
            def _():
                rider.start(rins, routs, rscr)

        if rider.mid is not None:
            assert len(grid) == 1

            @pl.when(ids[0] == (grid[0] * MID_NUM) // MID_DEN)
            def _():
                rider.mid(rins, routs, rscr)

        body(*pre_refs, *ins, *outs, *scr)

        if rider.finish is not None:
            @pl.when(last)
            def _():
                rider.finish(rins, routs, rscr)

    outs = pl.pallas_call(
        full, name=name,
        grid_spec=pltpu.PrefetchScalarGridSpec(
            num_scalar_prefetch=npre, grid=grid,
            in_specs=list(in_specs) + [ANYSPEC] * ri, out_specs=list(out_specs) + [ANYSPEC] * ro,
            scratch_shapes=list(scratch_shapes) + list(rider.scratch)),
        out_shape=list(out_shape) + list(rider.out_shapes),
        input_output_aliases={npre + ni + a: no + b for a, b in rider.aliases.items()},
        compiler_params=_params(len(grid)))(*pre, *args, *rider.arrays)
    return list(outs[:no]), list(outs[no:])


Job = collections.namedtuple("Job", "steps ins outs fn")


def _multi_call(name, jobs, kc, rider=None):
    n = max(job.steps for job in jobs)

    def clamped(index, steps):
        return lambda s, kc_ref: index(jnp.minimum(s, steps - 1), kc_ref)

    in_specs, out_specs, out_shape, args = [], [], [], []
    for job in jobs:
        for arr, block, index, *single in job.ins:
            mode = dict(pipeline_mode=pl.Buffered(1)) if single and single[0] else {}
            in_specs.append(pl.BlockSpec(block, clamped(index, job.steps), **mode))
            args.append(arr)
        for sds, block, index in job.outs:
            out_specs.append(pl.BlockSpec(block, clamped(index, job.steps)))
            out_shape.append(sds)
    n_in = len(args)

    def body(kc_ref, *refs):
        s = pl.program_id(0)
        i0, o0 = 0, n_in
        for job in jobs:
            ins, outs = refs[i0:i0 + len(job.ins)], refs[o0:o0 + len(job.outs)]
            i0, o0 = i0 + len(job.ins), o0 + len(job.outs)

            @pl.when(s < job.steps)
            def _():
                job.fn(s, kc_ref, ins, outs)

    outs, routs = _call(body, name=name, grid=(n,), in_specs=in_specs, out_specs=out_specs, out_shape=out_shape,
                        args=args, prefetch=kc, rider=rider)
    res, o0 = [], 0
    for job in jobs:
        res.append(outs[o0:o0 + len(job.outs)])
        o0 += len(job.outs)
    return res if rider is None else (res, routs)


def _rms_fwd(x, g):
    r = lax.rsqrt(jnp.mean(x * x, axis=-1, keepdims=True) + EPS)
    return x * r * g


def _rms_bwd(x, g, dy):
    r = lax.rsqrt(jnp.mean(x * x, axis=-1, keepdims=True) + EPS)
    xn = x * r
    dxn = dy * g
    dx = r * (dxn - xn * jnp.mean(dxn * xn, axis=-1, keepdims=True))
    return dx, dy * xn


def _rowsum(a):
    return jnp.sum(a, axis=0, keepdims=True)


def _sigmoid(z):
    return 1.0 / (1.0 + jnp.exp(-z))


def _dot(a, b):
    return jnp.dot(a, b, preferred_element_type=F32)


def _dot_nt(a, b):
    return lax.dot_general(a, b, (((1,), (1,)), ((), ())), preferred_element_type=F32)


def _dot_tn(a, b):
    return lax.dot_general(a, b, (((0,), (0,)), ((), ())), preferred_element_type=F32)


def _row_spec(tm, width=D):
    return pl.BlockSpec((tm, width), lambda i: (i, 0))


def _const_spec(shape):
    zeros = (0,) * len(shape)
    return pl.BlockSpec(tuple(shape), lambda *_: zeros)


def _pool_delta(he, pos):
    out = []
    for gi, w in enumerate(WINDOWS):
        hg = he[:, gi * POOL_G:(gi + 1) * POOL_G]
        s = hg
        k = 1
        while k < w:
            s = s + pltpu.roll(s, k, 0)
            k *= 2
        cnt = jnp.maximum(jnp.minimum(pos + 1, w), 1).astype(F32)
        out.append(s / cnt - hg)
    return out


def _load_with_halo_before(x_ref, i, tm):
    r0 = pl.multiple_of(i * tm, tm)
    hs = pl.multiple_of(jnp.maximum(i * tm - HALO, 0), 8)
    xh = jnp.where(i > 0, x_ref[pl.ds(hs, HALO), :], 0.0)
    xt = x_ref[pl.ds(r0, tm), :]
    return xt, jnp.concatenate([xh, xt], axis=0)


def _mixa_fwd_job(x, pre_g, pool_w, pool_scale, post_g):
    s_len = x.shape[0]

    def fn(i, kc_ref, ins, outs):
        x_ref, pg_ref, w_ref, sc_ref, qg_ref = ins
        y_ref, x1_ref = outs
        xt, xe = _load_with_halo_before(x_ref, i, TM)
        he = _rms_fwd(xe, pg_ref[0:1, :])
        pos = i * TM - HALO + lax.broadcasted_iota(jnp.int32, (TM + HALO, 1), 0)
        ds = _pool_delta(he, pos)
        ys = [_dot(ds[gi][HALO:, :].astype(BF), w_ref[gi]) for gi in range(len(WINDOWS))]
        y = jnp.concatenate(ys, axis=1) * sc_ref[...]
        y_ref[...] = y
        x1_ref[...] = xt + _rms_fwd(y, qg_ref[0:1, :])

    def whole(a):
        zeros = (0,) * a.ndim
        return (a, a.shape, lambda j, kc_ref: zeros, True)

    rows = lambda j, kc_ref: (j, 0)
    return Job(s_len // TM, [whole(a) for a in (x, pre_g, pool_w, pool_scale, post_g)],
               [(_sds((s_len, D)), (TM, D), rows), (_sds((s_len, D)), (TM, D), rows)], fn)


def _mixa_bwd(dx1, x, y, pre_g, pool_w, pool_scale, post_g, rider=None):
    s_len = x.shape[0]
    n = s_len // TM
    ng = len(WINDOWS)

    def body(dx_ref, x_ref, y_ref, pg_ref, w_ref, sc_ref, qg_ref,
             dx0_ref, dw_ref, dsc_ref, dqg_ref, dpg_ref, wacc):
        i = pl.program_id(0)

        @pl.when(i == 0)
        def _():
            wacc[...] = jnp.zeros_like(wacc)
            dsc_ref[...] = jnp.zeros_like(dsc_ref)
            dqg_ref[...] = jnp.zeros_like(dqg_ref)
            dpg_ref[...] = jnp.zeros_like(dpg_ref)

        r0 = pl.multiple_of(i * TM, TM)
        xt, xe = _load_with_halo_before(x_ref, i, TM)
        he = _rms_fwd(xe, pg_ref[0:1, :])
        pos_b = i * TM - HALO + lax.broadcasted_iota(jnp.int32, (TM + HALO, 1), 0)
        ds = _pool_delta(he, pos_b)

        last = i == n - 1
        a0 = pl.multiple_of(jnp.minimum(i * TM + TM, s_len - HALO), 8)
        ye = jnp.concatenate([y_ref[pl.ds(r0, TM), :], y_ref[pl.ds(a0, HALO), :]], axis=0)
        dt = dx_ref[pl.ds(r0, TM), :]
        de = jnp.concatenate([dt, jnp.where(last, 0.0, dx_ref[pl.ds(a0, HALO), :])], axis=0)
        dye, prod = _rms_bwd(ye, qg_ref[0:1, :], de)
        dqg_ref[...] += _rowsum(prod[:TM, :])
        dys = dye * sc_ref[...]
        pos_a = i * TM + lax.broadcasted_iota(jnp.int32, (TM + HALO, 1), 0)

        dhs, dscs = [], []
        for gi, w in enumerate(WINDOWS):
            sl = slice(gi * POOL_G, (gi + 1) * POOL_G)
            wg = w_ref[gi]
            dys_g = dys[:, sl].astype(BF)
            d_g = ds[gi][HALO:, :].astype(BF)
            ypre = _dot(d_g, wg)
            dscs.append(_rowsum(dye[:TM, sl] * ypre))
            wacc[gi] += _dot_tn(d_g, dys_g[:TM, :])
            dd = _dot_nt(dys_g, wg)
            cnt = jnp.minimum(pos_a + 1, w).astype(F32)
            a = dd / cnt
            k = 1
            while k < w:
                a = a + pltpu.roll(a, TM + HALO - k, 0)
                k *= 2
            dhs.append(a[:TM, :] - dd[:TM, :])
        dsc_ref[...] += jnp.concatenate(dscs, axis=1)
        dh = jnp.concatenate(dhs, axis=1)
        dxp, prod2 = _rms_bwd(xt, pg_ref[0:1, :], dh)
        dpg_ref[...] += _rowsum(prod2)
        dx0_ref[...] = dt + dxp

        @pl.when(last)
        def _():
            dw_ref[...] = wacc[...].astype(BF)

    return _call(
        body, name="mixa_bwd", grid=(n,), in_specs=[VSPEC] * 7,
        out_specs=[_row_spec(TM), _const_spec((ng, POOL_G, POOL_G)), _const_spec((1, D)),
                   _const_spec((1, D)), _const_spec((1, D))],
        out_shape=[_sds((s_len, D)), _sds((ng, POOL_G, POOL_G), BF), _sds((1, D)), _sds((1, D)), _sds((1, D))],
        scratch_shapes=[pltpu.VMEM((ng, POOL_G, POOL_G), F32)],
        args=[dx1, x, y, pre_g, pool_w, pool_scale, post_g], rider=rider)


def _ple_math(layer, x, p_blk, g_ref, wg_ref, wp_ref, qg_ref):
    r = _rms_fwd(x, g_ref[layer:layer + 1, :]).astype(BF)
    z = _dot(r, wg_ref[...])
    pe = _dot(p_blk.astype(BF), wp_ref[...])
    return z, pe, x + _rms_fwd(pe * _sigmoid(z), qg_ref[layer:layer + 1, :])


def _ple_bwd_math(layer, dx, x, z, pe, p_blk, g_ref, wg_ref, qg_ref):
    gate = _sigmoid(z)
    de, prod = _rms_bwd(pe * gate, qg_ref[layer:layer + 1, :], dx)
    dpe = (de * gate).astype(BF)
    dz = (de * pe * gate * (1.0 - gate)).astype(BF)
    dwp = _dot_tn(p_blk.astype(BF), dpe)
    g = g_ref[layer:layer + 1, :]
    dwg = _dot_tn(_rms_fwd(x, g).astype(BF), dz)
    dxp, prod2 = _rms_bwd(x, g, _dot_nt(dz, wg_ref[...]))
    return dx + dxp, dwp, dwg, _rowsum(prod2), _rowsum(prod)


def _ffn_fwd(layer, x1, pre_g, wgu, wd, post_g, rider=None, head=None):
    s_len = x1.shape[0]

    def body(*refs):
        if head:
            (x_ref, pg_ref, wgu_ref, wd_ref, qg_ref, p_ref, eg_ref, wg_ref, wp_ref, eq_ref, t_ref,
             f_ref, g_ref, u_ref, dx_ref, dwg_ref, dwp_ref, deg_ref, deq_ref, lv_ref, gacc, pacc) = refs
        else:
            x_ref, pg_ref, wgu_ref, wd_ref, qg_ref, f_ref, x2_ref, g_ref, u_ref = refs
        x = x_ref[...]
        h = _rms_fwd(x, pg_ref[layer:layer + 1, :]).astype(BF)
        f = jnp.zeros((TM, D), F32)
        for c in range(FF // FF_HALF):
            cols = slice(c * FF_HALF, (c + 1) * FF_HALF)
            g = _dot(h, wgu_ref[0, :, cols])
            u = _dot(h, wgu_ref[1, :, cols])
            g_ref[:, cols] = g.astype(BF)
            u_ref[:, cols] = u.astype(BF)
            act = g * _sigmoid(g) * u
            f = f + _dot(act.astype(BF), wd_ref[cols, :])
        f_ref[...] = f
        x2 = x + _rms_fwd(f, qg_ref[layer:layer + 1, :])
        if not head:
            x2_ref[...] = x2
        else:
            step = pl.program_id(0)

            @pl.when(step == 0)
            def _():
                for ref in (lv_ref, deg_ref, deq_ref, gacc, pacc):
                    ref[...] = jnp.zeros_like(ref)
            p_blk = p_ref[...]
            z, pe, x3 = _ple_math(layer, x2, p_blk, eg_ref, wg_ref, wp_ref, eq_ref)
            err = x3 - t_ref[...]
            lv_ref[...] += _rowsum(err * err)
            dx, dwp, dwg, deg, deq = _ple_bwd_math(layer, err * (1.0 / D), x2, z, pe, p_blk, eg_ref, wg_ref, eq_ref)
            dx_ref[...] = dx
            pacc[...] += dwp
            gacc[...] += dwg
            deg_ref[...] += deg
            deq_ref[...] += deq

            @pl.when(step == s_len // TM - 1)
            def _():
                dwg_ref[...] = gacc[...].astype(BF)
                dwp_ref[...] = pacc[...].astype(BF)

    in_specs = [_row_spec(TM), VSPEC, VSPEC, VSPEC, VSPEC]
    args = [x1, pre_g, wgu, wd, post_g]
    out_specs = [_row_spec(TM), _row_spec(TM), _row_spec(TM, FF), _row_spec(TM, FF)]
    out_shape = [_sds((s_len, D)), _sds((s_len, D)), _sds((s_len, FF), BF), _sds((s_len, FF), BF)]
    scratch = []
    if head:
        del out_specs[1], out_shape[1]
        p, ple_g, w_gate, w_proj, ple_post_g, target = head
        in_specs += [pl.BlockSpec((None, TM, PLE), lambda i: (layer, i, 0)), VSPEC, VSPEC, VSPEC, VSPEC, _row_spec(TM)]
        args += [p, ple_g, w_gate, w_proj, ple_post_g, target]
        out_specs += [_row_spec(TM), _const_spec((D, D)), _const_spec((PLE, D))] + [_const_spec((1, D))] * 3
        out_shape += [_sds((s_len, D)), _sds((D, D), BF), _sds((PLE, D), BF)] + [_sds((1, D))] * 3
        scratch = [pltpu.VMEM((D, D), F32), pltpu.VMEM((PLE, D), F32)]
    return _call(body, name=f"ffn_fwd{layer}", grid=(s_len // TM,), in_specs=in_specs, out_specs=out_specs,
                 out_shape=out_shape, args=args, scratch_shapes=scratch, rider=rider)


GU_PIECE = 128
DN_PIECE = 64
DN_SLOT = FF // N_CHIPS
HALF_D = D // 2
CHUNK_STRIDE = 6
CHUNK_START = (1, 7, 4, 10)


def _ffn_bwd(layer, dx2, x1, f, g_pre, u_pre, pre_g, wgu, wd, post_g, kc, rider=None):
    s_len = x1.shape[0]
    tm = TM_FFN_BWD
    n = s_len // tm
    nc = FF // FF_CHUNK
    n_gu, n_dn = FF_CHUNK // GU_PIECE, FF_CHUNK // DN_PIECE
    n_pieces = 2 * n_gu + n_dn
    n_blk = FF_HALF // GU_PIECE

    def edge_rows(c, i, kc_ref):
        return (jnp.where((c == 0) | (c == nc - 1), i, n - 1), 0)

    def chunk_at(c, kc_ref):
        k = kc_ref[0]
        start = jnp.where(k == 0, CHUNK_START[0], jnp.where(k == 1, CHUNK_START[1],
                                                            jnp.where(k == 2, CHUNK_START[2], CHUNK_START[3])))
        return ((c + start) * CHUNK_STRIDE) % nc

    def exchange(kc_ref, c, accg, accu, accd, own_gu_ref, land_gu_ref, own_dn_ref, land_dn_ref,
                 pl_gu, pl_dn, sib_gu, sib_dn, mine_gu, mine_dn, sum_gu, sum_dn,
                 psend, precv, ssend, lsem, rrecv):
        x, y, core = lax.axis_index("x"), lax.axis_index("y"), lax.axis_index("c")
        lower = core == 0

        def pair_copy(cc, part):
            p = cc % 2
            src, dst = ((sib_gu, pl_gu), (sib_dn, pl_dn))[part]
            return pltpu.make_async_remote_copy(src.at[p], dst.at[cc], psend.at[p, part], precv.at[cc, part],
                                                device_id=(x, y, 1 - core), device_id_type=MESH)

        def scatter(cc, wait):
            p = cc % 2
            hidden = chunk_at(cc, kc_ref) * FF_CHUNK

            assert n_gu == 2
            k0, k1 = hidden // FF_HALF, (hidden + GU_PIECE) // FF_HALF
            blk = (hidden - k0 * FF_HALF) // GU_PIECE
            for gu in range(2):
                @pl.when(k0 == k1)
                def _():
                    piece(p, wait, 2 * gu, sum_gu.at[p, gu], k0 + 2 * gu, 0, (pl.ds(blk, 2),))

                @pl.when(k0 != k1)
                def _():
                    piece(p, wait, 2 * gu, sum_gu.at[p, gu, 0], k0 + 2 * gu, 0, (blk,))
                    piece(p, wait, 2 * gu + 1, sum_gu.at[p, gu, 1], k1 + 2 * gu, 0, (0,))

            kd = hidden // DN_SLOT
            off = pl.multiple_of(hidden - kd * DN_SLOT, DN_PIECE)
            m = jnp.minimum((DN_SLOT - off) // DN_PIECE, n_dn)
            for mm in range(1, n_dn + 1):
                @pl.when(m == mm)
                def _():
                    rows = mm * DN_PIECE
                    piece(p, wait, 2 * n_gu, sum_dn.at[p, pl.ds(0, rows), :], kd, 1, (pl.ds(off, rows), slice(None)))
                    if mm < n_dn:
                        piece(p, wait, 2 * n_gu + 1, sum_dn.at[p, pl.ds(rows, FF_CHUNK - rows), :], kd + 1, 1,
                              (pl.ds(0, FF_CHUNK - rows), slice(None)))

        def piece(p, wait, pi, src, k, t, where):
            own_ref, land_ref = ((own_gu_ref, land_gu_ref), (own_dn_ref, land_dn_ref))[t]
            kx, ky = k // 2, k % 2
            fx, fy = (kx != x).astype(jnp.int32), (ky != y).astype(jnp.int32)
            local = (fx + fy) == 0
            j = jnp.maximum(fx + 2 * fy - 1, 0)

            @pl.when(local)
            def _():
                cp = pltpu.make_async_copy(src, own_ref.at[where], lsem.at[p, pi])
                if wait:
                    cp.wait()
                else:
                    cp.start()

            @pl.when(jnp.logical_not(local))
            def _():
                cp = pltpu.make_async_remote_copy(src, land_ref.at[(j,) + where], ssend.at[p, pi],
                                                  rrecv.at[t, j], device_id=(kx, ky, core), device_id_type=MESH)
                if wait:
                    cp.wait_send()
                else:
                    cp.start()

        def add_and_scatter(cc):
            p = cc % 2
            pair_copy(cc, 0).wait_recv()
            pair_copy(cc, 1).wait_recv()
            s_gu = (mine_gu[...] + pl_gu[cc].astype(F32)).astype(BF)
            for hc in range(n_gu):
                sum_gu[p, :, hc] = s_gu[:, :, hc * GU_PIECE:(hc + 1) * GU_PIECE]
            sum_dn[p] = (mine_dn[...] + pl_dn[cc].astype(F32)).astype(BF)
            scatter(cc, wait=False)

        @pl.when(c >= 1)
        def _():
            @pl.when(c >= 3)
            def _():
                scatter(c - 3, wait=True)
            add_and_scatter(c - 1)

        @pl.when(c >= 2)
        def _():
            pair_copy(c - 2, 0).wait_send()
            pair_copy(c - 2, 1).wait_send()

        p = c % 2
        my_rows = pl.ds(pl.multiple_of(core * HALF_D, HALF_D), HALF_D)
        sib_rows = pl.ds(pl.multiple_of((1 - core) * HALF_D, HALF_D), HALF_D)
        d_v = accd[...]
        sib_gu[p, 0] = accg[sib_rows, :].astype(BF)
        sib_gu[p, 1] = accu[sib_rows, :].astype(BF)
        sib_dn[p] = jnp.where(lower, d_v[:, HALF_D:], d_v[:, :HALF_D]).astype(BF)
        mine_gu[0] = accg[my_rows, :]
        mine_gu[1] = accu[my_rows, :]
        mine_dn[...] = jnp.where(lower, d_v[:, :HALF_D], d_v[:, HALF_D:])
        pair_copy(c, 0).start()
        pair_copy(c, 1).start()

        @pl.when(c == nc - 1)
        def _():
            scatter(nc - 3, wait=True)
            add_and_scatter(nc - 1)
            for cc in (nc - 2, nc - 1):
                pair_copy(cc, 0).wait_send()
                pair_copy(cc, 1).wait_send()
                scatter(cc, wait=True)
            for t, land_ref in enumerate((land_gu_ref, land_dn_ref)):
                for j in range(N_CHIPS - 1):
                    pltpu.make_async_remote_copy(land_ref.at[j], land_ref.at[j], ssend.at[0, 0], rrecv.at[t, j],
                                                 device_id=(x, y, core), device_id_type=MESH).wait_recv()

    def body(kc_ref, dx_ref, x_ref, f_ref, gp_ref, up_ref, pg_ref, wgu_ref, wd_ref, qg_ref,
             dx1_ref, dpg_ref, dqg_ref, own_gu_ref, land_gu_ref, own_dn_ref, land_dn_ref,
             h_s, df_s, dh_s, accg, accu, accd, *comm):
        c = pl.program_id(0)
        i = pl.program_id(1)
        rows = pl.ds(pl.multiple_of(i * tm, tm), tm)
        pg = pg_ref[layer:layer + 1, :]

        @pl.when((c == 0) & (i == 0))
        def _():
            dpg_ref[...] = jnp.zeros_like(dpg_ref)
            dqg_ref[...] = jnp.zeros_like(dqg_ref)

        @pl.when(c == 0)
        def _():
            h_s[rows, :] = _rms_fwd(x_ref[...], pg).astype(BF)
            df, prod = _rms_bwd(f_ref[...], qg_ref[layer:layer + 1, :], dx_ref[...])
            df_s[rows, :] = df.astype(BF)
            dqg_ref[...] += _rowsum(prod)

        @pl.when(i == 0)
        def _():
            accg[...] = jnp.zeros_like(accg)
            accu[...] = jnp.zeros_like(accu)
            accd[...] = jnp.zeros_like(accd)

        h = h_s[rows, :]
        df = df_s[rows, :]
        wg = wgu_ref[0]
        wu = wgu_ref[1]
        g = gp_ref[...].astype(F32)
        u = up_ref[...].astype(F32)
        sg = _sigmoid(g)
        a = g * sg
        dact = _dot_nt(df, wd_ref[...])
        accd[...] += _dot_tn((a * u).astype(BF), df)
        du = (dact * a).astype(BF)
        dg = (dact * u * (sg * (1.0 + g * (1.0 - sg)))).astype(BF)
        accg[...] += _dot_tn(h, dg)
        accu[...] += _dot_tn(h, du)
        dh = _dot_nt(dg, wg) + _dot_nt(du, wu)

        @pl.when(c == 0)
        def _():
            dh_s[rows, :] = dh

        @pl.when((c > 0) & (c < nc - 1))
        def _():
            dh_s[rows, :] += dh

        @pl.when(c == nc - 1)
        def _():
            dxp, prod = _rms_bwd(x_ref[...], pg, dh_s[rows, :] + dh)
            dpg_ref[...] += _rowsum(prod)
            dx1_ref[...] = dx_ref[...] + dxp

        @pl.when(i == n - 1)
        def _():
            exchange(kc_ref, c, accg, accu, accd, own_gu_ref, land_gu_ref, own_dn_ref, land_dn_ref, *comm)

    dma = pltpu.SemaphoreType.DMA
    return _call(
        body, name=f"ffn_bwd{layer}", grid=(nc, n),
        in_specs=[pl.BlockSpec((tm, D), edge_rows), pl.BlockSpec((tm, D), edge_rows),
                  pl.BlockSpec((tm, D), lambda c, i, kc_ref: (jnp.where(c == 0, i, n - 1), 0),
                               pipeline_mode=pl.Buffered(1)),
                  pl.BlockSpec((tm, FF_CHUNK), lambda c, i, kc_ref: (i, chunk_at(c, kc_ref))),
                  pl.BlockSpec((tm, FF_CHUNK), lambda c, i, kc_ref: (i, chunk_at(c, kc_ref))),
                  VSPEC,
                  pl.BlockSpec((2, D, FF_CHUNK), lambda c, i, kc_ref: (0, 0, chunk_at(c, kc_ref))),
                  pl.BlockSpec((FF_CHUNK, D), lambda c, i, kc_ref: (chunk_at(c, kc_ref), 0)),
                  VSPEC],
        out_specs=[pl.BlockSpec((tm, D), lambda c, i, kc_ref: (jnp.where(c == nc - 1, i, 0), 0)),
                   _const_spec((1, D)), _const_spec((1, D)), ANYSPEC, ANYSPEC, ANYSPEC, ANYSPEC],
        out_shape=[_sds((s_len, D)), _sds((1, D)), _sds((1, D)),
                   _sds((n_blk, HALF_D, GU_PIECE), BF), _sds((N_CHIPS - 1, n_blk, HALF_D, GU_PIECE), BF),
                   _sds((DN_SLOT, HALF_D), BF), _sds((N_CHIPS - 1, DN_SLOT, HALF_D), BF)],
        scratch_shapes=[pltpu.VMEM((s_len, D), BF), pltpu.VMEM((s_len, D), BF), pltpu.VMEM((s_len, D), F32),
                        pltpu.VMEM((D, FF_CHUNK), F32), pltpu.VMEM((D, FF_CHUNK), F32),
                        pltpu.VMEM((FF_CHUNK, D), F32),
                        pltpu.VMEM((nc, 2, HALF_D, FF_CHUNK), BF), pltpu.VMEM((nc, FF_CHUNK, HALF_D), BF),
                        pltpu.VMEM((2, 2, HALF_D, FF_CHUNK), BF), pltpu.VMEM((2, FF_CHUNK, HALF_D), BF),
                        pltpu.VMEM((2, HALF_D, FF_CHUNK), F32), pltpu.VMEM((FF_CHUNK, HALF_D), F32),
                        pltpu.VMEM((2, 2, n_gu, HALF_D, GU_PIECE), BF), pltpu.VMEM((2, FF_CHUNK, HALF_D), BF),
                        dma((2, 2)), dma((nc, 2)), dma((2, n_pieces)), dma((2, n_pieces)), dma((2, N_CHIPS - 1))],
        args=[dx2, x1, f, g_pre, u_pre, pre_g, wgu, wd, post_g], rider=rider, prefetch=kc)


def _ple_fwd(layer, x2, p, ple_g, w_gate, w_proj, post_g, qkv=None, rider=None):
    s_len = x2.shape[0]

    def body(*refs):
        if qkv:
            (x_ref, p_ref, g_ref, wg_ref, wp_ref, qg_ref, ng_ref, kg_ref, wq_ref, wkv_ref,
             z_ref, pe_ref, x3_ref, q_ref, kv_ref) = refs
        else:
            x_ref, p_ref, g_ref, wg_ref, wp_ref, qg_ref, z_ref, pe_ref, x3_ref = refs
        z, pe, x3 = _ple_math(layer, x_ref[...], p_ref[...], g_ref, wg_ref, wp_ref, qg_ref)
        z_ref[...] = z
        pe_ref[...] = pe
        x3_ref[...] = x3
        if qkv:
            q_ref[...] = _dot(_rms_fwd(x3, ng_ref[layer + 1:layer + 2, :]).astype(BF), wq_ref[...]).astype(BF)
            kv_ref[...] = _dot(_rms_fwd(x3, kg_ref[...]).astype(BF), wkv_ref[...]).astype(BF)

    p_spec = pl.BlockSpec((None, TM, PLE), lambda i: (layer, i, 0))
    in_specs = [_row_spec(TM), p_spec, VSPEC, VSPEC, VSPEC, VSPEC]
    args = [x2, p, ple_g, w_gate, w_proj, post_g]
    out_specs = [_row_spec(TM), _row_spec(TM), _row_spec(TM)]
    out_shape = [_sds((s_len, D))] * 3
    if qkv:
        in_specs += [VSPEC] * 4
        args += list(qkv)
        out_specs += [_row_spec(TM), _row_spec(TM, 2 * KVD)]
        out_shape += [_sds((s_len, D), BF), _sds((s_len, 2 * KVD), BF)]
    return _call(body, name=f"ple_fwd{layer}", grid=(s_len // TM,), in_specs=in_specs, out_specs=out_specs,
                 out_shape=out_shape, args=args, rider=rider)


def _ple_bwd(layer, dx3, x2, z, pe, p, ple_g, w_gate, post_g, rider=None):
    s_len = x2.shape[0]
    n = s_len // TM

    def body(dx_ref, x_ref, z_ref, pe_ref, p_ref, g_ref, wg_ref, qg_ref,
             dx2_ref, dwg_ref, dwp_ref, dg_ref, dqg_ref, gacc, pacc):
        i = pl.program_id(0)

        @pl.when(i == 0)
        def _():
            gacc[...] = jnp.zeros_like(gacc)
            pacc[...] = jnp.zeros_like(pacc)
            dg_ref[...] = jnp.zeros_like(dg_ref)
            dqg_ref[...] = jnp.zeros_like(dqg_ref)

        dx2, dwp, dwg, dg, dqg = _ple_bwd_math(layer, dx_ref[...], x_ref[...], z_ref[...], pe_ref[...], p_ref[...],
                                                g_ref, wg_ref, qg_ref)
        dx2_ref[...] = dx2
        pacc[...] += dwp
        gacc[...] += dwg
        dg_ref[...] += dg
        dqg_ref[...] += dqg

        @pl.when(i == n - 1)
        def _():
            dwg_ref[...] = gacc[...].astype(BF)
            dwp_ref[...] = pacc[...].astype(BF)

    p_spec = pl.BlockSpec((None, TM, PLE), lambda i: (layer, i, 0))
    return _call(
        body, name=f"ple_bwd{layer}", grid=(n,),
        in_specs=[_row_spec(TM), _row_spec(TM), _row_spec(TM), _row_spec(TM), p_spec, VSPEC, VSPEC, VSPEC],
        out_specs=[_row_spec(TM), _const_spec((D, D)), _const_spec((PLE, D)), _const_spec((1, D)), _const_spec((1, D))],
        out_shape=[_sds((s_len, D)), _sds((D, D), BF), _sds((PLE, D), BF), _sds((1, D)), _sds((1, D))],
        scratch_shapes=[pltpu.VMEM((D, D), F32), pltpu.VMEM((PLE, D), F32)],
        args=[dx3, x2, z, pe, p, ple_g, w_gate, post_g], rider=rider)


def _qkv_bwd(dq, dkv, x3, dx4, q_g, kv_g, w_q, w_kv):
    s_len = x3.shape[0]
    n = s_len // TM

    def body(dq_ref, dkv_ref, x_ref, dx_ref, qg_ref, kg_ref, wq_ref, wkv_ref,
             dx3_ref, dwq_ref, dwkv_ref, dqg_ref, dkg_ref, qacc, kacc):
        i = pl.program_id(0)

        @pl.when(i == 0)
        def _():
            qacc[...] = jnp.zeros_like(qacc)
            kacc[...] = jnp.zeros_like(kacc)
            dqg_ref[...] = jnp.zeros_like(dqg_ref)
            dkg_ref[...] = jnp.zeros_like(dkg_ref)

        x = x_ref[...]
        qg = qg_ref[1:2, :]
        kg = kg_ref[...]
        dq_v = dq_ref[...]
        dkv_v = dkv_ref[...].astype(BF)
        qacc[...] += _dot_tn(_rms_fwd(x, qg).astype(BF), dq_v)
        kacc[...] += _dot_tn(_rms_fwd(x, kg).astype(BF), dkv_v)
        dxq, prod_q = _rms_bwd(x, qg, _dot_nt(dq_v, wq_ref[...]))
        dxk, prod_k = _rms_bwd(x, kg, _dot_nt(dkv_v, wkv_ref[...]))
        dqg_ref[...] += _rowsum(prod_q)
        dkg_ref[...] += _rowsum(prod_k)
        dx3_ref[...] = dx_ref[...] + dxq + dxk

        @pl.when(i == n - 1)
        def _():
            dwq_ref[...] = qacc[...].astype(BF)
            dwkv_ref[...] = kacc[...].astype(BF)

    outs, _ = _call(
        body, name="qkv_bwd", grid=(n,),
        in_specs=[_row_spec(TM), _row_spec(TM, 2 * KVD), _row_spec(TM), _row_spec(TM), VSPEC, VSPEC, VSPEC, VSPEC],
        out_specs=[_row_spec(TM), _const_spec((D, D)), _const_spec((D, 2 * KVD)),
                   _const_spec((1, D)), _const_spec((1, D))],
        out_shape=[_sds((s_len, D)), _sds((D, D), BF), _sds((D, 2 * KVD), BF), _sds((1, D)), _sds((1, D))],
        scratch_shapes=[pltpu.VMEM((D, D), F32), pltpu.VMEM((D, 2 * KVD), F32)],
        args=[dq, dkv, x3, dx4, q_g, kv_g, w_q, w_kv])
    return outs


def _attn_group(i, q, kvw, sink_ref, g):
    rows = GQA * BLK
    heads = [GQA * g + j for j in range(GQA)]
    off = jnp.where(i > 0, BLK, 0)
    row = lax.broadcasted_iota(jnp.int32, (rows, 2 * BLK), 0)
    rel = (row % BLK) - lax.broadcasted_iota(jnp.int32, (rows, 2 * BLK), 1) + off
    valid = (rel >= 0) & (rel < BLK)
    head_of_row = lax.broadcasted_iota(jnp.int32, (rows, 1), 0) // BLK
    slope = jnp.zeros((rows, 1), F32)
    sink = jnp.zeros((rows, 1), F32)
    for j, h in enumerate(heads):
        slope = jnp.where(head_of_row == j, SLOPES[h], slope)
        sink = jnp.where(head_of_row == j, sink_ref[0, h], sink)
    qs = jnp.concatenate([q[:, h * HEAD_DIM:(h + 1) * HEAD_DIM] for h in heads], axis=0)
    k = kvw[:, g * HEAD_DIM:(g + 1) * HEAD_DIM]
    v = kvw[:, KVD + g * HEAD_DIM:KVD + (g + 1) * HEAD_DIM]
    s = _dot_nt(qs, k) * ATT_SCALE - slope * rel.astype(F32)
    s = jnp.where(valid, s, NEG_INF)
    m = jnp.maximum(jnp.max(s, axis=-1, keepdims=True), sink)
    e = jnp.exp(s - m)
    es = jnp.exp(sink - m)
    inv = 1.0 / (jnp.sum(e, axis=-1, keepdims=True) + es)
    return e * inv, es * inv, qs, k, v


def _unstack_heads(stacked):
    return [stacked[j * BLK:(j + 1) * BLK, :] for j in range(GQA)]


def _kv_window(kv_ref, i):
    ks = pl.multiple_of(jnp.maximum(i * BLK - BLK, 0), BLK)
    return ks, kv_ref[pl.ds(ks, 2 * BLK), :]


def _attn_fwd(q, kv, sinks, x3, w_o, post_g, rider=None):
    s_len = q.shape[0]

    def body(q_ref, kv_ref, sk_ref, x_ref, wo_ref, g_ref, a_ref, y_ref, x4_ref):
        i = pl.program_id(0)
        _, kvw = _kv_window(kv_ref, i)
        q = q_ref[...]
        outs = []
        for g in range(N_KV_HEADS):
            p, _, _, _, v = _attn_group(i, q, kvw, sk_ref, g)
            outs += _unstack_heads(_dot(p.astype(BF), v))
        attn = jnp.concatenate(outs, axis=1)
        a_ref[...] = attn
        y = _dot(attn.astype(BF), wo_ref[...])
        y_ref[...] = y
        x4_ref[...] = x_ref[...] + _rms_fwd(y, g_ref[1:2, :])

    return _call(body, name="attn_fwd", grid=(s_len // BLK,),
                 in_specs=[_row_spec(BLK), VSPEC, SSPEC, _row_spec(BLK), VSPEC, VSPEC],
                 out_specs=[_row_spec(BLK)] * 3, out_shape=[_sds((s_len, D))] * 3,
                 args=[q, kv, sinks, x3, w_o, post_g], rider=rider)


ATT_STEP_BLOCKS = 2


def _attn_bwd(dx4, y, attn, q, kv, sinks, w_o, post_g, rider=None):
    s_len = q.shape[0]
    rows = ATT_STEP_BLOCKS * BLK
    n = s_len // rows

    def body(dx_ref, y_ref, a_ref, q_ref, kv_ref, sk_ref, wo_ref, g_ref,
             dq_ref, dkv_ref, dwo_ref, dg_ref, dsk_ref, wacc):
        i = pl.program_id(0)

        @pl.when(i == 0)
        def _():
            dkv_ref[...] = jnp.zeros_like(dkv_ref)
            wacc[...] = jnp.zeros_like(wacc)
            dg_ref[...] = jnp.zeros_like(dg_ref)
            dsk_ref[...] = jnp.zeros_like(dsk_ref)

        dy, prod = _rms_bwd(y_ref[...], g_ref[1:2, :], dx_ref[...])
        dg_ref[...] += _rowsum(prod)
        dyb = dy.astype(BF)
        attn_all = a_ref[...]
        wacc[...] += _dot_tn(attn_all.astype(BF), dyb)
        d_o_all = _dot_nt(dyb, wo_ref[...])
        q_all = q_ref[...]
        lane = lax.broadcasted_iota(jnp.int32, (1, D), 1)
        dsk = jnp.zeros((1, D), F32)
        for sub in range(ATT_STEP_BLOCKS):
            blk = i * ATT_STEP_BLOCKS + sub
            sl = slice(sub * BLK, (sub + 1) * BLK)
            d_o, q = d_o_all[sl, :], q_all[sl, :]
            dod = d_o * attn_all[sl, :]
            ks, kvw = _kv_window(kv_ref, blk)
            dqs, dks, dvs = [], [], []
            for g in range(N_KV_HEADS):
                p, ps, qs, k, v = _attn_group(blk, q, kvw, sk_ref, g)
                cols = [slice((GQA * g + j) * HEAD_DIM, (GQA * g + j + 1) * HEAD_DIM) for j in range(GQA)]
                do_s = jnp.concatenate([d_o[:, c] for c in cols], axis=0).astype(BF)
                dsum = jnp.concatenate([jnp.sum(dod[:, c], axis=-1, keepdims=True) for c in cols], axis=0)
                dp = _dot_nt(do_s, v)
                dsb = (p * (dp - dsum) * ATT_SCALE).astype(BF)
                sink_part = ps * dsum
                for j in range(GQA):
                    dsk = dsk + jnp.where(lane == GQA * g + j, -_rowsum(sink_part[j * BLK:(j + 1) * BLK, :]), 0.0)
                dqs += _unstack_heads(_dot(dsb, k))
                dks.append(_dot_tn(dsb, qs))
                dvs.append(_dot_tn(p.astype(BF), do_s))
            dq_ref[sl, :] = jnp.concatenate(dqs, axis=1).astype(BF)
            dkv_ref[pl.ds(ks, 2 * BLK), :] += jnp.concatenate(dks + dvs, axis=1)
        dsk_ref[...] += dsk

        @pl.when(i == n - 1)
        def _():
            dwo_ref[...] = wacc[...].astype(BF)

    return _call(
        body, name="attn_bwd", grid=(n,),
        in_specs=[_row_spec(rows), _row_spec(rows), _row_spec(rows), _row_spec(rows), VSPEC, SSPEC, VSPEC, VSPEC],
        out_specs=[_row_spec(rows), _const_spec((s_len, 2 * KVD)), _const_spec((D, D)),
                   _const_spec((1, D)), _const_spec((1, D))],
        out_shape=[_sds((s_len, D), BF), _sds((s_len, 2 * KVD)), _sds((D, D), BF), _sds((1, D)), _sds((1, D))],
        scratch_shapes=[pltpu.VMEM((D, D), F32)],
        args=[dx4, y, attn, q, kv, sinks, w_o, post_g], rider=rider)


Big = collections.namedtuple("Big", "name src layer L A R C rb")


def _bigs():
    out = {"pool_w": Big("pool_w", "pool_w", None, 4, 4, POOL_G // N_CHIPS, POOL_G, 32)}
    for l in range(2):
        out[f"w_gu{l}"] = Big(f"w_gu{l}", "w_gu", l, 1, 2, D, FF_HALF, 256)
        out[f"w_down{l}"] = Big(f"w_down{l}", "w_down", l, 1, 4, FF // N_CHIPS, D, 352)
        out[f"w_ple_gate{l}"] = Big(f"w_ple_gate{l}", "w_ple_gate", l, 1, 4, D // N_CHIPS, D, 128)
        out[f"w_ple_proj{l}"] = Big(f"w_ple_proj{l}", "w_ple_proj", l, 1, 1, PLE, D // N_CHIPS, 128)
    out["w_q"] = Big("w_q", "w_q", None, 1, 4, D // N_CHIPS, D, 128)
    out["w_o"] = Big("w_o", "w_o", None, 1, 4, D // N_CHIPS, D, 128)
    out["w_kv"] = Big("w_kv", "w_kv", None, 1, 4, D // N_CHIPS, 2 * KVD, 128)
    return out


BIGS = _bigs()
POOL_SCALE = Big("pool_scale", "pool_scale", None, 1, 1, 1, D // N_CHIPS, 1)
BIG_SOURCES = ("w_gu", "w_down", "w_ple_gate", "w_ple_proj", "w_q", "w_o", "w_kv", "pool_w")


def _ncb(t):
    return N_CHIPS // t.A


def _full_shape(t, rows=None):
    return (t.L, t.A, t.R if rows is None else rows, _ncb(t) * t.C)


def _slot_index(t, k):
    return k // _ncb(t), k % _ncb(t)


def _slot(ref, t, k, row0, rows):
    a, cb = _slot_index(t, k)
    return ref.at[:, a, pl.ds(row0, rows), pl.ds(pl.multiple_of(cb * t.C, 128), t.C)]


def _place_job(t, w, out_dtype=BF):
    nb = next((nb for nb in (8, 4, 2, 1) if t.R % (16 * nb) == 0), 1) if t.L == 1 else 1
    rb = t.R // nb

    def fn(j, kc_ref, ins, outs):
        outs[0][...] = ins[0][...].astype(out_dtype)

    def in_map(j, kc_ref):
        return (j // nb if t.layer is None else t.layer, j % nb, 0)

    def out_map(j, kc_ref):
        a, cb = _slot_index(t, kc_ref[0])
        return (j // nb, a, j % nb, cb)

    return Job(t.L * nb, [(w, (None, rb, t.C), in_map)],
               [(_sds(_full_shape(t), out_dtype), (None, None, rb, t.C), out_map)], fn)


def _mesh_position():
    x, y, c = lax.axis_index("x"), lax.axis_index("y"), lax.axis_index("c")
    chips = [(1 - x, y), (x, 1 - y), (1 - x, 1 - y)]
    return x, y, c, chips


DIRECT_BELOW = 1024


def _gather_rider(parts, fulls):
    nt = len(parts)
    TO_X, TO_Y, FWD_X, FWD_Y, SIB_X, SIB_Y, SIB_D = range(7)

    def rows_of(ti, core):
        t, r0, r1 = parts[ti]
        h = (r1 - r0) // 2
        return r0 + core * h, h

    def copy(outs, sems, kind, ti, k_src, row0, rows, dev):
        region = _slot(outs[ti], parts[ti][0], k_src, row0, rows)
        return pltpu.make_async_remote_copy(region, region, sems[0].at[ti, kind], sems[1].at[ti, kind],
                                            device_id=dev, device_id_type=MESH)

    def plan(outs, sems):
        x, y, c, _ = _mesh_position()
        me, kx, ky, kd = 2 * x + y, 2 * (1 - x) + y, 2 * x + (1 - y), 2 * (1 - x) + (1 - y)
        dev_x, dev_y, dev_d, sib = (1 - x, y, c), (x, 1 - y, c), (1 - x, 1 - y, c), (x, y, 1 - c)

        def whole(ti):
            return 0, parts[ti][0].R

        def mk(kind, k_send, k_recv, dev, send_rows, recv_rows):
            def build(ti, side):
                k_src = k_send if side == "s" else k_recv
                row0, rows = (send_rows if side == "s" else recv_rows)(ti)
                return copy(outs, sems, kind, ti, k_src, row0, rows, dev)
            return build

        def first_half(core):
            return lambda ti: (rows_of(ti, core)[0], rows_of(ti, core)[1] // 2)

        def second_half(core):
            return lambda ti: (rows_of(ti, core)[0] + rows_of(ti, core)[1] // 2, rows_of(ti, core)[1] // 2)

        mine = lambda ti: rows_of(ti, c)
        theirs = lambda ti: rows_of(ti, 1 - c)
        split = {
            TO_X: mk(TO_X, me, kx, dev_x, mine, mine),
            TO_Y: mk(TO_Y, me, ky, dev_y, mine, mine),
            FWD_X: mk(FWD_X, ky, kd, dev_x, first_half(c), first_half(c)),
            FWD_Y: mk(FWD_Y, kx, kd, dev_y, second_half(c), second_half(c)),
            SIB_X: mk(SIB_X, kx, kx, sib, mine, theirs),
            SIB_Y: mk(SIB_Y, ky, ky, sib, mine, theirs),
            SIB_D: mk(SIB_D, kd, kd, sib, mine, theirs),
        }
        direct = {
            TO_X: mk(TO_X, me, kx, dev_x, whole, whole),
            TO_Y: mk(TO_Y, me, ky, dev_y, whole, whole),
            FWD_X: mk(FWD_X, me, kd, dev_d, whole, whole),
        }
        return split, direct

    is_split = [t.L * t.R * t.C >= DIRECT_BELOW for t, _, _ in parts]
    assert all(s or (r0, r1) == (0, t.R) for s, (t, r0, r1) in zip(is_split, parts))

    def start(ins, outs, sems):
        split, direct = plan(outs, sems)
        for ti in range(nt):
            kinds = split if is_split[ti] else direct
            kinds[TO_X](ti, "s").start()
            kinds[TO_Y](ti, "s").start()
            if not is_split[ti]:
                kinds[FWD_X](ti, "s").start()

    def mid(ins, outs, sems):
        split, _ = plan(outs, sems)
        for ti in range(nt):
            if is_split[ti]:
                split[TO_Y](ti, "r").wait_recv()
                split[FWD_X](ti, "s").start()
                split[SIB_Y](ti, "s").start()
        for ti in range(nt):
            if is_split[ti]:
                split[TO_X](ti, "r").wait_recv()
                split[FWD_Y](ti, "s").start()
                split[SIB_X](ti, "s").start()

    def finish(ins, outs, sems):
        split, direct = plan(outs, sems)
        for ti in range(nt):
            if is_split[ti]:
                split[FWD_X](ti, "r").wait_recv()
                split[FWD_Y](ti, "r").wait_recv()
                split[SIB_D](ti, "s").start()
            else:
                for kind in (TO_X, TO_Y, FWD_X):
                    direct[kind](ti, "r").wait_recv()
        for ti in range(nt):
            if is_split[ti]:
                for kind in (SIB_X, SIB_Y, SIB_D):
                    split[kind](ti, "r").wait_recv()
        for ti in range(nt):
            kinds = split if is_split[ti] else direct
            for kind in kinds:
                kinds[kind](ti, "s").wait_send()

    sems = pltpu.SemaphoreType.DMA((nt, 7))
    return Rider(list(fulls), [_sds(a.shape, a.dtype) for a in fulls], {i: i for i in range(nt)},
                 [sems, sems], start, mid, finish)


def _pair_exchange_rider(specs, grads):
    nt = len(specs)

    def copy(ins, outs, sems, ti, c, sibling):
        half = specs[ti].R // 2
        return pltpu.make_async_remote_copy(ins[ti].at[:, :, pl.ds((1 - c) * half, half), :], outs[ti],
                                            sems[0].at[ti], sems[1].at[ti], device_id=sibling, device_id_type=MESH)

    def start(ins, outs, sems):
        x, y, c, _ = _mesh_position()
        for ti in range(nt):
            copy(ins, outs, sems, ti, c, (x, y, 1 - c)).start()

    def finish(ins, outs, sems):
        x, y, c, _ = _mesh_position()
        for ti in range(nt):
            copy(ins, outs, sems, ti, c, (x, y, 1 - c)).wait()

    sems = pltpu.SemaphoreType.DMA((nt,))
    return Rider(list(grads), [_sds(_full_shape(t, t.R // 2), BF) for t in specs], {}, [sems, sems], start, None, finish)


def _pair_sum_job(t, g, land):
    assert t.L == 1
    half = t.R // 2
    nj = half // t.rb
    block = (None, t.A, t.rb, _ncb(t) * t.C)

    def fn(j, kc_ref, ins, outs):
        outs[0][...] = (ins[0][...].astype(F32) + ins[1][...].astype(F32)).astype(BF)

    return Job(nj,
               [(g, block, lambda j, kc_ref: (0, 0, kc_ref[1] * nj + j, 0)),
                (land, block, lambda j, kc_ref: (0, 0, j, 0))],
               [(_sds(_full_shape(t, half), BF), block, lambda j, kc_ref: (0, 0, j, 0))], fn)


def _scatter_rider(specs, sums):
    nt = len(specs)

    def copy(ins, outs, sems, ti, j, chip, c):
        t = specs[ti]
        cx, cy = chip
        return pltpu.make_async_remote_copy(_slot(ins[ti], t, 2 * cx + cy, 0, t.R // 2), outs[ti].at[j],
                                            sems[0].at[ti, j], sems[1].at[ti, j],
                                            device_id=(cx, cy, c), device_id_type=MESH)

    def start(ins, outs, sems):
        _, _, c, chips = _mesh_position()
        for j, chip in enumerate(chips):
            for ti in range(nt):
                copy(ins, outs, sems, ti, j, chip, c).start()

    def finish(ins, outs, sems):
        _, _, c, chips = _mesh_position()
        for j, chip in enumerate(chips):
            for ti in range(nt):
                copy(ins, outs, sems, ti, j, chip, c).wait()

    sems = pltpu.SemaphoreType.DMA((nt, N_CHIPS - 1))
    return Rider(list(sums), [_sds((N_CHIPS - 1, t.L, t.R // 2, t.C), BF) for t in specs], {}, [sems, sems],
                 start, None, finish)


def _chip_sum_job(ts, landed):
    t0 = ts[0]
    assert t0.L == 1
    half = t0.R // 2
    nj = half // t0.rb

    def local(j, li):
        return jnp.clip(j - li * nj, 0, nj - 1)

    ins = []
    for li, t in enumerate(ts):
        s, land = landed[t.name]

        def own_map(j, kc_ref, li=li, t=t):
            a, cb = _slot_index(t, kc_ref[0])
            return (0, a, local(j, li), cb)

        ins.append((s, (None, None, t.rb, t.C), own_map))
        ins.append((land, (N_CHIPS - 1, None, t.rb, t.C), lambda j, kc_ref, li=li: (0, 0, local(j, li), 0)))

    def fn(j, kc_ref, in_refs, outs):
        for li in range(len(ts)):
            @pl.when(j // nj == li)
            def _():
                acc = in_refs[2 * li][...].astype(F32)
                for k in range(N_CHIPS - 1):
                    acc = acc + in_refs[2 * li + 1][k].astype(F32)
                outs[0][...] = acc

    return Job(len(ts) * nj, ins,
               [(_sds((len(ts), t0.R, t0.C)), (None, t0.rb, t0.C),
                 lambda j, kc_ref: (j // nj, kc_ref[1] * nj + j % nj, 0))], fn)


def _adamw_job(rb, w, g, m, v):
    n_layers, r, c = w.shape
    nb = r // rb
    block = (None, rb, c)
    index = lambda j, kc_ref: (j // nb, j % nb, 0)

    def fn(j, kc_ref, ins, outs):
        g_v = ins[1][...]
        outs[0][...] = g_v
        outs[1][...], outs[2][...], outs[3][...] = _adamw_math(ins[0][...], g_v, ins[2][...], ins[3][...])

    return Job(n_layers * nb, [(a, block, index) for a in (w, g, m, v)],
               [(_sds(w.shape), block, index)] * 4, fn)


def _chip_sum_fused_job(ts, fused, by_cols):
    t0 = ts[0]
    own0 = fused[t0.name][0]
    if by_cols:
        rows, cols = own0.shape
    else:
        nb, rows, bw = own0.shape
        cols = nb * bw
    nj = rows // t0.rb

    def local(j, li):
        return jnp.clip(j - li * nj, 0, nj - 1)

    ins = []
    for li, t in enumerate(ts):
        own, land = fused[t.name]
        if by_cols:
            ins.append((own, (t.rb, cols), lambda j, kc_ref, li=li: (local(j, li), 0)))
            ins.append((land, (N_CHIPS - 1, t.rb, cols), lambda j, kc_ref, li=li: (0, local(j, li), 0)))
        else:
            ins.append((own, (nb, t.rb, bw), lambda j, kc_ref, li=li: (0, local(j, li), 0)))
            ins.append((land, (N_CHIPS - 1, nb, t.rb, bw), lambda j, kc_ref, li=li: (0, 0, local(j, li), 0)))

    def fn(j, kc_ref, in_refs, outs):
        for li in range(len(ts)):
            @pl.when(j // nj == li)
            def _():
                acc = in_refs[2 * li][...].astype(F32)
                for k in range(N_CHIPS - 1):
                    acc = acc + in_refs[2 * li + 1][k].astype(F32)
                outs[0][...] = acc if by_cols else jnp.concatenate([acc[b] for b in range(nb)], axis=1)

    def out_map(j, kc_ref):
        return (j // nj, j % nj, kc_ref[1]) if by_cols else (j // nj, kc_ref[1] * nj + j % nj, 0)

    return Job(len(ts) * nj, ins, [(_sds((len(ts), t0.R, t0.C)), (None, t0.rb, cols), out_map)], fn)


def _share_rider(halves, by_cols):
    nt = len(halves)

    def copy(outs, sems, ti, core, sibling):
        axis = 2 if by_cols[ti] else 1
        half = halves[ti].shape[axis] // 2
        piece = pl.ds(pl.multiple_of(core * half, 128 if by_cols[ti] else 8), half)
        part = outs[ti].at[:, :, piece] if by_cols[ti] else outs[ti].at[:, piece, :]
        return pltpu.make_async_remote_copy(part, part, sems[0].at[ti], sems[1].at[ti],
                                            device_id=sibling, device_id_type=MESH)

    def start(ins, outs, sems):
        x, y, c, _ = _mesh_position()
        for ti in range(nt):
            copy(outs, sems, ti, c, (x, y, 1 - c)).start()

    def finish(ins, outs, sems):
        x, y, c, _ = _mesh_position()
        for ti in range(nt):
            copy(outs, sems, ti, 1 - c, (x, y, 1 - c)).wait_recv()
        for ti in range(nt):
            copy(outs, sems, ti, c, (x, y, 1 - c)).wait_send()

    sems = pltpu.SemaphoreType.DMA((nt,))
    return Rider(list(halves), [_sds(a.shape, a.dtype) for a in halves], {i: i for i in range(nt)}, [sems, sems],
                 start, None, finish)


def _both(r1, r2):
    assert r1.mid is None and r2.mid is None
    ni, no, ns = len(r1.arrays), len(r1.out_shapes), len(r1.scratch)

    def split(fn1, fn2):
        def run(ins, outs, scr):
            fn1(ins[:ni], outs[:no], scr[:ns])
            fn2(ins[ni:], outs[no:], scr[ns:])
        return run

    aliases = dict(r1.aliases)
    aliases.update({ni + a: no + b for a, b in r2.aliases.items()})
    return Rider(r1.arrays + r2.arrays, r1.out_shapes + r2.out_shapes, aliases, r1.scratch + r2.scratch,
                 split(r1.start, r2.start), None, split(r1.finish, r2.finish))


def _adamw_math(w, g, m, v):
    m = B1 * m + (1.0 - B1) * g
    v = B2 * v + (1.0 - B2) * (g * g)
    delta = -LR * ((m / BC1) / (jnp.sqrt(v / BC2) + AEPS) + WD * w)
    return delta, m, v


GAIN_ROWS = {"pre_mix_g": 0, "post_mix_g": 2, "pre_ffn_g": 4, "post_ffn_g": 6, "ple_g": 8, "ple_post_g": 10}
ROW_KV_G, ROW_POOL_SCALE, ROW_SINKS, ROW_LOSS, PACK_ROWS = 12, 13, 14, 15, 16
SMALL_NAMES = tuple(GAIN_ROWS) + ("kv_g", "pool_scale", "sinks")


def _small_all_reduce(rows, dpool, rider=None):
    ng, pr = len(WINDOWS), POOL_G // N_CHIPS

    def body(*refs):
        row_refs = refs[:PACK_ROWS]
        dpool_ref, tot_ref, gpool_ref, pack, land, pland, send, recv, psend, precv = refs[PACK_ROWS:]
        x, y, c, _ = _mesh_position()
        me = 4 * x + 2 * y + c
        for r in range(PACK_ROWS):
            pack[r:r + 1, :] = row_refs[r][...]

        def shard_of(k):
            return dpool_ref.at[:, pl.ds(pl.multiple_of(k * pr, pr), pr), :]

        cps = []
        for j in range(1, N_DEV):
            px, py, pc = x ^ (j >> 2), y ^ ((j >> 1) & 1), c ^ (j & 1)
            cps.append(pltpu.make_async_remote_copy(pack, land.at[me], send.at[j], recv.at[j],
                                                    device_id=(px, py, pc), device_id_type=MESH))
            cps.append(pltpu.make_async_remote_copy(shard_of(2 * px + py), pland.at[me], psend.at[j], precv.at[j],
                                                    device_id=(px, py, pc), device_id_type=MESH))
        for cp in cps:
            cp.start()
        land[me] = pack[...]
        pland[me] = dpool_ref[:, pl.ds(pl.multiple_of((2 * x + y) * pr, pr), pr), :]
        for j in range(1, N_DEV):
            pltpu.make_async_remote_copy(pack, land.at[me ^ j], send.at[j], recv.at[j],
                                         device_id=(x, y, c), device_id_type=MESH).wait_recv()
            pltpu.make_async_remote_copy(shard_of(0), pland.at[me ^ j], psend.at[j], precv.at[j],
                                         device_id=(x, y, c), device_id_type=MESH).wait_recv()
        for cp in cps:
            cp.wait_send()
        tot = land[0]
        gp = pland[0].astype(F32)
        for d in range(1, N_DEV):
            tot = tot + land[d]
            gp = gp + pland[d].astype(F32)
        tot_ref[...] = tot
        gpool_ref[...] = gp

    sems = pltpu.SemaphoreType.DMA((N_DEV,))
    return _call(
        body, name="small_all_reduce", grid=(1,),
        in_specs=[VSPEC] * (PACK_ROWS + 1), out_specs=[VSPEC, VSPEC],
        out_shape=[_sds((PACK_ROWS, D)), _sds((ng, pr, POOL_G))],
        scratch_shapes=[pltpu.VMEM((PACK_ROWS, D), F32), pltpu.VMEM((N_DEV, PACK_ROWS, D), F32),
                        pltpu.VMEM((N_DEV, ng, pr, POOL_G), BF), sems, sems, sems, sems],
        args=[*rows, dpool], rider=rider)


def _small_adamw(tot, kc, small_w, small_m, small_v):
    names = SMALL_NAMES
    n = len(names)

    def body(*refs):
        tot_ref, kc_ref = refs[0], refs[1]
        w_refs = dict(zip(names, refs[2:2 + n]))
        m_refs = dict(zip(names, refs[2 + n:2 + 2 * n]))
        v_refs = dict(zip(names, refs[2 + 2 * n:2 + 3 * n]))
        loss_ref = refs[2 + 3 * n]
        out_refs = {nm: refs[3 + 3 * n + 4 * k: 7 + 3 * n + 4 * k] for k, nm in enumerate(names)}
        tot = tot_ref[...]
        loss_ref[...] = 0.5 * jnp.sum(tot[ROW_LOSS:ROW_LOSS + 1, :], axis=-1, keepdims=True) * (1.0 / D)

        def update(nm, g):
            g_ref, d_ref, nm_ref, nv_ref = out_refs[nm]
            g_ref[...] = g
            d_ref[...], nm_ref[...], nv_ref[...] = _adamw_math(w_refs[nm][...], g, m_refs[nm][...], v_refs[nm][...])

        for nm, r in GAIN_ROWS.items():
            update(nm, tot[r:r + 2, :])
        update("kv_g", tot[ROW_KV_G:ROW_KV_G + 1, :])
        k = kc_ref[0]
        width = D // N_CHIPS
        g_scale = jnp.zeros((1, width), F32)
        for kk in range(N_CHIPS):
            g_scale = g_scale + jnp.where(k == kk, tot[ROW_POOL_SCALE:ROW_POOL_SCALE + 1, kk * width:(kk + 1) * width], 0.0)
        update("pool_scale", g_scale)
        update("sinks", tot[ROW_SINKS:ROW_SINKS + 1, 0:N_HEADS])

    ins = [tot, kc] + [small_w[nm] for nm in names] + [small_m[nm] for nm in names] + [small_v[nm] for nm in names]
    out_shape = [_sds((1, 1))]
    for nm in names:
        out_shape += [_sds(small_w[nm].shape)] * 4
    outs = pl.pallas_call(
        body, name="small_adamw",
        in_specs=[VSPEC, SSPEC] + [VSPEC] * (3 * n), out_specs=[VSPEC] * len(out_shape), out_shape=out_shape,
        compiler_params=_params(),
    )(*ins)
    return outs[0], {nm: outs[1 + 4 * k: 5 + 4 * k] for k, nm in enumerate(names)}


def _compute_layout(t, full):
    if t.src == "w_gu":
        return full.reshape(2, D, FF)
    if t.src == "pool_w":
        return full.reshape(len(WINDOWS), POOL_G, POOL_G)
    if t.src == "pool_scale":
        return full.reshape(1, D)
    return full.reshape(t.A * t.R, _ncb(t) * t.C)


def kernel(x, p, pre_mix_g, post_mix_g, pre_ffn_g, post_ffn_g, pool_w, pool_scale, kv_g, w_kv, w_q, sinks, w_o, w_gu, w_down, ple_g, w_ple_gate, w_ple_proj, ple_post_g, loss_target, m_pre_mix_g, m_post_mix_g, m_pre_ffn_g, m_post_ffn_g, m_pool_w, m_pool_scale, m_kv_g, m_w_kv, m_w_q, m_sinks, m_w_o, m_w_gu, m_w_down, m_ple_g, m_w_ple_gate, m_w_ple_proj, m_ple_post_g, v_pre_mix_g, v_post_mix_g, v_pre_ffn_g, v_post_ffn_g, v_pool_w, v_pool_scale, v_kv_g, v_w_kv, v_w_q, v_sinks, v_w_o, v_w_gu, v_w_down, v_ple_g, v_w_ple_gate, v_w_ple_proj, v_ple_post_g):
    weights = dict(pre_mix_g=pre_mix_g, post_mix_g=post_mix_g, pre_ffn_g=pre_ffn_g, post_ffn_g=post_ffn_g,
                   pool_w=pool_w, pool_scale=pool_scale, kv_g=kv_g, w_kv=w_kv, w_q=w_q, sinks=sinks, w_o=w_o,
                   w_gu=w_gu, w_down=w_down, ple_g=ple_g, w_ple_gate=w_ple_gate, w_ple_proj=w_ple_proj,
                   ple_post_g=ple_post_g)
    m_in = dict(pre_mix_g=m_pre_mix_g, post_mix_g=m_post_mix_g, pre_ffn_g=m_pre_ffn_g, post_ffn_g=m_post_ffn_g,
                pool_w=m_pool_w, pool_scale=m_pool_scale, kv_g=m_kv_g, w_kv=m_w_kv, w_q=m_w_q, sinks=m_sinks,
                w_o=m_w_o, w_gu=m_w_gu, w_down=m_w_down, ple_g=m_ple_g, w_ple_gate=m_w_ple_gate,
                w_ple_proj=m_w_ple_proj, ple_post_g=m_ple_post_g)
    v_in = dict(pre_mix_g=v_pre_mix_g, post_mix_g=v_post_mix_g, pre_ffn_g=v_pre_ffn_g, post_ffn_g=v_post_ffn_g,
                pool_w=v_pool_w, pool_scale=v_pool_scale, kv_g=v_kv_g, w_kv=v_w_kv, w_q=v_w_q, sinks=v_sinks,
                w_o=v_w_o, w_gu=v_w_gu, w_down=v_w_down, ple_g=v_ple_g, w_ple_gate=v_w_ple_gate,
                w_ple_proj=v_w_ple_proj, ple_post_g=v_ple_post_g)
    order = ["pre_mix_g", "post_mix_g", "pre_ffn_g", "post_ffn_g", "pool_w", "pool_scale", "kv_g", "w_kv", "w_q",
             "sinks", "w_o", "w_gu", "w_down", "ple_g", "w_ple_gate", "w_ple_proj", "ple_post_g"]

    kc = jnp.stack([2 * lax.axis_index("x") + lax.axis_index("y"), lax.axis_index("c")]).astype(jnp.int32)
    s_len = x.shape[1]
    x2d = x.reshape(s_len, D)
    p3d = p.reshape(2, s_len, PLE)
    target = loss_target.reshape(s_len, D)
    kv_g2d = kv_g.reshape(1, D)
    gains = {nm: weights[nm] for nm in GAIN_ROWS}

    def shard_view(src, a):
        t = next(t for t in BIGS.values() if t.src == src)
        return a.reshape(-1, t.R, t.C)

    first, second = ["pool_w", "pool_scale"], ["w_gu0", "w_down0"]
    rest = [nm for nm in BIGS if nm not in first + second]
    specs = dict(BIGS, pool_scale=POOL_SCALE)
    placed = {}

    def place_job(nm):
        if nm == "pool_scale":
            return _place_job(POOL_SCALE, pool_scale.reshape(1, 1, D // N_CHIPS), F32)
        return _place_job(BIGS[nm], shard_view(BIGS[nm].src, weights[BIGS[nm].src]))

    def gather(names, rows=None):
        rows = rows or {}
        parts = [(specs[nm],) + tuple(rows.get(nm, (0, specs[nm].R))) for nm in names]
        return _gather_rider(parts, [placed[nm] for nm in names])

    def take(names, results):
        for nm, a in zip(names, results):
            placed[nm] = a

    def weight(nm):
        return _compute_layout(specs[nm], placed[nm])

    take(first, [r[0] for r in _multi_call("place_pool", [place_job(nm) for nm in first], kc)])
    cast, got = _multi_call("place_ffn0", [place_job(nm) for nm in second], kc, rider=gather(first))
    take(second, [r[0] for r in cast])
    take(first, got)
    jobs = [place_job(nm) for nm in rest]
    jobs.append(_mixa_fwd_job(x2d, gains["pre_mix_g"], weight("pool_w"), weight("pool_scale"), gains["post_mix_g"]))
    results, got = _multi_call("cast_and_mixa_fwd", jobs, kc, rider=gather(second))
    take(rest, [r[0] for r in results[:-1]])
    take(second, got)
    y0, x1 = results[-1]

    ride = ["w_ple_gate0", "w_ple_proj0", "w_q", "w_kv", "w_o", "w_gu1"]
    (f0, x2, g0, u0), got = _ffn_fwd(0, x1, gains["pre_ffn_g"], weight("w_gu0"), weight("w_down0"), gains["post_ffn_g"],
                             rider=gather(ride, {"w_gu1": (0, 320)}))
    take(ride, got)

    ride = ["w_ple_gate1", "w_ple_proj1", "w_gu1"]
    (z0, pe0, x3, q, kv), got = _ple_fwd(
        0, x2, p3d, gains["ple_g"], weight("w_ple_gate0"), weight("w_ple_proj0"), gains["ple_post_g"],
        qkv=(gains["pre_mix_g"], kv_g2d, weight("w_q"), weight("w_kv")),
        rider=gather(ride, {"w_gu1": (320, 704)}))
    take(ride, got)

    ride = ["w_down1", "w_gu1"]
    (attn, y1, x4), got = _attn_fwd(q, kv, sinks, x3, weight("w_o"), gains["post_mix_g"],
                                    rider=gather(ride, {"w_gu1": (704, D)}))
    take(ride, got)

    local = {}
    (f1, g1, u1, dx5, local["w_ple_gate1"], local["w_ple_proj1"], d_ple1, d_plepost1, loss_row), _ = _ffn_fwd(
        1, x4, gains["pre_ffn_g"], weight("w_gu1"), weight("w_down1"), gains["post_ffn_g"],
        head=(p3d, gains["ple_g"], weight("w_ple_gate1"), weight("w_ple_proj1"), gains["ple_post_g"], target))

    landed = {}
    fused = {}

    def local_grads(names):
        return [local[nm].reshape(_full_shape(BIGS[nm])) for nm in names]

    def pair_exchange(names):
        return _pair_exchange_rider([BIGS[nm] for nm in names], local_grads(names))

    def pair_sum(tag, names, lands):
        jobs = [_pair_sum_job(BIGS[nm], g, l) for nm, g, l in zip(names, local_grads(names), lands)]
        return [r[0] for r in _multi_call(f"pair_sum_{tag}", jobs, kc)]

    def scatter(names, sums):
        return _scatter_rider([BIGS[nm] for nm in names], sums)

    def keep(names, sums, got):
        for nm, s, l in zip(names, sums, got):
            landed[nm] = (s, l)

    group_a = ["w_ple_gate1", "w_ple_proj1"]
    (dx4, d_preffn1, d_postffn1, *scattered), lands_a = _ffn_bwd(
        1, dx5, x4, f1, g1, u1, gains["pre_ffn_g"], weight("w_gu1"), weight("w_down1"), gains["post_ffn_g"], kc,
        rider=pair_exchange(group_a))
    fused["w_gu1"], fused["w_down1"] = scattered[0:2], scattered[2:4]

    (dq, dkv, local["w_o"], d_postmix1, d_sinks), _ = _attn_bwd(
        dx4, y1, attn, q, kv, sinks, weight("w_o"), gains["post_mix_g"])
    dx3, local["w_q"], local["w_kv"], d_premix1, d_kvg = _qkv_bwd(
        dq, dkv, x3, dx4, gains["pre_mix_g"], kv_g2d, weight("w_q"), weight("w_kv"))

    group_b = ["w_o", "w_q", "w_kv"]
    (dx2, local["w_ple_gate0"], local["w_ple_proj0"], d_ple0, d_plepost0), lands_b = _ple_bwd(
        0, dx3, x2, z0, pe0, p3d, gains["ple_g"], weight("w_ple_gate0"), gains["ple_post_g"],
        rider=pair_exchange(group_b))
    group_ab = group_a + group_b
    sums_ab = pair_sum("ab", group_ab, lands_a + lands_b)

    group_c = ["w_ple_gate0", "w_ple_proj0"]
    (dx1, d_preffn0, d_postffn0, *scattered), got = _ffn_bwd(
        0, dx2, x1, f0, g0, u0, gains["pre_ffn_g"], weight("w_gu0"), weight("w_down0"), gains["post_ffn_g"], kc,
        rider=_both(pair_exchange(group_c), scatter(group_ab, sums_ab)))
    fused["w_gu0"], fused["w_down0"] = scattered[0:2], scattered[2:4]
    lands_c = got[:len(group_c)]
    keep(group_ab, sums_ab, got[len(group_c):])

    layers_of = lambda src: [t for t in BIGS.values() if t.src == src]
    own_scatter = ["w_gu", "w_down"]
    early = own_scatter + ["w_q", "w_o", "w_kv"]
    late = ["w_ple_gate", "w_ple_proj"]
    by_cols = lambda srcs: [src == "w_down" for src in srcs]
    jobs = [_chip_sum_fused_job(layers_of(src), fused, by_cols=src == "w_down") for src in own_scatter]
    jobs += [_chip_sum_job(layers_of(src), landed) for src in early if src not in own_scatter]
    jobs_c = [_pair_sum_job(BIGS[nm], g, l) for nm, g, l in zip(group_c, local_grads(group_c), lands_c)]
    sums = [r[0] for r in _multi_call("chip_sum_early", jobs + jobs_c, kc)]
    halves, sums_c = sums[:len(jobs)], sums[len(jobs):]
    (dx0, d_pool, d_scale, d_postmix0, d_premix0), got = _mixa_bwd(
        dx1, x2d, y0, gains["pre_mix_g"], weight("pool_w"), weight("pool_scale"), gains["post_mix_g"],
        rider=_both(scatter(group_c, sums_c), _share_rider(halves, by_cols(early))))
    keep(group_c, sums_c, got[:len(group_c)])
    full_grads = dict(zip(early, got[len(group_c):]))

    rows = [d_premix0, d_premix1, d_postmix0, d_postmix1, d_preffn0, d_preffn1, d_postffn0, d_postffn1,
            d_ple0, d_ple1, d_plepost0, d_plepost1, d_kvg, d_scale, d_sinks, loss_row]
    as2d = lambda a: a.reshape(1, D) if a.ndim == 1 else a
    halves = [r[0] for r in _multi_call("chip_sum_late", [_chip_sum_job(layers_of(src), landed) for src in late], kc)]
    (tot, g_pool), got = _small_all_reduce(rows, d_pool, rider=_share_rider(halves, by_cols(late)))
    full_grads.update(zip(late, got))
    full_grads["pool_w"] = g_pool
    loss, small = _small_adamw(tot, kc, {nm: as2d(weights[nm]) for nm in SMALL_NAMES},
                               {nm: as2d(m_in[nm]) for nm in SMALL_NAMES},
                               {nm: as2d(v_in[nm]) for nm in SMALL_NAMES})


    def adam_job(src):
        rb = layers_of(src)[0].rb // (1 if src in ("pool_w", "w_gu") else 2)
        return _adamw_job(rb, shard_view(src, weights[src]), full_grads[src],
                          shard_view(src, m_in[src]), shard_view(src, v_in[src]))

    out = {"grad": {}, "delta": {}, "new_m": {}, "new_v": {}}
    results = dict(zip(BIG_SOURCES, _multi_call("adamw", [adam_job(src) for src in BIG_SOURCES], kc)))
    for src in BIG_SOURCES:
        shape = weights[src].shape
        for kind, a in zip(("grad", "delta", "new_m", "new_v"), results[src]):
            out[kind][src] = a.reshape(shape)
    for nm in SMALL_NAMES:
        shape = weights[nm].shape
        for kind, a in zip(("grad", "delta", "new_m", "new_v"), small[nm]):
            out[kind][nm] = a.reshape(shape)

    return (loss.reshape(()), dx0.reshape(x.shape),
            *[out["grad"][nm] for nm in order], *[out["delta"][nm] for nm in order],
            *[out["new_m"][nm] for nm in order], *[out["new_v"][nm] for nm in order])
```

```python
import collections

import jax
import jax.numpy as jnp
from jax import lax
from jax.experimental import pallas as pl
from jax.experimental.pallas import tpu as pltpu

D = 1024
FF = 2816
N_HEADS = 16
HEAD_DIM = 64
N_KV_HEADS = 4
GQA = N_HEADS // N_KV_HEADS
KVD = N_KV_HEADS * HEAD_DIM
PLE = 256
BLK = 128
WINDOWS = (2, 4, 8, 16)
POOL_G = 256
HALO = 16
EPS = 1e-6
NEG_INF = -1e30
ATT_SCALE = HEAD_DIM ** -0.5
SLOPES = tuple(2.0 ** (-8.0 * (h + 1) / N_HEADS) for h in range(N_HEADS))
N_CHIPS = 4
N_DEV = 8

LR, B1, B2, AEPS, WD, STEP = 0.001, 0.9, 0.999, 1e-08, 0.01, 10
BC1 = 1.0 - B1 ** STEP
BC2 = 1.0 - B2 ** STEP

BF = jnp.bfloat16
F32 = jnp.float32
MESH = pl.DeviceIdType.MESH
VMEM_LIMIT_V7X = 58 * 1024 * 1024
TM = 256
TM_FFN_BWD = 512
FF_CHUNK = 256
FF_HALF = FF // 2

VSPEC = pl.BlockSpec(memory_space=pltpu.VMEM)
SSPEC = pl.BlockSpec(memory_space=pltpu.SMEM)
ANYSPEC = pl.BlockSpec(memory_space=pl.ANY)


def _params(n_grid=0):
    sem = ("arbitrary",) * n_grid if n_grid else None
    return pltpu.CompilerParams(dimension_semantics=sem, vmem_limit_bytes=VMEM_LIMIT_V7X)


def _sds(shape, dtype=F32):
    return jax.ShapeDtypeStruct(tuple(shape), dtype)


Rider = collections.namedtuple("Rider", "arrays out_shapes aliases scratch start mid finish")
MID_NUM, MID_DEN = 5, 8


def _call(body, *, name, grid, in_specs, out_specs, out_shape, args, scratch_shapes=(), rider=None, prefetch=None):
    ni, no, ns = len(in_specs), len(out_specs), len(scratch_shapes)
    npre = 0 if prefetch is None else 1
    pre = [] if prefetch is None else [prefetch]
    if rider is None:
        rider = Rider([], [], {}, [], None, None, None)
    ri, ro = len(rider.arrays), len(rider.out_shapes)

    def full(*refs):
        pre_refs, refs = refs[:npre], refs[npre:]
        ins, refs = refs[:ni], refs[ni:]
        rins, refs = refs[:ri], refs[ri:]
        outs, refs = refs[:no], refs[no:]
        routs, refs = refs[:ro], refs[ro:]
        scr, rscr = refs[:ns], refs[ns:]
        ids = [pl.program_id(a) for a in range(len(grid))]
        first = ids[0] == 0
        last = ids[0] == grid[0] - 1
        for a in range(1, len(grid)):
            first = first & (ids[a] == 0)
            last = last & (ids[a] == grid[a] - 1)

        if rider.start is not None:
            @pl.when(first)
            def _():
                rider.start(rins, routs, rscr)

        if rider.mid is not None:
            assert len(grid) == 1

            @pl.when(ids[0] == (grid[0] * MID_NUM) // MID_DEN)
            def _():
                rider.mid(rins, routs, rscr)

        body(*pre_refs, *ins, *outs, *scr)

        if rider.finish is not None:
            @pl.when(last)
            def _():
                rider.finish(rins, routs, rscr)

    outs = pl.pallas_call(
        full, name=name,
        grid_spec=pltpu.PrefetchScalarGridSpec(
            num_scalar_prefetch=npre, grid=grid,
            in_specs=list(in_specs) + [ANYSPEC] * ri, out_specs=list(out_specs) + [ANYSPEC] * ro,
            scratch_shapes=list(scratch_shapes) + list(rider.scratch)),
        out_shape=list(out_shape) + list(rider.out_shapes),
        input_output_aliases={npre + ni + a: no + b for a, b in rider.aliases.items()},
        compiler_params=_params(len(grid)))(*pre, *args, *rider.arrays)
    return list(outs[:no]), list(outs[no:])


Job = collections.namedtuple("Job", "steps ins outs fn")


def _multi_call(name, jobs, kc, rider=None):
    n = max(job.steps for job in jobs)

    def clamped(index, steps):
        return lambda s, kc_ref: index(jnp.minimum(s, steps - 1), kc_ref)

    in_specs, out_specs, out_shape, args = [], [], [], []
    for job in jobs:
        for arr, block, index, *single in job.ins:
            mode = dict(pipeline_mode=pl.Buffered(1)) if single and single[0] else {}
            in_specs.append(pl.BlockSpec(block, clamped(index, job.steps), **mode))
            args.append(arr)
        for sds, block, index in job.outs:
            out_specs.append(pl.BlockSpec(block, clamped(index, job.steps)))
            out_shape.append(sds)
    n_in = len(args)

    def body(kc_ref, *refs):
        s = pl.program_id(0)
        i0, o0 = 0, n_in
        for job in jobs:
            ins, outs = refs[i0:i0 + len(job.ins)], refs[o0:o0 + len(job.outs)]
            i0, o0 = i0 + len(job.ins), o0 + len(job.outs)

            @pl.when(s < job.steps)
            def _():
                job.fn(s, kc_ref, ins, outs)

    outs, routs = _call(body, name=name, grid=(n,), in_specs=in_specs, out_specs=out_specs, out_shape=out_shape,
                        args=args, prefetch=kc, rider=rider)
    res, o0 = [], 0
    for job in jobs:
        res.append(outs[o0:o0 + len(job.outs)])
        o0 += len(job.outs)
    return res if rider is None else (res, routs)


def _rms_fwd(x, g):
    r = lax.rsqrt(jnp.mean(x * x, axis=-1, keepdims=True) + EPS)
    return x * r * g


def _rms_bwd(x, g, dy):
    r = lax.rsqrt(jnp.mean(x * x, axis=-1, keepdims=True) + EPS)
    xn = x * r
    dxn = dy * g
    dx = r * (dxn - xn * jnp.mean(dxn * xn, axis=-1, keepdims=True))
    return dx, dy * xn


def _rowsum(a):
    return jnp.sum(a, axis=0, keepdims=True)


def _sigmoid(z):
    return 1.0 / (1.0 + jnp.exp(-z))


def _dot(a, b):
    return jnp.dot(a, b, preferred_element_type=F32)


def _dot_nt(a, b):
    return lax.dot_general(a, b, (((1,), (1,)), ((), ())), preferred_element_type=F32)


def _dot_tn(a, b):
    return lax.dot_general(a, b, (((0,), (0,)), ((), ())), preferred_element_type=F32)


def _row_spec(tm, width=D):
    return pl.BlockSpec((tm, width), lambda i: (i, 0))


def _const_spec(shape):
    zeros = (0,) * len(shape)
    return pl.BlockSpec(tuple(shape), lambda *_: zeros)


def _pool_delta(he, pos):
    out = []
    for gi, w in enumerate(WINDOWS):
        hg = he[:, gi * POOL_G:(gi + 1) * POOL_G]
        s = hg
        k = 1
        while k < w:
            s = s + pltpu.roll(s, k, 0)
            k *= 2
        cnt = jnp.maximum(jnp.minimum(pos + 1, w), 1).astype(F32)
        out.append(s / cnt - hg)
    return out


def _load_with_halo_before(x_ref, i, tm):
    r0 = pl.multiple_of(i * tm, tm)
    hs = pl.multiple_of(jnp.maximum(i * tm - HALO, 0), 8)
    xh = jnp.where(i > 0, x_ref[pl.ds(hs, HALO), :], 0.0)
    xt = x_ref[pl.ds(r0, tm), :]
    return xt, jnp.concatenate([xh, xt], axis=0)


def _mixa_fwd_job(x, pre_g, pool_w, pool_scale, post_g):
    s_len = x.shape[0]

    def fn(i, kc_ref, ins, outs):
        x_ref, pg_ref, w_ref, sc_ref, qg_ref = ins
        y_ref, x1_ref = outs
        xt, xe = _load_with_halo_before(x_ref, i, TM)
        he = _rms_fwd(xe, pg_ref[0:1, :])
        pos = i * TM - HALO + lax.broadcasted_iota(jnp.int32, (TM + HALO, 1), 0)
        ds = _pool_delta(he, pos)
        ys = [_dot(ds[gi][HALO:, :].astype(BF), w_ref[gi]) for gi in range(len(WINDOWS))]
        y = jnp.concatenate(ys, axis=1) * sc_ref[...]
        y_ref[...] = y
        x1_ref[...] = xt + _rms_fwd(y, qg_ref[0:1, :])

    def whole(a):
        zeros = (0,) * a.ndim
        return (a, a.shape, lambda j, kc_ref: zeros, True)

    rows = lambda j, kc_ref: (j, 0)
    return Job(s_len // TM, [whole(a) for a in (x, pre_g, pool_w, pool_scale, post_g)],
               [(_sds((s_len, D)), (TM, D), rows), (_sds((s_len, D)), (TM, D), rows)], fn)


def _mixa_bwd(dx1, x, y, pre_g, pool_w, pool_scale, post_g, rider=None):
    s_len = x.shape[0]
    n = s_len // TM
    ng = len(WINDOWS)

    def body(dx_ref, x_ref, y_ref, pg_ref, w_ref, sc_ref, qg_ref,
             dx0_ref, dw_ref, dsc_ref, dqg_ref, dpg_ref, wacc):
        i = pl.program_id(0)

        @pl.when(i == 0)
        def _():
            wacc[...] = jnp.zeros_like(wacc)
            dsc_ref[...] = jnp.zeros_like(dsc_ref)
            dqg_ref[...] = jnp.zeros_like(dqg_ref)
            dpg_ref[...] = jnp.zeros_like(dpg_ref)

        r0 = pl.multiple_of(i * TM, TM)
        xt, xe = _load_with_halo_before(x_ref, i, TM)
        he = _rms_fwd(xe, pg_ref[0:1, :])
        pos_b = i * TM - HALO + lax.broadcasted_iota(jnp.int32, (TM + HALO, 1), 0)
        ds = _pool_delta(he, pos_b)

        last = i == n - 1
        a0 = pl.multiple_of(jnp.minimum(i * TM + TM, s_len - HALO), 8)
        ye = jnp.concatenate([y_ref[pl.ds(r0, TM), :], y_ref[pl.ds(a0, HALO), :]], axis=0)
        dt = dx_ref[pl.ds(r0, TM), :]
        de = jnp.concatenate([dt, jnp.where(last, 0.0, dx_ref[pl.ds(a0, HALO), :])], axis=0)
        dye, prod = _rms_bwd(ye, qg_ref[0:1, :], de)
        dqg_ref[...] += _rowsum(prod[:TM, :])
        dys = dye * sc_ref[...]
        pos_a = i * TM + lax.broadcasted_iota(jnp.int32, (TM + HALO, 1), 0)

        dhs, dscs = [], []
        for gi, w in enumerate(WINDOWS):
            sl = slice(gi * POOL_G, (gi + 1) * POOL_G)
            wg = w_ref[gi]
            dys_g = dys[:, sl].astype(BF)
            d_g = ds[gi][HALO:, :].astype(BF)
            ypre = _dot(d_g, wg)
            dscs.append(_rowsum(dye[:TM, sl] * ypre))
            wacc[gi] += _dot_tn(d_g, dys_g[:TM, :])
            dd = _dot_nt(dys_g, wg)
            cnt = jnp.minimum(pos_a + 1, w).astype(F32)
            a = dd / cnt
            k = 1
            while k < w:
                a = a + pltpu.roll(a, TM + HALO - k, 0)
                k *= 2
            dhs.append(a[:TM, :] - dd[:TM, :])
        dsc_ref[...] += jnp.concatenate(dscs, axis=1)
        dh = jnp.concatenate(dhs, axis=1)
        dxp, prod2 = _rms_bwd(xt, pg_ref[0:1, :], dh)
        dpg_ref[...] += _rowsum(prod2)
        dx0_ref[...] = dt + dxp

        @pl.when(last)
        def _():
            dw_ref[...] = wacc[...].astype(BF)

    return _call(
        body, name="mixa_bwd", grid=(n,), in_specs=[VSPEC] * 7,
        out_specs=[_row_spec(TM), _const_spec((ng, POOL_G, POOL_G)), _const_spec((1, D)),
                   _const_spec((1, D)), _const_spec((1, D))],
        out_shape=[_sds((s_len, D)), _sds((ng, POOL_G, POOL_G), BF), _sds((1, D)), _sds((1, D)), _sds((1, D))],
        scratch_shapes=[pltpu.VMEM((ng, POOL_G, POOL_G), F32)],
        args=[dx1, x, y, pre_g, pool_w, pool_scale, post_g], rider=rider)


def _ple_math(layer, x, p_blk, g_ref, wg_ref, wp_ref, qg_ref):
    r = _rms_fwd(x, g_ref[layer:layer + 1, :]).astype(BF)
    z = _dot(r, wg_ref[...])
    pe = _dot(p_blk.astype(BF), wp_ref[...])
    return z, pe, x + _rms_fwd(pe * _sigmoid(z), qg_ref[layer:layer + 1, :])


def _ple_bwd_math(layer, dx, x, z, pe, p_blk, g_ref, wg_ref, qg_ref):
    gate = _sigmoid(z)
    de, prod = _rms_bwd(pe * gate, qg_ref[layer:layer + 1, :], dx)
    dpe = (de * gate).astype(BF)
    dz = (de * pe * gate * (1.0 - gate)).astype(BF)
    dwp = _dot_tn(p_blk.astype(BF), dpe)
    g = g_ref[layer:layer + 1, :]
    dwg = _dot_tn(_rms_fwd(x, g).astype(BF), dz)
    dxp, prod2 = _rms_bwd(x, g, _dot_nt(dz, wg_ref[...]))
    return dx + dxp, dwp, dwg, _rowsum(prod2), _rowsum(prod)


def _ffn_fwd(layer, x1, pre_g, wgu, wd, post_g, rider=None, head=None):
    s_len = x1.shape[0]

    def body(*refs):
        if head:
            (x_ref, pg_ref, wgu_ref, wd_ref, qg_ref, p_ref, eg_ref, wg_ref, wp_ref, eq_ref, t_ref,
             f_ref, g_ref, u_ref, dx_ref, dwg_ref, dwp_ref, deg_ref, deq_ref, lv_ref, gacc, pacc) = refs
        else:
            x_ref, pg_ref, wgu_ref, wd_ref, qg_ref, f_ref, x2_ref, g_ref, u_ref = refs
        x = x_ref[...]
        h = _rms_fwd(x, pg_ref[layer:layer + 1, :]).astype(BF)
        f = jnp.zeros((TM, D), F32)
        for c in range(FF // FF_HALF):
            cols = slice(c * FF_HALF, (c + 1) * FF_HALF)
            g = _dot(h, wgu_ref[0, :, cols])
            u = _dot(h, wgu_ref[1, :, cols])
            g_ref[:, cols] = g.astype(BF)
            u_ref[:, cols] = u.astype(BF)
            act = g * _sigmoid(g) * u
            f = f + _dot(act.astype(BF), wd_ref[cols, :])
        f_ref[...] = f
        x2 = x + _rms_fwd(f, qg_ref[layer:layer + 1, :])
        if not head:
            x2_ref[...] = x2
        else:
            step = pl.program_id(0)

            @pl.when(step == 0)
            def _():
                for ref in (lv_ref, deg_ref, deq_ref, gacc, pacc):
                    ref[...] = jnp.zeros_like(ref)
            p_blk = p_ref[...]
            z, pe, x3 = _ple_math(layer, x2, p_blk, eg_ref, wg_ref, wp_ref, eq_ref)
            err = x3 - t_ref[...]
            lv_ref[...] += _rowsum(err * err)
            dx, dwp, dwg, deg, deq = _ple_bwd_math(layer, err * (1.0 / D), x2, z, pe, p_blk, eg_ref, wg_ref, eq_ref)
            dx_ref[...] = dx
            pacc[...] += dwp
            gacc[...] += dwg
            deg_ref[...] += deg
            deq_ref[...] += deq

            @pl.when(step == s_len // TM - 1)
            def _():
                dwg_ref[...] = gacc[...].astype(BF)
                dwp_ref[...] = pacc[...].astype(BF)

    in_specs = [_row_spec(TM), VSPEC, VSPEC, VSPEC, VSPEC]
    args = [x1, pre_g, wgu, wd, post_g]
    out_specs = [_row_spec(TM), _row_spec(TM), _row_spec(TM, FF), _row_spec(TM, FF)]
    out_shape = [_sds((s_len, D)), _sds((s_len, D)), _sds((s_len, FF), BF), _sds((s_len, FF), BF)]
    scratch = []
    if head:
        del out_specs[1], out_shape[1]
        p, ple_g, w_gate, w_proj, ple_post_g, target = head
        in_specs += [pl.BlockSpec((None, TM, PLE), lambda i: (layer, i, 0)), VSPEC, VSPEC, VSPEC, VSPEC, _row_spec(TM)]
        args += [p, ple_g, w_gate, w_proj, ple_post_g, target]
        out_specs += [_row_spec(TM), _const_spec((D, D)), _const_spec((PLE, D))] + [_const_spec((1, D))] * 3
        out_shape += [_sds((s_len, D)), _sds((D, D), BF), _sds((PLE, D), BF)] + [_sds((1, D))] * 3
        scratch = [pltpu.VMEM((D, D), F32), pltpu.VMEM((PLE, D), F32)]
    return _call(body, name=f"ffn_fwd{layer}", grid=(s_len // TM,), in_specs=in_specs, out_specs=out_specs,
                 out_shape=out_shape, args=args, scratch_shapes=scratch, rider=rider)


GU_PIECE = 128
DN_PIECE = 64
DN_SLOT = FF // N_CHIPS
HALF_D = D // 2
CHUNK_STRIDE = 6
CHUNK_START = (1, 7, 4, 10)


def _ffn_bwd(layer, dx2, x1, f, g_pre, u_pre, pre_g, wgu, wd, post_g, kc, rider=None):
    s_len = x1.shape[0]
    tm = TM_FFN_BWD
    n = s_len // tm
    nc = FF // FF_CHUNK
    n_gu, n_dn = FF_CHUNK // GU_PIECE, FF_CHUNK // DN_PIECE
    n_pieces = 2 * n_gu + n_dn
    n_blk = FF_HALF // GU_PIECE

    def edge_rows(c, i, kc_ref):
        return (jnp.where((c == 0) | (c == nc - 1), i, n - 1), 0)

    def chunk_at(c, kc_ref):
        k = kc_ref[0]
        start = jnp.where(k == 0, CHUNK_START[0], jnp.where(k == 1, CHUNK_START[1],
                                                            jnp.where(k == 2, CHUNK_START[2], CHUNK_START[3])))
        return ((c + start) * CHUNK_STRIDE) % nc

    def exchange(kc_ref, c, accg, accu, accd, own_gu_ref, land_gu_ref, own_dn_ref, land_dn_ref,
                 pl_gu, pl_dn, sib_gu, sib_dn, mine_gu, mine_dn, sum_gu, sum_dn,
                 psend, precv, ssend, lsem, rrecv):
        x, y, core = lax.axis_index("x"), lax.axis_index("y"), lax.axis_index("c")
        lower = core == 0

        def pair_copy(cc, part):
            p = cc % 2
            src, dst = ((sib_gu, pl_gu), (sib_dn, pl_dn))[part]
            return pltpu.make_async_remote_copy(src.at[p], dst.at[cc], psend.at[p, part], precv.at[cc, part],
                                                device_id=(x, y, 1 - core), device_id_type=MESH)

        def scatter(cc, wait):
            p = cc % 2
            hidden = chunk_at(cc, kc_ref) * FF_CHUNK

            assert n_gu == 2
            k0, k1 = hidden // FF_HALF, (hidden + GU_PIECE) // FF_HALF
            blk = (hidden - k0 * FF_HALF) // GU_PIECE
            for gu in range(2):
                @pl.when(k0 == k1)
                def _():
                    piece(p, wait, 2 * gu, sum_gu.at[p, gu], k0 + 2 * gu, 0, (pl.ds(blk, 2),))

                @pl.when(k0 != k1)
                def _():
                    piece(p, wait, 2 * gu, sum_gu.at[p, gu, 0], k0 + 2 * gu, 0, (blk,))
                    piece(p, wait, 2 * gu + 1, sum_gu.at[p, gu, 1], k1 + 2 * gu, 0, (0,))

            kd = hidden // DN_SLOT
            off = pl.multiple_of(hidden - kd * DN_SLOT, DN_PIECE)
            m = jnp.minimum((DN_SLOT - off) // DN_PIECE, n_dn)
            for mm in range(1, n_dn + 1):
                @pl.when(m == mm)
                def _():
                    rows = mm * DN_PIECE
                    piece(p, wait, 2 * n_gu, sum_dn.at[p, pl.ds(0, rows), :], kd, 1, (pl.ds(off, rows), slice(None)))
                    if mm < n_dn:
                        piece(p, wait, 2 * n_gu + 1, sum_dn.at[p, pl.ds(rows, FF_CHUNK - rows), :], kd + 1, 1,
                              (pl.ds(0, FF_CHUNK - rows), slice(None)))

        def piece(p, wait, pi, src, k, t, where):
            own_ref, land_ref = ((own_gu_ref, land_gu_ref), (own_dn_ref, land_dn_ref))[t]
            kx, ky = k // 2, k % 2
            fx, fy = (kx != x).astype(jnp.int32), (ky != y).astype(jnp.int32)
            local = (fx + fy) == 0
            j = jnp.maximum(fx + 2 * fy - 1, 0)

            @pl.when(local)
            def _():
                cp = pltpu.make_async_copy(src, own_ref.at[where], lsem.at[p, pi])
                if wait:
                    cp.wait()
                else:
                    cp.start()

            @pl.when(jnp.logical_not(local))
            def _():
                cp = pltpu.make_async_remote_copy(src, land_ref.at[(j,) + where], ssend.at[p, pi],
                                                  rrecv.at[t, j], device_id=(kx, ky, core), device_id_type=MESH)
                if wait:
                    cp.wait_send()
                else:
                    cp.start()

        def add_and_scatter(cc):
            p = cc % 2
            pair_copy(cc, 0).wait_recv()
            pair_copy(cc, 1).wait_recv()
            s_gu = (mine_gu[...] + pl_gu[cc].astype(F32)).astype(BF)
            for hc in range(n_gu):
                sum_gu[p, :, hc] = s_gu[:, :, hc * GU_PIECE:(hc + 1) * GU_PIECE]
            sum_dn[p] = (mine_dn[...] + pl_dn[cc].astype(F32)).astype(BF)
            scatter(cc, wait=False)

        @pl.when(c >= 1)
        def _():
            @pl.when(c >= 3)
            def _():
                scatter(c - 3, wait=True)
            add_and_scatter(c - 1)

        @pl.when(c >= 2)
        def _():
            pair_copy(c - 2, 0).wait_send()
            pair_copy(c - 2, 1).wait_send()

        p = c % 2
        my_rows = pl.ds(pl.multiple_of(core * HALF_D, HALF_D), HALF_D)
        sib_rows = pl.ds(pl.multiple_of((1 - core) * HALF_D, HALF_D), HALF_D)
        d_v = accd[...]
        sib_gu[p, 0] = accg[sib_rows, :].astype(BF)
        sib_gu[p, 1] = accu[sib_rows, :].astype(BF)
        sib_dn[p] = jnp.where(lower, d_v[:, HALF_D:], d_v[:, :HALF_D]).astype(BF)
        mine_gu[0] = accg[my_rows, :]
        mine_gu[1] = accu[my_rows, :]
        mine_dn[...] = jnp.where(lower, d_v[:, :HALF_D], d_v[:, HALF_D:])
        pair_copy(c, 0).start()
        pair_copy(c, 1).start()

        @pl.when(c == nc - 1)
        def _():
            scatter(nc - 3, wait=True)
            add_and_scatter(nc - 1)
            for cc in (nc - 2, nc - 1):
                pair_copy(cc, 0).wait_send()
                pair_copy(cc, 1).wait_send()
                scatter(cc, wait=True)
            for t, land_ref in enumerate((land_gu_ref, land_dn_ref)):
                for j in range(N_CHIPS - 1):
                    pltpu.make_async_remote_copy(land_ref.at[j], land_ref.at[j], ssend.at[0, 0], rrecv.at[t, j],
                                                 device_id=(x, y, core), device_id_type=MESH).wait_recv()

    def body(kc_ref, dx_ref, x_ref, f_ref, gp_ref, up_ref, pg_ref, wgu_ref, wd_ref, qg_ref,
             dx1_ref, dpg_ref, dqg_ref, own_gu_ref, land_gu_ref, own_dn_ref, land_dn_ref,
             h_s, df_s, dh_s, accg, accu, accd, *comm):
        c = pl.program_id(0)
        i = pl.program_id(1)
        rows = pl.ds(pl.multiple_of(i * tm, tm), tm)
        pg = pg_ref[layer:layer + 1, :]

        @pl.when((c == 0) & (i == 0))
        def _():
            dpg_ref[...] = jnp.zeros_like(dpg_ref)
            dqg_ref[...] = jnp.zeros_like(dqg_ref)

        @pl.when(c == 0)
        def _():
            h_s[rows, :] = _rms_fwd(x_ref[...], pg).astype(BF)
            df, prod = _rms_bwd(f_ref[...], qg_ref[layer:layer + 1, :], dx_ref[...])
            df_s[rows, :] = df.astype(BF)
            dqg_ref[...] += _rowsum(prod)

        @pl.when(i == 0)
        def _():
            accg[...] = jnp.zeros_like(accg)
            accu[...] = jnp.zeros_like(accu)
            accd[...] = jnp.zeros_like(accd)

        h = h_s[rows, :]
        df = df_s[rows, :]
        wg = wgu_ref[0]
        wu = wgu_ref[1]
        g = gp_ref[...].astype(F32)
        u = up_ref[...].astype(F32)
        sg = _sigmoid(g)
        a = g * sg
        dact = _dot_nt(df, wd_ref[...])
        accd[...] += _dot_tn((a * u).astype(BF), df)
        du = (dact * a).astype(BF)
        dg = (dact * u * (sg * (1.0 + g * (1.0 - sg)))).astype(BF)
        accg[...] += _dot_tn(h, dg)
        accu[...] += _dot_tn(h, du)
        dh = _dot_nt(dg, wg) + _dot_nt(du, wu)

        @pl.when(c == 0)
        def _():
            dh_s[rows, :] = dh

        @pl.when((c > 0) & (c < nc - 1))
        def _():
            dh_s[rows, :] += dh

        @pl.when(c == nc - 1)
        def _():
            dxp, prod = _rms_bwd(x_ref[...], pg, dh_s[rows, :] + dh)
            dpg_ref[...] += _rowsum(prod)
            dx1_ref[...] = dx_ref[...] + dxp

        @pl.when(i == n - 1)
        def _():
            exchange(kc_ref, c, accg, accu, accd, own_gu_ref, land_gu_ref, own_dn_ref, land_dn_ref, *comm)

    dma = pltpu.SemaphoreType.DMA
    return _call(
        body, name=f"ffn_bwd{layer}", grid=(nc, n),
        in_specs=[pl.BlockSpec((tm, D), edge_rows), pl.BlockSpec((tm, D), edge_rows),
                  pl.BlockSpec((tm, D), lambda c, i, kc_ref: (jnp.where(c == 0, i, n - 1), 0),
                               pipeline_mode=pl.Buffered(1)),
                  pl.BlockSpec((tm, FF_CHUNK), lambda c, i, kc_ref: (i, chunk_at(c, kc_ref))),
                  pl.BlockSpec((tm, FF_CHUNK), lambda c, i, kc_ref: (i, chunk_at(c, kc_ref))),
                  VSPEC,
                  pl.BlockSpec((2, D, FF_CHUNK), lambda c, i, kc_ref: (0, 0, chunk_at(c, kc_ref))),
                  pl.BlockSpec((FF_CHUNK, D), lambda c, i, kc_ref: (chunk_at(c, kc_ref), 0)),
                  VSPEC],
        out_specs=[pl.BlockSpec((tm, D), lambda c, i, kc_ref: (jnp.where(c == nc - 1, i, 0), 0)),
                   _const_spec((1, D)), _const_spec((1, D)), ANYSPEC, ANYSPEC, ANYSPEC, ANYSPEC],
        out_shape=[_sds((s_len, D)), _sds((1, D)), _sds((1, D)),
                   _sds((n_blk, HALF_D, GU_PIECE), BF), _sds((N_CHIPS - 1, n_blk, HALF_D, GU_PIECE), BF),
                   _sds((DN_SLOT, HALF_D), BF), _sds((N_CHIPS - 1, DN_SLOT, HALF_D), BF)],
        scratch_shapes=[pltpu.VMEM((s_len, D), BF), pltpu.VMEM((s_len, D), BF), pltpu.VMEM((s_len, D), F32),
                        pltpu.VMEM((D, FF_CHUNK), F32), pltpu.VMEM((D, FF_CHUNK), F32),
                        pltpu.VMEM((FF_CHUNK, D), F32),
                        pltpu.VMEM((nc, 2, HALF_D, FF_CHUNK), BF), pltpu.VMEM((nc, FF_CHUNK, HALF_D), BF),
                        pltpu.VMEM((2, 2, HALF_D, FF_CHUNK), BF), pltpu.VMEM((2, FF_CHUNK, HALF_D), BF),
                        pltpu.VMEM((2, HALF_D, FF_CHUNK), F32), pltpu.VMEM((FF_CHUNK, HALF_D), F32),
                        pltpu.VMEM((2, 2, n_gu, HALF_D, GU_PIECE), BF), pltpu.VMEM((2, FF_CHUNK, HALF_D), BF),
                        dma((2, 2)), dma((nc, 2)), dma((2, n_pieces)), dma((2, n_pieces)), dma((2, N_CHIPS - 1))],
        args=[dx2, x1, f, g_pre, u_pre, pre_g, wgu, wd, post_g], rider=rider, prefetch=kc)


def _ple_fwd(layer, x2, p, ple_g, w_gate, w_proj, post_g, qkv=None, rider=None):
    s_len = x2.shape[0]

    def body(*refs):
        if qkv:
            (x_ref, p_ref, g_ref, wg_ref, wp_ref, qg_ref, ng_ref, kg_ref, wq_ref, wkv_ref,
             z_ref, pe_ref, x3_ref, q_ref, kv_ref) = refs
        else:
            x_ref, p_ref, g_ref, wg_ref, wp_ref, qg_ref, z_ref, pe_ref, x3_ref = refs
        z, pe, x3 = _ple_math(layer, x_ref[...], p_ref[...], g_ref, wg_ref, wp_ref, qg_ref)
        z_ref[...] = z
        pe_ref[...] = pe
        x3_ref[...] = x3
        if qkv:
            q_ref[...] = _dot(_rms_fwd(x3, ng_ref[layer + 1:layer + 2, :]).astype(BF), wq_ref[...]).astype(BF)
            kv_ref[...] = _dot(_rms_fwd(x3, kg_ref[...]).astype(BF), wkv_ref[...]).astype(BF)

    p_spec = pl.BlockSpec((None, TM, PLE), lambda i: (layer, i, 0))
    in_specs = [_row_spec(TM), p_spec, VSPEC, VSPEC, VSPEC, VSPEC]
    args = [x2, p, ple_g, w_gate, w_proj, post_g]
    out_specs = [_row_spec(TM), _row_spec(TM), _row_spec(TM)]
    out_shape = [_sds((s_len, D))] * 3
    if qkv:
        in_specs += [VSPEC] * 4
        args += list(qkv)
        out_specs += [_row_spec(TM), _row_spec(TM, 2 * KVD)]
        out_shape += [_sds((s_len, D), BF), _sds((s_len, 2 * KVD), BF)]
    return _call(body, name=f"ple_fwd{layer}", grid=(s_len // TM,), in_specs=in_specs, out_specs=out_specs,
                 out_shape=out_shape, args=args, rider=rider)


def _ple_bwd(layer, dx3, x2, z, pe, p, ple_g, w_gate, post_g, rider=None):
    s_len = x2.shape[0]
    n = s_len // TM

    def body(dx_ref, x_ref, z_ref, pe_ref, p_ref, g_ref, wg_ref, qg_ref,
             dx2_ref, dwg_ref, dwp_ref, dg_ref, dqg_ref, gacc, pacc):
        i = pl.program_id(0)

        @pl.when(i == 0)
        def _():
            gacc[...] = jnp.zeros_like(gacc)
            pacc[...] = jnp.zeros_like(pacc)
            dg_ref[...] = jnp.zeros_like(dg_ref)
            dqg_ref[...] = jnp.zeros_like(dqg_ref)

        dx2, dwp, dwg, dg, dqg = _ple_bwd_math(layer, dx_ref[...], x_ref[...], z_ref[...], pe_ref[...], p_ref[...],
                                                g_ref, wg_ref, qg_ref)
        dx2_ref[...] = dx2
        pacc[...] += dwp
        gacc[...] += dwg
        dg_ref[...] += dg
        dqg_ref[...] += dqg

        @pl.when(i == n - 1)
        def _():
            dwg_ref[...] = gacc[...].astype(BF)
            dwp_ref[...] = pacc[...].astype(BF)

    p_spec = pl.BlockSpec((None, TM, PLE), lambda i: (layer, i, 0))
    return _call(
        body, name=f"ple_bwd{layer}", grid=(n,),
        in_specs=[_row_spec(TM), _row_spec(TM), _row_spec(TM), _row_spec(TM), p_spec, VSPEC, VSPEC, VSPEC],
        out_specs=[_row_spec(TM), _const_spec((D, D)), _const_spec((PLE, D)), _const_spec((1, D)), _const_spec((1, D))],
        out_shape=[_sds((s_len, D)), _sds((D, D), BF), _sds((PLE, D), BF), _sds((1, D)), _sds((1, D))],
        scratch_shapes=[pltpu.VMEM((D, D), F32), pltpu.VMEM((PLE, D), F32)],
        args=[dx3, x2, z, pe, p, ple_g, w_gate, post_g], rider=rider)


def _qkv_bwd(dq, dkv, x3, dx4, q_g, kv_g, w_q, w_kv):
    s_len = x3.shape[0]
    n = s_len // TM

    def body(dq_ref, dkv_ref, x_ref, dx_ref, qg_ref, kg_ref, wq_ref, wkv_ref,
             dx3_ref, dwq_ref, dwkv_ref, dqg_ref, dkg_ref, qacc, kacc):
        i = pl.program_id(0)

        @pl.when(i == 0)
        def _():
            qacc[...] = jnp.zeros_like(qacc)
            kacc[...] = jnp.zeros_like(kacc)
            dqg_ref[...] = jnp.zeros_like(dqg_ref)
            dkg_ref[...] = jnp.zeros_like(dkg_ref)

        x = x_ref[...]
        qg = qg_ref[1:2, :]
        kg = kg_ref[...]
        dq_v = dq_ref[...]
        dkv_v = dkv_ref[...].astype(BF)
        qacc[...] += _dot_tn(_rms_fwd(x, qg).astype(BF), dq_v)
        kacc[...] += _dot_tn(_rms_fwd(x, kg).astype(BF), dkv_v)
        dxq, prod_q = _rms_bwd(x, qg, _dot_nt(dq_v, wq_ref[...]))
        dxk, prod_k = _rms_bwd(x, kg, _dot_nt(dkv_v, wkv_ref[...]))
        dqg_ref[...] += _rowsum(prod_q)
        dkg_ref[...] += _rowsum(prod_k)
        dx3_ref[...] = dx_ref[...] + dxq + dxk

        @pl.when(i == n - 1)
        def _():
            dwq_ref[...] = qacc[...].astype(BF)
            dwkv_ref[...] = kacc[...].astype(BF)

    outs, _ = _call(
        body, name="qkv_bwd", grid=(n,),
        in_specs=[_row_spec(TM), _row_spec(TM, 2 * KVD), _row_spec(TM), _row_spec(TM), VSPEC, VSPEC, VSPEC, VSPEC],
        out_specs=[_row_spec(TM), _const_spec((D, D)), _const_spec((D, 2 * KVD)),
                   _const_spec((1, D)), _const_spec((1, D))],
        out_shape=[_sds((s_len, D)), _sds((D, D), BF), _sds((D, 2 * KVD), BF), _sds((1, D)), _sds((1, D))],
        scratch_shapes=[pltpu.VMEM((D, D), F32), pltpu.VMEM((D, 2 * KVD), F32)],
        args=[dq, dkv, x3, dx4, q_g, kv_g, w_q, w_kv])
    return outs


def _attn_group(i, q, kvw, sink_ref, g):
    rows = GQA * BLK
    heads = [GQA * g + j for j in range(GQA)]
    off = jnp.where(i > 0, BLK, 0)
    row = lax.broadcasted_iota(jnp.int32, (rows, 2 * BLK), 0)
    rel = (row % BLK) - lax.broadcasted_iota(jnp.int32, (rows, 2 * BLK), 1) + off
    valid = (rel >= 0) & (rel < BLK)
    head_of_row = lax.broadcasted_iota(jnp.int32, (rows, 1), 0) // BLK
    slope = jnp.zeros((rows, 1), F32)
    sink = jnp.zeros((rows, 1), F32)
    for j, h in enumerate(heads):
        slope = jnp.where(head_of_row == j, SLOPES[h], slope)
        sink = jnp.where(head_of_row == j, sink_ref[0, h], sink)
    qs = jnp.concatenate([q[:, h * HEAD_DIM:(h + 1) * HEAD_DIM] for h in heads], axis=0)
    k = kvw[:, g * HEAD_DIM:(g + 1) * HEAD_DIM]
    v = kvw[:, KVD + g * HEAD_DIM:KVD + (g + 1) * HEAD_DIM]
    s = _dot_nt(qs, k) * ATT_SCALE - slope * rel.astype(F32)
    s = jnp.where(valid, s, NEG_INF)
    m = jnp.maximum(jnp.max(s, axis=-1, keepdims=True), sink)
    e = jnp.exp(s - m)
    es = jnp.exp(sink - m)
    inv = 1.0 / (jnp.sum(e, axis=-1, keepdims=True) + es)
    return e * inv, es * inv, qs, k, v


def _unstack_heads(stacked):
    return [stacked[j * BLK:(j + 1) * BLK, :] for j in range(GQA)]


def _kv_window(kv_ref, i):
    ks = pl.multiple_of(jnp.maximum(i * BLK - BLK, 0), BLK)
    return ks, kv_ref[pl.ds(ks, 2 * BLK), :]


def _attn_fwd(q, kv, sinks, x3, w_o, post_g, rider=None):
    s_len = q.shape[0]

    def body(q_ref, kv_ref, sk_ref, x_ref, wo_ref, g_ref, a_ref, y_ref, x4_ref):
        i = pl.program_id(0)
        _, kvw = _kv_window(kv_ref, i)
        q = q_ref[...]
        outs = []
        for g in range(N_KV_HEADS):
            p, _, _, _, v = _attn_group(i, q, kvw, sk_ref, g)
            outs += _unstack_heads(_dot(p.astype(BF), v))
        attn = jnp.concatenate(outs, axis=1)
        a_ref[...] = attn
        y = _dot(attn.astype(BF), wo_ref[...])
        y_ref[...] = y
        x4_ref[...] = x_ref[...] + _rms_fwd(y, g_ref[1:2, :])

    return _call(body, name="attn_fwd", grid=(s_len // BLK,),
                 in_specs=[_row_spec(BLK), VSPEC, SSPEC, _row_spec(BLK), VSPEC, VSPEC],
                 out_specs=[_row_spec(BLK)] * 3, out_shape=[_sds((s_len, D))] * 3,
                 args=[q, kv, sinks, x3, w_o, post_g], rider=rider)


ATT_STEP_BLOCKS = 2


def _attn_bwd(dx4, y, attn, q, kv, sinks, w_o, post_g, rider=None):
    s_len = q.shape[0]
    rows = ATT_STEP_BLOCKS * BLK
    n = s_len // rows

    def body(dx_ref, y_ref, a_ref, q_ref, kv_ref, sk_ref, wo_ref, g_ref,
             dq_ref, dkv_ref, dwo_ref, dg_ref, dsk_ref, wacc):
        i = pl.program_id(0)

        @pl.when(i == 0)
        def _():
            dkv_ref[...] = jnp.zeros_like(dkv_ref)
            wacc[...] = jnp.zeros_like(wacc)
            dg_ref[...] = jnp.zeros_like(dg_ref)
            dsk_ref[...] = jnp.zeros_like(dsk_ref)

        dy, prod = _rms_bwd(y_ref[...], g_ref[1:2, :], dx_ref[...])
        dg_ref[...] += _rowsum(prod)
        dyb = dy.astype(BF)
        attn_all = a_ref[...]
        wacc[...] += _dot_tn(attn_all.astype(BF), dyb)
        d_o_all = _dot_nt(dyb, wo_ref[...])
        q_all = q_ref[...]
        lane = lax.broadcasted_iota(jnp.int32, (1, D), 1)
        dsk = jnp.zeros((1, D), F32)
        for sub in range(ATT_STEP_BLOCKS):
            blk = i * ATT_STEP_BLOCKS + sub
            sl = slice(sub * BLK, (sub + 1) * BLK)
            d_o, q = d_o_all[sl, :], q_all[sl, :]
            dod = d_o * attn_all[sl, :]
            ks, kvw = _kv_window(kv_ref, blk)
            dqs, dks, dvs = [], [], []
            for g in range(N_KV_HEADS):
                p, ps, qs, k, v = _attn_group(blk, q, kvw, sk_ref, g)
                cols = [slice((GQA * g + j) * HEAD_DIM, (GQA * g + j + 1) * HEAD_DIM) for j in range(GQA)]
                do_s = jnp.concatenate([d_o[:, c] for c in cols], axis=0).astype(BF)
                dsum = jnp.concatenate([jnp.sum(dod[:, c], axis=-1, keepdims=True) for c in cols], axis=0)
                dp = _dot_nt(do_s, v)
                dsb = (p * (dp - dsum) * ATT_SCALE).astype(BF)
                sink_part = ps * dsum
                for j in range(GQA):
                    dsk = dsk + jnp.where(lane == GQA * g + j, -_rowsum(sink_part[j * BLK:(j + 1) * BLK, :]), 0.0)
                dqs += _unstack_heads(_dot(dsb, k))
                dks.append(_dot_tn(dsb, qs))
                dvs.append(_dot_tn(p.astype(BF), do_s))
            dq_ref[sl, :] = jnp.concatenate(dqs, axis=1).astype(BF)
            dkv_ref[pl.ds(ks, 2 * BLK), :] += jnp.concatenate(dks + dvs, axis=1)
        dsk_ref[...] += dsk

        @pl.when(i == n - 1)
        def _():
            dwo_ref[...] = wacc[...].astype(BF)

    return _call(
        body, name="attn_bwd", grid=(n,),
        in_specs=[_row_spec(rows), _row_spec(rows), _row_spec(rows), _row_spec(rows), VSPEC, SSPEC, VSPEC, VSPEC],
        out_specs=[_row_spec(rows), _const_spec((s_len, 2 * KVD)), _const_spec((D, D)),
                   _const_spec((1, D)), _const_spec((1, D))],
        out_shape=[_sds((s_len, D), BF), _sds((s_len, 2 * KVD)), _sds((D, D), BF), _sds((1, D)), _sds((1, D))],
        scratch_shapes=[pltpu.VMEM((D, D), F32)],
        args=[dx4, y, attn, q, kv, sinks, w_o, post_g], rider=rider)


Big = collections.namedtuple("Big", "name src layer L A R C rb")


def _bigs():
    out = {"pool_w": Big("pool_w", "pool_w", None, 4, 4, POOL_G // N_CHIPS, POOL_G, 32)}
    for l in range(2):
        out[f"w_gu{l}"] = Big(f"w_gu{l}", "w_gu", l, 1, 2, D, FF_HALF, 256)
        out[f"w_down{l}"] = Big(f"w_down{l}", "w_down", l, 1, 4, FF // N_CHIPS, D, 352)
        out[f"w_ple_gate{l}"] = Big(f"w_ple_gate{l}", "w_ple_gate", l, 1, 4, D // N_CHIPS, D, 128)
        out[f"w_ple_proj{l}"] = Big(f"w_ple_proj{l}", "w_ple_proj", l, 1, 1, PLE, D // N_CHIPS, 128)
    out["w_q"] = Big("w_q", "w_q", None, 1, 4, D // N_CHIPS, D, 128)
    out["w_o"] = Big("w_o", "w_o", None, 1, 4, D // N_CHIPS, D, 128)
    out["w_kv"] = Big("w_kv", "w_kv", None, 1, 4, D // N_CHIPS, 2 * KVD, 128)
    return out


BIGS = _bigs()
POOL_SCALE = Big("pool_scale", "pool_scale", None, 1, 1, 1, D // N_CHIPS, 1)
BIG_SOURCES = ("w_gu", "w_down", "w_ple_gate", "w_ple_proj", "w_q", "w_o", "w_kv", "pool_w")


def _ncb(t):
    return N_CHIPS // t.A


def _full_shape(t, rows=None):
    return (t.L, t.A, t.R if rows is None else rows, _ncb(t) * t.C)


def _slot_index(t, k):
    return k // _ncb(t), k % _ncb(t)


def _slot(ref, t, k, row0, rows):
    a, cb = _slot_index(t, k)
    return ref.at[:, a, pl.ds(row0, rows), pl.ds(pl.multiple_of(cb * t.C, 128), t.C)]


def _place_job(t, w, out_dtype=BF):
    nb = next((nb for nb in (8, 4, 2, 1) if t.R % (16 * nb) == 0), 1) if t.L == 1 else 1
    rb = t.R // nb

    def fn(j, kc_ref, ins, outs):
        outs[0][...] = ins[0][...].astype(out_dtype)

    def in_map(j, kc_ref):
        return (j // nb if t.layer is None else t.layer, j % nb, 0)

    def out_map(j, kc_ref):
        a, cb = _slot_index(t, kc_ref[0])
        return (j // nb, a, j % nb, cb)

    return Job(t.L * nb, [(w, (None, rb, t.C), in_map)],
               [(_sds(_full_shape(t), out_dtype), (None, None, rb, t.C), out_map)], fn)


def _mesh_position():
    x, y, c = lax.axis_index("x"), lax.axis_index("y"), lax.axis_index("c")
    chips = [(1 - x, y), (x, 1 - y), (1 - x, 1 - y)]
    return x, y, c, chips


DIRECT_BELOW = 1024


def _gather_rider(parts, fulls):
    nt = len(parts)
    TO_X, TO_Y, FWD_X, FWD_Y, SIB_X, SIB_Y, SIB_D = range(7)

    def rows_of(ti, core):
        t, r0, r1 = parts[ti]
        h = (r1 - r0) // 2
        return r0 + core * h, h

    def copy(outs, sems, kind, ti, k_src, row0, rows, dev):
        region = _slot(outs[ti], parts[ti][0], k_src, row0, rows)
        return pltpu.make_async_remote_copy(region, region, sems[0].at[ti, kind], sems[1].at[ti, kind],
                                            device_id=dev, device_id_type=MESH)

    def plan(outs, sems):
        x, y, c, _ = _mesh_position()
        me, kx, ky, kd = 2 * x + y, 2 * (1 - x) + y, 2 * x + (1 - y), 2 * (1 - x) + (1 - y)
        dev_x, dev_y, dev_d, sib = (1 - x, y, c), (x, 1 - y, c), (1 - x, 1 - y, c), (x, y, 1 - c)

        def whole(ti):
            return 0, parts[ti][0].R

        def mk(kind, k_send, k_recv, dev, send_rows, recv_rows):
            def build(ti, side):
                k_src = k_send if side == "s" else k_recv
                row0, rows = (send_rows if side == "s" else recv_rows)(ti)
                return copy(outs, sems, kind, ti, k_src, row0, rows, dev)
            return build

        def first_half(core):
            return lambda ti: (rows_of(ti, core)[0], rows_of(ti, core)[1] // 2)

        def second_half(core):
            return lambda ti: (rows_of(ti, core)[0] + rows_of(ti, core)[1] // 2, rows_of(ti, core)[1] // 2)

        mine = lambda ti: rows_of(ti, c)
        theirs = lambda ti: rows_of(ti, 1 - c)
        split = {
            TO_X: mk(TO_X, me, kx, dev_x, mine, mine),
            TO_Y: mk(TO_Y, me, ky, dev_y, mine, mine),
            FWD_X: mk(FWD_X, ky, kd, dev_x, first_half(c), first_half(c)),
            FWD_Y: mk(FWD_Y, kx, kd, dev_y, second_half(c), second_half(c)),
            SIB_X: mk(SIB_X, kx, kx, sib, mine, theirs),
            SIB_Y: mk(SIB_Y, ky, ky, sib, mine, theirs),
            SIB_D: mk(SIB_D, kd, kd, sib, mine, theirs),
        }
        direct = {
            TO_X: mk(TO_X, me, kx, dev_x, whole, whole),
            TO_Y: mk(TO_Y, me, ky, dev_y, whole, whole),
            FWD_X: mk(FWD_X, me, kd, dev_d, whole, whole),
        }
        return split, direct

    is_split = [t.L * t.R * t.C >= DIRECT_BELOW for t, _, _ in parts]
    assert all(s or (r0, r1) == (0, t.R) for s, (t, r0, r1) in zip(is_split, parts))

    def start(ins, outs, sems):
        split, direct = plan(outs, sems)
        for ti in range(nt):
            kinds = split if is_split[ti] else direct
            kinds[TO_X](ti, "s").start()
            kinds[TO_Y](ti, "s").start()
            if not is_split[ti]:
                kinds[FWD_X](ti, "s").start()

    def mid(ins, outs, sems):
        split, _ = plan(outs, sems)
        for ti in range(nt):
            if is_split[ti]:
                split[TO_Y](ti, "r").wait_recv()
                split[FWD_X](ti, "s").start()
                split[SIB_Y](ti, "s").start()
        for ti in range(nt):
            if is_split[ti]:
                split[TO_X](ti, "r").wait_recv()
                split[FWD_Y](ti, "s").start()
                split[SIB_X](ti, "s").start()

    def finish(ins, outs, sems):
        split, direct = plan(outs, sems)
        for ti in range(nt):
            if is_split[ti]:
                split[FWD_X](ti, "r").wait_recv()
                split[FWD_Y](ti, "r").wait_recv()
                split[SIB_D](ti, "s").start()
            else:
                for kind in (TO_X, TO_Y, FWD_X):
                    direct[kind](ti, "r").wait_recv()
        for ti in range(nt):
            if is_split[ti]:
                for kind in (SIB_X, SIB_Y, SIB_D):
                    split[kind](ti, "r").wait_recv()
        for ti in range(nt):
            kinds = split if is_split[ti] else direct
            for kind in kinds:
                kinds[kind](ti, "s").wait_send()

    sems = pltpu.SemaphoreType.DMA((nt, 7))
    return Rider(list(fulls), [_sds(a.shape, a.dtype) for a in fulls], {i: i for i in range(nt)},
                 [sems, sems], start, mid, finish)


def _pair_exchange_rider(specs, grads):
    nt = len(specs)

    def copy(ins, outs, sems, ti, c, sibling):
        half = specs[ti].R // 2
        return pltpu.make_async_remote_copy(ins[ti].at[:, :, pl.ds((1 - c) * half, half), :], outs[ti],
                                            sems[0].at[ti], sems[1].at[ti], device_id=sibling, device_id_type=MESH)

    def start(ins, outs, sems):
        x, y, c, _ = _mesh_position()
        for ti in range(nt):
            copy(ins, outs, sems, ti, c, (x, y, 1 - c)).start()

    def finish(ins, outs, sems):
        x, y, c, _ = _mesh_position()
        for ti in range(nt):
            copy(ins, outs, sems, ti, c, (x, y, 1 - c)).wait()

    sems = pltpu.SemaphoreType.DMA((nt,))
    return Rider(list(grads), [_sds(_full_shape(t, t.R // 2), BF) for t in specs], {}, [sems, sems], start, None, finish)


def _pair_sum_job(t, g, land):
    assert t.L == 1
    half = t.R // 2
    nj = half // t.rb
    block = (None, t.A, t.rb, _ncb(t) * t.C)

    def fn(j, kc_ref, ins, outs):
        outs[0][...] = (ins[0][...].astype(F32) + ins[1][...].astype(F32)).astype(BF)

    return Job(nj,
               [(g, block, lambda j, kc_ref: (0, 0, kc_ref[1] * nj + j, 0)),
                (land, block, lambda j, kc_ref: (0, 0, j, 0))],
               [(_sds(_full_shape(t, half), BF), block, lambda j, kc_ref: (0, 0, j, 0))], fn)


def _scatter_rider(specs, sums):
    nt = len(specs)

    def copy(ins, outs, sems, ti, j, chip, c):
        t = specs[ti]
        cx, cy = chip
        return pltpu.make_async_remote_copy(_slot(ins[ti], t, 2 * cx + cy, 0, t.R // 2), outs[ti].at[j],
                                            sems[0].at[ti, j], sems[1].at[ti, j],
                                            device_id=(cx, cy, c), device_id_type=MESH)

    def start(ins, outs, sems):
        _, _, c, chips = _mesh_position()
        for j, chip in enumerate(chips):
            for ti in range(nt):
                copy(ins, outs, sems, ti, j, chip, c).start()

    def finish(ins, outs, sems):
        _, _, c, chips = _mesh_position()
        for j, chip in enumerate(chips):
            for ti in range(nt):
                copy(ins, outs, sems, ti, j, chip, c).wait()

    sems = pltpu.SemaphoreType.DMA((nt, N_CHIPS - 1))
    return Rider(list(sums), [_sds((N_CHIPS - 1, t.L, t.R // 2, t.C), BF) for t in specs], {}, [sems, sems],
                 start, None, finish)


def _chip_sum_job(ts, landed):
    t0 = ts[0]
    assert t0.L == 1
    half = t0.R // 2
    nj = half // t0.rb

    def local(j, li):
        return jnp.clip(j - li * nj, 0, nj - 1)

    ins = []
    for li, t in enumerate(ts):
        s, land = landed[t.name]

        def own_map(j, kc_ref, li=li, t=t):
            a, cb = _slot_index(t, kc_ref[0])
            return (0, a, local(j, li), cb)

        ins.append((s, (None, None, t.rb, t.C), own_map))
        ins.append((land, (N_CHIPS - 1, None, t.rb, t.C), lambda j, kc_ref, li=li: (0, 0, local(j, li), 0)))

    def fn(j, kc_ref, in_refs, outs):
        for li in range(len(ts)):
            @pl.when(j // nj == li)
            def _():
                acc = in_refs[2 * li][...].astype(F32)
                for k in range(N_CHIPS - 1):
                    acc = acc + in_refs[2 * li + 1][k].astype(F32)
                outs[0][...] = acc

    return Job(len(ts) * nj, ins,
               [(_sds((len(ts), t0.R, t0.C)), (None, t0.rb, t0.C),
                 lambda j, kc_ref: (j // nj, kc_ref[1] * nj + j % nj, 0))], fn)


def _adamw_job(rb, w, g, m, v):
    n_layers, r, c = w.shape
    nb = r // rb
    block = (None, rb, c)
    index = lambda j, kc_ref: (j // nb, j % nb, 0)

    def fn(j, kc_ref, ins, outs):
        g_v = ins[1][...]
        outs[0][...] = g_v
        outs[1][...], outs[2][...], outs[3][...] = _adamw_math(ins[0][...], g_v, ins[2][...], ins[3][...])

    return Job(n_layers * nb, [(a, block, index) for a in (w, g, m, v)],
               [(_sds(w.shape), block, index)] * 4, fn)


def _chip_sum_fused_job(ts, fused, by_cols):
    t0 = ts[0]
    own0 = fused[t0.name][0]
    if by_cols:
        rows, cols = own0.shape
    else:
        nb, rows, bw = own0.shape
        cols = nb * bw
    nj = rows // t0.rb

    def local(j, li):
        return jnp.clip(j - li * nj, 0, nj - 1)

    ins = []
    for li, t in enumerate(ts):
        own, land = fused[t.name]
        if by_cols:
            ins.append((own, (t.rb, cols), lambda j, kc_ref, li=li: (local(j, li), 0)))
            ins.append((land, (N_CHIPS - 1, t.rb, cols), lambda j, kc_ref, li=li: (0, local(j, li), 0)))
        else:
            ins.append((own, (nb, t.rb, bw), lambda j, kc_ref, li=li: (0, local(j, li), 0)))
            ins.append((land, (N_CHIPS - 1, nb, t.rb, bw), lambda j, kc_ref, li=li: (0, 0, local(j, li), 0)))

    def fn(j, kc_ref, in_refs, outs):
        for li in range(len(ts)):
            @pl.when(j // nj == li)
            def _():
                acc = in_refs[2 * li][...].astype(F32)
                for k in range(N_CHIPS - 1):
                    acc = acc + in_refs[2 * li + 1][k].astype(F32)
                outs[0][...] = acc if by_cols else jnp.concatenate([acc[b] for b in range(nb)], axis=1)

    def out_map(j, kc_ref):
        return (j // nj, j % nj, kc_ref[1]) if by_cols else (j // nj, kc_ref[1] * nj + j % nj, 0)

    return Job(len(ts) * nj, ins, [(_sds((len(ts), t0.R, t0.C)), (None, t0.rb, cols), out_map)], fn)


def _share_rider(halves, by_cols):
    nt = len(halves)

    def copy(outs, sems, ti, core, sibling):
        axis = 2 if by_cols[ti] else 1
        half = halves[ti].shape[axis] // 2
        piece = pl.ds(pl.multiple_of(core * half, 128 if by_cols[ti] else 8), half)
        part = outs[ti].at[:, :, piece] if by_cols[ti] else outs[ti].at[:, piece, :]
        return pltpu.make_async_remote_copy(part, part, sems[0].at[ti], sems[1].at[ti],
                                            device_id=sibling, device_id_type=MESH)

    def start(ins, outs, sems):
        x, y, c, _ = _mesh_position()
        for ti in range(nt):
            copy(outs, sems, ti, c, (x, y, 1 - c)).start()

    def finish(ins, outs, sems):
        x, y, c, _ = _mesh_position()
        for ti in range(nt):
            copy(outs, sems, ti, 1 - c, (x, y, 1 - c)).wait_recv()
        for ti in range(nt):
            copy(outs, sems, ti, c, (x, y, 1 - c)).wait_send()

    sems = pltpu.SemaphoreType.DMA((nt,))
    return Rider(list(halves), [_sds(a.shape, a.dtype) for a in halves], {i: i for i in range(nt)}, [sems, sems],
                 start, None, finish)


def _both(r1, r2):
    assert r1.mid is None and r2.mid is None
    ni, no, ns = len(r1.arrays), len(r1.out_shapes), len(r1.scratch)

    def split(fn1, fn2):
        def run(ins, outs, scr):
            fn1(ins[:ni], outs[:no], scr[:ns])
            fn2(ins[ni:], outs[no:], scr[ns:])
        return run

    aliases = dict(r1.aliases)
    aliases.update({ni + a: no + b for a, b in r2.aliases.items()})
    return Rider(r1.arrays + r2.arrays, r1.out_shapes + r2.out_shapes, aliases, r1.scratch + r2.scratch,
                 split(r1.start, r2.start), None, split(r1.finish, r2.finish))


def _adamw_math(w, g, m, v):
    m = B1 * m + (1.0 - B1) * g
    v = B2 * v + (1.0 - B2) * (g * g)
    delta = -LR * ((m / BC1) / (jnp.sqrt(v / BC2) + AEPS) + WD * w)
    return delta, m, v


GAIN_ROWS = {"pre_mix_g": 0, "post_mix_g": 2, "pre_ffn_g": 4, "post_ffn_g": 6, "ple_g": 8, "ple_post_g": 10}
ROW_KV_G, ROW_POOL_SCALE, ROW_SINKS, ROW_LOSS, PACK_ROWS = 12, 13, 14, 15, 16
SMALL_NAMES = tuple(GAIN_ROWS) + ("kv_g", "pool_scale", "sinks")


def _small_all_reduce(rows, dpool, rider=None):
    ng, pr = len(WINDOWS), POOL_G // N_CHIPS

    def body(*refs):
        row_refs = refs[:PACK_ROWS]
        dpool_ref, tot_ref, gpool_ref, pack, land, pland, send, recv, psend, precv = refs[PACK_ROWS:]
        x, y, c, _ = _mesh_position()
        me = 4 * x + 2 * y + c
        for r in range(PACK_ROWS):
            pack[r:r + 1, :] = row_refs[r][...]

        def shard_of(k):
            return dpool_ref.at[:, pl.ds(pl.multiple_of(k * pr, pr), pr), :]

        cps = []
        for j in range(1, N_DEV):
            px, py, pc = x ^ (j >> 2), y ^ ((j >> 1) & 1), c ^ (j & 1)
            cps.append(pltpu.make_async_remote_copy(pack, land.at[me], send.at[j], recv.at[j],
                                                    device_id=(px, py, pc), device_id_type=MESH))
            cps.append(pltpu.make_async_remote_copy(shard_of(2 * px + py), pland.at[me], psend.at[j], precv.at[j],
                                                    device_id=(px, py, pc), device_id_type=MESH))
        for cp in cps:
            cp.start()
        land[me] = pack[...]
        pland[me] = dpool_ref[:, pl.ds(pl.multiple_of((2 * x + y) * pr, pr), pr), :]
        for j in range(1, N_DEV):
            pltpu.make_async_remote_copy(pack, land.at[me ^ j], send.at[j], recv.at[j],
                                         device_id=(x, y, c), device_id_type=MESH).wait_recv()
            pltpu.make_async_remote_copy(shard_of(0), pland.at[me ^ j], psend.at[j], precv.at[j],
                                         device_id=(x, y, c), device_id_type=MESH).wait_recv()
        for cp in cps:
            cp.wait_send()
        tot = land[0]
        gp = pland[0].astype(F32)
        for d in range(1, N_DEV):
            tot = tot + land[d]
            gp = gp + pland[d].astype(F32)
        tot_ref[...] = tot
        gpool_ref[...] = gp

    sems = pltpu.SemaphoreType.DMA((N_DEV,))
    return _call(
        body, name="small_all_reduce", grid=(1,),
        in_specs=[VSPEC] * (PACK_ROWS + 1), out_specs=[VSPEC, VSPEC],
        out_shape=[_sds((PACK_ROWS, D)), _sds((ng, pr, POOL_G))],
        scratch_shapes=[pltpu.VMEM((PACK_ROWS, D), F32), pltpu.VMEM((N_DEV, PACK_ROWS, D), F32),
                        pltpu.VMEM((N_DEV, ng, pr, POOL_G), BF), sems, sems, sems, sems],
        args=[*rows, dpool], rider=rider)


def _small_adamw(tot, kc, small_w, small_m, small_v):
    names = SMALL_NAMES
    n = len(names)

    def body(*refs):
        tot_ref, kc_ref = refs[0], refs[1]
        w_refs = dict(zip(names, refs[2:2 + n]))
        m_refs = dict(zip(names, refs[2 + n:2 + 2 * n]))
        v_refs = dict(zip(names, refs[2 + 2 * n:2 + 3 * n]))
        loss_ref = refs[2 + 3 * n]
        out_refs = {nm: refs[3 + 3 * n + 4 * k: 7 + 3 * n + 4 * k] for k, nm in enumerate(names)}
        tot = tot_ref[...]
        loss_ref[...] = 0.5 * jnp.sum(tot[ROW_LOSS:ROW_LOSS + 1, :], axis=-1, keepdims=True) * (1.0 / D)

        def update(nm, g):
            g_ref, d_ref, nm_ref, nv_ref = out_refs[nm]
            g_ref[...] = g
            d_ref[...], nm_ref[...], nv_ref[...] = _adamw_math(w_refs[nm][...], g, m_refs[nm][...], v_refs[nm][...])

        for nm, r in GAIN_ROWS.items():
            update(nm, tot[r:r + 2, :])
        update("kv_g", tot[ROW_KV_G:ROW_KV_G + 1, :])
        k = kc_ref[0]
        width = D // N_CHIPS
        g_scale = jnp.zeros((1, width), F32)
        for kk in range(N_CHIPS):
            g_scale = g_scale + jnp.where(k == kk, tot[ROW_POOL_SCALE:ROW_POOL_SCALE + 1, kk * width:(kk + 1) * width], 0.0)
        update("pool_scale", g_scale)
        update("sinks", tot[ROW_SINKS:ROW_SINKS + 1, 0:N_HEADS])

    ins = [tot, kc] + [small_w[nm] for nm in names] + [small_m[nm] for nm in names] + [small_v[nm] for nm in names]
    out_shape = [_sds((1, 1))]
    for nm in names:
        out_shape += [_sds(small_w[nm].shape)] * 4
    outs = pl.pallas_call(
        body, name="small_adamw",
        in_specs=[VSPEC, SSPEC] + [VSPEC] * (3 * n), out_specs=[VSPEC] * len(out_shape), out_shape=out_shape,
        compiler_params=_params(),
    )(*ins)
    return outs[0], {nm: outs[1 + 4 * k: 5 + 4 * k] for k, nm in enumerate(names)}


def _compute_layout(t, full):
    if t.src == "w_gu":
        return full.reshape(2, D, FF)
    if t.src == "pool_w":
        return full.reshape(len(WINDOWS), POOL_G, POOL_G)
    if t.src == "pool_scale":
        return full.reshape(1, D)
    return full.reshape(t.A * t.R, _ncb(t) * t.C)


def kernel(x, p, pre_mix_g, post_mix_g, pre_ffn_g, post_ffn_g, pool_w, pool_scale, kv_g, w_kv, w_q, sinks, w_o, w_gu, w_down, ple_g, w_ple_gate, w_ple_proj, ple_post_g, loss_target, m_pre_mix_g, m_post_mix_g, m_pre_ffn_g, m_post_ffn_g, m_pool_w, m_pool_scale, m_kv_g, m_w_kv, m_w_q, m_sinks, m_w_o, m_w_gu, m_w_down, m_ple_g, m_w_ple_gate, m_w_ple_proj, m_ple_post_g, v_pre_mix_g, v_post_mix_g, v_pre_ffn_g, v_post_ffn_g, v_pool_w, v_pool_scale, v_kv_g, v_w_kv, v_w_q, v_sinks, v_w_o, v_w_gu, v_w_down, v_ple_g, v_w_ple_gate, v_w_ple_proj, v_ple_post_g):
    weights = dict(pre_mix_g=pre_mix_g, post_mix_g=post_mix_g, pre_ffn_g=pre_ffn_g, post_ffn_g=post_ffn_g,
                   pool_w=pool_w, pool_scale=pool_scale, kv_g=kv_g, w_kv=w_kv, w_q=w_q, sinks=sinks, w_o=w_o,
                   w_gu=w_gu, w_down=w_down, ple_g=ple_g, w_ple_gate=w_ple_gate, w_ple_proj=w_ple_proj,
                   ple_post_g=ple_post_g)
    m_in = dict(pre_mix_g=m_pre_mix_g, post_mix_g=m_post_mix_g, pre_ffn_g=m_pre_ffn_g, post_ffn_g=m_post_ffn_g,
                pool_w=m_pool_w, pool_scale=m_pool_scale, kv_g=m_kv_g, w_kv=m_w_kv, w_q=m_w_q, sinks=m_sinks,
                w_o=m_w_o, w_gu=m_w_gu, w_down=m_w_down, ple_g=m_ple_g, w_ple_gate=m_w_ple_gate,
                w_ple_proj=m_w_ple_proj, ple_post_g=m_ple_post_g)
    v_in = dict(pre_mix_g=v_pre_mix_g, post_mix_g=v_post_mix_g, pre_ffn_g=v_pre_ffn_g, post_ffn_g=v_post_ffn_g,
                pool_w=v_pool_w, pool_scale=v_pool_scale, kv_g=v_kv_g, w_kv=v_w_kv, w_q=v_w_q, sinks=v_sinks,
                w_o=v_w_o, w_gu=v_w_gu, w_down=v_w_down, ple_g=v_ple_g, w_ple_gate=v_w_ple_gate,
                w_ple_proj=v_w_ple_proj, ple_post_g=v_ple_post_g)
    order = ["pre_mix_g", "post_mix_g", "pre_ffn_g", "post_ffn_g", "pool_w", "pool_scale", "kv_g", "w_kv", "w_q",
             "sinks", "w_o", "w_gu", "w_down", "ple_g", "w_ple_gate", "w_ple_proj", "ple_post_g"]

    kc = jnp.stack([2 * lax.axis_index("x") + lax.axis_index("y"), lax.axis_index("c")]).astype(jnp.int32)
    s_len = x.shape[1]
    x2d = x.reshape(s_len, D)
    p3d = p.reshape(2, s_len, PLE)
    target = loss_target.reshape(s_len, D)
    kv_g2d = kv_g.reshape(1, D)
    gains = {nm: weights[nm] for nm in GAIN_ROWS}

    def shard_view(src, a):
        t = next(t for t in BIGS.values() if t.src == src)
        return a.reshape(-1, t.R, t.C)

    first, second = ["pool_w", "pool_scale"], ["w_gu0", "w_down0"]
    rest = [nm for nm in BIGS if nm not in first + second]
    specs = dict(BIGS, pool_scale=POOL_SCALE)
    placed = {}

    def place_job(nm):
        if nm == "pool_scale":
            return _place_job(POOL_SCALE, pool_scale.reshape(1, 1, D // N_CHIPS), F32)
        return _place_job(BIGS[nm], shard_view(BIGS[nm].src, weights[BIGS[nm].src]))

    def gather(names, rows=None):
        rows = rows or {}
        parts = [(specs[nm],) + tuple(rows.get(nm, (0, specs[nm].R))) for nm in names]
        return _gather_rider(parts, [placed[nm] for nm in names])

    def take(names, results):
        for nm, a in zip(names, results):
            placed[nm] = a

    def weight(nm):
        return _compute_layout(specs[nm], placed[nm])

    take(first, [r[0] for r in _multi_call("place_pool", [place_job(nm) for nm in first], kc)])
    cast, got = _multi_call("place_ffn0", [place_job(nm) for nm in second], kc, rider=gather(first))
    take(second, [r[0] for r in cast])
    take(first, got)
    jobs = [place_job(nm) for nm in rest]
    jobs.append(_mixa_fwd_job(x2d, gains["pre_mix_g"], weight("pool_w"), weight("pool_scale"), gains["post_mix_g"]))
    results, got = _multi_call("cast_and_mixa_fwd", jobs, kc, rider=gather(second))
    take(rest, [r[0] for r in results[:-1]])
    take(second, got)
    y0, x1 = results[-1]

    ride = ["w_ple_gate0", "w_ple_proj0", "w_q", "w_kv", "w_o", "w_gu1"]
    (f0, x2, g0, u0), got = _ffn_fwd(0, x1, gains["pre_ffn_g"], weight("w_gu0"), weight("w_down0"), gains["post_ffn_g"],
                             rider=gather(ride, {"w_gu1": (0, 320)}))
    take(ride, got)

    ride = ["w_ple_gate1", "w_ple_proj1", "w_gu1"]
    (z0, pe0, x3, q, kv), got = _ple_fwd(
        0, x2, p3d, gains["ple_g"], weight("w_ple_gate0"), weight("w_ple_proj0"), gains["ple_post_g"],
        qkv=(gains["pre_mix_g"], kv_g2d, weight("w_q"), weight("w_kv")),
        rider=gather(ride, {"w_gu1": (320, 704)}))
    take(ride, got)

    ride = ["w_down1", "w_gu1"]
    (attn, y1, x4), got = _attn_fwd(q, kv, sinks, x3, weight("w_o"), gains["post_mix_g"],
                                    rider=gather(ride, {"w_gu1": (704, D)}))
    take(ride, got)

    local = {}
    (f1, g1, u1, dx5, local["w_ple_gate1"], local["w_ple_proj1"], d_ple1, d_plepost1, loss_row), _ = _ffn_fwd(
        1, x4, gains["pre_ffn_g"], weight("w_gu1"), weight("w_down1"), gains["post_ffn_g"],
        head=(p3d, gains["ple_g"], weight("w_ple_gate1"), weight("w_ple_proj1"), gains["ple_post_g"], target))

    landed = {}
    fused = {}

    def local_grads(names):
        return [local[nm].reshape(_full_shape(BIGS[nm])) for nm in names]

    def pair_exchange(names):
        return _pair_exchange_rider([BIGS[nm] for nm in names], local_grads(names))

    def pair_sum(tag, names, lands):
        jobs = [_pair_sum_job(BIGS[nm], g, l) for nm, g, l in zip(names, local_grads(names), lands)]
        return [r[0] for r in _multi_call(f"pair_sum_{tag}", jobs, kc)]

    def scatter(names, sums):
        return _scatter_rider([BIGS[nm] for nm in names], sums)

    def keep(names, sums, got):
        for nm, s, l in zip(names, sums, got):
            landed[nm] = (s, l)

    group_a = ["w_ple_gate1", "w_ple_proj1"]
    (dx4, d_preffn1, d_postffn1, *scattered), lands_a = _ffn_bwd(
        1, dx5, x4, f1, g1, u1, gains["pre_ffn_g"], weight("w_gu1"), weight("w_down1"), gains["post_ffn_g"], kc,
        rider=pair_exchange(group_a))
    fused["w_gu1"], fused["w_down1"] = scattered[0:2], scattered[2:4]

    (dq, dkv, local["w_o"], d_postmix1, d_sinks), _ = _attn_bwd(
        dx4, y1, attn, q, kv, sinks, weight("w_o"), gains["post_mix_g"])
    dx3, local["w_q"], local["w_kv"], d_premix1, d_kvg = _qkv_bwd(
        dq, dkv, x3, dx4, gains["pre_mix_g"], kv_g2d, weight("w_q"), weight("w_kv"))

    group_b = ["w_o", "w_q", "w_kv"]
    (dx2, local["w_ple_gate0"], local["w_ple_proj0"], d_ple0, d_plepost0), lands_b = _ple_bwd(
        0, dx3, x2, z0, pe0, p3d, gains["ple_g"], weight("w_ple_gate0"), gains["ple_post_g"],
        rider=pair_exchange(group_b))
    group_ab = group_a + group_b
    sums_ab = pair_sum("ab", group_ab, lands_a + lands_b)

    group_c = ["w_ple_gate0", "w_ple_proj0"]
    (dx1, d_preffn0, d_postffn0, *scattered), got = _ffn_bwd(
        0, dx2, x1, f0, g0, u0, gains["pre_ffn_g"], weight("w_gu0"), weight("w_down0"), gains["post_ffn_g"], kc,
        rider=_both(pair_exchange(group_c), scatter(group_ab, sums_ab)))
    fused["w_gu0"], fused["w_down0"] = scattered[0:2], scattered[2:4]
    lands_c = got[:len(group_c)]
    keep(group_ab, sums_ab, got[len(group_c):])

    layers_of = lambda src: [t for t in BIGS.values() if t.src == src]
    own_scatter = ["w_gu", "w_down"]
    early = own_scatter + ["w_q", "w_o", "w_kv"]
    late = ["w_ple_gate", "w_ple_proj"]
    by_cols = lambda srcs: [src == "w_down" for src in srcs]
    jobs = [_chip_sum_fused_job(layers_of(src), fused, by_cols=src == "w_down") for src in own_scatter]
    jobs += [_chip_sum_job(layers_of(src), landed) for src in early if src not in own_scatter]
    jobs_c = [_pair_sum_job(BIGS[nm], g, l) for nm, g, l in zip(group_c, local_grads(group_c), lands_c)]
    sums = [r[0] for r in _multi_call("chip_sum_early", jobs + jobs_c, kc)]
    halves, sums_c = sums[:len(jobs)], sums[len(jobs):]
    (dx0, d_pool, d_scale, d_postmix0, d_premix0), got = _mixa_bwd(
        dx1, x2d, y0, gains["pre_mix_g"], weight("pool_w"), weight("pool_scale"), gains["post_mix_g"],
        rider=_both(scatter(group_c, sums_c), _share_rider(halves, by_cols(early))))
    keep(group_c, sums_c, got[:len(group_c)])
    full_grads = dict(zip(early, got[len(group_c):]))

    rows = [d_premix0, d_premix1, d_postmix0, d_postmix1, d_preffn0, d_preffn1, d_postffn0, d_postffn1,
            d_ple0, d_ple1, d_plepost0, d_plepost1, d_kvg, d_scale, d_sinks, loss_row]
    as2d = lambda a: a.reshape(1, D) if a.ndim == 1 else a
    halves = [r[0] for r in _multi_call("chip_sum_late", [_chip_sum_job(layers_of(src), landed) for src in late], kc)]
    (tot, g_pool), got = _small_all_reduce(rows, d_pool, rider=_share_rider(halves, by_cols(late)))
    full_grads.update(zip(late, got))
    full_grads["pool_w"] = g_pool
    loss, small = _small_adamw(tot, kc, {nm: as2d(weights[nm]) for nm in SMALL_NAMES},
                               {nm: as2d(m_in[nm]) for nm in SMALL_NAMES},
                               {nm: as2d(v_in[nm]) for nm in SMALL_NAMES})


    def adam_job(src):
        rb = layers_of(src)[0].rb // (1 if src in ("pool_w", "w_gu") else 4)
        return _adamw_job(rb, shard_view(src, weights[src]), full_grads[src],
                          shard_view(src, m_in[src]), shard_view(src, v_in[src]))

    out = {"grad": {}, "delta": {}, "new_m": {}, "new_v": {}}
    results = dict(zip(BIG_SOURCES, _multi_call("adamw", [adam_job(src) for src in BIG_SOURCES], kc)))
    for src in BIG_SOURCES:
        shape = weights[src].shape
        for kind, a in zip(("grad", "delta", "new_m", "new_v"), results[src]):
            out[kind][src] = a.reshape(shape)
    for nm in SMALL_NAMES:
        shape = weights[nm].shape
        for kind, a in zip(("grad", "delta", "new_m", "new_v"), small[nm]):
            out[kind][nm] = a.reshape(shape)

    return (loss.reshape(()), dx0.reshape(x.shape),
            *[out["grad"][nm] for nm in order], *[out["delta"][nm] for nm in order],
            *[out["new_m"][nm] for nm in order], *[out["new_v"][nm] for nm in order])
```

```python
import collections

import jax
import jax.numpy as jnp
from jax import lax
from jax.experimental import pallas as pl
from jax.experimental.pallas import tpu as pltpu

D = 1024
FF = 2816
N_HEADS = 16
HEAD_DIM = 64
N_KV_HEADS = 4
GQA = N_HEADS // N_KV_HEADS
KVD = N_KV_HEADS * HEAD_DIM
PLE = 256
BLK = 128
WINDOWS = (2, 4, 8, 16)
POOL_G = 256
HALO = 16
EPS = 1e-6
NEG_INF = -1e30
ATT_SCALE = HEAD_DIM ** -0.5
SLOPES = tuple(2.0 ** (-8.0 * (h + 1) / N_HEADS) for h in range(N_HEADS))
N_CHIPS = 4
N_DEV = 8

LR, B1, B2, AEPS, WD, STEP = 0.001, 0.9, 0.999, 1e-08, 0.01, 10
BC1 = 1.0 - B1 ** STEP
BC2 = 1.0 - B2 ** STEP

BF = jnp.bfloat16
F32 = jnp.float32
MESH = pl.DeviceIdType.MESH
VMEM_LIMIT_V7X = 58 * 1024 * 1024
TM = 256
TM_FFN_BWD = 512
FF_CHUNK = 256
FF_HALF = FF // 2

VSPEC = pl.BlockSpec(memory_space=pltpu.VMEM)
SSPEC = pl.BlockSpec(memory_space=pltpu.SMEM)
ANYSPEC = pl.BlockSpec(memory_space=pl.ANY)


def _params(n_grid=0):
    sem = ("arbitrary",) * n_grid if n_grid else None
    return pltpu.CompilerParams(dimension_semantics=sem, vmem_limit_bytes=VMEM_LIMIT_V7X)


def _sds(shape, dtype=F32):
    return jax.ShapeDtypeStruct(tuple(shape), dtype)


Rider = collections.namedtuple("Rider", "arrays out_shapes aliases scratch start mid finish")
MID_NUM, MID_DEN = 5, 8


def _call(body, *, name, grid, in_specs, out_specs, out_shape, args, scratch_shapes=(), rider=None, prefetch=None):
    ni, no, ns = len(in_specs), len(out_specs), len(scratch_shapes)
    npre = 0 if prefetch is None else 1
    pre = [] if prefetch is None else [prefetch]
    if rider is None:
        rider = Rider([], [], {}, [], None, None, None)
    ri, ro = len(rider.arrays), len(rider.out_shapes)

    def full(*refs):
        pre_refs, refs = refs[:npre], refs[npre:]
        ins, refs = refs[:ni], refs[ni:]
        rins, refs = refs[:ri], refs[ri:]
        outs, refs = refs[:no], refs[no:]
        routs, refs = refs[:ro], refs[ro:]
        scr, rscr = refs[:ns], refs[ns:]
        ids = [pl.program_id(a) for a in range(len(grid))]
        first = ids[0] == 0
        last = ids[0] == grid[0] - 1
        for a in range(1, len(grid)):
            first = first & (ids[a] == 0)
            last = last & (ids[a] == grid[a] - 1)

        if rider.start is not None:
            @pl.when(first)
            def _():
                rider.start(rins, routs, rscr)

        if rider.mid is not None:
            assert len(grid) == 1

            @pl.when(ids[0] == (grid[0] * MID_NUM) // MID_DEN)
            def _():
                rider.mid(rins, routs, rscr)

        body(*pre_refs, *ins, *outs, *scr)

        if rider.finish is not None:
            @pl.when(last)
            def _():
                rider.finish(rins, routs, rscr)

    outs = pl.pallas_call(
        full, name=name,
        grid_spec=pltpu.PrefetchScalarGridSpec(
            num_scalar_prefetch=npre, grid=grid,
            in_specs=list(in_specs) + [ANYSPEC] * ri, out_specs=list(out_specs) + [ANYSPEC] * ro,
            scratch_shapes=list(scratch_shapes) + list(rider.scratch)),
        out_shape=list(out_shape) + list(rider.out_shapes),
        input_output_aliases={npre + ni + a: no + b for a, b in rider.aliases.items()},
        compiler_params=_params(len(grid)))(*pre, *args, *rider.arrays)
    return list(outs[:no]), list(outs[no:])


Job = collections.namedtuple("Job", "steps ins outs fn")


def _multi_call(name, jobs, kc, rider=None):
    n = max(job.steps for job in jobs)

    def clamped(index, steps):
        return lambda s, kc_ref: index(jnp.minimum(s, steps - 1), kc_ref)

    in_specs, out_specs, out_shape, args = [], [], [], []
    for job in jobs:
        for arr, block, index, *single in job.ins:
            mode = dict(pipeline_mode=pl.Buffered(1)) if single and single[0] else {}
            in_specs.append(pl.BlockSpec(block, clamped(index, job.steps), **mode))
            args.append(arr)
        for sds, block, index in job.outs:
            out_specs.append(pl.BlockSpec(block, clamped(index, job.steps)))
            out_shape.append(sds)
    n_in = len(args)

    def body(kc_ref, *refs):
        s = pl.program_id(0)
        i0, o0 = 0, n_in
        for job in jobs:
            ins, outs = refs[i0:i0 + len(job.ins)], refs[o0:o0 + len(job.outs)]
            i0, o0 = i0 + len(job.ins), o0 + len(job.outs)

            @pl.when(s < job.steps)
            def _():
                job.fn(s, kc_ref, ins, outs)

    outs, routs = _call(body, name=name, grid=(n,), in_specs=in_specs, out_specs=out_specs, out_shape=out_shape,
                        args=args, prefetch=kc, rider=rider)
    res, o0 = [], 0
    for job in jobs:
        res.append(outs[o0:o0 + len(job.outs)])
        o0 += len(job.outs)
    return res if rider is None else (res, routs)


def _rms_fwd(x, g):
    r = lax.rsqrt(jnp.mean(x * x, axis=-1, keepdims=True) + EPS)
    return x * r * g


def _rms_bwd(x, g, dy):
    r = lax.rsqrt(jnp.mean(x * x, axis=-1, keepdims=True) + EPS)
    xn = x * r
    dxn = dy * g
    dx = r * (dxn - xn * jnp.mean(dxn * xn, axis=-1, keepdims=True))
    return dx, dy * xn


def _rowsum(a):
    return jnp.sum(a, axis=0, keepdims=True)


def _sigmoid(z):
    return 1.0 / (1.0 + jnp.exp(-z))


def _dot(a, b):
    return jnp.dot(a, b, preferred_element_type=F32)


def _dot_nt(a, b):
    return lax.dot_general(a, b, (((1,), (1,)), ((), ())), preferred_element_type=F32)


def _dot_tn(a, b):
    return lax.dot_general(a, b, (((0,), (0,)), ((), ())), preferred_element_type=F32)


def _row_spec(tm, width=D):
    return pl.BlockSpec((tm, width), lambda i: (i, 0))


def _const_spec(shape):
    zeros = (0,) * len(shape)
    return pl.BlockSpec(tuple(shape), lambda *_: zeros)


def _pool_delta(he, pos):
    out = []
    for gi, w in enumerate(WINDOWS):
        hg = he[:, gi * POOL_G:(gi + 1) * POOL_G]
        s = hg
        k = 1
        while k < w:
            s = s + pltpu.roll(s, k, 0)
            k *= 2
        cnt = jnp.maximum(jnp.minimum(pos + 1, w), 1).astype(F32)
        out.append(s / cnt - hg)
    return out


def _load_with_halo_before(x_ref, i, tm):
    r0 = pl.multiple_of(i * tm, tm)
    hs = pl.multiple_of(jnp.maximum(i * tm - HALO, 0), 8)
    xh = jnp.where(i > 0, x_ref[pl.ds(hs, HALO), :], 0.0)
    xt = x_ref[pl.ds(r0, tm), :]
    return xt, jnp.concatenate([xh, xt], axis=0)


def _mixa_fwd_job(x, pre_g, pool_w, pool_scale, post_g):
    s_len = x.shape[0]

    def fn(i, kc_ref, ins, outs):
        x_ref, pg_ref, w_ref, sc_ref, qg_ref = ins
        y_ref, x1_ref = outs
        xt, xe = _load_with_halo_before(x_ref, i, TM)
        he = _rms_fwd(xe, pg_ref[0:1, :])
        pos = i * TM - HALO + lax.broadcasted_iota(jnp.int32, (TM + HALO, 1), 0)
        ds = _pool_delta(he, pos)
        ys = [_dot(ds[gi][HALO:, :].astype(BF), w_ref[gi]) for gi in range(len(WINDOWS))]
        y = jnp.concatenate(ys, axis=1) * sc_ref[...]
        y_ref[...] = y
        x1_ref[...] = xt + _rms_fwd(y, qg_ref[0:1, :])

    def whole(a):
        zeros = (0,) * a.ndim
        return (a, a.shape, lambda j, kc_ref: zeros, True)

    rows = lambda j, kc_ref: (j, 0)
    return Job(s_len // TM, [whole(a) for a in (x, pre_g, pool_w, pool_scale, post_g)],
               [(_sds((s_len, D)), (TM, D), rows), (_sds((s_len, D)), (TM, D), rows)], fn)


def _mixa_bwd(dx1, x, y, pre_g, pool_w, pool_scale, post_g, rider=None):
    s_len = x.shape[0]
    n = s_len // TM
    ng = len(WINDOWS)

    def body(dx_ref, x_ref, y_ref, pg_ref, w_ref, sc_ref, qg_ref,
             dx0_ref, dw_ref, dsc_ref, dqg_ref, dpg_ref, wacc):
        i = pl.program_id(0)

        @pl.when(i == 0)
        def _():
            wacc[...] = jnp.zeros_like(wacc)
            dsc_ref[...] = jnp.zeros_like(dsc_ref)
            dqg_ref[...] = jnp.zeros_like(dqg_ref)
            dpg_ref[...] = jnp.zeros_like(dpg_ref)

        r0 = pl.multiple_of(i * TM, TM)
        xt, xe = _load_with_halo_before(x_ref, i, TM)
        he = _rms_fwd(xe, pg_ref[0:1, :])
        pos_b = i * TM - HALO + lax.broadcasted_iota(jnp.int32, (TM + HALO, 1), 0)
        ds = _pool_delta(he, pos_b)

        last = i == n - 1
        a0 = pl.multiple_of(jnp.minimum(i * TM + TM, s_len - HALO), 8)
        ye = jnp.concatenate([y_ref[pl.ds(r0, TM), :], y_ref[pl.ds(a0, HALO), :]], axis=0)
        dt = dx_ref[pl.ds(r0, TM), :]
        de = jnp.concatenate([dt, jnp.where(last, 0.0, dx_ref[pl.ds(a0, HALO), :])], axis=0)
        dye, prod = _rms_bwd(ye, qg_ref[0:1, :], de)
        dqg_ref[...] += _rowsum(prod[:TM, :])
        dys = dye * sc_ref[...]
        pos_a = i * TM + lax.broadcasted_iota(jnp.int32, (TM + HALO, 1), 0)

        dhs, dscs = [], []
        for gi, w in enumerate(WINDOWS):
            sl = slice(gi * POOL_G, (gi + 1) * POOL_G)
            wg = w_ref[gi]
            dys_g = dys[:, sl].astype(BF)
            d_g = ds[gi][HALO:, :].astype(BF)
            ypre = _dot(d_g, wg)
            dscs.append(_rowsum(dye[:TM, sl] * ypre))
            wacc[gi] += _dot_tn(d_g, dys_g[:TM, :])
            dd = _dot_nt(dys_g, wg)
            cnt = jnp.minimum(pos_a + 1, w).astype(F32)
            a = dd / cnt
            k = 1
            while k < w:
                a = a + pltpu.roll(a, TM + HALO - k, 0)
                k *= 2
            dhs.append(a[:TM, :] - dd[:TM, :])
        dsc_ref[...] += jnp.concatenate(dscs, axis=1)
        dh = jnp.concatenate(dhs, axis=1)
        dxp, prod2 = _rms_bwd(xt, pg_ref[0:1, :], dh)
        dpg_ref[...] += _rowsum(prod2)
        dx0_ref[...] = dt + dxp

        @pl.when(last)
        def _():
            dw_ref[...] = wacc[...].astype(BF)

    return _call(
        body, name="mixa_bwd", grid=(n,), in_specs=[VSPEC] * 7,
        out_specs=[_row_spec(TM), _const_spec((ng, POOL_G, POOL_G)), _const_spec((1, D)),
                   _const_spec((1, D)), _const_spec((1, D))],
        out_shape=[_sds((s_len, D)), _sds((ng, POOL_G, POOL_G), BF), _sds((1, D)), _sds((1, D)), _sds((1, D))],
        scratch_shapes=[pltpu.VMEM((ng, POOL_G, POOL_G), F32)],
        args=[dx1, x, y, pre_g, pool_w, pool_scale, post_g], rider=rider)


def _ple_math(layer, x, p_blk, g_ref, wg_ref, wp_ref, qg_ref):
    r = _rms_fwd(x, g_ref[layer:layer + 1, :]).astype(BF)
    z = _dot(r, wg_ref[...])
    pe = _dot(p_blk.astype(BF), wp_ref[...])
    return z, pe, x + _rms_fwd(pe * _sigmoid(z), qg_ref[layer:layer + 1, :])


def _ple_bwd_math(layer, dx, x, z, pe, p_blk, g_ref, wg_ref, qg_ref):
    gate = _sigmoid(z)
    de, prod = _rms_bwd(pe * gate, qg_ref[layer:layer + 1, :], dx)
    dpe = (de * gate).astype(BF)
    dz = (de * pe * gate * (1.0 - gate)).astype(BF)
    dwp = _dot_tn(p_blk.astype(BF), dpe)
    g = g_ref[layer:layer + 1, :]
    dwg = _dot_tn(_rms_fwd(x, g).astype(BF), dz)
    dxp, prod2 = _rms_bwd(x, g, _dot_nt(dz, wg_ref[...]))
    return dx + dxp, dwp, dwg, _rowsum(prod2), _rowsum(prod)


def _ffn_fwd(layer, x1, pre_g, wgu, wd, post_g, rider=None, head=None):
    s_len = x1.shape[0]

    def body(*refs):
        if head:
            (x_ref, pg_ref, wgu_ref, wd_ref, qg_ref, p_ref, eg_ref, wg_ref, wp_ref, eq_ref, t_ref,
             f_ref, g_ref, u_ref, dx_ref, dwg_ref, dwp_ref, deg_ref, deq_ref, lv_ref, gacc, pacc) = refs
        else:
            x_ref, pg_ref, wgu_ref, wd_ref, qg_ref, f_ref, x2_ref, g_ref, u_ref = refs
        x = x_ref[...]
        h = _rms_fwd(x, pg_ref[layer:layer + 1, :]).astype(BF)
        f = jnp.zeros((TM, D), F32)
        for c in range(FF // FF_HALF):
            cols = slice(c * FF_HALF, (c + 1) * FF_HALF)
            g = _dot(h, wgu_ref[0, :, cols])
            u = _dot(h, wgu_ref[1, :, cols])
            g_ref[:, cols] = g.astype(BF)
            u_ref[:, cols] = u.astype(BF)
            act = g * _sigmoid(g) * u
            f = f + _dot(act.astype(BF), wd_ref[cols, :])
        f_ref[...] = f
        x2 = x + _rms_fwd(f, qg_ref[layer:layer + 1, :])
        if not head:
            x2_ref[...] = x2
        else:
            step = pl.program_id(0)

            @pl.when(step == 0)
            def _():
                for ref in (lv_ref, deg_ref, deq_ref, gacc, pacc):
                    ref[...] = jnp.zeros_like(ref)
            p_blk = p_ref[...]
            z, pe, x3 = _ple_math(layer, x2, p_blk, eg_ref, wg_ref, wp_ref, eq_ref)
            err = x3 - t_ref[...]
            lv_ref[...] += _rowsum(err * err)
            dx, dwp, dwg, deg, deq = _ple_bwd_math(layer, err * (1.0 / D), x2, z, pe, p_blk, eg_ref, wg_ref, eq_ref)
            dx_ref[...] = dx
            pacc[...] += dwp
            gacc[...] += dwg
            deg_ref[...] += deg
            deq_ref[...] += deq

            @pl.when(step == s_len // TM - 1)
            def _():
                dwg_ref[...] = gacc[...].astype(BF)
                dwp_ref[...] = pacc[...].astype(BF)

    in_specs = [_row_spec(TM), VSPEC, VSPEC, VSPEC, VSPEC]
    args = [x1, pre_g, wgu, wd, post_g]
    out_specs = [_row_spec(TM), _row_spec(TM), _row_spec(TM, FF), _row_spec(TM, FF)]
    out_shape = [_sds((s_len, D)), _sds((s_len, D)), _sds((s_len, FF), BF), _sds((s_len, FF), BF)]
    scratch = []
    if head:
        del out_specs[1], out_shape[1]
        p, ple_g, w_gate, w_proj, ple_post_g, target = head
        in_specs += [pl.BlockSpec((None, TM, PLE), lambda i: (layer, i, 0)), VSPEC, VSPEC, VSPEC, VSPEC, _row_spec(TM)]
        args += [p, ple_g, w_gate, w_proj, ple_post_g, target]
        out_specs += [_row_spec(TM), _const_spec((D, D)), _const_spec((PLE, D))] + [_const_spec((1, D))] * 3
        out_shape += [_sds((s_len, D)), _sds((D, D), BF), _sds((PLE, D), BF)] + [_sds((1, D))] * 3
        scratch = [pltpu.VMEM((D, D), F32), pltpu.VMEM((PLE, D), F32)]
    return _call(body, name=f"ffn_fwd{layer}", grid=(s_len // TM,), in_specs=in_specs, out_specs=out_specs,
                 out_shape=out_shape, args=args, scratch_shapes=scratch, rider=rider)


GU_PIECE = 128
DN_PIECE = 64
DN_SLOT = FF // N_CHIPS
HALF_D = D // 2
CHUNK_STRIDE = 6
CHUNK_START = (1, 7, 4, 10)


def _ffn_bwd(layer, dx2, x1, f, g_pre, u_pre, pre_g, wgu, wd, post_g, kc, rider=None):
    s_len = x1.shape[0]
    tm = TM_FFN_BWD
    n = s_len // tm
    nc = FF // FF_CHUNK
    n_gu, n_dn = FF_CHUNK // GU_PIECE, FF_CHUNK // DN_PIECE
    n_pieces = 2 * n_gu + n_dn
    n_blk = FF_HALF // GU_PIECE

    def edge_rows(c, i, kc_ref):
        return (jnp.where((c == 0) | (c == nc - 1), i, n - 1), 0)

    def chunk_at(c, kc_ref):
        k = kc_ref[0]
        start = jnp.where(k == 0, CHUNK_START[0], jnp.where(k == 1, CHUNK_START[1],
                                                            jnp.where(k == 2, CHUNK_START[2], CHUNK_START[3])))
        return ((c + start) * CHUNK_STRIDE) % nc

    def exchange(kc_ref, c, accg, accu, accd, own_gu_ref, land_gu_ref, own_dn_ref, land_dn_ref,
                 pl_gu, pl_dn, sib_gu, sib_dn, mine_gu, mine_dn, sum_gu, sum_dn,
                 psend, precv, ssend, lsem, rrecv):
        x, y, core = lax.axis_index("x"), lax.axis_index("y"), lax.axis_index("c")
        lower = core == 0

        def pair_copy(cc, part):
            p = cc % 2
            src, dst = ((sib_gu, pl_gu), (sib_dn, pl_dn))[part]
            return pltpu.make_async_remote_copy(src.at[p], dst.at[cc], psend.at[p, part], precv.at[cc, part],
                                                device_id=(x, y, 1 - core), device_id_type=MESH)

        def scatter(cc, wait):
            p = cc % 2
            hidden = chunk_at(cc, kc_ref) * FF_CHUNK

            assert n_gu == 2
            k0, k1 = hidden // FF_HALF, (hidden + GU_PIECE) // FF_HALF
            blk = (hidden - k0 * FF_HALF) // GU_PIECE
            for gu in range(2):
                @pl.when(k0 == k1)
                def _():
                    piece(p, wait, 2 * gu, sum_gu.at[p, gu], k0 + 2 * gu, 0, (pl.ds(blk, 2),))

                @pl.when(k0 != k1)
                def _():
                    piece(p, wait, 2 * gu, sum_gu.at[p, gu, 0], k0 + 2 * gu, 0, (blk,))
                    piece(p, wait, 2 * gu + 1, sum_gu.at[p, gu, 1], k1 + 2 * gu, 0, (0,))

            kd = hidden // DN_SLOT
            off = pl.multiple_of(hidden - kd * DN_SLOT, DN_PIECE)
            m = jnp.minimum((DN_SLOT - off) // DN_PIECE, n_dn)
            for mm in range(1, n_dn + 1):
                @pl.when(m == mm)
                def _():
                    rows = mm * DN_PIECE
                    piece(p, wait, 2 * n_gu, sum_dn.at[p, pl.ds(0, rows), :], kd, 1, (pl.ds(off, rows), slice(None)))
                    if mm < n_dn:
                        piece(p, wait, 2 * n_gu + 1, sum_dn.at[p, pl.ds(rows, FF_CHUNK - rows), :], kd + 1, 1,
                              (pl.ds(0, FF_CHUNK - rows), slice(None)))

        def piece(p, wait, pi, src, k, t, where):
            own_ref, land_ref = ((own_gu_ref, land_gu_ref), (own_dn_ref, land_dn_ref))[t]
            kx, ky = k // 2, k % 2
            fx, fy = (kx != x).astype(jnp.int32), (ky != y).astype(jnp.int32)
            local = (fx + fy) == 0
            j = jnp.maximum(fx + 2 * fy - 1, 0)

            @pl.when(local)
            def _():
                cp = pltpu.make_async_copy(src, own_ref.at[where], lsem.at[p, pi])
                if wait:
                    cp.wait()
                else:
                    cp.start()

            @pl.when(jnp.logical_not(local))
            def _():
                cp = pltpu.make_async_remote_copy(src, land_ref.at[(j,) + where], ssend.at[p, pi],
                                                  rrecv.at[t, j], device_id=(kx, ky, core), device_id_type=MESH)
                if wait:
                    cp.wait_send()
                else:
                    cp.start()

        def add_and_scatter(cc):
            p = cc % 2
            pair_copy(cc, 0).wait_recv()
            pair_copy(cc, 1).wait_recv()
            s_gu = (mine_gu[...] + pl_gu[cc].astype(F32)).astype(BF)
            for hc in range(n_gu):
                sum_gu[p, :, hc] = s_gu[:, :, hc * GU_PIECE:(hc + 1) * GU_PIECE]
            sum_dn[p] = (mine_dn[...] + pl_dn[cc].astype(F32)).astype(BF)
            scatter(cc, wait=False)

        @pl.when(c >= 1)
        def _():
            @pl.when(c >= 3)
            def _():
                scatter(c - 3, wait=True)
            add_and_scatter(c - 1)

        @pl.when(c >= 2)
        def _():
            pair_copy(c - 2, 0).wait_send()
            pair_copy(c - 2, 1).wait_send()

        p = c % 2
        my_rows = pl.ds(pl.multiple_of(core * HALF_D, HALF_D), HALF_D)
        sib_rows = pl.ds(pl.multiple_of((1 - core) * HALF_D, HALF_D), HALF_D)
        d_v = accd[...]
        sib_gu[p, 0] = accg[sib_rows, :].astype(BF)
        sib_gu[p, 1] = accu[sib_rows, :].astype(BF)
        sib_dn[p] = jnp.where(lower, d_v[:, HALF_D:], d_v[:, :HALF_D]).astype(BF)
        mine_gu[0] = accg[my_rows, :]
        mine_gu[1] = accu[my_rows, :]
        mine_dn[...] = jnp.where(lower, d_v[:, :HALF_D], d_v[:, HALF_D:])
        pair_copy(c, 0).start()
        pair_copy(c, 1).start()

        @pl.when(c == nc - 1)
        def _():
            scatter(nc - 3, wait=True)
            add_and_scatter(nc - 1)
            for cc in (nc - 2, nc - 1):
                pair_copy(cc, 0).wait_send()
                pair_copy(cc, 1).wait_send()
                scatter(cc, wait=True)
            for t, land_ref in enumerate((land_gu_ref, land_dn_ref)):
                for j in range(N_CHIPS - 1):
                    pltpu.make_async_remote_copy(land_ref.at[j], land_ref.at[j], ssend.at[0, 0], rrecv.at[t, j],
                                                 device_id=(x, y, core), device_id_type=MESH).wait_recv()

    def body(kc_ref, dx_ref, x_ref, f_ref, gp_ref, up_ref, pg_ref, wgu_ref, wd_ref, qg_ref,
             dx1_ref, dpg_ref, dqg_ref, own_gu_ref, land_gu_ref, own_dn_ref, land_dn_ref,
             h_s, df_s, dh_s, accg, accu, accd, *comm):
        c = pl.program_id(0)
        i = pl.program_id(1)
        rows = pl.ds(pl.multiple_of(i * tm, tm), tm)
        pg = pg_ref[layer:layer + 1, :]

        @pl.when((c == 0) & (i == 0))
        def _():
            dpg_ref[...] = jnp.zeros_like(dpg_ref)
            dqg_ref[...] = jnp.zeros_like(dqg_ref)

        @pl.when(c == 0)
        def _():
            h_s[rows, :] = _rms_fwd(x_ref[...], pg).astype(BF)
            df, prod = _rms_bwd(f_ref[...], qg_ref[layer:layer + 1, :], dx_ref[...])
            df_s[rows, :] = df.astype(BF)
            dqg_ref[...] += _rowsum(prod)

        @pl.when(i == 0)
        def _():
            accg[...] = jnp.zeros_like(accg)
            accu[...] = jnp.zeros_like(accu)
            accd[...] = jnp.zeros_like(accd)

        h = h_s[rows, :]
        df = df_s[rows, :]
        wg = wgu_ref[0]
        wu = wgu_ref[1]
        g = gp_ref[...].astype(F32)
        u = up_ref[...].astype(F32)
        sg = _sigmoid(g)
        a = g * sg
        dact = _dot_nt(df, wd_ref[...])
        accd[...] += _dot_tn((a * u).astype(BF), df)
        du = (dact * a).astype(BF)
        dg = (dact * u * (sg * (1.0 + g * (1.0 - sg)))).astype(BF)
        accg[...] += _dot_tn(h, dg)
        accu[...] += _dot_tn(h, du)
        dh = _dot_nt(dg, wg) + _dot_nt(du, wu)

        @pl.when(c == 0)
        def _():
            dh_s[rows, :] = dh

        @pl.when((c > 0) & (c < nc - 1))
        def _():
            dh_s[rows, :] += dh

        @pl.when(c == nc - 1)
        def _():
            dxp, prod = _rms_bwd(x_ref[...], pg, dh_s[rows, :] + dh)
            dpg_ref[...] += _rowsum(prod)
            dx1_ref[...] = dx_ref[...] + dxp

        @pl.when(i == n - 1)
        def _():
            exchange(kc_ref, c, accg, accu, accd, own_gu_ref, land_gu_ref, own_dn_ref, land_dn_ref, *comm)

    dma = pltpu.SemaphoreType.DMA
    return _call(
        body, name=f"ffn_bwd{layer}", grid=(nc, n),
        in_specs=[pl.BlockSpec((tm, D), edge_rows), pl.BlockSpec((tm, D), edge_rows),
                  pl.BlockSpec((tm, D), lambda c, i, kc_ref: (jnp.where(c == 0, i, n - 1), 0),
                               pipeline_mode=pl.Buffered(1)),
                  pl.BlockSpec((tm, FF_CHUNK), lambda c, i, kc_ref: (i, chunk_at(c, kc_ref))),
                  pl.BlockSpec((tm, FF_CHUNK), lambda c, i, kc_ref: (i, chunk_at(c, kc_ref))),
                  VSPEC,
                  pl.BlockSpec((2, D, FF_CHUNK), lambda c, i, kc_ref: (0, 0, chunk_at(c, kc_ref))),
                  pl.BlockSpec((FF_CHUNK, D), lambda c, i, kc_ref: (chunk_at(c, kc_ref), 0)),
                  VSPEC],
        out_specs=[pl.BlockSpec((tm, D), lambda c, i, kc_ref: (jnp.where(c == nc - 1, i, 0), 0)),
                   _const_spec((1, D)), _const_spec((1, D)), ANYSPEC, ANYSPEC, ANYSPEC, ANYSPEC],
        out_shape=[_sds((s_len, D)), _sds((1, D)), _sds((1, D)),
                   _sds((n_blk, HALF_D, GU_PIECE), BF), _sds((N_CHIPS - 1, n_blk, HALF_D, GU_PIECE), BF),
                   _sds((DN_SLOT, HALF_D), BF), _sds((N_CHIPS - 1, DN_SLOT, HALF_D), BF)],
        scratch_shapes=[pltpu.VMEM((s_len, D), BF), pltpu.VMEM((s_len, D), BF), pltpu.VMEM((s_len, D), F32),
                        pltpu.VMEM((D, FF_CHUNK), F32), pltpu.VMEM((D, FF_CHUNK), F32),
                        pltpu.VMEM((FF_CHUNK, D), F32),
                        pltpu.VMEM((nc, 2, HALF_D, FF_CHUNK), BF), pltpu.VMEM((nc, FF_CHUNK, HALF_D), BF),
                        pltpu.VMEM((2, 2, HALF_D, FF_CHUNK), BF), pltpu.VMEM((2, FF_CHUNK, HALF_D), BF),
                        pltpu.VMEM((2, HALF_D, FF_CHUNK), F32), pltpu.VMEM((FF_CHUNK, HALF_D), F32),
                        pltpu.VMEM((2, 2, n_gu, HALF_D, GU_PIECE), BF), pltpu.VMEM((2, FF_CHUNK, HALF_D), BF),
                        dma((2, 2)), dma((nc, 2)), dma((2, n_pieces)), dma((2, n_pieces)), dma((2, N_CHIPS - 1))],
        args=[dx2, x1, f, g_pre, u_pre, pre_g, wgu, wd, post_g], rider=rider, prefetch=kc)


def _ple_fwd(layer, x2, p, ple_g, w_gate, w_proj, post_g, qkv=None, rider=None):
    s_len = x2.shape[0]

    def body(*refs):
        if qkv:
            (x_ref, p_ref, g_ref, wg_ref, wp_ref, qg_ref, ng_ref, kg_ref, wq_ref, wkv_ref,
             z_ref, pe_ref, x3_ref, q_ref, kv_ref) = refs
        else:
            x_ref, p_ref, g_ref, wg_ref, wp_ref, qg_ref, z_ref, pe_ref, x3_ref = refs
        z, pe, x3 = _ple_math(layer, x_ref[...], p_ref[...], g_ref, wg_ref, wp_ref, qg_ref)
        z_ref[...] = z
        pe_ref[...] = pe
        x3_ref[...] = x3
        if qkv:
            q_ref[...] = _dot(_rms_fwd(x3, ng_ref[layer + 1:layer + 2, :]).astype(BF), wq_ref[...]).astype(BF)
            kv_ref[...] = _dot(_rms_fwd(x3, kg_ref[...]).astype(BF), wkv_ref[...]).astype(BF)

    p_spec = pl.BlockSpec((None, TM, PLE), lambda i: (layer, i, 0))
    in_specs = [_row_spec(TM), p_spec, VSPEC, VSPEC, VSPEC, VSPEC]
    args = [x2, p, ple_g, w_gate, w_proj, post_g]
    out_specs = [_row_spec(TM), _row_spec(TM), _row_spec(TM)]
    out_shape = [_sds((s_len, D))] * 3
    if qkv:
        in_specs += [VSPEC] * 4
        args += list(qkv)
        out_specs += [_row_spec(TM), _row_spec(TM, 2 * KVD)]
        out_shape += [_sds((s_len, D), BF), _sds((s_len, 2 * KVD), BF)]
    return _call(body, name=f"ple_fwd{layer}", grid=(s_len // TM,), in_specs=in_specs, out_specs=out_specs,
                 out_shape=out_shape, args=args, rider=rider)


def _ple_bwd(layer, dx3, x2, z, pe, p, ple_g, w_gate, post_g, rider=None):
    s_len = x2.shape[0]
    n = s_len // TM

    def body(dx_ref, x_ref, z_ref, pe_ref, p_ref, g_ref, wg_ref, qg_ref,
             dx2_ref, dwg_ref, dwp_ref, dg_ref, dqg_ref, gacc, pacc):
        i = pl.program_id(0)

        @pl.when(i == 0)
        def _():
            gacc[...] = jnp.zeros_like(gacc)
            pacc[...] = jnp.zeros_like(pacc)
            dg_ref[...] = jnp.zeros_like(dg_ref)
            dqg_ref[...] = jnp.zeros_like(dqg_ref)

        dx2, dwp, dwg, dg, dqg = _ple_bwd_math(layer, dx_ref[...], x_ref[...], z_ref[...], pe_ref[...], p_ref[...],
                                                g_ref, wg_ref, qg_ref)
        dx2_ref[...] = dx2
        pacc[...] += dwp
        gacc[...] += dwg
        dg_ref[...] += dg
        dqg_ref[...] += dqg

        @pl.when(i == n - 1)
        def _():
            dwg_ref[...] = gacc[...].astype(BF)
            dwp_ref[...] = pacc[...].astype(BF)

    p_spec = pl.BlockSpec((None, TM, PLE), lambda i: (layer, i, 0))
    return _call(
        body, name=f"ple_bwd{layer}", grid=(n,),
        in_specs=[_row_spec(TM), _row_spec(TM), _row_spec(TM), _row_spec(TM), p_spec, VSPEC, VSPEC, VSPEC],
        out_specs=[_row_spec(TM), _const_spec((D, D)), _const_spec((PLE, D)), _const_spec((1, D)), _const_spec((1, D))],
        out_shape=[_sds((s_len, D)), _sds((D, D), BF), _sds((PLE, D), BF), _sds((1, D)), _sds((1, D))],
        scratch_shapes=[pltpu.VMEM((D, D), F32), pltpu.VMEM((PLE, D), F32)],
        args=[dx3, x2, z, pe, p, ple_g, w_gate, post_g], rider=rider)


def _qkv_bwd(dq, dkv, x3, dx4, q_g, kv_g, w_q, w_kv):
    s_len = x3.shape[0]
    n = s_len // TM

    def body(dq_ref, dkv_ref, x_ref, dx_ref, qg_ref, kg_ref, wq_ref, wkv_ref,
             dx3_ref, dwq_ref, dwkv_ref, dqg_ref, dkg_ref, qacc, kacc):
        i = pl.program_id(0)

        @pl.when(i == 0)
        def _():
            qacc[...] = jnp.zeros_like(qacc)
            kacc[...] = jnp.zeros_like(kacc)
            dqg_ref[...] = jnp.zeros_like(dqg_ref)
            dkg_ref[...] = jnp.zeros_like(dkg_ref)

        x = x_ref[...]
        qg = qg_ref[1:2, :]
        kg = kg_ref[...]
        dq_v = dq_ref[...]
        dkv_v = dkv_ref[...].astype(BF)
        qacc[...] += _dot_tn(_rms_fwd(x, qg).astype(BF), dq_v)
        kacc[...] += _dot_tn(_rms_fwd(x, kg).astype(BF), dkv_v)
        dxq, prod_q = _rms_bwd(x, qg, _dot_nt(dq_v, wq_ref[...]))
        dxk, prod_k = _rms_bwd(x, kg, _dot_nt(dkv_v, wkv_ref[...]))
        dqg_ref[...] += _rowsum(prod_q)
        dkg_ref[...] += _rowsum(prod_k)
        dx3_ref[...] = dx_ref[...] + dxq + dxk

        @pl.when(i == n - 1)
        def _():
            dwq_ref[...] = qacc[...].astype(BF)
            dwkv_ref[...] = kacc[...].astype(BF)

    outs, _ = _call(
        body, name="qkv_bwd", grid=(n,),
        in_specs=[_row_spec(TM), _row_spec(TM, 2 * KVD), _row_spec(TM), _row_spec(TM), VSPEC, VSPEC, VSPEC, VSPEC],
        out_specs=[_row_spec(TM), _const_spec((D, D)), _const_spec((D, 2 * KVD)),
                   _const_spec((1, D)), _const_spec((1, D))],
        out_shape=[_sds((s_len, D)), _sds((D, D), BF), _sds((D, 2 * KVD), BF), _sds((1, D)), _sds((1, D))],
        scratch_shapes=[pltpu.VMEM((D, D), F32), pltpu.VMEM((D, 2 * KVD), F32)],
        args=[dq, dkv, x3, dx4, q_g, kv_g, w_q, w_kv])
    return outs


def _attn_group(i, q, kvw, sink_ref, g):
    rows = GQA * BLK
    heads = [GQA * g + j for j in range(GQA)]
    off = jnp.where(i > 0, BLK, 0)
    row = lax.broadcasted_iota(jnp.int32, (rows, 2 * BLK), 0)
    rel = (row % BLK) - lax.broadcasted_iota(jnp.int32, (rows, 2 * BLK), 1) + off
    valid = (rel >= 0) & (rel < BLK)
    head_of_row = lax.broadcasted_iota(jnp.int32, (rows, 1), 0) // BLK
    slope = jnp.zeros((rows, 1), F32)
    sink = jnp.zeros((rows, 1), F32)
    for j, h in enumerate(heads):
        slope = jnp.where(head_of_row == j, SLOPES[h], slope)
        sink = jnp.where(head_of_row == j, sink_ref[0, h], sink)
    qs = jnp.concatenate([q[:, h * HEAD_DIM:(h + 1) * HEAD_DIM] for h in heads], axis=0)
    k = kvw[:, g * HEAD_DIM:(g + 1) * HEAD_DIM]
    v = kvw[:, KVD + g * HEAD_DIM:KVD + (g + 1) * HEAD_DIM]
    s = _dot_nt(qs, k) * ATT_SCALE - slope * rel.astype(F32)
    s = jnp.where(valid, s, NEG_INF)
    m = jnp.maximum(jnp.max(s, axis=-1, keepdims=True), sink)
    e = jnp.exp(s - m)
    es = jnp.exp(sink - m)
    inv = 1.0 / (jnp.sum(e, axis=-1, keepdims=True) + es)
    return e * inv, es * inv, qs, k, v


def _unstack_heads(stacked):
    return [stacked[j * BLK:(j + 1) * BLK, :] for j in range(GQA)]


def _kv_window(kv_ref, i):
    ks = pl.multiple_of(jnp.maximum(i * BLK - BLK, 0), BLK)
    return ks, kv_ref[pl.ds(ks, 2 * BLK), :]


def _attn_fwd(q, kv, sinks, x3, w_o, post_g, rider=None):
    s_len = q.shape[0]

    def body(q_ref, kv_ref, sk_ref, x_ref, wo_ref, g_ref, a_ref, y_ref, x4_ref):
        i = pl.program_id(0)
        _, kvw = _kv_window(kv_ref, i)
        q = q_ref[...]
        outs = []
        for g in range(N_KV_HEADS):
            p, _, _, _, v = _attn_group(i, q, kvw, sk_ref, g)
            outs += _unstack_heads(_dot(p.astype(BF), v))
        attn = jnp.concatenate(outs, axis=1)
        a_ref[...] = attn
        y = _dot(attn.astype(BF), wo_ref[...])
        y_ref[...] = y
        x4_ref[...] = x_ref[...] + _rms_fwd(y, g_ref[1:2, :])

    return _call(body, name="attn_fwd", grid=(s_len // BLK,),
                 in_specs=[_row_spec(BLK), VSPEC, SSPEC, _row_spec(BLK), VSPEC, VSPEC],
                 out_specs=[_row_spec(BLK)] * 3, out_shape=[_sds((s_len, D))] * 3,
                 args=[q, kv, sinks, x3, w_o, post_g], rider=rider)


ATT_STEP_BLOCKS = 2


def _attn_bwd(dx4, y, attn, q, kv, sinks, w_o, post_g, rider=None):
    s_len = q.shape[0]
    rows = ATT_STEP_BLOCKS * BLK
    n = s_len // rows

    def body(dx_ref, y_ref, a_ref, q_ref, kv_ref, sk_ref, wo_ref, g_ref,
             dq_ref, dkv_ref, dwo_ref, dg_ref, dsk_ref, wacc):
        i = pl.program_id(0)

        @pl.when(i == 0)
        def _():
            dkv_ref[...] = jnp.zeros_like(dkv_ref)
            wacc[...] = jnp.zeros_like(wacc)
            dg_ref[...] = jnp.zeros_like(dg_ref)
            dsk_ref[...] = jnp.zeros_like(dsk_ref)

        dy, prod = _rms_bwd(y_ref[...], g_ref[1:2, :], dx_ref[...])
        dg_ref[...] += _rowsum(prod)
        dyb = dy.astype(BF)
        attn_all = a_ref[...]
        wacc[...] += _dot_tn(attn_all.astype(BF), dyb)
        d_o_all = _dot_nt(dyb, wo_ref[...])
        q_all = q_ref[...]
        lane = lax.broadcasted_iota(jnp.int32, (1, D), 1)
        dsk = jnp.zeros((1, D), F32)
        for sub in range(ATT_STEP_BLOCKS):
            blk = i * ATT_STEP_BLOCKS + sub
            sl = slice(sub * BLK, (sub + 1) * BLK)
            d_o, q = d_o_all[sl, :], q_all[sl, :]
            dod = d_o * attn_all[sl, :]
            ks, kvw = _kv_window(kv_ref, blk)
            dqs, dks, dvs = [], [], []
            for g in range(N_KV_HEADS):
                p, ps, qs, k, v = _attn_group(blk, q, kvw, sk_ref, g)
                cols = [slice((GQA * g + j) * HEAD_DIM, (GQA * g + j + 1) * HEAD_DIM) for j in range(GQA)]
                do_s = jnp.concatenate([d_o[:, c] for c in cols], axis=0).astype(BF)
                dsum = jnp.concatenate([jnp.sum(dod[:, c], axis=-1, keepdims=True) for c in cols], axis=0)
                dp = _dot_nt(do_s, v)
                dsb = (p * (dp - dsum) * ATT_SCALE).astype(BF)
                sink_part = ps * dsum
                for j in range(GQA):
                    dsk = dsk + jnp.where(lane == GQA * g + j, -_rowsum(sink_part[j * BLK:(j + 1) * BLK, :]), 0.0)
                dqs += _unstack_heads(_dot(dsb, k))
                dks.append(_dot_tn(dsb, qs))
                dvs.append(_dot_tn(p.astype(BF), do_s))
            dq_ref[sl, :] = jnp.concatenate(dqs, axis=1).astype(BF)
            dkv_ref[pl.ds(ks, 2 * BLK), :] += jnp.concatenate(dks + dvs, axis=1)
        dsk_ref[...] += dsk

        @pl.when(i == n - 1)
        def _():
            dwo_ref[...] = wacc[...].astype(BF)

    return _call(
        body, name="attn_bwd", grid=(n,),
        in_specs=[_row_spec(rows), _row_spec(rows), _row_spec(rows), _row_spec(rows), VSPEC, SSPEC, VSPEC, VSPEC],
        out_specs=[_row_spec(rows), _const_spec((s_len, 2 * KVD)), _const_spec((D, D)),
                   _const_spec((1, D)), _const_spec((1, D))],
        out_shape=[_sds((s_len, D), BF), _sds((s_len, 2 * KVD)), _sds((D, D), BF), _sds((1, D)), _sds((1, D))],
        scratch_shapes=[pltpu.VMEM((D, D), F32)],
        args=[dx4, y, attn, q, kv, sinks, w_o, post_g], rider=rider)


Big = collections.namedtuple("Big", "name src layer L A R C rb")


def _bigs():
    out = {"pool_w": Big("pool_w", "pool_w", None, 4, 4, POOL_G // N_CHIPS, POOL_G, 32)}
    for l in range(2):
        out[f"w_gu{l}"] = Big(f"w_gu{l}", "w_gu", l, 1, 2, D, FF_HALF, 256)
        out[f"w_down{l}"] = Big(f"w_down{l}", "w_down", l, 1, 4, FF // N_CHIPS, D, 352)
        out[f"w_ple_gate{l}"] = Big(f"w_ple_gate{l}", "w_ple_gate", l, 1, 4, D // N_CHIPS, D, 128)
        out[f"w_ple_proj{l}"] = Big(f"w_ple_proj{l}", "w_ple_proj", l, 1, 1, PLE, D // N_CHIPS, 128)
    out["w_q"] = Big("w_q", "w_q", None, 1, 4, D // N_CHIPS, D, 128)
    out["w_o"] = Big("w_o", "w_o", None, 1, 4, D // N_CHIPS, D, 128)
    out["w_kv"] = Big("w_kv", "w_kv", None, 1, 4, D // N_CHIPS, 2 * KVD, 128)
    return out


BIGS = _bigs()
POOL_SCALE = Big("pool_scale", "pool_scale", None, 1, 1, 1, D // N_CHIPS, 1)
BIG_SOURCES = ("w_gu", "w_down", "w_ple_gate", "w_ple_proj", "w_q", "w_o", "w_kv", "pool_w")


def _ncb(t):
    return N_CHIPS // t.A


def _full_shape(t, rows=None):
    return (t.L, t.A, t.R if rows is None else rows, _ncb(t) * t.C)


def _slot_index(t, k):
    return k // _ncb(t), k % _ncb(t)


def _slot(ref, t, k, row0, rows):
    a, cb = _slot_index(t, k)
    return ref.at[:, a, pl.ds(row0, rows), pl.ds(pl.multiple_of(cb * t.C, 128), t.C)]


def _place_job(t, w, out_dtype=BF):
    nb = next((nb for nb in (8, 4, 2, 1) if t.R % (16 * nb) == 0), 1) if t.L == 1 else 1
    rb = t.R // nb

    def fn(j, kc_ref, ins, outs):
        outs[0][...] = ins[0][...].astype(out_dtype)

    def in_map(j, kc_ref):
        return (j // nb if t.layer is None else t.layer, j % nb, 0)

    def out_map(j, kc_ref):
        a, cb = _slot_index(t, kc_ref[0])
        return (j // nb, a, j % nb, cb)

    return Job(t.L * nb, [(w, (None, rb, t.C), in_map)],
               [(_sds(_full_shape(t), out_dtype), (None, None, rb, t.C), out_map)], fn)


def _mesh_position():
    x, y, c = lax.axis_index("x"), lax.axis_index("y"), lax.axis_index("c")
    chips = [(1 - x, y), (x, 1 - y), (1 - x, 1 - y)]
    return x, y, c, chips


DIRECT_BELOW = 1024


def _gather_rider(parts, fulls):
    nt = len(parts)
    TO_X, TO_Y, FWD_X, FWD_Y, SIB_X, SIB_Y, SIB_D = range(7)

    def rows_of(ti, core):
        t, r0, r1 = parts[ti]
        h = (r1 - r0) // 2
        return r0 + core * h, h

    def copy(outs, sems, kind, ti, k_src, row0, rows, dev):
        region = _slot(outs[ti], parts[ti][0], k_src, row0, rows)
        return pltpu.make_async_remote_copy(region, region, sems[0].at[ti, kind], sems[1].at[ti, kind],
                                            device_id=dev, device_id_type=MESH)

    def plan(outs, sems):
        x, y, c, _ = _mesh_position()
        me, kx, ky, kd = 2 * x + y, 2 * (1 - x) + y, 2 * x + (1 - y), 2 * (1 - x) + (1 - y)
        dev_x, dev_y, dev_d, sib = (1 - x, y, c), (x, 1 - y, c), (1 - x, 1 - y, c), (x, y, 1 - c)

        def whole(ti):
            return 0, parts[ti][0].R

        def mk(kind, k_send, k_recv, dev, send_rows, recv_rows):
            def build(ti, side):
                k_src = k_send if side == "s" else k_recv
                row0, rows = (send_rows if side == "s" else recv_rows)(ti)
                return copy(outs, sems, kind, ti, k_src, row0, rows, dev)
            return build

        def first_half(core):
            return lambda ti: (rows_of(ti, core)[0], rows_of(ti, core)[1] // 2)

        def second_half(core):
            return lambda ti: (rows_of(ti, core)[0] + rows_of(ti, core)[1] // 2, rows_of(ti, core)[1] // 2)

        mine = lambda ti: rows_of(ti, c)
        theirs = lambda ti: rows_of(ti, 1 - c)
        split = {
            TO_X: mk(TO_X, me, kx, dev_x, mine, mine),
            TO_Y: mk(TO_Y, me, ky, dev_y, mine, mine),
            FWD_X: mk(FWD_X, ky, kd, dev_x, first_half(c), first_half(c)),
            FWD_Y: mk(FWD_Y, kx, kd, dev_y, second_half(c), second_half(c)),
            SIB_X: mk(SIB_X, kx, kx, sib, mine, theirs),
            SIB_Y: mk(SIB_Y, ky, ky, sib, mine, theirs),
            SIB_D: mk(SIB_D, kd, kd, sib, mine, theirs),
        }
        direct = {
            TO_X: mk(TO_X, me, kx, dev_x, whole, whole),
            TO_Y: mk(TO_Y, me, ky, dev_y, whole, whole),
            FWD_X: mk(FWD_X, me, kd, dev_d, whole, whole),
        }
        return split, direct

    is_split = [t.L * t.R * t.C >= DIRECT_BELOW for t, _, _ in parts]
    assert all(s or (r0, r1) == (0, t.R) for s, (t, r0, r1) in zip(is_split, parts))

    def start(ins, outs, sems):
        split, direct = plan(outs, sems)
        for ti in range(nt):
            kinds = split if is_split[ti] else direct
            kinds[TO_X](ti, "s").start()
            kinds[TO_Y](ti, "s").start()
            if not is_split[ti]:
                kinds[FWD_X](ti, "s").start()

    def mid(ins, outs, sems):
        split, _ = plan(outs, sems)
        for ti in range(nt):
            if is_split[ti]:
                split[TO_Y](ti, "r").wait_recv()
                split[FWD_X](ti, "s").start()
                split[SIB_Y](ti, "s").start()
        for ti in range(nt):
            if is_split[ti]:
                split[TO_X](ti, "r").wait_recv()
                split[FWD_Y](ti, "s").start()
                split[SIB_X](ti, "s").start()

    def finish(ins, outs, sems):
        split, direct = plan(outs, sems)
        for ti in range(nt):
            if is_split[ti]:
                split[FWD_X](ti, "r").wait_recv()
                split[FWD_Y](ti, "r").wait_recv()
                split[SIB_D](ti, "s").start()
            else:
                for kind in (TO_X, TO_Y, FWD_X):
                    direct[kind](ti, "r").wait_recv()
        for ti in range(nt):
            if is_split[ti]:
                for kind in (SIB_X, SIB_Y, SIB_D):
                    split[kind](ti, "r").wait_recv()
        for ti in range(nt):
            kinds = split if is_split[ti] else direct
            for kind in kinds:
                kinds[kind](ti, "s").wait_send()

    sems = pltpu.SemaphoreType.DMA((nt, 7))
    return Rider(list(fulls), [_sds(a.shape, a.dtype) for a in fulls], {i: i for i in range(nt)},
                 [sems, sems], start, mid, finish)


def _pair_exchange_rider(specs, grads):
    nt = len(specs)

    def copy(ins, outs, sems, ti, c, sibling):
        half = specs[ti].R // 2
        return pltpu.make_async_remote_copy(ins[ti].at[:, :, pl.ds((1 - c) * half, half), :], outs[ti],
                                            sems[0].at[ti], sems[1].at[ti], device_id=sibling, device_id_type=MESH)

    def start(ins, outs, sems):
        x, y, c, _ = _mesh_position()
        for ti in range(nt):
            copy(ins, outs, sems, ti, c, (x, y, 1 - c)).start()

    def finish(ins, outs, sems):
        x, y, c, _ = _mesh_position()
        for ti in range(nt):
            copy(ins, outs, sems, ti, c, (x, y, 1 - c)).wait()

    sems = pltpu.SemaphoreType.DMA((nt,))
    return Rider(list(grads), [_sds(_full_shape(t, t.R // 2), BF) for t in specs], {}, [sems, sems], start, None, finish)


def _pair_sum_job(t, g, land):
    assert t.L == 1
    half = t.R // 2
    nj = half // t.rb
    block = (None, t.A, t.rb, _ncb(t) * t.C)

    def fn(j, kc_ref, ins, outs):
        outs[0][...] = (ins[0][...].astype(F32) + ins[1][...].astype(F32)).astype(BF)

    return Job(nj,
               [(g, block, lambda j, kc_ref: (0, 0, kc_ref[1] * nj + j, 0)),
                (land, block, lambda j, kc_ref: (0, 0, j, 0))],
               [(_sds(_full_shape(t, half), BF), block, lambda j, kc_ref: (0, 0, j, 0))], fn)


def _scatter_rider(specs, sums):
    nt = len(specs)

    def copy(ins, outs, sems, ti, j, chip, c):
        t = specs[ti]
        cx, cy = chip
        return pltpu.make_async_remote_copy(_slot(ins[ti], t, 2 * cx + cy, 0, t.R // 2), outs[ti].at[j],
                                            sems[0].at[ti, j], sems[1].at[ti, j],
                                            device_id=(cx, cy, c), device_id_type=MESH)

    def start(ins, outs, sems):
        _, _, c, chips = _mesh_position()
        for j, chip in enumerate(chips):
            for ti in range(nt):
                copy(ins, outs, sems, ti, j, chip, c).start()

    def finish(ins, outs, sems):
        _, _, c, chips = _mesh_position()
        for j, chip in enumerate(chips):
            for ti in range(nt):
                copy(ins, outs, sems, ti, j, chip, c).wait()

    sems = pltpu.SemaphoreType.DMA((nt, N_CHIPS - 1))
    return Rider(list(sums), [_sds((N_CHIPS - 1, t.L, t.R // 2, t.C), BF) for t in specs], {}, [sems, sems],
                 start, None, finish)


def _chip_sum_job(ts, landed):
    t0 = ts[0]
    assert t0.L == 1
    half = t0.R // 2
    nj = half // t0.rb

    def local(j, li):
        return jnp.clip(j - li * nj, 0, nj - 1)

    ins = []
    for li, t in enumerate(ts):
        s, land = landed[t.name]

        def own_map(j, kc_ref, li=li, t=t):
            a, cb = _slot_index(t, kc_ref[0])
            return (0, a, local(j, li), cb)

        ins.append((s, (None, None, t.rb, t.C), own_map))
        ins.append((land, (N_CHIPS - 1, None, t.rb, t.C), lambda j, kc_ref, li=li: (0, 0, local(j, li), 0)))

    def fn(j, kc_ref, in_refs, outs):
        for li in range(len(ts)):
            @pl.when(j // nj == li)
            def _():
                acc = in_refs[2 * li][...].astype(F32)
                for k in range(N_CHIPS - 1):
                    acc = acc + in_refs[2 * li + 1][k].astype(F32)
                outs[0][...] = acc

    return Job(len(ts) * nj, ins,
               [(_sds((len(ts), t0.R, t0.C)), (None, t0.rb, t0.C),
                 lambda j, kc_ref: (j // nj, kc_ref[1] * nj + j % nj, 0))], fn)


def _adamw_job(rb, w, g, m, v):
    n_layers, r, c = w.shape
    nb = r // rb
    block = (None, rb, c)
    index = lambda j, kc_ref: (j // nb, j % nb, 0)

    def fn(j, kc_ref, ins, outs):
        g_v = ins[1][...]
        outs[0][...] = g_v
        outs[1][...], outs[2][...], outs[3][...] = _adamw_math(ins[0][...], g_v, ins[2][...], ins[3][...])

    return Job(n_layers * nb, [(a, block, index) for a in (w, g, m, v)],
               [(_sds(w.shape), block, index)] * 4, fn)


ADAMW_RING = 3


def _adamw_ring(name, rb, w, g, m, v):
    n_layers, r, c = w.shape
    nb = r // rb
    n = n_layers * nb

    def body(w_hbm, g_hbm, m_hbm, v_hbm, g_out, d_out, m_out, v_out, w_buf, g_buf, m_buf, v_buf, sem):
        s = pl.program_id(0)
        pairs = ((w_hbm, w_buf), (g_hbm, g_buf), (m_hbm, m_buf), (v_hbm, v_buf))

        def copies(step):
            slot = step % ADAMW_RING
            return [pltpu.make_async_copy(src.at[step // nb, pl.ds((step % nb) * rb, rb), :], buf.at[slot],
                                          sem.at[a, slot]) for a, (src, buf) in enumerate(pairs)]

        @pl.when(s == 0)
        def _():
            for ahead in range(min(ADAMW_RING - 1, n)):
                for cp in copies(ahead):
                    cp.start()

        @pl.when(s + ADAMW_RING - 1 < n)
        def _():
            for cp in copies(s + ADAMW_RING - 1):
                cp.start()

        for cp in copies(s):
            cp.wait()
        slot = s % ADAMW_RING
        g_v = g_buf[slot]
        g_out[...] = g_v
        d_out[...], m_out[...], v_out[...] = _adamw_math(w_buf[slot], g_v, m_buf[slot], v_buf[slot])

    block = pl.BlockSpec((None, rb, c), lambda i: (i // nb, i % nb, 0))
    ring = lambda a: pltpu.VMEM((ADAMW_RING, rb, c), a.dtype)
    outs, _ = _call(body, name=name, grid=(n,), in_specs=[ANYSPEC] * 4, out_specs=[block] * 4,
                    out_shape=[_sds(w.shape)] * 4, args=[w, g, m, v],
                    scratch_shapes=[ring(w), ring(g), ring(m), ring(v), pltpu.SemaphoreType.DMA((4, ADAMW_RING))])
    return outs


def _chip_sum_fused_job(ts, fused, by_cols):
    t0 = ts[0]
    own0 = fused[t0.name][0]
    if by_cols:
        rows, cols = own0.shape
    else:
        nb, rows, bw = own0.shape
        cols = nb * bw
    nj = rows // t0.rb

    def local(j, li):
        return jnp.clip(j - li * nj, 0, nj - 1)

    ins = []
    for li, t in enumerate(ts):
        own, land = fused[t.name]
        if by_cols:
            ins.append((own, (t.rb, cols), lambda j, kc_ref, li=li: (local(j, li), 0)))
            ins.append((land, (N_CHIPS - 1, t.rb, cols), lambda j, kc_ref, li=li: (0, local(j, li), 0)))
        else:
            ins.append((own, (nb, t.rb, bw), lambda j, kc_ref, li=li: (0, local(j, li), 0)))
            ins.append((land, (N_CHIPS - 1, nb, t.rb, bw), lambda j, kc_ref, li=li: (0, 0, local(j, li), 0)))

    def fn(j, kc_ref, in_refs, outs):
        for li in range(len(ts)):
            @pl.when(j // nj == li)
            def _():
                acc = in_refs[2 * li][...].astype(F32)
                for k in range(N_CHIPS - 1):
                    acc = acc + in_refs[2 * li + 1][k].astype(F32)
                outs[0][...] = acc if by_cols else jnp.concatenate([acc[b] for b in range(nb)], axis=1)

    def out_map(j, kc_ref):
        return (j // nj, j % nj, kc_ref[1]) if by_cols else (j // nj, kc_ref[1] * nj + j % nj, 0)

    return Job(len(ts) * nj, ins, [(_sds((len(ts), t0.R, t0.C)), (None, t0.rb, cols), out_map)], fn)


def _share_rider(halves, by_cols):
    nt = len(halves)

    def copy(outs, sems, ti, core, sibling):
        axis = 2 if by_cols[ti] else 1
        half = halves[ti].shape[axis] // 2
        piece = pl.ds(pl.multiple_of(core * half, 128 if by_cols[ti] else 8), half)
        part = outs[ti].at[:, :, piece] if by_cols[ti] else outs[ti].at[:, piece, :]
        return pltpu.make_async_remote_copy(part, part, sems[0].at[ti], sems[1].at[ti],
                                            device_id=sibling, device_id_type=MESH)

    def start(ins, outs, sems):
        x, y, c, _ = _mesh_position()
        for ti in range(nt):
            copy(outs, sems, ti, c, (x, y, 1 - c)).start()

    def finish(ins, outs, sems):
        x, y, c, _ = _mesh_position()
        for ti in range(nt):
            copy(outs, sems, ti, 1 - c, (x, y, 1 - c)).wait_recv()
        for ti in range(nt):
            copy(outs, sems, ti, c, (x, y, 1 - c)).wait_send()

    sems = pltpu.SemaphoreType.DMA((nt,))
    return Rider(list(halves), [_sds(a.shape, a.dtype) for a in halves], {i: i for i in range(nt)}, [sems, sems],
                 start, None, finish)


def _both(r1, r2):
    assert r1.mid is None and r2.mid is None
    ni, no, ns = len(r1.arrays), len(r1.out_shapes), len(r1.scratch)

    def split(fn1, fn2):
        def run(ins, outs, scr):
            fn1(ins[:ni], outs[:no], scr[:ns])
            fn2(ins[ni:], outs[no:], scr[ns:])
        return run

    aliases = dict(r1.aliases)
    aliases.update({ni + a: no + b for a, b in r2.aliases.items()})
    return Rider(r1.arrays + r2.arrays, r1.out_shapes + r2.out_shapes, aliases, r1.scratch + r2.scratch,
                 split(r1.start, r2.start), None, split(r1.finish, r2.finish))


def _adamw_math(w, g, m, v):
    m = B1 * m + (1.0 - B1) * g
    v = B2 * v + (1.0 - B2) * (g * g)
    delta = -LR * ((m / BC1) / (jnp.sqrt(v / BC2) + AEPS) + WD * w)
    return delta, m, v


GAIN_ROWS = {"pre_mix_g": 0, "post_mix_g": 2, "pre_ffn_g": 4, "post_ffn_g": 6, "ple_g": 8, "ple_post_g": 10}
ROW_KV_G, ROW_POOL_SCALE, ROW_SINKS, ROW_LOSS, PACK_ROWS = 12, 13, 14, 15, 16
SMALL_NAMES = tuple(GAIN_ROWS) + ("kv_g", "pool_scale", "sinks")


def _small_all_reduce(rows, dpool, rider=None):
    ng, pr = len(WINDOWS), POOL_G // N_CHIPS

    def body(*refs):
        row_refs = refs[:PACK_ROWS]
        dpool_ref, tot_ref, gpool_ref, pack, land, pland, send, recv, psend, precv = refs[PACK_ROWS:]
        x, y, c, _ = _mesh_position()
        me = 4 * x + 2 * y + c
        for r in range(PACK_ROWS):
            pack[r:r + 1, :] = row_refs[r][...]

        def shard_of(k):
            return dpool_ref.at[:, pl.ds(pl.multiple_of(k * pr, pr), pr), :]

        cps = []
        for j in range(1, N_DEV):
            px, py, pc = x ^ (j >> 2), y ^ ((j >> 1) & 1), c ^ (j & 1)
            cps.append(pltpu.make_async_remote_copy(pack, land.at[me], send.at[j], recv.at[j],
                                                    device_id=(px, py, pc), device_id_type=MESH))
            cps.append(pltpu.make_async_remote_copy(shard_of(2 * px + py), pland.at[me], psend.at[j], precv.at[j],
                                                    device_id=(px, py, pc), device_id_type=MESH))
        for cp in cps:
            cp.start()
        land[me] = pack[...]
        pland[me] = dpool_ref[:, pl.ds(pl.multiple_of((2 * x + y) * pr, pr), pr), :]
        for j in range(1, N_DEV):
            pltpu.make_async_remote_copy(pack, land.at[me ^ j], send.at[j], recv.at[j],
                                         device_id=(x, y, c), device_id_type=MESH).wait_recv()
            pltpu.make_async_remote_copy(shard_of(0), pland.at[me ^ j], psend.at[j], precv.at[j],
                                         device_id=(x, y, c), device_id_type=MESH).wait_recv()
        for cp in cps:
            cp.wait_send()
        tot = land[0]
        gp = pland[0].astype(F32)
        for d in range(1, N_DEV):
            tot = tot + land[d]
            gp = gp + pland[d].astype(F32)
        tot_ref[...] = tot
        gpool_ref[...] = gp

    sems = pltpu.SemaphoreType.DMA((N_DEV,))
    return _call(
        body, name="small_all_reduce", grid=(1,),
        in_specs=[VSPEC] * (PACK_ROWS + 1), out_specs=[VSPEC, VSPEC],
        out_shape=[_sds((PACK_ROWS, D)), _sds((ng, pr, POOL_G))],
        scratch_shapes=[pltpu.VMEM((PACK_ROWS, D), F32), pltpu.VMEM((N_DEV, PACK_ROWS, D), F32),
                        pltpu.VMEM((N_DEV, ng, pr, POOL_G), BF), sems, sems, sems, sems],
        args=[*rows, dpool], rider=rider)


def _small_adamw(tot, kc, small_w, small_m, small_v):
    names = SMALL_NAMES
    n = len(names)

    def body(*refs):
        tot_ref, kc_ref = refs[0], refs[1]
        w_refs = dict(zip(names, refs[2:2 + n]))
        m_refs = dict(zip(names, refs[2 + n:2 + 2 * n]))
        v_refs = dict(zip(names, refs[2 + 2 * n:2 + 3 * n]))
        loss_ref = refs[2 + 3 * n]
        out_refs = {nm: refs[3 + 3 * n + 4 * k: 7 + 3 * n + 4 * k] for k, nm in enumerate(names)}
        tot = tot_ref[...]
        loss_ref[...] = 0.5 * jnp.sum(tot[ROW_LOSS:ROW_LOSS + 1, :], axis=-1, keepdims=True) * (1.0 / D)

        def update(nm, g):
            g_ref, d_ref, nm_ref, nv_ref = out_refs[nm]
            g_ref[...] = g
            d_ref[...], nm_ref[...], nv_ref[...] = _adamw_math(w_refs[nm][...], g, m_refs[nm][...], v_refs[nm][...])

        for nm, r in GAIN_ROWS.items():
            update(nm, tot[r:r + 2, :])
        update("kv_g", tot[ROW_KV_G:ROW_KV_G + 1, :])
        k = kc_ref[0]
        width = D // N_CHIPS
        g_scale = jnp.zeros((1, width), F32)
        for kk in range(N_CHIPS):
            g_scale = g_scale + jnp.where(k == kk, tot[ROW_POOL_SCALE:ROW_POOL_SCALE + 1, kk * width:(kk + 1) * width], 0.0)
        update("pool_scale", g_scale)
        update("sinks", tot[ROW_SINKS:ROW_SINKS + 1, 0:N_HEADS])

    ins = [tot, kc] + [small_w[nm] for nm in names] + [small_m[nm] for nm in names] + [small_v[nm] for nm in names]
    out_shape = [_sds((1, 1))]
    for nm in names:
        out_shape += [_sds(small_w[nm].shape)] * 4
    outs = pl.pallas_call(
        body, name="small_adamw",
        in_specs=[VSPEC, SSPEC] + [VSPEC] * (3 * n), out_specs=[VSPEC] * len(out_shape), out_shape=out_shape,
        compiler_params=_params(),
    )(*ins)
    return outs[0], {nm: outs[1 + 4 * k: 5 + 4 * k] for k, nm in enumerate(names)}


def _compute_layout(t, full):
    if t.src == "w_gu":
        return full.reshape(2, D, FF)
    if t.src == "pool_w":
        return full.reshape(len(WINDOWS), POOL_G, POOL_G)
    if t.src == "pool_scale":
        return full.reshape(1, D)
    return full.reshape(t.A * t.R, _ncb(t) * t.C)


def kernel(x, p, pre_mix_g, post_mix_g, pre_ffn_g, post_ffn_g, pool_w, pool_scale, kv_g, w_kv, w_q, sinks, w_o, w_gu, w_down, ple_g, w_ple_gate, w_ple_proj, ple_post_g, loss_target, m_pre_mix_g, m_post_mix_g, m_pre_ffn_g, m_post_ffn_g, m_pool_w, m_pool_scale, m_kv_g, m_w_kv, m_w_q, m_sinks, m_w_o, m_w_gu, m_w_down, m_ple_g, m_w_ple_gate, m_w_ple_proj, m_ple_post_g, v_pre_mix_g, v_post_mix_g, v_pre_ffn_g, v_post_ffn_g, v_pool_w, v_pool_scale, v_kv_g, v_w_kv, v_w_q, v_sinks, v_w_o, v_w_gu, v_w_down, v_ple_g, v_w_ple_gate, v_w_ple_proj, v_ple_post_g):
    weights = dict(pre_mix_g=pre_mix_g, post_mix_g=post_mix_g, pre_ffn_g=pre_ffn_g, post_ffn_g=post_ffn_g,
                   pool_w=pool_w, pool_scale=pool_scale, kv_g=kv_g, w_kv=w_kv, w_q=w_q, sinks=sinks, w_o=w_o,
                   w_gu=w_gu, w_down=w_down, ple_g=ple_g, w_ple_gate=w_ple_gate, w_ple_proj=w_ple_proj,
                   ple_post_g=ple_post_g)
    m_in = dict(pre_mix_g=m_pre_mix_g, post_mix_g=m_post_mix_g, pre_ffn_g=m_pre_ffn_g, post_ffn_g=m_post_ffn_g,
                pool_w=m_pool_w, pool_scale=m_pool_scale, kv_g=m_kv_g, w_kv=m_w_kv, w_q=m_w_q, sinks=m_sinks,
                w_o=m_w_o, w_gu=m_w_gu, w_down=m_w_down, ple_g=m_ple_g, w_ple_gate=m_w_ple_gate,
                w_ple_proj=m_w_ple_proj, ple_post_g=m_ple_post_g)
    v_in = dict(pre_mix_g=v_pre_mix_g, post_mix_g=v_post_mix_g, pre_ffn_g=v_pre_ffn_g, post_ffn_g=v_post_ffn_g,
                pool_w=v_pool_w, pool_scale=v_pool_scale, kv_g=v_kv_g, w_kv=v_w_kv, w_q=v_w_q, sinks=v_sinks,
                w_o=v_w_o, w_gu=v_w_gu, w_down=v_w_down, ple_g=v_ple_g, w_ple_gate=v_w_ple_gate,
                w_ple_proj=v_w_ple_proj, ple_post_g=v_ple_post_g)
    order = ["pre_mix_g", "post_mix_g", "pre_ffn_g", "post_ffn_g", "pool_w", "pool_scale", "kv_g", "w_kv", "w_q",
             "sinks", "w_o", "w_gu", "w_down", "ple_g", "w_ple_gate", "w_ple_proj", "ple_post_g"]

    kc = jnp.stack([2 * lax.axis_index("x") + lax.axis_index("y"), lax.axis_index("c")]).astype(jnp.int32)
    s_len = x.shape[1]
    x2d = x.reshape(s_len, D)
    p3d = p.reshape(2, s_len, PLE)
    target = loss_target.reshape(s_len, D)
    kv_g2d = kv_g.reshape(1, D)
    gains = {nm: weights[nm] for nm in GAIN_ROWS}

    def shard_view(src, a):
        t = next(t for t in BIGS.values() if t.src == src)
        return a.reshape(-1, t.R, t.C)

    first, second = ["pool_w", "pool_scale"], ["w_gu0", "w_down0"]
    rest = [nm for nm in BIGS if nm not in first + second]
    specs = dict(BIGS, pool_scale=POOL_SCALE)
    placed = {}

    def place_job(nm):
        if nm == "pool_scale":
            return _place_job(POOL_SCALE, pool_scale.reshape(1, 1, D // N_CHIPS), F32)
        return _place_job(BIGS[nm], shard_view(BIGS[nm].src, weights[BIGS[nm].src]))

    def gather(names, rows=None):
        rows = rows or {}
        parts = [(specs[nm],) + tuple(rows.get(nm, (0, specs[nm].R))) for nm in names]
        return _gather_rider(parts, [placed[nm] for nm in names])

    def take(names, results):
        for nm, a in zip(names, results):
            placed[nm] = a

    def weight(nm):
        return _compute_layout(specs[nm], placed[nm])

    take(first, [r[0] for r in _multi_call("place_pool", [place_job(nm) for nm in first], kc)])
    cast, got = _multi_call("place_ffn0", [place_job(nm) for nm in second], kc, rider=gather(first))
    take(second, [r[0] for r in cast])
    take(first, got)
    jobs = [place_job(nm) for nm in rest]
    jobs.append(_mixa_fwd_job(x2d, gains["pre_mix_g"], weight("pool_w"), weight("pool_scale"), gains["post_mix_g"]))
    results, got = _multi_call("cast_and_mixa_fwd", jobs, kc, rider=gather(second))
    take(rest, [r[0] for r in results[:-1]])
    take(second, got)
    y0, x1 = results[-1]

    ride = ["w_ple_gate0", "w_ple_proj0", "w_q", "w_kv", "w_o", "w_gu1"]
    (f0, x2, g0, u0), got = _ffn_fwd(0, x1, gains["pre_ffn_g"], weight("w_gu0"), weight("w_down0"), gains["post_ffn_g"],
                             rider=gather(ride, {"w_gu1": (0, 320)}))
    take(ride, got)

    ride = ["w_ple_gate1", "w_ple_proj1", "w_gu1"]
    (z0, pe0, x3, q, kv), got = _ple_fwd(
        0, x2, p3d, gains["ple_g"], weight("w_ple_gate0"), weight("w_ple_proj0"), gains["ple_post_g"],
        qkv=(gains["pre_mix_g"], kv_g2d, weight("w_q"), weight("w_kv")),
        rider=gather(ride, {"w_gu1": (320, 704)}))
    take(ride, got)

    ride = ["w_down1", "w_gu1"]
    (attn, y1, x4), got = _attn_fwd(q, kv, sinks, x3, weight("w_o"), gains["post_mix_g"],
                                    rider=gather(ride, {"w_gu1": (704, D)}))
    take(ride, got)

    local = {}
    (f1, g1, u1, dx5, local["w_ple_gate1"], local["w_ple_proj1"], d_ple1, d_plepost1, loss_row), _ = _ffn_fwd(
        1, x4, gains["pre_ffn_g"], weight("w_gu1"), weight("w_down1"), gains["post_ffn_g"],
        head=(p3d, gains["ple_g"], weight("w_ple_gate1"), weight("w_ple_proj1"), gains["ple_post_g"], target))

    landed = {}
    fused = {}

    def local_grads(names):
        return [local[nm].reshape(_full_shape(BIGS[nm])) for nm in names]

    def pair_exchange(names):
        return _pair_exchange_rider([BIGS[nm] for nm in names], local_grads(names))

    def pair_sum(tag, names, lands):
        jobs = [_pair_sum_job(BIGS[nm], g, l) for nm, g, l in zip(names, local_grads(names), lands)]
        return [r[0] for r in _multi_call(f"pair_sum_{tag}", jobs, kc)]

    def scatter(names, sums):
        return _scatter_rider([BIGS[nm] for nm in names], sums)

    def keep(names, sums, got):
        for nm, s, l in zip(names, sums, got):
            landed[nm] = (s, l)

    group_a = ["w_ple_gate1", "w_ple_proj1"]
    (dx4, d_preffn1, d_postffn1, *scattered), lands_a = _ffn_bwd(
        1, dx5, x4, f1, g1, u1, gains["pre_ffn_g"], weight("w_gu1"), weight("w_down1"), gains["post_ffn_g"], kc,
        rider=pair_exchange(group_a))
    fused["w_gu1"], fused["w_down1"] = scattered[0:2], scattered[2:4]

    (dq, dkv, local["w_o"], d_postmix1, d_sinks), _ = _attn_bwd(
        dx4, y1, attn, q, kv, sinks, weight("w_o"), gains["post_mix_g"])
    dx3, local["w_q"], local["w_kv"], d_premix1, d_kvg = _qkv_bwd(
        dq, dkv, x3, dx4, gains["pre_mix_g"], kv_g2d, weight("w_q"), weight("w_kv"))

    group_b = ["w_o", "w_q", "w_kv"]
    (dx2, local["w_ple_gate0"], local["w_ple_proj0"], d_ple0, d_plepost0), lands_b = _ple_bwd(
        0, dx3, x2, z0, pe0, p3d, gains["ple_g"], weight("w_ple_gate0"), gains["ple_post_g"],
        rider=pair_exchange(group_b))
    group_ab = group_a + group_b
    sums_ab = pair_sum("ab", group_ab, lands_a + lands_b)

    group_c = ["w_ple_gate0", "w_ple_proj0"]
    (dx1, d_preffn0, d_postffn0, *scattered), got = _ffn_bwd(
        0, dx2, x1, f0, g0, u0, gains["pre_ffn_g"], weight("w_gu0"), weight("w_down0"), gains["post_ffn_g"], kc,
        rider=_both(pair_exchange(group_c), scatter(group_ab, sums_ab)))
    fused["w_gu0"], fused["w_down0"] = scattered[0:2], scattered[2:4]
    lands_c = got[:len(group_c)]
    keep(group_ab, sums_ab, got[len(group_c):])

    layers_of = lambda src: [t for t in BIGS.values() if t.src == src]
    own_scatter = ["w_gu", "w_down"]
    early = own_scatter + ["w_q", "w_o", "w_kv"]
    late = ["w_ple_gate", "w_ple_proj"]
    by_cols = lambda srcs: [src == "w_down" for src in srcs]
    jobs = [_chip_sum_fused_job(layers_of(src), fused, by_cols=src == "w_down") for src in own_scatter]
    jobs += [_chip_sum_job(layers_of(src), landed) for src in early if src not in own_scatter]
    jobs_c = [_pair_sum_job(BIGS[nm], g, l) for nm, g, l in zip(group_c, local_grads(group_c), lands_c)]
    sums = [r[0] for r in _multi_call("chip_sum_early", jobs + jobs_c, kc)]
    halves, sums_c = sums[:len(jobs)], sums[len(jobs):]
    (dx0, d_pool, d_scale, d_postmix0, d_premix0), got = _mixa_bwd(
        dx1, x2d, y0, gains["pre_mix_g"], weight("pool_w"), weight("pool_scale"), gains["post_mix_g"],
        rider=_both(scatter(group_c, sums_c), _share_rider(halves, by_cols(early))))
    keep(group_c, sums_c, got[:len(group_c)])
    full_grads = dict(zip(early, got[len(group_c):]))

    rows = [d_premix0, d_premix1, d_postmix0, d_postmix1, d_preffn0, d_preffn1, d_postffn0, d_postffn1,
            d_ple0, d_ple1, d_plepost0, d_plepost1, d_kvg, d_scale, d_sinks, loss_row]
    as2d = lambda a: a.reshape(1, D) if a.ndim == 1 else a
    halves = [r[0] for r in _multi_call("chip_sum_late", [_chip_sum_job(layers_of(src), landed) for src in late], kc)]
    (tot, g_pool), got = _small_all_reduce(rows, d_pool, rider=_share_rider(halves, by_cols(late)))
    full_grads.update(zip(late, got))
    full_grads["pool_w"] = g_pool
    loss, small = _small_adamw(tot, kc, {nm: as2d(weights[nm]) for nm in SMALL_NAMES},
                               {nm: as2d(m_in[nm]) for nm in SMALL_NAMES},
                               {nm: as2d(v_in[nm]) for nm in SMALL_NAMES})


    def adam_job(src):
        rb = layers_of(src)[0].rb // (1 if src in ("pool_w", "w_gu") else 2)
        return _adamw_job(rb, shard_view(src, weights[src]), full_grads[src],
                          shard_view(src, m_in[src]), shard_view(src, v_in[src]))

    out = {"grad": {}, "delta": {}, "new_m": {}, "new_v": {}}
    blockwise = [src for src in BIG_SOURCES if src != "w_gu"]
    results = dict(zip(blockwise, _multi_call("adamw", [adam_job(src) for src in blockwise], kc)))
    results["w_gu"] = _adamw_ring("adamw_gu", layers_of("w_gu")[0].rb, shard_view("w_gu", weights["w_gu"]),
                                  full_grads["w_gu"], shard_view("w_gu", m_in["w_gu"]),
                                  shard_view("w_gu", v_in["w_gu"]))
    for src in BIG_SOURCES:
        shape = weights[src].shape
        for kind, a in zip(("grad", "delta", "new_m", "new_v"), results[src]):
            out[kind][src] = a.reshape(shape)
    for nm in SMALL_NAMES:
        shape = weights[nm].shape
        for kind, a in zip(("grad", "delta", "new_m", "new_v"), small[nm]):
            out[kind][nm] = a.reshape(shape)

    return (loss.reshape(()), dx0.reshape(x.shape),
            *[out["grad"][nm] for nm in order], *[out["delta"][nm] for nm in order],
            *[out["new_m"][nm] for nm in order], *[out["new_v"][nm] for nm in order])
```
